```python
import math, functools
import jax, jax.numpy as jnp
from jax import lax
import numpy as np

D_MODEL = 1024
BATCH = 8
SEQ = 4096
DEPTH = 2

N_MIXERS = 2
N_A = (DEPTH + 1) // 2
N_B = DEPTH // 2
D_RNN = D_MODEL
RG_BLOCKS = 4
RG_BLOCK = D_RNN // RG_BLOCKS
RG_CONV = 4
RG_C = 8.0
GLA_HEADS = 4
GLA_DK = D_MODEL // 2 // GLA_HEADS
GLA_DV = D_MODEL // GLA_HEADS
GLA_RANK = 16
GLA_TAU = 16.0
GLA_CHUNK = 64
GLA_QK = GLA_HEADS * GLA_DK
GLA_IN = 2 * GLA_QK + 2 * D_MODEL + GLA_RANK
D_FF = ((8 * D_MODEL // 3 + 127) // 128) * 128
FFN_CONV = 3
EPS = 1e-6

kernel_name = "hybrid_rglru_gla_convffn_adaln"


def rmsnorm(x, g):
    x32 = x.astype(jnp.float32)
    y = x32 * lax.rsqrt(jnp.mean(x32 * x32, axis=-1, keepdims=True) + EPS)
    return (y * g.astype(jnp.float32)).astype(x.dtype)


def causal_dwconv(x, w, b):
    k_width = w.shape[0]
    s = x.shape[1]
    xp = jnp.pad(x, ((0, 0), (k_width - 1, 0), (0, 0)))
    y = b
    for k in range(k_width):
        y = y + xp[:, k:k + s] * w[k]
    return y


def _lru_combine(left, right):
    a1, b1 = left
    a2, b2 = right
    return a1 * a2, a2 * b1 + b2


def rglru_mixer(h, w_in, conv_w, conv_b, wa, ba, wx, bx, lam, w_out):
    bsz, s, _ = h.shape
    gate_br, x_br = jnp.split(h @ w_in, 2, axis=-1)
    x_br = causal_dwconv(x_br, conv_w, conv_b)
    xb = x_br.reshape(bsz, s, RG_BLOCKS, RG_BLOCK)
    r = jax.nn.sigmoid(jnp.einsum('bsgi,gij->bsgj', xb, wa).reshape(bsz, s, D_RNN) + ba)
    i_g = jax.nn.sigmoid(jnp.einsum('bsgi,gij->bsgj', xb, wx).reshape(bsz, s, D_RNN) + bx)
    log_a = -RG_C * r.astype(jnp.float32) * jax.nn.softplus(-lam.astype(jnp.float32))
    a = jnp.exp(log_a)
    mult = jnp.sqrt(-jnp.expm1(2.0 * log_a))
    u = mult * (i_g * x_br).astype(jnp.float32)
    _, hs = lax.associative_scan(_lru_combine, (a, u), axis=1)
    y = jax.nn.gelu(gate_br) * hs.astype(h.dtype)
    return y @ w_out


def gla_chunk_scan(q, k, v, g):
    n_c, bsz, nh, cl, dk = q.shape
    dv = v.shape[-1]
    mask = jnp.tril(jnp.ones((cl, cl), dtype=bool))[:, :, None]

    def step(state, inp):
        qc, kc, vc, gc = inp
        big_g = jnp.cumsum(gc, axis=2)
        o_inter = jnp.einsum('bhcd,bhde->bhce', qc * jnp.exp(big_g), state)
        diff = big_g[:, :, :, None, :] - big_g[:, :, None, :, :]
        decay = jnp.exp(jnp.where(mask, diff, -jnp.inf))
        attn = jnp.einsum('bhid,bhjd,bhijd->bhij', qc, kc, decay)
        o_intra = jnp.einsum('bhij,bhje->bhie', attn, vc)
        g_last = big_g[:, :, -1:, :]
        state = jnp.exp(g_last[:, :, 0, :])[..., None] * state + jnp.einsum(
            'bhjd,bhje->bhde', kc * jnp.exp(g_last - big_g), vc)
        return state, o_inter + o_intra

    state0 = jnp.zeros((bsz, nh, dk, dv), jnp.float32)
    _, o = lax.scan(step, state0, (q, k, v, g))
    return o


def gla_mixer(h, w_in, w_alpha, b_alpha, norm_g, w_out):
    bsz, s, _ = h.shape
    proj = h @ w_in
    q, k, v, r, z = jnp.split(proj, [GLA_QK, 2 * GLA_QK, 2 * GLA_QK + D_MODEL,
                                     2 * GLA_QK + 2 * D_MODEL], axis=-1)
    log_alpha = jax.nn.log_sigmoid((z @ w_alpha + b_alpha).astype(jnp.float32)) / GLA_TAU
    n_c = s // GLA_CHUNK

    def to_chunks(t, d):
        return t.reshape(bsz, n_c, GLA_CHUNK, GLA_HEADS, d).transpose(1, 0, 3, 2, 4).astype(jnp.float32)

    o = gla_chunk_scan(to_chunks(q * (GLA_DK ** -0.5), GLA_DK), to_chunks(k, GLA_DK),
                       to_chunks(v, GLA_DV), to_chunks(log_alpha, GLA_DK))
    o = o.transpose(1, 0, 3, 2, 4).reshape(bsz, s, GLA_HEADS, GLA_DV)
    o = rmsnorm(o, norm_g).reshape(bsz, s, D_MODEL).astype(h.dtype)
    return (o * jax.nn.silu(r)) @ w_out


def conv_ffn(h, w_up, conv_w, conv_b, w_down):
    u = causal_dwconv(h @ w_up, conv_w, conv_b)
    g, val = jnp.split(u, 2, axis=-1)
    return (jax.nn.gelu(g) * val) @ w_down


def _fwd_setup_inputs(seed: int = 0) -> dict:
    key = jax.random.key(seed)
    ks = iter(jax.random.split(key, 32))

    def nrm(shape, scale):
        return jax.random.normal(next(ks), shape, jnp.float32) * scale

    d = D_MODEL
    x = nrm((BATCH, SEQ, d), 1.0)
    c = nrm((BATCH, d), 1.0)
    ada_w = nrm((DEPTH, d, 6 * d), d ** -0.5)
    ada_b = nrm((DEPTH, 6 * d), 0.01)
    norm_g = 1.0 + nrm((DEPTH, 4, d), 0.05)
    ffn_w_up = nrm((DEPTH, d, 2 * D_FF), d ** -0.5)
    ffn_conv_w = nrm((DEPTH, FFN_CONV, 2 * D_FF), FFN_CONV ** -0.5)
    ffn_conv_b = nrm((DEPTH, 2 * D_FF), 0.01)
    ffn_w_down = nrm((DEPTH, D_FF, d), D_FF ** -0.5)
    rg_w_in = nrm((N_A, d, 2 * D_RNN), d ** -0.5)
    rg_conv_w = nrm((N_A, RG_CONV, D_RNN), RG_CONV ** -0.5)
    rg_conv_b = nrm((N_A, D_RNN), 0.01)
    rg_wa = nrm((N_A, RG_BLOCKS, RG_BLOCK, RG_BLOCK), RG_BLOCK ** -0.5)
    rg_ba = nrm((N_A, D_RNN), 0.01)
    rg_wx = nrm((N_A, RG_BLOCKS, RG_BLOCK, RG_BLOCK), RG_BLOCK ** -0.5)
    rg_bx = nrm((N_A, D_RNN), 0.01)
    a_pow = jax.random.uniform(next(ks), (N_A, D_RNN), jnp.float32, 0.9, 0.999)
    a_base = a_pow ** (1.0 / RG_C)
    rg_lambda = jnp.log(a_base) - jnp.log1p(-a_base)
    rg_w_out = nrm((N_A, D_RNN, d), D_RNN ** -0.5)
    gla_w_in = nrm((N_B, d, GLA_IN), d ** -0.5)
    gla_w_alpha = nrm((N_B, GLA_RANK, GLA_QK), GLA_RANK ** -0.5)
    gla_b_alpha = nrm((N_B, GLA_QK), 0.01)
    gla_norm_g = 1.0 + nrm((N_B, GLA_DV), 0.05)
    gla_w_out = nrm((N_B, d, d), d ** -0.5)
    return {"x": x, "c": c, "ada_w": ada_w, "ada_b": ada_b, "norm_g": norm_g,
            "ffn_w_up": ffn_w_up, "ffn_conv_w": ffn_conv_w, "ffn_conv_b": ffn_conv_b,
            "ffn_w_down": ffn_w_down, "rg_w_in": rg_w_in, "rg_conv_w": rg_conv_w,
            "rg_conv_b": rg_conv_b, "rg_wa": rg_wa, "rg_ba": rg_ba, "rg_wx": rg_wx,
            "rg_bx": rg_bx, "rg_lambda": rg_lambda, "rg_w_out": rg_w_out,
            "gla_w_in": gla_w_in, "gla_w_alpha": gla_w_alpha, "gla_b_alpha": gla_b_alpha,
            "gla_norm_g": gla_norm_g, "gla_w_out": gla_w_out}


def _fwd_reference(x, c, ada_w, ada_b, norm_g, ffn_w_up, ffn_conv_w, ffn_conv_b, ffn_w_down,
              rg_w_in, rg_conv_w, rg_conv_b, rg_wa, rg_ba, rg_wx, rg_bx, rg_lambda, rg_w_out,
              gla_w_in, gla_w_alpha, gla_b_alpha, gla_norm_g, gla_w_out):
    c_act = jax.nn.silu(c)
    for i in range(DEPTH):
        mod = (c_act @ ada_w[i] + ada_b[i])[:, None, :]
        sh_m, sc_m, gt_m, sh_f, sc_f, gt_f = jnp.split(mod, 6, axis=-1)
        j = i // N_MIXERS
        h = rmsnorm(x, norm_g[i, 0]) * (1.0 + sc_m) + sh_m
        if i % N_MIXERS == 0:
            y = rglru_mixer(h, rg_w_in[j], rg_conv_w[j], rg_conv_b[j], rg_wa[j], rg_ba[j],
                            rg_wx[j], rg_bx[j], rg_lambda[j], rg_w_out[j])
        else:
            y = gla_mixer(h, gla_w_in[j], gla_w_alpha[j], gla_b_alpha[j], gla_norm_g[j], gla_w_out[j])
        x = x + gt_m * rmsnorm(y, norm_g[i, 1])
        h = rmsnorm(x, norm_g[i, 2]) * (1.0 + sc_f) + sh_f
        y = conv_ffn(h, ffn_w_up[i], ffn_conv_w[i], ffn_conv_b[i], ffn_w_down[i])
        x = x + gt_f * rmsnorm(y, norm_g[i, 3])
    return x


import jax as _jax
import jax.numpy as _jnp

TWIN_FORMAT = 'train_step'
FWD_PARAMS = ['x', 'c', 'ada_w', 'ada_b', 'norm_g', 'ffn_w_up', 'ffn_conv_w', 'ffn_conv_b', 'ffn_w_down', 'rg_w_in', 'rg_conv_w', 'rg_conv_b', 'rg_wa', 'rg_ba', 'rg_wx', 'rg_bx', 'rg_lambda', 'rg_w_out', 'gla_w_in', 'gla_w_alpha', 'gla_b_alpha', 'gla_norm_g', 'gla_w_out']
TWIN_WEIGHTS = ['ada_w', 'ada_b', 'norm_g', 'ffn_w_up', 'ffn_conv_w', 'ffn_conv_b', 'ffn_w_down', 'rg_w_in', 'rg_conv_w', 'rg_conv_b', 'rg_wa', 'rg_ba', 'rg_wx', 'rg_bx', 'rg_lambda', 'rg_w_out', 'gla_w_in', 'gla_w_alpha', 'gla_b_alpha', 'gla_norm_g', 'gla_w_out']
TWIN_DIFF_INPUT = 'x'
TWIN_INPUTS = ['x', 'c', 'ada_w', 'ada_b', 'norm_g', 'ffn_w_up', 'ffn_conv_w', 'ffn_conv_b', 'ffn_w_down', 'rg_w_in', 'rg_conv_w', 'rg_conv_b', 'rg_wa', 'rg_ba', 'rg_wx', 'rg_bx', 'rg_lambda', 'rg_w_out', 'gla_w_in', 'gla_w_alpha', 'gla_b_alpha', 'gla_norm_g', 'gla_w_out', 'loss_target', 'm_ada_w', 'm_ada_b', 'm_norm_g', 'm_ffn_w_up', 'm_ffn_conv_w', 'm_ffn_conv_b', 'm_ffn_w_down', 'm_rg_w_in', 'm_rg_conv_w', 'm_rg_conv_b', 'm_rg_wa', 'm_rg_ba', 'm_rg_wx', 'm_rg_bx', 'm_rg_lambda', 'm_rg_w_out', 'm_gla_w_in', 'm_gla_w_alpha', 'm_gla_b_alpha', 'm_gla_norm_g', 'm_gla_w_out', 'v_ada_w', 'v_ada_b', 'v_norm_g', 'v_ffn_w_up', 'v_ffn_conv_w', 'v_ffn_conv_b', 'v_ffn_w_down', 'v_rg_w_in', 'v_rg_conv_w', 'v_rg_conv_b', 'v_rg_wa', 'v_rg_ba', 'v_rg_wx', 'v_rg_bx', 'v_rg_lambda', 'v_rg_w_out', 'v_gla_w_in', 'v_gla_w_alpha', 'v_gla_b_alpha', 'v_gla_norm_g', 'v_gla_w_out']
TWIN_OUTPUTS = ['loss', 'grad_x', 'grad_ada_w', 'grad_ada_b', 'grad_norm_g', 'grad_ffn_w_up', 'grad_ffn_conv_w', 'grad_ffn_conv_b', 'grad_ffn_w_down', 'grad_rg_w_in', 'grad_rg_conv_w', 'grad_rg_conv_b', 'grad_rg_wa', 'grad_rg_ba', 'grad_rg_wx', 'grad_rg_bx', 'grad_rg_lambda', 'grad_rg_w_out', 'grad_gla_w_in', 'grad_gla_w_alpha', 'grad_gla_b_alpha', 'grad_gla_norm_g', 'grad_gla_w_out', 'delta_ada_w', 'delta_ada_b', 'delta_norm_g', 'delta_ffn_w_up', 'delta_ffn_conv_w', 'delta_ffn_conv_b', 'delta_ffn_w_down', 'delta_rg_w_in', 'delta_rg_conv_w', 'delta_rg_conv_b', 'delta_rg_wa', 'delta_rg_ba', 'delta_rg_wx', 'delta_rg_bx', 'delta_rg_lambda', 'delta_rg_w_out', 'delta_gla_w_in', 'delta_gla_w_alpha', 'delta_gla_b_alpha', 'delta_gla_norm_g', 'delta_gla_w_out', 'new_m_ada_w', 'new_m_ada_b', 'new_m_norm_g', 'new_m_ffn_w_up', 'new_m_ffn_conv_w', 'new_m_ffn_conv_b', 'new_m_ffn_w_down', 'new_m_rg_w_in', 'new_m_rg_conv_w', 'new_m_rg_conv_b', 'new_m_rg_wa', 'new_m_rg_ba', 'new_m_rg_wx', 'new_m_rg_bx', 'new_m_rg_lambda', 'new_m_rg_w_out', 'new_m_gla_w_in', 'new_m_gla_w_alpha', 'new_m_gla_b_alpha', 'new_m_gla_norm_g', 'new_m_gla_w_out', 'new_v_ada_w', 'new_v_ada_b', 'new_v_norm_g', 'new_v_ffn_w_up', 'new_v_ffn_conv_w', 'new_v_ffn_conv_b', 'new_v_ffn_w_down', 'new_v_rg_w_in', 'new_v_rg_conv_w', 'new_v_rg_conv_b', 'new_v_rg_wa', 'new_v_rg_ba', 'new_v_rg_wx', 'new_v_rg_bx', 'new_v_rg_lambda', 'new_v_rg_w_out', 'new_v_gla_w_in', 'new_v_gla_w_alpha', 'new_v_gla_b_alpha', 'new_v_gla_norm_g', 'new_v_gla_w_out']
TWIN_LEAF_KINDS = {'loss': 'loss', 'grad_x': 'grad_x', 'grad_ada_w': 'grad_w', 'grad_ada_b': 'grad_w', 'grad_norm_g': 'grad_w', 'grad_ffn_w_up': 'grad_w', 'grad_ffn_conv_w': 'grad_w', 'grad_ffn_conv_b': 'grad_w', 'grad_ffn_w_down': 'grad_w', 'grad_rg_w_in': 'grad_w', 'grad_rg_conv_w': 'grad_w', 'grad_rg_conv_b': 'grad_w', 'grad_rg_wa': 'grad_w', 'grad_rg_ba': 'grad_w', 'grad_rg_wx': 'grad_w', 'grad_rg_bx': 'grad_w', 'grad_rg_lambda': 'grad_w', 'grad_rg_w_out': 'grad_w', 'grad_gla_w_in': 'grad_w', 'grad_gla_w_alpha': 'grad_w', 'grad_gla_b_alpha': 'grad_w', 'grad_gla_norm_g': 'grad_w', 'grad_gla_w_out': 'grad_w', 'delta_ada_w': 'delta_w', 'delta_ada_b': 'delta_w', 'delta_norm_g': 'delta_w', 'delta_ffn_w_up': 'delta_w', 'delta_ffn_conv_w': 'delta_w', 'delta_ffn_conv_b': 'delta_w', 'delta_ffn_w_down': 'delta_w', 'delta_rg_w_in': 'delta_w', 'delta_rg_conv_w': 'delta_w', 'delta_rg_conv_b': 'delta_w', 'delta_rg_wa': 'delta_w', 'delta_rg_ba': 'delta_w', 'delta_rg_wx': 'delta_w', 'delta_rg_bx': 'delta_w', 'delta_rg_lambda': 'delta_w', 'delta_rg_w_out': 'delta_w', 'delta_gla_w_in': 'delta_w', 'delta_gla_w_alpha': 'delta_w', 'delta_gla_b_alpha': 'delta_w', 'delta_gla_norm_g': 'delta_w', 'delta_gla_w_out': 'delta_w', 'new_m_ada_w': 'new_m', 'new_m_ada_b': 'new_m', 'new_m_norm_g': 'new_m', 'new_m_ffn_w_up': 'new_m', 'new_m_ffn_conv_w': 'new_m', 'new_m_ffn_conv_b': 'new_m', 'new_m_ffn_w_down': 'new_m', 'new_m_rg_w_in': 'new_m', 'new_m_rg_conv_w': 'new_m', 'new_m_rg_conv_b': 'new_m', 'new_m_rg_wa': 'new_m', 'new_m_rg_ba': 'new_m', 'new_m_rg_wx': 'new_m', 'new_m_rg_bx': 'new_m', 'new_m_rg_lambda': 'new_m', 'new_m_rg_w_out': 'new_m', 'new_m_gla_w_in': 'new_m', 'new_m_gla_w_alpha': 'new_m', 'new_m_gla_b_alpha': 'new_m', 'new_m_gla_norm_g': 'new_m', 'new_m_gla_w_out': 'new_m', 'new_v_ada_w': 'new_v', 'new_v_ada_b': 'new_v', 'new_v_norm_g': 'new_v', 'new_v_ffn_w_up': 'new_v', 'new_v_ffn_conv_w': 'new_v', 'new_v_ffn_conv_b': 'new_v', 'new_v_ffn_w_down': 'new_v', 'new_v_rg_w_in': 'new_v', 'new_v_rg_conv_w': 'new_v', 'new_v_rg_conv_b': 'new_v', 'new_v_rg_wa': 'new_v', 'new_v_rg_ba': 'new_v', 'new_v_rg_wx': 'new_v', 'new_v_rg_bx': 'new_v', 'new_v_rg_lambda': 'new_v', 'new_v_rg_w_out': 'new_v', 'new_v_gla_w_in': 'new_v', 'new_v_gla_w_alpha': 'new_v', 'new_v_gla_b_alpha': 'new_v', 'new_v_gla_norm_g': 'new_v', 'new_v_gla_w_out': 'new_v'}


def _forward(args):
    return _fwd_reference(*[args[k] for k in FWD_PARAMS])


def _output_shape():
    def fwd():
        inp = _fwd_setup_inputs(0)
        return _fwd_reference(*[inp[k] for k in FWD_PARAMS])
    out = _jax.eval_shape(fwd)
    return out.shape, out.dtype

N_MICROBATCH = 1
ADAM_LR = 0.001
ADAM_B1 = 0.9
ADAM_B2 = 0.999
ADAM_EPS = 1e-08
ADAM_WD = 0.01
ADAM_STEP = 10
PER_EXAMPLE_BATCH_AXIS = {'x': 0, 'c': 0, 'loss_target': 0}
SHARED_INPUTS = []
_WEIGHT_DTYPES = {'ada_w': _jnp.float32, 'ada_b': _jnp.float32, 'norm_g': _jnp.float32, 'ffn_w_up': _jnp.float32, 'ffn_conv_w': _jnp.float32, 'ffn_conv_b': _jnp.float32, 'ffn_w_down': _jnp.float32, 'rg_w_in': _jnp.float32, 'rg_conv_w': _jnp.float32, 'rg_conv_b': _jnp.float32, 'rg_wa': _jnp.float32, 'rg_ba': _jnp.float32, 'rg_wx': _jnp.float32, 'rg_bx': _jnp.float32, 'rg_lambda': _jnp.float32, 'rg_w_out': _jnp.float32, 'gla_w_in': _jnp.float32, 'gla_w_alpha': _jnp.float32, 'gla_b_alpha': _jnp.float32, 'gla_norm_g': _jnp.float32, 'gla_w_out': _jnp.float32}
MOMENT_SCALE = {'ada_w': 2.921469e+00, 'ada_b': 5.911439e+00, 'norm_g': 9.546933e+00, 'ffn_w_up': 5.993767e-01, 'ffn_conv_w': 6.784937e-01, 'ffn_conv_b': 8.983747e-01, 'ffn_w_down': 1.256790e+00, 'rg_w_in': 2.227566e+00, 'rg_conv_w': 2.729002e+00, 'rg_conv_b': 4.647548e+00, 'rg_wa': 2.121041e-01, 'rg_ba': 3.098310e-01, 'rg_wx': 5.864142e-01, 'rg_bx': 1.020999e+00, 'rg_lambda': 7.984608e-01, 'rg_w_out': 2.618363e+00, 'gla_w_in': 8.864728e-01, 'gla_w_alpha': 5.288150e-01, 'gla_b_alpha': 6.756663e-01, 'gla_norm_g': 1.902781e+00, 'gla_w_out': 1.015098e+00}


def _to_microbatches(a, axis):
    t = _jnp.moveaxis(a, axis, 0)
    t = t.reshape((N_MICROBATCH, t.shape[0] // N_MICROBATCH) + t.shape[1:])
    return _jnp.moveaxis(t, 1, axis + 1)


def setup_inputs(seed: int = 0) -> dict:
    inp = _fwd_setup_inputs(seed)
    key = _jax.random.fold_in(_jax.random.key(seed), 7919)
    shape, _ = _output_shape()
    out = dict(inp)
    out["loss_target"] = _jax.random.normal(_jax.random.fold_in(key, 0), shape, _jnp.float32)
    for i, name in enumerate(TWIN_WEIGHTS):
        w = inp[name].astype(_jnp.float32)
        if MOMENT_SCALE is None:
            s = _jnp.sqrt(_jnp.mean(_jnp.square(w)) + 1e-30)
        else:
            s = MOMENT_SCALE[name]
        km, kv = _jax.random.split(_jax.random.fold_in(key, i + 1))
        out[name] = w
        out["m_" + name] = s * _jax.random.normal(km, w.shape, _jnp.float32)
        out["v_" + name] = (s * s) * _jax.random.uniform(kv, w.shape, _jnp.float32, 0.5, 1.5)
    if N_MICROBATCH > 1:
        for name, axis in PER_EXAMPLE_BATCH_AXIS.items():
            out[name] = _to_microbatches(out[name], axis)
    return {'x': out['x'], 'c': out['c'], 'ada_w': out['ada_w'], 'ada_b': out['ada_b'], 'norm_g': out['norm_g'], 'ffn_w_up': out['ffn_w_up'], 'ffn_conv_w': out['ffn_conv_w'], 'ffn_conv_b': out['ffn_conv_b'], 'ffn_w_down': out['ffn_w_down'], 'rg_w_in': out['rg_w_in'], 'rg_conv_w': out['rg_conv_w'], 'rg_conv_b': out['rg_conv_b'], 'rg_wa': out['rg_wa'], 'rg_ba': out['rg_ba'], 'rg_wx': out['rg_wx'], 'rg_bx': out['rg_bx'], 'rg_lambda': out['rg_lambda'], 'rg_w_out': out['rg_w_out'], 'gla_w_in': out['gla_w_in'], 'gla_w_alpha': out['gla_w_alpha'], 'gla_b_alpha': out['gla_b_alpha'], 'gla_norm_g': out['gla_norm_g'], 'gla_w_out': out['gla_w_out'], 'loss_target': out['loss_target'], 'm_ada_w': out['m_ada_w'], 'm_ada_b': out['m_ada_b'], 'm_norm_g': out['m_norm_g'], 'm_ffn_w_up': out['m_ffn_w_up'], 'm_ffn_conv_w': out['m_ffn_conv_w'], 'm_ffn_conv_b': out['m_ffn_conv_b'], 'm_ffn_w_down': out['m_ffn_w_down'], 'm_rg_w_in': out['m_rg_w_in'], 'm_rg_conv_w': out['m_rg_conv_w'], 'm_rg_conv_b': out['m_rg_conv_b'], 'm_rg_wa': out['m_rg_wa'], 'm_rg_ba': out['m_rg_ba'], 'm_rg_wx': out['m_rg_wx'], 'm_rg_bx': out['m_rg_bx'], 'm_rg_lambda': out['m_rg_lambda'], 'm_rg_w_out': out['m_rg_w_out'], 'm_gla_w_in': out['m_gla_w_in'], 'm_gla_w_alpha': out['m_gla_w_alpha'], 'm_gla_b_alpha': out['m_gla_b_alpha'], 'm_gla_norm_g': out['m_gla_norm_g'], 'm_gla_w_out': out['m_gla_w_out'], 'v_ada_w': out['v_ada_w'], 'v_ada_b': out['v_ada_b'], 'v_norm_g': out['v_norm_g'], 'v_ffn_w_up': out['v_ffn_w_up'], 'v_ffn_conv_w': out['v_ffn_conv_w'], 'v_ffn_conv_b': out['v_ffn_conv_b'], 'v_ffn_w_down': out['v_ffn_w_down'], 'v_rg_w_in': out['v_rg_w_in'], 'v_rg_conv_w': out['v_rg_conv_w'], 'v_rg_conv_b': out['v_rg_conv_b'], 'v_rg_wa': out['v_rg_wa'], 'v_rg_ba': out['v_rg_ba'], 'v_rg_wx': out['v_rg_wx'], 'v_rg_bx': out['v_rg_bx'], 'v_rg_lambda': out['v_rg_lambda'], 'v_rg_w_out': out['v_rg_w_out'], 'v_gla_w_in': out['v_gla_w_in'], 'v_gla_w_alpha': out['v_gla_w_alpha'], 'v_gla_b_alpha': out['v_gla_b_alpha'], 'v_gla_norm_g': out['v_gla_norm_g'], 'v_gla_w_out': out['v_gla_w_out']}


def _loss(weights, diff, rest, loss_target):
    with _jax.named_scope("forward"):
        args = {**rest, TWIN_DIFF_INPUT: diff, **{k: w.astype(_WEIGHT_DTYPES[k]) for k, w in weights.items()}}
        y = _forward(args)
    with _jax.named_scope("loss_head"):
        err = _jnp.square(y.astype(_jnp.float32) - loss_target)
        return 0.5 * _jnp.sum(_jnp.mean(err, axis=-1)) if err.ndim else 0.5 * err


def _adamw(w, g, m, v):
    m = ADAM_B1 * m + (1.0 - ADAM_B1) * g
    v = ADAM_B2 * v + (1.0 - ADAM_B2) * _jnp.square(g)
    m_hat = m / (1.0 - ADAM_B1 ** ADAM_STEP)
    v_hat = v / (1.0 - ADAM_B2 ** ADAM_STEP)
    delta = -ADAM_LR * (m_hat / (_jnp.sqrt(v_hat) + ADAM_EPS) + ADAM_WD * w)
    return delta, m, v


def reference(x, c, ada_w, ada_b, norm_g, ffn_w_up, ffn_conv_w, ffn_conv_b, ffn_w_down, rg_w_in, rg_conv_w, rg_conv_b, rg_wa, rg_ba, rg_wx, rg_bx, rg_lambda, rg_w_out, gla_w_in, gla_w_alpha, gla_b_alpha, gla_norm_g, gla_w_out, loss_target, m_ada_w, m_ada_b, m_norm_g, m_ffn_w_up, m_ffn_conv_w, m_ffn_conv_b, m_ffn_w_down, m_rg_w_in, m_rg_conv_w, m_rg_conv_b, m_rg_wa, m_rg_ba, m_rg_wx, m_rg_bx, m_rg_lambda, m_rg_w_out, m_gla_w_in, m_gla_w_alpha, m_gla_b_alpha, m_gla_norm_g, m_gla_w_out, v_ada_w, v_ada_b, v_norm_g, v_ffn_w_up, v_ffn_conv_w, v_ffn_conv_b, v_ffn_w_down, v_rg_w_in, v_rg_conv_w, v_rg_conv_b, v_rg_wa, v_rg_ba, v_rg_wx, v_rg_bx, v_rg_lambda, v_rg_w_out, v_gla_w_in, v_gla_w_alpha, v_gla_b_alpha, v_gla_norm_g, v_gla_w_out):
    given = dict(x=x, c=c, ada_w=ada_w, ada_b=ada_b, norm_g=norm_g, ffn_w_up=ffn_w_up, ffn_conv_w=ffn_conv_w, ffn_conv_b=ffn_conv_b, ffn_w_down=ffn_w_down, rg_w_in=rg_w_in, rg_conv_w=rg_conv_w, rg_conv_b=rg_conv_b, rg_wa=rg_wa, rg_ba=rg_ba, rg_wx=rg_wx, rg_bx=rg_bx, rg_lambda=rg_lambda, rg_w_out=rg_w_out, gla_w_in=gla_w_in, gla_w_alpha=gla_w_alpha, gla_b_alpha=gla_b_alpha, gla_norm_g=gla_norm_g, gla_w_out=gla_w_out, loss_target=loss_target, m_ada_w=m_ada_w, m_ada_b=m_ada_b, m_norm_g=m_norm_g, m_ffn_w_up=m_ffn_w_up, m_ffn_conv_w=m_ffn_conv_w, m_ffn_conv_b=m_ffn_conv_b, m_ffn_w_down=m_ffn_w_down, m_rg_w_in=m_rg_w_in, m_rg_conv_w=m_rg_conv_w, m_rg_conv_b=m_rg_conv_b, m_rg_wa=m_rg_wa, m_rg_ba=m_rg_ba, m_rg_wx=m_rg_wx, m_rg_bx=m_rg_bx, m_rg_lambda=m_rg_lambda, m_rg_w_out=m_rg_w_out, m_gla_w_in=m_gla_w_in, m_gla_w_alpha=m_gla_w_alpha, m_gla_b_alpha=m_gla_b_alpha, m_gla_norm_g=m_gla_norm_g, m_gla_w_out=m_gla_w_out, v_ada_w=v_ada_w, v_ada_b=v_ada_b, v_norm_g=v_norm_g, v_ffn_w_up=v_ffn_w_up, v_ffn_conv_w=v_ffn_conv_w, v_ffn_conv_b=v_ffn_conv_b, v_ffn_w_down=v_ffn_w_down, v_rg_w_in=v_rg_w_in, v_rg_conv_w=v_rg_conv_w, v_rg_conv_b=v_rg_conv_b, v_rg_wa=v_rg_wa, v_rg_ba=v_rg_ba, v_rg_wx=v_rg_wx, v_rg_bx=v_rg_bx, v_rg_lambda=v_rg_lambda, v_rg_w_out=v_rg_w_out, v_gla_w_in=v_gla_w_in, v_gla_w_alpha=v_gla_w_alpha, v_gla_b_alpha=v_gla_b_alpha, v_gla_norm_g=v_gla_norm_g, v_gla_w_out=v_gla_w_out)
    weights = {n: given[n] for n in TWIN_WEIGHTS}
    shared = {n: given[n] for n in SHARED_INPUTS}
    per_example = {n: given[n] for n in ['x', 'c']}
    grad_fn = _jax.value_and_grad(_loss, argnums=(0, 1))

    def one_microbatch(ex, loss_target):
        ex = dict(ex)
        diff = ex.pop(TWIN_DIFF_INPUT)
        return grad_fn(weights, diff, {**shared, **ex}, loss_target)

    if N_MICROBATCH == 1:
        loss, (grad_w, grad_x) = one_microbatch(per_example, given["loss_target"])
    else:
        def body(carry, xs):
            loss_sum, grad_sum = carry
            l_k, (gw_k, gx_k) = one_microbatch(xs[0], xs[1])
            with _jax.named_scope("update"):
                return (loss_sum + l_k, _jax.tree.map(_jnp.add, grad_sum, gw_k)), gx_k

        init = (_jnp.zeros((), _jnp.float32), _jax.tree.map(_jnp.zeros_like, weights))
        (loss, grad_w), grad_x = _jax.lax.scan(body, init, (per_example, given["loss_target"]))
    with _jax.named_scope("update"):
        delta_w, new_m, new_v = {}, {}, {}
        for n in TWIN_WEIGHTS:
            delta_w[n], new_m[n], new_v[n] = _adamw(weights[n], grad_w[n], given["m_" + n], given["v_" + n])
    return (loss, grad_x, *[grad_w[n] for n in TWIN_WEIGHTS], *[delta_w[n] for n in TWIN_WEIGHTS],
            *[new_m[n] for n in TWIN_WEIGHTS], *[new_v[n] for n in TWIN_WEIGHTS])
```

```python
import functools

import jax
import jax.numpy as jnp
from jax import lax
from jax.experimental import pallas as pl
from jax.experimental.pallas import tpu as pltpu

F32 = jnp.float32
BF16 = jnp.bfloat16
MXU_DTYPE = BF16

EPS = 1e-6
RG_C = 8.0
RG_BLOCKS = 4
RG_CONV = 4
GLA_HEADS = 4
GLA_TAU = 16.0
GLA_CHUNK = 64
GLA_RANK = 16
FFN_CONV = 3
ADAM_LR = 0.001
ADAM_B1 = 0.9
ADAM_B2 = 0.999
ADAM_EPS = 1e-08
ADAM_WD = 0.01
ADAM_STEP = 10

LANES = 128
SUBLANES = 8
VMEM_LIMIT = 56 * 1024 * 1024
CB = 256
MESH = pl.DeviceIdType.MESH


def _params(*sem):
    return pltpu.CompilerParams(dimension_semantics=sem, vmem_limit_bytes=VMEM_LIMIT)


def _tile(dim, prefs):
    for p in prefs:
        if dim % p == 0:
            return p
    return dim


def _dot(a, b, dims):
    return lax.dot_general(a.astype(MXU_DTYPE), b.astype(MXU_DTYPE), (dims, ((), ())), preferred_element_type=F32)


def _dot_nn(a, b):
    return _dot(a, b, ((1,), (0,)))


def _dot_nt(a, b):
    return _dot(a, b, ((1,), (1,)))


def _dot_tn(a, b):
    return _dot(a, b, ((0,), (0,)))


def _mm(a, b, *, ta=False, tb=False, out_dtype=F32, name):
    k_dim, m_dim = (a.shape if ta else a.shape[::-1])
    n_dim = b.shape[0] if tb else b.shape[1]
    assert (b.shape[1] if tb else b.shape[0]) == k_dim
    tm = _tile(m_dim, (1024, 1408, 512, 256, 128))
    tn = _tile(n_dim, (1024, 1408, 896, 512, 256, 128))
    tk = _tile(k_dim, (1024, 1408, 896, 512, 256, 128))
    nk = k_dim // tk
    dims = ((0 if ta else 1,), (1 if tb else 0,))

    def body(a_ref, b_ref, o_ref, acc_ref):
        k = pl.program_id(2)

        @pl.when(k == 0)
        def _():
            acc_ref[...] = jnp.zeros_like(acc_ref)

        acc_ref[...] += _dot(a_ref[...], b_ref[...], dims)

        @pl.when(k == nk - 1)
        def _():
            o_ref[...] = acc_ref[...].astype(o_ref.dtype)

    a_spec = pl.BlockSpec((tk, tm), lambda i, j, k: (k, i)) if ta else pl.BlockSpec((tm, tk), lambda i, j, k: (i, k))
    b_spec = pl.BlockSpec((tn, tk), lambda i, j, k: (j, k)) if tb else pl.BlockSpec((tk, tn), lambda i, j, k: (k, j))
    return pl.pallas_call(
        body,
        grid=(m_dim // tm, n_dim // tn, nk),
        in_specs=[a_spec, b_spec],
        out_specs=pl.BlockSpec((tm, tn), lambda i, j, k: (i, j)),
        out_shape=jax.ShapeDtypeStruct((m_dim, n_dim), out_dtype),
        scratch_shapes=[pltpu.VMEM((tm, tn), F32)],
        compiler_params=_params("parallel", "parallel", "arbitrary"),
        name=name,
    )(a, b)


def _row_specs(s, d, ts):
    return pl.BlockSpec((ts, d), lambda i: (i, 0)), pl.BlockSpec((1, d), lambda i: (0, 0))


def _norm_mod_fwd(x, g, sc, sh, name):
    s, d = x.shape
    ts = _tile(s, (512,))

    def body(x_ref, g_ref, sc_ref, sh_ref, h_ref):
        xv = x_ref[...]
        r = lax.rsqrt(jnp.mean(xv * xv, axis=-1, keepdims=True) + EPS)
        h_ref[...] = (((xv * r) * g_ref[...]) * (1.0 + sc_ref[...]) + sh_ref[...]).astype(h_ref.dtype)

    row, vec = _row_specs(s, d, ts)
    return pl.pallas_call(
        body, grid=(s // ts,), in_specs=[row, vec, vec, vec], out_specs=row,
        out_shape=jax.ShapeDtypeStruct((s, d), MXU_DTYPE), compiler_params=_params("parallel"), name=name,
    )(x, g, sc, sh)


def _norm_mod_bwd(dh, x, g, sc, dres, name):
    s, d = x.shape
    ts = _tile(s, (512,))

    def body(dh_ref, x_ref, g_ref, sc_ref, dres_ref, dx_ref, dg_ref, dsc_ref, dsh_ref, acc_ref):
        i = pl.program_id(0)

        @pl.when(i == 0)
        def _():
            acc_ref[...] = jnp.zeros_like(acc_ref)

        xv, dhv = x_ref[...], dh_ref[...]
        r = lax.rsqrt(jnp.mean(xv * xv, axis=-1, keepdims=True) + EPS)
        n = xv * r
        acc_ref[0:1, :] += jnp.sum(dhv * n, axis=0, keepdims=True)
        acc_ref[1:2, :] += jnp.sum(dhv, axis=0, keepdims=True)
        dn = dhv * ((1.0 + sc_ref[...]) * g_ref[...])
        dx_ref[...] = dres_ref[...] + r * (dn - n * jnp.mean(dn * n, axis=-1, keepdims=True))
        dg_ref[...] = (1.0 + sc_ref[...]) * acc_ref[0:1, :]
        dsc_ref[...] = g_ref[...] * acc_ref[0:1, :]
        dsh_ref[...] = acc_ref[1:2, :]

    row, vec = _row_specs(s, d, ts)
    vshape = jax.ShapeDtypeStruct((1, d), F32)
    return pl.pallas_call(
        body, grid=(s // ts,), in_specs=[row, row, vec, vec, row], out_specs=[row, vec, vec, vec],
        out_shape=[jax.ShapeDtypeStruct((s, d), F32), vshape, vshape, vshape],
        scratch_shapes=[pltpu.VMEM((SUBLANES, d), F32)], compiler_params=_params("arbitrary"), name=name,
    )(dh, x, g, sc, dres)


def _post_fwd(x, y, g, gt, name):
    s, d = x.shape
    ts = _tile(s, (512,))

    def body(x_ref, y_ref, g_ref, gt_ref, o_ref):
        yv = y_ref[...]
        r = lax.rsqrt(jnp.mean(yv * yv, axis=-1, keepdims=True) + EPS)
        o_ref[...] = x_ref[...] + gt_ref[...] * ((yv * r) * g_ref[...])

    row, vec = _row_specs(s, d, ts)
    return pl.pallas_call(
        body, grid=(s // ts,), in_specs=[row, row, vec, vec], out_specs=row,
        out_shape=jax.ShapeDtypeStruct((s, d), F32), compiler_params=_params("parallel"), name=name,
    )(x, y, g, gt)


def _post_bwd(dxn, y, g, gt, name):
    s, d = y.shape
    ts = _tile(s, (512,))

    def body(dxn_ref, y_ref, g_ref, gt_ref, dy_ref, dg_ref, dgt_ref, acc_ref):
        i = pl.program_id(0)

        @pl.when(i == 0)
        def _():
            acc_ref[...] = jnp.zeros_like(acc_ref)

        yv, dv = y_ref[...], dxn_ref[...]
        r = lax.rsqrt(jnp.mean(yv * yv, axis=-1, keepdims=True) + EPS)
        n = yv * r
        acc_ref[0:1, :] += jnp.sum(dv * n, axis=0, keepdims=True)
        dn = dv * (gt_ref[...] * g_ref[...])
        dy_ref[...] = (r * (dn - n * jnp.mean(dn * n, axis=-1, keepdims=True))).astype(dy_ref.dtype)
        dg_ref[...] = gt_ref[...] * acc_ref[0:1, :]
        dgt_ref[...] = g_ref[...] * acc_ref[0:1, :]

    row, vec = _row_specs(s, d, ts)
    vshape = jax.ShapeDtypeStruct((1, d), F32)
    return pl.pallas_call(
        body, grid=(s // ts,), in_specs=[row, row, vec, vec], out_specs=[row, vec, vec],
        out_shape=[jax.ShapeDtypeStruct((s, d), MXU_DTYPE), vshape, vshape],
        scratch_shapes=[pltpu.VMEM((SUBLANES, d), F32)], compiler_params=_params("arbitrary"), name=name,
    )(dxn, y, g, gt)


def _loss_grad(x, tgt, name):
    s, d = x.shape
    ts = _tile(s, (512,))

    def body(x_ref, t_ref, col_ref, dx_ref):
        i = pl.program_id(0)

        @pl.when(i == 0)
        def _():
            col_ref[...] = jnp.zeros_like(col_ref)

        e = x_ref[...] - t_ref[...]
        col_ref[...] += jnp.sum(e * e, axis=0, keepdims=True)
        dx_ref[...] = e * (1.0 / d)

    row, vec = _row_specs(s, d, ts)
    return pl.pallas_call(
        body, grid=(s // ts,), in_specs=[row, row], out_specs=[vec, row],
        out_shape=[jax.ShapeDtypeStruct((1, d), F32), jax.ShapeDtypeStruct((s, d), F32)],
        compiler_params=_params("arbitrary"), name=name,
    )(x, tgt)


_GELU_C = 0.7978845608028654
_GELU_A = 0.044715


def _gelu(x):
    t = jnp.tanh(_GELU_C * (x + _GELU_A * x * x * x))
    return 0.5 * x * (1.0 + t), t


def _gelu_grad(x, t):
    return 0.5 * (1.0 + t) + 0.5 * x * (1.0 - t * t) * (_GELU_C * (1.0 + 3.0 * _GELU_A * x * x))


def _sigmoid(x):
    return 1.0 / (1.0 + jnp.exp(-x))


def _log1p_pos(y):
    u = 1.0 + y
    return jnp.where(u == 1.0, y, jnp.log(u) * (y / jnp.where(u == 1.0, 1.0, u - 1.0)))


def _softplus(x):
    return jnp.maximum(x, 0.0) + _log1p_pos(jnp.exp(-jnp.abs(x)))


def _one_minus_exp(z):
    u = jnp.exp(z)
    lg = jnp.log(jnp.where(u > 0.0, u, 1.0))
    safe = (u != 1.0) & (u > 0.0)
    return jnp.where(u == 1.0, -z, jnp.where(u > 0.0, (1.0 - u) * (z / jnp.where(safe, lg, 1.0)), 1.0))


def _halo_index(ts):
    return lambda j, t: (jnp.maximum(t * (ts // SUBLANES) - 1, 0), j)


def _ffn_mid_fwd(p, cw, cb, name):
    s, f2 = p.shape
    ts = _tile(s, (512,))
    nb, nt = f2 // (2 * CB), s // ts

    def body(p_ref, halo_ref, cw_ref, cb_ref, a_ref, scr):
        t = pl.program_id(1)
        scr[0:SUBLANES, :] = jnp.where(t > 0, halo_ref[...], 0.0)
        scr[SUBLANES:, :] = p_ref[...]
        u = (cb_ref[...] + cw_ref[0:1, :] * scr[6:6 + ts, :] + cw_ref[1:2, :] * scr[7:7 + ts, :]
             + cw_ref[2:3, :] * scr[8:8 + ts, :])
        a_ref[...] = (_gelu(u[:, :CB])[0] * u[:, CB:]).astype(a_ref.dtype)

    return pl.pallas_call(
        body, grid=(nb, nt),
        in_specs=[pl.BlockSpec((ts, 2 * CB), lambda j, t: (t, j)),
                  pl.BlockSpec((SUBLANES, 2 * CB), _halo_index(ts)),
                  pl.BlockSpec((FFN_CONV, 2 * CB), lambda j, t: (0, j)),
                  pl.BlockSpec((1, 2 * CB), lambda j, t: (0, j))],
        out_specs=pl.BlockSpec((ts, CB), lambda j, t: (t, j)),
        out_shape=jax.ShapeDtypeStruct((s, f2 // 2), MXU_DTYPE),
        scratch_shapes=[pltpu.VMEM((ts + SUBLANES, 2 * CB), F32)],
        compiler_params=_params("parallel", "arbitrary"), name=name,
    )(p, p, cw, cb)


def _ffn_mid_bwd(da, p, cw, cb, name):
    s, f2 = p.shape
    ts = _tile(s, (512,))
    nb, nt = f2 // (2 * CB), s // ts

    def body(da_ref, p_ref, halo_ref, cw_ref, cb_ref, dp_ref, dcw_ref, dcb_ref, scr, dscr):
        tt = pl.program_id(1)
        t = nt - 1 - tt
        scr[0:SUBLANES, :] = jnp.where(t > 0, halo_ref[...], 0.0)
        scr[SUBLANES:, :] = p_ref[...]
        pm2, pm1, p0 = scr[6:6 + ts, :], scr[7:7 + ts, :], scr[8:8 + ts, :]
        u = cb_ref[...] + cw_ref[0:1, :] * pm2 + cw_ref[1:2, :] * pm1 + cw_ref[2:3, :] * p0
        g, v = u[:, :CB], u[:, CB:]
        gel, th = _gelu(g)
        dav = da_ref[...]

        @pl.when(tt == 0)
        def _():
            dscr[ts:, :] = jnp.zeros((SUBLANES, 2 * CB), F32)
            dcw_ref[...] = jnp.zeros_like(dcw_ref)
            dcb_ref[...] = jnp.zeros_like(dcb_ref)

        dscr[0:ts, 0:CB] = dav * v * _gelu_grad(g, th)
        dscr[0:ts, CB:] = dav * gel
        du, du1, du2 = dscr[0:ts, :], dscr[1:1 + ts, :], dscr[2:2 + ts, :]
        dp_ref[...] = (cw_ref[2:3, :] * du + cw_ref[1:2, :] * du1 + cw_ref[0:1, :] * du2).astype(dp_ref.dtype)
        dcb_ref[...] += jnp.sum(du, axis=0, keepdims=True)
        dcw_ref[0:1, :] += jnp.sum(du * pm2, axis=0, keepdims=True)
        dcw_ref[1:2, :] += jnp.sum(du * pm1, axis=0, keepdims=True)
        dcw_ref[2:3, :] += jnp.sum(du * p0, axis=0, keepdims=True)
        dscr[ts:, :] = dscr[0:SUBLANES, :]

    rev = lambda j, t: (nt - 1 - t, j)
    return pl.pallas_call(
        body, grid=(nb, nt),
        in_specs=[pl.BlockSpec((ts, CB), rev),
                  pl.BlockSpec((ts, 2 * CB), rev),
                  pl.BlockSpec((SUBLANES, 2 * CB), lambda j, t: (jnp.maximum((nt - 1 - t) * (ts // SUBLANES) - 1, 0), j)),
                  pl.BlockSpec((FFN_CONV, 2 * CB), lambda j, t: (0, j)),
                  pl.BlockSpec((1, 2 * CB), lambda j, t: (0, j))],
        out_specs=[pl.BlockSpec((ts, 2 * CB), rev),
                   pl.BlockSpec((FFN_CONV, 2 * CB), lambda j, t: (0, j)),
                   pl.BlockSpec((1, 2 * CB), lambda j, t: (0, j))],
        out_shape=[jax.ShapeDtypeStruct((s, f2), MXU_DTYPE), jax.ShapeDtypeStruct((FFN_CONV, f2), F32),
                   jax.ShapeDtypeStruct((1, f2), F32)],
        scratch_shapes=[pltpu.VMEM((ts + SUBLANES, 2 * CB), F32), pltpu.VMEM((ts + SUBLANES, 2 * CB), F32)],
        compiler_params=_params("parallel", "arbitrary"), name=name,
    )(da, p, p, cw, cb)


def _rg_gates(xc, wa_ref, ba_ref, wx_ref, bx_ref, lam_ref):
    r = _sigmoid(_dot_nn(xc, wa_ref[0]) + ba_ref[...])
    ig = _sigmoid(_dot_nn(xc, wx_ref[0]) + bx_ref[...])
    sp = _softplus(-lam_ref[...])
    log_a = (-RG_C) * r * sp
    a = jnp.exp(log_a)
    mult = jnp.sqrt(_one_minus_exp(2.0 * log_a))
    return r, ig, sp, a, mult


def _rg_conv(scr, cw_ref, cb_ref, ts):
    views = [scr[5 + k:5 + k + ts, :] for k in range(RG_CONV)]
    xc = cb_ref[...]
    for k in range(RG_CONV):
        xc = xc + cw_ref[k:k + 1, :] * views[k]
    return xc, views


def _rg_param_specs():
    vec = pl.BlockSpec((1, CB), lambda g, t: (0, g))
    mat = pl.BlockSpec((1, CB, CB), lambda g, t: (g, 0, 0))
    return [pl.BlockSpec((RG_CONV, CB), lambda g, t: (0, g)), vec, mat, vec, mat, vec, vec]


def _rg_mid_fwd(pj, cw, cb, wa, ba, wx, bx, lam, name):
    s = pj.shape[0]
    nb = pj.shape[1] // (2 * CB)
    ts = _tile(s, (512,))
    nt = s // ts

    def body(pj_ref, halo_ref, cw_ref, cb_ref, wa_ref, ba_ref, wx_ref, bx_ref, lam_ref, y_ref, hs_ref,
             scr, a_scr, u_scr, h_scr):
        t = pl.program_id(1)

        @pl.when(t == 0)
        def _():
            h_scr[...] = jnp.zeros_like(h_scr)

        scr[0:SUBLANES, :] = jnp.where(t > 0, halo_ref[:, CB:], 0.0)
        scr[SUBLANES:, :] = pj_ref[:, CB:]
        xc, _ = _rg_conv(scr, cw_ref, cb_ref, ts)
        _, ig, _, a, mult = _rg_gates(xc, wa_ref, ba_ref, wx_ref, bx_ref, lam_ref)
        a_scr[...] = a
        u_scr[...] = mult * (ig * xc)

        def row(i, h):
            h = a_scr[pl.ds(i, 1), :] * h + u_scr[pl.ds(i, 1), :]
            hs_ref[pl.ds(i, 1), :] = h
            return h

        h_scr[0:1, :] = lax.fori_loop(0, ts, row, h_scr[0:1, :], unroll=8)
        y_ref[...] = (_gelu(pj_ref[:, :CB])[0] * hs_ref[...]).astype(y_ref.dtype)

    blk = pl.BlockSpec((ts, CB), lambda g, t: (t, g))
    return pl.pallas_call(
        body, grid=(nb, nt),
        in_specs=[pl.BlockSpec((ts, 2 * CB), lambda g, t: (t, g)),
                  pl.BlockSpec((SUBLANES, 2 * CB), _halo_index(ts))] + _rg_param_specs(),
        out_specs=[blk, blk],
        out_shape=[jax.ShapeDtypeStruct((s, nb * CB), MXU_DTYPE), jax.ShapeDtypeStruct((s, nb * CB), F32)],
        scratch_shapes=[pltpu.VMEM((ts + SUBLANES, CB), F32), pltpu.VMEM((ts, CB), F32), pltpu.VMEM((ts, CB), F32),
                        pltpu.VMEM((SUBLANES, CB), F32)],
        compiler_params=_params("parallel", "arbitrary"), name=name,
    )(pj, pj, cw, cb, wa, ba, wx, bx, lam)


def _rg_mid_bwd(dy, pj, hs, cw, cb, wa, ba, wx, bx, lam, name):
    s = pj.shape[0]
    nb = pj.shape[1] // (2 * CB)
    ts = _tile(s, (512,))
    nt = s // ts

    def body(dy_ref, pj_ref, halo_ref, hs_ref, hsh_ref, cw_ref, cb_ref, wa_ref, ba_ref, wx_ref, bx_ref, lam_ref,
             dpj_ref, dcw_ref, dcb_ref, dwa_ref, dba_ref, dwx_ref, dbx_ref, dlam_ref,
             scr, hscr, a_scr, d_scr, g_scr, dxscr, c_scr):
        tt = pl.program_id(1)
        t = nt - 1 - tt

        @pl.when(tt == 0)
        def _():
            c_scr[...] = jnp.zeros_like(c_scr)
            dxscr[ts:, :] = jnp.zeros((SUBLANES, CB), F32)
            for ref in (dcw_ref, dcb_ref, dwa_ref, dba_ref, dwx_ref, dbx_ref, dlam_ref):
                ref[...] = jnp.zeros_like(ref)

        scr[0:SUBLANES, :] = jnp.where(t > 0, halo_ref[:, CB:], 0.0)
        scr[SUBLANES:, :] = pj_ref[:, CB:]
        hscr[0:SUBLANES, :] = jnp.where(t > 0, hsh_ref[...], 0.0)
        hscr[SUBLANES:, :] = hs_ref[...]
        xc, views = _rg_conv(scr, cw_ref, cb_ref, ts)
        r, ig, sp, a, mult = _rg_gates(xc, wa_ref, ba_ref, wx_ref, bx_ref, lam_ref)
        gate = pj_ref[:, :CB]
        gel, th = _gelu(gate)
        dyv = dy_ref[...]
        dpj_ref[:, :CB] = (dyv * hs_ref[...] * _gelu_grad(gate, th)).astype(dpj_ref.dtype)
        a_scr[...] = a
        d_scr[...] = dyv * gel

        def row(k, c):
            i = ts - 1 - k
            gv = d_scr[pl.ds(i, 1), :] + c
            g_scr[pl.ds(i, 1), :] = gv
            return a_scr[pl.ds(i, 1), :] * gv

        c_scr[0:1, :] = lax.fori_loop(0, ts, row, c_scr[0:1, :], unroll=8)
        du = g_scr[...]
        da = du * hscr[7:7 + ts, :]
        dmult = du * (ig * xc)
        dig = du * (mult * xc)
        dxc = du * (mult * ig)
        dlog_a = da * a - dmult * (a * a / mult)
        dlam_ref[...] += jnp.sum(dlog_a * r, axis=0, keepdims=True) * (RG_C * _sigmoid(-lam_ref[...]))
        dpr = dlog_a * ((-RG_C) * sp) * (r * (1.0 - r))
        dpi = dig * (ig * (1.0 - ig))
        dba_ref[...] += jnp.sum(dpr, axis=0, keepdims=True)
        dbx_ref[...] += jnp.sum(dpi, axis=0, keepdims=True)
        dwa_ref[0] += _dot_tn(xc, dpr)
        dwx_ref[0] += _dot_tn(xc, dpi)
        dxc = dxc + _dot_nt(dpr, wa_ref[0]) + _dot_nt(dpi, wx_ref[0])
        dcb_ref[...] += jnp.sum(dxc, axis=0, keepdims=True)
        for k in range(RG_CONV):
            dcw_ref[k:k + 1, :] += jnp.sum(dxc * views[k], axis=0, keepdims=True)
        dxscr[0:ts, :] = dxc
        dxp = cw_ref[3:4, :] * dxc
        for k in range(RG_CONV - 1):
            dxp = dxp + cw_ref[k:k + 1, :] * dxscr[3 - k:3 - k + ts, :]
        dpj_ref[:, CB:] = dxp.astype(dpj_ref.dtype)
        dxscr[ts:, :] = dxscr[0:SUBLANES, :]

    rev = lambda g, t: (nt - 1 - t, g)
    rev_halo = lambda g, t: (jnp.maximum((nt - 1 - t) * (ts // SUBLANES) - 1, 0), g)
    vec = pl.BlockSpec((1, CB), lambda g, t: (0, g))
    mat = pl.BlockSpec((1, CB, CB), lambda g, t: (g, 0, 0))
    d = nb * CB
    vshape = jax.ShapeDtypeStruct((1, d), F32)
    mshape = jax.ShapeDtypeStruct((nb, CB, CB), F32)
    return pl.pallas_call(
        body, grid=(nb, nt),
        in_specs=[pl.BlockSpec((ts, CB), rev), pl.BlockSpec((ts, 2 * CB), rev), pl.BlockSpec((SUBLANES, 2 * CB), rev_halo),
                  pl.BlockSpec((ts, CB), rev), pl.BlockSpec((SUBLANES, CB), rev_halo)] + _rg_param_specs(),
        out_specs=[pl.BlockSpec((ts, 2 * CB), rev), pl.BlockSpec((RG_CONV, CB), lambda g, t: (0, g)), vec, mat, vec, mat, vec, vec],
        out_shape=[jax.ShapeDtypeStruct((s, 2 * d), MXU_DTYPE), jax.ShapeDtypeStruct((RG_CONV, d), F32), vshape, mshape, vshape,
                   mshape, vshape, vshape],
        scratch_shapes=[pltpu.VMEM((ts + SUBLANES, CB), F32), pltpu.VMEM((ts + SUBLANES, CB), F32), pltpu.VMEM((ts, CB), F32),
                        pltpu.VMEM((ts, CB), F32), pltpu.VMEM((ts, CB), F32), pltpu.VMEM((ts + SUBLANES, CB), F32),
                        pltpu.VMEM((SUBLANES, CB), F32)],
        compiler_params=_params("parallel", "arbitrary"), name=name,
    )(dy, pj, pj, hs, hs, cw, cb, wa, ba, wx, bx, lam)


GLA_DK = 128
GLA_DV = 256
GLA_HB = 2 * GLA_DK + 2 * GLA_DV + LANES
GLA_TS = 256


def _split3(x):
    hi = x.astype(BF16)
    r1 = x - hi.astype(F32)
    mid = r1.astype(BF16)
    lo = (r1 - mid.astype(F32)).astype(BF16)
    return hi, mid, lo


def _chunk_cumsum(x, reverse):
    n = x.shape[0]
    i = lax.broadcasted_iota(jnp.int32, (n, n), 0)
    j = lax.broadcasted_iota(jnp.int32, (n, n), 1)
    same = (i // GLA_CHUNK) == (j // GLA_CHUNK)
    tri = jnp.where(same & ((j >= i) if reverse else (j <= i)), 1.0, 0.0).astype(BF16)
    out = jnp.zeros(x.shape, F32)
    for piece in _split3(x):
        out = out + lax.dot_general(tri, piece, (((1,), (0,)), ((), ())), preferred_element_type=F32)
    return out


def _gla_split(blk):
    q = blk[:, 0:GLA_DK] * (GLA_DK ** -0.5)
    k = blk[:, GLA_DK:2 * GLA_DK]
    v = blk[:, 2 * GLA_DK:2 * GLA_DK + GLA_DV]
    r = blk[:, 2 * GLA_DK + GLA_DV:2 * GLA_DK + 2 * GLA_DV]
    z = blk[:, 2 * GLA_DK + 2 * GLA_DV:]
    return q, k, v, r, z


def _gla_decays(gc):
    gref = gc[GLA_CHUNK // 2:GLA_CHUNK // 2 + 1, :]
    glast = gc[GLA_CHUNK - 1:GLA_CHUNK, :]
    return jnp.exp(gc), jnp.exp(gc - gref), jnp.exp(gref - gc), jnp.exp(glast - gc), jnp.exp(glast)


def _causal_mask():
    i = lax.broadcasted_iota(jnp.int32, (GLA_CHUNK, GLA_CHUNK), 0)
    j = lax.broadcasted_iota(jnp.int32, (GLA_CHUNK, GLA_CHUNK), 1)
    return j <= i


def _log_sigmoid(x):
    return jnp.minimum(x, 0.0) - _log1p_pos(jnp.exp(-jnp.abs(x)))


def _gla_mid_fwd(pj, wal, bal, ng, name):
    s = pj.shape[0]
    nh = pj.shape[1] // GLA_HB
    ts = _tile(s, (GLA_TS,))
    nt, nc = s // ts, ts // GLA_CHUNK

    def body(pj_ref, wal_ref, bal_ref, ng_ref, act_ref, o_ref, st_ref, s_scr):
        t = pl.program_id(1)

        @pl.when(t == 0)
        def _():
            s_scr[...] = jnp.zeros_like(s_scr)

        q, k, v, r, z = _gla_split(pj_ref[...])
        g = _log_sigmoid(_dot_nn(z, wal_ref[0]) + bal_ref[0]) * (1.0 / GLA_TAU)
        gcum = _chunk_cumsum(g, False)
        mask = _causal_mask()
        for c in range(nc):
            sl = slice(c * GLA_CHUNK, (c + 1) * GLA_CHUNK)
            eg, eq, ek, ekd, egl = _gla_decays(gcum[sl])
            st = s_scr[...]
            st_ref[c, 0] = st
            attn = jnp.where(mask, _dot_nt(q[sl] * eq, k[sl] * ek), 0.0)
            o_ref[sl, :] = _dot_nt(q[sl] * eg, st) + _dot_nn(attn, v[sl])
            s_scr[...] = st * egl + _dot_tn(v[sl], k[sl] * ekd)
        o = o_ref[...]
        on = o * lax.rsqrt(jnp.mean(o * o, axis=-1, keepdims=True) + EPS)
        act_ref[...] = ((on * ng_ref[...]) * (r * _sigmoid(r))).astype(act_ref.dtype)

    blk = pl.BlockSpec((ts, GLA_DV), lambda h, t: (t, h))
    return pl.pallas_call(
        body, grid=(nh, nt),
        in_specs=[pl.BlockSpec((ts, GLA_HB), lambda h, t: (t, h)),
                  pl.BlockSpec((1, LANES, GLA_DK), lambda h, t: (h, 0, 0)),
                  pl.BlockSpec((1, 1, GLA_DK), lambda h, t: (h, 0, 0)),
                  pl.BlockSpec((1, GLA_DV), lambda h, t: (0, 0))],
        out_specs=[blk, blk, pl.BlockSpec((nc, 1, GLA_DV, GLA_DK), lambda h, t: (t, h, 0, 0))],
        out_shape=[jax.ShapeDtypeStruct((s, nh * GLA_DV), MXU_DTYPE), jax.ShapeDtypeStruct((s, nh * GLA_DV), F32),
                   jax.ShapeDtypeStruct((s // GLA_CHUNK, nh, GLA_DV, GLA_DK), F32)],
        scratch_shapes=[pltpu.VMEM((GLA_DV, GLA_DK), F32)],
        compiler_params=_params("parallel", "arbitrary"), name=name,
    )(pj, wal, bal, ng)


def _gla_mid_bwd(dact, pj, o, st, wal, bal, ng, name):
    s = pj.shape[0]
    nh = pj.shape[1] // GLA_HB
    ts = _tile(s, (GLA_TS,))
    nt, nc = s // ts, ts // GLA_CHUNK

    def body(dact_ref, pj_ref, o_ref, st_ref, wal_ref, bal_ref, ng_ref, dpj_ref, dwal_ref, dbal_ref, dng_ref,
             ds_scr, dg_scr):
        h, tt = pl.program_id(0), pl.program_id(1)

        @pl.when(tt == 0)
        def _():
            ds_scr[...] = jnp.zeros_like(ds_scr)
            dwal_ref[...] = jnp.zeros_like(dwal_ref)
            dbal_ref[...] = jnp.zeros_like(dbal_ref)

        @pl.when((tt == 0) & (h == 0))
        def _():
            dng_ref[...] = jnp.zeros_like(dng_ref)

        q, k, v, r, z = _gla_split(pj_ref[...])
        logit = _dot_nn(z, wal_ref[0]) + bal_ref[0]
        gcum = _chunk_cumsum(_log_sigmoid(logit) * (1.0 / GLA_TAU), False)
        ov = o_ref[...]
        ro = lax.rsqrt(jnp.mean(ov * ov, axis=-1, keepdims=True) + EPS)
        on = ov * ro
        sg = _sigmoid(r)
        sil = r * sg
        dav = dact_ref[...]
        dpj_ref[:, 2 * GLA_DK + GLA_DV:2 * GLA_DK + 2 * GLA_DV] = (
            dav * (on * ng_ref[...]) * (sg + sil * (1.0 - sg))).astype(dpj_ref.dtype)
        t1 = dav * sil
        dng_ref[...] += jnp.sum(t1 * on, axis=0, keepdims=True)
        dn = t1 * ng_ref[...]
        do = ro * (dn - on * jnp.mean(dn * on, axis=-1, keepdims=True))
        mask = _causal_mask()
        scale = GLA_DK ** -0.5
        for c in reversed(range(nc)):
            sl = slice(c * GLA_CHUNK, (c + 1) * GLA_CHUNK)
            eg, eq, ek, ekd, egl = _gla_decays(gcum[sl])
            qc, kc, vc, doc = q[sl], k[sl], v[sl], do[sl]
            qg, qt, kt, kd = qc * eg, qc * eq, kc * ek, kc * ekd
            sp = st_ref[c, 0]
            ds = ds_scr[...]
            attn = jnp.where(mask, _dot_nt(qt, kt), 0.0)
            dattn = jnp.where(mask, _dot_nt(doc, vc), 0.0)
            dqg = _dot_nn(doc, sp)
            dqt = _dot_nn(dattn, kt)
            dkt = _dot_tn(dattn, qt)
            dkd = _dot_nn(vc, ds)
            dpj_ref[sl, 2 * GLA_DK:2 * GLA_DK + GLA_DV] = (_dot_tn(attn, doc) + _dot_nt(kd, ds)).astype(dpj_ref.dtype)
            dpj_ref[sl, 0:GLA_DK] = (scale * (dqg * eg + dqt * eq)).astype(dpj_ref.dtype)
            dpj_ref[sl, GLA_DK:2 * GLA_DK] = (dkt * ek + dkd * ekd).astype(dpj_ref.dtype)
            kdd = dkd * kd
            dgl = jnp.sum(kdd, axis=0, keepdims=True) + jnp.sum(ds * sp, axis=0, keepdims=True) * egl
            row = lax.broadcasted_iota(jnp.int32, (GLA_CHUNK, GLA_DK), 0)
            dg_scr[sl, :] = dqg * qg + dqt * qt - dkt * kt - kdd + jnp.where(row == GLA_CHUNK - 1, dgl, 0.0)
            ds_scr[...] = ds * egl + _dot_tn(doc, qg)
        dlogit = _chunk_cumsum(dg_scr[...], True) * (1.0 / GLA_TAU) * _sigmoid(-logit)
        dpj_ref[:, 2 * GLA_DK + 2 * GLA_DV:] = _dot_nt(dlogit, wal_ref[0]).astype(dpj_ref.dtype)
        dwal_ref[0] += _dot_tn(z, dlogit)
        dbal_ref[0] += jnp.sum(dlogit, axis=0, keepdims=True)

    rev = lambda h, t: (nt - 1 - t, h)
    return pl.pallas_call(
        body, grid=(nh, nt),
        in_specs=[pl.BlockSpec((ts, GLA_DV), rev), pl.BlockSpec((ts, GLA_HB), rev), pl.BlockSpec((ts, GLA_DV), rev),
                  pl.BlockSpec((nc, 1, GLA_DV, GLA_DK), lambda h, t: (nt - 1 - t, h, 0, 0)),
                  pl.BlockSpec((1, LANES, GLA_DK), lambda h, t: (h, 0, 0)),
                  pl.BlockSpec((1, 1, GLA_DK), lambda h, t: (h, 0, 0)),
                  pl.BlockSpec((1, GLA_DV), lambda h, t: (0, 0))],
        out_specs=[pl.BlockSpec((ts, GLA_HB), rev),
                   pl.BlockSpec((1, LANES, GLA_DK), lambda h, t: (h, 0, 0)),
                   pl.BlockSpec((1, 1, GLA_DK), lambda h, t: (h, 0, 0)),
                   pl.BlockSpec((1, GLA_DV), lambda h, t: (0, 0))],
        out_shape=[jax.ShapeDtypeStruct((s, nh * GLA_HB), MXU_DTYPE), jax.ShapeDtypeStruct((nh, LANES, GLA_DK), F32),
                   jax.ShapeDtypeStruct((nh, 1, GLA_DK), F32), jax.ShapeDtypeStruct((1, GLA_DV), F32)],
        scratch_shapes=[pltpu.VMEM((GLA_DV, GLA_DK), F32), pltpu.VMEM((ts, GLA_DK), F32)],
        compiler_params=_params("arbitrary", "arbitrary"), name=name,
    )(dact, pj, o, st, wal, bal, ng)


def _adamw(w, g, m, v, name):
    rows, cols = w.shape
    tr = _tile(rows, (256, 128, 64, 32, 16, 8))
    c1 = 1.0 / (1.0 - ADAM_B1 ** ADAM_STEP)
    c2 = 1.0 / (1.0 - ADAM_B2 ** ADAM_STEP)

    def body(w_ref, g_ref, m_ref, v_ref, d_ref, mo_ref, vo_ref):
        gv = g_ref[...]
        m2 = ADAM_B1 * m_ref[...] + (1.0 - ADAM_B1) * gv
        v2 = ADAM_B2 * v_ref[...] + (1.0 - ADAM_B2) * (gv * gv)
        d_ref[...] = (-ADAM_LR) * ((m2 * c1) / (jnp.sqrt(v2 * c2) + ADAM_EPS) + ADAM_WD * w_ref[...])
        mo_ref[...] = m2
        vo_ref[...] = v2

    spec = pl.BlockSpec((tr, cols), lambda i: (i, 0))
    shape = jax.ShapeDtypeStruct((rows, cols), F32)
    return pl.pallas_call(
        body, grid=(rows // tr,), in_specs=[spec] * 4, out_specs=[spec] * 3, out_shape=[shape] * 3,
        compiler_params=_params("parallel"), name=name,
    )(w, g, m, v)


def _pair_cols(w):
    lead, n = w.shape[:-1], w.shape[-1]
    return jnp.swapaxes(w.reshape(*lead, 2, n // (2 * CB), CB), -3, -2).reshape(*lead, n)


def _unpair_cols(w):
    lead, n = w.shape[:-1], w.shape[-1]
    return jnp.swapaxes(w.reshape(*lead, n // (2 * CB), 2, CB), -3, -2).reshape(*lead, n)


def _gla_head_cols(w):
    d = w.shape[0]
    qk, dv = GLA_HEADS * GLA_DK, GLA_HEADS * GLA_DV
    q, k, v, r, z = jnp.split(w, [qk, 2 * qk, 2 * qk + dv, 2 * qk + 2 * dv], axis=1)
    zp = jnp.pad(z, ((0, 0), (0, LANES - GLA_RANK)))
    parts = [q.reshape(d, GLA_HEADS, GLA_DK), k.reshape(d, GLA_HEADS, GLA_DK), v.reshape(d, GLA_HEADS, GLA_DV),
             r.reshape(d, GLA_HEADS, GLA_DV), jnp.broadcast_to(zp[:, None, :], (d, GLA_HEADS, LANES))]
    return jnp.concatenate(parts, axis=2).reshape(d, GLA_HEADS * GLA_HB)


def _gla_unhead_cols(w):
    d = w.shape[0]
    w = w.reshape(d, GLA_HEADS, GLA_HB)
    o = 2 * GLA_DK + 2 * GLA_DV
    parts = [w[:, :, 0:GLA_DK].reshape(d, -1), w[:, :, GLA_DK:2 * GLA_DK].reshape(d, -1),
             w[:, :, 2 * GLA_DK:2 * GLA_DK + GLA_DV].reshape(d, -1), w[:, :, 2 * GLA_DK + GLA_DV:o].reshape(d, -1),
             jnp.sum(w[:, :, o:o + GLA_RANK], axis=1)]
    return jnp.concatenate(parts, axis=1)


def _gla_alpha_heads(w_alpha, b_alpha):
    wal = jnp.swapaxes(w_alpha.reshape(GLA_RANK, GLA_HEADS, GLA_DK), 0, 1)
    return jnp.pad(wal, ((0, 0), (0, LANES - GLA_RANK), (0, 0))), b_alpha.reshape(GLA_HEADS, 1, GLA_DK)


def _local_step(x, tgt, mod, w):
    depth = mod.shape[0]
    row = lambda v: v.reshape(1, -1)
    saved = []
    for i in range(depth):
        sh_m, sc_m, gt_m, sh_f, sc_f, gt_f = (mod[i, j:j + 1] for j in range(6))
        g0, g1, g2, g3 = (w["norm_g"][i, j:j + 1] for j in range(4))
        tag = f"_l{i}"
        h = _norm_mod_fwd(x, g0, sc_m, sh_m, "norm_mix" + tag)
        if i % 2 == 0:
            pj = _mm(h, w["rg_w_in"], name="rg_in" + tag)
            act, aux = _rg_mid_fwd(pj, w["rg_conv_w"], row(w["rg_conv_b"]), w["rg_wa"], row(w["rg_ba"]), w["rg_wx"],
                                   row(w["rg_bx"]), row(w["rg_lambda"]), "rg_mid" + tag)
            y = _mm(act, w["rg_w_out"], name="rg_out" + tag)
        else:
            pj = _mm(h, w["gla_w_in"], name="gla_in" + tag)
            act, *aux = _gla_mid_fwd(pj, w["gla_wal"], w["gla_bal"], row(w["gla_norm_g"]), "gla_mid" + tag)
            y = _mm(act, w["gla_w_out"], name="gla_out" + tag)
        x1 = _post_fwd(x, y, g1, gt_m, "post_mix" + tag)
        h2 = _norm_mod_fwd(x1, g2, sc_f, sh_f, "norm_ffn" + tag)
        p = _mm(h2, w["ffn_w_up"][i], name="ffn_up" + tag)
        a = _ffn_mid_fwd(p, w["ffn_conv_w"][i], w["ffn_conv_b"][i:i + 1], "ffn_mid" + tag)
        y2 = _mm(a, w["ffn_w_down"][i], name="ffn_down" + tag)
        x2 = _post_fwd(x1, y2, g3, gt_f, "post_ffn" + tag)
        saved.append((x, h, pj, act, aux, y, x1, h2, p, a, y2))
        x = x2

    cols, dx = _loss_grad(x, tgt, "loss")

    gr = {k: [None] * depth for k in ("norm_g", "ffn_w_up", "ffn_conv_w", "ffn_conv_b", "ffn_w_down", "mod")}
    for i in reversed(range(depth)):
        x0, h, pj, act, aux, y, x1, h2, p, a, y2 = saved[i]
        sh_m, sc_m, gt_m, sh_f, sc_f, gt_f = (mod[i, j:j + 1] for j in range(6))
        g0, g1, g2, g3 = (w["norm_g"][i, j:j + 1] for j in range(4))
        tag = f"_l{i}"
        dy2, d_g3, d_gt_f = _post_bwd(dx, y2, g3, gt_f, "post_ffn_b" + tag)
        da = _mm(dy2, w["ffn_w_down"][i], tb=True, name="ffn_down_dx" + tag)
        gr["ffn_w_down"][i] = _mm(a, dy2, ta=True, name="ffn_down_dw" + tag)
        dp, dcw, dcb = _ffn_mid_bwd(da, p, w["ffn_conv_w"][i], w["ffn_conv_b"][i:i + 1], "ffn_mid_b" + tag)
        gr["ffn_conv_w"][i], gr["ffn_conv_b"][i] = _unpair_cols(dcw), _unpair_cols(dcb)[0]
        dh2 = _mm(dp, w["ffn_w_up"][i], tb=True, name="ffn_up_dx" + tag)
        gr["ffn_w_up"][i] = _unpair_cols(_mm(h2, dp, ta=True, name="ffn_up_dw" + tag))
        dx1, d_g2, d_sc_f, d_sh_f = _norm_mod_bwd(dh2, x1, g2, sc_f, dx, "norm_ffn_b" + tag)
        dy, d_g1, d_gt_m = _post_bwd(dx1, y, g1, gt_m, "post_mix_b" + tag)
        if i % 2 == 0:
            dact = _mm(dy, w["rg_w_out"], tb=True, name="rg_out_dx" + tag)
            gr["rg_w_out"] = _mm(act, dy, ta=True, name="rg_out_dw" + tag)
            dpj, gr["rg_conv_w"], d_cb, gr["rg_wa"], d_ba, gr["rg_wx"], d_bx, d_lam = _rg_mid_bwd(
                dact, pj, aux, w["rg_conv_w"], row(w["rg_conv_b"]), w["rg_wa"], row(w["rg_ba"]), w["rg_wx"],
                row(w["rg_bx"]), row(w["rg_lambda"]), "rg_mid_b" + tag)
            gr["rg_conv_b"], gr["rg_ba"], gr["rg_bx"], gr["rg_lambda"] = d_cb[0], d_ba[0], d_bx[0], d_lam[0]
            dh = _mm(dpj, w["rg_w_in"], tb=True, name="rg_in_dx" + tag)
            gr["rg_w_in"] = _unpair_cols(_mm(h, dpj, ta=True, name="rg_in_dw" + tag))
        else:
            dact = _mm(dy, w["gla_w_out"], tb=True, name="gla_out_dx" + tag)
            gr["gla_w_out"] = _mm(act, dy, ta=True, name="gla_out_dw" + tag)
            dpj, d_wal, d_bal, d_ng = _gla_mid_bwd(dact, pj, aux[0], aux[1], w["gla_wal"], w["gla_bal"],
                                                   row(w["gla_norm_g"]), "gla_mid_b" + tag)
            gr["gla_w_alpha"] = jnp.swapaxes(d_wal[:, :GLA_RANK, :], 0, 1).reshape(GLA_RANK, GLA_HEADS * GLA_DK)
            gr["gla_b_alpha"], gr["gla_norm_g"] = d_bal.reshape(-1), d_ng[0]
            dh = _mm(dpj, w["gla_w_in"], tb=True, name="gla_in_dx" + tag)
            gr["gla_w_in"] = _gla_unhead_cols(_mm(h, dpj, ta=True, name="gla_in_dw" + tag))
        dx, d_g0, d_sc_m, d_sh_m = _norm_mod_bwd(dh, x0, g0, sc_m, dx1, "norm_mix_b" + tag)
        gr["norm_g"][i] = jnp.concatenate([d_g0, d_g1, d_g2, d_g3], axis=0)
        gr["mod"][i] = jnp.concatenate([d_sh_m, d_sc_m, d_gt_m, d_sh_f, d_sc_f, d_gt_f], axis=0)
    for k in ("norm_g", "ffn_w_up", "ffn_conv_w", "ffn_conv_b", "ffn_w_down", "mod"):
        gr[k] = jnp.stack(gr[k])
    return cols, dx, gr


def _kernel_weights(full):
    w = dict(full)
    w["ffn_w_up"] = _pair_cols(full["ffn_w_up"])
    w["ffn_conv_w"] = _pair_cols(full["ffn_conv_w"])
    w["ffn_conv_b"] = _pair_cols(full["ffn_conv_b"])
    w["rg_w_in"] = _pair_cols(full["rg_w_in"])
    w["gla_w_in"] = _gla_head_cols(full["gla_w_in"])
    w["gla_wal"], w["gla_bal"] = _gla_alpha_heads(full["gla_w_alpha"], full["gla_b_alpha"])
    return w


ADA_ROWS = 16


def _ada_fwd(c16, ada_w, ada_b, name):
    depth, d, n = ada_w.shape
    tn = _tile(n, (512, 256, 128))

    def body(c_ref, w_ref, b_ref, o_ref):
        cv = c_ref[...]
        o_ref[0] = _dot_nn(cv * _sigmoid(cv), w_ref[0]) + b_ref[0]

    return pl.pallas_call(
        body, grid=(depth, n // tn),
        in_specs=[pl.BlockSpec((ADA_ROWS, d), lambda l, j: (0, 0)), pl.BlockSpec((1, d, tn), lambda l, j: (l, 0, j)),
                  pl.BlockSpec((1, 1, tn), lambda l, j: (l, 0, j))],
        out_specs=pl.BlockSpec((1, ADA_ROWS, tn), lambda l, j: (l, 0, j)),
        out_shape=jax.ShapeDtypeStruct((depth, ADA_ROWS, n), F32),
        compiler_params=_params("parallel", "parallel"), name=name,
    )(c16, ada_w, ada_b)


def _ada_bwd(c16, dmod16, name):
    depth, _, n = dmod16.shape
    d = c16.shape[1]
    tn = _tile(n, (512, 256, 128))

    def body(c_ref, dm_ref, o_ref):
        cv = c_ref[...]
        o_ref[0] = _dot_tn(cv * _sigmoid(cv), dm_ref[0])

    return pl.pallas_call(
        body, grid=(depth, n // tn),
        in_specs=[pl.BlockSpec((ADA_ROWS, d), lambda l, j: (0, 0)), pl.BlockSpec((1, ADA_ROWS, tn), lambda l, j: (l, 0, j))],
        out_specs=pl.BlockSpec((1, d, tn), lambda l, j: (l, 0, j)),
        out_shape=jax.ShapeDtypeStruct((depth, d, n), F32),
        compiler_params=_params("parallel", "parallel"), name=name,
    )(c16, dmod16)


PACK_COLS = 1024
N_DEV = 8
N_CHIP = 4
_ANY = pl.BlockSpec(memory_space=pl.ANY)
_VMEM = pl.BlockSpec(memory_space=pltpu.VMEM)


def _place():
    return lax.axis_index("x"), lax.axis_index("y"), lax.axis_index("c")


def _other_chips(x, y):
    return [(1 - x, y), (x, 1 - y), (1 - x, 1 - y)]


def _rcopy(src, dst, send_sems, recv_sems, k, peer):
    return pltpu.make_async_remote_copy(src_ref=src, dst_ref=dst, send_sem=send_sems.at[k], recv_sem=recv_sems.at[k],
                                        device_id=peer, device_id_type=MESH)


def _all_gather_8(v, name):
    r, cc = v.shape

    def body(v_ref, out_ref, send_sems, recv_sems, local_sem):
        x, y, c = _place()
        me = 4 * x + 2 * y + c
        mine = pltpu.make_async_copy(v_ref, out_ref.at[me], local_sem)
        mine.start()
        peers = []
        for k in range(1, N_DEV):
            px = 1 - x if k & 4 else x
            py = 1 - y if k & 2 else y
            pc = 1 - c if k & 1 else c
            peers.append((px, py, pc))
        sends = [_rcopy(v_ref, out_ref.at[me], send_sems, recv_sems, k, p) for k, p in enumerate(peers)]
        for cp in sends:
            cp.start()
        for k, (px, py, pc) in enumerate(peers):
            _rcopy(v_ref, out_ref.at[4 * px + 2 * py + pc], send_sems, recv_sems, k, (px, py, pc)).wait_recv()
        for cp in sends:
            cp.wait_send()
        mine.wait()

    return pl.pallas_call(
        body, in_specs=[_VMEM], out_specs=_VMEM, out_shape=jax.ShapeDtypeStruct((N_DEV, r, cc), v.dtype),
        scratch_shapes=[pltpu.SemaphoreType.DMA((N_DEV - 1,)), pltpu.SemaphoreType.DMA((N_DEV - 1,)), pltpu.SemaphoreType.DMA],
        compiler_params=pltpu.CompilerParams(vmem_limit_bytes=VMEM_LIMIT), name=name,
    )(v)


def _gather_chips(buf, name):
    r, cc = buf.shape
    half = r // 2

    def body(buf_ref, out_ref, send_sems, recv_sems, local_sem):
        x, y, c = _place()
        chip = 2 * x + y
        mine_rows, their_rows = pl.ds(c * half, half), pl.ds((1 - c) * half, half)
        mine = pltpu.make_async_copy(buf_ref, out_ref.at[chip], local_sem)
        mine.start()
        chips = _other_chips(x, y)
        first = [_rcopy(buf_ref.at[mine_rows], out_ref.at[chip, mine_rows], send_sems, recv_sems, j, (px, py, c))
                 for j, (px, py) in enumerate(chips)]
        for cp in first:
            cp.start()
        passed = []
        for j, (px, py) in enumerate(chips):
            landed = out_ref.at[2 * px + py, mine_rows]
            _rcopy(buf_ref.at[mine_rows], landed, send_sems, recv_sems, j, (px, py, c)).wait_recv()
            fw = _rcopy(landed, landed, send_sems, recv_sems, N_CHIP - 1 + j, (x, y, 1 - c))
            fw.start()
            passed.append(fw)
        for j, (px, py) in enumerate(chips):
            landed = out_ref.at[2 * px + py, their_rows]
            _rcopy(landed, landed, send_sems, recv_sems, N_CHIP - 1 + j, (x, y, 1 - c)).wait_recv()
        for cp in first + passed:
            cp.wait_send()
        mine.wait()

    n_sem = 2 * (N_CHIP - 1)
    return pl.pallas_call(
        body, in_specs=[_ANY], out_specs=_ANY, out_shape=jax.ShapeDtypeStruct((N_CHIP, r, cc), buf.dtype),
        scratch_shapes=[pltpu.SemaphoreType.DMA((n_sem,)), pltpu.SemaphoreType.DMA((n_sem,)), pltpu.SemaphoreType.DMA],
        name=name,
    )(buf)


def _pair_exchange(g, name):
    n, _, half, cc = g.shape

    def body(g_ref, out_ref, send_sems, recv_sems):
        x, y, c = _place()
        cp = _rcopy(g_ref.at[:, 1 - c], out_ref, send_sems, recv_sems, 0, (x, y, 1 - c))
        cp.start()
        cp.wait()

    return pl.pallas_call(
        body, in_specs=[_ANY], out_specs=_ANY, out_shape=jax.ShapeDtypeStruct((n, half, cc), g.dtype),
        scratch_shapes=[pltpu.SemaphoreType.DMA((1,)), pltpu.SemaphoreType.DMA((1,))], name=name,
    )(g)


def _pair_sum(g, other, c_idx, name):
    n, _, half, cc = g.shape
    tr = _tile(half, (640, 512, 256, 128, 64, 32, 16))

    def body(c_ref, g_ref, o_ref, out_ref):
        out_ref[...] = (g_ref[0] + o_ref[...]).astype(out_ref.dtype)

    return pl.pallas_call(
        body,
        grid_spec=pltpu.PrefetchScalarGridSpec(
            num_scalar_prefetch=1, grid=(n, half // tr),
            in_specs=[pl.BlockSpec((1, 1, tr, cc), lambda k, i, c_ref: (k, c_ref[0], i, 0)),
                      pl.BlockSpec((1, tr, cc), lambda k, i, c_ref: (k, i, 0))],
            out_specs=pl.BlockSpec((1, tr, cc), lambda k, i, c_ref: (k, i, 0))),
        out_shape=jax.ShapeDtypeStruct((n, half, cc), BF16),
        compiler_params=_params("parallel", "parallel"), name=name,
    )(c_idx, g, other)


def _chip_exchange(p, name):
    n, half, cc = p.shape

    def body(p_ref, out_ref, send_sems, recv_sems, local_sem):
        x, y, c = _place()
        chip = 2 * x + y
        mine = pltpu.make_async_copy(p_ref.at[chip], out_ref.at[chip], local_sem)
        mine.start()
        chips = _other_chips(x, y)
        sends = [_rcopy(p_ref.at[2 * px + py], out_ref.at[chip], send_sems, recv_sems, j, (px, py, c))
                 for j, (px, py) in enumerate(chips)]
        for cp in sends:
            cp.start()
        for j, (px, py) in enumerate(chips):
            _rcopy(p_ref.at[chip], out_ref.at[2 * px + py], send_sems, recv_sems, j, (px, py, c)).wait_recv()
        for cp in sends:
            cp.wait_send()
        mine.wait()

    return pl.pallas_call(
        body, in_specs=[_ANY], out_specs=_ANY, out_shape=jax.ShapeDtypeStruct((n, half, cc), p.dtype),
        scratch_shapes=[pltpu.SemaphoreType.DMA((N_CHIP - 1,)), pltpu.SemaphoreType.DMA((N_CHIP - 1,)), pltpu.SemaphoreType.DMA],
        name=name,
    )(p)


def _sum_lead(v, name):
    n, r, cc = v.shape
    tr = _tile(r, (640, 512, 256, 128, 64, 32, 16, 8))

    def body(v_ref, o_ref):
        acc = v_ref[0].astype(F32)
        for k in range(1, n):
            acc = acc + v_ref[k].astype(F32)
        o_ref[...] = acc

    return pl.pallas_call(
        body, grid=(r // tr,), in_specs=[pl.BlockSpec((n, tr, cc), lambda i: (0, i, 0))],
        out_specs=pl.BlockSpec((tr, cc), lambda i: (i, 0)), out_shape=jax.ShapeDtypeStruct((r, cc), F32),
        compiler_params=_params("parallel"), name=name,
    )(v)


def _pair_share(red, name):
    half, cc = red.shape

    def body(r_ref, out_ref, send_sems, recv_sems, local_sem):
        x, y, c = _place()
        mine = pltpu.make_async_copy(r_ref, out_ref.at[c], local_sem)
        mine.start()
        cp = _rcopy(r_ref, out_ref.at[c], send_sems, recv_sems, 0, (x, y, 1 - c))
        cp.start()
        _rcopy(r_ref, out_ref.at[1 - c], send_sems, recv_sems, 0, (x, y, 1 - c)).wait_recv()
        cp.wait_send()
        mine.wait()

    return pl.pallas_call(
        body, in_specs=[_ANY], out_specs=_ANY, out_shape=jax.ShapeDtypeStruct((2, half, cc), red.dtype),
        scratch_shapes=[pltpu.SemaphoreType.DMA((1,)), pltpu.SemaphoreType.DMA((1,)), pltpu.SemaphoreType.DMA],
        name=name,
    )(red)


def _pack(arrs, rows_multiple, dtype):
    flat = jnp.concatenate([a.reshape(-1).astype(dtype) for a in arrs])
    unit = rows_multiple * PACK_COLS
    total = -(-flat.shape[0] // unit) * unit
    return jnp.pad(flat, (0, total - flat.shape[0])).reshape(-1, PACK_COLS)


def _unpack(buf, shapes):
    lead = buf.shape[:-2]
    flat = buf.reshape(*lead, -1)
    out, off = [], 0
    for shp in shapes:
        n = 1
        for s in shp:
            n *= s
        out.append(flat[..., off:off + n].reshape(*lead, *shp))
        off += n
    return out


def _join_shards(parts, axis):
    moved = jnp.moveaxis(parts, 0, axis)
    shp = list(moved.shape)
    shp[axis:axis + 2] = [shp[axis] * shp[axis + 1]]
    return moved.reshape(shp)


def _split_shards(full, axis):
    shp = list(full.shape)
    shp[axis:axis + 1] = [N_CHIP, shp[axis] // N_CHIP]
    return jnp.moveaxis(full.reshape(shp), axis, 0)


def _my_shard(full, axis, chip):
    n = full.shape[axis] // N_CHIP
    return lax.dynamic_slice_in_dim(full, chip * n, n, axis)


BIG = {"ffn_w_up": 2, "ffn_w_down": 1, "rg_w_in": 2, "rg_wa": 2, "rg_wx": 2, "rg_w_out": 1, "gla_w_in": 2, "gla_w_out": 1}
SMALL = {"norm_g": 2, "ffn_conv_w": 2, "rg_conv_w": 2, "gla_w_alpha": 2, "gla_b_alpha": 1, "gla_norm_g": 1,
         "ada_b": None, "ffn_conv_b": None, "rg_conv_b": None, "rg_ba": None, "rg_bx": None, "rg_lambda": None}
WEIGHTS = ["ada_w", "ada_b", "norm_g", "ffn_w_up", "ffn_conv_w", "ffn_conv_b", "ffn_w_down", "rg_w_in", "rg_conv_w", "rg_conv_b",
           "rg_wa", "rg_ba", "rg_wx", "rg_bx", "rg_lambda", "rg_w_out", "gla_w_in", "gla_w_alpha", "gla_b_alpha", "gla_norm_g",
           "gla_w_out"]
BIG_ROWS_MULTIPLE = 1280


def kernel(x, c, ada_w, ada_b, norm_g, ffn_w_up, ffn_conv_w, ffn_conv_b, ffn_w_down, rg_w_in, rg_conv_w, rg_conv_b, rg_wa, rg_ba, rg_wx, rg_bx, rg_lambda, rg_w_out, gla_w_in, gla_w_alpha, gla_b_alpha, gla_norm_g, gla_w_out, loss_target, m_ada_w, m_ada_b, m_norm_g, m_ffn_w_up, m_ffn_conv_w, m_ffn_conv_b, m_ffn_w_down, m_rg_w_in, m_rg_conv_w, m_rg_conv_b, m_rg_wa, m_rg_ba, m_rg_wx, m_rg_bx, m_rg_lambda, m_rg_w_out, m_gla_w_in, m_gla_w_alpha, m_gla_b_alpha, m_gla_norm_g, m_gla_w_out, v_ada_w, v_ada_b, v_norm_g, v_ffn_w_up, v_ffn_conv_w, v_ffn_conv_b, v_ffn_w_down, v_rg_w_in, v_rg_conv_w, v_rg_conv_b, v_rg_wa, v_rg_ba, v_rg_wx, v_rg_bx, v_rg_lambda, v_rg_w_out, v_gla_w_in, v_gla_w_alpha, v_gla_b_alpha, v_gla_norm_g, v_gla_w_out):
    wts = dict(ada_w=ada_w, ada_b=ada_b, norm_g=norm_g, ffn_w_up=ffn_w_up, ffn_conv_w=ffn_conv_w, ffn_conv_b=ffn_conv_b,
               ffn_w_down=ffn_w_down, rg_w_in=rg_w_in, rg_conv_w=rg_conv_w, rg_conv_b=rg_conv_b, rg_wa=rg_wa, rg_ba=rg_ba,
               rg_wx=rg_wx, rg_bx=rg_bx, rg_lambda=rg_lambda, rg_w_out=rg_w_out, gla_w_in=gla_w_in, gla_w_alpha=gla_w_alpha,
               gla_b_alpha=gla_b_alpha, gla_norm_g=gla_norm_g, gla_w_out=gla_w_out)
    mom1 = dict(ada_w=m_ada_w, ada_b=m_ada_b, norm_g=m_norm_g, ffn_w_up=m_ffn_w_up, ffn_conv_w=m_ffn_conv_w,
                ffn_conv_b=m_ffn_conv_b, ffn_w_down=m_ffn_w_down, rg_w_in=m_rg_w_in, rg_conv_w=m_rg_conv_w,
                rg_conv_b=m_rg_conv_b, rg_wa=m_rg_wa, rg_ba=m_rg_ba, rg_wx=m_rg_wx, rg_bx=m_rg_bx, rg_lambda=m_rg_lambda,
                rg_w_out=m_rg_w_out, gla_w_in=m_gla_w_in, gla_w_alpha=m_gla_w_alpha, gla_b_alpha=m_gla_b_alpha,
                gla_norm_g=m_gla_norm_g, gla_w_out=m_gla_w_out)
    mom2 = dict(ada_w=v_ada_w, ada_b=v_ada_b, norm_g=v_norm_g, ffn_w_up=v_ffn_w_up, ffn_conv_w=v_ffn_conv_w,
                ffn_conv_b=v_ffn_conv_b, ffn_w_down=v_ffn_w_down, rg_w_in=v_rg_w_in, rg_conv_w=v_rg_conv_w,
                rg_conv_b=v_rg_conv_b, rg_wa=v_rg_wa, rg_ba=v_rg_ba, rg_wx=v_rg_wx, rg_bx=v_rg_bx, rg_lambda=v_rg_lambda,
                rg_w_out=v_rg_w_out, gla_w_in=v_gla_w_in, gla_w_alpha=v_gla_w_alpha, gla_b_alpha=v_gla_b_alpha,
                gla_norm_g=v_gla_norm_g, gla_w_out=v_gla_w_out)
    xi, yi, ci = _place()
    chip, me = 2 * xi + yi, 4 * xi + 2 * yi + ci
    d = x.shape[-1]
    depth = ada_w.shape[0]
    n_ada = ada_w.shape[-1]
    sharded_small = [k for k, ax in SMALL.items() if ax is not None]

    sm = _all_gather_8(_pack([c] + [wts[k] for k in sharded_small], SUBLANES, F32), "gather_small")
    c_all = sm[:, 0, :]
    parts = _unpack(sm[0::2], [c.shape] + [wts[k].shape for k in sharded_small])[1:]
    full = {k: _join_shards(p, SMALL[k]) for k, p in zip(sharded_small, parts)}
    for k, ax in SMALL.items():
        if ax is None:
            full[k] = wts[k]

    c16 = jnp.pad(c_all, ((0, ADA_ROWS - N_DEV), (0, 0)))
    ada_b_mine = lax.dynamic_slice_in_dim(ada_b, chip * n_ada, n_ada, 1)[:, None, :]
    mod_cols = _ada_fwd(c16, ada_w, ada_b_mine, "ada_fwd")
    mod_all = _all_gather_8(mod_cols.reshape(-1, PACK_COLS), "gather_mod")[0::2].reshape(N_CHIP, depth, ADA_ROWS, n_ada)
    mod = jnp.swapaxes(lax.dynamic_index_in_dim(mod_all, me, 2, keepdims=False), 0, 1).reshape(depth, 6, d)

    big_shapes = [wts[k].shape for k in BIG]
    wall = _gather_chips(_pack([wts[k] for k in BIG], BIG_ROWS_MULTIPLE, BF16), "gather_weights")
    for k, p in zip(BIG, _unpack(wall, big_shapes)):
        full[k] = _join_shards(p, BIG[k])
    local = {k: (v if k in ("norm_g", "ffn_w_up", "ffn_conv_w", "ffn_conv_b", "ffn_w_down") else v[0]) for k, v in full.items()}

    cols, grad_x, gr = _local_step(x[0], loss_target[0], mod, _kernel_weights(local))
    loss = lax.psum(0.5 * jnp.sum(cols) / d, ("x", "y", "c"))

    small_names = [k for k in SMALL if k != "ada_b"]
    gs = _all_gather_8(_pack([gr[k] for k in small_names] + [gr["mod"]], SUBLANES, F32), "gather_small_grads")
    small_shapes = [full[k].shape for k in small_names] + [(depth, 6 * d)]
    *small_sum, g_ada_b = _unpack(_sum_lead(gs, "sum_small_grads"), small_shapes)
    grads = dict(zip(small_names, small_sum))
    grads["ada_b"] = g_ada_b
    for k in sharded_small:
        grads[k] = _my_shard(grads[k], SMALL[k], chip)
    dmod_all = _unpack(gs, small_shapes)[-1].reshape(N_DEV, depth, N_CHIP, n_ada)
    dmod_mine = jnp.swapaxes(lax.dynamic_index_in_dim(dmod_all, chip, 2, keepdims=False), 0, 1)
    grads["ada_w"] = _ada_bwd(c16, jnp.pad(dmod_mine, ((0, 0), (0, ADA_ROWS - N_DEV), (0, 0))), "ada_bwd")

    split = [_split_shards(gr[k].reshape(full[k].shape), BIG[k]).reshape(N_CHIP, -1) for k in BIG]
    flat = jnp.concatenate(split, axis=1)
    unit = BIG_ROWS_MULTIPLE * PACK_COLS
    total = -(-flat.shape[1] // unit) * unit
    gbuf = jnp.pad(flat, ((0, 0), (0, total - flat.shape[1]))).reshape(N_CHIP, 2, -1, PACK_COLS)
    theirs = _pair_exchange(gbuf, "grads_pair_exchange")
    psum = _pair_sum(gbuf, theirs, ci.reshape(1).astype(jnp.int32), "grads_pair_sum")
    arrived = _chip_exchange(psum, "grads_chip_exchange")
    red = _pair_share(_sum_lead(arrived, "grads_chip_sum"), "grads_pair_share")
    for k, g in zip(BIG, _unpack(red.reshape(-1, PACK_COLS), big_shapes)):
        grads[k] = g

    delta, new_m, new_v = {}, {}, {}
    for k in ["ada_w"] + list(BIG):
        shp = wts[k].shape
        two_d = lambda a: a.reshape(-1, shp[-1])
        outs = _adamw(two_d(wts[k]), two_d(grads[k]), two_d(mom1[k]), two_d(mom2[k]), "adamw_" + k)
        delta[k], new_m[k], new_v[k] = (o.reshape(shp) for o in outs)
    small_shard_shapes = [wts[k].shape for k in SMALL]
    packed = [_pack([src[k] for k in SMALL], SUBLANES, F32) for src in (wts, grads, mom1, mom2)]
    outs = _adamw(*packed, "adamw_small")
    for dst, o in zip((delta, new_m, new_v), outs):
        for k, a in zip(SMALL, _unpack(o, small_shard_shapes)):
            dst[k] = a

    return (loss, grad_x[None], *[grads[k] for k in WEIGHTS], *[delta[k] for k in WEIGHTS], *[new_m[k] for k in WEIGHTS],
            *[new_v[k] for k in WEIGHTS])
```

```python
import functools

import jax
import jax.numpy as jnp
from jax import lax
from jax.experimental import pallas as pl
from jax.experimental.pallas import tpu as pltpu

F32 = jnp.float32
BF16 = jnp.bfloat16
MXU_DTYPE = BF16

EPS = 1e-6
RG_C = 8.0
RG_BLOCKS = 4
RG_CONV = 4
GLA_HEADS = 4
GLA_TAU = 16.0
GLA_CHUNK = 64
GLA_RANK = 16
FFN_CONV = 3
ADAM_LR = 0.001
ADAM_B1 = 0.9
ADAM_B2 = 0.999
ADAM_EPS = 1e-08
ADAM_WD = 0.01
ADAM_STEP = 10

LANES = 128
SUBLANES = 8
VMEM_LIMIT = 56 * 1024 * 1024
CB = 256
MESH = pl.DeviceIdType.MESH


def _params(*sem):
    return pltpu.CompilerParams(dimension_semantics=sem, vmem_limit_bytes=VMEM_LIMIT)


def _tile(dim, prefs):
    for p in prefs:
        if dim % p == 0:
            return p
    return dim


def _dot(a, b, dims):
    return lax.dot_general(a.astype(MXU_DTYPE), b.astype(MXU_DTYPE), (dims, ((), ())), preferred_element_type=F32)


def _dot_nn(a, b):
    return _dot(a, b, ((1,), (0,)))


def _dot_nt(a, b):
    return _dot(a, b, ((1,), (1,)))


def _dot_tn(a, b):
    return _dot(a, b, ((0,), (0,)))


def _mm(a, b, *, ta=False, tb=False, out_dtype=F32, name):
    k_dim, m_dim = (a.shape if ta else a.shape[::-1])
    n_dim = b.shape[0] if tb else b.shape[1]
    assert (b.shape[1] if tb else b.shape[0]) == k_dim
    tm = _tile(m_dim, (1024, 1408, 512, 256, 128))
    tn = _tile(n_dim, (1024, 1408, 896, 512, 256, 128))
    tk = _tile(k_dim, (1024, 1408, 896, 512, 256, 128))
    nk = k_dim // tk
    dims = ((0 if ta else 1,), (1 if tb else 0,))

    def body(a_ref, b_ref, o_ref, acc_ref):
        k = pl.program_id(2)

        @pl.when(k == 0)
        def _():
            acc_ref[...] = jnp.zeros_like(acc_ref)

        acc_ref[...] += _dot(a_ref[...], b_ref[...], dims)

        @pl.when(k == nk - 1)
        def _():
            o_ref[...] = acc_ref[...].astype(o_ref.dtype)

    a_spec = pl.BlockSpec((tk, tm), lambda i, j, k: (k, i)) if ta else pl.BlockSpec((tm, tk), lambda i, j, k: (i, k))
    b_spec = pl.BlockSpec((tn, tk), lambda i, j, k: (j, k)) if tb else pl.BlockSpec((tk, tn), lambda i, j, k: (k, j))
    return pl.pallas_call(
        body,
        grid=(m_dim // tm, n_dim // tn, nk),
        in_specs=[a_spec, b_spec],
        out_specs=pl.BlockSpec((tm, tn), lambda i, j, k: (i, j)),
        out_shape=jax.ShapeDtypeStruct((m_dim, n_dim), out_dtype),
        scratch_shapes=[pltpu.VMEM((tm, tn), F32)],
        compiler_params=_params("parallel", "parallel", "arbitrary"),
        name=name,
    )(a, b)


def _row_specs(s, d, ts):
    return pl.BlockSpec((ts, d), lambda i: (i, 0)), pl.BlockSpec((1, d), lambda i: (0, 0))


def _norm_mod_fwd(x, g, sc, sh, name):
    s, d = x.shape
    ts = _tile(s, (512,))

    def body(x_ref, g_ref, sc_ref, sh_ref, h_ref):
        xv = x_ref[...]
        r = lax.rsqrt(jnp.mean(xv * xv, axis=-1, keepdims=True) + EPS)
        h_ref[...] = (((xv * r) * g_ref[...]) * (1.0 + sc_ref[...]) + sh_ref[...]).astype(h_ref.dtype)

    row, vec = _row_specs(s, d, ts)
    return pl.pallas_call(
        body, grid=(s // ts,), in_specs=[row, vec, vec, vec], out_specs=row,
        out_shape=jax.ShapeDtypeStruct((s, d), MXU_DTYPE), compiler_params=_params("parallel"), name=name,
    )(x, g, sc, sh)


def _norm_mod_bwd(dh, x, g, sc, dres, name):
    s, d = x.shape
    ts = _tile(s, (512,))

    def body(dh_ref, x_ref, g_ref, sc_ref, dres_ref, dx_ref, dg_ref, dsc_ref, dsh_ref, acc_ref):
        i = pl.program_id(0)

        @pl.when(i == 0)
        def _():
            acc_ref[...] = jnp.zeros_like(acc_ref)

        xv, dhv = x_ref[...], dh_ref[...]
        r = lax.rsqrt(jnp.mean(xv * xv, axis=-1, keepdims=True) + EPS)
        n = xv * r
        acc_ref[0:1, :] += jnp.sum(dhv * n, axis=0, keepdims=True)
        acc_ref[1:2, :] += jnp.sum(dhv, axis=0, keepdims=True)
        dn = dhv * ((1.0 + sc_ref[...]) * g_ref[...])
        dx_ref[...] = dres_ref[...] + r * (dn - n * jnp.mean(dn * n, axis=-1, keepdims=True))
        dg_ref[...] = (1.0 + sc_ref[...]) * acc_ref[0:1, :]
        dsc_ref[...] = g_ref[...] * acc_ref[0:1, :]
        dsh_ref[...] = acc_ref[1:2, :]

    row, vec = _row_specs(s, d, ts)
    vshape = jax.ShapeDtypeStruct((1, d), F32)
    return pl.pallas_call(
        body, grid=(s // ts,), in_specs=[row, row, vec, vec, row], out_specs=[row, vec, vec, vec],
        out_shape=[jax.ShapeDtypeStruct((s, d), F32), vshape, vshape, vshape],
        scratch_shapes=[pltpu.VMEM((SUBLANES, d), F32)], compiler_params=_params("arbitrary"), name=name,
    )(dh, x, g, sc, dres)


def _post_fwd(x, y, g, gt, name):
    s, d = x.shape
    ts = _tile(s, (512,))

    def body(x_ref, y_ref, g_ref, gt_ref, o_ref):
        yv = y_ref[...]
        r = lax.rsqrt(jnp.mean(yv * yv, axis=-1, keepdims=True) + EPS)
        o_ref[...] = x_ref[...] + gt_ref[...] * ((yv * r) * g_ref[...])

    row, vec = _row_specs(s, d, ts)
    return pl.pallas_call(
        body, grid=(s // ts,), in_specs=[row, row, vec, vec], out_specs=row,
        out_shape=jax.ShapeDtypeStruct((s, d), F32), compiler_params=_params("parallel"), name=name,
    )(x, y, g, gt)


def _post_bwd(dxn, y, g, gt, name):
    s, d = y.shape
    ts = _tile(s, (512,))

    def body(dxn_ref, y_ref, g_ref, gt_ref, dy_ref, dg_ref, dgt_ref, acc_ref):
        i = pl.program_id(0)

        @pl.when(i == 0)
        def _():
            acc_ref[...] = jnp.zeros_like(acc_ref)

        yv, dv = y_ref[...], dxn_ref[...]
        r = lax.rsqrt(jnp.mean(yv * yv, axis=-1, keepdims=True) + EPS)
        n = yv * r
        acc_ref[0:1, :] += jnp.sum(dv * n, axis=0, keepdims=True)
        dn = dv * (gt_ref[...] * g_ref[...])
        dy_ref[...] = (r * (dn - n * jnp.mean(dn * n, axis=-1, keepdims=True))).astype(dy_ref.dtype)
        dg_ref[...] = gt_ref[...] * acc_ref[0:1, :]
        dgt_ref[...] = g_ref[...] * acc_ref[0:1, :]

    row, vec = _row_specs(s, d, ts)
    vshape = jax.ShapeDtypeStruct((1, d), F32)
    return pl.pallas_call(
        body, grid=(s // ts,), in_specs=[row, row, vec, vec], out_specs=[row, vec, vec],
        out_shape=[jax.ShapeDtypeStruct((s, d), MXU_DTYPE), vshape, vshape],
        scratch_shapes=[pltpu.VMEM((SUBLANES, d), F32)], compiler_params=_params("arbitrary"), name=name,
    )(dxn, y, g, gt)


def _loss_grad(x, tgt, name):
    s, d = x.shape
    ts = _tile(s, (512,))

    def body(x_ref, t_ref, col_ref, dx_ref):
        i = pl.program_id(0)

        @pl.when(i == 0)
        def _():
            col_ref[...] = jnp.zeros_like(col_ref)

        e = x_ref[...] - t_ref[...]
        col_ref[...] += jnp.sum(e * e, axis=0, keepdims=True)
        dx_ref[...] = e * (1.0 / d)

    row, vec = _row_specs(s, d, ts)
    return pl.pallas_call(
        body, grid=(s // ts,), in_specs=[row, row], out_specs=[vec, row],
        out_shape=[jax.ShapeDtypeStruct((1, d), F32), jax.ShapeDtypeStruct((s, d), F32)],
        compiler_params=_params("arbitrary"), name=name,
    )(x, tgt)


_GELU_C = 0.7978845608028654
_GELU_A = 0.044715


def _gelu(x):
    t = jnp.tanh(_GELU_C * (x + _GELU_A * x * x * x))
    return 0.5 * x * (1.0 + t), t


def _gelu_grad(x, t):
    return 0.5 * (1.0 + t) + 0.5 * x * (1.0 - t * t) * (_GELU_C * (1.0 + 3.0 * _GELU_A * x * x))


def _sigmoid(x):
    return 1.0 / (1.0 + jnp.exp(-x))


def _log1p_pos(y):
    u = 1.0 + y
    return jnp.where(u == 1.0, y, jnp.log(u) * (y / jnp.where(u == 1.0, 1.0, u - 1.0)))


def _softplus(x):
    return jnp.maximum(x, 0.0) + _log1p_pos(jnp.exp(-jnp.abs(x)))


def _one_minus_exp(z):
    u = jnp.exp(z)
    lg = jnp.log(jnp.where(u > 0.0, u, 1.0))
    safe = (u != 1.0) & (u > 0.0)
    return jnp.where(u == 1.0, -z, jnp.where(u > 0.0, (1.0 - u) * (z / jnp.where(safe, lg, 1.0)), 1.0))


def _halo_index(ts):
    return lambda j, t: (jnp.maximum(t * (ts // SUBLANES) - 1, 0), j)


SLAB = 16


def _fold8(x):
    out = x[0:SUBLANES]
    for r in range(SUBLANES, x.shape[0], SUBLANES):
        out = out + x[r:r + SUBLANES]
    return out


def _ffn_mid_fwd(p, cw, cb, name):
    s, f2 = p.shape
    ts = _tile(s, (512,))
    nb, nt = f2 // (2 * CB), s // ts

    def body(p_ref, halo_ref, cw_ref, cb_ref, a_ref):
        t = pl.program_id(1)
        w0, w1, w2, bias = cw_ref[0:1, :], cw_ref[1:2, :], cw_ref[2:3, :], cb_ref[...]

        def slab(blk, r0):
            u = bias + w0 * blk[6:6 + SLAB] + w1 * blk[7:7 + SLAB] + w2 * blk[8:8 + SLAB]
            a_ref[pl.ds(r0, SLAB), :] = (_gelu(u[:, :CB])[0] * u[:, CB:]).astype(a_ref.dtype)

        slab(jnp.concatenate([jnp.where(t > 0, halo_ref[...], 0.0), p_ref[0:SLAB, :]], axis=0), 0)

        def loop(i, carry):
            r0 = pl.multiple_of(i * SLAB, SLAB)
            slab(p_ref[pl.ds(pl.multiple_of(r0 - SUBLANES, SUBLANES), SLAB + SUBLANES), :], r0)
            return carry

        lax.fori_loop(1, ts // SLAB, loop, 0, unroll=2)

    return pl.pallas_call(
        body, grid=(nb, nt),
        in_specs=[pl.BlockSpec((ts, 2 * CB), lambda j, t: (t, j)),
                  pl.BlockSpec((SUBLANES, 2 * CB), _halo_index(ts)),
                  pl.BlockSpec((FFN_CONV, 2 * CB), lambda j, t: (0, j)),
                  pl.BlockSpec((1, 2 * CB), lambda j, t: (0, j))],
        out_specs=pl.BlockSpec((ts, CB), lambda j, t: (t, j)),
        out_shape=jax.ShapeDtypeStruct((s, f2 // 2), MXU_DTYPE),
        compiler_params=_params("parallel", "arbitrary"), name=name,
    )(p, p, cw, cb)


def _ffn_mid_bwd(da, p, cw, cb, name):
    s, f2 = p.shape
    ts = _tile(s, (512,))
    nb, nt = f2 // (2 * CB), s // ts
    n_slab = ts // SLAB

    def body(da_ref, p_ref, halo_ref, cw_ref, cb_ref, dp_ref, dcw_ref, dcb_ref, next_du, acc):
        tt = pl.program_id(1)
        t = nt - 1 - tt
        w0, w1, w2, bias = cw_ref[0:1, :], cw_ref[1:2, :], cw_ref[2:3, :], cb_ref[...]

        @pl.when(tt == 0)
        def _():
            next_du[...] = jnp.zeros_like(next_du)
            acc[...] = jnp.zeros_like(acc)

        def slab(blk, r0, carry):
            pm2, pm1, p0 = blk[6:6 + SLAB], blk[7:7 + SLAB], blk[8:8 + SLAB]
            u = bias + w0 * pm2 + w1 * pm1 + w2 * p0
            g, v = u[:, :CB], u[:, CB:]
            gel, th = _gelu(g)
            dav = da_ref[pl.ds(r0, SLAB), :]
            du = jnp.concatenate([dav * v * _gelu_grad(g, th), dav * gel], axis=1)
            ext = jnp.concatenate([du, carry], axis=0)
            dp_ref[pl.ds(r0, SLAB), :] = (w2 * du + w1 * ext[1:1 + SLAB] + w0 * ext[2:2 + SLAB]).astype(dp_ref.dtype)
            acc[0] += _fold8(du)
            acc[1] += _fold8(du * pm2)
            acc[2] += _fold8(du * pm1)
            acc[3] += _fold8(du * p0)
            return du[0:SUBLANES]

        def loop(k, carry):
            r0 = pl.multiple_of((n_slab - 1 - k) * SLAB, SLAB)
            return slab(p_ref[pl.ds(pl.multiple_of(r0 - SUBLANES, SUBLANES), SLAB + SUBLANES), :], r0, carry)

        carry = lax.fori_loop(0, n_slab - 1, loop, next_du[...], unroll=2)
        first = jnp.concatenate([jnp.where(t > 0, halo_ref[...], 0.0), p_ref[0:SLAB, :]], axis=0)
        next_du[...] = slab(first, 0, carry)

        @pl.when(tt == nt - 1)
        def _():
            dcb_ref[...] = jnp.sum(acc[0], axis=0, keepdims=True)
            for k in range(FFN_CONV):
                dcw_ref[k:k + 1, :] = jnp.sum(acc[1 + k], axis=0, keepdims=True)

    rev = lambda j, t: (nt - 1 - t, j)
    return pl.pallas_call(
        body, grid=(nb, nt),
        in_specs=[pl.BlockSpec((ts, CB), rev),
                  pl.BlockSpec((ts, 2 * CB), rev),
                  pl.BlockSpec((SUBLANES, 2 * CB), lambda j, t: (jnp.maximum((nt - 1 - t) * (ts // SUBLANES) - 1, 0), j)),
                  pl.BlockSpec((FFN_CONV, 2 * CB), lambda j, t: (0, j)),
                  pl.BlockSpec((1, 2 * CB), lambda j, t: (0, j))],
        out_specs=[pl.BlockSpec((ts, 2 * CB), rev),
                   pl.BlockSpec((FFN_CONV, 2 * CB), lambda j, t: (0, j)),
                   pl.BlockSpec((1, 2 * CB), lambda j, t: (0, j))],
        out_shape=[jax.ShapeDtypeStruct((s, f2), MXU_DTYPE), jax.ShapeDtypeStruct((FFN_CONV, f2), F32),
                   jax.ShapeDtypeStruct((1, f2), F32)],
        scratch_shapes=[pltpu.VMEM((SUBLANES, 2 * CB), F32), pltpu.VMEM((1 + FFN_CONV, SUBLANES, 2 * CB), F32)],
        compiler_params=_params("parallel", "arbitrary"), name=name,
    )(da, p, p, cw, cb)


def _rg_gates(xc, wa_ref, ba_ref, wx_ref, bx_ref, lam_ref):
    r = _sigmoid(_dot_nn(xc, wa_ref[0]) + ba_ref[...])
    ig = _sigmoid(_dot_nn(xc, wx_ref[0]) + bx_ref[...])
    sp = _softplus(-lam_ref[...])
    log_a = (-RG_C) * r * sp
    a = jnp.exp(log_a)
    mult = jnp.sqrt(_one_minus_exp(2.0 * log_a))
    return r, ig, sp, a, mult


def _rg_conv(scr, cw_ref, cb_ref, ts):
    views = [scr[5 + k:5 + k + ts, :] for k in range(RG_CONV)]
    xc = cb_ref[...]
    for k in range(RG_CONV):
        xc = xc + cw_ref[k:k + 1, :] * views[k]
    return xc, views


def _rg_param_specs():
    vec = pl.BlockSpec((1, CB), lambda g, t: (0, g))
    mat = pl.BlockSpec((1, CB, CB), lambda g, t: (g, 0, 0))
    return [pl.BlockSpec((RG_CONV, CB), lambda g, t: (0, g)), vec, mat, vec, mat, vec, vec]


NSEG = SUBLANES
NQ = CB // LANES


def _lanes(q):
    return slice(q * LANES, (q + 1) * LANES)


def _seg_scan(a_scr, x_scr, loc_scr, dec_scr, ts, reverse):
    seg = ts // NSEG

    def step(k, carry):
        out = []
        rows = pl.ds(seg - 1 - k if reverse else k, NSEG, stride=seg)
        for q in range(NQ):
            st, dec = carry[q]
            a_q, x_q, loc_q, dec_q = a_scr.at[q], x_scr.at[q], loc_scr.at[q], dec_scr.at[q]
            av = a_q[rows, :]
            if reverse:
                loc_q[rows, :] = st
                dec_q[rows, :] = dec
                st = av * (x_q[rows, :] + st)
                dec = av * dec
            else:
                st = av * st + x_q[rows, :]
                dec = av * dec
                loc_q[rows, :] = st
                dec_q[rows, :] = dec
            out.append((st, dec))
        return tuple(out)

    init = tuple((jnp.zeros((NSEG, LANES), F32), jnp.ones((NSEG, LANES), F32)) for _ in range(NQ))
    return lax.fori_loop(0, seg, step, init, unroll=4)


def _seg_chain(fin, dec, c_in, reverse):
    rows = [None] * NSEG
    c = c_in
    for sgm in (reversed(range(NSEG)) if reverse else range(NSEG)):
        rows[sgm] = c
        c = fin[sgm:sgm + 1] + dec[sgm:sgm + 1] * c
    return jnp.concatenate(rows, axis=0), c


def _rg_mid_fwd(pj, cw, cb, wa, ba, wx, bx, lam, name):
    s = pj.shape[0]
    nb = pj.shape[1] // (2 * CB)
    ts = _tile(s, (512,))
    nt = s // ts
    seg = ts // NSEG

    def body(pj_ref, halo_ref, cw_ref, cb_ref, wa_ref, ba_ref, wx_ref, bx_ref, lam_ref, y_ref, hs_ref,
             scr, a_scr, u_scr, loc_scr, dec_scr, h_scr):
        t = pl.program_id(1)

        @pl.when(t == 0)
        def _():
            h_scr[...] = jnp.zeros_like(h_scr)

        scr[0:SUBLANES, :] = jnp.where(t > 0, halo_ref[:, CB:], 0.0)
        scr[SUBLANES:, :] = pj_ref[:, CB:]
        xc, _ = _rg_conv(scr, cw_ref, cb_ref, ts)
        _, ig, _, a, mult = _rg_gates(xc, wa_ref, ba_ref, wx_ref, bx_ref, lam_ref)
        u = mult * (ig * xc)
        for q in range(NQ):
            a_scr[q] = a[:, _lanes(q)]
            u_scr[q] = u[:, _lanes(q)]
        fin = _seg_scan(a_scr, u_scr, loc_scr, dec_scr, ts, False)
        for q in range(NQ):
            enter, leave = _seg_chain(fin[q][0], fin[q][1], h_scr[0:1, _lanes(q)], False)
            h_scr[0:1, _lanes(q)] = leave
            for sgm in range(NSEG):
                rows = slice(sgm * seg, (sgm + 1) * seg)
                hs_ref[rows, _lanes(q)] = loc_scr[q, rows, :] + dec_scr[q, rows, :] * enter[sgm:sgm + 1]
        y_ref[...] = (_gelu(pj_ref[:, :CB])[0] * hs_ref[...]).astype(y_ref.dtype)

    blk = pl.BlockSpec((ts, CB), lambda g, t: (t, g))
    lane_scr = pltpu.VMEM((NQ, ts, LANES), F32)
    return pl.pallas_call(
        body, grid=(nb, nt),
        in_specs=[pl.BlockSpec((ts, 2 * CB), lambda g, t: (t, g)),
                  pl.BlockSpec((SUBLANES, 2 * CB), _halo_index(ts))] + _rg_param_specs(),
        out_specs=[blk, blk],
        out_shape=[jax.ShapeDtypeStruct((s, nb * CB), MXU_DTYPE), jax.ShapeDtypeStruct((s, nb * CB), F32)],
        scratch_shapes=[pltpu.VMEM((ts + SUBLANES, CB), F32), lane_scr, lane_scr, lane_scr, lane_scr,
                        pltpu.VMEM((SUBLANES, CB), F32)],
        compiler_params=_params("parallel", "arbitrary"), name=name,
    )(pj, pj, cw, cb, wa, ba, wx, bx, lam)


def _rg_mid_bwd(dy, pj, hs, cw, cb, wa, ba, wx, bx, lam, name):
    s = pj.shape[0]
    nb = pj.shape[1] // (2 * CB)
    ts = _tile(s, (512,))
    nt = s // ts

    def body(dy_ref, pj_ref, halo_ref, hs_ref, hsh_ref, cw_ref, cb_ref, wa_ref, ba_ref, wx_ref, bx_ref, lam_ref,
             dpj_ref, dcw_ref, dcb_ref, dwa_ref, dba_ref, dwx_ref, dbx_ref, dlam_ref,
             scr, hscr, a_scr, d_scr, loc_scr, dec_scr, g_scr, dxscr, c_scr):
        tt = pl.program_id(1)
        t = nt - 1 - tt
        seg = ts // NSEG

        @pl.when(tt == 0)
        def _():
            c_scr[...] = jnp.zeros_like(c_scr)
            dxscr[ts:, :] = jnp.zeros((SUBLANES, CB), F32)
            for ref in (dcw_ref, dcb_ref, dwa_ref, dba_ref, dwx_ref, dbx_ref, dlam_ref):
                ref[...] = jnp.zeros_like(ref)

        scr[0:SUBLANES, :] = jnp.where(t > 0, halo_ref[:, CB:], 0.0)
        scr[SUBLANES:, :] = pj_ref[:, CB:]
        hscr[0:SUBLANES, :] = jnp.where(t > 0, hsh_ref[...], 0.0)
        hscr[SUBLANES:, :] = hs_ref[...]
        xc, views = _rg_conv(scr, cw_ref, cb_ref, ts)
        r, ig, sp, a, mult = _rg_gates(xc, wa_ref, ba_ref, wx_ref, bx_ref, lam_ref)
        gate = pj_ref[:, :CB]
        gel, th = _gelu(gate)
        dyv = dy_ref[...]
        dpj_ref[:, :CB] = (dyv * hs_ref[...] * _gelu_grad(gate, th)).astype(dpj_ref.dtype)
        dhs = dyv * gel
        for q in range(NQ):
            a_scr[q] = a[:, _lanes(q)]
            d_scr[q] = dhs[:, _lanes(q)]
        fin = _seg_scan(a_scr, d_scr, loc_scr, dec_scr, ts, True)
        for q in range(NQ):
            enter, leave = _seg_chain(fin[q][0], fin[q][1], c_scr[0:1, _lanes(q)], True)
            c_scr[0:1, _lanes(q)] = leave
            for sgm in range(NSEG):
                rows = slice(sgm * seg, (sgm + 1) * seg)
                g_scr[rows, _lanes(q)] = d_scr[q, rows, :] + loc_scr[q, rows, :] + dec_scr[q, rows, :] * enter[sgm:sgm + 1]
        du = g_scr[...]
        da = du * hscr[7:7 + ts, :]
        dmult = du * (ig * xc)
        dig = du * (mult * xc)
        dxc = du * (mult * ig)
        dlog_a = da * a - dmult * (a * a / mult)
        dlam_ref[...] += jnp.sum(dlog_a * r, axis=0, keepdims=True) * (RG_C * _sigmoid(-lam_ref[...]))
        dpr = dlog_a * ((-RG_C) * sp) * (r * (1.0 - r))
        dpi = dig * (ig * (1.0 - ig))
        dba_ref[...] += jnp.sum(dpr, axis=0, keepdims=True)
        dbx_ref[...] += jnp.sum(dpi, axis=0, keepdims=True)
        dwa_ref[0] += _dot_tn(xc, dpr)
        dwx_ref[0] += _dot_tn(xc, dpi)
        dxc = dxc + _dot_nt(dpr, wa_ref[0]) + _dot_nt(dpi, wx_ref[0])
        dcb_ref[...] += jnp.sum(dxc, axis=0, keepdims=True)
        for k in range(RG_CONV):
            dcw_ref[k:k + 1, :] += jnp.sum(dxc * views[k], axis=0, keepdims=True)
        dxscr[0:ts, :] = dxc
        dxp = cw_ref[3:4, :] * dxc
        for k in range(RG_CONV - 1):
            dxp = dxp + cw_ref[k:k + 1, :] * dxscr[3 - k:3 - k + ts, :]
        dpj_ref[:, CB:] = dxp.astype(dpj_ref.dtype)
        dxscr[ts:, :] = dxscr[0:SUBLANES, :]

    rev = lambda g, t: (nt - 1 - t, g)
    rev_halo = lambda g, t: (jnp.maximum((nt - 1 - t) * (ts // SUBLANES) - 1, 0), g)
    vec = pl.BlockSpec((1, CB), lambda g, t: (0, g))
    mat = pl.BlockSpec((1, CB, CB), lambda g, t: (g, 0, 0))
    d = nb * CB
    vshape = jax.ShapeDtypeStruct((1, d), F32)
    mshape = jax.ShapeDtypeStruct((nb, CB, CB), F32)
    return pl.pallas_call(
        body, grid=(nb, nt),
        in_specs=[pl.BlockSpec((ts, CB), rev), pl.BlockSpec((ts, 2 * CB), rev), pl.BlockSpec((SUBLANES, 2 * CB), rev_halo),
                  pl.BlockSpec((ts, CB), rev), pl.BlockSpec((SUBLANES, CB), rev_halo)] + _rg_param_specs(),
        out_specs=[pl.BlockSpec((ts, 2 * CB), rev), pl.BlockSpec((RG_CONV, CB), lambda g, t: (0, g)), vec, mat, vec, mat, vec, vec],
        out_shape=[jax.ShapeDtypeStruct((s, 2 * d), MXU_DTYPE), jax.ShapeDtypeStruct((RG_CONV, d), F32), vshape, mshape, vshape,
                   mshape, vshape, vshape],
        scratch_shapes=[pltpu.VMEM((ts + SUBLANES, CB), F32), pltpu.VMEM((ts + SUBLANES, CB), F32)]
        + [pltpu.VMEM((NQ, ts, LANES), F32)] * 4
        + [pltpu.VMEM((ts, CB), F32), pltpu.VMEM((ts + SUBLANES, CB), F32), pltpu.VMEM((SUBLANES, CB), F32)],
        compiler_params=_params("parallel", "arbitrary"), name=name,
    )(dy, pj, pj, hs, hs, cw, cb, wa, ba, wx, bx, lam)


GLA_DK = 128
GLA_DV = 256
GLA_HB = 2 * GLA_DK + 2 * GLA_DV + LANES
GLA_TS = 256


def _split3(x):
    hi = x.astype(BF16)
    r1 = x - hi.astype(F32)
    mid = r1.astype(BF16)
    lo = (r1 - mid.astype(F32)).astype(BF16)
    return hi, mid, lo


def _chunk_cumsum(x, reverse):
    n = x.shape[0]
    i = lax.broadcasted_iota(jnp.int32, (n, n), 0)
    j = lax.broadcasted_iota(jnp.int32, (n, n), 1)
    same = (i // GLA_CHUNK) == (j // GLA_CHUNK)
    tri = jnp.where(same & ((j >= i) if reverse else (j <= i)), 1.0, 0.0).astype(BF16)
    out = jnp.zeros(x.shape, F32)
    for piece in _split3(x):
        out = out + lax.dot_general(tri, piece, (((1,), (0,)), ((), ())), preferred_element_type=F32)
    return out


def _gla_split(blk):
    q = blk[:, 0:GLA_DK] * (GLA_DK ** -0.5)
    k = blk[:, GLA_DK:2 * GLA_DK]
    v = blk[:, 2 * GLA_DK:2 * GLA_DK + GLA_DV]
    r = blk[:, 2 * GLA_DK + GLA_DV:2 * GLA_DK + 2 * GLA_DV]
    z = blk[:, 2 * GLA_DK + 2 * GLA_DV:]
    return q, k, v, r, z


def _gla_decays(gc):
    gref = gc[GLA_CHUNK // 2:GLA_CHUNK // 2 + 1, :]
    glast = gc[GLA_CHUNK - 1:GLA_CHUNK, :]
    return jnp.exp(gc), jnp.exp(gc - gref), jnp.exp(gref - gc), jnp.exp(glast - gc), jnp.exp(glast)


def _causal_mask():
    i = lax.broadcasted_iota(jnp.int32, (GLA_CHUNK, GLA_CHUNK), 0)
    j = lax.broadcasted_iota(jnp.int32, (GLA_CHUNK, GLA_CHUNK), 1)
    return j <= i


def _log_sigmoid(x):
    return jnp.minimum(x, 0.0) - _log1p_pos(jnp.exp(-jnp.abs(x)))


def _gla_mid_fwd(pj, wal, bal, ng, name):
    s = pj.shape[0]
    nh = pj.shape[1] // GLA_HB
    ts = _tile(s, (GLA_TS,))
    nt, nc = s // ts, ts // GLA_CHUNK

    def body(pj_ref, wal_ref, bal_ref, ng_ref, act_ref, o_ref, st_ref, s_scr):
        t = pl.program_id(1)

        @pl.when(t == 0)
        def _():
            s_scr[...] = jnp.zeros_like(s_scr)

        q, k, v, r, z = _gla_split(pj_ref[...])
        g = _log_sigmoid(_dot_nn(z, wal_ref[0]) + bal_ref[0]) * (1.0 / GLA_TAU)
        gcum = _chunk_cumsum(g, False)
        mask = _causal_mask()
        for c in range(nc):
            sl = slice(c * GLA_CHUNK, (c + 1) * GLA_CHUNK)
            eg, eq, ek, ekd, egl = _gla_decays(gcum[sl])
            st = s_scr[...]
            st_ref[c, 0] = st
            attn = jnp.where(mask, _dot_nt(q[sl] * eq, k[sl] * ek), 0.0)
            o_ref[sl, :] = _dot_nt(q[sl] * eg, st) + _dot_nn(attn, v[sl])
            s_scr[...] = st * egl + _dot_tn(v[sl], k[sl] * ekd)
        o = o_ref[...]
        on = o * lax.rsqrt(jnp.mean(o * o, axis=-1, keepdims=True) + EPS)
        act_ref[...] = ((on * ng_ref[...]) * (r * _sigmoid(r))).astype(act_ref.dtype)

    blk = pl.BlockSpec((ts, GLA_DV), lambda h, t: (t, h))
    return pl.pallas_call(
        body, grid=(nh, nt),
        in_specs=[pl.BlockSpec((ts, GLA_HB), lambda h, t: (t, h)),
                  pl.BlockSpec((1, LANES, GLA_DK), lambda h, t: (h, 0, 0)),
                  pl.BlockSpec((1, 1, GLA_DK), lambda h, t: (h, 0, 0)),
                  pl.BlockSpec((1, GLA_DV), lambda h, t: (0, 0))],
        out_specs=[blk, blk, pl.BlockSpec((nc, 1, GLA_DV, GLA_DK), lambda h, t: (t, h, 0, 0))],
        out_shape=[jax.ShapeDtypeStruct((s, nh * GLA_DV), MXU_DTYPE), jax.ShapeDtypeStruct((s, nh * GLA_DV), F32),
                   jax.ShapeDtypeStruct((s // GLA_CHUNK, nh, GLA_DV, GLA_DK), F32)],
        scratch_shapes=[pltpu.VMEM((GLA_DV, GLA_DK), F32)],
        compiler_params=_params("parallel", "arbitrary"), name=name,
    )(pj, wal, bal, ng)


def _gla_mid_bwd(dact, pj, o, st, wal, bal, ng, name):
    s = pj.shape[0]
    nh = pj.shape[1] // GLA_HB
    ts = _tile(s, (GLA_TS,))
    nt, nc = s // ts, ts // GLA_CHUNK

    def body(dact_ref, pj_ref, o_ref, st_ref, wal_ref, bal_ref, ng_ref, dpj_ref, dwal_ref, dbal_ref, dng_ref,
             ds_scr, dg_scr):
        h, tt = pl.program_id(0), pl.program_id(1)

        @pl.when(tt == 0)
        def _():
            ds_scr[...] = jnp.zeros_like(ds_scr)
            dwal_ref[...] = jnp.zeros_like(dwal_ref)
            dbal_ref[...] = jnp.zeros_like(dbal_ref)

        @pl.when((tt == 0) & (h == 0))
        def _():
            dng_ref[...] = jnp.zeros_like(dng_ref)

        q, k, v, r, z = _gla_split(pj_ref[...])
        logit = _dot_nn(z, wal_ref[0]) + bal_ref[0]
        gcum = _chunk_cumsum(_log_sigmoid(logit) * (1.0 / GLA_TAU), False)
        ov = o_ref[...]
        ro = lax.rsqrt(jnp.mean(ov * ov, axis=-1, keepdims=True) + EPS)
        on = ov * ro
        sg = _sigmoid(r)
        sil = r * sg
        dav = dact_ref[...]
        dpj_ref[:, 2 * GLA_DK + GLA_DV:2 * GLA_DK + 2 * GLA_DV] = (
            dav * (on * ng_ref[...]) * (sg + sil * (1.0 - sg))).astype(dpj_ref.dtype)
        t1 = dav * sil
        dng_ref[...] += jnp.sum(t1 * on, axis=0, keepdims=True)
        dn = t1 * ng_ref[...]
        do = ro * (dn - on * jnp.mean(dn * on, axis=-1, keepdims=True))
        mask = _causal_mask()
        scale = GLA_DK ** -0.5
        for c in reversed(range(nc)):
            sl = slice(c * GLA_CHUNK, (c + 1) * GLA_CHUNK)
            eg, eq, ek, ekd, egl = _gla_decays(gcum[sl])
            qc, kc, vc, doc = q[sl], k[sl], v[sl], do[sl]
            qg, qt, kt, kd = qc * eg, qc * eq, kc * ek, kc * ekd
            sp = st_ref[c, 0]
            ds = ds_scr[...]
            attn = jnp.where(mask, _dot_nt(qt, kt), 0.0)
            dattn = jnp.where(mask, _dot_nt(doc, vc), 0.0)
            dqg = _dot_nn(doc, sp)
            dqt = _dot_nn(dattn, kt)
            dkt = _dot_tn(dattn, qt)
            dkd = _dot_nn(vc, ds)
            dpj_ref[sl, 2 * GLA_DK:2 * GLA_DK + GLA_DV] = (_dot_tn(attn, doc) + _dot_nt(kd, ds)).astype(dpj_ref.dtype)
            dpj_ref[sl, 0:GLA_DK] = (scale * (dqg * eg + dqt * eq)).astype(dpj_ref.dtype)
            dpj_ref[sl, GLA_DK:2 * GLA_DK] = (dkt * ek + dkd * ekd).astype(dpj_ref.dtype)
            kdd = dkd * kd
            dgl = jnp.sum(kdd, axis=0, keepdims=True) + jnp.sum(ds * sp, axis=0, keepdims=True) * egl
            row = lax.broadcasted_iota(jnp.int32, (GLA_CHUNK, GLA_DK), 0)
            dg_scr[sl, :] = dqg * qg + dqt * qt - dkt * kt - kdd + jnp.where(row == GLA_CHUNK - 1, dgl, 0.0)
            ds_scr[...] = ds * egl + _dot_tn(doc, qg)
        dlogit = _chunk_cumsum(dg_scr[...], True) * (1.0 / GLA_TAU) * _sigmoid(-logit)
        dpj_ref[:, 2 * GLA_DK + 2 * GLA_DV:] = _dot_nt(dlogit, wal_ref[0]).astype(dpj_ref.dtype)
        dwal_ref[0] += _dot_tn(z, dlogit)
        dbal_ref[0] += jnp.sum(dlogit, axis=0, keepdims=True)

    rev = lambda h, t: (nt - 1 - t, h)
    return pl.pallas_call(
        body, grid=(nh, nt),
        in_specs=[pl.BlockSpec((ts, GLA_DV), rev), pl.BlockSpec((ts, GLA_HB), rev), pl.BlockSpec((ts, GLA_DV), rev),
                  pl.BlockSpec((nc, 1, GLA_DV, GLA_DK), lambda h, t: (nt - 1 - t, h, 0, 0)),
                  pl.BlockSpec((1, LANES, GLA_DK), lambda h, t: (h, 0, 0)),
                  pl.BlockSpec((1, 1, GLA_DK), lambda h, t: (h, 0, 0)),
                  pl.BlockSpec((1, GLA_DV), lambda h, t: (0, 0))],
        out_specs=[pl.BlockSpec((ts, GLA_HB), rev),
                   pl.BlockSpec((1, LANES, GLA_DK), lambda h, t: (h, 0, 0)),
                   pl.BlockSpec((1, 1, GLA_DK), lambda h, t: (h, 0, 0)),
                   pl.BlockSpec((1, GLA_DV), lambda h, t: (0, 0))],
        out_shape=[jax.ShapeDtypeStruct((s, nh * GLA_HB), MXU_DTYPE), jax.ShapeDtypeStruct((nh, LANES, GLA_DK), F32),
                   jax.ShapeDtypeStruct((nh, 1, GLA_DK), F32), jax.ShapeDtypeStruct((1, GLA_DV), F32)],
        scratch_shapes=[pltpu.VMEM((GLA_DV, GLA_DK), F32), pltpu.VMEM((ts, GLA_DK), F32)],
        compiler_params=_params("arbitrary", "arbitrary"), name=name,
    )(dact, pj, o, st, wal, bal, ng)


def _adamw(w, g, m, v, name):
    rows, cols = w.shape
    tr = _tile(rows, (256, 128, 64, 32, 16, 8))
    c1 = 1.0 / (1.0 - ADAM_B1 ** ADAM_STEP)
    c2 = 1.0 / (1.0 - ADAM_B2 ** ADAM_STEP)

    def body(w_ref, g_ref, m_ref, v_ref, d_ref, mo_ref, vo_ref):
        gv = g_ref[...]
        m2 = ADAM_B1 * m_ref[...] + (1.0 - ADAM_B1) * gv
        v2 = ADAM_B2 * v_ref[...] + (1.0 - ADAM_B2) * (gv * gv)
        d_ref[...] = (-ADAM_LR) * ((m2 * c1) / (jnp.sqrt(v2 * c2) + ADAM_EPS) + ADAM_WD * w_ref[...])
        mo_ref[...] = m2
        vo_ref[...] = v2

    spec = pl.BlockSpec((tr, cols), lambda i: (i, 0))
    shape = jax.ShapeDtypeStruct((rows, cols), F32)
    return pl.pallas_call(
        body, grid=(rows // tr,), in_specs=[spec] * 4, out_specs=[spec] * 3, out_shape=[shape] * 3,
        compiler_params=_params("parallel"), name=name,
    )(w, g, m, v)


def _pair_cols(w):
    lead, n = w.shape[:-1], w.shape[-1]
    return jnp.swapaxes(w.reshape(*lead, 2, n // (2 * CB), CB), -3, -2).reshape(*lead, n)


def _unpair_cols(w):
    lead, n = w.shape[:-1], w.shape[-1]
    return jnp.swapaxes(w.reshape(*lead, n // (2 * CB), 2, CB), -3, -2).reshape(*lead, n)


def _gla_head_cols(w):
    d = w.shape[0]
    qk, dv = GLA_HEADS * GLA_DK, GLA_HEADS * GLA_DV
    q, k, v, r, z = jnp.split(w, [qk, 2 * qk, 2 * qk + dv, 2 * qk + 2 * dv], axis=1)
    zp = jnp.pad(z, ((0, 0), (0, LANES - GLA_RANK)))
    parts = [q.reshape(d, GLA_HEADS, GLA_DK), k.reshape(d, GLA_HEADS, GLA_DK), v.reshape(d, GLA_HEADS, GLA_DV),
             r.reshape(d, GLA_HEADS, GLA_DV), jnp.broadcast_to(zp[:, None, :], (d, GLA_HEADS, LANES))]
    return jnp.concatenate(parts, axis=2).reshape(d, GLA_HEADS * GLA_HB)


def _gla_unhead_cols(w):
    d = w.shape[0]
    w = w.reshape(d, GLA_HEADS, GLA_HB)
    o = 2 * GLA_DK + 2 * GLA_DV
    parts = [w[:, :, 0:GLA_DK].reshape(d, -1), w[:, :, GLA_DK:2 * GLA_DK].reshape(d, -1),
             w[:, :, 2 * GLA_DK:2 * GLA_DK + GLA_DV].reshape(d, -1), w[:, :, 2 * GLA_DK + GLA_DV:o].reshape(d, -1),
             jnp.sum(w[:, :, o:o + GLA_RANK], axis=1)]
    return jnp.concatenate(parts, axis=1)


def _gla_alpha_heads(w_alpha, b_alpha):
    wal = jnp.swapaxes(w_alpha.reshape(GLA_RANK, GLA_HEADS, GLA_DK), 0, 1)
    return jnp.pad(wal, ((0, 0), (0, LANES - GLA_RANK), (0, 0))), b_alpha.reshape(GLA_HEADS, 1, GLA_DK)


def _local_step(x, tgt, mod, w):
    depth = mod.shape[0]
    row = lambda v: v.reshape(1, -1)
    saved = []
    for i in range(depth):
        sh_m, sc_m, gt_m, sh_f, sc_f, gt_f = (mod[i, j:j + 1] for j in range(6))
        g0, g1, g2, g3 = (w["norm_g"][i, j:j + 1] for j in range(4))
        tag = f"_l{i}"
        h = _norm_mod_fwd(x, g0, sc_m, sh_m, "norm_mix" + tag)
        if i % 2 == 0:
            pj = _mm(h, w["rg_w_in"], name="rg_in" + tag)
            act, aux = _rg_mid_fwd(pj, w["rg_conv_w"], row(w["rg_conv_b"]), w["rg_wa"], row(w["rg_ba"]), w["rg_wx"],
                                   row(w["rg_bx"]), row(w["rg_lambda"]), "rg_mid" + tag)
            y = _mm(act, w["rg_w_out"], name="rg_out" + tag)
        else:
            pj = _mm(h, w["gla_w_in"], name="gla_in" + tag)
            act, *aux = _gla_mid_fwd(pj, w["gla_wal"], w["gla_bal"], row(w["gla_norm_g"]), "gla_mid" + tag)
            y = _mm(act, w["gla_w_out"], name="gla_out" + tag)
        x1 = _post_fwd(x, y, g1, gt_m, "post_mix" + tag)
        h2 = _norm_mod_fwd(x1, g2, sc_f, sh_f, "norm_ffn" + tag)
        p = _mm(h2, w["ffn_w_up"][i], name="ffn_up" + tag)
        a = _ffn_mid_fwd(p, w["ffn_conv_w"][i], w["ffn_conv_b"][i:i + 1], "ffn_mid" + tag)
        y2 = _mm(a, w["ffn_w_down"][i], name="ffn_down" + tag)
        x2 = _post_fwd(x1, y2, g3, gt_f, "post_ffn" + tag)
        saved.append((x, h, pj, act, aux, y, x1, h2, p, a, y2))
        x = x2

    cols, dx = _loss_grad(x, tgt, "loss")

    gr = {k: [None] * depth for k in ("norm_g", "ffn_w_up", "ffn_conv_w", "ffn_conv_b", "ffn_w_down", "mod")}
    for i in reversed(range(depth)):
        x0, h, pj, act, aux, y, x1, h2, p, a, y2 = saved[i]
        sh_m, sc_m, gt_m, sh_f, sc_f, gt_f = (mod[i, j:j + 1] for j in range(6))
        g0, g1, g2, g3 = (w["norm_g"][i, j:j + 1] for j in range(4))
        tag = f"_l{i}"
        dy2, d_g3, d_gt_f = _post_bwd(dx, y2, g3, gt_f, "post_ffn_b" + tag)
        da = _mm(dy2, w["ffn_w_down"][i], tb=True, name="ffn_down_dx" + tag)
        gr["ffn_w_down"][i] = _mm(a, dy2, ta=True, name="ffn_down_dw" + tag)
        dp, dcw, dcb = _ffn_mid_bwd(da, p, w["ffn_conv_w"][i], w["ffn_conv_b"][i:i + 1], "ffn_mid_b" + tag)
        gr["ffn_conv_w"][i], gr["ffn_conv_b"][i] = _unpair_cols(dcw), _unpair_cols(dcb)[0]
        dh2 = _mm(dp, w["ffn_w_up"][i], tb=True, name="ffn_up_dx" + tag)
        gr["ffn_w_up"][i] = _unpair_cols(_mm(h2, dp, ta=True, name="ffn_up_dw" + tag))
        dx1, d_g2, d_sc_f, d_sh_f = _norm_mod_bwd(dh2, x1, g2, sc_f, dx, "norm_ffn_b" + tag)
        dy, d_g1, d_gt_m = _post_bwd(dx1, y, g1, gt_m, "post_mix_b" + tag)
        if i % 2 == 0:
            dact = _mm(dy, w["rg_w_out"], tb=True, name="rg_out_dx" + tag)
            gr["rg_w_out"] = _mm(act, dy, ta=True, name="rg_out_dw" + tag)
            dpj, gr["rg_conv_w"], d_cb, gr["rg_wa"], d_ba, gr["rg_wx"], d_bx, d_lam = _rg_mid_bwd(
                dact, pj, aux, w["rg_conv_w"], row(w["rg_conv_b"]), w["rg_wa"], row(w["rg_ba"]), w["rg_wx"],
                row(w["rg_bx"]), row(w["rg_lambda"]), "rg_mid_b" + tag)
            gr["rg_conv_b"], gr["rg_ba"], gr["rg_bx"], gr["rg_lambda"] = d_cb[0], d_ba[0], d_bx[0], d_lam[0]
            dh = _mm(dpj, w["rg_w_in"], tb=True, name="rg_in_dx" + tag)
            gr["rg_w_in"] = _unpair_cols(_mm(h, dpj, ta=True, name="rg_in_dw" + tag))
        else:
            dact = _mm(dy, w["gla_w_out"], tb=True, name="gla_out_dx" + tag)
            gr["gla_w_out"] = _mm(act, dy, ta=True, name="gla_out_dw" + tag)
            dpj, d_wal, d_bal, d_ng = _gla_mid_bwd(dact, pj, aux[0], aux[1], w["gla_wal"], w["gla_bal"],
                                                   row(w["gla_norm_g"]), "gla_mid_b" + tag)
            gr["gla_w_alpha"] = jnp.swapaxes(d_wal[:, :GLA_RANK, :], 0, 1).reshape(GLA_RANK, GLA_HEADS * GLA_DK)
            gr["gla_b_alpha"], gr["gla_norm_g"] = d_bal.reshape(-1), d_ng[0]
            dh = _mm(dpj, w["gla_w_in"], tb=True, name="gla_in_dx" + tag)
            gr["gla_w_in"] = _gla_unhead_cols(_mm(h, dpj, ta=True, name="gla_in_dw" + tag))
        dx, d_g0, d_sc_m, d_sh_m = _norm_mod_bwd(dh, x0, g0, sc_m, dx1, "norm_mix_b" + tag)
        gr["norm_g"][i] = jnp.concatenate([d_g0, d_g1, d_g2, d_g3], axis=0)
        gr["mod"][i] = jnp.concatenate([d_sh_m, d_sc_m, d_gt_m, d_sh_f, d_sc_f, d_gt_f], axis=0)
    for k in ("norm_g", "ffn_w_up", "ffn_conv_w", "ffn_conv_b", "ffn_w_down", "mod"):
        gr[k] = jnp.stack(gr[k])
    return cols, dx, gr


def _kernel_weights(full):
    w = dict(full)
    w["ffn_w_up"] = _pair_cols(full["ffn_w_up"])
    w["ffn_conv_w"] = _pair_cols(full["ffn_conv_w"])
    w["ffn_conv_b"] = _pair_cols(full["ffn_conv_b"])
    w["rg_w_in"] = _pair_cols(full["rg_w_in"])
    w["gla_w_in"] = _gla_head_cols(full["gla_w_in"])
    w["gla_wal"], w["gla_bal"] = _gla_alpha_heads(full["gla_w_alpha"], full["gla_b_alpha"])
    return w


ADA_ROWS = 16


def _ada_fwd(c16, ada_w, ada_b, name):
    depth, d, n = ada_w.shape
    tn = _tile(n, (512, 256, 128))

    def body(c_ref, w_ref, b_ref, o_ref):
        cv = c_ref[...]
        o_ref[0] = _dot_nn(cv * _sigmoid(cv), w_ref[0]) + b_ref[0]

    return pl.pallas_call(
        body, grid=(depth, n // tn),
        in_specs=[pl.BlockSpec((ADA_ROWS, d), lambda l, j: (0, 0)), pl.BlockSpec((1, d, tn), lambda l, j: (l, 0, j)),
                  pl.BlockSpec((1, 1, tn), lambda l, j: (l, 0, j))],
        out_specs=pl.BlockSpec((1, ADA_ROWS, tn), lambda l, j: (l, 0, j)),
        out_shape=jax.ShapeDtypeStruct((depth, ADA_ROWS, n), F32),
        compiler_params=_params("parallel", "parallel"), name=name,
    )(c16, ada_w, ada_b)


def _ada_bwd(c16, dmod16, name):
    depth, _, n = dmod16.shape
    d = c16.shape[1]
    tn = _tile(n, (512, 256, 128))

    def body(c_ref, dm_ref, o_ref):
        cv = c_ref[...]
        o_ref[0] = _dot_tn(cv * _sigmoid(cv), dm_ref[0])

    return pl.pallas_call(
        body, grid=(depth, n // tn),
        in_specs=[pl.BlockSpec((ADA_ROWS, d), lambda l, j: (0, 0)), pl.BlockSpec((1, ADA_ROWS, tn), lambda l, j: (l, 0, j))],
        out_specs=pl.BlockSpec((1, d, tn), lambda l, j: (l, 0, j)),
        out_shape=jax.ShapeDtypeStruct((depth, d, n), F32),
        compiler_params=_params("parallel", "parallel"), name=name,
    )(c16, dmod16)


PACK_COLS = 1024
N_DEV = 8
N_CHIP = 4
_ANY = pl.BlockSpec(memory_space=pl.ANY)
_VMEM = pl.BlockSpec(memory_space=pltpu.VMEM)


def _place():
    return lax.axis_index("x"), lax.axis_index("y"), lax.axis_index("c")


def _other_chips(x, y):
    return [(1 - x, y), (x, 1 - y), (1 - x, 1 - y)]


def _rcopy(src, dst, send_sems, recv_sems, k, peer):
    return pltpu.make_async_remote_copy(src_ref=src, dst_ref=dst, send_sem=send_sems.at[k], recv_sem=recv_sems.at[k],
                                        device_id=peer, device_id_type=MESH)


def _all_gather_8(v, name):
    r, cc = v.shape

    def body(v_ref, out_ref, send_sems, recv_sems, local_sem):
        x, y, c = _place()
        me = 4 * x + 2 * y + c
        mine = pltpu.make_async_copy(v_ref, out_ref.at[me], local_sem)
        mine.start()
        peers = []
        for k in range(1, N_DEV):
            px = 1 - x if k & 4 else x
            py = 1 - y if k & 2 else y
            pc = 1 - c if k & 1 else c
            peers.append((px, py, pc))
        sends = [_rcopy(v_ref, out_ref.at[me], send_sems, recv_sems, k, p) for k, p in enumerate(peers)]
        for cp in sends:
            cp.start()
        for k, (px, py, pc) in enumerate(peers):
            _rcopy(v_ref, out_ref.at[4 * px + 2 * py + pc], send_sems, recv_sems, k, (px, py, pc)).wait_recv()
        for cp in sends:
            cp.wait_send()
        mine.wait()

    return pl.pallas_call(
        body, in_specs=[_VMEM], out_specs=_VMEM, out_shape=jax.ShapeDtypeStruct((N_DEV, r, cc), v.dtype),
        scratch_shapes=[pltpu.SemaphoreType.DMA((N_DEV - 1,)), pltpu.SemaphoreType.DMA((N_DEV - 1,)), pltpu.SemaphoreType.DMA],
        compiler_params=pltpu.CompilerParams(vmem_limit_bytes=VMEM_LIMIT), name=name,
    )(v)


def _gather_chips(buf, name):
    r, cc = buf.shape
    half = r // 2

    def body(buf_ref, out_ref, send_sems, recv_sems):
        x, y, c = _place()
        chip = 2 * x + y
        mine_rows, their_rows = pl.ds(c * half, half), pl.ds((1 - c) * half, half)
        chips = _other_chips(x, y)
        first = [_rcopy(buf_ref.at[mine_rows], out_ref.at[chip, mine_rows], send_sems, recv_sems, j, (px, py, c))
                 for j, (px, py) in enumerate(chips)]
        for cp in first:
            cp.start()
        passed = []
        for j, (px, py) in enumerate(chips):
            landed = out_ref.at[2 * px + py, mine_rows]
            _rcopy(buf_ref.at[mine_rows], landed, send_sems, recv_sems, j, (px, py, c)).wait_recv()
            fw = _rcopy(landed, landed, send_sems, recv_sems, N_CHIP - 1 + j, (x, y, 1 - c))
            fw.start()
            passed.append(fw)
        for j, (px, py) in enumerate(chips):
            landed = out_ref.at[2 * px + py, their_rows]
            _rcopy(landed, landed, send_sems, recv_sems, N_CHIP - 1 + j, (x, y, 1 - c)).wait_recv()
        for cp in first + passed:
            cp.wait_send()

    n_sem = 2 * (N_CHIP - 1)
    return pl.pallas_call(
        body, in_specs=[_ANY], out_specs=_ANY, out_shape=jax.ShapeDtypeStruct((N_CHIP, r, cc), buf.dtype),
        scratch_shapes=[pltpu.SemaphoreType.DMA((n_sem,)), pltpu.SemaphoreType.DMA((n_sem,))],
        name=name,
    )(buf)


def _pair_exchange(g, name):
    n, _, half, cc = g.shape

    def body(g_ref, out_ref, send_sems, recv_sems):
        x, y, c = _place()
        cp = _rcopy(g_ref.at[:, 1 - c], out_ref, send_sems, recv_sems, 0, (x, y, 1 - c))
        cp.start()
        cp.wait()

    return pl.pallas_call(
        body, in_specs=[_ANY], out_specs=_ANY, out_shape=jax.ShapeDtypeStruct((n, half, cc), g.dtype),
        scratch_shapes=[pltpu.SemaphoreType.DMA((1,)), pltpu.SemaphoreType.DMA((1,))], name=name,
    )(g)


def _pair_sum(g, other, c_idx, name):
    n, _, half, cc = g.shape
    tr = _tile(half, (640, 512, 256, 128, 64, 32, 16))

    def body(c_ref, g_ref, o_ref, out_ref):
        out_ref[...] = (g_ref[0] + o_ref[...]).astype(out_ref.dtype)

    return pl.pallas_call(
        body,
        grid_spec=pltpu.PrefetchScalarGridSpec(
            num_scalar_prefetch=1, grid=(n, half // tr),
            in_specs=[pl.BlockSpec((1, 1, tr, cc), lambda k, i, c_ref: (k, c_ref[0], i, 0)),
                      pl.BlockSpec((1, tr, cc), lambda k, i, c_ref: (k, i, 0))],
            out_specs=pl.BlockSpec((1, tr, cc), lambda k, i, c_ref: (k, i, 0))),
        out_shape=jax.ShapeDtypeStruct((n, half, cc), BF16),
        compiler_params=_params("parallel", "parallel"), name=name,
    )(c_idx, g, other)


def _chip_exchange(p, name):
    n, half, cc = p.shape

    def body(p_ref, out_ref, send_sems, recv_sems):
        x, y, c = _place()
        chip = 2 * x + y
        chips = _other_chips(x, y)
        sends = [_rcopy(p_ref.at[2 * px + py], out_ref.at[chip], send_sems, recv_sems, j, (px, py, c))
                 for j, (px, py) in enumerate(chips)]
        for cp in sends:
            cp.start()
        for j, (px, py) in enumerate(chips):
            _rcopy(p_ref.at[chip], out_ref.at[2 * px + py], send_sems, recv_sems, j, (px, py, c)).wait_recv()
        for cp in sends:
            cp.wait_send()

    return pl.pallas_call(
        body, in_specs=[_ANY], out_specs=_ANY, out_shape=jax.ShapeDtypeStruct((n, half, cc), p.dtype),
        scratch_shapes=[pltpu.SemaphoreType.DMA((N_CHIP - 1,)), pltpu.SemaphoreType.DMA((N_CHIP - 1,))],
        name=name,
    )(p)


def _sum_lead(v, name):
    n, r, cc = v.shape
    tr = _tile(r, (640, 512, 256, 128, 64, 32, 16, 8))

    def body(v_ref, o_ref):
        acc = v_ref[0].astype(F32)
        for k in range(1, n):
            acc = acc + v_ref[k].astype(F32)
        o_ref[...] = acc

    return pl.pallas_call(
        body, grid=(r // tr,), in_specs=[pl.BlockSpec((n, tr, cc), lambda i: (0, i, 0))],
        out_specs=pl.BlockSpec((tr, cc), lambda i: (i, 0)), out_shape=jax.ShapeDtypeStruct((r, cc), F32),
        compiler_params=_params("parallel"), name=name,
    )(v)


def _pair_share(red, name):
    half, cc = red.shape

    def body(r_ref, out_ref, send_sems, recv_sems):
        x, y, c = _place()
        cp = _rcopy(r_ref, out_ref.at[c], send_sems, recv_sems, 0, (x, y, 1 - c))
        cp.start()
        _rcopy(r_ref, out_ref.at[1 - c], send_sems, recv_sems, 0, (x, y, 1 - c)).wait_recv()
        cp.wait_send()

    return pl.pallas_call(
        body, in_specs=[_ANY], out_specs=_ANY, out_shape=jax.ShapeDtypeStruct((2, half, cc), red.dtype),
        scratch_shapes=[pltpu.SemaphoreType.DMA((1,)), pltpu.SemaphoreType.DMA((1,))],
        name=name,
    )(red)


def _pack(arrs, rows_multiple, dtype):
    flat = jnp.concatenate([a.reshape(-1).astype(dtype) for a in arrs])
    unit = rows_multiple * PACK_COLS
    total = -(-flat.shape[0] // unit) * unit
    return jnp.pad(flat, (0, total - flat.shape[0])).reshape(-1, PACK_COLS)


def _unpack(buf, shapes):
    lead = buf.shape[:-2]
    flat = buf.reshape(*lead, -1)
    out, off = [], 0
    for shp in shapes:
        n = 1
        for s in shp:
            n *= s
        out.append(flat[..., off:off + n].reshape(*lead, *shp))
        off += n
    return out


def _join_shards(parts, axis):
    moved = jnp.moveaxis(parts, 0, axis)
    shp = list(moved.shape)
    shp[axis:axis + 2] = [shp[axis] * shp[axis + 1]]
    return moved.reshape(shp)


def _split_shards(full, axis):
    shp = list(full.shape)
    shp[axis:axis + 1] = [N_CHIP, shp[axis] // N_CHIP]
    return jnp.moveaxis(full.reshape(shp), axis, 0)


def _my_shard(full, axis, chip):
    n = full.shape[axis] // N_CHIP
    return lax.dynamic_slice_in_dim(full, chip * n, n, axis)


BIG = {"ffn_w_up": 2, "ffn_w_down": 1, "rg_w_in": 2, "rg_wa": 2, "rg_wx": 2, "rg_w_out": 1, "gla_w_in": 2, "gla_w_out": 1}
SMALL = {"norm_g": 2, "ffn_conv_w": 2, "rg_conv_w": 2, "gla_w_alpha": 2, "gla_b_alpha": 1, "gla_norm_g": 1,
         "ada_b": None, "ffn_conv_b": None, "rg_conv_b": None, "rg_ba": None, "rg_bx": None, "rg_lambda": None}
WEIGHTS = ["ada_w", "ada_b", "norm_g", "ffn_w_up", "ffn_conv_w", "ffn_conv_b", "ffn_w_down", "rg_w_in", "rg_conv_w", "rg_conv_b",
           "rg_wa", "rg_ba", "rg_wx", "rg_bx", "rg_lambda", "rg_w_out", "gla_w_in", "gla_w_alpha", "gla_b_alpha", "gla_norm_g",
           "gla_w_out"]
BIG_ROWS_MULTIPLE = 1280


def kernel(x, c, ada_w, ada_b, norm_g, ffn_w_up, ffn_conv_w, ffn_conv_b, ffn_w_down, rg_w_in, rg_conv_w, rg_conv_b, rg_wa, rg_ba, rg_wx, rg_bx, rg_lambda, rg_w_out, gla_w_in, gla_w_alpha, gla_b_alpha, gla_norm_g, gla_w_out, loss_target, m_ada_w, m_ada_b, m_norm_g, m_ffn_w_up, m_ffn_conv_w, m_ffn_conv_b, m_ffn_w_down, m_rg_w_in, m_rg_conv_w, m_rg_conv_b, m_rg_wa, m_rg_ba, m_rg_wx, m_rg_bx, m_rg_lambda, m_rg_w_out, m_gla_w_in, m_gla_w_alpha, m_gla_b_alpha, m_gla_norm_g, m_gla_w_out, v_ada_w, v_ada_b, v_norm_g, v_ffn_w_up, v_ffn_conv_w, v_ffn_conv_b, v_ffn_w_down, v_rg_w_in, v_rg_conv_w, v_rg_conv_b, v_rg_wa, v_rg_ba, v_rg_wx, v_rg_bx, v_rg_lambda, v_rg_w_out, v_gla_w_in, v_gla_w_alpha, v_gla_b_alpha, v_gla_norm_g, v_gla_w_out):
    wts = dict(ada_w=ada_w, ada_b=ada_b, norm_g=norm_g, ffn_w_up=ffn_w_up, ffn_conv_w=ffn_conv_w, ffn_conv_b=ffn_conv_b,
               ffn_w_down=ffn_w_down, rg_w_in=rg_w_in, rg_conv_w=rg_conv_w, rg_conv_b=rg_conv_b, rg_wa=rg_wa, rg_ba=rg_ba,
               rg_wx=rg_wx, rg_bx=rg_bx, rg_lambda=rg_lambda, rg_w_out=rg_w_out, gla_w_in=gla_w_in, gla_w_alpha=gla_w_alpha,
               gla_b_alpha=gla_b_alpha, gla_norm_g=gla_norm_g, gla_w_out=gla_w_out)
    mom1 = dict(ada_w=m_ada_w, ada_b=m_ada_b, norm_g=m_norm_g, ffn_w_up=m_ffn_w_up, ffn_conv_w=m_ffn_conv_w,
                ffn_conv_b=m_ffn_conv_b, ffn_w_down=m_ffn_w_down, rg_w_in=m_rg_w_in, rg_conv_w=m_rg_conv_w,
                rg_conv_b=m_rg_conv_b, rg_wa=m_rg_wa, rg_ba=m_rg_ba, rg_wx=m_rg_wx, rg_bx=m_rg_bx, rg_lambda=m_rg_lambda,
                rg_w_out=m_rg_w_out, gla_w_in=m_gla_w_in, gla_w_alpha=m_gla_w_alpha, gla_b_alpha=m_gla_b_alpha,
                gla_norm_g=m_gla_norm_g, gla_w_out=m_gla_w_out)
    mom2 = dict(ada_w=v_ada_w, ada_b=v_ada_b, norm_g=v_norm_g, ffn_w_up=v_ffn_w_up, ffn_conv_w=v_ffn_conv_w,
                ffn_conv_b=v_ffn_conv_b, ffn_w_down=v_ffn_w_down, rg_w_in=v_rg_w_in, rg_conv_w=v_rg_conv_w,
                rg_conv_b=v_rg_conv_b, rg_wa=v_rg_wa, rg_ba=v_rg_ba, rg_wx=v_rg_wx, rg_bx=v_rg_bx, rg_lambda=v_rg_lambda,
                rg_w_out=v_rg_w_out, gla_w_in=v_gla_w_in, gla_w_alpha=v_gla_w_alpha, gla_b_alpha=v_gla_b_alpha,
                gla_norm_g=v_gla_norm_g, gla_w_out=v_gla_w_out)
    xi, yi, ci = _place()
    chip, me = 2 * xi + yi, 4 * xi + 2 * yi + ci
    d = x.shape[-1]
    depth = ada_w.shape[0]
    n_ada = ada_w.shape[-1]
    sharded_small = [k for k, ax in SMALL.items() if ax is not None]

    sm = _all_gather_8(_pack([c] + [wts[k] for k in sharded_small], SUBLANES, F32), "gather_small")
    c_all = sm[:, 0, :]
    parts = _unpack(sm[0::2], [c.shape] + [wts[k].shape for k in sharded_small])[1:]
    full = {k: _join_shards(p, SMALL[k]) for k, p in zip(sharded_small, parts)}
    for k, ax in SMALL.items():
        if ax is None:
            full[k] = wts[k]

    c16 = jnp.pad(c_all, ((0, ADA_ROWS - N_DEV), (0, 0)))
    ada_b_mine = lax.dynamic_slice_in_dim(ada_b, chip * n_ada, n_ada, 1)[:, None, :]
    mod_cols = _ada_fwd(c16, ada_w, ada_b_mine, "ada_fwd")
    mod_all = _all_gather_8(mod_cols.reshape(-1, PACK_COLS), "gather_mod")[0::2].reshape(N_CHIP, depth, ADA_ROWS, n_ada)
    mod = jnp.swapaxes(lax.dynamic_index_in_dim(mod_all, me, 2, keepdims=False), 0, 1).reshape(depth, 6, d)

    big_shapes = [wts[k].shape for k in BIG]
    wbuf = _pack([wts[k] for k in BIG], BIG_ROWS_MULTIPLE, BF16)
    wall = lax.dynamic_update_index_in_dim(_gather_chips(wbuf, "gather_weights"), wbuf, chip, 0)
    for k, p in zip(BIG, _unpack(wall, big_shapes)):
        full[k] = _join_shards(p, BIG[k])
    local = {k: (v if k in ("norm_g", "ffn_w_up", "ffn_conv_w", "ffn_conv_b", "ffn_w_down") else v[0]) for k, v in full.items()}

    cols, grad_x, gr = _local_step(x[0], loss_target[0], mod, _kernel_weights(local))
    loss = lax.psum(0.5 * jnp.sum(cols) / d, ("x", "y", "c"))

    small_names = [k for k in SMALL if k != "ada_b"]
    gs = _all_gather_8(_pack([gr[k] for k in small_names] + [gr["mod"]], SUBLANES, F32), "gather_small_grads")
    small_shapes = [full[k].shape for k in small_names] + [(depth, 6 * d)]
    *small_sum, g_ada_b = _unpack(_sum_lead(gs, "sum_small_grads"), small_shapes)
    grads = dict(zip(small_names, small_sum))
    grads["ada_b"] = g_ada_b
    for k in sharded_small:
        grads[k] = _my_shard(grads[k], SMALL[k], chip)
    dmod_all = _unpack(gs, small_shapes)[-1].reshape(N_DEV, depth, N_CHIP, n_ada)
    dmod_mine = jnp.swapaxes(lax.dynamic_index_in_dim(dmod_all, chip, 2, keepdims=False), 0, 1)
    grads["ada_w"] = _ada_bwd(c16, jnp.pad(dmod_mine, ((0, 0), (0, ADA_ROWS - N_DEV), (0, 0))), "ada_bwd")

    split = [_split_shards(gr[k].reshape(full[k].shape), BIG[k]).reshape(N_CHIP, -1) for k in BIG]
    flat = jnp.concatenate(split, axis=1)
    unit = BIG_ROWS_MULTIPLE * PACK_COLS
    total = -(-flat.shape[1] // unit) * unit
    gbuf = jnp.pad(flat, ((0, 0), (0, total - flat.shape[1]))).reshape(N_CHIP, 2, -1, PACK_COLS)
    theirs = _pair_exchange(gbuf, "grads_pair_exchange")
    psum = _pair_sum(gbuf, theirs, ci.reshape(1).astype(jnp.int32), "grads_pair_sum")
    arrived = _chip_exchange(psum, "grads_chip_exchange")
    arrived = lax.dynamic_update_index_in_dim(arrived, lax.dynamic_index_in_dim(psum, chip, 0, keepdims=False), chip, 0)
    red_half = _sum_lead(arrived, "grads_chip_sum")
    red = lax.dynamic_update_index_in_dim(_pair_share(red_half, "grads_pair_share"), red_half, ci, 0)
    for k, g in zip(BIG, _unpack(red.reshape(-1, PACK_COLS), big_shapes)):
        grads[k] = g

    delta, new_m, new_v = {}, {}, {}
    for k in ["ada_w"] + list(BIG):
        shp = wts[k].shape
        two_d = lambda a: a.reshape(-1, shp[-1])
        outs = _adamw(two_d(wts[k]), two_d(grads[k]), two_d(mom1[k]), two_d(mom2[k]), "adamw_" + k)
        delta[k], new_m[k], new_v[k] = (o.reshape(shp) for o in outs)
    small_shard_shapes = [wts[k].shape for k in SMALL]
    packed = [_pack([src[k] for k in SMALL], SUBLANES, F32) for src in (wts, grads, mom1, mom2)]
    outs = _adamw(*packed, "adamw_small")
    for dst, o in zip((delta, new_m, new_v), outs):
        for k, a in zip(SMALL, _unpack(o, small_shard_shapes)):
            dst[k] = a

    return (loss, grad_x[None], *[grads[k] for k in WEIGHTS], *[delta[k] for k in WEIGHTS], *[new_m[k] for k in WEIGHTS],
            *[new_v[k] for k in WEIGHTS])
```

```python
import jax
import jax.numpy as jnp
from jax import lax
from jax.experimental import pallas as pl
from jax.experimental.pallas import tpu as pltpu

F32 = jnp.float32
BF16 = jnp.bfloat16
MXU_DTYPE = BF16

EPS = 1e-6
RG_C = 8.0
RG_BLOCKS = 4
RG_CONV = 4
GLA_HEADS = 4
GLA_TAU = 16.0
GLA_CHUNK = 64
GLA_RANK = 16
FFN_CONV = 3
ADAM_LR = 0.001
ADAM_B1 = 0.9
ADAM_B2 = 0.999
ADAM_EPS = 1e-08
ADAM_WD = 0.01
ADAM_STEP = 10

LANES = 128
SUBLANES = 8
VMEM_LIMIT = 56 * 1024 * 1024
CB = 256
MESH = pl.DeviceIdType.MESH
N_DEV = 8
N_CHIP = 4


def _params(*sem):
    return pltpu.CompilerParams(dimension_semantics=sem, vmem_limit_bytes=VMEM_LIMIT)


def _tile(dim, prefs):
    for p in prefs:
        if dim % p == 0:
            return p
    return dim


def _dot(a, b, dims):
    return lax.dot_general(a.astype(MXU_DTYPE), b.astype(MXU_DTYPE), (dims, ((), ())), preferred_element_type=F32)


def _dot_nn(a, b):
    return _dot(a, b, ((1,), (0,)))


def _dot_nt(a, b):
    return _dot(a, b, ((1,), (1,)))


def _dot_tn(a, b):
    return _dot(a, b, ((0,), (0,)))


def _mm(a, b, *, ta=False, tb=False, a_parts=1, b_parts=1, w_slots=1, out_slots=1, out_dtype=F32, name):
    if ta:
        k_dim, m_dim = a.shape
        n_dim = b.shape[-1] * b_parts
    else:
        m_dim, k_dim = a.shape[-2], a.shape[-1] * a_parts
        n_dim = b.shape[-2] if tb else b.shape[-1] * w_slots
    n_unit = n_dim // max(b_parts, out_slots, 1 if tb else w_slots)
    k_unit = k_dim // max(a_parts, w_slots if tb else 1)
    tm = _tile(m_dim, (1024, 1408, 512, 256, 128))
    tn = _tile(n_unit, (1024, 1408, 896, 512, 256, 128))
    tk = _tile(k_unit, (1024, 1408, 896, 512, 256, 128))
    nk = k_dim // tk
    dims = ((0 if ta else 1,), (1 if tb else 0,))

    def spec(shape, parts, total, tile, col_grid, row_grid):
        per = total // parts // tile

        def index(i, j, k):
            g = {"i": i, "j": j, "k": k}
            col, row = g[col_grid], g[row_grid]
            return (row, col) if parts == 1 else (col // per, row, col % per)

        return pl.BlockSpec(shape if parts == 1 else (None,) + shape, index)

    def body(a_ref, b_ref, o_ref, acc_ref):
        k = pl.program_id(2)

        @pl.when(k == 0)
        def _():
            acc_ref[...] = jnp.zeros_like(acc_ref)

        acc_ref[...] += _dot(a_ref[...], b_ref[...], dims)

        @pl.when(k == nk - 1)
        def _():
            o_ref[...] = acc_ref[...].astype(o_ref.dtype)

    if ta:
        a_spec = spec((tk, tm), 1, m_dim, tm, "i", "k")
        b_spec = spec((tk, tn), b_parts, n_dim, tn, "j", "k")
    elif tb:
        a_spec = spec((tm, tk), a_parts, k_dim, tk, "k", "i")
        b_spec = spec((tn, tk), w_slots, k_dim, tk, "k", "j")
    else:
        a_spec = spec((tm, tk), a_parts, k_dim, tk, "k", "i")
        b_spec = spec((tk, tn), w_slots, n_dim, tn, "j", "k")
    out_shape = (m_dim, n_dim) if out_slots == 1 else (out_slots, m_dim, n_dim // out_slots)
    return pl.pallas_call(
        body,
        grid=(m_dim // tm, n_dim // tn, nk),
        in_specs=[a_spec, b_spec],
        out_specs=spec((tm, tn), out_slots, n_dim, tn, "j", "i"),
        out_shape=jax.ShapeDtypeStruct(out_shape, out_dtype),
        scratch_shapes=[pltpu.VMEM((tm, tn), F32)],
        compiler_params=_params("parallel", "parallel", "arbitrary"),
        name=name,
    )(a, b)


def _row_specs(s, d, ts):
    return pl.BlockSpec((ts, d), lambda i: (i, 0)), pl.BlockSpec((1, d), lambda i: (0, 0))


def _norm_mod_fwd(x, g, sc, sh, name):
    s, d = x.shape
    ts = _tile(s, (512,))

    def body(x_ref, g_ref, sc_ref, sh_ref, h_ref):
        xv = x_ref[...]
        r = lax.rsqrt(jnp.mean(xv * xv, axis=-1, keepdims=True) + EPS)
        h_ref[...] = (((xv * r) * g_ref[...]) * (1.0 + sc_ref[...]) + sh_ref[...]).astype(h_ref.dtype)

    row, vec = _row_specs(s, d, ts)
    return pl.pallas_call(
        body, grid=(s // ts,), in_specs=[row, vec, vec, vec], out_specs=row,
        out_shape=jax.ShapeDtypeStruct((s, d), MXU_DTYPE), compiler_params=_params("parallel"), name=name,
    )(x, g, sc, sh)


def _norm_mod_bwd(dh, x, g, sc, dres, name):
    s, d = x.shape
    ts = _tile(s, (512,))

    def body(dh_ref, x_ref, g_ref, sc_ref, dres_ref, dx_ref, dg_ref, dsc_ref, dsh_ref, acc_ref):
        i = pl.program_id(0)

        @pl.when(i == 0)
        def _():
            acc_ref[...] = jnp.zeros_like(acc_ref)

        xv, dhv = x_ref[...], dh_ref[...]
        r = lax.rsqrt(jnp.mean(xv * xv, axis=-1, keepdims=True) + EPS)
        n = xv * r
        acc_ref[0:1, :] += jnp.sum(dhv * n, axis=0, keepdims=True)
        acc_ref[1:2, :] += jnp.sum(dhv, axis=0, keepdims=True)
        dn = dhv * ((1.0 + sc_ref[...]) * g_ref[...])
        dx_ref[...] = dres_ref[...] + r * (dn - n * jnp.mean(dn * n, axis=-1, keepdims=True))
        dg_ref[...] = (1.0 + sc_ref[...]) * acc_ref[0:1, :]
        dsc_ref[...] = g_ref[...] * acc_ref[0:1, :]
        dsh_ref[...] = acc_ref[1:2, :]

    row, vec = _row_specs(s, d, ts)
    vshape = jax.ShapeDtypeStruct((1, d), F32)
    return pl.pallas_call(
        body, grid=(s // ts,), in_specs=[row, row, vec, vec, row], out_specs=[row, vec, vec, vec],
        out_shape=[jax.ShapeDtypeStruct((s, d), F32), vshape, vshape, vshape],
        scratch_shapes=[pltpu.VMEM((SUBLANES, d), F32)], compiler_params=_params("arbitrary"), name=name,
    )(dh, x, g, sc, dres)


def _post_fwd(x, y, g, gt, name):
    s, d = x.shape
    ts = _tile(s, (512,))

    def body(x_ref, y_ref, g_ref, gt_ref, o_ref):
        yv = y_ref[...]
        r = lax.rsqrt(jnp.mean(yv * yv, axis=-1, keepdims=True) + EPS)
        o_ref[...] = x_ref[...] + gt_ref[...] * ((yv * r) * g_ref[...])

    row, vec = _row_specs(s, d, ts)
    return pl.pallas_call(
        body, grid=(s // ts,), in_specs=[row, row, vec, vec], out_specs=row,
        out_shape=jax.ShapeDtypeStruct((s, d), F32), compiler_params=_params("parallel"), name=name,
    )(x, y, g, gt)


def _post_bwd(dxn, y, g, gt, name):
    s, d = y.shape
    ts = _tile(s, (512,))

    def body(dxn_ref, y_ref, g_ref, gt_ref, dy_ref, dg_ref, dgt_ref, acc_ref):
        i = pl.program_id(0)

        @pl.when(i == 0)
        def _():
            acc_ref[...] = jnp.zeros_like(acc_ref)

        yv, dv = y_ref[...], dxn_ref[...]
        r = lax.rsqrt(jnp.mean(yv * yv, axis=-1, keepdims=True) + EPS)
        n = yv * r
        acc_ref[0:1, :] += jnp.sum(dv * n, axis=0, keepdims=True)
        dn = dv * (gt_ref[...] * g_ref[...])
        dy_ref[...] = (r * (dn - n * jnp.mean(dn * n, axis=-1, keepdims=True))).astype(dy_ref.dtype)
        dg_ref[...] = gt_ref[...] * acc_ref[0:1, :]
        dgt_ref[...] = g_ref[...] * acc_ref[0:1, :]

    row, vec = _row_specs(s, d, ts)
    vshape = jax.ShapeDtypeStruct((1, d), F32)
    return pl.pallas_call(
        body, grid=(s // ts,), in_specs=[row, row, vec, vec], out_specs=[row, vec, vec],
        out_shape=[jax.ShapeDtypeStruct((s, d), MXU_DTYPE), vshape, vshape],
        scratch_shapes=[pltpu.VMEM((SUBLANES, d), F32)], compiler_params=_params("arbitrary"), name=name,
    )(dxn, y, g, gt)


def _loss_grad(x, tgt, name):
    s, d = x.shape
    ts = _tile(s, (512,))

    def body(x_ref, t_ref, col_ref, dx_ref):
        i = pl.program_id(0)

        @pl.when(i == 0)
        def _():
            col_ref[...] = jnp.zeros_like(col_ref)

        e = x_ref[...] - t_ref[...]
        col_ref[...] += jnp.sum(e * e, axis=0, keepdims=True)
        dx_ref[...] = e * (1.0 / d)

    row, vec = _row_specs(s, d, ts)
    return pl.pallas_call(
        body, grid=(s // ts,), in_specs=[row, row], out_specs=[vec, row],
        out_shape=[jax.ShapeDtypeStruct((1, d), F32), jax.ShapeDtypeStruct((s, d), F32)],
        compiler_params=_params("arbitrary"), name=name,
    )(x, tgt)


_GELU_C = 0.7978845608028654
_GELU_A = 0.044715


def _gelu(x):
    t = jnp.tanh(_GELU_C * (x + _GELU_A * x * x * x))
    return 0.5 * x * (1.0 + t), t


def _gelu_grad(x, t):
    return 0.5 * (1.0 + t) + 0.5 * x * (1.0 - t * t) * (_GELU_C * (1.0 + 3.0 * _GELU_A * x * x))


def _sigmoid(x):
    return 1.0 / (1.0 + jnp.exp(-x))


def _log1p_pos(y):
    u = 1.0 + y
    return jnp.where(u == 1.0, y, jnp.log(u) * (y / jnp.where(u == 1.0, 1.0, u - 1.0)))


def _softplus(x):
    return jnp.maximum(x, 0.0) + _log1p_pos(jnp.exp(-jnp.abs(x)))


def _one_minus_exp(z):
    u = jnp.exp(z)
    lg = jnp.log(jnp.where(u > 0.0, u, 1.0))
    safe = (u != 1.0) & (u > 0.0)
    return jnp.where(u == 1.0, -z, jnp.where(u > 0.0, (1.0 - u) * (z / jnp.where(safe, lg, 1.0)), 1.0))


SLAB = 16


def _cat(a, b):
    return jnp.concatenate([a, b], axis=1)


def _fold8(x):
    out = x[0:SUBLANES]
    for r in range(SUBLANES, x.shape[0], SUBLANES):
        out = out + x[r:r + SUBLANES]
    return out


def _pair_specs(shape, nb, index):
    return [pl.BlockSpec(shape, lambda j, t: index(j, t) + (j,)), pl.BlockSpec(shape, lambda j, t: index(j, t) + (j + nb,))]


def _halo_row(ts, time_of):
    return lambda j, t: (jnp.maximum(time_of(t) * (ts // SUBLANES) - 1, 0),)


def _ffn_mid_fwd(p, cw, cb, name):
    s, f2 = p.shape
    ts = _tile(s, (512,))
    nb, nt = f2 // (2 * CB), s // ts

    def body(pg_ref, pv_ref, hg_ref, hv_ref, cwg_ref, cwv_ref, cbg_ref, cbv_ref, a_ref):
        t = pl.program_id(1)
        cwv, bias = _cat(cwg_ref[...], cwv_ref[...]), _cat(cbg_ref[...], cbv_ref[...])
        w0, w1, w2 = cwv[0:1], cwv[1:2], cwv[2:3]

        def slab(blk, r0):
            u = bias + w0 * blk[6:6 + SLAB] + w1 * blk[7:7 + SLAB] + w2 * blk[8:8 + SLAB]
            a_ref[pl.ds(r0, SLAB), :] = (_gelu(u[:, :CB])[0] * u[:, CB:]).astype(a_ref.dtype)

        halo = jnp.where(t > 0, _cat(hg_ref[...], hv_ref[...]), 0.0)
        slab(jnp.concatenate([halo, _cat(pg_ref[0:SLAB, :], pv_ref[0:SLAB, :])], axis=0), 0)

        def loop(i, carry):
            r0 = pl.multiple_of(i * SLAB, SLAB)
            rows = pl.ds(pl.multiple_of(r0 - SUBLANES, SUBLANES), SLAB + SUBLANES)
            slab(_cat(pg_ref[rows, :], pv_ref[rows, :]), r0)
            return carry

        lax.fori_loop(1, ts // SLAB, loop, 0, unroll=2)

    fwd = lambda t: t
    return pl.pallas_call(
        body, grid=(nb, nt),
        in_specs=(_pair_specs((ts, CB), nb, lambda j, t: (t,)) + _pair_specs((SUBLANES, CB), nb, _halo_row(ts, fwd))
                  + _pair_specs((FFN_CONV, CB), nb, lambda j, t: (0,)) + _pair_specs((1, CB), nb, lambda j, t: (0,))),
        out_specs=pl.BlockSpec((ts, CB), lambda j, t: (t, j)),
        out_shape=jax.ShapeDtypeStruct((s, f2 // 2), MXU_DTYPE),
        compiler_params=_params("parallel", "arbitrary"), name=name,
    )(p, p, p, p, cw, cw, cb, cb)


def _ffn_mid_bwd(da, p, cw, cb, name):
    s, f2 = p.shape
    ts = _tile(s, (512,))
    nb, nt = f2 // (2 * CB), s // ts
    n_slab = ts // SLAB

    def body(da_ref, pg_ref, pv_ref, hg_ref, hv_ref, cwg_ref, cwv_ref, cbg_ref, cbv_ref, dp_ref, dcw_ref, dcb_ref,
             next_du, acc):
        tt = pl.program_id(1)
        t = nt - 1 - tt
        cwv, bias = _cat(cwg_ref[...], cwv_ref[...]), _cat(cbg_ref[...], cbv_ref[...])
        w0, w1, w2 = cwv[0:1], cwv[1:2], cwv[2:3]

        @pl.when(tt == 0)
        def _():
            next_du[...] = jnp.zeros_like(next_du)
            acc[...] = jnp.zeros_like(acc)

        def slab(blk, r0, carry):
            pm2, pm1, p0 = blk[6:6 + SLAB], blk[7:7 + SLAB], blk[8:8 + SLAB]
            u = bias + w0 * pm2 + w1 * pm1 + w2 * p0
            g, v = u[:, :CB], u[:, CB:]
            gel, th = _gelu(g)
            dav = da_ref[pl.ds(r0, SLAB), :]
            du = _cat(dav * v * _gelu_grad(g, th), dav * gel)
            ext = jnp.concatenate([du, carry], axis=0)
            dpv = (w2 * du + w1 * ext[1:1 + SLAB] + w0 * ext[2:2 + SLAB]).astype(dp_ref.dtype)
            dp_ref[0, pl.ds(r0, SLAB), :] = dpv[:, :CB]
            dp_ref[1, pl.ds(r0, SLAB), :] = dpv[:, CB:]
            acc[0] += _fold8(du)
            acc[1] += _fold8(du * pm2)
            acc[2] += _fold8(du * pm1)
            acc[3] += _fold8(du * p0)
            return du[0:SUBLANES]

        def loop(k, carry):
            r0 = pl.multiple_of((n_slab - 1 - k) * SLAB, SLAB)
            rows = pl.ds(pl.multiple_of(r0 - SUBLANES, SUBLANES), SLAB + SUBLANES)
            return slab(_cat(pg_ref[rows, :], pv_ref[rows, :]), r0, carry)

        carry = lax.fori_loop(0, n_slab - 1, loop, next_du[...], unroll=2)
        halo = jnp.where(t > 0, _cat(hg_ref[...], hv_ref[...]), 0.0)
        next_du[...] = slab(jnp.concatenate([halo, _cat(pg_ref[0:SLAB, :], pv_ref[0:SLAB, :])], axis=0), 0, carry)

        @pl.when(tt == nt - 1)
        def _():
            for half in range(2):
                cols = slice(half * CB, (half + 1) * CB)
                dcb_ref[half] = jnp.sum(acc[0][:, cols], axis=0, keepdims=True)
                for k in range(FFN_CONV):
                    dcw_ref[half, k:k + 1, :] = jnp.sum(acc[1 + k][:, cols], axis=0, keepdims=True)

    rev = lambda t: nt - 1 - t
    return pl.pallas_call(
        body, grid=(nb, nt),
        in_specs=([pl.BlockSpec((ts, CB), lambda j, t: (rev(t), j))] + _pair_specs((ts, CB), nb, lambda j, t: (rev(t),))
                  + _pair_specs((SUBLANES, CB), nb, _halo_row(ts, rev)) + _pair_specs((FFN_CONV, CB), nb, lambda j, t: (0,))
                  + _pair_specs((1, CB), nb, lambda j, t: (0,))),
        out_specs=[pl.BlockSpec((2, ts, CB), lambda j, t: (0, rev(t), j)),
                   pl.BlockSpec((2, FFN_CONV, CB), lambda j, t: (0, 0, j)),
                   pl.BlockSpec((2, 1, CB), lambda j, t: (0, 0, j))],
        out_shape=[jax.ShapeDtypeStruct((2, s, f2 // 2), MXU_DTYPE), jax.ShapeDtypeStruct((2, FFN_CONV, f2 // 2), F32),
                   jax.ShapeDtypeStruct((2, 1, f2 // 2), F32)],
        scratch_shapes=[pltpu.VMEM((SUBLANES, 2 * CB), F32), pltpu.VMEM((1 + FFN_CONV, SUBLANES, 2 * CB), F32)],
        compiler_params=_params("parallel", "arbitrary"), name=name,
    )(da, p, p, p, p, cw, cw, cb, cb)


def _rg_gates(xc, wa_ref, ba_ref, wx_ref, bx_ref, lam_ref):
    r = _sigmoid(_dot_nn(xc, wa_ref[0]) + ba_ref[...])
    ig = _sigmoid(_dot_nn(xc, wx_ref[0]) + bx_ref[...])
    sp = _softplus(-lam_ref[...])
    log_a = (-RG_C) * r * sp
    a = jnp.exp(log_a)
    mult = jnp.sqrt(_one_minus_exp(2.0 * log_a))
    return r, ig, sp, a, mult


def _rg_conv(scr, cw_ref, cb_ref, ts):
    views = [scr[5 + k:5 + k + ts, :] for k in range(RG_CONV)]
    xc = cb_ref[...]
    for k in range(RG_CONV):
        xc = xc + cw_ref[k:k + 1, :] * views[k]
    return xc, views


def _rg_param_specs():
    vec = pl.BlockSpec((1, CB), lambda g, t: (0, g))
    mat = pl.BlockSpec((1, CB, CB), lambda g, t: (g, 0, 0))
    return [pl.BlockSpec((RG_CONV, CB), lambda g, t: (0, g)), vec, mat, vec, mat, vec, vec]


NSEG = SUBLANES
NQ = CB // LANES


def _lanes(q):
    return slice(q * LANES, (q + 1) * LANES)


def _seg_scan(a_scr, x_scr, loc_scr, dec_scr, ts, reverse):
    seg = ts // NSEG

    def step(k, carry):
        out = []
        rows = pl.ds(seg - 1 - k if reverse else k, NSEG, stride=seg)
        for q in range(NQ):
            st, dec = carry[q]
            a_q, x_q, loc_q, dec_q = a_scr.at[q], x_scr.at[q], loc_scr.at[q], dec_scr.at[q]
            av = a_q[rows, :]
            if reverse:
                loc_q[rows, :] = st
                dec_q[rows, :] = dec
                st = av * (x_q[rows, :] + st)
                dec = av * dec
            else:
                st = av * st + x_q[rows, :]
                dec = av * dec
                loc_q[rows, :] = st
                dec_q[rows, :] = dec
            out.append((st, dec))
        return tuple(out)

    init = tuple((jnp.zeros((NSEG, LANES), F32), jnp.ones((NSEG, LANES), F32)) for _ in range(NQ))
    return lax.fori_loop(0, seg, step, init, unroll=4)


def _seg_chain(fin, dec, c_in, reverse):
    rows = [None] * NSEG
    c = c_in
    for sgm in (reversed(range(NSEG)) if reverse else range(NSEG)):
        rows[sgm] = c
        c = fin[sgm:sgm + 1] + dec[sgm:sgm + 1] * c
    return jnp.concatenate(rows, axis=0), c


def _rg_mid_fwd(pj, cw, cb, wa, ba, wx, bx, lam, name):
    s = pj.shape[0]
    nb = pj.shape[1] // (2 * CB)
    ts = _tile(s, (512,))
    nt = s // ts
    seg = ts // NSEG

    def body(gate_ref, x_ref, halo_ref, cw_ref, cb_ref, wa_ref, ba_ref, wx_ref, bx_ref, lam_ref, y_ref, hs_ref,
             scr, a_scr, u_scr, loc_scr, dec_scr, h_scr):
        t = pl.program_id(1)

        @pl.when(t == 0)
        def _():
            h_scr[...] = jnp.zeros_like(h_scr)

        scr[0:SUBLANES, :] = jnp.where(t > 0, halo_ref[...], 0.0)
        scr[SUBLANES:, :] = x_ref[...]
        xc, _ = _rg_conv(scr, cw_ref, cb_ref, ts)
        _, ig, _, a, mult = _rg_gates(xc, wa_ref, ba_ref, wx_ref, bx_ref, lam_ref)
        u = mult * (ig * xc)
        for q in range(NQ):
            a_scr[q] = a[:, _lanes(q)]
            u_scr[q] = u[:, _lanes(q)]
        fin = _seg_scan(a_scr, u_scr, loc_scr, dec_scr, ts, False)
        for q in range(NQ):
            enter, leave = _seg_chain(fin[q][0], fin[q][1], h_scr[0:1, _lanes(q)], False)
            h_scr[0:1, _lanes(q)] = leave
            for sgm in range(NSEG):
                rows = slice(sgm * seg, (sgm + 1) * seg)
                hs_ref[rows, _lanes(q)] = loc_scr[q, rows, :] + dec_scr[q, rows, :] * enter[sgm:sgm + 1]
        y_ref[...] = (_gelu(gate_ref[...])[0] * hs_ref[...]).astype(y_ref.dtype)

    blk = pl.BlockSpec((ts, CB), lambda g, t: (t, g))
    lane_scr = pltpu.VMEM((NQ, ts, LANES), F32)
    return pl.pallas_call(
        body, grid=(nb, nt),
        in_specs=_pair_specs((ts, CB), nb, lambda g, t: (t,))
        + [pl.BlockSpec((SUBLANES, CB), lambda g, t: _halo_row(ts, lambda u: u)(g, t) + (g + nb,))] + _rg_param_specs(),
        out_specs=[blk, blk],
        out_shape=[jax.ShapeDtypeStruct((s, nb * CB), MXU_DTYPE), jax.ShapeDtypeStruct((s, nb * CB), F32)],
        scratch_shapes=[pltpu.VMEM((ts + SUBLANES, CB), F32), lane_scr, lane_scr, lane_scr, lane_scr,
                        pltpu.VMEM((SUBLANES, CB), F32)],
        compiler_params=_params("parallel", "arbitrary"), name=name,
    )(pj, pj, pj, cw, cb, wa, ba, wx, bx, lam)


def _rg_mid_bwd(dy, pj, hs, cw, cb, wa, ba, wx, bx, lam, name):
    s = pj.shape[0]
    nb = pj.shape[1] // (2 * CB)
    ts = _tile(s, (512,))
    nt = s // ts

    def body(dy_ref, gate_ref, x_ref, halo_ref, hs_ref, hsh_ref, cw_ref, cb_ref, wa_ref, ba_ref, wx_ref, bx_ref, lam_ref,
             dpj_ref, dcw_ref, dcb_ref, dwa_ref, dba_ref, dwx_ref, dbx_ref, dlam_ref,
             scr, hscr, a_scr, d_scr, loc_scr, dec_scr, g_scr, dxscr, c_scr):
        tt = pl.program_id(1)
        t = nt - 1 - tt
        seg = ts // NSEG

        @pl.when(tt == 0)
        def _():
            c_scr[...] = jnp.zeros_like(c_scr)
            dxscr[ts:, :] = jnp.zeros((SUBLANES, CB), F32)
            for ref in (dcw_ref, dcb_ref, dwa_ref, dba_ref, dwx_ref, dbx_ref, dlam_ref):
                ref[...] = jnp.zeros_like(ref)

        scr[0:SUBLANES, :] = jnp.where(t > 0, halo_ref[...], 0.0)
        scr[SUBLANES:, :] = x_ref[...]
        hscr[0:SUBLANES, :] = jnp.where(t > 0, hsh_ref[...], 0.0)
        hscr[SUBLANES:, :] = hs_ref[...]
        xc, views = _rg_conv(scr, cw_ref, cb_ref, ts)
        r, ig, sp, a, mult = _rg_gates(xc, wa_ref, ba_ref, wx_ref, bx_ref, lam_ref)
        gate = gate_ref[...]
        gel, th = _gelu(gate)
        dyv = dy_ref[...]
        dpj_ref[0] = (dyv * hs_ref[...] * _gelu_grad(gate, th)).astype(dpj_ref.dtype)
        dhs = dyv * gel
        for q in range(NQ):
            a_scr[q] = a[:, _lanes(q)]
            d_scr[q] = dhs[:, _lanes(q)]
        fin = _seg_scan(a_scr, d_scr, loc_scr, dec_scr, ts, True)
        for q in range(NQ):
            enter, leave = _seg_chain(fin[q][0], fin[q][1], c_scr[0:1, _lanes(q)], True)
            c_scr[0:1, _lanes(q)] = leave
            for sgm in range(NSEG):
                rows = slice(sgm * seg, (sgm + 1) * seg)
                g_scr[rows, _lanes(q)] = d_scr[q, rows, :] + loc_scr[q, rows, :] + dec_scr[q, rows, :] * enter[sgm:sgm + 1]
        du = g_scr[...]
        da = du * hscr[7:7 + ts, :]
        dmult = du * (ig * xc)
        dig = du * (mult * xc)
        dxc = du * (mult * ig)
        dlog_a = da * a - dmult * (a * a / mult)
        dlam_ref[...] += jnp.sum(dlog_a * r, axis=0, keepdims=True) * (RG_C * _sigmoid(-lam_ref[...]))
        dpr = dlog_a * ((-RG_C) * sp) * (r * (1.0 - r))
        dpi = dig * (ig * (1.0 - ig))
        dba_ref[...] += jnp.sum(dpr, axis=0, keepdims=True)
        dbx_ref[...] += jnp.sum(dpi, axis=0, keepdims=True)
        dwa_ref[0] += _dot_tn(xc, dpr)
        dwx_ref[0] += _dot_tn(xc, dpi)
        dxc = dxc + _dot_nt(dpr, wa_ref[0]) + _dot_nt(dpi, wx_ref[0])
        dcb_ref[...] += jnp.sum(dxc, axis=0, keepdims=True)
        for k in range(RG_CONV):
            dcw_ref[k:k + 1, :] += jnp.sum(dxc * views[k], axis=0, keepdims=True)
        dxscr[0:ts, :] = dxc
        dxp = cw_ref[3:4, :] * dxc
        for k in range(RG_CONV - 1):
            dxp = dxp + cw_ref[k:k + 1, :] * dxscr[3 - k:3 - k + ts, :]
        dpj_ref[1] = dxp.astype(dpj_ref.dtype)
        dxscr[ts:, :] = dxscr[0:SUBLANES, :]

    rev = lambda g, t: (nt - 1 - t, g)
    rev_halo = lambda g, t: (jnp.maximum((nt - 1 - t) * (ts // SUBLANES) - 1, 0), g)
    vec = pl.BlockSpec((1, CB), lambda g, t: (0, g))
    mat = pl.BlockSpec((1, CB, CB), lambda g, t: (g, 0, 0))
    d = nb * CB
    vshape = jax.ShapeDtypeStruct((1, d), F32)
    mshape = jax.ShapeDtypeStruct((nb, CB, CB), F32)
    return pl.pallas_call(
        body, grid=(nb, nt),
        in_specs=[pl.BlockSpec((ts, CB), rev)] + _pair_specs((ts, CB), nb, lambda g, t: (nt - 1 - t,))
        + [pl.BlockSpec((SUBLANES, CB), lambda g, t: (rev_halo(g, t)[0], g + nb)),
           pl.BlockSpec((ts, CB), rev), pl.BlockSpec((SUBLANES, CB), rev_halo)] + _rg_param_specs(),
        out_specs=[pl.BlockSpec((2, ts, CB), lambda g, t: (0, nt - 1 - t, g)), pl.BlockSpec((RG_CONV, CB), lambda g, t: (0, g)),
                   vec, mat, vec, mat, vec, vec],
        out_shape=[jax.ShapeDtypeStruct((2, s, d), MXU_DTYPE), jax.ShapeDtypeStruct((RG_CONV, d), F32), vshape, mshape, vshape,
                   mshape, vshape, vshape],
        scratch_shapes=[pltpu.VMEM((ts + SUBLANES, CB), F32), pltpu.VMEM((ts + SUBLANES, CB), F32)]
        + [pltpu.VMEM((NQ, ts, LANES), F32)] * 4
        + [pltpu.VMEM((ts, CB), F32), pltpu.VMEM((ts + SUBLANES, CB), F32), pltpu.VMEM((SUBLANES, CB), F32)],
        compiler_params=_params("parallel", "arbitrary"), name=name,
    )(dy, pj, pj, pj, hs, hs, cw, cb, wa, ba, wx, bx, lam)


GLA_DK = 128
GLA_DV = 256
GLA_HB = 2 * GLA_DK + 2 * GLA_DV + LANES
GLA_TS = 256


def _split3(x):
    hi = x.astype(BF16)
    r1 = x - hi.astype(F32)
    mid = r1.astype(BF16)
    lo = (r1 - mid.astype(F32)).astype(BF16)
    return hi, mid, lo


def _chunk_cumsum(x, reverse):
    n = x.shape[0]
    i = lax.broadcasted_iota(jnp.int32, (n, n), 0)
    j = lax.broadcasted_iota(jnp.int32, (n, n), 1)
    same = (i // GLA_CHUNK) == (j // GLA_CHUNK)
    tri = jnp.where(same & ((j >= i) if reverse else (j <= i)), 1.0, 0.0).astype(BF16)
    out = jnp.zeros(x.shape, F32)
    for piece in _split3(x):
        out = out + lax.dot_general(tri, piece, (((1,), (0,)), ((), ())), preferred_element_type=F32)
    return out


def _gla_split(blk):
    q = blk[:, 0:GLA_DK] * (GLA_DK ** -0.5)
    k = blk[:, GLA_DK:2 * GLA_DK]
    v = blk[:, 2 * GLA_DK:2 * GLA_DK + GLA_DV]
    r = blk[:, 2 * GLA_DK + GLA_DV:2 * GLA_DK + 2 * GLA_DV]
    z = blk[:, 2 * GLA_DK + 2 * GLA_DV:]
    return q, k, v, r, z


def _gla_decays(gc):
    gref = gc[GLA_CHUNK // 2:GLA_CHUNK // 2 + 1, :]
    glast = gc[GLA_CHUNK - 1:GLA_CHUNK, :]
    return jnp.exp(gc), jnp.exp(gc - gref), jnp.exp(gref - gc), jnp.exp(glast - gc), jnp.exp(glast)


def _causal_mask():
    i = lax.broadcasted_iota(jnp.int32, (GLA_CHUNK, GLA_CHUNK), 0)
    j = lax.broadcasted_iota(jnp.int32, (GLA_CHUNK, GLA_CHUNK), 1)
    return j <= i


def _log_sigmoid(x):
    return jnp.minimum(x, 0.0) - _log1p_pos(jnp.exp(-jnp.abs(x)))


def _gla_mid_fwd(pj, wal, bal, ng, name):
    s = pj.shape[0]
    nh = pj.shape[1] // GLA_HB
    ts = _tile(s, (GLA_TS,))
    nt, nc = s // ts, ts // GLA_CHUNK

    def body(pj_ref, wal_ref, bal_ref, ng_ref, act_ref, o_ref, st_ref, s_scr):
        t = pl.program_id(1)

        @pl.when(t == 0)
        def _():
            s_scr[...] = jnp.zeros_like(s_scr)

        q, k, v, r, z = _gla_split(pj_ref[...])
        g = _log_sigmoid(_dot_nn(z, wal_ref[0]) + bal_ref[0]) * (1.0 / GLA_TAU)
        gcum = _chunk_cumsum(g, False)
        mask = _causal_mask()
        for c in range(nc):
            sl = slice(c * GLA_CHUNK, (c + 1) * GLA_CHUNK)
            eg, eq, ek, ekd, egl = _gla_decays(gcum[sl])
            st = s_scr[...]
            st_ref[c, 0] = st
            attn = jnp.where(mask, _dot_nt(q[sl] * eq, k[sl] * ek), 0.0)
            o_ref[sl, :] = _dot_nt(q[sl] * eg, st) + _dot_nn(attn, v[sl])
            s_scr[...] = st * egl + _dot_tn(v[sl], k[sl] * ekd)
        o = o_ref[...]
        on = o * lax.rsqrt(jnp.mean(o * o, axis=-1, keepdims=True) + EPS)
        act_ref[...] = ((on * ng_ref[...]) * (r * _sigmoid(r))).astype(act_ref.dtype)

    blk = pl.BlockSpec((ts, GLA_DV), lambda h, t: (t, h))
    return pl.pallas_call(
        body, grid=(nh, nt),
        in_specs=[pl.BlockSpec((ts, GLA_HB), lambda h, t: (t, h)),
                  pl.BlockSpec((1, LANES, GLA_DK), lambda h, t: (h, 0, 0)),
                  pl.BlockSpec((1, 1, GLA_DK), lambda h, t: (h, 0, 0)),
                  pl.BlockSpec((1, GLA_DV), lambda h, t: (0, 0))],
        out_specs=[blk, blk, pl.BlockSpec((nc, 1, GLA_DV, GLA_DK), lambda h, t: (t, h, 0, 0))],
        out_shape=[jax.ShapeDtypeStruct((s, nh * GLA_DV), MXU_DTYPE), jax.ShapeDtypeStruct((s, nh * GLA_DV), F32),
                   jax.ShapeDtypeStruct((s // GLA_CHUNK, nh, GLA_DV, GLA_DK), F32)],
        scratch_shapes=[pltpu.VMEM((GLA_DV, GLA_DK), F32)],
        compiler_params=_params("parallel", "arbitrary"), name=name,
    )(pj, wal, bal, ng)


def _gla_mid_bwd(dact, pj, o, st, wal, bal, ng, name):
    s = pj.shape[0]
    nh = pj.shape[1] // GLA_HB
    ts = _tile(s, (GLA_TS,))
    nt, nc = s // ts, ts // GLA_CHUNK

    def body(dact_ref, pj_ref, o_ref, st_ref, wal_ref, bal_ref, ng_ref, dpj_ref, dwal_ref, dbal_ref, dng_ref,
             ds_scr, dg_scr):
        h, tt = pl.program_id(0), pl.program_id(1)

        @pl.when(tt == 0)
        def _():
            ds_scr[...] = jnp.zeros_like(ds_scr)
            dwal_ref[...] = jnp.zeros_like(dwal_ref)
            dbal_ref[...] = jnp.zeros_like(dbal_ref)

        @pl.when((tt == 0) & (h == 0))
        def _():
            dng_ref[...] = jnp.zeros_like(dng_ref)

        q, k, v, r, z = _gla_split(pj_ref[...])
        logit = _dot_nn(z, wal_ref[0]) + bal_ref[0]
        gcum = _chunk_cumsum(_log_sigmoid(logit) * (1.0 / GLA_TAU), False)
        ov = o_ref[...]
        ro = lax.rsqrt(jnp.mean(ov * ov, axis=-1, keepdims=True) + EPS)
        on = ov * ro
        sg = _sigmoid(r)
        sil = r * sg
        dav = dact_ref[...]
        dpj_ref[:, 2 * GLA_DK + GLA_DV:2 * GLA_DK + 2 * GLA_DV] = (
            dav * (on * ng_ref[...]) * (sg + sil * (1.0 - sg))).astype(dpj_ref.dtype)
        t1 = dav * sil
        dng_ref[...] += jnp.sum(t1 * on, axis=0, keepdims=True)
        dn = t1 * ng_ref[...]
        do = ro * (dn - on * jnp.mean(dn * on, axis=-1, keepdims=True))
        mask = _causal_mask()
        scale = GLA_DK ** -0.5
        for c in reversed(range(nc)):
            sl = slice(c * GLA_CHUNK, (c + 1) * GLA_CHUNK)
            eg, eq, ek, ekd, egl = _gla_decays(gcum[sl])
            qc, kc, vc, doc = q[sl], k[sl], v[sl], do[sl]
            qg, qt, kt, kd = qc * eg, qc * eq, kc * ek, kc * ekd
            sp = st_ref[c, 0]
            ds = ds_scr[...]
            attn = jnp.where(mask, _dot_nt(qt, kt), 0.0)
            dattn = jnp.where(mask, _dot_nt(doc, vc), 0.0)
            dqg = _dot_nn(doc, sp)
            dqt = _dot_nn(dattn, kt)
            dkt = _dot_tn(dattn, qt)
            dkd = _dot_nn(vc, ds)
            dpj_ref[sl, 2 * GLA_DK:2 * GLA_DK + GLA_DV] = (_dot_tn(attn, doc) + _dot_nt(kd, ds)).astype(dpj_ref.dtype)
            dpj_ref[sl, 0:GLA_DK] = (scale * (dqg * eg + dqt * eq)).astype(dpj_ref.dtype)
            dpj_ref[sl, GLA_DK:2 * GLA_DK] = (dkt * ek + dkd * ekd).astype(dpj_ref.dtype)
            kdd = dkd * kd
            dgl = jnp.sum(kdd, axis=0, keepdims=True) + jnp.sum(ds * sp, axis=0, keepdims=True) * egl
            row = lax.broadcasted_iota(jnp.int32, (GLA_CHUNK, GLA_DK), 0)
            dg_scr[sl, :] = dqg * qg + dqt * qt - dkt * kt - kdd + jnp.where(row == GLA_CHUNK - 1, dgl, 0.0)
            ds_scr[...] = ds * egl + _dot_tn(doc, qg)
        dlogit = _chunk_cumsum(dg_scr[...], True) * (1.0 / GLA_TAU) * _sigmoid(-logit)
        dpj_ref[:, 2 * GLA_DK + 2 * GLA_DV:] = _dot_nt(dlogit, wal_ref[0]).astype(dpj_ref.dtype)
        dwal_ref[0] += _dot_tn(z, dlogit)
        dbal_ref[0] += jnp.sum(dlogit, axis=0, keepdims=True)

    rev = lambda h, t: (nt - 1 - t, h)
    return pl.pallas_call(
        body, grid=(nh, nt),
        in_specs=[pl.BlockSpec((ts, GLA_DV), rev), pl.BlockSpec((ts, GLA_HB), rev), pl.BlockSpec((ts, GLA_DV), rev),
                  pl.BlockSpec((nc, 1, GLA_DV, GLA_DK), lambda h, t: (nt - 1 - t, h, 0, 0)),
                  pl.BlockSpec((1, LANES, GLA_DK), lambda h, t: (h, 0, 0)),
                  pl.BlockSpec((1, 1, GLA_DK), lambda h, t: (h, 0, 0)),
                  pl.BlockSpec((1, GLA_DV), lambda h, t: (0, 0))],
        out_specs=[pl.BlockSpec((ts, GLA_HB), rev),
                   pl.BlockSpec((1, LANES, GLA_DK), lambda h, t: (h, 0, 0)),
                   pl.BlockSpec((1, 1, GLA_DK), lambda h, t: (h, 0, 0)),
                   pl.BlockSpec((1, GLA_DV), lambda h, t: (0, 0))],
        out_shape=[jax.ShapeDtypeStruct((s, nh * GLA_HB), MXU_DTYPE), jax.ShapeDtypeStruct((nh, LANES, GLA_DK), F32),
                   jax.ShapeDtypeStruct((nh, 1, GLA_DK), F32), jax.ShapeDtypeStruct((1, GLA_DV), F32)],
        scratch_shapes=[pltpu.VMEM((GLA_DV, GLA_DK), F32), pltpu.VMEM((ts, GLA_DK), F32)],
        compiler_params=_params("arbitrary", "arbitrary"), name=name,
    )(dact, pj, o, st, wal, bal, ng)


def _adamw(w, gs, m, v, name):
    layers, rows, cols = w.shape
    tr = _tile(rows, (256, 128, 64, 32, 16, 8))
    c1 = 1.0 / (1.0 - ADAM_B1 ** ADAM_STEP)
    c2 = 1.0 / (1.0 - ADAM_B2 ** ADAM_STEP)

    def body(*refs):
        g_refs, (w_ref, m_ref, v_ref, go_ref, d_ref, mo_ref, vo_ref) = refs[:layers], refs[layers:]
        gv = g_refs[0][...]
        for l in range(1, layers):
            gv = jnp.where(pl.program_id(0) == l, g_refs[l][...], gv)
        m2 = ADAM_B1 * m_ref[...] + (1.0 - ADAM_B1) * gv
        v2 = ADAM_B2 * v_ref[...] + (1.0 - ADAM_B2) * (gv * gv)
        d_ref[...] = (-ADAM_LR) * ((m2 * c1) / (jnp.sqrt(v2 * c2) + ADAM_EPS) + ADAM_WD * w_ref[...])
        go_ref[...] = gv
        mo_ref[...] = m2
        vo_ref[...] = v2

    g_spec = pl.BlockSpec((tr, cols), lambda l, i: (i, 0))
    spec = pl.BlockSpec((None, tr, cols), lambda l, i: (l, i, 0))
    shape = jax.ShapeDtypeStruct((layers, rows, cols), F32)
    return pl.pallas_call(
        body, grid=(layers, rows // tr), in_specs=[g_spec] * layers + [spec] * 3, out_specs=[spec] * 4, out_shape=[shape] * 4,
        compiler_params=_params("parallel", "parallel"), name=name,
    )(*gs, w, m, v)


def _gla_head_cols(w):
    d = w.shape[0]
    qk, dv = GLA_HEADS * GLA_DK, GLA_HEADS * GLA_DV
    q, k, v, r, z = jnp.split(w, [qk, 2 * qk, 2 * qk + dv, 2 * qk + 2 * dv], axis=1)
    zp = jnp.pad(z, ((0, 0), (0, LANES - GLA_RANK)))
    parts = [q.reshape(d, GLA_HEADS, GLA_DK), k.reshape(d, GLA_HEADS, GLA_DK), v.reshape(d, GLA_HEADS, GLA_DV),
             r.reshape(d, GLA_HEADS, GLA_DV), jnp.broadcast_to(zp[:, None, :], (d, GLA_HEADS, LANES))]
    return jnp.concatenate(parts, axis=2).reshape(d, GLA_HEADS * GLA_HB)


def _gla_unhead_cols(w):
    d = w.shape[0]
    w = w.reshape(d, GLA_HEADS, GLA_HB)
    o = 2 * GLA_DK + 2 * GLA_DV
    parts = [w[:, :, 0:GLA_DK].reshape(d, -1), w[:, :, GLA_DK:2 * GLA_DK].reshape(d, -1),
             w[:, :, 2 * GLA_DK:2 * GLA_DK + GLA_DV].reshape(d, -1), w[:, :, 2 * GLA_DK + GLA_DV:o].reshape(d, -1),
             jnp.sum(w[:, :, o:o + GLA_RANK], axis=1)]
    return jnp.concatenate(parts, axis=1)


def _gla_alpha_heads(w_alpha, b_alpha):
    wal = jnp.swapaxes(w_alpha.reshape(GLA_RANK, GLA_HEADS, GLA_DK), 0, 1)
    return jnp.pad(wal, ((0, 0), (0, LANES - GLA_RANK), (0, 0))), b_alpha.reshape(GLA_HEADS, 1, GLA_DK)


def _gla_layouts(w):
    w = dict(w)
    w["gla_wal"], w["gla_bal"] = _gla_alpha_heads(w["gla_w_alpha"], w["gla_b_alpha"])
    w["gla_w_in"] = _gla_head_cols(w["gla_w_in"])
    return w


def _col_slots(w):
    r, c = w.shape
    return jnp.moveaxis(w.reshape(r, N_CHIP, c // N_CHIP), 1, 0)


def _from_col_slots(w):
    n, r, c = w.shape
    return jnp.moveaxis(w, 0, 1).reshape(r, n * c)


def _block_rows_to_slots(w):
    g, r4, cc = w.shape
    return jnp.swapaxes(w.reshape(g, N_CHIP, r4 // N_CHIP, cc), 0, 1).reshape(N_CHIP, g * (r4 // N_CHIP), cc)


def _slots_to_block_rows(w, g):
    n, gr, cc = w.shape
    return jnp.swapaxes(w.reshape(n, g, gr // g, cc), 0, 1).reshape(g, n * (gr // g), cc)


def _local_step(x, tgt, mod, w):
    depth = mod.shape[0]
    row = lambda v: v.reshape(1, -1)
    saved = []
    for i in range(depth):
        sh_m, sc_m, gt_m, sh_f, sc_f, gt_f = (mod[i, j:j + 1] for j in range(6))
        g0, g1, g2, g3 = (w["norm_g"][i, j:j + 1] for j in range(4))
        tag = f"_l{i}"
        h = _norm_mod_fwd(x, g0, sc_m, sh_m, "norm_mix" + tag)
        if i % 2 == 0:
            pj = _mm(h, w["rg_w_in"], w_slots=N_CHIP, name="rg_in" + tag)
            act, aux = _rg_mid_fwd(pj, w["rg_conv_w"], row(w["rg_conv_b"]), w["rg_wa"], row(w["rg_ba"]), w["rg_wx"],
                                   row(w["rg_bx"]), row(w["rg_lambda"]), "rg_mid" + tag)
            y = _mm(act, w["rg_w_out"], name="rg_out" + tag)
        else:
            pj = _mm(h, w["gla_w_in"], name="gla_in" + tag)
            act, *aux = _gla_mid_fwd(pj, w["gla_wal"], w["gla_bal"], row(w["gla_norm_g"]), "gla_mid" + tag)
            y = _mm(act, w["gla_w_out"], name="gla_out" + tag)
        x1 = _post_fwd(x, y, g1, gt_m, "post_mix" + tag)
        h2 = _norm_mod_fwd(x1, g2, sc_f, sh_f, "norm_ffn" + tag)
        p = _mm(h2, w["ffn_w_up"][i], w_slots=N_CHIP, name="ffn_up" + tag)
        a = _ffn_mid_fwd(p, w["ffn_conv_w"][i], w["ffn_conv_b"][i:i + 1], "ffn_mid" + tag)
        y2 = _mm(a, w["ffn_w_down"][i], name="ffn_down" + tag)
        x2 = _post_fwd(x1, y2, g3, gt_f, "post_ffn" + tag)
        saved.append((x, h, pj, act, aux, y, x1, h2, p, a, y2))
        x = x2

    cols, dx = _loss_grad(x, tgt, "loss")

    stacked = ("norm_g", "ffn_conv_w", "ffn_conv_b", "mod")
    gr = {k: [None] * depth for k in stacked + ("ffn_w_up", "ffn_w_down")}
    for i in reversed(range(depth)):
        x0, h, pj, act, aux, y, x1, h2, p, a, y2 = saved[i]
        sh_m, sc_m, gt_m, sh_f, sc_f, gt_f = (mod[i, j:j + 1] for j in range(6))
        g0, g1, g2, g3 = (w["norm_g"][i, j:j + 1] for j in range(4))
        tag = f"_l{i}"
        dy2, d_g3, d_gt_f = _post_bwd(dx, y2, g3, gt_f, "post_ffn_b" + tag)
        da = _mm(dy2, w["ffn_w_down"][i], tb=True, name="ffn_down_dx" + tag)
        gr["ffn_w_down"][i] = _mm(a, dy2, ta=True, name="ffn_down_dw" + tag)
        dp, dcw, dcb = _ffn_mid_bwd(da, p, w["ffn_conv_w"][i], w["ffn_conv_b"][i:i + 1], "ffn_mid_b" + tag)
        gr["ffn_conv_w"][i], gr["ffn_conv_b"][i] = _cat(dcw[0], dcw[1]), _cat(dcb[0], dcb[1])[0]
        dh2 = _mm(dp, w["ffn_w_up"][i], tb=True, a_parts=2, w_slots=N_CHIP, name="ffn_up_dx" + tag)
        gr["ffn_w_up"][i] = _mm(h2, dp, ta=True, b_parts=2, out_slots=N_CHIP, name="ffn_up_dw" + tag)
        dx1, d_g2, d_sc_f, d_sh_f = _norm_mod_bwd(dh2, x1, g2, sc_f, dx, "norm_ffn_b" + tag)
        dy, d_g1, d_gt_m = _post_bwd(dx1, y, g1, gt_m, "post_mix_b" + tag)
        if i % 2 == 0:
            dact = _mm(dy, w["rg_w_out"], tb=True, name="rg_out_dx" + tag)
            gr["rg_w_out"] = _mm(act, dy, ta=True, name="rg_out_dw" + tag)
            dpj, gr["rg_conv_w"], d_cb, gr["rg_wa"], d_ba, gr["rg_wx"], d_bx, d_lam = _rg_mid_bwd(
                dact, pj, aux, w["rg_conv_w"], row(w["rg_conv_b"]), w["rg_wa"], row(w["rg_ba"]), w["rg_wx"],
                row(w["rg_bx"]), row(w["rg_lambda"]), "rg_mid_b" + tag)
            gr["rg_conv_b"], gr["rg_ba"], gr["rg_bx"], gr["rg_lambda"] = d_cb[0], d_ba[0], d_bx[0], d_lam[0]
            dh = _mm(dpj, w["rg_w_in"], tb=True, a_parts=2, w_slots=N_CHIP, name="rg_in_dx" + tag)
            gr["rg_w_in"] = _mm(h, dpj, ta=True, b_parts=2, out_slots=N_CHIP, name="rg_in_dw" + tag)
        else:
            dact = _mm(dy, w["gla_w_out"], tb=True, name="gla_out_dx" + tag)
            gr["gla_w_out"] = _mm(act, dy, ta=True, name="gla_out_dw" + tag)
            dpj, d_wal, d_bal, d_ng = _gla_mid_bwd(dact, pj, aux[0], aux[1], w["gla_wal"], w["gla_bal"],
                                                   row(w["gla_norm_g"]), "gla_mid_b" + tag)
            gr["gla_w_alpha"] = jnp.swapaxes(d_wal[:, :GLA_RANK, :], 0, 1).reshape(GLA_RANK, GLA_HEADS * GLA_DK)
            gr["gla_b_alpha"], gr["gla_norm_g"] = d_bal.reshape(-1), d_ng[0]
            dh = _mm(dpj, w["gla_w_in"], tb=True, name="gla_in_dx" + tag)
            gr["gla_w_in"] = _gla_unhead_cols(_mm(h, dpj, ta=True, name="gla_in_dw" + tag))
        dx, d_g0, d_sc_m, d_sh_m = _norm_mod_bwd(dh, x0, g0, sc_m, dx1, "norm_mix_b" + tag)
        gr["norm_g"][i] = jnp.concatenate([d_g0, d_g1, d_g2, d_g3], axis=0)
        gr["mod"][i] = jnp.concatenate([d_sh_m, d_sc_m, d_gt_m, d_sh_f, d_sc_f, d_gt_f], axis=0)
    for k in stacked:
        gr[k] = jnp.stack(gr[k])
    return cols, dx, gr


ADA_ROWS = 16


def _ada_fwd(c16, ada_w, ada_b, name):
    depth, d, n = ada_w.shape
    tn = _tile(n, (512, 256, 128))

    def body(c_ref, w_ref, b_ref, o_ref):
        cv = c_ref[...]
        o_ref[0] = _dot_nn(cv * _sigmoid(cv), w_ref[0]) + b_ref[0]

    return pl.pallas_call(
        body, grid=(depth, n // tn),
        in_specs=[pl.BlockSpec((ADA_ROWS, d), lambda l, j: (0, 0)), pl.BlockSpec((1, d, tn), lambda l, j: (l, 0, j)),
                  pl.BlockSpec((1, 1, tn), lambda l, j: (l, 0, j))],
        out_specs=pl.BlockSpec((1, ADA_ROWS, tn), lambda l, j: (l, 0, j)),
        out_shape=jax.ShapeDtypeStruct((depth, ADA_ROWS, n), F32),
        compiler_params=_params("parallel", "parallel"), name=name,
    )(c16, ada_w, ada_b)


def _ada_bwd(c16, dmod16, name):
    depth, _, n = dmod16.shape
    d = c16.shape[1]
    tn = _tile(n, (512, 256, 128))

    def body(c_ref, dm_ref, o_ref):
        cv = c_ref[...]
        o_ref[0] = _dot_tn(cv * _sigmoid(cv), dm_ref[0])

    return pl.pallas_call(
        body, grid=(depth, n // tn),
        in_specs=[pl.BlockSpec((ADA_ROWS, d), lambda l, j: (0, 0)), pl.BlockSpec((1, ADA_ROWS, tn), lambda l, j: (l, 0, j))],
        out_specs=pl.BlockSpec((1, d, tn), lambda l, j: (l, 0, j)),
        out_shape=jax.ShapeDtypeStruct((depth, d, n), F32),
        compiler_params=_params("parallel", "parallel"), name=name,
    )(c16, dmod16)


PACK_COLS = 1024
_ANY = pl.BlockSpec(memory_space=pl.ANY)
_VMEM = pl.BlockSpec(memory_space=pltpu.VMEM)


def _place():
    return lax.axis_index("x"), lax.axis_index("y"), lax.axis_index("c")


def _other_chips(x, y):
    return [(1 - x, y), (x, 1 - y), (1 - x, 1 - y)]


def _rcopy(src, dst, send_sems, recv_sems, k, peer):
    return pltpu.make_async_remote_copy(src_ref=src, dst_ref=dst, send_sem=send_sems.at[k], recv_sem=recv_sems.at[k],
                                        device_id=peer, device_id_type=MESH)


def _all_gather_8(v, name):
    r, cc = v.shape

    def body(v_ref, out_ref, send_sems, recv_sems, local_sem):
        x, y, c = _place()
        me = 4 * x + 2 * y + c
        mine = pltpu.make_async_copy(v_ref, out_ref.at[me], local_sem)
        mine.start()
        peers = []
        for k in range(1, N_DEV):
            px = 1 - x if k & 4 else x
            py = 1 - y if k & 2 else y
            pc = 1 - c if k & 1 else c
            peers.append((px, py, pc))
        sends = [_rcopy(v_ref, out_ref.at[me], send_sems, recv_sems, k, p) for k, p in enumerate(peers)]
        for cp in sends:
            cp.start()
        for k, (px, py, pc) in enumerate(peers):
            _rcopy(v_ref, out_ref.at[4 * px + 2 * py + pc], send_sems, recv_sems, k, (px, py, pc)).wait_recv()
        for cp in sends:
            cp.wait_send()
        mine.wait()

    return pl.pallas_call(
        body, in_specs=[_VMEM], out_specs=_VMEM, out_shape=jax.ShapeDtypeStruct((N_DEV, r, cc), v.dtype),
        scratch_shapes=[pltpu.SemaphoreType.DMA((N_DEV - 1,)), pltpu.SemaphoreType.DMA((N_DEV - 1,)), pltpu.SemaphoreType.DMA],
        compiler_params=pltpu.CompilerParams(vmem_limit_bytes=VMEM_LIMIT), name=name,
    )(v)


def _gather_chips(shards, name):
    n = len(shards)
    per = 2 * (N_CHIP - 1)

    def body(*refs):
        ins, outs, (send_sems, recv_sems) = refs[:n], refs[n:2 * n], refs[2 * n:]
        x, y, c = _place()
        chip = 2 * x + y
        chips = _other_chips(x, y)
        rows = [(pl.ds(c * (r.shape[0] // 2), r.shape[0] // 2), pl.ds((1 - c) * (r.shape[0] // 2), r.shape[0] // 2)) for r in ins]
        first = [_rcopy(ins[i].at[rows[i][0]], outs[i].at[chip, rows[i][0]], send_sems, recv_sems, per * i + j, (px, py, c))
                 for i in range(n) for j, (px, py) in enumerate(chips)]
        for cp in first:
            cp.start()
        passed = []
        for i in range(n):
            for j, (px, py) in enumerate(chips):
                landed = outs[i].at[2 * px + py, rows[i][0]]
                _rcopy(ins[i].at[rows[i][0]], landed, send_sems, recv_sems, per * i + j, (px, py, c)).wait_recv()
                fw = _rcopy(landed, landed, send_sems, recv_sems, per * i + N_CHIP - 1 + j, (x, y, 1 - c))
                fw.start()
                passed.append(fw)
        for i in range(n):
            for j, (px, py) in enumerate(chips):
                landed = outs[i].at[2 * px + py, rows[i][1]]
                _rcopy(landed, landed, send_sems, recv_sems, per * i + N_CHIP - 1 + j, (x, y, 1 - c)).wait_recv()
        for cp in first + passed:
            cp.wait_send()

    return pl.pallas_call(
        body, in_specs=[_ANY] * n, out_specs=[_ANY] * n,
        out_shape=[jax.ShapeDtypeStruct((N_CHIP,) + sh.shape, sh.dtype) for sh in shards],
        scratch_shapes=[pltpu.SemaphoreType.DMA((per * n,)), pltpu.SemaphoreType.DMA((per * n,))], name=name,
    )(*shards)


def _pair_exchange(gs, name):
    n = len(gs)

    def body(*refs):
        ins, outs, (send_sems, recv_sems) = refs[:n], refs[n:2 * n], refs[2 * n:]
        x, y, c = _place()
        copies = []
        for i in range(n):
            half = ins[i].shape[1] // 2
            copies.append(_rcopy(ins[i].at[:, pl.ds((1 - c) * half, half)], outs[i], send_sems, recv_sems, i, (x, y, 1 - c)))
        for cp in copies:
            cp.start()
        for cp in copies:
            cp.wait()

    return pl.pallas_call(
        body, in_specs=[_ANY] * n, out_specs=[_ANY] * n,
        out_shape=[jax.ShapeDtypeStruct((g.shape[0], g.shape[1] // 2, g.shape[2]), g.dtype) for g in gs],
        scratch_shapes=[pltpu.SemaphoreType.DMA((n,)), pltpu.SemaphoreType.DMA((n,))], name=name,
    )(*gs)


_ROW_TILES = (640, 512, 352, 256, 128, 64, 32, 16)


def _pair_sum(g, other, c_idx, name):
    n, half, cc = other.shape
    tr = _tile(half, _ROW_TILES)

    def body(c_ref, g_ref, o_ref, out_ref):
        out_ref[...] = (g_ref[...] + o_ref[...]).astype(out_ref.dtype)

    return pl.pallas_call(
        body,
        grid_spec=pltpu.PrefetchScalarGridSpec(
            num_scalar_prefetch=1, grid=(n, half // tr),
            in_specs=[pl.BlockSpec((None, None, tr, cc), lambda k, i, c_ref: (k, c_ref[0], i, 0)),
                      pl.BlockSpec((None, tr, cc), lambda k, i, c_ref: (k, i, 0))],
            out_specs=pl.BlockSpec((None, tr, cc), lambda k, i, c_ref: (k, i, 0))),
        out_shape=jax.ShapeDtypeStruct((n, half, cc), BF16),
        compiler_params=_params("parallel", "parallel"), name=name,
    )(c_idx, g.reshape(n, 2, half, cc), other)


def _chip_exchange(ps, name):
    n = len(ps)
    per = N_CHIP - 1

    def body(*refs):
        ins, outs, (send_sems, recv_sems) = refs[:n], refs[n:2 * n], refs[2 * n:]
        x, y, c = _place()
        chip = 2 * x + y
        chips = _other_chips(x, y)
        sends = [_rcopy(ins[i].at[2 * px + py], outs[i].at[chip], send_sems, recv_sems, per * i + j, (px, py, c))
                 for i in range(n) for j, (px, py) in enumerate(chips)]
        for cp in sends:
            cp.start()
        for i in range(n):
            for j, (px, py) in enumerate(chips):
                _rcopy(ins[i].at[chip], outs[i].at[2 * px + py], send_sems, recv_sems, per * i + j, (px, py, c)).wait_recv()
        for cp in sends:
            cp.wait_send()

    return pl.pallas_call(
        body, in_specs=[_ANY] * n, out_specs=[_ANY] * n, out_shape=[jax.ShapeDtypeStruct(p.shape, p.dtype) for p in ps],
        scratch_shapes=[pltpu.SemaphoreType.DMA((per * n,)), pltpu.SemaphoreType.DMA((per * n,))], name=name,
    )(*ps)


def _sum_lead(v, name):
    n, r, cc = v.shape
    tr = _tile(r, _ROW_TILES + (8,))

    def body(v_ref, o_ref):
        acc = v_ref[0].astype(F32)
        for k in range(1, n):
            acc = acc + v_ref[k].astype(F32)
        o_ref[...] = acc

    return pl.pallas_call(
        body, grid=(r // tr,), in_specs=[pl.BlockSpec((n, tr, cc), lambda i: (0, i, 0))],
        out_specs=pl.BlockSpec((tr, cc), lambda i: (i, 0)), out_shape=jax.ShapeDtypeStruct((r, cc), F32),
        compiler_params=_params("parallel"), name=name,
    )(v)


def _pair_share(reds, name):
    n = len(reds)

    def body(*refs):
        ins, outs, (send_sems, recv_sems) = refs[:n], refs[n:2 * n], refs[2 * n:]
        x, y, c = _place()
        copies = [_rcopy(ins[i], outs[i].at[c], send_sems, recv_sems, i, (x, y, 1 - c)) for i in range(n)]
        for cp in copies:
            cp.start()
        for i in range(n):
            _rcopy(ins[i], outs[i].at[1 - c], send_sems, recv_sems, i, (x, y, 1 - c)).wait_recv()
        for cp in copies:
            cp.wait_send()

    return pl.pallas_call(
        body, in_specs=[_ANY] * n, out_specs=[_ANY] * n, out_shape=[jax.ShapeDtypeStruct((2,) + r.shape, r.dtype) for r in reds],
        scratch_shapes=[pltpu.SemaphoreType.DMA((n,)), pltpu.SemaphoreType.DMA((n,))], name=name,
    )(*reds)


def _pack(arrs, rows_multiple, dtype):
    flat = jnp.concatenate([a.reshape(-1).astype(dtype) for a in arrs])
    unit = rows_multiple * PACK_COLS
    total = -(-flat.shape[0] // unit) * unit
    return jnp.pad(flat, (0, total - flat.shape[0])).reshape(-1, PACK_COLS)


def _unpack(buf, shapes):
    lead = buf.shape[:-2]
    flat = buf.reshape(*lead, -1)
    out, off = [], 0
    for shp in shapes:
        n = 1
        for s in shp:
            n *= s
        out.append(flat[..., off:off + n].reshape(*lead, *shp))
        off += n
    return out


def _join_shards(parts, axis):
    moved = jnp.moveaxis(parts, 0, axis)
    shp = list(moved.shape)
    shp[axis:axis + 2] = [shp[axis] * shp[axis + 1]]
    return moved.reshape(shp)


def _my_shard(full, axis, chip):
    n = full.shape[axis] // N_CHIP
    return lax.dynamic_slice_in_dim(full, chip * n, n, axis)


SMALL = {"norm_g": 2, "ffn_conv_w": 2, "rg_conv_w": 2, "gla_w_alpha": 2, "gla_b_alpha": 1, "gla_norm_g": 1,
         "ada_b": None, "ffn_conv_b": None, "rg_conv_b": None, "rg_ba": None, "rg_bx": None, "rg_lambda": None}
BIG = {"rg_w_in": True, "rg_wa": False, "rg_wx": False, "rg_w_out": False, "ffn_w_up": True, "ffn_w_down": False,
       "gla_w_in": True, "gla_w_out": False}
WEIGHTS = ["ada_w", "ada_b", "norm_g", "ffn_w_up", "ffn_conv_w", "ffn_conv_b", "ffn_w_down", "rg_w_in", "rg_conv_w", "rg_conv_b",
           "rg_wa", "rg_ba", "rg_wx", "rg_bx", "rg_lambda", "rg_w_out", "gla_w_in", "gla_w_alpha", "gla_b_alpha", "gla_norm_g",
           "gla_w_out"]


def kernel(x, c, ada_w, ada_b, norm_g, ffn_w_up, ffn_conv_w, ffn_conv_b, ffn_w_down, rg_w_in, rg_conv_w, rg_conv_b, rg_wa, rg_ba, rg_wx, rg_bx, rg_lambda, rg_w_out, gla_w_in, gla_w_alpha, gla_b_alpha, gla_norm_g, gla_w_out, loss_target, m_ada_w, m_ada_b, m_norm_g, m_ffn_w_up, m_ffn_conv_w, m_ffn_conv_b, m_ffn_w_down, m_rg_w_in, m_rg_conv_w, m_rg_conv_b, m_rg_wa, m_rg_ba, m_rg_wx, m_rg_bx, m_rg_lambda, m_rg_w_out, m_gla_w_in, m_gla_w_alpha, m_gla_b_alpha, m_gla_norm_g, m_gla_w_out, v_ada_w, v_ada_b, v_norm_g, v_ffn_w_up, v_ffn_conv_w, v_ffn_conv_b, v_ffn_w_down, v_rg_w_in, v_rg_conv_w, v_rg_conv_b, v_rg_wa, v_rg_ba, v_rg_wx, v_rg_bx, v_rg_lambda, v_rg_w_out, v_gla_w_in, v_gla_w_alpha, v_gla_b_alpha, v_gla_norm_g, v_gla_w_out):
    wts = dict(ada_w=ada_w, ada_b=ada_b, norm_g=norm_g, ffn_w_up=ffn_w_up, ffn_conv_w=ffn_conv_w, ffn_conv_b=ffn_conv_b,
               ffn_w_down=ffn_w_down, rg_w_in=rg_w_in, rg_conv_w=rg_conv_w, rg_conv_b=rg_conv_b, rg_wa=rg_wa, rg_ba=rg_ba,
               rg_wx=rg_wx, rg_bx=rg_bx, rg_lambda=rg_lambda, rg_w_out=rg_w_out, gla_w_in=gla_w_in, gla_w_alpha=gla_w_alpha,
               gla_b_alpha=gla_b_alpha, gla_norm_g=gla_norm_g, gla_w_out=gla_w_out)
    mom1 = dict(ada_w=m_ada_w, ada_b=m_ada_b, norm_g=m_norm_g, ffn_w_up=m_ffn_w_up, ffn_conv_w=m_ffn_conv_w,
                ffn_conv_b=m_ffn_conv_b, ffn_w_down=m_ffn_w_down, rg_w_in=m_rg_w_in, rg_conv_w=m_rg_conv_w,
                rg_conv_b=m_rg_conv_b, rg_wa=m_rg_wa, rg_ba=m_rg_ba, rg_wx=m_rg_wx, rg_bx=m_rg_bx, rg_lambda=m_rg_lambda,
                rg_w_out=m_rg_w_out, gla_w_in=m_gla_w_in, gla_w_alpha=m_gla_w_alpha, gla_b_alpha=m_gla_b_alpha,
                gla_norm_g=m_gla_norm_g, gla_w_out=m_gla_w_out)
    mom2 = dict(ada_w=v_ada_w, ada_b=v_ada_b, norm_g=v_norm_g, ffn_w_up=v_ffn_w_up, ffn_conv_w=v_ffn_conv_w,
                ffn_conv_b=v_ffn_conv_b, ffn_w_down=v_ffn_w_down, rg_w_in=v_rg_w_in, rg_conv_w=v_rg_conv_w,
                rg_conv_b=v_rg_conv_b, rg_wa=v_rg_wa, rg_ba=v_rg_ba, rg_wx=v_rg_wx, rg_bx=v_rg_bx, rg_lambda=v_rg_lambda,
                rg_w_out=v_rg_w_out, gla_w_in=v_gla_w_in, gla_w_alpha=v_gla_w_alpha, gla_b_alpha=v_gla_b_alpha,
                gla_norm_g=v_gla_norm_g, gla_w_out=v_gla_w_out)
    xi, yi, ci = _place()
    chip, me = 2 * xi + yi, 4 * xi + 2 * yi + ci
    d = x.shape[-1]
    depth = ada_w.shape[0]
    n_ada = ada_w.shape[-1]
    sharded_small = [k for k, ax in SMALL.items() if ax is not None]

    sm = _all_gather_8(_pack([c] + [wts[k] for k in sharded_small], SUBLANES, F32), "gather_small")
    c_all = sm[:, 0, :]
    parts = _unpack(sm[0::2], [c.shape] + [wts[k].shape for k in sharded_small])[1:]
    full = {k: _join_shards(p, SMALL[k]) for k, p in zip(sharded_small, parts)}
    for k, ax in SMALL.items():
        if ax is None:
            full[k] = wts[k]

    c16 = jnp.pad(c_all, ((0, ADA_ROWS - N_DEV), (0, 0)))
    ada_b_mine = lax.dynamic_slice_in_dim(ada_b, chip * n_ada, n_ada, 1)[:, None, :]
    mod_cols = _ada_fwd(c16, ada_w, ada_b_mine, "ada_fwd")
    mod_all = _all_gather_8(mod_cols.reshape(-1, PACK_COLS), "gather_mod")[0::2].reshape(N_CHIP, depth, ADA_ROWS, n_ada)
    mod = jnp.swapaxes(lax.dynamic_index_in_dim(mod_all, me, 2, keepdims=False), 0, 1).reshape(depth, 6, d)

    items = [(k, l) for k in BIG for l in range(wts[k].shape[0])]
    shards = [wts[k][l].reshape(-1, wts[k].shape[-1]).astype(BF16) for k, l in items]
    slots = [lax.dynamic_update_index_in_dim(got, sh, chip, 0) for got, sh in zip(_gather_chips(shards, "gather_weights"), shards)]
    rows_joined = lambda v: v.reshape(-1, v.shape[-1])
    local = {k: (v if k in ("norm_g", "ffn_conv_w", "ffn_conv_b") else v[0]) for k, v in full.items()}
    for k in BIG:
        got = [v for (kk, _), v in zip(items, slots) if kk == k]
        if k == "ffn_w_up":
            local[k] = got
        elif k == "ffn_w_down":
            local[k] = [rows_joined(v) for v in got]
        elif k in ("rg_wa", "rg_wx"):
            local[k] = _slots_to_block_rows(got[0], RG_BLOCKS)
        elif k == "gla_w_in":
            local[k] = _from_col_slots(got[0])
        else:
            local[k] = got[0] if BIG[k] else rows_joined(got[0])

    cols, grad_x, gr = _local_step(x[0], loss_target[0], mod, _gla_layouts(local))
    loss = lax.psum(0.5 * jnp.sum(cols) / d, ("x", "y", "c"))

    small_names = [k for k in SMALL if k != "ada_b"]
    gs = _all_gather_8(_pack([gr[k] for k in small_names] + [gr["mod"]], SUBLANES, F32), "gather_small_grads")
    small_shapes = [full[k].shape for k in small_names] + [(depth, 6 * d)]
    *small_sum, g_ada_b = _unpack(_sum_lead(gs, "sum_small_grads"), small_shapes)
    grads = dict(zip(small_names, small_sum))
    grads["ada_b"] = g_ada_b
    for k in sharded_small:
        grads[k] = _my_shard(grads[k], SMALL[k], chip)
    dmod_all = _unpack(gs, small_shapes)[-1].reshape(N_DEV, depth, N_CHIP, n_ada)
    dmod_mine = jnp.swapaxes(lax.dynamic_index_in_dim(dmod_all, chip, 2, keepdims=False), 0, 1)
    g_ada_w = _ada_bwd(c16, jnp.pad(dmod_mine, ((0, 0), (0, ADA_ROWS - N_DEV), (0, 0))), "ada_bwd")

    def grad_slots(k, l):
        if k in ("ffn_w_up", "ffn_w_down"):
            g = gr[k][l]
            return g if BIG[k] else g.reshape(N_CHIP, -1, g.shape[-1])
        if k in ("rg_wa", "rg_wx"):
            return _block_rows_to_slots(gr[k])
        if k == "gla_w_in":
            return _col_slots(gr[k])
        return gr[k] if BIG[k] else gr[k].reshape(N_CHIP, -1, gr[k].shape[-1])

    gslots = [grad_slots(k, l) for k, l in items]
    theirs = _pair_exchange(gslots, "grads_pair_exchange")
    c_idx = ci.reshape(1).astype(jnp.int32)
    psums = [_pair_sum(g, t, c_idx, f"grads_pair_sum_{k}{l}") for (k, l), g, t in zip(items, gslots, theirs)]
    arrived = _chip_exchange(psums, "grads_chip_exchange")
    arrived = [lax.dynamic_update_index_in_dim(a, lax.dynamic_index_in_dim(p, chip, 0, keepdims=False), chip, 0)
               for a, p in zip(arrived, psums)]
    halves = [_sum_lead(a, f"grads_chip_sum_{k}{l}") for (k, l), a in zip(items, arrived)]
    shared = _pair_share(halves, "grads_pair_share")
    reduced = [lax.dynamic_update_index_in_dim(s2, h, ci, 0).reshape(-1, h.shape[-1]) for s2, h in zip(shared, halves)]

    delta, new_m, new_v = {}, {}, {}

    def update(k, gs_k, view):
        shp = wts[k].shape
        outs = _adamw(view(wts[k]), gs_k, view(mom1[k]), view(mom2[k]), "adamw_" + k)
        grads[k], delta[k], new_m[k], new_v[k] = (o.reshape(shp) for o in outs)

    update("ada_w", [g_ada_w[l] for l in range(depth)], lambda a: a)
    for k in BIG:
        gs_k = [g for (kk, _), g in zip(items, reduced) if kk == k]
        update(k, gs_k, lambda a: a.reshape(a.shape[0], -1, a.shape[-1]))
    small_shard_shapes = [wts[k].shape for k in SMALL]
    packed = [_pack([src[k] for k in SMALL], SUBLANES, F32) for src in (wts, grads, mom1, mom2)]
    outs = _adamw(packed[0][None], [packed[1]], packed[2][None], packed[3][None], "adamw_small")
    for dst, o in zip((delta, new_m, new_v), outs[1:]):
        for k, a in zip(SMALL, _unpack(o[0], small_shard_shapes)):
            dst[k] = a

    return (loss, grad_x[None], *[grads[k] for k in WEIGHTS], *[delta[k] for k in WEIGHTS], *[new_m[k] for k in WEIGHTS],
            *[new_v[k] for k in WEIGHTS])
```

```python
import jax
import jax.numpy as jnp
from jax import lax
from jax.experimental import pallas as pl
from jax.experimental.pallas import tpu as pltpu

F32 = jnp.float32
BF16 = jnp.bfloat16
MXU_DTYPE = BF16

EPS = 1e-6
RG_C = 8.0
RG_BLOCKS = 4
RG_CONV = 4
GLA_HEADS = 4
GLA_TAU = 16.0
GLA_CHUNK = 64
GLA_RANK = 16
FFN_CONV = 3
ADAM_LR = 0.001
ADAM_B1 = 0.9
ADAM_B2 = 0.999
ADAM_EPS = 1e-08
ADAM_WD = 0.01
ADAM_STEP = 10

LANES = 128
SUBLANES = 8
VMEM_LIMIT = 56 * 1024 * 1024
CB = 256
MESH = pl.DeviceIdType.MESH
N_DEV = 8
N_CHIP = 4


def _params(*sem):
    return pltpu.CompilerParams(dimension_semantics=sem, vmem_limit_bytes=VMEM_LIMIT)


def _tile(dim, prefs):
    for p in prefs:
        if dim % p == 0:
            return p
    return dim


def _dot(a, b, dims):
    return lax.dot_general(a.astype(MXU_DTYPE), b.astype(MXU_DTYPE), (dims, ((), ())), preferred_element_type=F32)


def _dot_nn(a, b):
    return _dot(a, b, ((1,), (0,)))


def _dot_nt(a, b):
    return _dot(a, b, ((1,), (1,)))


def _dot_tn(a, b):
    return _dot(a, b, ((0,), (0,)))


def _mm(a, b, *, ta=False, tb=False, a_parts=1, b_parts=1, w_slots=1, out_slots=1, out_dtype=F32, name):
    if ta:
        k_dim, m_dim = a.shape
        n_dim = b.shape[-1] * b_parts
    else:
        m_dim, k_dim = a.shape[-2], a.shape[-1] * a_parts
        n_dim = b.shape[-2] if tb else b.shape[-1] * w_slots
    n_unit = n_dim // max(b_parts, out_slots, 1 if tb else w_slots)
    k_unit = k_dim // max(a_parts, w_slots if tb else 1)
    tm = _tile(m_dim, (1024, 1408, 512, 256, 128))
    tn = _tile(n_unit, (1024, 1408, 896, 512, 256, 128))
    tk = _tile(k_unit, (1024, 1408, 896, 512, 256, 128))
    nk = k_dim // tk
    dims = ((0 if ta else 1,), (1 if tb else 0,))

    def spec(shape, parts, total, tile, col_grid, row_grid):
        per = total // parts // tile

        def index(i, j, k):
            g = {"i": i, "j": j, "k": k}
            col, row = g[col_grid], g[row_grid]
            return (row, col) if parts == 1 else (col // per, row, col % per)

        return pl.BlockSpec(shape if parts == 1 else (None,) + shape, index)

    def body(a_ref, b_ref, o_ref, acc_ref):
        k = pl.program_id(2)

        @pl.when(k == 0)
        def _():
            acc_ref[...] = jnp.zeros_like(acc_ref)

        acc_ref[...] += _dot(a_ref[...], b_ref[...], dims)

        @pl.when(k == nk - 1)
        def _():
            o_ref[...] = acc_ref[...].astype(o_ref.dtype)

    if ta:
        a_spec = spec((tk, tm), 1, m_dim, tm, "i", "k")
        b_spec = spec((tk, tn), b_parts, n_dim, tn, "j", "k")
    elif tb:
        a_spec = spec((tm, tk), a_parts, k_dim, tk, "k", "i")
        b_spec = spec((tn, tk), w_slots, k_dim, tk, "k", "j")
    else:
        a_spec = spec((tm, tk), a_parts, k_dim, tk, "k", "i")
        b_spec = spec((tk, tn), w_slots, n_dim, tn, "j", "k")
    out_shape = (m_dim, n_dim) if out_slots == 1 else (out_slots, m_dim, n_dim // out_slots)
    return pl.pallas_call(
        body,
        grid=(m_dim // tm, n_dim // tn, nk),
        in_specs=[a_spec, b_spec],
        out_specs=spec((tm, tn), out_slots, n_dim, tn, "j", "i"),
        out_shape=jax.ShapeDtypeStruct(out_shape, out_dtype),
        scratch_shapes=[pltpu.VMEM((tm, tn), F32)],
        compiler_params=_params("parallel", "parallel", "arbitrary"),
        name=name,
    )(a, b)


def _row_specs(s, d, ts):
    return pl.BlockSpec((ts, d), lambda i: (i, 0)), pl.BlockSpec((1, d), lambda i: (0, 0))


def _norm_mod_fwd(x, g, sc, sh, name):
    s, d = x.shape
    ts = _tile(s, (512,))

    def body(x_ref, g_ref, sc_ref, sh_ref, h_ref):
        xv = x_ref[...]
        r = lax.rsqrt(jnp.mean(xv * xv, axis=-1, keepdims=True) + EPS)
        h_ref[...] = (((xv * r) * g_ref[...]) * (1.0 + sc_ref[...]) + sh_ref[...]).astype(h_ref.dtype)

    row, vec = _row_specs(s, d, ts)
    return pl.pallas_call(
        body, grid=(s // ts,), in_specs=[row, vec, vec, vec], out_specs=row,
        out_shape=jax.ShapeDtypeStruct((s, d), MXU_DTYPE), compiler_params=_params("parallel"), name=name,
    )(x, g, sc, sh)


def _norm_mod_bwd(dh, x, g, sc, dres, name):
    s, d = x.shape
    ts = _tile(s, (512,))

    def body(dh_ref, x_ref, g_ref, sc_ref, dres_ref, dx_ref, dg_ref, dsc_ref, dsh_ref, acc_ref):
        i = pl.program_id(0)

        @pl.when(i == 0)
        def _():
            acc_ref[...] = jnp.zeros_like(acc_ref)

        xv, dhv = x_ref[...], dh_ref[...]
        r = lax.rsqrt(jnp.mean(xv * xv, axis=-1, keepdims=True) + EPS)
        n = xv * r
        acc_ref[0:1, :] += jnp.sum(dhv * n, axis=0, keepdims=True)
        acc_ref[1:2, :] += jnp.sum(dhv, axis=0, keepdims=True)
        dn = dhv * ((1.0 + sc_ref[...]) * g_ref[...])
        dx_ref[...] = dres_ref[...] + r * (dn - n * jnp.mean(dn * n, axis=-1, keepdims=True))
        dg_ref[...] = (1.0 + sc_ref[...]) * acc_ref[0:1, :]
        dsc_ref[...] = g_ref[...] * acc_ref[0:1, :]
        dsh_ref[...] = acc_ref[1:2, :]

    row, vec = _row_specs(s, d, ts)
    vshape = jax.ShapeDtypeStruct((1, d), F32)
    return pl.pallas_call(
        body, grid=(s // ts,), in_specs=[row, row, vec, vec, row], out_specs=[row, vec, vec, vec],
        out_shape=[jax.ShapeDtypeStruct((s, d), F32), vshape, vshape, vshape],
        scratch_shapes=[pltpu.VMEM((SUBLANES, d), F32)], compiler_params=_params("arbitrary"), name=name,
    )(dh, x, g, sc, dres)


def _post_fwd(x, y, g, gt, name):
    s, d = x.shape
    ts = _tile(s, (512,))

    def body(x_ref, y_ref, g_ref, gt_ref, o_ref):
        yv = y_ref[...]
        r = lax.rsqrt(jnp.mean(yv * yv, axis=-1, keepdims=True) + EPS)
        o_ref[...] = x_ref[...] + gt_ref[...] * ((yv * r) * g_ref[...])

    row, vec = _row_specs(s, d, ts)
    return pl.pallas_call(
        body, grid=(s // ts,), in_specs=[row, row, vec, vec], out_specs=row,
        out_shape=jax.ShapeDtypeStruct((s, d), F32), compiler_params=_params("parallel"), name=name,
    )(x, y, g, gt)


def _post_bwd(dxn, y, g, gt, name):
    s, d = y.shape
    ts = _tile(s, (512,))

    def body(dxn_ref, y_ref, g_ref, gt_ref, dy_ref, dg_ref, dgt_ref, acc_ref):
        i = pl.program_id(0)

        @pl.when(i == 0)
        def _():
            acc_ref[...] = jnp.zeros_like(acc_ref)

        yv, dv = y_ref[...], dxn_ref[...]
        r = lax.rsqrt(jnp.mean(yv * yv, axis=-1, keepdims=True) + EPS)
        n = yv * r
        acc_ref[0:1, :] += jnp.sum(dv * n, axis=0, keepdims=True)
        dn = dv * (gt_ref[...] * g_ref[...])
        dy_ref[...] = (r * (dn - n * jnp.mean(dn * n, axis=-1, keepdims=True))).astype(dy_ref.dtype)
        dg_ref[...] = gt_ref[...] * acc_ref[0:1, :]
        dgt_ref[...] = g_ref[...] * acc_ref[0:1, :]

    row, vec = _row_specs(s, d, ts)
    vshape = jax.ShapeDtypeStruct((1, d), F32)
    return pl.pallas_call(
        body, grid=(s // ts,), in_specs=[row, row, vec, vec], out_specs=[row, vec, vec],
        out_shape=[jax.ShapeDtypeStruct((s, d), MXU_DTYPE), vshape, vshape],
        scratch_shapes=[pltpu.VMEM((SUBLANES, d), F32)], compiler_params=_params("arbitrary"), name=name,
    )(dxn, y, g, gt)


def _loss_grad(x, tgt, name):
    s, d = x.shape
    ts = _tile(s, (512,))

    def body(x_ref, t_ref, col_ref, dx_ref):
        i = pl.program_id(0)

        @pl.when(i == 0)
        def _():
            col_ref[...] = jnp.zeros_like(col_ref)

        e = x_ref[...] - t_ref[...]
        col_ref[...] += jnp.sum(e * e, axis=0, keepdims=True)
        dx_ref[...] = e * (1.0 / d)

    row, vec = _row_specs(s, d, ts)
    return pl.pallas_call(
        body, grid=(s // ts,), in_specs=[row, row], out_specs=[vec, row],
        out_shape=[jax.ShapeDtypeStruct((1, d), F32), jax.ShapeDtypeStruct((s, d), F32)],
        compiler_params=_params("arbitrary"), name=name,
    )(x, tgt)


_GELU_C = 0.7978845608028654
_GELU_A = 0.044715


def _gelu(x):
    t = jnp.tanh(_GELU_C * (x + _GELU_A * x * x * x))
    return 0.5 * x * (1.0 + t), t


def _gelu_grad(x, t):
    return 0.5 * (1.0 + t) + 0.5 * x * (1.0 - t * t) * (_GELU_C * (1.0 + 3.0 * _GELU_A * x * x))


def _sigmoid(x):
    return 1.0 / (1.0 + jnp.exp(-x))


def _log1p_pos(y):
    u = 1.0 + y
    return jnp.where(u == 1.0, y, jnp.log(u) * (y / jnp.where(u == 1.0, 1.0, u - 1.0)))


def _softplus(x):
    return jnp.maximum(x, 0.0) + _log1p_pos(jnp.exp(-jnp.abs(x)))


def _one_minus_exp(z):
    u = jnp.exp(z)
    lg = jnp.log(jnp.where(u > 0.0, u, 1.0))
    safe = (u != 1.0) & (u > 0.0)
    return jnp.where(u == 1.0, -z, jnp.where(u > 0.0, (1.0 - u) * (z / jnp.where(safe, lg, 1.0)), 1.0))


SLAB = 16


def _cat(a, b):
    return jnp.concatenate([a, b], axis=1)


def _fold8(x):
    out = x[0:SUBLANES]
    for r in range(SUBLANES, x.shape[0], SUBLANES):
        out = out + x[r:r + SUBLANES]
    return out


def _pair_specs(shape, nb, index):
    return [pl.BlockSpec(shape, lambda j, t: index(j, t) + (j,)), pl.BlockSpec(shape, lambda j, t: index(j, t) + (j + nb,))]


def _halo_row(ts, time_of):
    return lambda j, t: (jnp.maximum(time_of(t) * (ts // SUBLANES) - 1, 0),)


def _ffn_mid_fwd(p, cw, cb, name):
    s, f2 = p.shape
    ts = _tile(s, (512,))
    nb, nt = f2 // (2 * CB), s // ts

    def body(pg_ref, pv_ref, hg_ref, hv_ref, cwg_ref, cwv_ref, cbg_ref, cbv_ref, a_ref):
        t = pl.program_id(1)
        cwv, bias = _cat(cwg_ref[...], cwv_ref[...]), _cat(cbg_ref[...], cbv_ref[...])
        w0, w1, w2 = cwv[0:1], cwv[1:2], cwv[2:3]

        def slab(blk, r0):
            u = bias + w0 * blk[6:6 + SLAB] + w1 * blk[7:7 + SLAB] + w2 * blk[8:8 + SLAB]
            a_ref[pl.ds(r0, SLAB), :] = (_gelu(u[:, :CB])[0] * u[:, CB:]).astype(a_ref.dtype)

        halo = jnp.where(t > 0, _cat(hg_ref[...], hv_ref[...]), 0.0)
        slab(jnp.concatenate([halo, _cat(pg_ref[0:SLAB, :], pv_ref[0:SLAB, :])], axis=0), 0)

        def loop(i, carry):
            r0 = pl.multiple_of(i * SLAB, SLAB)
            rows = pl.ds(pl.multiple_of(r0 - SUBLANES, SUBLANES), SLAB + SUBLANES)
            slab(_cat(pg_ref[rows, :], pv_ref[rows, :]), r0)
            return carry

        lax.fori_loop(1, ts // SLAB, loop, 0, unroll=2)

    fwd = lambda t: t
    return pl.pallas_call(
        body, grid=(nb, nt),
        in_specs=(_pair_specs((ts, CB), nb, lambda j, t: (t,)) + _pair_specs((SUBLANES, CB), nb, _halo_row(ts, fwd))
                  + _pair_specs((FFN_CONV, CB), nb, lambda j, t: (0,)) + _pair_specs((1, CB), nb, lambda j, t: (0,))),
        out_specs=pl.BlockSpec((ts, CB), lambda j, t: (t, j)),
        out_shape=jax.ShapeDtypeStruct((s, f2 // 2), MXU_DTYPE),
        compiler_params=_params("parallel", "arbitrary"), name=name,
    )(p, p, p, p, cw, cw, cb, cb)


def _ffn_mid_bwd(da, p, cw, cb, name):
    s, f2 = p.shape
    ts = _tile(s, (512,))
    nb, nt = f2 // (2 * CB), s // ts
    n_slab = ts // SLAB

    def body(da_ref, pg_ref, pv_ref, hg_ref, hv_ref, cwg_ref, cwv_ref, cbg_ref, cbv_ref, dp_ref, dcw_ref, dcb_ref,
             next_du, acc):
        tt = pl.program_id(1)
        t = nt - 1 - tt
        cwv, bias = _cat(cwg_ref[...], cwv_ref[...]), _cat(cbg_ref[...], cbv_ref[...])
        w0, w1, w2 = cwv[0:1], cwv[1:2], cwv[2:3]

        @pl.when(tt == 0)
        def _():
            next_du[...] = jnp.zeros_like(next_du)
            acc[...] = jnp.zeros_like(acc)

        def slab(blk, r0, carry):
            pm2, pm1, p0 = blk[6:6 + SLAB], blk[7:7 + SLAB], blk[8:8 + SLAB]
            u = bias + w0 * pm2 + w1 * pm1 + w2 * p0
            g, v = u[:, :CB], u[:, CB:]
            gel, th = _gelu(g)
            dav = da_ref[pl.ds(r0, SLAB), :]
            du = _cat(dav * v * _gelu_grad(g, th), dav * gel)
            ext = jnp.concatenate([du, carry], axis=0)
            dpv = (w2 * du + w1 * ext[1:1 + SLAB] + w0 * ext[2:2 + SLAB]).astype(dp_ref.dtype)
            dp_ref[0, pl.ds(r0, SLAB), :] = dpv[:, :CB]
            dp_ref[1, pl.ds(r0, SLAB), :] = dpv[:, CB:]
            acc[0] += _fold8(du)
            acc[1] += _fold8(du * pm2)
            acc[2] += _fold8(du * pm1)
            acc[3] += _fold8(du * p0)
            return du[0:SUBLANES]

        def loop(k, carry):
            r0 = pl.multiple_of((n_slab - 1 - k) * SLAB, SLAB)
            rows = pl.ds(pl.multiple_of(r0 - SUBLANES, SUBLANES), SLAB + SUBLANES)
            return slab(_cat(pg_ref[rows, :], pv_ref[rows, :]), r0, carry)

        carry = lax.fori_loop(0, n_slab - 1, loop, next_du[...], unroll=2)
        halo = jnp.where(t > 0, _cat(hg_ref[...], hv_ref[...]), 0.0)
        next_du[...] = slab(jnp.concatenate([halo, _cat(pg_ref[0:SLAB, :], pv_ref[0:SLAB, :])], axis=0), 0, carry)

        @pl.when(tt == nt - 1)
        def _():
            for half in range(2):
                cols = slice(half * CB, (half + 1) * CB)
                dcb_ref[half] = jnp.sum(acc[0][:, cols], axis=0, keepdims=True)
                for k in range(FFN_CONV):
                    dcw_ref[half, k:k + 1, :] = jnp.sum(acc[1 + k][:, cols], axis=0, keepdims=True)

    rev = lambda t: nt - 1 - t
    return pl.pallas_call(
        body, grid=(nb, nt),
        in_specs=([pl.BlockSpec((ts, CB), lambda j, t: (rev(t), j))] + _pair_specs((ts, CB), nb, lambda j, t: (rev(t),))
                  + _pair_specs((SUBLANES, CB), nb, _halo_row(ts, rev)) + _pair_specs((FFN_CONV, CB), nb, lambda j, t: (0,))
                  + _pair_specs((1, CB), nb, lambda j, t: (0,))),
        out_specs=[pl.BlockSpec((2, ts, CB), lambda j, t: (0, rev(t), j)),
                   pl.BlockSpec((2, FFN_CONV, CB), lambda j, t: (0, 0, j)),
                   pl.BlockSpec((2, 1, CB), lambda j, t: (0, 0, j))],
        out_shape=[jax.ShapeDtypeStruct((2, s, f2 // 2), MXU_DTYPE), jax.ShapeDtypeStruct((2, FFN_CONV, f2 // 2), F32),
                   jax.ShapeDtypeStruct((2, 1, f2 // 2), F32)],
        scratch_shapes=[pltpu.VMEM((SUBLANES, 2 * CB), F32), pltpu.VMEM((1 + FFN_CONV, SUBLANES, 2 * CB), F32)],
        compiler_params=_params("parallel", "arbitrary"), name=name,
    )(da, p, p, p, p, cw, cw, cb, cb)


def _rg_gates(xc, wa_ref, ba_ref, wx_ref, bx_ref, lam_ref):
    r = _sigmoid(_dot_nn(xc, wa_ref[0]) + ba_ref[...])
    ig = _sigmoid(_dot_nn(xc, wx_ref[0]) + bx_ref[...])
    sp = _softplus(-lam_ref[...])
    log_a = (-RG_C) * r * sp
    a = jnp.exp(log_a)
    mult = jnp.sqrt(_one_minus_exp(2.0 * log_a))
    return r, ig, sp, a, mult


def _rg_conv(scr, cw_ref, cb_ref, ts):
    views = [scr[5 + k:5 + k + ts, :] for k in range(RG_CONV)]
    xc = cb_ref[...]
    for k in range(RG_CONV):
        xc = xc + cw_ref[k:k + 1, :] * views[k]
    return xc, views


def _rg_param_specs():
    vec = pl.BlockSpec((1, CB), lambda g, t: (0, g))
    mat = pl.BlockSpec((1, CB, CB), lambda g, t: (g, 0, 0))
    return [pl.BlockSpec((RG_CONV, CB), lambda g, t: (0, g)), vec, mat, vec, mat, vec, vec]


NSEG = SUBLANES
NQ = CB // LANES


def _lanes(q):
    return slice(q * LANES, (q + 1) * LANES)


def _seg_scan(a_scr, x_scr, loc_scr, dec_scr, ts, reverse):
    seg = ts // NSEG

    def step(k, carry):
        out = []
        rows = pl.ds(seg - 1 - k if reverse else k, NSEG, stride=seg)
        for q in range(NQ):
            st, dec = carry[q]
            a_q, x_q, loc_q, dec_q = a_scr.at[q], x_scr.at[q], loc_scr.at[q], dec_scr.at[q]
            av = a_q[rows, :]
            if reverse:
                loc_q[rows, :] = st
                dec_q[rows, :] = dec
                st = av * (x_q[rows, :] + st)
                dec = av * dec
            else:
                st = av * st + x_q[rows, :]
                dec = av * dec
                loc_q[rows, :] = st
                dec_q[rows, :] = dec
            out.append((st, dec))
        return tuple(out)

    init = tuple((jnp.zeros((NSEG, LANES), F32), jnp.ones((NSEG, LANES), F32)) for _ in range(NQ))
    return lax.fori_loop(0, seg, step, init, unroll=4)


def _seg_chain(fin, dec, c_in, reverse):
    rows = [None] * NSEG
    c = c_in
    for sgm in (reversed(range(NSEG)) if reverse else range(NSEG)):
        rows[sgm] = c
        c = fin[sgm:sgm + 1] + dec[sgm:sgm + 1] * c
    return jnp.concatenate(rows, axis=0), c


def _rg_mid_fwd(pj, cw, cb, wa, ba, wx, bx, lam, name):
    s = pj.shape[0]
    nb = pj.shape[1] // (2 * CB)
    ts = _tile(s, (512,))
    nt = s // ts
    seg = ts // NSEG

    def body(gate_ref, x_ref, halo_ref, cw_ref, cb_ref, wa_ref, ba_ref, wx_ref, bx_ref, lam_ref, y_ref, hs_ref,
             scr, a_scr, u_scr, loc_scr, dec_scr, h_scr):
        t = pl.program_id(1)

        @pl.when(t == 0)
        def _():
            h_scr[...] = jnp.zeros_like(h_scr)

        scr[0:SUBLANES, :] = jnp.where(t > 0, halo_ref[...], 0.0)
        scr[SUBLANES:, :] = x_ref[...]
        xc, _ = _rg_conv(scr, cw_ref, cb_ref, ts)
        _, ig, _, a, mult = _rg_gates(xc, wa_ref, ba_ref, wx_ref, bx_ref, lam_ref)
        u = mult * (ig * xc)
        for q in range(NQ):
            a_scr[q] = a[:, _lanes(q)]
            u_scr[q] = u[:, _lanes(q)]
        fin = _seg_scan(a_scr, u_scr, loc_scr, dec_scr, ts, False)
        for q in range(NQ):
            enter, leave = _seg_chain(fin[q][0], fin[q][1], h_scr[0:1, _lanes(q)], False)
            h_scr[0:1, _lanes(q)] = leave
            for sgm in range(NSEG):
                rows = slice(sgm * seg, (sgm + 1) * seg)
                hs_ref[rows, _lanes(q)] = loc_scr[q, rows, :] + dec_scr[q, rows, :] * enter[sgm:sgm + 1]
        y_ref[...] = (_gelu(gate_ref[...])[0] * hs_ref[...]).astype(y_ref.dtype)

    blk = pl.BlockSpec((ts, CB), lambda g, t: (t, g))
    lane_scr = pltpu.VMEM((NQ, ts, LANES), F32)
    return pl.pallas_call(
        body, grid=(nb, nt),
        in_specs=_pair_specs((ts, CB), nb, lambda g, t: (t,))
        + [pl.BlockSpec((SUBLANES, CB), lambda g, t: _halo_row(ts, lambda u: u)(g, t) + (g + nb,))] + _rg_param_specs(),
        out_specs=[blk, blk],
        out_shape=[jax.ShapeDtypeStruct((s, nb * CB), MXU_DTYPE), jax.ShapeDtypeStruct((s, nb * CB), F32)],
        scratch_shapes=[pltpu.VMEM((ts + SUBLANES, CB), F32), lane_scr, lane_scr, lane_scr, lane_scr,
                        pltpu.VMEM((SUBLANES, CB), F32)],
        compiler_params=_params("parallel", "arbitrary"), name=name,
    )(pj, pj, pj, cw, cb, wa, ba, wx, bx, lam)


def _rg_mid_bwd(dy, pj, hs, cw, cb, wa, ba, wx, bx, lam, name):
    s = pj.shape[0]
    nb = pj.shape[1] // (2 * CB)
    ts = _tile(s, (512,))
    nt = s // ts

    def body(dy_ref, gate_ref, x_ref, halo_ref, hs_ref, hsh_ref, cw_ref, cb_ref, wa_ref, ba_ref, wx_ref, bx_ref, lam_ref,
             dpj_ref, dcw_ref, dcb_ref, dwa_ref, dba_ref, dwx_ref, dbx_ref, dlam_ref,
             scr, hscr, a_scr, d_scr, loc_scr, dec_scr, g_scr, dxscr, c_scr):
        tt = pl.program_id(1)
        t = nt - 1 - tt
        seg = ts // NSEG

        @pl.when(tt == 0)
        def _():
            c_scr[...] = jnp.zeros_like(c_scr)
            dxscr[ts:, :] = jnp.zeros((SUBLANES, CB), F32)
            for ref in (dcw_ref, dcb_ref, dwa_ref, dba_ref, dwx_ref, dbx_ref, dlam_ref):
                ref[...] = jnp.zeros_like(ref)

        scr[0:SUBLANES, :] = jnp.where(t > 0, halo_ref[...], 0.0)
        scr[SUBLANES:, :] = x_ref[...]
        hscr[0:SUBLANES, :] = jnp.where(t > 0, hsh_ref[...], 0.0)
        hscr[SUBLANES:, :] = hs_ref[...]
        xc, views = _rg_conv(scr, cw_ref, cb_ref, ts)
        r, ig, sp, a, mult = _rg_gates(xc, wa_ref, ba_ref, wx_ref, bx_ref, lam_ref)
        gate = gate_ref[...]
        gel, th = _gelu(gate)
        dyv = dy_ref[...]
        dpj_ref[0] = (dyv * hs_ref[...] * _gelu_grad(gate, th)).astype(dpj_ref.dtype)
        dhs = dyv * gel
        for q in range(NQ):
            a_scr[q] = a[:, _lanes(q)]
            d_scr[q] = dhs[:, _lanes(q)]
        fin = _seg_scan(a_scr, d_scr, loc_scr, dec_scr, ts, True)
        for q in range(NQ):
            enter, leave = _seg_chain(fin[q][0], fin[q][1], c_scr[0:1, _lanes(q)], True)
            c_scr[0:1, _lanes(q)] = leave
            for sgm in range(NSEG):
                rows = slice(sgm * seg, (sgm + 1) * seg)
                g_scr[rows, _lanes(q)] = d_scr[q, rows, :] + loc_scr[q, rows, :] + dec_scr[q, rows, :] * enter[sgm:sgm + 1]
        du = g_scr[...]
        da = du * hscr[7:7 + ts, :]
        dmult = du * (ig * xc)
        dig = du * (mult * xc)
        dxc = du * (mult * ig)
        dlog_a = da * a - dmult * (a * a / mult)
        dlam_ref[...] += jnp.sum(dlog_a * r, axis=0, keepdims=True) * (RG_C * _sigmoid(-lam_ref[...]))
        dpr = dlog_a * ((-RG_C) * sp) * (r * (1.0 - r))
        dpi = dig * (ig * (1.0 - ig))
        dba_ref[...] += jnp.sum(dpr, axis=0, keepdims=True)
        dbx_ref[...] += jnp.sum(dpi, axis=0, keepdims=True)
        dwa_ref[0] += _dot_tn(xc, dpr)
        dwx_ref[0] += _dot_tn(xc, dpi)
        dxc = dxc + _dot_nt(dpr, wa_ref[0]) + _dot_nt(dpi, wx_ref[0])
        dcb_ref[...] += jnp.sum(dxc, axis=0, keepdims=True)
        for k in range(RG_CONV):
            dcw_ref[k:k + 1, :] += jnp.sum(dxc * views[k], axis=0, keepdims=True)
        dxscr[0:ts, :] = dxc
        dxp = cw_ref[3:4, :] * dxc
        for k in range(RG_CONV - 1):
            dxp = dxp + cw_ref[k:k + 1, :] * dxscr[3 - k:3 - k + ts, :]
        dpj_ref[1] = dxp.astype(dpj_ref.dtype)
        dxscr[ts:, :] = dxscr[0:SUBLANES, :]

    rev = lambda g, t: (nt - 1 - t, g)
    rev_halo = lambda g, t: (jnp.maximum((nt - 1 - t) * (ts // SUBLANES) - 1, 0), g)
    vec = pl.BlockSpec((1, CB), lambda g, t: (0, g))
    mat = pl.BlockSpec((1, CB, CB), lambda g, t: (g, 0, 0))
    d = nb * CB
    vshape = jax.ShapeDtypeStruct((1, d), F32)
    mshape = jax.ShapeDtypeStruct((nb, CB, CB), F32)
    return pl.pallas_call(
        body, grid=(nb, nt),
        in_specs=[pl.BlockSpec((ts, CB), rev)] + _pair_specs((ts, CB), nb, lambda g, t: (nt - 1 - t,))
        + [pl.BlockSpec((SUBLANES, CB), lambda g, t: (rev_halo(g, t)[0], g + nb)),
           pl.BlockSpec((ts, CB), rev), pl.BlockSpec((SUBLANES, CB), rev_halo)] + _rg_param_specs(),
        out_specs=[pl.BlockSpec((2, ts, CB), lambda g, t: (0, nt - 1 - t, g)), pl.BlockSpec((RG_CONV, CB), lambda g, t: (0, g)),
                   vec, mat, vec, mat, vec, vec],
        out_shape=[jax.ShapeDtypeStruct((2, s, d), MXU_DTYPE), jax.ShapeDtypeStruct((RG_CONV, d), F32), vshape, mshape, vshape,
                   mshape, vshape, vshape],
        scratch_shapes=[pltpu.VMEM((ts + SUBLANES, CB), F32), pltpu.VMEM((ts + SUBLANES, CB), F32)]
        + [pltpu.VMEM((NQ, ts, LANES), F32)] * 4
        + [pltpu.VMEM((ts, CB), F32), pltpu.VMEM((ts + SUBLANES, CB), F32), pltpu.VMEM((SUBLANES, CB), F32)],
        compiler_params=_params("parallel", "arbitrary"), name=name,
    )(dy, pj, pj, pj, hs, hs, cw, cb, wa, ba, wx, bx, lam)


GLA_DK = 128
GLA_DV = 256
GLA_HB = 2 * GLA_DK + 2 * GLA_DV + LANES
GLA_TS = 256


def _split3(x):
    hi = x.astype(BF16)
    r1 = x - hi.astype(F32)
    mid = r1.astype(BF16)
    lo = (r1 - mid.astype(F32)).astype(BF16)
    return hi, mid, lo


def _chunk_cumsum(x, reverse):
    n = x.shape[0]
    i = lax.broadcasted_iota(jnp.int32, (n, n), 0)
    j = lax.broadcasted_iota(jnp.int32, (n, n), 1)
    same = (i // GLA_CHUNK) == (j // GLA_CHUNK)
    tri = jnp.where(same & ((j >= i) if reverse else (j <= i)), 1.0, 0.0).astype(BF16)
    out = jnp.zeros(x.shape, F32)
    for piece in _split3(x):
        out = out + lax.dot_general(tri, piece, (((1,), (0,)), ((), ())), preferred_element_type=F32)
    return out


def _gla_split(blk):
    q = blk[:, 0:GLA_DK] * (GLA_DK ** -0.5)
    k = blk[:, GLA_DK:2 * GLA_DK]
    v = blk[:, 2 * GLA_DK:2 * GLA_DK + GLA_DV]
    r = blk[:, 2 * GLA_DK + GLA_DV:2 * GLA_DK + 2 * GLA_DV]
    z = blk[:, 2 * GLA_DK + 2 * GLA_DV:]
    return q, k, v, r, z


def _gla_decays(gc):
    gref = gc[GLA_CHUNK // 2:GLA_CHUNK // 2 + 1, :]
    glast = gc[GLA_CHUNK - 1:GLA_CHUNK, :]
    return jnp.exp(gc), jnp.exp(gc - gref), jnp.exp(gref - gc), jnp.exp(glast - gc), jnp.exp(glast)


def _causal_mask():
    i = lax.broadcasted_iota(jnp.int32, (GLA_CHUNK, GLA_CHUNK), 0)
    j = lax.broadcasted_iota(jnp.int32, (GLA_CHUNK, GLA_CHUNK), 1)
    return j <= i


def _log_sigmoid(x):
    return jnp.minimum(x, 0.0) - _log1p_pos(jnp.exp(-jnp.abs(x)))


def _gla_mid_fwd(pj, wal, bal, ng, name):
    s = pj.shape[0]
    nh = pj.shape[1] // GLA_HB
    ts = _tile(s, (GLA_TS,))
    nt, nc = s // ts, ts // GLA_CHUNK

    def body(pj_ref, wal_ref, bal_ref, ng_ref, act_ref, o_ref, st_ref, s_scr):
        t = pl.program_id(1)

        @pl.when(t == 0)
        def _():
            s_scr[...] = jnp.zeros_like(s_scr)

        q, k, v, r, z = _gla_split(pj_ref[...])
        g = _log_sigmoid(_dot_nn(z, wal_ref[0]) + bal_ref[0]) * (1.0 / GLA_TAU)
        gcum = _chunk_cumsum(g, False)
        mask = _causal_mask()
        for c in range(nc):
            sl = slice(c * GLA_CHUNK, (c + 1) * GLA_CHUNK)
            eg, eq, ek, ekd, egl = _gla_decays(gcum[sl])
            st = s_scr[...]
            st_ref[c, 0] = st
            attn = jnp.where(mask, _dot_nt(q[sl] * eq, k[sl] * ek), 0.0)
            o_ref[sl, :] = _dot_nt(q[sl] * eg, st) + _dot_nn(attn, v[sl])
            s_scr[...] = st * egl + _dot_tn(v[sl], k[sl] * ekd)
        o = o_ref[...]
        on = o * lax.rsqrt(jnp.mean(o * o, axis=-1, keepdims=True) + EPS)
        act_ref[...] = ((on * ng_ref[...]) * (r * _sigmoid(r))).astype(act_ref.dtype)

    blk = pl.BlockSpec((ts, GLA_DV), lambda h, t: (t, h))
    return pl.pallas_call(
        body, grid=(nh, nt),
        in_specs=[pl.BlockSpec((ts, GLA_HB), lambda h, t: (t, h)),
                  pl.BlockSpec((1, LANES, GLA_DK), lambda h, t: (h, 0, 0)),
                  pl.BlockSpec((1, 1, GLA_DK), lambda h, t: (h, 0, 0)),
                  pl.BlockSpec((1, GLA_DV), lambda h, t: (0, 0))],
        out_specs=[blk, blk, pl.BlockSpec((nc, 1, GLA_DV, GLA_DK), lambda h, t: (t, h, 0, 0))],
        out_shape=[jax.ShapeDtypeStruct((s, nh * GLA_DV), MXU_DTYPE), jax.ShapeDtypeStruct((s, nh * GLA_DV), F32),
                   jax.ShapeDtypeStruct((s // GLA_CHUNK, nh, GLA_DV, GLA_DK), F32)],
        scratch_shapes=[pltpu.VMEM((GLA_DV, GLA_DK), F32)],
        compiler_params=_params("parallel", "arbitrary"), name=name,
    )(pj, wal, bal, ng)


def _gla_mid_bwd(dact, pj, o, st, wal, bal, ng, name):
    s = pj.shape[0]
    nh = pj.shape[1] // GLA_HB
    ts = _tile(s, (GLA_TS,))
    nt, nc = s // ts, ts // GLA_CHUNK

    def body(dact_ref, pj_ref, o_ref, st_ref, wal_ref, bal_ref, ng_ref, dpj_ref, dwal_ref, dbal_ref, dng_ref,
             ds_scr, dg_scr):
        h, tt = pl.program_id(0), pl.program_id(1)

        @pl.when(tt == 0)
        def _():
            ds_scr[...] = jnp.zeros_like(ds_scr)
            dwal_ref[...] = jnp.zeros_like(dwal_ref)
            dbal_ref[...] = jnp.zeros_like(dbal_ref)

        @pl.when((tt == 0) & (h == 0))
        def _():
            dng_ref[...] = jnp.zeros_like(dng_ref)

        q, k, v, r, z = _gla_split(pj_ref[...])
        logit = _dot_nn(z, wal_ref[0]) + bal_ref[0]
        gcum = _chunk_cumsum(_log_sigmoid(logit) * (1.0 / GLA_TAU), False)
        ov = o_ref[...]
        ro = lax.rsqrt(jnp.mean(ov * ov, axis=-1, keepdims=True) + EPS)
        on = ov * ro
        sg = _sigmoid(r)
        sil = r * sg
        dav = dact_ref[...]
        dpj_ref[:, 2 * GLA_DK + GLA_DV:2 * GLA_DK + 2 * GLA_DV] = (
            dav * (on * ng_ref[...]) * (sg + sil * (1.0 - sg))).astype(dpj_ref.dtype)
        t1 = dav * sil
        dng_ref[...] += jnp.sum(t1 * on, axis=0, keepdims=True)
        dn = t1 * ng_ref[...]
        do = ro * (dn - on * jnp.mean(dn * on, axis=-1, keepdims=True))
        mask = _causal_mask()
        scale = GLA_DK ** -0.5
        for c in reversed(range(nc)):
            sl = slice(c * GLA_CHUNK, (c + 1) * GLA_CHUNK)
            eg, eq, ek, ekd, egl = _gla_decays(gcum[sl])
            qc, kc, vc, doc = q[sl], k[sl], v[sl], do[sl]
            qg, qt, kt, kd = qc * eg, qc * eq, kc * ek, kc * ekd
            sp = st_ref[c, 0]
            ds = ds_scr[...]
            attn = jnp.where(mask, _dot_nt(qt, kt), 0.0)
            dattn = jnp.where(mask, _dot_nt(doc, vc), 0.0)
            dqg = _dot_nn(doc, sp)
            dqt = _dot_nn(dattn, kt)
            dkt = _dot_tn(dattn, qt)
            dkd = _dot_nn(vc, ds)
            dpj_ref[sl, 2 * GLA_DK:2 * GLA_DK + GLA_DV] = (_dot_tn(attn, doc) + _dot_nt(kd, ds)).astype(dpj_ref.dtype)
            dpj_ref[sl, 0:GLA_DK] = (scale * (dqg * eg + dqt * eq)).astype(dpj_ref.dtype)
            dpj_ref[sl, GLA_DK:2 * GLA_DK] = (dkt * ek + dkd * ekd).astype(dpj_ref.dtype)
            kdd = dkd * kd
            dgl = jnp.sum(kdd, axis=0, keepdims=True) + jnp.sum(ds * sp, axis=0, keepdims=True) * egl
            row = lax.broadcasted_iota(jnp.int32, (GLA_CHUNK, GLA_DK), 0)
            dg_scr[sl, :] = dqg * qg + dqt * qt - dkt * kt - kdd + jnp.where(row == GLA_CHUNK - 1, dgl, 0.0)
            ds_scr[...] = ds * egl + _dot_tn(doc, qg)
        dlogit = _chunk_cumsum(dg_scr[...], True) * (1.0 / GLA_TAU) * _sigmoid(-logit)
        dpj_ref[:, 2 * GLA_DK + 2 * GLA_DV:] = _dot_nt(dlogit, wal_ref[0]).astype(dpj_ref.dtype)
        dwal_ref[0] += _dot_tn(z, dlogit)
        dbal_ref[0] += jnp.sum(dlogit, axis=0, keepdims=True)

    rev = lambda h, t: (nt - 1 - t, h)
    return pl.pallas_call(
        body, grid=(nh, nt),
        in_specs=[pl.BlockSpec((ts, GLA_DV), rev), pl.BlockSpec((ts, GLA_HB), rev), pl.BlockSpec((ts, GLA_DV), rev),
                  pl.BlockSpec((nc, 1, GLA_DV, GLA_DK), lambda h, t: (nt - 1 - t, h, 0, 0)),
                  pl.BlockSpec((1, LANES, GLA_DK), lambda h, t: (h, 0, 0)),
                  pl.BlockSpec((1, 1, GLA_DK), lambda h, t: (h, 0, 0)),
                  pl.BlockSpec((1, GLA_DV), lambda h, t: (0, 0))],
        out_specs=[pl.BlockSpec((ts, GLA_HB), rev),
                   pl.BlockSpec((1, LANES, GLA_DK), lambda h, t: (h, 0, 0)),
                   pl.BlockSpec((1, 1, GLA_DK), lambda h, t: (h, 0, 0)),
                   pl.BlockSpec((1, GLA_DV), lambda h, t: (0, 0))],
        out_shape=[jax.ShapeDtypeStruct((s, nh * GLA_HB), MXU_DTYPE), jax.ShapeDtypeStruct((nh, LANES, GLA_DK), F32),
                   jax.ShapeDtypeStruct((nh, 1, GLA_DK), F32), jax.ShapeDtypeStruct((1, GLA_DV), F32)],
        scratch_shapes=[pltpu.VMEM((GLA_DV, GLA_DK), F32), pltpu.VMEM((ts, GLA_DK), F32)],
        compiler_params=_params("arbitrary", "arbitrary"), name=name,
    )(dact, pj, o, st, wal, bal, ng)


def _adamw(w, gs, m, v, name):
    layers, rows, cols = w.shape
    tr = _tile(rows, (256, 128, 64, 32, 16, 8))
    c1 = 1.0 / (1.0 - ADAM_B1 ** ADAM_STEP)
    c2 = 1.0 / (1.0 - ADAM_B2 ** ADAM_STEP)

    def body(*refs):
        g_refs, (w_ref, m_ref, v_ref, go_ref, d_ref, mo_ref, vo_ref) = refs[:layers], refs[layers:]
        gv = g_refs[0][...]
        for l in range(1, layers):
            gv = jnp.where(pl.program_id(0) == l, g_refs[l][...], gv)
        m2 = ADAM_B1 * m_ref[...] + (1.0 - ADAM_B1) * gv
        v2 = ADAM_B2 * v_ref[...] + (1.0 - ADAM_B2) * (gv * gv)
        d_ref[...] = (-ADAM_LR) * ((m2 * c1) / (jnp.sqrt(v2 * c2) + ADAM_EPS) + ADAM_WD * w_ref[...])
        go_ref[...] = gv
        mo_ref[...] = m2
        vo_ref[...] = v2

    g_spec = pl.BlockSpec((tr, cols), lambda l, i: (i, 0))
    spec = pl.BlockSpec((None, tr, cols), lambda l, i: (l, i, 0))
    shape = jax.ShapeDtypeStruct((layers, rows, cols), F32)
    return pl.pallas_call(
        body, grid=(layers, rows // tr), in_specs=[g_spec] * layers + [spec] * 3, out_specs=[spec] * 4, out_shape=[shape] * 4,
        compiler_params=_params("parallel", "parallel"), name=name,
    )(*gs, w, m, v)


def _gla_head_cols(w):
    d = w.shape[0]
    qk, dv = GLA_HEADS * GLA_DK, GLA_HEADS * GLA_DV
    q, k, v, r, z = jnp.split(w, [qk, 2 * qk, 2 * qk + dv, 2 * qk + 2 * dv], axis=1)
    zp = jnp.pad(z, ((0, 0), (0, LANES - GLA_RANK)))
    parts = [q.reshape(d, GLA_HEADS, GLA_DK), k.reshape(d, GLA_HEADS, GLA_DK), v.reshape(d, GLA_HEADS, GLA_DV),
             r.reshape(d, GLA_HEADS, GLA_DV), jnp.broadcast_to(zp[:, None, :], (d, GLA_HEADS, LANES))]
    return jnp.concatenate(parts, axis=2).reshape(d, GLA_HEADS * GLA_HB)


def _gla_unhead_cols(w):
    d = w.shape[0]
    w = w.reshape(d, GLA_HEADS, GLA_HB)
    o = 2 * GLA_DK + 2 * GLA_DV
    parts = [w[:, :, 0:GLA_DK].reshape(d, -1), w[:, :, GLA_DK:2 * GLA_DK].reshape(d, -1),
             w[:, :, 2 * GLA_DK:2 * GLA_DK + GLA_DV].reshape(d, -1), w[:, :, 2 * GLA_DK + GLA_DV:o].reshape(d, -1),
             jnp.sum(w[:, :, o:o + GLA_RANK], axis=1)]
    return jnp.concatenate(parts, axis=1)


def _gla_alpha_heads(w_alpha, b_alpha):
    wal = jnp.swapaxes(w_alpha.reshape(GLA_RANK, GLA_HEADS, GLA_DK), 0, 1)
    return jnp.pad(wal, ((0, 0), (0, LANES - GLA_RANK), (0, 0))), b_alpha.reshape(GLA_HEADS, 1, GLA_DK)


def _gla_layouts(w):
    w = dict(w)
    w["gla_wal"], w["gla_bal"] = _gla_alpha_heads(w["gla_w_alpha"], w["gla_b_alpha"])
    w["gla_w_in"] = _gla_head_cols(w["gla_w_in"])
    return w


def _col_slots(w):
    r, c = w.shape
    return jnp.moveaxis(w.reshape(r, N_CHIP, c // N_CHIP), 1, 0)


def _from_col_slots(w):
    n, r, c = w.shape
    return jnp.moveaxis(w, 0, 1).reshape(r, n * c)


def _block_rows_to_slots(w):
    g, r4, cc = w.shape
    return jnp.swapaxes(w.reshape(g, N_CHIP, r4 // N_CHIP, cc), 0, 1).reshape(N_CHIP, g * (r4 // N_CHIP), cc)


def _slots_to_block_rows(w, g):
    n, gr, cc = w.shape
    return jnp.swapaxes(w.reshape(n, g, gr // g, cc), 0, 1).reshape(g, n * (gr // g), cc)


def _local_step(x, tgt, mod, w, fetch=None, done=None):
    depth = mod.shape[0]
    row = lambda v: v.reshape(1, -1)
    w = dict(w)
    w["ffn_w_up"], w["ffn_w_down"] = dict(enumerate(w["ffn_w_up"])), dict(enumerate(w["ffn_w_down"]))

    def arrive(stage, after):
        if fetch is not None:
            for k, v in fetch(stage, after).items():
                if isinstance(v, dict):
                    w[k].update(v)
                else:
                    w[k] = v

    saved = []
    for i in range(depth):
        if i == 1:
            arrive("l1", x)
        sh_m, sc_m, gt_m, sh_f, sc_f, gt_f = (mod[i, j:j + 1] for j in range(6))
        g0, g1, g2, g3 = (w["norm_g"][i, j:j + 1] for j in range(4))
        tag = f"_l{i}"
        h = _norm_mod_fwd(x, g0, sc_m, sh_m, "norm_mix" + tag)
        if i % 2 == 0:
            pj = _mm(h, w["rg_w_in"], w_slots=N_CHIP, name="rg_in" + tag)
            act, aux = _rg_mid_fwd(pj, w["rg_conv_w"], row(w["rg_conv_b"]), w["rg_wa"], row(w["rg_ba"]), w["rg_wx"],
                                   row(w["rg_bx"]), row(w["rg_lambda"]), "rg_mid" + tag)
            y = _mm(act, w["rg_w_out"], name="rg_out" + tag)
        else:
            pj = _mm(h, w["gla_w_in"], name="gla_in" + tag)
            act, *aux = _gla_mid_fwd(pj, w["gla_wal"], w["gla_bal"], row(w["gla_norm_g"]), "gla_mid" + tag)
            y = _mm(act, w["gla_w_out"], name="gla_out" + tag)
        x1 = _post_fwd(x, y, g1, gt_m, "post_mix" + tag)
        if i == 0:
            arrive("ffn0", x1)
        h2 = _norm_mod_fwd(x1, g2, sc_f, sh_f, "norm_ffn" + tag)
        p = _mm(h2, w["ffn_w_up"][i], w_slots=N_CHIP, name="ffn_up" + tag)
        a = _ffn_mid_fwd(p, w["ffn_conv_w"][i], w["ffn_conv_b"][i:i + 1], "ffn_mid" + tag)
        y2 = _mm(a, w["ffn_w_down"][i], name="ffn_down" + tag)
        x2 = _post_fwd(x1, y2, g3, gt_f, "post_ffn" + tag)
        saved.append((x, h, pj, act, aux, y, x1, h2, p, a, y2))
        x = x2

    cols, dx = _loss_grad(x, tgt, "loss")

    stacked = ("norm_g", "ffn_conv_w", "ffn_conv_b", "mod")
    gr = {k: [None] * depth for k in stacked + ("ffn_w_up", "ffn_w_down")}
    told = lambda stage: done(stage, gr) if done is not None else 0.0
    for i in reversed(range(depth)):
        x0, h, pj, act, aux, y, x1, h2, p, a, y2 = saved[i]
        sh_m, sc_m, gt_m, sh_f, sc_f, gt_f = (mod[i, j:j + 1] for j in range(6))
        g0, g1, g2, g3 = (w["norm_g"][i, j:j + 1] for j in range(4))
        tag = f"_l{i}"
        dy2, d_g3, d_gt_f = _post_bwd(dx, y2, g3, gt_f, "post_ffn_b" + tag)
        da = _mm(dy2, w["ffn_w_down"][i], tb=True, name="ffn_down_dx" + tag)
        gr["ffn_w_down"][i] = _mm(a, dy2, ta=True, name="ffn_down_dw" + tag)
        dp, dcw, dcb = _ffn_mid_bwd(da, p, w["ffn_conv_w"][i], w["ffn_conv_b"][i:i + 1], "ffn_mid_b" + tag)
        gr["ffn_conv_w"][i], gr["ffn_conv_b"][i] = _cat(dcw[0], dcw[1]), _cat(dcb[0], dcb[1])[0]
        dh2 = _mm(dp, w["ffn_w_up"][i], tb=True, a_parts=2, w_slots=N_CHIP, name="ffn_up_dx" + tag)
        gr["ffn_w_up"][i] = _mm(h2, dp, ta=True, b_parts=2, out_slots=N_CHIP, name="ffn_up_dw" + tag)
        dx1, d_g2, d_sc_f, d_sh_f = _norm_mod_bwd(dh2, x1, g2, sc_f, dx, "norm_ffn_b" + tag)
        if i == 0:
            gt_m = gt_m + told("ffn0")
        dy, d_g1, d_gt_m = _post_bwd(dx1, y, g1, gt_m, "post_mix_b" + tag)
        if i % 2 == 0:
            dact = _mm(dy, w["rg_w_out"], tb=True, name="rg_out_dx" + tag)
            gr["rg_w_out"] = _mm(act, dy, ta=True, name="rg_out_dw" + tag)
            dpj, gr["rg_conv_w"], d_cb, gr["rg_wa"], d_ba, gr["rg_wx"], d_bx, d_lam = _rg_mid_bwd(
                dact, pj, aux, w["rg_conv_w"], row(w["rg_conv_b"]), w["rg_wa"], row(w["rg_ba"]), w["rg_wx"],
                row(w["rg_bx"]), row(w["rg_lambda"]), "rg_mid_b" + tag)
            gr["rg_conv_b"], gr["rg_ba"], gr["rg_bx"], gr["rg_lambda"] = d_cb[0], d_ba[0], d_bx[0], d_lam[0]
            dh = _mm(dpj, w["rg_w_in"], tb=True, a_parts=2, w_slots=N_CHIP, name="rg_in_dx" + tag)
            gr["rg_w_in"] = _mm(h, dpj, ta=True, b_parts=2, out_slots=N_CHIP, name="rg_in_dw" + tag)
        else:
            dact = _mm(dy, w["gla_w_out"], tb=True, name="gla_out_dx" + tag)
            gr["gla_w_out"] = _mm(act, dy, ta=True, name="gla_out_dw" + tag)
            dpj, d_wal, d_bal, d_ng = _gla_mid_bwd(dact, pj, aux[0], aux[1], w["gla_wal"], w["gla_bal"],
                                                   row(w["gla_norm_g"]), "gla_mid_b" + tag)
            gr["gla_w_alpha"] = jnp.swapaxes(d_wal[:, :GLA_RANK, :], 0, 1).reshape(GLA_RANK, GLA_HEADS * GLA_DK)
            gr["gla_b_alpha"], gr["gla_norm_g"] = d_bal.reshape(-1), d_ng[0]
            dh = _mm(dpj, w["gla_w_in"], tb=True, name="gla_in_dx" + tag)
            gr["gla_w_in"] = _gla_unhead_cols(_mm(h, dpj, ta=True, name="gla_in_dw" + tag))
            mod = mod.at[0].add(told("l1"))
        dx, d_g0, d_sc_m, d_sh_m = _norm_mod_bwd(dh, x0, g0, sc_m, dx1, "norm_mix_b" + tag)
        gr["norm_g"][i] = jnp.concatenate([d_g0, d_g1, d_g2, d_g3], axis=0)
        gr["mod"][i] = jnp.concatenate([d_sh_m, d_sc_m, d_gt_m, d_sh_f, d_sc_f, d_gt_f], axis=0)
    for k in stacked:
        gr[k] = jnp.stack(gr[k])
    return cols, dx, gr


ADA_ROWS = 16


def _ada_fwd(c16, ada_w, ada_b, name):
    depth, d, n = ada_w.shape
    tn = _tile(n, (512, 256, 128))

    def body(c_ref, w_ref, b_ref, o_ref):
        cv = c_ref[...]
        o_ref[0] = _dot_nn(cv * _sigmoid(cv), w_ref[0]) + b_ref[0]

    return pl.pallas_call(
        body, grid=(depth, n // tn),
        in_specs=[pl.BlockSpec((ADA_ROWS, d), lambda l, j: (0, 0)), pl.BlockSpec((1, d, tn), lambda l, j: (l, 0, j)),
                  pl.BlockSpec((1, 1, tn), lambda l, j: (l, 0, j))],
        out_specs=pl.BlockSpec((1, ADA_ROWS, tn), lambda l, j: (l, 0, j)),
        out_shape=jax.ShapeDtypeStruct((depth, ADA_ROWS, n), F32),
        compiler_params=_params("parallel", "parallel"), name=name,
    )(c16, ada_w, ada_b)


def _ada_bwd(c16, dmod16, name):
    depth, _, n = dmod16.shape
    d = c16.shape[1]
    tn = _tile(n, (512, 256, 128))

    def body(c_ref, dm_ref, o_ref):
        cv = c_ref[...]
        o_ref[0] = _dot_tn(cv * _sigmoid(cv), dm_ref[0])

    return pl.pallas_call(
        body, grid=(depth, n // tn),
        in_specs=[pl.BlockSpec((ADA_ROWS, d), lambda l, j: (0, 0)), pl.BlockSpec((1, ADA_ROWS, tn), lambda l, j: (l, 0, j))],
        out_specs=pl.BlockSpec((1, d, tn), lambda l, j: (l, 0, j)),
        out_shape=jax.ShapeDtypeStruct((depth, d, n), F32),
        compiler_params=_params("parallel", "parallel"), name=name,
    )(c16, dmod16)


PACK_COLS = 1024
_ANY = pl.BlockSpec(memory_space=pl.ANY)
_VMEM = pl.BlockSpec(memory_space=pltpu.VMEM)


def _place():
    return lax.axis_index("x"), lax.axis_index("y"), lax.axis_index("c")


def _other_chips(x, y):
    return [(1 - x, y), (x, 1 - y), (1 - x, 1 - y)]


def _rcopy(src, dst, send_sems, recv_sems, k, peer):
    return pltpu.make_async_remote_copy(src_ref=src, dst_ref=dst, send_sem=send_sems.at[k], recv_sem=recv_sems.at[k],
                                        device_id=peer, device_id_type=MESH)


def _all_gather_8(v, name):
    r, cc = v.shape

    def body(v_ref, out_ref, send_sems, recv_sems, local_sem):
        x, y, c = _place()
        me = 4 * x + 2 * y + c
        mine = pltpu.make_async_copy(v_ref, out_ref.at[me], local_sem)
        mine.start()
        peers = []
        for k in range(1, N_DEV):
            px = 1 - x if k & 4 else x
            py = 1 - y if k & 2 else y
            pc = 1 - c if k & 1 else c
            peers.append((px, py, pc))
        sends = [_rcopy(v_ref, out_ref.at[me], send_sems, recv_sems, k, p) for k, p in enumerate(peers)]
        for cp in sends:
            cp.start()
        for k, (px, py, pc) in enumerate(peers):
            _rcopy(v_ref, out_ref.at[4 * px + 2 * py + pc], send_sems, recv_sems, k, (px, py, pc)).wait_recv()
        for cp in sends:
            cp.wait_send()
        mine.wait()

    return pl.pallas_call(
        body, in_specs=[_VMEM], out_specs=_VMEM, out_shape=jax.ShapeDtypeStruct((N_DEV, r, cc), v.dtype),
        scratch_shapes=[pltpu.SemaphoreType.DMA((N_DEV - 1,)), pltpu.SemaphoreType.DMA((N_DEV - 1,)), pltpu.SemaphoreType.DMA],
        compiler_params=pltpu.CompilerParams(vmem_limit_bytes=VMEM_LIMIT), name=name,
    )(v)


def _gather_chips(shards, name):
    n = len(shards)
    per = 2 * (N_CHIP - 1)

    def body(*refs):
        ins, outs, (send_sems, recv_sems) = refs[:n], refs[n:2 * n], refs[2 * n:]
        x, y, c = _place()
        chip = 2 * x + y
        chips = _other_chips(x, y)
        rows = [(pl.ds(c * (r.shape[0] // 2), r.shape[0] // 2), pl.ds((1 - c) * (r.shape[0] // 2), r.shape[0] // 2)) for r in ins]
        first = [_rcopy(ins[i].at[rows[i][0]], outs[i].at[chip, rows[i][0]], send_sems, recv_sems, per * i + j, (px, py, c))
                 for i in range(n) for j, (px, py) in enumerate(chips)]
        for cp in first:
            cp.start()
        passed = []
        for i in range(n):
            for j, (px, py) in enumerate(chips):
                landed = outs[i].at[2 * px + py, rows[i][0]]
                _rcopy(ins[i].at[rows[i][0]], landed, send_sems, recv_sems, per * i + j, (px, py, c)).wait_recv()
                fw = _rcopy(landed, landed, send_sems, recv_sems, per * i + N_CHIP - 1 + j, (x, y, 1 - c))
                fw.start()
                passed.append(fw)
        for i in range(n):
            for j, (px, py) in enumerate(chips):
                landed = outs[i].at[2 * px + py, rows[i][1]]
                _rcopy(landed, landed, send_sems, recv_sems, per * i + N_CHIP - 1 + j, (x, y, 1 - c)).wait_recv()
        for cp in first + passed:
            cp.wait_send()

    return pl.pallas_call(
        body, in_specs=[_ANY] * n, out_specs=[_ANY] * n,
        out_shape=[jax.ShapeDtypeStruct((N_CHIP,) + sh.shape, sh.dtype) for sh in shards],
        scratch_shapes=[pltpu.SemaphoreType.DMA((per * n,)), pltpu.SemaphoreType.DMA((per * n,))], name=name,
    )(*shards)


def _pair_exchange(gs, name):
    n = len(gs)

    def body(*refs):
        ins, outs, (send_sems, recv_sems) = refs[:n], refs[n:2 * n], refs[2 * n:]
        x, y, c = _place()
        copies = []
        for i in range(n):
            half = ins[i].shape[1] // 2
            copies.append(_rcopy(ins[i].at[:, pl.ds((1 - c) * half, half)], outs[i], send_sems, recv_sems, i, (x, y, 1 - c)))
        for cp in copies:
            cp.start()
        for cp in copies:
            cp.wait()

    return pl.pallas_call(
        body, in_specs=[_ANY] * n, out_specs=[_ANY] * n,
        out_shape=[jax.ShapeDtypeStruct((g.shape[0], g.shape[1] // 2, g.shape[2]), g.dtype) for g in gs],
        scratch_shapes=[pltpu.SemaphoreType.DMA((n,)), pltpu.SemaphoreType.DMA((n,))], name=name,
    )(*gs)


_ROW_TILES = (640, 512, 352, 256, 128, 64, 32, 16)


def _pair_sum(g, other, c_idx, name):
    n, half, cc = other.shape
    tr = _tile(half, _ROW_TILES)

    def body(c_ref, g_ref, o_ref, out_ref):
        out_ref[...] = (g_ref[...] + o_ref[...]).astype(out_ref.dtype)

    return pl.pallas_call(
        body,
        grid_spec=pltpu.PrefetchScalarGridSpec(
            num_scalar_prefetch=1, grid=(n, half // tr),
            in_specs=[pl.BlockSpec((None, None, tr, cc), lambda k, i, c_ref: (k, c_ref[0], i, 0)),
                      pl.BlockSpec((None, tr, cc), lambda k, i, c_ref: (k, i, 0))],
            out_specs=pl.BlockSpec((None, tr, cc), lambda k, i, c_ref: (k, i, 0))),
        out_shape=jax.ShapeDtypeStruct((n, half, cc), BF16),
        compiler_params=_params("parallel", "parallel"), name=name,
    )(c_idx, g.reshape(n, 2, half, cc), other)


def _chip_exchange(ps, name):
    n = len(ps)
    per = N_CHIP - 1

    def body(*refs):
        ins, outs, (send_sems, recv_sems) = refs[:n], refs[n:2 * n], refs[2 * n:]
        x, y, c = _place()
        chip = 2 * x + y
        chips = _other_chips(x, y)
        sends = [_rcopy(ins[i].at[2 * px + py], outs[i].at[chip], send_sems, recv_sems, per * i + j, (px, py, c))
                 for i in range(n) for j, (px, py) in enumerate(chips)]
        for cp in sends:
            cp.start()
        for i in range(n):
            for j, (px, py) in enumerate(chips):
                _rcopy(ins[i].at[chip], outs[i].at[2 * px + py], send_sems, recv_sems, per * i + j, (px, py, c)).wait_recv()
        for cp in sends:
            cp.wait_send()

    return pl.pallas_call(
        body, in_specs=[_ANY] * n, out_specs=[_ANY] * n, out_shape=[jax.ShapeDtypeStruct(p.shape, p.dtype) for p in ps],
        scratch_shapes=[pltpu.SemaphoreType.DMA((per * n,)), pltpu.SemaphoreType.DMA((per * n,))], name=name,
    )(*ps)


_HBM = pl.BlockSpec(memory_space=pltpu.HBM)
_SEM = pl.BlockSpec(memory_space=pltpu.SEMAPHORE)
_DATAFLOW = pltpu.SideEffectType.DATAFLOW_SIDE_EFFECTING


def _chip_copies(srcs, lands, send_sems, recv_sems, per_chip_src, arriving):
    x, y, c = _place()
    chip = 2 * x + y
    out = []
    for i, (src, land) in enumerate(zip(srcs, lands)):
        for j, (px, py) in enumerate(_other_chips(x, y)):
            there = 2 * px + py
            part = src.at[there] if per_chip_src else src
            out.append(_rcopy(part, land.at[there if arriving else chip], send_sems, recv_sems, (N_CHIP - 1) * i + j, (px, py, c)))
    return out


def _send_start(srcs, per_chip_src, name):
    n = len(srcs)
    n_sem = (N_CHIP - 1) * n
    lands = [lax.empty((N_CHIP,) + (s.shape[1:] if per_chip_src else s.shape), s.dtype) for s in srcs]

    def body(*refs):
        ins, zones, (send_sems, recv_sems) = refs[:n], refs[n:2 * n], refs[2 * n:2 * n + 2]
        for cp in _chip_copies(ins, zones, send_sems, recv_sems, per_chip_src, False):
            cp.start()
        refs[-1][...] = jnp.zeros_like(refs[-1])

    hbm = lambda a: pltpu.HBM(a.shape, a.dtype)
    outs = pl.pallas_call(
        body, name=name, in_specs=[_HBM] * (2 * n),
        out_shape=(pltpu.SemaphoreType.DMA((n_sem,)), pltpu.SemaphoreType.DMA((n_sem,)), *[hbm(a) for a in srcs],
                   *[hbm(a) for a in lands], jax.ShapeDtypeStruct((SUBLANES, LANES), F32)),
        out_specs=(_SEM, _SEM, *[_HBM] * (2 * n), _VMEM), input_output_aliases={i: 2 + i for i in range(2 * n)},
        compiler_params=pltpu.CompilerParams(has_side_effects=_DATAFLOW),
    )(*[pltpu.with_memory_space_constraint(a, pltpu.HBM) for a in list(srcs) + lands])
    return (outs[0], outs[1], list(outs[2:2 + n]), list(outs[2 + n:2 + 2 * n])), outs[-1][0, 0]


def _send_wait(state, after, per_chip_src, name):
    send_sems, recv_sems, srcs, lands = state
    n = len(srcs)

    def body(*refs):
        ins, zones, (send_s, recv_s) = refs[:n], refs[n:2 * n], refs[2 * n:2 * n + 2]
        for cp in _chip_copies(ins, zones, send_s, recv_s, per_chip_src, True):
            cp.wait_send()
            cp.wait_recv()

    hbm = lambda a: pltpu.HBM(a.shape, a.dtype)
    outs = pl.pallas_call(
        body, name=name, in_specs=[_HBM] * (2 * n) + [_SEM, _SEM, _ANY],
        out_shape=tuple(hbm(a) for a in srcs + lands), out_specs=tuple([_HBM] * (2 * n)),
        input_output_aliases={i: i for i in range(2 * n)},
        compiler_params=pltpu.CompilerParams(has_side_effects=_DATAFLOW),
    )(*srcs, *lands, send_sems, recv_sems, after)
    return list(outs[n:])


def _sum_lead(v, name):
    n, r, cc = v.shape
    tr = _tile(r, _ROW_TILES + (8,))

    def body(v_ref, o_ref):
        acc = v_ref[0].astype(F32)
        for k in range(1, n):
            acc = acc + v_ref[k].astype(F32)
        o_ref[...] = acc

    return pl.pallas_call(
        body, grid=(r // tr,), in_specs=[pl.BlockSpec((n, tr, cc), lambda i: (0, i, 0))],
        out_specs=pl.BlockSpec((tr, cc), lambda i: (i, 0)), out_shape=jax.ShapeDtypeStruct((r, cc), F32),
        compiler_params=_params("parallel"), name=name,
    )(v)


def _pair_share(reds, name):
    n = len(reds)

    def body(*refs):
        ins, outs, (send_sems, recv_sems) = refs[:n], refs[n:2 * n], refs[2 * n:]
        x, y, c = _place()
        copies = [_rcopy(ins[i], outs[i].at[c], send_sems, recv_sems, i, (x, y, 1 - c)) for i in range(n)]
        for cp in copies:
            cp.start()
        for i in range(n):
            _rcopy(ins[i], outs[i].at[1 - c], send_sems, recv_sems, i, (x, y, 1 - c)).wait_recv()
        for cp in copies:
            cp.wait_send()

    return pl.pallas_call(
        body, in_specs=[_ANY] * n, out_specs=[_ANY] * n, out_shape=[jax.ShapeDtypeStruct((2,) + r.shape, r.dtype) for r in reds],
        scratch_shapes=[pltpu.SemaphoreType.DMA((n,)), pltpu.SemaphoreType.DMA((n,))], name=name,
    )(*reds)


def _pack(arrs, rows_multiple, dtype):
    flat = jnp.concatenate([a.reshape(-1).astype(dtype) for a in arrs])
    unit = rows_multiple * PACK_COLS
    total = -(-flat.shape[0] // unit) * unit
    return jnp.pad(flat, (0, total - flat.shape[0])).reshape(-1, PACK_COLS)


def _unpack(buf, shapes):
    lead = buf.shape[:-2]
    flat = buf.reshape(*lead, -1)
    out, off = [], 0
    for shp in shapes:
        n = 1
        for s in shp:
            n *= s
        out.append(flat[..., off:off + n].reshape(*lead, *shp))
        off += n
    return out


def _join_shards(parts, axis):
    moved = jnp.moveaxis(parts, 0, axis)
    shp = list(moved.shape)
    shp[axis:axis + 2] = [shp[axis] * shp[axis + 1]]
    return moved.reshape(shp)


def _my_shard(full, axis, chip):
    n = full.shape[axis] // N_CHIP
    return lax.dynamic_slice_in_dim(full, chip * n, n, axis)


SMALL = {"norm_g": 2, "ffn_conv_w": 2, "rg_conv_w": 2, "gla_w_alpha": 2, "gla_b_alpha": 1, "gla_norm_g": 1,
         "ada_b": None, "ffn_conv_b": None, "rg_conv_b": None, "rg_ba": None, "rg_bx": None, "rg_lambda": None}
BIG = {"rg_w_in": True, "rg_wa": False, "rg_wx": False, "rg_w_out": False, "ffn_w_up": True, "ffn_w_down": False,
       "gla_w_in": True, "gla_w_out": False}
WEIGHTS = ["ada_w", "ada_b", "norm_g", "ffn_w_up", "ffn_conv_w", "ffn_conv_b", "ffn_w_down", "rg_w_in", "rg_conv_w", "rg_conv_b",
           "rg_wa", "rg_ba", "rg_wx", "rg_bx", "rg_lambda", "rg_w_out", "gla_w_in", "gla_w_alpha", "gla_b_alpha", "gla_norm_g",
           "gla_w_out"]


def kernel(x, c, ada_w, ada_b, norm_g, ffn_w_up, ffn_conv_w, ffn_conv_b, ffn_w_down, rg_w_in, rg_conv_w, rg_conv_b, rg_wa, rg_ba, rg_wx, rg_bx, rg_lambda, rg_w_out, gla_w_in, gla_w_alpha, gla_b_alpha, gla_norm_g, gla_w_out, loss_target, m_ada_w, m_ada_b, m_norm_g, m_ffn_w_up, m_ffn_conv_w, m_ffn_conv_b, m_ffn_w_down, m_rg_w_in, m_rg_conv_w, m_rg_conv_b, m_rg_wa, m_rg_ba, m_rg_wx, m_rg_bx, m_rg_lambda, m_rg_w_out, m_gla_w_in, m_gla_w_alpha, m_gla_b_alpha, m_gla_norm_g, m_gla_w_out, v_ada_w, v_ada_b, v_norm_g, v_ffn_w_up, v_ffn_conv_w, v_ffn_conv_b, v_ffn_w_down, v_rg_w_in, v_rg_conv_w, v_rg_conv_b, v_rg_wa, v_rg_ba, v_rg_wx, v_rg_bx, v_rg_lambda, v_rg_w_out, v_gla_w_in, v_gla_w_alpha, v_gla_b_alpha, v_gla_norm_g, v_gla_w_out):
    wts = dict(ada_w=ada_w, ada_b=ada_b, norm_g=norm_g, ffn_w_up=ffn_w_up, ffn_conv_w=ffn_conv_w, ffn_conv_b=ffn_conv_b,
               ffn_w_down=ffn_w_down, rg_w_in=rg_w_in, rg_conv_w=rg_conv_w, rg_conv_b=rg_conv_b, rg_wa=rg_wa, rg_ba=rg_ba,
               rg_wx=rg_wx, rg_bx=rg_bx, rg_lambda=rg_lambda, rg_w_out=rg_w_out, gla_w_in=gla_w_in, gla_w_alpha=gla_w_alpha,
               gla_b_alpha=gla_b_alpha, gla_norm_g=gla_norm_g, gla_w_out=gla_w_out)
    mom1 = dict(ada_w=m_ada_w, ada_b=m_ada_b, norm_g=m_norm_g, ffn_w_up=m_ffn_w_up, ffn_conv_w=m_ffn_conv_w,
                ffn_conv_b=m_ffn_conv_b, ffn_w_down=m_ffn_w_down, rg_w_in=m_rg_w_in, rg_conv_w=m_rg_conv_w,
                rg_conv_b=m_rg_conv_b, rg_wa=m_rg_wa, rg_ba=m_rg_ba, rg_wx=m_rg_wx, rg_bx=m_rg_bx, rg_lambda=m_rg_lambda,
                rg_w_out=m_rg_w_out, gla_w_in=m_gla_w_in, gla_w_alpha=m_gla_w_alpha, gla_b_alpha=m_gla_b_alpha,
                gla_norm_g=m_gla_norm_g, gla_w_out=m_gla_w_out)
    mom2 = dict(ada_w=v_ada_w, ada_b=v_ada_b, norm_g=v_norm_g, ffn_w_up=v_ffn_w_up, ffn_conv_w=v_ffn_conv_w,
                ffn_conv_b=v_ffn_conv_b, ffn_w_down=v_ffn_w_down, rg_w_in=v_rg_w_in, rg_conv_w=v_rg_conv_w,
                rg_conv_b=v_rg_conv_b, rg_wa=v_rg_wa, rg_ba=v_rg_ba, rg_wx=v_rg_wx, rg_bx=v_rg_bx, rg_lambda=v_rg_lambda,
                rg_w_out=v_rg_w_out, gla_w_in=v_gla_w_in, gla_w_alpha=v_gla_w_alpha, gla_b_alpha=v_gla_b_alpha,
                gla_norm_g=v_gla_norm_g, gla_w_out=v_gla_w_out)
    xi, yi, ci = _place()
    chip, me = 2 * xi + yi, 4 * xi + 2 * yi + ci
    d = x.shape[-1]
    depth = ada_w.shape[0]
    n_ada = ada_w.shape[-1]
    sharded_small = [k for k, ax in SMALL.items() if ax is not None]

    sm = _all_gather_8(_pack([c] + [wts[k] for k in sharded_small], SUBLANES, F32), "gather_small")
    c_all = sm[:, 0, :]
    parts = _unpack(sm[0::2], [c.shape] + [wts[k].shape for k in sharded_small])[1:]
    full = {k: _join_shards(p, SMALL[k]) for k, p in zip(sharded_small, parts)}
    for k, ax in SMALL.items():
        if ax is None:
            full[k] = wts[k]

    c16 = jnp.pad(c_all, ((0, ADA_ROWS - N_DEV), (0, 0)))
    ada_b_mine = lax.dynamic_slice_in_dim(ada_b, chip * n_ada, n_ada, 1)[:, None, :]
    mod_cols = _ada_fwd(c16, ada_w, ada_b_mine, "ada_fwd")
    mod_all = _all_gather_8(mod_cols.reshape(-1, PACK_COLS), "gather_mod")[0::2].reshape(N_CHIP, depth, ADA_ROWS, n_ada)
    mod = jnp.swapaxes(lax.dynamic_index_in_dim(mod_all, me, 2, keepdims=False), 0, 1).reshape(depth, 6, d)

    items = [(k, l) for k in BIG for l in range(wts[k].shape[0])]
    stage_of = lambda k, l: "rg" if k.startswith("rg_") else ("ffn0" if (k.startswith("ffn_") and l == 0) else "l1")
    staged = {st: [it for it in items if stage_of(*it) == st] for st in ("rg", "ffn0", "l1")}
    shard = lambda k, l: wts[k][l].reshape(-1, wts[k].shape[-1]).astype(BF16)
    own = lambda got, mine: [lax.dynamic_update_index_in_dim(g, m, chip, 0) for g, m in zip(got, mine)]
    rows_joined = lambda v: v.reshape(-1, v.shape[-1])

    def placed(its, slots):
        out = {"ffn_w_up": {}, "ffn_w_down": {}}
        for (k, l), v in zip(its, slots):
            if k == "ffn_w_up":
                out[k][l] = v
            elif k == "ffn_w_down":
                out[k][l] = rows_joined(v)
            elif k in ("rg_wa", "rg_wx"):
                out[k] = _slots_to_block_rows(v, RG_BLOCKS)
            elif k == "gla_w_in":
                out[k] = _gla_head_cols(_from_col_slots(v))
            else:
                out[k] = v if BIG[k] else rows_joined(v)
        return out

    sh_rg = [shard(k, l) for k, l in staged["rg"]]
    local = {k: (v if k in ("norm_g", "ffn_conv_w", "ffn_conv_b") else v[0]) for k, v in full.items()}
    local["gla_wal"], local["gla_bal"] = _gla_alpha_heads(local["gla_w_alpha"], local["gla_b_alpha"])
    local.update(placed(staged["rg"], own(_gather_chips(sh_rg, "gather_weights_rg"), sh_rg)))
    sh_late, flying = {}, {}
    sh_late["ffn0"] = [shard(k, l) for k, l in staged["ffn0"]]
    flying["ffn0"], tok = _send_start(sh_late["ffn0"], False, "weights_ffn0_start")
    sh_late["l1"] = [shard(k, l) + tok.astype(BF16) for k, l in staged["l1"]]
    flying["l1"], tok2 = _send_start(sh_late["l1"], False, "weights_l1_start")
    mod = mod + (tok + tok2)

    def fetch(stage, after):
        got = _send_wait(flying[stage], after, False, f"weights_{stage}_wait")
        return placed(staged[stage], own(got, sh_late[stage]))

    c_idx = ci.reshape(1).astype(jnp.int32)
    psums, sent = {}, {}

    def grad_slots(gr, k, l):
        g = gr[k][l] if k in ("ffn_w_up", "ffn_w_down") else gr[k]
        if k in ("rg_wa", "rg_wx"):
            return _block_rows_to_slots(g)
        if k == "gla_w_in":
            return _col_slots(g)
        return g if BIG[k] else g.reshape(N_CHIP, -1, g.shape[-1])

    def pair_sums(stage, gr):
        gslots = [grad_slots(gr, k, l) for k, l in staged[stage]]
        theirs = _pair_exchange(gslots, f"grads_{stage}_pair_exchange")
        psums[stage] = [_pair_sum(g, t, c_idx, f"grads_pair_sum_{k}{l}") for (k, l), g, t in zip(staged[stage], gslots, theirs)]

    def done(stage, gr):
        pair_sums(stage, gr)
        sent[stage], token = _send_start(psums[stage], True, f"grads_{stage}_start")
        return token

    cols, grad_x, gr = _local_step(x[0], loss_target[0], mod, local, fetch, done)
    loss = lax.psum(0.5 * jnp.sum(cols) / d, ("x", "y", "c"))

    small_names = [k for k in SMALL if k != "ada_b"]
    gs = _all_gather_8(_pack([gr[k] for k in small_names] + [gr["mod"]], SUBLANES, F32), "gather_small_grads")
    small_shapes = [full[k].shape for k in small_names] + [(depth, 6 * d)]
    *small_sum, g_ada_b = _unpack(_sum_lead(gs, "sum_small_grads"), small_shapes)
    grads = dict(zip(small_names, small_sum))
    grads["ada_b"] = g_ada_b
    for k in sharded_small:
        grads[k] = _my_shard(grads[k], SMALL[k], chip)
    dmod_all = _unpack(gs, small_shapes)[-1].reshape(N_DEV, depth, N_CHIP, n_ada)
    dmod_mine = jnp.swapaxes(lax.dynamic_index_in_dim(dmod_all, chip, 2, keepdims=False), 0, 1)
    g_ada_w = _ada_bwd(c16, jnp.pad(dmod_mine, ((0, 0), (0, ADA_ROWS - N_DEV), (0, 0))), "ada_bwd")

    pair_sums("rg", gr)
    arrived = {"rg": _chip_exchange(psums["rg"], "grads_rg_chip_exchange")}
    for stage in ("l1", "ffn0"):
        arrived[stage] = _send_wait(sent[stage], grad_x, True, f"grads_{stage}_wait")
    order = [(st, n) for st in ("rg", "ffn0", "l1") for n in range(len(staged[st]))]
    items = [staged[st][n] for st, n in order]
    mine_of = lambda p: lax.dynamic_index_in_dim(p, chip, 0, keepdims=False)
    halves = [_sum_lead(lax.dynamic_update_index_in_dim(arrived[st][n], mine_of(psums[st][n]), chip, 0),
                        "grads_chip_sum_%s%d" % staged[st][n]) for st, n in order]
    shared = _pair_share(halves, "grads_pair_share")
    reduced = [lax.dynamic_update_index_in_dim(s2, h, ci, 0).reshape(-1, h.shape[-1]) for s2, h in zip(shared, halves)]

    delta, new_m, new_v = {}, {}, {}

    def update(k, gs_k, view):
        shp = wts[k].shape
        outs = _adamw(view(wts[k]), gs_k, view(mom1[k]), view(mom2[k]), "adamw_" + k)
        grads[k], delta[k], new_m[k], new_v[k] = (o.reshape(shp) for o in outs)

    update("ada_w", [g_ada_w[l] for l in range(depth)], lambda a: a)
    for k in BIG:
        gs_k = [g for (kk, _), g in zip(items, reduced) if kk == k]
        update(k, gs_k, lambda a: a.reshape(a.shape[0], -1, a.shape[-1]))
    small_shard_shapes = [wts[k].shape for k in SMALL]
    packed = [_pack([src[k] for k in SMALL], SUBLANES, F32) for src in (wts, grads, mom1, mom2)]
    outs = _adamw(packed[0][None], [packed[1]], packed[2][None], packed[3][None], "adamw_small")
    for dst, o in zip((delta, new_m, new_v), outs[1:]):
        for k, a in zip(SMALL, _unpack(o[0], small_shard_shapes)):
            dst[k] = a

    return (loss, grad_x[None], *[grads[k] for k in WEIGHTS], *[delta[k] for k in WEIGHTS], *[new_m[k] for k in WEIGHTS],
            *[new_v[k] for k in WEIGHTS])
```

```python
import jax
import jax.numpy as jnp
from jax import lax
from jax.experimental import pallas as pl
from jax.experimental.pallas import tpu as pltpu

F32 = jnp.float32
BF16 = jnp.bfloat16
MXU_DTYPE = BF16

EPS = 1e-6
RG_C = 8.0
RG_BLOCKS = 4
RG_CONV = 4
GLA_HEADS = 4
GLA_TAU = 16.0
GLA_CHUNK = 64
GLA_RANK = 16
FFN_CONV = 3
ADAM_LR = 0.001
ADAM_B1 = 0.9
ADAM_B2 = 0.999
ADAM_EPS = 1e-08
ADAM_WD = 0.01
ADAM_STEP = 10

LANES = 128
SUBLANES = 8
VMEM_LIMIT = 56 * 1024 * 1024
CB = 256
MESH = pl.DeviceIdType.MESH
N_DEV = 8
N_CHIP = 4


def _params(*sem):
    return pltpu.CompilerParams(dimension_semantics=sem, vmem_limit_bytes=VMEM_LIMIT)


def _tile(dim, prefs):
    for p in prefs:
        if dim % p == 0:
            return p
    return dim


def _dot(a, b, dims):
    return lax.dot_general(a.astype(MXU_DTYPE), b.astype(MXU_DTYPE), (dims, ((), ())), preferred_element_type=F32)


def _dot_nn(a, b):
    return _dot(a, b, ((1,), (0,)))


def _dot_nt(a, b):
    return _dot(a, b, ((1,), (1,)))


def _dot_tn(a, b):
    return _dot(a, b, ((0,), (0,)))


def _mm(a, b, *, ta=False, tb=False, a_parts=1, b_parts=1, w_slots=1, out_slots=1, out_dtype=F32, name):
    if ta:
        k_dim, m_dim = a.shape
        n_dim = b.shape[-1] * b_parts
    else:
        m_dim, k_dim = a.shape[-2], a.shape[-1] * a_parts
        n_dim = b.shape[-2] if tb else b.shape[-1] * w_slots
    n_unit = n_dim // max(b_parts, out_slots, 1 if tb else w_slots)
    k_unit = k_dim // max(a_parts, w_slots if tb else 1)
    tm = _tile(m_dim, (1024, 1408, 512, 256, 128))
    tn = _tile(n_unit, (1024, 1408, 896, 512, 256, 128))
    tk = _tile(k_unit, (1024, 1408, 896, 512, 256, 128))
    nk = k_dim // tk
    dims = ((0 if ta else 1,), (1 if tb else 0,))

    def spec(shape, parts, total, tile, col_grid, row_grid):
        per = total // parts // tile

        def index(i, j, k):
            g = {"i": i, "j": j, "k": k}
            col, row = g[col_grid], g[row_grid]
            return (row, col) if parts == 1 else (col // per, row, col % per)

        return pl.BlockSpec(shape if parts == 1 else (None,) + shape, index)

    def body(a_ref, b_ref, o_ref, acc_ref):
        k = pl.program_id(2)

        @pl.when(k == 0)
        def _():
            acc_ref[...] = jnp.zeros_like(acc_ref)

        acc_ref[...] += _dot(a_ref[...], b_ref[...], dims)

        @pl.when(k == nk - 1)
        def _():
            o_ref[...] = acc_ref[...].astype(o_ref.dtype)

    if ta:
        a_spec = spec((tk, tm), 1, m_dim, tm, "i", "k")
        b_spec = spec((tk, tn), b_parts, n_dim, tn, "j", "k")
    elif tb:
        a_spec = spec((tm, tk), a_parts, k_dim, tk, "k", "i")
        b_spec = spec((tn, tk), w_slots, k_dim, tk, "k", "j")
    else:
        a_spec = spec((tm, tk), a_parts, k_dim, tk, "k", "i")
        b_spec = spec((tk, tn), w_slots, n_dim, tn, "j", "k")
    out_shape = (m_dim, n_dim) if out_slots == 1 else (out_slots, m_dim, n_dim // out_slots)
    return pl.pallas_call(
        body,
        grid=(m_dim // tm, n_dim // tn, nk),
        in_specs=[a_spec, b_spec],
        out_specs=spec((tm, tn), out_slots, n_dim, tn, "j", "i"),
        out_shape=jax.ShapeDtypeStruct(out_shape, out_dtype),
        scratch_shapes=[pltpu.VMEM((tm, tn), F32)],
        compiler_params=_params("parallel", "parallel", "arbitrary"),
        name=name,
    )(a, b)


def _row_specs(s, d, ts):
    return pl.BlockSpec((ts, d), lambda i: (i, 0)), pl.BlockSpec((1, d), lambda i: (0, 0))


def _norm_mod_fwd(x, g, sc, sh, name):
    s, d = x.shape
    ts = _tile(s, (512,))

    def body(x_ref, g_ref, sc_ref, sh_ref, h_ref):
        xv = x_ref[...]
        r = lax.rsqrt(jnp.mean(xv * xv, axis=-1, keepdims=True) + EPS)
        h_ref[...] = (((xv * r) * g_ref[...]) * (1.0 + sc_ref[...]) + sh_ref[...]).astype(h_ref.dtype)

    row, vec = _row_specs(s, d, ts)
    return pl.pallas_call(
        body, grid=(s // ts,), in_specs=[row, vec, vec, vec], out_specs=row,
        out_shape=jax.ShapeDtypeStruct((s, d), MXU_DTYPE), compiler_params=_params("parallel"), name=name,
    )(x, g, sc, sh)


def _norm_mod_bwd(dh, x, g, sc, dres, name):
    s, d = x.shape
    ts = _tile(s, (512,))

    def body(dh_ref, x_ref, g_ref, sc_ref, dres_ref, dx_ref, dg_ref, dsc_ref, dsh_ref, acc_ref):
        i = pl.program_id(0)

        @pl.when(i == 0)
        def _():
            acc_ref[...] = jnp.zeros_like(acc_ref)

        xv, dhv = x_ref[...], dh_ref[...]
        r = lax.rsqrt(jnp.mean(xv * xv, axis=-1, keepdims=True) + EPS)
        n = xv * r
        acc_ref[0:1, :] += jnp.sum(dhv * n, axis=0, keepdims=True)
        acc_ref[1:2, :] += jnp.sum(dhv, axis=0, keepdims=True)
        dn = dhv * ((1.0 + sc_ref[...]) * g_ref[...])
        dx_ref[...] = dres_ref[...] + r * (dn - n * jnp.mean(dn * n, axis=-1, keepdims=True))
        dg_ref[...] = (1.0 + sc_ref[...]) * acc_ref[0:1, :]
        dsc_ref[...] = g_ref[...] * acc_ref[0:1, :]
        dsh_ref[...] = acc_ref[1:2, :]

    row, vec = _row_specs(s, d, ts)
    vshape = jax.ShapeDtypeStruct((1, d), F32)
    return pl.pallas_call(
        body, grid=(s // ts,), in_specs=[row, row, vec, vec, row], out_specs=[row, vec, vec, vec],
        out_shape=[jax.ShapeDtypeStruct((s, d), F32), vshape, vshape, vshape],
        scratch_shapes=[pltpu.VMEM((SUBLANES, d), F32)], compiler_params=_params("arbitrary"), name=name,
    )(dh, x, g, sc, dres)


def _post_fwd(x, y, g, gt, name):
    s, d = x.shape
    ts = _tile(s, (512,))

    def body(x_ref, y_ref, g_ref, gt_ref, o_ref):
        yv = y_ref[...]
        r = lax.rsqrt(jnp.mean(yv * yv, axis=-1, keepdims=True) + EPS)
        o_ref[...] = x_ref[...] + gt_ref[...] * ((yv * r) * g_ref[...])

    row, vec = _row_specs(s, d, ts)
    return pl.pallas_call(
        body, grid=(s // ts,), in_specs=[row, row, vec, vec], out_specs=row,
        out_shape=jax.ShapeDtypeStruct((s, d), F32), compiler_params=_params("parallel"), name=name,
    )(x, y, g, gt)


def _post_bwd(dxn, y, g, gt, name):
    s, d = y.shape
    ts = _tile(s, (512,))

    def body(dxn_ref, y_ref, g_ref, gt_ref, dy_ref, dg_ref, dgt_ref, acc_ref):
        i = pl.program_id(0)

        @pl.when(i == 0)
        def _():
            acc_ref[...] = jnp.zeros_like(acc_ref)

        yv, dv = y_ref[...], dxn_ref[...]
        r = lax.rsqrt(jnp.mean(yv * yv, axis=-1, keepdims=True) + EPS)
        n = yv * r
        acc_ref[0:1, :] += jnp.sum(dv * n, axis=0, keepdims=True)
        dn = dv * (gt_ref[...] * g_ref[...])
        dy_ref[...] = (r * (dn - n * jnp.mean(dn * n, axis=-1, keepdims=True))).astype(dy_ref.dtype)
        dg_ref[...] = gt_ref[...] * acc_ref[0:1, :]
        dgt_ref[...] = g_ref[...] * acc_ref[0:1, :]

    row, vec = _row_specs(s, d, ts)
    vshape = jax.ShapeDtypeStruct((1, d), F32)
    return pl.pallas_call(
        body, grid=(s // ts,), in_specs=[row, row, vec, vec], out_specs=[row, vec, vec],
        out_shape=[jax.ShapeDtypeStruct((s, d), MXU_DTYPE), vshape, vshape],
        scratch_shapes=[pltpu.VMEM((SUBLANES, d), F32)], compiler_params=_params("arbitrary"), name=name,
    )(dxn, y, g, gt)


def _loss_grad(x, tgt, name):
    s, d = x.shape
    ts = _tile(s, (512,))

    def body(x_ref, t_ref, col_ref, dx_ref):
        i = pl.program_id(0)

        @pl.when(i == 0)
        def _():
            col_ref[...] = jnp.zeros_like(col_ref)

        e = x_ref[...] - t_ref[...]
        col_ref[...] += jnp.sum(e * e, axis=0, keepdims=True)
        dx_ref[...] = e * (1.0 / d)

    row, vec = _row_specs(s, d, ts)
    return pl.pallas_call(
        body, grid=(s // ts,), in_specs=[row, row], out_specs=[vec, row],
        out_shape=[jax.ShapeDtypeStruct((1, d), F32), jax.ShapeDtypeStruct((s, d), F32)],
        compiler_params=_params("arbitrary"), name=name,
    )(x, tgt)


_GELU_C = 0.7978845608028654
_GELU_A = 0.044715


def _gelu(x):
    t = jnp.tanh(_GELU_C * (x + _GELU_A * x * x * x))
    return 0.5 * x * (1.0 + t), t


def _gelu_grad(x, t):
    return 0.5 * (1.0 + t) + 0.5 * x * (1.0 - t * t) * (_GELU_C * (1.0 + 3.0 * _GELU_A * x * x))


def _sigmoid(x):
    return 1.0 / (1.0 + jnp.exp(-x))


def _log1p_pos(y):
    u = 1.0 + y
    return jnp.where(u == 1.0, y, jnp.log(u) * (y / jnp.where(u == 1.0, 1.0, u - 1.0)))


def _softplus(x):
    return jnp.maximum(x, 0.0) + _log1p_pos(jnp.exp(-jnp.abs(x)))


def _one_minus_exp(z):
    u = jnp.exp(z)
    lg = jnp.log(jnp.where(u > 0.0, u, 1.0))
    safe = (u != 1.0) & (u > 0.0)
    return jnp.where(u == 1.0, -z, jnp.where(u > 0.0, (1.0 - u) * (z / jnp.where(safe, lg, 1.0)), 1.0))


SLAB = 16


def _cat(a, b):
    return jnp.concatenate([a, b], axis=1)


def _fold8(x):
    out = x[0:SUBLANES]
    for r in range(SUBLANES, x.shape[0], SUBLANES):
        out = out + x[r:r + SUBLANES]
    return out


def _pair_specs(shape, nb, index):
    return [pl.BlockSpec(shape, lambda j, t: index(j, t) + (j,)), pl.BlockSpec(shape, lambda j, t: index(j, t) + (j + nb,))]


def _halo_row(ts, time_of):
    return lambda j, t: (jnp.maximum(time_of(t) * (ts // SUBLANES) - 1, 0),)


def _ffn_mid_fwd(p, cw, cb, name):
    s, f2 = p.shape
    ts = _tile(s, (512,))
    nb, nt = f2 // (2 * CB), s // ts

    def body(pg_ref, pv_ref, hg_ref, hv_ref, cwg_ref, cwv_ref, cbg_ref, cbv_ref, a_ref):
        t = pl.program_id(1)
        cwv, bias = _cat(cwg_ref[...], cwv_ref[...]), _cat(cbg_ref[...], cbv_ref[...])
        w0, w1, w2 = cwv[0:1], cwv[1:2], cwv[2:3]

        def slab(blk, r0):
            u = bias + w0 * blk[6:6 + SLAB] + w1 * blk[7:7 + SLAB] + w2 * blk[8:8 + SLAB]
            a_ref[pl.ds(r0, SLAB), :] = (_gelu(u[:, :CB])[0] * u[:, CB:]).astype(a_ref.dtype)

        halo = jnp.where(t > 0, _cat(hg_ref[...], hv_ref[...]), 0.0)
        slab(jnp.concatenate([halo, _cat(pg_ref[0:SLAB, :], pv_ref[0:SLAB, :])], axis=0), 0)

        def loop(i, carry):
            r0 = pl.multiple_of(i * SLAB, SLAB)
            rows = pl.ds(pl.multiple_of(r0 - SUBLANES, SUBLANES), SLAB + SUBLANES)
            slab(_cat(pg_ref[rows, :], pv_ref[rows, :]), r0)
            return carry

        lax.fori_loop(1, ts // SLAB, loop, 0, unroll=2)

    fwd = lambda t: t
    return pl.pallas_call(
        body, grid=(nb, nt),
        in_specs=(_pair_specs((ts, CB), nb, lambda j, t: (t,)) + _pair_specs((SUBLANES, CB), nb, _halo_row(ts, fwd))
                  + _pair_specs((FFN_CONV, CB), nb, lambda j, t: (0,)) + _pair_specs((1, CB), nb, lambda j, t: (0,))),
        out_specs=pl.BlockSpec((ts, CB), lambda j, t: (t, j)),
        out_shape=jax.ShapeDtypeStruct((s, f2 // 2), MXU_DTYPE),
        compiler_params=_params("parallel", "arbitrary"), name=name,
    )(p, p, p, p, cw, cw, cb, cb)


def _ffn_mid_bwd(da, p, cw, cb, name):
    s, f2 = p.shape
    ts = _tile(s, (512,))
    nb, nt = f2 // (2 * CB), s // ts
    n_slab = ts // SLAB

    def body(da_ref, pg_ref, pv_ref, hg_ref, hv_ref, cwg_ref, cwv_ref, cbg_ref, cbv_ref, dp_ref, dcw_ref, dcb_ref,
             next_du, acc):
        tt = pl.program_id(1)
        t = nt - 1 - tt
        cwv, bias = _cat(cwg_ref[...], cwv_ref[...]), _cat(cbg_ref[...], cbv_ref[...])
        w0, w1, w2 = cwv[0:1], cwv[1:2], cwv[2:3]

        @pl.when(tt == 0)
        def _():
            next_du[...] = jnp.zeros_like(next_du)
            acc[...] = jnp.zeros_like(acc)

        def slab(blk, r0, carry):
            pm2, pm1, p0 = blk[6:6 + SLAB], blk[7:7 + SLAB], blk[8:8 + SLAB]
            u = bias + w0 * pm2 + w1 * pm1 + w2 * p0
            g, v = u[:, :CB], u[:, CB:]
            gel, th = _gelu(g)
            dav = da_ref[pl.ds(r0, SLAB), :]
            du = _cat(dav * v * _gelu_grad(g, th), dav * gel)
            ext = jnp.concatenate([du, carry], axis=0)
            dpv = (w2 * du + w1 * ext[1:1 + SLAB] + w0 * ext[2:2 + SLAB]).astype(dp_ref.dtype)
            dp_ref[0, pl.ds(r0, SLAB), :] = dpv[:, :CB]
            dp_ref[1, pl.ds(r0, SLAB), :] = dpv[:, CB:]
            acc[0] += _fold8(du)
            acc[1] += _fold8(du * pm2)
            acc[2] += _fold8(du * pm1)
            acc[3] += _fold8(du * p0)
            return du[0:SUBLANES]

        def loop(k, carry):
            r0 = pl.multiple_of((n_slab - 1 - k) * SLAB, SLAB)
            rows = pl.ds(pl.multiple_of(r0 - SUBLANES, SUBLANES), SLAB + SUBLANES)
            return slab(_cat(pg_ref[rows, :], pv_ref[rows, :]), r0, carry)

        carry = lax.fori_loop(0, n_slab - 1, loop, next_du[...], unroll=2)
        halo = jnp.where(t > 0, _cat(hg_ref[...], hv_ref[...]), 0.0)
        next_du[...] = slab(jnp.concatenate([halo, _cat(pg_ref[0:SLAB, :], pv_ref[0:SLAB, :])], axis=0), 0, carry)

        @pl.when(tt == nt - 1)
        def _():
            for half in range(2):
                cols = slice(half * CB, (half + 1) * CB)
                dcb_ref[half] = jnp.sum(acc[0][:, cols], axis=0, keepdims=True)
                for k in range(FFN_CONV):
                    dcw_ref[half, k:k + 1, :] = jnp.sum(acc[1 + k][:, cols], axis=0, keepdims=True)

    rev = lambda t: nt - 1 - t
    return pl.pallas_call(
        body, grid=(nb, nt),
        in_specs=([pl.BlockSpec((ts, CB), lambda j, t: (rev(t), j))] + _pair_specs((ts, CB), nb, lambda j, t: (rev(t),))
                  + _pair_specs((SUBLANES, CB), nb, _halo_row(ts, rev)) + _pair_specs((FFN_CONV, CB), nb, lambda j, t: (0,))
                  + _pair_specs((1, CB), nb, lambda j, t: (0,))),
        out_specs=[pl.BlockSpec((2, ts, CB), lambda j, t: (0, rev(t), j)),
                   pl.BlockSpec((2, FFN_CONV, CB), lambda j, t: (0, 0, j)),
                   pl.BlockSpec((2, 1, CB), lambda j, t: (0, 0, j))],
        out_shape=[jax.ShapeDtypeStruct((2, s, f2 // 2), MXU_DTYPE), jax.ShapeDtypeStruct((2, FFN_CONV, f2 // 2), F32),
                   jax.ShapeDtypeStruct((2, 1, f2 // 2), F32)],
        scratch_shapes=[pltpu.VMEM((SUBLANES, 2 * CB), F32), pltpu.VMEM((1 + FFN_CONV, SUBLANES, 2 * CB), F32)],
        compiler_params=_params("parallel", "arbitrary"), name=name,
    )(da, p, p, p, p, cw, cw, cb, cb)


def _rg_gates(xc, wa_ref, ba_ref, wx_ref, bx_ref, lam_ref):
    r = _sigmoid(_dot_nn(xc, wa_ref[0]) + ba_ref[...])
    ig = _sigmoid(_dot_nn(xc, wx_ref[0]) + bx_ref[...])
    sp = _softplus(-lam_ref[...])
    log_a = (-RG_C) * r * sp
    a = jnp.exp(log_a)
    mult = jnp.sqrt(_one_minus_exp(2.0 * log_a))
    return r, ig, sp, a, mult


def _rg_conv(scr, cw_ref, cb_ref, ts):
    views = [scr[5 + k:5 + k + ts, :] for k in range(RG_CONV)]
    xc = cb_ref[...]
    for k in range(RG_CONV):
        xc = xc + cw_ref[k:k + 1, :] * views[k]
    return xc, views


def _rg_param_specs():
    vec = pl.BlockSpec((1, CB), lambda g, t: (0, g))
    mat = pl.BlockSpec((1, CB, CB), lambda g, t: (g, 0, 0))
    return [pl.BlockSpec((RG_CONV, CB), lambda g, t: (0, g)), vec, mat, vec, mat, vec, vec]


NSEG = SUBLANES
NQ = CB // LANES


def _lanes(q):
    return slice(q * LANES, (q + 1) * LANES)


def _seg_scan(a_scr, x_scr, loc_scr, dec_scr, ts, reverse):
    seg = ts // NSEG

    def step(k, carry):
        out = []
        rows = pl.ds(seg - 1 - k if reverse else k, NSEG, stride=seg)
        for q in range(NQ):
            st, dec = carry[q]
            a_q, x_q, loc_q, dec_q = a_scr.at[q], x_scr.at[q], loc_scr.at[q], dec_scr.at[q]
            av = a_q[rows, :]
            if reverse:
                loc_q[rows, :] = st
                dec_q[rows, :] = dec
                st = av * (x_q[rows, :] + st)
                dec = av * dec
            else:
                st = av * st + x_q[rows, :]
                dec = av * dec
                loc_q[rows, :] = st
                dec_q[rows, :] = dec
            out.append((st, dec))
        return tuple(out)

    init = tuple((jnp.zeros((NSEG, LANES), F32), jnp.ones((NSEG, LANES), F32)) for _ in range(NQ))
    return lax.fori_loop(0, seg, step, init, unroll=4)


def _seg_chain(fin, dec, c_in, reverse):
    rows = [None] * NSEG
    c = c_in
    for sgm in (reversed(range(NSEG)) if reverse else range(NSEG)):
        rows[sgm] = c
        c = fin[sgm:sgm + 1] + dec[sgm:sgm + 1] * c
    return jnp.concatenate(rows, axis=0), c


def _rg_mid_fwd(pj, cw, cb, wa, ba, wx, bx, lam, name):
    s = pj.shape[0]
    nb = pj.shape[1] // (2 * CB)
    ts = _tile(s, (512,))
    nt = s // ts
    seg = ts // NSEG

    def body(gate_ref, x_ref, halo_ref, cw_ref, cb_ref, wa_ref, ba_ref, wx_ref, bx_ref, lam_ref, y_ref, hs_ref,
             scr, a_scr, u_scr, loc_scr, dec_scr, h_scr):
        t = pl.program_id(1)

        @pl.when(t == 0)
        def _():
            h_scr[...] = jnp.zeros_like(h_scr)

        scr[0:SUBLANES, :] = jnp.where(t > 0, halo_ref[...], 0.0)
        scr[SUBLANES:, :] = x_ref[...]
        xc, _ = _rg_conv(scr, cw_ref, cb_ref, ts)
        _, ig, _, a, mult = _rg_gates(xc, wa_ref, ba_ref, wx_ref, bx_ref, lam_ref)
        u = mult * (ig * xc)
        for q in range(NQ):
            a_scr[q] = a[:, _lanes(q)]
            u_scr[q] = u[:, _lanes(q)]
        fin = _seg_scan(a_scr, u_scr, loc_scr, dec_scr, ts, False)
        for q in range(NQ):
            enter, leave = _seg_chain(fin[q][0], fin[q][1], h_scr[0:1, _lanes(q)], False)
            h_scr[0:1, _lanes(q)] = leave
            for sgm in range(NSEG):
                rows = slice(sgm * seg, (sgm + 1) * seg)
                hs_ref[rows, _lanes(q)] = loc_scr[q, rows, :] + dec_scr[q, rows, :] * enter[sgm:sgm + 1]
        y_ref[...] = (_gelu(gate_ref[...])[0] * hs_ref[...]).astype(y_ref.dtype)

    blk = pl.BlockSpec((ts, CB), lambda g, t: (t, g))
    lane_scr = pltpu.VMEM((NQ, ts, LANES), F32)
    return pl.pallas_call(
        body, grid=(nb, nt),
        in_specs=_pair_specs((ts, CB), nb, lambda g, t: (t,))
        + [pl.BlockSpec((SUBLANES, CB), lambda g, t: _halo_row(ts, lambda u: u)(g, t) + (g + nb,))] + _rg_param_specs(),
        out_specs=[blk, blk],
        out_shape=[jax.ShapeDtypeStruct((s, nb * CB), MXU_DTYPE), jax.ShapeDtypeStruct((s, nb * CB), F32)],
        scratch_shapes=[pltpu.VMEM((ts + SUBLANES, CB), F32), lane_scr, lane_scr, lane_scr, lane_scr,
                        pltpu.VMEM((SUBLANES, CB), F32)],
        compiler_params=_params("parallel", "arbitrary"), name=name,
    )(pj, pj, pj, cw, cb, wa, ba, wx, bx, lam)


def _rg_mid_bwd(dy, pj, hs, cw, cb, wa, ba, wx, bx, lam, name):
    s = pj.shape[0]
    nb = pj.shape[1] // (2 * CB)
    ts = _tile(s, (512,))
    nt = s // ts

    def body(dy_ref, gate_ref, x_ref, halo_ref, hs_ref, hsh_ref, cw_ref, cb_ref, wa_ref, ba_ref, wx_ref, bx_ref, lam_ref,
             dpj_ref, dcw_ref, dcb_ref, dwa_ref, dba_ref, dwx_ref, dbx_ref, dlam_ref,
             scr, hscr, a_scr, d_scr, loc_scr, dec_scr, g_scr, dxscr, c_scr):
        tt = pl.program_id(1)
        t = nt - 1 - tt
        seg = ts // NSEG

        @pl.when(tt == 0)
        def _():
            c_scr[...] = jnp.zeros_like(c_scr)
            dxscr[ts:, :] = jnp.zeros((SUBLANES, CB), F32)
            for ref in (dcw_ref, dcb_ref, dwa_ref, dba_ref, dwx_ref, dbx_ref, dlam_ref):
                ref[...] = jnp.zeros_like(ref)

        scr[0:SUBLANES, :] = jnp.where(t > 0, halo_ref[...], 0.0)
        scr[SUBLANES:, :] = x_ref[...]
        hscr[0:SUBLANES, :] = jnp.where(t > 0, hsh_ref[...], 0.0)
        hscr[SUBLANES:, :] = hs_ref[...]
        xc, views = _rg_conv(scr, cw_ref, cb_ref, ts)
        r, ig, sp, a, mult = _rg_gates(xc, wa_ref, ba_ref, wx_ref, bx_ref, lam_ref)
        gate = gate_ref[...]
        gel, th = _gelu(gate)
        dyv = dy_ref[...]
        dpj_ref[0] = (dyv * hs_ref[...] * _gelu_grad(gate, th)).astype(dpj_ref.dtype)
        dhs = dyv * gel
        for q in range(NQ):
            a_scr[q] = a[:, _lanes(q)]
            d_scr[q] = dhs[:, _lanes(q)]
        fin = _seg_scan(a_scr, d_scr, loc_scr, dec_scr, ts, True)
        for q in range(NQ):
            enter, leave = _seg_chain(fin[q][0], fin[q][1], c_scr[0:1, _lanes(q)], True)
            c_scr[0:1, _lanes(q)] = leave
            for sgm in range(NSEG):
                rows = slice(sgm * seg, (sgm + 1) * seg)
                g_scr[rows, _lanes(q)] = d_scr[q, rows, :] + loc_scr[q, rows, :] + dec_scr[q, rows, :] * enter[sgm:sgm + 1]
        du = g_scr[...]
        da = du * hscr[7:7 + ts, :]
        dmult = du * (ig * xc)
        dig = du * (mult * xc)
        dxc = du * (mult * ig)
        dlog_a = da * a - dmult * (a * a / mult)
        dlam_ref[...] += jnp.sum(dlog_a * r, axis=0, keepdims=True) * (RG_C * _sigmoid(-lam_ref[...]))
        dpr = dlog_a * ((-RG_C) * sp) * (r * (1.0 - r))
        dpi = dig * (ig * (1.0 - ig))
        dba_ref[...] += jnp.sum(dpr, axis=0, keepdims=True)
        dbx_ref[...] += jnp.sum(dpi, axis=0, keepdims=True)
        dwa_ref[0] += _dot_tn(xc, dpr)
        dwx_ref[0] += _dot_tn(xc, dpi)
        dxc = dxc + _dot_nt(dpr, wa_ref[0]) + _dot_nt(dpi, wx_ref[0])
        dcb_ref[...] += jnp.sum(dxc, axis=0, keepdims=True)
        for k in range(RG_CONV):
            dcw_ref[k:k + 1, :] += jnp.sum(dxc * views[k], axis=0, keepdims=True)
        dxscr[0:ts, :] = dxc
        dxp = cw_ref[3:4, :] * dxc
        for k in range(RG_CONV - 1):
            dxp = dxp + cw_ref[k:k + 1, :] * dxscr[3 - k:3 - k + ts, :]
        dpj_ref[1] = dxp.astype(dpj_ref.dtype)
        dxscr[ts:, :] = dxscr[0:SUBLANES, :]

    rev = lambda g, t: (nt - 1 - t, g)
    rev_halo = lambda g, t: (jnp.maximum((nt - 1 - t) * (ts // SUBLANES) - 1, 0), g)
    vec = pl.BlockSpec((1, CB), lambda g, t: (0, g))
    mat = pl.BlockSpec((1, CB, CB), lambda g, t: (g, 0, 0))
    d = nb * CB
    vshape = jax.ShapeDtypeStruct((1, d), F32)
    mshape = jax.ShapeDtypeStruct((nb, CB, CB), F32)
    return pl.pallas_call(
        body, grid=(nb, nt),
        in_specs=[pl.BlockSpec((ts, CB), rev)] + _pair_specs((ts, CB), nb, lambda g, t: (nt - 1 - t,))
        + [pl.BlockSpec((SUBLANES, CB), lambda g, t: (rev_halo(g, t)[0], g + nb)),
           pl.BlockSpec((ts, CB), rev), pl.BlockSpec((SUBLANES, CB), rev_halo)] + _rg_param_specs(),
        out_specs=[pl.BlockSpec((2, ts, CB), lambda g, t: (0, nt - 1 - t, g)), pl.BlockSpec((RG_CONV, CB), lambda g, t: (0, g)),
                   vec, mat, vec, mat, vec, vec],
        out_shape=[jax.ShapeDtypeStruct((2, s, d), MXU_DTYPE), jax.ShapeDtypeStruct((RG_CONV, d), F32), vshape, mshape, vshape,
                   mshape, vshape, vshape],
        scratch_shapes=[pltpu.VMEM((ts + SUBLANES, CB), F32), pltpu.VMEM((ts + SUBLANES, CB), F32)]
        + [pltpu.VMEM((NQ, ts, LANES), F32)] * 4
        + [pltpu.VMEM((ts, CB), F32), pltpu.VMEM((ts + SUBLANES, CB), F32), pltpu.VMEM((SUBLANES, CB), F32)],
        compiler_params=_params("parallel", "arbitrary"), name=name,
    )(dy, pj, pj, pj, hs, hs, cw, cb, wa, ba, wx, bx, lam)


GLA_DK = 128
GLA_DV = 256
GLA_HB = 2 * GLA_DK + 2 * GLA_DV + LANES
GLA_TS = 256


def _split3(x):
    hi = x.astype(BF16)
    r1 = x - hi.astype(F32)
    mid = r1.astype(BF16)
    lo = (r1 - mid.astype(F32)).astype(BF16)
    return hi, mid, lo


def _chunk_cumsum(x, reverse):
    n = x.shape[0]
    i = lax.broadcasted_iota(jnp.int32, (n, n), 0)
    j = lax.broadcasted_iota(jnp.int32, (n, n), 1)
    same = (i // GLA_CHUNK) == (j // GLA_CHUNK)
    tri = jnp.where(same & ((j >= i) if reverse else (j <= i)), 1.0, 0.0).astype(BF16)
    out = jnp.zeros(x.shape, F32)
    for piece in _split3(x):
        out = out + lax.dot_general(tri, piece, (((1,), (0,)), ((), ())), preferred_element_type=F32)
    return out


def _gla_split(blk):
    q = blk[:, 0:GLA_DK] * (GLA_DK ** -0.5)
    k = blk[:, GLA_DK:2 * GLA_DK]
    v = blk[:, 2 * GLA_DK:2 * GLA_DK + GLA_DV]
    r = blk[:, 2 * GLA_DK + GLA_DV:2 * GLA_DK + 2 * GLA_DV]
    z = blk[:, 2 * GLA_DK + 2 * GLA_DV:]
    return q, k, v, r, z


def _gla_decays(gc):
    gref = gc[GLA_CHUNK // 2:GLA_CHUNK // 2 + 1, :]
    glast = gc[GLA_CHUNK - 1:GLA_CHUNK, :]
    return jnp.exp(gc), jnp.exp(gc - gref), jnp.exp(gref - gc), jnp.exp(glast - gc), jnp.exp(glast)


def _causal_mask():
    i = lax.broadcasted_iota(jnp.int32, (GLA_CHUNK, GLA_CHUNK), 0)
    j = lax.broadcasted_iota(jnp.int32, (GLA_CHUNK, GLA_CHUNK), 1)
    return j <= i


def _log_sigmoid(x):
    return jnp.minimum(x, 0.0) - _log1p_pos(jnp.exp(-jnp.abs(x)))


def _gla_mid_fwd(pj, wal, bal, ng, name):
    s = pj.shape[0]
    nh = pj.shape[1] // GLA_HB
    ts = _tile(s, (GLA_TS,))
    nt, nc = s // ts, ts // GLA_CHUNK

    def body(pj_ref, wal_ref, bal_ref, ng_ref, act_ref, o_ref, st_ref, s_scr):
        t = pl.program_id(0)

        @pl.when(t == 0)
        def _():
            s_scr[...] = jnp.zeros_like(s_scr)

        heads = []
        for h in range(nh):
            q, k, v, r, z = _gla_split(pj_ref[:, h * GLA_HB:(h + 1) * GLA_HB])
            g = _log_sigmoid(_dot_nn(z, wal_ref[h]) + bal_ref[h]) * (1.0 / GLA_TAU)
            heads.append((q, k, v, r, _chunk_cumsum(g, False)))
        mask = _causal_mask()
        for c in range(nc):
            sl = slice(c * GLA_CHUNK, (c + 1) * GLA_CHUNK)
            for h, (q, k, v, r, gcum) in enumerate(heads):
                eg, eq, ek, ekd, egl = _gla_decays(gcum[sl])
                st = s_scr[h]
                st_ref[c, h] = st
                attn = jnp.where(mask, _dot_nt(q[sl] * eq, k[sl] * ek), 0.0)
                o_ref[sl, h * GLA_DV:(h + 1) * GLA_DV] = _dot_nt(q[sl] * eg, st) + _dot_nn(attn, v[sl])
                s_scr[h] = st * egl + _dot_tn(v[sl], k[sl] * ekd)
        for h, (q, k, v, r, gcum) in enumerate(heads):
            cols = slice(h * GLA_DV, (h + 1) * GLA_DV)
            o = o_ref[:, cols]
            on = o * lax.rsqrt(jnp.mean(o * o, axis=-1, keepdims=True) + EPS)
            act_ref[:, cols] = ((on * ng_ref[...]) * (r * _sigmoid(r))).astype(act_ref.dtype)

    blk = pl.BlockSpec((ts, nh * GLA_DV), lambda t: (t, 0))
    whole = lambda shape: pl.BlockSpec(shape, lambda t: (0,) * len(shape))
    return pl.pallas_call(
        body, grid=(nt,),
        in_specs=[pl.BlockSpec((ts, nh * GLA_HB), lambda t: (t, 0)), whole((nh, LANES, GLA_DK)), whole((nh, 1, GLA_DK)),
                  whole((1, GLA_DV))],
        out_specs=[blk, blk, pl.BlockSpec((nc, nh, GLA_DV, GLA_DK), lambda t: (t, 0, 0, 0))],
        out_shape=[jax.ShapeDtypeStruct((s, nh * GLA_DV), MXU_DTYPE), jax.ShapeDtypeStruct((s, nh * GLA_DV), F32),
                   jax.ShapeDtypeStruct((s // GLA_CHUNK, nh, GLA_DV, GLA_DK), F32)],
        scratch_shapes=[pltpu.VMEM((nh, GLA_DV, GLA_DK), F32)],
        compiler_params=_params("arbitrary"), name=name,
    )(pj, wal, bal, ng)


def _gla_mid_bwd(dact, pj, o, st, wal, bal, ng, name):
    s = pj.shape[0]
    nh = pj.shape[1] // GLA_HB
    ts = _tile(s, (GLA_TS,))
    nt, nc = s // ts, ts // GLA_CHUNK
    o_q, o_k, o_v, o_r, o_z = 0, GLA_DK, 2 * GLA_DK, 2 * GLA_DK + GLA_DV, 2 * GLA_DK + 2 * GLA_DV

    def body(dact_ref, pj_ref, o_ref, st_ref, wal_ref, bal_ref, ng_ref, dpj_ref, dwal_ref, dbal_ref, dng_ref,
             ds_scr, dg_scr):
        tt = pl.program_id(0)

        @pl.when(tt == 0)
        def _():
            ds_scr[...] = jnp.zeros_like(ds_scr)
            dwal_ref[...] = jnp.zeros_like(dwal_ref)
            dbal_ref[...] = jnp.zeros_like(dbal_ref)
            dng_ref[...] = jnp.zeros_like(dng_ref)

        heads = []
        for h in range(nh):
            base = h * GLA_HB
            q, k, v, r, z = _gla_split(pj_ref[:, base:base + GLA_HB])
            logit = _dot_nn(z, wal_ref[h]) + bal_ref[h]
            gcum = _chunk_cumsum(_log_sigmoid(logit) * (1.0 / GLA_TAU), False)
            ov = o_ref[:, h * GLA_DV:(h + 1) * GLA_DV]
            ro = lax.rsqrt(jnp.mean(ov * ov, axis=-1, keepdims=True) + EPS)
            on = ov * ro
            sg = _sigmoid(r)
            sil = r * sg
            dav = dact_ref[:, h * GLA_DV:(h + 1) * GLA_DV]
            dpj_ref[:, base + o_r:base + o_z] = (dav * (on * ng_ref[...]) * (sg + sil * (1.0 - sg))).astype(dpj_ref.dtype)
            t1 = dav * sil
            dng_ref[...] += jnp.sum(t1 * on, axis=0, keepdims=True)
            dn = t1 * ng_ref[...]
            do = ro * (dn - on * jnp.mean(dn * on, axis=-1, keepdims=True))
            heads.append((q, k, v, z, logit, gcum, do))
        mask = _causal_mask()
        scale = GLA_DK ** -0.5
        last_row = lax.broadcasted_iota(jnp.int32, (GLA_CHUNK, GLA_DK), 0) == GLA_CHUNK - 1
        for c in reversed(range(nc)):
            sl = slice(c * GLA_CHUNK, (c + 1) * GLA_CHUNK)
            for h, (q, k, v, z, logit, gcum, do) in enumerate(heads):
                base = h * GLA_HB
                eg, eq, ek, ekd, egl = _gla_decays(gcum[sl])
                qc, kc, vc, doc = q[sl], k[sl], v[sl], do[sl]
                qg, qt, kt, kd = qc * eg, qc * eq, kc * ek, kc * ekd
                sp = st_ref[c, h]
                ds = ds_scr[h]
                attn = jnp.where(mask, _dot_nt(qt, kt), 0.0)
                dattn = jnp.where(mask, _dot_nt(doc, vc), 0.0)
                dqg = _dot_nn(doc, sp)
                dqt = _dot_nn(dattn, kt)
                dkt = _dot_tn(dattn, qt)
                dkd = _dot_nn(vc, ds)
                dpj_ref[sl, base + o_v:base + o_r] = (_dot_tn(attn, doc) + _dot_nt(kd, ds)).astype(dpj_ref.dtype)
                dpj_ref[sl, base + o_q:base + o_k] = (scale * (dqg * eg + dqt * eq)).astype(dpj_ref.dtype)
                dpj_ref[sl, base + o_k:base + o_v] = (dkt * ek + dkd * ekd).astype(dpj_ref.dtype)
                kdd = dkd * kd
                dgl = jnp.sum(kdd, axis=0, keepdims=True) + jnp.sum(ds * sp, axis=0, keepdims=True) * egl
                dg_scr[h, sl, :] = dqg * qg + dqt * qt - dkt * kt - kdd + jnp.where(last_row, dgl, 0.0)
                ds_scr[h] = ds * egl + _dot_tn(doc, qg)
        for h, (q, k, v, z, logit, gcum, do) in enumerate(heads):
            base = h * GLA_HB
            dlogit = _chunk_cumsum(dg_scr[h], True) * (1.0 / GLA_TAU) * _sigmoid(-logit)
            dpj_ref[:, base + o_z:base + GLA_HB] = _dot_nt(dlogit, wal_ref[h]).astype(dpj_ref.dtype)
            dwal_ref[h] += _dot_tn(z, dlogit)
            dbal_ref[h] += jnp.sum(dlogit, axis=0, keepdims=True)

    rev = lambda t: (nt - 1 - t, 0)
    whole = lambda shape: pl.BlockSpec(shape, lambda t: (0,) * len(shape))
    wide = pl.BlockSpec((ts, nh * GLA_DV), rev)
    return pl.pallas_call(
        body, grid=(nt,),
        in_specs=[wide, pl.BlockSpec((ts, nh * GLA_HB), rev), wide,
                  pl.BlockSpec((nc, nh, GLA_DV, GLA_DK), lambda t: (nt - 1 - t, 0, 0, 0)),
                  whole((nh, LANES, GLA_DK)), whole((nh, 1, GLA_DK)), whole((1, GLA_DV))],
        out_specs=[pl.BlockSpec((ts, nh * GLA_HB), rev), whole((nh, LANES, GLA_DK)), whole((nh, 1, GLA_DK)), whole((1, GLA_DV))],
        out_shape=[jax.ShapeDtypeStruct((s, nh * GLA_HB), MXU_DTYPE), jax.ShapeDtypeStruct((nh, LANES, GLA_DK), F32),
                   jax.ShapeDtypeStruct((nh, 1, GLA_DK), F32), jax.ShapeDtypeStruct((1, GLA_DV), F32)],
        scratch_shapes=[pltpu.VMEM((nh, GLA_DV, GLA_DK), F32), pltpu.VMEM((nh, ts, GLA_DK), F32)],
        compiler_params=_params("arbitrary"), name=name,
    )(dact, pj, o, st, wal, bal, ng)


def _adamw(w, gs, m, v, name):
    layers, rows, cols = w.shape
    tr = _tile(rows, (256, 128, 64, 32, 16, 8))
    c1 = 1.0 / (1.0 - ADAM_B1 ** ADAM_STEP)
    c2 = 1.0 / (1.0 - ADAM_B2 ** ADAM_STEP)

    def body(*refs):
        g_refs, (w_ref, m_ref, v_ref, go_ref, d_ref, mo_ref, vo_ref) = refs[:layers], refs[layers:]
        gv = g_refs[0][...]
        for l in range(1, layers):
            gv = jnp.where(pl.program_id(0) == l, g_refs[l][...], gv)
        m2 = ADAM_B1 * m_ref[...] + (1.0 - ADAM_B1) * gv
        v2 = ADAM_B2 * v_ref[...] + (1.0 - ADAM_B2) * (gv * gv)
        d_ref[...] = (-ADAM_LR) * ((m2 * c1) / (jnp.sqrt(v2 * c2) + ADAM_EPS) + ADAM_WD * w_ref[...])
        go_ref[...] = gv
        mo_ref[...] = m2
        vo_ref[...] = v2

    g_spec = pl.BlockSpec((tr, cols), lambda l, i: (i, 0))
    spec = pl.BlockSpec((None, tr, cols), lambda l, i: (l, i, 0))
    shape = jax.ShapeDtypeStruct((layers, rows, cols), F32)
    return pl.pallas_call(
        body, grid=(layers, rows // tr), in_specs=[g_spec] * layers + [spec] * 3, out_specs=[spec] * 4, out_shape=[shape] * 4,
        compiler_params=_params("parallel", "parallel"), name=name,
    )(*gs, w, m, v)


def _gla_head_cols(w):
    d = w.shape[0]
    qk, dv = GLA_HEADS * GLA_DK, GLA_HEADS * GLA_DV
    q, k, v, r, z = jnp.split(w, [qk, 2 * qk, 2 * qk + dv, 2 * qk + 2 * dv], axis=1)
    zp = jnp.pad(z, ((0, 0), (0, LANES - GLA_RANK)))
    parts = [q.reshape(d, GLA_HEADS, GLA_DK), k.reshape(d, GLA_HEADS, GLA_DK), v.reshape(d, GLA_HEADS, GLA_DV),
             r.reshape(d, GLA_HEADS, GLA_DV), jnp.broadcast_to(zp[:, None, :], (d, GLA_HEADS, LANES))]
    return jnp.concatenate(parts, axis=2).reshape(d, GLA_HEADS * GLA_HB)


def _gla_unhead_cols(w):
    d = w.shape[0]
    w = w.reshape(d, GLA_HEADS, GLA_HB)
    o = 2 * GLA_DK + 2 * GLA_DV
    parts = [w[:, :, 0:GLA_DK].reshape(d, -1), w[:, :, GLA_DK:2 * GLA_DK].reshape(d, -1),
             w[:, :, 2 * GLA_DK:2 * GLA_DK + GLA_DV].reshape(d, -1), w[:, :, 2 * GLA_DK + GLA_DV:o].reshape(d, -1),
             jnp.sum(w[:, :, o:o + GLA_RANK], axis=1)]
    return jnp.concatenate(parts, axis=1)


def _gla_alpha_heads(w_alpha, b_alpha):
    wal = jnp.swapaxes(w_alpha.reshape(GLA_RANK, GLA_HEADS, GLA_DK), 0, 1)
    return jnp.pad(wal, ((0, 0), (0, LANES - GLA_RANK), (0, 0))), b_alpha.reshape(GLA_HEADS, 1, GLA_DK)


def _gla_layouts(w):
    w = dict(w)
    w["gla_wal"], w["gla_bal"] = _gla_alpha_heads(w["gla_w_alpha"], w["gla_b_alpha"])
    w["gla_w_in"] = _gla_head_cols(w["gla_w_in"])
    return w


def _col_slots(w):
    r, c = w.shape
    return jnp.moveaxis(w.reshape(r, N_CHIP, c // N_CHIP), 1, 0)


def _from_col_slots(w):
    n, r, c = w.shape
    return jnp.moveaxis(w, 0, 1).reshape(r, n * c)


def _block_rows_to_slots(w):
    g, r4, cc = w.shape
    return jnp.swapaxes(w.reshape(g, N_CHIP, r4 // N_CHIP, cc), 0, 1).reshape(N_CHIP, g * (r4 // N_CHIP), cc)


def _slots_to_block_rows(w, g):
    n, gr, cc = w.shape
    return jnp.swapaxes(w.reshape(n, g, gr // g, cc), 0, 1).reshape(g, n * (gr // g), cc)


def _local_step(x, tgt, mod, w, fetch=None, done=None):
    depth = mod.shape[0]
    row = lambda v: v.reshape(1, -1)
    w = dict(w)
    w["ffn_w_up"], w["ffn_w_down"] = dict(enumerate(w["ffn_w_up"])), dict(enumerate(w["ffn_w_down"]))

    def arrive(stage, after):
        if fetch is not None:
            for k, v in fetch(stage, after).items():
                if isinstance(v, dict):
                    w[k].update(v)
                else:
                    w[k] = v

    saved = []
    for i in range(depth):
        if i == 1:
            arrive("l1", x)
        sh_m, sc_m, gt_m, sh_f, sc_f, gt_f = (mod[i, j:j + 1] for j in range(6))
        g0, g1, g2, g3 = (w["norm_g"][i, j:j + 1] for j in range(4))
        tag = f"_l{i}"
        h = _norm_mod_fwd(x, g0, sc_m, sh_m, "norm_mix" + tag)
        if i % 2 == 0:
            pj = _mm(h, w["rg_w_in"], w_slots=N_CHIP, name="rg_in" + tag)
            act, aux = _rg_mid_fwd(pj, w["rg_conv_w"], row(w["rg_conv_b"]), w["rg_wa"], row(w["rg_ba"]), w["rg_wx"],
                                   row(w["rg_bx"]), row(w["rg_lambda"]), "rg_mid" + tag)
            y = _mm(act, w["rg_w_out"], name="rg_out" + tag)
        else:
            pj = _mm(h, w["gla_w_in"], name="gla_in" + tag)
            act, *aux = _gla_mid_fwd(pj, w["gla_wal"], w["gla_bal"], row(w["gla_norm_g"]), "gla_mid" + tag)
            y = _mm(act, w["gla_w_out"], name="gla_out" + tag)
        x1 = _post_fwd(x, y, g1, gt_m, "post_mix" + tag)
        if i == 0:
            arrive("ffn0", x1)
        h2 = _norm_mod_fwd(x1, g2, sc_f, sh_f, "norm_ffn" + tag)
        p = _mm(h2, w["ffn_w_up"][i], w_slots=N_CHIP, name="ffn_up" + tag)
        a = _ffn_mid_fwd(p, w["ffn_conv_w"][i], w["ffn_conv_b"][i:i + 1], "ffn_mid" + tag)
        y2 = _mm(a, w["ffn_w_down"][i], name="ffn_down" + tag)
        x2 = _post_fwd(x1, y2, g3, gt_f, "post_ffn" + tag)
        saved.append((x, h, pj, act, aux, y, x1, h2, p, a, y2))
        x = x2

    cols, dx = _loss_grad(x, tgt, "loss")

    stacked = ("norm_g", "ffn_conv_w", "ffn_conv_b", "mod")
    gr = {k: [None] * depth for k in stacked + ("ffn_w_up", "ffn_w_down")}
    told = lambda stage: done(stage, gr) if done is not None else 0.0
    for i in reversed(range(depth)):
        x0, h, pj, act, aux, y, x1, h2, p, a, y2 = saved[i]
        sh_m, sc_m, gt_m, sh_f, sc_f, gt_f = (mod[i, j:j + 1] for j in range(6))
        g0, g1, g2, g3 = (w["norm_g"][i, j:j + 1] for j in range(4))
        tag = f"_l{i}"
        dy2, d_g3, d_gt_f = _post_bwd(dx, y2, g3, gt_f, "post_ffn_b" + tag)
        da = _mm(dy2, w["ffn_w_down"][i], tb=True, name="ffn_down_dx" + tag)
        gr["ffn_w_down"][i] = _mm(a, dy2, ta=True, name="ffn_down_dw" + tag)
        dp, dcw, dcb = _ffn_mid_bwd(da, p, w["ffn_conv_w"][i], w["ffn_conv_b"][i:i + 1], "ffn_mid_b" + tag)
        gr["ffn_conv_w"][i], gr["ffn_conv_b"][i] = _cat(dcw[0], dcw[1]), _cat(dcb[0], dcb[1])[0]
        dh2 = _mm(dp, w["ffn_w_up"][i], tb=True, a_parts=2, w_slots=N_CHIP, name="ffn_up_dx" + tag)
        gr["ffn_w_up"][i] = _mm(h2, dp, ta=True, b_parts=2, out_slots=N_CHIP, name="ffn_up_dw" + tag)
        dx1, d_g2, d_sc_f, d_sh_f = _norm_mod_bwd(dh2, x1, g2, sc_f, dx, "norm_ffn_b" + tag)
        if i == 0:
            gt_m = gt_m + told("ffn0")
        dy, d_g1, d_gt_m = _post_bwd(dx1, y, g1, gt_m, "post_mix_b" + tag)
        if i % 2 == 0:
            dact = _mm(dy, w["rg_w_out"], tb=True, name="rg_out_dx" + tag)
            gr["rg_w_out"] = _mm(act, dy, ta=True, name="rg_out_dw" + tag)
            dpj, gr["rg_conv_w"], d_cb, gr["rg_wa"], d_ba, gr["rg_wx"], d_bx, d_lam = _rg_mid_bwd(
                dact, pj, aux, w["rg_conv_w"], row(w["rg_conv_b"]), w["rg_wa"], row(w["rg_ba"]), w["rg_wx"],
                row(w["rg_bx"]), row(w["rg_lambda"]), "rg_mid_b" + tag)
            gr["rg_conv_b"], gr["rg_ba"], gr["rg_bx"], gr["rg_lambda"] = d_cb[0], d_ba[0], d_bx[0], d_lam[0]
            dh = _mm(dpj, w["rg_w_in"], tb=True, a_parts=2, w_slots=N_CHIP, name="rg_in_dx" + tag)
            gr["rg_w_in"] = _mm(h, dpj, ta=True, b_parts=2, out_slots=N_CHIP, name="rg_in_dw" + tag)
        else:
            dact = _mm(dy, w["gla_w_out"], tb=True, name="gla_out_dx" + tag)
            gr["gla_w_out"] = _mm(act, dy, ta=True, name="gla_out_dw" + tag)
            dpj, d_wal, d_bal, d_ng = _gla_mid_bwd(dact, pj, aux[0], aux[1], w["gla_wal"], w["gla_bal"],
                                                   row(w["gla_norm_g"]), "gla_mid_b" + tag)
            gr["gla_w_alpha"] = jnp.swapaxes(d_wal[:, :GLA_RANK, :], 0, 1).reshape(GLA_RANK, GLA_HEADS * GLA_DK)
            gr["gla_b_alpha"], gr["gla_norm_g"] = d_bal.reshape(-1), d_ng[0]
            dh = _mm(dpj, w["gla_w_in"], tb=True, name="gla_in_dx" + tag)
            gr["gla_w_in"] = _gla_unhead_cols(_mm(h, dpj, ta=True, name="gla_in_dw" + tag))
            mod = mod.at[0].add(told("l1"))
        dx, d_g0, d_sc_m, d_sh_m = _norm_mod_bwd(dh, x0, g0, sc_m, dx1, "norm_mix_b" + tag)
        gr["norm_g"][i] = jnp.concatenate([d_g0, d_g1, d_g2, d_g3], axis=0)
        gr["mod"][i] = jnp.concatenate([d_sh_m, d_sc_m, d_gt_m, d_sh_f, d_sc_f, d_gt_f], axis=0)
    for k in stacked:
        gr[k] = jnp.stack(gr[k])
    return cols, dx, gr


ADA_ROWS = 16


def _ada_fwd(c16, ada_w, ada_b, name):
    depth, d, n = ada_w.shape
    tn = _tile(n, (512, 256, 128))

    def body(c_ref, w_ref, b_ref, o_ref):
        cv = c_ref[...]
        o_ref[0] = _dot_nn(cv * _sigmoid(cv), w_ref[0]) + b_ref[0]

    return pl.pallas_call(
        body, grid=(depth, n // tn),
        in_specs=[pl.BlockSpec((ADA_ROWS, d), lambda l, j: (0, 0)), pl.BlockSpec((1, d, tn), lambda l, j: (l, 0, j)),
                  pl.BlockSpec((1, 1, tn), lambda l, j: (l, 0, j))],
        out_specs=pl.BlockSpec((1, ADA_ROWS, tn), lambda l, j: (l, 0, j)),
        out_shape=jax.ShapeDtypeStruct((depth, ADA_ROWS, n), F32),
        compiler_params=_params("parallel", "parallel"), name=name,
    )(c16, ada_w, ada_b)


def _ada_bwd(c16, dmod16, name):
    depth, _, n = dmod16.shape
    d = c16.shape[1]
    tn = _tile(n, (512, 256, 128))

    def body(c_ref, dm_ref, o_ref):
        cv = c_ref[...]
        o_ref[0] = _dot_tn(cv * _sigmoid(cv), dm_ref[0])

    return pl.pallas_call(
        body, grid=(depth, n // tn),
        in_specs=[pl.BlockSpec((ADA_ROWS, d), lambda l, j: (0, 0)), pl.BlockSpec((1, ADA_ROWS, tn), lambda l, j: (l, 0, j))],
        out_specs=pl.BlockSpec((1, d, tn), lambda l, j: (l, 0, j)),
        out_shape=jax.ShapeDtypeStruct((depth, d, n), F32),
        compiler_params=_params("parallel", "parallel"), name=name,
    )(c16, dmod16)


PACK_COLS = 1024
_ANY = pl.BlockSpec(memory_space=pl.ANY)
_VMEM = pl.BlockSpec(memory_space=pltpu.VMEM)


def _place():
    return lax.axis_index("x"), lax.axis_index("y"), lax.axis_index("c")


def _other_chips(x, y):
    return [(1 - x, y), (x, 1 - y), (1 - x, 1 - y)]


def _rcopy(src, dst, send_sems, recv_sems, k, peer):
    return pltpu.make_async_remote_copy(src_ref=src, dst_ref=dst, send_sem=send_sems.at[k], recv_sem=recv_sems.at[k],
                                        device_id=peer, device_id_type=MESH)


def _all_gather_8(v, name):
    r, cc = v.shape

    def body(v_ref, out_ref, send_sems, recv_sems, local_sem):
        x, y, c = _place()
        me = 4 * x + 2 * y + c
        mine = pltpu.make_async_copy(v_ref, out_ref.at[me], local_sem)
        mine.start()
        peers = []
        for k in range(1, N_DEV):
            px = 1 - x if k & 4 else x
            py = 1 - y if k & 2 else y
            pc = 1 - c if k & 1 else c
            peers.append((px, py, pc))
        sends = [_rcopy(v_ref, out_ref.at[me], send_sems, recv_sems, k, p) for k, p in enumerate(peers)]
        for cp in sends:
            cp.start()
        for k, (px, py, pc) in enumerate(peers):
            _rcopy(v_ref, out_ref.at[4 * px + 2 * py + pc], send_sems, recv_sems, k, (px, py, pc)).wait_recv()
        for cp in sends:
            cp.wait_send()
        mine.wait()

    return pl.pallas_call(
        body, in_specs=[_VMEM], out_specs=_VMEM, out_shape=jax.ShapeDtypeStruct((N_DEV, r, cc), v.dtype),
        scratch_shapes=[pltpu.SemaphoreType.DMA((N_DEV - 1,)), pltpu.SemaphoreType.DMA((N_DEV - 1,)), pltpu.SemaphoreType.DMA],
        compiler_params=pltpu.CompilerParams(vmem_limit_bytes=VMEM_LIMIT), name=name,
    )(v)


def _gather_chips(shards, name):
    n = len(shards)
    per = 2 * (N_CHIP - 1)

    def body(*refs):
        ins, outs, (send_sems, recv_sems) = refs[:n], refs[n:2 * n], refs[2 * n:]
        x, y, c = _place()
        chip = 2 * x + y
        chips = _other_chips(x, y)
        rows = [(pl.ds(c * (r.shape[0] // 2), r.shape[0] // 2), pl.ds((1 - c) * (r.shape[0] // 2), r.shape[0] // 2)) for r in ins]
        first = [_rcopy(ins[i].at[rows[i][0]], outs[i].at[chip, rows[i][0]], send_sems, recv_sems, per * i + j, (px, py, c))
                 for i in range(n) for j, (px, py) in enumerate(chips)]
        for cp in first:
            cp.start()
        passed = []
        for i in range(n):
            for j, (px, py) in enumerate(chips):
                landed = outs[i].at[2 * px + py, rows[i][0]]
                _rcopy(ins[i].at[rows[i][0]], landed, send_sems, recv_sems, per * i + j, (px, py, c)).wait_recv()
                fw = _rcopy(landed, landed, send_sems, recv_sems, per * i + N_CHIP - 1 + j, (x, y, 1 - c))
                fw.start()
                passed.append(fw)
        for i in range(n):
            for j, (px, py) in enumerate(chips):
                landed = outs[i].at[2 * px + py, rows[i][1]]
                _rcopy(landed, landed, send_sems, recv_sems, per * i + N_CHIP - 1 + j, (x, y, 1 - c)).wait_recv()
        for cp in first + passed:
            cp.wait_send()

    return pl.pallas_call(
        body, in_specs=[_ANY] * n, out_specs=[_ANY] * n,
        out_shape=[jax.ShapeDtypeStruct((N_CHIP,) + sh.shape, sh.dtype) for sh in shards],
        scratch_shapes=[pltpu.SemaphoreType.DMA((per * n,)), pltpu.SemaphoreType.DMA((per * n,))], name=name,
    )(*shards)


def _pair_exchange(gs, name):
    n = len(gs)

    def body(*refs):
        ins, outs, (send_sems, recv_sems) = refs[:n], refs[n:2 * n], refs[2 * n:]
        x, y, c = _place()
        copies = []
        for i in range(n):
            half = ins[i].shape[1] // 2
            copies.append(_rcopy(ins[i].at[:, pl.ds((1 - c) * half, half)], outs[i], send_sems, recv_sems, i, (x, y, 1 - c)))
        for cp in copies:
            cp.start()
        for cp in copies:
            cp.wait()

    return pl.pallas_call(
        body, in_specs=[_ANY] * n, out_specs=[_ANY] * n,
        out_shape=[jax.ShapeDtypeStruct((g.shape[0], g.shape[1] // 2, g.shape[2]), g.dtype) for g in gs],
        scratch_shapes=[pltpu.SemaphoreType.DMA((n,)), pltpu.SemaphoreType.DMA((n,))], name=name,
    )(*gs)


_ROW_TILES = (640, 512, 352, 256, 128, 64, 32, 16)


def _pair_sum(g, other, c_idx, name):
    n, half, cc = other.shape
    tr = _tile(half, _ROW_TILES)

    def body(c_ref, g_ref, o_ref, out_ref):
        out_ref[...] = (g_ref[...] + o_ref[...]).astype(out_ref.dtype)

    return pl.pallas_call(
        body,
        grid_spec=pltpu.PrefetchScalarGridSpec(
            num_scalar_prefetch=1, grid=(n, half // tr),
            in_specs=[pl.BlockSpec((None, None, tr, cc), lambda k, i, c_ref: (k, c_ref[0], i, 0)),
                      pl.BlockSpec((None, tr, cc), lambda k, i, c_ref: (k, i, 0))],
            out_specs=pl.BlockSpec((None, tr, cc), lambda k, i, c_ref: (k, i, 0))),
        out_shape=jax.ShapeDtypeStruct((n, half, cc), BF16),
        compiler_params=_params("parallel", "parallel"), name=name,
    )(c_idx, g.reshape(n, 2, half, cc), other)


def _chip_exchange(ps, name):
    n = len(ps)
    per = N_CHIP - 1

    def body(*refs):
        ins, outs, (send_sems, recv_sems) = refs[:n], refs[n:2 * n], refs[2 * n:]
        x, y, c = _place()
        chip = 2 * x + y
        chips = _other_chips(x, y)
        sends = [_rcopy(ins[i].at[2 * px + py], outs[i].at[chip], send_sems, recv_sems, per * i + j, (px, py, c))
                 for i in range(n) for j, (px, py) in enumerate(chips)]
        for cp in sends:
            cp.start()
        for i in range(n):
            for j, (px, py) in enumerate(chips):
                _rcopy(ins[i].at[chip], outs[i].at[2 * px + py], send_sems, recv_sems, per * i + j, (px, py, c)).wait_recv()
        for cp in sends:
            cp.wait_send()

    return pl.pallas_call(
        body, in_specs=[_ANY] * n, out_specs=[_ANY] * n, out_shape=[jax.ShapeDtypeStruct(p.shape, p.dtype) for p in ps],
        scratch_shapes=[pltpu.SemaphoreType.DMA((per * n,)), pltpu.SemaphoreType.DMA((per * n,))], name=name,
    )(*ps)


_HBM = pl.BlockSpec(memory_space=pltpu.HBM)
_SEM = pl.BlockSpec(memory_space=pltpu.SEMAPHORE)
_DATAFLOW = pltpu.SideEffectType.DATAFLOW_SIDE_EFFECTING


def _chip_copies(srcs, lands, send_sems, recv_sems, per_chip_src, arriving):
    x, y, c = _place()
    chip = 2 * x + y
    out = []
    for i, (src, land) in enumerate(zip(srcs, lands)):
        for j, (px, py) in enumerate(_other_chips(x, y)):
            there = 2 * px + py
            part = src.at[there] if per_chip_src else src
            out.append(_rcopy(part, land.at[there if arriving else chip], send_sems, recv_sems, (N_CHIP - 1) * i + j, (px, py, c)))
    return out


def _send_start(srcs, per_chip_src, name):
    n = len(srcs)
    n_sem = (N_CHIP - 1) * n
    lands = [lax.empty((N_CHIP,) + (s.shape[1:] if per_chip_src else s.shape), s.dtype) for s in srcs]

    def body(*refs):
        ins, zones, (send_sems, recv_sems) = refs[:n], refs[n:2 * n], refs[2 * n:2 * n + 2]
        for cp in _chip_copies(ins, zones, send_sems, recv_sems, per_chip_src, False):
            cp.start()
        refs[-1][...] = jnp.zeros_like(refs[-1])

    hbm = lambda a: pltpu.HBM(a.shape, a.dtype)
    outs = pl.pallas_call(
        body, name=name, in_specs=[_HBM] * (2 * n),
        out_shape=(pltpu.SemaphoreType.DMA((n_sem,)), pltpu.SemaphoreType.DMA((n_sem,)), *[hbm(a) for a in srcs],
                   *[hbm(a) for a in lands], jax.ShapeDtypeStruct((SUBLANES, LANES), F32)),
        out_specs=(_SEM, _SEM, *[_HBM] * (2 * n), _VMEM), input_output_aliases={i: 2 + i for i in range(2 * n)},
        compiler_params=pltpu.CompilerParams(has_side_effects=_DATAFLOW),
    )(*[pltpu.with_memory_space_constraint(a, pltpu.HBM) for a in list(srcs) + lands])
    return (outs[0], outs[1], list(outs[2:2 + n]), list(outs[2 + n:2 + 2 * n])), outs[-1][0, 0]


def _send_wait(state, after, per_chip_src, name):
    send_sems, recv_sems, srcs, lands = state
    n = len(srcs)

    def body(*refs):
        ins, zones, (send_s, recv_s) = refs[:n], refs[n:2 * n], refs[2 * n:2 * n + 2]
        for cp in _chip_copies(ins, zones, send_s, recv_s, per_chip_src, True):
            cp.wait_send()
            cp.wait_recv()

    hbm = lambda a: pltpu.HBM(a.shape, a.dtype)
    outs = pl.pallas_call(
        body, name=name, in_specs=[_HBM] * (2 * n) + [_SEM, _SEM, _ANY],
        out_shape=tuple(hbm(a) for a in srcs + lands), out_specs=tuple([_HBM] * (2 * n)),
        input_output_aliases={i: i for i in range(2 * n)},
        compiler_params=pltpu.CompilerParams(has_side_effects=_DATAFLOW),
    )(*srcs, *lands, send_sems, recv_sems, after)
    return list(outs[n:])


def _sum_lead(v, name):
    n, r, cc = v.shape
    tr = _tile(r, _ROW_TILES + (8,))

    def body(v_ref, o_ref):
        acc = v_ref[0].astype(F32)
        for k in range(1, n):
            acc = acc + v_ref[k].astype(F32)
        o_ref[...] = acc

    return pl.pallas_call(
        body, grid=(r // tr,), in_specs=[pl.BlockSpec((n, tr, cc), lambda i: (0, i, 0))],
        out_specs=pl.BlockSpec((tr, cc), lambda i: (i, 0)), out_shape=jax.ShapeDtypeStruct((r, cc), F32),
        compiler_params=_params("parallel"), name=name,
    )(v)


def _pair_share(reds, name):
    n = len(reds)

    def body(*refs):
        ins, outs, (send_sems, recv_sems) = refs[:n], refs[n:2 * n], refs[2 * n:]
        x, y, c = _place()
        copies = [_rcopy(ins[i], outs[i].at[c], send_sems, recv_sems, i, (x, y, 1 - c)) for i in range(n)]
        for cp in copies:
            cp.start()
        for i in range(n):
            _rcopy(ins[i], outs[i].at[1 - c], send_sems, recv_sems, i, (x, y, 1 - c)).wait_recv()
        for cp in copies:
            cp.wait_send()

    return pl.pallas_call(
        body, in_specs=[_ANY] * n, out_specs=[_ANY] * n, out_shape=[jax.ShapeDtypeStruct((2,) + r.shape, r.dtype) for r in reds],
        scratch_shapes=[pltpu.SemaphoreType.DMA((n,)), pltpu.SemaphoreType.DMA((n,))], name=name,
    )(*reds)


def _pack(arrs, rows_multiple, dtype):
    flat = jnp.concatenate([a.reshape(-1).astype(dtype) for a in arrs])
    unit = rows_multiple * PACK_COLS
    total = -(-flat.shape[0] // unit) * unit
    return jnp.pad(flat, (0, total - flat.shape[0])).reshape(-1, PACK_COLS)


def _unpack(buf, shapes):
    lead = buf.shape[:-2]
    flat = buf.reshape(*lead, -1)
    out, off = [], 0
    for shp in shapes:
        n = 1
        for s in shp:
            n *= s
        out.append(flat[..., off:off + n].reshape(*lead, *shp))
        off += n
    return out


def _join_shards(parts, axis):
    moved = jnp.moveaxis(parts, 0, axis)
    shp = list(moved.shape)
    shp[axis:axis + 2] = [shp[axis] * shp[axis + 1]]
    return moved.reshape(shp)


def _my_shard(full, axis, chip):
    n = full.shape[axis] // N_CHIP
    return lax.dynamic_slice_in_dim(full, chip * n, n, axis)


SMALL = {"norm_g": 2, "ffn_conv_w": 2, "rg_conv_w": 2, "gla_w_alpha": 2, "gla_b_alpha": 1, "gla_norm_g": 1,
         "ada_b": None, "ffn_conv_b": None, "rg_conv_b": None, "rg_ba": None, "rg_bx": None, "rg_lambda": None}
BIG = {"rg_w_in": True, "rg_wa": False, "rg_wx": False, "rg_w_out": False, "ffn_w_up": True, "ffn_w_down": False,
       "gla_w_in": True, "gla_w_out": False}
WEIGHTS = ["ada_w", "ada_b", "norm_g", "ffn_w_up", "ffn_conv_w", "ffn_conv_b", "ffn_w_down", "rg_w_in", "rg_conv_w", "rg_conv_b",
           "rg_wa", "rg_ba", "rg_wx", "rg_bx", "rg_lambda", "rg_w_out", "gla_w_in", "gla_w_alpha", "gla_b_alpha", "gla_norm_g",
           "gla_w_out"]


def kernel(x, c, ada_w, ada_b, norm_g, ffn_w_up, ffn_conv_w, ffn_conv_b, ffn_w_down, rg_w_in, rg_conv_w, rg_conv_b, rg_wa, rg_ba, rg_wx, rg_bx, rg_lambda, rg_w_out, gla_w_in, gla_w_alpha, gla_b_alpha, gla_norm_g, gla_w_out, loss_target, m_ada_w, m_ada_b, m_norm_g, m_ffn_w_up, m_ffn_conv_w, m_ffn_conv_b, m_ffn_w_down, m_rg_w_in, m_rg_conv_w, m_rg_conv_b, m_rg_wa, m_rg_ba, m_rg_wx, m_rg_bx, m_rg_lambda, m_rg_w_out, m_gla_w_in, m_gla_w_alpha, m_gla_b_alpha, m_gla_norm_g, m_gla_w_out, v_ada_w, v_ada_b, v_norm_g, v_ffn_w_up, v_ffn_conv_w, v_ffn_conv_b, v_ffn_w_down, v_rg_w_in, v_rg_conv_w, v_rg_conv_b, v_rg_wa, v_rg_ba, v_rg_wx, v_rg_bx, v_rg_lambda, v_rg_w_out, v_gla_w_in, v_gla_w_alpha, v_gla_b_alpha, v_gla_norm_g, v_gla_w_out):
    wts = dict(ada_w=ada_w, ada_b=ada_b, norm_g=norm_g, ffn_w_up=ffn_w_up, ffn_conv_w=ffn_conv_w, ffn_conv_b=ffn_conv_b,
               ffn_w_down=ffn_w_down, rg_w_in=rg_w_in, rg_conv_w=rg_conv_w, rg_conv_b=rg_conv_b, rg_wa=rg_wa, rg_ba=rg_ba,
               rg_wx=rg_wx, rg_bx=rg_bx, rg_lambda=rg_lambda, rg_w_out=rg_w_out, gla_w_in=gla_w_in, gla_w_alpha=gla_w_alpha,
               gla_b_alpha=gla_b_alpha, gla_norm_g=gla_norm_g, gla_w_out=gla_w_out)
    mom1 = dict(ada_w=m_ada_w, ada_b=m_ada_b, norm_g=m_norm_g, ffn_w_up=m_ffn_w_up, ffn_conv_w=m_ffn_conv_w,
                ffn_conv_b=m_ffn_conv_b, ffn_w_down=m_ffn_w_down, rg_w_in=m_rg_w_in, rg_conv_w=m_rg_conv_w,
                rg_conv_b=m_rg_conv_b, rg_wa=m_rg_wa, rg_ba=m_rg_ba, rg_wx=m_rg_wx, rg_bx=m_rg_bx, rg_lambda=m_rg_lambda,
                rg_w_out=m_rg_w_out, gla_w_in=m_gla_w_in, gla_w_alpha=m_gla_w_alpha, gla_b_alpha=m_gla_b_alpha,
                gla_norm_g=m_gla_norm_g, gla_w_out=m_gla_w_out)
    mom2 = dict(ada_w=v_ada_w, ada_b=v_ada_b, norm_g=v_norm_g, ffn_w_up=v_ffn_w_up, ffn_conv_w=v_ffn_conv_w,
                ffn_conv_b=v_ffn_conv_b, ffn_w_down=v_ffn_w_down, rg_w_in=v_rg_w_in, rg_conv_w=v_rg_conv_w,
                rg_conv_b=v_rg_conv_b, rg_wa=v_rg_wa, rg_ba=v_rg_ba, rg_wx=v_rg_wx, rg_bx=v_rg_bx, rg_lambda=v_rg_lambda,
                rg_w_out=v_rg_w_out, gla_w_in=v_gla_w_in, gla_w_alpha=v_gla_w_alpha, gla_b_alpha=v_gla_b_alpha,
                gla_norm_g=v_gla_norm_g, gla_w_out=v_gla_w_out)
    xi, yi, ci = _place()
    chip, me = 2 * xi + yi, 4 * xi + 2 * yi + ci
    d = x.shape[-1]
    depth = ada_w.shape[0]
    n_ada = ada_w.shape[-1]
    sharded_small = [k for k, ax in SMALL.items() if ax is not None]

    sm = _all_gather_8(_pack([c] + [wts[k] for k in sharded_small], SUBLANES, F32), "gather_small")
    c_all = sm[:, 0, :]
    parts = _unpack(sm[0::2], [c.shape] + [wts[k].shape for k in sharded_small])[1:]
    full = {k: _join_shards(p, SMALL[k]) for k, p in zip(sharded_small, parts)}
    for k, ax in SMALL.items():
        if ax is None:
            full[k] = wts[k]

    c16 = jnp.pad(c_all, ((0, ADA_ROWS - N_DEV), (0, 0)))
    ada_b_mine = lax.dynamic_slice_in_dim(ada_b, chip * n_ada, n_ada, 1)[:, None, :]
    mod_cols = _ada_fwd(c16, ada_w, ada_b_mine, "ada_fwd")
    mod_all = _all_gather_8(mod_cols.reshape(-1, PACK_COLS), "gather_mod")[0::2].reshape(N_CHIP, depth, ADA_ROWS, n_ada)
    mod = jnp.swapaxes(lax.dynamic_index_in_dim(mod_all, me, 2, keepdims=False), 0, 1).reshape(depth, 6, d)

    items = [(k, l) for k in BIG for l in range(wts[k].shape[0])]
    stage_of = lambda k, l: "rg" if k.startswith("rg_") else ("ffn0" if (k.startswith("ffn_") and l == 0) else "l1")
    staged = {st: [it for it in items if stage_of(*it) == st] for st in ("rg", "ffn0", "l1")}
    staged["l1"].sort(key=lambda it: not it[0].startswith("gla_"))
    shard = lambda k, l: wts[k][l].reshape(-1, wts[k].shape[-1]).astype(BF16)
    own = lambda got, mine: [lax.dynamic_update_index_in_dim(g, m, chip, 0) for g, m in zip(got, mine)]
    rows_joined = lambda v: v.reshape(-1, v.shape[-1])

    def placed(its, slots):
        out = {"ffn_w_up": {}, "ffn_w_down": {}}
        for (k, l), v in zip(its, slots):
            if k == "ffn_w_up":
                out[k][l] = v
            elif k == "ffn_w_down":
                out[k][l] = rows_joined(v)
            elif k in ("rg_wa", "rg_wx"):
                out[k] = _slots_to_block_rows(v, RG_BLOCKS)
            elif k == "gla_w_in":
                out[k] = _gla_head_cols(_from_col_slots(v))
            else:
                out[k] = v if BIG[k] else rows_joined(v)
        return out

    sh_rg = [shard(k, l) for k, l in staged["rg"]]
    local = {k: (v if k in ("norm_g", "ffn_conv_w", "ffn_conv_b") else v[0]) for k, v in full.items()}
    local["gla_wal"], local["gla_bal"] = _gla_alpha_heads(local["gla_w_alpha"], local["gla_b_alpha"])
    local.update(placed(staged["rg"], own(_gather_chips(sh_rg, "gather_weights_rg"), sh_rg)))
    sh_late, flying = {}, {}
    after_rg = (local["rg_w_out"][0, 0].astype(F32) * 0.0).astype(BF16)
    sh_late["ffn0"] = [shard(k, l) + after_rg for k, l in staged["ffn0"]]
    flying["ffn0"], tok = _send_start(sh_late["ffn0"], False, "weights_ffn0_start")
    sh_late["l1"] = [shard(k, l) + tok.astype(BF16) for k, l in staged["l1"]]
    flying["l1"], tok2 = _send_start(sh_late["l1"], False, "weights_l1_start")
    mod = mod + (tok + tok2)

    def fetch(stage, after):
        got = _send_wait(flying[stage], after, False, f"weights_{stage}_wait")
        return placed(staged[stage], own(got, sh_late[stage]))

    c_idx = ci.reshape(1).astype(jnp.int32)
    psums, sent = {}, {}

    def grad_slots(gr, k, l):
        g = gr[k][l] if k in ("ffn_w_up", "ffn_w_down") else gr[k]
        if k in ("rg_wa", "rg_wx"):
            return _block_rows_to_slots(g)
        if k == "gla_w_in":
            return _col_slots(g)
        return g if BIG[k] else g.reshape(N_CHIP, -1, g.shape[-1])

    def pair_sums(stage, gr):
        gslots = [grad_slots(gr, k, l) for k, l in staged[stage]]
        theirs = _pair_exchange(gslots, f"grads_{stage}_pair_exchange")
        psums[stage] = [_pair_sum(g, t, c_idx, f"grads_pair_sum_{k}{l}") for (k, l), g, t in zip(staged[stage], gslots, theirs)]

    def done(stage, gr):
        pair_sums(stage, gr)
        sent[stage], token = _send_start(psums[stage], True, f"grads_{stage}_start")
        return token

    cols, grad_x, gr = _local_step(x[0], loss_target[0], mod, local, fetch, done)
    loss = lax.psum(0.5 * jnp.sum(cols) / d, ("x", "y", "c"))

    small_names = [k for k in SMALL if k != "ada_b"]
    gs = _all_gather_8(_pack([gr[k] for k in small_names] + [gr["mod"]], SUBLANES, F32), "gather_small_grads")
    small_shapes = [full[k].shape for k in small_names] + [(depth, 6 * d)]
    *small_sum, g_ada_b = _unpack(_sum_lead(gs, "sum_small_grads"), small_shapes)
    grads = dict(zip(small_names, small_sum))
    grads["ada_b"] = g_ada_b
    for k in sharded_small:
        grads[k] = _my_shard(grads[k], SMALL[k], chip)
    dmod_all = _unpack(gs, small_shapes)[-1].reshape(N_DEV, depth, N_CHIP, n_ada)
    dmod_mine = jnp.swapaxes(lax.dynamic_index_in_dim(dmod_all, chip, 2, keepdims=False), 0, 1)
    g_ada_w = _ada_bwd(c16, jnp.pad(dmod_mine, ((0, 0), (0, ADA_ROWS - N_DEV), (0, 0))), "ada_bwd")

    pair_sums("rg", gr)
    arrived = {"rg": _chip_exchange(psums["rg"], "grads_rg_chip_exchange")}
    for stage in ("l1", "ffn0"):
        arrived[stage] = _send_wait(sent[stage], grad_x, True, f"grads_{stage}_wait")
    order = [(st, n) for st in ("rg", "ffn0", "l1") for n in range(len(staged[st]))]
    items = [staged[st][n] for st, n in order]
    mine_of = lambda p: lax.dynamic_index_in_dim(p, chip, 0, keepdims=False)
    halves = [_sum_lead(lax.dynamic_update_index_in_dim(arrived[st][n], mine_of(psums[st][n]), chip, 0),
                        "grads_chip_sum_%s%d" % staged[st][n]) for st, n in order]
    shared = _pair_share(halves, "grads_pair_share")
    reduced = [lax.dynamic_update_index_in_dim(s2, h, ci, 0).reshape(-1, h.shape[-1]) for s2, h in zip(shared, halves)]

    delta, new_m, new_v = {}, {}, {}

    def update(k, gs_k, view):
        shp = wts[k].shape
        outs = _adamw(view(wts[k]), gs_k, view(mom1[k]), view(mom2[k]), "adamw_" + k)
        grads[k], delta[k], new_m[k], new_v[k] = (o.reshape(shp) for o in outs)

    update("ada_w", [g_ada_w[l] for l in range(depth)], lambda a: a)
    for k in BIG:
        gs_k = [g for (kk, _), g in zip(items, reduced) if kk == k]
        update(k, gs_k, lambda a: a.reshape(a.shape[0], -1, a.shape[-1]))
    small_shard_shapes = [wts[k].shape for k in SMALL]
    packed = [_pack([src[k] for k in SMALL], SUBLANES, F32) for src in (wts, grads, mom1, mom2)]
    outs = _adamw(packed[0][None], [packed[1]], packed[2][None], packed[3][None], "adamw_small")
    for dst, o in zip((delta, new_m, new_v), outs[1:]):
        for k, a in zip(SMALL, _unpack(o[0], small_shard_shapes)):
            dst[k] = a

    return (loss, grad_x[None], *[grads[k] for k in WEIGHTS], *[delta[k] for k in WEIGHTS], *[new_m[k] for k in WEIGHTS],
            *[new_v[k] for k in WEIGHTS])
```

```python
import jax
import jax.numpy as jnp
from jax import lax
from jax.experimental import pallas as pl
from jax.experimental.pallas import tpu as pltpu

F32 = jnp.float32
BF16 = jnp.bfloat16
MXU_DTYPE = BF16

EPS = 1e-6
RG_C = 8.0
RG_BLOCKS = 4
RG_CONV = 4
GLA_HEADS = 4
GLA_TAU = 16.0
GLA_CHUNK = 64
GLA_RANK = 16
FFN_CONV = 3
ADAM_LR = 0.001
ADAM_B1 = 0.9
ADAM_B2 = 0.999
ADAM_EPS = 1e-08
ADAM_WD = 0.01
ADAM_STEP = 10

LANES = 128
SUBLANES = 8
VMEM_LIMIT = 56 * 1024 * 1024
CB = 256
MESH = pl.DeviceIdType.MESH
N_DEV = 8
N_CHIP = 4


def _params(*sem):
    return pltpu.CompilerParams(dimension_semantics=sem, vmem_limit_bytes=VMEM_LIMIT)


def _tile(dim, prefs):
    for p in prefs:
        if dim % p == 0:
            return p
    return dim


def _dot(a, b, dims):
    return lax.dot_general(a.astype(MXU_DTYPE), b.astype(MXU_DTYPE), (dims, ((), ())), preferred_element_type=F32)


def _dot_nn(a, b):
    return _dot(a, b, ((1,), (0,)))


def _dot_nt(a, b):
    return _dot(a, b, ((1,), (1,)))


def _dot_tn(a, b):
    return _dot(a, b, ((0,), (0,)))


def _mm(a, b, *, ta=False, tb=False, a_parts=1, b_parts=1, w_slots=1, out_slots=1, out_dtype=F32, name):
    if ta:
        k_dim, m_dim = a.shape
        n_dim = b.shape[-1] * b_parts
    else:
        m_dim, k_dim = a.shape[-2], a.shape[-1] * a_parts
        n_dim = b.shape[-2] if tb else b.shape[-1] * w_slots
    n_unit = n_dim // max(b_parts, out_slots, 1 if tb else w_slots)
    k_unit = k_dim // max(a_parts, w_slots if tb else 1)
    tm = _tile(m_dim, (1024, 1408, 512, 256, 128))
    tn = _tile(n_unit, (1024, 1408, 896, 512, 256, 128))
    tk = _tile(k_unit, (1024, 1408, 896, 512, 256, 128))
    nk = k_dim // tk
    dims = ((0 if ta else 1,), (1 if tb else 0,))

    def spec(shape, parts, total, tile, col_grid, row_grid):
        per = total // parts // tile

        def index(i, j, k):
            g = {"i": i, "j": j, "k": k}
            col, row = g[col_grid], g[row_grid]
            return (row, col) if parts == 1 else (col // per, row, col % per)

        return pl.BlockSpec(shape if parts == 1 else (None,) + shape, index)

    def body(a_ref, b_ref, o_ref, acc_ref):
        k = pl.program_id(2)

        @pl.when(k == 0)
        def _():
            acc_ref[...] = jnp.zeros_like(acc_ref)

        acc_ref[...] += _dot(a_ref[...], b_ref[...], dims)

        @pl.when(k == nk - 1)
        def _():
            o_ref[...] = acc_ref[...].astype(o_ref.dtype)

    if ta:
        a_spec = spec((tk, tm), 1, m_dim, tm, "i", "k")
        b_spec = spec((tk, tn), b_parts, n_dim, tn, "j", "k")
    elif tb:
        a_spec = spec((tm, tk), a_parts, k_dim, tk, "k", "i")
        b_spec = spec((tn, tk), w_slots, k_dim, tk, "k", "j")
    else:
        a_spec = spec((tm, tk), a_parts, k_dim, tk, "k", "i")
        b_spec = spec((tk, tn), w_slots, n_dim, tn, "j", "k")
    out_shape = (m_dim, n_dim) if out_slots == 1 else (out_slots, m_dim, n_dim // out_slots)
    return pl.pallas_call(
        body,
        grid=(m_dim // tm, n_dim // tn, nk),
        in_specs=[a_spec, b_spec],
        out_specs=spec((tm, tn), out_slots, n_dim, tn, "j", "i"),
        out_shape=jax.ShapeDtypeStruct(out_shape, out_dtype),
        scratch_shapes=[pltpu.VMEM((tm, tn), F32)],
        compiler_params=_params("parallel", "parallel", "arbitrary"),
        name=name,
    )(a, b)


def _row_specs(s, d, ts):
    return pl.BlockSpec((ts, d), lambda i: (i, 0)), pl.BlockSpec((1, d), lambda i: (0, 0))


def _norm_mod_fwd(x, g, sc, sh, name):
    s, d = x.shape
    ts = _tile(s, (512,))

    def body(x_ref, g_ref, sc_ref, sh_ref, h_ref):
        xv = x_ref[...]
        r = lax.rsqrt(jnp.mean(xv * xv, axis=-1, keepdims=True) + EPS)
        h_ref[...] = (((xv * r) * g_ref[...]) * (1.0 + sc_ref[...]) + sh_ref[...]).astype(h_ref.dtype)

    row, vec = _row_specs(s, d, ts)
    return pl.pallas_call(
        body, grid=(s // ts,), in_specs=[row, vec, vec, vec], out_specs=row,
        out_shape=jax.ShapeDtypeStruct((s, d), MXU_DTYPE), compiler_params=_params("parallel"), name=name,
    )(x, g, sc, sh)


def _norm_mod_bwd(dh, x, g, sc, dres, name):
    s, d = x.shape
    ts = _tile(s, (512,))

    def body(dh_ref, x_ref, g_ref, sc_ref, dres_ref, dx_ref, dg_ref, dsc_ref, dsh_ref, acc_ref):
        i = pl.program_id(0)

        @pl.when(i == 0)
        def _():
            acc_ref[...] = jnp.zeros_like(acc_ref)

        xv, dhv = x_ref[...], dh_ref[...]
        r = lax.rsqrt(jnp.mean(xv * xv, axis=-1, keepdims=True) + EPS)
        n = xv * r
        acc_ref[0:1, :] += jnp.sum(dhv * n, axis=0, keepdims=True)
        acc_ref[1:2, :] += jnp.sum(dhv, axis=0, keepdims=True)
        dn = dhv * ((1.0 + sc_ref[...]) * g_ref[...])
        dx_ref[...] = dres_ref[...] + r * (dn - n * jnp.mean(dn * n, axis=-1, keepdims=True))
        dg_ref[...] = (1.0 + sc_ref[...]) * acc_ref[0:1, :]
        dsc_ref[...] = g_ref[...] * acc_ref[0:1, :]
        dsh_ref[...] = acc_ref[1:2, :]

    row, vec = _row_specs(s, d, ts)
    vshape = jax.ShapeDtypeStruct((1, d), F32)
    return pl.pallas_call(
        body, grid=(s // ts,), in_specs=[row, row, vec, vec, row], out_specs=[row, vec, vec, vec],
        out_shape=[jax.ShapeDtypeStruct((s, d), F32), vshape, vshape, vshape],
        scratch_shapes=[pltpu.VMEM((SUBLANES, d), F32)], compiler_params=_params("arbitrary"), name=name,
    )(dh, x, g, sc, dres)


def _post_fwd(x, y, g, gt, name):
    s, d = x.shape
    ts = _tile(s, (512,))

    def body(x_ref, y_ref, g_ref, gt_ref, o_ref):
        yv = y_ref[...]
        r = lax.rsqrt(jnp.mean(yv * yv, axis=-1, keepdims=True) + EPS)
        o_ref[...] = x_ref[...] + gt_ref[...] * ((yv * r) * g_ref[...])

    row, vec = _row_specs(s, d, ts)
    return pl.pallas_call(
        body, grid=(s // ts,), in_specs=[row, row, vec, vec], out_specs=row,
        out_shape=jax.ShapeDtypeStruct((s, d), F32), compiler_params=_params("parallel"), name=name,
    )(x, y, g, gt)


def _post_bwd(dxn, y, g, gt, name):
    s, d = y.shape
    ts = _tile(s, (512,))

    def body(dxn_ref, y_ref, g_ref, gt_ref, dy_ref, dg_ref, dgt_ref, acc_ref):
        i = pl.program_id(0)

        @pl.when(i == 0)
        def _():
            acc_ref[...] = jnp.zeros_like(acc_ref)

        yv, dv = y_ref[...], dxn_ref[...]
        r = lax.rsqrt(jnp.mean(yv * yv, axis=-1, keepdims=True) + EPS)
        n = yv * r
        acc_ref[0:1, :] += jnp.sum(dv * n, axis=0, keepdims=True)
        dn = dv * (gt_ref[...] * g_ref[...])
        dy_ref[...] = (r * (dn - n * jnp.mean(dn * n, axis=-1, keepdims=True))).astype(dy_ref.dtype)
        dg_ref[...] = gt_ref[...] * acc_ref[0:1, :]
        dgt_ref[...] = g_ref[...] * acc_ref[0:1, :]

    row, vec = _row_specs(s, d, ts)
    vshape = jax.ShapeDtypeStruct((1, d), F32)
    return pl.pallas_call(
        body, grid=(s // ts,), in_specs=[row, row, vec, vec], out_specs=[row, vec, vec],
        out_shape=[jax.ShapeDtypeStruct((s, d), MXU_DTYPE), vshape, vshape],
        scratch_shapes=[pltpu.VMEM((SUBLANES, d), F32)], compiler_params=_params("arbitrary"), name=name,
    )(dxn, y, g, gt)


def _loss_grad(x, tgt, name):
    s, d = x.shape
    ts = _tile(s, (512,))

    def body(x_ref, t_ref, col_ref, dx_ref):
        i = pl.program_id(0)

        @pl.when(i == 0)
        def _():
            col_ref[...] = jnp.zeros_like(col_ref)

        e = x_ref[...] - t_ref[...]
        col_ref[...] += jnp.sum(e * e, axis=0, keepdims=True)
        dx_ref[...] = e * (1.0 / d)

    row, vec = _row_specs(s, d, ts)
    return pl.pallas_call(
        body, grid=(s // ts,), in_specs=[row, row], out_specs=[vec, row],
        out_shape=[jax.ShapeDtypeStruct((1, d), F32), jax.ShapeDtypeStruct((s, d), F32)],
        compiler_params=_params("arbitrary"), name=name,
    )(x, tgt)


_GELU_C = 0.7978845608028654
_GELU_A = 0.044715


def _gelu(x):
    t = jnp.tanh(_GELU_C * (x + _GELU_A * x * x * x))
    return 0.5 * x * (1.0 + t), t


def _gelu_grad(x, t):
    return 0.5 * (1.0 + t) + 0.5 * x * (1.0 - t * t) * (_GELU_C * (1.0 + 3.0 * _GELU_A * x * x))


def _sigmoid(x):
    return 1.0 / (1.0 + jnp.exp(-x))


def _log1p_pos(y):
    u = 1.0 + y
    return jnp.where(u == 1.0, y, jnp.log(u) * (y / jnp.where(u == 1.0, 1.0, u - 1.0)))


def _softplus(x):
    return jnp.maximum(x, 0.0) + _log1p_pos(jnp.exp(-jnp.abs(x)))


def _one_minus_exp(z):
    u = jnp.exp(z)
    lg = jnp.log(jnp.where(u > 0.0, u, 1.0))
    safe = (u != 1.0) & (u > 0.0)
    return jnp.where(u == 1.0, -z, jnp.where(u > 0.0, (1.0 - u) * (z / jnp.where(safe, lg, 1.0)), 1.0))


SLAB = 16


def _cat(a, b):
    return jnp.concatenate([a, b], axis=1)


def _fold8(x):
    out = x[0:SUBLANES]
    for r in range(SUBLANES, x.shape[0], SUBLANES):
        out = out + x[r:r + SUBLANES]
    return out


def _pair_specs(shape, nb, index):
    return [pl.BlockSpec(shape, lambda j, t: index(j, t) + (j,)), pl.BlockSpec(shape, lambda j, t: index(j, t) + (j + nb,))]


def _halo_row(ts, time_of):
    return lambda j, t: (jnp.maximum(time_of(t) * (ts // SUBLANES) - 1, 0),)


def _ffn_mid_fwd(p, cw, cb, name):
    s, f2 = p.shape
    ts = _tile(s, (512,))
    nb, nt = f2 // (2 * CB), s // ts

    def body(pg_ref, pv_ref, hg_ref, hv_ref, cwg_ref, cwv_ref, cbg_ref, cbv_ref, a_ref):
        t = pl.program_id(1)
        cwv, bias = _cat(cwg_ref[...], cwv_ref[...]), _cat(cbg_ref[...], cbv_ref[...])
        w0, w1, w2 = cwv[0:1], cwv[1:2], cwv[2:3]

        def slab(blk, r0):
            u = bias + w0 * blk[6:6 + SLAB] + w1 * blk[7:7 + SLAB] + w2 * blk[8:8 + SLAB]
            a_ref[pl.ds(r0, SLAB), :] = (_gelu(u[:, :CB])[0] * u[:, CB:]).astype(a_ref.dtype)

        halo = jnp.where(t > 0, _cat(hg_ref[...], hv_ref[...]), 0.0)
        slab(jnp.concatenate([halo, _cat(pg_ref[0:SLAB, :], pv_ref[0:SLAB, :])], axis=0), 0)

        def loop(i, carry):
            r0 = pl.multiple_of(i * SLAB, SLAB)
            rows = pl.ds(pl.multiple_of(r0 - SUBLANES, SUBLANES), SLAB + SUBLANES)
            slab(_cat(pg_ref[rows, :], pv_ref[rows, :]), r0)
            return carry

        lax.fori_loop(1, ts // SLAB, loop, 0, unroll=2)

    fwd = lambda t: t
    return pl.pallas_call(
        body, grid=(nb, nt),
        in_specs=(_pair_specs((ts, CB), nb, lambda j, t: (t,)) + _pair_specs((SUBLANES, CB), nb, _halo_row(ts, fwd))
                  + _pair_specs((FFN_CONV, CB), nb, lambda j, t: (0,)) + _pair_specs((1, CB), nb, lambda j, t: (0,))),
        out_specs=pl.BlockSpec((ts, CB), lambda j, t: (t, j)),
        out_shape=jax.ShapeDtypeStruct((s, f2 // 2), MXU_DTYPE),
        compiler_params=_params("parallel", "arbitrary"), name=name,
    )(p, p, p, p, cw, cw, cb, cb)


def _ffn_mid_bwd(da, p, cw, cb, name):
    s, f2 = p.shape
    ts = _tile(s, (512,))
    nb, nt = f2 // (2 * CB), s // ts
    n_slab = ts // SLAB

    def body(da_ref, pg_ref, pv_ref, hg_ref, hv_ref, cwg_ref, cwv_ref, cbg_ref, cbv_ref, dp_ref, dcw_ref, dcb_ref,
             next_du, acc):
        tt = pl.program_id(1)
        t = nt - 1 - tt
        cwv, bias = _cat(cwg_ref[...], cwv_ref[...]), _cat(cbg_ref[...], cbv_ref[...])
        w0, w1, w2 = cwv[0:1], cwv[1:2], cwv[2:3]

        @pl.when(tt == 0)
        def _():
            next_du[...] = jnp.zeros_like(next_du)
            acc[...] = jnp.zeros_like(acc)

        def slab(blk, r0, carry):
            pm2, pm1, p0 = blk[6:6 + SLAB], blk[7:7 + SLAB], blk[8:8 + SLAB]
            u = bias + w0 * pm2 + w1 * pm1 + w2 * p0
            g, v = u[:, :CB], u[:, CB:]
            gel, th = _gelu(g)
            dav = da_ref[pl.ds(r0, SLAB), :]
            du = _cat(dav * v * _gelu_grad(g, th), dav * gel)
            ext = jnp.concatenate([du, carry], axis=0)
            dpv = (w2 * du + w1 * ext[1:1 + SLAB] + w0 * ext[2:2 + SLAB]).astype(dp_ref.dtype)
            dp_ref[0, pl.ds(r0, SLAB), :] = dpv[:, :CB]
            dp_ref[1, pl.ds(r0, SLAB), :] = dpv[:, CB:]
            acc[0] += _fold8(du)
            acc[1] += _fold8(du * pm2)
            acc[2] += _fold8(du * pm1)
            acc[3] += _fold8(du * p0)
            return du[0:SUBLANES]

        def loop(k, carry):
            r0 = pl.multiple_of((n_slab - 1 - k) * SLAB, SLAB)
            rows = pl.ds(pl.multiple_of(r0 - SUBLANES, SUBLANES), SLAB + SUBLANES)
            return slab(_cat(pg_ref[rows, :], pv_ref[rows, :]), r0, carry)

        carry = lax.fori_loop(0, n_slab - 1, loop, next_du[...], unroll=2)
        halo = jnp.where(t > 0, _cat(hg_ref[...], hv_ref[...]), 0.0)
        next_du[...] = slab(jnp.concatenate([halo, _cat(pg_ref[0:SLAB, :], pv_ref[0:SLAB, :])], axis=0), 0, carry)

        @pl.when(tt == nt - 1)
        def _():
            for half in range(2):
                cols = slice(half * CB, (half + 1) * CB)
                dcb_ref[half] = jnp.sum(acc[0][:, cols], axis=0, keepdims=True)
                for k in range(FFN_CONV):
                    dcw_ref[half, k:k + 1, :] = jnp.sum(acc[1 + k][:, cols], axis=0, keepdims=True)

    rev = lambda t: nt - 1 - t
    return pl.pallas_call(
        body, grid=(nb, nt),
        in_specs=([pl.BlockSpec((ts, CB), lambda j, t: (rev(t), j))] + _pair_specs((ts, CB), nb, lambda j, t: (rev(t),))
                  + _pair_specs((SUBLANES, CB), nb, _halo_row(ts, rev)) + _pair_specs((FFN_CONV, CB), nb, lambda j, t: (0,))
                  + _pair_specs((1, CB), nb, lambda j, t: (0,))),
        out_specs=[pl.BlockSpec((2, ts, CB), lambda j, t: (0, rev(t), j)),
                   pl.BlockSpec((2, FFN_CONV, CB), lambda j, t: (0, 0, j)),
                   pl.BlockSpec((2, 1, CB), lambda j, t: (0, 0, j))],
        out_shape=[jax.ShapeDtypeStruct((2, s, f2 // 2), MXU_DTYPE), jax.ShapeDtypeStruct((2, FFN_CONV, f2 // 2), F32),
                   jax.ShapeDtypeStruct((2, 1, f2 // 2), F32)],
        scratch_shapes=[pltpu.VMEM((SUBLANES, 2 * CB), F32), pltpu.VMEM((1 + FFN_CONV, SUBLANES, 2 * CB), F32)],
        compiler_params=_params("parallel", "arbitrary"), name=name,
    )(da, p, p, p, p, cw, cw, cb, cb)


def _rg_gates(xc, wa_ref, ba_ref, wx_ref, bx_ref, lam_ref):
    r = _sigmoid(_dot_nn(xc, wa_ref[0]) + ba_ref[...])
    ig = _sigmoid(_dot_nn(xc, wx_ref[0]) + bx_ref[...])
    sp = _softplus(-lam_ref[...])
    log_a = (-RG_C) * r * sp
    a = jnp.exp(log_a)
    mult = jnp.sqrt(_one_minus_exp(2.0 * log_a))
    return r, ig, sp, a, mult


def _rg_conv(scr, cw_ref, cb_ref, ts):
    views = [scr[5 + k:5 + k + ts, :] for k in range(RG_CONV)]
    xc = cb_ref[...]
    for k in range(RG_CONV):
        xc = xc + cw_ref[k:k + 1, :] * views[k]
    return xc, views


def _rg_param_specs():
    vec = pl.BlockSpec((1, CB), lambda g, t: (0, g))
    mat = pl.BlockSpec((1, CB, CB), lambda g, t: (g, 0, 0))
    return [pl.BlockSpec((RG_CONV, CB), lambda g, t: (0, g)), vec, mat, vec, mat, vec, vec]


NSEG = SUBLANES
NQ = CB // LANES


def _lanes(q):
    return slice(q * LANES, (q + 1) * LANES)


def _seg_scan(a_scr, x_scr, loc_scr, dec_scr, ts, reverse):
    seg = ts // NSEG

    def step(k, carry):
        out = []
        rows = pl.ds(seg - 1 - k if reverse else k, NSEG, stride=seg)
        for q in range(NQ):
            st, dec = carry[q]
            a_q, x_q, loc_q, dec_q = a_scr.at[q], x_scr.at[q], loc_scr.at[q], dec_scr.at[q]
            av = a_q[rows, :]
            if reverse:
                loc_q[rows, :] = st
                dec_q[rows, :] = dec
                st = av * (x_q[rows, :] + st)
                dec = av * dec
            else:
                st = av * st + x_q[rows, :]
                dec = av * dec
                loc_q[rows, :] = st
                dec_q[rows, :] = dec
            out.append((st, dec))
        return tuple(out)

    init = tuple((jnp.zeros((NSEG, LANES), F32), jnp.ones((NSEG, LANES), F32)) for _ in range(NQ))
    return lax.fori_loop(0, seg, step, init, unroll=4)


def _seg_chain(fin, dec, c_in, reverse):
    rows = [None] * NSEG
    c = c_in
    for sgm in (reversed(range(NSEG)) if reverse else range(NSEG)):
        rows[sgm] = c
        c = fin[sgm:sgm + 1] + dec[sgm:sgm + 1] * c
    return jnp.concatenate(rows, axis=0), c


def _rg_mid_fwd(pj, cw, cb, wa, ba, wx, bx, lam, name):
    s = pj.shape[0]
    nb = pj.shape[1] // (2 * CB)
    ts = _tile(s, (512,))
    nt = s // ts
    seg = ts // NSEG

    def body(gate_ref, x_ref, halo_ref, cw_ref, cb_ref, wa_ref, ba_ref, wx_ref, bx_ref, lam_ref, y_ref, hs_ref,
             scr, a_scr, u_scr, loc_scr, dec_scr, h_scr):
        t = pl.program_id(1)

        @pl.when(t == 0)
        def _():
            h_scr[...] = jnp.zeros_like(h_scr)

        scr[0:SUBLANES, :] = jnp.where(t > 0, halo_ref[...], 0.0)
        scr[SUBLANES:, :] = x_ref[...]
        xc, _ = _rg_conv(scr, cw_ref, cb_ref, ts)
        _, ig, _, a, mult = _rg_gates(xc, wa_ref, ba_ref, wx_ref, bx_ref, lam_ref)
        u = mult * (ig * xc)
        for q in range(NQ):
            a_scr[q] = a[:, _lanes(q)]
            u_scr[q] = u[:, _lanes(q)]
        fin = _seg_scan(a_scr, u_scr, loc_scr, dec_scr, ts, False)
        for q in range(NQ):
            enter, leave = _seg_chain(fin[q][0], fin[q][1], h_scr[0:1, _lanes(q)], False)
            h_scr[0:1, _lanes(q)] = leave
            for sgm in range(NSEG):
                rows = slice(sgm * seg, (sgm + 1) * seg)
                hs_ref[rows, _lanes(q)] = loc_scr[q, rows, :] + dec_scr[q, rows, :] * enter[sgm:sgm + 1]
        y_ref[...] = (_gelu(gate_ref[...])[0] * hs_ref[...]).astype(y_ref.dtype)

    blk = pl.BlockSpec((ts, CB), lambda g, t: (t, g))
    lane_scr = pltpu.VMEM((NQ, ts, LANES), F32)
    return pl.pallas_call(
        body, grid=(nb, nt),
        in_specs=_pair_specs((ts, CB), nb, lambda g, t: (t,))
        + [pl.BlockSpec((SUBLANES, CB), lambda g, t: _halo_row(ts, lambda u: u)(g, t) + (g + nb,))] + _rg_param_specs(),
        out_specs=[blk, blk],
        out_shape=[jax.ShapeDtypeStruct((s, nb * CB), MXU_DTYPE), jax.ShapeDtypeStruct((s, nb * CB), F32)],
        scratch_shapes=[pltpu.VMEM((ts + SUBLANES, CB), F32), lane_scr, lane_scr, lane_scr, lane_scr,
                        pltpu.VMEM((SUBLANES, CB), F32)],
        compiler_params=_params("parallel", "arbitrary"), name=name,
    )(pj, pj, pj, cw, cb, wa, ba, wx, bx, lam)


def _rg_mid_bwd(dy, pj, hs, cw, cb, wa, ba, wx, bx, lam, name):
    s = pj.shape[0]
    nb = pj.shape[1] // (2 * CB)
    ts = _tile(s, (512,))
    nt = s // ts

    def body(dy_ref, gate_ref, x_ref, halo_ref, hs_ref, hsh_ref, cw_ref, cb_ref, wa_ref, ba_ref, wx_ref, bx_ref, lam_ref,
             dpj_ref, dcw_ref, dcb_ref, dwa_ref, dba_ref, dwx_ref, dbx_ref, dlam_ref,
             scr, hscr, a_scr, d_scr, loc_scr, dec_scr, g_scr, dxscr, c_scr):
        tt = pl.program_id(1)
        t = nt - 1 - tt
        seg = ts // NSEG

        @pl.when(tt == 0)
        def _():
            c_scr[...] = jnp.zeros_like(c_scr)
            dxscr[ts:, :] = jnp.zeros((SUBLANES, CB), F32)
            for ref in (dcw_ref, dcb_ref, dwa_ref, dba_ref, dwx_ref, dbx_ref, dlam_ref):
                ref[...] = jnp.zeros_like(ref)

        scr[0:SUBLANES, :] = jnp.where(t > 0, halo_ref[...], 0.0)
        scr[SUBLANES:, :] = x_ref[...]
        hscr[0:SUBLANES, :] = jnp.where(t > 0, hsh_ref[...], 0.0)
        hscr[SUBLANES:, :] = hs_ref[...]
        xc, views = _rg_conv(scr, cw_ref, cb_ref, ts)
        r, ig, sp, a, mult = _rg_gates(xc, wa_ref, ba_ref, wx_ref, bx_ref, lam_ref)
        gate = gate_ref[...]
        gel, th = _gelu(gate)
        dyv = dy_ref[...]
        dpj_ref[0] = (dyv * hs_ref[...] * _gelu_grad(gate, th)).astype(dpj_ref.dtype)
        dhs = dyv * gel
        for q in range(NQ):
            a_scr[q] = a[:, _lanes(q)]
            d_scr[q] = dhs[:, _lanes(q)]
        fin = _seg_scan(a_scr, d_scr, loc_scr, dec_scr, ts, True)
        for q in range(NQ):
            enter, leave = _seg_chain(fin[q][0], fin[q][1], c_scr[0:1, _lanes(q)], True)
            c_scr[0:1, _lanes(q)] = leave
            for sgm in range(NSEG):
                rows = slice(sgm * seg, (sgm + 1) * seg)
                g_scr[rows, _lanes(q)] = d_scr[q, rows, :] + loc_scr[q, rows, :] + dec_scr[q, rows, :] * enter[sgm:sgm + 1]
        du = g_scr[...]
        da = du * hscr[7:7 + ts, :]
        dmult = du * (ig * xc)
        dig = du * (mult * xc)
        dxc = du * (mult * ig)
        dlog_a = da * a - dmult * (a * a / mult)
        dlam_ref[...] += jnp.sum(dlog_a * r, axis=0, keepdims=True) * (RG_C * _sigmoid(-lam_ref[...]))
        dpr = dlog_a * ((-RG_C) * sp) * (r * (1.0 - r))
        dpi = dig * (ig * (1.0 - ig))
        dba_ref[...] += jnp.sum(dpr, axis=0, keepdims=True)
        dbx_ref[...] += jnp.sum(dpi, axis=0, keepdims=True)
        dwa_ref[0] += _dot_tn(xc, dpr)
        dwx_ref[0] += _dot_tn(xc, dpi)
        dxc = dxc + _dot_nt(dpr, wa_ref[0]) + _dot_nt(dpi, wx_ref[0])
        dcb_ref[...] += jnp.sum(dxc, axis=0, keepdims=True)
        for k in range(RG_CONV):
            dcw_ref[k:k + 1, :] += jnp.sum(dxc * views[k], axis=0, keepdims=True)
        dxscr[0:ts, :] = dxc
        dxp = cw_ref[3:4, :] * dxc
        for k in range(RG_CONV - 1):
            dxp = dxp + cw_ref[k:k + 1, :] * dxscr[3 - k:3 - k + ts, :]
        dpj_ref[1] = dxp.astype(dpj_ref.dtype)
        dxscr[ts:, :] = dxscr[0:SUBLANES, :]

    rev = lambda g, t: (nt - 1 - t, g)
    rev_halo = lambda g, t: (jnp.maximum((nt - 1 - t) * (ts // SUBLANES) - 1, 0), g)
    vec = pl.BlockSpec((1, CB), lambda g, t: (0, g))
    mat = pl.BlockSpec((1, CB, CB), lambda g, t: (g, 0, 0))
    d = nb * CB
    vshape = jax.ShapeDtypeStruct((1, d), F32)
    mshape = jax.ShapeDtypeStruct((nb, CB, CB), F32)
    return pl.pallas_call(
        body, grid=(nb, nt),
        in_specs=[pl.BlockSpec((ts, CB), rev)] + _pair_specs((ts, CB), nb, lambda g, t: (nt - 1 - t,))
        + [pl.BlockSpec((SUBLANES, CB), lambda g, t: (rev_halo(g, t)[0], g + nb)),
           pl.BlockSpec((ts, CB), rev), pl.BlockSpec((SUBLANES, CB), rev_halo)] + _rg_param_specs(),
        out_specs=[pl.BlockSpec((2, ts, CB), lambda g, t: (0, nt - 1 - t, g)), pl.BlockSpec((RG_CONV, CB), lambda g, t: (0, g)),
                   vec, mat, vec, mat, vec, vec],
        out_shape=[jax.ShapeDtypeStruct((2, s, d), MXU_DTYPE), jax.ShapeDtypeStruct((RG_CONV, d), F32), vshape, mshape, vshape,
                   mshape, vshape, vshape],
        scratch_shapes=[pltpu.VMEM((ts + SUBLANES, CB), F32), pltpu.VMEM((ts + SUBLANES, CB), F32)]
        + [pltpu.VMEM((NQ, ts, LANES), F32)] * 4
        + [pltpu.VMEM((ts, CB), F32), pltpu.VMEM((ts + SUBLANES, CB), F32), pltpu.VMEM((SUBLANES, CB), F32)],
        compiler_params=_params("parallel", "arbitrary"), name=name,
    )(dy, pj, pj, pj, hs, hs, cw, cb, wa, ba, wx, bx, lam)


GLA_DK = 128
GLA_DV = 256
GLA_HB = 2 * GLA_DK + 2 * GLA_DV + LANES
GLA_TS = 256


def _split3(x):
    hi = x.astype(BF16)
    r1 = x - hi.astype(F32)
    mid = r1.astype(BF16)
    lo = (r1 - mid.astype(F32)).astype(BF16)
    return hi, mid, lo


def _chunk_cumsum(x, reverse):
    n = x.shape[0]
    i = lax.broadcasted_iota(jnp.int32, (n, n), 0)
    j = lax.broadcasted_iota(jnp.int32, (n, n), 1)
    same = (i // GLA_CHUNK) == (j // GLA_CHUNK)
    tri = jnp.where(same & ((j >= i) if reverse else (j <= i)), 1.0, 0.0).astype(BF16)
    out = jnp.zeros(x.shape, F32)
    for piece in _split3(x):
        out = out + lax.dot_general(tri, piece, (((1,), (0,)), ((), ())), preferred_element_type=F32)
    return out


def _gla_split(blk):
    q = blk[:, 0:GLA_DK] * (GLA_DK ** -0.5)
    k = blk[:, GLA_DK:2 * GLA_DK]
    v = blk[:, 2 * GLA_DK:2 * GLA_DK + GLA_DV]
    r = blk[:, 2 * GLA_DK + GLA_DV:2 * GLA_DK + 2 * GLA_DV]
    z = blk[:, 2 * GLA_DK + 2 * GLA_DV:]
    return q, k, v, r, z


def _gla_decays(gc):
    gref = gc[GLA_CHUNK // 2:GLA_CHUNK // 2 + 1, :]
    glast = gc[GLA_CHUNK - 1:GLA_CHUNK, :]
    return jnp.exp(gc), jnp.exp(gc - gref), jnp.exp(gref - gc), jnp.exp(glast - gc), jnp.exp(glast)


def _causal_mask():
    i = lax.broadcasted_iota(jnp.int32, (GLA_CHUNK, GLA_CHUNK), 0)
    j = lax.broadcasted_iota(jnp.int32, (GLA_CHUNK, GLA_CHUNK), 1)
    return j <= i


def _log_sigmoid(x):
    return jnp.minimum(x, 0.0) - _log1p_pos(jnp.exp(-jnp.abs(x)))


def _gla_mid_fwd(pj, wal, bal, ng, name):
    s = pj.shape[0]
    nh = pj.shape[1] // GLA_HB
    ts = _tile(s, (GLA_TS,))
    nt, nc = s // ts, ts // GLA_CHUNK

    def body(pj_ref, wal_ref, bal_ref, ng_ref, act_ref, o_ref, st_ref, s_scr):
        t = pl.program_id(0)

        @pl.when(t == 0)
        def _():
            s_scr[...] = jnp.zeros_like(s_scr)

        heads = []
        for h in range(nh):
            q, k, v, r, z = _gla_split(pj_ref[:, h * GLA_HB:(h + 1) * GLA_HB])
            g = _log_sigmoid(_dot_nn(z, wal_ref[h]) + bal_ref[h]) * (1.0 / GLA_TAU)
            heads.append((q, k, v, r, _chunk_cumsum(g, False)))
        mask = _causal_mask()
        for c in range(nc):
            sl = slice(c * GLA_CHUNK, (c + 1) * GLA_CHUNK)
            for h, (q, k, v, r, gcum) in enumerate(heads):
                eg, eq, ek, ekd, egl = _gla_decays(gcum[sl])
                st = s_scr[h]
                st_ref[c, h] = st
                attn = jnp.where(mask, _dot_nt(q[sl] * eq, k[sl] * ek), 0.0)
                o_ref[sl, h * GLA_DV:(h + 1) * GLA_DV] = _dot_nt(q[sl] * eg, st) + _dot_nn(attn, v[sl])
                s_scr[h] = st * egl + _dot_tn(v[sl], k[sl] * ekd)
        for h, (q, k, v, r, gcum) in enumerate(heads):
            cols = slice(h * GLA_DV, (h + 1) * GLA_DV)
            o = o_ref[:, cols]
            on = o * lax.rsqrt(jnp.mean(o * o, axis=-1, keepdims=True) + EPS)
            act_ref[:, cols] = ((on * ng_ref[...]) * (r * _sigmoid(r))).astype(act_ref.dtype)

    blk = pl.BlockSpec((ts, nh * GLA_DV), lambda t: (t, 0))
    whole = lambda shape: pl.BlockSpec(shape, lambda t: (0,) * len(shape))
    return pl.pallas_call(
        body, grid=(nt,),
        in_specs=[pl.BlockSpec((ts, nh * GLA_HB), lambda t: (t, 0)), whole((nh, LANES, GLA_DK)), whole((nh, 1, GLA_DK)),
                  whole((1, GLA_DV))],
        out_specs=[blk, blk, pl.BlockSpec((nc, nh, GLA_DV, GLA_DK), lambda t: (t, 0, 0, 0))],
        out_shape=[jax.ShapeDtypeStruct((s, nh * GLA_DV), MXU_DTYPE), jax.ShapeDtypeStruct((s, nh * GLA_DV), F32),
                   jax.ShapeDtypeStruct((s // GLA_CHUNK, nh, GLA_DV, GLA_DK), F32)],
        scratch_shapes=[pltpu.VMEM((nh, GLA_DV, GLA_DK), F32)],
        compiler_params=_params("arbitrary"), name=name,
    )(pj, wal, bal, ng)


def _gla_mid_bwd(dact, pj, o, st, wal, bal, ng, name):
    s = pj.shape[0]
    nh = pj.shape[1] // GLA_HB
    ts = _tile(s, (GLA_TS,))
    nt, nc = s // ts, ts // GLA_CHUNK
    o_q, o_k, o_v, o_r, o_z = 0, GLA_DK, 2 * GLA_DK, 2 * GLA_DK + GLA_DV, 2 * GLA_DK + 2 * GLA_DV

    def body(dact_ref, pj_ref, o_ref, st_ref, wal_ref, bal_ref, ng_ref, dpj_ref, dwal_ref, dbal_ref, dng_ref,
             ds_scr, dg_scr):
        tt = pl.program_id(0)

        @pl.when(tt == 0)
        def _():
            ds_scr[...] = jnp.zeros_like(ds_scr)
            dwal_ref[...] = jnp.zeros_like(dwal_ref)
            dbal_ref[...] = jnp.zeros_like(dbal_ref)
            dng_ref[...] = jnp.zeros_like(dng_ref)

        heads = []
        for h in range(nh):
            base = h * GLA_HB
            q, k, v, r, z = _gla_split(pj_ref[:, base:base + GLA_HB])
            logit = _dot_nn(z, wal_ref[h]) + bal_ref[h]
            gcum = _chunk_cumsum(_log_sigmoid(logit) * (1.0 / GLA_TAU), False)
            ov = o_ref[:, h * GLA_DV:(h + 1) * GLA_DV]
            ro = lax.rsqrt(jnp.mean(ov * ov, axis=-1, keepdims=True) + EPS)
            on = ov * ro
            sg = _sigmoid(r)
            sil = r * sg
            dav = dact_ref[:, h * GLA_DV:(h + 1) * GLA_DV]
            dpj_ref[:, base + o_r:base + o_z] = (dav * (on * ng_ref[...]) * (sg + sil * (1.0 - sg))).astype(dpj_ref.dtype)
            t1 = dav * sil
            dng_ref[...] += jnp.sum(t1 * on, axis=0, keepdims=True)
            dn = t1 * ng_ref[...]
            do = ro * (dn - on * jnp.mean(dn * on, axis=-1, keepdims=True))
            heads.append((q, k, v, z, logit, gcum, do))
        mask = _causal_mask()
        scale = GLA_DK ** -0.5
        last_row = lax.broadcasted_iota(jnp.int32, (GLA_CHUNK, GLA_DK), 0) == GLA_CHUNK - 1
        for c in reversed(range(nc)):
            sl = slice(c * GLA_CHUNK, (c + 1) * GLA_CHUNK)
            for h, (q, k, v, z, logit, gcum, do) in enumerate(heads):
                base = h * GLA_HB
                eg, eq, ek, ekd, egl = _gla_decays(gcum[sl])
                qc, kc, vc, doc = q[sl], k[sl], v[sl], do[sl]
                qg, qt, kt, kd = qc * eg, qc * eq, kc * ek, kc * ekd
                sp = st_ref[c, h]
                ds = ds_scr[h]
                attn = jnp.where(mask, _dot_nt(qt, kt), 0.0)
                dattn = jnp.where(mask, _dot_nt(doc, vc), 0.0)
                dqg = _dot_nn(doc, sp)
                dqt = _dot_nn(dattn, kt)
                dkt = _dot_tn(dattn, qt)
                dkd = _dot_nn(vc, ds)
                dpj_ref[sl, base + o_v:base + o_r] = (_dot_tn(attn, doc) + _dot_nt(kd, ds)).astype(dpj_ref.dtype)
                dpj_ref[sl, base + o_q:base + o_k] = (scale * (dqg * eg + dqt * eq)).astype(dpj_ref.dtype)
                dpj_ref[sl, base + o_k:base + o_v] = (dkt * ek + dkd * ekd).astype(dpj_ref.dtype)
                kdd = dkd * kd
                dgl = jnp.sum(kdd, axis=0, keepdims=True) + jnp.sum(ds * sp, axis=0, keepdims=True) * egl
                dg_scr[h, sl, :] = dqg * qg + dqt * qt - dkt * kt - kdd + jnp.where(last_row, dgl, 0.0)
                ds_scr[h] = ds * egl + _dot_tn(doc, qg)
        for h, (q, k, v, z, logit, gcum, do) in enumerate(heads):
            base = h * GLA_HB
            dlogit = _chunk_cumsum(dg_scr[h], True) * (1.0 / GLA_TAU) * _sigmoid(-logit)
            dpj_ref[:, base + o_z:base + GLA_HB] = _dot_nt(dlogit, wal_ref[h]).astype(dpj_ref.dtype)
            dwal_ref[h] += _dot_tn(z, dlogit)
            dbal_ref[h] += jnp.sum(dlogit, axis=0, keepdims=True)

    rev = lambda t: (nt - 1 - t, 0)
    whole = lambda shape: pl.BlockSpec(shape, lambda t: (0,) * len(shape))
    wide = pl.BlockSpec((ts, nh * GLA_DV), rev)
    return pl.pallas_call(
        body, grid=(nt,),
        in_specs=[wide, pl.BlockSpec((ts, nh * GLA_HB), rev), wide,
                  pl.BlockSpec((nc, nh, GLA_DV, GLA_DK), lambda t: (nt - 1 - t, 0, 0, 0)),
                  whole((nh, LANES, GLA_DK)), whole((nh, 1, GLA_DK)), whole((1, GLA_DV))],
        out_specs=[pl.BlockSpec((ts, nh * GLA_HB), rev), whole((nh, LANES, GLA_DK)), whole((nh, 1, GLA_DK)), whole((1, GLA_DV))],
        out_shape=[jax.ShapeDtypeStruct((s, nh * GLA_HB), MXU_DTYPE), jax.ShapeDtypeStruct((nh, LANES, GLA_DK), F32),
                   jax.ShapeDtypeStruct((nh, 1, GLA_DK), F32), jax.ShapeDtypeStruct((1, GLA_DV), F32)],
        scratch_shapes=[pltpu.VMEM((nh, GLA_DV, GLA_DK), F32), pltpu.VMEM((nh, ts, GLA_DK), F32)],
        compiler_params=_params("arbitrary"), name=name,
    )(dact, pj, o, st, wal, bal, ng)


def _adamw(w, gs, m, v, name):
    layers, rows, cols = w.shape
    tr = _tile(rows, (256, 128, 64, 32, 16, 8))
    c1 = 1.0 / (1.0 - ADAM_B1 ** ADAM_STEP)
    c2 = 1.0 / (1.0 - ADAM_B2 ** ADAM_STEP)

    def body(*refs):
        g_refs, (w_ref, m_ref, v_ref, go_ref, d_ref, mo_ref, vo_ref) = refs[:layers], refs[layers:]
        gv = g_refs[0][...]
        for l in range(1, layers):
            gv = jnp.where(pl.program_id(0) == l, g_refs[l][...], gv)
        m2 = ADAM_B1 * m_ref[...] + (1.0 - ADAM_B1) * gv
        v2 = ADAM_B2 * v_ref[...] + (1.0 - ADAM_B2) * (gv * gv)
        d_ref[...] = (-ADAM_LR) * ((m2 * c1) / (jnp.sqrt(v2 * c2) + ADAM_EPS) + ADAM_WD * w_ref[...])
        go_ref[...] = gv
        mo_ref[...] = m2
        vo_ref[...] = v2

    g_spec = pl.BlockSpec((tr, cols), lambda l, i: (i, 0))
    spec = pl.BlockSpec((None, tr, cols), lambda l, i: (l, i, 0))
    shape = jax.ShapeDtypeStruct((layers, rows, cols), F32)
    return pl.pallas_call(
        body, grid=(layers, rows // tr), in_specs=[g_spec] * layers + [spec] * 3, out_specs=[spec] * 4, out_shape=[shape] * 4,
        compiler_params=_params("parallel", "parallel"), name=name,
    )(*gs, w, m, v)


def _gla_head_cols(w):
    d = w.shape[0]
    qk, dv = GLA_HEADS * GLA_DK, GLA_HEADS * GLA_DV
    q, k, v, r, z = jnp.split(w, [qk, 2 * qk, 2 * qk + dv, 2 * qk + 2 * dv], axis=1)
    zp = jnp.pad(z, ((0, 0), (0, LANES - GLA_RANK)))
    parts = [q.reshape(d, GLA_HEADS, GLA_DK), k.reshape(d, GLA_HEADS, GLA_DK), v.reshape(d, GLA_HEADS, GLA_DV),
             r.reshape(d, GLA_HEADS, GLA_DV), jnp.broadcast_to(zp[:, None, :], (d, GLA_HEADS, LANES))]
    return jnp.concatenate(parts, axis=2).reshape(d, GLA_HEADS * GLA_HB)


def _gla_unhead_cols(w):
    d = w.shape[0]
    w = w.reshape(d, GLA_HEADS, GLA_HB)
    o = 2 * GLA_DK + 2 * GLA_DV
    parts = [w[:, :, 0:GLA_DK].reshape(d, -1), w[:, :, GLA_DK:2 * GLA_DK].reshape(d, -1),
             w[:, :, 2 * GLA_DK:2 * GLA_DK + GLA_DV].reshape(d, -1), w[:, :, 2 * GLA_DK + GLA_DV:o].reshape(d, -1),
             jnp.sum(w[:, :, o:o + GLA_RANK], axis=1)]
    return jnp.concatenate(parts, axis=1)


def _gla_alpha_heads(w_alpha, b_alpha):
    wal = jnp.swapaxes(w_alpha.reshape(GLA_RANK, GLA_HEADS, GLA_DK), 0, 1)
    return jnp.pad(wal, ((0, 0), (0, LANES - GLA_RANK), (0, 0))), b_alpha.reshape(GLA_HEADS, 1, GLA_DK)


def _gla_layouts(w):
    w = dict(w)
    w["gla_wal"], w["gla_bal"] = _gla_alpha_heads(w["gla_w_alpha"], w["gla_b_alpha"])
    w["gla_w_in"] = _gla_head_cols(w["gla_w_in"])
    return w


def _col_slots(w):
    r, c = w.shape
    return jnp.moveaxis(w.reshape(r, N_CHIP, c // N_CHIP), 1, 0)


def _from_col_slots(w):
    n, r, c = w.shape
    return jnp.moveaxis(w, 0, 1).reshape(r, n * c)


def _block_rows_to_slots(w):
    g, r4, cc = w.shape
    return jnp.swapaxes(w.reshape(g, N_CHIP, r4 // N_CHIP, cc), 0, 1).reshape(N_CHIP, g * (r4 // N_CHIP), cc)


def _slots_to_block_rows(w, g):
    n, gr, cc = w.shape
    return jnp.swapaxes(w.reshape(n, g, gr // g, cc), 0, 1).reshape(g, n * (gr // g), cc)


def _local_step(x, tgt, mod, w, fetch=None, done=None):
    depth = mod.shape[0]
    row = lambda v: v.reshape(1, -1)
    w = dict(w)
    w["ffn_w_up"], w["ffn_w_down"] = dict(enumerate(w["ffn_w_up"])), dict(enumerate(w["ffn_w_down"]))

    def arrive(stage, after):
        if fetch is not None:
            for k, v in fetch(stage, after).items():
                if isinstance(v, dict):
                    w[k].update(v)
                else:
                    w[k] = v

    saved = []
    for i in range(depth):
        if i == 1:
            arrive("l1", x)
        sh_m, sc_m, gt_m, sh_f, sc_f, gt_f = (mod[i, j:j + 1] for j in range(6))
        g0, g1, g2, g3 = (w["norm_g"][i, j:j + 1] for j in range(4))
        tag = f"_l{i}"
        h = _norm_mod_fwd(x, g0, sc_m, sh_m, "norm_mix" + tag)
        if i % 2 == 0:
            pj = _mm(h, w["rg_w_in"], w_slots=N_CHIP, name="rg_in" + tag)
            act, aux = _rg_mid_fwd(pj, w["rg_conv_w"], row(w["rg_conv_b"]), w["rg_wa"], row(w["rg_ba"]), w["rg_wx"],
                                   row(w["rg_bx"]), row(w["rg_lambda"]), "rg_mid" + tag)
            y = _mm(act, w["rg_w_out"], name="rg_out" + tag)
        else:
            pj = _mm(h, w["gla_w_in"], name="gla_in" + tag)
            act, *aux = _gla_mid_fwd(pj, w["gla_wal"], w["gla_bal"], row(w["gla_norm_g"]), "gla_mid" + tag)
            y = _mm(act, w["gla_w_out"], name="gla_out" + tag)
        x1 = _post_fwd(x, y, g1, gt_m, "post_mix" + tag)
        if i == 0:
            arrive("ffn0", x1)
        h2 = _norm_mod_fwd(x1, g2, sc_f, sh_f, "norm_ffn" + tag)
        p = _mm(h2, w["ffn_w_up"][i], w_slots=N_CHIP, name="ffn_up" + tag)
        a = _ffn_mid_fwd(p, w["ffn_conv_w"][i], w["ffn_conv_b"][i:i + 1], "ffn_mid" + tag)
        y2 = _mm(a, w["ffn_w_down"][i], name="ffn_down" + tag)
        x2 = _post_fwd(x1, y2, g3, gt_f, "post_ffn" + tag)
        saved.append((x, h, pj, act, aux, y, x1, h2, p, a, y2))
        x = x2

    cols, dx = _loss_grad(x, tgt, "loss")

    stacked = ("norm_g", "ffn_conv_w", "ffn_conv_b", "mod")
    gr = {k: [None] * depth for k in stacked + ("ffn_w_up", "ffn_w_down")}
    told = lambda stage: done(stage, gr) if done is not None else 0.0
    for i in reversed(range(depth)):
        x0, h, pj, act, aux, y, x1, h2, p, a, y2 = saved[i]
        sh_m, sc_m, gt_m, sh_f, sc_f, gt_f = (mod[i, j:j + 1] for j in range(6))
        g0, g1, g2, g3 = (w["norm_g"][i, j:j + 1] for j in range(4))
        tag = f"_l{i}"
        dy2, d_g3, d_gt_f = _post_bwd(dx, y2, g3, gt_f, "post_ffn_b" + tag)
        da = _mm(dy2, w["ffn_w_down"][i], tb=True, name="ffn_down_dx" + tag)
        gr["ffn_w_down"][i] = _mm(a, dy2, ta=True, name="ffn_down_dw" + tag)
        dp, dcw, dcb = _ffn_mid_bwd(da, p, w["ffn_conv_w"][i], w["ffn_conv_b"][i:i + 1], "ffn_mid_b" + tag)
        gr["ffn_conv_w"][i], gr["ffn_conv_b"][i] = _cat(dcw[0], dcw[1]), _cat(dcb[0], dcb[1])[0]
        dh2 = _mm(dp, w["ffn_w_up"][i], tb=True, a_parts=2, w_slots=N_CHIP, name="ffn_up_dx" + tag)
        gr["ffn_w_up"][i] = _mm(h2, dp, ta=True, b_parts=2, out_slots=N_CHIP, name="ffn_up_dw" + tag)
        dx1, d_g2, d_sc_f, d_sh_f = _norm_mod_bwd(dh2, x1, g2, sc_f, dx, "norm_ffn_b" + tag)
        if i == 0:
            gt_m = gt_m + told("ffn0")
        dy, d_g1, d_gt_m = _post_bwd(dx1, y, g1, gt_m, "post_mix_b" + tag)
        if i % 2 == 0:
            dact = _mm(dy, w["rg_w_out"], tb=True, name="rg_out_dx" + tag)
            gr["rg_w_out"] = _mm(act, dy, ta=True, name="rg_out_dw" + tag)
            dpj, gr["rg_conv_w"], d_cb, gr["rg_wa"], d_ba, gr["rg_wx"], d_bx, d_lam = _rg_mid_bwd(
                dact, pj, aux, w["rg_conv_w"], row(w["rg_conv_b"]), w["rg_wa"], row(w["rg_ba"]), w["rg_wx"],
                row(w["rg_bx"]), row(w["rg_lambda"]), "rg_mid_b" + tag)
            gr["rg_conv_b"], gr["rg_ba"], gr["rg_bx"], gr["rg_lambda"] = d_cb[0], d_ba[0], d_bx[0], d_lam[0]
            dh = _mm(dpj, w["rg_w_in"], tb=True, a_parts=2, w_slots=N_CHIP, name="rg_in_dx" + tag)
            gr["rg_w_in"] = _mm(h, dpj, ta=True, b_parts=2, out_slots=N_CHIP, name="rg_in_dw" + tag)
        else:
            dact = _mm(dy, w["gla_w_out"], tb=True, name="gla_out_dx" + tag)
            gr["gla_w_out"] = _mm(act, dy, ta=True, name="gla_out_dw" + tag)
            dpj, d_wal, d_bal, d_ng = _gla_mid_bwd(dact, pj, aux[0], aux[1], w["gla_wal"], w["gla_bal"],
                                                   row(w["gla_norm_g"]), "gla_mid_b" + tag)
            gr["gla_w_alpha"] = jnp.swapaxes(d_wal[:, :GLA_RANK, :], 0, 1).reshape(GLA_RANK, GLA_HEADS * GLA_DK)
            gr["gla_b_alpha"], gr["gla_norm_g"] = d_bal.reshape(-1), d_ng[0]
            dh = _mm(dpj, w["gla_w_in"], tb=True, name="gla_in_dx" + tag)
            gr["gla_w_in"] = _gla_unhead_cols(_mm(h, dpj, ta=True, name="gla_in_dw" + tag))
            mod = mod.at[0].add(told("l1"))
        dx, d_g0, d_sc_m, d_sh_m = _norm_mod_bwd(dh, x0, g0, sc_m, dx1, "norm_mix_b" + tag)
        gr["norm_g"][i] = jnp.concatenate([d_g0, d_g1, d_g2, d_g3], axis=0)
        gr["mod"][i] = jnp.concatenate([d_sh_m, d_sc_m, d_gt_m, d_sh_f, d_sc_f, d_gt_f], axis=0)
    for k in stacked:
        gr[k] = jnp.stack(gr[k])
    return cols, dx, gr


ADA_ROWS = 16


def _ada_fwd(c16, ada_w, ada_b, name):
    depth, d, n = ada_w.shape
    tn = _tile(n, (512, 256, 128))

    def body(c_ref, w_ref, b_ref, o_ref):
        cv = c_ref[...]
        o_ref[0] = _dot_nn(cv * _sigmoid(cv), w_ref[0]) + b_ref[0]

    return pl.pallas_call(
        body, grid=(depth, n // tn),
        in_specs=[pl.BlockSpec((ADA_ROWS, d), lambda l, j: (0, 0)), pl.BlockSpec((1, d, tn), lambda l, j: (l, 0, j)),
                  pl.BlockSpec((1, 1, tn), lambda l, j: (l, 0, j))],
        out_specs=pl.BlockSpec((1, ADA_ROWS, tn), lambda l, j: (l, 0, j)),
        out_shape=jax.ShapeDtypeStruct((depth, ADA_ROWS, n), F32),
        compiler_params=_params("parallel", "parallel"), name=name,
    )(c16, ada_w, ada_b)


def _ada_bwd(c16, dmod16, name):
    depth, _, n = dmod16.shape
    d = c16.shape[1]
    tn = _tile(n, (512, 256, 128))

    def body(c_ref, dm_ref, o_ref):
        cv = c_ref[...]
        o_ref[0] = _dot_tn(cv * _sigmoid(cv), dm_ref[0])

    return pl.pallas_call(
        body, grid=(depth, n // tn),
        in_specs=[pl.BlockSpec((ADA_ROWS, d), lambda l, j: (0, 0)), pl.BlockSpec((1, ADA_ROWS, tn), lambda l, j: (l, 0, j))],
        out_specs=pl.BlockSpec((1, d, tn), lambda l, j: (l, 0, j)),
        out_shape=jax.ShapeDtypeStruct((depth, d, n), F32),
        compiler_params=_params("parallel", "parallel"), name=name,
    )(c16, dmod16)


PACK_COLS = 1024
_ANY = pl.BlockSpec(memory_space=pl.ANY)
_VMEM = pl.BlockSpec(memory_space=pltpu.VMEM)


def _place():
    return lax.axis_index("x"), lax.axis_index("y"), lax.axis_index("c")


def _other_chips(x, y):
    return [(1 - x, y), (x, 1 - y), (1 - x, 1 - y)]


def _rcopy(src, dst, send_sems, recv_sems, k, peer):
    return pltpu.make_async_remote_copy(src_ref=src, dst_ref=dst, send_sem=send_sems.at[k], recv_sem=recv_sems.at[k],
                                        device_id=peer, device_id_type=MESH)


def _all_gather_8(v, name):
    r, cc = v.shape

    def body(v_ref, out_ref, send_sems, recv_sems, local_sem):
        x, y, c = _place()
        me = 4 * x + 2 * y + c
        mine = pltpu.make_async_copy(v_ref, out_ref.at[me], local_sem)
        mine.start()
        peers = []
        for k in range(1, N_DEV):
            px = 1 - x if k & 4 else x
            py = 1 - y if k & 2 else y
            pc = 1 - c if k & 1 else c
            peers.append((px, py, pc))
        sends = [_rcopy(v_ref, out_ref.at[me], send_sems, recv_sems, k, p) for k, p in enumerate(peers)]
        for cp in sends:
            cp.start()
        for k, (px, py, pc) in enumerate(peers):
            _rcopy(v_ref, out_ref.at[4 * px + 2 * py + pc], send_sems, recv_sems, k, (px, py, pc)).wait_recv()
        for cp in sends:
            cp.wait_send()
        mine.wait()

    return pl.pallas_call(
        body, in_specs=[_VMEM], out_specs=_VMEM, out_shape=jax.ShapeDtypeStruct((N_DEV, r, cc), v.dtype),
        scratch_shapes=[pltpu.SemaphoreType.DMA((N_DEV - 1,)), pltpu.SemaphoreType.DMA((N_DEV - 1,)), pltpu.SemaphoreType.DMA],
        compiler_params=pltpu.CompilerParams(vmem_limit_bytes=VMEM_LIMIT), name=name,
    )(v)


def _gather_chips(shards, name):
    n = len(shards)
    per = 2 * (N_CHIP - 1)

    def body(*refs):
        ins, outs, (send_sems, recv_sems) = refs[:n], refs[n:2 * n], refs[2 * n:]
        x, y, c = _place()
        chip = 2 * x + y
        chips = _other_chips(x, y)
        rows = [(pl.ds(c * (r.shape[0] // 2), r.shape[0] // 2), pl.ds((1 - c) * (r.shape[0] // 2), r.shape[0] // 2)) for r in ins]
        first = [_rcopy(ins[i].at[rows[i][0]], outs[i].at[chip, rows[i][0]], send_sems, recv_sems, per * i + j, (px, py, c))
                 for i in range(n) for j, (px, py) in enumerate(chips)]
        for cp in first:
            cp.start()
        passed = []
        for i in range(n):
            for j, (px, py) in enumerate(chips):
                landed = outs[i].at[2 * px + py, rows[i][0]]
                _rcopy(ins[i].at[rows[i][0]], landed, send_sems, recv_sems, per * i + j, (px, py, c)).wait_recv()
                fw = _rcopy(landed, landed, send_sems, recv_sems, per * i + N_CHIP - 1 + j, (x, y, 1 - c))
                fw.start()
                passed.append(fw)
        for i in range(n):
            for j, (px, py) in enumerate(chips):
                landed = outs[i].at[2 * px + py, rows[i][1]]
                _rcopy(landed, landed, send_sems, recv_sems, per * i + N_CHIP - 1 + j, (x, y, 1 - c)).wait_recv()
        for cp in first + passed:
            cp.wait_send()

    return pl.pallas_call(
        body, in_specs=[_ANY] * n, out_specs=[_ANY] * n,
        out_shape=[jax.ShapeDtypeStruct((N_CHIP,) + sh.shape, sh.dtype) for sh in shards],
        scratch_shapes=[pltpu.SemaphoreType.DMA((per * n,)), pltpu.SemaphoreType.DMA((per * n,))], name=name,
    )(*shards)


def _pair_exchange(gs, name):
    n = len(gs)

    def body(*refs):
        ins, outs, (send_sems, recv_sems) = refs[:n], refs[n:2 * n], refs[2 * n:]
        x, y, c = _place()
        copies = []
        for i in range(n):
            half = ins[i].shape[1] // 2
            copies.append(_rcopy(ins[i].at[:, pl.ds((1 - c) * half, half)], outs[i], send_sems, recv_sems, i, (x, y, 1 - c)))
        for cp in copies:
            cp.start()
        for cp in copies:
            cp.wait()

    return pl.pallas_call(
        body, in_specs=[_ANY] * n, out_specs=[_ANY] * n,
        out_shape=[jax.ShapeDtypeStruct((g.shape[0], g.shape[1] // 2, g.shape[2]), g.dtype) for g in gs],
        scratch_shapes=[pltpu.SemaphoreType.DMA((n,)), pltpu.SemaphoreType.DMA((n,))], name=name,
    )(*gs)


_ROW_TILES = (640, 512, 352, 256, 128, 64, 32, 16)


def _pair_sum(g, other, c_idx, name):
    n, half, cc = other.shape
    tr = _tile(half, _ROW_TILES)

    def body(c_ref, g_ref, o_ref, out_ref):
        out_ref[...] = (g_ref[...] + o_ref[...]).astype(out_ref.dtype)

    return pl.pallas_call(
        body,
        grid_spec=pltpu.PrefetchScalarGridSpec(
            num_scalar_prefetch=1, grid=(n, half // tr),
            in_specs=[pl.BlockSpec((None, None, tr, cc), lambda k, i, c_ref: (k, c_ref[0], i, 0)),
                      pl.BlockSpec((None, tr, cc), lambda k, i, c_ref: (k, i, 0))],
            out_specs=pl.BlockSpec((None, tr, cc), lambda k, i, c_ref: (k, i, 0))),
        out_shape=jax.ShapeDtypeStruct((n, half, cc), BF16),
        compiler_params=_params("parallel", "parallel"), name=name,
    )(c_idx, g.reshape(n, 2, half, cc), other)


def _chip_exchange(ps, name):
    n = len(ps)
    per = N_CHIP - 1

    def body(*refs):
        ins, outs, (send_sems, recv_sems) = refs[:n], refs[n:2 * n], refs[2 * n:]
        x, y, c = _place()
        chip = 2 * x + y
        chips = _other_chips(x, y)
        sends = [_rcopy(ins[i].at[2 * px + py], outs[i].at[chip], send_sems, recv_sems, per * i + j, (px, py, c))
                 for i in range(n) for j, (px, py) in enumerate(chips)]
        for cp in sends:
            cp.start()
        for i in range(n):
            for j, (px, py) in enumerate(chips):
                _rcopy(ins[i].at[chip], outs[i].at[2 * px + py], send_sems, recv_sems, per * i + j, (px, py, c)).wait_recv()
        for cp in sends:
            cp.wait_send()

    return pl.pallas_call(
        body, in_specs=[_ANY] * n, out_specs=[_ANY] * n, out_shape=[jax.ShapeDtypeStruct(p.shape, p.dtype) for p in ps],
        scratch_shapes=[pltpu.SemaphoreType.DMA((per * n,)), pltpu.SemaphoreType.DMA((per * n,))], name=name,
    )(*ps)


_HBM = pl.BlockSpec(memory_space=pltpu.HBM)
_SEM = pl.BlockSpec(memory_space=pltpu.SEMAPHORE)
_DATAFLOW = pltpu.SideEffectType.DATAFLOW_SIDE_EFFECTING


def _chip_copies(srcs, lands, send_sems, recv_sems, per_chip_src, arriving):
    x, y, c = _place()
    chip = 2 * x + y
    out = []
    for i, (src, land) in enumerate(zip(srcs, lands)):
        for j, (px, py) in enumerate(_other_chips(x, y)):
            there = 2 * px + py
            part = src.at[there] if per_chip_src else src
            out.append(_rcopy(part, land.at[there if arriving else chip], send_sems, recv_sems, (N_CHIP - 1) * i + j, (px, py, c)))
    return out


def _send_start(srcs, per_chip_src, name):
    n = len(srcs)
    n_sem = (N_CHIP - 1) * n
    lands = [lax.empty((N_CHIP,) + (s.shape[1:] if per_chip_src else s.shape), s.dtype) for s in srcs]

    def body(*refs):
        ins, zones, (send_sems, recv_sems) = refs[:n], refs[n:2 * n], refs[2 * n:2 * n + 2]
        for cp in _chip_copies(ins, zones, send_sems, recv_sems, per_chip_src, False):
            cp.start()
        refs[-1][...] = jnp.zeros_like(refs[-1])

    hbm = lambda a: pltpu.HBM(a.shape, a.dtype)
    outs = pl.pallas_call(
        body, name=name, in_specs=[_HBM] * (2 * n),
        out_shape=(pltpu.SemaphoreType.DMA((n_sem,)), pltpu.SemaphoreType.DMA((n_sem,)), *[hbm(a) for a in srcs],
                   *[hbm(a) for a in lands], jax.ShapeDtypeStruct((SUBLANES, LANES), F32)),
        out_specs=(_SEM, _SEM, *[_HBM] * (2 * n), _VMEM), input_output_aliases={i: 2 + i for i in range(2 * n)},
        compiler_params=pltpu.CompilerParams(has_side_effects=_DATAFLOW),
    )(*[pltpu.with_memory_space_constraint(a, pltpu.HBM) for a in list(srcs) + lands])
    return (outs[0], outs[1], list(outs[2:2 + n]), list(outs[2 + n:2 + 2 * n])), outs[-1][0, 0]


def _send_wait(state, after, per_chip_src, name):
    send_sems, recv_sems, srcs, lands = state
    n = len(srcs)

    def body(*refs):
        ins, zones, (send_s, recv_s) = refs[:n], refs[n:2 * n], refs[2 * n:2 * n + 2]
        for cp in _chip_copies(ins, zones, send_s, recv_s, per_chip_src, True):
            cp.wait_send()
            cp.wait_recv()

    hbm = lambda a: pltpu.HBM(a.shape, a.dtype)
    outs = pl.pallas_call(
        body, name=name, in_specs=[_HBM] * (2 * n) + [_SEM, _SEM, _ANY],
        out_shape=tuple(hbm(a) for a in srcs + lands), out_specs=tuple([_HBM] * (2 * n)),
        input_output_aliases={i: i for i in range(2 * n)},
        compiler_params=pltpu.CompilerParams(has_side_effects=_DATAFLOW),
    )(*srcs, *lands, send_sems, recv_sems, after)
    return list(outs[n:])


def _sum_lead(v, name):
    n, r, cc = v.shape
    tr = _tile(r, _ROW_TILES + (8,))

    def body(v_ref, o_ref):
        acc = v_ref[0].astype(F32)
        for k in range(1, n):
            acc = acc + v_ref[k].astype(F32)
        o_ref[...] = acc

    return pl.pallas_call(
        body, grid=(r // tr,), in_specs=[pl.BlockSpec((n, tr, cc), lambda i: (0, i, 0))],
        out_specs=pl.BlockSpec((tr, cc), lambda i: (i, 0)), out_shape=jax.ShapeDtypeStruct((r, cc), F32),
        compiler_params=_params("parallel"), name=name,
    )(v)


def _pair_share(reds, name):
    n = len(reds)

    def body(*refs):
        ins, outs, (send_sems, recv_sems) = refs[:n], refs[n:2 * n], refs[2 * n:]
        x, y, c = _place()
        copies = [_rcopy(ins[i], outs[i].at[c], send_sems, recv_sems, i, (x, y, 1 - c)) for i in range(n)]
        for cp in copies:
            cp.start()
        for i in range(n):
            _rcopy(ins[i], outs[i].at[1 - c], send_sems, recv_sems, i, (x, y, 1 - c)).wait_recv()
        for cp in copies:
            cp.wait_send()

    return pl.pallas_call(
        body, in_specs=[_ANY] * n, out_specs=[_ANY] * n, out_shape=[jax.ShapeDtypeStruct((2,) + r.shape, r.dtype) for r in reds],
        scratch_shapes=[pltpu.SemaphoreType.DMA((n,)), pltpu.SemaphoreType.DMA((n,))], name=name,
    )(*reds)


def _pack(arrs, rows_multiple, dtype):
    flat = jnp.concatenate([a.reshape(-1).astype(dtype) for a in arrs])
    unit = rows_multiple * PACK_COLS
    total = -(-flat.shape[0] // unit) * unit
    return jnp.pad(flat, (0, total - flat.shape[0])).reshape(-1, PACK_COLS)


def _unpack(buf, shapes):
    lead = buf.shape[:-2]
    flat = buf.reshape(*lead, -1)
    out, off = [], 0
    for shp in shapes:
        n = 1
        for s in shp:
            n *= s
        out.append(flat[..., off:off + n].reshape(*lead, *shp))
        off += n
    return out


def _join_shards(parts, axis):
    moved = jnp.moveaxis(parts, 0, axis)
    shp = list(moved.shape)
    shp[axis:axis + 2] = [shp[axis] * shp[axis + 1]]
    return moved.reshape(shp)


def _my_shard(full, axis, chip):
    n = full.shape[axis] // N_CHIP
    return lax.dynamic_slice_in_dim(full, chip * n, n, axis)


SMALL = {"norm_g": 2, "ffn_conv_w": 2, "rg_conv_w": 2, "gla_w_alpha": 2, "gla_b_alpha": 1, "gla_norm_g": 1,
         "ada_b": None, "ffn_conv_b": None, "rg_conv_b": None, "rg_ba": None, "rg_bx": None, "rg_lambda": None}
BIG = {"rg_w_in": True, "rg_wa": False, "rg_wx": False, "rg_w_out": False, "ffn_w_up": True, "ffn_w_down": False,
       "gla_w_in": True, "gla_w_out": False}
WEIGHTS = ["ada_w", "ada_b", "norm_g", "ffn_w_up", "ffn_conv_w", "ffn_conv_b", "ffn_w_down", "rg_w_in", "rg_conv_w", "rg_conv_b",
           "rg_wa", "rg_ba", "rg_wx", "rg_bx", "rg_lambda", "rg_w_out", "gla_w_in", "gla_w_alpha", "gla_b_alpha", "gla_norm_g",
           "gla_w_out"]


def kernel(x, c, ada_w, ada_b, norm_g, ffn_w_up, ffn_conv_w, ffn_conv_b, ffn_w_down, rg_w_in, rg_conv_w, rg_conv_b, rg_wa, rg_ba, rg_wx, rg_bx, rg_lambda, rg_w_out, gla_w_in, gla_w_alpha, gla_b_alpha, gla_norm_g, gla_w_out, loss_target, m_ada_w, m_ada_b, m_norm_g, m_ffn_w_up, m_ffn_conv_w, m_ffn_conv_b, m_ffn_w_down, m_rg_w_in, m_rg_conv_w, m_rg_conv_b, m_rg_wa, m_rg_ba, m_rg_wx, m_rg_bx, m_rg_lambda, m_rg_w_out, m_gla_w_in, m_gla_w_alpha, m_gla_b_alpha, m_gla_norm_g, m_gla_w_out, v_ada_w, v_ada_b, v_norm_g, v_ffn_w_up, v_ffn_conv_w, v_ffn_conv_b, v_ffn_w_down, v_rg_w_in, v_rg_conv_w, v_rg_conv_b, v_rg_wa, v_rg_ba, v_rg_wx, v_rg_bx, v_rg_lambda, v_rg_w_out, v_gla_w_in, v_gla_w_alpha, v_gla_b_alpha, v_gla_norm_g, v_gla_w_out):
    wts = dict(ada_w=ada_w, ada_b=ada_b, norm_g=norm_g, ffn_w_up=ffn_w_up, ffn_conv_w=ffn_conv_w, ffn_conv_b=ffn_conv_b,
               ffn_w_down=ffn_w_down, rg_w_in=rg_w_in, rg_conv_w=rg_conv_w, rg_conv_b=rg_conv_b, rg_wa=rg_wa, rg_ba=rg_ba,
               rg_wx=rg_wx, rg_bx=rg_bx, rg_lambda=rg_lambda, rg_w_out=rg_w_out, gla_w_in=gla_w_in, gla_w_alpha=gla_w_alpha,
               gla_b_alpha=gla_b_alpha, gla_norm_g=gla_norm_g, gla_w_out=gla_w_out)
    mom1 = dict(ada_w=m_ada_w, ada_b=m_ada_b, norm_g=m_norm_g, ffn_w_up=m_ffn_w_up, ffn_conv_w=m_ffn_conv_w,
                ffn_conv_b=m_ffn_conv_b, ffn_w_down=m_ffn_w_down, rg_w_in=m_rg_w_in, rg_conv_w=m_rg_conv_w,
                rg_conv_b=m_rg_conv_b, rg_wa=m_rg_wa, rg_ba=m_rg_ba, rg_wx=m_rg_wx, rg_bx=m_rg_bx, rg_lambda=m_rg_lambda,
                rg_w_out=m_rg_w_out, gla_w_in=m_gla_w_in, gla_w_alpha=m_gla_w_alpha, gla_b_alpha=m_gla_b_alpha,
                gla_norm_g=m_gla_norm_g, gla_w_out=m_gla_w_out)
    mom2 = dict(ada_w=v_ada_w, ada_b=v_ada_b, norm_g=v_norm_g, ffn_w_up=v_ffn_w_up, ffn_conv_w=v_ffn_conv_w,
                ffn_conv_b=v_ffn_conv_b, ffn_w_down=v_ffn_w_down, rg_w_in=v_rg_w_in, rg_conv_w=v_rg_conv_w,
                rg_conv_b=v_rg_conv_b, rg_wa=v_rg_wa, rg_ba=v_rg_ba, rg_wx=v_rg_wx, rg_bx=v_rg_bx, rg_lambda=v_rg_lambda,
                rg_w_out=v_rg_w_out, gla_w_in=v_gla_w_in, gla_w_alpha=v_gla_w_alpha, gla_b_alpha=v_gla_b_alpha,
                gla_norm_g=v_gla_norm_g, gla_w_out=v_gla_w_out)
    xi, yi, ci = _place()
    chip, me = 2 * xi + yi, 4 * xi + 2 * yi + ci
    d = x.shape[-1]
    depth = ada_w.shape[0]
    n_ada = ada_w.shape[-1]
    sharded_small = [k for k, ax in SMALL.items() if ax is not None]

    sm = _all_gather_8(_pack([c] + [wts[k] for k in sharded_small], SUBLANES, F32), "gather_small")
    c_all = sm[:, 0, :]
    parts = _unpack(sm[0::2], [c.shape] + [wts[k].shape for k in sharded_small])[1:]
    full = {k: _join_shards(p, SMALL[k]) for k, p in zip(sharded_small, parts)}
    for k, ax in SMALL.items():
        if ax is None:
            full[k] = wts[k]

    c16 = jnp.pad(c_all, ((0, ADA_ROWS - N_DEV), (0, 0)))
    ada_b_mine = lax.dynamic_slice_in_dim(ada_b, chip * n_ada, n_ada, 1)[:, None, :]
    mod_cols = _ada_fwd(c16, ada_w, ada_b_mine, "ada_fwd")
    mod_all = _all_gather_8(mod_cols.reshape(-1, PACK_COLS), "gather_mod")[0::2].reshape(N_CHIP, depth, ADA_ROWS, n_ada)
    mod = jnp.swapaxes(lax.dynamic_index_in_dim(mod_all, me, 2, keepdims=False), 0, 1).reshape(depth, 6, d)

    items = [(k, l) for k in BIG for l in range(wts[k].shape[0])]
    stage_of = lambda k, l: "rg" if k.startswith("rg_") else ("ffn0" if (k.startswith("ffn_") and l == 0) else "l1")
    staged = {st: [it for it in items if stage_of(*it) == st] for st in ("rg", "ffn0", "l1")}
    staged["l1"].sort(key=lambda it: not it[0].startswith("gla_"))
    shard = lambda k, l: wts[k][l].reshape(-1, wts[k].shape[-1]).astype(BF16)
    own = lambda got, mine: [lax.dynamic_update_index_in_dim(g, m, chip, 0) for g, m in zip(got, mine)]
    rows_joined = lambda v: v.reshape(-1, v.shape[-1])

    def placed(its, slots):
        out = {"ffn_w_up": {}, "ffn_w_down": {}}
        for (k, l), v in zip(its, slots):
            if k == "ffn_w_up":
                out[k][l] = v
            elif k == "ffn_w_down":
                out[k][l] = rows_joined(v)
            elif k in ("rg_wa", "rg_wx"):
                out[k] = _slots_to_block_rows(v, RG_BLOCKS)
            elif k == "gla_w_in":
                out[k] = _gla_head_cols(_from_col_slots(v))
            else:
                out[k] = v if BIG[k] else rows_joined(v)
        return out

    after_mod = (mod[0, 0, 0] * 0.0).astype(BF16)
    sh_rg = [shard(k, l) + after_mod for k, l in staged["rg"]]
    local = {k: (v if k in ("norm_g", "ffn_conv_w", "ffn_conv_b") else v[0]) for k, v in full.items()}
    local["gla_wal"], local["gla_bal"] = _gla_alpha_heads(local["gla_w_alpha"], local["gla_b_alpha"])
    local.update(placed(staged["rg"], own(_gather_chips(sh_rg, "gather_weights_rg"), sh_rg)))
    sh_late, flying = {}, {}
    after_rg = (local["rg_w_out"][0, 0].astype(F32) * 0.0).astype(BF16)
    sh_late["ffn0"] = [shard(k, l) + after_rg for k, l in staged["ffn0"]]
    flying["ffn0"], tok = _send_start(sh_late["ffn0"], False, "weights_ffn0_start")
    sh_late["l1"] = [shard(k, l) + tok.astype(BF16) for k, l in staged["l1"]]
    flying["l1"], tok2 = _send_start(sh_late["l1"], False, "weights_l1_start")
    mod = mod + (tok + tok2)

    def fetch(stage, after):
        got = _send_wait(flying[stage], after, False, f"weights_{stage}_wait")
        return placed(staged[stage], own(got, sh_late[stage]))

    c_idx = ci.reshape(1).astype(jnp.int32)
    psums, sent = {}, {}

    def grad_slots(gr, k, l):
        g = gr[k][l] if k in ("ffn_w_up", "ffn_w_down") else gr[k]
        if k in ("rg_wa", "rg_wx"):
            return _block_rows_to_slots(g)
        if k == "gla_w_in":
            return _col_slots(g)
        return g if BIG[k] else g.reshape(N_CHIP, -1, g.shape[-1])

    def pair_sums(stage, gr):
        gslots = [grad_slots(gr, k, l) for k, l in staged[stage]]
        theirs = _pair_exchange(gslots, f"grads_{stage}_pair_exchange")
        psums[stage] = [_pair_sum(g, t, c_idx, f"grads_pair_sum_{k}{l}") for (k, l), g, t in zip(staged[stage], gslots, theirs)]

    def done(stage, gr):
        pair_sums(stage, gr)
        sent[stage], token = _send_start(psums[stage], True, f"grads_{stage}_start")
        return token

    cols, grad_x, gr = _local_step(x[0], loss_target[0], mod, local, fetch, done)
    loss = lax.psum(0.5 * jnp.sum(cols) / d, ("x", "y", "c"))

    small_names = [k for k in SMALL if k != "ada_b"]
    gs = _all_gather_8(_pack([gr[k] for k in small_names] + [gr["mod"]], SUBLANES, F32), "gather_small_grads")
    small_shapes = [full[k].shape for k in small_names] + [(depth, 6 * d)]
    *small_sum, g_ada_b = _unpack(_sum_lead(gs, "sum_small_grads"), small_shapes)
    grads = dict(zip(small_names, small_sum))
    grads["ada_b"] = g_ada_b
    for k in sharded_small:
        grads[k] = _my_shard(grads[k], SMALL[k], chip)
    dmod_all = _unpack(gs, small_shapes)[-1].reshape(N_DEV, depth, N_CHIP, n_ada)
    dmod_mine = jnp.swapaxes(lax.dynamic_index_in_dim(dmod_all, chip, 2, keepdims=False), 0, 1)
    g_ada_w = _ada_bwd(c16, jnp.pad(dmod_mine, ((0, 0), (0, ADA_ROWS - N_DEV), (0, 0))), "ada_bwd")

    pair_sums("rg", gr)
    arrived = {"rg": _chip_exchange(psums["rg"], "grads_rg_chip_exchange")}
    for stage in ("l1", "ffn0"):
        arrived[stage] = _send_wait(sent[stage], grad_x, True, f"grads_{stage}_wait")
    order = [(st, n) for st in ("rg", "ffn0", "l1") for n in range(len(staged[st]))]
    items = [staged[st][n] for st, n in order]
    mine_of = lambda p: lax.dynamic_index_in_dim(p, chip, 0, keepdims=False)
    halves = [_sum_lead(lax.dynamic_update_index_in_dim(arrived[st][n], mine_of(psums[st][n]), chip, 0),
                        "grads_chip_sum_%s%d" % staged[st][n]) for st, n in order]
    shared = _pair_share(halves, "grads_pair_share")
    reduced = [lax.dynamic_update_index_in_dim(s2, h, ci, 0).reshape(-1, h.shape[-1]) for s2, h in zip(shared, halves)]

    delta, new_m, new_v = {}, {}, {}

    def update(k, gs_k, view):
        shp = wts[k].shape
        outs = _adamw(view(wts[k]), gs_k, view(mom1[k]), view(mom2[k]), "adamw_" + k)
        grads[k], delta[k], new_m[k], new_v[k] = (o.reshape(shp) for o in outs)

    update("ada_w", [g_ada_w[l] for l in range(depth)], lambda a: a)
    for k in BIG:
        gs_k = [g for (kk, _), g in zip(items, reduced) if kk == k]
        update(k, gs_k, lambda a: a.reshape(a.shape[0], -1, a.shape[-1]))
    small_shard_shapes = [wts[k].shape for k in SMALL]
    packed = [_pack([src[k] for k in SMALL], SUBLANES, F32) for src in (wts, grads, mom1, mom2)]
    outs = _adamw(packed[0][None], [packed[1]], packed[2][None], packed[3][None], "adamw_small")
    for dst, o in zip((delta, new_m, new_v), outs[1:]):
        for k, a in zip(SMALL, _unpack(o[0], small_shard_shapes)):
            dst[k] = a

    return (loss, grad_x[None], *[grads[k] for k in WEIGHTS], *[delta[k] for k in WEIGHTS], *[new_m[k] for k in WEIGHTS],
            *[new_v[k] for k in WEIGHTS])
```

```python
import jax
import jax.numpy as jnp
from jax import lax
from jax.experimental import pallas as pl
from jax.experimental.pallas import tpu as pltpu

F32 = jnp.float32
BF16 = jnp.bfloat16
MXU_DTYPE = BF16

EPS = 1e-6
RG_C = 8.0
RG_BLOCKS = 4
RG_CONV = 4
GLA_HEADS = 4
GLA_TAU = 16.0
GLA_CHUNK = 64
GLA_RANK = 16
FFN_CONV = 3
ADAM_LR = 0.001
ADAM_B1 = 0.9
ADAM_B2 = 0.999
ADAM_EPS = 1e-08
ADAM_WD = 0.01
ADAM_STEP = 10

LANES = 128
SUBLANES = 8
VMEM_LIMIT = 56 * 1024 * 1024
CB = 256
MESH = pl.DeviceIdType.MESH
N_DEV = 8
N_CHIP = 4


def _params(*sem):
    return pltpu.CompilerParams(dimension_semantics=sem, vmem_limit_bytes=VMEM_LIMIT)


def _tile(dim, prefs):
    for p in prefs:
        if dim % p == 0:
            return p
    return dim


def _dot(a, b, dims):
    return lax.dot_general(a.astype(MXU_DTYPE), b.astype(MXU_DTYPE), (dims, ((), ())), preferred_element_type=F32)


def _dot_nn(a, b):
    return _dot(a, b, ((1,), (0,)))


def _dot_nt(a, b):
    return _dot(a, b, ((1,), (1,)))


def _dot_tn(a, b):
    return _dot(a, b, ((0,), (0,)))


def _mm(a, b, *, ta=False, tb=False, a_parts=1, b_parts=1, w_slots=1, out_slots=1, out_dtype=F32, name):
    if ta:
        k_dim, m_dim = a.shape
        n_dim = b.shape[-1] * b_parts
    else:
        m_dim, k_dim = a.shape[-2], a.shape[-1] * a_parts
        n_dim = b.shape[-2] if tb else b.shape[-1] * w_slots
    n_unit = n_dim // max(b_parts, out_slots, 1 if tb else w_slots)
    k_unit = k_dim // max(a_parts, w_slots if tb else 1)
    tm = _tile(m_dim, (1024, 1408, 512, 256, 128))
    tn = _tile(n_unit, (1024, 1408, 896, 512, 256, 128))
    tk = _tile(k_unit, (1024, 1408, 896, 512, 256, 128))
    nk = k_dim // tk
    dims = ((0 if ta else 1,), (1 if tb else 0,))

    def spec(shape, parts, total, tile, col_grid, row_grid):
        per = total // parts // tile

        def index(i, j, k):
            g = {"i": i, "j": j, "k": k}
            col, row = g[col_grid], g[row_grid]
            return (row, col) if parts == 1 else (col // per, row, col % per)

        return pl.BlockSpec(shape if parts == 1 else (None,) + shape, index)

    def body(a_ref, b_ref, o_ref, acc_ref):
        k = pl.program_id(2)

        @pl.when(k == 0)
        def _():
            acc_ref[...] = jnp.zeros_like(acc_ref)

        acc_ref[...] += _dot(a_ref[...], b_ref[...], dims)

        @pl.when(k == nk - 1)
        def _():
            o_ref[...] = acc_ref[...].astype(o_ref.dtype)

    if ta:
        a_spec = spec((tk, tm), 1, m_dim, tm, "i", "k")
        b_spec = spec((tk, tn), b_parts, n_dim, tn, "j", "k")
    elif tb:
        a_spec = spec((tm, tk), a_parts, k_dim, tk, "k", "i")
        b_spec = spec((tn, tk), w_slots, k_dim, tk, "k", "j")
    else:
        a_spec = spec((tm, tk), a_parts, k_dim, tk, "k", "i")
        b_spec = spec((tk, tn), w_slots, n_dim, tn, "j", "k")
    out_shape = (m_dim, n_dim) if out_slots == 1 else (out_slots, m_dim, n_dim // out_slots)
    return pl.pallas_call(
        body,
        grid=(m_dim // tm, n_dim // tn, nk),
        in_specs=[a_spec, b_spec],
        out_specs=spec((tm, tn), out_slots, n_dim, tn, "j", "i"),
        out_shape=jax.ShapeDtypeStruct(out_shape, out_dtype),
        scratch_shapes=[pltpu.VMEM((tm, tn), F32)],
        compiler_params=_params("parallel", "parallel", "arbitrary"),
        name=name,
    )(a, b)


def _row_specs(s, d, ts):
    return pl.BlockSpec((ts, d), lambda i: (i, 0)), pl.BlockSpec((1, d), lambda i: (0, 0))


def _norm_mod_fwd(x, g, sc, sh, name):
    s, d = x.shape
    ts = _tile(s, (512,))

    def body(x_ref, g_ref, sc_ref, sh_ref, h_ref):
        xv = x_ref[...]
        r = lax.rsqrt(jnp.mean(xv * xv, axis=-1, keepdims=True) + EPS)
        h_ref[...] = (((xv * r) * g_ref[...]) * (1.0 + sc_ref[...]) + sh_ref[...]).astype(h_ref.dtype)

    row, vec = _row_specs(s, d, ts)
    return pl.pallas_call(
        body, grid=(s // ts,), in_specs=[row, vec, vec, vec], out_specs=row,
        out_shape=jax.ShapeDtypeStruct((s, d), MXU_DTYPE), compiler_params=_params("parallel"), name=name,
    )(x, g, sc, sh)


def _norm_mod_bwd(dh, x, g, sc, dres, name):
    s, d = x.shape
    ts = _tile(s, (512,))

    def body(dh_ref, x_ref, g_ref, sc_ref, dres_ref, dx_ref, dg_ref, dsc_ref, dsh_ref, acc_ref):
        i = pl.program_id(0)

        @pl.when(i == 0)
        def _():
            acc_ref[...] = jnp.zeros_like(acc_ref)

        xv, dhv = x_ref[...], dh_ref[...]
        r = lax.rsqrt(jnp.mean(xv * xv, axis=-1, keepdims=True) + EPS)
        n = xv * r
        acc_ref[0:1, :] += jnp.sum(dhv * n, axis=0, keepdims=True)
        acc_ref[1:2, :] += jnp.sum(dhv, axis=0, keepdims=True)
        dn = dhv * ((1.0 + sc_ref[...]) * g_ref[...])
        dx_ref[...] = dres_ref[...] + r * (dn - n * jnp.mean(dn * n, axis=-1, keepdims=True))
        dg_ref[...] = (1.0 + sc_ref[...]) * acc_ref[0:1, :]
        dsc_ref[...] = g_ref[...] * acc_ref[0:1, :]
        dsh_ref[...] = acc_ref[1:2, :]

    row, vec = _row_specs(s, d, ts)
    vshape = jax.ShapeDtypeStruct((1, d), F32)
    return pl.pallas_call(
        body, grid=(s // ts,), in_specs=[row, row, vec, vec, row], out_specs=[row, vec, vec, vec],
        out_shape=[jax.ShapeDtypeStruct((s, d), F32), vshape, vshape, vshape],
        scratch_shapes=[pltpu.VMEM((SUBLANES, d), F32)], compiler_params=_params("arbitrary"), name=name,
    )(dh, x, g, sc, dres)


def _post_fwd(x, y, g, gt, name):
    s, d = x.shape
    ts = _tile(s, (512,))

    def body(x_ref, y_ref, g_ref, gt_ref, o_ref):
        yv = y_ref[...]
        r = lax.rsqrt(jnp.mean(yv * yv, axis=-1, keepdims=True) + EPS)
        o_ref[...] = x_ref[...] + gt_ref[...] * ((yv * r) * g_ref[...])

    row, vec = _row_specs(s, d, ts)
    return pl.pallas_call(
        body, grid=(s // ts,), in_specs=[row, row, vec, vec], out_specs=row,
        out_shape=jax.ShapeDtypeStruct((s, d), F32), compiler_params=_params("parallel"), name=name,
    )(x, y, g, gt)


def _post_bwd(dxn, y, g, gt, name):
    s, d = y.shape
    ts = _tile(s, (512,))

    def body(dxn_ref, y_ref, g_ref, gt_ref, dy_ref, dg_ref, dgt_ref, acc_ref):
        i = pl.program_id(0)

        @pl.when(i == 0)
        def _():
            acc_ref[...] = jnp.zeros_like(acc_ref)

        yv, dv = y_ref[...], dxn_ref[...]
        r = lax.rsqrt(jnp.mean(yv * yv, axis=-1, keepdims=True) + EPS)
        n = yv * r
        acc_ref[0:1, :] += jnp.sum(dv * n, axis=0, keepdims=True)
        dn = dv * (gt_ref[...] * g_ref[...])
        dy_ref[...] = (r * (dn - n * jnp.mean(dn * n, axis=-1, keepdims=True))).astype(dy_ref.dtype)
        dg_ref[...] = gt_ref[...] * acc_ref[0:1, :]
        dgt_ref[...] = g_ref[...] * acc_ref[0:1, :]

    row, vec = _row_specs(s, d, ts)
    vshape = jax.ShapeDtypeStruct((1, d), F32)
    return pl.pallas_call(
        body, grid=(s // ts,), in_specs=[row, row, vec, vec], out_specs=[row, vec, vec],
        out_shape=[jax.ShapeDtypeStruct((s, d), MXU_DTYPE), vshape, vshape],
        scratch_shapes=[pltpu.VMEM((SUBLANES, d), F32)], compiler_params=_params("arbitrary"), name=name,
    )(dxn, y, g, gt)


def _loss_grad(x, tgt, name):
    s, d = x.shape
    ts = _tile(s, (512,))

    def body(x_ref, t_ref, col_ref, dx_ref):
        i = pl.program_id(0)

        @pl.when(i == 0)
        def _():
            col_ref[...] = jnp.zeros_like(col_ref)

        e = x_ref[...] - t_ref[...]
        col_ref[...] += jnp.sum(e * e, axis=0, keepdims=True)
        dx_ref[...] = e * (1.0 / d)

    row, vec = _row_specs(s, d, ts)
    return pl.pallas_call(
        body, grid=(s // ts,), in_specs=[row, row], out_specs=[vec, row],
        out_shape=[jax.ShapeDtypeStruct((1, d), F32), jax.ShapeDtypeStruct((s, d), F32)],
        compiler_params=_params("arbitrary"), name=name,
    )(x, tgt)


_GELU_C = 0.7978845608028654
_GELU_A = 0.044715


def _gelu(x):
    t = jnp.tanh(_GELU_C * (x + _GELU_A * x * x * x))
    return 0.5 * x * (1.0 + t), t


def _gelu_grad(x, t):
    return 0.5 * (1.0 + t) + 0.5 * x * (1.0 - t * t) * (_GELU_C * (1.0 + 3.0 * _GELU_A * x * x))


def _sigmoid(x):
    return 1.0 / (1.0 + jnp.exp(-x))


def _log1p_pos(y):
    u = 1.0 + y
    return jnp.where(u == 1.0, y, jnp.log(u) * (y / jnp.where(u == 1.0, 1.0, u - 1.0)))


def _softplus(x):
    return jnp.maximum(x, 0.0) + _log1p_pos(jnp.exp(-jnp.abs(x)))


def _one_minus_exp(z):
    u = jnp.exp(z)
    lg = jnp.log(jnp.where(u > 0.0, u, 1.0))
    safe = (u != 1.0) & (u > 0.0)
    return jnp.where(u == 1.0, -z, jnp.where(u > 0.0, (1.0 - u) * (z / jnp.where(safe, lg, 1.0)), 1.0))


SLAB = 16


def _cat(a, b):
    return jnp.concatenate([a, b], axis=1)


def _fold8(x):
    out = x[0:SUBLANES]
    for r in range(SUBLANES, x.shape[0], SUBLANES):
        out = out + x[r:r + SUBLANES]
    return out


def _pair_specs(shape, nb, index):
    return [pl.BlockSpec(shape, lambda j, t: index(j, t) + (j,)), pl.BlockSpec(shape, lambda j, t: index(j, t) + (j + nb,))]


def _halo_row(ts, time_of):
    return lambda j, t: (jnp.maximum(time_of(t) * (ts // SUBLANES) - 1, 0),)


def _ffn_mid_fwd(p, cw, cb, name):
    s, f2 = p.shape
    ts = _tile(s, (512,))
    nb, nt = f2 // (2 * CB), s // ts

    def body(pg_ref, pv_ref, hg_ref, hv_ref, cwg_ref, cwv_ref, cbg_ref, cbv_ref, a_ref):
        t = pl.program_id(1)
        cwv, bias = _cat(cwg_ref[...], cwv_ref[...]), _cat(cbg_ref[...], cbv_ref[...])
        w0, w1, w2 = cwv[0:1], cwv[1:2], cwv[2:3]

        def slab(blk, r0):
            u = bias + w0 * blk[6:6 + SLAB] + w1 * blk[7:7 + SLAB] + w2 * blk[8:8 + SLAB]
            a_ref[pl.ds(r0, SLAB), :] = (_gelu(u[:, :CB])[0] * u[:, CB:]).astype(a_ref.dtype)

        halo = jnp.where(t > 0, _cat(hg_ref[...], hv_ref[...]), 0.0)
        slab(jnp.concatenate([halo, _cat(pg_ref[0:SLAB, :], pv_ref[0:SLAB, :])], axis=0), 0)

        def loop(i, carry):
            r0 = pl.multiple_of(i * SLAB, SLAB)
            rows = pl.ds(pl.multiple_of(r0 - SUBLANES, SUBLANES), SLAB + SUBLANES)
            slab(_cat(pg_ref[rows, :], pv_ref[rows, :]), r0)
            return carry

        lax.fori_loop(1, ts // SLAB, loop, 0, unroll=2)

    fwd = lambda t: t
    return pl.pallas_call(
        body, grid=(nb, nt),
        in_specs=(_pair_specs((ts, CB), nb, lambda j, t: (t,)) + _pair_specs((SUBLANES, CB), nb, _halo_row(ts, fwd))
                  + _pair_specs((FFN_CONV, CB), nb, lambda j, t: (0,)) + _pair_specs((1, CB), nb, lambda j, t: (0,))),
        out_specs=pl.BlockSpec((ts, CB), lambda j, t: (t, j)),
        out_shape=jax.ShapeDtypeStruct((s, f2 // 2), MXU_DTYPE),
        compiler_params=_params("parallel", "arbitrary"), name=name,
    )(p, p, p, p, cw, cw, cb, cb)


def _ffn_mid_bwd(da, p, cw, cb, name):
    s, f2 = p.shape
    ts = _tile(s, (512,))
    nb, nt = f2 // (2 * CB), s // ts
    n_slab = ts // SLAB

    def body(da_ref, pg_ref, pv_ref, hg_ref, hv_ref, cwg_ref, cwv_ref, cbg_ref, cbv_ref, dp_ref, dcw_ref, dcb_ref,
             next_du, acc):
        tt = pl.program_id(1)
        t = nt - 1 - tt
        cwv, bias = _cat(cwg_ref[...], cwv_ref[...]), _cat(cbg_ref[...], cbv_ref[...])
        w0, w1, w2 = cwv[0:1], cwv[1:2], cwv[2:3]

        @pl.when(tt == 0)
        def _():
            next_du[...] = jnp.zeros_like(next_du)
            acc[...] = jnp.zeros_like(acc)

        def slab(blk, r0, carry):
            pm2, pm1, p0 = blk[6:6 + SLAB], blk[7:7 + SLAB], blk[8:8 + SLAB]
            u = bias + w0 * pm2 + w1 * pm1 + w2 * p0
            g, v = u[:, :CB], u[:, CB:]
            gel, th = _gelu(g)
            dav = da_ref[pl.ds(r0, SLAB), :]
            du = _cat(dav * v * _gelu_grad(g, th), dav * gel)
            ext = jnp.concatenate([du, carry], axis=0)
            dpv = (w2 * du + w1 * ext[1:1 + SLAB] + w0 * ext[2:2 + SLAB]).astype(dp_ref.dtype)
            dp_ref[0, pl.ds(r0, SLAB), :] = dpv[:, :CB]
            dp_ref[1, pl.ds(r0, SLAB), :] = dpv[:, CB:]
            acc[0] += _fold8(du)
            acc[1] += _fold8(du * pm2)
            acc[2] += _fold8(du * pm1)
            acc[3] += _fold8(du * p0)
            return du[0:SUBLANES]

        def loop(k, carry):
            r0 = pl.multiple_of((n_slab - 1 - k) * SLAB, SLAB)
            rows = pl.ds(pl.multiple_of(r0 - SUBLANES, SUBLANES), SLAB + SUBLANES)
            return slab(_cat(pg_ref[rows, :], pv_ref[rows, :]), r0, carry)

        carry = lax.fori_loop(0, n_slab - 1, loop, next_du[...], unroll=2)
        halo = jnp.where(t > 0, _cat(hg_ref[...], hv_ref[...]), 0.0)
        next_du[...] = slab(jnp.concatenate([halo, _cat(pg_ref[0:SLAB, :], pv_ref[0:SLAB, :])], axis=0), 0, carry)

        @pl.when(tt == nt - 1)
        def _():
            for half in range(2):
                cols = slice(half * CB, (half + 1) * CB)
                dcb_ref[half] = jnp.sum(acc[0][:, cols], axis=0, keepdims=True)
                for k in range(FFN_CONV):
                    dcw_ref[half, k:k + 1, :] = jnp.sum(acc[1 + k][:, cols], axis=0, keepdims=True)

    rev = lambda t: nt - 1 - t
    return pl.pallas_call(
        body, grid=(nb, nt),
        in_specs=([pl.BlockSpec((ts, CB), lambda j, t: (rev(t), j))] + _pair_specs((ts, CB), nb, lambda j, t: (rev(t),))
                  + _pair_specs((SUBLANES, CB), nb, _halo_row(ts, rev)) + _pair_specs((FFN_CONV, CB), nb, lambda j, t: (0,))
                  + _pair_specs((1, CB), nb, lambda j, t: (0,))),
        out_specs=[pl.BlockSpec((2, ts, CB), lambda j, t: (0, rev(t), j)),
                   pl.BlockSpec((2, FFN_CONV, CB), lambda j, t: (0, 0, j)),
                   pl.BlockSpec((2, 1, CB), lambda j, t: (0, 0, j))],
        out_shape=[jax.ShapeDtypeStruct((2, s, f2 // 2), MXU_DTYPE), jax.ShapeDtypeStruct((2, FFN_CONV, f2 // 2), F32),
                   jax.ShapeDtypeStruct((2, 1, f2 // 2), F32)],
        scratch_shapes=[pltpu.VMEM((SUBLANES, 2 * CB), F32), pltpu.VMEM((1 + FFN_CONV, SUBLANES, 2 * CB), F32)],
        compiler_params=_params("parallel", "arbitrary"), name=name,
    )(da, p, p, p, p, cw, cw, cb, cb)


def _rg_gates(xc, wa_ref, ba_ref, wx_ref, bx_ref, lam_ref):
    r = _sigmoid(_dot_nn(xc, wa_ref[0]) + ba_ref[...])
    ig = _sigmoid(_dot_nn(xc, wx_ref[0]) + bx_ref[...])
    sp = _softplus(-lam_ref[...])
    log_a = (-RG_C) * r * sp
    a = jnp.exp(log_a)
    mult = jnp.sqrt(_one_minus_exp(2.0 * log_a))
    return r, ig, sp, a, mult


def _rg_conv(scr, cw_ref, cb_ref, ts):
    views = [scr[5 + k:5 + k + ts, :] for k in range(RG_CONV)]
    xc = cb_ref[...]
    for k in range(RG_CONV):
        xc = xc + cw_ref[k:k + 1, :] * views[k]
    return xc, views


def _rg_param_specs():
    vec = pl.BlockSpec((1, CB), lambda g, t: (0, g))
    mat = pl.BlockSpec((1, CB, CB), lambda g, t: (g, 0, 0))
    return [pl.BlockSpec((RG_CONV, CB), lambda g, t: (0, g)), vec, mat, vec, mat, vec, vec]


NSEG = SUBLANES
NQ = CB // LANES


def _lanes(q):
    return slice(q * LANES, (q + 1) * LANES)


def _seg_scan(a_scr, x_scr, loc_scr, dec_scr, ts, reverse):
    seg = ts // NSEG

    def step(k, carry):
        out = []
        rows = pl.ds(seg - 1 - k if reverse else k, NSEG, stride=seg)
        for q in range(NQ):
            st, dec = carry[q]
            a_q, x_q, loc_q, dec_q = a_scr.at[q], x_scr.at[q], loc_scr.at[q], dec_scr.at[q]
            av = a_q[rows, :]
            if reverse:
                loc_q[rows, :] = st
                dec_q[rows, :] = dec
                st = av * (x_q[rows, :] + st)
                dec = av * dec
            else:
                st = av * st + x_q[rows, :]
                dec = av * dec
                loc_q[rows, :] = st
                dec_q[rows, :] = dec
            out.append((st, dec))
        return tuple(out)

    init = tuple((jnp.zeros((NSEG, LANES), F32), jnp.ones((NSEG, LANES), F32)) for _ in range(NQ))
    return lax.fori_loop(0, seg, step, init, unroll=4)


def _seg_chain(fin, dec, c_in, reverse):
    rows = [None] * NSEG
    c = c_in
    for sgm in (reversed(range(NSEG)) if reverse else range(NSEG)):
        rows[sgm] = c
        c = fin[sgm:sgm + 1] + dec[sgm:sgm + 1] * c
    return jnp.concatenate(rows, axis=0), c


def _rg_mid_fwd(pj, cw, cb, wa, ba, wx, bx, lam, name):
    s = pj.shape[0]
    nb = pj.shape[1] // (2 * CB)
    ts = _tile(s, (512,))
    nt = s // ts
    seg = ts // NSEG

    def body(gate_ref, x_ref, halo_ref, cw_ref, cb_ref, wa_ref, ba_ref, wx_ref, bx_ref, lam_ref, y_ref, hs_ref,
             scr, a_scr, u_scr, loc_scr, dec_scr, h_scr):
        t = pl.program_id(1)

        @pl.when(t == 0)
        def _():
            h_scr[...] = jnp.zeros_like(h_scr)

        scr[0:SUBLANES, :] = jnp.where(t > 0, halo_ref[...], 0.0)
        scr[SUBLANES:, :] = x_ref[...]
        xc, _ = _rg_conv(scr, cw_ref, cb_ref, ts)
        _, ig, _, a, mult = _rg_gates(xc, wa_ref, ba_ref, wx_ref, bx_ref, lam_ref)
        u = mult * (ig * xc)
        for q in range(NQ):
            a_scr[q] = a[:, _lanes(q)]
            u_scr[q] = u[:, _lanes(q)]
        fin = _seg_scan(a_scr, u_scr, loc_scr, dec_scr, ts, False)
        for q in range(NQ):
            enter, leave = _seg_chain(fin[q][0], fin[q][1], h_scr[0:1, _lanes(q)], False)
            h_scr[0:1, _lanes(q)] = leave
            for sgm in range(NSEG):
                rows = slice(sgm * seg, (sgm + 1) * seg)
                hs_ref[rows, _lanes(q)] = loc_scr[q, rows, :] + dec_scr[q, rows, :] * enter[sgm:sgm + 1]
        y_ref[...] = (_gelu(gate_ref[...])[0] * hs_ref[...]).astype(y_ref.dtype)

    blk = pl.BlockSpec((ts, CB), lambda g, t: (t, g))
    lane_scr = pltpu.VMEM((NQ, ts, LANES), F32)
    return pl.pallas_call(
        body, grid=(nb, nt),
        in_specs=_pair_specs((ts, CB), nb, lambda g, t: (t,))
        + [pl.BlockSpec((SUBLANES, CB), lambda g, t: _halo_row(ts, lambda u: u)(g, t) + (g + nb,))] + _rg_param_specs(),
        out_specs=[blk, blk],
        out_shape=[jax.ShapeDtypeStruct((s, nb * CB), MXU_DTYPE), jax.ShapeDtypeStruct((s, nb * CB), F32)],
        scratch_shapes=[pltpu.VMEM((ts + SUBLANES, CB), F32), lane_scr, lane_scr, lane_scr, lane_scr,
                        pltpu.VMEM((SUBLANES, CB), F32)],
        compiler_params=_params("parallel", "arbitrary"), name=name,
    )(pj, pj, pj, cw, cb, wa, ba, wx, bx, lam)


def _rg_mid_bwd(dy, pj, hs, cw, cb, wa, ba, wx, bx, lam, name):
    s = pj.shape[0]
    nb = pj.shape[1] // (2 * CB)
    ts = _tile(s, (512,))
    nt = s // ts

    def body(dy_ref, gate_ref, x_ref, halo_ref, hs_ref, hsh_ref, cw_ref, cb_ref, wa_ref, ba_ref, wx_ref, bx_ref, lam_ref,
             dpj_ref, dcw_ref, dcb_ref, dwa_ref, dba_ref, dwx_ref, dbx_ref, dlam_ref,
             scr, hscr, a_scr, d_scr, loc_scr, dec_scr, g_scr, dxscr, c_scr):
        tt = pl.program_id(1)
        t = nt - 1 - tt
        seg = ts // NSEG

        @pl.when(tt == 0)
        def _():
            c_scr[...] = jnp.zeros_like(c_scr)
            dxscr[ts:, :] = jnp.zeros((SUBLANES, CB), F32)
            for ref in (dcw_ref, dcb_ref, dwa_ref, dba_ref, dwx_ref, dbx_ref, dlam_ref):
                ref[...] = jnp.zeros_like(ref)

        scr[0:SUBLANES, :] = jnp.where(t > 0, halo_ref[...], 0.0)
        scr[SUBLANES:, :] = x_ref[...]
        hscr[0:SUBLANES, :] = jnp.where(t > 0, hsh_ref[...], 0.0)
        hscr[SUBLANES:, :] = hs_ref[...]
        xc, views = _rg_conv(scr, cw_ref, cb_ref, ts)
        r, ig, sp, a, mult = _rg_gates(xc, wa_ref, ba_ref, wx_ref, bx_ref, lam_ref)
        gate = gate_ref[...]
        gel, th = _gelu(gate)
        dyv = dy_ref[...]
        dpj_ref[0] = (dyv * hs_ref[...] * _gelu_grad(gate, th)).astype(dpj_ref.dtype)
        dhs = dyv * gel
        for q in range(NQ):
            a_scr[q] = a[:, _lanes(q)]
            d_scr[q] = dhs[:, _lanes(q)]
        fin = _seg_scan(a_scr, d_scr, loc_scr, dec_scr, ts, True)
        for q in range(NQ):
            enter, leave = _seg_chain(fin[q][0], fin[q][1], c_scr[0:1, _lanes(q)], True)
            c_scr[0:1, _lanes(q)] = leave
            for sgm in range(NSEG):
                rows = slice(sgm * seg, (sgm + 1) * seg)
                g_scr[rows, _lanes(q)] = d_scr[q, rows, :] + loc_scr[q, rows, :] + dec_scr[q, rows, :] * enter[sgm:sgm + 1]
        du = g_scr[...]
        da = du * hscr[7:7 + ts, :]
        dmult = du * (ig * xc)
        dig = du * (mult * xc)
        dxc = du * (mult * ig)
        dlog_a = da * a - dmult * (a * a / mult)
        dlam_ref[...] += jnp.sum(dlog_a * r, axis=0, keepdims=True) * (RG_C * _sigmoid(-lam_ref[...]))
        dpr = dlog_a * ((-RG_C) * sp) * (r * (1.0 - r))
        dpi = dig * (ig * (1.0 - ig))
        dba_ref[...] += jnp.sum(dpr, axis=0, keepdims=True)
        dbx_ref[...] += jnp.sum(dpi, axis=0, keepdims=True)
        dwa_ref[0] += _dot_tn(xc, dpr)
        dwx_ref[0] += _dot_tn(xc, dpi)
        dxc = dxc + _dot_nt(dpr, wa_ref[0]) + _dot_nt(dpi, wx_ref[0])
        dcb_ref[...] += jnp.sum(dxc, axis=0, keepdims=True)
        for k in range(RG_CONV):
            dcw_ref[k:k + 1, :] += jnp.sum(dxc * views[k], axis=0, keepdims=True)
        dxscr[0:ts, :] = dxc
        dxp = cw_ref[3:4, :] * dxc
        for k in range(RG_CONV - 1):
            dxp = dxp + cw_ref[k:k + 1, :] * dxscr[3 - k:3 - k + ts, :]
        dpj_ref[1] = dxp.astype(dpj_ref.dtype)
        dxscr[ts:, :] = dxscr[0:SUBLANES, :]

    rev = lambda g, t: (nt - 1 - t, g)
    rev_halo = lambda g, t: (jnp.maximum((nt - 1 - t) * (ts // SUBLANES) - 1, 0), g)
    vec = pl.BlockSpec((1, CB), lambda g, t: (0, g))
    mat = pl.BlockSpec((1, CB, CB), lambda g, t: (g, 0, 0))
    d = nb * CB
    vshape = jax.ShapeDtypeStruct((1, d), F32)
    mshape = jax.ShapeDtypeStruct((nb, CB, CB), F32)
    return pl.pallas_call(
        body, grid=(nb, nt),
        in_specs=[pl.BlockSpec((ts, CB), rev)] + _pair_specs((ts, CB), nb, lambda g, t: (nt - 1 - t,))
        + [pl.BlockSpec((SUBLANES, CB), lambda g, t: (rev_halo(g, t)[0], g + nb)),
           pl.BlockSpec((ts, CB), rev), pl.BlockSpec((SUBLANES, CB), rev_halo)] + _rg_param_specs(),
        out_specs=[pl.BlockSpec((2, ts, CB), lambda g, t: (0, nt - 1 - t, g)), pl.BlockSpec((RG_CONV, CB), lambda g, t: (0, g)),
                   vec, mat, vec, mat, vec, vec],
        out_shape=[jax.ShapeDtypeStruct((2, s, d), MXU_DTYPE), jax.ShapeDtypeStruct((RG_CONV, d), F32), vshape, mshape, vshape,
                   mshape, vshape, vshape],
        scratch_shapes=[pltpu.VMEM((ts + SUBLANES, CB), F32), pltpu.VMEM((ts + SUBLANES, CB), F32)]
        + [pltpu.VMEM((NQ, ts, LANES), F32)] * 4
        + [pltpu.VMEM((ts, CB), F32), pltpu.VMEM((ts + SUBLANES, CB), F32), pltpu.VMEM((SUBLANES, CB), F32)],
        compiler_params=_params("parallel", "arbitrary"), name=name,
    )(dy, pj, pj, pj, hs, hs, cw, cb, wa, ba, wx, bx, lam)


GLA_DK = 128
GLA_DV = 256
GLA_HB = 2 * GLA_DK + 2 * GLA_DV + LANES
GLA_TS = 256


def _split3(x):
    hi = x.astype(BF16)
    r1 = x - hi.astype(F32)
    mid = r1.astype(BF16)
    lo = (r1 - mid.astype(F32)).astype(BF16)
    return hi, mid, lo


def _chunk_cumsum(x, reverse):
    n = x.shape[0]
    i = lax.broadcasted_iota(jnp.int32, (n, n), 0)
    j = lax.broadcasted_iota(jnp.int32, (n, n), 1)
    same = (i // GLA_CHUNK) == (j // GLA_CHUNK)
    tri = jnp.where(same & ((j >= i) if reverse else (j <= i)), 1.0, 0.0).astype(BF16)
    out = jnp.zeros(x.shape, F32)
    for piece in _split3(x):
        out = out + lax.dot_general(tri, piece, (((1,), (0,)), ((), ())), preferred_element_type=F32)
    return out


def _gla_split(blk):
    q = blk[:, 0:GLA_DK] * (GLA_DK ** -0.5)
    k = blk[:, GLA_DK:2 * GLA_DK]
    v = blk[:, 2 * GLA_DK:2 * GLA_DK + GLA_DV]
    r = blk[:, 2 * GLA_DK + GLA_DV:2 * GLA_DK + 2 * GLA_DV]
    z = blk[:, 2 * GLA_DK + 2 * GLA_DV:]
    return q, k, v, r, z


def _gla_decays(gc):
    gref = gc[GLA_CHUNK // 2:GLA_CHUNK // 2 + 1, :]
    glast = gc[GLA_CHUNK - 1:GLA_CHUNK, :]
    return jnp.exp(gc), jnp.exp(gc - gref), jnp.exp(gref - gc), jnp.exp(glast - gc), jnp.exp(glast)


def _causal_mask():
    i = lax.broadcasted_iota(jnp.int32, (GLA_CHUNK, GLA_CHUNK), 0)
    j = lax.broadcasted_iota(jnp.int32, (GLA_CHUNK, GLA_CHUNK), 1)
    return j <= i


def _log_sigmoid(x):
    return jnp.minimum(x, 0.0) - _log1p_pos(jnp.exp(-jnp.abs(x)))


def _gla_mid_fwd(pj, wal, bal, ng, name):
    s = pj.shape[0]
    nh = pj.shape[1] // GLA_HB
    ts = _tile(s, (GLA_TS,))
    nt, nc = s // ts, ts // GLA_CHUNK

    def body(pj_ref, wal_ref, bal_ref, ng_ref, act_ref, o_ref, st_ref, s_scr):
        t = pl.program_id(0)

        @pl.when(t == 0)
        def _():
            s_scr[...] = jnp.zeros_like(s_scr)

        heads = []
        for h in range(nh):
            q, k, v, r, z = _gla_split(pj_ref[:, h * GLA_HB:(h + 1) * GLA_HB])
            g = _log_sigmoid(_dot_nn(z, wal_ref[h]) + bal_ref[h]) * (1.0 / GLA_TAU)
            heads.append((q, k, v, r, _chunk_cumsum(g, False)))
        mask = _causal_mask()
        for c in range(nc):
            sl = slice(c * GLA_CHUNK, (c + 1) * GLA_CHUNK)
            for h, (q, k, v, r, gcum) in enumerate(heads):
                eg, eq, ek, ekd, egl = _gla_decays(gcum[sl])
                st = s_scr[h]
                st_ref[c, h] = st
                attn = jnp.where(mask, _dot_nt(q[sl] * eq, k[sl] * ek), 0.0)
                o_ref[sl, h * GLA_DV:(h + 1) * GLA_DV] = _dot_nt(q[sl] * eg, st) + _dot_nn(attn, v[sl])
                s_scr[h] = st * egl + _dot_tn(v[sl], k[sl] * ekd)
        for h, (q, k, v, r, gcum) in enumerate(heads):
            cols = slice(h * GLA_DV, (h + 1) * GLA_DV)
            o = o_ref[:, cols]
            on = o * lax.rsqrt(jnp.mean(o * o, axis=-1, keepdims=True) + EPS)
            act_ref[:, cols] = ((on * ng_ref[...]) * (r * _sigmoid(r))).astype(act_ref.dtype)

    blk = pl.BlockSpec((ts, nh * GLA_DV), lambda t: (t, 0))
    whole = lambda shape: pl.BlockSpec(shape, lambda t: (0,) * len(shape))
    return pl.pallas_call(
        body, grid=(nt,),
        in_specs=[pl.BlockSpec((ts, nh * GLA_HB), lambda t: (t, 0)), whole((nh, LANES, GLA_DK)), whole((nh, 1, GLA_DK)),
                  whole((1, GLA_DV))],
        out_specs=[blk, blk, pl.BlockSpec((nc, nh, GLA_DV, GLA_DK), lambda t: (t, 0, 0, 0))],
        out_shape=[jax.ShapeDtypeStruct((s, nh * GLA_DV), MXU_DTYPE), jax.ShapeDtypeStruct((s, nh * GLA_DV), F32),
                   jax.ShapeDtypeStruct((s // GLA_CHUNK, nh, GLA_DV, GLA_DK), F32)],
        scratch_shapes=[pltpu.VMEM((nh, GLA_DV, GLA_DK), F32)],
        compiler_params=_params("arbitrary"), name=name,
    )(pj, wal, bal, ng)


def _gla_mid_bwd(dact, pj, o, st, wal, bal, ng, name):
    s = pj.shape[0]
    nh = pj.shape[1] // GLA_HB
    ts = _tile(s, (GLA_TS,))
    nt, nc = s // ts, ts // GLA_CHUNK
    o_q, o_k, o_v, o_r, o_z = 0, GLA_DK, 2 * GLA_DK, 2 * GLA_DK + GLA_DV, 2 * GLA_DK + 2 * GLA_DV

    def body(dact_ref, pj_ref, o_ref, st_ref, wal_ref, bal_ref, ng_ref, dpj_ref, dwal_ref, dbal_ref, dng_ref,
             ds_scr, dg_scr):
        tt = pl.program_id(0)

        @pl.when(tt == 0)
        def _():
            ds_scr[...] = jnp.zeros_like(ds_scr)
            dwal_ref[...] = jnp.zeros_like(dwal_ref)
            dbal_ref[...] = jnp.zeros_like(dbal_ref)
            dng_ref[...] = jnp.zeros_like(dng_ref)

        heads = []
        for h in range(nh):
            base = h * GLA_HB
            q, k, v, r, z = _gla_split(pj_ref[:, base:base + GLA_HB])
            logit = _dot_nn(z, wal_ref[h]) + bal_ref[h]
            gcum = _chunk_cumsum(_log_sigmoid(logit) * (1.0 / GLA_TAU), False)
            ov = o_ref[:, h * GLA_DV:(h + 1) * GLA_DV]
            ro = lax.rsqrt(jnp.mean(ov * ov, axis=-1, keepdims=True) + EPS)
            on = ov * ro
            sg = _sigmoid(r)
            sil = r * sg
            dav = dact_ref[:, h * GLA_DV:(h + 1) * GLA_DV]
            dpj_ref[:, base + o_r:base + o_z] = (dav * (on * ng_ref[...]) * (sg + sil * (1.0 - sg))).astype(dpj_ref.dtype)
            t1 = dav * sil
            dng_ref[...] += jnp.sum(t1 * on, axis=0, keepdims=True)
            dn = t1 * ng_ref[...]
            do = ro * (dn - on * jnp.mean(dn * on, axis=-1, keepdims=True))
            heads.append((q, k, v, z, logit, gcum, do))
        mask = _causal_mask()
        scale = GLA_DK ** -0.5
        last_row = lax.broadcasted_iota(jnp.int32, (GLA_CHUNK, GLA_DK), 0) == GLA_CHUNK - 1
        for c in reversed(range(nc)):
            sl = slice(c * GLA_CHUNK, (c + 1) * GLA_CHUNK)
            for h, (q, k, v, z, logit, gcum, do) in enumerate(heads):
                base = h * GLA_HB
                eg, eq, ek, ekd, egl = _gla_decays(gcum[sl])
                qc, kc, vc, doc = q[sl], k[sl], v[sl], do[sl]
                qg, qt, kt, kd = qc * eg, qc * eq, kc * ek, kc * ekd
                sp = st_ref[c, h]
                ds = ds_scr[h]
                attn = jnp.where(mask, _dot_nt(qt, kt), 0.0)
                dattn = jnp.where(mask, _dot_nt(doc, vc), 0.0)
                dqg = _dot_nn(doc, sp)
                dqt = _dot_nn(dattn, kt)
                dkt = _dot_tn(dattn, qt)
                dkd = _dot_nn(vc, ds)
                dpj_ref[sl, base + o_v:base + o_r] = (_dot_tn(attn, doc) + _dot_nt(kd, ds)).astype(dpj_ref.dtype)
                dpj_ref[sl, base + o_q:base + o_k] = (scale * (dqg * eg + dqt * eq)).astype(dpj_ref.dtype)
                dpj_ref[sl, base + o_k:base + o_v] = (dkt * ek + dkd * ekd).astype(dpj_ref.dtype)
                kdd = dkd * kd
                dgl = jnp.sum(kdd, axis=0, keepdims=True) + jnp.sum(ds * sp, axis=0, keepdims=True) * egl
                dg_scr[h, sl, :] = dqg * qg + dqt * qt - dkt * kt - kdd + jnp.where(last_row, dgl, 0.0)
                ds_scr[h] = ds * egl + _dot_tn(doc, qg)
        for h, (q, k, v, z, logit, gcum, do) in enumerate(heads):
            base = h * GLA_HB
            dlogit = _chunk_cumsum(dg_scr[h], True) * (1.0 / GLA_TAU) * _sigmoid(-logit)
            dpj_ref[:, base + o_z:base + GLA_HB] = _dot_nt(dlogit, wal_ref[h]).astype(dpj_ref.dtype)
            dwal_ref[h] += _dot_tn(z, dlogit)
            dbal_ref[h] += jnp.sum(dlogit, axis=0, keepdims=True)

    rev = lambda t: (nt - 1 - t, 0)
    whole = lambda shape: pl.BlockSpec(shape, lambda t: (0,) * len(shape))
    wide = pl.BlockSpec((ts, nh * GLA_DV), rev)
    return pl.pallas_call(
        body, grid=(nt,),
        in_specs=[wide, pl.BlockSpec((ts, nh * GLA_HB), rev), wide,
                  pl.BlockSpec((nc, nh, GLA_DV, GLA_DK), lambda t: (nt - 1 - t, 0, 0, 0)),
                  whole((nh, LANES, GLA_DK)), whole((nh, 1, GLA_DK)), whole((1, GLA_DV))],
        out_specs=[pl.BlockSpec((ts, nh * GLA_HB), rev), whole((nh, LANES, GLA_DK)), whole((nh, 1, GLA_DK)), whole((1, GLA_DV))],
        out_shape=[jax.ShapeDtypeStruct((s, nh * GLA_HB), MXU_DTYPE), jax.ShapeDtypeStruct((nh, LANES, GLA_DK), F32),
                   jax.ShapeDtypeStruct((nh, 1, GLA_DK), F32), jax.ShapeDtypeStruct((1, GLA_DV), F32)],
        scratch_shapes=[pltpu.VMEM((nh, GLA_DV, GLA_DK), F32), pltpu.VMEM((nh, ts, GLA_DK), F32)],
        compiler_params=_params("arbitrary"), name=name,
    )(dact, pj, o, st, wal, bal, ng)


def _adamw(w, gs, m, v, name, after=None):
    layers, rows, cols = w.shape
    gs = list(gs) if isinstance(gs, (list, tuple)) else gs
    n_g = len(gs) if isinstance(gs, list) else 1
    if rows % SUBLANES == 0:
        tr, tc = _tile(rows, (256, 128, 64, 32, 16, 8)), cols
    else:
        tr, tc = rows, _tile(cols, (256, 128))
    c1 = 1.0 / (1.0 - ADAM_B1 ** ADAM_STEP)
    c2 = 1.0 / (1.0 - ADAM_B2 ** ADAM_STEP)

    def body(*refs):
        g_refs, (w_ref, m_ref, v_ref) = refs[:n_g], refs[n_g:n_g + 3]
        go_ref, d_ref, mo_ref, vo_ref = refs[-4:]
        gv = g_refs[0][...]
        for l in range(1, n_g):
            gv = jnp.where(pl.program_id(0) == l, g_refs[l][...], gv)
        m2 = ADAM_B1 * m_ref[...] + (1.0 - ADAM_B1) * gv
        v2 = ADAM_B2 * v_ref[...] + (1.0 - ADAM_B2) * (gv * gv)
        d_ref[...] = (-ADAM_LR) * ((m2 * c1) / (jnp.sqrt(v2 * c2) + ADAM_EPS) + ADAM_WD * w_ref[...])
        go_ref[...] = gv
        mo_ref[...] = m2
        vo_ref[...] = v2

    spec = pl.BlockSpec((None, tr, tc), lambda l, i, j: (l, i, j))
    g_specs = [pl.BlockSpec((tr, tc), lambda l, i, j: (i, j))] * n_g if isinstance(gs, list) else [spec]
    extra = [] if after is None else [(after, _ANY)]
    shape = jax.ShapeDtypeStruct((layers, rows, cols), F32)
    return pl.pallas_call(
        body, grid=(layers, rows // tr, cols // tc), in_specs=g_specs + [spec] * 3 + [sp for _, sp in extra],
        out_specs=[spec] * 4, out_shape=[shape] * 4, compiler_params=_params("parallel", "parallel", "parallel"), name=name,
    )(*(gs if isinstance(gs, list) else [gs]), w, m, v, *[a for a, _ in extra])


def _gla_head_cols(w):
    d = w.shape[0]
    qk, dv = GLA_HEADS * GLA_DK, GLA_HEADS * GLA_DV
    q, k, v, r, z = jnp.split(w, [qk, 2 * qk, 2 * qk + dv, 2 * qk + 2 * dv], axis=1)
    zp = jnp.pad(z, ((0, 0), (0, LANES - GLA_RANK)))
    parts = [q.reshape(d, GLA_HEADS, GLA_DK), k.reshape(d, GLA_HEADS, GLA_DK), v.reshape(d, GLA_HEADS, GLA_DV),
             r.reshape(d, GLA_HEADS, GLA_DV), jnp.broadcast_to(zp[:, None, :], (d, GLA_HEADS, LANES))]
    return jnp.concatenate(parts, axis=2).reshape(d, GLA_HEADS * GLA_HB)


def _gla_unhead_cols(w):
    d = w.shape[0]
    w = w.reshape(d, GLA_HEADS, GLA_HB)
    o = 2 * GLA_DK + 2 * GLA_DV
    parts = [w[:, :, 0:GLA_DK].reshape(d, -1), w[:, :, GLA_DK:2 * GLA_DK].reshape(d, -1),
             w[:, :, 2 * GLA_DK:2 * GLA_DK + GLA_DV].reshape(d, -1), w[:, :, 2 * GLA_DK + GLA_DV:o].reshape(d, -1),
             jnp.sum(w[:, :, o:o + GLA_RANK], axis=1)]
    return jnp.concatenate(parts, axis=1)


def _gla_alpha_heads(w_alpha, b_alpha):
    wal = jnp.swapaxes(w_alpha.reshape(GLA_RANK, GLA_HEADS, GLA_DK), 0, 1)
    return jnp.pad(wal, ((0, 0), (0, LANES - GLA_RANK), (0, 0))), b_alpha.reshape(GLA_HEADS, 1, GLA_DK)


def _gla_layouts(w):
    w = dict(w)
    w["gla_wal"], w["gla_bal"] = _gla_alpha_heads(w["gla_w_alpha"], w["gla_b_alpha"])
    w["gla_w_in"] = _gla_head_cols(w["gla_w_in"])
    return w


def _col_slots(w):
    r, c = w.shape
    return jnp.moveaxis(w.reshape(r, N_CHIP, c // N_CHIP), 1, 0)


def _from_col_slots(w):
    n, r, c = w.shape
    return jnp.moveaxis(w, 0, 1).reshape(r, n * c)


def _block_rows_to_slots(w):
    g, r4, cc = w.shape
    return jnp.swapaxes(w.reshape(g, N_CHIP, r4 // N_CHIP, cc), 0, 1).reshape(N_CHIP, g * (r4 // N_CHIP), cc)


def _slots_to_block_rows(w, g):
    n, gr, cc = w.shape
    return jnp.swapaxes(w.reshape(n, g, gr // g, cc), 0, 1).reshape(g, n * (gr // g), cc)


def _local_step(x, tgt, mod, w, fetch=None, done=None, later=None):
    depth = mod.shape[0]
    row = lambda v: v.reshape(1, -1)
    w = dict(w)
    w["ffn_w_up"], w["ffn_w_down"] = dict(enumerate(w["ffn_w_up"])), dict(enumerate(w["ffn_w_down"]))

    def arrive(stage, after):
        if fetch is not None:
            for k, v in fetch(stage, after).items():
                if isinstance(v, dict):
                    w[k].update(v)
                else:
                    w[k] = v

    saved = []
    for i in range(depth):
        if i == 1:
            arrive("l1", x)
        sh_m, sc_m, gt_m, sh_f, sc_f, gt_f = (mod[i, j:j + 1] for j in range(6))
        g0, g1, g2, g3 = (w["norm_g"][i, j:j + 1] for j in range(4))
        tag = f"_l{i}"
        h = _norm_mod_fwd(x, g0, sc_m, sh_m, "norm_mix" + tag)
        if i % 2 == 0:
            pj = _mm(h, w["rg_w_in"], w_slots=N_CHIP, name="rg_in" + tag)
            act, aux = _rg_mid_fwd(pj, w["rg_conv_w"], row(w["rg_conv_b"]), w["rg_wa"], row(w["rg_ba"]), w["rg_wx"],
                                   row(w["rg_bx"]), row(w["rg_lambda"]), "rg_mid" + tag)
            y = _mm(act, w["rg_w_out"], name="rg_out" + tag)
        else:
            pj = _mm(h, w["gla_w_in"], name="gla_in" + tag)
            act, *aux = _gla_mid_fwd(pj, w["gla_wal"], w["gla_bal"], row(w["gla_norm_g"]), "gla_mid" + tag)
            y = _mm(act, w["gla_w_out"], name="gla_out" + tag)
        x1 = _post_fwd(x, y, g1, gt_m, "post_mix" + tag)
        if i == 0:
            arrive("ffn0", x1)
        h2 = _norm_mod_fwd(x1, g2, sc_f, sh_f, "norm_ffn" + tag)
        p = _mm(h2, w["ffn_w_up"][i], w_slots=N_CHIP, name="ffn_up" + tag)
        a = _ffn_mid_fwd(p, w["ffn_conv_w"][i], w["ffn_conv_b"][i:i + 1], "ffn_mid" + tag)
        y2 = _mm(a, w["ffn_w_down"][i], name="ffn_down" + tag)
        x2 = _post_fwd(x1, y2, g3, gt_f, "post_ffn" + tag)
        saved.append((x, h, pj, act, aux, y, x1, h2, p, a, y2))
        x = x2

    cols, dx = _loss_grad(x, tgt, "loss")

    stacked = ("norm_g", "ffn_conv_w", "ffn_conv_b", "mod")
    gr = {k: [None] * depth for k in stacked + ("ffn_w_up", "ffn_w_down")}
    told = lambda stage: done(stage, gr) if done is not None else 0.0
    told_later = lambda stage, after: later(stage, after) if later is not None else 0.0
    for i in reversed(range(depth)):
        x0, h, pj, act, aux, y, x1, h2, p, a, y2 = saved[i]
        sh_m, sc_m, gt_m, sh_f, sc_f, gt_f = (mod[i, j:j + 1] for j in range(6))
        g0, g1, g2, g3 = (w["norm_g"][i, j:j + 1] for j in range(4))
        tag = f"_l{i}"
        dy2, d_g3, d_gt_f = _post_bwd(dx, y2, g3, gt_f, "post_ffn_b" + tag)
        da = _mm(dy2, w["ffn_w_down"][i], tb=True, name="ffn_down_dx" + tag)
        gr["ffn_w_down"][i] = _mm(a, dy2, ta=True, name="ffn_down_dw" + tag)
        conv_b = w["ffn_conv_b"][i:i + 1] + (told_later("l1", da) if i == 0 else 0.0)
        dp, dcw, dcb = _ffn_mid_bwd(da, p, w["ffn_conv_w"][i], conv_b, "ffn_mid_b" + tag)
        gr["ffn_conv_w"][i], gr["ffn_conv_b"][i] = _cat(dcw[0], dcw[1]), _cat(dcb[0], dcb[1])[0]
        dh2 = _mm(dp, w["ffn_w_up"][i], tb=True, a_parts=2, w_slots=N_CHIP, name="ffn_up_dx" + tag)
        gr["ffn_w_up"][i] = _mm(h2, dp, ta=True, b_parts=2, out_slots=N_CHIP, name="ffn_up_dw" + tag)
        dx1, d_g2, d_sc_f, d_sh_f = _norm_mod_bwd(dh2, x1, g2, sc_f, dx, "norm_ffn_b" + tag)
        if i == 0:
            gt_m = gt_m + told("ffn0")
        dy, d_g1, d_gt_m = _post_bwd(dx1, y, g1, gt_m, "post_mix_b" + tag)
        if i % 2 == 0:
            dact = _mm(dy, w["rg_w_out"], tb=True, name="rg_out_dx" + tag)
            gr["rg_w_out"] = _mm(act, dy, ta=True, name="rg_out_dw" + tag)
            lam = row(w["rg_lambda"]) + told_later("ffn0", gr["rg_w_out"])
            dpj, gr["rg_conv_w"], d_cb, gr["rg_wa"], d_ba, gr["rg_wx"], d_bx, d_lam = _rg_mid_bwd(
                dact, pj, aux, w["rg_conv_w"], row(w["rg_conv_b"]), w["rg_wa"], row(w["rg_ba"]), w["rg_wx"],
                row(w["rg_bx"]), lam, "rg_mid_b" + tag)
            gr["rg_conv_b"], gr["rg_ba"], gr["rg_bx"], gr["rg_lambda"] = d_cb[0], d_ba[0], d_bx[0], d_lam[0]
            dh = _mm(dpj, w["rg_w_in"], tb=True, a_parts=2, w_slots=N_CHIP, name="rg_in_dx" + tag)
            gr["rg_w_in"] = _mm(h, dpj, ta=True, b_parts=2, out_slots=N_CHIP, name="rg_in_dw" + tag)
        else:
            dact = _mm(dy, w["gla_w_out"], tb=True, name="gla_out_dx" + tag)
            gr["gla_w_out"] = _mm(act, dy, ta=True, name="gla_out_dw" + tag)
            dpj, d_wal, d_bal, d_ng = _gla_mid_bwd(dact, pj, aux[0], aux[1], w["gla_wal"], w["gla_bal"],
                                                   row(w["gla_norm_g"]), "gla_mid_b" + tag)
            gr["gla_w_alpha"] = jnp.swapaxes(d_wal[:, :GLA_RANK, :], 0, 1).reshape(GLA_RANK, GLA_HEADS * GLA_DK)
            gr["gla_b_alpha"], gr["gla_norm_g"] = d_bal.reshape(-1), d_ng[0]
            dh = _mm(dpj, w["gla_w_in"], tb=True, name="gla_in_dx" + tag)
            gr["gla_w_in"] = _gla_unhead_cols(_mm(h, dpj, ta=True, name="gla_in_dw" + tag))
            mod = mod.at[0].add(told("l1"))
        dx, d_g0, d_sc_m, d_sh_m = _norm_mod_bwd(dh, x0, g0, sc_m, dx1, "norm_mix_b" + tag)
        gr["norm_g"][i] = jnp.concatenate([d_g0, d_g1, d_g2, d_g3], axis=0)
        gr["mod"][i] = jnp.concatenate([d_sh_m, d_sc_m, d_gt_m, d_sh_f, d_sc_f, d_gt_f], axis=0)
    for k in stacked:
        gr[k] = jnp.stack(gr[k])
    return cols, dx, gr


ADA_ROWS = 16


def _ada_fwd(c16, ada_w, ada_b, name):
    depth, d, n = ada_w.shape
    tn = _tile(n, (512, 256, 128))

    def body(c_ref, w_ref, b_ref, o_ref):
        cv = c_ref[...]
        o_ref[0] = _dot_nn(cv * _sigmoid(cv), w_ref[0]) + b_ref[0]

    return pl.pallas_call(
        body, grid=(depth, n // tn),
        in_specs=[pl.BlockSpec((ADA_ROWS, d), lambda l, j: (0, 0)), pl.BlockSpec((1, d, tn), lambda l, j: (l, 0, j)),
                  pl.BlockSpec((1, 1, tn), lambda l, j: (l, 0, j))],
        out_specs=pl.BlockSpec((1, ADA_ROWS, tn), lambda l, j: (l, 0, j)),
        out_shape=jax.ShapeDtypeStruct((depth, ADA_ROWS, n), F32),
        compiler_params=_params("parallel", "parallel"), name=name,
    )(c16, ada_w, ada_b)


def _ada_bwd(c16, dmod16, name):
    depth, _, n = dmod16.shape
    d = c16.shape[1]
    tn = _tile(n, (512, 256, 128))

    def body(c_ref, dm_ref, o_ref):
        cv = c_ref[...]
        o_ref[0] = _dot_tn(cv * _sigmoid(cv), dm_ref[0])

    return pl.pallas_call(
        body, grid=(depth, n // tn),
        in_specs=[pl.BlockSpec((ADA_ROWS, d), lambda l, j: (0, 0)), pl.BlockSpec((1, ADA_ROWS, tn), lambda l, j: (l, 0, j))],
        out_specs=pl.BlockSpec((1, d, tn), lambda l, j: (l, 0, j)),
        out_shape=jax.ShapeDtypeStruct((depth, d, n), F32),
        compiler_params=_params("parallel", "parallel"), name=name,
    )(c16, dmod16)


PACK_COLS = 1024
_ANY = pl.BlockSpec(memory_space=pl.ANY)
_VMEM = pl.BlockSpec(memory_space=pltpu.VMEM)


def _place():
    return lax.axis_index("x"), lax.axis_index("y"), lax.axis_index("c")


def _other_chips(x, y):
    return [(1 - x, y), (x, 1 - y), (1 - x, 1 - y)]


def _rcopy(src, dst, send_sems, recv_sems, k, peer):
    return pltpu.make_async_remote_copy(src_ref=src, dst_ref=dst, send_sem=send_sems.at[k], recv_sem=recv_sems.at[k],
                                        device_id=peer, device_id_type=MESH)


def _all_gather_8(v, name):
    r, cc = v.shape

    def body(v_ref, out_ref, send_sems, recv_sems, local_sem):
        x, y, c = _place()
        me = 4 * x + 2 * y + c
        mine = pltpu.make_async_copy(v_ref, out_ref.at[me], local_sem)
        mine.start()
        peers = []
        for k in range(1, N_DEV):
            px = 1 - x if k & 4 else x
            py = 1 - y if k & 2 else y
            pc = 1 - c if k & 1 else c
            peers.append((px, py, pc))
        sends = [_rcopy(v_ref, out_ref.at[me], send_sems, recv_sems, k, p) for k, p in enumerate(peers)]
        for cp in sends:
            cp.start()
        for k, (px, py, pc) in enumerate(peers):
            _rcopy(v_ref, out_ref.at[4 * px + 2 * py + pc], send_sems, recv_sems, k, (px, py, pc)).wait_recv()
        for cp in sends:
            cp.wait_send()
        mine.wait()

    return pl.pallas_call(
        body, in_specs=[_VMEM], out_specs=_VMEM, out_shape=jax.ShapeDtypeStruct((N_DEV, r, cc), v.dtype),
        scratch_shapes=[pltpu.SemaphoreType.DMA((N_DEV - 1,)), pltpu.SemaphoreType.DMA((N_DEV - 1,)), pltpu.SemaphoreType.DMA],
        compiler_params=pltpu.CompilerParams(vmem_limit_bytes=VMEM_LIMIT), name=name,
    )(v)


def _gather_chips(shards, name):
    n = len(shards)
    per = 2 * (N_CHIP - 1)

    def body(*refs):
        ins, outs, (send_sems, recv_sems) = refs[:n], refs[n:2 * n], refs[2 * n:]
        x, y, c = _place()
        chip = 2 * x + y
        chips = _other_chips(x, y)
        rows = [(pl.ds(c * (r.shape[0] // 2), r.shape[0] // 2), pl.ds((1 - c) * (r.shape[0] // 2), r.shape[0] // 2)) for r in ins]
        first = [_rcopy(ins[i].at[rows[i][0]], outs[i].at[chip, rows[i][0]], send_sems, recv_sems, per * i + j, (px, py, c))
                 for i in range(n) for j, (px, py) in enumerate(chips)]
        for cp in first:
            cp.start()
        passed = []
        for i in range(n):
            for j, (px, py) in enumerate(chips):
                landed = outs[i].at[2 * px + py, rows[i][0]]
                _rcopy(ins[i].at[rows[i][0]], landed, send_sems, recv_sems, per * i + j, (px, py, c)).wait_recv()
                fw = _rcopy(landed, landed, send_sems, recv_sems, per * i + N_CHIP - 1 + j, (x, y, 1 - c))
                fw.start()
                passed.append(fw)
        for i in range(n):
            for j, (px, py) in enumerate(chips):
                landed = outs[i].at[2 * px + py, rows[i][1]]
                _rcopy(landed, landed, send_sems, recv_sems, per * i + N_CHIP - 1 + j, (x, y, 1 - c)).wait_recv()
        for cp in first + passed:
            cp.wait_send()

    return pl.pallas_call(
        body, in_specs=[_ANY] * n, out_specs=[_ANY] * n,
        out_shape=[jax.ShapeDtypeStruct((N_CHIP,) + sh.shape, sh.dtype) for sh in shards],
        scratch_shapes=[pltpu.SemaphoreType.DMA((per * n,)), pltpu.SemaphoreType.DMA((per * n,))], name=name,
    )(*shards)


def _pair_exchange(gs, name):
    n = len(gs)

    def body(*refs):
        ins, outs, (send_sems, recv_sems) = refs[:n], refs[n:2 * n], refs[2 * n:]
        x, y, c = _place()
        copies = []
        for i in range(n):
            half = ins[i].shape[1] // 2
            copies.append(_rcopy(ins[i].at[:, pl.ds((1 - c) * half, half)], outs[i], send_sems, recv_sems, i, (x, y, 1 - c)))
        for cp in copies:
            cp.start()
        for cp in copies:
            cp.wait()

    return pl.pallas_call(
        body, in_specs=[_ANY] * n, out_specs=[_ANY] * n,
        out_shape=[jax.ShapeDtypeStruct((g.shape[0], g.shape[1] // 2, g.shape[2]), g.dtype) for g in gs],
        scratch_shapes=[pltpu.SemaphoreType.DMA((n,)), pltpu.SemaphoreType.DMA((n,))], name=name,
    )(*gs)


_ROW_TILES = (640, 512, 352, 256, 128, 64, 32, 16)


def _pair_sum(g, other, c_idx, name):
    n, half, cc = other.shape
    tr = _tile(half, _ROW_TILES)

    def body(c_ref, g_ref, o_ref, out_ref):
        out_ref[...] = (g_ref[...] + o_ref[...]).astype(out_ref.dtype)

    return pl.pallas_call(
        body,
        grid_spec=pltpu.PrefetchScalarGridSpec(
            num_scalar_prefetch=1, grid=(n, half // tr),
            in_specs=[pl.BlockSpec((None, None, tr, cc), lambda k, i, c_ref: (k, c_ref[0], i, 0)),
                      pl.BlockSpec((None, tr, cc), lambda k, i, c_ref: (k, i, 0))],
            out_specs=pl.BlockSpec((None, tr, cc), lambda k, i, c_ref: (k, i, 0))),
        out_shape=jax.ShapeDtypeStruct((n, half, cc), BF16),
        compiler_params=_params("parallel", "parallel"), name=name,
    )(c_idx, g.reshape(n, 2, half, cc), other)


def _chip_exchange(ps, name):
    n = len(ps)
    per = N_CHIP - 1

    def body(*refs):
        ins, outs, (send_sems, recv_sems) = refs[:n], refs[n:2 * n], refs[2 * n:]
        x, y, c = _place()
        chip = 2 * x + y
        chips = _other_chips(x, y)
        sends = [_rcopy(ins[i].at[2 * px + py], outs[i].at[chip], send_sems, recv_sems, per * i + j, (px, py, c))
                 for i in range(n) for j, (px, py) in enumerate(chips)]
        for cp in sends:
            cp.start()
        for i in range(n):
            for j, (px, py) in enumerate(chips):
                _rcopy(ins[i].at[chip], outs[i].at[2 * px + py], send_sems, recv_sems, per * i + j, (px, py, c)).wait_recv()
        for cp in sends:
            cp.wait_send()

    return pl.pallas_call(
        body, in_specs=[_ANY] * n, out_specs=[_ANY] * n, out_shape=[jax.ShapeDtypeStruct(p.shape, p.dtype) for p in ps],
        scratch_shapes=[pltpu.SemaphoreType.DMA((per * n,)), pltpu.SemaphoreType.DMA((per * n,))], name=name,
    )(*ps)


_HBM = pl.BlockSpec(memory_space=pltpu.HBM)
_SEM = pl.BlockSpec(memory_space=pltpu.SEMAPHORE)
_DATAFLOW = pltpu.SideEffectType.DATAFLOW_SIDE_EFFECTING


def _split_copies(srcs, lands, send_sems, recv_sems, mode, arriving):
    x, y, c = _place()
    chip = 2 * x + y
    out = []
    for i, (src, land) in enumerate(zip(srcs, lands)):
        if mode == "pair":
            half = src.shape[1] // 2
            out.append(_rcopy(src.at[:, pl.ds((1 - c) * half, half)], land, send_sems, recv_sems, i, (x, y, 1 - c)))
            continue
        for j, (px, py) in enumerate(_other_chips(x, y)):
            there = 2 * px + py
            part = src.at[there] if mode == "slots" else src
            out.append(_rcopy(part, land.at[there if arriving else chip], send_sems, recv_sems, (N_CHIP - 1) * i + j, (px, py, c)))
    return out


def _land_shape(src, mode):
    if mode == "pair":
        return (src.shape[0], src.shape[1] // 2, src.shape[2])
    return (N_CHIP,) + (src.shape[1:] if mode == "slots" else src.shape)


def _send_start(srcs, mode, name):
    n = len(srcs)
    n_sem = n if mode == "pair" else (N_CHIP - 1) * n
    lands = [lax.empty(_land_shape(s, mode), s.dtype) for s in srcs]

    def body(*refs):
        ins, zones, (send_sems, recv_sems) = refs[:n], refs[n:2 * n], refs[2 * n:2 * n + 2]
        for cp in _split_copies(ins, zones, send_sems, recv_sems, mode, False):
            cp.start()
        refs[-1][...] = jnp.zeros_like(refs[-1])

    hbm = lambda a: pltpu.HBM(a.shape, a.dtype)
    outs = pl.pallas_call(
        body, name=name, in_specs=[_HBM] * (2 * n),
        out_shape=(pltpu.SemaphoreType.DMA((n_sem,)), pltpu.SemaphoreType.DMA((n_sem,)), *[hbm(a) for a in srcs],
                   *[hbm(a) for a in lands], jax.ShapeDtypeStruct((SUBLANES, LANES), F32)),
        out_specs=(_SEM, _SEM, *[_HBM] * (2 * n), _VMEM), input_output_aliases={i: 2 + i for i in range(2 * n)},
        compiler_params=pltpu.CompilerParams(has_side_effects=_DATAFLOW),
    )(*[pltpu.with_memory_space_constraint(a, pltpu.HBM) for a in list(srcs) + lands])
    return (outs[0], outs[1], list(outs[2:2 + n]), list(outs[2 + n:2 + 2 * n])), outs[-1]


def _send_wait(state, after, mode, name):
    send_sems, recv_sems, srcs, lands = state
    n = len(srcs)

    def body(*refs):
        ins, zones, (send_s, recv_s) = refs[:n], refs[n:2 * n], refs[2 * n:2 * n + 2]
        for cp in _split_copies(ins, zones, send_s, recv_s, mode, True):
            cp.wait_send()
            cp.wait_recv()

    hbm = lambda a: pltpu.HBM(a.shape, a.dtype)
    outs = pl.pallas_call(
        body, name=name, in_specs=[_HBM] * (2 * n) + [_SEM, _SEM, _ANY],
        out_shape=tuple(hbm(a) for a in srcs + lands), out_specs=tuple([_HBM] * (2 * n)),
        input_output_aliases={i: i for i in range(2 * n)},
        compiler_params=pltpu.CompilerParams(has_side_effects=_DATAFLOW),
    )(*srcs, *lands, send_sems, recv_sems, after)
    return list(outs[n:])


def _sum_lead(v, name):
    n, r, cc = v.shape
    tr = _tile(r, _ROW_TILES + (8,))

    def body(v_ref, o_ref):
        acc = v_ref[0].astype(F32)
        for k in range(1, n):
            acc = acc + v_ref[k].astype(F32)
        o_ref[...] = acc

    return pl.pallas_call(
        body, grid=(r // tr,), in_specs=[pl.BlockSpec((n, tr, cc), lambda i: (0, i, 0))],
        out_specs=pl.BlockSpec((tr, cc), lambda i: (i, 0)), out_shape=jax.ShapeDtypeStruct((r, cc), F32),
        compiler_params=_params("parallel"), name=name,
    )(v)


def _pair_share(reds, name):
    n = len(reds)

    def body(*refs):
        ins, outs, (send_sems, recv_sems) = refs[:n], refs[n:2 * n], refs[2 * n:]
        x, y, c = _place()
        copies = [_rcopy(ins[i], outs[i].at[c], send_sems, recv_sems, i, (x, y, 1 - c)) for i in range(n)]
        for cp in copies:
            cp.start()
        for i in range(n):
            _rcopy(ins[i], outs[i].at[1 - c], send_sems, recv_sems, i, (x, y, 1 - c)).wait_recv()
        for cp in copies:
            cp.wait_send()

    return pl.pallas_call(
        body, in_specs=[_ANY] * n, out_specs=[_ANY] * n, out_shape=[jax.ShapeDtypeStruct((2,) + r.shape, r.dtype) for r in reds],
        scratch_shapes=[pltpu.SemaphoreType.DMA((n,)), pltpu.SemaphoreType.DMA((n,))], name=name,
    )(*reds)


def _pack(arrs, rows_multiple, dtype):
    flat = jnp.concatenate([a.reshape(-1).astype(dtype) for a in arrs])
    unit = rows_multiple * PACK_COLS
    total = -(-flat.shape[0] // unit) * unit
    return jnp.pad(flat, (0, total - flat.shape[0])).reshape(-1, PACK_COLS)


def _unpack(buf, shapes):
    lead = buf.shape[:-2]
    flat = buf.reshape(*lead, -1)
    out, off = [], 0
    for shp in shapes:
        n = 1
        for s in shp:
            n *= s
        out.append(flat[..., off:off + n].reshape(*lead, *shp))
        off += n
    return out


def _join_shards(parts, axis):
    moved = jnp.moveaxis(parts, 0, axis)
    shp = list(moved.shape)
    shp[axis:axis + 2] = [shp[axis] * shp[axis + 1]]
    return moved.reshape(shp)


def _my_shard(full, axis, chip):
    n = full.shape[axis] // N_CHIP
    return lax.dynamic_slice_in_dim(full, chip * n, n, axis)


SMALL = {"norm_g": 2, "ffn_conv_w": 2, "rg_conv_w": 2, "gla_w_alpha": 2, "gla_b_alpha": 1, "gla_norm_g": 1,
         "ada_b": None, "ffn_conv_b": None, "rg_conv_b": None, "rg_ba": None, "rg_bx": None, "rg_lambda": None}
BIG = {"rg_w_in": True, "rg_wa": False, "rg_wx": False, "rg_w_out": False, "ffn_w_up": True, "ffn_w_down": False,
       "gla_w_in": True, "gla_w_out": False}
WEIGHTS = ["ada_w", "ada_b", "norm_g", "ffn_w_up", "ffn_conv_w", "ffn_conv_b", "ffn_w_down", "rg_w_in", "rg_conv_w", "rg_conv_b",
           "rg_wa", "rg_ba", "rg_wx", "rg_bx", "rg_lambda", "rg_w_out", "gla_w_in", "gla_w_alpha", "gla_b_alpha", "gla_norm_g",
           "gla_w_out"]


def kernel(x, c, ada_w, ada_b, norm_g, ffn_w_up, ffn_conv_w, ffn_conv_b, ffn_w_down, rg_w_in, rg_conv_w, rg_conv_b, rg_wa, rg_ba, rg_wx, rg_bx, rg_lambda, rg_w_out, gla_w_in, gla_w_alpha, gla_b_alpha, gla_norm_g, gla_w_out, loss_target, m_ada_w, m_ada_b, m_norm_g, m_ffn_w_up, m_ffn_conv_w, m_ffn_conv_b, m_ffn_w_down, m_rg_w_in, m_rg_conv_w, m_rg_conv_b, m_rg_wa, m_rg_ba, m_rg_wx, m_rg_bx, m_rg_lambda, m_rg_w_out, m_gla_w_in, m_gla_w_alpha, m_gla_b_alpha, m_gla_norm_g, m_gla_w_out, v_ada_w, v_ada_b, v_norm_g, v_ffn_w_up, v_ffn_conv_w, v_ffn_conv_b, v_ffn_w_down, v_rg_w_in, v_rg_conv_w, v_rg_conv_b, v_rg_wa, v_rg_ba, v_rg_wx, v_rg_bx, v_rg_lambda, v_rg_w_out, v_gla_w_in, v_gla_w_alpha, v_gla_b_alpha, v_gla_norm_g, v_gla_w_out):
    wts = dict(ada_w=ada_w, ada_b=ada_b, norm_g=norm_g, ffn_w_up=ffn_w_up, ffn_conv_w=ffn_conv_w, ffn_conv_b=ffn_conv_b,
               ffn_w_down=ffn_w_down, rg_w_in=rg_w_in, rg_conv_w=rg_conv_w, rg_conv_b=rg_conv_b, rg_wa=rg_wa, rg_ba=rg_ba,
               rg_wx=rg_wx, rg_bx=rg_bx, rg_lambda=rg_lambda, rg_w_out=rg_w_out, gla_w_in=gla_w_in, gla_w_alpha=gla_w_alpha,
               gla_b_alpha=gla_b_alpha, gla_norm_g=gla_norm_g, gla_w_out=gla_w_out)
    mom1 = dict(ada_w=m_ada_w, ada_b=m_ada_b, norm_g=m_norm_g, ffn_w_up=m_ffn_w_up, ffn_conv_w=m_ffn_conv_w,
                ffn_conv_b=m_ffn_conv_b, ffn_w_down=m_ffn_w_down, rg_w_in=m_rg_w_in, rg_conv_w=m_rg_conv_w,
                rg_conv_b=m_rg_conv_b, rg_wa=m_rg_wa, rg_ba=m_rg_ba, rg_wx=m_rg_wx, rg_bx=m_rg_bx, rg_lambda=m_rg_lambda,
                rg_w_out=m_rg_w_out, gla_w_in=m_gla_w_in, gla_w_alpha=m_gla_w_alpha, gla_b_alpha=m_gla_b_alpha,
                gla_norm_g=m_gla_norm_g, gla_w_out=m_gla_w_out)
    mom2 = dict(ada_w=v_ada_w, ada_b=v_ada_b, norm_g=v_norm_g, ffn_w_up=v_ffn_w_up, ffn_conv_w=v_ffn_conv_w,
                ffn_conv_b=v_ffn_conv_b, ffn_w_down=v_ffn_w_down, rg_w_in=v_rg_w_in, rg_conv_w=v_rg_conv_w,
                rg_conv_b=v_rg_conv_b, rg_wa=v_rg_wa, rg_ba=v_rg_ba, rg_wx=v_rg_wx, rg_bx=v_rg_bx, rg_lambda=v_rg_lambda,
                rg_w_out=v_rg_w_out, gla_w_in=v_gla_w_in, gla_w_alpha=v_gla_w_alpha, gla_b_alpha=v_gla_b_alpha,
                gla_norm_g=v_gla_norm_g, gla_w_out=v_gla_w_out)
    xi, yi, ci = _place()
    chip, me = 2 * xi + yi, 4 * xi + 2 * yi + ci
    d = x.shape[-1]
    depth = ada_w.shape[0]
    n_ada = ada_w.shape[-1]
    sharded_small = [k for k, ax in SMALL.items() if ax is not None]

    sm = _all_gather_8(_pack([c] + [wts[k] for k in sharded_small], SUBLANES, F32), "gather_small")
    c_all = sm[:, 0, :]
    parts = _unpack(sm[0::2], [c.shape] + [wts[k].shape for k in sharded_small])[1:]
    full = {k: _join_shards(p, SMALL[k]) for k, p in zip(sharded_small, parts)}
    for k, ax in SMALL.items():
        if ax is None:
            full[k] = wts[k]

    c16 = jnp.pad(c_all, ((0, ADA_ROWS - N_DEV), (0, 0)))
    ada_b_mine = lax.dynamic_slice_in_dim(ada_b, chip * n_ada, n_ada, 1)[:, None, :]
    mod_cols = _ada_fwd(c16, ada_w, ada_b_mine, "ada_fwd")
    mod_all = _all_gather_8(mod_cols.reshape(-1, PACK_COLS), "gather_mod")[0::2].reshape(N_CHIP, depth, ADA_ROWS, n_ada)
    mod = jnp.swapaxes(lax.dynamic_index_in_dim(mod_all, me, 2, keepdims=False), 0, 1).reshape(depth, 6, d)

    items = [(k, l) for k in BIG for l in range(wts[k].shape[0])]
    stage_of = lambda k, l: "rg" if k.startswith("rg_") else ("ffn0" if (k.startswith("ffn_") and l == 0) else "l1")
    staged = {st: [it for it in items if stage_of(*it) == st] for st in ("rg", "ffn0", "l1")}
    staged["l1"].sort(key=lambda it: not it[0].startswith("gla_"))
    shard = lambda k, l: wts[k][l].reshape(-1, wts[k].shape[-1]).astype(BF16)
    own = lambda got, mine: [lax.dynamic_update_index_in_dim(g, m, chip, 0) for g, m in zip(got, mine)]
    rows_joined = lambda v: v.reshape(-1, v.shape[-1])

    def placed(its, slots):
        out = {"ffn_w_up": {}, "ffn_w_down": {}}
        for (k, l), v in zip(its, slots):
            if k == "ffn_w_up":
                out[k][l] = v
            elif k == "ffn_w_down":
                out[k][l] = rows_joined(v)
            elif k in ("rg_wa", "rg_wx"):
                out[k] = _slots_to_block_rows(v, RG_BLOCKS)
            elif k == "gla_w_in":
                out[k] = _gla_head_cols(_from_col_slots(v))
            else:
                out[k] = v if BIG[k] else rows_joined(v)
        return out

    after_mod = (mod[0, 0, 0] * 0.0).astype(BF16)
    sh_rg = [shard(k, l) + after_mod for k, l in staged["rg"]]
    local = {k: (v if k in ("norm_g", "ffn_conv_w", "ffn_conv_b") else v[0]) for k, v in full.items()}
    local["gla_wal"], local["gla_bal"] = _gla_alpha_heads(local["gla_w_alpha"], local["gla_b_alpha"])
    local.update(placed(staged["rg"], own(_gather_chips(sh_rg, "gather_weights_rg"), sh_rg)))
    sh_late, flying = {}, {}
    after_rg = (local["rg_w_out"][0, 0].astype(F32) * 0.0).astype(BF16)
    sh_late["ffn0"] = [shard(k, l) + after_rg for k, l in staged["ffn0"]]
    flying["ffn0"], tok = _send_start(sh_late["ffn0"], "whole", "weights_ffn0_start")
    sh_late["l1"] = [shard(k, l) + tok[0, 0].astype(BF16) for k, l in staged["l1"]]
    flying["l1"], tok2 = _send_start(sh_late["l1"], "whole", "weights_l1_start")
    mod = mod + (tok[0, 0] + tok2[0, 0])

    def fetch(stage, after):
        got = _send_wait(flying[stage], after, "whole", f"weights_{stage}_wait")
        return placed(staged[stage], own(got, sh_late[stage]))

    c_idx = ci.reshape(1).astype(jnp.int32)
    gslots, paired, psums, sent = {}, {}, {}, {}

    def grad_slots(gr, k, l):
        g = gr[k][l] if k in ("ffn_w_up", "ffn_w_down") else gr[k]
        if k in ("rg_wa", "rg_wx"):
            return _block_rows_to_slots(g)
        if k == "gla_w_in":
            return _col_slots(g)
        return g if BIG[k] else g.reshape(N_CHIP, -1, g.shape[-1])

    def done(stage, gr):
        gslots[stage] = [grad_slots(gr, k, l) for k, l in staged[stage]]
        paired[stage], token = _send_start(gslots[stage], "pair", f"grads_{stage}_pair_start")
        return token[0, 0]

    def later(stage, after):
        theirs = _send_wait(paired[stage], after, "pair", f"grads_{stage}_pair_wait")
        psums[stage] = [_pair_sum(g, t, c_idx, f"grads_pair_sum_{k}{l}") for (k, l), g, t in zip(staged[stage], gslots[stage], theirs)]
        sent[stage], token = _send_start(psums[stage], "slots", f"grads_{stage}_start")
        return token[0, 0]

    cols, grad_x, gr = _local_step(x[0], loss_target[0], mod, local, fetch, done, later)
    loss = lax.psum(0.5 * jnp.sum(cols) / d, ("x", "y", "c"))

    small_names = [k for k in SMALL if k != "ada_b"]
    gs = _all_gather_8(_pack([gr[k] for k in small_names] + [gr["mod"]], SUBLANES, F32), "gather_small_grads")
    small_shapes = [full[k].shape for k in small_names] + [(depth, 6 * d)]
    *small_sum, g_ada_b = _unpack(_sum_lead(gs, "sum_small_grads"), small_shapes)
    grads = dict(zip(small_names, small_sum))
    grads["ada_b"] = g_ada_b
    for k in sharded_small:
        grads[k] = _my_shard(grads[k], SMALL[k], chip)
    dmod_all = _unpack(gs, small_shapes)[-1].reshape(N_DEV, depth, N_CHIP, n_ada)
    dmod_mine = jnp.swapaxes(lax.dynamic_index_in_dim(dmod_all, chip, 2, keepdims=False), 0, 1)
    g_ada_w = _ada_bwd(c16, jnp.pad(dmod_mine, ((0, 0), (0, ADA_ROWS - N_DEV), (0, 0))), "ada_bwd")

    gslots["rg"] = [grad_slots(gr, k, l) for k, l in staged["rg"]]
    theirs = _pair_exchange(gslots["rg"], "grads_rg_pair_exchange")
    psums["rg"] = [_pair_sum(g, t, c_idx, f"grads_pair_sum_{k}{l}") for (k, l), g, t in zip(staged["rg"], gslots["rg"], theirs)]
    sent["rg"], rg_sent = _send_start(psums["rg"], "slots", "grads_rg_start")
    mine_of = lambda p: lax.dynamic_index_in_dim(p, chip, 0, keepdims=False)
    delta, new_m, new_v = {}, {}, {}

    def reduce_and_update(stages, after, dep):
        its = [(st, n) for st in stages for n in range(len(staged[st]))]
        arrived = {st: _send_wait(sent[st], after, "slots", f"grads_{st}_wait") for st in stages}
        halves = [_sum_lead(lax.dynamic_update_index_in_dim(arrived[st][n], mine_of(psums[st][n]), chip, 0),
                            "grads_chip_sum_%s%d" % staged[st][n]) for st, n in its]
        shared = _pair_share(halves, "grads_pair_share_" + stages[0])
        reduced = [lax.dynamic_update_index_in_dim(s2, h, ci, 0).reshape(-1, h.shape[-1]) for s2, h in zip(shared, halves)]
        last = None
        for k in BIG:
            gs_k = [g for (st, n), g in zip(its, reduced) if staged[st][n][0] == k]
            if gs_k:
                last = update(k, gs_k, dep)
        return last

    def update(k, gs_k, dep=None):
        shp = wts[k].shape
        if k == "gla_w_in":
            view, back = (lambda a: jnp.swapaxes(a, 1, 2)), (lambda o: jnp.swapaxes(o, 1, 2))
            gs_k = [g.T for g in gs_k]
        else:
            view, back = (lambda a: a.reshape(a.shape[0], -1, a.shape[-1])), (lambda o: o.reshape(shp))
        outs = _adamw(view(wts[k]), gs_k, view(mom1[k]), view(mom2[k]), "adamw_" + k, dep)
        grads[k], delta[k], new_m[k], new_v[k] = (back(o) for o in outs)
        return new_v[k]

    done_late = reduce_and_update(("ffn0", "l1"), grad_x, rg_sent)
    update("ada_w", g_ada_w, rg_sent)
    small_shard_shapes = [wts[k].shape for k in SMALL]
    packed = [_pack([src[k] for k in SMALL], SUBLANES, F32) for src in (wts, grads, mom1, mom2)]
    outs = _adamw(packed[0][None], [packed[1]], packed[2][None], packed[3][None], "adamw_small", rg_sent)
    for dst, o in zip((delta, new_m, new_v), outs[1:]):
        for k, a in zip(SMALL, _unpack(o[0], small_shard_shapes)):
            dst[k] = a
    reduce_and_update(("rg",), done_late, None)

    return (loss, grad_x[None], *[grads[k] for k in WEIGHTS], *[delta[k] for k in WEIGHTS], *[new_m[k] for k in WEIGHTS],
            *[new_v[k] for k in WEIGHTS])
```

```python
import jax
import jax.numpy as jnp
from jax import lax
from jax.experimental import pallas as pl
from jax.experimental.pallas import tpu as pltpu

F32 = jnp.float32
BF16 = jnp.bfloat16
MXU_DTYPE = BF16

EPS = 1e-6
RG_C = 8.0
RG_BLOCKS = 4
RG_CONV = 4
GLA_HEADS = 4
GLA_TAU = 16.0
GLA_CHUNK = 64
GLA_RANK = 16
FFN_CONV = 3
ADAM_LR = 0.001
ADAM_B1 = 0.9
ADAM_B2 = 0.999
ADAM_EPS = 1e-08
ADAM_WD = 0.01
ADAM_STEP = 10

LANES = 128
SUBLANES = 8
VMEM_LIMIT = 56 * 1024 * 1024
CB = 256
MESH = pl.DeviceIdType.MESH
N_DEV = 8
N_CHIP = 4


def _params(*sem):
    return pltpu.CompilerParams(dimension_semantics=sem, vmem_limit_bytes=VMEM_LIMIT)


def _tile(dim, prefs):
    for p in prefs:
        if dim % p == 0:
            return p
    return dim


def _dot(a, b, dims):
    return lax.dot_general(a.astype(MXU_DTYPE), b.astype(MXU_DTYPE), (dims, ((), ())), preferred_element_type=F32)


def _dot_nn(a, b):
    return _dot(a, b, ((1,), (0,)))


def _dot_nt(a, b):
    return _dot(a, b, ((1,), (1,)))


def _dot_tn(a, b):
    return _dot(a, b, ((0,), (0,)))


def _mm(a, b, *, ta=False, tb=False, a_parts=1, b_parts=1, w_slots=1, out_slots=1, out_dtype=F32, name):
    if ta:
        k_dim, m_dim = a.shape
        n_dim = b.shape[-1] * b_parts
    else:
        m_dim, k_dim = a.shape[-2], a.shape[-1] * a_parts
        n_dim = b.shape[-2] if tb else b.shape[-1] * w_slots
    n_unit = n_dim // max(b_parts, out_slots, 1 if tb else w_slots)
    k_unit = k_dim // max(a_parts, w_slots if tb else 1)
    tm = _tile(m_dim, (1024, 1408, 512, 256, 128))
    tn = _tile(n_unit, (1024, 1408, 896, 512, 256, 128))
    tk = _tile(k_unit, (1024, 1408, 896, 512, 256, 128))
    nk = k_dim // tk
    dims = ((0 if ta else 1,), (1 if tb else 0,))

    def spec(shape, parts, total, tile, col_grid, row_grid):
        per = total // parts // tile

        def index(i, j, k):
            g = {"i": i, "j": j, "k": k}
            col, row = g[col_grid], g[row_grid]
            return (row, col) if parts == 1 else (col // per, row, col % per)

        return pl.BlockSpec(shape if parts == 1 else (None,) + shape, index)

    def body(a_ref, b_ref, o_ref, acc_ref):
        k = pl.program_id(2)

        @pl.when(k == 0)
        def _():
            acc_ref[...] = jnp.zeros_like(acc_ref)

        acc_ref[...] += _dot(a_ref[...], b_ref[...], dims)

        @pl.when(k == nk - 1)
        def _():
            o_ref[...] = acc_ref[...].astype(o_ref.dtype)

    if ta:
        a_spec = spec((tk, tm), 1, m_dim, tm, "i", "k")
        b_spec = spec((tk, tn), b_parts, n_dim, tn, "j", "k")
    elif tb:
        a_spec = spec((tm, tk), a_parts, k_dim, tk, "k", "i")
        b_spec = spec((tn, tk), w_slots, k_dim, tk, "k", "j")
    else:
        a_spec = spec((tm, tk), a_parts, k_dim, tk, "k", "i")
        b_spec = spec((tk, tn), w_slots, n_dim, tn, "j", "k")
    out_shape = (m_dim, n_dim) if out_slots == 1 else (out_slots, m_dim, n_dim // out_slots)
    return pl.pallas_call(
        body,
        grid=(m_dim // tm, n_dim // tn, nk),
        in_specs=[a_spec, b_spec],
        out_specs=spec((tm, tn), out_slots, n_dim, tn, "j", "i"),
        out_shape=jax.ShapeDtypeStruct(out_shape, out_dtype),
        scratch_shapes=[pltpu.VMEM((tm, tn), F32)],
        compiler_params=_params("parallel", "parallel", "arbitrary"),
        name=name,
    )(a, b)


def _row_specs(s, d, ts):
    return pl.BlockSpec((ts, d), lambda i: (i, 0)), pl.BlockSpec((1, d), lambda i: (0, 0))


def _norm_mod_fwd(x, g, sc, sh, name):
    s, d = x.shape
    ts = _tile(s, (512,))

    def body(x_ref, g_ref, sc_ref, sh_ref, h_ref):
        xv = x_ref[...]
        r = lax.rsqrt(jnp.mean(xv * xv, axis=-1, keepdims=True) + EPS)
        h_ref[...] = (((xv * r) * g_ref[...]) * (1.0 + sc_ref[...]) + sh_ref[...]).astype(h_ref.dtype)

    row, vec = _row_specs(s, d, ts)
    return pl.pallas_call(
        body, grid=(s // ts,), in_specs=[row, vec, vec, vec], out_specs=row,
        out_shape=jax.ShapeDtypeStruct((s, d), MXU_DTYPE), compiler_params=_params("parallel"), name=name,
    )(x, g, sc, sh)


def _norm_mod_bwd(dh, x, g, sc, dres, name):
    s, d = x.shape
    ts = _tile(s, (512,))

    def body(dh_ref, x_ref, g_ref, sc_ref, dres_ref, dx_ref, dg_ref, dsc_ref, dsh_ref, acc_ref):
        i = pl.program_id(0)

        @pl.when(i == 0)
        def _():
            acc_ref[...] = jnp.zeros_like(acc_ref)

        xv, dhv = x_ref[...], dh_ref[...]
        r = lax.rsqrt(jnp.mean(xv * xv, axis=-1, keepdims=True) + EPS)
        n = xv * r
        acc_ref[0:1, :] += jnp.sum(dhv * n, axis=0, keepdims=True)
        acc_ref[1:2, :] += jnp.sum(dhv, axis=0, keepdims=True)
        dn = dhv * ((1.0 + sc_ref[...]) * g_ref[...])
        dx_ref[...] = dres_ref[...] + r * (dn - n * jnp.mean(dn * n, axis=-1, keepdims=True))
        dg_ref[...] = (1.0 + sc_ref[...]) * acc_ref[0:1, :]
        dsc_ref[...] = g_ref[...] * acc_ref[0:1, :]
        dsh_ref[...] = acc_ref[1:2, :]

    row, vec = _row_specs(s, d, ts)
    vshape = jax.ShapeDtypeStruct((1, d), F32)
    return pl.pallas_call(
        body, grid=(s // ts,), in_specs=[row, row, vec, vec, row], out_specs=[row, vec, vec, vec],
        out_shape=[jax.ShapeDtypeStruct((s, d), F32), vshape, vshape, vshape],
        scratch_shapes=[pltpu.VMEM((SUBLANES, d), F32)], compiler_params=_params("arbitrary"), name=name,
    )(dh, x, g, sc, dres)


def _post_fwd(x, y, g, gt, name):
    s, d = x.shape
    ts = _tile(s, (512,))

    def body(x_ref, y_ref, g_ref, gt_ref, o_ref):
        yv = y_ref[...]
        r = lax.rsqrt(jnp.mean(yv * yv, axis=-1, keepdims=True) + EPS)
        o_ref[...] = x_ref[...] + gt_ref[...] * ((yv * r) * g_ref[...])

    row, vec = _row_specs(s, d, ts)
    return pl.pallas_call(
        body, grid=(s // ts,), in_specs=[row, row, vec, vec], out_specs=row,
        out_shape=jax.ShapeDtypeStruct((s, d), F32), compiler_params=_params("parallel"), name=name,
    )(x, y, g, gt)


def _post_bwd(dxn, y, g, gt, name):
    s, d = y.shape
    ts = _tile(s, (512,))

    def body(dxn_ref, y_ref, g_ref, gt_ref, dy_ref, dg_ref, dgt_ref, acc_ref):
        i = pl.program_id(0)

        @pl.when(i == 0)
        def _():
            acc_ref[...] = jnp.zeros_like(acc_ref)

        yv, dv = y_ref[...], dxn_ref[...]
        r = lax.rsqrt(jnp.mean(yv * yv, axis=-1, keepdims=True) + EPS)
        n = yv * r
        acc_ref[0:1, :] += jnp.sum(dv * n, axis=0, keepdims=True)
        dn = dv * (gt_ref[...] * g_ref[...])
        dy_ref[...] = (r * (dn - n * jnp.mean(dn * n, axis=-1, keepdims=True))).astype(dy_ref.dtype)
        dg_ref[...] = gt_ref[...] * acc_ref[0:1, :]
        dgt_ref[...] = g_ref[...] * acc_ref[0:1, :]

    row, vec = _row_specs(s, d, ts)
    vshape = jax.ShapeDtypeStruct((1, d), F32)
    return pl.pallas_call(
        body, grid=(s // ts,), in_specs=[row, row, vec, vec], out_specs=[row, vec, vec],
        out_shape=[jax.ShapeDtypeStruct((s, d), MXU_DTYPE), vshape, vshape],
        scratch_shapes=[pltpu.VMEM((SUBLANES, d), F32)], compiler_params=_params("arbitrary"), name=name,
    )(dxn, y, g, gt)


def _loss_grad(x, tgt, name):
    s, d = x.shape
    ts = _tile(s, (512,))

    def body(x_ref, t_ref, col_ref, dx_ref):
        i = pl.program_id(0)

        @pl.when(i == 0)
        def _():
            col_ref[...] = jnp.zeros_like(col_ref)

        e = x_ref[...] - t_ref[...]
        col_ref[...] += jnp.sum(e * e, axis=0, keepdims=True)
        dx_ref[...] = e * (1.0 / d)

    row, vec = _row_specs(s, d, ts)
    return pl.pallas_call(
        body, grid=(s // ts,), in_specs=[row, row], out_specs=[vec, row],
        out_shape=[jax.ShapeDtypeStruct((1, d), F32), jax.ShapeDtypeStruct((s, d), F32)],
        compiler_params=_params("arbitrary"), name=name,
    )(x, tgt)


_GELU_C = 0.7978845608028654
_GELU_A = 0.044715


def _gelu(x):
    t = jnp.tanh(_GELU_C * (x + _GELU_A * x * x * x))
    return 0.5 * x * (1.0 + t), t


def _gelu_grad(x, t):
    return 0.5 * (1.0 + t) + 0.5 * x * (1.0 - t * t) * (_GELU_C * (1.0 + 3.0 * _GELU_A * x * x))


def _sigmoid(x):
    return 1.0 / (1.0 + jnp.exp(-x))


def _log1p_pos(y):
    u = 1.0 + y
    return jnp.where(u == 1.0, y, jnp.log(u) * (y / jnp.where(u == 1.0, 1.0, u - 1.0)))


def _softplus(x):
    return jnp.maximum(x, 0.0) + _log1p_pos(jnp.exp(-jnp.abs(x)))


def _one_minus_exp(z):
    u = jnp.exp(z)
    lg = jnp.log(jnp.where(u > 0.0, u, 1.0))
    safe = (u != 1.0) & (u > 0.0)
    return jnp.where(u == 1.0, -z, jnp.where(u > 0.0, (1.0 - u) * (z / jnp.where(safe, lg, 1.0)), 1.0))


SLAB = 16


def _cat(a, b):
    return jnp.concatenate([a, b], axis=1)


def _fold8(x):
    out = x[0:SUBLANES]
    for r in range(SUBLANES, x.shape[0], SUBLANES):
        out = out + x[r:r + SUBLANES]
    return out


def _pair_specs(shape, nb, index):
    return [pl.BlockSpec(shape, lambda j, t: index(j, t) + (j,)), pl.BlockSpec(shape, lambda j, t: index(j, t) + (j + nb,))]


def _halo_row(ts, time_of):
    return lambda j, t: (jnp.maximum(time_of(t) * (ts // SUBLANES) - 1, 0),)


def _ffn_mid_fwd(p, cw, cb, name):
    s, f2 = p.shape
    ts = _tile(s, (512,))
    nb, nt = f2 // (2 * CB), s // ts

    def body(pg_ref, pv_ref, hg_ref, hv_ref, cwg_ref, cwv_ref, cbg_ref, cbv_ref, a_ref, ga_ref, gb_ref):
        t = pl.program_id(1)
        cwv, bias = _cat(cwg_ref[...], cwv_ref[...]), _cat(cbg_ref[...], cbv_ref[...])
        w0, w1, w2 = cwv[0:1], cwv[1:2], cwv[2:3]

        def slab(blk, r0):
            u = bias + w0 * blk[6:6 + SLAB] + w1 * blk[7:7 + SLAB] + w2 * blk[8:8 + SLAB]
            g, v = u[:, :CB], u[:, CB:]
            gel, th = _gelu(g)
            rows = pl.ds(r0, SLAB)
            a_ref[rows, :] = (gel * v).astype(a_ref.dtype)
            ga_ref[rows, :] = gel.astype(ga_ref.dtype)
            gb_ref[rows, :] = (v * _gelu_grad(g, th)).astype(gb_ref.dtype)

        halo = jnp.where(t > 0, _cat(hg_ref[...], hv_ref[...]), 0.0)
        slab(jnp.concatenate([halo, _cat(pg_ref[0:SLAB, :], pv_ref[0:SLAB, :])], axis=0), 0)

        def loop(i, carry):
            r0 = pl.multiple_of(i * SLAB, SLAB)
            rows = pl.ds(pl.multiple_of(r0 - SUBLANES, SUBLANES), SLAB + SUBLANES)
            slab(_cat(pg_ref[rows, :], pv_ref[rows, :]), r0)
            return carry

        lax.fori_loop(1, ts // SLAB, loop, 0, unroll=2)

    fwd = lambda t: t
    out = pl.BlockSpec((ts, CB), lambda j, t: (t, j))
    shape = jax.ShapeDtypeStruct((s, f2 // 2), MXU_DTYPE)
    return pl.pallas_call(
        body, grid=(nb, nt),
        in_specs=(_pair_specs((ts, CB), nb, lambda j, t: (t,)) + _pair_specs((SUBLANES, CB), nb, _halo_row(ts, fwd))
                  + _pair_specs((FFN_CONV, CB), nb, lambda j, t: (0,)) + _pair_specs((1, CB), nb, lambda j, t: (0,))),
        out_specs=[out, out, out], out_shape=[shape, shape, shape],
        compiler_params=_params("parallel", "arbitrary"), name=name,
    )(p, p, p, p, cw, cw, cb, cb)


def _ffn_mid_bwd(da, p, ga, gb, cw, name):
    s, f2 = p.shape
    ts = _tile(s, (512,))
    nb, nt = f2 // (2 * CB), s // ts
    n_slab = ts // SLAB

    def body(da_ref, ga_ref, gb_ref, pg_ref, pv_ref, cwg_ref, cwv_ref, dp_ref, dcw_ref, dcb_ref, next_du, acc):
        tt = pl.program_id(1)
        cwv = _cat(cwg_ref[...], cwv_ref[...])
        w0, w1, w2 = cwv[0:1], cwv[1:2], cwv[2:3]

        @pl.when(tt == 0)
        def _():
            next_du[...] = jnp.zeros_like(next_du)
            acc[...] = jnp.zeros_like(acc)

        def loop(k, carry):
            r0 = pl.multiple_of((n_slab - 1 - k) * SLAB, SLAB)
            rows = pl.ds(r0, SLAB)
            dav = da_ref[rows, :]
            du = _cat(dav * gb_ref[rows, :].astype(F32), dav * ga_ref[rows, :].astype(F32))
            p0 = _cat(pg_ref[rows, :], pv_ref[rows, :])
            ext = jnp.concatenate([du, carry], axis=0)
            du1, du2 = ext[1:1 + SLAB], ext[2:2 + SLAB]
            dpv = (w2 * du + w1 * du1 + w0 * du2).astype(dp_ref.dtype)
            dp_ref[0, rows, :] = dpv[:, :CB]
            dp_ref[1, rows, :] = dpv[:, CB:]
            acc[0] += _fold8(du)
            acc[1] += _fold8(du2 * p0)
            acc[2] += _fold8(du1 * p0)
            acc[3] += _fold8(du * p0)
            return du[0:SUBLANES]

        next_du[...] = lax.fori_loop(0, n_slab, loop, next_du[...], unroll=2)

        @pl.when(tt == nt - 1)
        def _():
            for half in range(2):
                cols = slice(half * CB, (half + 1) * CB)
                dcb_ref[half] = jnp.sum(acc[0][:, cols], axis=0, keepdims=True)
                for k in range(FFN_CONV):
                    dcw_ref[half, k:k + 1, :] = jnp.sum(acc[1 + k][:, cols], axis=0, keepdims=True)

    rev = lambda t: nt - 1 - t
    tile = pl.BlockSpec((ts, CB), lambda j, t: (rev(t), j))
    return pl.pallas_call(
        body, grid=(nb, nt),
        in_specs=([tile, tile, tile] + _pair_specs((ts, CB), nb, lambda j, t: (rev(t),))
                  + _pair_specs((FFN_CONV, CB), nb, lambda j, t: (0,))),
        out_specs=[pl.BlockSpec((2, ts, CB), lambda j, t: (0, rev(t), j)),
                   pl.BlockSpec((2, FFN_CONV, CB), lambda j, t: (0, 0, j)),
                   pl.BlockSpec((2, 1, CB), lambda j, t: (0, 0, j))],
        out_shape=[jax.ShapeDtypeStruct((2, s, f2 // 2), MXU_DTYPE), jax.ShapeDtypeStruct((2, FFN_CONV, f2 // 2), F32),
                   jax.ShapeDtypeStruct((2, 1, f2 // 2), F32)],
        scratch_shapes=[pltpu.VMEM((SUBLANES, 2 * CB), F32), pltpu.VMEM((1 + FFN_CONV, SUBLANES, 2 * CB), F32)],
        compiler_params=_params("parallel", "arbitrary"), name=name,
    )(da, ga, gb, p, p, cw, cw)


def _rg_gates(xc, wa_ref, ba_ref, wx_ref, bx_ref, lam_ref):
    r = _sigmoid(_dot_nn(xc, wa_ref[0]) + ba_ref[...])
    ig = _sigmoid(_dot_nn(xc, wx_ref[0]) + bx_ref[...])
    sp = _softplus(-lam_ref[...])
    log_a = (-RG_C) * r * sp
    a = jnp.exp(log_a)
    mult = jnp.sqrt(_one_minus_exp(2.0 * log_a))
    return r, ig, sp, a, mult


def _rg_conv(scr, cw_ref, cb_ref, ts):
    views = [scr[5 + k:5 + k + ts, :] for k in range(RG_CONV)]
    xc = cb_ref[...]
    for k in range(RG_CONV):
        xc = xc + cw_ref[k:k + 1, :] * views[k]
    return xc, views


def _rg_param_specs():
    vec = pl.BlockSpec((1, CB), lambda g, t: (0, g))
    mat = pl.BlockSpec((1, CB, CB), lambda g, t: (g, 0, 0))
    return [pl.BlockSpec((RG_CONV, CB), lambda g, t: (0, g)), vec, mat, vec, mat, vec, vec]


NSEG = SUBLANES
NQ = CB // LANES


def _lanes(q):
    return slice(q * LANES, (q + 1) * LANES)


def _seg_scan(a_scr, x_scr, loc_scr, dec_scr, ts, reverse):
    seg = ts // NSEG

    def step(k, carry):
        out = []
        rows = pl.ds(seg - 1 - k if reverse else k, NSEG, stride=seg)
        for q in range(NQ):
            st, dec = carry[q]
            a_q, x_q, loc_q, dec_q = a_scr.at[q], x_scr.at[q], loc_scr.at[q], dec_scr.at[q]
            av = a_q[rows, :]
            if reverse:
                loc_q[rows, :] = st
                dec_q[rows, :] = dec
                st = av * (x_q[rows, :] + st)
                dec = av * dec
            else:
                st = av * st + x_q[rows, :]
                dec = av * dec
                loc_q[rows, :] = st
                dec_q[rows, :] = dec
            out.append((st, dec))
        return tuple(out)

    init = tuple((jnp.zeros((NSEG, LANES), F32), jnp.ones((NSEG, LANES), F32)) for _ in range(NQ))
    return lax.fori_loop(0, seg, step, init, unroll=4)


def _seg_chain(fin, dec, c_in, reverse):
    rows = [None] * NSEG
    c = c_in
    for sgm in (reversed(range(NSEG)) if reverse else range(NSEG)):
        rows[sgm] = c
        c = fin[sgm:sgm + 1] + dec[sgm:sgm + 1] * c
    return jnp.concatenate(rows, axis=0), c


def _rg_mid_fwd(pj, cw, cb, wa, ba, wx, bx, lam, name):
    s = pj.shape[0]
    nb = pj.shape[1] // (2 * CB)
    ts = _tile(s, (512,))
    nt = s // ts
    seg = ts // NSEG

    def body(gate_ref, x_ref, halo_ref, cw_ref, cb_ref, wa_ref, ba_ref, wx_ref, bx_ref, lam_ref, y_ref, hs_ref,
             scr, a_scr, u_scr, loc_scr, dec_scr, h_scr):
        t = pl.program_id(1)

        @pl.when(t == 0)
        def _():
            h_scr[...] = jnp.zeros_like(h_scr)

        scr[0:SUBLANES, :] = jnp.where(t > 0, halo_ref[...], 0.0)
        scr[SUBLANES:, :] = x_ref[...]
        xc, _ = _rg_conv(scr, cw_ref, cb_ref, ts)
        _, ig, _, a, mult = _rg_gates(xc, wa_ref, ba_ref, wx_ref, bx_ref, lam_ref)
        u = mult * (ig * xc)
        for q in range(NQ):
            a_scr[q] = a[:, _lanes(q)]
            u_scr[q] = u[:, _lanes(q)]
        fin = _seg_scan(a_scr, u_scr, loc_scr, dec_scr, ts, False)
        for q in range(NQ):
            enter, leave = _seg_chain(fin[q][0], fin[q][1], h_scr[0:1, _lanes(q)], False)
            h_scr[0:1, _lanes(q)] = leave
            for sgm in range(NSEG):
                rows = slice(sgm * seg, (sgm + 1) * seg)
                hs_ref[rows, _lanes(q)] = loc_scr[q, rows, :] + dec_scr[q, rows, :] * enter[sgm:sgm + 1]
        y_ref[...] = (_gelu(gate_ref[...])[0] * hs_ref[...]).astype(y_ref.dtype)

    blk = pl.BlockSpec((ts, CB), lambda g, t: (t, g))
    lane_scr = pltpu.VMEM((NQ, ts, LANES), F32)
    return pl.pallas_call(
        body, grid=(nb, nt),
        in_specs=_pair_specs((ts, CB), nb, lambda g, t: (t,))
        + [pl.BlockSpec((SUBLANES, CB), lambda g, t: _halo_row(ts, lambda u: u)(g, t) + (g + nb,))] + _rg_param_specs(),
        out_specs=[blk, blk],
        out_shape=[jax.ShapeDtypeStruct((s, nb * CB), MXU_DTYPE), jax.ShapeDtypeStruct((s, nb * CB), F32)],
        scratch_shapes=[pltpu.VMEM((ts + SUBLANES, CB), F32), lane_scr, lane_scr, lane_scr, lane_scr,
                        pltpu.VMEM((SUBLANES, CB), F32)],
        compiler_params=_params("parallel", "arbitrary"), name=name,
    )(pj, pj, pj, cw, cb, wa, ba, wx, bx, lam)


def _rg_mid_bwd(dy, pj, hs, cw, cb, wa, ba, wx, bx, lam, name):
    s = pj.shape[0]
    nb = pj.shape[1] // (2 * CB)
    ts = _tile(s, (512,))
    nt = s // ts

    def body(dy_ref, gate_ref, x_ref, halo_ref, hs_ref, hsh_ref, cw_ref, cb_ref, wa_ref, ba_ref, wx_ref, bx_ref, lam_ref,
             dpj_ref, dcw_ref, dcb_ref, dwa_ref, dba_ref, dwx_ref, dbx_ref, dlam_ref,
             scr, hscr, a_scr, d_scr, loc_scr, dec_scr, g_scr, dxscr, c_scr):
        tt = pl.program_id(1)
        t = nt - 1 - tt
        seg = ts // NSEG

        @pl.when(tt == 0)
        def _():
            c_scr[...] = jnp.zeros_like(c_scr)
            dxscr[ts:, :] = jnp.zeros((SUBLANES, CB), F32)
            for ref in (dcw_ref, dcb_ref, dwa_ref, dba_ref, dwx_ref, dbx_ref, dlam_ref):
                ref[...] = jnp.zeros_like(ref)

        scr[0:SUBLANES, :] = jnp.where(t > 0, halo_ref[...], 0.0)
        scr[SUBLANES:, :] = x_ref[...]
        hscr[0:SUBLANES, :] = jnp.where(t > 0, hsh_ref[...], 0.0)
        hscr[SUBLANES:, :] = hs_ref[...]
        xc, views = _rg_conv(scr, cw_ref, cb_ref, ts)
        r, ig, sp, a, mult = _rg_gates(xc, wa_ref, ba_ref, wx_ref, bx_ref, lam_ref)
        gate = gate_ref[...]
        gel, th = _gelu(gate)
        dyv = dy_ref[...]
        dpj_ref[0] = (dyv * hs_ref[...] * _gelu_grad(gate, th)).astype(dpj_ref.dtype)
        dhs = dyv * gel
        for q in range(NQ):
            a_scr[q] = a[:, _lanes(q)]
            d_scr[q] = dhs[:, _lanes(q)]
        fin = _seg_scan(a_scr, d_scr, loc_scr, dec_scr, ts, True)
        for q in range(NQ):
            enter, leave = _seg_chain(fin[q][0], fin[q][1], c_scr[0:1, _lanes(q)], True)
            c_scr[0:1, _lanes(q)] = leave
            for sgm in range(NSEG):
                rows = slice(sgm * seg, (sgm + 1) * seg)
                g_scr[rows, _lanes(q)] = d_scr[q, rows, :] + loc_scr[q, rows, :] + dec_scr[q, rows, :] * enter[sgm:sgm + 1]
        du = g_scr[...]
        da = du * hscr[7:7 + ts, :]
        dmult = du * (ig * xc)
        dig = du * (mult * xc)
        dxc = du * (mult * ig)
        dlog_a = da * a - dmult * (a * a / mult)
        dlam_ref[...] += jnp.sum(dlog_a * r, axis=0, keepdims=True) * (RG_C * _sigmoid(-lam_ref[...]))
        dpr = dlog_a * ((-RG_C) * sp) * (r * (1.0 - r))
        dpi = dig * (ig * (1.0 - ig))
        dba_ref[...] += jnp.sum(dpr, axis=0, keepdims=True)
        dbx_ref[...] += jnp.sum(dpi, axis=0, keepdims=True)
        dwa_ref[0] += _dot_tn(xc, dpr)
        dwx_ref[0] += _dot_tn(xc, dpi)
        dxc = dxc + _dot_nt(dpr, wa_ref[0]) + _dot_nt(dpi, wx_ref[0])
        dcb_ref[...] += jnp.sum(dxc, axis=0, keepdims=True)
        for k in range(RG_CONV):
            dcw_ref[k:k + 1, :] += jnp.sum(dxc * views[k], axis=0, keepdims=True)
        dxscr[0:ts, :] = dxc
        dxp = cw_ref[3:4, :] * dxc
        for k in range(RG_CONV - 1):
            dxp = dxp + cw_ref[k:k + 1, :] * dxscr[3 - k:3 - k + ts, :]
        dpj_ref[1] = dxp.astype(dpj_ref.dtype)
        dxscr[ts:, :] = dxscr[0:SUBLANES, :]

    rev = lambda g, t: (nt - 1 - t, g)
    rev_halo = lambda g, t: (jnp.maximum((nt - 1 - t) * (ts // SUBLANES) - 1, 0), g)
    vec = pl.BlockSpec((1, CB), lambda g, t: (0, g))
    mat = pl.BlockSpec((1, CB, CB), lambda g, t: (g, 0, 0))
    d = nb * CB
    vshape = jax.ShapeDtypeStruct((1, d), F32)
    mshape = jax.ShapeDtypeStruct((nb, CB, CB), F32)
    return pl.pallas_call(
        body, grid=(nb, nt),
        in_specs=[pl.BlockSpec((ts, CB), rev)] + _pair_specs((ts, CB), nb, lambda g, t: (nt - 1 - t,))
        + [pl.BlockSpec((SUBLANES, CB), lambda g, t: (rev_halo(g, t)[0], g + nb)),
           pl.BlockSpec((ts, CB), rev), pl.BlockSpec((SUBLANES, CB), rev_halo)] + _rg_param_specs(),
        out_specs=[pl.BlockSpec((2, ts, CB), lambda g, t: (0, nt - 1 - t, g)), pl.BlockSpec((RG_CONV, CB), lambda g, t: (0, g)),
                   vec, mat, vec, mat, vec, vec],
        out_shape=[jax.ShapeDtypeStruct((2, s, d), MXU_DTYPE), jax.ShapeDtypeStruct((RG_CONV, d), F32), vshape, mshape, vshape,
                   mshape, vshape, vshape],
        scratch_shapes=[pltpu.VMEM((ts + SUBLANES, CB), F32), pltpu.VMEM((ts + SUBLANES, CB), F32)]
        + [pltpu.VMEM((NQ, ts, LANES), F32)] * 4
        + [pltpu.VMEM((ts, CB), F32), pltpu.VMEM((ts + SUBLANES, CB), F32), pltpu.VMEM((SUBLANES, CB), F32)],
        compiler_params=_params("parallel", "arbitrary"), name=name,
    )(dy, pj, pj, pj, hs, hs, cw, cb, wa, ba, wx, bx, lam)


GLA_DK = 128
GLA_DV = 256
GLA_HB = 2 * GLA_DK + 2 * GLA_DV + LANES
GLA_TS = 256


def _split3(x):
    hi = x.astype(BF16)
    r1 = x - hi.astype(F32)
    mid = r1.astype(BF16)
    lo = (r1 - mid.astype(F32)).astype(BF16)
    return hi, mid, lo


def _chunk_cumsum(x, reverse):
    n = x.shape[0]
    i = lax.broadcasted_iota(jnp.int32, (n, n), 0)
    j = lax.broadcasted_iota(jnp.int32, (n, n), 1)
    same = (i // GLA_CHUNK) == (j // GLA_CHUNK)
    tri = jnp.where(same & ((j >= i) if reverse else (j <= i)), 1.0, 0.0).astype(BF16)
    out = jnp.zeros(x.shape, F32)
    for piece in _split3(x):
        out = out + lax.dot_general(tri, piece, (((1,), (0,)), ((), ())), preferred_element_type=F32)
    return out


def _gla_split(blk):
    q = blk[:, 0:GLA_DK] * (GLA_DK ** -0.5)
    k = blk[:, GLA_DK:2 * GLA_DK]
    v = blk[:, 2 * GLA_DK:2 * GLA_DK + GLA_DV]
    r = blk[:, 2 * GLA_DK + GLA_DV:2 * GLA_DK + 2 * GLA_DV]
    z = blk[:, 2 * GLA_DK + 2 * GLA_DV:]
    return q, k, v, r, z


def _gla_decays(gc):
    gref = gc[GLA_CHUNK // 2:GLA_CHUNK // 2 + 1, :]
    glast = gc[GLA_CHUNK - 1:GLA_CHUNK, :]
    return jnp.exp(gc), jnp.exp(gc - gref), jnp.exp(gref - gc), jnp.exp(glast - gc), jnp.exp(glast)


def _causal_mask():
    i = lax.broadcasted_iota(jnp.int32, (GLA_CHUNK, GLA_CHUNK), 0)
    j = lax.broadcasted_iota(jnp.int32, (GLA_CHUNK, GLA_CHUNK), 1)
    return j <= i


def _log_sigmoid(x):
    return jnp.minimum(x, 0.0) - _log1p_pos(jnp.exp(-jnp.abs(x)))


def _gla_mid_fwd(pj, wal, bal, ng, name):
    s = pj.shape[0]
    nh = pj.shape[1] // GLA_HB
    ts = _tile(s, (GLA_TS,))
    nt, nc = s // ts, ts // GLA_CHUNK

    def body(pj_ref, wal_ref, bal_ref, ng_ref, act_ref, o_ref, st_ref, s_scr):
        t = pl.program_id(0)

        @pl.when(t == 0)
        def _():
            s_scr[...] = jnp.zeros_like(s_scr)

        heads = []
        for h in range(nh):
            q, k, v, r, z = _gla_split(pj_ref[:, h * GLA_HB:(h + 1) * GLA_HB])
            g = _log_sigmoid(_dot_nn(z, wal_ref[h]) + bal_ref[h]) * (1.0 / GLA_TAU)
            heads.append((q, k, v, r, _chunk_cumsum(g, False)))
        mask = _causal_mask()
        for c in range(nc):
            sl = slice(c * GLA_CHUNK, (c + 1) * GLA_CHUNK)
            for h, (q, k, v, r, gcum) in enumerate(heads):
                eg, eq, ek, ekd, egl = _gla_decays(gcum[sl])
                st = s_scr[h]
                st_ref[c, h] = st
                attn = jnp.where(mask, _dot_nt(q[sl] * eq, k[sl] * ek), 0.0)
                o_ref[sl, h * GLA_DV:(h + 1) * GLA_DV] = _dot_nt(q[sl] * eg, st) + _dot_nn(attn, v[sl])
                s_scr[h] = st * egl + _dot_tn(v[sl], k[sl] * ekd)
        for h, (q, k, v, r, gcum) in enumerate(heads):
            cols = slice(h * GLA_DV, (h + 1) * GLA_DV)
            o = o_ref[:, cols]
            on = o * lax.rsqrt(jnp.mean(o * o, axis=-1, keepdims=True) + EPS)
            act_ref[:, cols] = ((on * ng_ref[...]) * (r * _sigmoid(r))).astype(act_ref.dtype)

    blk = pl.BlockSpec((ts, nh * GLA_DV), lambda t: (t, 0))
    whole = lambda shape: pl.BlockSpec(shape, lambda t: (0,) * len(shape))
    return pl.pallas_call(
        body, grid=(nt,),
        in_specs=[pl.BlockSpec((ts, nh * GLA_HB), lambda t: (t, 0)), whole((nh, LANES, GLA_DK)), whole((nh, 1, GLA_DK)),
                  whole((1, GLA_DV))],
        out_specs=[blk, blk, pl.BlockSpec((nc, nh, GLA_DV, GLA_DK), lambda t: (t, 0, 0, 0))],
        out_shape=[jax.ShapeDtypeStruct((s, nh * GLA_DV), MXU_DTYPE), jax.ShapeDtypeStruct((s, nh * GLA_DV), F32),
                   jax.ShapeDtypeStruct((s // GLA_CHUNK, nh, GLA_DV, GLA_DK), F32)],
        scratch_shapes=[pltpu.VMEM((nh, GLA_DV, GLA_DK), F32)],
        compiler_params=_params("arbitrary"), name=name,
    )(pj, wal, bal, ng)


def _gla_mid_bwd(dact, pj, o, st, wal, bal, ng, name):
    s = pj.shape[0]
    nh = pj.shape[1] // GLA_HB
    ts = _tile(s, (GLA_TS,))
    nt, nc = s // ts, ts // GLA_CHUNK
    o_q, o_k, o_v, o_r, o_z = 0, GLA_DK, 2 * GLA_DK, 2 * GLA_DK + GLA_DV, 2 * GLA_DK + 2 * GLA_DV

    def body(dact_ref, pj_ref, o_ref, st_ref, wal_ref, bal_ref, ng_ref, dpj_ref, dwal_ref, dbal_ref, dng_ref,
             ds_scr, dg_scr):
        tt = pl.program_id(0)

        @pl.when(tt == 0)
        def _():
            ds_scr[...] = jnp.zeros_like(ds_scr)
            dwal_ref[...] = jnp.zeros_like(dwal_ref)
            dbal_ref[...] = jnp.zeros_like(dbal_ref)
            dng_ref[...] = jnp.zeros_like(dng_ref)

        heads = []
        for h in range(nh):
            base = h * GLA_HB
            q, k, v, r, z = _gla_split(pj_ref[:, base:base + GLA_HB])
            logit = _dot_nn(z, wal_ref[h]) + bal_ref[h]
            gcum = _chunk_cumsum(_log_sigmoid(logit) * (1.0 / GLA_TAU), False)
            ov = o_ref[:, h * GLA_DV:(h + 1) * GLA_DV]
            ro = lax.rsqrt(jnp.mean(ov * ov, axis=-1, keepdims=True) + EPS)
            on = ov * ro
            sg = _sigmoid(r)
            sil = r * sg
            dav = dact_ref[:, h * GLA_DV:(h + 1) * GLA_DV]
            dpj_ref[:, base + o_r:base + o_z] = (dav * (on * ng_ref[...]) * (sg + sil * (1.0 - sg))).astype(dpj_ref.dtype)
            t1 = dav * sil
            dng_ref[...] += jnp.sum(t1 * on, axis=0, keepdims=True)
            dn = t1 * ng_ref[...]
            do = ro * (dn - on * jnp.mean(dn * on, axis=-1, keepdims=True))
            heads.append((q, k, v, z, logit, gcum, do))
        mask = _causal_mask()
        scale = GLA_DK ** -0.5
        last_row = lax.broadcasted_iota(jnp.int32, (GLA_CHUNK, GLA_DK), 0) == GLA_CHUNK - 1
        for c in reversed(range(nc)):
            sl = slice(c * GLA_CHUNK, (c + 1) * GLA_CHUNK)
            for h, (q, k, v, z, logit, gcum, do) in enumerate(heads):
                base = h * GLA_HB
                eg, eq, ek, ekd, egl = _gla_decays(gcum[sl])
                qc, kc, vc, doc = q[sl], k[sl], v[sl], do[sl]
                qg, qt, kt, kd = qc * eg, qc * eq, kc * ek, kc * ekd
                sp = st_ref[c, h]
                ds = ds_scr[h]
                attn = jnp.where(mask, _dot_nt(qt, kt), 0.0)
                dattn = jnp.where(mask, _dot_nt(doc, vc), 0.0)
                dqg = _dot_nn(doc, sp)
                dqt = _dot_nn(dattn, kt)
                dkt = _dot_tn(dattn, qt)
                dkd = _dot_nn(vc, ds)
                dpj_ref[sl, base + o_v:base + o_r] = (_dot_tn(attn, doc) + _dot_nt(kd, ds)).astype(dpj_ref.dtype)
                dpj_ref[sl, base + o_q:base + o_k] = (scale * (dqg * eg + dqt * eq)).astype(dpj_ref.dtype)
                dpj_ref[sl, base + o_k:base + o_v] = (dkt * ek + dkd * ekd).astype(dpj_ref.dtype)
                kdd = dkd * kd
                dgl = jnp.sum(kdd, axis=0, keepdims=True) + jnp.sum(ds * sp, axis=0, keepdims=True) * egl
                dg_scr[h, sl, :] = dqg * qg + dqt * qt - dkt * kt - kdd + jnp.where(last_row, dgl, 0.0)
                ds_scr[h] = ds * egl + _dot_tn(doc, qg)
        for h, (q, k, v, z, logit, gcum, do) in enumerate(heads):
            base = h * GLA_HB
            dlogit = _chunk_cumsum(dg_scr[h], True) * (1.0 / GLA_TAU) * _sigmoid(-logit)
            dpj_ref[:, base + o_z:base + GLA_HB] = _dot_nt(dlogit, wal_ref[h]).astype(dpj_ref.dtype)
            dwal_ref[h] += _dot_tn(z, dlogit)
            dbal_ref[h] += jnp.sum(dlogit, axis=0, keepdims=True)

    rev = lambda t: (nt - 1 - t, 0)
    whole = lambda shape: pl.BlockSpec(shape, lambda t: (0,) * len(shape))
    wide = pl.BlockSpec((ts, nh * GLA_DV), rev)
    return pl.pallas_call(
        body, grid=(nt,),
        in_specs=[wide, pl.BlockSpec((ts, nh * GLA_HB), rev), wide,
                  pl.BlockSpec((nc, nh, GLA_DV, GLA_DK), lambda t: (nt - 1 - t, 0, 0, 0)),
                  whole((nh, LANES, GLA_DK)), whole((nh, 1, GLA_DK)), whole((1, GLA_DV))],
        out_specs=[pl.BlockSpec((ts, nh * GLA_HB), rev), whole((nh, LANES, GLA_DK)), whole((nh, 1, GLA_DK)), whole((1, GLA_DV))],
        out_shape=[jax.ShapeDtypeStruct((s, nh * GLA_HB), MXU_DTYPE), jax.ShapeDtypeStruct((nh, LANES, GLA_DK), F32),
                   jax.ShapeDtypeStruct((nh, 1, GLA_DK), F32), jax.ShapeDtypeStruct((1, GLA_DV), F32)],
        scratch_shapes=[pltpu.VMEM((nh, GLA_DV, GLA_DK), F32), pltpu.VMEM((nh, ts, GLA_DK), F32)],
        compiler_params=_params("arbitrary"), name=name,
    )(dact, pj, o, st, wal, bal, ng)


def _adamw(w, gs, m, v, name, after=None):
    layers, rows, cols = w.shape
    gs = list(gs) if isinstance(gs, (list, tuple)) else gs
    n_g = len(gs) if isinstance(gs, list) else 1
    if rows % SUBLANES == 0:
        tr, tc = _tile(rows, (256, 128, 64, 32, 16, 8)), cols
    else:
        tr, tc = rows, _tile(cols, (256, 128))
    c1 = 1.0 / (1.0 - ADAM_B1 ** ADAM_STEP)
    c2 = 1.0 / (1.0 - ADAM_B2 ** ADAM_STEP)

    def body(*refs):
        g_refs, (w_ref, m_ref, v_ref) = refs[:n_g], refs[n_g:n_g + 3]
        go_ref, d_ref, mo_ref, vo_ref = refs[-4:]
        gv = g_refs[0][...]
        for l in range(1, n_g):
            gv = jnp.where(pl.program_id(0) == l, g_refs[l][...], gv)
        m2 = ADAM_B1 * m_ref[...] + (1.0 - ADAM_B1) * gv
        v2 = ADAM_B2 * v_ref[...] + (1.0 - ADAM_B2) * (gv * gv)
        d_ref[...] = (-ADAM_LR) * ((m2 * c1) / (jnp.sqrt(v2 * c2) + ADAM_EPS) + ADAM_WD * w_ref[...])
        go_ref[...] = gv
        mo_ref[...] = m2
        vo_ref[...] = v2

    spec = pl.BlockSpec((None, tr, tc), lambda l, i, j: (l, i, j))
    g_specs = [pl.BlockSpec((tr, tc), lambda l, i, j: (i, j))] * n_g if isinstance(gs, list) else [spec]
    extra = [] if after is None else [(after, _ANY)]
    shape = jax.ShapeDtypeStruct((layers, rows, cols), F32)
    return pl.pallas_call(
        body, grid=(layers, rows // tr, cols // tc), in_specs=g_specs + [spec] * 3 + [sp for _, sp in extra],
        out_specs=[spec] * 4, out_shape=[shape] * 4, compiler_params=_params("parallel", "parallel", "parallel"), name=name,
    )(*(gs if isinstance(gs, list) else [gs]), w, m, v, *[a for a, _ in extra])


def _gla_head_cols(w):
    d = w.shape[0]
    qk, dv = GLA_HEADS * GLA_DK, GLA_HEADS * GLA_DV
    q, k, v, r, z = jnp.split(w, [qk, 2 * qk, 2 * qk + dv, 2 * qk + 2 * dv], axis=1)
    zp = jnp.pad(z, ((0, 0), (0, LANES - GLA_RANK)))
    parts = [q.reshape(d, GLA_HEADS, GLA_DK), k.reshape(d, GLA_HEADS, GLA_DK), v.reshape(d, GLA_HEADS, GLA_DV),
             r.reshape(d, GLA_HEADS, GLA_DV), jnp.broadcast_to(zp[:, None, :], (d, GLA_HEADS, LANES))]
    return jnp.concatenate(parts, axis=2).reshape(d, GLA_HEADS * GLA_HB)


def _gla_unhead_cols(w):
    d = w.shape[0]
    w = w.reshape(d, GLA_HEADS, GLA_HB)
    o = 2 * GLA_DK + 2 * GLA_DV
    parts = [w[:, :, 0:GLA_DK].reshape(d, -1), w[:, :, GLA_DK:2 * GLA_DK].reshape(d, -1),
             w[:, :, 2 * GLA_DK:2 * GLA_DK + GLA_DV].reshape(d, -1), w[:, :, 2 * GLA_DK + GLA_DV:o].reshape(d, -1),
             jnp.sum(w[:, :, o:o + GLA_RANK], axis=1)]
    return jnp.concatenate(parts, axis=1)


def _gla_alpha_heads(w_alpha, b_alpha):
    wal = jnp.swapaxes(w_alpha.reshape(GLA_RANK, GLA_HEADS, GLA_DK), 0, 1)
    return jnp.pad(wal, ((0, 0), (0, LANES - GLA_RANK), (0, 0))), b_alpha.reshape(GLA_HEADS, 1, GLA_DK)


def _gla_layouts(w):
    w = dict(w)
    w["gla_wal"], w["gla_bal"] = _gla_alpha_heads(w["gla_w_alpha"], w["gla_b_alpha"])
    w["gla_w_in"] = _gla_head_cols(w["gla_w_in"])
    return w


def _col_slots(w):
    r, c = w.shape
    return jnp.moveaxis(w.reshape(r, N_CHIP, c // N_CHIP), 1, 0)


def _from_col_slots(w):
    n, r, c = w.shape
    return jnp.moveaxis(w, 0, 1).reshape(r, n * c)


def _block_rows_to_slots(w):
    g, r4, cc = w.shape
    return jnp.swapaxes(w.reshape(g, N_CHIP, r4 // N_CHIP, cc), 0, 1).reshape(N_CHIP, g * (r4 // N_CHIP), cc)


def _slots_to_block_rows(w, g):
    n, gr, cc = w.shape
    return jnp.swapaxes(w.reshape(n, g, gr // g, cc), 0, 1).reshape(g, n * (gr // g), cc)


def _local_step(x, tgt, mod, w, fetch=None, done=None, later=None):
    depth = mod.shape[0]
    row = lambda v: v.reshape(1, -1)
    w = dict(w)
    w["ffn_w_up"], w["ffn_w_down"] = dict(enumerate(w["ffn_w_up"])), dict(enumerate(w["ffn_w_down"]))

    def arrive(stage, after):
        if fetch is not None:
            for k, v in fetch(stage, after).items():
                if isinstance(v, dict):
                    w[k].update(v)
                else:
                    w[k] = v

    saved = []
    for i in range(depth):
        if i == 1:
            arrive("l1", x)
        sh_m, sc_m, gt_m, sh_f, sc_f, gt_f = (mod[i, j:j + 1] for j in range(6))
        g0, g1, g2, g3 = (w["norm_g"][i, j:j + 1] for j in range(4))
        tag = f"_l{i}"
        h = _norm_mod_fwd(x, g0, sc_m, sh_m, "norm_mix" + tag)
        if i % 2 == 0:
            pj = _mm(h, w["rg_w_in"], w_slots=N_CHIP, name="rg_in" + tag)
            act, aux = _rg_mid_fwd(pj, w["rg_conv_w"], row(w["rg_conv_b"]), w["rg_wa"], row(w["rg_ba"]), w["rg_wx"],
                                   row(w["rg_bx"]), row(w["rg_lambda"]), "rg_mid" + tag)
            y = _mm(act, w["rg_w_out"], name="rg_out" + tag)
        else:
            pj = _mm(h, w["gla_w_in"], name="gla_in" + tag)
            act, *aux = _gla_mid_fwd(pj, w["gla_wal"], w["gla_bal"], row(w["gla_norm_g"]), "gla_mid" + tag)
            y = _mm(act, w["gla_w_out"], name="gla_out" + tag)
        x1 = _post_fwd(x, y, g1, gt_m, "post_mix" + tag)
        if i == 0:
            arrive("ffn0", x1)
        h2 = _norm_mod_fwd(x1, g2, sc_f, sh_f, "norm_ffn" + tag)
        p = _mm(h2, w["ffn_w_up"][i], w_slots=N_CHIP, name="ffn_up" + tag)
        a, ga, gb = _ffn_mid_fwd(p, w["ffn_conv_w"][i], w["ffn_conv_b"][i:i + 1], "ffn_mid" + tag)
        y2 = _mm(a, w["ffn_w_down"][i], name="ffn_down" + tag)
        x2 = _post_fwd(x1, y2, g3, gt_f, "post_ffn" + tag)
        saved.append((x, h, pj, act, aux, y, x1, h2, p, (a, ga, gb), y2))
        x = x2

    cols, dx = _loss_grad(x, tgt, "loss")

    stacked = ("norm_g", "ffn_conv_w", "ffn_conv_b", "mod")
    gr = {k: [None] * depth for k in stacked + ("ffn_w_up", "ffn_w_down")}
    told = lambda stage: done(stage, gr) if done is not None else 0.0
    told_later = lambda stage, after: later(stage, after) if later is not None else 0.0
    for i in reversed(range(depth)):
        x0, h, pj, act, aux, y, x1, h2, p, (a, ga, gb), y2 = saved[i]
        sh_m, sc_m, gt_m, sh_f, sc_f, gt_f = (mod[i, j:j + 1] for j in range(6))
        g0, g1, g2, g3 = (w["norm_g"][i, j:j + 1] for j in range(4))
        tag = f"_l{i}"
        dy2, d_g3, d_gt_f = _post_bwd(dx, y2, g3, gt_f, "post_ffn_b" + tag)
        da = _mm(dy2, w["ffn_w_down"][i], tb=True, name="ffn_down_dx" + tag)
        gr["ffn_w_down"][i] = _mm(a, dy2, ta=True, name="ffn_down_dw" + tag)
        conv_w = w["ffn_conv_w"][i] + (told_later("l1", da) if i == 0 else 0.0)
        dp, dcw, dcb = _ffn_mid_bwd(da, p, ga, gb, conv_w, "ffn_mid_b" + tag)
        gr["ffn_conv_w"][i], gr["ffn_conv_b"][i] = _cat(dcw[0], dcw[1]), _cat(dcb[0], dcb[1])[0]
        dh2 = _mm(dp, w["ffn_w_up"][i], tb=True, a_parts=2, w_slots=N_CHIP, name="ffn_up_dx" + tag)
        gr["ffn_w_up"][i] = _mm(h2, dp, ta=True, b_parts=2, out_slots=N_CHIP, name="ffn_up_dw" + tag)
        dx1, d_g2, d_sc_f, d_sh_f = _norm_mod_bwd(dh2, x1, g2, sc_f, dx, "norm_ffn_b" + tag)
        if i == 0:
            gt_m = gt_m + told("ffn0")
        dy, d_g1, d_gt_m = _post_bwd(dx1, y, g1, gt_m, "post_mix_b" + tag)
        if i % 2 == 0:
            dact = _mm(dy, w["rg_w_out"], tb=True, name="rg_out_dx" + tag)
            gr["rg_w_out"] = _mm(act, dy, ta=True, name="rg_out_dw" + tag)
            lam = row(w["rg_lambda"]) + told_later("ffn0", gr["rg_w_out"])
            dpj, gr["rg_conv_w"], d_cb, gr["rg_wa"], d_ba, gr["rg_wx"], d_bx, d_lam = _rg_mid_bwd(
                dact, pj, aux, w["rg_conv_w"], row(w["rg_conv_b"]), w["rg_wa"], row(w["rg_ba"]), w["rg_wx"],
                row(w["rg_bx"]), lam, "rg_mid_b" + tag)
            gr["rg_conv_b"], gr["rg_ba"], gr["rg_bx"], gr["rg_lambda"] = d_cb[0], d_ba[0], d_bx[0], d_lam[0]
            dh = _mm(dpj, w["rg_w_in"], tb=True, a_parts=2, w_slots=N_CHIP, name="rg_in_dx" + tag)
            gr["rg_w_in"] = _mm(h, dpj, ta=True, b_parts=2, out_slots=N_CHIP, name="rg_in_dw" + tag)
        else:
            dact = _mm(dy, w["gla_w_out"], tb=True, name="gla_out_dx" + tag)
            gr["gla_w_out"] = _mm(act, dy, ta=True, name="gla_out_dw" + tag)
            dpj, d_wal, d_bal, d_ng = _gla_mid_bwd(dact, pj, aux[0], aux[1], w["gla_wal"], w["gla_bal"],
                                                   row(w["gla_norm_g"]), "gla_mid_b" + tag)
            gr["gla_w_alpha"] = jnp.swapaxes(d_wal[:, :GLA_RANK, :], 0, 1).reshape(GLA_RANK, GLA_HEADS * GLA_DK)
            gr["gla_b_alpha"], gr["gla_norm_g"] = d_bal.reshape(-1), d_ng[0]
            dh = _mm(dpj, w["gla_w_in"], tb=True, name="gla_in_dx" + tag)
            gr["gla_w_in"] = _gla_unhead_cols(_mm(h, dpj, ta=True, name="gla_in_dw" + tag))
            mod = mod.at[0].add(told("l1"))
        dx, d_g0, d_sc_m, d_sh_m = _norm_mod_bwd(dh, x0, g0, sc_m, dx1, "norm_mix_b" + tag)
        gr["norm_g"][i] = jnp.concatenate([d_g0, d_g1, d_g2, d_g3], axis=0)
        gr["mod"][i] = jnp.concatenate([d_sh_m, d_sc_m, d_gt_m, d_sh_f, d_sc_f, d_gt_f], axis=0)
    for k in stacked:
        gr[k] = jnp.stack(gr[k])
    return cols, dx, gr


ADA_ROWS = 16


def _ada_fwd(c16, ada_w, ada_b, name):
    depth, d, n = ada_w.shape
    tn = _tile(n, (512, 256, 128))

    def body(c_ref, w_ref, b_ref, o_ref):
        cv = c_ref[...]
        o_ref[0] = _dot_nn(cv * _sigmoid(cv), w_ref[0]) + b_ref[0]

    return pl.pallas_call(
        body, grid=(depth, n // tn),
        in_specs=[pl.BlockSpec((ADA_ROWS, d), lambda l, j: (0, 0)), pl.BlockSpec((1, d, tn), lambda l, j: (l, 0, j)),
                  pl.BlockSpec((1, 1, tn), lambda l, j: (l, 0, j))],
        out_specs=pl.BlockSpec((1, ADA_ROWS, tn), lambda l, j: (l, 0, j)),
        out_shape=jax.ShapeDtypeStruct((depth, ADA_ROWS, n), F32),
        compiler_params=_params("parallel", "parallel"), name=name,
    )(c16, ada_w, ada_b)


def _ada_bwd(c16, dmod16, name):
    depth, _, n = dmod16.shape
    d = c16.shape[1]
    tn = _tile(n, (512, 256, 128))

    def body(c_ref, dm_ref, o_ref):
        cv = c_ref[...]
        o_ref[0] = _dot_tn(cv * _sigmoid(cv), dm_ref[0])

    return pl.pallas_call(
        body, grid=(depth, n // tn),
        in_specs=[pl.BlockSpec((ADA_ROWS, d), lambda l, j: (0, 0)), pl.BlockSpec((1, ADA_ROWS, tn), lambda l, j: (l, 0, j))],
        out_specs=pl.BlockSpec((1, d, tn), lambda l, j: (l, 0, j)),
        out_shape=jax.ShapeDtypeStruct((depth, d, n), F32),
        compiler_params=_params("parallel", "parallel"), name=name,
    )(c16, dmod16)


PACK_COLS = 1024
_ANY = pl.BlockSpec(memory_space=pl.ANY)
_VMEM = pl.BlockSpec(memory_space=pltpu.VMEM)


def _place():
    return lax.axis_index("x"), lax.axis_index("y"), lax.axis_index("c")


def _other_chips(x, y):
    return [(1 - x, y), (x, 1 - y), (1 - x, 1 - y)]


def _rcopy(src, dst, send_sems, recv_sems, k, peer):
    return pltpu.make_async_remote_copy(src_ref=src, dst_ref=dst, send_sem=send_sems.at[k], recv_sem=recv_sems.at[k],
                                        device_id=peer, device_id_type=MESH)


def _all_gather_8(v, name):
    r, cc = v.shape

    def body(v_ref, out_ref, send_sems, recv_sems, local_sem):
        x, y, c = _place()
        me = 4 * x + 2 * y + c
        mine = pltpu.make_async_copy(v_ref, out_ref.at[me], local_sem)
        mine.start()
        peers = []
        for k in range(1, N_DEV):
            px = 1 - x if k & 4 else x
            py = 1 - y if k & 2 else y
            pc = 1 - c if k & 1 else c
            peers.append((px, py, pc))
        sends = [_rcopy(v_ref, out_ref.at[me], send_sems, recv_sems, k, p) for k, p in enumerate(peers)]
        for cp in sends:
            cp.start()
        for k, (px, py, pc) in enumerate(peers):
            _rcopy(v_ref, out_ref.at[4 * px + 2 * py + pc], send_sems, recv_sems, k, (px, py, pc)).wait_recv()
        for cp in sends:
            cp.wait_send()
        mine.wait()

    return pl.pallas_call(
        body, in_specs=[_VMEM], out_specs=_VMEM, out_shape=jax.ShapeDtypeStruct((N_DEV, r, cc), v.dtype),
        scratch_shapes=[pltpu.SemaphoreType.DMA((N_DEV - 1,)), pltpu.SemaphoreType.DMA((N_DEV - 1,)), pltpu.SemaphoreType.DMA],
        compiler_params=pltpu.CompilerParams(vmem_limit_bytes=VMEM_LIMIT), name=name,
    )(v)


def _gather_chips(shards, name):
    n = len(shards)
    per = 2 * (N_CHIP - 1)

    def body(*refs):
        ins, outs, (send_sems, recv_sems) = refs[:n], refs[n:2 * n], refs[2 * n:]
        x, y, c = _place()
        chip = 2 * x + y
        chips = _other_chips(x, y)
        rows = [(pl.ds(c * (r.shape[0] // 2), r.shape[0] // 2), pl.ds((1 - c) * (r.shape[0] // 2), r.shape[0] // 2)) for r in ins]
        first = [_rcopy(ins[i].at[rows[i][0]], outs[i].at[chip, rows[i][0]], send_sems, recv_sems, per * i + j, (px, py, c))
                 for i in range(n) for j, (px, py) in enumerate(chips)]
        for cp in first:
            cp.start()
        passed = []
        for i in range(n):
            for j, (px, py) in enumerate(chips):
                landed = outs[i].at[2 * px + py, rows[i][0]]
                _rcopy(ins[i].at[rows[i][0]], landed, send_sems, recv_sems, per * i + j, (px, py, c)).wait_recv()
                fw = _rcopy(landed, landed, send_sems, recv_sems, per * i + N_CHIP - 1 + j, (x, y, 1 - c))
                fw.start()
                passed.append(fw)
        for i in range(n):
            for j, (px, py) in enumerate(chips):
                landed = outs[i].at[2 * px + py, rows[i][1]]
                _rcopy(landed, landed, send_sems, recv_sems, per * i + N_CHIP - 1 + j, (x, y, 1 - c)).wait_recv()
        for cp in first + passed:
            cp.wait_send()

    return pl.pallas_call(
        body, in_specs=[_ANY] * n, out_specs=[_ANY] * n,
        out_shape=[jax.ShapeDtypeStruct((N_CHIP,) + sh.shape, sh.dtype) for sh in shards],
        scratch_shapes=[pltpu.SemaphoreType.DMA((per * n,)), pltpu.SemaphoreType.DMA((per * n,))], name=name,
    )(*shards)


def _pair_exchange(gs, name):
    n = len(gs)

    def body(*refs):
        ins, outs, (send_sems, recv_sems) = refs[:n], refs[n:2 * n], refs[2 * n:]
        x, y, c = _place()
        copies = []
        for i in range(n):
            half = ins[i].shape[1] // 2
            copies.append(_rcopy(ins[i].at[:, pl.ds((1 - c) * half, half)], outs[i], send_sems, recv_sems, i, (x, y, 1 - c)))
        for cp in copies:
            cp.start()
        for cp in copies:
            cp.wait()

    return pl.pallas_call(
        body, in_specs=[_ANY] * n, out_specs=[_ANY] * n,
        out_shape=[jax.ShapeDtypeStruct((g.shape[0], g.shape[1] // 2, g.shape[2]), g.dtype) for g in gs],
        scratch_shapes=[pltpu.SemaphoreType.DMA((n,)), pltpu.SemaphoreType.DMA((n,))], name=name,
    )(*gs)


_ROW_TILES = (640, 512, 352, 256, 128, 64, 32, 16)


def _pair_sum(g, other, c_idx, name):
    n, half, cc = other.shape
    tr = _tile(half, _ROW_TILES)

    def body(c_ref, g_ref, o_ref, out_ref):
        out_ref[...] = (g_ref[...] + o_ref[...]).astype(out_ref.dtype)

    return pl.pallas_call(
        body,
        grid_spec=pltpu.PrefetchScalarGridSpec(
            num_scalar_prefetch=1, grid=(n, half // tr),
            in_specs=[pl.BlockSpec((None, None, tr, cc), lambda k, i, c_ref: (k, c_ref[0], i, 0)),
                      pl.BlockSpec((None, tr, cc), lambda k, i, c_ref: (k, i, 0))],
            out_specs=pl.BlockSpec((None, tr, cc), lambda k, i, c_ref: (k, i, 0))),
        out_shape=jax.ShapeDtypeStruct((n, half, cc), BF16),
        compiler_params=_params("parallel", "parallel"), name=name,
    )(c_idx, g.reshape(n, 2, half, cc), other)


def _chip_exchange(ps, name):
    n = len(ps)
    per = N_CHIP - 1

    def body(*refs):
        ins, outs, (send_sems, recv_sems) = refs[:n], refs[n:2 * n], refs[2 * n:]
        x, y, c = _place()
        chip = 2 * x + y
        chips = _other_chips(x, y)
        sends = [_rcopy(ins[i].at[2 * px + py], outs[i].at[chip], send_sems, recv_sems, per * i + j, (px, py, c))
                 for i in range(n) for j, (px, py) in enumerate(chips)]
        for cp in sends:
            cp.start()
        for i in range(n):
            for j, (px, py) in enumerate(chips):
                _rcopy(ins[i].at[chip], outs[i].at[2 * px + py], send_sems, recv_sems, per * i + j, (px, py, c)).wait_recv()
        for cp in sends:
            cp.wait_send()

    return pl.pallas_call(
        body, in_specs=[_ANY] * n, out_specs=[_ANY] * n, out_shape=[jax.ShapeDtypeStruct(p.shape, p.dtype) for p in ps],
        scratch_shapes=[pltpu.SemaphoreType.DMA((per * n,)), pltpu.SemaphoreType.DMA((per * n,))], name=name,
    )(*ps)


_HBM = pl.BlockSpec(memory_space=pltpu.HBM)
_SEM = pl.BlockSpec(memory_space=pltpu.SEMAPHORE)
_DATAFLOW = pltpu.SideEffectType.DATAFLOW_SIDE_EFFECTING


def _split_copies(srcs, lands, send_sems, recv_sems, mode, arriving):
    x, y, c = _place()
    chip = 2 * x + y
    out = []
    for i, (src, land) in enumerate(zip(srcs, lands)):
        if mode == "pair":
            half = src.shape[1] // 2
            out.append(_rcopy(src.at[:, pl.ds((1 - c) * half, half)], land, send_sems, recv_sems, i, (x, y, 1 - c)))
            continue
        for j, (px, py) in enumerate(_other_chips(x, y)):
            there = 2 * px + py
            part = src.at[there] if mode == "slots" else src
            out.append(_rcopy(part, land.at[there if arriving else chip], send_sems, recv_sems, (N_CHIP - 1) * i + j, (px, py, c)))
    return out


def _land_shape(src, mode):
    if mode == "pair":
        return (src.shape[0], src.shape[1] // 2, src.shape[2])
    return (N_CHIP,) + (src.shape[1:] if mode == "slots" else src.shape)


def _send_start(srcs, mode, name):
    n = len(srcs)
    n_sem = n if mode == "pair" else (N_CHIP - 1) * n
    lands = [lax.empty(_land_shape(s, mode), s.dtype) for s in srcs]

    def body(*refs):
        ins, zones, (send_sems, recv_sems) = refs[:n], refs[n:2 * n], refs[2 * n:2 * n + 2]
        for cp in _split_copies(ins, zones, send_sems, recv_sems, mode, False):
            cp.start()
        refs[-1][...] = jnp.zeros_like(refs[-1])

    hbm = lambda a: pltpu.HBM(a.shape, a.dtype)
    outs = pl.pallas_call(
        body, name=name, in_specs=[_HBM] * (2 * n),
        out_shape=(pltpu.SemaphoreType.DMA((n_sem,)), pltpu.SemaphoreType.DMA((n_sem,)), *[hbm(a) for a in srcs],
                   *[hbm(a) for a in lands], jax.ShapeDtypeStruct((SUBLANES, LANES), F32)),
        out_specs=(_SEM, _SEM, *[_HBM] * (2 * n), _VMEM), input_output_aliases={i: 2 + i for i in range(2 * n)},
        compiler_params=pltpu.CompilerParams(has_side_effects=_DATAFLOW),
    )(*[pltpu.with_memory_space_constraint(a, pltpu.HBM) for a in list(srcs) + lands])
    return (outs[0], outs[1], list(outs[2:2 + n]), list(outs[2 + n:2 + 2 * n])), outs[-1]


def _send_wait(state, after, mode, name):
    send_sems, recv_sems, srcs, lands = state
    n = len(srcs)

    def body(*refs):
        ins, zones, (send_s, recv_s) = refs[:n], refs[n:2 * n], refs[2 * n:2 * n + 2]
        for cp in _split_copies(ins, zones, send_s, recv_s, mode, True):
            cp.wait_send()
            cp.wait_recv()

    hbm = lambda a: pltpu.HBM(a.shape, a.dtype)
    outs = pl.pallas_call(
        body, name=name, in_specs=[_HBM] * (2 * n) + [_SEM, _SEM, _ANY],
        out_shape=tuple(hbm(a) for a in srcs + lands), out_specs=tuple([_HBM] * (2 * n)),
        input_output_aliases={i: i for i in range(2 * n)},
        compiler_params=pltpu.CompilerParams(has_side_effects=_DATAFLOW),
    )(*srcs, *lands, send_sems, recv_sems, after)
    return list(outs[n:])


def _sum_lead(v, name):
    n, r, cc = v.shape
    tr = _tile(r, _ROW_TILES + (8,))

    def body(v_ref, o_ref):
        acc = v_ref[0].astype(F32)
        for k in range(1, n):
            acc = acc + v_ref[k].astype(F32)
        o_ref[...] = acc

    return pl.pallas_call(
        body, grid=(r // tr,), in_specs=[pl.BlockSpec((n, tr, cc), lambda i: (0, i, 0))],
        out_specs=pl.BlockSpec((tr, cc), lambda i: (i, 0)), out_shape=jax.ShapeDtypeStruct((r, cc), F32),
        compiler_params=_params("parallel"), name=name,
    )(v)


def _chip_sum(arrived, mine, chip_idx, name):
    n, r, cc = arrived.shape
    tr = _tile(r, _ROW_TILES)

    def body(chip_ref, a_ref, m_ref, o_ref):
        acc = jnp.zeros((tr, cc), F32)
        for k in range(n):
            acc = acc + jnp.where(chip_ref[0] == k, m_ref[...], a_ref[k]).astype(F32)
        o_ref[...] = acc

    return pl.pallas_call(
        body,
        grid_spec=pltpu.PrefetchScalarGridSpec(
            num_scalar_prefetch=1, grid=(r // tr,),
            in_specs=[pl.BlockSpec((n, tr, cc), lambda i, chip_ref: (0, i, 0)),
                      pl.BlockSpec((None, tr, cc), lambda i, chip_ref: (chip_ref[0], i, 0))],
            out_specs=pl.BlockSpec((tr, cc), lambda i, chip_ref: (i, 0))),
        out_shape=jax.ShapeDtypeStruct((r, cc), F32), compiler_params=_params("parallel"), name=name,
    )(chip_idx, arrived, mine)


def _pair_share(reds, name):
    n = len(reds)

    def body(*refs):
        ins, outs, (send_sems, recv_sems) = refs[:n], refs[n:2 * n], refs[2 * n:]
        x, y, c = _place()
        copies = [_rcopy(ins[i], outs[i].at[c], send_sems, recv_sems, i, (x, y, 1 - c)) for i in range(n)]
        for cp in copies:
            cp.start()
        for i in range(n):
            _rcopy(ins[i], outs[i].at[1 - c], send_sems, recv_sems, i, (x, y, 1 - c)).wait_recv()
        for cp in copies:
            cp.wait_send()

    return pl.pallas_call(
        body, in_specs=[_ANY] * n, out_specs=[_ANY] * n, out_shape=[jax.ShapeDtypeStruct((2,) + r.shape, r.dtype) for r in reds],
        scratch_shapes=[pltpu.SemaphoreType.DMA((n,)), pltpu.SemaphoreType.DMA((n,))], name=name,
    )(*reds)


def _pack(arrs, rows_multiple, dtype):
    flat = jnp.concatenate([a.reshape(-1).astype(dtype) for a in arrs])
    unit = rows_multiple * PACK_COLS
    total = -(-flat.shape[0] // unit) * unit
    return jnp.pad(flat, (0, total - flat.shape[0])).reshape(-1, PACK_COLS)


def _unpack(buf, shapes):
    lead = buf.shape[:-2]
    flat = buf.reshape(*lead, -1)
    out, off = [], 0
    for shp in shapes:
        n = 1
        for s in shp:
            n *= s
        out.append(flat[..., off:off + n].reshape(*lead, *shp))
        off += n
    return out


def _join_shards(parts, axis):
    moved = jnp.moveaxis(parts, 0, axis)
    shp = list(moved.shape)
    shp[axis:axis + 2] = [shp[axis] * shp[axis + 1]]
    return moved.reshape(shp)


def _my_shard(full, axis, chip):
    n = full.shape[axis] // N_CHIP
    return lax.dynamic_slice_in_dim(full, chip * n, n, axis)


SMALL = {"norm_g": 2, "ffn_conv_w": 2, "rg_conv_w": 2, "gla_w_alpha": 2, "gla_b_alpha": 1, "gla_norm_g": 1,
         "ada_b": None, "ffn_conv_b": None, "rg_conv_b": None, "rg_ba": None, "rg_bx": None, "rg_lambda": None}
BIG = {"rg_w_in": True, "rg_wa": False, "rg_wx": False, "rg_w_out": False, "ffn_w_up": True, "ffn_w_down": False,
       "gla_w_in": True, "gla_w_out": False}
WEIGHTS = ["ada_w", "ada_b", "norm_g", "ffn_w_up", "ffn_conv_w", "ffn_conv_b", "ffn_w_down", "rg_w_in", "rg_conv_w", "rg_conv_b",
           "rg_wa", "rg_ba", "rg_wx", "rg_bx", "rg_lambda", "rg_w_out", "gla_w_in", "gla_w_alpha", "gla_b_alpha", "gla_norm_g",
           "gla_w_out"]


def kernel(x, c, ada_w, ada_b, norm_g, ffn_w_up, ffn_conv_w, ffn_conv_b, ffn_w_down, rg_w_in, rg_conv_w, rg_conv_b, rg_wa, rg_ba, rg_wx, rg_bx, rg_lambda, rg_w_out, gla_w_in, gla_w_alpha, gla_b_alpha, gla_norm_g, gla_w_out, loss_target, m_ada_w, m_ada_b, m_norm_g, m_ffn_w_up, m_ffn_conv_w, m_ffn_conv_b, m_ffn_w_down, m_rg_w_in, m_rg_conv_w, m_rg_conv_b, m_rg_wa, m_rg_ba, m_rg_wx, m_rg_bx, m_rg_lambda, m_rg_w_out, m_gla_w_in, m_gla_w_alpha, m_gla_b_alpha, m_gla_norm_g, m_gla_w_out, v_ada_w, v_ada_b, v_norm_g, v_ffn_w_up, v_ffn_conv_w, v_ffn_conv_b, v_ffn_w_down, v_rg_w_in, v_rg_conv_w, v_rg_conv_b, v_rg_wa, v_rg_ba, v_rg_wx, v_rg_bx, v_rg_lambda, v_rg_w_out, v_gla_w_in, v_gla_w_alpha, v_gla_b_alpha, v_gla_norm_g, v_gla_w_out):
    wts = dict(ada_w=ada_w, ada_b=ada_b, norm_g=norm_g, ffn_w_up=ffn_w_up, ffn_conv_w=ffn_conv_w, ffn_conv_b=ffn_conv_b,
               ffn_w_down=ffn_w_down, rg_w_in=rg_w_in, rg_conv_w=rg_conv_w, rg_conv_b=rg_conv_b, rg_wa=rg_wa, rg_ba=rg_ba,
               rg_wx=rg_wx, rg_bx=rg_bx, rg_lambda=rg_lambda, rg_w_out=rg_w_out, gla_w_in=gla_w_in, gla_w_alpha=gla_w_alpha,
               gla_b_alpha=gla_b_alpha, gla_norm_g=gla_norm_g, gla_w_out=gla_w_out)
    mom1 = dict(ada_w=m_ada_w, ada_b=m_ada_b, norm_g=m_norm_g, ffn_w_up=m_ffn_w_up, ffn_conv_w=m_ffn_conv_w,
                ffn_conv_b=m_ffn_conv_b, ffn_w_down=m_ffn_w_down, rg_w_in=m_rg_w_in, rg_conv_w=m_rg_conv_w,
                rg_conv_b=m_rg_conv_b, rg_wa=m_rg_wa, rg_ba=m_rg_ba, rg_wx=m_rg_wx, rg_bx=m_rg_bx, rg_lambda=m_rg_lambda,
                rg_w_out=m_rg_w_out, gla_w_in=m_gla_w_in, gla_w_alpha=m_gla_w_alpha, gla_b_alpha=m_gla_b_alpha,
                gla_norm_g=m_gla_norm_g, gla_w_out=m_gla_w_out)
    mom2 = dict(ada_w=v_ada_w, ada_b=v_ada_b, norm_g=v_norm_g, ffn_w_up=v_ffn_w_up, ffn_conv_w=v_ffn_conv_w,
                ffn_conv_b=v_ffn_conv_b, ffn_w_down=v_ffn_w_down, rg_w_in=v_rg_w_in, rg_conv_w=v_rg_conv_w,
                rg_conv_b=v_rg_conv_b, rg_wa=v_rg_wa, rg_ba=v_rg_ba, rg_wx=v_rg_wx, rg_bx=v_rg_bx, rg_lambda=v_rg_lambda,
                rg_w_out=v_rg_w_out, gla_w_in=v_gla_w_in, gla_w_alpha=v_gla_w_alpha, gla_b_alpha=v_gla_b_alpha,
                gla_norm_g=v_gla_norm_g, gla_w_out=v_gla_w_out)
    xi, yi, ci = _place()
    chip, me = 2 * xi + yi, 4 * xi + 2 * yi + ci
    d = x.shape[-1]
    depth = ada_w.shape[0]
    n_ada = ada_w.shape[-1]
    sharded_small = [k for k, ax in SMALL.items() if ax is not None]

    sm = _all_gather_8(_pack([c] + [wts[k] for k in sharded_small], SUBLANES, F32), "gather_small")
    c_all = sm[:, 0, :]
    parts = _unpack(sm[0::2], [c.shape] + [wts[k].shape for k in sharded_small])[1:]
    full = {k: _join_shards(p, SMALL[k]) for k, p in zip(sharded_small, parts)}
    for k, ax in SMALL.items():
        if ax is None:
            full[k] = wts[k]

    c16 = jnp.pad(c_all, ((0, ADA_ROWS - N_DEV), (0, 0)))
    ada_b_mine = lax.dynamic_slice_in_dim(ada_b, chip * n_ada, n_ada, 1)[:, None, :]
    mod_cols = _ada_fwd(c16, ada_w, ada_b_mine, "ada_fwd")
    mod_all = _all_gather_8(mod_cols.reshape(-1, PACK_COLS), "gather_mod")[0::2].reshape(N_CHIP, depth, ADA_ROWS, n_ada)
    mod = jnp.swapaxes(lax.dynamic_index_in_dim(mod_all, me, 2, keepdims=False), 0, 1).reshape(depth, 6, d)

    items = [(k, l) for k in BIG for l in range(wts[k].shape[0])]
    stage_of = lambda k, l: "rg" if k.startswith("rg_") else ("ffn0" if (k.startswith("ffn_") and l == 0) else "l1")
    staged = {st: [it for it in items if stage_of(*it) == st] for st in ("rg", "ffn0", "l1")}
    staged["l1"].sort(key=lambda it: not it[0].startswith("gla_"))
    shard = lambda k, l: wts[k][l].reshape(-1, wts[k].shape[-1]).astype(BF16)
    own = lambda got, mine: [lax.dynamic_update_index_in_dim(g, m, chip, 0) for g, m in zip(got, mine)]
    rows_joined = lambda v: v.reshape(-1, v.shape[-1])

    def placed(its, slots):
        out = {"ffn_w_up": {}, "ffn_w_down": {}}
        for (k, l), v in zip(its, slots):
            if k == "ffn_w_up":
                out[k][l] = v
            elif k == "ffn_w_down":
                out[k][l] = rows_joined(v)
            elif k in ("rg_wa", "rg_wx"):
                out[k] = _slots_to_block_rows(v, RG_BLOCKS)
            elif k == "gla_w_in":
                out[k] = _gla_head_cols(_from_col_slots(v))
            else:
                out[k] = v if BIG[k] else rows_joined(v)
        return out

    after_mod = (mod[0, 0, 0] * 0.0).astype(BF16)
    sh_rg = [shard(k, l) + after_mod for k, l in staged["rg"]]
    local = {k: (v if k in ("norm_g", "ffn_conv_w", "ffn_conv_b") else v[0]) for k, v in full.items()}
    local["gla_wal"], local["gla_bal"] = _gla_alpha_heads(local["gla_w_alpha"], local["gla_b_alpha"])
    local.update(placed(staged["rg"], own(_gather_chips(sh_rg, "gather_weights_rg"), sh_rg)))
    sh_late, flying = {}, {}
    after_rg = (local["rg_w_out"][0, 0].astype(F32) * 0.0).astype(BF16)
    sh_late["ffn0"] = [shard(k, l) + after_rg for k, l in staged["ffn0"]]
    flying["ffn0"], tok = _send_start(sh_late["ffn0"], "whole", "weights_ffn0_start")
    sh_late["l1"] = [shard(k, l) + tok[0, 0].astype(BF16) for k, l in staged["l1"]]
    flying["l1"], tok2 = _send_start(sh_late["l1"], "whole", "weights_l1_start")
    mod = mod + (tok[0, 0] + tok2[0, 0])

    def fetch(stage, after):
        got = _send_wait(flying[stage], after, "whole", f"weights_{stage}_wait")
        return placed(staged[stage], own(got, sh_late[stage]))

    c_idx = ci.reshape(1).astype(jnp.int32)
    gslots, paired, psums, sent = {}, {}, {}, {}

    def grad_slots(gr, k, l):
        g = gr[k][l] if k in ("ffn_w_up", "ffn_w_down") else gr[k]
        if k in ("rg_wa", "rg_wx"):
            return _block_rows_to_slots(g)
        if k == "gla_w_in":
            return _col_slots(g)
        return g if BIG[k] else g.reshape(N_CHIP, -1, g.shape[-1])

    def done(stage, gr):
        gslots[stage] = [grad_slots(gr, k, l) for k, l in staged[stage]]
        paired[stage], token = _send_start(gslots[stage], "pair", f"grads_{stage}_pair_start")
        return token[0, 0]

    def later(stage, after):
        theirs = _send_wait(paired[stage], after, "pair", f"grads_{stage}_pair_wait")
        psums[stage] = [_pair_sum(g, t, c_idx, f"grads_pair_sum_{k}{l}") for (k, l), g, t in zip(staged[stage], gslots[stage], theirs)]
        sent[stage], token = _send_start(psums[stage], "slots", f"grads_{stage}_start")
        return token[0, 0]

    cols, grad_x, gr = _local_step(x[0], loss_target[0], mod, local, fetch, done, later)
    loss = lax.psum(0.5 * jnp.sum(cols) / d, ("x", "y", "c"))

    small_names = [k for k in SMALL if k != "ada_b"]
    gs = _all_gather_8(_pack([gr[k] for k in small_names] + [gr["mod"]], SUBLANES, F32), "gather_small_grads")
    small_shapes = [full[k].shape for k in small_names] + [(depth, 6 * d)]
    *small_sum, g_ada_b = _unpack(_sum_lead(gs, "sum_small_grads"), small_shapes)
    grads = dict(zip(small_names, small_sum))
    grads["ada_b"] = g_ada_b
    for k in sharded_small:
        grads[k] = _my_shard(grads[k], SMALL[k], chip)
    dmod_all = _unpack(gs, small_shapes)[-1].reshape(N_DEV, depth, N_CHIP, n_ada)
    dmod_mine = jnp.swapaxes(lax.dynamic_index_in_dim(dmod_all, chip, 2, keepdims=False), 0, 1)
    g_ada_w = _ada_bwd(c16, jnp.pad(dmod_mine, ((0, 0), (0, ADA_ROWS - N_DEV), (0, 0))), "ada_bwd")

    gslots["rg"] = [grad_slots(gr, k, l) for k, l in staged["rg"]]
    theirs = _pair_exchange(gslots["rg"], "grads_rg_pair_exchange")
    psums["rg"] = [_pair_sum(g, t, c_idx, f"grads_pair_sum_{k}{l}") for (k, l), g, t in zip(staged["rg"], gslots["rg"], theirs)]
    sent["rg"], rg_sent = _send_start(psums["rg"], "slots", "grads_rg_start")
    chip_idx = chip.reshape(1).astype(jnp.int32)
    delta, new_m, new_v = {}, {}, {}

    def reduce_and_update(stages, after, dep):
        its = [(st, n) for st in stages for n in range(len(staged[st]))]
        arrived = {st: _send_wait(sent[st], after, "slots", f"grads_{st}_wait") for st in stages}
        halves = [_chip_sum(arrived[st][n], psums[st][n], chip_idx, "grads_chip_sum_%s%d" % staged[st][n]) for st, n in its]
        shared = _pair_share(halves, "grads_pair_share_" + stages[0])
        reduced = [lax.dynamic_update_index_in_dim(s2, h, ci, 0).reshape(-1, h.shape[-1]) for s2, h in zip(shared, halves)]
        last = None
        for k in BIG:
            gs_k = [g for (st, n), g in zip(its, reduced) if staged[st][n][0] == k]
            if gs_k:
                last = update(k, gs_k, dep)
        return last

    def update(k, gs_k, dep=None):
        shp = wts[k].shape
        if k == "gla_w_in":
            view, back = (lambda a: jnp.swapaxes(a, 1, 2)), (lambda o: jnp.swapaxes(o, 1, 2))
            gs_k = [g.T for g in gs_k]
        else:
            view, back = (lambda a: a.reshape(a.shape[0], -1, a.shape[-1])), (lambda o: o.reshape(shp))
        outs = _adamw(view(wts[k]), gs_k, view(mom1[k]), view(mom2[k]), "adamw_" + k, dep)
        grads[k], delta[k], new_m[k], new_v[k] = (back(o) for o in outs)
        return new_v[k]

    done_late = reduce_and_update(("ffn0", "l1"), grad_x, rg_sent)
    update("ada_w", g_ada_w, rg_sent)
    small_shard_shapes = [wts[k].shape for k in SMALL]
    packed = [_pack([src[k] for k in SMALL], SUBLANES, F32) for src in (wts, grads, mom1, mom2)]
    outs = _adamw(packed[0][None], [packed[1]], packed[2][None], packed[3][None], "adamw_small", rg_sent)
    for dst, o in zip((delta, new_m, new_v), outs[1:]):
        for k, a in zip(SMALL, _unpack(o[0], small_shard_shapes)):
            dst[k] = a
    reduce_and_update(("rg",), done_late, None)

    return (loss, grad_x[None], *[grads[k] for k in WEIGHTS], *[delta[k] for k in WEIGHTS], *[new_m[k] for k in WEIGHTS],
            *[new_v[k] for k in WEIGHTS])
```

```python
import jax
import jax.numpy as jnp
from jax import lax
from jax.experimental import pallas as pl
from jax.experimental.pallas import tpu as pltpu

F32 = jnp.float32
BF16 = jnp.bfloat16
MXU_DTYPE = BF16

EPS = 1e-6
RG_C = 8.0
RG_BLOCKS = 4
RG_CONV = 4
GLA_HEADS = 4
GLA_TAU = 16.0
GLA_CHUNK = 64
GLA_RANK = 16
FFN_CONV = 3
ADAM_LR = 0.001
ADAM_B1 = 0.9
ADAM_B2 = 0.999
ADAM_EPS = 1e-08
ADAM_WD = 0.01
ADAM_STEP = 10

LANES = 128
SUBLANES = 8
VMEM_LIMIT = 56 * 1024 * 1024
CB = 256
MESH = pl.DeviceIdType.MESH
N_DEV = 8
N_CHIP = 4


def _params(*sem):
    return pltpu.CompilerParams(dimension_semantics=sem, vmem_limit_bytes=VMEM_LIMIT)


def _tile(dim, prefs):
    for p in prefs:
        if dim % p == 0:
            return p
    return dim


def _dot(a, b, dims):
    return lax.dot_general(a.astype(MXU_DTYPE), b.astype(MXU_DTYPE), (dims, ((), ())), preferred_element_type=F32)


def _dot_nn(a, b):
    return _dot(a, b, ((1,), (0,)))


def _dot_nt(a, b):
    return _dot(a, b, ((1,), (1,)))


def _dot_tn(a, b):
    return _dot(a, b, ((0,), (0,)))


def _mm(a, b, *, ta=False, tb=False, a_parts=1, b_parts=1, w_slots=1, out_slots=1, out_dtype=F32, name):
    if ta:
        k_dim, m_dim = a.shape
        n_dim = b.shape[-1] * b_parts
    else:
        m_dim, k_dim = a.shape[-2], a.shape[-1] * a_parts
        n_dim = b.shape[-2] if tb else b.shape[-1] * w_slots
    n_unit = n_dim // max(b_parts, out_slots, 1 if tb else w_slots)
    k_unit = k_dim // max(a_parts, w_slots if tb else 1)
    tm = _tile(m_dim, (1024, 1408, 512, 256, 128))
    tn = _tile(n_unit, (1024, 1408, 896, 512, 256, 128))
    tk = _tile(k_unit, (1024, 1408, 896, 512, 256, 128))
    nk = k_dim // tk
    dims = ((0 if ta else 1,), (1 if tb else 0,))

    def spec(shape, parts, total, tile, col_grid, row_grid):
        per = total // parts // tile

        def index(i, j, k):
            g = {"i": i, "j": j, "k": k}
            col, row = g[col_grid], g[row_grid]
            return (row, col) if parts == 1 else (col // per, row, col % per)

        return pl.BlockSpec(shape if parts == 1 else (None,) + shape, index)

    def body(a_ref, b_ref, o_ref, acc_ref):
        k = pl.program_id(2)

        @pl.when(k == 0)
        def _():
            acc_ref[...] = jnp.zeros_like(acc_ref)

        acc_ref[...] += _dot(a_ref[...], b_ref[...], dims)

        @pl.when(k == nk - 1)
        def _():
            o_ref[...] = acc_ref[...].astype(o_ref.dtype)

    if ta:
        a_spec = spec((tk, tm), 1, m_dim, tm, "i", "k")
        b_spec = spec((tk, tn), b_parts, n_dim, tn, "j", "k")
    elif tb:
        a_spec = spec((tm, tk), a_parts, k_dim, tk, "k", "i")
        b_spec = spec((tn, tk), w_slots, k_dim, tk, "k", "j")
    else:
        a_spec = spec((tm, tk), a_parts, k_dim, tk, "k", "i")
        b_spec = spec((tk, tn), w_slots, n_dim, tn, "j", "k")
    out_shape = (m_dim, n_dim) if out_slots == 1 else (out_slots, m_dim, n_dim // out_slots)
    return pl.pallas_call(
        body,
        grid=(m_dim // tm, n_dim // tn, nk),
        in_specs=[a_spec, b_spec],
        out_specs=spec((tm, tn), out_slots, n_dim, tn, "j", "i"),
        out_shape=jax.ShapeDtypeStruct(out_shape, out_dtype),
        scratch_shapes=[pltpu.VMEM((tm, tn), F32)],
        compiler_params=_params("parallel", "parallel", "arbitrary"),
        name=name,
    )(a, b)


def _row_specs(s, d, ts):
    return pl.BlockSpec((ts, d), lambda i: (i, 0)), pl.BlockSpec((1, d), lambda i: (0, 0))


def _norm_mod_fwd(x, g, sc, sh, name):
    s, d = x.shape
    ts = _tile(s, (512,))

    def body(x_ref, g_ref, sc_ref, sh_ref, h_ref):
        xv = x_ref[...]
        r = lax.rsqrt(jnp.mean(xv * xv, axis=-1, keepdims=True) + EPS)
        h_ref[...] = (((xv * r) * g_ref[...]) * (1.0 + sc_ref[...]) + sh_ref[...]).astype(h_ref.dtype)

    row, vec = _row_specs(s, d, ts)
    return pl.pallas_call(
        body, grid=(s // ts,), in_specs=[row, vec, vec, vec], out_specs=row,
        out_shape=jax.ShapeDtypeStruct((s, d), MXU_DTYPE), compiler_params=_params("parallel"), name=name,
    )(x, g, sc, sh)


def _norm_mod_bwd(dh, x, g, sc, dres, name):
    s, d = x.shape
    ts = _tile(s, (512,))

    def body(dh_ref, x_ref, g_ref, sc_ref, dres_ref, dx_ref, dg_ref, dsc_ref, dsh_ref, acc_ref):
        i = pl.program_id(0)

        @pl.when(i == 0)
        def _():
            acc_ref[...] = jnp.zeros_like(acc_ref)

        xv, dhv = x_ref[...], dh_ref[...]
        r = lax.rsqrt(jnp.mean(xv * xv, axis=-1, keepdims=True) + EPS)
        n = xv * r
        acc_ref[0:1, :] += jnp.sum(dhv * n, axis=0, keepdims=True)
        acc_ref[1:2, :] += jnp.sum(dhv, axis=0, keepdims=True)
        dn = dhv * ((1.0 + sc_ref[...]) * g_ref[...])
        dx_ref[...] = dres_ref[...] + r * (dn - n * jnp.mean(dn * n, axis=-1, keepdims=True))
        dg_ref[...] = (1.0 + sc_ref[...]) * acc_ref[0:1, :]
        dsc_ref[...] = g_ref[...] * acc_ref[0:1, :]
        dsh_ref[...] = acc_ref[1:2, :]

    row, vec = _row_specs(s, d, ts)
    vshape = jax.ShapeDtypeStruct((1, d), F32)
    return pl.pallas_call(
        body, grid=(s // ts,), in_specs=[row, row, vec, vec, row], out_specs=[row, vec, vec, vec],
        out_shape=[jax.ShapeDtypeStruct((s, d), F32), vshape, vshape, vshape],
        scratch_shapes=[pltpu.VMEM((SUBLANES, d), F32)], compiler_params=_params("arbitrary"), name=name,
    )(dh, x, g, sc, dres)


def _post_fwd(x, y, g, gt, name):
    s, d = x.shape
    ts = _tile(s, (512,))

    def body(x_ref, y_ref, g_ref, gt_ref, o_ref):
        yv = y_ref[...]
        r = lax.rsqrt(jnp.mean(yv * yv, axis=-1, keepdims=True) + EPS)
        o_ref[...] = x_ref[...] + gt_ref[...] * ((yv * r) * g_ref[...])

    row, vec = _row_specs(s, d, ts)
    return pl.pallas_call(
        body, grid=(s // ts,), in_specs=[row, row, vec, vec], out_specs=row,
        out_shape=jax.ShapeDtypeStruct((s, d), F32), compiler_params=_params("parallel"), name=name,
    )(x, y, g, gt)


def _post_bwd(dxn, y, g, gt, name):
    s, d = y.shape
    ts = _tile(s, (512,))

    def body(dxn_ref, y_ref, g_ref, gt_ref, dy_ref, dg_ref, dgt_ref, acc_ref):
        i = pl.program_id(0)

        @pl.when(i == 0)
        def _():
            acc_ref[...] = jnp.zeros_like(acc_ref)

        yv, dv = y_ref[...], dxn_ref[...]
        r = lax.rsqrt(jnp.mean(yv * yv, axis=-1, keepdims=True) + EPS)
        n = yv * r
        acc_ref[0:1, :] += jnp.sum(dv * n, axis=0, keepdims=True)
        dn = dv * (gt_ref[...] * g_ref[...])
        dy_ref[...] = (r * (dn - n * jnp.mean(dn * n, axis=-1, keepdims=True))).astype(dy_ref.dtype)
        dg_ref[...] = gt_ref[...] * acc_ref[0:1, :]
        dgt_ref[...] = g_ref[...] * acc_ref[0:1, :]

    row, vec = _row_specs(s, d, ts)
    vshape = jax.ShapeDtypeStruct((1, d), F32)
    return pl.pallas_call(
        body, grid=(s // ts,), in_specs=[row, row, vec, vec], out_specs=[row, vec, vec],
        out_shape=[jax.ShapeDtypeStruct((s, d), MXU_DTYPE), vshape, vshape],
        scratch_shapes=[pltpu.VMEM((SUBLANES, d), F32)], compiler_params=_params("arbitrary"), name=name,
    )(dxn, y, g, gt)


def _loss_grad(x, tgt, name):
    s, d = x.shape
    ts = _tile(s, (512,))

    def body(x_ref, t_ref, col_ref, dx_ref):
        i = pl.program_id(0)

        @pl.when(i == 0)
        def _():
            col_ref[...] = jnp.zeros_like(col_ref)

        e = x_ref[...] - t_ref[...]
        col_ref[...] += jnp.sum(e * e, axis=0, keepdims=True)
        dx_ref[...] = e * (1.0 / d)

    row, vec = _row_specs(s, d, ts)
    return pl.pallas_call(
        body, grid=(s // ts,), in_specs=[row, row], out_specs=[vec, row],
        out_shape=[jax.ShapeDtypeStruct((1, d), F32), jax.ShapeDtypeStruct((s, d), F32)],
        compiler_params=_params("arbitrary"), name=name,
    )(x, tgt)


_GELU_C = 0.7978845608028654
_GELU_A = 0.044715


def _gelu(x):
    t = jnp.tanh(_GELU_C * (x + _GELU_A * x * x * x))
    return 0.5 * x * (1.0 + t), t


def _gelu_grad(x, t):
    return 0.5 * (1.0 + t) + 0.5 * x * (1.0 - t * t) * (_GELU_C * (1.0 + 3.0 * _GELU_A * x * x))


def _sigmoid(x):
    return 1.0 / (1.0 + jnp.exp(-x))


def _log1p_pos(y):
    u = 1.0 + y
    return jnp.where(u == 1.0, y, jnp.log(u) * (y / jnp.where(u == 1.0, 1.0, u - 1.0)))


def _softplus(x):
    return jnp.maximum(x, 0.0) + _log1p_pos(jnp.exp(-jnp.abs(x)))


def _one_minus_exp(z):
    u = jnp.exp(z)
    lg = jnp.log(jnp.where(u > 0.0, u, 1.0))
    safe = (u != 1.0) & (u > 0.0)
    return jnp.where(u == 1.0, -z, jnp.where(u > 0.0, (1.0 - u) * (z / jnp.where(safe, lg, 1.0)), 1.0))


SLAB = 16


def _cat(a, b):
    return jnp.concatenate([a, b], axis=1)


def _fold8(x):
    out = x[0:SUBLANES]
    for r in range(SUBLANES, x.shape[0], SUBLANES):
        out = out + x[r:r + SUBLANES]
    return out


def _pair_specs(shape, nb, index):
    return [pl.BlockSpec(shape, lambda j, t: index(j, t) + (j,)), pl.BlockSpec(shape, lambda j, t: index(j, t) + (j + nb,))]


def _halo_row(ts, time_of):
    return lambda j, t: (jnp.maximum(time_of(t) * (ts // SUBLANES) - 1, 0),)


def _ffn_mid_fwd(p, cw, cb, name):
    s, f2 = p.shape
    ts = _tile(s, (512,))
    nb, nt = f2 // (2 * CB), s // ts

    def body(pg_ref, pv_ref, hg_ref, hv_ref, cwg_ref, cwv_ref, cbg_ref, cbv_ref, a_ref, ga_ref, gb_ref):
        t = pl.program_id(1)
        cwv, bias = _cat(cwg_ref[...], cwv_ref[...]), _cat(cbg_ref[...], cbv_ref[...])
        w0, w1, w2 = cwv[0:1], cwv[1:2], cwv[2:3]

        def slab(blk, r0):
            u = bias + w0 * blk[6:6 + SLAB] + w1 * blk[7:7 + SLAB] + w2 * blk[8:8 + SLAB]
            g, v = u[:, :CB], u[:, CB:]
            gel, th = _gelu(g)
            rows = pl.ds(r0, SLAB)
            a_ref[rows, :] = (gel * v).astype(a_ref.dtype)
            ga_ref[rows, :] = gel.astype(ga_ref.dtype)
            gb_ref[rows, :] = (v * _gelu_grad(g, th)).astype(gb_ref.dtype)

        halo = jnp.where(t > 0, _cat(hg_ref[...], hv_ref[...]), 0.0)
        slab(jnp.concatenate([halo, _cat(pg_ref[0:SLAB, :], pv_ref[0:SLAB, :])], axis=0), 0)

        def loop(i, carry):
            r0 = pl.multiple_of(i * SLAB, SLAB)
            rows = pl.ds(pl.multiple_of(r0 - SUBLANES, SUBLANES), SLAB + SUBLANES)
            slab(_cat(pg_ref[rows, :], pv_ref[rows, :]), r0)
            return carry

        lax.fori_loop(1, ts // SLAB, loop, 0, unroll=2)

    fwd = lambda t: t
    out = pl.BlockSpec((ts, CB), lambda j, t: (t, j))
    shape = jax.ShapeDtypeStruct((s, f2 // 2), MXU_DTYPE)
    return pl.pallas_call(
        body, grid=(nb, nt),
        in_specs=(_pair_specs((ts, CB), nb, lambda j, t: (t,)) + _pair_specs((SUBLANES, CB), nb, _halo_row(ts, fwd))
                  + _pair_specs((FFN_CONV, CB), nb, lambda j, t: (0,)) + _pair_specs((1, CB), nb, lambda j, t: (0,))),
        out_specs=[out, out, out], out_shape=[shape, shape, shape],
        compiler_params=_params("parallel", "arbitrary"), name=name,
    )(p, p, p, p, cw, cw, cb, cb)


def _ffn_mid_bwd(da, p, ga, gb, cw, name):
    s, f2 = p.shape
    ts = _tile(s, (512,))
    nb, nt = f2 // (2 * CB), s // ts
    n_slab = ts // SLAB

    def body(da_ref, ga_ref, gb_ref, pg_ref, pv_ref, cwg_ref, cwv_ref, dp_ref, dcw_ref, dcb_ref, next_du, acc):
        tt = pl.program_id(1)
        cwv = _cat(cwg_ref[...], cwv_ref[...])
        w0, w1, w2 = cwv[0:1], cwv[1:2], cwv[2:3]

        @pl.when(tt == 0)
        def _():
            next_du[...] = jnp.zeros_like(next_du)
            acc[...] = jnp.zeros_like(acc)

        def loop(k, carry):
            nxt, s_b, s_w0, s_w1, s_w2 = carry
            r0 = pl.multiple_of((n_slab - 1 - k) * SLAB, SLAB)
            rows = pl.ds(r0, SLAB)
            dav = da_ref[rows, :]
            du = _cat(dav * gb_ref[rows, :].astype(F32), dav * ga_ref[rows, :].astype(F32))
            p0 = _cat(pg_ref[rows, :], pv_ref[rows, :])
            ext = jnp.concatenate([du, nxt], axis=0)
            du1, du2 = ext[1:1 + SLAB], ext[2:2 + SLAB]
            dpv = (w2 * du + w1 * du1 + w0 * du2).astype(dp_ref.dtype)
            dp_ref[0, rows, :] = dpv[:, :CB]
            dp_ref[1, rows, :] = dpv[:, CB:]
            return (du[0:SUBLANES], s_b + _fold8(du), s_w0 + _fold8(du2 * p0), s_w1 + _fold8(du1 * p0),
                    s_w2 + _fold8(du * p0))

        zero = jnp.zeros((SUBLANES, 2 * CB), F32)
        nxt, *sums = lax.fori_loop(0, n_slab, loop, (next_du[...], zero, zero, zero, zero), unroll=2)
        next_du[...] = nxt
        for k, part in enumerate(sums):
            acc[k] += part

        @pl.when(tt == nt - 1)
        def _():
            for half in range(2):
                cols = slice(half * CB, (half + 1) * CB)
                dcb_ref[half] = jnp.sum(acc[0][:, cols], axis=0, keepdims=True)
                for k in range(FFN_CONV):
                    dcw_ref[half, k:k + 1, :] = jnp.sum(acc[1 + k][:, cols], axis=0, keepdims=True)

    rev = lambda t: nt - 1 - t
    tile = pl.BlockSpec((ts, CB), lambda j, t: (rev(t), j))
    return pl.pallas_call(
        body, grid=(nb, nt),
        in_specs=([tile, tile, tile] + _pair_specs((ts, CB), nb, lambda j, t: (rev(t),))
                  + _pair_specs((FFN_CONV, CB), nb, lambda j, t: (0,))),
        out_specs=[pl.BlockSpec((2, ts, CB), lambda j, t: (0, rev(t), j)),
                   pl.BlockSpec((2, FFN_CONV, CB), lambda j, t: (0, 0, j)),
                   pl.BlockSpec((2, 1, CB), lambda j, t: (0, 0, j))],
        out_shape=[jax.ShapeDtypeStruct((2, s, f2 // 2), MXU_DTYPE), jax.ShapeDtypeStruct((2, FFN_CONV, f2 // 2), F32),
                   jax.ShapeDtypeStruct((2, 1, f2 // 2), F32)],
        scratch_shapes=[pltpu.VMEM((SUBLANES, 2 * CB), F32), pltpu.VMEM((1 + FFN_CONV, SUBLANES, 2 * CB), F32)],
        compiler_params=_params("parallel", "arbitrary"), name=name,
    )(da, ga, gb, p, p, cw, cw)


def _rg_gates(xc, wa_ref, ba_ref, wx_ref, bx_ref, lam_ref):
    r = _sigmoid(_dot_nn(xc, wa_ref[0]) + ba_ref[...])
    ig = _sigmoid(_dot_nn(xc, wx_ref[0]) + bx_ref[...])
    sp = _softplus(-lam_ref[...])
    log_a = (-RG_C) * r * sp
    a = jnp.exp(log_a)
    mult = jnp.sqrt(_one_minus_exp(2.0 * log_a))
    return r, ig, sp, a, mult


def _rg_conv(scr, cw_ref, cb_ref, ts):
    views = [scr[5 + k:5 + k + ts, :] for k in range(RG_CONV)]
    xc = cb_ref[...]
    for k in range(RG_CONV):
        xc = xc + cw_ref[k:k + 1, :] * views[k]
    return xc, views


def _rg_param_specs():
    vec = pl.BlockSpec((1, CB), lambda g, t: (0, g))
    mat = pl.BlockSpec((1, CB, CB), lambda g, t: (g, 0, 0))
    return [pl.BlockSpec((RG_CONV, CB), lambda g, t: (0, g)), vec, mat, vec, mat, vec, vec]


def _scan_rows(a_scr, x_scr, out_ref, carry, ts, reverse):
    row = lax.broadcasted_iota(jnp.int32, (SUBLANES, a_scr.shape[1]), 0)
    last = SUBLANES - 1

    def group(k, c):
        r0 = pl.multiple_of((ts // SUBLANES - 1 - k if reverse else k) * SUBLANES, SUBLANES)
        rows = pl.ds(r0, SUBLANES)
        a, x = a_scr[rows, :], x_scr[rows, :]
        if reverse:
            first_a = a[0:1]
            x = jnp.where(row == last, x + c, x)
            a = jnp.where(row == last, 1.0, pltpu.roll(a, last, axis=0))
            for sh in (1, 2, 4):
                keep = row < SUBLANES - sh
                x = x + a * jnp.where(keep, pltpu.roll(x, SUBLANES - sh, axis=0), 0.0)
                a = a * jnp.where(keep, pltpu.roll(a, SUBLANES - sh, axis=0), 1.0)
            out_ref[rows, :] = x
            return first_a * x[0:1]
        for sh in (1, 2, 4):
            keep = row >= sh
            x = a * jnp.where(keep, pltpu.roll(x, sh, axis=0), 0.0) + x
            a = a * jnp.where(keep, pltpu.roll(a, sh, axis=0), 1.0)
        h = x + a * c
        out_ref[rows, :] = h
        return h[last:last + 1]

    return lax.fori_loop(0, ts // SUBLANES, group, carry, unroll=4)


def _rg_mid_fwd(pj, cw, cb, wa, ba, wx, bx, lam, name):
    s = pj.shape[0]
    nb = pj.shape[1] // (2 * CB)
    ts = _tile(s, (512,))
    nt = s // ts

    def body(gate_ref, x_ref, halo_ref, cw_ref, cb_ref, wa_ref, ba_ref, wx_ref, bx_ref, lam_ref, y_ref, hs_ref,
             scr, a_scr, u_scr, h_scr):
        t = pl.program_id(1)

        @pl.when(t == 0)
        def _():
            h_scr[...] = jnp.zeros_like(h_scr)

        scr[0:SUBLANES, :] = jnp.where(t > 0, halo_ref[...], 0.0)
        scr[SUBLANES:, :] = x_ref[...]
        xc, _ = _rg_conv(scr, cw_ref, cb_ref, ts)
        _, ig, _, a, mult = _rg_gates(xc, wa_ref, ba_ref, wx_ref, bx_ref, lam_ref)
        a_scr[...] = a
        u_scr[...] = mult * (ig * xc)
        h_scr[0:1, :] = _scan_rows(a_scr, u_scr, hs_ref, h_scr[0:1, :], ts, False)
        y_ref[...] = (_gelu(gate_ref[...])[0] * hs_ref[...]).astype(y_ref.dtype)

    blk = pl.BlockSpec((ts, CB), lambda g, t: (t, g))
    return pl.pallas_call(
        body, grid=(nb, nt),
        in_specs=_pair_specs((ts, CB), nb, lambda g, t: (t,))
        + [pl.BlockSpec((SUBLANES, CB), lambda g, t: _halo_row(ts, lambda u: u)(g, t) + (g + nb,))] + _rg_param_specs(),
        out_specs=[blk, blk],
        out_shape=[jax.ShapeDtypeStruct((s, nb * CB), MXU_DTYPE), jax.ShapeDtypeStruct((s, nb * CB), F32)],
        scratch_shapes=[pltpu.VMEM((ts + SUBLANES, CB), F32), pltpu.VMEM((ts, CB), F32), pltpu.VMEM((ts, CB), F32),
                        pltpu.VMEM((SUBLANES, CB), F32)],
        compiler_params=_params("parallel", "arbitrary"), name=name,
    )(pj, pj, pj, cw, cb, wa, ba, wx, bx, lam)


def _rg_mid_bwd(dy, pj, hs, cw, cb, wa, ba, wx, bx, lam, name):
    s = pj.shape[0]
    nb = pj.shape[1] // (2 * CB)
    ts = _tile(s, (512,))
    nt = s // ts

    def body(dy_ref, gate_ref, x_ref, halo_ref, hs_ref, hsh_ref, cw_ref, cb_ref, wa_ref, ba_ref, wx_ref, bx_ref, lam_ref,
             dpj_ref, dcw_ref, dcb_ref, dwa_ref, dba_ref, dwx_ref, dbx_ref, dlam_ref,
             scr, hscr, a_scr, d_scr, g_scr, dxscr, c_scr):
        tt = pl.program_id(1)
        t = nt - 1 - tt

        @pl.when(tt == 0)
        def _():
            c_scr[...] = jnp.zeros_like(c_scr)
            dxscr[ts:, :] = jnp.zeros((SUBLANES, CB), F32)
            for ref in (dcw_ref, dcb_ref, dwa_ref, dba_ref, dwx_ref, dbx_ref, dlam_ref):
                ref[...] = jnp.zeros_like(ref)

        scr[0:SUBLANES, :] = jnp.where(t > 0, halo_ref[...], 0.0)
        scr[SUBLANES:, :] = x_ref[...]
        hscr[0:SUBLANES, :] = jnp.where(t > 0, hsh_ref[...], 0.0)
        hscr[SUBLANES:, :] = hs_ref[...]
        xc, views = _rg_conv(scr, cw_ref, cb_ref, ts)
        r, ig, sp, a, mult = _rg_gates(xc, wa_ref, ba_ref, wx_ref, bx_ref, lam_ref)
        gate = gate_ref[...]
        gel, th = _gelu(gate)
        dyv = dy_ref[...]
        dpj_ref[0] = (dyv * hs_ref[...] * _gelu_grad(gate, th)).astype(dpj_ref.dtype)
        a_scr[...] = a
        d_scr[...] = dyv * gel
        c_scr[0:1, :] = _scan_rows(a_scr, d_scr, g_scr, c_scr[0:1, :], ts, True)
        du = g_scr[...]
        da = du * hscr[7:7 + ts, :]
        dmult = du * (ig * xc)
        dig = du * (mult * xc)
        dxc = du * (mult * ig)
        dlog_a = da * a - dmult * (a * a / mult)
        dlam_ref[...] += jnp.sum(dlog_a * r, axis=0, keepdims=True) * (RG_C * _sigmoid(-lam_ref[...]))
        dpr = dlog_a * ((-RG_C) * sp) * (r * (1.0 - r))
        dpi = dig * (ig * (1.0 - ig))
        dba_ref[...] += jnp.sum(dpr, axis=0, keepdims=True)
        dbx_ref[...] += jnp.sum(dpi, axis=0, keepdims=True)
        dwa_ref[0] += _dot_tn(xc, dpr)
        dwx_ref[0] += _dot_tn(xc, dpi)
        dxc = dxc + _dot_nt(dpr, wa_ref[0]) + _dot_nt(dpi, wx_ref[0])
        dcb_ref[...] += jnp.sum(dxc, axis=0, keepdims=True)
        for k in range(RG_CONV):
            dcw_ref[k:k + 1, :] += jnp.sum(dxc * views[k], axis=0, keepdims=True)
        dxscr[0:ts, :] = dxc
        dxp = cw_ref[3:4, :] * dxc
        for k in range(RG_CONV - 1):
            dxp = dxp + cw_ref[k:k + 1, :] * dxscr[3 - k:3 - k + ts, :]
        dpj_ref[1] = dxp.astype(dpj_ref.dtype)
        dxscr[ts:, :] = dxscr[0:SUBLANES, :]

    rev = lambda g, t: (nt - 1 - t, g)
    rev_halo = lambda g, t: (jnp.maximum((nt - 1 - t) * (ts // SUBLANES) - 1, 0), g)
    vec = pl.BlockSpec((1, CB), lambda g, t: (0, g))
    mat = pl.BlockSpec((1, CB, CB), lambda g, t: (g, 0, 0))
    d = nb * CB
    vshape = jax.ShapeDtypeStruct((1, d), F32)
    mshape = jax.ShapeDtypeStruct((nb, CB, CB), F32)
    return pl.pallas_call(
        body, grid=(nb, nt),
        in_specs=[pl.BlockSpec((ts, CB), rev)] + _pair_specs((ts, CB), nb, lambda g, t: (nt - 1 - t,))
        + [pl.BlockSpec((SUBLANES, CB), lambda g, t: (rev_halo(g, t)[0], g + nb)),
           pl.BlockSpec((ts, CB), rev), pl.BlockSpec((SUBLANES, CB), rev_halo)] + _rg_param_specs(),
        out_specs=[pl.BlockSpec((2, ts, CB), lambda g, t: (0, nt - 1 - t, g)), pl.BlockSpec((RG_CONV, CB), lambda g, t: (0, g)),
                   vec, mat, vec, mat, vec, vec],
        out_shape=[jax.ShapeDtypeStruct((2, s, d), MXU_DTYPE), jax.ShapeDtypeStruct((RG_CONV, d), F32), vshape, mshape, vshape,
                   mshape, vshape, vshape],
        scratch_shapes=[pltpu.VMEM((ts + SUBLANES, CB), F32), pltpu.VMEM((ts + SUBLANES, CB), F32), pltpu.VMEM((ts, CB), F32),
                        pltpu.VMEM((ts, CB), F32), pltpu.VMEM((ts, CB), F32), pltpu.VMEM((ts + SUBLANES, CB), F32),
                        pltpu.VMEM((SUBLANES, CB), F32)],
        compiler_params=_params("parallel", "arbitrary"), name=name,
    )(dy, pj, pj, pj, hs, hs, cw, cb, wa, ba, wx, bx, lam)


GLA_DK = 128
GLA_DV = 256
GLA_HB = 2 * GLA_DK + 2 * GLA_DV + LANES
GLA_TS = 256


def _split3(x):
    hi = x.astype(BF16)
    r1 = x - hi.astype(F32)
    mid = r1.astype(BF16)
    lo = (r1 - mid.astype(F32)).astype(BF16)
    return hi, mid, lo


def _chunk_cumsum(x, reverse):
    n = x.shape[0]
    i = lax.broadcasted_iota(jnp.int32, (n, n), 0)
    j = lax.broadcasted_iota(jnp.int32, (n, n), 1)
    same = (i // GLA_CHUNK) == (j // GLA_CHUNK)
    tri = jnp.where(same & ((j >= i) if reverse else (j <= i)), 1.0, 0.0).astype(BF16)
    out = jnp.zeros(x.shape, F32)
    for piece in _split3(x):
        out = out + lax.dot_general(tri, piece, (((1,), (0,)), ((), ())), preferred_element_type=F32)
    return out


def _gla_split(blk):
    q = blk[:, 0:GLA_DK] * (GLA_DK ** -0.5)
    k = blk[:, GLA_DK:2 * GLA_DK]
    v = blk[:, 2 * GLA_DK:2 * GLA_DK + GLA_DV]
    r = blk[:, 2 * GLA_DK + GLA_DV:2 * GLA_DK + 2 * GLA_DV]
    z = blk[:, 2 * GLA_DK + 2 * GLA_DV:]
    return q, k, v, r, z


def _gla_decays(gc):
    gref = gc[GLA_CHUNK // 2:GLA_CHUNK // 2 + 1, :]
    glast = gc[GLA_CHUNK - 1:GLA_CHUNK, :]
    return jnp.exp(gc), jnp.exp(gc - gref), jnp.exp(gref - gc), jnp.exp(glast - gc), jnp.exp(glast)


def _causal_mask():
    i = lax.broadcasted_iota(jnp.int32, (GLA_CHUNK, GLA_CHUNK), 0)
    j = lax.broadcasted_iota(jnp.int32, (GLA_CHUNK, GLA_CHUNK), 1)
    return j <= i


def _log_sigmoid(x):
    return jnp.minimum(x, 0.0) - _log1p_pos(jnp.exp(-jnp.abs(x)))


def _gla_mid_fwd(pj, wal, bal, ng, name):
    s = pj.shape[0]
    nh = pj.shape[1] // GLA_HB
    ts = _tile(s, (GLA_TS,))
    nt, nc = s // ts, ts // GLA_CHUNK

    def body(pj_ref, wal_ref, bal_ref, ng_ref, act_ref, o_ref, st_ref, s_scr):
        t = pl.program_id(0)

        @pl.when(t == 0)
        def _():
            s_scr[...] = jnp.zeros_like(s_scr)

        heads = []
        for h in range(nh):
            q, k, v, r, z = _gla_split(pj_ref[:, h * GLA_HB:(h + 1) * GLA_HB])
            g = _log_sigmoid(_dot_nn(z, wal_ref[h]) + bal_ref[h]) * (1.0 / GLA_TAU)
            heads.append((q, k, v, r, _chunk_cumsum(g, False)))
        mask = _causal_mask()
        for c in range(nc):
            sl = slice(c * GLA_CHUNK, (c + 1) * GLA_CHUNK)
            for h, (q, k, v, r, gcum) in enumerate(heads):
                eg, eq, ek, ekd, egl = _gla_decays(gcum[sl])
                st = s_scr[h]
                st_ref[c, h] = st
                attn = jnp.where(mask, _dot_nt(q[sl] * eq, k[sl] * ek), 0.0)
                o_ref[sl, h * GLA_DV:(h + 1) * GLA_DV] = _dot_nt(q[sl] * eg, st) + _dot_nn(attn, v[sl])
                s_scr[h] = st * egl + _dot_tn(v[sl], k[sl] * ekd)
        for h, (q, k, v, r, gcum) in enumerate(heads):
            cols = slice(h * GLA_DV, (h + 1) * GLA_DV)
            o = o_ref[:, cols]
            on = o * lax.rsqrt(jnp.mean(o * o, axis=-1, keepdims=True) + EPS)
            act_ref[:, cols] = ((on * ng_ref[...]) * (r * _sigmoid(r))).astype(act_ref.dtype)

    blk = pl.BlockSpec((ts, nh * GLA_DV), lambda t: (t, 0))
    whole = lambda shape: pl.BlockSpec(shape, lambda t: (0,) * len(shape))
    return pl.pallas_call(
        body, grid=(nt,),
        in_specs=[pl.BlockSpec((ts, nh * GLA_HB), lambda t: (t, 0)), whole((nh, LANES, GLA_DK)), whole((nh, 1, GLA_DK)),
                  whole((1, GLA_DV))],
        out_specs=[blk, blk, pl.BlockSpec((nc, nh, GLA_DV, GLA_DK), lambda t: (t, 0, 0, 0))],
        out_shape=[jax.ShapeDtypeStruct((s, nh * GLA_DV), MXU_DTYPE), jax.ShapeDtypeStruct((s, nh * GLA_DV), F32),
                   jax.ShapeDtypeStruct((s // GLA_CHUNK, nh, GLA_DV, GLA_DK), F32)],
        scratch_shapes=[pltpu.VMEM((nh, GLA_DV, GLA_DK), F32)],
        compiler_params=_params("arbitrary"), name=name,
    )(pj, wal, bal, ng)


def _gla_mid_bwd(dact, pj, o, st, wal, bal, ng, name):
    s = pj.shape[0]
    nh = pj.shape[1] // GLA_HB
    ts = _tile(s, (GLA_TS,))
    nt, nc = s // ts, ts // GLA_CHUNK
    o_q, o_k, o_v, o_r, o_z = 0, GLA_DK, 2 * GLA_DK, 2 * GLA_DK + GLA_DV, 2 * GLA_DK + 2 * GLA_DV

    def body(dact_ref, pj_ref, o_ref, st_ref, wal_ref, bal_ref, ng_ref, dpj_ref, dwal_ref, dbal_ref, dng_ref,
             ds_scr, dg_scr):
        tt = pl.program_id(0)

        @pl.when(tt == 0)
        def _():
            ds_scr[...] = jnp.zeros_like(ds_scr)
            dwal_ref[...] = jnp.zeros_like(dwal_ref)
            dbal_ref[...] = jnp.zeros_like(dbal_ref)
            dng_ref[...] = jnp.zeros_like(dng_ref)

        heads = []
        for h in range(nh):
            base = h * GLA_HB
            q, k, v, r, z = _gla_split(pj_ref[:, base:base + GLA_HB])
            logit = _dot_nn(z, wal_ref[h]) + bal_ref[h]
            gcum = _chunk_cumsum(_log_sigmoid(logit) * (1.0 / GLA_TAU), False)
            ov = o_ref[:, h * GLA_DV:(h + 1) * GLA_DV]
            ro = lax.rsqrt(jnp.mean(ov * ov, axis=-1, keepdims=True) + EPS)
            on = ov * ro
            sg = _sigmoid(r)
            sil = r * sg
            dav = dact_ref[:, h * GLA_DV:(h + 1) * GLA_DV]
            dpj_ref[:, base + o_r:base + o_z] = (dav * (on * ng_ref[...]) * (sg + sil * (1.0 - sg))).astype(dpj_ref.dtype)
            t1 = dav * sil
            dng_ref[...] += jnp.sum(t1 * on, axis=0, keepdims=True)
            dn = t1 * ng_ref[...]
            do = ro * (dn - on * jnp.mean(dn * on, axis=-1, keepdims=True))
            heads.append((q, k, v, z, logit, gcum, do))
        mask = _causal_mask()
        scale = GLA_DK ** -0.5
        last_row = lax.broadcasted_iota(jnp.int32, (GLA_CHUNK, GLA_DK), 0) == GLA_CHUNK - 1
        for c in reversed(range(nc)):
            sl = slice(c * GLA_CHUNK, (c + 1) * GLA_CHUNK)
            for h, (q, k, v, z, logit, gcum, do) in enumerate(heads):
                base = h * GLA_HB
                eg, eq, ek, ekd, egl = _gla_decays(gcum[sl])
                qc, kc, vc, doc = q[sl], k[sl], v[sl], do[sl]
                qg, qt, kt, kd = qc * eg, qc * eq, kc * ek, kc * ekd
                sp = st_ref[c, h]
                ds = ds_scr[h]
                attn = jnp.where(mask, _dot_nt(qt, kt), 0.0)
                dattn = jnp.where(mask, _dot_nt(doc, vc), 0.0)
                dqg = _dot_nn(doc, sp)
                dqt = _dot_nn(dattn, kt)
                dkt = _dot_tn(dattn, qt)
                dkd = _dot_nn(vc, ds)
                dpj_ref[sl, base + o_v:base + o_r] = (_dot_tn(attn, doc) + _dot_nt(kd, ds)).astype(dpj_ref.dtype)
                dpj_ref[sl, base + o_q:base + o_k] = (scale * (dqg * eg + dqt * eq)).astype(dpj_ref.dtype)
                dpj_ref[sl, base + o_k:base + o_v] = (dkt * ek + dkd * ekd).astype(dpj_ref.dtype)
                kdd = dkd * kd
                dgl = jnp.sum(kdd, axis=0, keepdims=True) + jnp.sum(ds * sp, axis=0, keepdims=True) * egl
                dg_scr[h, sl, :] = dqg * qg + dqt * qt - dkt * kt - kdd + jnp.where(last_row, dgl, 0.0)
                ds_scr[h] = ds * egl + _dot_tn(doc, qg)
        for h, (q, k, v, z, logit, gcum, do) in enumerate(heads):
            base = h * GLA_HB
            dlogit = _chunk_cumsum(dg_scr[h], True) * (1.0 / GLA_TAU) * _sigmoid(-logit)
            dpj_ref[:, base + o_z:base + GLA_HB] = _dot_nt(dlogit, wal_ref[h]).astype(dpj_ref.dtype)
            dwal_ref[h] += _dot_tn(z, dlogit)
            dbal_ref[h] += jnp.sum(dlogit, axis=0, keepdims=True)

    rev = lambda t: (nt - 1 - t, 0)
    whole = lambda shape: pl.BlockSpec(shape, lambda t: (0,) * len(shape))
    wide = pl.BlockSpec((ts, nh * GLA_DV), rev)
    return pl.pallas_call(
        body, grid=(nt,),
        in_specs=[wide, pl.BlockSpec((ts, nh * GLA_HB), rev), wide,
                  pl.BlockSpec((nc, nh, GLA_DV, GLA_DK), lambda t: (nt - 1 - t, 0, 0, 0)),
                  whole((nh, LANES, GLA_DK)), whole((nh, 1, GLA_DK)), whole((1, GLA_DV))],
        out_specs=[pl.BlockSpec((ts, nh * GLA_HB), rev), whole((nh, LANES, GLA_DK)), whole((nh, 1, GLA_DK)), whole((1, GLA_DV))],
        out_shape=[jax.ShapeDtypeStruct((s, nh * GLA_HB), MXU_DTYPE), jax.ShapeDtypeStruct((nh, LANES, GLA_DK), F32),
                   jax.ShapeDtypeStruct((nh, 1, GLA_DK), F32), jax.ShapeDtypeStruct((1, GLA_DV), F32)],
        scratch_shapes=[pltpu.VMEM((nh, GLA_DV, GLA_DK), F32), pltpu.VMEM((nh, ts, GLA_DK), F32)],
        compiler_params=_params("arbitrary"), name=name,
    )(dact, pj, o, st, wal, bal, ng)


def _adamw(w, gs, m, v, name, after=None):
    layers, rows, cols = w.shape
    gs = list(gs) if isinstance(gs, (list, tuple)) else gs
    n_g = len(gs) if isinstance(gs, list) else 1
    if rows % SUBLANES == 0:
        tr, tc = _tile(rows, (256, 128, 64, 32, 16, 8)), cols
    else:
        tr, tc = rows, _tile(cols, (256, 128))
    c1 = 1.0 / (1.0 - ADAM_B1 ** ADAM_STEP)
    c2 = 1.0 / (1.0 - ADAM_B2 ** ADAM_STEP)

    def body(*refs):
        g_refs, (w_ref, m_ref, v_ref) = refs[:n_g], refs[n_g:n_g + 3]
        go_ref, d_ref, mo_ref, vo_ref = refs[-4:]
        gv = g_refs[0][...]
        for l in range(1, n_g):
            gv = jnp.where(pl.program_id(0) == l, g_refs[l][...], gv)
        m2 = ADAM_B1 * m_ref[...] + (1.0 - ADAM_B1) * gv
        v2 = ADAM_B2 * v_ref[...] + (1.0 - ADAM_B2) * (gv * gv)
        d_ref[...] = (-ADAM_LR) * ((m2 * c1) / (jnp.sqrt(v2 * c2) + ADAM_EPS) + ADAM_WD * w_ref[...])
        go_ref[...] = gv
        mo_ref[...] = m2
        vo_ref[...] = v2

    spec = pl.BlockSpec((None, tr, tc), lambda l, i, j: (l, i, j))
    g_specs = [pl.BlockSpec((tr, tc), lambda l, i, j: (i, j))] * n_g if isinstance(gs, list) else [spec]
    extra = [] if after is None else [(after, _ANY)]
    shape = jax.ShapeDtypeStruct((layers, rows, cols), F32)
    return pl.pallas_call(
        body, grid=(layers, rows // tr, cols // tc), in_specs=g_specs + [spec] * 3 + [sp for _, sp in extra],
        out_specs=[spec] * 4, out_shape=[shape] * 4, compiler_params=_params("parallel", "parallel", "parallel"), name=name,
    )(*(gs if isinstance(gs, list) else [gs]), w, m, v, *[a for a, _ in extra])


def _gla_head_cols(w):
    d = w.shape[0]
    qk, dv = GLA_HEADS * GLA_DK, GLA_HEADS * GLA_DV
    q, k, v, r, z = jnp.split(w, [qk, 2 * qk, 2 * qk + dv, 2 * qk + 2 * dv], axis=1)
    zp = jnp.pad(z, ((0, 0), (0, LANES - GLA_RANK)))
    parts = [q.reshape(d, GLA_HEADS, GLA_DK), k.reshape(d, GLA_HEADS, GLA_DK), v.reshape(d, GLA_HEADS, GLA_DV),
             r.reshape(d, GLA_HEADS, GLA_DV), jnp.broadcast_to(zp[:, None, :], (d, GLA_HEADS, LANES))]
    return jnp.concatenate(parts, axis=2).reshape(d, GLA_HEADS * GLA_HB)


def _gla_unhead_cols(w):
    d = w.shape[0]
    w = w.reshape(d, GLA_HEADS, GLA_HB)
    o = 2 * GLA_DK + 2 * GLA_DV
    parts = [w[:, :, 0:GLA_DK].reshape(d, -1), w[:, :, GLA_DK:2 * GLA_DK].reshape(d, -1),
             w[:, :, 2 * GLA_DK:2 * GLA_DK + GLA_DV].reshape(d, -1), w[:, :, 2 * GLA_DK + GLA_DV:o].reshape(d, -1),
             jnp.sum(w[:, :, o:o + GLA_RANK], axis=1)]
    return jnp.concatenate(parts, axis=1)


def _gla_alpha_heads(w_alpha, b_alpha):
    wal = jnp.swapaxes(w_alpha.reshape(GLA_RANK, GLA_HEADS, GLA_DK), 0, 1)
    return jnp.pad(wal, ((0, 0), (0, LANES - GLA_RANK), (0, 0))), b_alpha.reshape(GLA_HEADS, 1, GLA_DK)


def _gla_layouts(w):
    w = dict(w)
    w["gla_wal"], w["gla_bal"] = _gla_alpha_heads(w["gla_w_alpha"], w["gla_b_alpha"])
    w["gla_w_in"] = _gla_head_cols(w["gla_w_in"])
    return w


def _col_slots(w):
    r, c = w.shape
    return jnp.moveaxis(w.reshape(r, N_CHIP, c // N_CHIP), 1, 0)


def _from_col_slots(w):
    n, r, c = w.shape
    return jnp.moveaxis(w, 0, 1).reshape(r, n * c)


def _block_rows_to_slots(w):
    g, r4, cc = w.shape
    return jnp.swapaxes(w.reshape(g, N_CHIP, r4 // N_CHIP, cc), 0, 1).reshape(N_CHIP, g * (r4 // N_CHIP), cc)


def _slots_to_block_rows(w, g):
    n, gr, cc = w.shape
    return jnp.swapaxes(w.reshape(n, g, gr // g, cc), 0, 1).reshape(g, n * (gr // g), cc)


def _local_step(x, tgt, mod, w, fetch=None, done=None, later=None):
    depth = mod.shape[0]
    row = lambda v: v.reshape(1, -1)
    w = dict(w)
    w["ffn_w_up"], w["ffn_w_down"] = dict(enumerate(w["ffn_w_up"])), dict(enumerate(w["ffn_w_down"]))

    def arrive(stage, after):
        if fetch is not None:
            for k, v in fetch(stage, after).items():
                if isinstance(v, dict):
                    w[k].update(v)
                else:
                    w[k] = v

    saved = []
    for i in range(depth):
        if i == 1:
            arrive("l1", x)
        sh_m, sc_m, gt_m, sh_f, sc_f, gt_f = (mod[i, j:j + 1] for j in range(6))
        g0, g1, g2, g3 = (w["norm_g"][i, j:j + 1] for j in range(4))
        tag = f"_l{i}"
        h = _norm_mod_fwd(x, g0, sc_m, sh_m, "norm_mix" + tag)
        if i % 2 == 0:
            pj = _mm(h, w["rg_w_in"], w_slots=N_CHIP, name="rg_in" + tag)
            act, aux = _rg_mid_fwd(pj, w["rg_conv_w"], row(w["rg_conv_b"]), w["rg_wa"], row(w["rg_ba"]), w["rg_wx"],
                                   row(w["rg_bx"]), row(w["rg_lambda"]), "rg_mid" + tag)
            y = _mm(act, w["rg_w_out"], name="rg_out" + tag)
        else:
            pj = _mm(h, w["gla_w_in"], name="gla_in" + tag)
            act, *aux = _gla_mid_fwd(pj, w["gla_wal"], w["gla_bal"], row(w["gla_norm_g"]), "gla_mid" + tag)
            y = _mm(act, w["gla_w_out"], name="gla_out" + tag)
        x1 = _post_fwd(x, y, g1, gt_m, "post_mix" + tag)
        if i == 0:
            arrive("ffn0", x1)
        h2 = _norm_mod_fwd(x1, g2, sc_f, sh_f, "norm_ffn" + tag)
        p = _mm(h2, w["ffn_w_up"][i], w_slots=N_CHIP, name="ffn_up" + tag)
        a, ga, gb = _ffn_mid_fwd(p, w["ffn_conv_w"][i], w["ffn_conv_b"][i:i + 1], "ffn_mid" + tag)
        y2 = _mm(a, w["ffn_w_down"][i], name="ffn_down" + tag)
        x2 = _post_fwd(x1, y2, g3, gt_f, "post_ffn" + tag)
        saved.append((x, h, pj, act, aux, y, x1, h2, p, (a, ga, gb), y2))
        x = x2

    cols, dx = _loss_grad(x, tgt, "loss")

    stacked = ("norm_g", "ffn_conv_w", "ffn_conv_b", "mod")
    gr = {k: [None] * depth for k in stacked + ("ffn_w_up", "ffn_w_down")}
    told = lambda stage: done(stage, gr) if done is not None else 0.0
    told_later = lambda stage, after: later(stage, after) if later is not None else 0.0
    for i in reversed(range(depth)):
        x0, h, pj, act, aux, y, x1, h2, p, (a, ga, gb), y2 = saved[i]
        sh_m, sc_m, gt_m, sh_f, sc_f, gt_f = (mod[i, j:j + 1] for j in range(6))
        g0, g1, g2, g3 = (w["norm_g"][i, j:j + 1] for j in range(4))
        tag = f"_l{i}"
        dy2, d_g3, d_gt_f = _post_bwd(dx, y2, g3, gt_f, "post_ffn_b" + tag)
        da = _mm(dy2, w["ffn_w_down"][i], tb=True, name="ffn_down_dx" + tag)
        gr["ffn_w_down"][i] = _mm(a, dy2, ta=True, name="ffn_down_dw" + tag)
        conv_w = w["ffn_conv_w"][i] + (told_later("l1", da) if i == 0 else 0.0)
        dp, dcw, dcb = _ffn_mid_bwd(da, p, ga, gb, conv_w, "ffn_mid_b" + tag)
        gr["ffn_conv_w"][i], gr["ffn_conv_b"][i] = _cat(dcw[0], dcw[1]), _cat(dcb[0], dcb[1])[0]
        dh2 = _mm(dp, w["ffn_w_up"][i], tb=True, a_parts=2, w_slots=N_CHIP, name="ffn_up_dx" + tag)
        gr["ffn_w_up"][i] = _mm(h2, dp, ta=True, b_parts=2, out_slots=N_CHIP, name="ffn_up_dw" + tag)
        dx1, d_g2, d_sc_f, d_sh_f = _norm_mod_bwd(dh2, x1, g2, sc_f, dx, "norm_ffn_b" + tag)
        if i == 0:
            gt_m = gt_m + told("ffn0")
        dy, d_g1, d_gt_m = _post_bwd(dx1, y, g1, gt_m, "post_mix_b" + tag)
        if i % 2 == 0:
            dact = _mm(dy, w["rg_w_out"], tb=True, name="rg_out_dx" + tag)
            gr["rg_w_out"] = _mm(act, dy, ta=True, name="rg_out_dw" + tag)
            lam = row(w["rg_lambda"]) + told_later("ffn0", gr["rg_w_out"])
            dpj, gr["rg_conv_w"], d_cb, gr["rg_wa"], d_ba, gr["rg_wx"], d_bx, d_lam = _rg_mid_bwd(
                dact, pj, aux, w["rg_conv_w"], row(w["rg_conv_b"]), w["rg_wa"], row(w["rg_ba"]), w["rg_wx"],
                row(w["rg_bx"]), lam, "rg_mid_b" + tag)
            gr["rg_conv_b"], gr["rg_ba"], gr["rg_bx"], gr["rg_lambda"] = d_cb[0], d_ba[0], d_bx[0], d_lam[0]
            dh = _mm(dpj, w["rg_w_in"], tb=True, a_parts=2, w_slots=N_CHIP, name="rg_in_dx" + tag)
            gr["rg_w_in"] = _mm(h, dpj, ta=True, b_parts=2, out_slots=N_CHIP, name="rg_in_dw" + tag)
        else:
            dact = _mm(dy, w["gla_w_out"], tb=True, name="gla_out_dx" + tag)
            gr["gla_w_out"] = _mm(act, dy, ta=True, name="gla_out_dw" + tag)
            dpj, d_wal, d_bal, d_ng = _gla_mid_bwd(dact, pj, aux[0], aux[1], w["gla_wal"], w["gla_bal"],
                                                   row(w["gla_norm_g"]), "gla_mid_b" + tag)
            gr["gla_w_alpha"] = jnp.swapaxes(d_wal[:, :GLA_RANK, :], 0, 1).reshape(GLA_RANK, GLA_HEADS * GLA_DK)
            gr["gla_b_alpha"], gr["gla_norm_g"] = d_bal.reshape(-1), d_ng[0]
            dh = _mm(dpj, w["gla_w_in"], tb=True, name="gla_in_dx" + tag)
            gr["gla_w_in"] = _gla_unhead_cols(_mm(h, dpj, ta=True, name="gla_in_dw" + tag))
            mod = mod.at[0].add(told("l1"))
        dx, d_g0, d_sc_m, d_sh_m = _norm_mod_bwd(dh, x0, g0, sc_m, dx1, "norm_mix_b" + tag)
        gr["norm_g"][i] = jnp.concatenate([d_g0, d_g1, d_g2, d_g3], axis=0)
        gr["mod"][i] = jnp.concatenate([d_sh_m, d_sc_m, d_gt_m, d_sh_f, d_sc_f, d_gt_f], axis=0)
    for k in stacked:
        gr[k] = jnp.stack(gr[k])
    return cols, dx, gr


ADA_ROWS = 16


def _ada_fwd(c16, ada_w, ada_b, name):
    depth, d, n = ada_w.shape
    tn = _tile(n, (512, 256, 128))

    def body(c_ref, w_ref, b_ref, o_ref):
        cv = c_ref[...]
        o_ref[0] = _dot_nn(cv * _sigmoid(cv), w_ref[0]) + b_ref[0]

    return pl.pallas_call(
        body, grid=(depth, n // tn),
        in_specs=[pl.BlockSpec((ADA_ROWS, d), lambda l, j: (0, 0)), pl.BlockSpec((1, d, tn), lambda l, j: (l, 0, j)),
                  pl.BlockSpec((1, 1, tn), lambda l, j: (l, 0, j))],
        out_specs=pl.BlockSpec((1, ADA_ROWS, tn), lambda l, j: (l, 0, j)),
        out_shape=jax.ShapeDtypeStruct((depth, ADA_ROWS, n), F32),
        compiler_params=_params("parallel", "parallel"), name=name,
    )(c16, ada_w, ada_b)


def _ada_bwd(c16, dmod16, name):
    depth, _, n = dmod16.shape
    d = c16.shape[1]
    tn = _tile(n, (512, 256, 128))

    def body(c_ref, dm_ref, o_ref):
        cv = c_ref[...]
        o_ref[0] = _dot_tn(cv * _sigmoid(cv), dm_ref[0])

    return pl.pallas_call(
        body, grid=(depth, n // tn),
        in_specs=[pl.BlockSpec((ADA_ROWS, d), lambda l, j: (0, 0)), pl.BlockSpec((1, ADA_ROWS, tn), lambda l, j: (l, 0, j))],
        out_specs=pl.BlockSpec((1, d, tn), lambda l, j: (l, 0, j)),
        out_shape=jax.ShapeDtypeStruct((depth, d, n), F32),
        compiler_params=_params("parallel", "parallel"), name=name,
    )(c16, dmod16)


PACK_COLS = 1024
_ANY = pl.BlockSpec(memory_space=pl.ANY)
_VMEM = pl.BlockSpec(memory_space=pltpu.VMEM)


def _place():
    return lax.axis_index("x"), lax.axis_index("y"), lax.axis_index("c")


def _other_chips(x, y):
    return [(1 - x, y), (x, 1 - y), (1 - x, 1 - y)]


def _rcopy(src, dst, send_sems, recv_sems, k, peer):
    return pltpu.make_async_remote_copy(src_ref=src, dst_ref=dst, send_sem=send_sems.at[k], recv_sem=recv_sems.at[k],
                                        device_id=peer, device_id_type=MESH)


def _all_gather_8(v, name):
    r, cc = v.shape

    def body(v_ref, out_ref, send_sems, recv_sems, local_sem):
        x, y, c = _place()
        me = 4 * x + 2 * y + c
        mine = pltpu.make_async_copy(v_ref, out_ref.at[me], local_sem)
        mine.start()
        peers = []
        for k in range(1, N_DEV):
            px = 1 - x if k & 4 else x
            py = 1 - y if k & 2 else y
            pc = 1 - c if k & 1 else c
            peers.append((px, py, pc))
        sends = [_rcopy(v_ref, out_ref.at[me], send_sems, recv_sems, k, p) for k, p in enumerate(peers)]
        for cp in sends:
            cp.start()
        for k, (px, py, pc) in enumerate(peers):
            _rcopy(v_ref, out_ref.at[4 * px + 2 * py + pc], send_sems, recv_sems, k, (px, py, pc)).wait_recv()
        for cp in sends:
            cp.wait_send()
        mine.wait()

    return pl.pallas_call(
        body, in_specs=[_VMEM], out_specs=_VMEM, out_shape=jax.ShapeDtypeStruct((N_DEV, r, cc), v.dtype),
        scratch_shapes=[pltpu.SemaphoreType.DMA((N_DEV - 1,)), pltpu.SemaphoreType.DMA((N_DEV - 1,)), pltpu.SemaphoreType.DMA],
        compiler_params=pltpu.CompilerParams(vmem_limit_bytes=VMEM_LIMIT), name=name,
    )(v)


def _gather_chips(shards, name):
    n = len(shards)
    per = 2 * (N_CHIP - 1)

    def body(*refs):
        ins, outs, (send_sems, recv_sems) = refs[:n], refs[n:2 * n], refs[2 * n:]
        x, y, c = _place()
        chip = 2 * x + y
        chips = _other_chips(x, y)
        rows = [(pl.ds(c * (r.shape[0] // 2), r.shape[0] // 2), pl.ds((1 - c) * (r.shape[0] // 2), r.shape[0] // 2)) for r in ins]
        first = [_rcopy(ins[i].at[rows[i][0]], outs[i].at[chip, rows[i][0]], send_sems, recv_sems, per * i + j, (px, py, c))
                 for i in range(n) for j, (px, py) in enumerate(chips)]
        for cp in first:
            cp.start()
        passed = []
        for i in range(n):
            for j, (px, py) in enumerate(chips):
                landed = outs[i].at[2 * px + py, rows[i][0]]
                _rcopy(ins[i].at[rows[i][0]], landed, send_sems, recv_sems, per * i + j, (px, py, c)).wait_recv()
                fw = _rcopy(landed, landed, send_sems, recv_sems, per * i + N_CHIP - 1 + j, (x, y, 1 - c))
                fw.start()
                passed.append(fw)
        for i in range(n):
            for j, (px, py) in enumerate(chips):
                landed = outs[i].at[2 * px + py, rows[i][1]]
                _rcopy(landed, landed, send_sems, recv_sems, per * i + N_CHIP - 1 + j, (x, y, 1 - c)).wait_recv()
        for cp in first + passed:
            cp.wait_send()

    return pl.pallas_call(
        body, in_specs=[_ANY] * n, out_specs=[_ANY] * n,
        out_shape=[jax.ShapeDtypeStruct((N_CHIP,) + sh.shape, sh.dtype) for sh in shards],
        scratch_shapes=[pltpu.SemaphoreType.DMA((per * n,)), pltpu.SemaphoreType.DMA((per * n,))], name=name,
    )(*shards)


def _pair_exchange(gs, name):
    n = len(gs)

    def body(*refs):
        ins, outs, (send_sems, recv_sems) = refs[:n], refs[n:2 * n], refs[2 * n:]
        x, y, c = _place()
        copies = []
        for i in range(n):
            half = ins[i].shape[1] // 2
            copies.append(_rcopy(ins[i].at[:, pl.ds((1 - c) * half, half)], outs[i], send_sems, recv_sems, i, (x, y, 1 - c)))
        for cp in copies:
            cp.start()
        for cp in copies:
            cp.wait()

    return pl.pallas_call(
        body, in_specs=[_ANY] * n, out_specs=[_ANY] * n,
        out_shape=[jax.ShapeDtypeStruct((g.shape[0], g.shape[1] // 2, g.shape[2]), g.dtype) for g in gs],
        scratch_shapes=[pltpu.SemaphoreType.DMA((n,)), pltpu.SemaphoreType.DMA((n,))], name=name,
    )(*gs)


_ROW_TILES = (640, 512, 352, 256, 128, 64, 32, 16)


def _pair_sum(g, other, c_idx, name):
    n, half, cc = other.shape
    tr = _tile(half, _ROW_TILES)

    def body(c_ref, g_ref, o_ref, out_ref):
        out_ref[...] = (g_ref[...] + o_ref[...]).astype(out_ref.dtype)

    return pl.pallas_call(
        body,
        grid_spec=pltpu.PrefetchScalarGridSpec(
            num_scalar_prefetch=1, grid=(n, half // tr),
            in_specs=[pl.BlockSpec((None, None, tr, cc), lambda k, i, c_ref: (k, c_ref[0], i, 0)),
                      pl.BlockSpec((None, tr, cc), lambda k, i, c_ref: (k, i, 0))],
            out_specs=pl.BlockSpec((None, tr, cc), lambda k, i, c_ref: (k, i, 0))),
        out_shape=jax.ShapeDtypeStruct((n, half, cc), BF16),
        compiler_params=_params("parallel", "parallel"), name=name,
    )(c_idx, g.reshape(n, 2, half, cc), other)


def _chip_exchange(ps, name):
    n = len(ps)
    per = N_CHIP - 1

    def body(*refs):
        ins, outs, (send_sems, recv_sems) = refs[:n], refs[n:2 * n], refs[2 * n:]
        x, y, c = _place()
        chip = 2 * x + y
        chips = _other_chips(x, y)
        sends = [_rcopy(ins[i].at[2 * px + py], outs[i].at[chip], send_sems, recv_sems, per * i + j, (px, py, c))
                 for i in range(n) for j, (px, py) in enumerate(chips)]
        for cp in sends:
            cp.start()
        for i in range(n):
            for j, (px, py) in enumerate(chips):
                _rcopy(ins[i].at[chip], outs[i].at[2 * px + py], send_sems, recv_sems, per * i + j, (px, py, c)).wait_recv()
        for cp in sends:
            cp.wait_send()

    return pl.pallas_call(
        body, in_specs=[_ANY] * n, out_specs=[_ANY] * n, out_shape=[jax.ShapeDtypeStruct(p.shape, p.dtype) for p in ps],
        scratch_shapes=[pltpu.SemaphoreType.DMA((per * n,)), pltpu.SemaphoreType.DMA((per * n,))], name=name,
    )(*ps)


_HBM = pl.BlockSpec(memory_space=pltpu.HBM)
_SEM = pl.BlockSpec(memory_space=pltpu.SEMAPHORE)
_DATAFLOW = pltpu.SideEffectType.DATAFLOW_SIDE_EFFECTING


def _split_copies(srcs, lands, send_sems, recv_sems, mode, arriving):
    x, y, c = _place()
    chip = 2 * x + y
    out = []
    for i, (src, land) in enumerate(zip(srcs, lands)):
        if mode == "pair":
            half = src.shape[1] // 2
            out.append(_rcopy(src.at[:, pl.ds((1 - c) * half, half)], land, send_sems, recv_sems, i, (x, y, 1 - c)))
            continue
        for j, (px, py) in enumerate(_other_chips(x, y)):
            there = 2 * px + py
            part = src.at[there] if mode == "slots" else src
            out.append(_rcopy(part, land.at[there if arriving else chip], send_sems, recv_sems, (N_CHIP - 1) * i + j, (px, py, c)))
    return out


def _land_shape(src, mode):
    if mode == "pair":
        return (src.shape[0], src.shape[1] // 2, src.shape[2])
    return (N_CHIP,) + (src.shape[1:] if mode == "slots" else src.shape)


def _send_start(srcs, mode, name):
    n = len(srcs)
    n_sem = n if mode == "pair" else (N_CHIP - 1) * n
    lands = [lax.empty(_land_shape(s, mode), s.dtype) for s in srcs]

    def body(*refs):
        ins, zones, (send_sems, recv_sems) = refs[:n], refs[n:2 * n], refs[2 * n:2 * n + 2]
        for cp in _split_copies(ins, zones, send_sems, recv_sems, mode, False):
            cp.start()
        refs[-1][...] = jnp.zeros_like(refs[-1])

    hbm = lambda a: pltpu.HBM(a.shape, a.dtype)
    outs = pl.pallas_call(
        body, name=name, in_specs=[_HBM] * (2 * n),
        out_shape=(pltpu.SemaphoreType.DMA((n_sem,)), pltpu.SemaphoreType.DMA((n_sem,)), *[hbm(a) for a in srcs],
                   *[hbm(a) for a in lands], jax.ShapeDtypeStruct((SUBLANES, LANES), F32)),
        out_specs=(_SEM, _SEM, *[_HBM] * (2 * n), _VMEM), input_output_aliases={i: 2 + i for i in range(2 * n)},
        compiler_params=pltpu.CompilerParams(has_side_effects=_DATAFLOW),
    )(*[pltpu.with_memory_space_constraint(a, pltpu.HBM) for a in list(srcs) + lands])
    return (outs[0], outs[1], list(outs[2:2 + n]), list(outs[2 + n:2 + 2 * n])), outs[-1]


def _send_wait(state, after, mode, name):
    send_sems, recv_sems, srcs, lands = state
    n = len(srcs)

    def body(*refs):
        ins, zones, (send_s, recv_s) = refs[:n], refs[n:2 * n], refs[2 * n:2 * n + 2]
        for cp in _split_copies(ins, zones, send_s, recv_s, mode, True):
            cp.wait_send()
            cp.wait_recv()

    hbm = lambda a: pltpu.HBM(a.shape, a.dtype)
    outs = pl.pallas_call(
        body, name=name, in_specs=[_HBM] * (2 * n) + [_SEM, _SEM, _ANY],
        out_shape=tuple(hbm(a) for a in srcs + lands), out_specs=tuple([_HBM] * (2 * n)),
        input_output_aliases={i: i for i in range(2 * n)},
        compiler_params=pltpu.CompilerParams(has_side_effects=_DATAFLOW),
    )(*srcs, *lands, send_sems, recv_sems, after)
    return list(outs[n:])


def _sum_lead(v, name):
    n, r, cc = v.shape
    tr = _tile(r, _ROW_TILES + (8,))

    def body(v_ref, o_ref):
        acc = v_ref[0].astype(F32)
        for k in range(1, n):
            acc = acc + v_ref[k].astype(F32)
        o_ref[...] = acc

    return pl.pallas_call(
        body, grid=(r // tr,), in_specs=[pl.BlockSpec((n, tr, cc), lambda i: (0, i, 0))],
        out_specs=pl.BlockSpec((tr, cc), lambda i: (i, 0)), out_shape=jax.ShapeDtypeStruct((r, cc), F32),
        compiler_params=_params("parallel"), name=name,
    )(v)


def _chip_sum(arrived, mine, chip_idx, name):
    n, r, cc = arrived.shape
    tr = _tile(r, _ROW_TILES)

    def body(chip_ref, a_ref, m_ref, o_ref):
        acc = jnp.zeros((tr, cc), F32)
        for k in range(n):
            acc = acc + jnp.where(chip_ref[0] == k, m_ref[...], a_ref[k]).astype(F32)
        o_ref[...] = acc

    return pl.pallas_call(
        body,
        grid_spec=pltpu.PrefetchScalarGridSpec(
            num_scalar_prefetch=1, grid=(r // tr,),
            in_specs=[pl.BlockSpec((n, tr, cc), lambda i, chip_ref: (0, i, 0)),
                      pl.BlockSpec((None, tr, cc), lambda i, chip_ref: (chip_ref[0], i, 0))],
            out_specs=pl.BlockSpec((tr, cc), lambda i, chip_ref: (i, 0))),
        out_shape=jax.ShapeDtypeStruct((r, cc), F32), compiler_params=_params("parallel"), name=name,
    )(chip_idx, arrived, mine)


def _pair_share(reds, name):
    n = len(reds)

    def body(*refs):
        ins, outs, (send_sems, recv_sems) = refs[:n], refs[n:2 * n], refs[2 * n:]
        x, y, c = _place()
        copies = [_rcopy(ins[i], outs[i].at[c], send_sems, recv_sems, i, (x, y, 1 - c)) for i in range(n)]
        for cp in copies:
            cp.start()
        for i in range(n):
            _rcopy(ins[i], outs[i].at[1 - c], send_sems, recv_sems, i, (x, y, 1 - c)).wait_recv()
        for cp in copies:
            cp.wait_send()

    return pl.pallas_call(
        body, in_specs=[_ANY] * n, out_specs=[_ANY] * n, out_shape=[jax.ShapeDtypeStruct((2,) + r.shape, r.dtype) for r in reds],
        scratch_shapes=[pltpu.SemaphoreType.DMA((n,)), pltpu.SemaphoreType.DMA((n,))], name=name,
    )(*reds)


def _pack(arrs, rows_multiple, dtype):
    flat = jnp.concatenate([a.reshape(-1).astype(dtype) for a in arrs])
    unit = rows_multiple * PACK_COLS
    total = -(-flat.shape[0] // unit) * unit
    return jnp.pad(flat, (0, total - flat.shape[0])).reshape(-1, PACK_COLS)


def _unpack(buf, shapes):
    lead = buf.shape[:-2]
    flat = buf.reshape(*lead, -1)
    out, off = [], 0
    for shp in shapes:
        n = 1
        for s in shp:
            n *= s
        out.append(flat[..., off:off + n].reshape(*lead, *shp))
        off += n
    return out


def _join_shards(parts, axis):
    moved = jnp.moveaxis(parts, 0, axis)
    shp = list(moved.shape)
    shp[axis:axis + 2] = [shp[axis] * shp[axis + 1]]
    return moved.reshape(shp)


def _my_shard(full, axis, chip):
    n = full.shape[axis] // N_CHIP
    return lax.dynamic_slice_in_dim(full, chip * n, n, axis)


SMALL = {"norm_g": 2, "ffn_conv_w": 2, "rg_conv_w": 2, "gla_w_alpha": 2, "gla_b_alpha": 1, "gla_norm_g": 1,
         "ada_b": None, "ffn_conv_b": None, "rg_conv_b": None, "rg_ba": None, "rg_bx": None, "rg_lambda": None}
BIG = {"rg_w_in": True, "rg_wa": False, "rg_wx": False, "rg_w_out": False, "ffn_w_up": True, "ffn_w_down": False,
       "gla_w_in": True, "gla_w_out": False}
WEIGHTS = ["ada_w", "ada_b", "norm_g", "ffn_w_up", "ffn_conv_w", "ffn_conv_b", "ffn_w_down", "rg_w_in", "rg_conv_w", "rg_conv_b",
           "rg_wa", "rg_ba", "rg_wx", "rg_bx", "rg_lambda", "rg_w_out", "gla_w_in", "gla_w_alpha", "gla_b_alpha", "gla_norm_g",
           "gla_w_out"]


def kernel(x, c, ada_w, ada_b, norm_g, ffn_w_up, ffn_conv_w, ffn_conv_b, ffn_w_down, rg_w_in, rg_conv_w, rg_conv_b, rg_wa, rg_ba, rg_wx, rg_bx, rg_lambda, rg_w_out, gla_w_in, gla_w_alpha, gla_b_alpha, gla_norm_g, gla_w_out, loss_target, m_ada_w, m_ada_b, m_norm_g, m_ffn_w_up, m_ffn_conv_w, m_ffn_conv_b, m_ffn_w_down, m_rg_w_in, m_rg_conv_w, m_rg_conv_b, m_rg_wa, m_rg_ba, m_rg_wx, m_rg_bx, m_rg_lambda, m_rg_w_out, m_gla_w_in, m_gla_w_alpha, m_gla_b_alpha, m_gla_norm_g, m_gla_w_out, v_ada_w, v_ada_b, v_norm_g, v_ffn_w_up, v_ffn_conv_w, v_ffn_conv_b, v_ffn_w_down, v_rg_w_in, v_rg_conv_w, v_rg_conv_b, v_rg_wa, v_rg_ba, v_rg_wx, v_rg_bx, v_rg_lambda, v_rg_w_out, v_gla_w_in, v_gla_w_alpha, v_gla_b_alpha, v_gla_norm_g, v_gla_w_out):
    wts = dict(ada_w=ada_w, ada_b=ada_b, norm_g=norm_g, ffn_w_up=ffn_w_up, ffn_conv_w=ffn_conv_w, ffn_conv_b=ffn_conv_b,
               ffn_w_down=ffn_w_down, rg_w_in=rg_w_in, rg_conv_w=rg_conv_w, rg_conv_b=rg_conv_b, rg_wa=rg_wa, rg_ba=rg_ba,
               rg_wx=rg_wx, rg_bx=rg_bx, rg_lambda=rg_lambda, rg_w_out=rg_w_out, gla_w_in=gla_w_in, gla_w_alpha=gla_w_alpha,
               gla_b_alpha=gla_b_alpha, gla_norm_g=gla_norm_g, gla_w_out=gla_w_out)
    mom1 = dict(ada_w=m_ada_w, ada_b=m_ada_b, norm_g=m_norm_g, ffn_w_up=m_ffn_w_up, ffn_conv_w=m_ffn_conv_w,
                ffn_conv_b=m_ffn_conv_b, ffn_w_down=m_ffn_w_down, rg_w_in=m_rg_w_in, rg_conv_w=m_rg_conv_w,
                rg_conv_b=m_rg_conv_b, rg_wa=m_rg_wa, rg_ba=m_rg_ba, rg_wx=m_rg_wx, rg_bx=m_rg_bx, rg_lambda=m_rg_lambda,
                rg_w_out=m_rg_w_out, gla_w_in=m_gla_w_in, gla_w_alpha=m_gla_w_alpha, gla_b_alpha=m_gla_b_alpha,
                gla_norm_g=m_gla_norm_g, gla_w_out=m_gla_w_out)
    mom2 = dict(ada_w=v_ada_w, ada_b=v_ada_b, norm_g=v_norm_g, ffn_w_up=v_ffn_w_up, ffn_conv_w=v_ffn_conv_w,
                ffn_conv_b=v_ffn_conv_b, ffn_w_down=v_ffn_w_down, rg_w_in=v_rg_w_in, rg_conv_w=v_rg_conv_w,
                rg_conv_b=v_rg_conv_b, rg_wa=v_rg_wa, rg_ba=v_rg_ba, rg_wx=v_rg_wx, rg_bx=v_rg_bx, rg_lambda=v_rg_lambda,
                rg_w_out=v_rg_w_out, gla_w_in=v_gla_w_in, gla_w_alpha=v_gla_w_alpha, gla_b_alpha=v_gla_b_alpha,
                gla_norm_g=v_gla_norm_g, gla_w_out=v_gla_w_out)
    xi, yi, ci = _place()
    chip, me = 2 * xi + yi, 4 * xi + 2 * yi + ci
    d = x.shape[-1]
    depth = ada_w.shape[0]
    n_ada = ada_w.shape[-1]
    sharded_small = [k for k, ax in SMALL.items() if ax is not None]

    sm = _all_gather_8(_pack([c] + [wts[k] for k in sharded_small], SUBLANES, F32), "gather_small")
    c_all = sm[:, 0, :]
    parts = _unpack(sm[0::2], [c.shape] + [wts[k].shape for k in sharded_small])[1:]
    full = {k: _join_shards(p, SMALL[k]) for k, p in zip(sharded_small, parts)}
    for k, ax in SMALL.items():
        if ax is None:
            full[k] = wts[k]

    c16 = jnp.pad(c_all, ((0, ADA_ROWS - N_DEV), (0, 0)))
    ada_b_mine = lax.dynamic_slice_in_dim(ada_b, chip * n_ada, n_ada, 1)[:, None, :]
    mod_cols = _ada_fwd(c16, ada_w, ada_b_mine, "ada_fwd")
    mod_all = _all_gather_8(mod_cols.reshape(-1, PACK_COLS), "gather_mod")[0::2].reshape(N_CHIP, depth, ADA_ROWS, n_ada)
    mod = jnp.swapaxes(lax.dynamic_index_in_dim(mod_all, me, 2, keepdims=False), 0, 1).reshape(depth, 6, d)

    items = [(k, l) for k in BIG for l in range(wts[k].shape[0])]
    stage_of = lambda k, l: "rg" if k.startswith("rg_") else ("ffn0" if (k.startswith("ffn_") and l == 0) else "l1")
    staged = {st: [it for it in items if stage_of(*it) == st] for st in ("rg", "ffn0", "l1")}
    staged["l1"].sort(key=lambda it: not it[0].startswith("gla_"))
    shard = lambda k, l: wts[k][l].reshape(-1, wts[k].shape[-1]).astype(BF16)
    own = lambda got, mine: [lax.dynamic_update_index_in_dim(g, m, chip, 0) for g, m in zip(got, mine)]
    rows_joined = lambda v: v.reshape(-1, v.shape[-1])

    def placed(its, slots):
        out = {"ffn_w_up": {}, "ffn_w_down": {}}
        for (k, l), v in zip(its, slots):
            if k == "ffn_w_up":
                out[k][l] = v
            elif k == "ffn_w_down":
                out[k][l] = rows_joined(v)
            elif k in ("rg_wa", "rg_wx"):
                out[k] = _slots_to_block_rows(v, RG_BLOCKS)
            elif k == "gla_w_in":
                out[k] = _gla_head_cols(_from_col_slots(v))
            else:
                out[k] = v if BIG[k] else rows_joined(v)
        return out

    after_mod = (mod[0, 0, 0] * 0.0).astype(BF16)
    sh_rg = [shard(k, l) + after_mod for k, l in staged["rg"]]
    local = {k: (v if k in ("norm_g", "ffn_conv_w", "ffn_conv_b") else v[0]) for k, v in full.items()}
    local["gla_wal"], local["gla_bal"] = _gla_alpha_heads(local["gla_w_alpha"], local["gla_b_alpha"])
    local.update(placed(staged["rg"], own(_gather_chips(sh_rg, "gather_weights_rg"), sh_rg)))
    sh_late, flying = {}, {}
    after_rg = (local["rg_w_out"][0, 0].astype(F32) * 0.0).astype(BF16)
    sh_late["ffn0"] = [shard(k, l) + after_rg for k, l in staged["ffn0"]]
    flying["ffn0"], tok = _send_start(sh_late["ffn0"], "whole", "weights_ffn0_start")
    sh_late["l1"] = [shard(k, l) + tok[0, 0].astype(BF16) for k, l in staged["l1"]]
    flying["l1"], tok2 = _send_start(sh_late["l1"], "whole", "weights_l1_start")
    mod = mod + (tok[0, 0] + tok2[0, 0])

    def fetch(stage, after):
        got = _send_wait(flying[stage], after, "whole", f"weights_{stage}_wait")
        return placed(staged[stage], own(got, sh_late[stage]))

    c_idx = ci.reshape(1).astype(jnp.int32)
    gslots, paired, psums, sent = {}, {}, {}, {}

    def grad_slots(gr, k, l):
        g = gr[k][l] if k in ("ffn_w_up", "ffn_w_down") else gr[k]
        if k in ("rg_wa", "rg_wx"):
            return _block_rows_to_slots(g)
        if k == "gla_w_in":
            return _col_slots(g)
        return g if BIG[k] else g.reshape(N_CHIP, -1, g.shape[-1])

    def done(stage, gr):
        gslots[stage] = [grad_slots(gr, k, l) for k, l in staged[stage]]
        paired[stage], token = _send_start(gslots[stage], "pair", f"grads_{stage}_pair_start")
        return token[0, 0]

    def later(stage, after):
        theirs = _send_wait(paired[stage], after, "pair", f"grads_{stage}_pair_wait")
        psums[stage] = [_pair_sum(g, t, c_idx, f"grads_pair_sum_{k}{l}") for (k, l), g, t in zip(staged[stage], gslots[stage], theirs)]
        sent[stage], token = _send_start(psums[stage], "slots", f"grads_{stage}_start")
        return token[0, 0]

    cols, grad_x, gr = _local_step(x[0], loss_target[0], mod, local, fetch, done, later)
    loss = lax.psum(0.5 * jnp.sum(cols) / d, ("x", "y", "c"))

    small_names = [k for k in SMALL if k != "ada_b"]
    gs = _all_gather_8(_pack([gr[k] for k in small_names] + [gr["mod"]], SUBLANES, F32), "gather_small_grads")
    small_shapes = [full[k].shape for k in small_names] + [(depth, 6 * d)]
    *small_sum, g_ada_b = _unpack(_sum_lead(gs, "sum_small_grads"), small_shapes)
    grads = dict(zip(small_names, small_sum))
    grads["ada_b"] = g_ada_b
    for k in sharded_small:
        grads[k] = _my_shard(grads[k], SMALL[k], chip)
    dmod_all = _unpack(gs, small_shapes)[-1].reshape(N_DEV, depth, N_CHIP, n_ada)
    dmod_mine = jnp.swapaxes(lax.dynamic_index_in_dim(dmod_all, chip, 2, keepdims=False), 0, 1)
    g_ada_w = _ada_bwd(c16, jnp.pad(dmod_mine, ((0, 0), (0, ADA_ROWS - N_DEV), (0, 0))), "ada_bwd")

    gslots["rg"] = [grad_slots(gr, k, l) for k, l in staged["rg"]]
    theirs = _pair_exchange(gslots["rg"], "grads_rg_pair_exchange")
    psums["rg"] = [_pair_sum(g, t, c_idx, f"grads_pair_sum_{k}{l}") for (k, l), g, t in zip(staged["rg"], gslots["rg"], theirs)]
    sent["rg"], rg_sent = _send_start(psums["rg"], "slots", "grads_rg_start")
    chip_idx = chip.reshape(1).astype(jnp.int32)
    delta, new_m, new_v = {}, {}, {}

    def reduce_and_update(stages, after, dep):
        its = [(st, n) for st in stages for n in range(len(staged[st]))]
        arrived = {st: _send_wait(sent[st], after, "slots", f"grads_{st}_wait") for st in stages}
        halves = [_chip_sum(arrived[st][n], psums[st][n], chip_idx, "grads_chip_sum_%s%d" % staged[st][n]) for st, n in its]
        shared = _pair_share(halves, "grads_pair_share_" + stages[0])
        reduced = [lax.dynamic_update_index_in_dim(s2, h, ci, 0).reshape(-1, h.shape[-1]) for s2, h in zip(shared, halves)]
        last = None
        for k in BIG:
            gs_k = [g for (st, n), g in zip(its, reduced) if staged[st][n][0] == k]
            if gs_k:
                last = update(k, gs_k, dep)
        return last

    def update(k, gs_k, dep=None):
        shp = wts[k].shape
        if k == "gla_w_in":
            view, back = (lambda a: jnp.swapaxes(a, 1, 2)), (lambda o: jnp.swapaxes(o, 1, 2))
            gs_k = [g.T for g in gs_k]
        else:
            view, back = (lambda a: a.reshape(a.shape[0], -1, a.shape[-1])), (lambda o: o.reshape(shp))
        outs = _adamw(view(wts[k]), gs_k, view(mom1[k]), view(mom2[k]), "adamw_" + k, dep)
        grads[k], delta[k], new_m[k], new_v[k] = (back(o) for o in outs)
        return new_v[k]

    done_late = reduce_and_update(("ffn0", "l1"), grad_x, rg_sent)
    update("ada_w", g_ada_w, rg_sent)
    small_shard_shapes = [wts[k].shape for k in SMALL]
    packed = [_pack([src[k] for k in SMALL], SUBLANES, F32) for src in (wts, grads, mom1, mom2)]
    outs = _adamw(packed[0][None], [packed[1]], packed[2][None], packed[3][None], "adamw_small", rg_sent)
    for dst, o in zip((delta, new_m, new_v), outs[1:]):
        for k, a in zip(SMALL, _unpack(o[0], small_shard_shapes)):
            dst[k] = a
    reduce_and_update(("rg",), done_late, None)

    return (loss, grad_x[None], *[grads[k] for k in WEIGHTS], *[delta[k] for k in WEIGHTS], *[new_m[k] for k in WEIGHTS],
            *[new_v[k] for k in WEIGHTS])
```

```python
import jax
import jax.numpy as jnp
from jax import lax
from jax.experimental import pallas as pl
from jax.experimental.pallas import tpu as pltpu

F32 = jnp.float32
BF16 = jnp.bfloat16
MXU_DTYPE = BF16

EPS = 1e-6
RG_C = 8.0
RG_BLOCKS = 4
RG_CONV = 4
GLA_HEADS = 4
GLA_TAU = 16.0
GLA_CHUNK = 64
GLA_RANK = 16
FFN_CONV = 3
ADAM_LR = 0.001
ADAM_B1 = 0.9
ADAM_B2 = 0.999
ADAM_EPS = 1e-08
ADAM_WD = 0.01
ADAM_STEP = 10

LANES = 128
SUBLANES = 8
VMEM_LIMIT = 56 * 1024 * 1024
CB = 256
MESH = pl.DeviceIdType.MESH
N_DEV = 8
N_CHIP = 4


def _params(*sem):
    return pltpu.CompilerParams(dimension_semantics=sem, vmem_limit_bytes=VMEM_LIMIT)


def _tile(dim, prefs):
    for p in prefs:
        if dim % p == 0:
            return p
    return dim


def _dot(a, b, dims):
    return lax.dot_general(a.astype(MXU_DTYPE), b.astype(MXU_DTYPE), (dims, ((), ())), preferred_element_type=F32)


def _dot_nn(a, b):
    return _dot(a, b, ((1,), (0,)))


def _dot_nt(a, b):
    return _dot(a, b, ((1,), (1,)))


def _dot_tn(a, b):
    return _dot(a, b, ((0,), (0,)))


def _mm(a, b, *, ta=False, tb=False, a_parts=1, b_parts=1, w_slots=1, out_slots=1, out_dtype=F32, name):
    if ta:
        k_dim, m_dim = a.shape
        n_dim = b.shape[-1] * b_parts
    else:
        m_dim, k_dim = a.shape[-2], a.shape[-1] * a_parts
        n_dim = b.shape[-2] if tb else b.shape[-1] * w_slots
    n_unit = n_dim // max(b_parts, out_slots, 1 if tb else w_slots)
    k_unit = k_dim // max(a_parts, w_slots if tb else 1)
    tm = _tile(m_dim, (1024, 1408, 512, 256, 128))
    tn = _tile(n_unit, (1024, 1408, 896, 512, 256, 128))
    tk = _tile(k_unit, (1024, 1408, 896, 512, 256, 128))
    nk = k_dim // tk
    dims = ((0 if ta else 1,), (1 if tb else 0,))

    def spec(shape, parts, total, tile, col_grid, row_grid):
        per = total // parts // tile

        def index(i, j, k):
            g = {"i": i, "j": j, "k": k}
            col, row = g[col_grid], g[row_grid]
            return (row, col) if parts == 1 else (col // per, row, col % per)

        return pl.BlockSpec(shape if parts == 1 else (None,) + shape, index)

    def body(a_ref, b_ref, o_ref, acc_ref):
        k = pl.program_id(2)

        @pl.when(k == 0)
        def _():
            acc_ref[...] = jnp.zeros_like(acc_ref)

        acc_ref[...] += _dot(a_ref[...], b_ref[...], dims)

        @pl.when(k == nk - 1)
        def _():
            o_ref[...] = acc_ref[...].astype(o_ref.dtype)

    if ta:
        a_spec = spec((tk, tm), 1, m_dim, tm, "i", "k")
        b_spec = spec((tk, tn), b_parts, n_dim, tn, "j", "k")
    elif tb:
        a_spec = spec((tm, tk), a_parts, k_dim, tk, "k", "i")
        b_spec = spec((tn, tk), w_slots, k_dim, tk, "k", "j")
    else:
        a_spec = spec((tm, tk), a_parts, k_dim, tk, "k", "i")
        b_spec = spec((tk, tn), w_slots, n_dim, tn, "j", "k")
    out_shape = (m_dim, n_dim) if out_slots == 1 else (out_slots, m_dim, n_dim // out_slots)
    return pl.pallas_call(
        body,
        grid=(m_dim // tm, n_dim // tn, nk),
        in_specs=[a_spec, b_spec],
        out_specs=spec((tm, tn), out_slots, n_dim, tn, "j", "i"),
        out_shape=jax.ShapeDtypeStruct(out_shape, out_dtype),
        scratch_shapes=[pltpu.VMEM((tm, tn), F32)],
        compiler_params=_params("parallel", "parallel", "arbitrary"),
        name=name,
    )(a, b)


def _row_specs(s, d, ts):
    return pl.BlockSpec((ts, d), lambda i: (i, 0)), pl.BlockSpec((1, d), lambda i: (0, 0))


def _norm_mod_fwd(x, g, sc, sh, name):
    s, d = x.shape
    ts = _tile(s, (512,))

    def body(x_ref, g_ref, sc_ref, sh_ref, h_ref):
        xv = x_ref[...]
        r = lax.rsqrt(jnp.mean(xv * xv, axis=-1, keepdims=True) + EPS)
        h_ref[...] = (((xv * r) * g_ref[...]) * (1.0 + sc_ref[...]) + sh_ref[...]).astype(h_ref.dtype)

    row, vec = _row_specs(s, d, ts)
    return pl.pallas_call(
        body, grid=(s // ts,), in_specs=[row, vec, vec, vec], out_specs=row,
        out_shape=jax.ShapeDtypeStruct((s, d), MXU_DTYPE), compiler_params=_params("parallel"), name=name,
    )(x, g, sc, sh)


def _norm_mod_bwd(dh, x, g, sc, dres, name):
    s, d = x.shape
    ts = _tile(s, (512,))

    def body(dh_ref, x_ref, g_ref, sc_ref, dres_ref, dx_ref, dg_ref, dsc_ref, dsh_ref, acc_ref):
        i = pl.program_id(0)

        @pl.when(i == 0)
        def _():
            acc_ref[...] = jnp.zeros_like(acc_ref)

        xv, dhv = x_ref[...], dh_ref[...]
        r = lax.rsqrt(jnp.mean(xv * xv, axis=-1, keepdims=True) + EPS)
        n = xv * r
        acc_ref[0:1, :] += jnp.sum(dhv * n, axis=0, keepdims=True)
        acc_ref[1:2, :] += jnp.sum(dhv, axis=0, keepdims=True)
        dn = dhv * ((1.0 + sc_ref[...]) * g_ref[...])
        dx_ref[...] = dres_ref[...] + r * (dn - n * jnp.mean(dn * n, axis=-1, keepdims=True))
        dg_ref[...] = (1.0 + sc_ref[...]) * acc_ref[0:1, :]
        dsc_ref[...] = g_ref[...] * acc_ref[0:1, :]
        dsh_ref[...] = acc_ref[1:2, :]

    row, vec = _row_specs(s, d, ts)
    vshape = jax.ShapeDtypeStruct((1, d), F32)
    return pl.pallas_call(
        body, grid=(s // ts,), in_specs=[row, row, vec, vec, row], out_specs=[row, vec, vec, vec],
        out_shape=[jax.ShapeDtypeStruct((s, d), F32), vshape, vshape, vshape],
        scratch_shapes=[pltpu.VMEM((SUBLANES, d), F32)], compiler_params=_params("arbitrary"), name=name,
    )(dh, x, g, sc, dres)


def _post_fwd(x, y, g, gt, name):
    s, d = x.shape
    ts = _tile(s, (512,))

    def body(x_ref, y_ref, g_ref, gt_ref, o_ref):
        yv = y_ref[...]
        r = lax.rsqrt(jnp.mean(yv * yv, axis=-1, keepdims=True) + EPS)
        o_ref[...] = x_ref[...] + gt_ref[...] * ((yv * r) * g_ref[...])

    row, vec = _row_specs(s, d, ts)
    return pl.pallas_call(
        body, grid=(s // ts,), in_specs=[row, row, vec, vec], out_specs=row,
        out_shape=jax.ShapeDtypeStruct((s, d), F32), compiler_params=_params("parallel"), name=name,
    )(x, y, g, gt)


def _post_bwd(dxn, y, g, gt, name):
    s, d = y.shape
    ts = _tile(s, (512,))

    def body(dxn_ref, y_ref, g_ref, gt_ref, dy_ref, dg_ref, dgt_ref, acc_ref):
        i = pl.program_id(0)

        @pl.when(i == 0)
        def _():
            acc_ref[...] = jnp.zeros_like(acc_ref)

        yv, dv = y_ref[...], dxn_ref[...]
        r = lax.rsqrt(jnp.mean(yv * yv, axis=-1, keepdims=True) + EPS)
        n = yv * r
        acc_ref[0:1, :] += jnp.sum(dv * n, axis=0, keepdims=True)
        dn = dv * (gt_ref[...] * g_ref[...])
        dy_ref[...] = (r * (dn - n * jnp.mean(dn * n, axis=-1, keepdims=True))).astype(dy_ref.dtype)
        dg_ref[...] = gt_ref[...] * acc_ref[0:1, :]
        dgt_ref[...] = g_ref[...] * acc_ref[0:1, :]

    row, vec = _row_specs(s, d, ts)
    vshape = jax.ShapeDtypeStruct((1, d), F32)
    return pl.pallas_call(
        body, grid=(s // ts,), in_specs=[row, row, vec, vec], out_specs=[row, vec, vec],
        out_shape=[jax.ShapeDtypeStruct((s, d), MXU_DTYPE), vshape, vshape],
        scratch_shapes=[pltpu.VMEM((SUBLANES, d), F32)], compiler_params=_params("arbitrary"), name=name,
    )(dxn, y, g, gt)


def _loss_grad(x, tgt, name):
    s, d = x.shape
    ts = _tile(s, (512,))

    def body(x_ref, t_ref, col_ref, dx_ref):
        i = pl.program_id(0)

        @pl.when(i == 0)
        def _():
            col_ref[...] = jnp.zeros_like(col_ref)

        e = x_ref[...] - t_ref[...]
        col_ref[...] += jnp.sum(e * e, axis=0, keepdims=True)
        dx_ref[...] = e * (1.0 / d)

    row, vec = _row_specs(s, d, ts)
    return pl.pallas_call(
        body, grid=(s // ts,), in_specs=[row, row], out_specs=[vec, row],
        out_shape=[jax.ShapeDtypeStruct((1, d), F32), jax.ShapeDtypeStruct((s, d), F32)],
        compiler_params=_params("arbitrary"), name=name,
    )(x, tgt)


_GELU_C = 0.7978845608028654
_GELU_A = 0.044715


def _gelu(x):
    t = jnp.tanh(_GELU_C * (x + _GELU_A * x * x * x))
    return 0.5 * x * (1.0 + t), t


def _gelu_grad(x, t):
    return 0.5 * (1.0 + t) + 0.5 * x * (1.0 - t * t) * (_GELU_C * (1.0 + 3.0 * _GELU_A * x * x))


def _sigmoid(x):
    return 1.0 / (1.0 + jnp.exp(-x))


def _log1p_pos(y):
    u = 1.0 + y
    return jnp.where(u == 1.0, y, jnp.log(u) * (y / jnp.where(u == 1.0, 1.0, u - 1.0)))


def _softplus(x):
    return jnp.maximum(x, 0.0) + _log1p_pos(jnp.exp(-jnp.abs(x)))


def _one_minus_exp(z):
    u = jnp.exp(z)
    lg = jnp.log(jnp.where(u > 0.0, u, 1.0))
    safe = (u != 1.0) & (u > 0.0)
    return jnp.where(u == 1.0, -z, jnp.where(u > 0.0, (1.0 - u) * (z / jnp.where(safe, lg, 1.0)), 1.0))


SLAB = 16


def _cat(a, b):
    return jnp.concatenate([a, b], axis=1)


def _pair_specs(shape, nb, index):
    return [pl.BlockSpec(shape, lambda j, t: index(j, t) + (j,)), pl.BlockSpec(shape, lambda j, t: index(j, t) + (j + nb,))]


def _halo_row(ts, time_of):
    return lambda j, t: (jnp.maximum(time_of(t) * (ts // SUBLANES) - 1, 0),)


def _rows_from(groups, k):
    row = lax.broadcasted_iota(jnp.int32, groups[0].shape, 0)
    turned = [pltpu.roll(g, SUBLANES - k, axis=0) for g in groups]
    return [jnp.where(row < SUBLANES - k, lo, hi) for lo, hi in zip(turned[:-1], turned[1:])]


def _ffn_mid_fwd(p, cw, cb, name):
    s, f2 = p.shape
    ts = _tile(s, (512,))
    nb, nt = f2 // (2 * CB), s // ts
    n_grp = SLAB // SUBLANES

    def body(pg_ref, pv_ref, hg_ref, hv_ref, cwg_ref, cwv_ref, cbg_ref, cbv_ref, a_ref, ga_ref, gb_ref):
        t = pl.program_id(1)
        cwv, bias = _cat(cwg_ref[...], cwv_ref[...]), _cat(cbg_ref[...], cbv_ref[...])
        w0, w1, w2 = cwv[0:1], cwv[1:2], cwv[2:3]

        def slab(before, cur, r0):
            pm2, pm1 = _rows_from([before] + cur, SUBLANES - 2), _rows_from([before] + cur, SUBLANES - 1)
            u = jnp.concatenate([bias + w0 * pm2[i] + w1 * pm1[i] + w2 * cur[i] for i in range(n_grp)], axis=0)
            g, v = u[:, :CB], u[:, CB:]
            gel, th = _gelu(g)
            rows = pl.ds(r0, SLAB)
            a_ref[rows, :] = (gel * v).astype(a_ref.dtype)
            ga_ref[rows, :] = gel.astype(ga_ref.dtype)
            gb_ref[rows, :] = (v * _gelu_grad(g, th)).astype(gb_ref.dtype)

        def pieces(rows):
            blk = _cat(pg_ref[rows, :], pv_ref[rows, :])
            return [blk[i * SUBLANES:(i + 1) * SUBLANES] for i in range(blk.shape[0] // SUBLANES)]

        slab(jnp.where(t > 0, _cat(hg_ref[...], hv_ref[...]), 0.0), pieces(pl.ds(0, SLAB)), 0)

        def loop(i, carry):
            r0 = pl.multiple_of(i * SLAB, SLAB)
            got = pieces(pl.ds(pl.multiple_of(r0 - SUBLANES, SUBLANES), SLAB + SUBLANES))
            slab(got[0], got[1:], r0)
            return carry

        lax.fori_loop(1, ts // SLAB, loop, 0, unroll=2)

    fwd = lambda t: t
    out = pl.BlockSpec((ts, CB), lambda j, t: (t, j))
    shape = jax.ShapeDtypeStruct((s, f2 // 2), MXU_DTYPE)
    return pl.pallas_call(
        body, grid=(nb, nt),
        in_specs=(_pair_specs((ts, CB), nb, lambda j, t: (t,)) + _pair_specs((SUBLANES, CB), nb, _halo_row(ts, fwd))
                  + _pair_specs((FFN_CONV, CB), nb, lambda j, t: (0,)) + _pair_specs((1, CB), nb, lambda j, t: (0,))),
        out_specs=[out, out, out], out_shape=[shape, shape, shape],
        compiler_params=_params("parallel", "arbitrary"), name=name,
    )(p, p, p, p, cw, cw, cb, cb)


def _ffn_mid_bwd(da, p, ga, gb, cw, name):
    s, f2 = p.shape
    ts = _tile(s, (512,))
    nb, nt = f2 // (2 * CB), s // ts
    n_slab = ts // SLAB
    n_grp = SLAB // SUBLANES
    per_trip = 2

    def body(da_ref, ga_ref, gb_ref, pg_ref, pv_ref, cwg_ref, cwv_ref, dp_ref, dcw_ref, dcb_ref, next_du, acc):
        tt = pl.program_id(1)
        cwv = _cat(cwg_ref[...], cwv_ref[...])
        w0, w1, w2 = cwv[0:1], cwv[1:2], cwv[2:3]

        @pl.when(tt == 0)
        def _():
            next_du[...] = jnp.zeros_like(next_du)
            acc[...] = jnp.zeros_like(acc)

        def slab(r0, after, sums):
            rows = pl.ds(r0, SLAB)
            dav = da_ref[rows, :]
            du = _cat(dav * gb_ref[rows, :].astype(F32), dav * ga_ref[rows, :].astype(F32))
            p0 = _cat(pg_ref[rows, :], pv_ref[rows, :])
            cur = [du[i * SUBLANES:(i + 1) * SUBLANES] for i in range(n_grp)]
            du1, du2 = _rows_from(cur + [after], 1), _rows_from(cur + [after], 2)
            dpv = jnp.concatenate([w2 * cur[i] + w1 * du1[i] + w0 * du2[i] for i in range(n_grp)], axis=0).astype(dp_ref.dtype)
            dp_ref[0, rows, :] = dpv[:, :CB]
            dp_ref[1, rows, :] = dpv[:, CB:]
            for i in range(n_grp):
                pi = p0[i * SUBLANES:(i + 1) * SUBLANES]
                parts = (cur[i], du2[i] * pi, du1[i] * pi, cur[i] * pi)
                sums = parts if sums is None else tuple(x + y for x, y in zip(sums, parts))
            return cur[0], sums

        def loop(k, after):
            sums = None
            for j in range(per_trip):
                r0 = pl.multiple_of((n_slab - 1 - (k * per_trip + j)) * SLAB, SLAB)
                after, sums = slab(r0, after, sums)
            for q, part in enumerate(sums):
                acc[q] += part
            return after

        next_du[...] = lax.fori_loop(0, n_slab // per_trip, loop, next_du[...])

        @pl.when(tt == nt - 1)
        def _():
            for half in range(2):
                cols = slice(half * CB, (half + 1) * CB)
                dcb_ref[half] = jnp.sum(acc[0][:, cols], axis=0, keepdims=True)
                for k in range(FFN_CONV):
                    dcw_ref[half, k:k + 1, :] = jnp.sum(acc[1 + k][:, cols], axis=0, keepdims=True)

    rev = lambda t: nt - 1 - t
    tile = pl.BlockSpec((ts, CB), lambda j, t: (rev(t), j))
    return pl.pallas_call(
        body, grid=(nb, nt),
        in_specs=([tile, tile, tile] + _pair_specs((ts, CB), nb, lambda j, t: (rev(t),))
                  + _pair_specs((FFN_CONV, CB), nb, lambda j, t: (0,))),
        out_specs=[pl.BlockSpec((2, ts, CB), lambda j, t: (0, rev(t), j)),
                   pl.BlockSpec((2, FFN_CONV, CB), lambda j, t: (0, 0, j)),
                   pl.BlockSpec((2, 1, CB), lambda j, t: (0, 0, j))],
        out_shape=[jax.ShapeDtypeStruct((2, s, f2 // 2), MXU_DTYPE), jax.ShapeDtypeStruct((2, FFN_CONV, f2 // 2), F32),
                   jax.ShapeDtypeStruct((2, 1, f2 // 2), F32)],
        scratch_shapes=[pltpu.VMEM((SUBLANES, 2 * CB), F32), pltpu.VMEM((1 + FFN_CONV, SUBLANES, 2 * CB), F32)],
        compiler_params=_params("parallel", "arbitrary"), name=name,
    )(da, ga, gb, p, p, cw, cw)


def _rg_gates(xc, wa_ref, ba_ref, wx_ref, bx_ref, lam_ref):
    r = _sigmoid(_dot_nn(xc, wa_ref[0]) + ba_ref[...])
    ig = _sigmoid(_dot_nn(xc, wx_ref[0]) + bx_ref[...])
    sp = _softplus(-lam_ref[...])
    log_a = (-RG_C) * r * sp
    a = jnp.exp(log_a)
    mult = jnp.sqrt(_one_minus_exp(2.0 * log_a))
    return r, ig, sp, a, mult


def _rg_conv(scr, cw_ref, cb_ref, ts):
    views = [scr[5 + k:5 + k + ts, :] for k in range(RG_CONV)]
    xc = cb_ref[...]
    for k in range(RG_CONV):
        xc = xc + cw_ref[k:k + 1, :] * views[k]
    return xc, views


def _rg_param_specs():
    vec = pl.BlockSpec((1, CB), lambda g, t: (0, g))
    mat = pl.BlockSpec((1, CB, CB), lambda g, t: (g, 0, 0))
    return [pl.BlockSpec((RG_CONV, CB), lambda g, t: (0, g)), vec, mat, vec, mat, vec, vec]


def _scan_rows(a_scr, x_scr, out_ref, carry, ts, reverse):
    row = lax.broadcasted_iota(jnp.int32, (SUBLANES, a_scr.shape[1]), 0)
    last = SUBLANES - 1

    def group(k, c):
        r0 = pl.multiple_of((ts // SUBLANES - 1 - k if reverse else k) * SUBLANES, SUBLANES)
        rows = pl.ds(r0, SUBLANES)
        a, x = a_scr[rows, :], x_scr[rows, :]
        if reverse:
            first_a = a[0:1]
            x = jnp.where(row == last, x + c, x)
            a = jnp.where(row == last, 1.0, pltpu.roll(a, last, axis=0))
            for sh in (1, 2, 4):
                keep = row < SUBLANES - sh
                x = x + a * jnp.where(keep, pltpu.roll(x, SUBLANES - sh, axis=0), 0.0)
                a = a * jnp.where(keep, pltpu.roll(a, SUBLANES - sh, axis=0), 1.0)
            out_ref[rows, :] = x
            return first_a * x[0:1]
        for sh in (1, 2, 4):
            keep = row >= sh
            x = a * jnp.where(keep, pltpu.roll(x, sh, axis=0), 0.0) + x
            a = a * jnp.where(keep, pltpu.roll(a, sh, axis=0), 1.0)
        h = x + a * c
        out_ref[rows, :] = h
        return h[last:last + 1]

    return lax.fori_loop(0, ts // SUBLANES, group, carry, unroll=4)


def _rg_mid_fwd(pj, cw, cb, wa, ba, wx, bx, lam, name):
    s = pj.shape[0]
    nb = pj.shape[1] // (2 * CB)
    ts = _tile(s, (512,))
    nt = s // ts

    def body(gate_ref, x_ref, halo_ref, cw_ref, cb_ref, wa_ref, ba_ref, wx_ref, bx_ref, lam_ref, y_ref, hs_ref,
             scr, a_scr, u_scr, h_scr):
        t = pl.program_id(1)

        @pl.when(t == 0)
        def _():
            h_scr[...] = jnp.zeros_like(h_scr)

        scr[0:SUBLANES, :] = jnp.where(t > 0, halo_ref[...], 0.0)
        scr[SUBLANES:, :] = x_ref[...]
        xc, _ = _rg_conv(scr, cw_ref, cb_ref, ts)
        _, ig, _, a, mult = _rg_gates(xc, wa_ref, ba_ref, wx_ref, bx_ref, lam_ref)
        a_scr[...] = a
        u_scr[...] = mult * (ig * xc)
        h_scr[0:1, :] = _scan_rows(a_scr, u_scr, hs_ref, h_scr[0:1, :], ts, False)
        y_ref[...] = (_gelu(gate_ref[...])[0] * hs_ref[...]).astype(y_ref.dtype)

    blk = pl.BlockSpec((ts, CB), lambda g, t: (t, g))
    return pl.pallas_call(
        body, grid=(nb, nt),
        in_specs=_pair_specs((ts, CB), nb, lambda g, t: (t,))
        + [pl.BlockSpec((SUBLANES, CB), lambda g, t: _halo_row(ts, lambda u: u)(g, t) + (g + nb,))] + _rg_param_specs(),
        out_specs=[blk, blk],
        out_shape=[jax.ShapeDtypeStruct((s, nb * CB), MXU_DTYPE), jax.ShapeDtypeStruct((s, nb * CB), F32)],
        scratch_shapes=[pltpu.VMEM((ts + SUBLANES, CB), F32), pltpu.VMEM((ts, CB), F32), pltpu.VMEM((ts, CB), F32),
                        pltpu.VMEM((SUBLANES, CB), F32)],
        compiler_params=_params("parallel", "arbitrary"), name=name,
    )(pj, pj, pj, cw, cb, wa, ba, wx, bx, lam)


def _rg_mid_bwd(dy, pj, hs, cw, cb, wa, ba, wx, bx, lam, name):
    s = pj.shape[0]
    nb = pj.shape[1] // (2 * CB)
    ts = _tile(s, (512,))
    nt = s // ts

    def body(dy_ref, gate_ref, x_ref, halo_ref, hs_ref, hsh_ref, cw_ref, cb_ref, wa_ref, ba_ref, wx_ref, bx_ref, lam_ref,
             dpj_ref, dcw_ref, dcb_ref, dwa_ref, dba_ref, dwx_ref, dbx_ref, dlam_ref,
             scr, hscr, a_scr, d_scr, g_scr, dxscr, c_scr):
        tt = pl.program_id(1)
        t = nt - 1 - tt

        @pl.when(tt == 0)
        def _():
            c_scr[...] = jnp.zeros_like(c_scr)
            dxscr[ts:, :] = jnp.zeros((SUBLANES, CB), F32)
            for ref in (dcw_ref, dcb_ref, dwa_ref, dba_ref, dwx_ref, dbx_ref, dlam_ref):
                ref[...] = jnp.zeros_like(ref)

        scr[0:SUBLANES, :] = jnp.where(t > 0, halo_ref[...], 0.0)
        scr[SUBLANES:, :] = x_ref[...]
        hscr[0:SUBLANES, :] = jnp.where(t > 0, hsh_ref[...], 0.0)
        hscr[SUBLANES:, :] = hs_ref[...]
        xc, views = _rg_conv(scr, cw_ref, cb_ref, ts)
        r, ig, sp, a, mult = _rg_gates(xc, wa_ref, ba_ref, wx_ref, bx_ref, lam_ref)
        gate = gate_ref[...]
        gel, th = _gelu(gate)
        dyv = dy_ref[...]
        dpj_ref[0] = (dyv * hs_ref[...] * _gelu_grad(gate, th)).astype(dpj_ref.dtype)
        a_scr[...] = a
        d_scr[...] = dyv * gel
        c_scr[0:1, :] = _scan_rows(a_scr, d_scr, g_scr, c_scr[0:1, :], ts, True)
        du = g_scr[...]
        da = du * hscr[7:7 + ts, :]
        dmult = du * (ig * xc)
        dig = du * (mult * xc)
        dxc = du * (mult * ig)
        dlog_a = da * a - dmult * (a * a / mult)
        dlam_ref[...] += jnp.sum(dlog_a * r, axis=0, keepdims=True) * (RG_C * _sigmoid(-lam_ref[...]))
        dpr = dlog_a * ((-RG_C) * sp) * (r * (1.0 - r))
        dpi = dig * (ig * (1.0 - ig))
        dba_ref[...] += jnp.sum(dpr, axis=0, keepdims=True)
        dbx_ref[...] += jnp.sum(dpi, axis=0, keepdims=True)
        dwa_ref[0] += _dot_tn(xc, dpr)
        dwx_ref[0] += _dot_tn(xc, dpi)
        dxc = dxc + _dot_nt(dpr, wa_ref[0]) + _dot_nt(dpi, wx_ref[0])
        dcb_ref[...] += jnp.sum(dxc, axis=0, keepdims=True)
        for k in range(RG_CONV):
            dcw_ref[k:k + 1, :] += jnp.sum(dxc * views[k], axis=0, keepdims=True)
        dxscr[0:ts, :] = dxc
        dxp = cw_ref[3:4, :] * dxc
        for k in range(RG_CONV - 1):
            dxp = dxp + cw_ref[k:k + 1, :] * dxscr[3 - k:3 - k + ts, :]
        dpj_ref[1] = dxp.astype(dpj_ref.dtype)
        dxscr[ts:, :] = dxscr[0:SUBLANES, :]

    rev = lambda g, t: (nt - 1 - t, g)
    rev_halo = lambda g, t: (jnp.maximum((nt - 1 - t) * (ts // SUBLANES) - 1, 0), g)
    vec = pl.BlockSpec((1, CB), lambda g, t: (0, g))
    mat = pl.BlockSpec((1, CB, CB), lambda g, t: (g, 0, 0))
    d = nb * CB
    vshape = jax.ShapeDtypeStruct((1, d), F32)
    mshape = jax.ShapeDtypeStruct((nb, CB, CB), F32)
    return pl.pallas_call(
        body, grid=(nb, nt),
        in_specs=[pl.BlockSpec((ts, CB), rev)] + _pair_specs((ts, CB), nb, lambda g, t: (nt - 1 - t,))
        + [pl.BlockSpec((SUBLANES, CB), lambda g, t: (rev_halo(g, t)[0], g + nb)),
           pl.BlockSpec((ts, CB), rev), pl.BlockSpec((SUBLANES, CB), rev_halo)] + _rg_param_specs(),
        out_specs=[pl.BlockSpec((2, ts, CB), lambda g, t: (0, nt - 1 - t, g)), pl.BlockSpec((RG_CONV, CB), lambda g, t: (0, g)),
                   vec, mat, vec, mat, vec, vec],
        out_shape=[jax.ShapeDtypeStruct((2, s, d), MXU_DTYPE), jax.ShapeDtypeStruct((RG_CONV, d), F32), vshape, mshape, vshape,
                   mshape, vshape, vshape],
        scratch_shapes=[pltpu.VMEM((ts + SUBLANES, CB), F32), pltpu.VMEM((ts + SUBLANES, CB), F32), pltpu.VMEM((ts, CB), F32),
                        pltpu.VMEM((ts, CB), F32), pltpu.VMEM((ts, CB), F32), pltpu.VMEM((ts + SUBLANES, CB), F32),
                        pltpu.VMEM((SUBLANES, CB), F32)],
        compiler_params=_params("parallel", "arbitrary"), name=name,
    )(dy, pj, pj, pj, hs, hs, cw, cb, wa, ba, wx, bx, lam)


GLA_DK = 128
GLA_DV = 256
GLA_HB = 2 * GLA_DK + 2 * GLA_DV + LANES
GLA_TS = 256


def _split3(x):
    hi = x.astype(BF16)
    r1 = x - hi.astype(F32)
    mid = r1.astype(BF16)
    lo = (r1 - mid.astype(F32)).astype(BF16)
    return hi, mid, lo


def _chunk_cumsum(x, reverse):
    n = x.shape[0]
    i = lax.broadcasted_iota(jnp.int32, (n, n), 0)
    j = lax.broadcasted_iota(jnp.int32, (n, n), 1)
    same = (i // GLA_CHUNK) == (j // GLA_CHUNK)
    tri = jnp.where(same & ((j >= i) if reverse else (j <= i)), 1.0, 0.0).astype(BF16)
    out = jnp.zeros(x.shape, F32)
    for piece in _split3(x):
        out = out + lax.dot_general(tri, piece, (((1,), (0,)), ((), ())), preferred_element_type=F32)
    return out


def _gla_split(blk):
    q = blk[:, 0:GLA_DK] * (GLA_DK ** -0.5)
    k = blk[:, GLA_DK:2 * GLA_DK]
    v = blk[:, 2 * GLA_DK:2 * GLA_DK + GLA_DV]
    r = blk[:, 2 * GLA_DK + GLA_DV:2 * GLA_DK + 2 * GLA_DV]
    z = blk[:, 2 * GLA_DK + 2 * GLA_DV:]
    return q, k, v, r, z


def _gla_decays(gc):
    gref = gc[GLA_CHUNK // 2:GLA_CHUNK // 2 + 1, :]
    glast = gc[GLA_CHUNK - 1:GLA_CHUNK, :]
    return jnp.exp(gc), jnp.exp(gc - gref), jnp.exp(gref - gc), jnp.exp(glast - gc), jnp.exp(glast)


def _causal_mask():
    i = lax.broadcasted_iota(jnp.int32, (GLA_CHUNK, GLA_CHUNK), 0)
    j = lax.broadcasted_iota(jnp.int32, (GLA_CHUNK, GLA_CHUNK), 1)
    return j <= i


def _log_sigmoid(x):
    return jnp.minimum(x, 0.0) - _log1p_pos(jnp.exp(-jnp.abs(x)))


def _gla_mid_fwd(pj, wal, bal, ng, name):
    s = pj.shape[0]
    nh = pj.shape[1] // GLA_HB
    ts = _tile(s, (GLA_TS,))
    nt, nc = s // ts, ts // GLA_CHUNK

    def body(pj_ref, wal_ref, bal_ref, ng_ref, act_ref, o_ref, st_ref, s_scr):
        t = pl.program_id(0)

        @pl.when(t == 0)
        def _():
            s_scr[...] = jnp.zeros_like(s_scr)

        heads = []
        for h in range(nh):
            q, k, v, r, z = _gla_split(pj_ref[:, h * GLA_HB:(h + 1) * GLA_HB])
            g = _log_sigmoid(_dot_nn(z, wal_ref[h]) + bal_ref[h]) * (1.0 / GLA_TAU)
            heads.append((q, k, v, r, _chunk_cumsum(g, False)))
        mask = _causal_mask()
        for c in range(nc):
            sl = slice(c * GLA_CHUNK, (c + 1) * GLA_CHUNK)
            for h, (q, k, v, r, gcum) in enumerate(heads):
                eg, eq, ek, ekd, egl = _gla_decays(gcum[sl])
                st = s_scr[h]
                st_ref[c, h] = st
                attn = jnp.where(mask, _dot_nt(q[sl] * eq, k[sl] * ek), 0.0)
                o_ref[sl, h * GLA_DV:(h + 1) * GLA_DV] = _dot_nt(q[sl] * eg, st) + _dot_nn(attn, v[sl])
                s_scr[h] = st * egl + _dot_tn(v[sl], k[sl] * ekd)
        for h, (q, k, v, r, gcum) in enumerate(heads):
            cols = slice(h * GLA_DV, (h + 1) * GLA_DV)
            o = o_ref[:, cols]
            on = o * lax.rsqrt(jnp.mean(o * o, axis=-1, keepdims=True) + EPS)
            act_ref[:, cols] = ((on * ng_ref[...]) * (r * _sigmoid(r))).astype(act_ref.dtype)

    blk = pl.BlockSpec((ts, nh * GLA_DV), lambda t: (t, 0))
    whole = lambda shape: pl.BlockSpec(shape, lambda t: (0,) * len(shape))
    return pl.pallas_call(
        body, grid=(nt,),
        in_specs=[pl.BlockSpec((ts, nh * GLA_HB), lambda t: (t, 0)), whole((nh, LANES, GLA_DK)), whole((nh, 1, GLA_DK)),
                  whole((1, GLA_DV))],
        out_specs=[blk, blk, pl.BlockSpec((nc, nh, GLA_DV, GLA_DK), lambda t: (t, 0, 0, 0))],
        out_shape=[jax.ShapeDtypeStruct((s, nh * GLA_DV), MXU_DTYPE), jax.ShapeDtypeStruct((s, nh * GLA_DV), F32),
                   jax.ShapeDtypeStruct((s // GLA_CHUNK, nh, GLA_DV, GLA_DK), F32)],
        scratch_shapes=[pltpu.VMEM((nh, GLA_DV, GLA_DK), F32)],
        compiler_params=_params("arbitrary"), name=name,
    )(pj, wal, bal, ng)


def _gla_mid_bwd(dact, pj, o, st, wal, bal, ng, name):
    s = pj.shape[0]
    nh = pj.shape[1] // GLA_HB
    ts = _tile(s, (GLA_TS,))
    nt, nc = s // ts, ts // GLA_CHUNK
    o_q, o_k, o_v, o_r, o_z = 0, GLA_DK, 2 * GLA_DK, 2 * GLA_DK + GLA_DV, 2 * GLA_DK + 2 * GLA_DV

    def body(dact_ref, pj_ref, o_ref, st_ref, wal_ref, bal_ref, ng_ref, dpj_ref, dwal_ref, dbal_ref, dng_ref,
             ds_scr, dg_scr):
        tt = pl.program_id(0)

        @pl.when(tt == 0)
        def _():
            ds_scr[...] = jnp.zeros_like(ds_scr)
            dwal_ref[...] = jnp.zeros_like(dwal_ref)
            dbal_ref[...] = jnp.zeros_like(dbal_ref)
            dng_ref[...] = jnp.zeros_like(dng_ref)

        heads = []
        for h in range(nh):
            base = h * GLA_HB
            q, k, v, r, z = _gla_split(pj_ref[:, base:base + GLA_HB])
            logit = _dot_nn(z, wal_ref[h]) + bal_ref[h]
            gcum = _chunk_cumsum(_log_sigmoid(logit) * (1.0 / GLA_TAU), False)
            ov = o_ref[:, h * GLA_DV:(h + 1) * GLA_DV]
            ro = lax.rsqrt(jnp.mean(ov * ov, axis=-1, keepdims=True) + EPS)
            on = ov * ro
            sg = _sigmoid(r)
            sil = r * sg
            dav = dact_ref[:, h * GLA_DV:(h + 1) * GLA_DV]
            dpj_ref[:, base + o_r:base + o_z] = (dav * (on * ng_ref[...]) * (sg + sil * (1.0 - sg))).astype(dpj_ref.dtype)
            t1 = dav * sil
            dng_ref[...] += jnp.sum(t1 * on, axis=0, keepdims=True)
            dn = t1 * ng_ref[...]
            do = ro * (dn - on * jnp.mean(dn * on, axis=-1, keepdims=True))
            heads.append((q, k, v, z, logit, gcum, do))
        mask = _causal_mask()
        scale = GLA_DK ** -0.5
        last_row = lax.broadcasted_iota(jnp.int32, (GLA_CHUNK, GLA_DK), 0) == GLA_CHUNK - 1
        for c in reversed(range(nc)):
            sl = slice(c * GLA_CHUNK, (c + 1) * GLA_CHUNK)
            for h, (q, k, v, z, logit, gcum, do) in enumerate(heads):
                base = h * GLA_HB
                eg, eq, ek, ekd, egl = _gla_decays(gcum[sl])
                qc, kc, vc, doc = q[sl], k[sl], v[sl], do[sl]
                qg, qt, kt, kd = qc * eg, qc * eq, kc * ek, kc * ekd
                sp = st_ref[c, h]
                ds = ds_scr[h]
                attn = jnp.where(mask, _dot_nt(qt, kt), 0.0)
                dattn = jnp.where(mask, _dot_nt(doc, vc), 0.0)
                dqg = _dot_nn(doc, sp)
                dqt = _dot_nn(dattn, kt)
                dkt = _dot_tn(dattn, qt)
                dkd = _dot_nn(vc, ds)
                dpj_ref[sl, base + o_v:base + o_r] = (_dot_tn(attn, doc) + _dot_nt(kd, ds)).astype(dpj_ref.dtype)
                dpj_ref[sl, base + o_q:base + o_k] = (scale * (dqg * eg + dqt * eq)).astype(dpj_ref.dtype)
                dpj_ref[sl, base + o_k:base + o_v] = (dkt * ek + dkd * ekd).astype(dpj_ref.dtype)
                kdd = dkd * kd
                dgl = jnp.sum(kdd, axis=0, keepdims=True) + jnp.sum(ds * sp, axis=0, keepdims=True) * egl
                dg_scr[h, sl, :] = dqg * qg + dqt * qt - dkt * kt - kdd + jnp.where(last_row, dgl, 0.0)
                ds_scr[h] = ds * egl + _dot_tn(doc, qg)
        for h, (q, k, v, z, logit, gcum, do) in enumerate(heads):
            base = h * GLA_HB
            dlogit = _chunk_cumsum(dg_scr[h], True) * (1.0 / GLA_TAU) * _sigmoid(-logit)
            dpj_ref[:, base + o_z:base + GLA_HB] = _dot_nt(dlogit, wal_ref[h]).astype(dpj_ref.dtype)
            dwal_ref[h] += _dot_tn(z, dlogit)
            dbal_ref[h] += jnp.sum(dlogit, axis=0, keepdims=True)

    rev = lambda t: (nt - 1 - t, 0)
    whole = lambda shape: pl.BlockSpec(shape, lambda t: (0,) * len(shape))
    wide = pl.BlockSpec((ts, nh * GLA_DV), rev)
    return pl.pallas_call(
        body, grid=(nt,),
        in_specs=[wide, pl.BlockSpec((ts, nh * GLA_HB), rev), wide,
                  pl.BlockSpec((nc, nh, GLA_DV, GLA_DK), lambda t: (nt - 1 - t, 0, 0, 0)),
                  whole((nh, LANES, GLA_DK)), whole((nh, 1, GLA_DK)), whole((1, GLA_DV))],
        out_specs=[pl.BlockSpec((ts, nh * GLA_HB), rev), whole((nh, LANES, GLA_DK)), whole((nh, 1, GLA_DK)), whole((1, GLA_DV))],
        out_shape=[jax.ShapeDtypeStruct((s, nh * GLA_HB), MXU_DTYPE), jax.ShapeDtypeStruct((nh, LANES, GLA_DK), F32),
                   jax.ShapeDtypeStruct((nh, 1, GLA_DK), F32), jax.ShapeDtypeStruct((1, GLA_DV), F32)],
        scratch_shapes=[pltpu.VMEM((nh, GLA_DV, GLA_DK), F32), pltpu.VMEM((nh, ts, GLA_DK), F32)],
        compiler_params=_params("arbitrary"), name=name,
    )(dact, pj, o, st, wal, bal, ng)


def _adamw(w, gs, m, v, name, after=None):
    layers, rows, cols = w.shape
    gs = list(gs) if isinstance(gs, (list, tuple)) else gs
    n_g = len(gs) if isinstance(gs, list) else 1
    if rows % SUBLANES == 0:
        tr, tc = _tile(rows, (256, 128, 64, 32, 16, 8)), cols
    else:
        tr, tc = rows, _tile(cols, (256, 128))
    c1 = 1.0 / (1.0 - ADAM_B1 ** ADAM_STEP)
    c2 = 1.0 / (1.0 - ADAM_B2 ** ADAM_STEP)

    def body(*refs):
        g_refs, (w_ref, m_ref, v_ref) = refs[:n_g], refs[n_g:n_g + 3]
        go_ref, d_ref, mo_ref, vo_ref = refs[-4:]
        gv = g_refs[0][...]
        for l in range(1, n_g):
            gv = jnp.where(pl.program_id(0) == l, g_refs[l][...], gv)
        m2 = ADAM_B1 * m_ref[...] + (1.0 - ADAM_B1) * gv
        v2 = ADAM_B2 * v_ref[...] + (1.0 - ADAM_B2) * (gv * gv)
        d_ref[...] = (-ADAM_LR) * ((m2 * c1) / (jnp.sqrt(v2 * c2) + ADAM_EPS) + ADAM_WD * w_ref[...])
        go_ref[...] = gv
        mo_ref[...] = m2
        vo_ref[...] = v2

    spec = pl.BlockSpec((None, tr, tc), lambda l, i, j: (l, i, j))
    g_specs = [pl.BlockSpec((tr, tc), lambda l, i, j: (i, j))] * n_g if isinstance(gs, list) else [spec]
    extra = [] if after is None else [(after, _ANY)]
    shape = jax.ShapeDtypeStruct((layers, rows, cols), F32)
    return pl.pallas_call(
        body, grid=(layers, rows // tr, cols // tc), in_specs=g_specs + [spec] * 3 + [sp for _, sp in extra],
        out_specs=[spec] * 4, out_shape=[shape] * 4, compiler_params=_params("parallel", "parallel", "parallel"), name=name,
    )(*(gs if isinstance(gs, list) else [gs]), w, m, v, *[a for a, _ in extra])


def _gla_head_cols(w):
    d = w.shape[0]
    qk, dv = GLA_HEADS * GLA_DK, GLA_HEADS * GLA_DV
    q, k, v, r, z = jnp.split(w, [qk, 2 * qk, 2 * qk + dv, 2 * qk + 2 * dv], axis=1)
    zp = jnp.pad(z, ((0, 0), (0, LANES - GLA_RANK)))
    parts = [q.reshape(d, GLA_HEADS, GLA_DK), k.reshape(d, GLA_HEADS, GLA_DK), v.reshape(d, GLA_HEADS, GLA_DV),
             r.reshape(d, GLA_HEADS, GLA_DV), jnp.broadcast_to(zp[:, None, :], (d, GLA_HEADS, LANES))]
    return jnp.concatenate(parts, axis=2).reshape(d, GLA_HEADS * GLA_HB)


def _gla_unhead_cols(w):
    d = w.shape[0]
    w = w.reshape(d, GLA_HEADS, GLA_HB)
    o = 2 * GLA_DK + 2 * GLA_DV
    parts = [w[:, :, 0:GLA_DK].reshape(d, -1), w[:, :, GLA_DK:2 * GLA_DK].reshape(d, -1),
             w[:, :, 2 * GLA_DK:2 * GLA_DK + GLA_DV].reshape(d, -1), w[:, :, 2 * GLA_DK + GLA_DV:o].reshape(d, -1),
             jnp.sum(w[:, :, o:o + GLA_RANK], axis=1)]
    return jnp.concatenate(parts, axis=1)


def _gla_alpha_heads(w_alpha, b_alpha):
    wal = jnp.swapaxes(w_alpha.reshape(GLA_RANK, GLA_HEADS, GLA_DK), 0, 1)
    return jnp.pad(wal, ((0, 0), (0, LANES - GLA_RANK), (0, 0))), b_alpha.reshape(GLA_HEADS, 1, GLA_DK)


def _gla_layouts(w):
    w = dict(w)
    w["gla_wal"], w["gla_bal"] = _gla_alpha_heads(w["gla_w_alpha"], w["gla_b_alpha"])
    w["gla_w_in"] = _gla_head_cols(w["gla_w_in"])
    return w


def _col_slots(w):
    r, c = w.shape
    return jnp.moveaxis(w.reshape(r, N_CHIP, c // N_CHIP), 1, 0)


def _from_col_slots(w):
    n, r, c = w.shape
    return jnp.moveaxis(w, 0, 1).reshape(r, n * c)


def _block_rows_to_slots(w):
    g, r4, cc = w.shape
    return jnp.swapaxes(w.reshape(g, N_CHIP, r4 // N_CHIP, cc), 0, 1).reshape(N_CHIP, g * (r4 // N_CHIP), cc)


def _slots_to_block_rows(w, g):
    n, gr, cc = w.shape
    return jnp.swapaxes(w.reshape(n, g, gr // g, cc), 0, 1).reshape(g, n * (gr // g), cc)


def _local_step(x, tgt, mod, w, fetch=None, done=None, later=None):
    depth = mod.shape[0]
    row = lambda v: v.reshape(1, -1)
    w = dict(w)
    w["ffn_w_up"], w["ffn_w_down"] = dict(enumerate(w["ffn_w_up"])), dict(enumerate(w["ffn_w_down"]))

    def arrive(stage, after):
        if fetch is not None:
            for k, v in fetch(stage, after).items():
                if isinstance(v, dict):
                    w[k].update(v)
                else:
                    w[k] = v

    saved = []
    for i in range(depth):
        if i == 1:
            arrive("l1", x)
        sh_m, sc_m, gt_m, sh_f, sc_f, gt_f = (mod[i, j:j + 1] for j in range(6))
        g0, g1, g2, g3 = (w["norm_g"][i, j:j + 1] for j in range(4))
        tag = f"_l{i}"
        h = _norm_mod_fwd(x, g0, sc_m, sh_m, "norm_mix" + tag)
        if i % 2 == 0:
            pj = _mm(h, w["rg_w_in"], w_slots=N_CHIP, name="rg_in" + tag)
            act, aux = _rg_mid_fwd(pj, w["rg_conv_w"], row(w["rg_conv_b"]), w["rg_wa"], row(w["rg_ba"]), w["rg_wx"],
                                   row(w["rg_bx"]), row(w["rg_lambda"]), "rg_mid" + tag)
            y = _mm(act, w["rg_w_out"], name="rg_out" + tag)
        else:
            pj = _mm(h, w["gla_w_in"], name="gla_in" + tag)
            act, *aux = _gla_mid_fwd(pj, w["gla_wal"], w["gla_bal"], row(w["gla_norm_g"]), "gla_mid" + tag)
            y = _mm(act, w["gla_w_out"], name="gla_out" + tag)
        x1 = _post_fwd(x, y, g1, gt_m, "post_mix" + tag)
        if i == 0:
            arrive("ffn0", x1)
        h2 = _norm_mod_fwd(x1, g2, sc_f, sh_f, "norm_ffn" + tag)
        p = _mm(h2, w["ffn_w_up"][i], w_slots=N_CHIP, name="ffn_up" + tag)
        a, ga, gb = _ffn_mid_fwd(p, w["ffn_conv_w"][i], w["ffn_conv_b"][i:i + 1], "ffn_mid" + tag)
        y2 = _mm(a, w["ffn_w_down"][i], name="ffn_down" + tag)
        x2 = _post_fwd(x1, y2, g3, gt_f, "post_ffn" + tag)
        saved.append((x, h, pj, act, aux, y, x1, h2, p, (a, ga, gb), y2))
        x = x2

    cols, dx = _loss_grad(x, tgt, "loss")

    stacked = ("norm_g", "ffn_conv_w", "ffn_conv_b", "mod")
    gr = {k: [None] * depth for k in stacked + ("ffn_w_up", "ffn_w_down")}
    told = lambda stage: done(stage, gr) if done is not None else 0.0
    told_later = lambda stage, after: later(stage, after) if later is not None else 0.0
    for i in reversed(range(depth)):
        x0, h, pj, act, aux, y, x1, h2, p, (a, ga, gb), y2 = saved[i]
        sh_m, sc_m, gt_m, sh_f, sc_f, gt_f = (mod[i, j:j + 1] for j in range(6))
        g0, g1, g2, g3 = (w["norm_g"][i, j:j + 1] for j in range(4))
        tag = f"_l{i}"
        dy2, d_g3, d_gt_f = _post_bwd(dx, y2, g3, gt_f, "post_ffn_b" + tag)
        da = _mm(dy2, w["ffn_w_down"][i], tb=True, name="ffn_down_dx" + tag)
        gr["ffn_w_down"][i] = _mm(a, dy2, ta=True, name="ffn_down_dw" + tag)
        conv_w = w["ffn_conv_w"][i] + (told_later("l1", da) if i == 0 else 0.0)
        dp, dcw, dcb = _ffn_mid_bwd(da, p, ga, gb, conv_w, "ffn_mid_b" + tag)
        gr["ffn_conv_w"][i], gr["ffn_conv_b"][i] = _cat(dcw[0], dcw[1]), _cat(dcb[0], dcb[1])[0]
        dh2 = _mm(dp, w["ffn_w_up"][i], tb=True, a_parts=2, w_slots=N_CHIP, name="ffn_up_dx" + tag)
        gr["ffn_w_up"][i] = _mm(h2, dp, ta=True, b_parts=2, out_slots=N_CHIP, name="ffn_up_dw" + tag)
        dx1, d_g2, d_sc_f, d_sh_f = _norm_mod_bwd(dh2, x1, g2, sc_f, dx, "norm_ffn_b" + tag)
        if i == 0:
            gt_m = gt_m + told("ffn0")
        dy, d_g1, d_gt_m = _post_bwd(dx1, y, g1, gt_m, "post_mix_b" + tag)
        if i % 2 == 0:
            dact = _mm(dy, w["rg_w_out"], tb=True, name="rg_out_dx" + tag)
            gr["rg_w_out"] = _mm(act, dy, ta=True, name="rg_out_dw" + tag)
            lam = row(w["rg_lambda"]) + told_later("ffn0", gr["rg_w_out"])
            dpj, gr["rg_conv_w"], d_cb, gr["rg_wa"], d_ba, gr["rg_wx"], d_bx, d_lam = _rg_mid_bwd(
                dact, pj, aux, w["rg_conv_w"], row(w["rg_conv_b"]), w["rg_wa"], row(w["rg_ba"]), w["rg_wx"],
                row(w["rg_bx"]), lam, "rg_mid_b" + tag)
            gr["rg_conv_b"], gr["rg_ba"], gr["rg_bx"], gr["rg_lambda"] = d_cb[0], d_ba[0], d_bx[0], d_lam[0]
            dh = _mm(dpj, w["rg_w_in"], tb=True, a_parts=2, w_slots=N_CHIP, name="rg_in_dx" + tag)
            gr["rg_w_in"] = _mm(h, dpj, ta=True, b_parts=2, out_slots=N_CHIP, name="rg_in_dw" + tag)
        else:
            dact = _mm(dy, w["gla_w_out"], tb=True, name="gla_out_dx" + tag)
            gr["gla_w_out"] = _mm(act, dy, ta=True, name="gla_out_dw" + tag)
            dpj, d_wal, d_bal, d_ng = _gla_mid_bwd(dact, pj, aux[0], aux[1], w["gla_wal"], w["gla_bal"],
                                                   row(w["gla_norm_g"]), "gla_mid_b" + tag)
            gr["gla_w_alpha"] = jnp.swapaxes(d_wal[:, :GLA_RANK, :], 0, 1).reshape(GLA_RANK, GLA_HEADS * GLA_DK)
            gr["gla_b_alpha"], gr["gla_norm_g"] = d_bal.reshape(-1), d_ng[0]
            dh = _mm(dpj, w["gla_w_in"], tb=True, name="gla_in_dx" + tag)
            gr["gla_w_in"] = _gla_unhead_cols(_mm(h, dpj, ta=True, name="gla_in_dw" + tag))
            mod = mod.at[0].add(told("l1"))
        dx, d_g0, d_sc_m, d_sh_m = _norm_mod_bwd(dh, x0, g0, sc_m, dx1, "norm_mix_b" + tag)
        gr["norm_g"][i] = jnp.concatenate([d_g0, d_g1, d_g2, d_g3], axis=0)
        gr["mod"][i] = jnp.concatenate([d_sh_m, d_sc_m, d_gt_m, d_sh_f, d_sc_f, d_gt_f], axis=0)
    for k in stacked:
        gr[k] = jnp.stack(gr[k])
    return cols, dx, gr


ADA_ROWS = 16


def _ada_fwd(c16, ada_w, ada_b, name):
    depth, d, n = ada_w.shape
    tn = _tile(n, (512, 256, 128))

    def body(c_ref, w_ref, b_ref, o_ref):
        cv = c_ref[...]
        o_ref[0] = _dot_nn(cv * _sigmoid(cv), w_ref[0]) + b_ref[0]

    return pl.pallas_call(
        body, grid=(depth, n // tn),
        in_specs=[pl.BlockSpec((ADA_ROWS, d), lambda l, j: (0, 0)), pl.BlockSpec((1, d, tn), lambda l, j: (l, 0, j)),
                  pl.BlockSpec((1, 1, tn), lambda l, j: (l, 0, j))],
        out_specs=pl.BlockSpec((1, ADA_ROWS, tn), lambda l, j: (l, 0, j)),
        out_shape=jax.ShapeDtypeStruct((depth, ADA_ROWS, n), F32),
        compiler_params=_params("parallel", "parallel"), name=name,
    )(c16, ada_w, ada_b)


def _ada_bwd(c16, dmod16, name):
    depth, _, n = dmod16.shape
    d = c16.shape[1]
    tn = _tile(n, (512, 256, 128))

    def body(c_ref, dm_ref, o_ref):
        cv = c_ref[...]
        o_ref[0] = _dot_tn(cv * _sigmoid(cv), dm_ref[0])

    return pl.pallas_call(
        body, grid=(depth, n // tn),
        in_specs=[pl.BlockSpec((ADA_ROWS, d), lambda l, j: (0, 0)), pl.BlockSpec((1, ADA_ROWS, tn), lambda l, j: (l, 0, j))],
        out_specs=pl.BlockSpec((1, d, tn), lambda l, j: (l, 0, j)),
        out_shape=jax.ShapeDtypeStruct((depth, d, n), F32),
        compiler_params=_params("parallel", "parallel"), name=name,
    )(c16, dmod16)


PACK_COLS = 1024
_ANY = pl.BlockSpec(memory_space=pl.ANY)
_VMEM = pl.BlockSpec(memory_space=pltpu.VMEM)


def _place():
    return lax.axis_index("x"), lax.axis_index("y"), lax.axis_index("c")


def _other_chips(x, y):
    return [(1 - x, y), (x, 1 - y), (1 - x, 1 - y)]


def _rcopy(src, dst, send_sems, recv_sems, k, peer):
    return pltpu.make_async_remote_copy(src_ref=src, dst_ref=dst, send_sem=send_sems.at[k], recv_sem=recv_sems.at[k],
                                        device_id=peer, device_id_type=MESH)


def _all_gather_8(v, name):
    r, cc = v.shape

    def body(v_ref, out_ref, send_sems, recv_sems, local_sem):
        x, y, c = _place()
        me = 4 * x + 2 * y + c
        mine = pltpu.make_async_copy(v_ref, out_ref.at[me], local_sem)
        mine.start()
        peers = []
        for k in range(1, N_DEV):
            px = 1 - x if k & 4 else x
            py = 1 - y if k & 2 else y
            pc = 1 - c if k & 1 else c
            peers.append((px, py, pc))
        sends = [_rcopy(v_ref, out_ref.at[me], send_sems, recv_sems, k, p) for k, p in enumerate(peers)]
        for cp in sends:
            cp.start()
        for k, (px, py, pc) in enumerate(peers):
            _rcopy(v_ref, out_ref.at[4 * px + 2 * py + pc], send_sems, recv_sems, k, (px, py, pc)).wait_recv()
        for cp in sends:
            cp.wait_send()
        mine.wait()

    return pl.pallas_call(
        body, in_specs=[_VMEM], out_specs=_VMEM, out_shape=jax.ShapeDtypeStruct((N_DEV, r, cc), v.dtype),
        scratch_shapes=[pltpu.SemaphoreType.DMA((N_DEV - 1,)), pltpu.SemaphoreType.DMA((N_DEV - 1,)), pltpu.SemaphoreType.DMA],
        compiler_params=pltpu.CompilerParams(vmem_limit_bytes=VMEM_LIMIT), name=name,
    )(v)


def _gather_chips(shards, name):
    n = len(shards)
    per = 2 * (N_CHIP - 1)

    def body(*refs):
        ins, outs, (send_sems, recv_sems) = refs[:n], refs[n:2 * n], refs[2 * n:]
        x, y, c = _place()
        chip = 2 * x + y
        chips = _other_chips(x, y)
        rows = [(pl.ds(c * (r.shape[0] // 2), r.shape[0] // 2), pl.ds((1 - c) * (r.shape[0] // 2), r.shape[0] // 2)) for r in ins]
        first = [_rcopy(ins[i].at[rows[i][0]], outs[i].at[chip, rows[i][0]], send_sems, recv_sems, per * i + j, (px, py, c))
                 for i in range(n) for j, (px, py) in enumerate(chips)]
        for cp in first:
            cp.start()
        passed = []
        for i in range(n):
            for j, (px, py) in enumerate(chips):
                landed = outs[i].at[2 * px + py, rows[i][0]]
                _rcopy(ins[i].at[rows[i][0]], landed, send_sems, recv_sems, per * i + j, (px, py, c)).wait_recv()
                fw = _rcopy(landed, landed, send_sems, recv_sems, per * i + N_CHIP - 1 + j, (x, y, 1 - c))
                fw.start()
                passed.append(fw)
        for i in range(n):
            for j, (px, py) in enumerate(chips):
                landed = outs[i].at[2 * px + py, rows[i][1]]
                _rcopy(landed, landed, send_sems, recv_sems, per * i + N_CHIP - 1 + j, (x, y, 1 - c)).wait_recv()
        for cp in first + passed:
            cp.wait_send()

    return pl.pallas_call(
        body, in_specs=[_ANY] * n, out_specs=[_ANY] * n,
        out_shape=[jax.ShapeDtypeStruct((N_CHIP,) + sh.shape, sh.dtype) for sh in shards],
        scratch_shapes=[pltpu.SemaphoreType.DMA((per * n,)), pltpu.SemaphoreType.DMA((per * n,))], name=name,
    )(*shards)


def _pair_exchange(gs, name):
    n = len(gs)

    def body(*refs):
        ins, outs, (send_sems, recv_sems) = refs[:n], refs[n:2 * n], refs[2 * n:]
        x, y, c = _place()
        copies = []
        for i in range(n):
            half = ins[i].shape[1] // 2
            copies.append(_rcopy(ins[i].at[:, pl.ds((1 - c) * half, half)], outs[i], send_sems, recv_sems, i, (x, y, 1 - c)))
        for cp in copies:
            cp.start()
        for cp in copies:
            cp.wait()

    return pl.pallas_call(
        body, in_specs=[_ANY] * n, out_specs=[_ANY] * n,
        out_shape=[jax.ShapeDtypeStruct((g.shape[0], g.shape[1] // 2, g.shape[2]), g.dtype) for g in gs],
        scratch_shapes=[pltpu.SemaphoreType.DMA((n,)), pltpu.SemaphoreType.DMA((n,))], name=name,
    )(*gs)


_ROW_TILES = (640, 512, 352, 256, 128, 64, 32, 16)


def _pair_sum(g, other, c_idx, name):
    n, half, cc = other.shape
    tr = _tile(half, _ROW_TILES)

    def body(c_ref, g_ref, o_ref, out_ref):
        out_ref[...] = (g_ref[...] + o_ref[...]).astype(out_ref.dtype)

    return pl.pallas_call(
        body,
        grid_spec=pltpu.PrefetchScalarGridSpec(
            num_scalar_prefetch=1, grid=(n, half // tr),
            in_specs=[pl.BlockSpec((None, None, tr, cc), lambda k, i, c_ref: (k, c_ref[0], i, 0)),
                      pl.BlockSpec((None, tr, cc), lambda k, i, c_ref: (k, i, 0))],
            out_specs=pl.BlockSpec((None, tr, cc), lambda k, i, c_ref: (k, i, 0))),
        out_shape=jax.ShapeDtypeStruct((n, half, cc), BF16),
        compiler_params=_params("parallel", "parallel"), name=name,
    )(c_idx, g.reshape(n, 2, half, cc), other)


def _chip_exchange(ps, name):
    n = len(ps)
    per = N_CHIP - 1

    def body(*refs):
        ins, outs, (send_sems, recv_sems) = refs[:n], refs[n:2 * n], refs[2 * n:]
        x, y, c = _place()
        chip = 2 * x + y
        chips = _other_chips(x, y)
        sends = [_rcopy(ins[i].at[2 * px + py], outs[i].at[chip], send_sems, recv_sems, per * i + j, (px, py, c))
                 for i in range(n) for j, (px, py) in enumerate(chips)]
        for cp in sends:
            cp.start()
        for i in range(n):
            for j, (px, py) in enumerate(chips):
                _rcopy(ins[i].at[chip], outs[i].at[2 * px + py], send_sems, recv_sems, per * i + j, (px, py, c)).wait_recv()
        for cp in sends:
            cp.wait_send()

    return pl.pallas_call(
        body, in_specs=[_ANY] * n, out_specs=[_ANY] * n, out_shape=[jax.ShapeDtypeStruct(p.shape, p.dtype) for p in ps],
        scratch_shapes=[pltpu.SemaphoreType.DMA((per * n,)), pltpu.SemaphoreType.DMA((per * n,))], name=name,
    )(*ps)


_HBM = pl.BlockSpec(memory_space=pltpu.HBM)
_SEM = pl.BlockSpec(memory_space=pltpu.SEMAPHORE)
_DATAFLOW = pltpu.SideEffectType.DATAFLOW_SIDE_EFFECTING


def _split_copies(srcs, lands, send_sems, recv_sems, mode, arriving):
    x, y, c = _place()
    chip = 2 * x + y
    out = []
    for i, (src, land) in enumerate(zip(srcs, lands)):
        if mode == "pair":
            half = src.shape[1] // 2
            out.append(_rcopy(src.at[:, pl.ds((1 - c) * half, half)], land, send_sems, recv_sems, i, (x, y, 1 - c)))
            continue
        for j, (px, py) in enumerate(_other_chips(x, y)):
            there = 2 * px + py
            part = src.at[there] if mode == "slots" else src
            out.append(_rcopy(part, land.at[there if arriving else chip], send_sems, recv_sems, (N_CHIP - 1) * i + j, (px, py, c)))
    return out


def _land_shape(src, mode):
    if mode == "pair":
        return (src.shape[0], src.shape[1] // 2, src.shape[2])
    return (N_CHIP,) + (src.shape[1:] if mode == "slots" else src.shape)


def _send_start(srcs, mode, name):
    n = len(srcs)
    n_sem = n if mode == "pair" else (N_CHIP - 1) * n
    lands = [lax.empty(_land_shape(s, mode), s.dtype) for s in srcs]

    def body(*refs):
        ins, zones, (send_sems, recv_sems) = refs[:n], refs[n:2 * n], refs[2 * n:2 * n + 2]
        for cp in _split_copies(ins, zones, send_sems, recv_sems, mode, False):
            cp.start()
        refs[-1][...] = jnp.zeros_like(refs[-1])

    hbm = lambda a: pltpu.HBM(a.shape, a.dtype)
    outs = pl.pallas_call(
        body, name=name, in_specs=[_HBM] * (2 * n),
        out_shape=(pltpu.SemaphoreType.DMA((n_sem,)), pltpu.SemaphoreType.DMA((n_sem,)), *[hbm(a) for a in srcs],
                   *[hbm(a) for a in lands], jax.ShapeDtypeStruct((SUBLANES, LANES), F32)),
        out_specs=(_SEM, _SEM, *[_HBM] * (2 * n), _VMEM), input_output_aliases={i: 2 + i for i in range(2 * n)},
        compiler_params=pltpu.CompilerParams(has_side_effects=_DATAFLOW),
    )(*[pltpu.with_memory_space_constraint(a, pltpu.HBM) for a in list(srcs) + lands])
    return (outs[0], outs[1], list(outs[2:2 + n]), list(outs[2 + n:2 + 2 * n])), outs[-1]


def _send_wait(state, after, mode, name):
    send_sems, recv_sems, srcs, lands = state
    n = len(srcs)

    def body(*refs):
        ins, zones, (send_s, recv_s) = refs[:n], refs[n:2 * n], refs[2 * n:2 * n + 2]
        for cp in _split_copies(ins, zones, send_s, recv_s, mode, True):
            cp.wait_send()
            cp.wait_recv()

    hbm = lambda a: pltpu.HBM(a.shape, a.dtype)
    outs = pl.pallas_call(
        body, name=name, in_specs=[_HBM] * (2 * n) + [_SEM, _SEM, _ANY],
        out_shape=tuple(hbm(a) for a in srcs + lands), out_specs=tuple([_HBM] * (2 * n)),
        input_output_aliases={i: i for i in range(2 * n)},
        compiler_params=pltpu.CompilerParams(has_side_effects=_DATAFLOW),
    )(*srcs, *lands, send_sems, recv_sems, after)
    return list(outs[n:])


def _sum_lead(v, name):
    n, r, cc = v.shape
    tr = _tile(r, _ROW_TILES + (8,))

    def body(v_ref, o_ref):
        acc = v_ref[0].astype(F32)
        for k in range(1, n):
            acc = acc + v_ref[k].astype(F32)
        o_ref[...] = acc

    return pl.pallas_call(
        body, grid=(r // tr,), in_specs=[pl.BlockSpec((n, tr, cc), lambda i: (0, i, 0))],
        out_specs=pl.BlockSpec((tr, cc), lambda i: (i, 0)), out_shape=jax.ShapeDtypeStruct((r, cc), F32),
        compiler_params=_params("parallel"), name=name,
    )(v)


def _chip_sum(arrived, mine, chip_idx, name):
    n, r, cc = arrived.shape
    tr = _tile(r, _ROW_TILES)

    def body(chip_ref, a_ref, m_ref, o_ref):
        acc = jnp.zeros((tr, cc), F32)
        for k in range(n):
            acc = acc + jnp.where(chip_ref[0] == k, m_ref[...], a_ref[k]).astype(F32)
        o_ref[...] = acc

    return pl.pallas_call(
        body,
        grid_spec=pltpu.PrefetchScalarGridSpec(
            num_scalar_prefetch=1, grid=(r // tr,),
            in_specs=[pl.BlockSpec((n, tr, cc), lambda i, chip_ref: (0, i, 0)),
                      pl.BlockSpec((None, tr, cc), lambda i, chip_ref: (chip_ref[0], i, 0))],
            out_specs=pl.BlockSpec((tr, cc), lambda i, chip_ref: (i, 0))),
        out_shape=jax.ShapeDtypeStruct((r, cc), F32), compiler_params=_params("parallel"), name=name,
    )(chip_idx, arrived, mine)


def _pair_share(reds, name):
    n = len(reds)

    def body(*refs):
        ins, outs, (send_sems, recv_sems) = refs[:n], refs[n:2 * n], refs[2 * n:]
        x, y, c = _place()
        copies = [_rcopy(ins[i], outs[i].at[c], send_sems, recv_sems, i, (x, y, 1 - c)) for i in range(n)]
        for cp in copies:
            cp.start()
        for i in range(n):
            _rcopy(ins[i], outs[i].at[1 - c], send_sems, recv_sems, i, (x, y, 1 - c)).wait_recv()
        for cp in copies:
            cp.wait_send()

    return pl.pallas_call(
        body, in_specs=[_ANY] * n, out_specs=[_ANY] * n, out_shape=[jax.ShapeDtypeStruct((2,) + r.shape, r.dtype) for r in reds],
        scratch_shapes=[pltpu.SemaphoreType.DMA((n,)), pltpu.SemaphoreType.DMA((n,))], name=name,
    )(*reds)


def _pack(arrs, rows_multiple, dtype):
    flat = jnp.concatenate([a.reshape(-1).astype(dtype) for a in arrs])
    unit = rows_multiple * PACK_COLS
    total = -(-flat.shape[0] // unit) * unit
    return jnp.pad(flat, (0, total - flat.shape[0])).reshape(-1, PACK_COLS)


def _unpack(buf, shapes):
    lead = buf.shape[:-2]
    flat = buf.reshape(*lead, -1)
    out, off = [], 0
    for shp in shapes:
        n = 1
        for s in shp:
            n *= s
        out.append(flat[..., off:off + n].reshape(*lead, *shp))
        off += n
    return out


def _join_shards(parts, axis):
    moved = jnp.moveaxis(parts, 0, axis)
    shp = list(moved.shape)
    shp[axis:axis + 2] = [shp[axis] * shp[axis + 1]]
    return moved.reshape(shp)


def _my_shard(full, axis, chip):
    n = full.shape[axis] // N_CHIP
    return lax.dynamic_slice_in_dim(full, chip * n, n, axis)


SMALL = {"norm_g": 2, "ffn_conv_w": 2, "rg_conv_w": 2, "gla_w_alpha": 2, "gla_b_alpha": 1, "gla_norm_g": 1,
         "ada_b": None, "ffn_conv_b": None, "rg_conv_b": None, "rg_ba": None, "rg_bx": None, "rg_lambda": None}
BIG = {"rg_w_in": True, "rg_wa": False, "rg_wx": False, "rg_w_out": False, "ffn_w_up": True, "ffn_w_down": False,
       "gla_w_in": True, "gla_w_out": False}
WEIGHTS = ["ada_w", "ada_b", "norm_g", "ffn_w_up", "ffn_conv_w", "ffn_conv_b", "ffn_w_down", "rg_w_in", "rg_conv_w", "rg_conv_b",
           "rg_wa", "rg_ba", "rg_wx", "rg_bx", "rg_lambda", "rg_w_out", "gla_w_in", "gla_w_alpha", "gla_b_alpha", "gla_norm_g",
           "gla_w_out"]


def kernel(x, c, ada_w, ada_b, norm_g, ffn_w_up, ffn_conv_w, ffn_conv_b, ffn_w_down, rg_w_in, rg_conv_w, rg_conv_b, rg_wa, rg_ba, rg_wx, rg_bx, rg_lambda, rg_w_out, gla_w_in, gla_w_alpha, gla_b_alpha, gla_norm_g, gla_w_out, loss_target, m_ada_w, m_ada_b, m_norm_g, m_ffn_w_up, m_ffn_conv_w, m_ffn_conv_b, m_ffn_w_down, m_rg_w_in, m_rg_conv_w, m_rg_conv_b, m_rg_wa, m_rg_ba, m_rg_wx, m_rg_bx, m_rg_lambda, m_rg_w_out, m_gla_w_in, m_gla_w_alpha, m_gla_b_alpha, m_gla_norm_g, m_gla_w_out, v_ada_w, v_ada_b, v_norm_g, v_ffn_w_up, v_ffn_conv_w, v_ffn_conv_b, v_ffn_w_down, v_rg_w_in, v_rg_conv_w, v_rg_conv_b, v_rg_wa, v_rg_ba, v_rg_wx, v_rg_bx, v_rg_lambda, v_rg_w_out, v_gla_w_in, v_gla_w_alpha, v_gla_b_alpha, v_gla_norm_g, v_gla_w_out):
    wts = dict(ada_w=ada_w, ada_b=ada_b, norm_g=norm_g, ffn_w_up=ffn_w_up, ffn_conv_w=ffn_conv_w, ffn_conv_b=ffn_conv_b,
               ffn_w_down=ffn_w_down, rg_w_in=rg_w_in, rg_conv_w=rg_conv_w, rg_conv_b=rg_conv_b, rg_wa=rg_wa, rg_ba=rg_ba,
               rg_wx=rg_wx, rg_bx=rg_bx, rg_lambda=rg_lambda, rg_w_out=rg_w_out, gla_w_in=gla_w_in, gla_w_alpha=gla_w_alpha,
               gla_b_alpha=gla_b_alpha, gla_norm_g=gla_norm_g, gla_w_out=gla_w_out)
    mom1 = dict(ada_w=m_ada_w, ada_b=m_ada_b, norm_g=m_norm_g, ffn_w_up=m_ffn_w_up, ffn_conv_w=m_ffn_conv_w,
                ffn_conv_b=m_ffn_conv_b, ffn_w_down=m_ffn_w_down, rg_w_in=m_rg_w_in, rg_conv_w=m_rg_conv_w,
                rg_conv_b=m_rg_conv_b, rg_wa=m_rg_wa, rg_ba=m_rg_ba, rg_wx=m_rg_wx, rg_bx=m_rg_bx, rg_lambda=m_rg_lambda,
                rg_w_out=m_rg_w_out, gla_w_in=m_gla_w_in, gla_w_alpha=m_gla_w_alpha, gla_b_alpha=m_gla_b_alpha,
                gla_norm_g=m_gla_norm_g, gla_w_out=m_gla_w_out)
    mom2 = dict(ada_w=v_ada_w, ada_b=v_ada_b, norm_g=v_norm_g, ffn_w_up=v_ffn_w_up, ffn_conv_w=v_ffn_conv_w,
                ffn_conv_b=v_ffn_conv_b, ffn_w_down=v_ffn_w_down, rg_w_in=v_rg_w_in, rg_conv_w=v_rg_conv_w,
                rg_conv_b=v_rg_conv_b, rg_wa=v_rg_wa, rg_ba=v_rg_ba, rg_wx=v_rg_wx, rg_bx=v_rg_bx, rg_lambda=v_rg_lambda,
                rg_w_out=v_rg_w_out, gla_w_in=v_gla_w_in, gla_w_alpha=v_gla_w_alpha, gla_b_alpha=v_gla_b_alpha,
                gla_norm_g=v_gla_norm_g, gla_w_out=v_gla_w_out)
    xi, yi, ci = _place()
    chip, me = 2 * xi + yi, 4 * xi + 2 * yi + ci
    d = x.shape[-1]
    depth = ada_w.shape[0]
    n_ada = ada_w.shape[-1]
    sharded_small = [k for k, ax in SMALL.items() if ax is not None]

    sm = _all_gather_8(_pack([c] + [wts[k] for k in sharded_small], SUBLANES, F32), "gather_small")
    c_all = sm[:, 0, :]
    parts = _unpack(sm[0::2], [c.shape] + [wts[k].shape for k in sharded_small])[1:]
    full = {k: _join_shards(p, SMALL[k]) for k, p in zip(sharded_small, parts)}
    for k, ax in SMALL.items():
        if ax is None:
            full[k] = wts[k]

    c16 = jnp.pad(c_all, ((0, ADA_ROWS - N_DEV), (0, 0)))
    ada_b_mine = lax.dynamic_slice_in_dim(ada_b, chip * n_ada, n_ada, 1)[:, None, :]
    mod_cols = _ada_fwd(c16, ada_w, ada_b_mine, "ada_fwd")
    mod_all = _all_gather_8(mod_cols.reshape(-1, PACK_COLS), "gather_mod")[0::2].reshape(N_CHIP, depth, ADA_ROWS, n_ada)
    mod = jnp.swapaxes(lax.dynamic_index_in_dim(mod_all, me, 2, keepdims=False), 0, 1).reshape(depth, 6, d)

    items = [(k, l) for k in BIG for l in range(wts[k].shape[0])]
    stage_of = lambda k, l: "rg" if k.startswith("rg_") else ("ffn0" if (k.startswith("ffn_") and l == 0) else "l1")
    staged = {st: [it for it in items if stage_of(*it) == st] for st in ("rg", "ffn0", "l1")}
    staged["l1"].sort(key=lambda it: not it[0].startswith("gla_"))
    shard = lambda k, l: wts[k][l].reshape(-1, wts[k].shape[-1]).astype(BF16)
    own = lambda got, mine: [lax.dynamic_update_index_in_dim(g, m, chip, 0) for g, m in zip(got, mine)]
    rows_joined = lambda v: v.reshape(-1, v.shape[-1])

    def placed(its, slots):
        out = {"ffn_w_up": {}, "ffn_w_down": {}}
        for (k, l), v in zip(its, slots):
            if k == "ffn_w_up":
                out[k][l] = v
            elif k == "ffn_w_down":
                out[k][l] = rows_joined(v)
            elif k in ("rg_wa", "rg_wx"):
                out[k] = _slots_to_block_rows(v, RG_BLOCKS)
            elif k == "gla_w_in":
                out[k] = _gla_head_cols(_from_col_slots(v))
            else:
                out[k] = v if BIG[k] else rows_joined(v)
        return out

    after_mod = (mod[0, 0, 0] * 0.0).astype(BF16)
    sh_rg = [shard(k, l) + after_mod for k, l in staged["rg"]]
    local = {k: (v if k in ("norm_g", "ffn_conv_w", "ffn_conv_b") else v[0]) for k, v in full.items()}
    local["gla_wal"], local["gla_bal"] = _gla_alpha_heads(local["gla_w_alpha"], local["gla_b_alpha"])
    local.update(placed(staged["rg"], own(_gather_chips(sh_rg, "gather_weights_rg"), sh_rg)))
    sh_late, flying = {}, {}
    after_rg = (local["rg_w_out"][0, 0].astype(F32) * 0.0).astype(BF16)
    sh_late["ffn0"] = [shard(k, l) + after_rg for k, l in staged["ffn0"]]
    flying["ffn0"], tok = _send_start(sh_late["ffn0"], "whole", "weights_ffn0_start")
    sh_late["l1"] = [shard(k, l) + tok[0, 0].astype(BF16) for k, l in staged["l1"]]
    flying["l1"], tok2 = _send_start(sh_late["l1"], "whole", "weights_l1_start")
    mod = mod + (tok[0, 0] + tok2[0, 0])

    def fetch(stage, after):
        got = _send_wait(flying[stage], after, "whole", f"weights_{stage}_wait")
        return placed(staged[stage], own(got, sh_late[stage]))

    c_idx = ci.reshape(1).astype(jnp.int32)
    gslots, paired, psums, sent = {}, {}, {}, {}

    def grad_slots(gr, k, l):
        g = gr[k][l] if k in ("ffn_w_up", "ffn_w_down") else gr[k]
        if k in ("rg_wa", "rg_wx"):
            return _block_rows_to_slots(g)
        if k == "gla_w_in":
            return _col_slots(g)
        return g if BIG[k] else g.reshape(N_CHIP, -1, g.shape[-1])

    def done(stage, gr):
        gslots[stage] = [grad_slots(gr, k, l) for k, l in staged[stage]]
        paired[stage], token = _send_start(gslots[stage], "pair", f"grads_{stage}_pair_start")
        return token[0, 0]

    def later(stage, after):
        theirs = _send_wait(paired[stage], after, "pair", f"grads_{stage}_pair_wait")
        psums[stage] = [_pair_sum(g, t, c_idx, f"grads_pair_sum_{k}{l}") for (k, l), g, t in zip(staged[stage], gslots[stage], theirs)]
        sent[stage], token = _send_start(psums[stage], "slots", f"grads_{stage}_start")
        return token[0, 0]

    cols, grad_x, gr = _local_step(x[0], loss_target[0], mod, local, fetch, done, later)
    loss = lax.psum(0.5 * jnp.sum(cols) / d, ("x", "y", "c"))

    small_names = [k for k in SMALL if k != "ada_b"]
    gs = _all_gather_8(_pack([gr[k] for k in small_names] + [gr["mod"]], SUBLANES, F32), "gather_small_grads")
    small_shapes = [full[k].shape for k in small_names] + [(depth, 6 * d)]
    *small_sum, g_ada_b = _unpack(_sum_lead(gs, "sum_small_grads"), small_shapes)
    grads = dict(zip(small_names, small_sum))
    grads["ada_b"] = g_ada_b
    for k in sharded_small:
        grads[k] = _my_shard(grads[k], SMALL[k], chip)
    dmod_all = _unpack(gs, small_shapes)[-1].reshape(N_DEV, depth, N_CHIP, n_ada)
    dmod_mine = jnp.swapaxes(lax.dynamic_index_in_dim(dmod_all, chip, 2, keepdims=False), 0, 1)
    g_ada_w = _ada_bwd(c16, jnp.pad(dmod_mine, ((0, 0), (0, ADA_ROWS - N_DEV), (0, 0))), "ada_bwd")

    gslots["rg"] = [grad_slots(gr, k, l) for k, l in staged["rg"]]
    theirs = _pair_exchange(gslots["rg"], "grads_rg_pair_exchange")
    psums["rg"] = [_pair_sum(g, t, c_idx, f"grads_pair_sum_{k}{l}") for (k, l), g, t in zip(staged["rg"], gslots["rg"], theirs)]
    sent["rg"], rg_sent = _send_start(psums["rg"], "slots", "grads_rg_start")
    chip_idx = chip.reshape(1).astype(jnp.int32)
    delta, new_m, new_v = {}, {}, {}

    def reduce_and_update(stages, after, dep):
        its = [(st, n) for st in stages for n in range(len(staged[st]))]
        arrived = {st: _send_wait(sent[st], after, "slots", f"grads_{st}_wait") for st in stages}
        halves = [_chip_sum(arrived[st][n], psums[st][n], chip_idx, "grads_chip_sum_%s%d" % staged[st][n]) for st, n in its]
        shared = _pair_share(halves, "grads_pair_share_" + stages[0])
        reduced = [lax.dynamic_update_index_in_dim(s2, h, ci, 0).reshape(-1, h.shape[-1]) for s2, h in zip(shared, halves)]
        last = None
        for k in BIG:
            gs_k = [g for (st, n), g in zip(its, reduced) if staged[st][n][0] == k]
            if gs_k:
                last = update(k, gs_k, dep)
        return last

    def update(k, gs_k, dep=None):
        shp = wts[k].shape
        if k == "gla_w_in":
            view, back = (lambda a: jnp.swapaxes(a, 1, 2)), (lambda o: jnp.swapaxes(o, 1, 2))
            gs_k = [g.T for g in gs_k]
        else:
            view, back = (lambda a: a.reshape(a.shape[0], -1, a.shape[-1])), (lambda o: o.reshape(shp))
        outs = _adamw(view(wts[k]), gs_k, view(mom1[k]), view(mom2[k]), "adamw_" + k, dep)
        grads[k], delta[k], new_m[k], new_v[k] = (back(o) for o in outs)
        return new_v[k]

    done_late = reduce_and_update(("ffn0", "l1"), grad_x, rg_sent)
    update("ada_w", g_ada_w, rg_sent)
    small_shard_shapes = [wts[k].shape for k in SMALL]
    packed = [_pack([src[k] for k in SMALL], SUBLANES, F32) for src in (wts, grads, mom1, mom2)]
    outs = _adamw(packed[0][None], [packed[1]], packed[2][None], packed[3][None], "adamw_small", rg_sent)
    for dst, o in zip((delta, new_m, new_v), outs[1:]):
        for k, a in zip(SMALL, _unpack(o[0], small_shard_shapes)):
            dst[k] = a
    reduce_and_update(("rg",), done_late, None)

    return (loss, grad_x[None], *[grads[k] for k in WEIGHTS], *[delta[k] for k in WEIGHTS], *[new_m[k] for k in WEIGHTS],
            *[new_v[k] for k in WEIGHTS])
```

```python
import jax
import jax.numpy as jnp
from jax import lax
from jax.experimental import pallas as pl
from jax.experimental.pallas import tpu as pltpu

F32 = jnp.float32
BF16 = jnp.bfloat16
MXU_DTYPE = BF16

EPS = 1e-6
RG_C = 8.0
RG_BLOCKS = 4
RG_CONV = 4
GLA_HEADS = 4
GLA_TAU = 16.0
GLA_CHUNK = 64
GLA_RANK = 16
FFN_CONV = 3
ADAM_LR = 0.001
ADAM_B1 = 0.9
ADAM_B2 = 0.999
ADAM_EPS = 1e-08
ADAM_WD = 0.01
ADAM_STEP = 10

LANES = 128
SUBLANES = 8
VMEM_LIMIT = 56 * 1024 * 1024
CB = 256
MESH = pl.DeviceIdType.MESH
N_DEV = 8
N_CHIP = 4


def _params(*sem):
    return pltpu.CompilerParams(dimension_semantics=sem, vmem_limit_bytes=VMEM_LIMIT)


def _tile(dim, prefs):
    for p in prefs:
        if dim % p == 0:
            return p
    return dim


def _dot(a, b, dims):
    return lax.dot_general(a.astype(MXU_DTYPE), b.astype(MXU_DTYPE), (dims, ((), ())), preferred_element_type=F32)


def _dot_nn(a, b):
    return _dot(a, b, ((1,), (0,)))


def _dot_nt(a, b):
    return _dot(a, b, ((1,), (1,)))


def _dot_tn(a, b):
    return _dot(a, b, ((0,), (0,)))


def _mm(a, b, *, ta=False, tb=False, a_parts=1, b_parts=1, w_slots=1, out_slots=1, out_dtype=F32, tm_max=1024, name):
    if ta:
        k_dim, m_dim = a.shape
        n_dim = b.shape[-1] * b_parts
    else:
        m_dim, k_dim = a.shape[-2], a.shape[-1] * a_parts
        n_dim = b.shape[-2] if tb else b.shape[-1] * w_slots
    n_unit = n_dim // max(b_parts, out_slots, 1 if tb else w_slots)
    k_unit = k_dim // max(a_parts, w_slots if tb else 1)
    tm = _tile(m_dim, tuple(t for t in (1024, 1408, 512, 256, 128) if t <= max(tm_max, 128)))
    tn = _tile(n_unit, (1024, 1408, 896, 512, 256, 128))
    tk = _tile(k_unit, (1024, 1408, 896, 512, 256, 128))
    nk = k_dim // tk
    dims = ((0 if ta else 1,), (1 if tb else 0,))

    def spec(shape, parts, total, tile, col_grid, row_grid):
        per = total // parts // tile

        def index(i, j, k):
            g = {"i": i, "j": j, "k": k}
            col, row = g[col_grid], g[row_grid]
            return (row, col) if parts == 1 else (col // per, row, col % per)

        return pl.BlockSpec(shape if parts == 1 else (None,) + shape, index)

    def body(a_ref, b_ref, o_ref, *acc):
        prod = _dot(a_ref[...], b_ref[...], dims)
        if nk == 1:
            o_ref[...] = prod.astype(o_ref.dtype)
            return
        acc_ref, k = acc[0], pl.program_id(2)

        @pl.when(k == 0)
        def _():
            acc_ref[...] = prod

        @pl.when((k > 0) & (k < nk - 1))
        def _():
            acc_ref[...] += prod

        @pl.when(k == nk - 1)
        def _():
            o_ref[...] = (acc_ref[...] + prod).astype(o_ref.dtype)

    if ta:
        a_spec = spec((tk, tm), 1, m_dim, tm, "i", "k")
        b_spec = spec((tk, tn), b_parts, n_dim, tn, "j", "k")
    elif tb:
        a_spec = spec((tm, tk), a_parts, k_dim, tk, "k", "i")
        b_spec = spec((tn, tk), w_slots, k_dim, tk, "k", "j")
    else:
        a_spec = spec((tm, tk), a_parts, k_dim, tk, "k", "i")
        b_spec = spec((tk, tn), w_slots, n_dim, tn, "j", "k")
    out_shape = (m_dim, n_dim) if out_slots == 1 else (out_slots, m_dim, n_dim // out_slots)
    return pl.pallas_call(
        body,
        grid=(m_dim // tm, n_dim // tn, nk),
        in_specs=[a_spec, b_spec],
        out_specs=spec((tm, tn), out_slots, n_dim, tn, "j", "i"),
        out_shape=jax.ShapeDtypeStruct(out_shape, out_dtype),
        scratch_shapes=[pltpu.VMEM((tm, tn), F32)] if nk > 1 else [],
        compiler_params=_params("parallel", "parallel", "arbitrary"),
        name=name,
    )(a, b)


def _row_specs(s, d, ts):
    return pl.BlockSpec((ts, d), lambda i: (i, 0)), pl.BlockSpec((1, d), lambda i: (0, 0))


def _norm_mod_fwd(x, g, sc, sh, name):
    s, d = x.shape
    ts = _tile(s, (512,))

    def body(x_ref, g_ref, sc_ref, sh_ref, h_ref):
        xv = x_ref[...]
        r = lax.rsqrt(jnp.mean(xv * xv, axis=-1, keepdims=True) + EPS)
        h_ref[...] = (((xv * r) * g_ref[...]) * (1.0 + sc_ref[...]) + sh_ref[...]).astype(h_ref.dtype)

    row, vec = _row_specs(s, d, ts)
    return pl.pallas_call(
        body, grid=(s // ts,), in_specs=[row, vec, vec, vec], out_specs=row,
        out_shape=jax.ShapeDtypeStruct((s, d), MXU_DTYPE), compiler_params=_params("parallel"), name=name,
    )(x, g, sc, sh)


def _norm_mod_bwd(dh, x, g, sc, dres, name):
    s, d = x.shape
    ts = _tile(s, (512,))

    def body(dh_ref, x_ref, g_ref, sc_ref, dres_ref, dx_ref, dg_ref, dsc_ref, dsh_ref, acc_ref):
        i = pl.program_id(0)

        @pl.when(i == 0)
        def _():
            acc_ref[...] = jnp.zeros_like(acc_ref)

        xv, dhv = x_ref[...], dh_ref[...]
        r = lax.rsqrt(jnp.mean(xv * xv, axis=-1, keepdims=True) + EPS)
        n = xv * r
        acc_ref[0:1, :] += jnp.sum(dhv * n, axis=0, keepdims=True)
        acc_ref[1:2, :] += jnp.sum(dhv, axis=0, keepdims=True)
        dn = dhv * ((1.0 + sc_ref[...]) * g_ref[...])
        dx_ref[...] = dres_ref[...] + r * (dn - n * jnp.mean(dn * n, axis=-1, keepdims=True))
        dg_ref[...] = (1.0 + sc_ref[...]) * acc_ref[0:1, :]
        dsc_ref[...] = g_ref[...] * acc_ref[0:1, :]
        dsh_ref[...] = acc_ref[1:2, :]

    row, vec = _row_specs(s, d, ts)
    vshape = jax.ShapeDtypeStruct((1, d), F32)
    return pl.pallas_call(
        body, grid=(s // ts,), in_specs=[row, row, vec, vec, row], out_specs=[row, vec, vec, vec],
        out_shape=[jax.ShapeDtypeStruct((s, d), F32), vshape, vshape, vshape],
        scratch_shapes=[pltpu.VMEM((SUBLANES, d), F32)], compiler_params=_params("arbitrary"), name=name,
    )(dh, x, g, sc, dres)


def _post_fwd(x, y, g, gt, name):
    s, d = x.shape
    ts = _tile(s, (512,))

    def body(x_ref, y_ref, g_ref, gt_ref, o_ref):
        yv = y_ref[...]
        r = lax.rsqrt(jnp.mean(yv * yv, axis=-1, keepdims=True) + EPS)
        o_ref[...] = x_ref[...] + gt_ref[...] * ((yv * r) * g_ref[...])

    row, vec = _row_specs(s, d, ts)
    return pl.pallas_call(
        body, grid=(s // ts,), in_specs=[row, row, vec, vec], out_specs=row,
        out_shape=jax.ShapeDtypeStruct((s, d), F32), compiler_params=_params("parallel"), name=name,
    )(x, y, g, gt)


def _post_bwd(dxn, y, g, gt, name):
    s, d = y.shape
    ts = _tile(s, (512,))

    def body(dxn_ref, y_ref, g_ref, gt_ref, dy_ref, dg_ref, dgt_ref, acc_ref):
        i = pl.program_id(0)

        @pl.when(i == 0)
        def _():
            acc_ref[...] = jnp.zeros_like(acc_ref)

        yv, dv = y_ref[...], dxn_ref[...]
        r = lax.rsqrt(jnp.mean(yv * yv, axis=-1, keepdims=True) + EPS)
        n = yv * r
        acc_ref[0:1, :] += jnp.sum(dv * n, axis=0, keepdims=True)
        dn = dv * (gt_ref[...] * g_ref[...])
        dy_ref[...] = (r * (dn - n * jnp.mean(dn * n, axis=-1, keepdims=True))).astype(dy_ref.dtype)
        dg_ref[...] = gt_ref[...] * acc_ref[0:1, :]
        dgt_ref[...] = g_ref[...] * acc_ref[0:1, :]

    row, vec = _row_specs(s, d, ts)
    vshape = jax.ShapeDtypeStruct((1, d), F32)
    return pl.pallas_call(
        body, grid=(s // ts,), in_specs=[row, row, vec, vec], out_specs=[row, vec, vec],
        out_shape=[jax.ShapeDtypeStruct((s, d), MXU_DTYPE), vshape, vshape],
        scratch_shapes=[pltpu.VMEM((SUBLANES, d), F32)], compiler_params=_params("arbitrary"), name=name,
    )(dxn, y, g, gt)


def _loss_grad(x, tgt, name):
    s, d = x.shape
    ts = _tile(s, (512,))

    def body(x_ref, t_ref, col_ref, dx_ref):
        i = pl.program_id(0)

        @pl.when(i == 0)
        def _():
            col_ref[...] = jnp.zeros_like(col_ref)

        e = x_ref[...] - t_ref[...]
        col_ref[...] += jnp.sum(e * e, axis=0, keepdims=True)
        dx_ref[...] = e * (1.0 / d)

    row, vec = _row_specs(s, d, ts)
    return pl.pallas_call(
        body, grid=(s // ts,), in_specs=[row, row], out_specs=[vec, row],
        out_shape=[jax.ShapeDtypeStruct((1, d), F32), jax.ShapeDtypeStruct((s, d), F32)],
        compiler_params=_params("arbitrary"), name=name,
    )(x, tgt)


_GELU_C = 0.7978845608028654
_GELU_A = 0.044715


def _gelu(x):
    t = jnp.tanh(_GELU_C * (x + _GELU_A * x * x * x))
    return 0.5 * x * (1.0 + t), t


def _gelu_grad(x, t):
    return 0.5 * (1.0 + t) + 0.5 * x * (1.0 - t * t) * (_GELU_C * (1.0 + 3.0 * _GELU_A * x * x))


def _sigmoid(x):
    return 1.0 / (1.0 + jnp.exp(-x))


def _log1p_pos(y):
    u = 1.0 + y
    return jnp.where(u == 1.0, y, jnp.log(u) * (y / jnp.where(u == 1.0, 1.0, u - 1.0)))


def _softplus(x):
    return jnp.maximum(x, 0.0) + _log1p_pos(jnp.exp(-jnp.abs(x)))


def _one_minus_exp(z):
    u = jnp.exp(z)
    lg = jnp.log(jnp.where(u > 0.0, u, 1.0))
    safe = (u != 1.0) & (u > 0.0)
    return jnp.where(u == 1.0, -z, jnp.where(u > 0.0, (1.0 - u) * (z / jnp.where(safe, lg, 1.0)), 1.0))


SLAB = 16


def _cat(a, b):
    return jnp.concatenate([a, b], axis=1)


def _pair_specs(shape, nb, index):
    return [pl.BlockSpec(shape, lambda j, t: index(j, t) + (j,)), pl.BlockSpec(shape, lambda j, t: index(j, t) + (j + nb,))]


def _halo_row(ts, time_of):
    return lambda j, t: (jnp.maximum(time_of(t) * (ts // SUBLANES) - 1, 0),)


def _rows_from(groups, k):
    row = lax.broadcasted_iota(jnp.int32, groups[0].shape, 0)
    turned = [pltpu.roll(g, SUBLANES - k, axis=0) for g in groups]
    return [jnp.where(row < SUBLANES - k, lo, hi) for lo, hi in zip(turned[:-1], turned[1:])]


def _ffn_mid_fwd(p, cw, cb, name):
    s, f2 = p.shape
    ts = _tile(s, (1024, 512))
    nb, nt = f2 // (2 * CB), s // ts
    n_grp = SLAB // SUBLANES

    def body(pg_ref, pv_ref, hg_ref, hv_ref, cwg_ref, cwv_ref, cbg_ref, cbv_ref, a_ref, ga_ref, gb_ref):
        t = pl.program_id(1)
        cwv, bias = _cat(cwg_ref[...], cwv_ref[...]), _cat(cbg_ref[...], cbv_ref[...])
        w0, w1, w2 = cwv[0:1], cwv[1:2], cwv[2:3]

        def slab(before, cur, r0):
            pm2, pm1 = _rows_from([before] + cur, SUBLANES - 2), _rows_from([before] + cur, SUBLANES - 1)
            u = jnp.concatenate([bias + w0 * pm2[i] + w1 * pm1[i] + w2 * cur[i] for i in range(n_grp)], axis=0)
            g, v = u[:, :CB], u[:, CB:]
            gel, th = _gelu(g)
            rows = pl.ds(r0, SLAB)
            a_ref[rows, :] = (gel * v).astype(a_ref.dtype)
            ga_ref[rows, :] = gel.astype(ga_ref.dtype)
            gb_ref[rows, :] = (v * _gelu_grad(g, th)).astype(gb_ref.dtype)

        def pieces(rows):
            blk = _cat(pg_ref[rows, :], pv_ref[rows, :])
            return [blk[i * SUBLANES:(i + 1) * SUBLANES] for i in range(blk.shape[0] // SUBLANES)]

        slab(jnp.where(t > 0, _cat(hg_ref[...], hv_ref[...]), 0.0), pieces(pl.ds(0, SLAB)), 0)

        def loop(i, carry):
            r0 = pl.multiple_of(i * SLAB, SLAB)
            got = pieces(pl.ds(pl.multiple_of(r0 - SUBLANES, SUBLANES), SLAB + SUBLANES))
            slab(got[0], got[1:], r0)
            return carry

        lax.fori_loop(1, ts // SLAB, loop, 0, unroll=2)

    fwd = lambda t: t
    out = pl.BlockSpec((ts, CB), lambda j, t: (t, j))
    shape = jax.ShapeDtypeStruct((s, f2 // 2), MXU_DTYPE)
    return pl.pallas_call(
        body, grid=(nb, nt),
        in_specs=(_pair_specs((ts, CB), nb, lambda j, t: (t,)) + _pair_specs((SUBLANES, CB), nb, _halo_row(ts, fwd))
                  + _pair_specs((FFN_CONV, CB), nb, lambda j, t: (0,)) + _pair_specs((1, CB), nb, lambda j, t: (0,))),
        out_specs=[out, out, out], out_shape=[shape, shape, shape],
        compiler_params=_params("parallel", "arbitrary"), name=name,
    )(p, p, p, p, cw, cw, cb, cb)


def _ffn_mid_bwd(da, p, ga, gb, cw, name):
    s, f2 = p.shape
    ts = _tile(s, (1024, 512))
    nb, nt = f2 // (2 * CB), s // ts
    n_slab = ts // SLAB
    n_grp = SLAB // SUBLANES
    per_trip = 2

    def body(da_ref, ga_ref, gb_ref, pg_ref, pv_ref, cwg_ref, cwv_ref, dp_ref, dcw_ref, dcb_ref, next_du, acc):
        tt = pl.program_id(1)
        cwv = _cat(cwg_ref[...], cwv_ref[...])
        w0, w1, w2 = cwv[0:1], cwv[1:2], cwv[2:3]

        @pl.when(tt == 0)
        def _():
            next_du[...] = jnp.zeros_like(next_du)
            acc[...] = jnp.zeros_like(acc)

        def slab(r0, after, sums):
            rows = pl.ds(r0, SLAB)
            dav = da_ref[rows, :]
            du = _cat(dav * gb_ref[rows, :].astype(F32), dav * ga_ref[rows, :].astype(F32))
            p0 = _cat(pg_ref[rows, :], pv_ref[rows, :])
            cur = [du[i * SUBLANES:(i + 1) * SUBLANES] for i in range(n_grp)]
            du1, du2 = _rows_from(cur + [after], 1), _rows_from(cur + [after], 2)
            dpv = jnp.concatenate([w2 * cur[i] + w1 * du1[i] + w0 * du2[i] for i in range(n_grp)], axis=0).astype(dp_ref.dtype)
            dp_ref[0, rows, :] = dpv[:, :CB]
            dp_ref[1, rows, :] = dpv[:, CB:]
            for i in range(n_grp):
                pi = p0[i * SUBLANES:(i + 1) * SUBLANES]
                parts = (cur[i], du2[i] * pi, du1[i] * pi, cur[i] * pi)
                sums = parts if sums is None else tuple(x + y for x, y in zip(sums, parts))
            return cur[0], sums

        def loop(k, after):
            sums = None
            for j in range(per_trip):
                r0 = pl.multiple_of((n_slab - 1 - (k * per_trip + j)) * SLAB, SLAB)
                after, sums = slab(r0, after, sums)
            for q, part in enumerate(sums):
                acc[q] += part
            return after

        next_du[...] = lax.fori_loop(0, n_slab // per_trip, loop, next_du[...])

        @pl.when(tt == nt - 1)
        def _():
            for half in range(2):
                cols = slice(half * CB, (half + 1) * CB)
                dcb_ref[half] = jnp.sum(acc[0][:, cols], axis=0, keepdims=True)
                for k in range(FFN_CONV):
                    dcw_ref[half, k:k + 1, :] = jnp.sum(acc[1 + k][:, cols], axis=0, keepdims=True)

    rev = lambda t: nt - 1 - t
    tile = pl.BlockSpec((ts, CB), lambda j, t: (rev(t), j))
    return pl.pallas_call(
        body, grid=(nb, nt),
        in_specs=([tile, tile, tile] + _pair_specs((ts, CB), nb, lambda j, t: (rev(t),))
                  + _pair_specs((FFN_CONV, CB), nb, lambda j, t: (0,))),
        out_specs=[pl.BlockSpec((2, ts, CB), lambda j, t: (0, rev(t), j)),
                   pl.BlockSpec((2, FFN_CONV, CB), lambda j, t: (0, 0, j)),
                   pl.BlockSpec((2, 1, CB), lambda j, t: (0, 0, j))],
        out_shape=[jax.ShapeDtypeStruct((2, s, f2 // 2), MXU_DTYPE), jax.ShapeDtypeStruct((2, FFN_CONV, f2 // 2), F32),
                   jax.ShapeDtypeStruct((2, 1, f2 // 2), F32)],
        scratch_shapes=[pltpu.VMEM((SUBLANES, 2 * CB), F32), pltpu.VMEM((1 + FFN_CONV, SUBLANES, 2 * CB), F32)],
        compiler_params=_params("parallel", "arbitrary"), name=name,
    )(da, ga, gb, p, p, cw, cw)


def _rg_gates(xc, wa_ref, ba_ref, wx_ref, bx_ref, lam_ref):
    r = _sigmoid(_dot_nn(xc, wa_ref[0]) + ba_ref[...])
    ig = _sigmoid(_dot_nn(xc, wx_ref[0]) + bx_ref[...])
    sp = _softplus(-lam_ref[...])
    log_a = (-RG_C) * r * sp
    a = jnp.exp(log_a)
    mult = jnp.sqrt(_one_minus_exp(2.0 * log_a))
    return r, ig, sp, a, mult


def _rg_conv(scr, cw_ref, cb_ref, ts):
    views = [scr[5 + k:5 + k + ts, :] for k in range(RG_CONV)]
    xc = cb_ref[...]
    for k in range(RG_CONV):
        xc = xc + cw_ref[k:k + 1, :] * views[k]
    return xc, views


def _rg_param_specs():
    vec = pl.BlockSpec((1, CB), lambda g, t: (0, g))
    mat = pl.BlockSpec((1, CB, CB), lambda g, t: (g, 0, 0))
    return [pl.BlockSpec((RG_CONV, CB), lambda g, t: (0, g)), vec, mat, vec, mat, vec, vec]


def _scan_rows(a_scr, x_scr, out_ref, carry, ts, reverse):
    row = lax.broadcasted_iota(jnp.int32, (SUBLANES, a_scr.shape[1]), 0)
    last = SUBLANES - 1

    def group(k, c):
        r0 = pl.multiple_of((ts // SUBLANES - 1 - k if reverse else k) * SUBLANES, SUBLANES)
        rows = pl.ds(r0, SUBLANES)
        a, x = a_scr[rows, :], x_scr[rows, :]
        if reverse:
            first_a = a[0:1]
            x = jnp.where(row == last, x + c, x)
            a = jnp.where(row == last, 1.0, pltpu.roll(a, last, axis=0))
            for sh in (1, 2, 4):
                keep = row < SUBLANES - sh
                x = x + a * jnp.where(keep, pltpu.roll(x, SUBLANES - sh, axis=0), 0.0)
                a = a * jnp.where(keep, pltpu.roll(a, SUBLANES - sh, axis=0), 1.0)
            out_ref[rows, :] = x
            return first_a * x[0:1]
        for sh in (1, 2, 4):
            keep = row >= sh
            x = a * jnp.where(keep, pltpu.roll(x, sh, axis=0), 0.0) + x
            a = a * jnp.where(keep, pltpu.roll(a, sh, axis=0), 1.0)
        h = x + a * c
        out_ref[rows, :] = h
        return h[last:last + 1]

    return lax.fori_loop(0, ts // SUBLANES, group, carry, unroll=4)


def _rg_mid_fwd(pj, cw, cb, wa, ba, wx, bx, lam, name):
    s = pj.shape[0]
    nb = pj.shape[1] // (2 * CB)
    ts = _tile(s, (512,))
    nt = s // ts

    def body(gate_ref, x_ref, halo_ref, cw_ref, cb_ref, wa_ref, ba_ref, wx_ref, bx_ref, lam_ref, y_ref, hs_ref,
             scr, a_scr, u_scr, h_scr):
        t = pl.program_id(1)

        @pl.when(t == 0)
        def _():
            h_scr[...] = jnp.zeros_like(h_scr)

        scr[0:SUBLANES, :] = jnp.where(t > 0, halo_ref[...], 0.0)
        scr[SUBLANES:, :] = x_ref[...]
        xc, _ = _rg_conv(scr, cw_ref, cb_ref, ts)
        _, ig, _, a, mult = _rg_gates(xc, wa_ref, ba_ref, wx_ref, bx_ref, lam_ref)
        a_scr[...] = a
        u_scr[...] = mult * (ig * xc)
        h_scr[0:1, :] = _scan_rows(a_scr, u_scr, hs_ref, h_scr[0:1, :], ts, False)
        y_ref[...] = (_gelu(gate_ref[...])[0] * hs_ref[...]).astype(y_ref.dtype)

    blk = pl.BlockSpec((ts, CB), lambda g, t: (t, g))
    return pl.pallas_call(
        body, grid=(nb, nt),
        in_specs=_pair_specs((ts, CB), nb, lambda g, t: (t,))
        + [pl.BlockSpec((SUBLANES, CB), lambda g, t: _halo_row(ts, lambda u: u)(g, t) + (g + nb,))] + _rg_param_specs(),
        out_specs=[blk, blk],
        out_shape=[jax.ShapeDtypeStruct((s, nb * CB), MXU_DTYPE), jax.ShapeDtypeStruct((s, nb * CB), F32)],
        scratch_shapes=[pltpu.VMEM((ts + SUBLANES, CB), F32), pltpu.VMEM((ts, CB), F32), pltpu.VMEM((ts, CB), F32),
                        pltpu.VMEM((SUBLANES, CB), F32)],
        compiler_params=_params("parallel", "arbitrary"), name=name,
    )(pj, pj, pj, cw, cb, wa, ba, wx, bx, lam)


def _rg_mid_bwd(dy, pj, hs, cw, cb, wa, ba, wx, bx, lam, name):
    s = pj.shape[0]
    nb = pj.shape[1] // (2 * CB)
    ts = _tile(s, (512,))
    nt = s // ts

    def body(dy_ref, gate_ref, x_ref, halo_ref, hs_ref, hsh_ref, cw_ref, cb_ref, wa_ref, ba_ref, wx_ref, bx_ref, lam_ref,
             dpj_ref, dcw_ref, dcb_ref, dwa_ref, dba_ref, dwx_ref, dbx_ref, dlam_ref,
             scr, hscr, a_scr, d_scr, g_scr, dxscr, c_scr):
        tt = pl.program_id(1)
        t = nt - 1 - tt

        @pl.when(tt == 0)
        def _():
            c_scr[...] = jnp.zeros_like(c_scr)
            dxscr[ts:, :] = jnp.zeros((SUBLANES, CB), F32)
            for ref in (dcw_ref, dcb_ref, dwa_ref, dba_ref, dwx_ref, dbx_ref, dlam_ref):
                ref[...] = jnp.zeros_like(ref)

        scr[0:SUBLANES, :] = jnp.where(t > 0, halo_ref[...], 0.0)
        scr[SUBLANES:, :] = x_ref[...]
        hscr[0:SUBLANES, :] = jnp.where(t > 0, hsh_ref[...], 0.0)
        hscr[SUBLANES:, :] = hs_ref[...]
        xc, views = _rg_conv(scr, cw_ref, cb_ref, ts)
        r, ig, sp, a, mult = _rg_gates(xc, wa_ref, ba_ref, wx_ref, bx_ref, lam_ref)
        gate = gate_ref[...]
        gel, th = _gelu(gate)
        dyv = dy_ref[...]
        dpj_ref[0] = (dyv * hs_ref[...] * _gelu_grad(gate, th)).astype(dpj_ref.dtype)
        a_scr[...] = a
        d_scr[...] = dyv * gel
        c_scr[0:1, :] = _scan_rows(a_scr, d_scr, g_scr, c_scr[0:1, :], ts, True)
        du = g_scr[...]
        da = du * hscr[7:7 + ts, :]
        dmult = du * (ig * xc)
        dig = du * (mult * xc)
        dxc = du * (mult * ig)
        dlog_a = da * a - dmult * (a * a / mult)
        dlam_ref[...] += jnp.sum(dlog_a * r, axis=0, keepdims=True) * (RG_C * _sigmoid(-lam_ref[...]))
        dpr = dlog_a * ((-RG_C) * sp) * (r * (1.0 - r))
        dpi = dig * (ig * (1.0 - ig))
        dba_ref[...] += jnp.sum(dpr, axis=0, keepdims=True)
        dbx_ref[...] += jnp.sum(dpi, axis=0, keepdims=True)
        dwa_ref[0] += _dot_tn(xc, dpr)
        dwx_ref[0] += _dot_tn(xc, dpi)
        dxc = dxc + _dot_nt(dpr, wa_ref[0]) + _dot_nt(dpi, wx_ref[0])
        dcb_ref[...] += jnp.sum(dxc, axis=0, keepdims=True)
        for k in range(RG_CONV):
            dcw_ref[k:k + 1, :] += jnp.sum(dxc * views[k], axis=0, keepdims=True)
        dxscr[0:ts, :] = dxc
        dxp = cw_ref[3:4, :] * dxc
        for k in range(RG_CONV - 1):
            dxp = dxp + cw_ref[k:k + 1, :] * dxscr[3 - k:3 - k + ts, :]
        dpj_ref[1] = dxp.astype(dpj_ref.dtype)
        dxscr[ts:, :] = dxscr[0:SUBLANES, :]

    rev = lambda g, t: (nt - 1 - t, g)
    rev_halo = lambda g, t: (jnp.maximum((nt - 1 - t) * (ts // SUBLANES) - 1, 0), g)
    vec = pl.BlockSpec((1, CB), lambda g, t: (0, g))
    mat = pl.BlockSpec((1, CB, CB), lambda g, t: (g, 0, 0))
    d = nb * CB
    vshape = jax.ShapeDtypeStruct((1, d), F32)
    mshape = jax.ShapeDtypeStruct((nb, CB, CB), F32)
    return pl.pallas_call(
        body, grid=(nb, nt),
        in_specs=[pl.BlockSpec((ts, CB), rev)] + _pair_specs((ts, CB), nb, lambda g, t: (nt - 1 - t,))
        + [pl.BlockSpec((SUBLANES, CB), lambda g, t: (rev_halo(g, t)[0], g + nb)),
           pl.BlockSpec((ts, CB), rev), pl.BlockSpec((SUBLANES, CB), rev_halo)] + _rg_param_specs(),
        out_specs=[pl.BlockSpec((2, ts, CB), lambda g, t: (0, nt - 1 - t, g)), pl.BlockSpec((RG_CONV, CB), lambda g, t: (0, g)),
                   vec, mat, vec, mat, vec, vec],
        out_shape=[jax.ShapeDtypeStruct((2, s, d), MXU_DTYPE), jax.ShapeDtypeStruct((RG_CONV, d), F32), vshape, mshape, vshape,
                   mshape, vshape, vshape],
        scratch_shapes=[pltpu.VMEM((ts + SUBLANES, CB), F32), pltpu.VMEM((ts + SUBLANES, CB), F32), pltpu.VMEM((ts, CB), F32),
                        pltpu.VMEM((ts, CB), F32), pltpu.VMEM((ts, CB), F32), pltpu.VMEM((ts + SUBLANES, CB), F32),
                        pltpu.VMEM((SUBLANES, CB), F32)],
        compiler_params=_params("parallel", "arbitrary"), name=name,
    )(dy, pj, pj, pj, hs, hs, cw, cb, wa, ba, wx, bx, lam)


GLA_DK = 128
GLA_DV = 256
GLA_O_K = GLA_HEADS * GLA_DK
GLA_O_V = 2 * GLA_HEADS * GLA_DK
GLA_O_R = GLA_O_V + GLA_HEADS * GLA_DV
GLA_O_Z = GLA_O_R + GLA_HEADS * GLA_DV
GLA_IN = GLA_O_Z + GLA_RANK
GLA_TS = 256


def _dk(h, base=0):
    return slice(base + h * GLA_DK, base + (h + 1) * GLA_DK)


def _dv(h, base=0):
    return slice(base + h * GLA_DV, base + (h + 1) * GLA_DV)


def _split3(x):
    hi = x.astype(BF16)
    r1 = x - hi.astype(F32)
    mid = r1.astype(BF16)
    lo = (r1 - mid.astype(F32)).astype(BF16)
    return hi, mid, lo


def _chunk_cumsum(x, reverse):
    n = x.shape[0]
    i = lax.broadcasted_iota(jnp.int32, (n, n), 0)
    j = lax.broadcasted_iota(jnp.int32, (n, n), 1)
    same = (i // GLA_CHUNK) == (j // GLA_CHUNK)
    tri = jnp.where(same & ((j >= i) if reverse else (j <= i)), 1.0, 0.0).astype(BF16)
    out = jnp.zeros(x.shape, F32)
    for piece in _split3(x):
        out = out + lax.dot_general(tri, piece, (((1,), (0,)), ((), ())), preferred_element_type=F32)
    return out


def _gla_head(pj_ref, h):
    return (pj_ref[:, _dk(h)] * (GLA_DK ** -0.5), pj_ref[:, _dk(h, GLA_O_K)], pj_ref[:, _dv(h, GLA_O_V)],
            pj_ref[:, _dv(h, GLA_O_R)])


def _gla_decays(gc):
    gref = gc[GLA_CHUNK // 2:GLA_CHUNK // 2 + 1, :]
    glast = gc[GLA_CHUNK - 1:GLA_CHUNK, :]
    return jnp.exp(gc), jnp.exp(gc - gref), jnp.exp(gref - gc), jnp.exp(glast - gc), jnp.exp(glast)


def _causal_mask():
    i = lax.broadcasted_iota(jnp.int32, (GLA_CHUNK, GLA_CHUNK), 0)
    j = lax.broadcasted_iota(jnp.int32, (GLA_CHUNK, GLA_CHUNK), 1)
    return j <= i


def _log_sigmoid(x):
    return jnp.minimum(x, 0.0) - _log1p_pos(jnp.exp(-jnp.abs(x)))


def _gla_mid_fwd(pj, wal, bal, ng, name):
    s, nh = pj.shape[0], GLA_HEADS
    ts = _tile(s, (GLA_TS,))
    nt, nc = s // ts, ts // GLA_CHUNK

    def body(pj_ref, wal_ref, bal_ref, ng_ref, act_ref, o_ref, st_ref, s_scr):
        t = pl.program_id(0)

        @pl.when(t == 0)
        def _():
            s_scr[...] = jnp.zeros_like(s_scr)

        heads = []
        z = pj_ref[:, GLA_O_Z:]
        for h in range(nh):
            q, k, v, r = _gla_head(pj_ref, h)
            g = _log_sigmoid(_dot_nn(z, wal_ref[:, _dk(h)]) + bal_ref[:, _dk(h)]) * (1.0 / GLA_TAU)
            heads.append((q, k, v, r, _chunk_cumsum(g, False)))
        mask = _causal_mask()
        for c in range(nc):
            sl = slice(c * GLA_CHUNK, (c + 1) * GLA_CHUNK)
            for h, (q, k, v, r, gcum) in enumerate(heads):
                eg, eq, ek, ekd, egl = _gla_decays(gcum[sl])
                st = s_scr[h]
                st_ref[c, h] = st
                attn = jnp.where(mask, _dot_nt(q[sl] * eq, k[sl] * ek), 0.0)
                o_ref[sl, h * GLA_DV:(h + 1) * GLA_DV] = _dot_nt(q[sl] * eg, st) + _dot_nn(attn, v[sl])
                s_scr[h] = st * egl + _dot_tn(v[sl], k[sl] * ekd)
        for h, (q, k, v, r, gcum) in enumerate(heads):
            cols = slice(h * GLA_DV, (h + 1) * GLA_DV)
            o = o_ref[:, cols]
            on = o * lax.rsqrt(jnp.mean(o * o, axis=-1, keepdims=True) + EPS)
            act_ref[:, cols] = ((on * ng_ref[...]) * (r * _sigmoid(r))).astype(act_ref.dtype)

    blk = pl.BlockSpec((ts, nh * GLA_DV), lambda t: (t, 0))
    whole = lambda shape: pl.BlockSpec(shape, lambda t: (0,) * len(shape))
    return pl.pallas_call(
        body, grid=(nt,),
        in_specs=[pl.BlockSpec((ts, GLA_IN), lambda t: (t, 0)), whole((GLA_RANK, nh * GLA_DK)), whole((1, nh * GLA_DK)),
                  whole((1, GLA_DV))],
        out_specs=[blk, blk, pl.BlockSpec((nc, nh, GLA_DV, GLA_DK), lambda t: (t, 0, 0, 0))],
        out_shape=[jax.ShapeDtypeStruct((s, nh * GLA_DV), MXU_DTYPE), jax.ShapeDtypeStruct((s, nh * GLA_DV), F32),
                   jax.ShapeDtypeStruct((s // GLA_CHUNK, nh, GLA_DV, GLA_DK), F32)],
        scratch_shapes=[pltpu.VMEM((nh, GLA_DV, GLA_DK), F32)],
        compiler_params=_params("arbitrary"), name=name,
    )(pj, wal, bal, ng)


def _gla_mid_bwd(dact, pj, o, st, wal, bal, ng, name):
    s, nh = pj.shape[0], GLA_HEADS
    ts = _tile(s, (GLA_TS,))
    nt, nc = s // ts, ts // GLA_CHUNK

    def body(dact_ref, pj_ref, o_ref, st_ref, wal_ref, bal_ref, ng_ref, dpj_ref, dwal_ref, dbal_ref, dng_ref,
             ds_scr, dg_scr):
        tt = pl.program_id(0)

        @pl.when(tt == 0)
        def _():
            ds_scr[...] = jnp.zeros_like(ds_scr)
            dwal_ref[...] = jnp.zeros_like(dwal_ref)
            dbal_ref[...] = jnp.zeros_like(dbal_ref)
            dng_ref[...] = jnp.zeros_like(dng_ref)

        heads = []
        z = pj_ref[:, GLA_O_Z:]
        for h in range(nh):
            q, k, v, r = _gla_head(pj_ref, h)
            logit = _dot_nn(z, wal_ref[:, _dk(h)]) + bal_ref[:, _dk(h)]
            gcum = _chunk_cumsum(_log_sigmoid(logit) * (1.0 / GLA_TAU), False)
            ov = o_ref[:, h * GLA_DV:(h + 1) * GLA_DV]
            ro = lax.rsqrt(jnp.mean(ov * ov, axis=-1, keepdims=True) + EPS)
            on = ov * ro
            sg = _sigmoid(r)
            sil = r * sg
            dav = dact_ref[:, h * GLA_DV:(h + 1) * GLA_DV]
            dpj_ref[:, _dv(h, GLA_O_R)] = (dav * (on * ng_ref[...]) * (sg + sil * (1.0 - sg))).astype(dpj_ref.dtype)
            t1 = dav * sil
            dng_ref[...] += jnp.sum(t1 * on, axis=0, keepdims=True)
            dn = t1 * ng_ref[...]
            do = ro * (dn - on * jnp.mean(dn * on, axis=-1, keepdims=True))
            heads.append((q, k, v, logit, gcum, do))
        mask = _causal_mask()
        scale = GLA_DK ** -0.5
        last_row = lax.broadcasted_iota(jnp.int32, (GLA_CHUNK, GLA_DK), 0) == GLA_CHUNK - 1
        for c in reversed(range(nc)):
            sl = slice(c * GLA_CHUNK, (c + 1) * GLA_CHUNK)
            for h, (q, k, v, logit, gcum, do) in enumerate(heads):
                eg, eq, ek, ekd, egl = _gla_decays(gcum[sl])
                qc, kc, vc, doc = q[sl], k[sl], v[sl], do[sl]
                qg, qt, kt, kd = qc * eg, qc * eq, kc * ek, kc * ekd
                sp = st_ref[c, h]
                ds = ds_scr[h]
                attn = jnp.where(mask, _dot_nt(qt, kt), 0.0)
                dattn = jnp.where(mask, _dot_nt(doc, vc), 0.0)
                dqg = _dot_nn(doc, sp)
                dqt = _dot_nn(dattn, kt)
                dkt = _dot_tn(dattn, qt)
                dkd = _dot_nn(vc, ds)
                dpj_ref[sl, _dv(h, GLA_O_V)] = (_dot_tn(attn, doc) + _dot_nt(kd, ds)).astype(dpj_ref.dtype)
                dpj_ref[sl, _dk(h)] = (scale * (dqg * eg + dqt * eq)).astype(dpj_ref.dtype)
                dpj_ref[sl, _dk(h, GLA_O_K)] = (dkt * ek + dkd * ekd).astype(dpj_ref.dtype)
                kdd = dkd * kd
                dgl = jnp.sum(kdd, axis=0, keepdims=True) + jnp.sum(ds * sp, axis=0, keepdims=True) * egl
                dg_scr[h, sl, :] = dqg * qg + dqt * qt - dkt * kt - kdd + jnp.where(last_row, dgl, 0.0)
                ds_scr[h] = ds * egl + _dot_tn(doc, qg)
        dz = jnp.zeros((ts, GLA_RANK), F32)
        for h, (q, k, v, logit, gcum, do) in enumerate(heads):
            dlogit = _chunk_cumsum(dg_scr[h], True) * (1.0 / GLA_TAU) * _sigmoid(-logit)
            dz = dz + _dot_nt(dlogit, wal_ref[:, _dk(h)])
            dwal_ref[:, _dk(h)] += _dot_tn(z, dlogit)
            dbal_ref[:, _dk(h)] += jnp.sum(dlogit, axis=0, keepdims=True)
        dpj_ref[:, GLA_O_Z:] = dz.astype(dpj_ref.dtype)

    rev = lambda t: (nt - 1 - t, 0)
    whole = lambda shape: pl.BlockSpec(shape, lambda t: (0,) * len(shape))
    wide = pl.BlockSpec((ts, nh * GLA_DV), rev)
    return pl.pallas_call(
        body, grid=(nt,),
        in_specs=[wide, pl.BlockSpec((ts, GLA_IN), rev), wide,
                  pl.BlockSpec((nc, nh, GLA_DV, GLA_DK), lambda t: (nt - 1 - t, 0, 0, 0)),
                  whole((GLA_RANK, nh * GLA_DK)), whole((1, nh * GLA_DK)), whole((1, GLA_DV))],
        out_specs=[pl.BlockSpec((ts, GLA_IN), rev), whole((GLA_RANK, nh * GLA_DK)), whole((1, nh * GLA_DK)), whole((1, GLA_DV))],
        out_shape=[jax.ShapeDtypeStruct((s, GLA_IN), MXU_DTYPE), jax.ShapeDtypeStruct((GLA_RANK, nh * GLA_DK), F32),
                   jax.ShapeDtypeStruct((1, nh * GLA_DK), F32), jax.ShapeDtypeStruct((1, GLA_DV), F32)],
        scratch_shapes=[pltpu.VMEM((nh, GLA_DV, GLA_DK), F32), pltpu.VMEM((nh, ts, GLA_DK), F32)],
        compiler_params=_params("arbitrary"), name=name,
    )(dact, pj, o, st, wal, bal, ng)


def _adamw(w, gs, m, v, name, after=None):
    layers, rows, cols = w.shape
    gs = list(gs) if isinstance(gs, (list, tuple)) else gs
    n_g = len(gs) if isinstance(gs, list) else 1
    if rows % SUBLANES == 0:
        tr, tc = _tile(rows, (256, 128, 64, 32, 16, 8)), cols
    else:
        tr, tc = rows, _tile(cols, (256, 128))
    c1 = 1.0 / (1.0 - ADAM_B1 ** ADAM_STEP)
    c2 = 1.0 / (1.0 - ADAM_B2 ** ADAM_STEP)

    def body(*refs):
        g_refs, (w_ref, m_ref, v_ref) = refs[:n_g], refs[n_g:n_g + 3]
        go_ref, d_ref, mo_ref, vo_ref = refs[-4:]
        gv = g_refs[0][...]
        for l in range(1, n_g):
            gv = jnp.where(pl.program_id(0) == l, g_refs[l][...], gv)
        m2 = ADAM_B1 * m_ref[...] + (1.0 - ADAM_B1) * gv
        v2 = ADAM_B2 * v_ref[...] + (1.0 - ADAM_B2) * (gv * gv)
        d_ref[...] = (-ADAM_LR) * ((m2 * c1) / (jnp.sqrt(v2 * c2) + ADAM_EPS) + ADAM_WD * w_ref[...])
        go_ref[...] = gv
        mo_ref[...] = m2
        vo_ref[...] = v2

    spec = pl.BlockSpec((None, tr, tc), lambda l, i, j: (l, i, j))
    g_specs = [pl.BlockSpec((tr, tc), lambda l, i, j: (i, j))] * n_g if isinstance(gs, list) else [spec]
    extra = [] if after is None else [(after, _ANY)]
    shape = jax.ShapeDtypeStruct((layers, rows, cols), F32)
    return pl.pallas_call(
        body, grid=(layers, rows // tr, cols // tc), in_specs=g_specs + [spec] * 3 + [sp for _, sp in extra],
        out_specs=[spec] * 4, out_shape=[shape] * 4, compiler_params=_params("parallel", "parallel", "parallel"), name=name,
    )(*(gs if isinstance(gs, list) else [gs]), w, m, v, *[a for a, _ in extra])


def _col_slots(w):
    r, c = w.shape
    return jnp.moveaxis(w.reshape(r, N_CHIP, c // N_CHIP), 1, 0)


def _from_col_slots(w):
    n, r, c = w.shape
    return jnp.moveaxis(w, 0, 1).reshape(r, n * c)


def _block_rows_to_slots(w):
    g, r4, cc = w.shape
    return jnp.swapaxes(w.reshape(g, N_CHIP, r4 // N_CHIP, cc), 0, 1).reshape(N_CHIP, g * (r4 // N_CHIP), cc)


def _slots_to_block_rows(w, g):
    n, gr, cc = w.shape
    return jnp.swapaxes(w.reshape(n, g, gr // g, cc), 0, 1).reshape(g, n * (gr // g), cc)


def _local_step(x, tgt, mod, w, fetch=None, done=None, later=None):
    depth = mod.shape[0]
    row = lambda v: v.reshape(1, -1)
    w = dict(w)
    w["ffn_w_up"], w["ffn_w_down"] = dict(enumerate(w["ffn_w_up"])), dict(enumerate(w["ffn_w_down"]))

    def arrive(stage, after):
        if fetch is not None:
            for k, v in fetch(stage, after).items():
                if isinstance(v, dict):
                    w[k].update(v)
                else:
                    w[k] = v

    saved = []
    for i in range(depth):
        if i == 1:
            arrive("l1", x)
        sh_m, sc_m, gt_m, sh_f, sc_f, gt_f = (mod[i, j:j + 1] for j in range(6))
        g0, g1, g2, g3 = (w["norm_g"][i, j:j + 1] for j in range(4))
        tag = f"_l{i}"
        h = _norm_mod_fwd(x, g0, sc_m, sh_m, "norm_mix" + tag)
        if i % 2 == 0:
            pj = _mm(h, w["rg_w_in"], w_slots=N_CHIP, name="rg_in" + tag)
            act, aux = _rg_mid_fwd(pj, w["rg_conv_w"], row(w["rg_conv_b"]), w["rg_wa"], row(w["rg_ba"]), w["rg_wx"],
                                   row(w["rg_bx"]), row(w["rg_lambda"]), "rg_mid" + tag)
            y = _mm(act, w["rg_w_out"], name="rg_out" + tag)
        else:
            pj = _mm(h, w["gla_w_in"], tm_max=512, name="gla_in" + tag)
            act, *aux = _gla_mid_fwd(pj, w["gla_w_alpha"], row(w["gla_b_alpha"]), row(w["gla_norm_g"]), "gla_mid" + tag)
            y = _mm(act, w["gla_w_out"], name="gla_out" + tag)
        x1 = _post_fwd(x, y, g1, gt_m, "post_mix" + tag)
        if i == 0:
            arrive("ffn0", x1)
        h2 = _norm_mod_fwd(x1, g2, sc_f, sh_f, "norm_ffn" + tag)
        p = _mm(h2, w["ffn_w_up"][i], w_slots=N_CHIP, name="ffn_up" + tag)
        a, ga, gb = _ffn_mid_fwd(p, w["ffn_conv_w"][i], w["ffn_conv_b"][i:i + 1], "ffn_mid" + tag)
        y2 = _mm(a, w["ffn_w_down"][i], name="ffn_down" + tag)
        x2 = _post_fwd(x1, y2, g3, gt_f, "post_ffn" + tag)
        saved.append((x, h, pj, act, aux, y, x1, h2, p, (a, ga, gb), y2))
        x = x2

    cols, dx = _loss_grad(x, tgt, "loss")

    stacked = ("norm_g", "ffn_conv_w", "ffn_conv_b", "mod")
    gr = {k: [None] * depth for k in stacked + ("ffn_w_up", "ffn_w_down")}
    told = lambda stage: done(stage, gr) if done is not None else 0.0
    told_later = lambda stage, after: later(stage, after) if later is not None else 0.0
    for i in reversed(range(depth)):
        x0, h, pj, act, aux, y, x1, h2, p, (a, ga, gb), y2 = saved[i]
        sh_m, sc_m, gt_m, sh_f, sc_f, gt_f = (mod[i, j:j + 1] for j in range(6))
        g0, g1, g2, g3 = (w["norm_g"][i, j:j + 1] for j in range(4))
        tag = f"_l{i}"
        dy2, d_g3, d_gt_f = _post_bwd(dx, y2, g3, gt_f, "post_ffn_b" + tag)
        da = _mm(dy2, w["ffn_w_down"][i], tb=True, name="ffn_down_dx" + tag)
        gr["ffn_w_down"][i] = _mm(a, dy2, ta=True, name="ffn_down_dw" + tag)
        conv_w = w["ffn_conv_w"][i] + (told_later("l1", da) if i == 0 else 0.0)
        dp, dcw, dcb = _ffn_mid_bwd(da, p, ga, gb, conv_w, "ffn_mid_b" + tag)
        gr["ffn_conv_w"][i], gr["ffn_conv_b"][i] = _cat(dcw[0], dcw[1]), _cat(dcb[0], dcb[1])[0]
        dh2 = _mm(dp, w["ffn_w_up"][i], tb=True, a_parts=2, w_slots=N_CHIP, name="ffn_up_dx" + tag)
        gr["ffn_w_up"][i] = _mm(h2, dp, ta=True, b_parts=2, out_slots=N_CHIP, name="ffn_up_dw" + tag)
        dx1, d_g2, d_sc_f, d_sh_f = _norm_mod_bwd(dh2, x1, g2, sc_f, dx, "norm_ffn_b" + tag)
        if i == 0:
            gt_m = gt_m + told("ffn0")
        dy, d_g1, d_gt_m = _post_bwd(dx1, y, g1, gt_m, "post_mix_b" + tag)
        if i % 2 == 0:
            dact = _mm(dy, w["rg_w_out"], tb=True, name="rg_out_dx" + tag)
            gr["rg_w_out"] = _mm(act, dy, ta=True, name="rg_out_dw" + tag)
            lam = row(w["rg_lambda"]) + told_later("ffn0", gr["rg_w_out"])
            dpj, gr["rg_conv_w"], d_cb, gr["rg_wa"], d_ba, gr["rg_wx"], d_bx, d_lam = _rg_mid_bwd(
                dact, pj, aux, w["rg_conv_w"], row(w["rg_conv_b"]), w["rg_wa"], row(w["rg_ba"]), w["rg_wx"],
                row(w["rg_bx"]), lam, "rg_mid_b" + tag)
            gr["rg_conv_b"], gr["rg_ba"], gr["rg_bx"], gr["rg_lambda"] = d_cb[0], d_ba[0], d_bx[0], d_lam[0]
            dh = _mm(dpj, w["rg_w_in"], tb=True, a_parts=2, w_slots=N_CHIP, name="rg_in_dx" + tag)
            gr["rg_w_in"] = _mm(h, dpj, ta=True, b_parts=2, out_slots=N_CHIP, name="rg_in_dw" + tag)
        else:
            dact = _mm(dy, w["gla_w_out"], tb=True, name="gla_out_dx" + tag)
            gr["gla_w_out"] = _mm(act, dy, ta=True, name="gla_out_dw" + tag)
            dpj, gr["gla_w_alpha"], d_bal, d_ng = _gla_mid_bwd(dact, pj, aux[0], aux[1], w["gla_w_alpha"], row(w["gla_b_alpha"]),
                                                               row(w["gla_norm_g"]), "gla_mid_b" + tag)
            gr["gla_b_alpha"], gr["gla_norm_g"] = d_bal[0], d_ng[0]
            dh = _mm(dpj, w["gla_w_in"], tb=True, name="gla_in_dx" + tag)
            gr["gla_w_in"] = _mm(h, dpj, ta=True, tm_max=512, name="gla_in_dw" + tag)
            mod = mod.at[0].add(told("l1"))
        dx, d_g0, d_sc_m, d_sh_m = _norm_mod_bwd(dh, x0, g0, sc_m, dx1, "norm_mix_b" + tag)
        gr["norm_g"][i] = jnp.concatenate([d_g0, d_g1, d_g2, d_g3], axis=0)
        gr["mod"][i] = jnp.concatenate([d_sh_m, d_sc_m, d_gt_m, d_sh_f, d_sc_f, d_gt_f], axis=0)
    for k in stacked:
        gr[k] = jnp.stack(gr[k])
    return cols, dx, gr


ADA_ROWS = 16


def _ada_fwd(c16, ada_w, ada_b, name):
    depth, d, n = ada_w.shape
    tn = _tile(n, (512, 256, 128))

    def body(c_ref, w_ref, b_ref, o_ref):
        cv = c_ref[...]
        o_ref[0] = _dot_nn(cv * _sigmoid(cv), w_ref[0]) + b_ref[0]

    return pl.pallas_call(
        body, grid=(depth, n // tn),
        in_specs=[pl.BlockSpec((ADA_ROWS, d), lambda l, j: (0, 0)), pl.BlockSpec((1, d, tn), lambda l, j: (l, 0, j)),
                  pl.BlockSpec((1, 1, tn), lambda l, j: (l, 0, j))],
        out_specs=pl.BlockSpec((1, ADA_ROWS, tn), lambda l, j: (l, 0, j)),
        out_shape=jax.ShapeDtypeStruct((depth, ADA_ROWS, n), F32),
        compiler_params=_params("parallel", "parallel"), name=name,
    )(c16, ada_w, ada_b)


def _ada_bwd(c16, dmod16, name):
    depth, _, n = dmod16.shape
    d = c16.shape[1]
    tn = _tile(n, (512, 256, 128))

    def body(c_ref, dm_ref, o_ref):
        cv = c_ref[...]
        o_ref[0] = _dot_tn(cv * _sigmoid(cv), dm_ref[0])

    return pl.pallas_call(
        body, grid=(depth, n // tn),
        in_specs=[pl.BlockSpec((ADA_ROWS, d), lambda l, j: (0, 0)), pl.BlockSpec((1, ADA_ROWS, tn), lambda l, j: (l, 0, j))],
        out_specs=pl.BlockSpec((1, d, tn), lambda l, j: (l, 0, j)),
        out_shape=jax.ShapeDtypeStruct((depth, d, n), F32),
        compiler_params=_params("parallel", "parallel"), name=name,
    )(c16, dmod16)


PACK_COLS = 1024
_ANY = pl.BlockSpec(memory_space=pl.ANY)
_VMEM = pl.BlockSpec(memory_space=pltpu.VMEM)


def _place():
    return lax.axis_index("x"), lax.axis_index("y"), lax.axis_index("c")


def _other_chips(x, y):
    return [(1 - x, y), (x, 1 - y), (1 - x, 1 - y)]


def _rcopy(src, dst, send_sems, recv_sems, k, peer):
    return pltpu.make_async_remote_copy(src_ref=src, dst_ref=dst, send_sem=send_sems.at[k], recv_sem=recv_sems.at[k],
                                        device_id=peer, device_id_type=MESH)


def _all_gather_8(v, name):
    r, cc = v.shape

    def body(v_ref, out_ref, send_sems, recv_sems, local_sem):
        x, y, c = _place()
        me = 4 * x + 2 * y + c
        mine = pltpu.make_async_copy(v_ref, out_ref.at[me], local_sem)
        mine.start()
        peers = []
        for k in range(1, N_DEV):
            px = 1 - x if k & 4 else x
            py = 1 - y if k & 2 else y
            pc = 1 - c if k & 1 else c
            peers.append((px, py, pc))
        sends = [_rcopy(v_ref, out_ref.at[me], send_sems, recv_sems, k, p) for k, p in enumerate(peers)]
        for cp in sends:
            cp.start()
        for k, (px, py, pc) in enumerate(peers):
            _rcopy(v_ref, out_ref.at[4 * px + 2 * py + pc], send_sems, recv_sems, k, (px, py, pc)).wait_recv()
        for cp in sends:
            cp.wait_send()
        mine.wait()

    return pl.pallas_call(
        body, in_specs=[_VMEM], out_specs=_VMEM, out_shape=jax.ShapeDtypeStruct((N_DEV, r, cc), v.dtype),
        scratch_shapes=[pltpu.SemaphoreType.DMA((N_DEV - 1,)), pltpu.SemaphoreType.DMA((N_DEV - 1,)), pltpu.SemaphoreType.DMA],
        compiler_params=pltpu.CompilerParams(vmem_limit_bytes=VMEM_LIMIT), name=name,
    )(v)


def _gather_chips(shards, name):
    n = len(shards)
    per = 2 * (N_CHIP - 1)

    def body(*refs):
        ins, outs, (send_sems, recv_sems) = refs[:n], refs[n:2 * n], refs[2 * n:]
        x, y, c = _place()
        chip = 2 * x + y
        chips = _other_chips(x, y)
        rows = [(pl.ds(c * (r.shape[0] // 2), r.shape[0] // 2), pl.ds((1 - c) * (r.shape[0] // 2), r.shape[0] // 2)) for r in ins]
        first = [_rcopy(ins[i].at[rows[i][0]], outs[i].at[chip, rows[i][0]], send_sems, recv_sems, per * i + j, (px, py, c))
                 for i in range(n) for j, (px, py) in enumerate(chips)]
        for cp in first:
            cp.start()
        passed = []
        for i in range(n):
            for j, (px, py) in enumerate(chips):
                landed = outs[i].at[2 * px + py, rows[i][0]]
                _rcopy(ins[i].at[rows[i][0]], landed, send_sems, recv_sems, per * i + j, (px, py, c)).wait_recv()
                fw = _rcopy(landed, landed, send_sems, recv_sems, per * i + N_CHIP - 1 + j, (x, y, 1 - c))
                fw.start()
                passed.append(fw)
        for i in range(n):
            for j, (px, py) in enumerate(chips):
                landed = outs[i].at[2 * px + py, rows[i][1]]
                _rcopy(landed, landed, send_sems, recv_sems, per * i + N_CHIP - 1 + j, (x, y, 1 - c)).wait_recv()
        for cp in first + passed:
            cp.wait_send()

    return pl.pallas_call(
        body, in_specs=[_ANY] * n, out_specs=[_ANY] * n,
        out_shape=[jax.ShapeDtypeStruct((N_CHIP,) + sh.shape, sh.dtype) for sh in shards],
        scratch_shapes=[pltpu.SemaphoreType.DMA((per * n,)), pltpu.SemaphoreType.DMA((per * n,))], name=name,
    )(*shards)


def _pair_exchange(gs, name):
    n = len(gs)

    def body(*refs):
        ins, outs, (send_sems, recv_sems) = refs[:n], refs[n:2 * n], refs[2 * n:]
        x, y, c = _place()
        copies = []
        for i in range(n):
            half = ins[i].shape[1] // 2
            copies.append(_rcopy(ins[i].at[:, pl.ds((1 - c) * half, half)], outs[i], send_sems, recv_sems, i, (x, y, 1 - c)))
        for cp in copies:
            cp.start()
        for cp in copies:
            cp.wait()

    return pl.pallas_call(
        body, in_specs=[_ANY] * n, out_specs=[_ANY] * n,
        out_shape=[jax.ShapeDtypeStruct((g.shape[0], g.shape[1] // 2, g.shape[2]), g.dtype) for g in gs],
        scratch_shapes=[pltpu.SemaphoreType.DMA((n,)), pltpu.SemaphoreType.DMA((n,))], name=name,
    )(*gs)


_ROW_TILES = (640, 512, 352, 256, 128, 64, 32, 16)


def _pair_sum(g, other, c_idx, name):
    n, half, cc = other.shape
    tr = _tile(half, _ROW_TILES)

    def body(c_ref, g_ref, o_ref, out_ref):
        out_ref[...] = (g_ref[...] + o_ref[...]).astype(out_ref.dtype)

    return pl.pallas_call(
        body,
        grid_spec=pltpu.PrefetchScalarGridSpec(
            num_scalar_prefetch=1, grid=(n, half // tr),
            in_specs=[pl.BlockSpec((None, None, tr, cc), lambda k, i, c_ref: (k, c_ref[0], i, 0)),
                      pl.BlockSpec((None, tr, cc), lambda k, i, c_ref: (k, i, 0))],
            out_specs=pl.BlockSpec((None, tr, cc), lambda k, i, c_ref: (k, i, 0))),
        out_shape=jax.ShapeDtypeStruct((n, half, cc), BF16),
        compiler_params=_params("parallel", "parallel"), name=name,
    )(c_idx, g.reshape(n, 2, half, cc), other)


def _chip_exchange(ps, name):
    n = len(ps)
    per = N_CHIP - 1

    def body(*refs):
        ins, outs, (send_sems, recv_sems) = refs[:n], refs[n:2 * n], refs[2 * n:]
        x, y, c = _place()
        chip = 2 * x + y
        chips = _other_chips(x, y)
        sends = [_rcopy(ins[i].at[2 * px + py], outs[i].at[chip], send_sems, recv_sems, per * i + j, (px, py, c))
                 for i in range(n) for j, (px, py) in enumerate(chips)]
        for cp in sends:
            cp.start()
        for i in range(n):
            for j, (px, py) in enumerate(chips):
                _rcopy(ins[i].at[chip], outs[i].at[2 * px + py], send_sems, recv_sems, per * i + j, (px, py, c)).wait_recv()
        for cp in sends:
            cp.wait_send()

    return pl.pallas_call(
        body, in_specs=[_ANY] * n, out_specs=[_ANY] * n, out_shape=[jax.ShapeDtypeStruct(p.shape, p.dtype) for p in ps],
        scratch_shapes=[pltpu.SemaphoreType.DMA((per * n,)), pltpu.SemaphoreType.DMA((per * n,))], name=name,
    )(*ps)


_HBM = pl.BlockSpec(memory_space=pltpu.HBM)
_SEM = pl.BlockSpec(memory_space=pltpu.SEMAPHORE)
_DATAFLOW = pltpu.SideEffectType.DATAFLOW_SIDE_EFFECTING


def _split_copies(srcs, lands, send_sems, recv_sems, mode, arriving):
    x, y, c = _place()
    chip = 2 * x + y
    out = []
    for i, (src, land) in enumerate(zip(srcs, lands)):
        if mode == "pair":
            half = src.shape[1] // 2
            out.append(_rcopy(src.at[:, pl.ds((1 - c) * half, half)], land, send_sems, recv_sems, i, (x, y, 1 - c)))
            continue
        for j, (px, py) in enumerate(_other_chips(x, y)):
            there = 2 * px + py
            part = src.at[there] if mode == "slots" else src
            out.append(_rcopy(part, land.at[there if arriving else chip], send_sems, recv_sems, (N_CHIP - 1) * i + j, (px, py, c)))
    return out


def _land_shape(src, mode):
    if mode == "pair":
        return (src.shape[0], src.shape[1] // 2, src.shape[2])
    return (N_CHIP,) + (src.shape[1:] if mode == "slots" else src.shape)


def _send_start(srcs, mode, name):
    n = len(srcs)
    n_sem = n if mode == "pair" else (N_CHIP - 1) * n
    lands = [lax.empty(_land_shape(s, mode), s.dtype) for s in srcs]

    def body(*refs):
        ins, zones, (send_sems, recv_sems) = refs[:n], refs[n:2 * n], refs[2 * n:2 * n + 2]
        for cp in _split_copies(ins, zones, send_sems, recv_sems, mode, False):
            cp.start()
        refs[-1][...] = jnp.zeros_like(refs[-1])

    hbm = lambda a: pltpu.HBM(a.shape, a.dtype)
    outs = pl.pallas_call(
        body, name=name, in_specs=[_HBM] * (2 * n),
        out_shape=(pltpu.SemaphoreType.DMA((n_sem,)), pltpu.SemaphoreType.DMA((n_sem,)), *[hbm(a) for a in srcs],
                   *[hbm(a) for a in lands], jax.ShapeDtypeStruct((SUBLANES, LANES), F32)),
        out_specs=(_SEM, _SEM, *[_HBM] * (2 * n), _VMEM), input_output_aliases={i: 2 + i for i in range(2 * n)},
        compiler_params=pltpu.CompilerParams(has_side_effects=_DATAFLOW),
    )(*[pltpu.with_memory_space_constraint(a, pltpu.HBM) for a in list(srcs) + lands])
    return (outs[0], outs[1], list(outs[2:2 + n]), list(outs[2 + n:2 + 2 * n])), outs[-1]


def _send_wait(state, after, mode, name):
    send_sems, recv_sems, srcs, lands = state
    n = len(srcs)

    def body(*refs):
        ins, zones, (send_s, recv_s) = refs[:n], refs[n:2 * n], refs[2 * n:2 * n + 2]
        for cp in _split_copies(ins, zones, send_s, recv_s, mode, True):
            cp.wait_send()
            cp.wait_recv()

    hbm = lambda a: pltpu.HBM(a.shape, a.dtype)
    outs = pl.pallas_call(
        body, name=name, in_specs=[_HBM] * (2 * n) + [_SEM, _SEM, _ANY],
        out_shape=tuple(hbm(a) for a in srcs + lands), out_specs=tuple([_HBM] * (2 * n)),
        input_output_aliases={i: i for i in range(2 * n)},
        compiler_params=pltpu.CompilerParams(has_side_effects=_DATAFLOW),
    )(*srcs, *lands, send_sems, recv_sems, after)
    return list(outs[n:])


def _sum_lead(v, name):
    n, r, cc = v.shape
    tr = _tile(r, _ROW_TILES + (8,))

    def body(v_ref, o_ref):
        acc = v_ref[0].astype(F32)
        for k in range(1, n):
            acc = acc + v_ref[k].astype(F32)
        o_ref[...] = acc

    return pl.pallas_call(
        body, grid=(r // tr,), in_specs=[pl.BlockSpec((n, tr, cc), lambda i: (0, i, 0))],
        out_specs=pl.BlockSpec((tr, cc), lambda i: (i, 0)), out_shape=jax.ShapeDtypeStruct((r, cc), F32),
        compiler_params=_params("parallel"), name=name,
    )(v)


def _chip_sum(arrived, mine, chip_idx, name):
    n, r, cc = arrived.shape
    tr = _tile(r, _ROW_TILES)

    def body(chip_ref, a_ref, m_ref, o_ref):
        acc = jnp.zeros((tr, cc), F32)
        for k in range(n):
            acc = acc + jnp.where(chip_ref[0] == k, m_ref[...], a_ref[k]).astype(F32)
        o_ref[...] = acc

    return pl.pallas_call(
        body,
        grid_spec=pltpu.PrefetchScalarGridSpec(
            num_scalar_prefetch=1, grid=(r // tr,),
            in_specs=[pl.BlockSpec((n, tr, cc), lambda i, chip_ref: (0, i, 0)),
                      pl.BlockSpec((None, tr, cc), lambda i, chip_ref: (chip_ref[0], i, 0))],
            out_specs=pl.BlockSpec((tr, cc), lambda i, chip_ref: (i, 0))),
        out_shape=jax.ShapeDtypeStruct((r, cc), F32), compiler_params=_params("parallel"), name=name,
    )(chip_idx, arrived, mine)


def _pair_share(reds, name):
    n = len(reds)

    def body(*refs):
        ins, outs, (send_sems, recv_sems) = refs[:n], refs[n:2 * n], refs[2 * n:]
        x, y, c = _place()
        copies = [_rcopy(ins[i], outs[i].at[c], send_sems, recv_sems, i, (x, y, 1 - c)) for i in range(n)]
        for cp in copies:
            cp.start()
        for i in range(n):
            _rcopy(ins[i], outs[i].at[1 - c], send_sems, recv_sems, i, (x, y, 1 - c)).wait_recv()
        for cp in copies:
            cp.wait_send()

    return pl.pallas_call(
        body, in_specs=[_ANY] * n, out_specs=[_ANY] * n, out_shape=[jax.ShapeDtypeStruct((2,) + r.shape, r.dtype) for r in reds],
        scratch_shapes=[pltpu.SemaphoreType.DMA((n,)), pltpu.SemaphoreType.DMA((n,))], name=name,
    )(*reds)


def _pack(arrs, rows_multiple, dtype):
    flat = jnp.concatenate([a.reshape(-1).astype(dtype) for a in arrs])
    unit = rows_multiple * PACK_COLS
    total = -(-flat.shape[0] // unit) * unit
    return jnp.pad(flat, (0, total - flat.shape[0])).reshape(-1, PACK_COLS)


def _unpack(buf, shapes):
    lead = buf.shape[:-2]
    flat = buf.reshape(*lead, -1)
    out, off = [], 0
    for shp in shapes:
        n = 1
        for s in shp:
            n *= s
        out.append(flat[..., off:off + n].reshape(*lead, *shp))
        off += n
    return out


def _join_shards(parts, axis):
    moved = jnp.moveaxis(parts, 0, axis)
    shp = list(moved.shape)
    shp[axis:axis + 2] = [shp[axis] * shp[axis + 1]]
    return moved.reshape(shp)


def _my_shard(full, axis, chip):
    n = full.shape[axis] // N_CHIP
    return lax.dynamic_slice_in_dim(full, chip * n, n, axis)


SMALL = {"norm_g": 2, "ffn_conv_w": 2, "rg_conv_w": 2, "gla_w_alpha": 2, "gla_b_alpha": 1, "gla_norm_g": 1,
         "ada_b": None, "ffn_conv_b": None, "rg_conv_b": None, "rg_ba": None, "rg_bx": None, "rg_lambda": None}
BIG = {"rg_w_in": True, "rg_wa": False, "rg_wx": False, "rg_w_out": False, "ffn_w_up": True, "ffn_w_down": False,
       "gla_w_in": True, "gla_w_out": False}
WEIGHTS = ["ada_w", "ada_b", "norm_g", "ffn_w_up", "ffn_conv_w", "ffn_conv_b", "ffn_w_down", "rg_w_in", "rg_conv_w", "rg_conv_b",
           "rg_wa", "rg_ba", "rg_wx", "rg_bx", "rg_lambda", "rg_w_out", "gla_w_in", "gla_w_alpha", "gla_b_alpha", "gla_norm_g",
           "gla_w_out"]


def kernel(x, c, ada_w, ada_b, norm_g, ffn_w_up, ffn_conv_w, ffn_conv_b, ffn_w_down, rg_w_in, rg_conv_w, rg_conv_b, rg_wa, rg_ba, rg_wx, rg_bx, rg_lambda, rg_w_out, gla_w_in, gla_w_alpha, gla_b_alpha, gla_norm_g, gla_w_out, loss_target, m_ada_w, m_ada_b, m_norm_g, m_ffn_w_up, m_ffn_conv_w, m_ffn_conv_b, m_ffn_w_down, m_rg_w_in, m_rg_conv_w, m_rg_conv_b, m_rg_wa, m_rg_ba, m_rg_wx, m_rg_bx, m_rg_lambda, m_rg_w_out, m_gla_w_in, m_gla_w_alpha, m_gla_b_alpha, m_gla_norm_g, m_gla_w_out, v_ada_w, v_ada_b, v_norm_g, v_ffn_w_up, v_ffn_conv_w, v_ffn_conv_b, v_ffn_w_down, v_rg_w_in, v_rg_conv_w, v_rg_conv_b, v_rg_wa, v_rg_ba, v_rg_wx, v_rg_bx, v_rg_lambda, v_rg_w_out, v_gla_w_in, v_gla_w_alpha, v_gla_b_alpha, v_gla_norm_g, v_gla_w_out):
    wts = dict(ada_w=ada_w, ada_b=ada_b, norm_g=norm_g, ffn_w_up=ffn_w_up, ffn_conv_w=ffn_conv_w, ffn_conv_b=ffn_conv_b,
               ffn_w_down=ffn_w_down, rg_w_in=rg_w_in, rg_conv_w=rg_conv_w, rg_conv_b=rg_conv_b, rg_wa=rg_wa, rg_ba=rg_ba,
               rg_wx=rg_wx, rg_bx=rg_bx, rg_lambda=rg_lambda, rg_w_out=rg_w_out, gla_w_in=gla_w_in, gla_w_alpha=gla_w_alpha,
               gla_b_alpha=gla_b_alpha, gla_norm_g=gla_norm_g, gla_w_out=gla_w_out)
    mom1 = dict(ada_w=m_ada_w, ada_b=m_ada_b, norm_g=m_norm_g, ffn_w_up=m_ffn_w_up, ffn_conv_w=m_ffn_conv_w,
                ffn_conv_b=m_ffn_conv_b, ffn_w_down=m_ffn_w_down, rg_w_in=m_rg_w_in, rg_conv_w=m_rg_conv_w,
                rg_conv_b=m_rg_conv_b, rg_wa=m_rg_wa, rg_ba=m_rg_ba, rg_wx=m_rg_wx, rg_bx=m_rg_bx, rg_lambda=m_rg_lambda,
                rg_w_out=m_rg_w_out, gla_w_in=m_gla_w_in, gla_w_alpha=m_gla_w_alpha, gla_b_alpha=m_gla_b_alpha,
                gla_norm_g=m_gla_norm_g, gla_w_out=m_gla_w_out)
    mom2 = dict(ada_w=v_ada_w, ada_b=v_ada_b, norm_g=v_norm_g, ffn_w_up=v_ffn_w_up, ffn_conv_w=v_ffn_conv_w,
                ffn_conv_b=v_ffn_conv_b, ffn_w_down=v_ffn_w_down, rg_w_in=v_rg_w_in, rg_conv_w=v_rg_conv_w,
                rg_conv_b=v_rg_conv_b, rg_wa=v_rg_wa, rg_ba=v_rg_ba, rg_wx=v_rg_wx, rg_bx=v_rg_bx, rg_lambda=v_rg_lambda,
                rg_w_out=v_rg_w_out, gla_w_in=v_gla_w_in, gla_w_alpha=v_gla_w_alpha, gla_b_alpha=v_gla_b_alpha,
                gla_norm_g=v_gla_norm_g, gla_w_out=v_gla_w_out)
    xi, yi, ci = _place()
    chip, me = 2 * xi + yi, 4 * xi + 2 * yi + ci
    d = x.shape[-1]
    depth = ada_w.shape[0]
    n_ada = ada_w.shape[-1]
    sharded_small = [k for k, ax in SMALL.items() if ax is not None]

    sm = _all_gather_8(_pack([c] + [wts[k] for k in sharded_small], SUBLANES, F32), "gather_small")
    c_all = sm[:, 0, :]
    parts = _unpack(sm[0::2], [c.shape] + [wts[k].shape for k in sharded_small])[1:]
    full = {k: _join_shards(p, SMALL[k]) for k, p in zip(sharded_small, parts)}
    for k, ax in SMALL.items():
        if ax is None:
            full[k] = wts[k]

    c16 = jnp.pad(c_all, ((0, ADA_ROWS - N_DEV), (0, 0)))
    ada_b_mine = lax.dynamic_slice_in_dim(ada_b, chip * n_ada, n_ada, 1)[:, None, :]
    mod_cols = _ada_fwd(c16, ada_w, ada_b_mine, "ada_fwd")
    mod_all = _all_gather_8(mod_cols.reshape(-1, PACK_COLS), "gather_mod")[0::2].reshape(N_CHIP, depth, ADA_ROWS, n_ada)
    mod = jnp.swapaxes(lax.dynamic_index_in_dim(mod_all, me, 2, keepdims=False), 0, 1).reshape(depth, 6, d)

    items = [(k, l) for k in BIG for l in range(wts[k].shape[0])]
    stage_of = lambda k, l: "rg" if k.startswith("rg_") else ("ffn0" if (k.startswith("ffn_") and l == 0) else "l1")
    staged = {st: [it for it in items if stage_of(*it) == st] for st in ("rg", "ffn0", "l1")}
    staged["l1"].sort(key=lambda it: not it[0].startswith("gla_"))
    shard = lambda k, l: wts[k][l].reshape(-1, wts[k].shape[-1]).astype(BF16)
    own = lambda got, mine: [lax.dynamic_update_index_in_dim(g, m, chip, 0) for g, m in zip(got, mine)]
    rows_joined = lambda v: v.reshape(-1, v.shape[-1])

    def placed(its, slots):
        out = {"ffn_w_up": {}, "ffn_w_down": {}}
        for (k, l), v in zip(its, slots):
            if k == "ffn_w_up":
                out[k][l] = v
            elif k == "ffn_w_down":
                out[k][l] = rows_joined(v)
            elif k in ("rg_wa", "rg_wx"):
                out[k] = _slots_to_block_rows(v, RG_BLOCKS)
            elif k == "gla_w_in":
                out[k] = _from_col_slots(v)
            else:
                out[k] = v if BIG[k] else rows_joined(v)
        return out

    after_mod = (mod[0, 0, 0] * 0.0).astype(BF16)
    sh_rg = [shard(k, l) + after_mod for k, l in staged["rg"]]
    local = {k: (v if k in ("norm_g", "ffn_conv_w", "ffn_conv_b") else v[0]) for k, v in full.items()}
    local.update(placed(staged["rg"], own(_gather_chips(sh_rg, "gather_weights_rg"), sh_rg)))
    sh_late, flying = {}, {}
    after_rg = (local["rg_w_out"][0, 0].astype(F32) * 0.0).astype(BF16)
    sh_late["ffn0"] = [shard(k, l) + after_rg for k, l in staged["ffn0"]]
    flying["ffn0"], tok = _send_start(sh_late["ffn0"], "whole", "weights_ffn0_start")
    sh_late["l1"] = [shard(k, l) + tok[0, 0].astype(BF16) for k, l in staged["l1"]]
    flying["l1"], tok2 = _send_start(sh_late["l1"], "whole", "weights_l1_start")
    mod = mod + (tok[0, 0] + tok2[0, 0])

    def fetch(stage, after):
        got = _send_wait(flying[stage], after, "whole", f"weights_{stage}_wait")
        return placed(staged[stage], own(got, sh_late[stage]))

    c_idx = ci.reshape(1).astype(jnp.int32)
    gslots, paired, psums, sent = {}, {}, {}, {}

    def grad_slots(gr, k, l):
        g = gr[k][l] if k in ("ffn_w_up", "ffn_w_down") else gr[k]
        if k in ("rg_wa", "rg_wx"):
            return _block_rows_to_slots(g)
        if k == "gla_w_in":
            return _col_slots(g)
        return g if BIG[k] else g.reshape(N_CHIP, -1, g.shape[-1])

    def done(stage, gr):
        gslots[stage] = [grad_slots(gr, k, l) for k, l in staged[stage]]
        paired[stage], token = _send_start(gslots[stage], "pair", f"grads_{stage}_pair_start")
        return token[0, 0]

    def later(stage, after):
        theirs = _send_wait(paired[stage], after, "pair", f"grads_{stage}_pair_wait")
        psums[stage] = [_pair_sum(g, t, c_idx, f"grads_pair_sum_{k}{l}") for (k, l), g, t in zip(staged[stage], gslots[stage], theirs)]
        sent[stage], token = _send_start(psums[stage], "slots", f"grads_{stage}_start")
        return token[0, 0]

    cols, grad_x, gr = _local_step(x[0], loss_target[0], mod, local, fetch, done, later)
    loss = lax.psum(0.5 * jnp.sum(cols) / d, ("x", "y", "c"))

    small_names = [k for k in SMALL if k != "ada_b"]
    gs = _all_gather_8(_pack([gr[k] for k in small_names] + [gr["mod"]], SUBLANES, F32), "gather_small_grads")
    small_shapes = [full[k].shape for k in small_names] + [(depth, 6 * d)]
    *small_sum, g_ada_b = _unpack(_sum_lead(gs, "sum_small_grads"), small_shapes)
    grads = dict(zip(small_names, small_sum))
    grads["ada_b"] = g_ada_b
    for k in sharded_small:
        grads[k] = _my_shard(grads[k], SMALL[k], chip)
    dmod_all = _unpack(gs, small_shapes)[-1].reshape(N_DEV, depth, N_CHIP, n_ada)
    dmod_mine = jnp.swapaxes(lax.dynamic_index_in_dim(dmod_all, chip, 2, keepdims=False), 0, 1)
    g_ada_w = _ada_bwd(c16, jnp.pad(dmod_mine, ((0, 0), (0, ADA_ROWS - N_DEV), (0, 0))), "ada_bwd")

    gslots["rg"] = [grad_slots(gr, k, l) for k, l in staged["rg"]]
    theirs = _pair_exchange(gslots["rg"], "grads_rg_pair_exchange")
    psums["rg"] = [_pair_sum(g, t, c_idx, f"grads_pair_sum_{k}{l}") for (k, l), g, t in zip(staged["rg"], gslots["rg"], theirs)]
    sent["rg"], rg_sent = _send_start(psums["rg"], "slots", "grads_rg_start")
    chip_idx = chip.reshape(1).astype(jnp.int32)
    delta, new_m, new_v = {}, {}, {}

    def reduce_and_update(stages, after, dep):
        its = [(st, n) for st in stages for n in range(len(staged[st]))]
        arrived = {st: _send_wait(sent[st], after, "slots", f"grads_{st}_wait") for st in stages}
        halves = [_chip_sum(arrived[st][n], psums[st][n], chip_idx, "grads_chip_sum_%s%d" % staged[st][n]) for st, n in its]
        shared = _pair_share(halves, "grads_pair_share_" + stages[0])
        reduced = [lax.dynamic_update_index_in_dim(s2, h, ci, 0).reshape(-1, h.shape[-1]) for s2, h in zip(shared, halves)]
        last = None
        for k in BIG:
            gs_k = [g for (st, n), g in zip(its, reduced) if staged[st][n][0] == k]
            if gs_k:
                last = update(k, gs_k, dep)
        return last

    def update(k, gs_k, dep=None):
        shp = wts[k].shape
        if k == "gla_w_in":
            view, back = (lambda a: jnp.swapaxes(a, 1, 2)), (lambda o: jnp.swapaxes(o, 1, 2))
            gs_k = [g.T for g in gs_k]
        else:
            view, back = (lambda a: a.reshape(a.shape[0], -1, a.shape[-1])), (lambda o: o.reshape(shp))
        outs = _adamw(view(wts[k]), gs_k, view(mom1[k]), view(mom2[k]), "adamw_" + k, dep)
        grads[k], delta[k], new_m[k], new_v[k] = (back(o) for o in outs)
        return new_v[k]

    done_late = reduce_and_update(("ffn0", "l1"), grad_x, rg_sent)
    update("ada_w", g_ada_w, rg_sent)
    small_shard_shapes = [wts[k].shape for k in SMALL]
    packed = [_pack([src[k] for k in SMALL], SUBLANES, F32) for src in (wts, grads, mom1, mom2)]
    outs = _adamw(packed[0][None], [packed[1]], packed[2][None], packed[3][None], "adamw_small", rg_sent)
    for dst, o in zip((delta, new_m, new_v), outs[1:]):
        for k, a in zip(SMALL, _unpack(o[0], small_shard_shapes)):
            dst[k] = a
    reduce_and_update(("rg",), done_late, None)

    return (loss, grad_x[None], *[grads[k] for k in WEIGHTS], *[delta[k] for k in WEIGHTS], *[new_m[k] for k in WEIGHTS],
            *[new_v[k] for k in WEIGHTS])
```

```python
import jax
import jax.numpy as jnp
from jax import lax
from jax.experimental import pallas as pl
from jax.experimental.pallas import tpu as pltpu

F32 = jnp.float32
BF16 = jnp.bfloat16
MXU_DTYPE = BF16

EPS = 1e-6
RG_C = 8.0
RG_BLOCKS = 4
RG_CONV = 4
GLA_HEADS = 4
GLA_TAU = 16.0
GLA_CHUNK = 64
GLA_RANK = 16
FFN_CONV = 3
ADAM_LR = 0.001
ADAM_B1 = 0.9
ADAM_B2 = 0.999
ADAM_EPS = 1e-08
ADAM_WD = 0.01
ADAM_STEP = 10

LANES = 128
SUBLANES = 8
VMEM_LIMIT = 56 * 1024 * 1024
CB = 256
MESH = pl.DeviceIdType.MESH
N_DEV = 8
N_CHIP = 4


def _params(*sem):
    return pltpu.CompilerParams(dimension_semantics=sem, vmem_limit_bytes=VMEM_LIMIT)


def _tile(dim, prefs):
    for p in prefs:
        if dim % p == 0:
            return p
    return dim


def _dot(a, b, dims):
    return lax.dot_general(a.astype(MXU_DTYPE), b.astype(MXU_DTYPE), (dims, ((), ())), preferred_element_type=F32)


def _dot_nn(a, b):
    return _dot(a, b, ((1,), (0,)))


def _dot_nt(a, b):
    return _dot(a, b, ((1,), (1,)))


def _dot_tn(a, b):
    return _dot(a, b, ((0,), (0,)))


def _mm(a, b, *, ta=False, tb=False, a_parts=1, b_parts=1, w_slots=1, out_slots=1, out_dtype=F32, tm_max=1024, name):
    if ta:
        k_dim, m_dim = a.shape
        n_dim = b.shape[-1] * b_parts
    else:
        m_dim, k_dim = a.shape[-2], a.shape[-1] * a_parts
        n_dim = b.shape[-2] if tb else b.shape[-1] * w_slots
    n_unit = n_dim // max(b_parts, out_slots, 1 if tb else w_slots)
    k_unit = k_dim // max(a_parts, w_slots if tb else 1)
    tm = _tile(m_dim, tuple(t for t in (1024, 1408, 512, 256, 128) if t <= max(tm_max, 128)))
    tn = _tile(n_unit, (1024, 1408, 896, 512, 256, 128))
    tk = _tile(k_unit, (1024, 1408, 896, 512, 256, 128))
    nk = k_dim // tk
    dims = ((0 if ta else 1,), (1 if tb else 0,))

    def spec(shape, parts, total, tile, col_grid, row_grid):
        per = total // parts // tile

        def index(i, j, k):
            g = {"i": i, "j": j, "k": k}
            col, row = g[col_grid], g[row_grid]
            return (row, col) if parts == 1 else (col // per, row, col % per)

        return pl.BlockSpec(shape if parts == 1 else (None,) + shape, index)

    def body(a_ref, b_ref, o_ref, *acc):
        if nk == 1:
            o_ref[...] = _dot(a_ref[...], b_ref[...], dims).astype(o_ref.dtype)
            return
        acc_ref, k = acc[0], pl.program_id(2)

        @pl.when(k == 0)
        def _():
            acc_ref[...] = jnp.zeros_like(acc_ref)

        acc_ref[...] += _dot(a_ref[...], b_ref[...], dims)

        @pl.when(k == nk - 1)
        def _():
            o_ref[...] = acc_ref[...].astype(o_ref.dtype)

    if ta:
        a_spec = spec((tk, tm), 1, m_dim, tm, "i", "k")
        b_spec = spec((tk, tn), b_parts, n_dim, tn, "j", "k")
    elif tb:
        a_spec = spec((tm, tk), a_parts, k_dim, tk, "k", "i")
        b_spec = spec((tn, tk), w_slots, k_dim, tk, "k", "j")
    else:
        a_spec = spec((tm, tk), a_parts, k_dim, tk, "k", "i")
        b_spec = spec((tk, tn), w_slots, n_dim, tn, "j", "k")
    out_shape = (m_dim, n_dim) if out_slots == 1 else (out_slots, m_dim, n_dim // out_slots)
    return pl.pallas_call(
        body,
        grid=(m_dim // tm, n_dim // tn, nk),
        in_specs=[a_spec, b_spec],
        out_specs=spec((tm, tn), out_slots, n_dim, tn, "j", "i"),
        out_shape=jax.ShapeDtypeStruct(out_shape, out_dtype),
        scratch_shapes=[pltpu.VMEM((tm, tn), F32)] if nk > 1 else [],
        compiler_params=_params("parallel", "parallel", "arbitrary"),
        name=name,
    )(a, b)


ROW_TILES = (1024, 512)


def _row_specs(s, d, ts):
    return pl.BlockSpec((ts, d), lambda i: (i, 0)), pl.BlockSpec((1, d), lambda i: (0, 0))


def _norm_mod_fwd(x, g, sc, sh, name):
    s, d = x.shape
    ts = _tile(s, ROW_TILES)

    def body(x_ref, g_ref, sc_ref, sh_ref, h_ref):
        xv = x_ref[...]
        r = lax.rsqrt(jnp.mean(xv * xv, axis=-1, keepdims=True) + EPS)
        h_ref[...] = (((xv * r) * g_ref[...]) * (1.0 + sc_ref[...]) + sh_ref[...]).astype(h_ref.dtype)

    row, vec = _row_specs(s, d, ts)
    return pl.pallas_call(
        body, grid=(s // ts,), in_specs=[row, vec, vec, vec], out_specs=row,
        out_shape=jax.ShapeDtypeStruct((s, d), MXU_DTYPE), compiler_params=_params("parallel"), name=name,
    )(x, g, sc, sh)


def _norm_mod_bwd(dh, x, g, sc, dres, name):
    s, d = x.shape
    ts = _tile(s, ROW_TILES)

    def body(dh_ref, x_ref, g_ref, sc_ref, dres_ref, dx_ref, dg_ref, dsc_ref, dsh_ref, acc_ref):
        i = pl.program_id(0)

        @pl.when(i == 0)
        def _():
            acc_ref[...] = jnp.zeros_like(acc_ref)

        xv, dhv = x_ref[...], dh_ref[...]
        r = lax.rsqrt(jnp.mean(xv * xv, axis=-1, keepdims=True) + EPS)
        n = xv * r
        acc_ref[0:1, :] += jnp.sum(dhv * n, axis=0, keepdims=True)
        acc_ref[1:2, :] += jnp.sum(dhv, axis=0, keepdims=True)
        dn = dhv * ((1.0 + sc_ref[...]) * g_ref[...])
        dx_ref[...] = dres_ref[...] + r * (dn - n * jnp.mean(dn * n, axis=-1, keepdims=True))
        dg_ref[...] = (1.0 + sc_ref[...]) * acc_ref[0:1, :]
        dsc_ref[...] = g_ref[...] * acc_ref[0:1, :]
        dsh_ref[...] = acc_ref[1:2, :]

    row, vec = _row_specs(s, d, ts)
    vshape = jax.ShapeDtypeStruct((1, d), F32)
    return pl.pallas_call(
        body, grid=(s // ts,), in_specs=[row, row, vec, vec, row], out_specs=[row, vec, vec, vec],
        out_shape=[jax.ShapeDtypeStruct((s, d), F32), vshape, vshape, vshape],
        scratch_shapes=[pltpu.VMEM((SUBLANES, d), F32)], compiler_params=_params("arbitrary"), name=name,
    )(dh, x, g, sc, dres)


def _post_fwd(x, y, g, gt, name):
    s, d = x.shape
    ts = _tile(s, ROW_TILES)

    def body(x_ref, y_ref, g_ref, gt_ref, o_ref):
        yv = y_ref[...]
        r = lax.rsqrt(jnp.mean(yv * yv, axis=-1, keepdims=True) + EPS)
        o_ref[...] = x_ref[...] + gt_ref[...] * ((yv * r) * g_ref[...])

    row, vec = _row_specs(s, d, ts)
    return pl.pallas_call(
        body, grid=(s // ts,), in_specs=[row, row, vec, vec], out_specs=row,
        out_shape=jax.ShapeDtypeStruct((s, d), F32), compiler_params=_params("parallel"), name=name,
    )(x, y, g, gt)


def _post_bwd(dxn, y, g, gt, name):
    s, d = y.shape
    ts = _tile(s, ROW_TILES)

    def body(dxn_ref, y_ref, g_ref, gt_ref, dy_ref, dg_ref, dgt_ref, acc_ref):
        i = pl.program_id(0)

        @pl.when(i == 0)
        def _():
            acc_ref[...] = jnp.zeros_like(acc_ref)

        yv, dv = y_ref[...], dxn_ref[...]
        r = lax.rsqrt(jnp.mean(yv * yv, axis=-1, keepdims=True) + EPS)
        n = yv * r
        acc_ref[0:1, :] += jnp.sum(dv * n, axis=0, keepdims=True)
        dn = dv * (gt_ref[...] * g_ref[...])
        dy_ref[...] = (r * (dn - n * jnp.mean(dn * n, axis=-1, keepdims=True))).astype(dy_ref.dtype)
        dg_ref[...] = gt_ref[...] * acc_ref[0:1, :]
        dgt_ref[...] = g_ref[...] * acc_ref[0:1, :]

    row, vec = _row_specs(s, d, ts)
    vshape = jax.ShapeDtypeStruct((1, d), F32)
    return pl.pallas_call(
        body, grid=(s // ts,), in_specs=[row, row, vec, vec], out_specs=[row, vec, vec],
        out_shape=[jax.ShapeDtypeStruct((s, d), MXU_DTYPE), vshape, vshape],
        scratch_shapes=[pltpu.VMEM((SUBLANES, d), F32)], compiler_params=_params("arbitrary"), name=name,
    )(dxn, y, g, gt)


def _loss_grad(x, tgt, name):
    s, d = x.shape
    ts = _tile(s, ROW_TILES)

    def body(x_ref, t_ref, col_ref, dx_ref):
        i = pl.program_id(0)

        @pl.when(i == 0)
        def _():
            col_ref[...] = jnp.zeros_like(col_ref)

        e = x_ref[...] - t_ref[...]
        col_ref[...] += jnp.sum(e * e, axis=0, keepdims=True)
        dx_ref[...] = e * (1.0 / d)

    row, vec = _row_specs(s, d, ts)
    return pl.pallas_call(
        body, grid=(s // ts,), in_specs=[row, row], out_specs=[vec, row],
        out_shape=[jax.ShapeDtypeStruct((1, d), F32), jax.ShapeDtypeStruct((s, d), F32)],
        compiler_params=_params("arbitrary"), name=name,
    )(x, tgt)


_GELU_C = 0.7978845608028654
_GELU_A = 0.044715


def _gelu(x):
    t = jnp.tanh(_GELU_C * (x + _GELU_A * x * x * x))
    return 0.5 * x * (1.0 + t), t


def _gelu_grad(x, t):
    return 0.5 * (1.0 + t) + 0.5 * x * (1.0 - t * t) * (_GELU_C * (1.0 + 3.0 * _GELU_A * x * x))


def _sigmoid(x):
    return 1.0 / (1.0 + jnp.exp(-x))


def _log1p_pos(y):
    u = 1.0 + y
    return jnp.where(u == 1.0, y, jnp.log(u) * (y / jnp.where(u == 1.0, 1.0, u - 1.0)))


def _softplus(x):
    return jnp.maximum(x, 0.0) + _log1p_pos(jnp.exp(-jnp.abs(x)))


def _one_minus_exp(z):
    u = jnp.exp(z)
    lg = jnp.log(jnp.where(u > 0.0, u, 1.0))
    safe = (u != 1.0) & (u > 0.0)
    return jnp.where(u == 1.0, -z, jnp.where(u > 0.0, (1.0 - u) * (z / jnp.where(safe, lg, 1.0)), 1.0))


SLAB = 16


def _cat(a, b):
    return jnp.concatenate([a, b], axis=1)


def _pair_specs(shape, nb, index):
    return [pl.BlockSpec(shape, lambda j, t: index(j, t) + (j,)), pl.BlockSpec(shape, lambda j, t: index(j, t) + (j + nb,))]


def _halo_row(ts, time_of):
    return lambda j, t: (jnp.maximum(time_of(t) * (ts // SUBLANES) - 1, 0),)


def _rows_from(groups, k):
    row = lax.broadcasted_iota(jnp.int32, groups[0].shape, 0)
    turned = [pltpu.roll(g, SUBLANES - k, axis=0) for g in groups]
    return [jnp.where(row < SUBLANES - k, lo, hi) for lo, hi in zip(turned[:-1], turned[1:])]


def _ffn_mid_fwd(p, cw, cb, name):
    s, f2 = p.shape
    ts = _tile(s, (1024, 512))
    nb, nt = f2 // (2 * CB), s // ts
    n_grp = SLAB // SUBLANES

    def body(pg_ref, pv_ref, hg_ref, hv_ref, cwg_ref, cwv_ref, cbg_ref, cbv_ref, a_ref, ga_ref, gb_ref):
        t = pl.program_id(1)
        cwv, bias = _cat(cwg_ref[...], cwv_ref[...]), _cat(cbg_ref[...], cbv_ref[...])
        w0, w1, w2 = cwv[0:1], cwv[1:2], cwv[2:3]

        def slab(before, cur, r0):
            pm2, pm1 = _rows_from([before] + cur, SUBLANES - 2), _rows_from([before] + cur, SUBLANES - 1)
            u = jnp.concatenate([bias + w0 * pm2[i] + w1 * pm1[i] + w2 * cur[i] for i in range(n_grp)], axis=0)
            g, v = u[:, :CB], u[:, CB:]
            gel, th = _gelu(g)
            rows = pl.ds(r0, SLAB)
            a_ref[rows, :] = (gel * v).astype(a_ref.dtype)
            ga_ref[rows, :] = gel.astype(ga_ref.dtype)
            gb_ref[rows, :] = (v * _gelu_grad(g, th)).astype(gb_ref.dtype)

        def pieces(rows):
            blk = _cat(pg_ref[rows, :], pv_ref[rows, :])
            return [blk[i * SUBLANES:(i + 1) * SUBLANES] for i in range(blk.shape[0] // SUBLANES)]

        slab(jnp.where(t > 0, _cat(hg_ref[...], hv_ref[...]), 0.0), pieces(pl.ds(0, SLAB)), 0)

        def loop(i, carry):
            r0 = pl.multiple_of(i * SLAB, SLAB)
            got = pieces(pl.ds(pl.multiple_of(r0 - SUBLANES, SUBLANES), SLAB + SUBLANES))
            slab(got[0], got[1:], r0)
            return carry

        lax.fori_loop(1, ts // SLAB, loop, 0, unroll=2)

    fwd = lambda t: t
    out = pl.BlockSpec((ts, CB), lambda j, t: (t, j))
    shape = jax.ShapeDtypeStruct((s, f2 // 2), MXU_DTYPE)
    return pl.pallas_call(
        body, grid=(nb, nt),
        in_specs=(_pair_specs((ts, CB), nb, lambda j, t: (t,)) + _pair_specs((SUBLANES, CB), nb, _halo_row(ts, fwd))
                  + _pair_specs((FFN_CONV, CB), nb, lambda j, t: (0,)) + _pair_specs((1, CB), nb, lambda j, t: (0,))),
        out_specs=[out, out, out], out_shape=[shape, shape, shape],
        compiler_params=_params("parallel", "arbitrary"), name=name,
    )(p, p, p, p, cw, cw, cb, cb)


def _ffn_mid_bwd(da, p, ga, gb, cw, name):
    s, f2 = p.shape
    ts = _tile(s, (1024, 512))
    nb, nt = f2 // (2 * CB), s // ts
    n_slab = ts // SLAB
    n_grp = SLAB // SUBLANES
    per_trip = 2

    def body(da_ref, ga_ref, gb_ref, pg_ref, pv_ref, cwg_ref, cwv_ref, dp_ref, dcw_ref, dcb_ref, next_du, acc):
        tt = pl.program_id(1)
        cwv = _cat(cwg_ref[...], cwv_ref[...])
        w0, w1, w2 = cwv[0:1], cwv[1:2], cwv[2:3]

        @pl.when(tt == 0)
        def _():
            next_du[...] = jnp.zeros_like(next_du)
            acc[...] = jnp.zeros_like(acc)

        def slab(r0, after, sums):
            rows = pl.ds(r0, SLAB)
            dav = da_ref[rows, :]
            du = _cat(dav * gb_ref[rows, :].astype(F32), dav * ga_ref[rows, :].astype(F32))
            p0 = _cat(pg_ref[rows, :], pv_ref[rows, :])
            cur = [du[i * SUBLANES:(i + 1) * SUBLANES] for i in range(n_grp)]
            du1, du2 = _rows_from(cur + [after], 1), _rows_from(cur + [after], 2)
            dpv = jnp.concatenate([w2 * cur[i] + w1 * du1[i] + w0 * du2[i] for i in range(n_grp)], axis=0).astype(dp_ref.dtype)
            dp_ref[0, rows, :] = dpv[:, :CB]
            dp_ref[1, rows, :] = dpv[:, CB:]
            for i in range(n_grp):
                pi = p0[i * SUBLANES:(i + 1) * SUBLANES]
                parts = (cur[i], du2[i] * pi, du1[i] * pi, cur[i] * pi)
                sums = parts if sums is None else tuple(x + y for x, y in zip(sums, parts))
            return cur[0], sums

        def loop(k, after):
            sums = None
            for j in range(per_trip):
                r0 = pl.multiple_of((n_slab - 1 - (k * per_trip + j)) * SLAB, SLAB)
                after, sums = slab(r0, after, sums)
            for q, part in enumerate(sums):
                acc[q] += part
            return after

        next_du[...] = lax.fori_loop(0, n_slab // per_trip, loop, next_du[...])

        @pl.when(tt == nt - 1)
        def _():
            for half in range(2):
                cols = slice(half * CB, (half + 1) * CB)
                dcb_ref[half] = jnp.sum(acc[0][:, cols], axis=0, keepdims=True)
                for k in range(FFN_CONV):
                    dcw_ref[half, k:k + 1, :] = jnp.sum(acc[1 + k][:, cols], axis=0, keepdims=True)

    rev = lambda t: nt - 1 - t
    tile = pl.BlockSpec((ts, CB), lambda j, t: (rev(t), j))
    return pl.pallas_call(
        body, grid=(nb, nt),
        in_specs=([tile, tile, tile] + _pair_specs((ts, CB), nb, lambda j, t: (rev(t),))
                  + _pair_specs((FFN_CONV, CB), nb, lambda j, t: (0,))),
        out_specs=[pl.BlockSpec((2, ts, CB), lambda j, t: (0, rev(t), j)),
                   pl.BlockSpec((2, FFN_CONV, CB), lambda j, t: (0, 0, j)),
                   pl.BlockSpec((2, 1, CB), lambda j, t: (0, 0, j))],
        out_shape=[jax.ShapeDtypeStruct((2, s, f2 // 2), MXU_DTYPE), jax.ShapeDtypeStruct((2, FFN_CONV, f2 // 2), F32),
                   jax.ShapeDtypeStruct((2, 1, f2 // 2), F32)],
        scratch_shapes=[pltpu.VMEM((SUBLANES, 2 * CB), F32), pltpu.VMEM((1 + FFN_CONV, SUBLANES, 2 * CB), F32)],
        compiler_params=_params("parallel", "arbitrary"), name=name,
    )(da, ga, gb, p, p, cw, cw)


def _rg_gates(xc, wa_ref, ba_ref, wx_ref, bx_ref, lam_ref):
    r = _sigmoid(_dot_nn(xc, wa_ref[0]) + ba_ref[...])
    ig = _sigmoid(_dot_nn(xc, wx_ref[0]) + bx_ref[...])
    sp = _softplus(-lam_ref[...])
    log_a = (-RG_C) * r * sp
    a = jnp.exp(log_a)
    mult = jnp.sqrt(_one_minus_exp(2.0 * log_a))
    return r, ig, sp, a, mult


def _rg_conv(scr, cw_ref, cb_ref, ts):
    views = [scr[5 + k:5 + k + ts, :] for k in range(RG_CONV)]
    xc = cb_ref[...]
    for k in range(RG_CONV):
        xc = xc + cw_ref[k:k + 1, :] * views[k]
    return xc, views


def _rg_param_specs():
    vec = pl.BlockSpec((1, CB), lambda g, t: (0, g))
    mat = pl.BlockSpec((1, CB, CB), lambda g, t: (g, 0, 0))
    return [pl.BlockSpec((RG_CONV, CB), lambda g, t: (0, g)), vec, mat, vec, mat, vec, vec]


def _scan_rows(a_scr, x_scr, out_ref, carry, ts, reverse):
    row = lax.broadcasted_iota(jnp.int32, (SUBLANES, a_scr.shape[1]), 0)
    last = SUBLANES - 1

    def group(k, c):
        r0 = pl.multiple_of((ts // SUBLANES - 1 - k if reverse else k) * SUBLANES, SUBLANES)
        rows = pl.ds(r0, SUBLANES)
        a, x = a_scr[rows, :], x_scr[rows, :]
        if reverse:
            first_a = a[0:1]
            x = jnp.where(row == last, x + c, x)
            a = jnp.where(row == last, 1.0, pltpu.roll(a, last, axis=0))
            for sh in (1, 2, 4):
                keep = row < SUBLANES - sh
                x = x + a * jnp.where(keep, pltpu.roll(x, SUBLANES - sh, axis=0), 0.0)
                a = a * jnp.where(keep, pltpu.roll(a, SUBLANES - sh, axis=0), 1.0)
            out_ref[rows, :] = x
            return first_a * x[0:1]
        for sh in (1, 2, 4):
            keep = row >= sh
            x = a * jnp.where(keep, pltpu.roll(x, sh, axis=0), 0.0) + x
            a = a * jnp.where(keep, pltpu.roll(a, sh, axis=0), 1.0)
        h = x + a * c
        out_ref[rows, :] = h
        return h[last:last + 1]

    return lax.fori_loop(0, ts // SUBLANES, group, carry, unroll=4)


def _rg_mid_fwd(pj, cw, cb, wa, ba, wx, bx, lam, name):
    s = pj.shape[0]
    nb = pj.shape[1] // (2 * CB)
    ts = _tile(s, (512,))
    nt = s // ts

    def body(gate_ref, x_ref, halo_ref, cw_ref, cb_ref, wa_ref, ba_ref, wx_ref, bx_ref, lam_ref, y_ref, hs_ref,
             scr, a_scr, u_scr, h_scr):
        t = pl.program_id(1)

        @pl.when(t == 0)
        def _():
            h_scr[...] = jnp.zeros_like(h_scr)

        scr[0:SUBLANES, :] = jnp.where(t > 0, halo_ref[...], 0.0)
        scr[SUBLANES:, :] = x_ref[...]
        xc, _ = _rg_conv(scr, cw_ref, cb_ref, ts)
        _, ig, _, a, mult = _rg_gates(xc, wa_ref, ba_ref, wx_ref, bx_ref, lam_ref)
        a_scr[...] = a
        u_scr[...] = mult * (ig * xc)
        h_scr[0:1, :] = _scan_rows(a_scr, u_scr, hs_ref, h_scr[0:1, :], ts, False)
        y_ref[...] = (_gelu(gate_ref[...])[0] * hs_ref[...]).astype(y_ref.dtype)

    blk = pl.BlockSpec((ts, CB), lambda g, t: (t, g))
    return pl.pallas_call(
        body, grid=(nb, nt),
        in_specs=_pair_specs((ts, CB), nb, lambda g, t: (t,))
        + [pl.BlockSpec((SUBLANES, CB), lambda g, t: _halo_row(ts, lambda u: u)(g, t) + (g + nb,))] + _rg_param_specs(),
        out_specs=[blk, blk],
        out_shape=[jax.ShapeDtypeStruct((s, nb * CB), MXU_DTYPE), jax.ShapeDtypeStruct((s, nb * CB), F32)],
        scratch_shapes=[pltpu.VMEM((ts + SUBLANES, CB), F32), pltpu.VMEM((ts, CB), F32), pltpu.VMEM((ts, CB), F32),
                        pltpu.VMEM((SUBLANES, CB), F32)],
        compiler_params=_params("parallel", "arbitrary"), name=name,
    )(pj, pj, pj, cw, cb, wa, ba, wx, bx, lam)


def _rg_mid_bwd(dy, pj, hs, cw, cb, wa, ba, wx, bx, lam, name):
    s = pj.shape[0]
    nb = pj.shape[1] // (2 * CB)
    ts = _tile(s, (512,))
    nt = s // ts

    def body(dy_ref, gate_ref, x_ref, halo_ref, hs_ref, hsh_ref, cw_ref, cb_ref, wa_ref, ba_ref, wx_ref, bx_ref, lam_ref,
             dpj_ref, dcw_ref, dcb_ref, dwa_ref, dba_ref, dwx_ref, dbx_ref, dlam_ref,
             scr, hscr, a_scr, d_scr, g_scr, dxscr, c_scr):
        tt = pl.program_id(1)
        t = nt - 1 - tt

        @pl.when(tt == 0)
        def _():
            c_scr[...] = jnp.zeros_like(c_scr)
            dxscr[ts:, :] = jnp.zeros((SUBLANES, CB), F32)
            for ref in (dcw_ref, dcb_ref, dwa_ref, dba_ref, dwx_ref, dbx_ref, dlam_ref):
                ref[...] = jnp.zeros_like(ref)

        scr[0:SUBLANES, :] = jnp.where(t > 0, halo_ref[...], 0.0)
        scr[SUBLANES:, :] = x_ref[...]
        hscr[0:SUBLANES, :] = jnp.where(t > 0, hsh_ref[...], 0.0)
        hscr[SUBLANES:, :] = hs_ref[...]
        xc, views = _rg_conv(scr, cw_ref, cb_ref, ts)
        r, ig, sp, a, mult = _rg_gates(xc, wa_ref, ba_ref, wx_ref, bx_ref, lam_ref)
        gate = gate_ref[...]
        gel, th = _gelu(gate)
        dyv = dy_ref[...]
        dpj_ref[0] = (dyv * hs_ref[...] * _gelu_grad(gate, th)).astype(dpj_ref.dtype)
        a_scr[...] = a
        d_scr[...] = dyv * gel
        c_scr[0:1, :] = _scan_rows(a_scr, d_scr, g_scr, c_scr[0:1, :], ts, True)
        du = g_scr[...]
        da = du * hscr[7:7 + ts, :]
        dmult = du * (ig * xc)
        dig = du * (mult * xc)
        dxc = du * (mult * ig)
        dlog_a = da * a - dmult * (a * a / mult)
        dlam_ref[...] += jnp.sum(dlog_a * r, axis=0, keepdims=True) * (RG_C * _sigmoid(-lam_ref[...]))
        dpr = dlog_a * ((-RG_C) * sp) * (r * (1.0 - r))
        dpi = dig * (ig * (1.0 - ig))
        dba_ref[...] += jnp.sum(dpr, axis=0, keepdims=True)
        dbx_ref[...] += jnp.sum(dpi, axis=0, keepdims=True)
        dwa_ref[0] += _dot_tn(xc, dpr)
        dwx_ref[0] += _dot_tn(xc, dpi)
        dxc = dxc + _dot_nt(dpr, wa_ref[0]) + _dot_nt(dpi, wx_ref[0])
        dcb_ref[...] += jnp.sum(dxc, axis=0, keepdims=True)
        for k in range(RG_CONV):
            dcw_ref[k:k + 1, :] += jnp.sum(dxc * views[k], axis=0, keepdims=True)
        dxscr[0:ts, :] = dxc
        dxp = cw_ref[3:4, :] * dxc
        for k in range(RG_CONV - 1):
            dxp = dxp + cw_ref[k:k + 1, :] * dxscr[3 - k:3 - k + ts, :]
        dpj_ref[1] = dxp.astype(dpj_ref.dtype)
        dxscr[ts:, :] = dxscr[0:SUBLANES, :]

    rev = lambda g, t: (nt - 1 - t, g)
    rev_halo = lambda g, t: (jnp.maximum((nt - 1 - t) * (ts // SUBLANES) - 1, 0), g)
    vec = pl.BlockSpec((1, CB), lambda g, t: (0, g))
    mat = pl.BlockSpec((1, CB, CB), lambda g, t: (g, 0, 0))
    d = nb * CB
    vshape = jax.ShapeDtypeStruct((1, d), F32)
    mshape = jax.ShapeDtypeStruct((nb, CB, CB), F32)
    return pl.pallas_call(
        body, grid=(nb, nt),
        in_specs=[pl.BlockSpec((ts, CB), rev)] + _pair_specs((ts, CB), nb, lambda g, t: (nt - 1 - t,))
        + [pl.BlockSpec((SUBLANES, CB), lambda g, t: (rev_halo(g, t)[0], g + nb)),
           pl.BlockSpec((ts, CB), rev), pl.BlockSpec((SUBLANES, CB), rev_halo)] + _rg_param_specs(),
        out_specs=[pl.BlockSpec((2, ts, CB), lambda g, t: (0, nt - 1 - t, g)), pl.BlockSpec((RG_CONV, CB), lambda g, t: (0, g)),
                   vec, mat, vec, mat, vec, vec],
        out_shape=[jax.ShapeDtypeStruct((2, s, d), MXU_DTYPE), jax.ShapeDtypeStruct((RG_CONV, d), F32), vshape, mshape, vshape,
                   mshape, vshape, vshape],
        scratch_shapes=[pltpu.VMEM((ts + SUBLANES, CB), F32), pltpu.VMEM((ts + SUBLANES, CB), F32), pltpu.VMEM((ts, CB), F32),
                        pltpu.VMEM((ts, CB), F32), pltpu.VMEM((ts, CB), F32), pltpu.VMEM((ts + SUBLANES, CB), F32),
                        pltpu.VMEM((SUBLANES, CB), F32)],
        compiler_params=_params("parallel", "arbitrary"), name=name,
    )(dy, pj, pj, pj, hs, hs, cw, cb, wa, ba, wx, bx, lam)


GLA_DK = 128
GLA_DV = 256
GLA_O_K = GLA_HEADS * GLA_DK
GLA_O_V = 2 * GLA_HEADS * GLA_DK
GLA_O_R = GLA_O_V + GLA_HEADS * GLA_DV
GLA_O_Z = GLA_O_R + GLA_HEADS * GLA_DV
GLA_IN = GLA_O_Z + GLA_RANK
GLA_TS = 256


def _dk(h, base=0):
    return slice(base + h * GLA_DK, base + (h + 1) * GLA_DK)


def _dv(h, base=0):
    return slice(base + h * GLA_DV, base + (h + 1) * GLA_DV)


def _split3(x):
    hi = x.astype(BF16)
    r1 = x - hi.astype(F32)
    mid = r1.astype(BF16)
    lo = (r1 - mid.astype(F32)).astype(BF16)
    return hi, mid, lo


def _chunk_cumsum(x, reverse):
    n = x.shape[0]
    i = lax.broadcasted_iota(jnp.int32, (n, n), 0)
    j = lax.broadcasted_iota(jnp.int32, (n, n), 1)
    same = (i // GLA_CHUNK) == (j // GLA_CHUNK)
    tri = jnp.where(same & ((j >= i) if reverse else (j <= i)), 1.0, 0.0).astype(BF16)
    out = jnp.zeros(x.shape, F32)
    for piece in _split3(x):
        out = out + lax.dot_general(tri, piece, (((1,), (0,)), ((), ())), preferred_element_type=F32)
    return out


def _gla_head(pj_ref, h):
    return (pj_ref[:, _dk(h)] * (GLA_DK ** -0.5), pj_ref[:, _dk(h, GLA_O_K)], pj_ref[:, _dv(h, GLA_O_V)],
            pj_ref[:, _dv(h, GLA_O_R)])


def _gla_decays(gc):
    gref = gc[GLA_CHUNK // 2:GLA_CHUNK // 2 + 1, :]
    glast = gc[GLA_CHUNK - 1:GLA_CHUNK, :]
    return jnp.exp(gc), jnp.exp(gc - gref), jnp.exp(gref - gc), jnp.exp(glast - gc), jnp.exp(glast)


def _causal_mask():
    i = lax.broadcasted_iota(jnp.int32, (GLA_CHUNK, GLA_CHUNK), 0)
    j = lax.broadcasted_iota(jnp.int32, (GLA_CHUNK, GLA_CHUNK), 1)
    return j <= i


def _log_sigmoid(x):
    return jnp.minimum(x, 0.0) - _log1p_pos(jnp.exp(-jnp.abs(x)))


def _gla_mid_fwd(pj, wal, bal, ng, name):
    s, nh = pj.shape[0], GLA_HEADS
    ts = _tile(s, (GLA_TS,))
    nt, nc = s // ts, ts // GLA_CHUNK

    def body(pj_ref, wal_ref, bal_ref, ng_ref, act_ref, o_ref, st_ref, s_scr):
        t = pl.program_id(0)

        @pl.when(t == 0)
        def _():
            s_scr[...] = jnp.zeros_like(s_scr)

        heads = []
        z = pj_ref[:, GLA_O_Z:]
        for h in range(nh):
            q, k, v, r = _gla_head(pj_ref, h)
            g = _log_sigmoid(_dot_nn(z, wal_ref[:, _dk(h)]) + bal_ref[:, _dk(h)]) * (1.0 / GLA_TAU)
            heads.append((q, k, v, r, _chunk_cumsum(g, False)))
        mask = _causal_mask()
        for c in range(nc):
            sl = slice(c * GLA_CHUNK, (c + 1) * GLA_CHUNK)
            for h, (q, k, v, r, gcum) in enumerate(heads):
                eg, eq, ek, ekd, egl = _gla_decays(gcum[sl])
                st = s_scr[h]
                st_ref[c, h] = st
                attn = jnp.where(mask, _dot_nt(q[sl] * eq, k[sl] * ek), 0.0)
                o_ref[sl, h * GLA_DV:(h + 1) * GLA_DV] = _dot_nt(q[sl] * eg, st) + _dot_nn(attn, v[sl])
                s_scr[h] = st * egl + _dot_tn(v[sl], k[sl] * ekd)
        for h, (q, k, v, r, gcum) in enumerate(heads):
            cols = slice(h * GLA_DV, (h + 1) * GLA_DV)
            o = o_ref[:, cols]
            on = o * lax.rsqrt(jnp.mean(o * o, axis=-1, keepdims=True) + EPS)
            act_ref[:, cols] = ((on * ng_ref[...]) * (r * _sigmoid(r))).astype(act_ref.dtype)

    blk = pl.BlockSpec((ts, nh * GLA_DV), lambda t: (t, 0))
    whole = lambda shape: pl.BlockSpec(shape, lambda t: (0,) * len(shape))
    return pl.pallas_call(
        body, grid=(nt,),
        in_specs=[pl.BlockSpec((ts, GLA_IN), lambda t: (t, 0)), whole((GLA_RANK, nh * GLA_DK)), whole((1, nh * GLA_DK)),
                  whole((1, GLA_DV))],
        out_specs=[blk, blk, pl.BlockSpec((nc, nh, GLA_DV, GLA_DK), lambda t: (t, 0, 0, 0))],
        out_shape=[jax.ShapeDtypeStruct((s, nh * GLA_DV), MXU_DTYPE), jax.ShapeDtypeStruct((s, nh * GLA_DV), F32),
                   jax.ShapeDtypeStruct((s // GLA_CHUNK, nh, GLA_DV, GLA_DK), F32)],
        scratch_shapes=[pltpu.VMEM((nh, GLA_DV, GLA_DK), F32)],
        compiler_params=_params("arbitrary"), name=name,
    )(pj, wal, bal, ng)


def _gla_mid_bwd(dact, pj, o, st, wal, bal, ng, name):
    s, nh = pj.shape[0], GLA_HEADS
    ts = _tile(s, (GLA_TS,))
    nt, nc = s // ts, ts // GLA_CHUNK

    def body(dact_ref, pj_ref, o_ref, st_ref, wal_ref, bal_ref, ng_ref, dpj_ref, dwal_ref, dbal_ref, dng_ref,
             ds_scr, dg_scr):
        tt = pl.program_id(0)

        @pl.when(tt == 0)
        def _():
            ds_scr[...] = jnp.zeros_like(ds_scr)
            dwal_ref[...] = jnp.zeros_like(dwal_ref)
            dbal_ref[...] = jnp.zeros_like(dbal_ref)
            dng_ref[...] = jnp.zeros_like(dng_ref)

        heads = []
        z = pj_ref[:, GLA_O_Z:]
        for h in range(nh):
            q, k, v, r = _gla_head(pj_ref, h)
            logit = _dot_nn(z, wal_ref[:, _dk(h)]) + bal_ref[:, _dk(h)]
            gcum = _chunk_cumsum(_log_sigmoid(logit) * (1.0 / GLA_TAU), False)
            ov = o_ref[:, h * GLA_DV:(h + 1) * GLA_DV]
            ro = lax.rsqrt(jnp.mean(ov * ov, axis=-1, keepdims=True) + EPS)
            on = ov * ro
            sg = _sigmoid(r)
            sil = r * sg
            dav = dact_ref[:, h * GLA_DV:(h + 1) * GLA_DV]
            dpj_ref[:, _dv(h, GLA_O_R)] = (dav * (on * ng_ref[...]) * (sg + sil * (1.0 - sg))).astype(dpj_ref.dtype)
            t1 = dav * sil
            dng_ref[...] += jnp.sum(t1 * on, axis=0, keepdims=True)
            dn = t1 * ng_ref[...]
            do = ro * (dn - on * jnp.mean(dn * on, axis=-1, keepdims=True))
            heads.append((q, k, v, logit, gcum, do))
        mask = _causal_mask()
        scale = GLA_DK ** -0.5
        last_row = lax.broadcasted_iota(jnp.int32, (GLA_CHUNK, GLA_DK), 0) == GLA_CHUNK - 1
        for c in reversed(range(nc)):
            sl = slice(c * GLA_CHUNK, (c + 1) * GLA_CHUNK)
            for h, (q, k, v, logit, gcum, do) in enumerate(heads):
                eg, eq, ek, ekd, egl = _gla_decays(gcum[sl])
                qc, kc, vc, doc = q[sl], k[sl], v[sl], do[sl]
                qg, qt, kt, kd = qc * eg, qc * eq, kc * ek, kc * ekd
                sp = st_ref[c, h]
                ds = ds_scr[h]
                attn = jnp.where(mask, _dot_nt(qt, kt), 0.0)
                dattn = jnp.where(mask, _dot_nt(doc, vc), 0.0)
                dqg = _dot_nn(doc, sp)
                dqt = _dot_nn(dattn, kt)
                dkt = _dot_tn(dattn, qt)
                dkd = _dot_nn(vc, ds)
                dpj_ref[sl, _dv(h, GLA_O_V)] = (_dot_tn(attn, doc) + _dot_nt(kd, ds)).astype(dpj_ref.dtype)
                dpj_ref[sl, _dk(h)] = (scale * (dqg * eg + dqt * eq)).astype(dpj_ref.dtype)
                dpj_ref[sl, _dk(h, GLA_O_K)] = (dkt * ek + dkd * ekd).astype(dpj_ref.dtype)
                kdd = dkd * kd
                dgl = jnp.sum(kdd, axis=0, keepdims=True) + jnp.sum(ds * sp, axis=0, keepdims=True) * egl
                dg_scr[h, sl, :] = dqg * qg + dqt * qt - dkt * kt - kdd + jnp.where(last_row, dgl, 0.0)
                ds_scr[h] = ds * egl + _dot_tn(doc, qg)
        dz = jnp.zeros((ts, GLA_RANK), F32)
        for h, (q, k, v, logit, gcum, do) in enumerate(heads):
            dlogit = _chunk_cumsum(dg_scr[h], True) * (1.0 / GLA_TAU) * _sigmoid(-logit)
            dz = dz + _dot_nt(dlogit, wal_ref[:, _dk(h)])
            dwal_ref[:, _dk(h)] += _dot_tn(z, dlogit)
            dbal_ref[:, _dk(h)] += jnp.sum(dlogit, axis=0, keepdims=True)
        dpj_ref[:, GLA_O_Z:] = dz.astype(dpj_ref.dtype)

    rev = lambda t: (nt - 1 - t, 0)
    whole = lambda shape: pl.BlockSpec(shape, lambda t: (0,) * len(shape))
    wide = pl.BlockSpec((ts, nh * GLA_DV), rev)
    return pl.pallas_call(
        body, grid=(nt,),
        in_specs=[wide, pl.BlockSpec((ts, GLA_IN), rev), wide,
                  pl.BlockSpec((nc, nh, GLA_DV, GLA_DK), lambda t: (nt - 1 - t, 0, 0, 0)),
                  whole((GLA_RANK, nh * GLA_DK)), whole((1, nh * GLA_DK)), whole((1, GLA_DV))],
        out_specs=[pl.BlockSpec((ts, GLA_IN), rev), whole((GLA_RANK, nh * GLA_DK)), whole((1, nh * GLA_DK)), whole((1, GLA_DV))],
        out_shape=[jax.ShapeDtypeStruct((s, GLA_IN), MXU_DTYPE), jax.ShapeDtypeStruct((GLA_RANK, nh * GLA_DK), F32),
                   jax.ShapeDtypeStruct((1, nh * GLA_DK), F32), jax.ShapeDtypeStruct((1, GLA_DV), F32)],
        scratch_shapes=[pltpu.VMEM((nh, GLA_DV, GLA_DK), F32), pltpu.VMEM((nh, ts, GLA_DK), F32)],
        compiler_params=_params("arbitrary"), name=name,
    )(dact, pj, o, st, wal, bal, ng)


def _adamw(w, gs, m, v, name, after=None):
    layers, rows, cols = w.shape
    gs = list(gs) if isinstance(gs, (list, tuple)) else gs
    n_g = len(gs) if isinstance(gs, list) else 1
    if rows % SUBLANES == 0:
        tr, tc = _tile(rows, (256, 128, 64, 32, 16, 8)), cols
    else:
        tr, tc = rows, _tile(cols, (256, 128))
    c1 = 1.0 / (1.0 - ADAM_B1 ** ADAM_STEP)
    c2 = 1.0 / (1.0 - ADAM_B2 ** ADAM_STEP)

    def body(*refs):
        g_refs, (w_ref, m_ref, v_ref) = refs[:n_g], refs[n_g:n_g + 3]
        go_ref, d_ref, mo_ref, vo_ref = refs[-4:]
        gv = g_refs[0][...]
        for l in range(1, n_g):
            gv = jnp.where(pl.program_id(0) == l, g_refs[l][...], gv)
        m2 = ADAM_B1 * m_ref[...] + (1.0 - ADAM_B1) * gv
        v2 = ADAM_B2 * v_ref[...] + (1.0 - ADAM_B2) * (gv * gv)
        d_ref[...] = (-ADAM_LR) * ((m2 * c1) / (jnp.sqrt(v2 * c2) + ADAM_EPS) + ADAM_WD * w_ref[...])
        go_ref[...] = gv
        mo_ref[...] = m2
        vo_ref[...] = v2

    spec = pl.BlockSpec((None, tr, tc), lambda l, i, j: (l, i, j))
    g_specs = [pl.BlockSpec((tr, tc), lambda l, i, j: (i, j))] * n_g if isinstance(gs, list) else [spec]
    extra = [] if after is None else [(after, _ANY)]
    shape = jax.ShapeDtypeStruct((layers, rows, cols), F32)
    return pl.pallas_call(
        body, grid=(layers, rows // tr, cols // tc), in_specs=g_specs + [spec] * 3 + [sp for _, sp in extra],
        out_specs=[spec] * 4, out_shape=[shape] * 4, compiler_params=_params("parallel", "parallel", "parallel"), name=name,
    )(*(gs if isinstance(gs, list) else [gs]), w, m, v, *[a for a, _ in extra])


def _col_slots(w):
    r, c = w.shape
    return jnp.moveaxis(w.reshape(r, N_CHIP, c // N_CHIP), 1, 0)


def _from_col_slots(w):
    n, r, c = w.shape
    return jnp.moveaxis(w, 0, 1).reshape(r, n * c)


def _block_rows_to_slots(w):
    g, r4, cc = w.shape
    return jnp.swapaxes(w.reshape(g, N_CHIP, r4 // N_CHIP, cc), 0, 1).reshape(N_CHIP, g * (r4 // N_CHIP), cc)


def _slots_to_block_rows(w, g):
    n, gr, cc = w.shape
    return jnp.swapaxes(w.reshape(n, g, gr // g, cc), 0, 1).reshape(g, n * (gr // g), cc)


def _local_step(x, tgt, mod, w, fetch=None, done=None, later=None):
    depth = mod.shape[0]
    row = lambda v: v.reshape(1, -1)
    w = dict(w)
    w["ffn_w_up"], w["ffn_w_down"] = dict(enumerate(w["ffn_w_up"])), dict(enumerate(w["ffn_w_down"]))

    def arrive(stage, after):
        if fetch is not None:
            for k, v in fetch(stage, after).items():
                if isinstance(v, dict):
                    w[k].update(v)
                else:
                    w[k] = v

    saved = []
    for i in range(depth):
        if i == 1:
            arrive("l1", x)
        sh_m, sc_m, gt_m, sh_f, sc_f, gt_f = (mod[i, j:j + 1] for j in range(6))
        g0, g1, g2, g3 = (w["norm_g"][i, j:j + 1] for j in range(4))
        tag = f"_l{i}"
        h = _norm_mod_fwd(x, g0, sc_m, sh_m, "norm_mix" + tag)
        if i % 2 == 0:
            pj = _mm(h, w["rg_w_in"], w_slots=N_CHIP, name="rg_in" + tag)
            act, aux = _rg_mid_fwd(pj, w["rg_conv_w"], row(w["rg_conv_b"]), w["rg_wa"], row(w["rg_ba"]), w["rg_wx"],
                                   row(w["rg_bx"]), row(w["rg_lambda"]), "rg_mid" + tag)
            y = _mm(act, w["rg_w_out"], name="rg_out" + tag)
        else:
            pj = _mm(h, w["gla_w_in"], tm_max=512, name="gla_in" + tag)
            act, *aux = _gla_mid_fwd(pj, w["gla_w_alpha"], row(w["gla_b_alpha"]), row(w["gla_norm_g"]), "gla_mid" + tag)
            y = _mm(act, w["gla_w_out"], name="gla_out" + tag)
        x1 = _post_fwd(x, y, g1, gt_m, "post_mix" + tag)
        if i == 0:
            arrive("ffn0", x1)
        h2 = _norm_mod_fwd(x1, g2, sc_f, sh_f, "norm_ffn" + tag)
        p = _mm(h2, w["ffn_w_up"][i], w_slots=N_CHIP, name="ffn_up" + tag)
        a, ga, gb = _ffn_mid_fwd(p, w["ffn_conv_w"][i], w["ffn_conv_b"][i:i + 1], "ffn_mid" + tag)
        y2 = _mm(a, w["ffn_w_down"][i], name="ffn_down" + tag)
        x2 = _post_fwd(x1, y2, g3, gt_f, "post_ffn" + tag)
        saved.append((x, h, pj, act, aux, y, x1, h2, p, (a, ga, gb), y2))
        x = x2

    cols, dx = _loss_grad(x, tgt, "loss")

    stacked = ("norm_g", "ffn_conv_w", "ffn_conv_b", "mod")
    gr = {k: [None] * depth for k in stacked + ("ffn_w_up", "ffn_w_down")}
    told = lambda stage: done(stage, gr) if done is not None else 0.0
    told_later = lambda stage, after: later(stage, after) if later is not None else 0.0
    for i in reversed(range(depth)):
        x0, h, pj, act, aux, y, x1, h2, p, (a, ga, gb), y2 = saved[i]
        sh_m, sc_m, gt_m, sh_f, sc_f, gt_f = (mod[i, j:j + 1] for j in range(6))
        g0, g1, g2, g3 = (w["norm_g"][i, j:j + 1] for j in range(4))
        tag = f"_l{i}"
        dy2, d_g3, d_gt_f = _post_bwd(dx, y2, g3, gt_f, "post_ffn_b" + tag)
        da = _mm(dy2, w["ffn_w_down"][i], tb=True, name="ffn_down_dx" + tag)
        gr["ffn_w_down"][i] = _mm(a, dy2, ta=True, name="ffn_down_dw" + tag)
        conv_w = w["ffn_conv_w"][i] + (told_later("l1", da) if i == 0 else 0.0)
        dp, dcw, dcb = _ffn_mid_bwd(da, p, ga, gb, conv_w, "ffn_mid_b" + tag)
        gr["ffn_conv_w"][i], gr["ffn_conv_b"][i] = _cat(dcw[0], dcw[1]), _cat(dcb[0], dcb[1])[0]
        dh2 = _mm(dp, w["ffn_w_up"][i], tb=True, a_parts=2, w_slots=N_CHIP, name="ffn_up_dx" + tag)
        gr["ffn_w_up"][i] = _mm(h2, dp, ta=True, b_parts=2, out_slots=N_CHIP, name="ffn_up_dw" + tag)
        dx1, d_g2, d_sc_f, d_sh_f = _norm_mod_bwd(dh2, x1, g2, sc_f, dx, "norm_ffn_b" + tag)
        if i == 0:
            gt_m = gt_m + told("ffn0")
        dy, d_g1, d_gt_m = _post_bwd(dx1, y, g1, gt_m, "post_mix_b" + tag)
        if i % 2 == 0:
            dact = _mm(dy, w["rg_w_out"], tb=True, name="rg_out_dx" + tag)
            gr["rg_w_out"] = _mm(act, dy, ta=True, name="rg_out_dw" + tag)
            lam = row(w["rg_lambda"]) + told_later("ffn0", gr["rg_w_out"])
            dpj, gr["rg_conv_w"], d_cb, gr["rg_wa"], d_ba, gr["rg_wx"], d_bx, d_lam = _rg_mid_bwd(
                dact, pj, aux, w["rg_conv_w"], row(w["rg_conv_b"]), w["rg_wa"], row(w["rg_ba"]), w["rg_wx"],
                row(w["rg_bx"]), lam, "rg_mid_b" + tag)
            gr["rg_conv_b"], gr["rg_ba"], gr["rg_bx"], gr["rg_lambda"] = d_cb[0], d_ba[0], d_bx[0], d_lam[0]
            dh = _mm(dpj, w["rg_w_in"], tb=True, a_parts=2, w_slots=N_CHIP, name="rg_in_dx" + tag)
            gr["rg_w_in"] = _mm(h, dpj, ta=True, b_parts=2, out_slots=N_CHIP, name="rg_in_dw" + tag)
        else:
            dact = _mm(dy, w["gla_w_out"], tb=True, name="gla_out_dx" + tag)
            gr["gla_w_out"] = _mm(act, dy, ta=True, name="gla_out_dw" + tag)
            dpj, gr["gla_w_alpha"], d_bal, d_ng = _gla_mid_bwd(dact, pj, aux[0], aux[1], w["gla_w_alpha"], row(w["gla_b_alpha"]),
                                                               row(w["gla_norm_g"]), "gla_mid_b" + tag)
            gr["gla_b_alpha"], gr["gla_norm_g"] = d_bal[0], d_ng[0]
            dh = _mm(dpj, w["gla_w_in"], tb=True, name="gla_in_dx" + tag)
            gr["gla_w_in"] = _mm(h, dpj, ta=True, tm_max=512, name="gla_in_dw" + tag)
            mod = mod.at[0].add(told("l1"))
        dx, d_g0, d_sc_m, d_sh_m = _norm_mod_bwd(dh, x0, g0, sc_m, dx1, "norm_mix_b" + tag)
        gr["norm_g"][i] = jnp.concatenate([d_g0, d_g1, d_g2, d_g3], axis=0)
        gr["mod"][i] = jnp.concatenate([d_sh_m, d_sc_m, d_gt_m, d_sh_f, d_sc_f, d_gt_f], axis=0)
    for k in stacked:
        gr[k] = jnp.stack(gr[k])
    return cols, dx, gr


ADA_ROWS = 16


def _ada_fwd(c16, ada_w, ada_b, name):
    depth, d, n = ada_w.shape
    tn = _tile(n, (512, 256, 128))

    def body(c_ref, w_ref, b_ref, o_ref):
        cv = c_ref[...]
        o_ref[0] = _dot_nn(cv * _sigmoid(cv), w_ref[0]) + b_ref[0]

    return pl.pallas_call(
        body, grid=(depth, n // tn),
        in_specs=[pl.BlockSpec((ADA_ROWS, d), lambda l, j: (0, 0)), pl.BlockSpec((1, d, tn), lambda l, j: (l, 0, j)),
                  pl.BlockSpec((1, 1, tn), lambda l, j: (l, 0, j))],
        out_specs=pl.BlockSpec((1, ADA_ROWS, tn), lambda l, j: (l, 0, j)),
        out_shape=jax.ShapeDtypeStruct((depth, ADA_ROWS, n), F32),
        compiler_params=_params("parallel", "parallel"), name=name,
    )(c16, ada_w, ada_b)


def _ada_bwd(c16, dmod16, name):
    depth, _, n = dmod16.shape
    d = c16.shape[1]
    tn = _tile(n, (512, 256, 128))

    def body(c_ref, dm_ref, o_ref):
        cv = c_ref[...]
        o_ref[0] = _dot_tn(cv * _sigmoid(cv), dm_ref[0])

    return pl.pallas_call(
        body, grid=(depth, n // tn),
        in_specs=[pl.BlockSpec((ADA_ROWS, d), lambda l, j: (0, 0)), pl.BlockSpec((1, ADA_ROWS, tn), lambda l, j: (l, 0, j))],
        out_specs=pl.BlockSpec((1, d, tn), lambda l, j: (l, 0, j)),
        out_shape=jax.ShapeDtypeStruct((depth, d, n), F32),
        compiler_params=_params("parallel", "parallel"), name=name,
    )(c16, dmod16)


PACK_COLS = 1024
_ANY = pl.BlockSpec(memory_space=pl.ANY)
_VMEM = pl.BlockSpec(memory_space=pltpu.VMEM)


def _place():
    return lax.axis_index("x"), lax.axis_index("y"), lax.axis_index("c")


def _other_chips(x, y):
    return [(1 - x, y), (x, 1 - y), (1 - x, 1 - y)]


def _rcopy(src, dst, send_sems, recv_sems, k, peer):
    return pltpu.make_async_remote_copy(src_ref=src, dst_ref=dst, send_sem=send_sems.at[k], recv_sem=recv_sems.at[k],
                                        device_id=peer, device_id_type=MESH)


def _all_gather_8(v, name):
    r, cc = v.shape

    def body(v_ref, out_ref, send_sems, recv_sems, local_sem):
        x, y, c = _place()
        me = 4 * x + 2 * y + c
        mine = pltpu.make_async_copy(v_ref, out_ref.at[me], local_sem)
        mine.start()
        peers = []
        for k in range(1, N_DEV):
            px = 1 - x if k & 4 else x
            py = 1 - y if k & 2 else y
            pc = 1 - c if k & 1 else c
            peers.append((px, py, pc))
        sends = [_rcopy(v_ref, out_ref.at[me], send_sems, recv_sems, k, p) for k, p in enumerate(peers)]
        for cp in sends:
            cp.start()
        for k, (px, py, pc) in enumerate(peers):
            _rcopy(v_ref, out_ref.at[4 * px + 2 * py + pc], send_sems, recv_sems, k, (px, py, pc)).wait_recv()
        for cp in sends:
            cp.wait_send()
        mine.wait()

    return pl.pallas_call(
        body, in_specs=[_VMEM], out_specs=_VMEM, out_shape=jax.ShapeDtypeStruct((N_DEV, r, cc), v.dtype),
        scratch_shapes=[pltpu.SemaphoreType.DMA((N_DEV - 1,)), pltpu.SemaphoreType.DMA((N_DEV - 1,)), pltpu.SemaphoreType.DMA],
        compiler_params=pltpu.CompilerParams(vmem_limit_bytes=VMEM_LIMIT), name=name,
    )(v)


def _gather_chips(shards, name):
    n = len(shards)
    per = 2 * (N_CHIP - 1)

    def body(*refs):
        ins, outs, (send_sems, recv_sems) = refs[:n], refs[n:2 * n], refs[2 * n:]
        x, y, c = _place()
        chip = 2 * x + y
        chips = _other_chips(x, y)
        rows = [(pl.ds(c * (r.shape[0] // 2), r.shape[0] // 2), pl.ds((1 - c) * (r.shape[0] // 2), r.shape[0] // 2)) for r in ins]
        first = [_rcopy(ins[i].at[rows[i][0]], outs[i].at[chip, rows[i][0]], send_sems, recv_sems, per * i + j, (px, py, c))
                 for i in range(n) for j, (px, py) in enumerate(chips)]
        for cp in first:
            cp.start()
        passed = []
        for i in range(n):
            for j, (px, py) in enumerate(chips):
                landed = outs[i].at[2 * px + py, rows[i][0]]
                _rcopy(ins[i].at[rows[i][0]], landed, send_sems, recv_sems, per * i + j, (px, py, c)).wait_recv()
                fw = _rcopy(landed, landed, send_sems, recv_sems, per * i + N_CHIP - 1 + j, (x, y, 1 - c))
                fw.start()
                passed.append(fw)
        for i in range(n):
            for j, (px, py) in enumerate(chips):
                landed = outs[i].at[2 * px + py, rows[i][1]]
                _rcopy(landed, landed, send_sems, recv_sems, per * i + N_CHIP - 1 + j, (x, y, 1 - c)).wait_recv()
        for cp in first + passed:
            cp.wait_send()

    return pl.pallas_call(
        body, in_specs=[_ANY] * n, out_specs=[_ANY] * n,
        out_shape=[jax.ShapeDtypeStruct((N_CHIP,) + sh.shape, sh.dtype) for sh in shards],
        scratch_shapes=[pltpu.SemaphoreType.DMA((per * n,)), pltpu.SemaphoreType.DMA((per * n,))], name=name,
    )(*shards)


def _pair_exchange(gs, name):
    n = len(gs)

    def body(*refs):
        ins, outs, (send_sems, recv_sems) = refs[:n], refs[n:2 * n], refs[2 * n:]
        x, y, c = _place()
        copies = []
        for i in range(n):
            half = ins[i].shape[1] // 2
            copies.append(_rcopy(ins[i].at[:, pl.ds((1 - c) * half, half)], outs[i], send_sems, recv_sems, i, (x, y, 1 - c)))
        for cp in copies:
            cp.start()
        for cp in copies:
            cp.wait()

    return pl.pallas_call(
        body, in_specs=[_ANY] * n, out_specs=[_ANY] * n,
        out_shape=[jax.ShapeDtypeStruct((g.shape[0], g.shape[1] // 2, g.shape[2]), g.dtype) for g in gs],
        scratch_shapes=[pltpu.SemaphoreType.DMA((n,)), pltpu.SemaphoreType.DMA((n,))], name=name,
    )(*gs)


_ROW_TILES = (640, 512, 352, 256, 128, 64, 32, 16)


def _pair_sum(g, other, c_idx, name):
    n, half, cc = other.shape
    tr = _tile(half, _ROW_TILES)

    def body(c_ref, g_ref, o_ref, out_ref):
        out_ref[...] = (g_ref[...] + o_ref[...]).astype(out_ref.dtype)

    return pl.pallas_call(
        body,
        grid_spec=pltpu.PrefetchScalarGridSpec(
            num_scalar_prefetch=1, grid=(n, half // tr),
            in_specs=[pl.BlockSpec((None, None, tr, cc), lambda k, i, c_ref: (k, c_ref[0], i, 0)),
                      pl.BlockSpec((None, tr, cc), lambda k, i, c_ref: (k, i, 0))],
            out_specs=pl.BlockSpec((None, tr, cc), lambda k, i, c_ref: (k, i, 0))),
        out_shape=jax.ShapeDtypeStruct((n, half, cc), BF16),
        compiler_params=_params("parallel", "parallel"), name=name,
    )(c_idx, g.reshape(n, 2, half, cc), other)


def _chip_exchange(ps, name):
    n = len(ps)
    per = N_CHIP - 1

    def body(*refs):
        ins, outs, (send_sems, recv_sems) = refs[:n], refs[n:2 * n], refs[2 * n:]
        x, y, c = _place()
        chip = 2 * x + y
        chips = _other_chips(x, y)
        sends = [_rcopy(ins[i].at[2 * px + py], outs[i].at[chip], send_sems, recv_sems, per * i + j, (px, py, c))
                 for i in range(n) for j, (px, py) in enumerate(chips)]
        for cp in sends:
            cp.start()
        for i in range(n):
            for j, (px, py) in enumerate(chips):
                _rcopy(ins[i].at[chip], outs[i].at[2 * px + py], send_sems, recv_sems, per * i + j, (px, py, c)).wait_recv()
        for cp in sends:
            cp.wait_send()

    return pl.pallas_call(
        body, in_specs=[_ANY] * n, out_specs=[_ANY] * n, out_shape=[jax.ShapeDtypeStruct(p.shape, p.dtype) for p in ps],
        scratch_shapes=[pltpu.SemaphoreType.DMA((per * n,)), pltpu.SemaphoreType.DMA((per * n,))], name=name,
    )(*ps)


_HBM = pl.BlockSpec(memory_space=pltpu.HBM)
_SEM = pl.BlockSpec(memory_space=pltpu.SEMAPHORE)
_DATAFLOW = pltpu.SideEffectType.DATAFLOW_SIDE_EFFECTING


def _split_copies(srcs, lands, send_sems, recv_sems, mode, arriving):
    x, y, c = _place()
    chip = 2 * x + y
    out = []
    for i, (src, land) in enumerate(zip(srcs, lands)):
        if mode == "pair":
            half = src.shape[1] // 2
            out.append(_rcopy(src.at[:, pl.ds((1 - c) * half, half)], land, send_sems, recv_sems, i, (x, y, 1 - c)))
            continue
        for j, (px, py) in enumerate(_other_chips(x, y)):
            there = 2 * px + py
            part = src.at[there] if mode == "slots" else src
            out.append(_rcopy(part, land.at[there if arriving else chip], send_sems, recv_sems, (N_CHIP - 1) * i + j, (px, py, c)))
    return out


def _land_shape(src, mode):
    if mode == "pair":
        return (src.shape[0], src.shape[1] // 2, src.shape[2])
    return (N_CHIP,) + (src.shape[1:] if mode == "slots" else src.shape)


def _send_start(srcs, mode, name):
    n = len(srcs)
    n_sem = n if mode == "pair" else (N_CHIP - 1) * n
    lands = [lax.empty(_land_shape(s, mode), s.dtype) for s in srcs]

    def body(*refs):
        ins, zones, (send_sems, recv_sems) = refs[:n], refs[n:2 * n], refs[2 * n:2 * n + 2]
        for cp in _split_copies(ins, zones, send_sems, recv_sems, mode, False):
            cp.start()
        refs[-1][...] = jnp.zeros_like(refs[-1])

    hbm = lambda a: pltpu.HBM(a.shape, a.dtype)
    outs = pl.pallas_call(
        body, name=name, in_specs=[_HBM] * (2 * n),
        out_shape=(pltpu.SemaphoreType.DMA((n_sem,)), pltpu.SemaphoreType.DMA((n_sem,)), *[hbm(a) for a in srcs],
                   *[hbm(a) for a in lands], jax.ShapeDtypeStruct((SUBLANES, LANES), F32)),
        out_specs=(_SEM, _SEM, *[_HBM] * (2 * n), _VMEM), input_output_aliases={i: 2 + i for i in range(2 * n)},
        compiler_params=pltpu.CompilerParams(has_side_effects=_DATAFLOW),
    )(*[pltpu.with_memory_space_constraint(a, pltpu.HBM) for a in list(srcs) + lands])
    return (outs[0], outs[1], list(outs[2:2 + n]), list(outs[2 + n:2 + 2 * n])), outs[-1]


def _send_wait(state, after, mode, name):
    send_sems, recv_sems, srcs, lands = state
    n = len(srcs)

    def body(*refs):
        ins, zones, (send_s, recv_s) = refs[:n], refs[n:2 * n], refs[2 * n:2 * n + 2]
        for cp in _split_copies(ins, zones, send_s, recv_s, mode, True):
            cp.wait_send()
            cp.wait_recv()

    hbm = lambda a: pltpu.HBM(a.shape, a.dtype)
    outs = pl.pallas_call(
        body, name=name, in_specs=[_HBM] * (2 * n) + [_SEM, _SEM, _ANY],
        out_shape=tuple(hbm(a) for a in srcs + lands), out_specs=tuple([_HBM] * (2 * n)),
        input_output_aliases={i: i for i in range(2 * n)},
        compiler_params=pltpu.CompilerParams(has_side_effects=_DATAFLOW),
    )(*srcs, *lands, send_sems, recv_sems, after)
    return list(outs[n:])


def _sum_lead(v, name):
    n, r, cc = v.shape
    tr = _tile(r, _ROW_TILES + (8,))

    def body(v_ref, o_ref):
        acc = v_ref[0].astype(F32)
        for k in range(1, n):
            acc = acc + v_ref[k].astype(F32)
        o_ref[...] = acc

    return pl.pallas_call(
        body, grid=(r // tr,), in_specs=[pl.BlockSpec((n, tr, cc), lambda i: (0, i, 0))],
        out_specs=pl.BlockSpec((tr, cc), lambda i: (i, 0)), out_shape=jax.ShapeDtypeStruct((r, cc), F32),
        compiler_params=_params("parallel"), name=name,
    )(v)


def _chip_sum(arrived, mine, chip_idx, name):
    n, r, cc = arrived.shape
    tr = _tile(r, _ROW_TILES)

    def body(chip_ref, a_ref, m_ref, o_ref):
        acc = jnp.zeros((tr, cc), F32)
        for k in range(n):
            acc = acc + jnp.where(chip_ref[0] == k, m_ref[...], a_ref[k]).astype(F32)
        o_ref[...] = acc

    return pl.pallas_call(
        body,
        grid_spec=pltpu.PrefetchScalarGridSpec(
            num_scalar_prefetch=1, grid=(r // tr,),
            in_specs=[pl.BlockSpec((n, tr, cc), lambda i, chip_ref: (0, i, 0)),
                      pl.BlockSpec((None, tr, cc), lambda i, chip_ref: (chip_ref[0], i, 0))],
            out_specs=pl.BlockSpec((tr, cc), lambda i, chip_ref: (i, 0))),
        out_shape=jax.ShapeDtypeStruct((r, cc), F32), compiler_params=_params("parallel"), name=name,
    )(chip_idx, arrived, mine)


def _pair_share(reds, name):
    n = len(reds)

    def body(*refs):
        ins, outs, (send_sems, recv_sems) = refs[:n], refs[n:2 * n], refs[2 * n:]
        x, y, c = _place()
        copies = [_rcopy(ins[i], outs[i].at[c], send_sems, recv_sems, i, (x, y, 1 - c)) for i in range(n)]
        for cp in copies:
            cp.start()
        for i in range(n):
            _rcopy(ins[i], outs[i].at[1 - c], send_sems, recv_sems, i, (x, y, 1 - c)).wait_recv()
        for cp in copies:
            cp.wait_send()

    return pl.pallas_call(
        body, in_specs=[_ANY] * n, out_specs=[_ANY] * n, out_shape=[jax.ShapeDtypeStruct((2,) + r.shape, r.dtype) for r in reds],
        scratch_shapes=[pltpu.SemaphoreType.DMA((n,)), pltpu.SemaphoreType.DMA((n,))], name=name,
    )(*reds)


def _pack(arrs, rows_multiple, dtype):
    flat = jnp.concatenate([a.reshape(-1).astype(dtype) for a in arrs])
    unit = rows_multiple * PACK_COLS
    total = -(-flat.shape[0] // unit) * unit
    return jnp.pad(flat, (0, total - flat.shape[0])).reshape(-1, PACK_COLS)


def _unpack(buf, shapes):
    lead = buf.shape[:-2]
    flat = buf.reshape(*lead, -1)
    out, off = [], 0
    for shp in shapes:
        n = 1
        for s in shp:
            n *= s
        out.append(flat[..., off:off + n].reshape(*lead, *shp))
        off += n
    return out


def _join_shards(parts, axis):
    moved = jnp.moveaxis(parts, 0, axis)
    shp = list(moved.shape)
    shp[axis:axis + 2] = [shp[axis] * shp[axis + 1]]
    return moved.reshape(shp)


def _my_shard(full, axis, chip):
    n = full.shape[axis] // N_CHIP
    return lax.dynamic_slice_in_dim(full, chip * n, n, axis)


SMALL = {"norm_g": 2, "ffn_conv_w": 2, "rg_conv_w": 2, "gla_w_alpha": 2, "gla_b_alpha": 1, "gla_norm_g": 1,
         "ada_b": None, "ffn_conv_b": None, "rg_conv_b": None, "rg_ba": None, "rg_bx": None, "rg_lambda": None}
BIG = {"rg_w_in": True, "rg_wa": False, "rg_wx": False, "rg_w_out": False, "ffn_w_up": True, "ffn_w_down": False,
       "gla_w_in": True, "gla_w_out": False}
WEIGHTS = ["ada_w", "ada_b", "norm_g", "ffn_w_up", "ffn_conv_w", "ffn_conv_b", "ffn_w_down", "rg_w_in", "rg_conv_w", "rg_conv_b",
           "rg_wa", "rg_ba", "rg_wx", "rg_bx", "rg_lambda", "rg_w_out", "gla_w_in", "gla_w_alpha", "gla_b_alpha", "gla_norm_g",
           "gla_w_out"]


def kernel(x, c, ada_w, ada_b, norm_g, ffn_w_up, ffn_conv_w, ffn_conv_b, ffn_w_down, rg_w_in, rg_conv_w, rg_conv_b, rg_wa, rg_ba, rg_wx, rg_bx, rg_lambda, rg_w_out, gla_w_in, gla_w_alpha, gla_b_alpha, gla_norm_g, gla_w_out, loss_target, m_ada_w, m_ada_b, m_norm_g, m_ffn_w_up, m_ffn_conv_w, m_ffn_conv_b, m_ffn_w_down, m_rg_w_in, m_rg_conv_w, m_rg_conv_b, m_rg_wa, m_rg_ba, m_rg_wx, m_rg_bx, m_rg_lambda, m_rg_w_out, m_gla_w_in, m_gla_w_alpha, m_gla_b_alpha, m_gla_norm_g, m_gla_w_out, v_ada_w, v_ada_b, v_norm_g, v_ffn_w_up, v_ffn_conv_w, v_ffn_conv_b, v_ffn_w_down, v_rg_w_in, v_rg_conv_w, v_rg_conv_b, v_rg_wa, v_rg_ba, v_rg_wx, v_rg_bx, v_rg_lambda, v_rg_w_out, v_gla_w_in, v_gla_w_alpha, v_gla_b_alpha, v_gla_norm_g, v_gla_w_out):
    wts = dict(ada_w=ada_w, ada_b=ada_b, norm_g=norm_g, ffn_w_up=ffn_w_up, ffn_conv_w=ffn_conv_w, ffn_conv_b=ffn_conv_b,
               ffn_w_down=ffn_w_down, rg_w_in=rg_w_in, rg_conv_w=rg_conv_w, rg_conv_b=rg_conv_b, rg_wa=rg_wa, rg_ba=rg_ba,
               rg_wx=rg_wx, rg_bx=rg_bx, rg_lambda=rg_lambda, rg_w_out=rg_w_out, gla_w_in=gla_w_in, gla_w_alpha=gla_w_alpha,
               gla_b_alpha=gla_b_alpha, gla_norm_g=gla_norm_g, gla_w_out=gla_w_out)
    mom1 = dict(ada_w=m_ada_w, ada_b=m_ada_b, norm_g=m_norm_g, ffn_w_up=m_ffn_w_up, ffn_conv_w=m_ffn_conv_w,
                ffn_conv_b=m_ffn_conv_b, ffn_w_down=m_ffn_w_down, rg_w_in=m_rg_w_in, rg_conv_w=m_rg_conv_w,
                rg_conv_b=m_rg_conv_b, rg_wa=m_rg_wa, rg_ba=m_rg_ba, rg_wx=m_rg_wx, rg_bx=m_rg_bx, rg_lambda=m_rg_lambda,
                rg_w_out=m_rg_w_out, gla_w_in=m_gla_w_in, gla_w_alpha=m_gla_w_alpha, gla_b_alpha=m_gla_b_alpha,
                gla_norm_g=m_gla_norm_g, gla_w_out=m_gla_w_out)
    mom2 = dict(ada_w=v_ada_w, ada_b=v_ada_b, norm_g=v_norm_g, ffn_w_up=v_ffn_w_up, ffn_conv_w=v_ffn_conv_w,
                ffn_conv_b=v_ffn_conv_b, ffn_w_down=v_ffn_w_down, rg_w_in=v_rg_w_in, rg_conv_w=v_rg_conv_w,
                rg_conv_b=v_rg_conv_b, rg_wa=v_rg_wa, rg_ba=v_rg_ba, rg_wx=v_rg_wx, rg_bx=v_rg_bx, rg_lambda=v_rg_lambda,
                rg_w_out=v_rg_w_out, gla_w_in=v_gla_w_in, gla_w_alpha=v_gla_w_alpha, gla_b_alpha=v_gla_b_alpha,
                gla_norm_g=v_gla_norm_g, gla_w_out=v_gla_w_out)
    xi, yi, ci = _place()
    chip, me = 2 * xi + yi, 4 * xi + 2 * yi + ci
    d = x.shape[-1]
    depth = ada_w.shape[0]
    n_ada = ada_w.shape[-1]
    sharded_small = [k for k, ax in SMALL.items() if ax is not None]

    sm = _all_gather_8(_pack([c] + [wts[k] for k in sharded_small], SUBLANES, F32), "gather_small")
    c_all = sm[:, 0, :]
    parts = _unpack(sm[0::2], [c.shape] + [wts[k].shape for k in sharded_small])[1:]
    full = {k: _join_shards(p, SMALL[k]) for k, p in zip(sharded_small, parts)}
    for k, ax in SMALL.items():
        if ax is None:
            full[k] = wts[k]

    c16 = jnp.pad(c_all, ((0, ADA_ROWS - N_DEV), (0, 0)))
    ada_b_mine = lax.dynamic_slice_in_dim(ada_b, chip * n_ada, n_ada, 1)[:, None, :]
    mod_cols = _ada_fwd(c16, ada_w, ada_b_mine, "ada_fwd")
    mod_all = _all_gather_8(mod_cols.reshape(-1, PACK_COLS), "gather_mod")[0::2].reshape(N_CHIP, depth, ADA_ROWS, n_ada)
    mod = jnp.swapaxes(lax.dynamic_index_in_dim(mod_all, me, 2, keepdims=False), 0, 1).reshape(depth, 6, d)

    items = [(k, l) for k in BIG for l in range(wts[k].shape[0])]
    stage_of = lambda k, l: "rg" if k.startswith("rg_") else ("ffn0" if (k.startswith("ffn_") and l == 0) else "l1")
    staged = {st: [it for it in items if stage_of(*it) == st] for st in ("rg", "ffn0", "l1")}
    staged["l1"].sort(key=lambda it: not it[0].startswith("gla_"))
    shard = lambda k, l: wts[k][l].reshape(-1, wts[k].shape[-1]).astype(BF16)
    own = lambda got, mine: [lax.dynamic_update_index_in_dim(g, m, chip, 0) for g, m in zip(got, mine)]
    rows_joined = lambda v: v.reshape(-1, v.shape[-1])

    def placed(its, slots):
        out = {"ffn_w_up": {}, "ffn_w_down": {}}
        for (k, l), v in zip(its, slots):
            if k == "ffn_w_up":
                out[k][l] = v
            elif k == "ffn_w_down":
                out[k][l] = rows_joined(v)
            elif k in ("rg_wa", "rg_wx"):
                out[k] = _slots_to_block_rows(v, RG_BLOCKS)
            elif k == "gla_w_in":
                out[k] = _from_col_slots(v)
            else:
                out[k] = v if BIG[k] else rows_joined(v)
        return out

    after_mod = (mod[0, 0, 0] * 0.0).astype(BF16)
    sh_rg = [shard(k, l) + after_mod for k, l in staged["rg"]]
    local = {k: (v if k in ("norm_g", "ffn_conv_w", "ffn_conv_b") else v[0]) for k, v in full.items()}
    local.update(placed(staged["rg"], own(_gather_chips(sh_rg, "gather_weights_rg"), sh_rg)))
    sh_late, flying = {}, {}
    after_rg = (local["rg_w_out"][0, 0].astype(F32) * 0.0).astype(BF16)
    sh_late["ffn0"] = [shard(k, l) + after_rg for k, l in staged["ffn0"]]
    flying["ffn0"], tok = _send_start(sh_late["ffn0"], "whole", "weights_ffn0_start")
    sh_late["l1"] = [shard(k, l) + tok[0, 0].astype(BF16) for k, l in staged["l1"]]
    flying["l1"], tok2 = _send_start(sh_late["l1"], "whole", "weights_l1_start")
    mod = mod + (tok[0, 0] + tok2[0, 0])

    def fetch(stage, after):
        got = _send_wait(flying[stage], after, "whole", f"weights_{stage}_wait")
        return placed(staged[stage], own(got, sh_late[stage]))

    c_idx = ci.reshape(1).astype(jnp.int32)
    gslots, paired, psums, sent = {}, {}, {}, {}

    def grad_slots(gr, k, l):
        g = gr[k][l] if k in ("ffn_w_up", "ffn_w_down") else gr[k]
        if k in ("rg_wa", "rg_wx"):
            return _block_rows_to_slots(g)
        if k == "gla_w_in":
            return _col_slots(g)
        return g if BIG[k] else g.reshape(N_CHIP, -1, g.shape[-1])

    def done(stage, gr):
        gslots[stage] = [grad_slots(gr, k, l) for k, l in staged[stage]]
        paired[stage], token = _send_start(gslots[stage], "pair", f"grads_{stage}_pair_start")
        return token[0, 0]

    def later(stage, after):
        theirs = _send_wait(paired[stage], after, "pair", f"grads_{stage}_pair_wait")
        psums[stage] = [_pair_sum(g, t, c_idx, f"grads_pair_sum_{k}{l}") for (k, l), g, t in zip(staged[stage], gslots[stage], theirs)]
        sent[stage], token = _send_start(psums[stage], "slots", f"grads_{stage}_start")
        return token[0, 0]

    cols, grad_x, gr = _local_step(x[0], loss_target[0], mod, local, fetch, done, later)
    loss = lax.psum(0.5 * jnp.sum(cols) / d, ("x", "y", "c"))

    small_names = [k for k in SMALL if k != "ada_b"]
    gs = _all_gather_8(_pack([gr[k] for k in small_names] + [gr["mod"]], SUBLANES, F32), "gather_small_grads")
    small_shapes = [full[k].shape for k in small_names] + [(depth, 6 * d)]
    *small_sum, g_ada_b = _unpack(_sum_lead(gs, "sum_small_grads"), small_shapes)
    grads = dict(zip(small_names, small_sum))
    grads["ada_b"] = g_ada_b
    for k in sharded_small:
        grads[k] = _my_shard(grads[k], SMALL[k], chip)
    dmod_all = _unpack(gs, small_shapes)[-1].reshape(N_DEV, depth, N_CHIP, n_ada)
    dmod_mine = jnp.swapaxes(lax.dynamic_index_in_dim(dmod_all, chip, 2, keepdims=False), 0, 1)
    g_ada_w = _ada_bwd(c16, jnp.pad(dmod_mine, ((0, 0), (0, ADA_ROWS - N_DEV), (0, 0))), "ada_bwd")

    gslots["rg"] = [grad_slots(gr, k, l) for k, l in staged["rg"]]
    theirs = _pair_exchange(gslots["rg"], "grads_rg_pair_exchange")
    psums["rg"] = [_pair_sum(g, t, c_idx, f"grads_pair_sum_{k}{l}") for (k, l), g, t in zip(staged["rg"], gslots["rg"], theirs)]
    sent["rg"], rg_sent = _send_start(psums["rg"], "slots", "grads_rg_start")
    chip_idx = chip.reshape(1).astype(jnp.int32)
    delta, new_m, new_v = {}, {}, {}

    def reduce_and_update(stages, after, dep):
        its = [(st, n) for st in stages for n in range(len(staged[st]))]
        arrived = {st: _send_wait(sent[st], after, "slots", f"grads_{st}_wait") for st in stages}
        halves = [_chip_sum(arrived[st][n], psums[st][n], chip_idx, "grads_chip_sum_%s%d" % staged[st][n]) for st, n in its]
        shared = _pair_share(halves, "grads_pair_share_" + stages[0])
        reduced = [lax.dynamic_update_index_in_dim(s2, h, ci, 0).reshape(-1, h.shape[-1]) for s2, h in zip(shared, halves)]
        last = None
        for k in BIG:
            gs_k = [g for (st, n), g in zip(its, reduced) if staged[st][n][0] == k]
            if gs_k:
                last = update(k, gs_k, dep)
        return last

    def update(k, gs_k, dep=None):
        shp = wts[k].shape
        if k == "gla_w_in":
            view, back = (lambda a: jnp.swapaxes(a, 1, 2)), (lambda o: jnp.swapaxes(o, 1, 2))
            gs_k = [g.T for g in gs_k]
        else:
            view, back = (lambda a: a.reshape(a.shape[0], -1, a.shape[-1])), (lambda o: o.reshape(shp))
        outs = _adamw(view(wts[k]), gs_k, view(mom1[k]), view(mom2[k]), "adamw_" + k, dep)
        grads[k], delta[k], new_m[k], new_v[k] = (back(o) for o in outs)
        return new_v[k]

    done_late = reduce_and_update(("ffn0", "l1"), grad_x, rg_sent)
    update("ada_w", g_ada_w, rg_sent)
    small_shard_shapes = [wts[k].shape for k in SMALL]
    packed = [_pack([src[k] for k in SMALL], SUBLANES, F32) for src in (wts, grads, mom1, mom2)]
    outs = _adamw(packed[0][None], [packed[1]], packed[2][None], packed[3][None], "adamw_small", rg_sent)
    for dst, o in zip((delta, new_m, new_v), outs[1:]):
        for k, a in zip(SMALL, _unpack(o[0], small_shard_shapes)):
            dst[k] = a
    reduce_and_update(("rg",), done_late, None)

    return (loss, grad_x[None], *[grads[k] for k in WEIGHTS], *[delta[k] for k in WEIGHTS], *[new_m[k] for k in WEIGHTS],
            *[new_v[k] for k in WEIGHTS])
```

```python
import jax
import jax.numpy as jnp
from jax import lax
from jax.experimental import pallas as pl
from jax.experimental.pallas import tpu as pltpu

F32 = jnp.float32
BF16 = jnp.bfloat16
MXU_DTYPE = BF16

EPS = 1e-6
RG_C = 8.0
RG_BLOCKS = 4
RG_CONV = 4
GLA_HEADS = 4
GLA_TAU = 16.0
GLA_CHUNK = 64
GLA_RANK = 16
FFN_CONV = 3
ADAM_LR = 0.001
ADAM_B1 = 0.9
ADAM_B2 = 0.999
ADAM_EPS = 1e-08
ADAM_WD = 0.01
ADAM_STEP = 10

LANES = 128
SUBLANES = 8
VMEM_LIMIT = 56 * 1024 * 1024
CB = 256
MESH = pl.DeviceIdType.MESH
N_DEV = 8
N_CHIP = 4


def _params(*sem):
    return pltpu.CompilerParams(dimension_semantics=sem, vmem_limit_bytes=VMEM_LIMIT)


def _tile(dim, prefs):
    for p in prefs:
        if dim % p == 0:
            return p
    return dim


def _dot(a, b, dims):
    return lax.dot_general(a.astype(MXU_DTYPE), b.astype(MXU_DTYPE), (dims, ((), ())), preferred_element_type=F32)


def _dot_nn(a, b):
    return _dot(a, b, ((1,), (0,)))


def _dot_nt(a, b):
    return _dot(a, b, ((1,), (1,)))


def _dot_tn(a, b):
    return _dot(a, b, ((0,), (0,)))


def _mm(a, b, *, ta=False, tb=False, a_parts=1, b_parts=1, w_slots=1, out_slots=1, out_dtype=F32, tm_max=1024, name):
    if ta:
        k_dim, m_dim = a.shape
        n_dim = b.shape[-1] * b_parts
    else:
        m_dim, k_dim = a.shape[-2], a.shape[-1] * a_parts
        n_dim = b.shape[-2] if tb else b.shape[-1] * w_slots
    n_unit = n_dim // max(b_parts, out_slots, 1 if tb else w_slots)
    k_unit = k_dim // max(a_parts, w_slots if tb else 1)
    tm = _tile(m_dim, tuple(t for t in (1024, 1408, 512, 256, 128) if t <= max(tm_max, 128)))
    tn = _tile(n_unit, (1024, 1408, 896, 512, 256, 128))
    tk = _tile(k_unit, (1024, 1408, 896, 512, 256, 128))
    nk = k_dim // tk
    dims = ((0 if ta else 1,), (1 if tb else 0,))

    def spec(shape, parts, total, tile, col_grid, row_grid):
        per = total // parts // tile

        def index(i, j, k):
            g = {"i": i, "j": j, "k": k}
            col, row = g[col_grid], g[row_grid]
            return (row, col) if parts == 1 else (col // per, row, col % per)

        return pl.BlockSpec(shape if parts == 1 else (None,) + shape, index)

    def body(a_ref, b_ref, o_ref, *acc):
        if nk == 1:
            o_ref[...] = _dot(a_ref[...], b_ref[...], dims).astype(o_ref.dtype)
            return
        acc_ref, k = acc[0], pl.program_id(2)

        @pl.when(k == 0)
        def _():
            acc_ref[...] = jnp.zeros_like(acc_ref)

        acc_ref[...] += _dot(a_ref[...], b_ref[...], dims)

        @pl.when(k == nk - 1)
        def _():
            o_ref[...] = acc_ref[...].astype(o_ref.dtype)

    if ta:
        a_spec = spec((tk, tm), 1, m_dim, tm, "i", "k")
        b_spec = spec((tk, tn), b_parts, n_dim, tn, "j", "k")
    elif tb:
        a_spec = spec((tm, tk), a_parts, k_dim, tk, "k", "i")
        b_spec = spec((tn, tk), w_slots, k_dim, tk, "k", "j")
    else:
        a_spec = spec((tm, tk), a_parts, k_dim, tk, "k", "i")
        b_spec = spec((tk, tn), w_slots, n_dim, tn, "j", "k")
    out_shape = (m_dim, n_dim) if out_slots == 1 else (out_slots, m_dim, n_dim // out_slots)
    return pl.pallas_call(
        body,
        grid=(m_dim // tm, n_dim // tn, nk),
        in_specs=[a_spec, b_spec],
        out_specs=spec((tm, tn), out_slots, n_dim, tn, "j", "i"),
        out_shape=jax.ShapeDtypeStruct(out_shape, out_dtype),
        scratch_shapes=[pltpu.VMEM((tm, tn), F32)] if nk > 1 else [],
        compiler_params=_params("parallel", "parallel", "arbitrary"),
        name=name,
    )(a, b)


ROW_TILES = (1024, 512)


def _row_specs(s, d, ts):
    return pl.BlockSpec((ts, d), lambda i: (i, 0)), pl.BlockSpec((1, d), lambda i: (0, 0))


def _norm_mod_fwd(x, g, sc, sh, name):
    s, d = x.shape
    ts = _tile(s, ROW_TILES)

    def body(x_ref, g_ref, sc_ref, sh_ref, h_ref):
        xv = x_ref[...]
        r = lax.rsqrt(jnp.mean(xv * xv, axis=-1, keepdims=True) + EPS)
        h_ref[...] = (((xv * r) * g_ref[...]) * (1.0 + sc_ref[...]) + sh_ref[...]).astype(h_ref.dtype)

    row, vec = _row_specs(s, d, ts)
    return pl.pallas_call(
        body, grid=(s // ts,), in_specs=[row, vec, vec, vec], out_specs=row,
        out_shape=jax.ShapeDtypeStruct((s, d), MXU_DTYPE), compiler_params=_params("parallel"), name=name,
    )(x, g, sc, sh)


def _norm_mod_bwd(dh, x, g, sc, dres, name):
    s, d = x.shape
    ts = _tile(s, ROW_TILES)

    def body(dh_ref, x_ref, g_ref, sc_ref, dres_ref, dx_ref, dg_ref, dsc_ref, dsh_ref, acc_ref):
        i = pl.program_id(0)

        @pl.when(i == 0)
        def _():
            acc_ref[...] = jnp.zeros_like(acc_ref)

        xv, dhv = x_ref[...], dh_ref[...]
        r = lax.rsqrt(jnp.mean(xv * xv, axis=-1, keepdims=True) + EPS)
        n = xv * r
        acc_ref[0:1, :] += jnp.sum(dhv * n, axis=0, keepdims=True)
        acc_ref[1:2, :] += jnp.sum(dhv, axis=0, keepdims=True)
        dn = dhv * ((1.0 + sc_ref[...]) * g_ref[...])
        dx_ref[...] = dres_ref[...] + r * (dn - n * jnp.mean(dn * n, axis=-1, keepdims=True))
        dg_ref[...] = (1.0 + sc_ref[...]) * acc_ref[0:1, :]
        dsc_ref[...] = g_ref[...] * acc_ref[0:1, :]
        dsh_ref[...] = acc_ref[1:2, :]

    row, vec = _row_specs(s, d, ts)
    vshape = jax.ShapeDtypeStruct((1, d), F32)
    return pl.pallas_call(
        body, grid=(s // ts,), in_specs=[row, row, vec, vec, row], out_specs=[row, vec, vec, vec],
        out_shape=[jax.ShapeDtypeStruct((s, d), F32), vshape, vshape, vshape],
        scratch_shapes=[pltpu.VMEM((SUBLANES, d), F32)], compiler_params=_params("arbitrary"), name=name,
    )(dh, x, g, sc, dres)


def _post_fwd(x, y, g, gt, name):
    s, d = x.shape
    ts = _tile(s, ROW_TILES)

    def body(x_ref, y_ref, g_ref, gt_ref, o_ref):
        yv = y_ref[...]
        r = lax.rsqrt(jnp.mean(yv * yv, axis=-1, keepdims=True) + EPS)
        o_ref[...] = x_ref[...] + gt_ref[...] * ((yv * r) * g_ref[...])

    row, vec = _row_specs(s, d, ts)
    return pl.pallas_call(
        body, grid=(s // ts,), in_specs=[row, row, vec, vec], out_specs=row,
        out_shape=jax.ShapeDtypeStruct((s, d), F32), compiler_params=_params("parallel"), name=name,
    )(x, y, g, gt)


def _post_norm_fwd(x, y, g, gt, g2, sc, sh, name):
    s, d = x.shape
    ts = _tile(s, ROW_TILES)

    def body(x_ref, y_ref, g_ref, gt_ref, g2_ref, sc_ref, sh_ref, o_ref, h_ref):
        yv = y_ref[...]
        r = lax.rsqrt(jnp.mean(yv * yv, axis=-1, keepdims=True) + EPS)
        xn = x_ref[...] + gt_ref[...] * ((yv * r) * g_ref[...])
        o_ref[...] = xn
        r2 = lax.rsqrt(jnp.mean(xn * xn, axis=-1, keepdims=True) + EPS)
        h_ref[...] = (((xn * r2) * g2_ref[...]) * (1.0 + sc_ref[...]) + sh_ref[...]).astype(h_ref.dtype)

    row, vec = _row_specs(s, d, ts)
    return pl.pallas_call(
        body, grid=(s // ts,), in_specs=[row, row] + [vec] * 5, out_specs=[row, row],
        out_shape=[jax.ShapeDtypeStruct((s, d), F32), jax.ShapeDtypeStruct((s, d), MXU_DTYPE)],
        compiler_params=_params("parallel"), name=name,
    )(x, y, g, gt, g2, sc, sh)


def _post_bwd(dxn, y, g, gt, name):
    s, d = y.shape
    ts = _tile(s, ROW_TILES)

    def body(dxn_ref, y_ref, g_ref, gt_ref, dy_ref, dg_ref, dgt_ref, acc_ref):
        i = pl.program_id(0)

        @pl.when(i == 0)
        def _():
            acc_ref[...] = jnp.zeros_like(acc_ref)

        yv, dv = y_ref[...], dxn_ref[...]
        r = lax.rsqrt(jnp.mean(yv * yv, axis=-1, keepdims=True) + EPS)
        n = yv * r
        acc_ref[0:1, :] += jnp.sum(dv * n, axis=0, keepdims=True)
        dn = dv * (gt_ref[...] * g_ref[...])
        dy_ref[...] = (r * (dn - n * jnp.mean(dn * n, axis=-1, keepdims=True))).astype(dy_ref.dtype)
        dg_ref[...] = gt_ref[...] * acc_ref[0:1, :]
        dgt_ref[...] = g_ref[...] * acc_ref[0:1, :]

    row, vec = _row_specs(s, d, ts)
    vshape = jax.ShapeDtypeStruct((1, d), F32)
    return pl.pallas_call(
        body, grid=(s // ts,), in_specs=[row, row, vec, vec], out_specs=[row, vec, vec],
        out_shape=[jax.ShapeDtypeStruct((s, d), MXU_DTYPE), vshape, vshape],
        scratch_shapes=[pltpu.VMEM((SUBLANES, d), F32)], compiler_params=_params("arbitrary"), name=name,
    )(dxn, y, g, gt)


def _loss_grad(x, tgt, name):
    s, d = x.shape
    ts = _tile(s, ROW_TILES)

    def body(x_ref, t_ref, col_ref, dx_ref):
        i = pl.program_id(0)

        @pl.when(i == 0)
        def _():
            col_ref[...] = jnp.zeros_like(col_ref)

        e = x_ref[...] - t_ref[...]
        col_ref[...] += jnp.sum(e * e, axis=0, keepdims=True)
        dx_ref[...] = e * (1.0 / d)

    row, vec = _row_specs(s, d, ts)
    return pl.pallas_call(
        body, grid=(s // ts,), in_specs=[row, row], out_specs=[vec, row],
        out_shape=[jax.ShapeDtypeStruct((1, d), F32), jax.ShapeDtypeStruct((s, d), F32)],
        compiler_params=_params("arbitrary"), name=name,
    )(x, tgt)


_GELU_C = 0.7978845608028654
_GELU_A = 0.044715


def _gelu(x):
    t = jnp.tanh(_GELU_C * (x + _GELU_A * x * x * x))
    return 0.5 * x * (1.0 + t), t


def _gelu_grad(x, t):
    return 0.5 * (1.0 + t) + 0.5 * x * (1.0 - t * t) * (_GELU_C * (1.0 + 3.0 * _GELU_A * x * x))


def _sigmoid(x):
    return 1.0 / (1.0 + jnp.exp(-x))


def _log1p_pos(y):
    u = 1.0 + y
    return jnp.where(u == 1.0, y, jnp.log(u) * (y / jnp.where(u == 1.0, 1.0, u - 1.0)))


def _softplus(x):
    return jnp.maximum(x, 0.0) + _log1p_pos(jnp.exp(-jnp.abs(x)))


def _one_minus_exp(z):
    u = jnp.exp(z)
    lg = jnp.log(jnp.where(u > 0.0, u, 1.0))
    safe = (u != 1.0) & (u > 0.0)
    return jnp.where(u == 1.0, -z, jnp.where(u > 0.0, (1.0 - u) * (z / jnp.where(safe, lg, 1.0)), 1.0))


SLAB = 16


def _cat(a, b):
    return jnp.concatenate([a, b], axis=1)


def _pair_specs(shape, nb, index):
    return [pl.BlockSpec(shape, lambda j, t: index(j, t) + (j,)), pl.BlockSpec(shape, lambda j, t: index(j, t) + (j + nb,))]


def _halo_row(ts, time_of):
    return lambda j, t: (jnp.maximum(time_of(t) * (ts // SUBLANES) - 1, 0),)


def _rows_from(groups, k):
    row = lax.broadcasted_iota(jnp.int32, groups[0].shape, 0)
    turned = [pltpu.roll(g, SUBLANES - k, axis=0) for g in groups]
    return [jnp.where(row < SUBLANES - k, lo, hi) for lo, hi in zip(turned[:-1], turned[1:])]


def _ffn_mid_fwd(p, cw, cb, name):
    s, f2 = p.shape
    ts = _tile(s, (1024, 512))
    nb, nt = f2 // (2 * CB), s // ts
    n_grp = SLAB // SUBLANES

    def body(pg_ref, pv_ref, hg_ref, hv_ref, cwg_ref, cwv_ref, cbg_ref, cbv_ref, a_ref, ga_ref, gb_ref):
        t = pl.program_id(1)
        cwv, bias = _cat(cwg_ref[...], cwv_ref[...]), _cat(cbg_ref[...], cbv_ref[...])
        w0, w1, w2 = cwv[0:1], cwv[1:2], cwv[2:3]

        def slab(before, cur, r0):
            pm2, pm1 = _rows_from([before] + cur, SUBLANES - 2), _rows_from([before] + cur, SUBLANES - 1)
            u = jnp.concatenate([bias + w0 * pm2[i] + w1 * pm1[i] + w2 * cur[i] for i in range(n_grp)], axis=0)
            g, v = u[:, :CB], u[:, CB:]
            gel, th = _gelu(g)
            rows = pl.ds(r0, SLAB)
            a_ref[rows, :] = (gel * v).astype(a_ref.dtype)
            ga_ref[rows, :] = gel.astype(ga_ref.dtype)
            gb_ref[rows, :] = (v * _gelu_grad(g, th)).astype(gb_ref.dtype)

        def pieces(rows):
            blk = _cat(pg_ref[rows, :], pv_ref[rows, :])
            return [blk[i * SUBLANES:(i + 1) * SUBLANES] for i in range(blk.shape[0] // SUBLANES)]

        slab(jnp.where(t > 0, _cat(hg_ref[...], hv_ref[...]), 0.0), pieces(pl.ds(0, SLAB)), 0)

        def loop(i, carry):
            r0 = pl.multiple_of(i * SLAB, SLAB)
            got = pieces(pl.ds(pl.multiple_of(r0 - SUBLANES, SUBLANES), SLAB + SUBLANES))
            slab(got[0], got[1:], r0)
            return carry

        lax.fori_loop(1, ts // SLAB, loop, 0, unroll=2)

    fwd = lambda t: t
    out = pl.BlockSpec((ts, CB), lambda j, t: (t, j))
    shape = jax.ShapeDtypeStruct((s, f2 // 2), MXU_DTYPE)
    return pl.pallas_call(
        body, grid=(nb, nt),
        in_specs=(_pair_specs((ts, CB), nb, lambda j, t: (t,)) + _pair_specs((SUBLANES, CB), nb, _halo_row(ts, fwd))
                  + _pair_specs((FFN_CONV, CB), nb, lambda j, t: (0,)) + _pair_specs((1, CB), nb, lambda j, t: (0,))),
        out_specs=[out, out, out], out_shape=[shape, shape, shape],
        compiler_params=_params("parallel", "arbitrary"), name=name,
    )(p, p, p, p, cw, cw, cb, cb)


def _ffn_mid_bwd(da, p, ga, gb, cw, name):
    s, f2 = p.shape
    ts = _tile(s, (1024, 512))
    nb, nt = f2 // (2 * CB), s // ts
    n_slab = ts // SLAB
    n_grp = SLAB // SUBLANES
    per_trip = 2

    def body(da_ref, ga_ref, gb_ref, pg_ref, pv_ref, cwg_ref, cwv_ref, dp_ref, dcw_ref, dcb_ref, next_du, acc):
        tt = pl.program_id(1)
        cwv = _cat(cwg_ref[...], cwv_ref[...])
        w0, w1, w2 = cwv[0:1], cwv[1:2], cwv[2:3]

        @pl.when(tt == 0)
        def _():
            next_du[...] = jnp.zeros_like(next_du)
            acc[...] = jnp.zeros_like(acc)

        def slab(r0, after, sums):
            rows = pl.ds(r0, SLAB)
            dav = da_ref[rows, :]
            du = _cat(dav * gb_ref[rows, :].astype(F32), dav * ga_ref[rows, :].astype(F32))
            p0 = _cat(pg_ref[rows, :], pv_ref[rows, :])
            cur = [du[i * SUBLANES:(i + 1) * SUBLANES] for i in range(n_grp)]
            du1, du2 = _rows_from(cur + [after], 1), _rows_from(cur + [after], 2)
            dpv = jnp.concatenate([w2 * cur[i] + w1 * du1[i] + w0 * du2[i] for i in range(n_grp)], axis=0).astype(dp_ref.dtype)
            dp_ref[0, rows, :] = dpv[:, :CB]
            dp_ref[1, rows, :] = dpv[:, CB:]
            for i in range(n_grp):
                pi = p0[i * SUBLANES:(i + 1) * SUBLANES]
                parts = (cur[i], du2[i] * pi, du1[i] * pi, cur[i] * pi)
                sums = parts if sums is None else tuple(x + y for x, y in zip(sums, parts))
            return cur[0], sums

        def loop(k, after):
            sums = None
            for j in range(per_trip):
                r0 = pl.multiple_of((n_slab - 1 - (k * per_trip + j)) * SLAB, SLAB)
                after, sums = slab(r0, after, sums)
            for q, part in enumerate(sums):
                acc[q] += part
            return after

        next_du[...] = lax.fori_loop(0, n_slab // per_trip, loop, next_du[...])

        @pl.when(tt == nt - 1)
        def _():
            for half in range(2):
                cols = slice(half * CB, (half + 1) * CB)
                dcb_ref[half] = jnp.sum(acc[0][:, cols], axis=0, keepdims=True)
                for k in range(FFN_CONV):
                    dcw_ref[half, k:k + 1, :] = jnp.sum(acc[1 + k][:, cols], axis=0, keepdims=True)

    rev = lambda t: nt - 1 - t
    tile = pl.BlockSpec((ts, CB), lambda j, t: (rev(t), j))
    return pl.pallas_call(
        body, grid=(nb, nt),
        in_specs=([tile, tile, tile] + _pair_specs((ts, CB), nb, lambda j, t: (rev(t),))
                  + _pair_specs((FFN_CONV, CB), nb, lambda j, t: (0,))),
        out_specs=[pl.BlockSpec((2, ts, CB), lambda j, t: (0, rev(t), j)),
                   pl.BlockSpec((2, FFN_CONV, CB), lambda j, t: (0, 0, j)),
                   pl.BlockSpec((2, 1, CB), lambda j, t: (0, 0, j))],
        out_shape=[jax.ShapeDtypeStruct((2, s, f2 // 2), MXU_DTYPE), jax.ShapeDtypeStruct((2, FFN_CONV, f2 // 2), F32),
                   jax.ShapeDtypeStruct((2, 1, f2 // 2), F32)],
        scratch_shapes=[pltpu.VMEM((SUBLANES, 2 * CB), F32), pltpu.VMEM((1 + FFN_CONV, SUBLANES, 2 * CB), F32)],
        compiler_params=_params("parallel", "arbitrary"), name=name,
    )(da, ga, gb, p, p, cw, cw)


def _rg_gates(xc, wa_ref, ba_ref, wx_ref, bx_ref, lam_ref):
    r = _sigmoid(_dot_nn(xc, wa_ref[0]) + ba_ref[...])
    ig = _sigmoid(_dot_nn(xc, wx_ref[0]) + bx_ref[...])
    sp = _softplus(-lam_ref[...])
    log_a = (-RG_C) * r * sp
    a = jnp.exp(log_a)
    mult = jnp.sqrt(_one_minus_exp(2.0 * log_a))
    return r, ig, sp, a, mult


def _rg_conv(scr, cw_ref, cb_ref, ts):
    views = [scr[5 + k:5 + k + ts, :] for k in range(RG_CONV)]
    xc = cb_ref[...]
    for k in range(RG_CONV):
        xc = xc + cw_ref[k:k + 1, :] * views[k]
    return xc, views


def _rg_param_specs():
    vec = pl.BlockSpec((1, CB), lambda g, t: (0, g))
    mat = pl.BlockSpec((1, CB, CB), lambda g, t: (g, 0, 0))
    return [pl.BlockSpec((RG_CONV, CB), lambda g, t: (0, g)), vec, mat, vec, mat, vec, vec]


def _scan_rows(a_scr, x_scr, out_ref, carry, ts, reverse):
    row = lax.broadcasted_iota(jnp.int32, (SUBLANES, a_scr.shape[1]), 0)
    last = SUBLANES - 1

    def group(k, c):
        r0 = pl.multiple_of((ts // SUBLANES - 1 - k if reverse else k) * SUBLANES, SUBLANES)
        rows = pl.ds(r0, SUBLANES)
        a, x = a_scr[rows, :], x_scr[rows, :]
        if reverse:
            first_a = a[0:1]
            x = jnp.where(row == last, x + c, x)
            a = jnp.where(row == last, 1.0, pltpu.roll(a, last, axis=0))
            for sh in (1, 2, 4):
                keep = row < SUBLANES - sh
                x = x + a * jnp.where(keep, pltpu.roll(x, SUBLANES - sh, axis=0), 0.0)
                a = a * jnp.where(keep, pltpu.roll(a, SUBLANES - sh, axis=0), 1.0)
            out_ref[rows, :] = x
            return first_a * x[0:1]
        for sh in (1, 2, 4):
            keep = row >= sh
            x = a * jnp.where(keep, pltpu.roll(x, sh, axis=0), 0.0) + x
            a = a * jnp.where(keep, pltpu.roll(a, sh, axis=0), 1.0)
        h = x + a * c
        out_ref[rows, :] = h
        return h[last:last + 1]

    return lax.fori_loop(0, ts // SUBLANES, group, carry, unroll=4)


def _rg_mid_fwd(pj, cw, cb, wa, ba, wx, bx, lam, name):
    s = pj.shape[0]
    nb = pj.shape[1] // (2 * CB)
    ts = _tile(s, (512,))
    nt = s // ts

    def body(gate_ref, x_ref, halo_ref, cw_ref, cb_ref, wa_ref, ba_ref, wx_ref, bx_ref, lam_ref, y_ref, hs_ref,
             scr, a_scr, u_scr, h_scr):
        t = pl.program_id(1)

        @pl.when(t == 0)
        def _():
            h_scr[...] = jnp.zeros_like(h_scr)

        scr[0:SUBLANES, :] = jnp.where(t > 0, halo_ref[...], 0.0)
        scr[SUBLANES:, :] = x_ref[...]
        xc, _ = _rg_conv(scr, cw_ref, cb_ref, ts)
        _, ig, _, a, mult = _rg_gates(xc, wa_ref, ba_ref, wx_ref, bx_ref, lam_ref)
        a_scr[...] = a
        u_scr[...] = mult * (ig * xc)
        h_scr[0:1, :] = _scan_rows(a_scr, u_scr, hs_ref, h_scr[0:1, :], ts, False)
        y_ref[...] = (_gelu(gate_ref[...])[0] * hs_ref[...]).astype(y_ref.dtype)

    blk = pl.BlockSpec((ts, CB), lambda g, t: (t, g))
    return pl.pallas_call(
        body, grid=(nb, nt),
        in_specs=_pair_specs((ts, CB), nb, lambda g, t: (t,))
        + [pl.BlockSpec((SUBLANES, CB), lambda g, t: _halo_row(ts, lambda u: u)(g, t) + (g + nb,))] + _rg_param_specs(),
        out_specs=[blk, blk],
        out_shape=[jax.ShapeDtypeStruct((s, nb * CB), MXU_DTYPE), jax.ShapeDtypeStruct((s, nb * CB), F32)],
        scratch_shapes=[pltpu.VMEM((ts + SUBLANES, CB), F32), pltpu.VMEM((ts, CB), F32), pltpu.VMEM((ts, CB), F32),
                        pltpu.VMEM((SUBLANES, CB), F32)],
        compiler_params=_params("parallel", "arbitrary"), name=name,
    )(pj, pj, pj, cw, cb, wa, ba, wx, bx, lam)


def _rg_mid_bwd(dy, pj, hs, cw, cb, wa, ba, wx, bx, lam, name):
    s = pj.shape[0]
    nb = pj.shape[1] // (2 * CB)
    ts = _tile(s, (512,))
    nt = s // ts

    def body(dy_ref, gate_ref, x_ref, halo_ref, hs_ref, hsh_ref, cw_ref, cb_ref, wa_ref, ba_ref, wx_ref, bx_ref, lam_ref,
             dpj_ref, dcw_ref, dcb_ref, dwa_ref, dba_ref, dwx_ref, dbx_ref, dlam_ref,
             scr, hscr, a_scr, d_scr, g_scr, dxscr, c_scr):
        tt = pl.program_id(1)
        t = nt - 1 - tt

        @pl.when(tt == 0)
        def _():
            c_scr[...] = jnp.zeros_like(c_scr)
            dxscr[ts:, :] = jnp.zeros((SUBLANES, CB), F32)
            for ref in (dcw_ref, dcb_ref, dwa_ref, dba_ref, dwx_ref, dbx_ref, dlam_ref):
                ref[...] = jnp.zeros_like(ref)

        scr[0:SUBLANES, :] = jnp.where(t > 0, halo_ref[...], 0.0)
        scr[SUBLANES:, :] = x_ref[...]
        hscr[0:SUBLANES, :] = jnp.where(t > 0, hsh_ref[...], 0.0)
        hscr[SUBLANES:, :] = hs_ref[...]
        xc, views = _rg_conv(scr, cw_ref, cb_ref, ts)
        r, ig, sp, a, mult = _rg_gates(xc, wa_ref, ba_ref, wx_ref, bx_ref, lam_ref)
        gate = gate_ref[...]
        gel, th = _gelu(gate)
        dyv = dy_ref[...]
        dpj_ref[0] = (dyv * hs_ref[...] * _gelu_grad(gate, th)).astype(dpj_ref.dtype)
        a_scr[...] = a
        d_scr[...] = dyv * gel
        c_scr[0:1, :] = _scan_rows(a_scr, d_scr, g_scr, c_scr[0:1, :], ts, True)
        du = g_scr[...]
        da = du * hscr[7:7 + ts, :]
        dmult = du * (ig * xc)
        dig = du * (mult * xc)
        dxc = du * (mult * ig)
        dlog_a = da * a - dmult * (a * a / mult)
        dlam_ref[...] += jnp.sum(dlog_a * r, axis=0, keepdims=True) * (RG_C * _sigmoid(-lam_ref[...]))
        dpr = dlog_a * ((-RG_C) * sp) * (r * (1.0 - r))
        dpi = dig * (ig * (1.0 - ig))
        dba_ref[...] += jnp.sum(dpr, axis=0, keepdims=True)
        dbx_ref[...] += jnp.sum(dpi, axis=0, keepdims=True)
        dwa_ref[0] += _dot_tn(xc, dpr)
        dwx_ref[0] += _dot_tn(xc, dpi)
        dxc = dxc + _dot_nt(dpr, wa_ref[0]) + _dot_nt(dpi, wx_ref[0])
        dcb_ref[...] += jnp.sum(dxc, axis=0, keepdims=True)
        for k in range(RG_CONV):
            dcw_ref[k:k + 1, :] += jnp.sum(dxc * views[k], axis=0, keepdims=True)
        dxscr[0:ts, :] = dxc
        dxp = cw_ref[3:4, :] * dxc
        for k in range(RG_CONV - 1):
            dxp = dxp + cw_ref[k:k + 1, :] * dxscr[3 - k:3 - k + ts, :]
        dpj_ref[1] = dxp.astype(dpj_ref.dtype)
        dxscr[ts:, :] = dxscr[0:SUBLANES, :]

    rev = lambda g, t: (nt - 1 - t, g)
    rev_halo = lambda g, t: (jnp.maximum((nt - 1 - t) * (ts // SUBLANES) - 1, 0), g)
    vec = pl.BlockSpec((1, CB), lambda g, t: (0, g))
    mat = pl.BlockSpec((1, CB, CB), lambda g, t: (g, 0, 0))
    d = nb * CB
    vshape = jax.ShapeDtypeStruct((1, d), F32)
    mshape = jax.ShapeDtypeStruct((nb, CB, CB), F32)
    return pl.pallas_call(
        body, grid=(nb, nt),
        in_specs=[pl.BlockSpec((ts, CB), rev)] + _pair_specs((ts, CB), nb, lambda g, t: (nt - 1 - t,))
        + [pl.BlockSpec((SUBLANES, CB), lambda g, t: (rev_halo(g, t)[0], g + nb)),
           pl.BlockSpec((ts, CB), rev), pl.BlockSpec((SUBLANES, CB), rev_halo)] + _rg_param_specs(),
        out_specs=[pl.BlockSpec((2, ts, CB), lambda g, t: (0, nt - 1 - t, g)), pl.BlockSpec((RG_CONV, CB), lambda g, t: (0, g)),
                   vec, mat, vec, mat, vec, vec],
        out_shape=[jax.ShapeDtypeStruct((2, s, d), MXU_DTYPE), jax.ShapeDtypeStruct((RG_CONV, d), F32), vshape, mshape, vshape,
                   mshape, vshape, vshape],
        scratch_shapes=[pltpu.VMEM((ts + SUBLANES, CB), F32), pltpu.VMEM((ts + SUBLANES, CB), F32), pltpu.VMEM((ts, CB), F32),
                        pltpu.VMEM((ts, CB), F32), pltpu.VMEM((ts, CB), F32), pltpu.VMEM((ts + SUBLANES, CB), F32),
                        pltpu.VMEM((SUBLANES, CB), F32)],
        compiler_params=_params("parallel", "arbitrary"), name=name,
    )(dy, pj, pj, pj, hs, hs, cw, cb, wa, ba, wx, bx, lam)


GLA_DK = 128
GLA_DV = 256
GLA_O_K = GLA_HEADS * GLA_DK
GLA_O_V = 2 * GLA_HEADS * GLA_DK
GLA_O_R = GLA_O_V + GLA_HEADS * GLA_DV
GLA_O_Z = GLA_O_R + GLA_HEADS * GLA_DV
GLA_IN = GLA_O_Z + GLA_RANK
GLA_TS = 256


def _dk(h, base=0):
    return slice(base + h * GLA_DK, base + (h + 1) * GLA_DK)


def _dv(h, base=0):
    return slice(base + h * GLA_DV, base + (h + 1) * GLA_DV)


def _split3(x):
    hi = x.astype(BF16)
    r1 = x - hi.astype(F32)
    mid = r1.astype(BF16)
    lo = (r1 - mid.astype(F32)).astype(BF16)
    return hi, mid, lo


def _chunk_cumsum(x, reverse):
    n = x.shape[0]
    i = lax.broadcasted_iota(jnp.int32, (n, n), 0)
    j = lax.broadcasted_iota(jnp.int32, (n, n), 1)
    same = (i // GLA_CHUNK) == (j // GLA_CHUNK)
    tri = jnp.where(same & ((j >= i) if reverse else (j <= i)), 1.0, 0.0).astype(BF16)
    out = jnp.zeros(x.shape, F32)
    for piece in _split3(x):
        out = out + lax.dot_general(tri, piece, (((1,), (0,)), ((), ())), preferred_element_type=F32)
    return out


def _gla_head(pj_ref, h):
    return (pj_ref[:, _dk(h)] * (GLA_DK ** -0.5), pj_ref[:, _dk(h, GLA_O_K)], pj_ref[:, _dv(h, GLA_O_V)],
            pj_ref[:, _dv(h, GLA_O_R)])


def _gla_decays(gc):
    gref = gc[GLA_CHUNK // 2:GLA_CHUNK // 2 + 1, :]
    glast = gc[GLA_CHUNK - 1:GLA_CHUNK, :]
    return jnp.exp(gc), jnp.exp(gc - gref), jnp.exp(gref - gc), jnp.exp(glast - gc), jnp.exp(glast)


def _causal_mask():
    i = lax.broadcasted_iota(jnp.int32, (GLA_CHUNK, GLA_CHUNK), 0)
    j = lax.broadcasted_iota(jnp.int32, (GLA_CHUNK, GLA_CHUNK), 1)
    return j <= i


def _log_sigmoid(x):
    return jnp.minimum(x, 0.0) - _log1p_pos(jnp.exp(-jnp.abs(x)))


def _gla_mid_fwd(pj, wal, bal, ng, name):
    s, nh = pj.shape[0], GLA_HEADS
    ts = _tile(s, (GLA_TS,))
    nt, nc = s // ts, ts // GLA_CHUNK

    def body(pj_ref, wal_ref, bal_ref, ng_ref, act_ref, o_ref, st_ref, s_scr):
        t = pl.program_id(0)

        @pl.when(t == 0)
        def _():
            s_scr[...] = jnp.zeros_like(s_scr)

        heads = []
        z = pj_ref[:, GLA_O_Z:]
        for h in range(nh):
            q, k, v, r = _gla_head(pj_ref, h)
            g = _log_sigmoid(_dot_nn(z, wal_ref[:, _dk(h)]) + bal_ref[:, _dk(h)]) * (1.0 / GLA_TAU)
            heads.append((q, k, v, r, _chunk_cumsum(g, False)))
        mask = _causal_mask()
        for c in range(nc):
            sl = slice(c * GLA_CHUNK, (c + 1) * GLA_CHUNK)
            for h, (q, k, v, r, gcum) in enumerate(heads):
                eg, eq, ek, ekd, egl = _gla_decays(gcum[sl])
                st = s_scr[h]
                st_ref[c, h] = st
                attn = jnp.where(mask, _dot_nt(q[sl] * eq, k[sl] * ek), 0.0)
                o_ref[sl, h * GLA_DV:(h + 1) * GLA_DV] = _dot_nt(q[sl] * eg, st) + _dot_nn(attn, v[sl])
                s_scr[h] = st * egl + _dot_tn(v[sl], k[sl] * ekd)
        for h, (q, k, v, r, gcum) in enumerate(heads):
            cols = slice(h * GLA_DV, (h + 1) * GLA_DV)
            o = o_ref[:, cols]
            on = o * lax.rsqrt(jnp.mean(o * o, axis=-1, keepdims=True) + EPS)
            act_ref[:, cols] = ((on * ng_ref[...]) * (r * _sigmoid(r))).astype(act_ref.dtype)

    blk = pl.BlockSpec((ts, nh * GLA_DV), lambda t: (t, 0))
    whole = lambda shape: pl.BlockSpec(shape, lambda t: (0,) * len(shape))
    return pl.pallas_call(
        body, grid=(nt,),
        in_specs=[pl.BlockSpec((ts, GLA_IN), lambda t: (t, 0)), whole((GLA_RANK, nh * GLA_DK)), whole((1, nh * GLA_DK)),
                  whole((1, GLA_DV))],
        out_specs=[blk, blk, pl.BlockSpec((nc, nh, GLA_DV, GLA_DK), lambda t: (t, 0, 0, 0))],
        out_shape=[jax.ShapeDtypeStruct((s, nh * GLA_DV), MXU_DTYPE), jax.ShapeDtypeStruct((s, nh * GLA_DV), F32),
                   jax.ShapeDtypeStruct((s // GLA_CHUNK, nh, GLA_DV, GLA_DK), F32)],
        scratch_shapes=[pltpu.VMEM((nh, GLA_DV, GLA_DK), F32)],
        compiler_params=_params("arbitrary"), name=name,
    )(pj, wal, bal, ng)


def _gla_mid_bwd(dact, pj, o, st, wal, bal, ng, name):
    s, nh = pj.shape[0], GLA_HEADS
    ts = _tile(s, (GLA_TS,))
    nt, nc = s // ts, ts // GLA_CHUNK

    def body(dact_ref, pj_ref, o_ref, st_ref, wal_ref, bal_ref, ng_ref, dpj_ref, dwal_ref, dbal_ref, dng_ref,
             ds_scr, dg_scr):
        tt = pl.program_id(0)

        @pl.when(tt == 0)
        def _():
            ds_scr[...] = jnp.zeros_like(ds_scr)
            dwal_ref[...] = jnp.zeros_like(dwal_ref)
            dbal_ref[...] = jnp.zeros_like(dbal_ref)
            dng_ref[...] = jnp.zeros_like(dng_ref)

        heads = []
        z = pj_ref[:, GLA_O_Z:]
        for h in range(nh):
            q, k, v, r = _gla_head(pj_ref, h)
            logit = _dot_nn(z, wal_ref[:, _dk(h)]) + bal_ref[:, _dk(h)]
            gcum = _chunk_cumsum(_log_sigmoid(logit) * (1.0 / GLA_TAU), False)
            ov = o_ref[:, h * GLA_DV:(h + 1) * GLA_DV]
            ro = lax.rsqrt(jnp.mean(ov * ov, axis=-1, keepdims=True) + EPS)
            on = ov * ro
            sg = _sigmoid(r)
            sil = r * sg
            dav = dact_ref[:, h * GLA_DV:(h + 1) * GLA_DV]
            dpj_ref[:, _dv(h, GLA_O_R)] = (dav * (on * ng_ref[...]) * (sg + sil * (1.0 - sg))).astype(dpj_ref.dtype)
            t1 = dav * sil
            dng_ref[...] += jnp.sum(t1 * on, axis=0, keepdims=True)
            dn = t1 * ng_ref[...]
            do = ro * (dn - on * jnp.mean(dn * on, axis=-1, keepdims=True))
            heads.append((q, k, v, logit, gcum, do))
        mask = _causal_mask()
        scale = GLA_DK ** -0.5
        last_row = lax.broadcasted_iota(jnp.int32, (GLA_CHUNK, GLA_DK), 0) == GLA_CHUNK - 1
        for c in reversed(range(nc)):
            sl = slice(c * GLA_CHUNK, (c + 1) * GLA_CHUNK)
            for h, (q, k, v, logit, gcum, do) in enumerate(heads):
                eg, eq, ek, ekd, egl = _gla_decays(gcum[sl])
                qc, kc, vc, doc = q[sl], k[sl], v[sl], do[sl]
                qg, qt, kt, kd = qc * eg, qc * eq, kc * ek, kc * ekd
                sp = st_ref[c, h]
                ds = ds_scr[h]
                attn = jnp.where(mask, _dot_nt(qt, kt), 0.0)
                dattn = jnp.where(mask, _dot_nt(doc, vc), 0.0)
                dqg = _dot_nn(doc, sp)
                dqt = _dot_nn(dattn, kt)
                dkt = _dot_tn(dattn, qt)
                dkd = _dot_nn(vc, ds)
                dpj_ref[sl, _dv(h, GLA_O_V)] = (_dot_tn(attn, doc) + _dot_nt(kd, ds)).astype(dpj_ref.dtype)
                dpj_ref[sl, _dk(h)] = (scale * (dqg * eg + dqt * eq)).astype(dpj_ref.dtype)
                dpj_ref[sl, _dk(h, GLA_O_K)] = (dkt * ek + dkd * ekd).astype(dpj_ref.dtype)
                kdd = dkd * kd
                dgl = jnp.sum(kdd, axis=0, keepdims=True) + jnp.sum(ds * sp, axis=0, keepdims=True) * egl
                dg_scr[h, sl, :] = dqg * qg + dqt * qt - dkt * kt - kdd + jnp.where(last_row, dgl, 0.0)
                ds_scr[h] = ds * egl + _dot_tn(doc, qg)
        dz = jnp.zeros((ts, GLA_RANK), F32)
        for h, (q, k, v, logit, gcum, do) in enumerate(heads):
            dlogit = _chunk_cumsum(dg_scr[h], True) * (1.0 / GLA_TAU) * _sigmoid(-logit)
            dz = dz + _dot_nt(dlogit, wal_ref[:, _dk(h)])
            dwal_ref[:, _dk(h)] += _dot_tn(z, dlogit)
            dbal_ref[:, _dk(h)] += jnp.sum(dlogit, axis=0, keepdims=True)
        dpj_ref[:, GLA_O_Z:] = dz.astype(dpj_ref.dtype)

    rev = lambda t: (nt - 1 - t, 0)
    whole = lambda shape: pl.BlockSpec(shape, lambda t: (0,) * len(shape))
    wide = pl.BlockSpec((ts, nh * GLA_DV), rev)
    return pl.pallas_call(
        body, grid=(nt,),
        in_specs=[wide, pl.BlockSpec((ts, GLA_IN), rev), wide,
                  pl.BlockSpec((nc, nh, GLA_DV, GLA_DK), lambda t: (nt - 1 - t, 0, 0, 0)),
                  whole((GLA_RANK, nh * GLA_DK)), whole((1, nh * GLA_DK)), whole((1, GLA_DV))],
        out_specs=[pl.BlockSpec((ts, GLA_IN), rev), whole((GLA_RANK, nh * GLA_DK)), whole((1, nh * GLA_DK)), whole((1, GLA_DV))],
        out_shape=[jax.ShapeDtypeStruct((s, GLA_IN), MXU_DTYPE), jax.ShapeDtypeStruct((GLA_RANK, nh * GLA_DK), F32),
                   jax.ShapeDtypeStruct((1, nh * GLA_DK), F32), jax.ShapeDtypeStruct((1, GLA_DV), F32)],
        scratch_shapes=[pltpu.VMEM((nh, GLA_DV, GLA_DK), F32), pltpu.VMEM((nh, ts, GLA_DK), F32)],
        compiler_params=_params("arbitrary"), name=name,
    )(dact, pj, o, st, wal, bal, ng)


def _adamw(w, gs, m, v, name, after=None):
    layers, rows, cols = w.shape
    gs = list(gs) if isinstance(gs, (list, tuple)) else gs
    n_g = len(gs) if isinstance(gs, list) else 1
    if rows % SUBLANES == 0:
        tr, tc = _tile(rows, (256, 128, 64, 32, 16, 8)), cols
    else:
        tr, tc = rows, _tile(cols, (256, 128))
    c1 = 1.0 / (1.0 - ADAM_B1 ** ADAM_STEP)
    c2 = 1.0 / (1.0 - ADAM_B2 ** ADAM_STEP)

    def body(*refs):
        g_refs, (w_ref, m_ref, v_ref) = refs[:n_g], refs[n_g:n_g + 3]
        go_ref, d_ref, mo_ref, vo_ref = refs[-4:]
        gv = g_refs[0][...]
        for l in range(1, n_g):
            gv = jnp.where(pl.program_id(0) == l, g_refs[l][...], gv)
        m2 = ADAM_B1 * m_ref[...] + (1.0 - ADAM_B1) * gv
        v2 = ADAM_B2 * v_ref[...] + (1.0 - ADAM_B2) * (gv * gv)
        d_ref[...] = (-ADAM_LR) * ((m2 * c1) / (jnp.sqrt(v2 * c2) + ADAM_EPS) + ADAM_WD * w_ref[...])
        go_ref[...] = gv
        mo_ref[...] = m2
        vo_ref[...] = v2

    spec = pl.BlockSpec((None, tr, tc), lambda l, i, j: (l, i, j))
    g_specs = [pl.BlockSpec((tr, tc), lambda l, i, j: (i, j))] * n_g if isinstance(gs, list) else [spec]
    extra = [] if after is None else [(after, _ANY)]
    shape = jax.ShapeDtypeStruct((layers, rows, cols), F32)
    return pl.pallas_call(
        body, grid=(layers, rows // tr, cols // tc), in_specs=g_specs + [spec] * 3 + [sp for _, sp in extra],
        out_specs=[spec] * 4, out_shape=[shape] * 4, compiler_params=_params("parallel", "parallel", "parallel"), name=name,
    )(*(gs if isinstance(gs, list) else [gs]), w, m, v, *[a for a, _ in extra])


def _col_slots(w):
    r, c = w.shape
    return jnp.moveaxis(w.reshape(r, N_CHIP, c // N_CHIP), 1, 0)


def _from_col_slots(w):
    n, r, c = w.shape
    return jnp.moveaxis(w, 0, 1).reshape(r, n * c)


def _block_rows_to_slots(w):
    g, r4, cc = w.shape
    return jnp.swapaxes(w.reshape(g, N_CHIP, r4 // N_CHIP, cc), 0, 1).reshape(N_CHIP, g * (r4 // N_CHIP), cc)


def _slots_to_block_rows(w, g):
    n, gr, cc = w.shape
    return jnp.swapaxes(w.reshape(n, g, gr // g, cc), 0, 1).reshape(g, n * (gr // g), cc)


def _local_step(x, tgt, mod, w, fetch=None, done=None, later=None):
    depth = mod.shape[0]
    row = lambda v: v.reshape(1, -1)
    w = dict(w)
    w["ffn_w_up"], w["ffn_w_down"] = dict(enumerate(w["ffn_w_up"])), dict(enumerate(w["ffn_w_down"]))

    def arrive(stage, after):
        if fetch is not None:
            for k, v in fetch(stage, after).items():
                if isinstance(v, dict):
                    w[k].update(v)
                else:
                    w[k] = v

    saved = []
    for i in range(depth):
        if i == 1:
            arrive("l1", x)
        sh_m, sc_m, gt_m, sh_f, sc_f, gt_f = (mod[i, j:j + 1] for j in range(6))
        g0, g1, g2, g3 = (w["norm_g"][i, j:j + 1] for j in range(4))
        tag = f"_l{i}"
        if i == 0:
            h = _norm_mod_fwd(x, g0, sc_m, sh_m, "norm_mix" + tag)
        if i % 2 == 0:
            pj = _mm(h, w["rg_w_in"], w_slots=N_CHIP, name="rg_in" + tag)
            act, aux = _rg_mid_fwd(pj, w["rg_conv_w"], row(w["rg_conv_b"]), w["rg_wa"], row(w["rg_ba"]), w["rg_wx"],
                                   row(w["rg_bx"]), row(w["rg_lambda"]), "rg_mid" + tag)
            y = _mm(act, w["rg_w_out"], name="rg_out" + tag)
        else:
            pj = _mm(h, w["gla_w_in"], tm_max=512, name="gla_in" + tag)
            act, *aux = _gla_mid_fwd(pj, w["gla_w_alpha"], row(w["gla_b_alpha"]), row(w["gla_norm_g"]), "gla_mid" + tag)
            y = _mm(act, w["gla_w_out"], name="gla_out" + tag)
        x1, h2 = _post_norm_fwd(x, y, g1, gt_m, g2, sc_f, sh_f, "post_mix" + tag)
        if i == 0:
            arrive("ffn0", x1)
        p = _mm(h2, w["ffn_w_up"][i], w_slots=N_CHIP, name="ffn_up" + tag)
        a, ga, gb = _ffn_mid_fwd(p, w["ffn_conv_w"][i], w["ffn_conv_b"][i:i + 1], "ffn_mid" + tag)
        y2 = _mm(a, w["ffn_w_down"][i], name="ffn_down" + tag)
        saved_h = h
        if i + 1 < depth:
            nxt = [mod[i + 1, j:j + 1] for j in range(2)] + [w["norm_g"][i + 1, 0:1]]
            x2, h = _post_norm_fwd(x1, y2, g3, gt_f, nxt[2], nxt[1], nxt[0], "post_ffn" + tag)
        else:
            x2 = _post_fwd(x1, y2, g3, gt_f, "post_ffn" + tag)
        saved.append((x, saved_h, pj, act, aux, y, x1, h2, p, (a, ga, gb), y2))
        x = x2

    cols, dx = _loss_grad(x, tgt, "loss")

    stacked = ("norm_g", "ffn_conv_w", "ffn_conv_b", "mod")
    gr = {k: [None] * depth for k in stacked + ("ffn_w_up", "ffn_w_down")}
    told = lambda stage: done(stage, gr) if done is not None else 0.0
    told_later = lambda stage, after: later(stage, after) if later is not None else 0.0
    for i in reversed(range(depth)):
        x0, h, pj, act, aux, y, x1, h2, p, (a, ga, gb), y2 = saved[i]
        sh_m, sc_m, gt_m, sh_f, sc_f, gt_f = (mod[i, j:j + 1] for j in range(6))
        g0, g1, g2, g3 = (w["norm_g"][i, j:j + 1] for j in range(4))
        tag = f"_l{i}"
        dy2, d_g3, d_gt_f = _post_bwd(dx, y2, g3, gt_f, "post_ffn_b" + tag)
        da = _mm(dy2, w["ffn_w_down"][i], tb=True, name="ffn_down_dx" + tag)
        gr["ffn_w_down"][i] = _mm(a, dy2, ta=True, name="ffn_down_dw" + tag)
        conv_w = w["ffn_conv_w"][i] + (told_later("l1", da) if i == 0 else 0.0)
        dp, dcw, dcb = _ffn_mid_bwd(da, p, ga, gb, conv_w, "ffn_mid_b" + tag)
        gr["ffn_conv_w"][i], gr["ffn_conv_b"][i] = _cat(dcw[0], dcw[1]), _cat(dcb[0], dcb[1])[0]
        dh2 = _mm(dp, w["ffn_w_up"][i], tb=True, a_parts=2, w_slots=N_CHIP, name="ffn_up_dx" + tag)
        gr["ffn_w_up"][i] = _mm(h2, dp, ta=True, b_parts=2, out_slots=N_CHIP, name="ffn_up_dw" + tag)
        dx1, d_g2, d_sc_f, d_sh_f = _norm_mod_bwd(dh2, x1, g2, sc_f, dx, "norm_ffn_b" + tag)
        if i == 0:
            gt_m = gt_m + told("ffn0")
        dy, d_g1, d_gt_m = _post_bwd(dx1, y, g1, gt_m, "post_mix_b" + tag)
        if i % 2 == 0:
            dact = _mm(dy, w["rg_w_out"], tb=True, name="rg_out_dx" + tag)
            gr["rg_w_out"] = _mm(act, dy, ta=True, name="rg_out_dw" + tag)
            lam = row(w["rg_lambda"]) + told_later("ffn0", gr["rg_w_out"])
            dpj, gr["rg_conv_w"], d_cb, gr["rg_wa"], d_ba, gr["rg_wx"], d_bx, d_lam = _rg_mid_bwd(
                dact, pj, aux, w["rg_conv_w"], row(w["rg_conv_b"]), w["rg_wa"], row(w["rg_ba"]), w["rg_wx"],
                row(w["rg_bx"]), lam, "rg_mid_b" + tag)
            gr["rg_conv_b"], gr["rg_ba"], gr["rg_bx"], gr["rg_lambda"] = d_cb[0], d_ba[0], d_bx[0], d_lam[0]
            dh = _mm(dpj, w["rg_w_in"], tb=True, a_parts=2, w_slots=N_CHIP, name="rg_in_dx" + tag)
            gr["rg_w_in"] = _mm(h, dpj, ta=True, b_parts=2, out_slots=N_CHIP, name="rg_in_dw" + tag)
        else:
            dact = _mm(dy, w["gla_w_out"], tb=True, name="gla_out_dx" + tag)
            gr["gla_w_out"] = _mm(act, dy, ta=True, name="gla_out_dw" + tag)
            dpj, gr["gla_w_alpha"], d_bal, d_ng = _gla_mid_bwd(dact, pj, aux[0], aux[1], w["gla_w_alpha"], row(w["gla_b_alpha"]),
                                                               row(w["gla_norm_g"]), "gla_mid_b" + tag)
            gr["gla_b_alpha"], gr["gla_norm_g"] = d_bal[0], d_ng[0]
            dh = _mm(dpj, w["gla_w_in"], tb=True, name="gla_in_dx" + tag)
            gr["gla_w_in"] = _mm(h, dpj, ta=True, tm_max=512, name="gla_in_dw" + tag)
            mod = mod.at[0].add(told("l1"))
        dx, d_g0, d_sc_m, d_sh_m = _norm_mod_bwd(dh, x0, g0, sc_m, dx1, "norm_mix_b" + tag)
        gr["norm_g"][i] = jnp.concatenate([d_g0, d_g1, d_g2, d_g3], axis=0)
        gr["mod"][i] = jnp.concatenate([d_sh_m, d_sc_m, d_gt_m, d_sh_f, d_sc_f, d_gt_f], axis=0)
    for k in stacked:
        gr[k] = jnp.stack(gr[k])
    return cols, dx, gr


ADA_ROWS = 16


def _ada_fwd(c16, ada_w, ada_b, name):
    depth, d, n = ada_w.shape
    tn = _tile(n, (512, 256, 128))

    def body(c_ref, w_ref, b_ref, o_ref):
        cv = c_ref[...]
        o_ref[0] = _dot_nn(cv * _sigmoid(cv), w_ref[0]) + b_ref[0]

    return pl.pallas_call(
        body, grid=(depth, n // tn),
        in_specs=[pl.BlockSpec((ADA_ROWS, d), lambda l, j: (0, 0)), pl.BlockSpec((1, d, tn), lambda l, j: (l, 0, j)),
                  pl.BlockSpec((1, 1, tn), lambda l, j: (l, 0, j))],
        out_specs=pl.BlockSpec((1, ADA_ROWS, tn), lambda l, j: (l, 0, j)),
        out_shape=jax.ShapeDtypeStruct((depth, ADA_ROWS, n), F32),
        compiler_params=_params("parallel", "parallel"), name=name,
    )(c16, ada_w, ada_b)


def _ada_bwd(c16, dmod16, name):
    depth, _, n = dmod16.shape
    d = c16.shape[1]
    tn = _tile(n, (512, 256, 128))

    def body(c_ref, dm_ref, o_ref):
        cv = c_ref[...]
        o_ref[0] = _dot_tn(cv * _sigmoid(cv), dm_ref[0])

    return pl.pallas_call(
        body, grid=(depth, n // tn),
        in_specs=[pl.BlockSpec((ADA_ROWS, d), lambda l, j: (0, 0)), pl.BlockSpec((1, ADA_ROWS, tn), lambda l, j: (l, 0, j))],
        out_specs=pl.BlockSpec((1, d, tn), lambda l, j: (l, 0, j)),
        out_shape=jax.ShapeDtypeStruct((depth, d, n), F32),
        compiler_params=_params("parallel", "parallel"), name=name,
    )(c16, dmod16)


PACK_COLS = 1024
_ANY = pl.BlockSpec(memory_space=pl.ANY)
_VMEM = pl.BlockSpec(memory_space=pltpu.VMEM)


def _place():
    return lax.axis_index("x"), lax.axis_index("y"), lax.axis_index("c")


def _other_chips(x, y):
    return [(1 - x, y), (x, 1 - y), (1 - x, 1 - y)]


def _rcopy(src, dst, send_sems, recv_sems, k, peer):
    return pltpu.make_async_remote_copy(src_ref=src, dst_ref=dst, send_sem=send_sems.at[k], recv_sem=recv_sems.at[k],
                                        device_id=peer, device_id_type=MESH)


def _all_gather_8(v, name):
    r, cc = v.shape

    def body(v_ref, out_ref, send_sems, recv_sems, local_sem):
        x, y, c = _place()
        me = 4 * x + 2 * y + c
        mine = pltpu.make_async_copy(v_ref, out_ref.at[me], local_sem)
        mine.start()
        peers = []
        for k in range(1, N_DEV):
            px = 1 - x if k & 4 else x
            py = 1 - y if k & 2 else y
            pc = 1 - c if k & 1 else c
            peers.append((px, py, pc))
        sends = [_rcopy(v_ref, out_ref.at[me], send_sems, recv_sems, k, p) for k, p in enumerate(peers)]
        for cp in sends:
            cp.start()
        for k, (px, py, pc) in enumerate(peers):
            _rcopy(v_ref, out_ref.at[4 * px + 2 * py + pc], send_sems, recv_sems, k, (px, py, pc)).wait_recv()
        for cp in sends:
            cp.wait_send()
        mine.wait()

    return pl.pallas_call(
        body, in_specs=[_VMEM], out_specs=_VMEM, out_shape=jax.ShapeDtypeStruct((N_DEV, r, cc), v.dtype),
        scratch_shapes=[pltpu.SemaphoreType.DMA((N_DEV - 1,)), pltpu.SemaphoreType.DMA((N_DEV - 1,)), pltpu.SemaphoreType.DMA],
        compiler_params=pltpu.CompilerParams(vmem_limit_bytes=VMEM_LIMIT), name=name,
    )(v)


def _gather_chips(shards, name):
    n = len(shards)
    per = 2 * (N_CHIP - 1)

    def body(*refs):
        ins, outs, (send_sems, recv_sems) = refs[:n], refs[n:2 * n], refs[2 * n:]
        x, y, c = _place()
        chip = 2 * x + y
        chips = _other_chips(x, y)
        rows = [(pl.ds(c * (r.shape[0] // 2), r.shape[0] // 2), pl.ds((1 - c) * (r.shape[0] // 2), r.shape[0] // 2)) for r in ins]
        first = [_rcopy(ins[i].at[rows[i][0]], outs[i].at[chip, rows[i][0]], send_sems, recv_sems, per * i + j, (px, py, c))
                 for i in range(n) for j, (px, py) in enumerate(chips)]
        for cp in first:
            cp.start()
        passed = []
        for i in range(n):
            for j, (px, py) in enumerate(chips):
                landed = outs[i].at[2 * px + py, rows[i][0]]
                _rcopy(ins[i].at[rows[i][0]], landed, send_sems, recv_sems, per * i + j, (px, py, c)).wait_recv()
                fw = _rcopy(landed, landed, send_sems, recv_sems, per * i + N_CHIP - 1 + j, (x, y, 1 - c))
                fw.start()
                passed.append(fw)
        for i in range(n):
            for j, (px, py) in enumerate(chips):
                landed = outs[i].at[2 * px + py, rows[i][1]]
                _rcopy(landed, landed, send_sems, recv_sems, per * i + N_CHIP - 1 + j, (x, y, 1 - c)).wait_recv()
        for cp in first + passed:
            cp.wait_send()

    return pl.pallas_call(
        body, in_specs=[_ANY] * n, out_specs=[_ANY] * n,
        out_shape=[jax.ShapeDtypeStruct((N_CHIP,) + sh.shape, sh.dtype) for sh in shards],
        scratch_shapes=[pltpu.SemaphoreType.DMA((per * n,)), pltpu.SemaphoreType.DMA((per * n,))], name=name,
    )(*shards)


def _pair_exchange(gs, name):
    n = len(gs)

    def body(*refs):
        ins, outs, (send_sems, recv_sems) = refs[:n], refs[n:2 * n], refs[2 * n:]
        x, y, c = _place()
        copies = []
        for i in range(n):
            half = ins[i].shape[1] // 2
            copies.append(_rcopy(ins[i].at[:, pl.ds((1 - c) * half, half)], outs[i], send_sems, recv_sems, i, (x, y, 1 - c)))
        for cp in copies:
            cp.start()
        for cp in copies:
            cp.wait()

    return pl.pallas_call(
        body, in_specs=[_ANY] * n, out_specs=[_ANY] * n,
        out_shape=[jax.ShapeDtypeStruct((g.shape[0], g.shape[1] // 2, g.shape[2]), g.dtype) for g in gs],
        scratch_shapes=[pltpu.SemaphoreType.DMA((n,)), pltpu.SemaphoreType.DMA((n,))], name=name,
    )(*gs)


_ROW_TILES = (640, 512, 352, 256, 128, 64, 32, 16)


def _pair_sum(g, other, c_idx, name):
    n, half, cc = other.shape
    tr = _tile(half, _ROW_TILES)

    def body(c_ref, g_ref, o_ref, out_ref):
        out_ref[...] = (g_ref[...] + o_ref[...]).astype(out_ref.dtype)

    return pl.pallas_call(
        body,
        grid_spec=pltpu.PrefetchScalarGridSpec(
            num_scalar_prefetch=1, grid=(n, half // tr),
            in_specs=[pl.BlockSpec((None, None, tr, cc), lambda k, i, c_ref: (k, c_ref[0], i, 0)),
                      pl.BlockSpec((None, tr, cc), lambda k, i, c_ref: (k, i, 0))],
            out_specs=pl.BlockSpec((None, tr, cc), lambda k, i, c_ref: (k, i, 0))),
        out_shape=jax.ShapeDtypeStruct((n, half, cc), BF16),
        compiler_params=_params("parallel", "parallel"), name=name,
    )(c_idx, g.reshape(n, 2, half, cc), other)


def _chip_exchange(ps, name):
    n = len(ps)
    per = N_CHIP - 1

    def body(*refs):
        ins, outs, (send_sems, recv_sems) = refs[:n], refs[n:2 * n], refs[2 * n:]
        x, y, c = _place()
        chip = 2 * x + y
        chips = _other_chips(x, y)
        sends = [_rcopy(ins[i].at[2 * px + py], outs[i].at[chip], send_sems, recv_sems, per * i + j, (px, py, c))
                 for i in range(n) for j, (px, py) in enumerate(chips)]
        for cp in sends:
            cp.start()
        for i in range(n):
            for j, (px, py) in enumerate(chips):
                _rcopy(ins[i].at[chip], outs[i].at[2 * px + py], send_sems, recv_sems, per * i + j, (px, py, c)).wait_recv()
        for cp in sends:
            cp.wait_send()

    return pl.pallas_call(
        body, in_specs=[_ANY] * n, out_specs=[_ANY] * n, out_shape=[jax.ShapeDtypeStruct(p.shape, p.dtype) for p in ps],
        scratch_shapes=[pltpu.SemaphoreType.DMA((per * n,)), pltpu.SemaphoreType.DMA((per * n,))], name=name,
    )(*ps)


_HBM = pl.BlockSpec(memory_space=pltpu.HBM)
_SEM = pl.BlockSpec(memory_space=pltpu.SEMAPHORE)
_DATAFLOW = pltpu.SideEffectType.DATAFLOW_SIDE_EFFECTING


def _split_copies(srcs, lands, send_sems, recv_sems, mode, arriving):
    x, y, c = _place()
    chip = 2 * x + y
    out = []
    for i, (src, land) in enumerate(zip(srcs, lands)):
        if mode == "pair":
            half = src.shape[1] // 2
            out.append(_rcopy(src.at[:, pl.ds((1 - c) * half, half)], land, send_sems, recv_sems, i, (x, y, 1 - c)))
            continue
        for j, (px, py) in enumerate(_other_chips(x, y)):
            there = 2 * px + py
            part = src.at[there] if mode == "slots" else src
            out.append(_rcopy(part, land.at[there if arriving else chip], send_sems, recv_sems, (N_CHIP - 1) * i + j, (px, py, c)))
    return out


def _land_shape(src, mode):
    if mode == "pair":
        return (src.shape[0], src.shape[1] // 2, src.shape[2])
    return (N_CHIP,) + (src.shape[1:] if mode == "slots" else src.shape)


def _send_start(srcs, mode, name):
    n = len(srcs)
    n_sem = n if mode == "pair" else (N_CHIP - 1) * n
    lands = [lax.empty(_land_shape(s, mode), s.dtype) for s in srcs]

    def body(*refs):
        ins, zones, (send_sems, recv_sems) = refs[:n], refs[n:2 * n], refs[2 * n:2 * n + 2]
        for cp in _split_copies(ins, zones, send_sems, recv_sems, mode, False):
            cp.start()
        refs[-1][...] = jnp.zeros_like(refs[-1])

    hbm = lambda a: pltpu.HBM(a.shape, a.dtype)
    outs = pl.pallas_call(
        body, name=name, in_specs=[_HBM] * (2 * n),
        out_shape=(pltpu.SemaphoreType.DMA((n_sem,)), pltpu.SemaphoreType.DMA((n_sem,)), *[hbm(a) for a in srcs],
                   *[hbm(a) for a in lands], jax.ShapeDtypeStruct((SUBLANES, LANES), F32)),
        out_specs=(_SEM, _SEM, *[_HBM] * (2 * n), _VMEM), input_output_aliases={i: 2 + i for i in range(2 * n)},
        compiler_params=pltpu.CompilerParams(has_side_effects=_DATAFLOW),
    )(*[pltpu.with_memory_space_constraint(a, pltpu.HBM) for a in list(srcs) + lands])
    return (outs[0], outs[1], list(outs[2:2 + n]), list(outs[2 + n:2 + 2 * n])), outs[-1]


def _send_wait(state, after, mode, name):
    send_sems, recv_sems, srcs, lands = state
    n = len(srcs)

    def body(*refs):
        ins, zones, (send_s, recv_s) = refs[:n], refs[n:2 * n], refs[2 * n:2 * n + 2]
        for cp in _split_copies(ins, zones, send_s, recv_s, mode, True):
            cp.wait_send()
            cp.wait_recv()

    hbm = lambda a: pltpu.HBM(a.shape, a.dtype)
    outs = pl.pallas_call(
        body, name=name, in_specs=[_HBM] * (2 * n) + [_SEM, _SEM, _ANY],
        out_shape=tuple(hbm(a) for a in srcs + lands), out_specs=tuple([_HBM] * (2 * n)),
        input_output_aliases={i: i for i in range(2 * n)},
        compiler_params=pltpu.CompilerParams(has_side_effects=_DATAFLOW),
    )(*srcs, *lands, send_sems, recv_sems, after)
    return list(outs[:n]), list(outs[n:])


def _sum_lead(v, name):
    n, r, cc = v.shape
    tr = _tile(r, _ROW_TILES + (8,))

    def body(v_ref, o_ref):
        acc = v_ref[0].astype(F32)
        for k in range(1, n):
            acc = acc + v_ref[k].astype(F32)
        o_ref[...] = acc

    return pl.pallas_call(
        body, grid=(r // tr,), in_specs=[pl.BlockSpec((n, tr, cc), lambda i: (0, i, 0))],
        out_specs=pl.BlockSpec((tr, cc), lambda i: (i, 0)), out_shape=jax.ShapeDtypeStruct((r, cc), F32),
        compiler_params=_params("parallel"), name=name,
    )(v)


def _chip_sum(arrived, mine, chip_idx, name):
    n, r, cc = arrived.shape
    tr = _tile(r, _ROW_TILES)

    def body(chip_ref, a_ref, m_ref, o_ref):
        acc = jnp.zeros((tr, cc), F32)
        for k in range(n):
            acc = acc + jnp.where(chip_ref[0] == k, m_ref[...], a_ref[k]).astype(F32)
        o_ref[...] = acc

    return pl.pallas_call(
        body,
        grid_spec=pltpu.PrefetchScalarGridSpec(
            num_scalar_prefetch=1, grid=(r // tr,),
            in_specs=[pl.BlockSpec((n, tr, cc), lambda i, chip_ref: (0, i, 0)),
                      pl.BlockSpec((None, tr, cc), lambda i, chip_ref: (chip_ref[0], i, 0))],
            out_specs=pl.BlockSpec((tr, cc), lambda i, chip_ref: (i, 0))),
        out_shape=jax.ShapeDtypeStruct((r, cc), F32), compiler_params=_params("parallel"), name=name,
    )(chip_idx, arrived, mine)


def _pair_share(reds, name):
    n = len(reds)

    def body(*refs):
        ins, outs, (send_sems, recv_sems) = refs[:n], refs[n:2 * n], refs[2 * n:]
        x, y, c = _place()
        copies = [_rcopy(ins[i], outs[i].at[c], send_sems, recv_sems, i, (x, y, 1 - c)) for i in range(n)]
        for cp in copies:
            cp.start()
        for i in range(n):
            _rcopy(ins[i], outs[i].at[1 - c], send_sems, recv_sems, i, (x, y, 1 - c)).wait_recv()
        for cp in copies:
            cp.wait_send()

    return pl.pallas_call(
        body, in_specs=[_ANY] * n, out_specs=[_ANY] * n, out_shape=[jax.ShapeDtypeStruct((2,) + r.shape, r.dtype) for r in reds],
        scratch_shapes=[pltpu.SemaphoreType.DMA((n,)), pltpu.SemaphoreType.DMA((n,))], name=name,
    )(*reds)


def _pack(arrs, rows_multiple, dtype):
    flat = jnp.concatenate([a.reshape(-1).astype(dtype) for a in arrs])
    unit = rows_multiple * PACK_COLS
    total = -(-flat.shape[0] // unit) * unit
    return jnp.pad(flat, (0, total - flat.shape[0])).reshape(-1, PACK_COLS)


def _unpack(buf, shapes):
    lead = buf.shape[:-2]
    flat = buf.reshape(*lead, -1)
    out, off = [], 0
    for shp in shapes:
        n = 1
        for s in shp:
            n *= s
        out.append(flat[..., off:off + n].reshape(*lead, *shp))
        off += n
    return out


def _join_shards(parts, axis):
    moved = jnp.moveaxis(parts, 0, axis)
    shp = list(moved.shape)
    shp[axis:axis + 2] = [shp[axis] * shp[axis + 1]]
    return moved.reshape(shp)


def _my_shard(full, axis, chip):
    n = full.shape[axis] // N_CHIP
    return lax.dynamic_slice_in_dim(full, chip * n, n, axis)


SMALL = {"norm_g": 2, "ffn_conv_w": 2, "rg_conv_w": 2, "gla_w_alpha": 2, "gla_b_alpha": 1, "gla_norm_g": 1,
         "ada_b": None, "ffn_conv_b": None, "rg_conv_b": None, "rg_ba": None, "rg_bx": None, "rg_lambda": None}
BIG = {"rg_w_in": True, "rg_wa": False, "rg_wx": False, "rg_w_out": False, "ffn_w_up": True, "ffn_w_down": False,
       "gla_w_in": True, "gla_w_out": False}
WEIGHTS = ["ada_w", "ada_b", "norm_g", "ffn_w_up", "ffn_conv_w", "ffn_conv_b", "ffn_w_down", "rg_w_in", "rg_conv_w", "rg_conv_b",
           "rg_wa", "rg_ba", "rg_wx", "rg_bx", "rg_lambda", "rg_w_out", "gla_w_in", "gla_w_alpha", "gla_b_alpha", "gla_norm_g",
           "gla_w_out"]


def kernel(x, c, ada_w, ada_b, norm_g, ffn_w_up, ffn_conv_w, ffn_conv_b, ffn_w_down, rg_w_in, rg_conv_w, rg_conv_b, rg_wa, rg_ba, rg_wx, rg_bx, rg_lambda, rg_w_out, gla_w_in, gla_w_alpha, gla_b_alpha, gla_norm_g, gla_w_out, loss_target, m_ada_w, m_ada_b, m_norm_g, m_ffn_w_up, m_ffn_conv_w, m_ffn_conv_b, m_ffn_w_down, m_rg_w_in, m_rg_conv_w, m_rg_conv_b, m_rg_wa, m_rg_ba, m_rg_wx, m_rg_bx, m_rg_lambda, m_rg_w_out, m_gla_w_in, m_gla_w_alpha, m_gla_b_alpha, m_gla_norm_g, m_gla_w_out, v_ada_w, v_ada_b, v_norm_g, v_ffn_w_up, v_ffn_conv_w, v_ffn_conv_b, v_ffn_w_down, v_rg_w_in, v_rg_conv_w, v_rg_conv_b, v_rg_wa, v_rg_ba, v_rg_wx, v_rg_bx, v_rg_lambda, v_rg_w_out, v_gla_w_in, v_gla_w_alpha, v_gla_b_alpha, v_gla_norm_g, v_gla_w_out):
    wts = dict(ada_w=ada_w, ada_b=ada_b, norm_g=norm_g, ffn_w_up=ffn_w_up, ffn_conv_w=ffn_conv_w, ffn_conv_b=ffn_conv_b,
               ffn_w_down=ffn_w_down, rg_w_in=rg_w_in, rg_conv_w=rg_conv_w, rg_conv_b=rg_conv_b, rg_wa=rg_wa, rg_ba=rg_ba,
               rg_wx=rg_wx, rg_bx=rg_bx, rg_lambda=rg_lambda, rg_w_out=rg_w_out, gla_w_in=gla_w_in, gla_w_alpha=gla_w_alpha,
               gla_b_alpha=gla_b_alpha, gla_norm_g=gla_norm_g, gla_w_out=gla_w_out)
    mom1 = dict(ada_w=m_ada_w, ada_b=m_ada_b, norm_g=m_norm_g, ffn_w_up=m_ffn_w_up, ffn_conv_w=m_ffn_conv_w,
                ffn_conv_b=m_ffn_conv_b, ffn_w_down=m_ffn_w_down, rg_w_in=m_rg_w_in, rg_conv_w=m_rg_conv_w,
                rg_conv_b=m_rg_conv_b, rg_wa=m_rg_wa, rg_ba=m_rg_ba, rg_wx=m_rg_wx, rg_bx=m_rg_bx, rg_lambda=m_rg_lambda,
                rg_w_out=m_rg_w_out, gla_w_in=m_gla_w_in, gla_w_alpha=m_gla_w_alpha, gla_b_alpha=m_gla_b_alpha,
                gla_norm_g=m_gla_norm_g, gla_w_out=m_gla_w_out)
    mom2 = dict(ada_w=v_ada_w, ada_b=v_ada_b, norm_g=v_norm_g, ffn_w_up=v_ffn_w_up, ffn_conv_w=v_ffn_conv_w,
                ffn_conv_b=v_ffn_conv_b, ffn_w_down=v_ffn_w_down, rg_w_in=v_rg_w_in, rg_conv_w=v_rg_conv_w,
                rg_conv_b=v_rg_conv_b, rg_wa=v_rg_wa, rg_ba=v_rg_ba, rg_wx=v_rg_wx, rg_bx=v_rg_bx, rg_lambda=v_rg_lambda,
                rg_w_out=v_rg_w_out, gla_w_in=v_gla_w_in, gla_w_alpha=v_gla_w_alpha, gla_b_alpha=v_gla_b_alpha,
                gla_norm_g=v_gla_norm_g, gla_w_out=v_gla_w_out)
    xi, yi, ci = _place()
    chip, me = 2 * xi + yi, 4 * xi + 2 * yi + ci
    d = x.shape[-1]
    depth = ada_w.shape[0]
    n_ada = ada_w.shape[-1]
    sharded_small = [k for k, ax in SMALL.items() if ax is not None]

    sm = _all_gather_8(_pack([c] + [wts[k] for k in sharded_small], SUBLANES, F32), "gather_small")
    c_all = sm[:, 0, :]
    parts = _unpack(sm[0::2], [c.shape] + [wts[k].shape for k in sharded_small])[1:]
    full = {k: _join_shards(p, SMALL[k]) for k, p in zip(sharded_small, parts)}
    for k, ax in SMALL.items():
        if ax is None:
            full[k] = wts[k]

    c16 = jnp.pad(c_all, ((0, ADA_ROWS - N_DEV), (0, 0)))
    ada_b_mine = lax.dynamic_slice_in_dim(ada_b, chip * n_ada, n_ada, 1)[:, None, :]
    mod_cols = _ada_fwd(c16, ada_w, ada_b_mine, "ada_fwd")
    mod_all = _all_gather_8(mod_cols.reshape(-1, PACK_COLS), "gather_mod")[0::2].reshape(N_CHIP, depth, ADA_ROWS, n_ada)
    mod = jnp.swapaxes(lax.dynamic_index_in_dim(mod_all, me, 2, keepdims=False), 0, 1).reshape(depth, 6, d)

    items = [(k, l) for k in BIG for l in range(wts[k].shape[0])]
    stage_of = lambda k, l: "rg" if k.startswith("rg_") else ("ffn0" if (k.startswith("ffn_") and l == 0) else "l1")
    staged = {st: [it for it in items if stage_of(*it) == st] for st in ("rg", "ffn0", "l1")}
    staged["l1"].sort(key=lambda it: not it[0].startswith("gla_"))
    shard = lambda k, l: wts[k][l].reshape(-1, wts[k].shape[-1]).astype(BF16)
    own = lambda got, mine: [lax.dynamic_update_index_in_dim(g, m, chip, 0) for g, m in zip(got, mine)]
    rows_joined = lambda v: v.reshape(-1, v.shape[-1])

    def placed(its, slots):
        out = {"ffn_w_up": {}, "ffn_w_down": {}}
        for (k, l), v in zip(its, slots):
            if k == "ffn_w_up":
                out[k][l] = v
            elif k == "ffn_w_down":
                out[k][l] = rows_joined(v)
            elif k in ("rg_wa", "rg_wx"):
                out[k] = _slots_to_block_rows(v, RG_BLOCKS)
            elif k == "gla_w_in":
                out[k] = _from_col_slots(v)
            else:
                out[k] = v if BIG[k] else rows_joined(v)
        return out

    after_mod = (mod[0, 0, 0] * 0.0).astype(BF16)
    sh_rg = [shard(k, l) + after_mod for k, l in staged["rg"]]
    local = {k: (v if k in ("norm_g", "ffn_conv_w", "ffn_conv_b") else v[0]) for k, v in full.items()}
    local.update(placed(staged["rg"], own(_gather_chips(sh_rg, "gather_weights_rg"), sh_rg)))
    sh_late, flying = {}, {}
    after_rg = (local["rg_w_out"][0, 0].astype(F32) * 0.0).astype(BF16)
    sh_late["ffn0"] = [shard(k, l) + after_rg for k, l in staged["ffn0"]]
    flying["ffn0"], tok = _send_start(sh_late["ffn0"], "whole", "weights_ffn0_start")
    sh_late["l1"] = [shard(k, l) + tok[0, 0].astype(BF16) for k, l in staged["l1"]]
    flying["l1"], tok2 = _send_start(sh_late["l1"], "whole", "weights_l1_start")
    mod = mod + (tok[0, 0] + tok2[0, 0])

    def fetch(stage, after):
        mine, got = _send_wait(flying[stage], after, "whole", f"weights_{stage}_wait")
        return placed(staged[stage], own(got, mine))

    c_idx = ci.reshape(1).astype(jnp.int32)
    gslots, paired, psums, sent = {}, {}, {}, {}

    def grad_slots(gr, k, l):
        g = gr[k][l] if k in ("ffn_w_up", "ffn_w_down") else gr[k]
        if k in ("rg_wa", "rg_wx"):
            return _block_rows_to_slots(g)
        if k == "gla_w_in":
            return _col_slots(g)
        return g if BIG[k] else g.reshape(N_CHIP, -1, g.shape[-1])

    def done(stage, gr):
        gslots[stage] = [grad_slots(gr, k, l) for k, l in staged[stage]]
        paired[stage], token = _send_start(gslots[stage], "pair", f"grads_{stage}_pair_start")
        return token[0, 0]

    def later(stage, after):
        mine, theirs = _send_wait(paired[stage], after, "pair", f"grads_{stage}_pair_wait")
        psums[stage] = [_pair_sum(g, t, c_idx, f"grads_pair_sum_{k}{l}") for (k, l), g, t in zip(staged[stage], mine, theirs)]
        sent[stage], token = _send_start(psums[stage], "slots", f"grads_{stage}_start")
        return token[0, 0]

    cols, grad_x, gr = _local_step(x[0], loss_target[0], mod, local, fetch, done, later)
    loss = lax.psum(0.5 * jnp.sum(cols) / d, ("x", "y", "c"))

    small_names = [k for k in SMALL if k != "ada_b"]
    gs = _all_gather_8(_pack([gr[k] for k in small_names] + [gr["mod"]], SUBLANES, F32), "gather_small_grads")
    small_shapes = [full[k].shape for k in small_names] + [(depth, 6 * d)]
    *small_sum, g_ada_b = _unpack(_sum_lead(gs, "sum_small_grads"), small_shapes)
    grads = dict(zip(small_names, small_sum))
    grads["ada_b"] = g_ada_b
    for k in sharded_small:
        grads[k] = _my_shard(grads[k], SMALL[k], chip)
    dmod_all = _unpack(gs, small_shapes)[-1].reshape(N_DEV, depth, N_CHIP, n_ada)
    dmod_mine = jnp.swapaxes(lax.dynamic_index_in_dim(dmod_all, chip, 2, keepdims=False), 0, 1)
    g_ada_w = _ada_bwd(c16, jnp.pad(dmod_mine, ((0, 0), (0, ADA_ROWS - N_DEV), (0, 0))), "ada_bwd")

    gslots["rg"] = [grad_slots(gr, k, l) for k, l in staged["rg"]]
    theirs = _pair_exchange(gslots["rg"], "grads_rg_pair_exchange")
    psums["rg"] = [_pair_sum(g, t, c_idx, f"grads_pair_sum_{k}{l}") for (k, l), g, t in zip(staged["rg"], gslots["rg"], theirs)]
    sent["rg"], rg_sent = _send_start(psums["rg"], "slots", "grads_rg_start")
    chip_idx = chip.reshape(1).astype(jnp.int32)
    delta, new_m, new_v = {}, {}, {}

    def reduce_and_update(stages, after, dep):
        its = [(st, n) for st in stages for n in range(len(staged[st]))]
        back = {st: _send_wait(sent[st], after, "slots", f"grads_{st}_wait") for st in stages}
        halves = [_chip_sum(back[st][1][n], back[st][0][n], chip_idx, "grads_chip_sum_%s%d" % staged[st][n]) for st, n in its]
        shared = _pair_share(halves, "grads_pair_share_" + stages[0])
        reduced = [lax.dynamic_update_index_in_dim(s2, h, ci, 0).reshape(-1, h.shape[-1]) for s2, h in zip(shared, halves)]
        last = None
        for k in BIG:
            gs_k = [g for (st, n), g in zip(its, reduced) if staged[st][n][0] == k]
            if gs_k:
                last = update(k, gs_k, dep)
        return last

    def update(k, gs_k, dep=None):
        shp = wts[k].shape
        if k == "gla_w_in":
            view, back = (lambda a: jnp.swapaxes(a, 1, 2)), (lambda o: jnp.swapaxes(o, 1, 2))
            gs_k = [g.T for g in gs_k]
        else:
            view, back = (lambda a: a.reshape(a.shape[0], -1, a.shape[-1])), (lambda o: o.reshape(shp))
        outs = _adamw(view(wts[k]), gs_k, view(mom1[k]), view(mom2[k]), "adamw_" + k, dep)
        grads[k], delta[k], new_m[k], new_v[k] = (back(o) for o in outs)
        return new_v[k]

    done_late = reduce_and_update(("ffn0", "l1"), grad_x, rg_sent)
    update("ada_w", g_ada_w, rg_sent)
    small_shard_shapes = [wts[k].shape for k in SMALL]
    packed = [_pack([src[k] for k in SMALL], SUBLANES, F32) for src in (wts, grads, mom1, mom2)]
    outs = _adamw(packed[0][None], [packed[1]], packed[2][None], packed[3][None], "adamw_small", rg_sent)
    for dst, o in zip((delta, new_m, new_v), outs[1:]):
        for k, a in zip(SMALL, _unpack(o[0], small_shard_shapes)):
            dst[k] = a
    reduce_and_update(("rg",), done_late, None)

    return (loss, grad_x[None], *[grads[k] for k in WEIGHTS], *[delta[k] for k in WEIGHTS], *[new_m[k] for k in WEIGHTS],
            *[new_v[k] for k in WEIGHTS])
```

```python
import jax
import jax.numpy as jnp
from jax import lax
from jax.experimental import pallas as pl
from jax.experimental.pallas import tpu as pltpu

F32 = jnp.float32
BF16 = jnp.bfloat16
MXU_DTYPE = BF16

EPS = 1e-6
RG_C = 8.0
RG_BLOCKS = 4
RG_CONV = 4
GLA_HEADS = 4
GLA_TAU = 16.0
GLA_CHUNK = 64
GLA_RANK = 16
FFN_CONV = 3
ADAM_LR = 0.001
ADAM_B1 = 0.9
ADAM_B2 = 0.999
ADAM_EPS = 1e-08
ADAM_WD = 0.01
ADAM_STEP = 10

LANES = 128
SUBLANES = 8
VMEM_LIMIT = 56 * 1024 * 1024
CB = 256
MESH = pl.DeviceIdType.MESH
N_DEV = 8
N_CHIP = 4


def _params(*sem):
    return pltpu.CompilerParams(dimension_semantics=sem, vmem_limit_bytes=VMEM_LIMIT)


def _tile(dim, prefs):
    for p in prefs:
        if dim % p == 0:
            return p
    return dim


def _dot(a, b, dims):
    return lax.dot_general(a.astype(MXU_DTYPE), b.astype(MXU_DTYPE), (dims, ((), ())), preferred_element_type=F32)


def _dot_nn(a, b):
    return _dot(a, b, ((1,), (0,)))


def _dot_nt(a, b):
    return _dot(a, b, ((1,), (1,)))


def _dot_tn(a, b):
    return _dot(a, b, ((0,), (0,)))


def _mm(a, b, *, ta=False, tb=False, a_parts=1, b_parts=1, w_slots=1, out_slots=1, out_dtype=F32, tm_max=1024, name):
    if ta:
        k_dim, m_dim = a.shape
        n_dim = b.shape[-1] * b_parts
    else:
        m_dim, k_dim = a.shape[-2], a.shape[-1] * a_parts
        n_dim = b.shape[-2] if tb else b.shape[-1] * w_slots
    n_unit = n_dim // max(b_parts, out_slots, 1 if tb else w_slots)
    k_unit = k_dim // max(a_parts, w_slots if tb else 1)
    tm = _tile(m_dim, tuple(t for t in (1024, 1408, 512, 256, 128) if t <= max(tm_max, 128)))
    tn = _tile(n_unit, (1024, 1408, 896, 512, 256, 128))
    tk = _tile(k_unit, (1024, 1408, 896, 512, 256, 128))
    nk = k_dim // tk
    dims = ((0 if ta else 1,), (1 if tb else 0,))

    def spec(shape, parts, total, tile, col_grid, row_grid):
        per = total // parts // tile

        def index(i, j, k):
            g = {"i": i, "j": j, "k": k}
            col, row = g[col_grid], g[row_grid]
            return (row, col) if parts == 1 else (col // per, row, col % per)

        return pl.BlockSpec(shape if parts == 1 else (None,) + shape, index)

    def body(a_ref, b_ref, o_ref, *acc):
        if nk == 1:
            o_ref[...] = _dot(a_ref[...], b_ref[...], dims).astype(o_ref.dtype)
            return
        acc_ref, k = acc[0], pl.program_id(2)

        @pl.when(k == 0)
        def _():
            acc_ref[...] = jnp.zeros_like(acc_ref)

        acc_ref[...] += _dot(a_ref[...], b_ref[...], dims)

        @pl.when(k == nk - 1)
        def _():
            o_ref[...] = acc_ref[...].astype(o_ref.dtype)

    if ta:
        a_spec = spec((tk, tm), 1, m_dim, tm, "i", "k")
        b_spec = spec((tk, tn), b_parts, n_dim, tn, "j", "k")
    elif tb:
        a_spec = spec((tm, tk), a_parts, k_dim, tk, "k", "i")
        b_spec = spec((tn, tk), w_slots, k_dim, tk, "k", "j")
    else:
        a_spec = spec((tm, tk), a_parts, k_dim, tk, "k", "i")
        b_spec = spec((tk, tn), w_slots, n_dim, tn, "j", "k")
    out_shape = (m_dim, n_dim) if out_slots == 1 else (out_slots, m_dim, n_dim // out_slots)
    return pl.pallas_call(
        body,
        grid=(m_dim // tm, n_dim // tn, nk),
        in_specs=[a_spec, b_spec],
        out_specs=spec((tm, tn), out_slots, n_dim, tn, "j", "i"),
        out_shape=jax.ShapeDtypeStruct(out_shape, out_dtype),
        scratch_shapes=[pltpu.VMEM((tm, tn), F32)] if nk > 1 else [],
        compiler_params=_params("parallel", "parallel", "arbitrary"),
        name=name,
    )(a, b)


ROW_TILES = (1024, 512)


def _row_specs(s, d, ts):
    return pl.BlockSpec((ts, d), lambda i: (i, 0)), pl.BlockSpec((1, d), lambda i: (0, 0))


def _norm_mod_fwd(x, g, sc, sh, name):
    s, d = x.shape
    ts = _tile(s, ROW_TILES)

    def body(x_ref, g_ref, sc_ref, sh_ref, h_ref):
        xv = x_ref[...]
        r = lax.rsqrt(jnp.mean(xv * xv, axis=-1, keepdims=True) + EPS)
        h_ref[...] = (((xv * r) * g_ref[...]) * (1.0 + sc_ref[...]) + sh_ref[...]).astype(h_ref.dtype)

    row, vec = _row_specs(s, d, ts)
    return pl.pallas_call(
        body, grid=(s // ts,), in_specs=[row, vec, vec, vec], out_specs=row,
        out_shape=jax.ShapeDtypeStruct((s, d), MXU_DTYPE), compiler_params=_params("parallel"), name=name,
    )(x, g, sc, sh)


def _norm_mod_bwd(dh, x, g, sc, dres, name):
    s, d = x.shape
    ts = _tile(s, ROW_TILES)

    def body(dh_ref, x_ref, g_ref, sc_ref, dres_ref, dx_ref, dg_ref, dsc_ref, dsh_ref, acc_ref):
        i = pl.program_id(0)

        @pl.when(i == 0)
        def _():
            acc_ref[...] = jnp.zeros_like(acc_ref)

        xv, dhv = x_ref[...], dh_ref[...]
        r = lax.rsqrt(jnp.mean(xv * xv, axis=-1, keepdims=True) + EPS)
        n = xv * r
        acc_ref[0:1, :] += jnp.sum(dhv * n, axis=0, keepdims=True)
        acc_ref[1:2, :] += jnp.sum(dhv, axis=0, keepdims=True)
        dn = dhv * ((1.0 + sc_ref[...]) * g_ref[...])
        dx_ref[...] = dres_ref[...] + r * (dn - n * jnp.mean(dn * n, axis=-1, keepdims=True))
        dg_ref[...] = (1.0 + sc_ref[...]) * acc_ref[0:1, :]
        dsc_ref[...] = g_ref[...] * acc_ref[0:1, :]
        dsh_ref[...] = acc_ref[1:2, :]

    row, vec = _row_specs(s, d, ts)
    vshape = jax.ShapeDtypeStruct((1, d), F32)
    return pl.pallas_call(
        body, grid=(s // ts,), in_specs=[row, row, vec, vec, row], out_specs=[row, vec, vec, vec],
        out_shape=[jax.ShapeDtypeStruct((s, d), F32), vshape, vshape, vshape],
        scratch_shapes=[pltpu.VMEM((SUBLANES, d), F32)], compiler_params=_params("arbitrary"), name=name,
    )(dh, x, g, sc, dres)


def _post_norm_fwd(x, y, g, gt, g2, sc, sh, name):
    s, d = x.shape
    ts = _tile(s, ROW_TILES)

    def body(x_ref, y_ref, g_ref, gt_ref, g2_ref, sc_ref, sh_ref, o_ref, h_ref):
        yv = y_ref[...]
        r = lax.rsqrt(jnp.mean(yv * yv, axis=-1, keepdims=True) + EPS)
        xn = x_ref[...] + gt_ref[...] * ((yv * r) * g_ref[...])
        o_ref[...] = xn
        r2 = lax.rsqrt(jnp.mean(xn * xn, axis=-1, keepdims=True) + EPS)
        h_ref[...] = (((xn * r2) * g2_ref[...]) * (1.0 + sc_ref[...]) + sh_ref[...]).astype(h_ref.dtype)

    row, vec = _row_specs(s, d, ts)
    return pl.pallas_call(
        body, grid=(s // ts,), in_specs=[row, row] + [vec] * 5, out_specs=[row, row],
        out_shape=[jax.ShapeDtypeStruct((s, d), F32), jax.ShapeDtypeStruct((s, d), MXU_DTYPE)],
        compiler_params=_params("parallel"), name=name,
    )(x, y, g, gt, g2, sc, sh)


def _post_bwd(dxn, y, g, gt, name):
    s, d = y.shape
    ts = _tile(s, ROW_TILES)

    def body(dxn_ref, y_ref, g_ref, gt_ref, dy_ref, dg_ref, dgt_ref, acc_ref):
        i = pl.program_id(0)

        @pl.when(i == 0)
        def _():
            acc_ref[...] = jnp.zeros_like(acc_ref)

        yv, dv = y_ref[...], dxn_ref[...]
        r = lax.rsqrt(jnp.mean(yv * yv, axis=-1, keepdims=True) + EPS)
        n = yv * r
        acc_ref[0:1, :] += jnp.sum(dv * n, axis=0, keepdims=True)
        dn = dv * (gt_ref[...] * g_ref[...])
        dy_ref[...] = (r * (dn - n * jnp.mean(dn * n, axis=-1, keepdims=True))).astype(dy_ref.dtype)
        dg_ref[...] = gt_ref[...] * acc_ref[0:1, :]
        dgt_ref[...] = g_ref[...] * acc_ref[0:1, :]

    row, vec = _row_specs(s, d, ts)
    vshape = jax.ShapeDtypeStruct((1, d), F32)
    return pl.pallas_call(
        body, grid=(s // ts,), in_specs=[row, row, vec, vec], out_specs=[row, vec, vec],
        out_shape=[jax.ShapeDtypeStruct((s, d), MXU_DTYPE), vshape, vshape],
        scratch_shapes=[pltpu.VMEM((SUBLANES, d), F32)], compiler_params=_params("arbitrary"), name=name,
    )(dxn, y, g, gt)


def _norm_post_bwd(dh, x, g, sc, dres, y, gp, gt, name):
    s, d = x.shape
    ts = _tile(s, (512,))

    def body(dh_ref, x_ref, g_ref, sc_ref, dres_ref, y_ref, gp_ref, gt_ref,
             dx_ref, dy_ref, dg_ref, dsc_ref, dsh_ref, dgp_ref, dgt_ref, acc_ref):
        i = pl.program_id(0)

        @pl.when(i == 0)
        def _():
            acc_ref[...] = jnp.zeros_like(acc_ref)

        xv, dhv = x_ref[...], dh_ref[...]
        r = lax.rsqrt(jnp.mean(xv * xv, axis=-1, keepdims=True) + EPS)
        n = xv * r
        acc_ref[0:1, :] += jnp.sum(dhv * n, axis=0, keepdims=True)
        acc_ref[1:2, :] += jnp.sum(dhv, axis=0, keepdims=True)
        dn = dhv * ((1.0 + sc_ref[...]) * g_ref[...])
        dx = dres_ref[...] + r * (dn - n * jnp.mean(dn * n, axis=-1, keepdims=True))
        dx_ref[...] = dx
        yv = y_ref[...]
        ry = lax.rsqrt(jnp.mean(yv * yv, axis=-1, keepdims=True) + EPS)
        ny = yv * ry
        acc_ref[2:3, :] += jnp.sum(dx * ny, axis=0, keepdims=True)
        dny = dx * (gt_ref[...] * gp_ref[...])
        dy_ref[...] = (ry * (dny - ny * jnp.mean(dny * ny, axis=-1, keepdims=True))).astype(dy_ref.dtype)
        dg_ref[...] = (1.0 + sc_ref[...]) * acc_ref[0:1, :]
        dsc_ref[...] = g_ref[...] * acc_ref[0:1, :]
        dsh_ref[...] = acc_ref[1:2, :]
        dgp_ref[...] = gt_ref[...] * acc_ref[2:3, :]
        dgt_ref[...] = gp_ref[...] * acc_ref[2:3, :]

    row, vec = _row_specs(s, d, ts)
    vshape = jax.ShapeDtypeStruct((1, d), F32)
    return pl.pallas_call(
        body, grid=(s // ts,), in_specs=[row, row, vec, vec, row, row, vec, vec], out_specs=[row, row] + [vec] * 5,
        out_shape=[jax.ShapeDtypeStruct((s, d), F32), jax.ShapeDtypeStruct((s, d), MXU_DTYPE)] + [vshape] * 5,
        scratch_shapes=[pltpu.VMEM((SUBLANES, d), F32)], compiler_params=_params("arbitrary"), name=name,
    )(dh, x, g, sc, dres, y, gp, gt)


def _post_loss(x, y, g, gt, tgt, name):
    s, d = x.shape
    ts = _tile(s, ROW_TILES)

    def body(x_ref, y_ref, g_ref, gt_ref, t_ref, col_ref, dx_ref):
        i = pl.program_id(0)

        @pl.when(i == 0)
        def _():
            col_ref[...] = jnp.zeros_like(col_ref)

        yv = y_ref[...]
        r = lax.rsqrt(jnp.mean(yv * yv, axis=-1, keepdims=True) + EPS)
        e = (x_ref[...] + gt_ref[...] * ((yv * r) * g_ref[...])) - t_ref[...]
        col_ref[...] += jnp.sum(e * e, axis=0, keepdims=True)
        dx_ref[...] = e * (1.0 / d)

    row, vec = _row_specs(s, d, ts)
    return pl.pallas_call(
        body, grid=(s // ts,), in_specs=[row, row, vec, vec, row], out_specs=[vec, row],
        out_shape=[jax.ShapeDtypeStruct((1, d), F32), jax.ShapeDtypeStruct((s, d), F32)],
        compiler_params=_params("arbitrary"), name=name,
    )(x, y, g, gt, tgt)


_GELU_C = 0.7978845608028654
_GELU_A = 0.044715


def _gelu(x):
    t = jnp.tanh(_GELU_C * (x + _GELU_A * x * x * x))
    return 0.5 * x * (1.0 + t), t


def _gelu_grad(x, t):
    return 0.5 * (1.0 + t) + 0.5 * x * (1.0 - t * t) * (_GELU_C * (1.0 + 3.0 * _GELU_A * x * x))


def _sigmoid(x):
    return 1.0 / (1.0 + jnp.exp(-x))


def _log1p_pos(y):
    u = 1.0 + y
    return jnp.where(u == 1.0, y, jnp.log(u) * (y / jnp.where(u == 1.0, 1.0, u - 1.0)))


def _softplus(x):
    return jnp.maximum(x, 0.0) + _log1p_pos(jnp.exp(-jnp.abs(x)))


def _one_minus_sq_exp(x, ex):
    z = 2.0 * x
    series = -z * (1.0 + z * (1.0 / 2 + z * (1.0 / 6 + z * (1.0 / 24 + z * (1.0 / 120)))))
    return jnp.where(z > -0.05, series, 1.0 - ex * ex)


SLAB = 16


def _cat(a, b):
    return jnp.concatenate([a, b], axis=1)


def _pair_specs(shape, nb, index):
    return [pl.BlockSpec(shape, lambda j, t: index(j, t) + (j,)), pl.BlockSpec(shape, lambda j, t: index(j, t) + (j + nb,))]


def _halo_row(ts, time_of):
    return lambda j, t: (jnp.maximum(time_of(t) * (ts // SUBLANES) - 1, 0),)


def _rows_from(groups, k):
    row = lax.broadcasted_iota(jnp.int32, groups[0].shape, 0)
    turned = [pltpu.roll(g, SUBLANES - k, axis=0) for g in groups]
    return [jnp.where(row < SUBLANES - k, lo, hi) for lo, hi in zip(turned[:-1], turned[1:])]


def _ffn_mid_fwd(p, cw, cb, name):
    s, f2 = p.shape
    ts = _tile(s, (1024, 512))
    nb, nt = f2 // (2 * CB), s // ts
    n_grp = SLAB // SUBLANES

    def body(pg_ref, pv_ref, hg_ref, hv_ref, cwg_ref, cwv_ref, cbg_ref, cbv_ref, a_ref, ga_ref, gb_ref):
        t = pl.program_id(1)
        cwv, bias = _cat(cwg_ref[...], cwv_ref[...]), _cat(cbg_ref[...], cbv_ref[...])
        w0, w1, w2 = cwv[0:1], cwv[1:2], cwv[2:3]

        def slab(before, cur, r0):
            pm2, pm1 = _rows_from([before] + cur, SUBLANES - 2), _rows_from([before] + cur, SUBLANES - 1)
            u = jnp.concatenate([bias + w0 * pm2[i] + w1 * pm1[i] + w2 * cur[i] for i in range(n_grp)], axis=0)
            g, v = u[:, :CB], u[:, CB:]
            gel, th = _gelu(g)
            rows = pl.ds(r0, SLAB)
            a_ref[rows, :] = (gel * v).astype(a_ref.dtype)
            ga_ref[rows, :] = gel.astype(ga_ref.dtype)
            gb_ref[rows, :] = (v * _gelu_grad(g, th)).astype(gb_ref.dtype)

        def pieces(rows):
            blk = _cat(pg_ref[rows, :], pv_ref[rows, :])
            return [blk[i * SUBLANES:(i + 1) * SUBLANES] for i in range(blk.shape[0] // SUBLANES)]

        slab(jnp.where(t > 0, _cat(hg_ref[...], hv_ref[...]), 0.0), pieces(pl.ds(0, SLAB)), 0)

        def loop(i, carry):
            r0 = pl.multiple_of(i * SLAB, SLAB)
            got = pieces(pl.ds(pl.multiple_of(r0 - SUBLANES, SUBLANES), SLAB + SUBLANES))
            slab(got[0], got[1:], r0)
            return carry

        lax.fori_loop(1, ts // SLAB, loop, 0, unroll=2)

    fwd = lambda t: t
    out = pl.BlockSpec((ts, CB), lambda j, t: (t, j))
    shape = jax.ShapeDtypeStruct((s, f2 // 2), MXU_DTYPE)
    return pl.pallas_call(
        body, grid=(nb, nt),
        in_specs=(_pair_specs((ts, CB), nb, lambda j, t: (t,)) + _pair_specs((SUBLANES, CB), nb, _halo_row(ts, fwd))
                  + _pair_specs((FFN_CONV, CB), nb, lambda j, t: (0,)) + _pair_specs((1, CB), nb, lambda j, t: (0,))),
        out_specs=[out, out, out], out_shape=[shape, shape, shape],
        compiler_params=_params("parallel", "arbitrary"), name=name,
    )(p, p, p, p, cw, cw, cb, cb)


def _ffn_mid_bwd(da, p, ga, gb, cw, name):
    s, f2 = p.shape
    ts = _tile(s, (1024, 512))
    nb, nt = f2 // (2 * CB), s // ts
    n_slab = ts // SLAB
    n_grp = SLAB // SUBLANES
    per_trip = 2

    def body(da_ref, ga_ref, gb_ref, pg_ref, pv_ref, cwg_ref, cwv_ref, dp_ref, dcw_ref, dcb_ref, next_du, acc):
        tt = pl.program_id(1)
        cwv = _cat(cwg_ref[...], cwv_ref[...])
        w0, w1, w2 = cwv[0:1], cwv[1:2], cwv[2:3]

        @pl.when(tt == 0)
        def _():
            next_du[...] = jnp.zeros_like(next_du)
            acc[...] = jnp.zeros_like(acc)

        def slab(r0, after, sums):
            rows = pl.ds(r0, SLAB)
            dav = da_ref[rows, :]
            du = _cat(dav * gb_ref[rows, :].astype(F32), dav * ga_ref[rows, :].astype(F32))
            p0 = _cat(pg_ref[rows, :], pv_ref[rows, :])
            cur = [du[i * SUBLANES:(i + 1) * SUBLANES] for i in range(n_grp)]
            du1, du2 = _rows_from(cur + [after], 1), _rows_from(cur + [after], 2)
            dpv = jnp.concatenate([w2 * cur[i] + w1 * du1[i] + w0 * du2[i] for i in range(n_grp)], axis=0).astype(dp_ref.dtype)
            dp_ref[0, rows, :] = dpv[:, :CB]
            dp_ref[1, rows, :] = dpv[:, CB:]
            for i in range(n_grp):
                pi = p0[i * SUBLANES:(i + 1) * SUBLANES]
                parts = (cur[i], du2[i] * pi, du1[i] * pi, cur[i] * pi)
                sums = parts if sums is None else tuple(x + y for x, y in zip(sums, parts))
            return cur[0], sums

        def loop(k, after):
            sums = None
            for j in range(per_trip):
                r0 = pl.multiple_of((n_slab - 1 - (k * per_trip + j)) * SLAB, SLAB)
                after, sums = slab(r0, after, sums)
            for q, part in enumerate(sums):
                acc[q] += part
            return after

        next_du[...] = lax.fori_loop(0, n_slab // per_trip, loop, next_du[...])

        @pl.when(tt == nt - 1)
        def _():
            for half in range(2):
                cols = slice(half * CB, (half + 1) * CB)
                dcb_ref[half] = jnp.sum(acc[0][:, cols], axis=0, keepdims=True)
                for k in range(FFN_CONV):
                    dcw_ref[half, k:k + 1, :] = jnp.sum(acc[1 + k][:, cols], axis=0, keepdims=True)

    rev = lambda t: nt - 1 - t
    tile = pl.BlockSpec((ts, CB), lambda j, t: (rev(t), j))
    return pl.pallas_call(
        body, grid=(nb, nt),
        in_specs=([tile, tile, tile] + _pair_specs((ts, CB), nb, lambda j, t: (rev(t),))
                  + _pair_specs((FFN_CONV, CB), nb, lambda j, t: (0,))),
        out_specs=[pl.BlockSpec((2, ts, CB), lambda j, t: (0, rev(t), j)),
                   pl.BlockSpec((2, FFN_CONV, CB), lambda j, t: (0, 0, j)),
                   pl.BlockSpec((2, 1, CB), lambda j, t: (0, 0, j))],
        out_shape=[jax.ShapeDtypeStruct((2, s, f2 // 2), MXU_DTYPE), jax.ShapeDtypeStruct((2, FFN_CONV, f2 // 2), F32),
                   jax.ShapeDtypeStruct((2, 1, f2 // 2), F32)],
        scratch_shapes=[pltpu.VMEM((SUBLANES, 2 * CB), F32), pltpu.VMEM((1 + FFN_CONV, SUBLANES, 2 * CB), F32)],
        compiler_params=_params("parallel", "arbitrary"), name=name,
    )(da, ga, gb, p, p, cw, cw)


def _rg_gates(xc, wa_ref, ba_ref, wx_ref, bx_ref, lam_ref):
    r = _sigmoid(_dot_nn(xc, wa_ref[0]) + ba_ref[...])
    ig = _sigmoid(_dot_nn(xc, wx_ref[0]) + bx_ref[...])
    sp = _softplus(-lam_ref[...])
    log_a = (-RG_C) * r * sp
    a = jnp.exp(log_a)
    mult = jnp.sqrt(_one_minus_sq_exp(log_a, a))
    return r, ig, sp, a, mult


def _rg_conv(scr, cw_ref, cb_ref, ts):
    views = [scr[5 + k:5 + k + ts, :] for k in range(RG_CONV)]
    xc = cb_ref[...]
    for k in range(RG_CONV):
        xc = xc + cw_ref[k:k + 1, :] * views[k]
    return xc, views


def _rg_param_specs():
    vec = pl.BlockSpec((1, CB), lambda g, t: (0, g))
    mat = pl.BlockSpec((1, CB, CB), lambda g, t: (g, 0, 0))
    return [pl.BlockSpec((RG_CONV, CB), lambda g, t: (0, g)), vec, mat, vec, mat, vec, vec]


def _scan_rows(a_scr, x_scr, out_ref, carry, ts, reverse):
    n = ts // SUBLANES
    row = lax.broadcasted_iota(jnp.int32, (SUBLANES, a_scr.shape[1]), 0)
    last = SUBLANES - 1

    def rows_of(k):
        return pl.ds(pl.multiple_of(k * SUBLANES, SUBLANES), SUBLANES)

    def local(k, _):
        rows = rows_of(k)
        a, x = a_scr[rows, :], x_scr[rows, :]
        if reverse:
            a = jnp.where(row == last, 1.0, pltpu.roll(a, last, axis=0))
            for sh in (1, 2, 4):
                keep = row < SUBLANES - sh
                x = x + a * jnp.where(keep, pltpu.roll(x, SUBLANES - sh, axis=0), 0.0)
                a = a * jnp.where(keep, pltpu.roll(a, SUBLANES - sh, axis=0), 1.0)
        else:
            for sh in (1, 2, 4):
                keep = row >= sh
                x = a * jnp.where(keep, pltpu.roll(x, sh, axis=0), 0.0) + x
                a = a * jnp.where(keep, pltpu.roll(a, sh, axis=0), 1.0)
        out_ref[rows, :] = x
        x_scr[rows, :] = a
        return 0

    lax.fori_loop(0, n, local, 0, unroll=4)

    def chain(k, c):
        rows = rows_of(n - 1 - k if reverse else k)
        v = out_ref[rows, :] + x_scr[rows, :] * c
        out_ref[rows, :] = v
        return a_scr[rows, :][0:1] * v[0:1] if reverse else v[last:last + 1]

    return lax.fori_loop(0, n, chain, carry, unroll=4)


def _rg_mid_fwd(pj, cw, cb, wa, ba, wx, bx, lam, name):
    s = pj.shape[0]
    nb = pj.shape[1] // (2 * CB)
    ts = _tile(s, (512,))
    nt = s // ts

    def body(gate_ref, x_ref, halo_ref, cw_ref, cb_ref, wa_ref, ba_ref, wx_ref, bx_ref, lam_ref, y_ref, hs_ref,
             scr, a_scr, u_scr, h_scr):
        t = pl.program_id(1)

        @pl.when(t == 0)
        def _():
            h_scr[...] = jnp.zeros_like(h_scr)

        scr[0:SUBLANES, :] = jnp.where(t > 0, halo_ref[...], 0.0)
        scr[SUBLANES:, :] = x_ref[...]
        xc, _ = _rg_conv(scr, cw_ref, cb_ref, ts)
        _, ig, _, a, mult = _rg_gates(xc, wa_ref, ba_ref, wx_ref, bx_ref, lam_ref)
        a_scr[...] = a
        u_scr[...] = mult * (ig * xc)
        h_scr[0:1, :] = _scan_rows(a_scr, u_scr, hs_ref, h_scr[0:1, :], ts, False)
        y_ref[...] = (_gelu(gate_ref[...])[0] * hs_ref[...]).astype(y_ref.dtype)

    blk = pl.BlockSpec((ts, CB), lambda g, t: (t, g))
    return pl.pallas_call(
        body, grid=(nb, nt),
        in_specs=_pair_specs((ts, CB), nb, lambda g, t: (t,))
        + [pl.BlockSpec((SUBLANES, CB), lambda g, t: _halo_row(ts, lambda u: u)(g, t) + (g + nb,))] + _rg_param_specs(),
        out_specs=[blk, blk],
        out_shape=[jax.ShapeDtypeStruct((s, nb * CB), MXU_DTYPE), jax.ShapeDtypeStruct((s, nb * CB), F32)],
        scratch_shapes=[pltpu.VMEM((ts + SUBLANES, CB), F32), pltpu.VMEM((ts, CB), F32), pltpu.VMEM((ts, CB), F32),
                        pltpu.VMEM((SUBLANES, CB), F32)],
        compiler_params=_params("parallel", "arbitrary"), name=name,
    )(pj, pj, pj, cw, cb, wa, ba, wx, bx, lam)


def _rg_mid_bwd(dy, pj, hs, cw, cb, wa, ba, wx, bx, lam, name):
    s = pj.shape[0]
    nb = pj.shape[1] // (2 * CB)
    ts = _tile(s, (512,))
    nt = s // ts

    def body(dy_ref, gate_ref, x_ref, halo_ref, hs_ref, hsh_ref, cw_ref, cb_ref, wa_ref, ba_ref, wx_ref, bx_ref, lam_ref,
             dpj_ref, dcw_ref, dcb_ref, dwa_ref, dba_ref, dwx_ref, dbx_ref, dlam_ref,
             scr, hscr, a_scr, d_scr, g_scr, dxscr, c_scr):
        tt = pl.program_id(1)
        t = nt - 1 - tt

        @pl.when(tt == 0)
        def _():
            c_scr[...] = jnp.zeros_like(c_scr)
            dxscr[ts:, :] = jnp.zeros((SUBLANES, CB), F32)
            for ref in (dcw_ref, dcb_ref, dwa_ref, dba_ref, dwx_ref, dbx_ref, dlam_ref):
                ref[...] = jnp.zeros_like(ref)

        scr[0:SUBLANES, :] = jnp.where(t > 0, halo_ref[...], 0.0)
        scr[SUBLANES:, :] = x_ref[...]
        hscr[0:SUBLANES, :] = jnp.where(t > 0, hsh_ref[...], 0.0)
        hscr[SUBLANES:, :] = hs_ref[...]
        xc, views = _rg_conv(scr, cw_ref, cb_ref, ts)
        r, ig, sp, a, mult = _rg_gates(xc, wa_ref, ba_ref, wx_ref, bx_ref, lam_ref)
        gate = gate_ref[...]
        gel, th = _gelu(gate)
        dyv = dy_ref[...]
        dpj_ref[0] = (dyv * hs_ref[...] * _gelu_grad(gate, th)).astype(dpj_ref.dtype)
        a_scr[...] = a
        d_scr[...] = dyv * gel
        c_scr[0:1, :] = _scan_rows(a_scr, d_scr, g_scr, c_scr[0:1, :], ts, True)
        du = g_scr[...]
        da = du * hscr[7:7 + ts, :]
        dmult = du * (ig * xc)
        dig = du * (mult * xc)
        dxc = du * (mult * ig)
        dlog_a = da * a - dmult * (a * a / mult)
        dlam_ref[...] += jnp.sum(dlog_a * r, axis=0, keepdims=True) * (RG_C * _sigmoid(-lam_ref[...]))
        dpr = dlog_a * ((-RG_C) * sp) * (r * (1.0 - r))
        dpi = dig * (ig * (1.0 - ig))
        dba_ref[...] += jnp.sum(dpr, axis=0, keepdims=True)
        dbx_ref[...] += jnp.sum(dpi, axis=0, keepdims=True)
        dwa_ref[0] += _dot_tn(xc, dpr)
        dwx_ref[0] += _dot_tn(xc, dpi)
        dxc = dxc + _dot_nt(dpr, wa_ref[0]) + _dot_nt(dpi, wx_ref[0])
        dcb_ref[...] += jnp.sum(dxc, axis=0, keepdims=True)
        for k in range(RG_CONV):
            dcw_ref[k:k + 1, :] += jnp.sum(dxc * views[k], axis=0, keepdims=True)
        dxscr[0:ts, :] = dxc
        dxp = cw_ref[3:4, :] * dxc
        for k in range(RG_CONV - 1):
            dxp = dxp + cw_ref[k:k + 1, :] * dxscr[3 - k:3 - k + ts, :]
        dpj_ref[1] = dxp.astype(dpj_ref.dtype)
        dxscr[ts:, :] = dxscr[0:SUBLANES, :]

    rev = lambda g, t: (nt - 1 - t, g)
    rev_halo = lambda g, t: (jnp.maximum((nt - 1 - t) * (ts // SUBLANES) - 1, 0), g)
    vec = pl.BlockSpec((1, CB), lambda g, t: (0, g))
    mat = pl.BlockSpec((1, CB, CB), lambda g, t: (g, 0, 0))
    d = nb * CB
    vshape = jax.ShapeDtypeStruct((1, d), F32)
    mshape = jax.ShapeDtypeStruct((nb, CB, CB), F32)
    return pl.pallas_call(
        body, grid=(nb, nt),
        in_specs=[pl.BlockSpec((ts, CB), rev)] + _pair_specs((ts, CB), nb, lambda g, t: (nt - 1 - t,))
        + [pl.BlockSpec((SUBLANES, CB), lambda g, t: (rev_halo(g, t)[0], g + nb)),
           pl.BlockSpec((ts, CB), rev), pl.BlockSpec((SUBLANES, CB), rev_halo)] + _rg_param_specs(),
        out_specs=[pl.BlockSpec((2, ts, CB), lambda g, t: (0, nt - 1 - t, g)), pl.BlockSpec((RG_CONV, CB), lambda g, t: (0, g)),
                   vec, mat, vec, mat, vec, vec],
        out_shape=[jax.ShapeDtypeStruct((2, s, d), MXU_DTYPE), jax.ShapeDtypeStruct((RG_CONV, d), F32), vshape, mshape, vshape,
                   mshape, vshape, vshape],
        scratch_shapes=[pltpu.VMEM((ts + SUBLANES, CB), F32), pltpu.VMEM((ts + SUBLANES, CB), F32), pltpu.VMEM((ts, CB), F32),
                        pltpu.VMEM((ts, CB), F32), pltpu.VMEM((ts, CB), F32), pltpu.VMEM((ts + SUBLANES, CB), F32),
                        pltpu.VMEM((SUBLANES, CB), F32)],
        compiler_params=_params("parallel", "arbitrary"), name=name,
    )(dy, pj, pj, pj, hs, hs, cw, cb, wa, ba, wx, bx, lam)


GLA_DK = 128
GLA_DV = 256
GLA_O_K = GLA_HEADS * GLA_DK
GLA_O_V = 2 * GLA_HEADS * GLA_DK
GLA_O_R = GLA_O_V + GLA_HEADS * GLA_DV
GLA_O_Z = GLA_O_R + GLA_HEADS * GLA_DV
GLA_IN = GLA_O_Z + GLA_RANK
GLA_TS = 256


def _dk(h, base=0):
    return slice(base + h * GLA_DK, base + (h + 1) * GLA_DK)


def _dv(h, base=0):
    return slice(base + h * GLA_DV, base + (h + 1) * GLA_DV)


def _split3(x):
    hi = x.astype(BF16)
    r1 = x - hi.astype(F32)
    mid = r1.astype(BF16)
    lo = (r1 - mid.astype(F32)).astype(BF16)
    return hi, mid, lo


def _chunk_cumsum(x, reverse):
    n = x.shape[0]
    i = lax.broadcasted_iota(jnp.int32, (n, n), 0)
    j = lax.broadcasted_iota(jnp.int32, (n, n), 1)
    same = (i // GLA_CHUNK) == (j // GLA_CHUNK)
    tri = jnp.where(same & ((j >= i) if reverse else (j <= i)), 1.0, 0.0).astype(BF16)
    out = jnp.zeros(x.shape, F32)
    for piece in _split3(x):
        out = out + lax.dot_general(tri, piece, (((1,), (0,)), ((), ())), preferred_element_type=F32)
    return out


def _gla_head(pj_ref, h):
    return (pj_ref[:, _dk(h)] * (GLA_DK ** -0.5), pj_ref[:, _dk(h, GLA_O_K)], pj_ref[:, _dv(h, GLA_O_V)],
            pj_ref[:, _dv(h, GLA_O_R)])


def _gla_decays(gc):
    gref = gc[GLA_CHUNK // 2:GLA_CHUNK // 2 + 1, :]
    glast = gc[GLA_CHUNK - 1:GLA_CHUNK, :]
    return jnp.exp(gc), jnp.exp(gc - gref), jnp.exp(gref - gc), jnp.exp(glast - gc), jnp.exp(glast)


def _causal_mask():
    i = lax.broadcasted_iota(jnp.int32, (GLA_CHUNK, GLA_CHUNK), 0)
    j = lax.broadcasted_iota(jnp.int32, (GLA_CHUNK, GLA_CHUNK), 1)
    return j <= i


def _log_sigmoid(x):
    return jnp.minimum(x, 0.0) - _log1p_pos(jnp.exp(-jnp.abs(x)))


def _gla_mid_fwd(pj, wal, bal, ng, name):
    s, nh = pj.shape[0], GLA_HEADS
    ts = _tile(s, (GLA_TS,))
    nt, nc = s // ts, ts // GLA_CHUNK

    def body(pj_ref, wal_ref, bal_ref, ng_ref, act_ref, o_ref, st_ref, s_scr):
        t = pl.program_id(0)

        @pl.when(t == 0)
        def _():
            s_scr[...] = jnp.zeros_like(s_scr)

        heads = []
        z = pj_ref[:, GLA_O_Z:]
        for h in range(nh):
            q, k, v, r = _gla_head(pj_ref, h)
            g = _log_sigmoid(_dot_nn(z, wal_ref[:, _dk(h)]) + bal_ref[:, _dk(h)]) * (1.0 / GLA_TAU)
            heads.append((q, k, v, r, _chunk_cumsum(g, False)))
        mask = _causal_mask()
        for c in range(nc):
            sl = slice(c * GLA_CHUNK, (c + 1) * GLA_CHUNK)
            for h, (q, k, v, r, gcum) in enumerate(heads):
                eg, eq, ek, ekd, egl = _gla_decays(gcum[sl])
                st = s_scr[h]
                st_ref[c, h] = st
                attn = jnp.where(mask, _dot_nt(q[sl] * eq, k[sl] * ek), 0.0)
                o_ref[sl, h * GLA_DV:(h + 1) * GLA_DV] = _dot_nt(q[sl] * eg, st) + _dot_nn(attn, v[sl])
                s_scr[h] = st * egl + _dot_tn(v[sl], k[sl] * ekd)
        for h, (q, k, v, r, gcum) in enumerate(heads):
            cols = slice(h * GLA_DV, (h + 1) * GLA_DV)
            o = o_ref[:, cols]
            on = o * lax.rsqrt(jnp.mean(o * o, axis=-1, keepdims=True) + EPS)
            act_ref[:, cols] = ((on * ng_ref[...]) * (r * _sigmoid(r))).astype(act_ref.dtype)

    blk = pl.BlockSpec((ts, nh * GLA_DV), lambda t: (t, 0))
    whole = lambda shape: pl.BlockSpec(shape, lambda t: (0,) * len(shape))
    return pl.pallas_call(
        body, grid=(nt,),
        in_specs=[pl.BlockSpec((ts, GLA_IN), lambda t: (t, 0)), whole((GLA_RANK, nh * GLA_DK)), whole((1, nh * GLA_DK)),
                  whole((1, GLA_DV))],
        out_specs=[blk, blk, pl.BlockSpec((nc, nh, GLA_DV, GLA_DK), lambda t: (t, 0, 0, 0))],
        out_shape=[jax.ShapeDtypeStruct((s, nh * GLA_DV), MXU_DTYPE), jax.ShapeDtypeStruct((s, nh * GLA_DV), F32),
                   jax.ShapeDtypeStruct((s // GLA_CHUNK, nh, GLA_DV, GLA_DK), F32)],
        scratch_shapes=[pltpu.VMEM((nh, GLA_DV, GLA_DK), F32)],
        compiler_params=_params("arbitrary"), name=name,
    )(pj, wal, bal, ng)


def _gla_mid_bwd(dact, pj, o, st, wal, bal, ng, name):
    s, nh = pj.shape[0], GLA_HEADS
    ts = _tile(s, (GLA_TS,))
    nt, nc = s // ts, ts // GLA_CHUNK

    def body(dact_ref, pj_ref, o_ref, st_ref, wal_ref, bal_ref, ng_ref, dpj_ref, dwal_ref, dbal_ref, dng_ref,
             ds_scr, dg_scr):
        tt = pl.program_id(0)

        @pl.when(tt == 0)
        def _():
            ds_scr[...] = jnp.zeros_like(ds_scr)
            dwal_ref[...] = jnp.zeros_like(dwal_ref)
            dbal_ref[...] = jnp.zeros_like(dbal_ref)
            dng_ref[...] = jnp.zeros_like(dng_ref)

        heads = []
        z = pj_ref[:, GLA_O_Z:]
        for h in range(nh):
            q, k, v, r = _gla_head(pj_ref, h)
            logit = _dot_nn(z, wal_ref[:, _dk(h)]) + bal_ref[:, _dk(h)]
            gcum = _chunk_cumsum(_log_sigmoid(logit) * (1.0 / GLA_TAU), False)
            ov = o_ref[:, h * GLA_DV:(h + 1) * GLA_DV]
            ro = lax.rsqrt(jnp.mean(ov * ov, axis=-1, keepdims=True) + EPS)
            on = ov * ro
            sg = _sigmoid(r)
            sil = r * sg
            dav = dact_ref[:, h * GLA_DV:(h + 1) * GLA_DV]
            dpj_ref[:, _dv(h, GLA_O_R)] = (dav * (on * ng_ref[...]) * (sg + sil * (1.0 - sg))).astype(dpj_ref.dtype)
            t1 = dav * sil
            dng_ref[...] += jnp.sum(t1 * on, axis=0, keepdims=True)
            dn = t1 * ng_ref[...]
            do = ro * (dn - on * jnp.mean(dn * on, axis=-1, keepdims=True))
            heads.append((q, k, v, logit, gcum, do))
        mask = _causal_mask()
        scale = GLA_DK ** -0.5
        last_row = lax.broadcasted_iota(jnp.int32, (GLA_CHUNK, GLA_DK), 0) == GLA_CHUNK - 1
        for c in reversed(range(nc)):
            sl = slice(c * GLA_CHUNK, (c + 1) * GLA_CHUNK)
            for h, (q, k, v, logit, gcum, do) in enumerate(heads):
                eg, eq, ek, ekd, egl = _gla_decays(gcum[sl])
                qc, kc, vc, doc = q[sl], k[sl], v[sl], do[sl]
                qg, qt, kt, kd = qc * eg, qc * eq, kc * ek, kc * ekd
                sp = st_ref[c, h]
                ds = ds_scr[h]
                attn = jnp.where(mask, _dot_nt(qt, kt), 0.0)
                dattn = jnp.where(mask, _dot_nt(doc, vc), 0.0)
                dqg = _dot_nn(doc, sp)
                dqt = _dot_nn(dattn, kt)
                dkt = _dot_tn(dattn, qt)
                dkd = _dot_nn(vc, ds)
                dpj_ref[sl, _dv(h, GLA_O_V)] = (_dot_tn(attn, doc) + _dot_nt(kd, ds)).astype(dpj_ref.dtype)
                dpj_ref[sl, _dk(h)] = (scale * (dqg * eg + dqt * eq)).astype(dpj_ref.dtype)
                dpj_ref[sl, _dk(h, GLA_O_K)] = (dkt * ek + dkd * ekd).astype(dpj_ref.dtype)
                kdd = dkd * kd
                dgl = jnp.sum(kdd, axis=0, keepdims=True) + jnp.sum(ds * sp, axis=0, keepdims=True) * egl
                dg_scr[h, sl, :] = dqg * qg + dqt * qt - dkt * kt - kdd + jnp.where(last_row, dgl, 0.0)
                ds_scr[h] = ds * egl + _dot_tn(doc, qg)
        dz = jnp.zeros((ts, GLA_RANK), F32)
        for h, (q, k, v, logit, gcum, do) in enumerate(heads):
            dlogit = _chunk_cumsum(dg_scr[h], True) * (1.0 / GLA_TAU) * _sigmoid(-logit)
            dz = dz + _dot_nt(dlogit, wal_ref[:, _dk(h)])
            dwal_ref[:, _dk(h)] += _dot_tn(z, dlogit)
            dbal_ref[:, _dk(h)] += jnp.sum(dlogit, axis=0, keepdims=True)
        dpj_ref[:, GLA_O_Z:] = dz.astype(dpj_ref.dtype)

    rev = lambda t: (nt - 1 - t, 0)
    whole = lambda shape: pl.BlockSpec(shape, lambda t: (0,) * len(shape))
    wide = pl.BlockSpec((ts, nh * GLA_DV), rev)
    return pl.pallas_call(
        body, grid=(nt,),
        in_specs=[wide, pl.BlockSpec((ts, GLA_IN), rev), wide,
                  pl.BlockSpec((nc, nh, GLA_DV, GLA_DK), lambda t: (nt - 1 - t, 0, 0, 0)),
                  whole((GLA_RANK, nh * GLA_DK)), whole((1, nh * GLA_DK)), whole((1, GLA_DV))],
        out_specs=[pl.BlockSpec((ts, GLA_IN), rev), whole((GLA_RANK, nh * GLA_DK)), whole((1, nh * GLA_DK)), whole((1, GLA_DV))],
        out_shape=[jax.ShapeDtypeStruct((s, GLA_IN), MXU_DTYPE), jax.ShapeDtypeStruct((GLA_RANK, nh * GLA_DK), F32),
                   jax.ShapeDtypeStruct((1, nh * GLA_DK), F32), jax.ShapeDtypeStruct((1, GLA_DV), F32)],
        scratch_shapes=[pltpu.VMEM((nh, GLA_DV, GLA_DK), F32), pltpu.VMEM((nh, ts, GLA_DK), F32)],
        compiler_params=_params("arbitrary"), name=name,
    )(dact, pj, o, st, wal, bal, ng)


def _adamw(w, gs, m, v, name, after=None):
    layers, rows, cols = w.shape
    gs = list(gs) if isinstance(gs, (list, tuple)) else gs
    n_g = len(gs) if isinstance(gs, list) else 1
    if rows % SUBLANES == 0:
        tr, tc = _tile(rows, (256, 128, 64, 32, 16, 8)), cols
    else:
        tr, tc = rows, _tile(cols, (256, 128))
    c1 = 1.0 / (1.0 - ADAM_B1 ** ADAM_STEP)
    c2 = 1.0 / (1.0 - ADAM_B2 ** ADAM_STEP)

    def body(*refs):
        g_refs, (w_ref, m_ref, v_ref) = refs[:n_g], refs[n_g:n_g + 3]
        go_ref, d_ref, mo_ref, vo_ref = refs[-4:]
        gv = g_refs[0][...]
        for l in range(1, n_g):
            gv = jnp.where(pl.program_id(0) == l, g_refs[l][...], gv)
        m2 = ADAM_B1 * m_ref[...] + (1.0 - ADAM_B1) * gv
        v2 = ADAM_B2 * v_ref[...] + (1.0 - ADAM_B2) * (gv * gv)
        d_ref[...] = (-ADAM_LR) * ((m2 * c1) / (jnp.sqrt(v2 * c2) + ADAM_EPS) + ADAM_WD * w_ref[...])
        go_ref[...] = gv
        mo_ref[...] = m2
        vo_ref[...] = v2

    spec = pl.BlockSpec((None, tr, tc), lambda l, i, j: (l, i, j))
    g_specs = [pl.BlockSpec((tr, tc), lambda l, i, j: (i, j))] * n_g if isinstance(gs, list) else [spec]
    extra = [] if after is None else [(after, _ANY)]
    shape = jax.ShapeDtypeStruct((layers, rows, cols), F32)
    return pl.pallas_call(
        body, grid=(layers, rows // tr, cols // tc), in_specs=g_specs + [spec] * 3 + [sp for _, sp in extra],
        out_specs=[spec] * 4, out_shape=[shape] * 4, compiler_params=_params("parallel", "parallel", "parallel"), name=name,
    )(*(gs if isinstance(gs, list) else [gs]), w, m, v, *[a for a, _ in extra])


def _col_slots(w):
    r, c = w.shape
    return jnp.moveaxis(w.reshape(r, N_CHIP, c // N_CHIP), 1, 0)


def _from_col_slots(w):
    n, r, c = w.shape
    return jnp.moveaxis(w, 0, 1).reshape(r, n * c)


def _block_rows_to_slots(w):
    g, r4, cc = w.shape
    return jnp.swapaxes(w.reshape(g, N_CHIP, r4 // N_CHIP, cc), 0, 1).reshape(N_CHIP, g * (r4 // N_CHIP), cc)


def _slots_to_block_rows(w, g):
    n, gr, cc = w.shape
    return jnp.swapaxes(w.reshape(n, g, gr // g, cc), 0, 1).reshape(g, n * (gr // g), cc)


def _local_step(x, tgt, mod, w, fetch=None, done=None, later=None):
    depth = mod.shape[0]
    row = lambda v: v.reshape(1, -1)
    w = dict(w)
    w["ffn_w_up"], w["ffn_w_down"] = dict(enumerate(w["ffn_w_up"])), dict(enumerate(w["ffn_w_down"]))

    def arrive(stage, after):
        if fetch is not None:
            for k, v in fetch(stage, after).items():
                if isinstance(v, dict):
                    w[k].update(v)
                else:
                    w[k] = v

    saved = []
    for i in range(depth):
        if i == 1:
            arrive("l1", x)
        sh_m, sc_m, gt_m, sh_f, sc_f, gt_f = (mod[i, j:j + 1] for j in range(6))
        g0, g1, g2, g3 = (w["norm_g"][i, j:j + 1] for j in range(4))
        tag = f"_l{i}"
        if i == 0:
            h = _norm_mod_fwd(x, g0, sc_m, sh_m, "norm_mix" + tag)
        if i % 2 == 0:
            pj = _mm(h, w["rg_w_in"], w_slots=N_CHIP, name="rg_in" + tag)
            act, aux = _rg_mid_fwd(pj, w["rg_conv_w"], row(w["rg_conv_b"]), w["rg_wa"], row(w["rg_ba"]), w["rg_wx"],
                                   row(w["rg_bx"]), row(w["rg_lambda"]), "rg_mid" + tag)
            y = _mm(act, w["rg_w_out"], name="rg_out" + tag)
        else:
            pj = _mm(h, w["gla_w_in"], tm_max=512, name="gla_in" + tag)
            act, *aux = _gla_mid_fwd(pj, w["gla_w_alpha"], row(w["gla_b_alpha"]), row(w["gla_norm_g"]), "gla_mid" + tag)
            y = _mm(act, w["gla_w_out"], name="gla_out" + tag)
        x1, h2 = _post_norm_fwd(x, y, g1, gt_m, g2, sc_f, sh_f, "post_mix" + tag)
        if i == 0:
            arrive("ffn0", x1)
        p = _mm(h2, w["ffn_w_up"][i], w_slots=N_CHIP, name="ffn_up" + tag)
        a, ga, gb = _ffn_mid_fwd(p, w["ffn_conv_w"][i], w["ffn_conv_b"][i:i + 1], "ffn_mid" + tag)
        y2 = _mm(a, w["ffn_w_down"][i], name="ffn_down" + tag)
        saved_h = h
        if i + 1 < depth:
            nxt = [mod[i + 1, j:j + 1] for j in range(2)] + [w["norm_g"][i + 1, 0:1]]
            x2, h = _post_norm_fwd(x1, y2, g3, gt_f, nxt[2], nxt[1], nxt[0], "post_ffn" + tag)
        else:
            x2 = None
            cols, dx = _post_loss(x1, y2, g3, gt_f, tgt, "post_ffn_loss")
        saved.append((x, saved_h, pj, act, aux, y, x1, h2, p, (a, ga, gb), y2))
        x = x2

    stacked = ("norm_g", "ffn_conv_w", "ffn_conv_b", "mod")
    gr = {k: [None] * depth for k in stacked + ("ffn_w_up", "ffn_w_down")}
    told = lambda stage: done(stage, gr) if done is not None else 0.0
    told_later = lambda stage, after: later(stage, after) if later is not None else 0.0
    for i in reversed(range(depth)):
        x0, h, pj, act, aux, y, x1, h2, p, (a, ga, gb), y2 = saved[i]
        sh_m, sc_m, gt_m, sh_f, sc_f, gt_f = (mod[i, j:j + 1] for j in range(6))
        g0, g1, g2, g3 = (w["norm_g"][i, j:j + 1] for j in range(4))
        tag = f"_l{i}"
        if i == depth - 1:
            dy2, d_g3, d_gt_f = _post_bwd(dx, y2, g3, gt_f, "post_ffn_b" + tag)
        else:
            dy2, d_g3, d_gt_f = ahead
        da = _mm(dy2, w["ffn_w_down"][i], tb=True, name="ffn_down_dx" + tag)
        gr["ffn_w_down"][i] = _mm(a, dy2, ta=True, name="ffn_down_dw" + tag)
        conv_w = w["ffn_conv_w"][i] + (told_later("l1", da) if i == 0 else 0.0)
        dp, dcw, dcb = _ffn_mid_bwd(da, p, ga, gb, conv_w, "ffn_mid_b" + tag)
        gr["ffn_conv_w"][i], gr["ffn_conv_b"][i] = _cat(dcw[0], dcw[1]), _cat(dcb[0], dcb[1])[0]
        dh2 = _mm(dp, w["ffn_w_up"][i], tb=True, a_parts=2, w_slots=N_CHIP, name="ffn_up_dx" + tag)
        gr["ffn_w_up"][i] = _mm(h2, dp, ta=True, b_parts=2, out_slots=N_CHIP, name="ffn_up_dw" + tag)
        if i == 0:
            gt_m = gt_m + told("ffn0")
        dx1, dy, d_g2, d_sc_f, d_sh_f, d_g1, d_gt_m = _norm_post_bwd(dh2, x1, g2, sc_f, dx, y, g1, gt_m, "norm_ffn_b" + tag)
        if i % 2 == 0:
            dact = _mm(dy, w["rg_w_out"], tb=True, name="rg_out_dx" + tag)
            gr["rg_w_out"] = _mm(act, dy, ta=True, name="rg_out_dw" + tag)
            lam = row(w["rg_lambda"]) + told_later("ffn0", gr["rg_w_out"])
            dpj, gr["rg_conv_w"], d_cb, gr["rg_wa"], d_ba, gr["rg_wx"], d_bx, d_lam = _rg_mid_bwd(
                dact, pj, aux, w["rg_conv_w"], row(w["rg_conv_b"]), w["rg_wa"], row(w["rg_ba"]), w["rg_wx"],
                row(w["rg_bx"]), lam, "rg_mid_b" + tag)
            gr["rg_conv_b"], gr["rg_ba"], gr["rg_bx"], gr["rg_lambda"] = d_cb[0], d_ba[0], d_bx[0], d_lam[0]
            dh = _mm(dpj, w["rg_w_in"], tb=True, a_parts=2, w_slots=N_CHIP, name="rg_in_dx" + tag)
            gr["rg_w_in"] = _mm(h, dpj, ta=True, b_parts=2, out_slots=N_CHIP, name="rg_in_dw" + tag)
        else:
            dact = _mm(dy, w["gla_w_out"], tb=True, name="gla_out_dx" + tag)
            gr["gla_w_out"] = _mm(act, dy, ta=True, name="gla_out_dw" + tag)
            dpj, gr["gla_w_alpha"], d_bal, d_ng = _gla_mid_bwd(dact, pj, aux[0], aux[1], w["gla_w_alpha"], row(w["gla_b_alpha"]),
                                                               row(w["gla_norm_g"]), "gla_mid_b" + tag)
            gr["gla_b_alpha"], gr["gla_norm_g"] = d_bal[0], d_ng[0]
            dh = _mm(dpj, w["gla_w_in"], tb=True, name="gla_in_dx" + tag)
            gr["gla_w_in"] = _mm(h, dpj, ta=True, tm_max=512, name="gla_in_dw" + tag)
            mod = mod.at[0].add(told("l1"))
        if i > 0:
            dx, dy_below, d_g0, d_sc_m, d_sh_m, d_g_below, d_gt_below = _norm_post_bwd(
                dh, x0, g0, sc_m, dx1, saved[i - 1][-1], w["norm_g"][i - 1, 3:4], mod[i - 1, 5:6], "norm_mix_b" + tag)
            ahead = (dy_below, d_g_below, d_gt_below)
        else:
            dx, d_g0, d_sc_m, d_sh_m = _norm_mod_bwd(dh, x0, g0, sc_m, dx1, "norm_mix_b" + tag)
        gr["norm_g"][i] = jnp.concatenate([d_g0, d_g1, d_g2, d_g3], axis=0)
        gr["mod"][i] = jnp.concatenate([d_sh_m, d_sc_m, d_gt_m, d_sh_f, d_sc_f, d_gt_f], axis=0)
    for k in stacked:
        gr[k] = jnp.stack(gr[k])
    return cols, dx, gr


ADA_ROWS = 16


def _ada_fwd(c16, ada_w, ada_b, name):
    depth, d, n = ada_w.shape
    tn = _tile(n, (512, 256, 128))

    def body(c_ref, w_ref, b_ref, o_ref):
        cv = c_ref[...]
        o_ref[0] = _dot_nn(cv * _sigmoid(cv), w_ref[0]) + b_ref[0]

    return pl.pallas_call(
        body, grid=(depth, n // tn),
        in_specs=[pl.BlockSpec((ADA_ROWS, d), lambda l, j: (0, 0)), pl.BlockSpec((1, d, tn), lambda l, j: (l, 0, j)),
                  pl.BlockSpec((1, 1, tn), lambda l, j: (l, 0, j))],
        out_specs=pl.BlockSpec((1, ADA_ROWS, tn), lambda l, j: (l, 0, j)),
        out_shape=jax.ShapeDtypeStruct((depth, ADA_ROWS, n), F32),
        compiler_params=_params("parallel", "parallel"), name=name,
    )(c16, ada_w, ada_b)


def _ada_bwd(c16, dmod16, name):
    depth, _, n = dmod16.shape
    d = c16.shape[1]
    tn = _tile(n, (512, 256, 128))

    def body(c_ref, dm_ref, o_ref):
        cv = c_ref[...]
        o_ref[0] = _dot_tn(cv * _sigmoid(cv), dm_ref[0])

    return pl.pallas_call(
        body, grid=(depth, n // tn),
        in_specs=[pl.BlockSpec((ADA_ROWS, d), lambda l, j: (0, 0)), pl.BlockSpec((1, ADA_ROWS, tn), lambda l, j: (l, 0, j))],
        out_specs=pl.BlockSpec((1, d, tn), lambda l, j: (l, 0, j)),
        out_shape=jax.ShapeDtypeStruct((depth, d, n), F32),
        compiler_params=_params("parallel", "parallel"), name=name,
    )(c16, dmod16)


PACK_COLS = 1024
_ANY = pl.BlockSpec(memory_space=pl.ANY)
_VMEM = pl.BlockSpec(memory_space=pltpu.VMEM)


def _place():
    return lax.axis_index("x"), lax.axis_index("y"), lax.axis_index("c")


def _other_chips(x, y):
    return [(1 - x, y), (x, 1 - y), (1 - x, 1 - y)]


def _rcopy(src, dst, send_sems, recv_sems, k, peer):
    return pltpu.make_async_remote_copy(src_ref=src, dst_ref=dst, send_sem=send_sems.at[k], recv_sem=recv_sems.at[k],
                                        device_id=peer, device_id_type=MESH)


def _all_gather_8(v, name):
    r, cc = v.shape

    def body(v_ref, out_ref, send_sems, recv_sems, local_sem):
        x, y, c = _place()
        me = 4 * x + 2 * y + c
        mine = pltpu.make_async_copy(v_ref, out_ref.at[me], local_sem)
        mine.start()
        peers = []
        for k in range(1, N_DEV):
            px = 1 - x if k & 4 else x
            py = 1 - y if k & 2 else y
            pc = 1 - c if k & 1 else c
            peers.append((px, py, pc))
        sends = [_rcopy(v_ref, out_ref.at[me], send_sems, recv_sems, k, p) for k, p in enumerate(peers)]
        for cp in sends:
            cp.start()
        for k, (px, py, pc) in enumerate(peers):
            _rcopy(v_ref, out_ref.at[4 * px + 2 * py + pc], send_sems, recv_sems, k, (px, py, pc)).wait_recv()
        for cp in sends:
            cp.wait_send()
        mine.wait()

    return pl.pallas_call(
        body, in_specs=[_VMEM], out_specs=_VMEM, out_shape=jax.ShapeDtypeStruct((N_DEV, r, cc), v.dtype),
        scratch_shapes=[pltpu.SemaphoreType.DMA((N_DEV - 1,)), pltpu.SemaphoreType.DMA((N_DEV - 1,)), pltpu.SemaphoreType.DMA],
        compiler_params=pltpu.CompilerParams(vmem_limit_bytes=VMEM_LIMIT), name=name,
    )(v)


def _gather_chips(shards, name):
    n = len(shards)
    per = 2 * (N_CHIP - 1)

    def body(*refs):
        ins, outs, (send_sems, recv_sems) = refs[:n], refs[n:2 * n], refs[2 * n:]
        x, y, c = _place()
        chip = 2 * x + y
        chips = _other_chips(x, y)
        rows = [(pl.ds(c * (r.shape[0] // 2), r.shape[0] // 2), pl.ds((1 - c) * (r.shape[0] // 2), r.shape[0] // 2)) for r in ins]
        first = [_rcopy(ins[i].at[rows[i][0]], outs[i].at[chip, rows[i][0]], send_sems, recv_sems, per * i + j, (px, py, c))
                 for i in range(n) for j, (px, py) in enumerate(chips)]
        for cp in first:
            cp.start()
        passed = []
        for i in range(n):
            for j, (px, py) in enumerate(chips):
                landed = outs[i].at[2 * px + py, rows[i][0]]
                _rcopy(ins[i].at[rows[i][0]], landed, send_sems, recv_sems, per * i + j, (px, py, c)).wait_recv()
                fw = _rcopy(landed, landed, send_sems, recv_sems, per * i + N_CHIP - 1 + j, (x, y, 1 - c))
                fw.start()
                passed.append(fw)
        for i in range(n):
            for j, (px, py) in enumerate(chips):
                landed = outs[i].at[2 * px + py, rows[i][1]]
                _rcopy(landed, landed, send_sems, recv_sems, per * i + N_CHIP - 1 + j, (x, y, 1 - c)).wait_recv()
        for cp in first + passed:
            cp.wait_send()

    return pl.pallas_call(
        body, in_specs=[_ANY] * n, out_specs=[_ANY] * n,
        out_shape=[jax.ShapeDtypeStruct((N_CHIP,) + sh.shape, sh.dtype) for sh in shards],
        scratch_shapes=[pltpu.SemaphoreType.DMA((per * n,)), pltpu.SemaphoreType.DMA((per * n,))], name=name,
    )(*shards)


def _pair_exchange(gs, name):
    n = len(gs)

    def body(*refs):
        ins, outs, (send_sems, recv_sems) = refs[:n], refs[n:2 * n], refs[2 * n:]
        x, y, c = _place()
        copies = []
        for i in range(n):
            half = ins[i].shape[1] // 2
            copies.append(_rcopy(ins[i].at[:, pl.ds((1 - c) * half, half)], outs[i], send_sems, recv_sems, i, (x, y, 1 - c)))
        for cp in copies:
            cp.start()
        for cp in copies:
            cp.wait()

    return pl.pallas_call(
        body, in_specs=[_ANY] * n, out_specs=[_ANY] * n,
        out_shape=[jax.ShapeDtypeStruct((g.shape[0], g.shape[1] // 2, g.shape[2]), g.dtype) for g in gs],
        scratch_shapes=[pltpu.SemaphoreType.DMA((n,)), pltpu.SemaphoreType.DMA((n,))], name=name,
    )(*gs)


_ROW_TILES = (640, 512, 352, 256, 128, 64, 32, 16)


def _pair_sum(g, other, c_idx, name):
    n, half, cc = other.shape
    tr = _tile(half, _ROW_TILES)

    def body(c_ref, g_ref, o_ref, out_ref):
        out_ref[...] = (g_ref[...] + o_ref[...]).astype(out_ref.dtype)

    return pl.pallas_call(
        body,
        grid_spec=pltpu.PrefetchScalarGridSpec(
            num_scalar_prefetch=1, grid=(n, half // tr),
            in_specs=[pl.BlockSpec((None, None, tr, cc), lambda k, i, c_ref: (k, c_ref[0], i, 0)),
                      pl.BlockSpec((None, tr, cc), lambda k, i, c_ref: (k, i, 0))],
            out_specs=pl.BlockSpec((None, tr, cc), lambda k, i, c_ref: (k, i, 0))),
        out_shape=jax.ShapeDtypeStruct((n, half, cc), BF16),
        compiler_params=_params("parallel", "parallel"), name=name,
    )(c_idx, g.reshape(n, 2, half, cc), other)


def _chip_exchange(ps, name):
    n = len(ps)
    per = N_CHIP - 1

    def body(*refs):
        ins, outs, (send_sems, recv_sems) = refs[:n], refs[n:2 * n], refs[2 * n:]
        x, y, c = _place()
        chip = 2 * x + y
        chips = _other_chips(x, y)
        sends = [_rcopy(ins[i].at[2 * px + py], outs[i].at[chip], send_sems, recv_sems, per * i + j, (px, py, c))
                 for i in range(n) for j, (px, py) in enumerate(chips)]
        for cp in sends:
            cp.start()
        for i in range(n):
            for j, (px, py) in enumerate(chips):
                _rcopy(ins[i].at[chip], outs[i].at[2 * px + py], send_sems, recv_sems, per * i + j, (px, py, c)).wait_recv()
        for cp in sends:
            cp.wait_send()

    return pl.pallas_call(
        body, in_specs=[_ANY] * n, out_specs=[_ANY] * n, out_shape=[jax.ShapeDtypeStruct(p.shape, p.dtype) for p in ps],
        scratch_shapes=[pltpu.SemaphoreType.DMA((per * n,)), pltpu.SemaphoreType.DMA((per * n,))], name=name,
    )(*ps)


_HBM = pl.BlockSpec(memory_space=pltpu.HBM)
_SEM = pl.BlockSpec(memory_space=pltpu.SEMAPHORE)
_DATAFLOW = pltpu.SideEffectType.DATAFLOW_SIDE_EFFECTING


def _split_copies(srcs, lands, send_sems, recv_sems, mode, arriving):
    x, y, c = _place()
    chip = 2 * x + y
    out = []
    for i, (src, land) in enumerate(zip(srcs, lands)):
        if mode == "pair":
            half = src.shape[1] // 2
            out.append(_rcopy(src.at[:, pl.ds((1 - c) * half, half)], land, send_sems, recv_sems, i, (x, y, 1 - c)))
            continue
        for j, (px, py) in enumerate(_other_chips(x, y)):
            there = 2 * px + py
            part = src.at[there] if mode == "slots" else src
            out.append(_rcopy(part, land.at[there if arriving else chip], send_sems, recv_sems, (N_CHIP - 1) * i + j, (px, py, c)))
    return out


def _land_shape(src, mode):
    if mode == "pair":
        return (src.shape[0], src.shape[1] // 2, src.shape[2])
    return (N_CHIP,) + (src.shape[1:] if mode == "slots" else src.shape)


def _send_start(srcs, mode, name):
    n = len(srcs)
    n_sem = n if mode == "pair" else (N_CHIP - 1) * n
    lands = [lax.empty(_land_shape(s, mode), s.dtype) for s in srcs]

    def body(*refs):
        ins, zones, (send_sems, recv_sems) = refs[:n], refs[n:2 * n], refs[2 * n:2 * n + 2]
        for cp in _split_copies(ins, zones, send_sems, recv_sems, mode, False):
            cp.start()
        refs[-1][...] = jnp.zeros_like(refs[-1])

    hbm = lambda a: pltpu.HBM(a.shape, a.dtype)
    outs = pl.pallas_call(
        body, name=name, in_specs=[_HBM] * (2 * n),
        out_shape=(pltpu.SemaphoreType.DMA((n_sem,)), pltpu.SemaphoreType.DMA((n_sem,)), *[hbm(a) for a in srcs],
                   *[hbm(a) for a in lands], jax.ShapeDtypeStruct((SUBLANES, LANES), F32)),
        out_specs=(_SEM, _SEM, *[_HBM] * (2 * n), _VMEM), input_output_aliases={i: 2 + i for i in range(2 * n)},
        compiler_params=pltpu.CompilerParams(has_side_effects=_DATAFLOW),
    )(*[pltpu.with_memory_space_constraint(a, pltpu.HBM) for a in list(srcs) + lands])
    return (outs[0], outs[1], list(outs[2:2 + n]), list(outs[2 + n:2 + 2 * n])), outs[-1]


def _send_wait(state, after, mode, name):
    send_sems, recv_sems, srcs, lands = state
    n = len(srcs)

    def body(*refs):
        ins, zones, (send_s, recv_s) = refs[:n], refs[n:2 * n], refs[2 * n:2 * n + 2]
        for cp in _split_copies(ins, zones, send_s, recv_s, mode, True):
            cp.wait_send()
            cp.wait_recv()

    hbm = lambda a: pltpu.HBM(a.shape, a.dtype)
    outs = pl.pallas_call(
        body, name=name, in_specs=[_HBM] * (2 * n) + [_SEM, _SEM, _ANY],
        out_shape=tuple(hbm(a) for a in srcs + lands), out_specs=tuple([_HBM] * (2 * n)),
        input_output_aliases={i: i for i in range(2 * n)},
        compiler_params=pltpu.CompilerParams(has_side_effects=_DATAFLOW),
    )(*srcs, *lands, send_sems, recv_sems, after)
    return list(outs[:n]), list(outs[n:])


def _sum_lead(v, name):
    n, r, cc = v.shape
    tr = _tile(r, _ROW_TILES + (8,))

    def body(v_ref, o_ref):
        acc = v_ref[0].astype(F32)
        for k in range(1, n):
            acc = acc + v_ref[k].astype(F32)
        o_ref[...] = acc

    return pl.pallas_call(
        body, grid=(r // tr,), in_specs=[pl.BlockSpec((n, tr, cc), lambda i: (0, i, 0))],
        out_specs=pl.BlockSpec((tr, cc), lambda i: (i, 0)), out_shape=jax.ShapeDtypeStruct((r, cc), F32),
        compiler_params=_params("parallel"), name=name,
    )(v)


def _chip_sum(arrived, mine, chip_idx, name):
    n, r, cc = arrived.shape
    tr = _tile(r, _ROW_TILES)

    def body(chip_ref, a_ref, m_ref, o_ref):
        acc = jnp.zeros((tr, cc), F32)
        for k in range(n):
            acc = acc + jnp.where(chip_ref[0] == k, m_ref[...], a_ref[k]).astype(F32)
        o_ref[...] = acc

    return pl.pallas_call(
        body,
        grid_spec=pltpu.PrefetchScalarGridSpec(
            num_scalar_prefetch=1, grid=(r // tr,),
            in_specs=[pl.BlockSpec((n, tr, cc), lambda i, chip_ref: (0, i, 0)),
                      pl.BlockSpec((None, tr, cc), lambda i, chip_ref: (chip_ref[0], i, 0))],
            out_specs=pl.BlockSpec((tr, cc), lambda i, chip_ref: (i, 0))),
        out_shape=jax.ShapeDtypeStruct((r, cc), F32), compiler_params=_params("parallel"), name=name,
    )(chip_idx, arrived, mine)


def _pair_share(reds, name):
    n = len(reds)

    def body(*refs):
        ins, outs, (send_sems, recv_sems) = refs[:n], refs[n:2 * n], refs[2 * n:]
        x, y, c = _place()
        copies = [_rcopy(ins[i], outs[i].at[c], send_sems, recv_sems, i, (x, y, 1 - c)) for i in range(n)]
        for cp in copies:
            cp.start()
        for i in range(n):
            _rcopy(ins[i], outs[i].at[1 - c], send_sems, recv_sems, i, (x, y, 1 - c)).wait_recv()
        for cp in copies:
            cp.wait_send()

    return pl.pallas_call(
        body, in_specs=[_ANY] * n, out_specs=[_ANY] * n, out_shape=[jax.ShapeDtypeStruct((2,) + r.shape, r.dtype) for r in reds],
        scratch_shapes=[pltpu.SemaphoreType.DMA((n,)), pltpu.SemaphoreType.DMA((n,))], name=name,
    )(*reds)


def _pack(arrs, rows_multiple, dtype):
    flat = jnp.concatenate([a.reshape(-1).astype(dtype) for a in arrs])
    unit = rows_multiple * PACK_COLS
    total = -(-flat.shape[0] // unit) * unit
    return jnp.pad(flat, (0, total - flat.shape[0])).reshape(-1, PACK_COLS)


def _unpack(buf, shapes):
    lead = buf.shape[:-2]
    flat = buf.reshape(*lead, -1)
    out, off = [], 0
    for shp in shapes:
        n = 1
        for s in shp:
            n *= s
        out.append(flat[..., off:off + n].reshape(*lead, *shp))
        off += n
    return out


def _join_shards(parts, axis):
    moved = jnp.moveaxis(parts, 0, axis)
    shp = list(moved.shape)
    shp[axis:axis + 2] = [shp[axis] * shp[axis + 1]]
    return moved.reshape(shp)


def _my_shard(full, axis, chip):
    n = full.shape[axis] // N_CHIP
    return lax.dynamic_slice_in_dim(full, chip * n, n, axis)


SMALL = {"norm_g": 2, "ffn_conv_w": 2, "rg_conv_w": 2, "gla_w_alpha": 2, "gla_b_alpha": 1, "gla_norm_g": 1,
         "ada_b": None, "ffn_conv_b": None, "rg_conv_b": None, "rg_ba": None, "rg_bx": None, "rg_lambda": None}
BIG = {"rg_w_in": True, "rg_wa": False, "rg_wx": False, "rg_w_out": False, "ffn_w_up": True, "ffn_w_down": False,
       "gla_w_in": True, "gla_w_out": False}
WEIGHTS = ["ada_w", "ada_b", "norm_g", "ffn_w_up", "ffn_conv_w", "ffn_conv_b", "ffn_w_down", "rg_w_in", "rg_conv_w", "rg_conv_b",
           "rg_wa", "rg_ba", "rg_wx", "rg_bx", "rg_lambda", "rg_w_out", "gla_w_in", "gla_w_alpha", "gla_b_alpha", "gla_norm_g",
           "gla_w_out"]


def kernel(x, c, ada_w, ada_b, norm_g, ffn_w_up, ffn_conv_w, ffn_conv_b, ffn_w_down, rg_w_in, rg_conv_w, rg_conv_b, rg_wa, rg_ba, rg_wx, rg_bx, rg_lambda, rg_w_out, gla_w_in, gla_w_alpha, gla_b_alpha, gla_norm_g, gla_w_out, loss_target, m_ada_w, m_ada_b, m_norm_g, m_ffn_w_up, m_ffn_conv_w, m_ffn_conv_b, m_ffn_w_down, m_rg_w_in, m_rg_conv_w, m_rg_conv_b, m_rg_wa, m_rg_ba, m_rg_wx, m_rg_bx, m_rg_lambda, m_rg_w_out, m_gla_w_in, m_gla_w_alpha, m_gla_b_alpha, m_gla_norm_g, m_gla_w_out, v_ada_w, v_ada_b, v_norm_g, v_ffn_w_up, v_ffn_conv_w, v_ffn_conv_b, v_ffn_w_down, v_rg_w_in, v_rg_conv_w, v_rg_conv_b, v_rg_wa, v_rg_ba, v_rg_wx, v_rg_bx, v_rg_lambda, v_rg_w_out, v_gla_w_in, v_gla_w_alpha, v_gla_b_alpha, v_gla_norm_g, v_gla_w_out):
    wts = dict(ada_w=ada_w, ada_b=ada_b, norm_g=norm_g, ffn_w_up=ffn_w_up, ffn_conv_w=ffn_conv_w, ffn_conv_b=ffn_conv_b,
               ffn_w_down=ffn_w_down, rg_w_in=rg_w_in, rg_conv_w=rg_conv_w, rg_conv_b=rg_conv_b, rg_wa=rg_wa, rg_ba=rg_ba,
               rg_wx=rg_wx, rg_bx=rg_bx, rg_lambda=rg_lambda, rg_w_out=rg_w_out, gla_w_in=gla_w_in, gla_w_alpha=gla_w_alpha,
               gla_b_alpha=gla_b_alpha, gla_norm_g=gla_norm_g, gla_w_out=gla_w_out)
    mom1 = dict(ada_w=m_ada_w, ada_b=m_ada_b, norm_g=m_norm_g, ffn_w_up=m_ffn_w_up, ffn_conv_w=m_ffn_conv_w,
                ffn_conv_b=m_ffn_conv_b, ffn_w_down=m_ffn_w_down, rg_w_in=m_rg_w_in, rg_conv_w=m_rg_conv_w,
                rg_conv_b=m_rg_conv_b, rg_wa=m_rg_wa, rg_ba=m_rg_ba, rg_wx=m_rg_wx, rg_bx=m_rg_bx, rg_lambda=m_rg_lambda,
                rg_w_out=m_rg_w_out, gla_w_in=m_gla_w_in, gla_w_alpha=m_gla_w_alpha, gla_b_alpha=m_gla_b_alpha,
                gla_norm_g=m_gla_norm_g, gla_w_out=m_gla_w_out)
    mom2 = dict(ada_w=v_ada_w, ada_b=v_ada_b, norm_g=v_norm_g, ffn_w_up=v_ffn_w_up, ffn_conv_w=v_ffn_conv_w,
                ffn_conv_b=v_ffn_conv_b, ffn_w_down=v_ffn_w_down, rg_w_in=v_rg_w_in, rg_conv_w=v_rg_conv_w,
                rg_conv_b=v_rg_conv_b, rg_wa=v_rg_wa, rg_ba=v_rg_ba, rg_wx=v_rg_wx, rg_bx=v_rg_bx, rg_lambda=v_rg_lambda,
                rg_w_out=v_rg_w_out, gla_w_in=v_gla_w_in, gla_w_alpha=v_gla_w_alpha, gla_b_alpha=v_gla_b_alpha,
                gla_norm_g=v_gla_norm_g, gla_w_out=v_gla_w_out)
    xi, yi, ci = _place()
    chip, me = 2 * xi + yi, 4 * xi + 2 * yi + ci
    d = x.shape[-1]
    depth = ada_w.shape[0]
    n_ada = ada_w.shape[-1]
    sharded_small = [k for k, ax in SMALL.items() if ax is not None]

    sm = _all_gather_8(_pack([c] + [wts[k] for k in sharded_small], SUBLANES, F32), "gather_small")
    c_all = sm[:, 0, :]
    parts = _unpack(sm[0::2], [c.shape] + [wts[k].shape for k in sharded_small])[1:]
    full = {k: _join_shards(p, SMALL[k]) for k, p in zip(sharded_small, parts)}
    for k, ax in SMALL.items():
        if ax is None:
            full[k] = wts[k]

    c16 = jnp.pad(c_all, ((0, ADA_ROWS - N_DEV), (0, 0)))
    ada_b_mine = lax.dynamic_slice_in_dim(ada_b, chip * n_ada, n_ada, 1)[:, None, :]
    mod_cols = _ada_fwd(c16, ada_w, ada_b_mine, "ada_fwd")
    mod_all = _all_gather_8(mod_cols.reshape(-1, PACK_COLS), "gather_mod")[0::2].reshape(N_CHIP, depth, ADA_ROWS, n_ada)
    mod = jnp.swapaxes(lax.dynamic_index_in_dim(mod_all, me, 2, keepdims=False), 0, 1).reshape(depth, 6, d)

    items = [(k, l) for k in BIG for l in range(wts[k].shape[0])]
    stage_of = lambda k, l: "rg" if k.startswith("rg_") else ("ffn0" if (k.startswith("ffn_") and l == 0) else "l1")
    staged = {st: [it for it in items if stage_of(*it) == st] for st in ("rg", "ffn0", "l1")}
    staged["l1"].sort(key=lambda it: not it[0].startswith("gla_"))
    shard = lambda k, l: wts[k][l].reshape(-1, wts[k].shape[-1]).astype(BF16)
    own = lambda got, mine: [lax.dynamic_update_index_in_dim(g, m, chip, 0) for g, m in zip(got, mine)]
    rows_joined = lambda v: v.reshape(-1, v.shape[-1])

    def placed(its, slots):
        out = {"ffn_w_up": {}, "ffn_w_down": {}}
        for (k, l), v in zip(its, slots):
            if k == "ffn_w_up":
                out[k][l] = v
            elif k == "ffn_w_down":
                out[k][l] = rows_joined(v)
            elif k in ("rg_wa", "rg_wx"):
                out[k] = _slots_to_block_rows(v, RG_BLOCKS)
            elif k == "gla_w_in":
                out[k] = _from_col_slots(v)
            else:
                out[k] = v if BIG[k] else rows_joined(v)
        return out

    after_mod = (mod[0, 0, 0] * 0.0).astype(BF16)
    sh_rg = [shard(k, l) + after_mod for k, l in staged["rg"]]
    local = {k: (v if k in ("norm_g", "ffn_conv_w", "ffn_conv_b") else v[0]) for k, v in full.items()}
    local.update(placed(staged["rg"], own(_gather_chips(sh_rg, "gather_weights_rg"), sh_rg)))
    sh_late, flying = {}, {}
    after_rg = (local["rg_w_out"][0, 0].astype(F32) * 0.0).astype(BF16)
    sh_late["ffn0"] = [shard(k, l) + after_rg for k, l in staged["ffn0"]]
    flying["ffn0"], tok = _send_start(sh_late["ffn0"], "whole", "weights_ffn0_start")
    sh_late["l1"] = [shard(k, l) + tok[0, 0].astype(BF16) for k, l in staged["l1"]]
    flying["l1"], tok2 = _send_start(sh_late["l1"], "whole", "weights_l1_start")
    mod = mod + (tok[0, 0] + tok2[0, 0])

    def fetch(stage, after):
        mine, got = _send_wait(flying[stage], after, "whole", f"weights_{stage}_wait")
        return placed(staged[stage], own(got, mine))

    c_idx = ci.reshape(1).astype(jnp.int32)
    gslots, paired, psums, sent = {}, {}, {}, {}

    def grad_slots(gr, k, l):
        g = gr[k][l] if k in ("ffn_w_up", "ffn_w_down") else gr[k]
        if k in ("rg_wa", "rg_wx"):
            return _block_rows_to_slots(g)
        if k == "gla_w_in":
            return _col_slots(g)
        return g if BIG[k] else g.reshape(N_CHIP, -1, g.shape[-1])

    def done(stage, gr):
        gslots[stage] = [grad_slots(gr, k, l) for k, l in staged[stage]]
        paired[stage], token = _send_start(gslots[stage], "pair", f"grads_{stage}_pair_start")
        return token[0, 0]

    def later(stage, after):
        mine, theirs = _send_wait(paired[stage], after, "pair", f"grads_{stage}_pair_wait")
        psums[stage] = [_pair_sum(g, t, c_idx, f"grads_pair_sum_{k}{l}") for (k, l), g, t in zip(staged[stage], mine, theirs)]
        sent[stage], token = _send_start(psums[stage], "slots", f"grads_{stage}_start")
        return token[0, 0]

    cols, grad_x, gr = _local_step(x[0], loss_target[0], mod, local, fetch, done, later)
    loss = lax.psum(0.5 * jnp.sum(cols) / d, ("x", "y", "c"))

    small_names = [k for k in SMALL if k != "ada_b"]
    gs = _all_gather_8(_pack([gr[k] for k in small_names] + [gr["mod"]], SUBLANES, F32), "gather_small_grads")
    small_shapes = [full[k].shape for k in small_names] + [(depth, 6 * d)]
    *small_sum, g_ada_b = _unpack(_sum_lead(gs, "sum_small_grads"), small_shapes)
    grads = dict(zip(small_names, small_sum))
    grads["ada_b"] = g_ada_b
    for k in sharded_small:
        grads[k] = _my_shard(grads[k], SMALL[k], chip)
    dmod_all = _unpack(gs, small_shapes)[-1].reshape(N_DEV, depth, N_CHIP, n_ada)
    dmod_mine = jnp.swapaxes(lax.dynamic_index_in_dim(dmod_all, chip, 2, keepdims=False), 0, 1)
    g_ada_w = _ada_bwd(c16, jnp.pad(dmod_mine, ((0, 0), (0, ADA_ROWS - N_DEV), (0, 0))), "ada_bwd")

    gslots["rg"] = [grad_slots(gr, k, l) for k, l in staged["rg"]]
    theirs = _pair_exchange(gslots["rg"], "grads_rg_pair_exchange")
    psums["rg"] = [_pair_sum(g, t, c_idx, f"grads_pair_sum_{k}{l}") for (k, l), g, t in zip(staged["rg"], gslots["rg"], theirs)]
    sent["rg"], rg_sent = _send_start(psums["rg"], "slots", "grads_rg_start")
    chip_idx = chip.reshape(1).astype(jnp.int32)
    delta, new_m, new_v = {}, {}, {}

    def reduce_and_update(stages, after, dep):
        its = [(st, n) for st in stages for n in range(len(staged[st]))]
        back = {st: _send_wait(sent[st], after, "slots", f"grads_{st}_wait") for st in stages}
        halves = [_chip_sum(back[st][1][n], back[st][0][n], chip_idx, "grads_chip_sum_%s%d" % staged[st][n]) for st, n in its]
        shared = _pair_share(halves, "grads_pair_share_" + stages[0])
        reduced = [lax.dynamic_update_index_in_dim(s2, h, ci, 0).reshape(-1, h.shape[-1]) for s2, h in zip(shared, halves)]
        last = None
        for k in BIG:
            gs_k = [g for (st, n), g in zip(its, reduced) if staged[st][n][0] == k]
            if gs_k:
                last = update(k, gs_k, dep)
        return last

    def update(k, gs_k, dep=None):
        shp = wts[k].shape
        if k == "gla_w_in":
            view, back = (lambda a: jnp.swapaxes(a, 1, 2)), (lambda o: jnp.swapaxes(o, 1, 2))
            gs_k = [g.T for g in gs_k]
        else:
            view, back = (lambda a: a.reshape(a.shape[0], -1, a.shape[-1])), (lambda o: o.reshape(shp))
        outs = _adamw(view(wts[k]), gs_k, view(mom1[k]), view(mom2[k]), "adamw_" + k, dep)
        grads[k], delta[k], new_m[k], new_v[k] = (back(o) for o in outs)
        return new_v[k]

    done_late = reduce_and_update(("ffn0", "l1"), grad_x, rg_sent)
    update("ada_w", g_ada_w, rg_sent)
    small_shard_shapes = [wts[k].shape for k in SMALL]
    packed = [_pack([src[k] for k in SMALL], SUBLANES, F32) for src in (wts, grads, mom1, mom2)]
    outs = _adamw(packed[0][None], [packed[1]], packed[2][None], packed[3][None], "adamw_small", rg_sent)
    for dst, o in zip((delta, new_m, new_v), outs[1:]):
        for k, a in zip(SMALL, _unpack(o[0], small_shard_shapes)):
            dst[k] = a
    reduce_and_update(("rg",), done_late, None)

    return (loss, grad_x[None], *[grads[k] for k in WEIGHTS], *[delta[k] for k in WEIGHTS], *[new_m[k] for k in WEIGHTS],
            *[new_v[k] for k in WEIGHTS])
```

```python
import jax
import jax.numpy as jnp
from jax import lax
from jax.experimental import pallas as pl
from jax.experimental.pallas import tpu as pltpu

F32 = jnp.float32
BF16 = jnp.bfloat16
MXU_DTYPE = BF16

EPS = 1e-6
RG_C = 8.0
RG_BLOCKS = 4
RG_CONV = 4
GLA_HEADS = 4
GLA_TAU = 16.0
GLA_CHUNK = 64
GLA_RANK = 16
FFN_CONV = 3
ADAM_LR = 0.001
ADAM_B1 = 0.9
ADAM_B2 = 0.999
ADAM_EPS = 1e-08
ADAM_WD = 0.01
ADAM_STEP = 10

LANES = 128
SUBLANES = 8
VMEM_LIMIT = 56 * 1024 * 1024
CB = 256
MESH = pl.DeviceIdType.MESH
N_DEV = 8
N_CHIP = 4


def _params(*sem):
    return pltpu.CompilerParams(dimension_semantics=sem, vmem_limit_bytes=VMEM_LIMIT)


def _tile(dim, prefs):
    for p in prefs:
        if dim % p == 0:
            return p
    return dim


def _dot(a, b, dims):
    return lax.dot_general(a.astype(MXU_DTYPE), b.astype(MXU_DTYPE), (dims, ((), ())), preferred_element_type=F32)


def _dot_nn(a, b):
    return _dot(a, b, ((1,), (0,)))


def _dot_nt(a, b):
    return _dot(a, b, ((1,), (1,)))


def _dot_tn(a, b):
    return _dot(a, b, ((0,), (0,)))


def _mm(a, b, *, ta=False, tb=False, a_parts=1, b_parts=1, w_slots=1, out_slots=1, out_dtype=F32, tm_max=1408, name):
    if ta:
        k_dim, m_dim = a.shape
        n_dim = b.shape[-1] * b_parts
    else:
        m_dim, k_dim = a.shape[-2], a.shape[-1] * a_parts
        n_dim = b.shape[-2] if tb else b.shape[-1] * w_slots
    n_unit = n_dim // max(b_parts, out_slots, 1 if tb else w_slots)
    k_unit = k_dim // max(a_parts, w_slots if tb else 1)
    tm = _tile(m_dim, tuple(t for t in (1024, 1408, 512, 256, 128) if t <= max(tm_max, 128)))
    tn = _tile(n_unit, (1024, 1408, 896, 512, 256, 128))
    tk = _tile(k_unit, (1024, 1408, 896, 512, 256, 128))
    nk = k_dim // tk
    dims = ((0 if ta else 1,), (1 if tb else 0,))

    def spec(shape, parts, total, tile, col_grid, row_grid):
        per = total // parts // tile

        def index(i, j, k):
            g = {"i": i, "j": j, "k": k}
            col, row = g[col_grid], g[row_grid]
            return (row, col) if parts == 1 else (col // per, row, col % per)

        return pl.BlockSpec(shape if parts == 1 else (None,) + shape, index)

    def body(a_ref, b_ref, o_ref, *acc):
        if nk == 1:
            o_ref[...] = _dot(a_ref[...], b_ref[...], dims).astype(o_ref.dtype)
            return
        acc_ref, k = acc[0], pl.program_id(2)

        @pl.when(k == 0)
        def _():
            acc_ref[...] = jnp.zeros_like(acc_ref)

        acc_ref[...] += _dot(a_ref[...], b_ref[...], dims)

        @pl.when(k == nk - 1)
        def _():
            o_ref[...] = acc_ref[...].astype(o_ref.dtype)

    if ta:
        a_spec = spec((tk, tm), 1, m_dim, tm, "i", "k")
        b_spec = spec((tk, tn), b_parts, n_dim, tn, "j", "k")
    elif tb:
        a_spec = spec((tm, tk), a_parts, k_dim, tk, "k", "i")
        b_spec = spec((tn, tk), w_slots, k_dim, tk, "k", "j")
    else:
        a_spec = spec((tm, tk), a_parts, k_dim, tk, "k", "i")
        b_spec = spec((tk, tn), w_slots, n_dim, tn, "j", "k")
    out_shape = (m_dim, n_dim) if out_slots == 1 else (out_slots, m_dim, n_dim // out_slots)
    return pl.pallas_call(
        body,
        grid=(m_dim // tm, n_dim // tn, nk),
        in_specs=[a_spec, b_spec],
        out_specs=spec((tm, tn), out_slots, n_dim, tn, "j", "i"),
        out_shape=jax.ShapeDtypeStruct(out_shape, out_dtype),
        scratch_shapes=[pltpu.VMEM((tm, tn), F32)] if nk > 1 else [],
        compiler_params=_params("parallel", "parallel", "arbitrary"),
        name=name,
    )(a, b)


ROW_TILES = (1024, 512)


def _row_specs(s, d, ts):
    return pl.BlockSpec((ts, d), lambda i: (i, 0)), pl.BlockSpec((1, d), lambda i: (0, 0))


def _norm_mod_fwd(x, g, sc, sh, name):
    s, d = x.shape
    ts = _tile(s, ROW_TILES)

    def body(x_ref, g_ref, sc_ref, sh_ref, h_ref):
        xv = x_ref[...]
        r = lax.rsqrt(jnp.mean(xv * xv, axis=-1, keepdims=True) + EPS)
        h_ref[...] = (((xv * r) * g_ref[...]) * (1.0 + sc_ref[...]) + sh_ref[...]).astype(h_ref.dtype)

    row, vec = _row_specs(s, d, ts)
    return pl.pallas_call(
        body, grid=(s // ts,), in_specs=[row, vec, vec, vec], out_specs=row,
        out_shape=jax.ShapeDtypeStruct((s, d), MXU_DTYPE), compiler_params=_params("parallel"), name=name,
    )(x, g, sc, sh)


def _norm_mod_bwd(dh, x, g, sc, dres, name):
    s, d = x.shape
    ts = _tile(s, ROW_TILES)

    def body(dh_ref, x_ref, g_ref, sc_ref, dres_ref, dx_ref, dg_ref, dsc_ref, dsh_ref, acc_ref):
        i = pl.program_id(0)

        @pl.when(i == 0)
        def _():
            acc_ref[...] = jnp.zeros_like(acc_ref)

        xv, dhv = x_ref[...], dh_ref[...]
        r = lax.rsqrt(jnp.mean(xv * xv, axis=-1, keepdims=True) + EPS)
        n = xv * r
        acc_ref[0:1, :] += jnp.sum(dhv * n, axis=0, keepdims=True)
        acc_ref[1:2, :] += jnp.sum(dhv, axis=0, keepdims=True)
        dn = dhv * ((1.0 + sc_ref[...]) * g_ref[...])
        dx_ref[...] = dres_ref[...] + r * (dn - n * jnp.mean(dn * n, axis=-1, keepdims=True))
        dg_ref[...] = (1.0 + sc_ref[...]) * acc_ref[0:1, :]
        dsc_ref[...] = g_ref[...] * acc_ref[0:1, :]
        dsh_ref[...] = acc_ref[1:2, :]

    row, vec = _row_specs(s, d, ts)
    vshape = jax.ShapeDtypeStruct((1, d), F32)
    return pl.pallas_call(
        body, grid=(s // ts,), in_specs=[row, row, vec, vec, row], out_specs=[row, vec, vec, vec],
        out_shape=[jax.ShapeDtypeStruct((s, d), F32), vshape, vshape, vshape],
        scratch_shapes=[pltpu.VMEM((SUBLANES, d), F32)], compiler_params=_params("arbitrary"), name=name,
    )(dh, x, g, sc, dres)


def _post_norm_fwd(x, y, g, gt, g2, sc, sh, name):
    s, d = x.shape
    ts = _tile(s, ROW_TILES)

    def body(x_ref, y_ref, g_ref, gt_ref, g2_ref, sc_ref, sh_ref, o_ref, h_ref):
        yv = y_ref[...]
        r = lax.rsqrt(jnp.mean(yv * yv, axis=-1, keepdims=True) + EPS)
        xn = x_ref[...] + gt_ref[...] * ((yv * r) * g_ref[...])
        o_ref[...] = xn
        r2 = lax.rsqrt(jnp.mean(xn * xn, axis=-1, keepdims=True) + EPS)
        h_ref[...] = (((xn * r2) * g2_ref[...]) * (1.0 + sc_ref[...]) + sh_ref[...]).astype(h_ref.dtype)

    row, vec = _row_specs(s, d, ts)
    return pl.pallas_call(
        body, grid=(s // ts,), in_specs=[row, row] + [vec] * 5, out_specs=[row, row],
        out_shape=[jax.ShapeDtypeStruct((s, d), F32), jax.ShapeDtypeStruct((s, d), MXU_DTYPE)],
        compiler_params=_params("parallel"), name=name,
    )(x, y, g, gt, g2, sc, sh)


def _post_bwd(dxn, y, g, gt, name):
    s, d = y.shape
    ts = _tile(s, ROW_TILES)

    def body(dxn_ref, y_ref, g_ref, gt_ref, dy_ref, dg_ref, dgt_ref, acc_ref):
        i = pl.program_id(0)

        @pl.when(i == 0)
        def _():
            acc_ref[...] = jnp.zeros_like(acc_ref)

        yv, dv = y_ref[...], dxn_ref[...]
        r = lax.rsqrt(jnp.mean(yv * yv, axis=-1, keepdims=True) + EPS)
        n = yv * r
        acc_ref[0:1, :] += jnp.sum(dv * n, axis=0, keepdims=True)
        dn = dv * (gt_ref[...] * g_ref[...])
        dy_ref[...] = (r * (dn - n * jnp.mean(dn * n, axis=-1, keepdims=True))).astype(dy_ref.dtype)
        dg_ref[...] = gt_ref[...] * acc_ref[0:1, :]
        dgt_ref[...] = g_ref[...] * acc_ref[0:1, :]

    row, vec = _row_specs(s, d, ts)
    vshape = jax.ShapeDtypeStruct((1, d), F32)
    return pl.pallas_call(
        body, grid=(s // ts,), in_specs=[row, row, vec, vec], out_specs=[row, vec, vec],
        out_shape=[jax.ShapeDtypeStruct((s, d), MXU_DTYPE), vshape, vshape],
        scratch_shapes=[pltpu.VMEM((SUBLANES, d), F32)], compiler_params=_params("arbitrary"), name=name,
    )(dxn, y, g, gt)


def _norm_post_bwd(dh, x, g, sc, dres, y, gp, gt, name):
    s, d = x.shape
    ts = _tile(s, (512,))

    def body(dh_ref, x_ref, g_ref, sc_ref, dres_ref, y_ref, gp_ref, gt_ref,
             dx_ref, dy_ref, dg_ref, dsc_ref, dsh_ref, dgp_ref, dgt_ref, acc_ref):
        i = pl.program_id(0)

        @pl.when(i == 0)
        def _():
            acc_ref[...] = jnp.zeros_like(acc_ref)

        xv, dhv = x_ref[...], dh_ref[...]
        r = lax.rsqrt(jnp.mean(xv * xv, axis=-1, keepdims=True) + EPS)
        n = xv * r
        acc_ref[0:1, :] += jnp.sum(dhv * n, axis=0, keepdims=True)
        acc_ref[1:2, :] += jnp.sum(dhv, axis=0, keepdims=True)
        dn = dhv * ((1.0 + sc_ref[...]) * g_ref[...])
        dx = dres_ref[...] + r * (dn - n * jnp.mean(dn * n, axis=-1, keepdims=True))
        dx_ref[...] = dx
        yv = y_ref[...]
        ry = lax.rsqrt(jnp.mean(yv * yv, axis=-1, keepdims=True) + EPS)
        ny = yv * ry
        acc_ref[2:3, :] += jnp.sum(dx * ny, axis=0, keepdims=True)
        dny = dx * (gt_ref[...] * gp_ref[...])
        dy_ref[...] = (ry * (dny - ny * jnp.mean(dny * ny, axis=-1, keepdims=True))).astype(dy_ref.dtype)
        dg_ref[...] = (1.0 + sc_ref[...]) * acc_ref[0:1, :]
        dsc_ref[...] = g_ref[...] * acc_ref[0:1, :]
        dsh_ref[...] = acc_ref[1:2, :]
        dgp_ref[...] = gt_ref[...] * acc_ref[2:3, :]
        dgt_ref[...] = gp_ref[...] * acc_ref[2:3, :]

    row, vec = _row_specs(s, d, ts)
    vshape = jax.ShapeDtypeStruct((1, d), F32)
    return pl.pallas_call(
        body, grid=(s // ts,), in_specs=[row, row, vec, vec, row, row, vec, vec], out_specs=[row, row] + [vec] * 5,
        out_shape=[jax.ShapeDtypeStruct((s, d), F32), jax.ShapeDtypeStruct((s, d), MXU_DTYPE)] + [vshape] * 5,
        scratch_shapes=[pltpu.VMEM((SUBLANES, d), F32)], compiler_params=_params("arbitrary"), name=name,
    )(dh, x, g, sc, dres, y, gp, gt)


def _post_loss(x, y, g, gt, tgt, name):
    s, d = x.shape
    ts = _tile(s, ROW_TILES)

    def body(x_ref, y_ref, g_ref, gt_ref, t_ref, col_ref, dx_ref):
        i = pl.program_id(0)

        @pl.when(i == 0)
        def _():
            col_ref[...] = jnp.zeros_like(col_ref)

        yv = y_ref[...]
        r = lax.rsqrt(jnp.mean(yv * yv, axis=-1, keepdims=True) + EPS)
        e = (x_ref[...] + gt_ref[...] * ((yv * r) * g_ref[...])) - t_ref[...]
        col_ref[...] += jnp.sum(e * e, axis=0, keepdims=True)
        dx_ref[...] = e * (1.0 / d)

    row, vec = _row_specs(s, d, ts)
    return pl.pallas_call(
        body, grid=(s // ts,), in_specs=[row, row, vec, vec, row], out_specs=[vec, row],
        out_shape=[jax.ShapeDtypeStruct((1, d), F32), jax.ShapeDtypeStruct((s, d), F32)],
        compiler_params=_params("arbitrary"), name=name,
    )(x, y, g, gt, tgt)


_GELU_C = 0.7978845608028654
_GELU_A = 0.044715


def _gelu(x):
    t = jnp.tanh(_GELU_C * (x + _GELU_A * x * x * x))
    return 0.5 * x * (1.0 + t), t


def _gelu_grad(x, t):
    return 0.5 * (1.0 + t) + 0.5 * x * (1.0 - t * t) * (_GELU_C * (1.0 + 3.0 * _GELU_A * x * x))


def _sigmoid(x):
    return 1.0 / (1.0 + jnp.exp(-x))


def _log1p_pos(y):
    u = 1.0 + y
    return jnp.where(u == 1.0, y, jnp.log(u) * (y / jnp.where(u == 1.0, 1.0, u - 1.0)))


def _softplus(x):
    return jnp.maximum(x, 0.0) + _log1p_pos(jnp.exp(-jnp.abs(x)))


def _one_minus_sq_exp(x, ex):
    z = 2.0 * x
    series = -z * (1.0 + z * (1.0 / 2 + z * (1.0 / 6 + z * (1.0 / 24 + z * (1.0 / 120)))))
    return jnp.where(z > -0.05, series, 1.0 - ex * ex)


SLAB = 16


def _cat(a, b):
    return jnp.concatenate([a, b], axis=1)


def _pair_specs(shape, nb, index):
    return [pl.BlockSpec(shape, lambda j, t: index(j, t) + (j,)), pl.BlockSpec(shape, lambda j, t: index(j, t) + (j + nb,))]


def _halo_row(ts, time_of):
    return lambda j, t: (jnp.maximum(time_of(t) * (ts // SUBLANES) - 1, 0),)


def _rows_from(groups, k):
    row = lax.broadcasted_iota(jnp.int32, groups[0].shape, 0)
    turned = [pltpu.roll(g, SUBLANES - k, axis=0) for g in groups]
    return [jnp.where(row < SUBLANES - k, lo, hi) for lo, hi in zip(turned[:-1], turned[1:])]


def _ffn_mid_fwd(p, cw, cb, name):
    s, f2 = p.shape
    ts = _tile(s, (1024, 512))
    nb, nt = f2 // (2 * CB), s // ts
    n_grp = SLAB // SUBLANES

    def body(pg_ref, pv_ref, hg_ref, hv_ref, cwg_ref, cwv_ref, cbg_ref, cbv_ref, a_ref, ga_ref, gb_ref):
        t = pl.program_id(1)
        cwv, bias = _cat(cwg_ref[...], cwv_ref[...]), _cat(cbg_ref[...], cbv_ref[...])
        w0, w1, w2 = cwv[0:1], cwv[1:2], cwv[2:3]

        def slab(before, cur, r0):
            pm2, pm1 = _rows_from([before] + cur, SUBLANES - 2), _rows_from([before] + cur, SUBLANES - 1)
            u = jnp.concatenate([bias + w0 * pm2[i] + w1 * pm1[i] + w2 * cur[i] for i in range(n_grp)], axis=0)
            g, v = u[:, :CB], u[:, CB:]
            gel, th = _gelu(g)
            rows = pl.ds(r0, SLAB)
            a_ref[rows, :] = (gel * v).astype(a_ref.dtype)
            ga_ref[rows, :] = gel.astype(ga_ref.dtype)
            gb_ref[rows, :] = (v * _gelu_grad(g, th)).astype(gb_ref.dtype)

        def pieces(rows):
            blk = _cat(pg_ref[rows, :], pv_ref[rows, :])
            return [blk[i * SUBLANES:(i + 1) * SUBLANES] for i in range(blk.shape[0] // SUBLANES)]

        slab(jnp.where(t > 0, _cat(hg_ref[...], hv_ref[...]), 0.0), pieces(pl.ds(0, SLAB)), 0)

        def loop(i, carry):
            r0 = pl.multiple_of(i * SLAB, SLAB)
            got = pieces(pl.ds(pl.multiple_of(r0 - SUBLANES, SUBLANES), SLAB + SUBLANES))
            slab(got[0], got[1:], r0)
            return carry

        lax.fori_loop(1, ts // SLAB, loop, 0, unroll=2)

    fwd = lambda t: t
    out = pl.BlockSpec((ts, CB), lambda j, t: (t, j))
    shape = jax.ShapeDtypeStruct((s, f2 // 2), MXU_DTYPE)
    return pl.pallas_call(
        body, grid=(nb, nt),
        in_specs=(_pair_specs((ts, CB), nb, lambda j, t: (t,)) + _pair_specs((SUBLANES, CB), nb, _halo_row(ts, fwd))
                  + _pair_specs((FFN_CONV, CB), nb, lambda j, t: (0,)) + _pair_specs((1, CB), nb, lambda j, t: (0,))),
        out_specs=[out, out, out], out_shape=[shape, shape, shape],
        compiler_params=_params("parallel", "arbitrary"), name=name,
    )(p, p, p, p, cw, cw, cb, cb)


def _ffn_mid_bwd(da, p, ga, gb, cw, name):
    s, f2 = p.shape
    ts = _tile(s, (1024, 512))
    nb, nt = f2 // (2 * CB), s // ts
    n_slab = ts // SLAB
    n_grp = SLAB // SUBLANES
    per_trip = 2

    def body(da_ref, ga_ref, gb_ref, pg_ref, pv_ref, cwg_ref, cwv_ref, dp_ref, dcw_ref, dcb_ref, next_du, acc):
        tt = pl.program_id(1)
        cwv = _cat(cwg_ref[...], cwv_ref[...])
        w0, w1, w2 = cwv[0:1], cwv[1:2], cwv[2:3]

        @pl.when(tt == 0)
        def _():
            next_du[...] = jnp.zeros_like(next_du)
            acc[...] = jnp.zeros_like(acc)

        def slab(r0, after, sums):
            rows = pl.ds(r0, SLAB)
            dav = da_ref[rows, :]
            du = _cat(dav * gb_ref[rows, :].astype(F32), dav * ga_ref[rows, :].astype(F32))
            p0 = _cat(pg_ref[rows, :], pv_ref[rows, :])
            cur = [du[i * SUBLANES:(i + 1) * SUBLANES] for i in range(n_grp)]
            du1, du2 = _rows_from(cur + [after], 1), _rows_from(cur + [after], 2)
            dpv = jnp.concatenate([w2 * cur[i] + w1 * du1[i] + w0 * du2[i] for i in range(n_grp)], axis=0).astype(dp_ref.dtype)
            dp_ref[0, rows, :] = dpv[:, :CB]
            dp_ref[1, rows, :] = dpv[:, CB:]
            for i in range(n_grp):
                pi = p0[i * SUBLANES:(i + 1) * SUBLANES]
                parts = (cur[i], du2[i] * pi, du1[i] * pi, cur[i] * pi)
                sums = parts if sums is None else tuple(x + y for x, y in zip(sums, parts))
            return cur[0], sums

        def loop(k, after):
            sums = None
            for j in range(per_trip):
                r0 = pl.multiple_of((n_slab - 1 - (k * per_trip + j)) * SLAB, SLAB)
                after, sums = slab(r0, after, sums)
            for q, part in enumerate(sums):
                acc[q] += part
            return after

        next_du[...] = lax.fori_loop(0, n_slab // per_trip, loop, next_du[...])

        @pl.when(tt == nt - 1)
        def _():
            for half in range(2):
                cols = slice(half * CB, (half + 1) * CB)
                dcb_ref[half] = jnp.sum(acc[0][:, cols], axis=0, keepdims=True)
                for k in range(FFN_CONV):
                    dcw_ref[half, k:k + 1, :] = jnp.sum(acc[1 + k][:, cols], axis=0, keepdims=True)

    rev = lambda t: nt - 1 - t
    tile = pl.BlockSpec((ts, CB), lambda j, t: (rev(t), j))
    return pl.pallas_call(
        body, grid=(nb, nt),
        in_specs=([tile, tile, tile] + _pair_specs((ts, CB), nb, lambda j, t: (rev(t),))
                  + _pair_specs((FFN_CONV, CB), nb, lambda j, t: (0,))),
        out_specs=[pl.BlockSpec((2, ts, CB), lambda j, t: (0, rev(t), j)),
                   pl.BlockSpec((2, FFN_CONV, CB), lambda j, t: (0, 0, j)),
                   pl.BlockSpec((2, 1, CB), lambda j, t: (0, 0, j))],
        out_shape=[jax.ShapeDtypeStruct((2, s, f2 // 2), MXU_DTYPE), jax.ShapeDtypeStruct((2, FFN_CONV, f2 // 2), F32),
                   jax.ShapeDtypeStruct((2, 1, f2 // 2), F32)],
        scratch_shapes=[pltpu.VMEM((SUBLANES, 2 * CB), F32), pltpu.VMEM((1 + FFN_CONV, SUBLANES, 2 * CB), F32)],
        compiler_params=_params("parallel", "arbitrary"), name=name,
    )(da, ga, gb, p, p, cw, cw)


def _rg_gates(xc, wa_ref, ba_ref, wx_ref, bx_ref, lam_ref):
    r = _sigmoid(_dot_nn(xc, wa_ref[0]) + ba_ref[...])
    ig = _sigmoid(_dot_nn(xc, wx_ref[0]) + bx_ref[...])
    sp = _softplus(-lam_ref[...])
    log_a = (-RG_C) * r * sp
    a = jnp.exp(log_a)
    mult = jnp.sqrt(_one_minus_sq_exp(log_a, a))
    return r, ig, sp, a, mult


def _rg_conv(scr, cw_ref, cb_ref, ts):
    views = [scr[5 + k:5 + k + ts, :] for k in range(RG_CONV)]
    xc = cb_ref[...]
    for k in range(RG_CONV):
        xc = xc + cw_ref[k:k + 1, :] * views[k]
    return xc, views


def _rg_param_specs():
    vec = pl.BlockSpec((1, CB), lambda g, t: (0, g))
    mat = pl.BlockSpec((1, CB, CB), lambda g, t: (g, 0, 0))
    return [pl.BlockSpec((RG_CONV, CB), lambda g, t: (0, g)), vec, mat, vec, mat, vec, vec]


def _scan_rows(a_scr, x_scr, out_ref, carry, ts, reverse):
    n = ts // SUBLANES
    row = lax.broadcasted_iota(jnp.int32, (SUBLANES, a_scr.shape[1]), 0)
    last = SUBLANES - 1

    def rows_of(k):
        return pl.ds(pl.multiple_of(k * SUBLANES, SUBLANES), SUBLANES)

    def local(k, _):
        rows = rows_of(k)
        a, x = a_scr[rows, :], x_scr[rows, :]
        if reverse:
            a = jnp.where(row == last, 1.0, pltpu.roll(a, last, axis=0))
            for sh in (1, 2, 4):
                keep = row < SUBLANES - sh
                x = x + a * jnp.where(keep, pltpu.roll(x, SUBLANES - sh, axis=0), 0.0)
                a = a * jnp.where(keep, pltpu.roll(a, SUBLANES - sh, axis=0), 1.0)
        else:
            for sh in (1, 2, 4):
                keep = row >= sh
                x = a * jnp.where(keep, pltpu.roll(x, sh, axis=0), 0.0) + x
                a = a * jnp.where(keep, pltpu.roll(a, sh, axis=0), 1.0)
        out_ref[rows, :] = x
        x_scr[rows, :] = a
        return 0

    lax.fori_loop(0, n, local, 0, unroll=4)

    def chain(k, c):
        rows = rows_of(n - 1 - k if reverse else k)
        v = out_ref[rows, :] + x_scr[rows, :] * c
        out_ref[rows, :] = v
        return a_scr[rows, :][0:1] * v[0:1] if reverse else v[last:last + 1]

    return lax.fori_loop(0, n, chain, carry, unroll=4)


def _rg_mid_fwd(pj, cw, cb, wa, ba, wx, bx, lam, name):
    s = pj.shape[0]
    nb = pj.shape[1] // (2 * CB)
    ts = _tile(s, (512,))
    nt = s // ts

    def body(gate_ref, x_ref, halo_ref, cw_ref, cb_ref, wa_ref, ba_ref, wx_ref, bx_ref, lam_ref, y_ref, hs_ref,
             scr, a_scr, u_scr, h_scr):
        t = pl.program_id(1)

        @pl.when(t == 0)
        def _():
            h_scr[...] = jnp.zeros_like(h_scr)

        scr[0:SUBLANES, :] = jnp.where(t > 0, halo_ref[...], 0.0)
        scr[SUBLANES:, :] = x_ref[...]
        xc, _ = _rg_conv(scr, cw_ref, cb_ref, ts)
        _, ig, _, a, mult = _rg_gates(xc, wa_ref, ba_ref, wx_ref, bx_ref, lam_ref)
        a_scr[...] = a
        u_scr[...] = mult * (ig * xc)
        h_scr[0:1, :] = _scan_rows(a_scr, u_scr, hs_ref, h_scr[0:1, :], ts, False)
        y_ref[...] = (_gelu(gate_ref[...])[0] * hs_ref[...]).astype(y_ref.dtype)

    blk = pl.BlockSpec((ts, CB), lambda g, t: (t, g))
    return pl.pallas_call(
        body, grid=(nb, nt),
        in_specs=_pair_specs((ts, CB), nb, lambda g, t: (t,))
        + [pl.BlockSpec((SUBLANES, CB), lambda g, t: _halo_row(ts, lambda u: u)(g, t) + (g + nb,))] + _rg_param_specs(),
        out_specs=[blk, blk],
        out_shape=[jax.ShapeDtypeStruct((s, nb * CB), MXU_DTYPE), jax.ShapeDtypeStruct((s, nb * CB), F32)],
        scratch_shapes=[pltpu.VMEM((ts + SUBLANES, CB), F32), pltpu.VMEM((ts, CB), F32), pltpu.VMEM((ts, CB), F32),
                        pltpu.VMEM((SUBLANES, CB), F32)],
        compiler_params=_params("parallel", "arbitrary"), name=name,
    )(pj, pj, pj, cw, cb, wa, ba, wx, bx, lam)


def _rg_mid_bwd(dy, pj, hs, cw, cb, wa, ba, wx, bx, lam, name):
    s = pj.shape[0]
    nb = pj.shape[1] // (2 * CB)
    ts = _tile(s, (512,))
    nt = s // ts

    def body(dy_ref, gate_ref, x_ref, halo_ref, hs_ref, hsh_ref, cw_ref, cb_ref, wa_ref, ba_ref, wx_ref, bx_ref, lam_ref,
             dpj_ref, dcw_ref, dcb_ref, dwa_ref, dba_ref, dwx_ref, dbx_ref, dlam_ref,
             scr, hscr, a_scr, d_scr, g_scr, dxscr, c_scr):
        tt = pl.program_id(1)
        t = nt - 1 - tt

        @pl.when(tt == 0)
        def _():
            c_scr[...] = jnp.zeros_like(c_scr)
            dxscr[ts:, :] = jnp.zeros((SUBLANES, CB), F32)
            for ref in (dcw_ref, dcb_ref, dwa_ref, dba_ref, dwx_ref, dbx_ref, dlam_ref):
                ref[...] = jnp.zeros_like(ref)

        scr[0:SUBLANES, :] = jnp.where(t > 0, halo_ref[...], 0.0)
        scr[SUBLANES:, :] = x_ref[...]
        hscr[0:SUBLANES, :] = jnp.where(t > 0, hsh_ref[...], 0.0)
        hscr[SUBLANES:, :] = hs_ref[...]
        xc, views = _rg_conv(scr, cw_ref, cb_ref, ts)
        r, ig, sp, a, mult = _rg_gates(xc, wa_ref, ba_ref, wx_ref, bx_ref, lam_ref)
        gate = gate_ref[...]
        gel, th = _gelu(gate)
        dyv = dy_ref[...]
        dpj_ref[0] = (dyv * hs_ref[...] * _gelu_grad(gate, th)).astype(dpj_ref.dtype)
        a_scr[...] = a
        d_scr[...] = dyv * gel
        c_scr[0:1, :] = _scan_rows(a_scr, d_scr, g_scr, c_scr[0:1, :], ts, True)
        du = g_scr[...]
        da = du * hscr[7:7 + ts, :]
        dmult = du * (ig * xc)
        dig = du * (mult * xc)
        dxc = du * (mult * ig)
        dlog_a = da * a - dmult * (a * a / mult)
        dlam_ref[...] += jnp.sum(dlog_a * r, axis=0, keepdims=True) * (RG_C * _sigmoid(-lam_ref[...]))
        dpr = dlog_a * ((-RG_C) * sp) * (r * (1.0 - r))
        dpi = dig * (ig * (1.0 - ig))
        dba_ref[...] += jnp.sum(dpr, axis=0, keepdims=True)
        dbx_ref[...] += jnp.sum(dpi, axis=0, keepdims=True)
        dwa_ref[0] += _dot_tn(xc, dpr)
        dwx_ref[0] += _dot_tn(xc, dpi)
        dxc = dxc + _dot_nt(dpr, wa_ref[0]) + _dot_nt(dpi, wx_ref[0])
        dcb_ref[...] += jnp.sum(dxc, axis=0, keepdims=True)
        for k in range(RG_CONV):
            dcw_ref[k:k + 1, :] += jnp.sum(dxc * views[k], axis=0, keepdims=True)
        dxscr[0:ts, :] = dxc
        dxp = cw_ref[3:4, :] * dxc
        for k in range(RG_CONV - 1):
            dxp = dxp + cw_ref[k:k + 1, :] * dxscr[3 - k:3 - k + ts, :]
        dpj_ref[1] = dxp.astype(dpj_ref.dtype)
        dxscr[ts:, :] = dxscr[0:SUBLANES, :]

    rev = lambda g, t: (nt - 1 - t, g)
    rev_halo = lambda g, t: (jnp.maximum((nt - 1 - t) * (ts // SUBLANES) - 1, 0), g)
    vec = pl.BlockSpec((1, CB), lambda g, t: (0, g))
    mat = pl.BlockSpec((1, CB, CB), lambda g, t: (g, 0, 0))
    d = nb * CB
    vshape = jax.ShapeDtypeStruct((1, d), F32)
    mshape = jax.ShapeDtypeStruct((nb, CB, CB), F32)
    return pl.pallas_call(
        body, grid=(nb, nt),
        in_specs=[pl.BlockSpec((ts, CB), rev)] + _pair_specs((ts, CB), nb, lambda g, t: (nt - 1 - t,))
        + [pl.BlockSpec((SUBLANES, CB), lambda g, t: (rev_halo(g, t)[0], g + nb)),
           pl.BlockSpec((ts, CB), rev), pl.BlockSpec((SUBLANES, CB), rev_halo)] + _rg_param_specs(),
        out_specs=[pl.BlockSpec((2, ts, CB), lambda g, t: (0, nt - 1 - t, g)), pl.BlockSpec((RG_CONV, CB), lambda g, t: (0, g)),
                   vec, mat, vec, mat, vec, vec],
        out_shape=[jax.ShapeDtypeStruct((2, s, d), MXU_DTYPE), jax.ShapeDtypeStruct((RG_CONV, d), F32), vshape, mshape, vshape,
                   mshape, vshape, vshape],
        scratch_shapes=[pltpu.VMEM((ts + SUBLANES, CB), F32), pltpu.VMEM((ts + SUBLANES, CB), F32), pltpu.VMEM((ts, CB), F32),
                        pltpu.VMEM((ts, CB), F32), pltpu.VMEM((ts, CB), F32), pltpu.VMEM((ts + SUBLANES, CB), F32),
                        pltpu.VMEM((SUBLANES, CB), F32)],
        compiler_params=_params("parallel", "arbitrary"), name=name,
    )(dy, pj, pj, pj, hs, hs, cw, cb, wa, ba, wx, bx, lam)


GLA_DK = 128
GLA_DV = 256
GLA_O_K = GLA_HEADS * GLA_DK
GLA_O_V = 2 * GLA_HEADS * GLA_DK
GLA_O_R = GLA_O_V + GLA_HEADS * GLA_DV
GLA_O_Z = GLA_O_R + GLA_HEADS * GLA_DV
GLA_IN = GLA_O_Z + GLA_RANK
GLA_TS = 256


def _dk(h, base=0):
    return slice(base + h * GLA_DK, base + (h + 1) * GLA_DK)


def _dv(h, base=0):
    return slice(base + h * GLA_DV, base + (h + 1) * GLA_DV)


def _split3(x):
    hi = x.astype(BF16)
    r1 = x - hi.astype(F32)
    mid = r1.astype(BF16)
    lo = (r1 - mid.astype(F32)).astype(BF16)
    return hi, mid, lo


def _chunk_cumsum(x, reverse):
    n = x.shape[0]
    i = lax.broadcasted_iota(jnp.int32, (n, n), 0)
    j = lax.broadcasted_iota(jnp.int32, (n, n), 1)
    same = (i // GLA_CHUNK) == (j // GLA_CHUNK)
    tri = jnp.where(same & ((j >= i) if reverse else (j <= i)), 1.0, 0.0).astype(BF16)
    out = jnp.zeros(x.shape, F32)
    for piece in _split3(x):
        out = out + lax.dot_general(tri, piece, (((1,), (0,)), ((), ())), preferred_element_type=F32)
    return out


def _gla_head(pj_ref, h):
    return (pj_ref[:, _dk(h)] * (GLA_DK ** -0.5), pj_ref[:, _dk(h, GLA_O_K)], pj_ref[:, _dv(h, GLA_O_V)],
            pj_ref[:, _dv(h, GLA_O_R)])


def _gla_decays(gc):
    gref = gc[GLA_CHUNK // 2:GLA_CHUNK // 2 + 1, :]
    glast = gc[GLA_CHUNK - 1:GLA_CHUNK, :]
    return jnp.exp(gc), jnp.exp(gc - gref), jnp.exp(gref - gc), jnp.exp(glast - gc), jnp.exp(glast)


def _causal_mask():
    i = lax.broadcasted_iota(jnp.int32, (GLA_CHUNK, GLA_CHUNK), 0)
    j = lax.broadcasted_iota(jnp.int32, (GLA_CHUNK, GLA_CHUNK), 1)
    return j <= i


def _log_sigmoid(x):
    return jnp.minimum(x, 0.0) - _log1p_pos(jnp.exp(-jnp.abs(x)))


def _gla_mid_fwd(pj, wal, bal, ng, name):
    s, nh = pj.shape[0], GLA_HEADS
    ts = _tile(s, (GLA_TS,))
    nt, nc = s // ts, ts // GLA_CHUNK

    def body(pj_ref, wal_ref, bal_ref, ng_ref, act_ref, o_ref, st_ref, s_scr):
        t = pl.program_id(0)

        @pl.when(t == 0)
        def _():
            s_scr[...] = jnp.zeros_like(s_scr)

        heads = []
        z = pj_ref[:, GLA_O_Z:]
        for h in range(nh):
            q, k, v, r = _gla_head(pj_ref, h)
            g = _log_sigmoid(_dot_nn(z, wal_ref[:, _dk(h)]) + bal_ref[:, _dk(h)]) * (1.0 / GLA_TAU)
            heads.append((q, k, v, r, _chunk_cumsum(g, False)))
        mask = _causal_mask()
        for c in range(nc):
            sl = slice(c * GLA_CHUNK, (c + 1) * GLA_CHUNK)
            for h, (q, k, v, r, gcum) in enumerate(heads):
                eg, eq, ek, ekd, egl = _gla_decays(gcum[sl])
                st = s_scr[h]
                st_ref[c, h] = st
                attn = jnp.where(mask, _dot_nt(q[sl] * eq, k[sl] * ek), 0.0)
                o_ref[sl, h * GLA_DV:(h + 1) * GLA_DV] = _dot_nt(q[sl] * eg, st) + _dot_nn(attn, v[sl])
                s_scr[h] = st * egl + _dot_tn(v[sl], k[sl] * ekd)
        for h, (q, k, v, r, gcum) in enumerate(heads):
            cols = slice(h * GLA_DV, (h + 1) * GLA_DV)
            o = o_ref[:, cols]
            on = o * lax.rsqrt(jnp.mean(o * o, axis=-1, keepdims=True) + EPS)
            act_ref[:, cols] = ((on * ng_ref[...]) * (r * _sigmoid(r))).astype(act_ref.dtype)

    blk = pl.BlockSpec((ts, nh * GLA_DV), lambda t: (t, 0))
    whole = lambda shape: pl.BlockSpec(shape, lambda t: (0,) * len(shape))
    return pl.pallas_call(
        body, grid=(nt,),
        in_specs=[pl.BlockSpec((ts, GLA_IN), lambda t: (t, 0)), whole((GLA_RANK, nh * GLA_DK)), whole((1, nh * GLA_DK)),
                  whole((1, GLA_DV))],
        out_specs=[blk, blk, pl.BlockSpec((nc, nh, GLA_DV, GLA_DK), lambda t: (t, 0, 0, 0))],
        out_shape=[jax.ShapeDtypeStruct((s, nh * GLA_DV), MXU_DTYPE), jax.ShapeDtypeStruct((s, nh * GLA_DV), F32),
                   jax.ShapeDtypeStruct((s // GLA_CHUNK, nh, GLA_DV, GLA_DK), F32)],
        scratch_shapes=[pltpu.VMEM((nh, GLA_DV, GLA_DK), F32)],
        compiler_params=_params("arbitrary"), name=name,
    )(pj, wal, bal, ng)


def _gla_mid_bwd(dact, pj, o, st, wal, bal, ng, name):
    s, nh = pj.shape[0], GLA_HEADS
    ts = _tile(s, (GLA_TS,))
    nt, nc = s // ts, ts // GLA_CHUNK

    def body(dact_ref, pj_ref, o_ref, st_ref, wal_ref, bal_ref, ng_ref, dpj_ref, dwal_ref, dbal_ref, dng_ref,
             ds_scr, dg_scr):
        tt = pl.program_id(0)

        @pl.when(tt == 0)
        def _():
            ds_scr[...] = jnp.zeros_like(ds_scr)
            dwal_ref[...] = jnp.zeros_like(dwal_ref)
            dbal_ref[...] = jnp.zeros_like(dbal_ref)
            dng_ref[...] = jnp.zeros_like(dng_ref)

        heads = []
        z = pj_ref[:, GLA_O_Z:]
        for h in range(nh):
            q, k, v, r = _gla_head(pj_ref, h)
            logit = _dot_nn(z, wal_ref[:, _dk(h)]) + bal_ref[:, _dk(h)]
            gcum = _chunk_cumsum(_log_sigmoid(logit) * (1.0 / GLA_TAU), False)
            ov = o_ref[:, h * GLA_DV:(h + 1) * GLA_DV]
            ro = lax.rsqrt(jnp.mean(ov * ov, axis=-1, keepdims=True) + EPS)
            on = ov * ro
            sg = _sigmoid(r)
            sil = r * sg
            dav = dact_ref[:, h * GLA_DV:(h + 1) * GLA_DV]
            dpj_ref[:, _dv(h, GLA_O_R)] = (dav * (on * ng_ref[...]) * (sg + sil * (1.0 - sg))).astype(dpj_ref.dtype)
            t1 = dav * sil
            dng_ref[...] += jnp.sum(t1 * on, axis=0, keepdims=True)
            dn = t1 * ng_ref[...]
            do = ro * (dn - on * jnp.mean(dn * on, axis=-1, keepdims=True))
            heads.append((q, k, v, logit, gcum, do))
        mask = _causal_mask()
        scale = GLA_DK ** -0.5
        last_row = lax.broadcasted_iota(jnp.int32, (GLA_CHUNK, GLA_DK), 0) == GLA_CHUNK - 1
        for c in reversed(range(nc)):
            sl = slice(c * GLA_CHUNK, (c + 1) * GLA_CHUNK)
            for h, (q, k, v, logit, gcum, do) in enumerate(heads):
                eg, eq, ek, ekd, egl = _gla_decays(gcum[sl])
                qc, kc, vc, doc = q[sl], k[sl], v[sl], do[sl]
                qg, qt, kt, kd = qc * eg, qc * eq, kc * ek, kc * ekd
                sp = st_ref[c, h]
                ds = ds_scr[h]
                attn = jnp.where(mask, _dot_nt(qt, kt), 0.0)
                dattn = jnp.where(mask, _dot_nt(doc, vc), 0.0)
                dqg = _dot_nn(doc, sp)
                dqt = _dot_nn(dattn, kt)
                dkt = _dot_tn(dattn, qt)
                dkd = _dot_nn(vc, ds)
                dpj_ref[sl, _dv(h, GLA_O_V)] = (_dot_tn(attn, doc) + _dot_nt(kd, ds)).astype(dpj_ref.dtype)
                dpj_ref[sl, _dk(h)] = (scale * (dqg * eg + dqt * eq)).astype(dpj_ref.dtype)
                dpj_ref[sl, _dk(h, GLA_O_K)] = (dkt * ek + dkd * ekd).astype(dpj_ref.dtype)
                kdd = dkd * kd
                dgl = jnp.sum(kdd, axis=0, keepdims=True) + jnp.sum(ds * sp, axis=0, keepdims=True) * egl
                dg_scr[h, sl, :] = dqg * qg + dqt * qt - dkt * kt - kdd + jnp.where(last_row, dgl, 0.0)
                ds_scr[h] = ds * egl + _dot_tn(doc, qg)
        dz = jnp.zeros((ts, GLA_RANK), F32)
        for h, (q, k, v, logit, gcum, do) in enumerate(heads):
            dlogit = _chunk_cumsum(dg_scr[h], True) * (1.0 / GLA_TAU) * _sigmoid(-logit)
            dz = dz + _dot_nt(dlogit, wal_ref[:, _dk(h)])
            dwal_ref[:, _dk(h)] += _dot_tn(z, dlogit)
            dbal_ref[:, _dk(h)] += jnp.sum(dlogit, axis=0, keepdims=True)
        dpj_ref[:, GLA_O_Z:] = dz.astype(dpj_ref.dtype)

    rev = lambda t: (nt - 1 - t, 0)
    whole = lambda shape: pl.BlockSpec(shape, lambda t: (0,) * len(shape))
    wide = pl.BlockSpec((ts, nh * GLA_DV), rev)
    return pl.pallas_call(
        body, grid=(nt,),
        in_specs=[wide, pl.BlockSpec((ts, GLA_IN), rev), wide,
                  pl.BlockSpec((nc, nh, GLA_DV, GLA_DK), lambda t: (nt - 1 - t, 0, 0, 0)),
                  whole((GLA_RANK, nh * GLA_DK)), whole((1, nh * GLA_DK)), whole((1, GLA_DV))],
        out_specs=[pl.BlockSpec((ts, GLA_IN), rev), whole((GLA_RANK, nh * GLA_DK)), whole((1, nh * GLA_DK)), whole((1, GLA_DV))],
        out_shape=[jax.ShapeDtypeStruct((s, GLA_IN), MXU_DTYPE), jax.ShapeDtypeStruct((GLA_RANK, nh * GLA_DK), F32),
                   jax.ShapeDtypeStruct((1, nh * GLA_DK), F32), jax.ShapeDtypeStruct((1, GLA_DV), F32)],
        scratch_shapes=[pltpu.VMEM((nh, GLA_DV, GLA_DK), F32), pltpu.VMEM((nh, ts, GLA_DK), F32)],
        compiler_params=_params("arbitrary"), name=name,
    )(dact, pj, o, st, wal, bal, ng)


def _adamw(w, gs, m, v, name, after=None):
    layers, rows, cols = w.shape
    gs = list(gs) if isinstance(gs, (list, tuple)) else gs
    n_g = len(gs) if isinstance(gs, list) else 1
    if rows % SUBLANES == 0:
        tr, tc = _tile(rows, (256, 128, 64, 32, 16, 8)), cols
    else:
        tr, tc = rows, _tile(cols, (256, 128))
    c1 = 1.0 / (1.0 - ADAM_B1 ** ADAM_STEP)
    c2 = 1.0 / (1.0 - ADAM_B2 ** ADAM_STEP)

    def body(*refs):
        g_refs, (w_ref, m_ref, v_ref) = refs[:n_g], refs[n_g:n_g + 3]
        go_ref, d_ref, mo_ref, vo_ref = refs[-4:]
        gv = g_refs[0][...]
        for l in range(1, n_g):
            gv = jnp.where(pl.program_id(0) == l, g_refs[l][...], gv)
        m2 = ADAM_B1 * m_ref[...] + (1.0 - ADAM_B1) * gv
        v2 = ADAM_B2 * v_ref[...] + (1.0 - ADAM_B2) * (gv * gv)
        d_ref[...] = (-ADAM_LR) * ((m2 * c1) / (jnp.sqrt(v2 * c2) + ADAM_EPS) + ADAM_WD * w_ref[...])
        go_ref[...] = gv
        mo_ref[...] = m2
        vo_ref[...] = v2

    spec = pl.BlockSpec((None, tr, tc), lambda l, i, j: (l, i, j))
    g_specs = [pl.BlockSpec((tr, tc), lambda l, i, j: (i, j))] * n_g if isinstance(gs, list) else [spec]
    extra = [] if after is None else [(after, _ANY)]
    shape = jax.ShapeDtypeStruct((layers, rows, cols), F32)
    return pl.pallas_call(
        body, grid=(layers, rows // tr, cols // tc), in_specs=g_specs + [spec] * 3 + [sp for _, sp in extra],
        out_specs=[spec] * 4, out_shape=[shape] * 4, compiler_params=_params("parallel", "parallel", "parallel"), name=name,
    )(*(gs if isinstance(gs, list) else [gs]), w, m, v, *[a for a, _ in extra])


def _col_slots(w):
    r, c = w.shape
    return jnp.moveaxis(w.reshape(r, N_CHIP, c // N_CHIP), 1, 0)


def _from_col_slots(w):
    n, r, c = w.shape
    return jnp.moveaxis(w, 0, 1).reshape(r, n * c)


def _block_rows_to_slots(w):
    g, r4, cc = w.shape
    return jnp.swapaxes(w.reshape(g, N_CHIP, r4 // N_CHIP, cc), 0, 1).reshape(N_CHIP, g * (r4 // N_CHIP), cc)


def _slots_to_block_rows(w, g):
    n, gr, cc = w.shape
    return jnp.swapaxes(w.reshape(n, g, gr // g, cc), 0, 1).reshape(g, n * (gr // g), cc)


def _local_step(x, tgt, mod, w, fetch=None, done=None, later=None):
    depth = mod.shape[0]
    row = lambda v: v.reshape(1, -1)
    w = dict(w)
    w["ffn_w_up"], w["ffn_w_down"] = dict(enumerate(w["ffn_w_up"])), dict(enumerate(w["ffn_w_down"]))

    def arrive(stage, after):
        if fetch is not None:
            for k, v in fetch(stage, after).items():
                if isinstance(v, dict):
                    w[k].update(v)
                else:
                    w[k] = v

    saved = []
    for i in range(depth):
        if i == 1:
            arrive("gla", x)
        sh_m, sc_m, gt_m, sh_f, sc_f, gt_f = (mod[i, j:j + 1] for j in range(6))
        g0, g1, g2, g3 = (w["norm_g"][i, j:j + 1] for j in range(4))
        tag = f"_l{i}"
        if i == 0:
            h = _norm_mod_fwd(x, g0, sc_m, sh_m, "norm_mix" + tag)
        if i % 2 == 0:
            pj = _mm(h, w["rg_w_in"], w_slots=N_CHIP, name="rg_in" + tag)
            act, aux = _rg_mid_fwd(pj, w["rg_conv_w"], row(w["rg_conv_b"]), w["rg_wa"], row(w["rg_ba"]), w["rg_wx"],
                                   row(w["rg_bx"]), row(w["rg_lambda"]), "rg_mid" + tag)
            y = _mm(act, w["rg_w_out"], name="rg_out" + tag)
        else:
            pj = _mm(h, w["gla_w_in"], tm_max=512, name="gla_in" + tag)
            act, *aux = _gla_mid_fwd(pj, w["gla_w_alpha"], row(w["gla_b_alpha"]), row(w["gla_norm_g"]), "gla_mid" + tag)
            y = _mm(act, w["gla_w_out"], name="gla_out" + tag)
        x1, h2 = _post_norm_fwd(x, y, g1, gt_m, g2, sc_f, sh_f, "post_mix" + tag)
        arrive(f"ffn{i}", x1)
        p = _mm(h2, w["ffn_w_up"][i], w_slots=N_CHIP, name="ffn_up" + tag)
        a, ga, gb = _ffn_mid_fwd(p, w["ffn_conv_w"][i], w["ffn_conv_b"][i:i + 1], "ffn_mid" + tag)
        y2 = _mm(a, w["ffn_w_down"][i], name="ffn_down" + tag)
        saved_h = h
        if i + 1 < depth:
            nxt = [mod[i + 1, j:j + 1] for j in range(2)] + [w["norm_g"][i + 1, 0:1]]
            x2, h = _post_norm_fwd(x1, y2, g3, gt_f, nxt[2], nxt[1], nxt[0], "post_ffn" + tag)
        else:
            x2 = None
            cols, dx = _post_loss(x1, y2, g3, gt_f, tgt, "post_ffn_loss")
        saved.append((x, saved_h, pj, act, aux, y, x1, h2, p, (a, ga, gb), y2))
        x = x2

    stacked = ("norm_g", "ffn_conv_w", "ffn_conv_b", "mod")
    gr = {k: [None] * depth for k in stacked + ("ffn_w_up", "ffn_w_down")}
    told = lambda stage: done(stage, gr) if done is not None else 0.0
    told_later = lambda stage, after: later(stage, after) if later is not None else 0.0
    for i in reversed(range(depth)):
        x0, h, pj, act, aux, y, x1, h2, p, (a, ga, gb), y2 = saved[i]
        sh_m, sc_m, gt_m, sh_f, sc_f, gt_f = (mod[i, j:j + 1] for j in range(6))
        g0, g1, g2, g3 = (w["norm_g"][i, j:j + 1] for j in range(4))
        tag = f"_l{i}"
        if i == depth - 1:
            dy2, d_g3, d_gt_f = _post_bwd(dx, y2, g3, gt_f, "post_ffn_b" + tag)
        else:
            dy2, d_g3, d_gt_f = ahead
        da = _mm(dy2, w["ffn_w_down"][i], tb=True, name="ffn_down_dx" + tag)
        gr["ffn_w_down"][i] = _mm(a, dy2, ta=True, name="ffn_down_dw" + tag)
        conv_w = w["ffn_conv_w"][i] + (told_later("l1", da) if i == 0 else 0.0)
        dp, dcw, dcb = _ffn_mid_bwd(da, p, ga, gb, conv_w, "ffn_mid_b" + tag)
        gr["ffn_conv_w"][i], gr["ffn_conv_b"][i] = _cat(dcw[0], dcw[1]), _cat(dcb[0], dcb[1])[0]
        dh2 = _mm(dp, w["ffn_w_up"][i], tb=True, a_parts=2, w_slots=N_CHIP, name="ffn_up_dx" + tag)
        gr["ffn_w_up"][i] = _mm(h2, dp, ta=True, b_parts=2, out_slots=N_CHIP, name="ffn_up_dw" + tag)
        if i == 0:
            gt_m = gt_m + told("ffn0")
        dx1, dy, d_g2, d_sc_f, d_sh_f, d_g1, d_gt_m = _norm_post_bwd(dh2, x1, g2, sc_f, dx, y, g1, gt_m, "norm_ffn_b" + tag)
        if i % 2 == 0:
            dact = _mm(dy, w["rg_w_out"], tb=True, name="rg_out_dx" + tag)
            gr["rg_w_out"] = _mm(act, dy, ta=True, name="rg_out_dw" + tag)
            lam = row(w["rg_lambda"]) + told_later("ffn0", gr["rg_w_out"])
            dpj, gr["rg_conv_w"], d_cb, gr["rg_wa"], d_ba, gr["rg_wx"], d_bx, d_lam = _rg_mid_bwd(
                dact, pj, aux, w["rg_conv_w"], row(w["rg_conv_b"]), w["rg_wa"], row(w["rg_ba"]), w["rg_wx"],
                row(w["rg_bx"]), lam, "rg_mid_b" + tag)
            gr["rg_conv_b"], gr["rg_ba"], gr["rg_bx"], gr["rg_lambda"] = d_cb[0], d_ba[0], d_bx[0], d_lam[0]
            dh = _mm(dpj, w["rg_w_in"], tb=True, a_parts=2, w_slots=N_CHIP, name="rg_in_dx" + tag)
            gr["rg_w_in"] = _mm(h, dpj, ta=True, b_parts=2, out_slots=N_CHIP, name="rg_in_dw" + tag)
        else:
            dact = _mm(dy, w["gla_w_out"], tb=True, name="gla_out_dx" + tag)
            gr["gla_w_out"] = _mm(act, dy, ta=True, name="gla_out_dw" + tag)
            dpj, gr["gla_w_alpha"], d_bal, d_ng = _gla_mid_bwd(dact, pj, aux[0], aux[1], w["gla_w_alpha"], row(w["gla_b_alpha"]),
                                                               row(w["gla_norm_g"]), "gla_mid_b" + tag)
            gr["gla_b_alpha"], gr["gla_norm_g"] = d_bal[0], d_ng[0]
            dh = _mm(dpj, w["gla_w_in"], tb=True, name="gla_in_dx" + tag)
            gr["gla_w_in"] = _mm(h, dpj, ta=True, tm_max=512, name="gla_in_dw" + tag)
            mod = mod.at[0].add(told("l1"))
        if i > 0:
            dx, dy_below, d_g0, d_sc_m, d_sh_m, d_g_below, d_gt_below = _norm_post_bwd(
                dh, x0, g0, sc_m, dx1, saved[i - 1][-1], w["norm_g"][i - 1, 3:4], mod[i - 1, 5:6], "norm_mix_b" + tag)
            ahead = (dy_below, d_g_below, d_gt_below)
        else:
            dx, d_g0, d_sc_m, d_sh_m = _norm_mod_bwd(dh, x0, g0, sc_m, dx1, "norm_mix_b" + tag)
        gr["norm_g"][i] = jnp.concatenate([d_g0, d_g1, d_g2, d_g3], axis=0)
        gr["mod"][i] = jnp.concatenate([d_sh_m, d_sc_m, d_gt_m, d_sh_f, d_sc_f, d_gt_f], axis=0)
    for k in stacked:
        gr[k] = jnp.stack(gr[k])
    return cols, dx, gr


ADA_ROWS = 16


def _ada_fwd(c16, ada_w, ada_b, name):
    depth, d, n = ada_w.shape
    tn = _tile(n, (512, 256, 128))

    def body(c_ref, w_ref, b_ref, o_ref):
        cv = c_ref[...]
        o_ref[0] = _dot_nn(cv * _sigmoid(cv), w_ref[0]) + b_ref[0]

    return pl.pallas_call(
        body, grid=(depth, n // tn),
        in_specs=[pl.BlockSpec((ADA_ROWS, d), lambda l, j: (0, 0)), pl.BlockSpec((1, d, tn), lambda l, j: (l, 0, j)),
                  pl.BlockSpec((1, 1, tn), lambda l, j: (l, 0, j))],
        out_specs=pl.BlockSpec((1, ADA_ROWS, tn), lambda l, j: (l, 0, j)),
        out_shape=jax.ShapeDtypeStruct((depth, ADA_ROWS, n), F32),
        compiler_params=_params("parallel", "parallel"), name=name,
    )(c16, ada_w, ada_b)


def _ada_bwd(c16, dmod16, name):
    depth, _, n = dmod16.shape
    d = c16.shape[1]
    tn = _tile(n, (512, 256, 128))

    def body(c_ref, dm_ref, o_ref):
        cv = c_ref[...]
        o_ref[0] = _dot_tn(cv * _sigmoid(cv), dm_ref[0])

    return pl.pallas_call(
        body, grid=(depth, n // tn),
        in_specs=[pl.BlockSpec((ADA_ROWS, d), lambda l, j: (0, 0)), pl.BlockSpec((1, ADA_ROWS, tn), lambda l, j: (l, 0, j))],
        out_specs=pl.BlockSpec((1, d, tn), lambda l, j: (l, 0, j)),
        out_shape=jax.ShapeDtypeStruct((depth, d, n), F32),
        compiler_params=_params("parallel", "parallel"), name=name,
    )(c16, dmod16)


PACK_COLS = 1024
_ANY = pl.BlockSpec(memory_space=pl.ANY)
_VMEM = pl.BlockSpec(memory_space=pltpu.VMEM)


def _place():
    return lax.axis_index("x"), lax.axis_index("y"), lax.axis_index("c")


def _other_chips(x, y):
    return [(1 - x, y), (x, 1 - y), (1 - x, 1 - y)]


def _rcopy(src, dst, send_sems, recv_sems, k, peer):
    return pltpu.make_async_remote_copy(src_ref=src, dst_ref=dst, send_sem=send_sems.at[k], recv_sem=recv_sems.at[k],
                                        device_id=peer, device_id_type=MESH)


def _all_gather_8(v, name):
    r, cc = v.shape

    def body(v_ref, out_ref, send_sems, recv_sems, local_sem):
        x, y, c = _place()
        me = 4 * x + 2 * y + c
        mine = pltpu.make_async_copy(v_ref, out_ref.at[me], local_sem)
        mine.start()
        peers = []
        for k in range(1, N_DEV):
            px = 1 - x if k & 4 else x
            py = 1 - y if k & 2 else y
            pc = 1 - c if k & 1 else c
            peers.append((px, py, pc))
        sends = [_rcopy(v_ref, out_ref.at[me], send_sems, recv_sems, k, p) for k, p in enumerate(peers)]
        for cp in sends:
            cp.start()
        for k, (px, py, pc) in enumerate(peers):
            _rcopy(v_ref, out_ref.at[4 * px + 2 * py + pc], send_sems, recv_sems, k, (px, py, pc)).wait_recv()
        for cp in sends:
            cp.wait_send()
        mine.wait()

    return pl.pallas_call(
        body, in_specs=[_VMEM], out_specs=_VMEM, out_shape=jax.ShapeDtypeStruct((N_DEV, r, cc), v.dtype),
        scratch_shapes=[pltpu.SemaphoreType.DMA((N_DEV - 1,)), pltpu.SemaphoreType.DMA((N_DEV - 1,)), pltpu.SemaphoreType.DMA],
        compiler_params=pltpu.CompilerParams(vmem_limit_bytes=VMEM_LIMIT), name=name,
    )(v)


def _gather_chips(shards, name):
    n = len(shards)
    per = 2 * (N_CHIP - 1)

    def body(*refs):
        ins, outs, (send_sems, recv_sems) = refs[:n], refs[n:2 * n], refs[2 * n:]
        x, y, c = _place()
        chip = 2 * x + y
        chips = _other_chips(x, y)
        rows = [(pl.ds(c * (r.shape[0] // 2), r.shape[0] // 2), pl.ds((1 - c) * (r.shape[0] // 2), r.shape[0] // 2)) for r in ins]
        first = [_rcopy(ins[i].at[rows[i][0]], outs[i].at[chip, rows[i][0]], send_sems, recv_sems, per * i + j, (px, py, c))
                 for i in range(n) for j, (px, py) in enumerate(chips)]
        for cp in first:
            cp.start()
        passed = []
        for i in range(n):
            for j, (px, py) in enumerate(chips):
                landed = outs[i].at[2 * px + py, rows[i][0]]
                _rcopy(ins[i].at[rows[i][0]], landed, send_sems, recv_sems, per * i + j, (px, py, c)).wait_recv()
                fw = _rcopy(landed, landed, send_sems, recv_sems, per * i + N_CHIP - 1 + j, (x, y, 1 - c))
                fw.start()
                passed.append(fw)
        for i in range(n):
            for j, (px, py) in enumerate(chips):
                landed = outs[i].at[2 * px + py, rows[i][1]]
                _rcopy(landed, landed, send_sems, recv_sems, per * i + N_CHIP - 1 + j, (x, y, 1 - c)).wait_recv()
        for cp in first + passed:
            cp.wait_send()

    return pl.pallas_call(
        body, in_specs=[_ANY] * n, out_specs=[_ANY] * n,
        out_shape=[jax.ShapeDtypeStruct((N_CHIP,) + sh.shape, sh.dtype) for sh in shards],
        scratch_shapes=[pltpu.SemaphoreType.DMA((per * n,)), pltpu.SemaphoreType.DMA((per * n,))], name=name,
    )(*shards)


def _pair_exchange(gs, name):
    n = len(gs)

    def body(*refs):
        ins, outs, (send_sems, recv_sems) = refs[:n], refs[n:2 * n], refs[2 * n:]
        x, y, c = _place()
        copies = []
        for i in range(n):
            half = ins[i].shape[1] // 2
            copies.append(_rcopy(ins[i].at[:, pl.ds((1 - c) * half, half)], outs[i], send_sems, recv_sems, i, (x, y, 1 - c)))
        for cp in copies:
            cp.start()
        for cp in copies:
            cp.wait()

    return pl.pallas_call(
        body, in_specs=[_ANY] * n, out_specs=[_ANY] * n,
        out_shape=[jax.ShapeDtypeStruct((g.shape[0], g.shape[1] // 2, g.shape[2]), g.dtype) for g in gs],
        scratch_shapes=[pltpu.SemaphoreType.DMA((n,)), pltpu.SemaphoreType.DMA((n,))], name=name,
    )(*gs)


_ROW_TILES = (640, 512, 352, 256, 128, 64, 32, 16)


def _pair_sum(g, other, c_idx, name):
    n, half, cc = other.shape
    tr = _tile(half, _ROW_TILES)

    def body(c_ref, g_ref, o_ref, out_ref):
        out_ref[...] = (g_ref[...] + o_ref[...]).astype(out_ref.dtype)

    return pl.pallas_call(
        body,
        grid_spec=pltpu.PrefetchScalarGridSpec(
            num_scalar_prefetch=1, grid=(n, half // tr),
            in_specs=[pl.BlockSpec((None, None, tr, cc), lambda k, i, c_ref: (k, c_ref[0], i, 0)),
                      pl.BlockSpec((None, tr, cc), lambda k, i, c_ref: (k, i, 0))],
            out_specs=pl.BlockSpec((None, tr, cc), lambda k, i, c_ref: (k, i, 0))),
        out_shape=jax.ShapeDtypeStruct((n, half, cc), BF16),
        compiler_params=_params("parallel", "parallel"), name=name,
    )(c_idx, g.reshape(n, 2, half, cc), other)


def _chip_exchange(ps, name):
    n = len(ps)
    per = N_CHIP - 1

    def body(*refs):
        ins, outs, (send_sems, recv_sems) = refs[:n], refs[n:2 * n], refs[2 * n:]
        x, y, c = _place()
        chip = 2 * x + y
        chips = _other_chips(x, y)
        sends = [_rcopy(ins[i].at[2 * px + py], outs[i].at[chip], send_sems, recv_sems, per * i + j, (px, py, c))
                 for i in range(n) for j, (px, py) in enumerate(chips)]
        for cp in sends:
            cp.start()
        for i in range(n):
            for j, (px, py) in enumerate(chips):
                _rcopy(ins[i].at[chip], outs[i].at[2 * px + py], send_sems, recv_sems, per * i + j, (px, py, c)).wait_recv()
        for cp in sends:
            cp.wait_send()

    return pl.pallas_call(
        body, in_specs=[_ANY] * n, out_specs=[_ANY] * n, out_shape=[jax.ShapeDtypeStruct(p.shape, p.dtype) for p in ps],
        scratch_shapes=[pltpu.SemaphoreType.DMA((per * n,)), pltpu.SemaphoreType.DMA((per * n,))], name=name,
    )(*ps)


_HBM = pl.BlockSpec(memory_space=pltpu.HBM)
_SEM = pl.BlockSpec(memory_space=pltpu.SEMAPHORE)
_DATAFLOW = pltpu.SideEffectType.DATAFLOW_SIDE_EFFECTING


def _split_copies(srcs, lands, send_sems, recv_sems, mode, arriving):
    x, y, c = _place()
    chip = 2 * x + y
    out = []
    for i, (src, land) in enumerate(zip(srcs, lands)):
        if mode == "pair":
            half = src.shape[1] // 2
            out.append(_rcopy(src.at[:, pl.ds((1 - c) * half, half)], land, send_sems, recv_sems, i, (x, y, 1 - c)))
            continue
        for j, (px, py) in enumerate(_other_chips(x, y)):
            there = 2 * px + py
            part = src.at[there] if mode == "slots" else src
            out.append(_rcopy(part, land.at[there if arriving else chip], send_sems, recv_sems, (N_CHIP - 1) * i + j, (px, py, c)))
    return out


def _land_shape(src, mode):
    if mode == "pair":
        return (src.shape[0], src.shape[1] // 2, src.shape[2])
    return (N_CHIP,) + (src.shape[1:] if mode == "slots" else src.shape)


def _send_start(srcs, mode, name):
    n = len(srcs)
    n_sem = n if mode == "pair" else (N_CHIP - 1) * n
    lands = [lax.empty(_land_shape(s, mode), s.dtype) for s in srcs]

    def body(*refs):
        ins, zones, (send_sems, recv_sems) = refs[:n], refs[n:2 * n], refs[2 * n:2 * n + 2]
        for cp in _split_copies(ins, zones, send_sems, recv_sems, mode, False):
            cp.start()
        refs[-1][...] = jnp.zeros_like(refs[-1])

    hbm = lambda a: pltpu.HBM(a.shape, a.dtype)
    outs = pl.pallas_call(
        body, name=name, in_specs=[_HBM] * (2 * n),
        out_shape=(pltpu.SemaphoreType.DMA((n_sem,)), pltpu.SemaphoreType.DMA((n_sem,)), *[hbm(a) for a in srcs],
                   *[hbm(a) for a in lands], jax.ShapeDtypeStruct((SUBLANES, LANES), F32)),
        out_specs=(_SEM, _SEM, *[_HBM] * (2 * n), _VMEM), input_output_aliases={i: 2 + i for i in range(2 * n)},
        compiler_params=pltpu.CompilerParams(has_side_effects=_DATAFLOW),
    )(*[pltpu.with_memory_space_constraint(a, pltpu.HBM) for a in list(srcs) + lands])
    return (outs[0], outs[1], list(outs[2:2 + n]), list(outs[2 + n:2 + 2 * n])), outs[-1]


def _send_wait(state, after, mode, name):
    send_sems, recv_sems, srcs, lands = state
    n = len(srcs)

    def body(*refs):
        ins, zones, (send_s, recv_s) = refs[:n], refs[n:2 * n], refs[2 * n:2 * n + 2]
        for cp in _split_copies(ins, zones, send_s, recv_s, mode, True):
            cp.wait_send()
            cp.wait_recv()

    hbm = lambda a: pltpu.HBM(a.shape, a.dtype)
    outs = pl.pallas_call(
        body, name=name, in_specs=[_HBM] * (2 * n) + [_SEM, _SEM, _ANY],
        out_shape=tuple(hbm(a) for a in srcs + lands), out_specs=tuple([_HBM] * (2 * n)),
        input_output_aliases={i: i for i in range(2 * n)},
        compiler_params=pltpu.CompilerParams(has_side_effects=_DATAFLOW),
    )(*srcs, *lands, send_sems, recv_sems, after)
    return list(outs[:n]), list(outs[n:])


def _sum_lead(v, name):
    n, r, cc = v.shape
    tr = _tile(r, _ROW_TILES + (8,))

    def body(v_ref, o_ref):
        acc = v_ref[0].astype(F32)
        for k in range(1, n):
            acc = acc + v_ref[k].astype(F32)
        o_ref[...] = acc

    return pl.pallas_call(
        body, grid=(r // tr,), in_specs=[pl.BlockSpec((n, tr, cc), lambda i: (0, i, 0))],
        out_specs=pl.BlockSpec((tr, cc), lambda i: (i, 0)), out_shape=jax.ShapeDtypeStruct((r, cc), F32),
        compiler_params=_params("parallel"), name=name,
    )(v)


def _chip_sum(arrived, mine, chip_idx, name):
    n, r, cc = arrived.shape
    tr = _tile(r, _ROW_TILES)

    def body(chip_ref, a_ref, m_ref, o_ref):
        acc = jnp.zeros((tr, cc), F32)
        for k in range(n):
            acc = acc + jnp.where(chip_ref[0] == k, m_ref[...], a_ref[k]).astype(F32)
        o_ref[...] = acc

    return pl.pallas_call(
        body,
        grid_spec=pltpu.PrefetchScalarGridSpec(
            num_scalar_prefetch=1, grid=(r // tr,),
            in_specs=[pl.BlockSpec((n, tr, cc), lambda i, chip_ref: (0, i, 0)),
                      pl.BlockSpec((None, tr, cc), lambda i, chip_ref: (chip_ref[0], i, 0))],
            out_specs=pl.BlockSpec((tr, cc), lambda i, chip_ref: (i, 0))),
        out_shape=jax.ShapeDtypeStruct((r, cc), F32), compiler_params=_params("parallel"), name=name,
    )(chip_idx, arrived, mine)


def _pair_share(reds, name):
    n = len(reds)

    def body(*refs):
        ins, outs, (send_sems, recv_sems) = refs[:n], refs[n:2 * n], refs[2 * n:]
        x, y, c = _place()
        copies = [_rcopy(ins[i], outs[i].at[c], send_sems, recv_sems, i, (x, y, 1 - c)) for i in range(n)]
        for cp in copies:
            cp.start()
        for i in range(n):
            _rcopy(ins[i], outs[i].at[1 - c], send_sems, recv_sems, i, (x, y, 1 - c)).wait_recv()
        for cp in copies:
            cp.wait_send()

    return pl.pallas_call(
        body, in_specs=[_ANY] * n, out_specs=[_ANY] * n, out_shape=[jax.ShapeDtypeStruct((2,) + r.shape, r.dtype) for r in reds],
        scratch_shapes=[pltpu.SemaphoreType.DMA((n,)), pltpu.SemaphoreType.DMA((n,))], name=name,
    )(*reds)


def _pack(arrs, rows_multiple, dtype):
    flat = jnp.concatenate([a.reshape(-1).astype(dtype) for a in arrs])
    unit = rows_multiple * PACK_COLS
    total = -(-flat.shape[0] // unit) * unit
    return jnp.pad(flat, (0, total - flat.shape[0])).reshape(-1, PACK_COLS)


def _unpack(buf, shapes):
    lead = buf.shape[:-2]
    flat = buf.reshape(*lead, -1)
    out, off = [], 0
    for shp in shapes:
        n = 1
        for s in shp:
            n *= s
        out.append(flat[..., off:off + n].reshape(*lead, *shp))
        off += n
    return out


def _join_shards(parts, axis):
    moved = jnp.moveaxis(parts, 0, axis)
    shp = list(moved.shape)
    shp[axis:axis + 2] = [shp[axis] * shp[axis + 1]]
    return moved.reshape(shp)


def _my_shard(full, axis, chip):
    n = full.shape[axis] // N_CHIP
    return lax.dynamic_slice_in_dim(full, chip * n, n, axis)


SMALL = {"norm_g": 2, "ffn_conv_w": 2, "rg_conv_w": 2, "gla_w_alpha": 2, "gla_b_alpha": 1, "gla_norm_g": 1,
         "ada_b": None, "ffn_conv_b": None, "rg_conv_b": None, "rg_ba": None, "rg_bx": None, "rg_lambda": None}
BIG = {"rg_w_in": True, "rg_wa": False, "rg_wx": False, "rg_w_out": False, "ffn_w_up": True, "ffn_w_down": False,
       "gla_w_in": True, "gla_w_out": False}
WEIGHTS = ["ada_w", "ada_b", "norm_g", "ffn_w_up", "ffn_conv_w", "ffn_conv_b", "ffn_w_down", "rg_w_in", "rg_conv_w", "rg_conv_b",
           "rg_wa", "rg_ba", "rg_wx", "rg_bx", "rg_lambda", "rg_w_out", "gla_w_in", "gla_w_alpha", "gla_b_alpha", "gla_norm_g",
           "gla_w_out"]


def kernel(x, c, ada_w, ada_b, norm_g, ffn_w_up, ffn_conv_w, ffn_conv_b, ffn_w_down, rg_w_in, rg_conv_w, rg_conv_b, rg_wa, rg_ba, rg_wx, rg_bx, rg_lambda, rg_w_out, gla_w_in, gla_w_alpha, gla_b_alpha, gla_norm_g, gla_w_out, loss_target, m_ada_w, m_ada_b, m_norm_g, m_ffn_w_up, m_ffn_conv_w, m_ffn_conv_b, m_ffn_w_down, m_rg_w_in, m_rg_conv_w, m_rg_conv_b, m_rg_wa, m_rg_ba, m_rg_wx, m_rg_bx, m_rg_lambda, m_rg_w_out, m_gla_w_in, m_gla_w_alpha, m_gla_b_alpha, m_gla_norm_g, m_gla_w_out, v_ada_w, v_ada_b, v_norm_g, v_ffn_w_up, v_ffn_conv_w, v_ffn_conv_b, v_ffn_w_down, v_rg_w_in, v_rg_conv_w, v_rg_conv_b, v_rg_wa, v_rg_ba, v_rg_wx, v_rg_bx, v_rg_lambda, v_rg_w_out, v_gla_w_in, v_gla_w_alpha, v_gla_b_alpha, v_gla_norm_g, v_gla_w_out):
    wts = dict(ada_w=ada_w, ada_b=ada_b, norm_g=norm_g, ffn_w_up=ffn_w_up, ffn_conv_w=ffn_conv_w, ffn_conv_b=ffn_conv_b,
               ffn_w_down=ffn_w_down, rg_w_in=rg_w_in, rg_conv_w=rg_conv_w, rg_conv_b=rg_conv_b, rg_wa=rg_wa, rg_ba=rg_ba,
               rg_wx=rg_wx, rg_bx=rg_bx, rg_lambda=rg_lambda, rg_w_out=rg_w_out, gla_w_in=gla_w_in, gla_w_alpha=gla_w_alpha,
               gla_b_alpha=gla_b_alpha, gla_norm_g=gla_norm_g, gla_w_out=gla_w_out)
    mom1 = dict(ada_w=m_ada_w, ada_b=m_ada_b, norm_g=m_norm_g, ffn_w_up=m_ffn_w_up, ffn_conv_w=m_ffn_conv_w,
                ffn_conv_b=m_ffn_conv_b, ffn_w_down=m_ffn_w_down, rg_w_in=m_rg_w_in, rg_conv_w=m_rg_conv_w,
                rg_conv_b=m_rg_conv_b, rg_wa=m_rg_wa, rg_ba=m_rg_ba, rg_wx=m_rg_wx, rg_bx=m_rg_bx, rg_lambda=m_rg_lambda,
                rg_w_out=m_rg_w_out, gla_w_in=m_gla_w_in, gla_w_alpha=m_gla_w_alpha, gla_b_alpha=m_gla_b_alpha,
                gla_norm_g=m_gla_norm_g, gla_w_out=m_gla_w_out)
    mom2 = dict(ada_w=v_ada_w, ada_b=v_ada_b, norm_g=v_norm_g, ffn_w_up=v_ffn_w_up, ffn_conv_w=v_ffn_conv_w,
                ffn_conv_b=v_ffn_conv_b, ffn_w_down=v_ffn_w_down, rg_w_in=v_rg_w_in, rg_conv_w=v_rg_conv_w,
                rg_conv_b=v_rg_conv_b, rg_wa=v_rg_wa, rg_ba=v_rg_ba, rg_wx=v_rg_wx, rg_bx=v_rg_bx, rg_lambda=v_rg_lambda,
                rg_w_out=v_rg_w_out, gla_w_in=v_gla_w_in, gla_w_alpha=v_gla_w_alpha, gla_b_alpha=v_gla_b_alpha,
                gla_norm_g=v_gla_norm_g, gla_w_out=v_gla_w_out)
    xi, yi, ci = _place()
    chip, me = 2 * xi + yi, 4 * xi + 2 * yi + ci
    d = x.shape[-1]
    depth = ada_w.shape[0]
    n_ada = ada_w.shape[-1]
    sharded_small = [k for k, ax in SMALL.items() if ax is not None]

    sm = _all_gather_8(_pack([c] + [wts[k] for k in sharded_small], SUBLANES, F32), "gather_small")
    c_all = sm[:, 0, :]
    parts = _unpack(sm[0::2], [c.shape] + [wts[k].shape for k in sharded_small])[1:]
    full = {k: _join_shards(p, SMALL[k]) for k, p in zip(sharded_small, parts)}
    for k, ax in SMALL.items():
        if ax is None:
            full[k] = wts[k]

    c16 = jnp.pad(c_all, ((0, ADA_ROWS - N_DEV), (0, 0)))
    ada_b_mine = lax.dynamic_slice_in_dim(ada_b, chip * n_ada, n_ada, 1)[:, None, :]
    mod_cols = _ada_fwd(c16, ada_w, ada_b_mine, "ada_fwd")
    mod_all = _all_gather_8(mod_cols.reshape(-1, PACK_COLS), "gather_mod")[0::2].reshape(N_CHIP, depth, ADA_ROWS, n_ada)
    mod = jnp.swapaxes(lax.dynamic_index_in_dim(mod_all, me, 2, keepdims=False), 0, 1).reshape(depth, 6, d)

    items = [(k, l) for k in BIG for l in range(wts[k].shape[0])]
    stage_of = lambda k, l: "rg" if k.startswith("rg_") else ("ffn0" if (k.startswith("ffn_") and l == 0) else "l1")
    staged = {st: [it for it in items if stage_of(*it) == st] for st in ("rg", "ffn0", "l1")}
    staged["gla"] = [it for it in staged["l1"] if it[0].startswith("gla_")]
    staged["ffn1"] = [it for it in staged["l1"] if it[0].startswith("ffn_")]
    staged["l1"] = staged["gla"] + staged["ffn1"]
    shard = lambda k, l: wts[k][l].reshape(-1, wts[k].shape[-1]).astype(BF16)
    own = lambda got, mine: [lax.dynamic_update_index_in_dim(g, m, chip, 0) for g, m in zip(got, mine)]
    rows_joined = lambda v: v.reshape(-1, v.shape[-1])

    def placed(its, slots):
        out = {"ffn_w_up": {}, "ffn_w_down": {}}
        for (k, l), v in zip(its, slots):
            if k == "ffn_w_up":
                out[k][l] = v
            elif k == "ffn_w_down":
                out[k][l] = rows_joined(v)
            elif k in ("rg_wa", "rg_wx"):
                out[k] = _slots_to_block_rows(v, RG_BLOCKS)
            elif k == "gla_w_in":
                out[k] = _from_col_slots(v)
            else:
                out[k] = v if BIG[k] else rows_joined(v)
        return out

    after_mod = (mod[0, 0, 0] * 0.0).astype(BF16)
    sh_rg = [shard(k, l) + after_mod for k, l in staged["rg"]]
    local = {k: (v if k in ("norm_g", "ffn_conv_w", "ffn_conv_b") else v[0]) for k, v in full.items()}
    local.update(placed(staged["rg"], own(_gather_chips(sh_rg, "gather_weights_rg"), sh_rg)))
    sh_late, flying = {}, {}
    after_rg = (local["rg_w_out"][0, 0].astype(F32) * 0.0).astype(BF16)
    sh_late["ffn0"] = [shard(k, l) + after_rg for k, l in staged["ffn0"]]
    flying["ffn0"], tok = _send_start(sh_late["ffn0"], "whole", "weights_ffn0_start")
    for stage in ("gla", "ffn1"):
        sh_late[stage] = [shard(k, l) + tok[0, 0].astype(BF16) for k, l in staged[stage]]
        flying[stage], tok = _send_start(sh_late[stage], "whole", f"weights_{stage}_start")
    mod = mod + tok[0, 0]

    def fetch(stage, after):
        mine, got = _send_wait(flying[stage], after, "whole", f"weights_{stage}_wait")
        return placed(staged[stage], own(got, mine))

    c_idx = ci.reshape(1).astype(jnp.int32)
    gslots, paired, psums, sent = {}, {}, {}, {}

    def grad_slots(gr, k, l):
        g = gr[k][l] if k in ("ffn_w_up", "ffn_w_down") else gr[k]
        if k in ("rg_wa", "rg_wx"):
            return _block_rows_to_slots(g)
        if k == "gla_w_in":
            return _col_slots(g)
        return g if BIG[k] else g.reshape(N_CHIP, -1, g.shape[-1])

    def done(stage, gr):
        gslots[stage] = [grad_slots(gr, k, l) for k, l in staged[stage]]
        paired[stage], token = _send_start(gslots[stage], "pair", f"grads_{stage}_pair_start")
        return token[0, 0]

    def later(stage, after):
        mine, theirs = _send_wait(paired[stage], after, "pair", f"grads_{stage}_pair_wait")
        psums[stage] = [_pair_sum(g, t, c_idx, f"grads_pair_sum_{k}{l}") for (k, l), g, t in zip(staged[stage], mine, theirs)]
        sent[stage], token = _send_start(psums[stage], "slots", f"grads_{stage}_start")
        return token[0, 0]

    cols, grad_x, gr = _local_step(x[0], loss_target[0], mod, local, fetch, done, later)
    loss = lax.psum(0.5 * jnp.sum(cols) / d, ("x", "y", "c"))

    small_names = [k for k in SMALL if k != "ada_b"]
    gs = _all_gather_8(_pack([gr[k] for k in small_names] + [gr["mod"]], SUBLANES, F32), "gather_small_grads")
    small_shapes = [full[k].shape for k in small_names] + [(depth, 6 * d)]
    *small_sum, g_ada_b = _unpack(_sum_lead(gs, "sum_small_grads"), small_shapes)
    grads = dict(zip(small_names, small_sum))
    grads["ada_b"] = g_ada_b
    for k in sharded_small:
        grads[k] = _my_shard(grads[k], SMALL[k], chip)
    dmod_all = _unpack(gs, small_shapes)[-1].reshape(N_DEV, depth, N_CHIP, n_ada)
    dmod_mine = jnp.swapaxes(lax.dynamic_index_in_dim(dmod_all, chip, 2, keepdims=False), 0, 1)
    g_ada_w = _ada_bwd(c16, jnp.pad(dmod_mine, ((0, 0), (0, ADA_ROWS - N_DEV), (0, 0))), "ada_bwd")

    gslots["rg"] = [grad_slots(gr, k, l) for k, l in staged["rg"]]
    theirs = _pair_exchange(gslots["rg"], "grads_rg_pair_exchange")
    psums["rg"] = [_pair_sum(g, t, c_idx, f"grads_pair_sum_{k}{l}") for (k, l), g, t in zip(staged["rg"], gslots["rg"], theirs)]
    sent["rg"], rg_sent = _send_start(psums["rg"], "slots", "grads_rg_start")
    chip_idx = chip.reshape(1).astype(jnp.int32)
    delta, new_m, new_v = {}, {}, {}

    def reduce_and_update(stages, after, dep):
        its = [(st, n) for st in stages for n in range(len(staged[st]))]
        back = {st: _send_wait(sent[st], after, "slots", f"grads_{st}_wait") for st in stages}
        halves = [_chip_sum(back[st][1][n], back[st][0][n], chip_idx, "grads_chip_sum_%s%d" % staged[st][n]) for st, n in its]
        shared = _pair_share(halves, "grads_pair_share_" + stages[0])
        reduced = [lax.dynamic_update_index_in_dim(s2, h, ci, 0).reshape(-1, h.shape[-1]) for s2, h in zip(shared, halves)]
        last = None
        for k in BIG:
            gs_k = [g for (st, n), g in zip(its, reduced) if staged[st][n][0] == k]
            if gs_k:
                last = update(k, gs_k, dep)
        return last

    def update(k, gs_k, dep=None):
        shp = wts[k].shape
        if k == "gla_w_in":
            view, back = (lambda a: jnp.swapaxes(a, 1, 2)), (lambda o: jnp.swapaxes(o, 1, 2))
            gs_k = [g.T for g in gs_k]
        else:
            view, back = (lambda a: a.reshape(a.shape[0], -1, a.shape[-1])), (lambda o: o.reshape(shp))
        outs = _adamw(view(wts[k]), gs_k, view(mom1[k]), view(mom2[k]), "adamw_" + k, dep)
        grads[k], delta[k], new_m[k], new_v[k] = (back(o) for o in outs)
        return new_v[k]

    done_late = reduce_and_update(("ffn0", "l1"), grad_x, rg_sent)
    update("ada_w", g_ada_w, rg_sent)
    small_shard_shapes = [wts[k].shape for k in SMALL]
    packed = [_pack([src[k] for k in SMALL], SUBLANES, F32) for src in (wts, grads, mom1, mom2)]
    outs = _adamw(packed[0][None], [packed[1]], packed[2][None], packed[3][None], "adamw_small", rg_sent)
    for dst, o in zip((delta, new_m, new_v), outs[1:]):
        for k, a in zip(SMALL, _unpack(o[0], small_shard_shapes)):
            dst[k] = a
    reduce_and_update(("rg",), done_late, None)

    return (loss, grad_x[None], *[grads[k] for k in WEIGHTS], *[delta[k] for k in WEIGHTS], *[new_m[k] for k in WEIGHTS],
            *[new_v[k] for k in WEIGHTS])
```

```python
import jax
import jax.numpy as jnp
from jax import lax
from jax.experimental import pallas as pl
from jax.experimental.pallas import tpu as pltpu

F32 = jnp.float32
BF16 = jnp.bfloat16
MXU_DTYPE = BF16

EPS = 1e-6
RG_C = 8.0
RG_BLOCKS = 4
RG_CONV = 4
GLA_HEADS = 4
GLA_TAU = 16.0
GLA_CHUNK = 64
GLA_RANK = 16
FFN_CONV = 3
ADAM_LR = 0.001
ADAM_B1 = 0.9
ADAM_B2 = 0.999
ADAM_EPS = 1e-08
ADAM_WD = 0.01
ADAM_STEP = 10

LANES = 128
SUBLANES = 8
VMEM_LIMIT = 56 * 1024 * 1024
CB = 256
MESH = pl.DeviceIdType.MESH
N_DEV = 8
N_CHIP = 4


def _params(*sem):
    return pltpu.CompilerParams(dimension_semantics=sem, vmem_limit_bytes=VMEM_LIMIT)


def _tile(dim, prefs):
    for p in prefs:
        if dim % p == 0:
            return p
    return dim


def _dot(a, b, dims):
    return lax.dot_general(a.astype(MXU_DTYPE), b.astype(MXU_DTYPE), (dims, ((), ())), preferred_element_type=F32)


def _dot_nn(a, b):
    return _dot(a, b, ((1,), (0,)))


def _dot_nt(a, b):
    return _dot(a, b, ((1,), (1,)))


def _dot_tn(a, b):
    return _dot(a, b, ((0,), (0,)))


def _mm(a, b, *, ta=False, tb=False, a_parts=1, b_parts=1, w_slots=1, out_slots=1, out_dtype=F32, tm_max=1408, name):
    if ta:
        k_dim, m_dim = a.shape
        n_dim = b.shape[-1] * b_parts
    else:
        m_dim, k_dim = a.shape[-2], a.shape[-1] * a_parts
        n_dim = b.shape[-2] if tb else b.shape[-1] * w_slots
    n_unit = n_dim // max(b_parts, out_slots, 1 if tb else w_slots)
    k_unit = k_dim // max(a_parts, w_slots if tb else 1)
    tm = _tile(m_dim, tuple(t for t in (1024, 1408, 512, 256, 128) if t <= max(tm_max, 128)))
    tn = _tile(n_unit, (1024, 1408, 896, 512, 256, 128))
    tk = _tile(k_unit, (1024, 1408, 896, 512, 256, 128))
    nk = k_dim // tk
    dims = ((0 if ta else 1,), (1 if tb else 0,))

    def spec(shape, parts, total, tile, col_grid, row_grid):
        per = total // parts // tile

        def index(i, j, k):
            g = {"i": i, "j": j, "k": k}
            col, row = g[col_grid], g[row_grid]
            return (row, col) if parts == 1 else (col // per, row, col % per)

        return pl.BlockSpec(shape if parts == 1 else (None,) + shape, index)

    def body(a_ref, b_ref, o_ref, *acc):
        if nk == 1:
            o_ref[...] = _dot(a_ref[...], b_ref[...], dims).astype(o_ref.dtype)
            return
        acc_ref, k = acc[0], pl.program_id(2)

        @pl.when(k == 0)
        def _():
            acc_ref[...] = jnp.zeros_like(acc_ref)

        acc_ref[...] += _dot(a_ref[...], b_ref[...], dims)

        @pl.when(k == nk - 1)
        def _():
            o_ref[...] = acc_ref[...].astype(o_ref.dtype)

    if ta:
        a_spec = spec((tk, tm), 1, m_dim, tm, "i", "k")
        b_spec = spec((tk, tn), b_parts, n_dim, tn, "j", "k")
    elif tb:
        a_spec = spec((tm, tk), a_parts, k_dim, tk, "k", "i")
        b_spec = spec((tn, tk), w_slots, k_dim, tk, "k", "j")
    else:
        a_spec = spec((tm, tk), a_parts, k_dim, tk, "k", "i")
        b_spec = spec((tk, tn), w_slots, n_dim, tn, "j", "k")
    out_shape = (m_dim, n_dim) if out_slots == 1 else (out_slots, m_dim, n_dim // out_slots)
    return pl.pallas_call(
        body,
        grid=(m_dim // tm, n_dim // tn, nk),
        in_specs=[a_spec, b_spec],
        out_specs=spec((tm, tn), out_slots, n_dim, tn, "j", "i"),
        out_shape=jax.ShapeDtypeStruct(out_shape, out_dtype),
        scratch_shapes=[pltpu.VMEM((tm, tn), F32)] if nk > 1 else [],
        compiler_params=_params("parallel", "parallel", "arbitrary"),
        name=name,
    )(a, b)


ROW_TILES = (1024, 512)


def _row_specs(s, d, ts):
    return pl.BlockSpec((ts, d), lambda i: (i, 0)), pl.BlockSpec((1, d), lambda i: (0, 0))


def _norm_mod_fwd(x, g, sc, sh, name):
    s, d = x.shape
    ts = _tile(s, ROW_TILES)

    def body(x_ref, g_ref, sc_ref, sh_ref, h_ref):
        xv = x_ref[...]
        r = lax.rsqrt(jnp.mean(xv * xv, axis=-1, keepdims=True) + EPS)
        h_ref[...] = (((xv * r) * g_ref[...]) * (1.0 + sc_ref[...]) + sh_ref[...]).astype(h_ref.dtype)

    row, vec = _row_specs(s, d, ts)
    return pl.pallas_call(
        body, grid=(s // ts,), in_specs=[row, vec, vec, vec], out_specs=row,
        out_shape=jax.ShapeDtypeStruct((s, d), MXU_DTYPE), compiler_params=_params("parallel"), name=name,
    )(x, g, sc, sh)


def _norm_mod_bwd(dh, x, g, sc, dres, name):
    s, d = x.shape
    ts = _tile(s, ROW_TILES)

    def body(dh_ref, x_ref, g_ref, sc_ref, dres_ref, dx_ref, dg_ref, dsc_ref, dsh_ref, acc_ref):
        i = pl.program_id(0)

        @pl.when(i == 0)
        def _():
            acc_ref[...] = jnp.zeros_like(acc_ref)

        xv, dhv = x_ref[...], dh_ref[...]
        r = lax.rsqrt(jnp.mean(xv * xv, axis=-1, keepdims=True) + EPS)
        n = xv * r
        acc_ref[0:1, :] += jnp.sum(dhv * n, axis=0, keepdims=True)
        acc_ref[1:2, :] += jnp.sum(dhv, axis=0, keepdims=True)
        dn = dhv * ((1.0 + sc_ref[...]) * g_ref[...])
        dx_ref[...] = dres_ref[...] + r * (dn - n * jnp.mean(dn * n, axis=-1, keepdims=True))
        dg_ref[...] = (1.0 + sc_ref[...]) * acc_ref[0:1, :]
        dsc_ref[...] = g_ref[...] * acc_ref[0:1, :]
        dsh_ref[...] = acc_ref[1:2, :]

    row, vec = _row_specs(s, d, ts)
    vshape = jax.ShapeDtypeStruct((1, d), F32)
    return pl.pallas_call(
        body, grid=(s // ts,), in_specs=[row, row, vec, vec, row], out_specs=[row, vec, vec, vec],
        out_shape=[jax.ShapeDtypeStruct((s, d), F32), vshape, vshape, vshape],
        scratch_shapes=[pltpu.VMEM((SUBLANES, d), F32)], compiler_params=_params("arbitrary"), name=name,
    )(dh, x, g, sc, dres)


def _post_norm_fwd(x, y, g, gt, g2, sc, sh, name):
    s, d = x.shape
    ts = _tile(s, ROW_TILES)

    def body(x_ref, y_ref, g_ref, gt_ref, g2_ref, sc_ref, sh_ref, o_ref, h_ref):
        yv = y_ref[...]
        r = lax.rsqrt(jnp.mean(yv * yv, axis=-1, keepdims=True) + EPS)
        xn = x_ref[...] + gt_ref[...] * ((yv * r) * g_ref[...])
        o_ref[...] = xn
        r2 = lax.rsqrt(jnp.mean(xn * xn, axis=-1, keepdims=True) + EPS)
        h_ref[...] = (((xn * r2) * g2_ref[...]) * (1.0 + sc_ref[...]) + sh_ref[...]).astype(h_ref.dtype)

    row, vec = _row_specs(s, d, ts)
    return pl.pallas_call(
        body, grid=(s // ts,), in_specs=[row, row] + [vec] * 5, out_specs=[row, row],
        out_shape=[jax.ShapeDtypeStruct((s, d), F32), jax.ShapeDtypeStruct((s, d), MXU_DTYPE)],
        compiler_params=_params("parallel"), name=name,
    )(x, y, g, gt, g2, sc, sh)


def _post_bwd(dxn, y, g, gt, name):
    s, d = y.shape
    ts = _tile(s, ROW_TILES)

    def body(dxn_ref, y_ref, g_ref, gt_ref, dy_ref, dg_ref, dgt_ref, acc_ref):
        i = pl.program_id(0)

        @pl.when(i == 0)
        def _():
            acc_ref[...] = jnp.zeros_like(acc_ref)

        yv, dv = y_ref[...], dxn_ref[...]
        r = lax.rsqrt(jnp.mean(yv * yv, axis=-1, keepdims=True) + EPS)
        n = yv * r
        acc_ref[0:1, :] += jnp.sum(dv * n, axis=0, keepdims=True)
        dn = dv * (gt_ref[...] * g_ref[...])
        dy_ref[...] = (r * (dn - n * jnp.mean(dn * n, axis=-1, keepdims=True))).astype(dy_ref.dtype)
        dg_ref[...] = gt_ref[...] * acc_ref[0:1, :]
        dgt_ref[...] = g_ref[...] * acc_ref[0:1, :]

    row, vec = _row_specs(s, d, ts)
    vshape = jax.ShapeDtypeStruct((1, d), F32)
    return pl.pallas_call(
        body, grid=(s // ts,), in_specs=[row, row, vec, vec], out_specs=[row, vec, vec],
        out_shape=[jax.ShapeDtypeStruct((s, d), MXU_DTYPE), vshape, vshape],
        scratch_shapes=[pltpu.VMEM((SUBLANES, d), F32)], compiler_params=_params("arbitrary"), name=name,
    )(dxn, y, g, gt)


def _norm_post_bwd(dh, x, g, sc, dres, y, gp, gt, name):
    s, d = x.shape
    ts = _tile(s, (512,))

    def body(dh_ref, x_ref, g_ref, sc_ref, dres_ref, y_ref, gp_ref, gt_ref,
             dx_ref, dy_ref, dg_ref, dsc_ref, dsh_ref, dgp_ref, dgt_ref, acc_ref):
        i = pl.program_id(0)

        @pl.when(i == 0)
        def _():
            acc_ref[...] = jnp.zeros_like(acc_ref)

        xv, dhv = x_ref[...], dh_ref[...]
        r = lax.rsqrt(jnp.mean(xv * xv, axis=-1, keepdims=True) + EPS)
        n = xv * r
        acc_ref[0:1, :] += jnp.sum(dhv * n, axis=0, keepdims=True)
        acc_ref[1:2, :] += jnp.sum(dhv, axis=0, keepdims=True)
        dn = dhv * ((1.0 + sc_ref[...]) * g_ref[...])
        dx = dres_ref[...] + r * (dn - n * jnp.mean(dn * n, axis=-1, keepdims=True))
        dx_ref[...] = dx
        yv = y_ref[...]
        ry = lax.rsqrt(jnp.mean(yv * yv, axis=-1, keepdims=True) + EPS)
        ny = yv * ry
        acc_ref[2:3, :] += jnp.sum(dx * ny, axis=0, keepdims=True)
        dny = dx * (gt_ref[...] * gp_ref[...])
        dy_ref[...] = (ry * (dny - ny * jnp.mean(dny * ny, axis=-1, keepdims=True))).astype(dy_ref.dtype)
        dg_ref[...] = (1.0 + sc_ref[...]) * acc_ref[0:1, :]
        dsc_ref[...] = g_ref[...] * acc_ref[0:1, :]
        dsh_ref[...] = acc_ref[1:2, :]
        dgp_ref[...] = gt_ref[...] * acc_ref[2:3, :]
        dgt_ref[...] = gp_ref[...] * acc_ref[2:3, :]

    row, vec = _row_specs(s, d, ts)
    vshape = jax.ShapeDtypeStruct((1, d), F32)
    return pl.pallas_call(
        body, grid=(s // ts,), in_specs=[row, row, vec, vec, row, row, vec, vec], out_specs=[row, row] + [vec] * 5,
        out_shape=[jax.ShapeDtypeStruct((s, d), F32), jax.ShapeDtypeStruct((s, d), MXU_DTYPE)] + [vshape] * 5,
        scratch_shapes=[pltpu.VMEM((SUBLANES, d), F32)], compiler_params=_params("arbitrary"), name=name,
    )(dh, x, g, sc, dres, y, gp, gt)


def _post_loss(x, y, g, gt, tgt, name):
    s, d = x.shape
    ts = _tile(s, ROW_TILES)

    def body(x_ref, y_ref, g_ref, gt_ref, t_ref, col_ref, dx_ref):
        i = pl.program_id(0)

        @pl.when(i == 0)
        def _():
            col_ref[...] = jnp.zeros_like(col_ref)

        yv = y_ref[...]
        r = lax.rsqrt(jnp.mean(yv * yv, axis=-1, keepdims=True) + EPS)
        e = (x_ref[...] + gt_ref[...] * ((yv * r) * g_ref[...])) - t_ref[...]
        col_ref[...] += jnp.sum(e * e, axis=0, keepdims=True)
        dx_ref[...] = e * (1.0 / d)

    row, vec = _row_specs(s, d, ts)
    return pl.pallas_call(
        body, grid=(s // ts,), in_specs=[row, row, vec, vec, row], out_specs=[vec, row],
        out_shape=[jax.ShapeDtypeStruct((1, d), F32), jax.ShapeDtypeStruct((s, d), F32)],
        compiler_params=_params("arbitrary"), name=name,
    )(x, y, g, gt, tgt)


_GELU_C = 0.7978845608028654
_GELU_A = 0.044715


def _gelu(x):
    t = jnp.tanh(_GELU_C * (x + _GELU_A * x * x * x))
    return 0.5 * x * (1.0 + t), t


def _gelu_grad(x, t):
    return 0.5 * (1.0 + t) + 0.5 * x * (1.0 - t * t) * (_GELU_C * (1.0 + 3.0 * _GELU_A * x * x))


def _sigmoid(x):
    return 1.0 / (1.0 + jnp.exp(-x))


def _log1p_pos(y):
    u = 1.0 + y
    return jnp.where(u == 1.0, y, jnp.log(u) * (y / jnp.where(u == 1.0, 1.0, u - 1.0)))


def _softplus(x):
    return jnp.maximum(x, 0.0) + _log1p_pos(jnp.exp(-jnp.abs(x)))


def _one_minus_sq_exp(x, ex):
    z = 2.0 * x
    series = -z * (1.0 + z * (1.0 / 2 + z * (1.0 / 6 + z * (1.0 / 24 + z * (1.0 / 120)))))
    return jnp.where(z > -0.05, series, 1.0 - ex * ex)


SLAB = 16


def _cat(a, b):
    return jnp.concatenate([a, b], axis=1)


def _pair_specs(shape, nb, index):
    return [pl.BlockSpec(shape, lambda j, t: index(j, t) + (j,)), pl.BlockSpec(shape, lambda j, t: index(j, t) + (j + nb,))]


def _halo_row(ts, time_of):
    return lambda j, t: (jnp.maximum(time_of(t) * (ts // SUBLANES) - 1, 0),)


def _rows_from(groups, k):
    row = lax.broadcasted_iota(jnp.int32, groups[0].shape, 0)
    turned = [pltpu.roll(g, SUBLANES - k, axis=0) for g in groups]
    return [jnp.where(row < SUBLANES - k, lo, hi) for lo, hi in zip(turned[:-1], turned[1:])]


def _ffn_mid_fwd(p, cw, cb, name):
    s, f2 = p.shape
    ts = _tile(s, (1024, 512))
    nb, nt = f2 // (2 * CB), s // ts
    n_grp = SLAB // SUBLANES

    def body(pg_ref, pv_ref, hg_ref, hv_ref, cwg_ref, cwv_ref, cbg_ref, cbv_ref, a_ref, ga_ref, gb_ref):
        t = pl.program_id(1)
        cwv, bias = _cat(cwg_ref[...], cwv_ref[...]), _cat(cbg_ref[...], cbv_ref[...])
        w0, w1, w2 = cwv[0:1], cwv[1:2], cwv[2:3]

        def slab(before, cur, r0):
            pm2, pm1 = _rows_from([before] + cur, SUBLANES - 2), _rows_from([before] + cur, SUBLANES - 1)
            u = jnp.concatenate([bias + w0 * pm2[i] + w1 * pm1[i] + w2 * cur[i] for i in range(n_grp)], axis=0)
            g, v = u[:, :CB], u[:, CB:]
            gel, th = _gelu(g)
            rows = pl.ds(r0, SLAB)
            a_ref[rows, :] = (gel * v).astype(a_ref.dtype)
            ga_ref[rows, :] = gel.astype(ga_ref.dtype)
            gb_ref[rows, :] = (v * _gelu_grad(g, th)).astype(gb_ref.dtype)

        def pieces(rows):
            blk = _cat(pg_ref[rows, :], pv_ref[rows, :])
            return [blk[i * SUBLANES:(i + 1) * SUBLANES] for i in range(blk.shape[0] // SUBLANES)]

        slab(jnp.where(t > 0, _cat(hg_ref[...], hv_ref[...]), 0.0), pieces(pl.ds(0, SLAB)), 0)

        def loop(i, carry):
            r0 = pl.multiple_of(i * SLAB, SLAB)
            got = pieces(pl.ds(pl.multiple_of(r0 - SUBLANES, SUBLANES), SLAB + SUBLANES))
            slab(got[0], got[1:], r0)
            return carry

        lax.fori_loop(1, ts // SLAB, loop, 0, unroll=2)

    fwd = lambda t: t
    out = pl.BlockSpec((ts, CB), lambda j, t: (t, j))
    shape = jax.ShapeDtypeStruct((s, f2 // 2), MXU_DTYPE)
    return pl.pallas_call(
        body, grid=(nb, nt),
        in_specs=(_pair_specs((ts, CB), nb, lambda j, t: (t,)) + _pair_specs((SUBLANES, CB), nb, _halo_row(ts, fwd))
                  + _pair_specs((FFN_CONV, CB), nb, lambda j, t: (0,)) + _pair_specs((1, CB), nb, lambda j, t: (0,))),
        out_specs=[out, out, out], out_shape=[shape, shape, shape],
        compiler_params=_params("parallel", "arbitrary"), name=name,
    )(p, p, p, p, cw, cw, cb, cb)


def _ffn_mid_bwd(da, p, ga, gb, cw, name):
    s, f2 = p.shape
    ts = _tile(s, (1024, 512))
    nb, nt = f2 // (2 * CB), s // ts
    n_slab = ts // SLAB
    n_grp = SLAB // SUBLANES
    per_trip = 2

    def body(da_ref, ga_ref, gb_ref, pg_ref, pv_ref, cwg_ref, cwv_ref, dp_ref, dcw_ref, dcb_ref, next_du, acc):
        tt = pl.program_id(1)
        cwv = _cat(cwg_ref[...], cwv_ref[...])
        w0, w1, w2 = cwv[0:1], cwv[1:2], cwv[2:3]

        @pl.when(tt == 0)
        def _():
            next_du[...] = jnp.zeros_like(next_du)
            acc[...] = jnp.zeros_like(acc)

        def slab(r0, after, sums):
            rows = pl.ds(r0, SLAB)
            dav = da_ref[rows, :]
            du = _cat(dav * gb_ref[rows, :].astype(F32), dav * ga_ref[rows, :].astype(F32))
            p0 = _cat(pg_ref[rows, :], pv_ref[rows, :])
            cur = [du[i * SUBLANES:(i + 1) * SUBLANES] for i in range(n_grp)]
            du1, du2 = _rows_from(cur + [after], 1), _rows_from(cur + [after], 2)
            dpv = jnp.concatenate([w2 * cur[i] + w1 * du1[i] + w0 * du2[i] for i in range(n_grp)], axis=0).astype(dp_ref.dtype)
            dp_ref[0, rows, :] = dpv[:, :CB]
            dp_ref[1, rows, :] = dpv[:, CB:]
            for i in range(n_grp):
                pi = p0[i * SUBLANES:(i + 1) * SUBLANES]
                parts = (cur[i], du2[i] * pi, du1[i] * pi, cur[i] * pi)
                sums = parts if sums is None else tuple(x + y for x, y in zip(sums, parts))
            return cur[0], sums

        def loop(k, after):
            sums = None
            for j in range(per_trip):
                r0 = pl.multiple_of((n_slab - 1 - (k * per_trip + j)) * SLAB, SLAB)
                after, sums = slab(r0, after, sums)
            for q, part in enumerate(sums):
                acc[q] += part
            return after

        next_du[...] = lax.fori_loop(0, n_slab // per_trip, loop, next_du[...])

        @pl.when(tt == nt - 1)
        def _():
            for half in range(2):
                cols = slice(half * CB, (half + 1) * CB)
                dcb_ref[half] = jnp.sum(acc[0][:, cols], axis=0, keepdims=True)
                for k in range(FFN_CONV):
                    dcw_ref[half, k:k + 1, :] = jnp.sum(acc[1 + k][:, cols], axis=0, keepdims=True)

    rev = lambda t: nt - 1 - t
    tile = pl.BlockSpec((ts, CB), lambda j, t: (rev(t), j))
    return pl.pallas_call(
        body, grid=(nb, nt),
        in_specs=([tile, tile, tile] + _pair_specs((ts, CB), nb, lambda j, t: (rev(t),))
                  + _pair_specs((FFN_CONV, CB), nb, lambda j, t: (0,))),
        out_specs=[pl.BlockSpec((2, ts, CB), lambda j, t: (0, rev(t), j)),
                   pl.BlockSpec((2, FFN_CONV, CB), lambda j, t: (0, 0, j)),
                   pl.BlockSpec((2, 1, CB), lambda j, t: (0, 0, j))],
        out_shape=[jax.ShapeDtypeStruct((2, s, f2 // 2), MXU_DTYPE), jax.ShapeDtypeStruct((2, FFN_CONV, f2 // 2), F32),
                   jax.ShapeDtypeStruct((2, 1, f2 // 2), F32)],
        scratch_shapes=[pltpu.VMEM((SUBLANES, 2 * CB), F32), pltpu.VMEM((1 + FFN_CONV, SUBLANES, 2 * CB), F32)],
        compiler_params=_params("parallel", "arbitrary"), name=name,
    )(da, ga, gb, p, p, cw, cw)


def _rg_gates(xc, wa_ref, ba_ref, wx_ref, bx_ref, lam_ref):
    r = _sigmoid(_dot_nn(xc, wa_ref[0]) + ba_ref[...])
    ig = _sigmoid(_dot_nn(xc, wx_ref[0]) + bx_ref[...])
    sp = _softplus(-lam_ref[...])
    log_a = (-RG_C) * r * sp
    a = jnp.exp(log_a)
    mult = jnp.sqrt(_one_minus_sq_exp(log_a, a))
    return r, ig, sp, a, mult


def _rg_conv(scr, cw_ref, cb_ref, ts):
    views = [scr[5 + k:5 + k + ts, :] for k in range(RG_CONV)]
    xc = cb_ref[...]
    for k in range(RG_CONV):
        xc = xc + cw_ref[k:k + 1, :] * views[k]
    return xc, views


def _rg_param_specs():
    vec = pl.BlockSpec((1, CB), lambda g, t: (0, g))
    mat = pl.BlockSpec((1, CB, CB), lambda g, t: (g, 0, 0))
    return [pl.BlockSpec((RG_CONV, CB), lambda g, t: (0, g)), vec, mat, vec, mat, vec, vec]


def _scan_rows(a_scr, x_scr, out_ref, carry, ts, reverse):
    n = ts // SUBLANES
    row = lax.broadcasted_iota(jnp.int32, (SUBLANES, a_scr.shape[1]), 0)
    last = SUBLANES - 1

    def rows_of(k):
        return pl.ds(pl.multiple_of(k * SUBLANES, SUBLANES), SUBLANES)

    def local(k, _):
        rows = rows_of(k)
        a, x = a_scr[rows, :], x_scr[rows, :]
        if reverse:
            a = jnp.where(row == last, 1.0, pltpu.roll(a, last, axis=0))
            for sh in (1, 2, 4):
                keep = row < SUBLANES - sh
                x = x + a * jnp.where(keep, pltpu.roll(x, SUBLANES - sh, axis=0), 0.0)
                a = a * jnp.where(keep, pltpu.roll(a, SUBLANES - sh, axis=0), 1.0)
        else:
            for sh in (1, 2, 4):
                keep = row >= sh
                x = a * jnp.where(keep, pltpu.roll(x, sh, axis=0), 0.0) + x
                a = a * jnp.where(keep, pltpu.roll(a, sh, axis=0), 1.0)
        out_ref[rows, :] = x
        x_scr[rows, :] = a
        return 0

    lax.fori_loop(0, n, local, 0, unroll=4)

    def chain(k, c):
        rows = rows_of(n - 1 - k if reverse else k)
        v = out_ref[rows, :] + x_scr[rows, :] * c
        out_ref[rows, :] = v
        return a_scr[rows, :][0:1] * v[0:1] if reverse else v[last:last + 1]

    return lax.fori_loop(0, n, chain, carry, unroll=4)


def _rg_mid_fwd(pj, cw, cb, wa, ba, wx, bx, lam, name):
    s = pj.shape[0]
    nb = pj.shape[1] // (2 * CB)
    ts = _tile(s, (512,))
    nt = s // ts

    def body(gate_ref, x_ref, halo_ref, cw_ref, cb_ref, wa_ref, ba_ref, wx_ref, bx_ref, lam_ref, y_ref, hs_ref,
             scr, a_scr, u_scr, h_scr):
        t = pl.program_id(1)

        @pl.when(t == 0)
        def _():
            h_scr[...] = jnp.zeros_like(h_scr)

        scr[0:SUBLANES, :] = jnp.where(t > 0, halo_ref[...], 0.0)
        scr[SUBLANES:, :] = x_ref[...]
        xc, _ = _rg_conv(scr, cw_ref, cb_ref, ts)
        _, ig, _, a, mult = _rg_gates(xc, wa_ref, ba_ref, wx_ref, bx_ref, lam_ref)
        a_scr[...] = a
        u_scr[...] = mult * (ig * xc)
        h_scr[0:1, :] = _scan_rows(a_scr, u_scr, hs_ref, h_scr[0:1, :], ts, False)
        y_ref[...] = (_gelu(gate_ref[...])[0] * hs_ref[...]).astype(y_ref.dtype)

    blk = pl.BlockSpec((ts, CB), lambda g, t: (t, g))
    return pl.pallas_call(
        body, grid=(nb, nt),
        in_specs=_pair_specs((ts, CB), nb, lambda g, t: (t,))
        + [pl.BlockSpec((SUBLANES, CB), lambda g, t: _halo_row(ts, lambda u: u)(g, t) + (g + nb,))] + _rg_param_specs(),
        out_specs=[blk, blk],
        out_shape=[jax.ShapeDtypeStruct((s, nb * CB), MXU_DTYPE), jax.ShapeDtypeStruct((s, nb * CB), F32)],
        scratch_shapes=[pltpu.VMEM((ts + SUBLANES, CB), F32), pltpu.VMEM((ts, CB), F32), pltpu.VMEM((ts, CB), F32),
                        pltpu.VMEM((SUBLANES, CB), F32)],
        compiler_params=_params("parallel", "arbitrary"), name=name,
    )(pj, pj, pj, cw, cb, wa, ba, wx, bx, lam)


def _rg_mid_bwd(dy, pj, hs, cw, cb, wa, ba, wx, bx, lam, name):
    s = pj.shape[0]
    nb = pj.shape[1] // (2 * CB)
    ts = _tile(s, (512,))
    nt = s // ts

    def body(dy_ref, gate_ref, x_ref, halo_ref, hs_ref, hsh_ref, cw_ref, cb_ref, wa_ref, ba_ref, wx_ref, bx_ref, lam_ref,
             dpj_ref, dcw_ref, dcb_ref, dwa_ref, dba_ref, dwx_ref, dbx_ref, dlam_ref,
             scr, hscr, a_scr, d_scr, g_scr, dxscr, c_scr):
        tt = pl.program_id(1)
        t = nt - 1 - tt

        @pl.when(tt == 0)
        def _():
            c_scr[...] = jnp.zeros_like(c_scr)
            dxscr[ts:, :] = jnp.zeros((SUBLANES, CB), F32)
            for ref in (dcw_ref, dcb_ref, dwa_ref, dba_ref, dwx_ref, dbx_ref, dlam_ref):
                ref[...] = jnp.zeros_like(ref)

        scr[0:SUBLANES, :] = jnp.where(t > 0, halo_ref[...], 0.0)
        scr[SUBLANES:, :] = x_ref[...]
        hscr[0:SUBLANES, :] = jnp.where(t > 0, hsh_ref[...], 0.0)
        hscr[SUBLANES:, :] = hs_ref[...]
        xc, views = _rg_conv(scr, cw_ref, cb_ref, ts)
        r, ig, sp, a, mult = _rg_gates(xc, wa_ref, ba_ref, wx_ref, bx_ref, lam_ref)
        gate = gate_ref[...]
        gel, th = _gelu(gate)
        dyv = dy_ref[...]
        dpj_ref[0] = (dyv * hs_ref[...] * _gelu_grad(gate, th)).astype(dpj_ref.dtype)
        a_scr[...] = a
        d_scr[...] = dyv * gel
        c_scr[0:1, :] = _scan_rows(a_scr, d_scr, g_scr, c_scr[0:1, :], ts, True)
        du = g_scr[...]
        da = du * hscr[7:7 + ts, :]
        dmult = du * (ig * xc)
        dig = du * (mult * xc)
        dxc = du * (mult * ig)
        dlog_a = da * a - dmult * (a * a / mult)
        dlam_ref[...] += jnp.sum(dlog_a * r, axis=0, keepdims=True) * (RG_C * _sigmoid(-lam_ref[...]))
        dpr = dlog_a * ((-RG_C) * sp) * (r * (1.0 - r))
        dpi = dig * (ig * (1.0 - ig))
        dba_ref[...] += jnp.sum(dpr, axis=0, keepdims=True)
        dbx_ref[...] += jnp.sum(dpi, axis=0, keepdims=True)
        dwa_ref[0] += _dot_tn(xc, dpr)
        dwx_ref[0] += _dot_tn(xc, dpi)
        dxc = dxc + _dot_nt(dpr, wa_ref[0]) + _dot_nt(dpi, wx_ref[0])
        dcb_ref[...] += jnp.sum(dxc, axis=0, keepdims=True)
        for k in range(RG_CONV):
            dcw_ref[k:k + 1, :] += jnp.sum(dxc * views[k], axis=0, keepdims=True)
        dxscr[0:ts, :] = dxc
        dxp = cw_ref[3:4, :] * dxc
        for k in range(RG_CONV - 1):
            dxp = dxp + cw_ref[k:k + 1, :] * dxscr[3 - k:3 - k + ts, :]
        dpj_ref[1] = dxp.astype(dpj_ref.dtype)
        dxscr[ts:, :] = dxscr[0:SUBLANES, :]

    rev = lambda g, t: (nt - 1 - t, g)
    rev_halo = lambda g, t: (jnp.maximum((nt - 1 - t) * (ts // SUBLANES) - 1, 0), g)
    vec = pl.BlockSpec((1, CB), lambda g, t: (0, g))
    mat = pl.BlockSpec((1, CB, CB), lambda g, t: (g, 0, 0))
    d = nb * CB
    vshape = jax.ShapeDtypeStruct((1, d), F32)
    mshape = jax.ShapeDtypeStruct((nb, CB, CB), F32)
    return pl.pallas_call(
        body, grid=(nb, nt),
        in_specs=[pl.BlockSpec((ts, CB), rev)] + _pair_specs((ts, CB), nb, lambda g, t: (nt - 1 - t,))
        + [pl.BlockSpec((SUBLANES, CB), lambda g, t: (rev_halo(g, t)[0], g + nb)),
           pl.BlockSpec((ts, CB), rev), pl.BlockSpec((SUBLANES, CB), rev_halo)] + _rg_param_specs(),
        out_specs=[pl.BlockSpec((2, ts, CB), lambda g, t: (0, nt - 1 - t, g)), pl.BlockSpec((RG_CONV, CB), lambda g, t: (0, g)),
                   vec, mat, vec, mat, vec, vec],
        out_shape=[jax.ShapeDtypeStruct((2, s, d), MXU_DTYPE), jax.ShapeDtypeStruct((RG_CONV, d), F32), vshape, mshape, vshape,
                   mshape, vshape, vshape],
        scratch_shapes=[pltpu.VMEM((ts + SUBLANES, CB), F32), pltpu.VMEM((ts + SUBLANES, CB), F32), pltpu.VMEM((ts, CB), F32),
                        pltpu.VMEM((ts, CB), F32), pltpu.VMEM((ts, CB), F32), pltpu.VMEM((ts + SUBLANES, CB), F32),
                        pltpu.VMEM((SUBLANES, CB), F32)],
        compiler_params=_params("parallel", "arbitrary"), name=name,
    )(dy, pj, pj, pj, hs, hs, cw, cb, wa, ba, wx, bx, lam)


GLA_DK = 128
GLA_DV = 256
GLA_O_K = GLA_HEADS * GLA_DK
GLA_O_V = 2 * GLA_HEADS * GLA_DK
GLA_O_R = GLA_O_V + GLA_HEADS * GLA_DV
GLA_O_Z = GLA_O_R + GLA_HEADS * GLA_DV
GLA_IN = GLA_O_Z + GLA_RANK
GLA_TS = 256


def _dk(h, base=0):
    return slice(base + h * GLA_DK, base + (h + 1) * GLA_DK)


def _dv(h, base=0):
    return slice(base + h * GLA_DV, base + (h + 1) * GLA_DV)


def _split3(x):
    hi = x.astype(BF16)
    r1 = x - hi.astype(F32)
    mid = r1.astype(BF16)
    lo = (r1 - mid.astype(F32)).astype(BF16)
    return hi, mid, lo


def _chunk_cumsum(x, reverse):
    n = x.shape[0]
    i = lax.broadcasted_iota(jnp.int32, (n, n), 0)
    j = lax.broadcasted_iota(jnp.int32, (n, n), 1)
    same = (i // GLA_CHUNK) == (j // GLA_CHUNK)
    tri = jnp.where(same & ((j >= i) if reverse else (j <= i)), 1.0, 0.0).astype(BF16)
    out = jnp.zeros(x.shape, F32)
    for piece in _split3(x):
        out = out + lax.dot_general(tri, piece, (((1,), (0,)), ((), ())), preferred_element_type=F32)
    return out


def _gla_head(pj_ref, h):
    return (pj_ref[:, _dk(h)] * (GLA_DK ** -0.5), pj_ref[:, _dk(h, GLA_O_K)], pj_ref[:, _dv(h, GLA_O_V)],
            pj_ref[:, _dv(h, GLA_O_R)])


def _gla_decays(gc):
    gref = gc[GLA_CHUNK // 2:GLA_CHUNK // 2 + 1, :]
    glast = gc[GLA_CHUNK - 1:GLA_CHUNK, :]
    return jnp.exp(gc), jnp.exp(gc - gref), jnp.exp(gref - gc), jnp.exp(glast - gc), jnp.exp(glast)


def _causal_mask():
    i = lax.broadcasted_iota(jnp.int32, (GLA_CHUNK, GLA_CHUNK), 0)
    j = lax.broadcasted_iota(jnp.int32, (GLA_CHUNK, GLA_CHUNK), 1)
    return j <= i


def _log_sigmoid(x):
    return jnp.minimum(x, 0.0) - _log1p_pos(jnp.exp(-jnp.abs(x)))


def _gla_mid_fwd(pj, wal, bal, ng, name):
    s, nh = pj.shape[0], GLA_HEADS
    ts = _tile(s, (GLA_TS,))
    nt, nc = s // ts, ts // GLA_CHUNK

    def body(pj_ref, wal_ref, bal_ref, ng_ref, act_ref, o_ref, st_ref, s_scr):
        t = pl.program_id(0)

        @pl.when(t == 0)
        def _():
            s_scr[...] = jnp.zeros_like(s_scr)

        heads = []
        z = pj_ref[:, GLA_O_Z:]
        for h in range(nh):
            q, k, v, r = _gla_head(pj_ref, h)
            g = _log_sigmoid(_dot_nn(z, wal_ref[:, _dk(h)]) + bal_ref[:, _dk(h)]) * (1.0 / GLA_TAU)
            heads.append((q, k, v, r, _chunk_cumsum(g, False)))
        mask = _causal_mask()
        for c in range(nc):
            sl = slice(c * GLA_CHUNK, (c + 1) * GLA_CHUNK)
            for h, (q, k, v, r, gcum) in enumerate(heads):
                eg, eq, ek, ekd, egl = _gla_decays(gcum[sl])
                st = s_scr[h]
                st_ref[c, h] = st
                attn = jnp.where(mask, _dot_nt(q[sl] * eq, k[sl] * ek), 0.0)
                o_ref[sl, h * GLA_DV:(h + 1) * GLA_DV] = _dot_nt(q[sl] * eg, st) + _dot_nn(attn, v[sl])
                s_scr[h] = st * egl + _dot_tn(v[sl], k[sl] * ekd)
        for h, (q, k, v, r, gcum) in enumerate(heads):
            cols = slice(h * GLA_DV, (h + 1) * GLA_DV)
            o = o_ref[:, cols]
            on = o * lax.rsqrt(jnp.mean(o * o, axis=-1, keepdims=True) + EPS)
            act_ref[:, cols] = ((on * ng_ref[...]) * (r * _sigmoid(r))).astype(act_ref.dtype)

    blk = pl.BlockSpec((ts, nh * GLA_DV), lambda t: (t, 0))
    whole = lambda shape: pl.BlockSpec(shape, lambda t: (0,) * len(shape))
    return pl.pallas_call(
        body, grid=(nt,),
        in_specs=[pl.BlockSpec((ts, GLA_IN), lambda t: (t, 0)), whole((GLA_RANK, nh * GLA_DK)), whole((1, nh * GLA_DK)),
                  whole((1, GLA_DV))],
        out_specs=[blk, blk, pl.BlockSpec((nc, nh, GLA_DV, GLA_DK), lambda t: (t, 0, 0, 0))],
        out_shape=[jax.ShapeDtypeStruct((s, nh * GLA_DV), MXU_DTYPE), jax.ShapeDtypeStruct((s, nh * GLA_DV), F32),
                   jax.ShapeDtypeStruct((s // GLA_CHUNK, nh, GLA_DV, GLA_DK), F32)],
        scratch_shapes=[pltpu.VMEM((nh, GLA_DV, GLA_DK), F32)],
        compiler_params=_params("arbitrary"), name=name,
    )(pj, wal, bal, ng)


def _gla_mid_bwd(dact, pj, o, st, wal, bal, ng, name):
    s, nh = pj.shape[0], GLA_HEADS
    ts = _tile(s, (GLA_TS,))
    nt, nc = s // ts, ts // GLA_CHUNK

    def body(dact_ref, pj_ref, o_ref, st_ref, wal_ref, bal_ref, ng_ref, dpj_ref, dwal_ref, dbal_ref, dng_ref,
             ds_scr, dg_scr):
        tt = pl.program_id(0)

        @pl.when(tt == 0)
        def _():
            ds_scr[...] = jnp.zeros_like(ds_scr)
            dwal_ref[...] = jnp.zeros_like(dwal_ref)
            dbal_ref[...] = jnp.zeros_like(dbal_ref)
            dng_ref[...] = jnp.zeros_like(dng_ref)

        heads = []
        z = pj_ref[:, GLA_O_Z:]
        for h in range(nh):
            q, k, v, r = _gla_head(pj_ref, h)
            logit = _dot_nn(z, wal_ref[:, _dk(h)]) + bal_ref[:, _dk(h)]
            gcum = _chunk_cumsum(_log_sigmoid(logit) * (1.0 / GLA_TAU), False)
            ov = o_ref[:, h * GLA_DV:(h + 1) * GLA_DV]
            ro = lax.rsqrt(jnp.mean(ov * ov, axis=-1, keepdims=True) + EPS)
            on = ov * ro
            sg = _sigmoid(r)
            sil = r * sg
            dav = dact_ref[:, h * GLA_DV:(h + 1) * GLA_DV]
            dpj_ref[:, _dv(h, GLA_O_R)] = (dav * (on * ng_ref[...]) * (sg + sil * (1.0 - sg))).astype(dpj_ref.dtype)
            t1 = dav * sil
            dng_ref[...] += jnp.sum(t1 * on, axis=0, keepdims=True)
            dn = t1 * ng_ref[...]
            do = ro * (dn - on * jnp.mean(dn * on, axis=-1, keepdims=True))
            heads.append((q, k, v, logit, gcum, do))
        mask = _causal_mask()
        scale = GLA_DK ** -0.5
        last_row = lax.broadcasted_iota(jnp.int32, (GLA_CHUNK, GLA_DK), 0) == GLA_CHUNK - 1
        for c in reversed(range(nc)):
            sl = slice(c * GLA_CHUNK, (c + 1) * GLA_CHUNK)
            for h, (q, k, v, logit, gcum, do) in enumerate(heads):
                eg, eq, ek, ekd, egl = _gla_decays(gcum[sl])
                qc, kc, vc, doc = q[sl], k[sl], v[sl], do[sl]
                qg, qt, kt, kd = qc * eg, qc * eq, kc * ek, kc * ekd
                sp = st_ref[c, h]
                ds = ds_scr[h]
                attn = jnp.where(mask, _dot_nt(qt, kt), 0.0)
                dattn = jnp.where(mask, _dot_nt(doc, vc), 0.0)
                dqg = _dot_nn(doc, sp)
                dqt = _dot_nn(dattn, kt)
                dkt = _dot_tn(dattn, qt)
                dkd = _dot_nn(vc, ds)
                dpj_ref[sl, _dv(h, GLA_O_V)] = (_dot_tn(attn, doc) + _dot_nt(kd, ds)).astype(dpj_ref.dtype)
                dpj_ref[sl, _dk(h)] = (scale * (dqg * eg + dqt * eq)).astype(dpj_ref.dtype)
                dpj_ref[sl, _dk(h, GLA_O_K)] = (dkt * ek + dkd * ekd).astype(dpj_ref.dtype)
                kdd = dkd * kd
                dgl = jnp.sum(kdd, axis=0, keepdims=True) + jnp.sum(ds * sp, axis=0, keepdims=True) * egl
                dg_scr[h, sl, :] = dqg * qg + dqt * qt - dkt * kt - kdd + jnp.where(last_row, dgl, 0.0)
                ds_scr[h] = ds * egl + _dot_tn(doc, qg)
        dz = jnp.zeros((ts, GLA_RANK), F32)
        for h, (q, k, v, logit, gcum, do) in enumerate(heads):
            dlogit = _chunk_cumsum(dg_scr[h], True) * (1.0 / GLA_TAU) * _sigmoid(-logit)
            dz = dz + _dot_nt(dlogit, wal_ref[:, _dk(h)])
            dwal_ref[:, _dk(h)] += _dot_tn(z, dlogit)
            dbal_ref[:, _dk(h)] += jnp.sum(dlogit, axis=0, keepdims=True)
        dpj_ref[:, GLA_O_Z:] = dz.astype(dpj_ref.dtype)

    rev = lambda t: (nt - 1 - t, 0)
    whole = lambda shape: pl.BlockSpec(shape, lambda t: (0,) * len(shape))
    wide = pl.BlockSpec((ts, nh * GLA_DV), rev)
    return pl.pallas_call(
        body, grid=(nt,),
        in_specs=[wide, pl.BlockSpec((ts, GLA_IN), rev), wide,
                  pl.BlockSpec((nc, nh, GLA_DV, GLA_DK), lambda t: (nt - 1 - t, 0, 0, 0)),
                  whole((GLA_RANK, nh * GLA_DK)), whole((1, nh * GLA_DK)), whole((1, GLA_DV))],
        out_specs=[pl.BlockSpec((ts, GLA_IN), rev), whole((GLA_RANK, nh * GLA_DK)), whole((1, nh * GLA_DK)), whole((1, GLA_DV))],
        out_shape=[jax.ShapeDtypeStruct((s, GLA_IN), MXU_DTYPE), jax.ShapeDtypeStruct((GLA_RANK, nh * GLA_DK), F32),
                   jax.ShapeDtypeStruct((1, nh * GLA_DK), F32), jax.ShapeDtypeStruct((1, GLA_DV), F32)],
        scratch_shapes=[pltpu.VMEM((nh, GLA_DV, GLA_DK), F32), pltpu.VMEM((nh, ts, GLA_DK), F32)],
        compiler_params=_params("arbitrary"), name=name,
    )(dact, pj, o, st, wal, bal, ng)


def _adamw(w, gs, m, v, name, after=None):
    layers, rows, cols = w.shape
    gs = list(gs) if isinstance(gs, (list, tuple)) else gs
    n_g = len(gs) if isinstance(gs, list) else 1
    if rows % SUBLANES == 0:
        tr, tc = _tile(rows, (256, 128, 64, 32, 16, 8)), cols
    else:
        tr, tc = rows, _tile(cols, (256, 128))
    c1 = 1.0 / (1.0 - ADAM_B1 ** ADAM_STEP)
    c2 = 1.0 / (1.0 - ADAM_B2 ** ADAM_STEP)

    def body(*refs):
        g_refs, (w_ref, m_ref, v_ref) = refs[:n_g], refs[n_g:n_g + 3]
        go_ref, d_ref, mo_ref, vo_ref = refs[-4:]
        gv = g_refs[0][...]
        for l in range(1, n_g):
            gv = jnp.where(pl.program_id(0) == l, g_refs[l][...], gv)
        m2 = ADAM_B1 * m_ref[...] + (1.0 - ADAM_B1) * gv
        v2 = ADAM_B2 * v_ref[...] + (1.0 - ADAM_B2) * (gv * gv)
        d_ref[...] = (-ADAM_LR) * ((m2 * c1) / (jnp.sqrt(v2 * c2) + ADAM_EPS) + ADAM_WD * w_ref[...])
        go_ref[...] = gv
        mo_ref[...] = m2
        vo_ref[...] = v2

    spec = pl.BlockSpec((None, tr, tc), lambda l, i, j: (l, i, j))
    g_specs = [pl.BlockSpec((tr, tc), lambda l, i, j: (i, j))] * n_g if isinstance(gs, list) else [spec]
    extra = [] if after is None else [(after, _ANY)]
    shape = jax.ShapeDtypeStruct((layers, rows, cols), F32)
    return pl.pallas_call(
        body, grid=(layers, rows // tr, cols // tc), in_specs=g_specs + [spec] * 3 + [sp for _, sp in extra],
        out_specs=[spec] * 4, out_shape=[shape] * 4, compiler_params=_params("parallel", "parallel", "parallel"), name=name,
    )(*(gs if isinstance(gs, list) else [gs]), w, m, v, *[a for a, _ in extra])


def _col_slots(w):
    r, c = w.shape
    return jnp.moveaxis(w.reshape(r, N_CHIP, c // N_CHIP), 1, 0)


def _from_col_slots(w):
    n, r, c = w.shape
    return jnp.moveaxis(w, 0, 1).reshape(r, n * c)


def _block_rows_to_slots(w):
    g, r4, cc = w.shape
    return jnp.swapaxes(w.reshape(g, N_CHIP, r4 // N_CHIP, cc), 0, 1).reshape(N_CHIP, g * (r4 // N_CHIP), cc)


def _slots_to_block_rows(w, g):
    n, gr, cc = w.shape
    return jnp.swapaxes(w.reshape(n, g, gr // g, cc), 0, 1).reshape(g, n * (gr // g), cc)


def _local_step(x, tgt, mod, w, fetch=None, done=None, later=None):
    depth = mod.shape[0]
    row = lambda v: v.reshape(1, -1)
    w = dict(w)
    w["ffn_w_up"], w["ffn_w_down"] = dict(enumerate(w["ffn_w_up"])), dict(enumerate(w["ffn_w_down"]))

    def arrive(stage, after):
        if fetch is not None:
            for k, v in fetch(stage, after).items():
                if isinstance(v, dict):
                    w[k].update(v)
                else:
                    w[k] = v

    saved = []
    for i in range(depth):
        if i == 1:
            arrive("gla", x)
        sh_m, sc_m, gt_m, sh_f, sc_f, gt_f = (mod[i, j:j + 1] for j in range(6))
        g0, g1, g2, g3 = (w["norm_g"][i, j:j + 1] for j in range(4))
        tag = f"_l{i}"
        if i == 0:
            h = _norm_mod_fwd(x, g0, sc_m, sh_m, "norm_mix" + tag)
        if i % 2 == 0:
            pj = _mm(h, w["rg_w_in"], w_slots=N_CHIP, name="rg_in" + tag)
            act, aux = _rg_mid_fwd(pj, w["rg_conv_w"], row(w["rg_conv_b"]), w["rg_wa"], row(w["rg_ba"]), w["rg_wx"],
                                   row(w["rg_bx"]), row(w["rg_lambda"]), "rg_mid" + tag)
            y = _mm(act, w["rg_w_out"], name="rg_out" + tag)
        else:
            pj = _mm(h, w["gla_w_in"], tm_max=512, name="gla_in" + tag)
            act, *aux = _gla_mid_fwd(pj, w["gla_w_alpha"], row(w["gla_b_alpha"]), row(w["gla_norm_g"]), "gla_mid" + tag)
            y = _mm(act, w["gla_w_out"], name="gla_out" + tag)
        x1, h2 = _post_norm_fwd(x, y, g1, gt_m, g2, sc_f, sh_f, "post_mix" + tag)
        arrive(f"ffn{i}", x1)
        p = _mm(h2, w["ffn_w_up"][i], w_slots=N_CHIP, name="ffn_up" + tag)
        a, ga, gb = _ffn_mid_fwd(p, w["ffn_conv_w"][i], w["ffn_conv_b"][i:i + 1], "ffn_mid" + tag)
        y2 = _mm(a, w["ffn_w_down"][i], name="ffn_down" + tag)
        saved_h = h
        if i + 1 < depth:
            nxt = [mod[i + 1, j:j + 1] for j in range(2)] + [w["norm_g"][i + 1, 0:1]]
            x2, h = _post_norm_fwd(x1, y2, g3, gt_f, nxt[2], nxt[1], nxt[0], "post_ffn" + tag)
        else:
            x2 = None
            cols, dx = _post_loss(x1, y2, g3, gt_f, tgt, "post_ffn_loss")
        saved.append((x, saved_h, pj, act, aux, y, x1, h2, p, (a, ga, gb), y2))
        x = x2

    stacked = ("norm_g", "ffn_conv_w", "ffn_conv_b", "mod")
    gr = {k: [None] * depth for k in stacked + ("ffn_w_up", "ffn_w_down")}
    told = lambda stage: done(stage, gr) if done is not None else 0.0
    told_later = lambda stage, after: later(stage, after) if later is not None else 0.0
    for i in reversed(range(depth)):
        x0, h, pj, act, aux, y, x1, h2, p, (a, ga, gb), y2 = saved[i]
        sh_m, sc_m, gt_m, sh_f, sc_f, gt_f = (mod[i, j:j + 1] for j in range(6))
        g0, g1, g2, g3 = (w["norm_g"][i, j:j + 1] for j in range(4))
        tag = f"_l{i}"
        if i == depth - 1:
            dy2, d_g3, d_gt_f = _post_bwd(dx, y2, g3, gt_f, "post_ffn_b" + tag)
        else:
            dy2, d_g3, d_gt_f = ahead
        da = _mm(dy2, w["ffn_w_down"][i], tb=True, name="ffn_down_dx" + tag)
        gr["ffn_w_down"][i] = _mm(a, dy2, ta=True, name="ffn_down_dw" + tag)
        conv_w = w["ffn_conv_w"][i] + (told_later("l1", da) if i == 0 else 0.0)
        dp, dcw, dcb = _ffn_mid_bwd(da, p, ga, gb, conv_w, "ffn_mid_b" + tag)
        gr["ffn_conv_w"][i], gr["ffn_conv_b"][i] = _cat(dcw[0], dcw[1]), _cat(dcb[0], dcb[1])[0]
        dh2 = _mm(dp, w["ffn_w_up"][i], tb=True, a_parts=2, w_slots=N_CHIP, name="ffn_up_dx" + tag)
        gr["ffn_w_up"][i] = _mm(h2, dp, ta=True, b_parts=2, out_slots=N_CHIP, name="ffn_up_dw" + tag)
        if i == 0:
            gt_m = gt_m + told("ffn0")
        dx1, dy, d_g2, d_sc_f, d_sh_f, d_g1, d_gt_m = _norm_post_bwd(dh2, x1, g2, sc_f, dx, y, g1, gt_m, "norm_ffn_b" + tag)
        if i % 2 == 0:
            dact = _mm(dy, w["rg_w_out"], tb=True, name="rg_out_dx" + tag)
            gr["rg_w_out"] = _mm(act, dy, ta=True, name="rg_out_dw" + tag)
            lam = row(w["rg_lambda"]) + told_later("ffn0", gr["rg_w_out"])
            dpj, gr["rg_conv_w"], d_cb, gr["rg_wa"], d_ba, gr["rg_wx"], d_bx, d_lam = _rg_mid_bwd(
                dact, pj, aux, w["rg_conv_w"], row(w["rg_conv_b"]), w["rg_wa"], row(w["rg_ba"]), w["rg_wx"],
                row(w["rg_bx"]), lam, "rg_mid_b" + tag)
            gr["rg_conv_b"], gr["rg_ba"], gr["rg_bx"], gr["rg_lambda"] = d_cb[0], d_ba[0], d_bx[0], d_lam[0]
            dh = _mm(dpj, w["rg_w_in"], tb=True, a_parts=2, w_slots=N_CHIP, name="rg_in_dx" + tag)
            gr["rg_w_in"] = _mm(h, dpj, ta=True, b_parts=2, out_slots=N_CHIP, name="rg_in_dw" + tag)
            sc_m = sc_m + told("rg")
        else:
            dact = _mm(dy, w["gla_w_out"], tb=True, name="gla_out_dx" + tag)
            gr["gla_w_out"] = _mm(act, dy, ta=True, name="gla_out_dw" + tag)
            dpj, gr["gla_w_alpha"], d_bal, d_ng = _gla_mid_bwd(dact, pj, aux[0], aux[1], w["gla_w_alpha"], row(w["gla_b_alpha"]),
                                                               row(w["gla_norm_g"]), "gla_mid_b" + tag)
            gr["gla_b_alpha"], gr["gla_norm_g"] = d_bal[0], d_ng[0]
            dh = _mm(dpj, w["gla_w_in"], tb=True, name="gla_in_dx" + tag)
            gr["gla_w_in"] = _mm(h, dpj, ta=True, tm_max=512, name="gla_in_dw" + tag)
            mod = mod.at[0].add(told("l1"))
        if i > 0:
            dx, dy_below, d_g0, d_sc_m, d_sh_m, d_g_below, d_gt_below = _norm_post_bwd(
                dh, x0, g0, sc_m, dx1, saved[i - 1][-1], w["norm_g"][i - 1, 3:4], mod[i - 1, 5:6], "norm_mix_b" + tag)
            ahead = (dy_below, d_g_below, d_gt_below)
        else:
            dx, d_g0, d_sc_m, d_sh_m = _norm_mod_bwd(dh, x0, g0, sc_m, dx1, "norm_mix_b" + tag)
        gr["norm_g"][i] = jnp.concatenate([d_g0, d_g1, d_g2, d_g3], axis=0)
        gr["mod"][i] = jnp.concatenate([d_sh_m, d_sc_m, d_gt_m, d_sh_f, d_sc_f, d_gt_f], axis=0)
    for k in stacked:
        gr[k] = jnp.stack(gr[k])
    return cols, dx, gr


ADA_ROWS = 16


def _ada_fwd(c16, ada_w, ada_b, name):
    depth, d, n = ada_w.shape
    tn = _tile(n, (512, 256, 128))

    def body(c_ref, w_ref, b_ref, o_ref):
        cv = c_ref[...]
        o_ref[0] = _dot_nn(cv * _sigmoid(cv), w_ref[0]) + b_ref[0]

    return pl.pallas_call(
        body, grid=(depth, n // tn),
        in_specs=[pl.BlockSpec((ADA_ROWS, d), lambda l, j: (0, 0)), pl.BlockSpec((1, d, tn), lambda l, j: (l, 0, j)),
                  pl.BlockSpec((1, 1, tn), lambda l, j: (l, 0, j))],
        out_specs=pl.BlockSpec((1, ADA_ROWS, tn), lambda l, j: (l, 0, j)),
        out_shape=jax.ShapeDtypeStruct((depth, ADA_ROWS, n), F32),
        compiler_params=_params("parallel", "parallel"), name=name,
    )(c16, ada_w, ada_b)


def _ada_bwd(c16, dmod16, name):
    depth, _, n = dmod16.shape
    d = c16.shape[1]
    tn = _tile(n, (512, 256, 128))

    def body(c_ref, dm_ref, o_ref):
        cv = c_ref[...]
        o_ref[0] = _dot_tn(cv * _sigmoid(cv), dm_ref[0])

    return pl.pallas_call(
        body, grid=(depth, n // tn),
        in_specs=[pl.BlockSpec((ADA_ROWS, d), lambda l, j: (0, 0)), pl.BlockSpec((1, ADA_ROWS, tn), lambda l, j: (l, 0, j))],
        out_specs=pl.BlockSpec((1, d, tn), lambda l, j: (l, 0, j)),
        out_shape=jax.ShapeDtypeStruct((depth, d, n), F32),
        compiler_params=_params("parallel", "parallel"), name=name,
    )(c16, dmod16)


PACK_COLS = 1024
_ANY = pl.BlockSpec(memory_space=pl.ANY)
_VMEM = pl.BlockSpec(memory_space=pltpu.VMEM)


def _place():
    return lax.axis_index("x"), lax.axis_index("y"), lax.axis_index("c")


def _other_chips(x, y):
    return [(1 - x, y), (x, 1 - y), (1 - x, 1 - y)]


def _rcopy(src, dst, send_sems, recv_sems, k, peer):
    return pltpu.make_async_remote_copy(src_ref=src, dst_ref=dst, send_sem=send_sems.at[k], recv_sem=recv_sems.at[k],
                                        device_id=peer, device_id_type=MESH)


def _all_gather_8(v, name):
    r, cc = v.shape

    def body(v_ref, out_ref, send_sems, recv_sems, local_sem):
        x, y, c = _place()
        me = 4 * x + 2 * y + c
        mine = pltpu.make_async_copy(v_ref, out_ref.at[me], local_sem)
        mine.start()
        peers = []
        for k in range(1, N_DEV):
            px = 1 - x if k & 4 else x
            py = 1 - y if k & 2 else y
            pc = 1 - c if k & 1 else c
            peers.append((px, py, pc))
        sends = [_rcopy(v_ref, out_ref.at[me], send_sems, recv_sems, k, p) for k, p in enumerate(peers)]
        for cp in sends:
            cp.start()
        for k, (px, py, pc) in enumerate(peers):
            _rcopy(v_ref, out_ref.at[4 * px + 2 * py + pc], send_sems, recv_sems, k, (px, py, pc)).wait_recv()
        for cp in sends:
            cp.wait_send()
        mine.wait()

    return pl.pallas_call(
        body, in_specs=[_VMEM], out_specs=_VMEM, out_shape=jax.ShapeDtypeStruct((N_DEV, r, cc), v.dtype),
        scratch_shapes=[pltpu.SemaphoreType.DMA((N_DEV - 1,)), pltpu.SemaphoreType.DMA((N_DEV - 1,)), pltpu.SemaphoreType.DMA],
        compiler_params=pltpu.CompilerParams(vmem_limit_bytes=VMEM_LIMIT), name=name,
    )(v)


def _gather_chips(shards, name):
    n = len(shards)
    per = 2 * (N_CHIP - 1)

    def body(*refs):
        ins, outs, (send_sems, recv_sems) = refs[:n], refs[n:2 * n], refs[2 * n:]
        x, y, c = _place()
        chip = 2 * x + y
        chips = _other_chips(x, y)
        rows = [(pl.ds(c * (r.shape[0] // 2), r.shape[0] // 2), pl.ds((1 - c) * (r.shape[0] // 2), r.shape[0] // 2)) for r in ins]
        first = [_rcopy(ins[i].at[rows[i][0]], outs[i].at[chip, rows[i][0]], send_sems, recv_sems, per * i + j, (px, py, c))
                 for i in range(n) for j, (px, py) in enumerate(chips)]
        for cp in first:
            cp.start()
        passed = []
        for i in range(n):
            for j, (px, py) in enumerate(chips):
                landed = outs[i].at[2 * px + py, rows[i][0]]
                _rcopy(ins[i].at[rows[i][0]], landed, send_sems, recv_sems, per * i + j, (px, py, c)).wait_recv()
                fw = _rcopy(landed, landed, send_sems, recv_sems, per * i + N_CHIP - 1 + j, (x, y, 1 - c))
                fw.start()
                passed.append(fw)
        for i in range(n):
            for j, (px, py) in enumerate(chips):
                landed = outs[i].at[2 * px + py, rows[i][1]]
                _rcopy(landed, landed, send_sems, recv_sems, per * i + N_CHIP - 1 + j, (x, y, 1 - c)).wait_recv()
        for cp in first + passed:
            cp.wait_send()

    return pl.pallas_call(
        body, in_specs=[_ANY] * n, out_specs=[_ANY] * n,
        out_shape=[jax.ShapeDtypeStruct((N_CHIP,) + sh.shape, sh.dtype) for sh in shards],
        scratch_shapes=[pltpu.SemaphoreType.DMA((per * n,)), pltpu.SemaphoreType.DMA((per * n,))], name=name,
    )(*shards)


def _pair_exchange(gs, name):
    n = len(gs)

    def body(*refs):
        ins, outs, (send_sems, recv_sems) = refs[:n], refs[n:2 * n], refs[2 * n:]
        x, y, c = _place()
        copies = []
        for i in range(n):
            half = ins[i].shape[1] // 2
            copies.append(_rcopy(ins[i].at[:, pl.ds((1 - c) * half, half)], outs[i], send_sems, recv_sems, i, (x, y, 1 - c)))
        for cp in copies:
            cp.start()
        for cp in copies:
            cp.wait()

    return pl.pallas_call(
        body, in_specs=[_ANY] * n, out_specs=[_ANY] * n,
        out_shape=[jax.ShapeDtypeStruct((g.shape[0], g.shape[1] // 2, g.shape[2]), g.dtype) for g in gs],
        scratch_shapes=[pltpu.SemaphoreType.DMA((n,)), pltpu.SemaphoreType.DMA((n,))], name=name,
    )(*gs)


_ROW_TILES = (640, 512, 352, 256, 128, 64, 32, 16)


def _pair_sum(g, other, c_idx, name):
    n, half, cc = other.shape
    tr = _tile(half, _ROW_TILES)

    def body(c_ref, g_ref, o_ref, out_ref):
        out_ref[...] = (g_ref[...] + o_ref[...]).astype(out_ref.dtype)

    return pl.pallas_call(
        body,
        grid_spec=pltpu.PrefetchScalarGridSpec(
            num_scalar_prefetch=1, grid=(n, half // tr),
            in_specs=[pl.BlockSpec((None, None, tr, cc), lambda k, i, c_ref: (k, c_ref[0], i, 0)),
                      pl.BlockSpec((None, tr, cc), lambda k, i, c_ref: (k, i, 0))],
            out_specs=pl.BlockSpec((None, tr, cc), lambda k, i, c_ref: (k, i, 0))),
        out_shape=jax.ShapeDtypeStruct((n, half, cc), BF16),
        compiler_params=_params("parallel", "parallel"), name=name,
    )(c_idx, g.reshape(n, 2, half, cc), other)


def _chip_exchange(ps, name):
    n = len(ps)
    per = N_CHIP - 1

    def body(*refs):
        ins, outs, (send_sems, recv_sems) = refs[:n], refs[n:2 * n], refs[2 * n:]
        x, y, c = _place()
        chip = 2 * x + y
        chips = _other_chips(x, y)
        sends = [_rcopy(ins[i].at[2 * px + py], outs[i].at[chip], send_sems, recv_sems, per * i + j, (px, py, c))
                 for i in range(n) for j, (px, py) in enumerate(chips)]
        for cp in sends:
            cp.start()
        for i in range(n):
            for j, (px, py) in enumerate(chips):
                _rcopy(ins[i].at[chip], outs[i].at[2 * px + py], send_sems, recv_sems, per * i + j, (px, py, c)).wait_recv()
        for cp in sends:
            cp.wait_send()

    return pl.pallas_call(
        body, in_specs=[_ANY] * n, out_specs=[_ANY] * n, out_shape=[jax.ShapeDtypeStruct(p.shape, p.dtype) for p in ps],
        scratch_shapes=[pltpu.SemaphoreType.DMA((per * n,)), pltpu.SemaphoreType.DMA((per * n,))], name=name,
    )(*ps)


_HBM = pl.BlockSpec(memory_space=pltpu.HBM)
_SEM = pl.BlockSpec(memory_space=pltpu.SEMAPHORE)
_DATAFLOW = pltpu.SideEffectType.DATAFLOW_SIDE_EFFECTING


def _split_copies(srcs, lands, send_sems, recv_sems, mode, arriving):
    x, y, c = _place()
    chip = 2 * x + y
    out = []
    for i, (src, land) in enumerate(zip(srcs, lands)):
        if mode == "all":
            for k in range(1, N_DEV):
                px, py, pc = (1 - x if k & 4 else x), (1 - y if k & 2 else y), (1 - c if k & 1 else c)
                slot = 4 * px + 2 * py + pc if arriving else 2 * chip + c
                out.append(_rcopy(src, land.at[slot], send_sems, recv_sems, (N_DEV - 1) * i + k - 1, (px, py, pc)))
            continue
        if mode == "pair":
            half = src.shape[1] // 2
            out.append(_rcopy(src.at[:, pl.ds((1 - c) * half, half)], land, send_sems, recv_sems, i, (x, y, 1 - c)))
            continue
        for j, (px, py) in enumerate(_other_chips(x, y)):
            there = 2 * px + py
            part = src.at[there] if mode == "slots" else src
            out.append(_rcopy(part, land.at[there if arriving else chip], send_sems, recv_sems, (N_CHIP - 1) * i + j, (px, py, c)))
    return out


def _land_shape(src, mode):
    if mode == "pair":
        return (src.shape[0], src.shape[1] // 2, src.shape[2])
    if mode == "all":
        return (N_DEV,) + src.shape
    return (N_CHIP,) + (src.shape[1:] if mode == "slots" else src.shape)


def _send_start(srcs, mode, name):
    n = len(srcs)
    n_sem = {"pair": 1, "all": N_DEV - 1}.get(mode, N_CHIP - 1) * n
    lands = [lax.empty(_land_shape(s, mode), s.dtype) for s in srcs]

    def body(*refs):
        ins, zones, (send_sems, recv_sems) = refs[:n], refs[n:2 * n], refs[2 * n:2 * n + 2]
        for cp in _split_copies(ins, zones, send_sems, recv_sems, mode, False):
            cp.start()
        refs[-1][...] = jnp.zeros_like(refs[-1])

    hbm = lambda a: pltpu.HBM(a.shape, a.dtype)
    outs = pl.pallas_call(
        body, name=name, in_specs=[_HBM] * (2 * n),
        out_shape=(pltpu.SemaphoreType.DMA((n_sem,)), pltpu.SemaphoreType.DMA((n_sem,)), *[hbm(a) for a in srcs],
                   *[hbm(a) for a in lands], jax.ShapeDtypeStruct((SUBLANES, LANES), F32)),
        out_specs=(_SEM, _SEM, *[_HBM] * (2 * n), _VMEM), input_output_aliases={i: 2 + i for i in range(2 * n)},
        compiler_params=pltpu.CompilerParams(has_side_effects=_DATAFLOW),
    )(*[pltpu.with_memory_space_constraint(a, pltpu.HBM) for a in list(srcs) + lands])
    return (outs[0], outs[1], list(outs[2:2 + n]), list(outs[2 + n:2 + 2 * n])), outs[-1]


def _send_wait(state, after, mode, name):
    send_sems, recv_sems, srcs, lands = state
    n = len(srcs)

    def body(*refs):
        ins, zones, (send_s, recv_s) = refs[:n], refs[n:2 * n], refs[2 * n:2 * n + 2]
        for cp in _split_copies(ins, zones, send_s, recv_s, mode, True):
            cp.wait_send()
            cp.wait_recv()

    hbm = lambda a: pltpu.HBM(a.shape, a.dtype)
    outs = pl.pallas_call(
        body, name=name, in_specs=[_HBM] * (2 * n) + [_SEM, _SEM, _ANY],
        out_shape=tuple(hbm(a) for a in srcs + lands), out_specs=tuple([_HBM] * (2 * n)),
        input_output_aliases={i: i for i in range(2 * n)},
        compiler_params=pltpu.CompilerParams(has_side_effects=_DATAFLOW),
    )(*srcs, *lands, send_sems, recv_sems, after)
    return list(outs[:n]), list(outs[n:])


def _sum_lead(v, name):
    n, r, cc = v.shape
    tr = _tile(r, _ROW_TILES + (8,))

    def body(v_ref, o_ref):
        acc = v_ref[0].astype(F32)
        for k in range(1, n):
            acc = acc + v_ref[k].astype(F32)
        o_ref[...] = acc

    return pl.pallas_call(
        body, grid=(r // tr,), in_specs=[pl.BlockSpec((n, tr, cc), lambda i: (0, i, 0))],
        out_specs=pl.BlockSpec((tr, cc), lambda i: (i, 0)), out_shape=jax.ShapeDtypeStruct((r, cc), F32),
        compiler_params=_params("parallel"), name=name,
    )(v)


def _chip_sum(arrived, mine, chip_idx, name):
    n, r, cc = arrived.shape
    tr = _tile(r, _ROW_TILES)

    def body(chip_ref, a_ref, m_ref, o_ref):
        acc = jnp.zeros((tr, cc), F32)
        for k in range(n):
            acc = acc + jnp.where(chip_ref[0] == k, m_ref[...], a_ref[k]).astype(F32)
        o_ref[...] = acc

    return pl.pallas_call(
        body,
        grid_spec=pltpu.PrefetchScalarGridSpec(
            num_scalar_prefetch=1, grid=(r // tr,),
            in_specs=[pl.BlockSpec((n, tr, cc), lambda i, chip_ref: (0, i, 0)),
                      pl.BlockSpec((None, tr, cc), lambda i, chip_ref: (chip_ref[0], i, 0))],
            out_specs=pl.BlockSpec((tr, cc), lambda i, chip_ref: (i, 0))),
        out_shape=jax.ShapeDtypeStruct((r, cc), F32), compiler_params=_params("parallel"), name=name,
    )(chip_idx, arrived, mine)


def _pair_share(reds, name):
    n = len(reds)

    def body(*refs):
        ins, outs, (send_sems, recv_sems) = refs[:n], refs[n:2 * n], refs[2 * n:]
        x, y, c = _place()
        copies = [_rcopy(ins[i], outs[i].at[c], send_sems, recv_sems, i, (x, y, 1 - c)) for i in range(n)]
        for cp in copies:
            cp.start()
        for i in range(n):
            _rcopy(ins[i], outs[i].at[1 - c], send_sems, recv_sems, i, (x, y, 1 - c)).wait_recv()
        for cp in copies:
            cp.wait_send()

    return pl.pallas_call(
        body, in_specs=[_ANY] * n, out_specs=[_ANY] * n, out_shape=[jax.ShapeDtypeStruct((2,) + r.shape, r.dtype) for r in reds],
        scratch_shapes=[pltpu.SemaphoreType.DMA((n,)), pltpu.SemaphoreType.DMA((n,))], name=name,
    )(*reds)


def _pack(arrs, rows_multiple, dtype):
    flat = jnp.concatenate([a.reshape(-1).astype(dtype) for a in arrs])
    unit = rows_multiple * PACK_COLS
    total = -(-flat.shape[0] // unit) * unit
    return jnp.pad(flat, (0, total - flat.shape[0])).reshape(-1, PACK_COLS)


def _unpack(buf, shapes):
    lead = buf.shape[:-2]
    flat = buf.reshape(*lead, -1)
    out, off = [], 0
    for shp in shapes:
        n = 1
        for s in shp:
            n *= s
        out.append(flat[..., off:off + n].reshape(*lead, *shp))
        off += n
    return out


def _join_shards(parts, axis):
    moved = jnp.moveaxis(parts, 0, axis)
    shp = list(moved.shape)
    shp[axis:axis + 2] = [shp[axis] * shp[axis + 1]]
    return moved.reshape(shp)


def _my_shard(full, axis, chip):
    n = full.shape[axis] // N_CHIP
    return lax.dynamic_slice_in_dim(full, chip * n, n, axis)


SMALL = {"norm_g": 2, "ffn_conv_w": 2, "rg_conv_w": 2, "gla_w_alpha": 2, "gla_b_alpha": 1, "gla_norm_g": 1,
         "ada_b": None, "ffn_conv_b": None, "rg_conv_b": None, "rg_ba": None, "rg_bx": None, "rg_lambda": None}
BIG = {"rg_w_in": True, "rg_wa": False, "rg_wx": False, "rg_w_out": False, "ffn_w_up": True, "ffn_w_down": False,
       "gla_w_in": True, "gla_w_out": False}
WEIGHTS = ["ada_w", "ada_b", "norm_g", "ffn_w_up", "ffn_conv_w", "ffn_conv_b", "ffn_w_down", "rg_w_in", "rg_conv_w", "rg_conv_b",
           "rg_wa", "rg_ba", "rg_wx", "rg_bx", "rg_lambda", "rg_w_out", "gla_w_in", "gla_w_alpha", "gla_b_alpha", "gla_norm_g",
           "gla_w_out"]


def kernel(x, c, ada_w, ada_b, norm_g, ffn_w_up, ffn_conv_w, ffn_conv_b, ffn_w_down, rg_w_in, rg_conv_w, rg_conv_b, rg_wa, rg_ba, rg_wx, rg_bx, rg_lambda, rg_w_out, gla_w_in, gla_w_alpha, gla_b_alpha, gla_norm_g, gla_w_out, loss_target, m_ada_w, m_ada_b, m_norm_g, m_ffn_w_up, m_ffn_conv_w, m_ffn_conv_b, m_ffn_w_down, m_rg_w_in, m_rg_conv_w, m_rg_conv_b, m_rg_wa, m_rg_ba, m_rg_wx, m_rg_bx, m_rg_lambda, m_rg_w_out, m_gla_w_in, m_gla_w_alpha, m_gla_b_alpha, m_gla_norm_g, m_gla_w_out, v_ada_w, v_ada_b, v_norm_g, v_ffn_w_up, v_ffn_conv_w, v_ffn_conv_b, v_ffn_w_down, v_rg_w_in, v_rg_conv_w, v_rg_conv_b, v_rg_wa, v_rg_ba, v_rg_wx, v_rg_bx, v_rg_lambda, v_rg_w_out, v_gla_w_in, v_gla_w_alpha, v_gla_b_alpha, v_gla_norm_g, v_gla_w_out):
    wts = dict(ada_w=ada_w, ada_b=ada_b, norm_g=norm_g, ffn_w_up=ffn_w_up, ffn_conv_w=ffn_conv_w, ffn_conv_b=ffn_conv_b,
               ffn_w_down=ffn_w_down, rg_w_in=rg_w_in, rg_conv_w=rg_conv_w, rg_conv_b=rg_conv_b, rg_wa=rg_wa, rg_ba=rg_ba,
               rg_wx=rg_wx, rg_bx=rg_bx, rg_lambda=rg_lambda, rg_w_out=rg_w_out, gla_w_in=gla_w_in, gla_w_alpha=gla_w_alpha,
               gla_b_alpha=gla_b_alpha, gla_norm_g=gla_norm_g, gla_w_out=gla_w_out)
    mom1 = dict(ada_w=m_ada_w, ada_b=m_ada_b, norm_g=m_norm_g, ffn_w_up=m_ffn_w_up, ffn_conv_w=m_ffn_conv_w,
                ffn_conv_b=m_ffn_conv_b, ffn_w_down=m_ffn_w_down, rg_w_in=m_rg_w_in, rg_conv_w=m_rg_conv_w,
                rg_conv_b=m_rg_conv_b, rg_wa=m_rg_wa, rg_ba=m_rg_ba, rg_wx=m_rg_wx, rg_bx=m_rg_bx, rg_lambda=m_rg_lambda,
                rg_w_out=m_rg_w_out, gla_w_in=m_gla_w_in, gla_w_alpha=m_gla_w_alpha, gla_b_alpha=m_gla_b_alpha,
                gla_norm_g=m_gla_norm_g, gla_w_out=m_gla_w_out)
    mom2 = dict(ada_w=v_ada_w, ada_b=v_ada_b, norm_g=v_norm_g, ffn_w_up=v_ffn_w_up, ffn_conv_w=v_ffn_conv_w,
                ffn_conv_b=v_ffn_conv_b, ffn_w_down=v_ffn_w_down, rg_w_in=v_rg_w_in, rg_conv_w=v_rg_conv_w,
                rg_conv_b=v_rg_conv_b, rg_wa=v_rg_wa, rg_ba=v_rg_ba, rg_wx=v_rg_wx, rg_bx=v_rg_bx, rg_lambda=v_rg_lambda,
                rg_w_out=v_rg_w_out, gla_w_in=v_gla_w_in, gla_w_alpha=v_gla_w_alpha, gla_b_alpha=v_gla_b_alpha,
                gla_norm_g=v_gla_norm_g, gla_w_out=v_gla_w_out)
    xi, yi, ci = _place()
    chip, me = 2 * xi + yi, 4 * xi + 2 * yi + ci
    d = x.shape[-1]
    depth = ada_w.shape[0]
    n_ada = ada_w.shape[-1]
    sharded_small = [k for k, ax in SMALL.items() if ax is not None]

    sm = _all_gather_8(_pack([c] + [wts[k] for k in sharded_small], SUBLANES, F32), "gather_small")
    c_all = sm[:, 0, :]
    parts = _unpack(sm[0::2], [c.shape] + [wts[k].shape for k in sharded_small])[1:]
    full = {k: _join_shards(p, SMALL[k]) for k, p in zip(sharded_small, parts)}
    for k, ax in SMALL.items():
        if ax is None:
            full[k] = wts[k]

    c16 = jnp.pad(c_all, ((0, ADA_ROWS - N_DEV), (0, 0)))
    ada_b_mine = lax.dynamic_slice_in_dim(ada_b, chip * n_ada, n_ada, 1)[:, None, :]
    mod_cols = _ada_fwd(c16, ada_w, ada_b_mine, "ada_fwd")
    mod_all = _all_gather_8(mod_cols.reshape(-1, PACK_COLS), "gather_mod")[0::2].reshape(N_CHIP, depth, ADA_ROWS, n_ada)
    mod = jnp.swapaxes(lax.dynamic_index_in_dim(mod_all, me, 2, keepdims=False), 0, 1).reshape(depth, 6, d)

    items = [(k, l) for k in BIG for l in range(wts[k].shape[0])]
    stage_of = lambda k, l: "rg" if k.startswith("rg_") else ("ffn0" if (k.startswith("ffn_") and l == 0) else "l1")
    staged = {st: [it for it in items if stage_of(*it) == st] for st in ("rg", "ffn0", "l1")}
    staged["gla"] = [it for it in staged["l1"] if it[0].startswith("gla_")]
    staged["ffn1"] = [it for it in staged["l1"] if it[0].startswith("ffn_")]
    staged["l1"] = staged["gla"] + staged["ffn1"]
    shard = lambda k, l: wts[k][l].reshape(-1, wts[k].shape[-1]).astype(BF16)
    own = lambda got, mine: [lax.dynamic_update_index_in_dim(g, m, chip, 0) for g, m in zip(got, mine)]
    rows_joined = lambda v: v.reshape(-1, v.shape[-1])

    def placed(its, slots):
        out = {"ffn_w_up": {}, "ffn_w_down": {}}
        for (k, l), v in zip(its, slots):
            if k == "ffn_w_up":
                out[k][l] = v
            elif k == "ffn_w_down":
                out[k][l] = rows_joined(v)
            elif k in ("rg_wa", "rg_wx"):
                out[k] = _slots_to_block_rows(v, RG_BLOCKS)
            elif k == "gla_w_in":
                out[k] = _from_col_slots(v)
            else:
                out[k] = v if BIG[k] else rows_joined(v)
        return out

    after_mod = (mod[0, 0, 0] * 0.0).astype(BF16)
    sh_rg = [shard(k, l) + after_mod for k, l in staged["rg"]]
    local = {k: (v if k in ("norm_g", "ffn_conv_w", "ffn_conv_b") else v[0]) for k, v in full.items()}
    local.update(placed(staged["rg"], own(_gather_chips(sh_rg, "gather_weights_rg"), sh_rg)))
    sh_late, flying = {}, {}
    after_rg = (local["rg_w_out"][0, 0].astype(F32) * 0.0).astype(BF16)
    sh_late["ffn0"] = [shard(k, l) + after_rg for k, l in staged["ffn0"]]
    flying["ffn0"], tok = _send_start(sh_late["ffn0"], "whole", "weights_ffn0_start")
    for stage in ("gla", "ffn1"):
        sh_late[stage] = [shard(k, l) + tok[0, 0].astype(BF16) for k, l in staged[stage]]
        flying[stage], tok = _send_start(sh_late[stage], "whole", f"weights_{stage}_start")
    mod = mod + tok[0, 0]

    def fetch(stage, after):
        mine, got = _send_wait(flying[stage], after, "whole", f"weights_{stage}_wait")
        return placed(staged[stage], own(got, mine))

    c_idx = ci.reshape(1).astype(jnp.int32)
    gslots, paired, psums, sent, started = {}, {}, {}, {}, {}

    def grad_slots(gr, k, l):
        g = gr[k][l] if k in ("ffn_w_up", "ffn_w_down") else gr[k]
        if k in ("rg_wa", "rg_wx"):
            return _block_rows_to_slots(g)
        if k == "gla_w_in":
            return _col_slots(g)
        return g if BIG[k] else g.reshape(N_CHIP, -1, g.shape[-1])

    def done(stage, gr):
        gslots[stage] = [grad_slots(gr, k, l) for k, l in staged[stage]]
        paired[stage], token = _send_start(gslots[stage], "pair", f"grads_{stage}_pair_start")
        return token[0, 0]

    def later(stage, after):
        mine, theirs = _send_wait(paired[stage], after, "pair", f"grads_{stage}_pair_wait")
        psums[stage] = [_pair_sum(g, t, c_idx, f"grads_pair_sum_{k}{l}") for (k, l), g, t in zip(staged[stage], mine, theirs)]
        sent[stage], started[stage] = _send_start(psums[stage], "slots", f"grads_{stage}_start")
        return started[stage][0, 0]

    cols, grad_x, gr = _local_step(x[0], loss_target[0], mod, local, fetch, done, later)
    loss = lax.psum(0.5 * jnp.sum(cols) / d, ("x", "y", "c"))

    small_names = [k for k in SMALL if k != "ada_b"]
    small_flying, small_sent = _send_start([_pack([gr[k] for k in small_names] + [gr["mod"]], SUBLANES, F32)], "all",
                                           "grads_small_start")
    small_shapes = [full[k].shape for k in small_names] + [(depth, 6 * d)]
    chip_idx = chip.reshape(1).astype(jnp.int32)
    delta, new_m, new_v = {}, {}, {}
    grads = {}

    def reduce_and_update(stages, after, dep):
        its = [(st, n) for st in stages for n in range(len(staged[st]))]
        back = {st: _send_wait(sent[st], after, "slots", f"grads_{st}_wait") for st in stages}
        halves = [_chip_sum(back[st][1][n], back[st][0][n], chip_idx, "grads_chip_sum_%s%d" % staged[st][n]) for st, n in its]
        shared = _pair_share(halves, "grads_pair_share_" + stages[0])
        reduced = [lax.dynamic_update_index_in_dim(s2, h, ci, 0).reshape(-1, h.shape[-1]) for s2, h in zip(shared, halves)]
        last = None
        for k in BIG:
            gs_k = [g for (st, n), g in zip(its, reduced) if staged[st][n][0] == k]
            if gs_k:
                last = update(k, gs_k, dep)
        return last

    def update(k, gs_k, dep=None):
        shp = wts[k].shape
        if k == "gla_w_in":
            view, back = (lambda a: jnp.swapaxes(a, 1, 2)), (lambda o: jnp.swapaxes(o, 1, 2))
            gs_k = [g.T for g in gs_k]
        else:
            view, back = (lambda a: a.reshape(a.shape[0], -1, a.shape[-1])), (lambda o: o.reshape(shp))
        outs = _adamw(view(wts[k]), gs_k, view(mom1[k]), view(mom2[k]), "adamw_" + k, dep)
        grads[k], delta[k], new_m[k], new_v[k] = (back(o) for o in outs)
        return new_v[k]

    later("rg", small_sent)
    done_late = reduce_and_update(("ffn0", "l1"), grad_x, started["rg"])
    (small_mine,), (gs,) = _send_wait(small_flying, done_late, "all", "grads_small_wait")
    gs = lax.dynamic_update_index_in_dim(gs, small_mine, 2 * chip + ci, 0)
    *small_sum, g_ada_b = _unpack(_sum_lead(gs, "sum_small_grads"), small_shapes)
    grads.update(zip(small_names, small_sum))
    grads["ada_b"] = g_ada_b
    for k in sharded_small:
        grads[k] = _my_shard(grads[k], SMALL[k], chip)
    dmod_all = _unpack(gs, small_shapes)[-1].reshape(N_DEV, depth, N_CHIP, n_ada)
    dmod_mine = jnp.swapaxes(lax.dynamic_index_in_dim(dmod_all, chip, 2, keepdims=False), 0, 1)
    g_ada_w = _ada_bwd(c16, jnp.pad(dmod_mine, ((0, 0), (0, ADA_ROWS - N_DEV), (0, 0))), "ada_bwd")
    update("ada_w", g_ada_w)
    small_shard_shapes = [wts[k].shape for k in SMALL]
    packed = [_pack([src[k] for k in SMALL], SUBLANES, F32) for src in (wts, grads, mom1, mom2)]
    outs = _adamw(packed[0][None], [packed[1]], packed[2][None], packed[3][None], "adamw_small")
    for dst, o in zip((delta, new_m, new_v), outs[1:]):
        for k, a in zip(SMALL, _unpack(o[0], small_shard_shapes)):
            dst[k] = a
    reduce_and_update(("rg",), outs[3], None)

    return (loss, grad_x[None], *[grads[k] for k in WEIGHTS], *[delta[k] for k in WEIGHTS], *[new_m[k] for k in WEIGHTS],
            *[new_v[k] for k in WEIGHTS])
```

```python
import jax
import jax.numpy as jnp
from jax import lax
from jax.experimental import pallas as pl
from jax.experimental.pallas import tpu as pltpu

F32 = jnp.float32
BF16 = jnp.bfloat16
MXU_DTYPE = BF16

EPS = 1e-6
RG_C = 8.0
RG_BLOCKS = 4
RG_CONV = 4
GLA_HEADS = 4
GLA_TAU = 16.0
GLA_CHUNK = 64
GLA_RANK = 16
FFN_CONV = 3
ADAM_LR = 0.001
ADAM_B1 = 0.9
ADAM_B2 = 0.999
ADAM_EPS = 1e-08
ADAM_WD = 0.01
ADAM_STEP = 10

LANES = 128
SUBLANES = 8
VMEM_LIMIT = 56 * 1024 * 1024
CB = 256
MESH = pl.DeviceIdType.MESH
N_DEV = 8
N_CHIP = 4


def _params(*sem):
    return pltpu.CompilerParams(dimension_semantics=sem, vmem_limit_bytes=VMEM_LIMIT)


def _tile(dim, prefs):
    for p in prefs:
        if dim % p == 0:
            return p
    return dim


def _dot(a, b, dims):
    return lax.dot_general(a.astype(MXU_DTYPE), b.astype(MXU_DTYPE), (dims, ((), ())), preferred_element_type=F32)


def _dot_nn(a, b):
    return _dot(a, b, ((1,), (0,)))


def _dot_nt(a, b):
    return _dot(a, b, ((1,), (1,)))


def _dot_tn(a, b):
    return _dot(a, b, ((0,), (0,)))


def _mm(a, b, *, ta=False, tb=False, a_parts=1, b_parts=1, w_slots=1, out_slots=1, out_dtype=F32, tm_max=1408, name):
    if ta:
        k_dim, m_dim = a.shape
        n_dim = b.shape[-1] * b_parts
    else:
        m_dim, k_dim = a.shape[-2], a.shape[-1] * a_parts
        n_dim = b.shape[-2] if tb else b.shape[-1] * w_slots
    n_unit = n_dim // max(b_parts, out_slots, 1 if tb else w_slots)
    k_unit = k_dim // max(a_parts, w_slots if tb else 1)
    tm = _tile(m_dim, tuple(t for t in (1024, 1408, 512, 256, 128) if t <= max(tm_max, 128)))
    tn = _tile(n_unit, (1024, 1408, 896, 512, 256, 128))
    tk = _tile(k_unit, (1024, 1408, 896, 512, 256, 128))
    nk = k_dim // tk
    dims = ((0 if ta else 1,), (1 if tb else 0,))

    def spec(shape, parts, total, tile, col_grid, row_grid):
        per = total // parts // tile

        def index(i, j, k):
            g = {"i": i, "j": j, "k": k}
            col, row = g[col_grid], g[row_grid]
            return (row, col) if parts == 1 else (col // per, row, col % per)

        return pl.BlockSpec(shape if parts == 1 else (None,) + shape, index)

    def body(a_ref, b_ref, o_ref, *acc):
        if nk == 1:
            o_ref[...] = _dot(a_ref[...], b_ref[...], dims).astype(o_ref.dtype)
            return
        acc_ref, k = acc[0], pl.program_id(2)

        @pl.when(k == 0)
        def _():
            acc_ref[...] = jnp.zeros_like(acc_ref)

        acc_ref[...] += _dot(a_ref[...], b_ref[...], dims)

        @pl.when(k == nk - 1)
        def _():
            o_ref[...] = acc_ref[...].astype(o_ref.dtype)

    if ta:
        a_spec = spec((tk, tm), 1, m_dim, tm, "i", "k")
        b_spec = spec((tk, tn), b_parts, n_dim, tn, "j", "k")
    elif tb:
        a_spec = spec((tm, tk), a_parts, k_dim, tk, "k", "i")
        b_spec = spec((tn, tk), w_slots, k_dim, tk, "k", "j")
    else:
        a_spec = spec((tm, tk), a_parts, k_dim, tk, "k", "i")
        b_spec = spec((tk, tn), w_slots, n_dim, tn, "j", "k")
    out_shape = (m_dim, n_dim) if out_slots == 1 else (out_slots, m_dim, n_dim // out_slots)
    return pl.pallas_call(
        body,
        grid=(m_dim // tm, n_dim // tn, nk),
        in_specs=[a_spec, b_spec],
        out_specs=spec((tm, tn), out_slots, n_dim, tn, "j", "i"),
        out_shape=jax.ShapeDtypeStruct(out_shape, out_dtype),
        scratch_shapes=[pltpu.VMEM((tm, tn), F32)] if nk > 1 else [],
        compiler_params=_params("parallel", "parallel", "arbitrary"),
        name=name,
    )(a, b)


ROW_TILES = (1024, 512)


def _row_specs(s, d, ts):
    return pl.BlockSpec((ts, d), lambda i: (i, 0)), pl.BlockSpec((1, d), lambda i: (0, 0))


def _norm_mod_fwd(x, g, sc, sh, name):
    s, d = x.shape
    ts = _tile(s, ROW_TILES)

    def body(x_ref, g_ref, sc_ref, sh_ref, h_ref):
        xv = x_ref[...]
        r = lax.rsqrt(jnp.mean(xv * xv, axis=-1, keepdims=True) + EPS)
        h_ref[...] = (((xv * r) * g_ref[...]) * (1.0 + sc_ref[...]) + sh_ref[...]).astype(h_ref.dtype)

    row, vec = _row_specs(s, d, ts)
    return pl.pallas_call(
        body, grid=(s // ts,), in_specs=[row, vec, vec, vec], out_specs=row,
        out_shape=jax.ShapeDtypeStruct((s, d), MXU_DTYPE), compiler_params=_params("parallel"), name=name,
    )(x, g, sc, sh)


def _norm_mod_bwd(dh, x, g, sc, dres, name):
    s, d = x.shape
    ts = _tile(s, ROW_TILES)

    def body(dh_ref, x_ref, g_ref, sc_ref, dres_ref, dx_ref, dg_ref, dsc_ref, dsh_ref, acc_ref):
        i = pl.program_id(0)

        @pl.when(i == 0)
        def _():
            acc_ref[...] = jnp.zeros_like(acc_ref)

        xv, dhv = x_ref[...], dh_ref[...]
        r = lax.rsqrt(jnp.mean(xv * xv, axis=-1, keepdims=True) + EPS)
        n = xv * r
        acc_ref[0:1, :] += jnp.sum(dhv * n, axis=0, keepdims=True)
        acc_ref[1:2, :] += jnp.sum(dhv, axis=0, keepdims=True)
        dn = dhv * ((1.0 + sc_ref[...]) * g_ref[...])
        dx_ref[...] = dres_ref[...] + r * (dn - n * jnp.mean(dn * n, axis=-1, keepdims=True))
        dg_ref[...] = (1.0 + sc_ref[...]) * acc_ref[0:1, :]
        dsc_ref[...] = g_ref[...] * acc_ref[0:1, :]
        dsh_ref[...] = acc_ref[1:2, :]

    row, vec = _row_specs(s, d, ts)
    vshape = jax.ShapeDtypeStruct((1, d), F32)
    return pl.pallas_call(
        body, grid=(s // ts,), in_specs=[row, row, vec, vec, row], out_specs=[row, vec, vec, vec],
        out_shape=[jax.ShapeDtypeStruct((s, d), F32), vshape, vshape, vshape],
        scratch_shapes=[pltpu.VMEM((SUBLANES, d), F32)], compiler_params=_params("arbitrary"), name=name,
    )(dh, x, g, sc, dres)


def _post_norm_fwd(x, y, g, gt, g2, sc, sh, name):
    s, d = x.shape
    ts = _tile(s, ROW_TILES)

    def body(x_ref, y_ref, g_ref, gt_ref, g2_ref, sc_ref, sh_ref, o_ref, h_ref):
        yv = y_ref[...]
        r = lax.rsqrt(jnp.mean(yv * yv, axis=-1, keepdims=True) + EPS)
        xn = x_ref[...] + gt_ref[...] * ((yv * r) * g_ref[...])
        o_ref[...] = xn
        r2 = lax.rsqrt(jnp.mean(xn * xn, axis=-1, keepdims=True) + EPS)
        h_ref[...] = (((xn * r2) * g2_ref[...]) * (1.0 + sc_ref[...]) + sh_ref[...]).astype(h_ref.dtype)

    row, vec = _row_specs(s, d, ts)
    return pl.pallas_call(
        body, grid=(s // ts,), in_specs=[row, row] + [vec] * 5, out_specs=[row, row],
        out_shape=[jax.ShapeDtypeStruct((s, d), F32), jax.ShapeDtypeStruct((s, d), MXU_DTYPE)],
        compiler_params=_params("parallel"), name=name,
    )(x, y, g, gt, g2, sc, sh)


def _post_bwd(dxn, y, g, gt, name):
    s, d = y.shape
    ts = _tile(s, ROW_TILES)

    def body(dxn_ref, y_ref, g_ref, gt_ref, dy_ref, dg_ref, dgt_ref, acc_ref):
        i = pl.program_id(0)

        @pl.when(i == 0)
        def _():
            acc_ref[...] = jnp.zeros_like(acc_ref)

        yv, dv = y_ref[...], dxn_ref[...]
        r = lax.rsqrt(jnp.mean(yv * yv, axis=-1, keepdims=True) + EPS)
        n = yv * r
        acc_ref[0:1, :] += jnp.sum(dv * n, axis=0, keepdims=True)
        dn = dv * (gt_ref[...] * g_ref[...])
        dy_ref[...] = (r * (dn - n * jnp.mean(dn * n, axis=-1, keepdims=True))).astype(dy_ref.dtype)
        dg_ref[...] = gt_ref[...] * acc_ref[0:1, :]
        dgt_ref[...] = g_ref[...] * acc_ref[0:1, :]

    row, vec = _row_specs(s, d, ts)
    vshape = jax.ShapeDtypeStruct((1, d), F32)
    return pl.pallas_call(
        body, grid=(s // ts,), in_specs=[row, row, vec, vec], out_specs=[row, vec, vec],
        out_shape=[jax.ShapeDtypeStruct((s, d), MXU_DTYPE), vshape, vshape],
        scratch_shapes=[pltpu.VMEM((SUBLANES, d), F32)], compiler_params=_params("arbitrary"), name=name,
    )(dxn, y, g, gt)


def _norm_post_bwd(dh, x, g, sc, dres, y, gp, gt, name):
    s, d = x.shape
    ts = _tile(s, (512,))

    def body(dh_ref, x_ref, g_ref, sc_ref, dres_ref, y_ref, gp_ref, gt_ref,
             dx_ref, dy_ref, dg_ref, dsc_ref, dsh_ref, dgp_ref, dgt_ref, acc_ref):
        i = pl.program_id(0)

        @pl.when(i == 0)
        def _():
            acc_ref[...] = jnp.zeros_like(acc_ref)

        xv, dhv = x_ref[...], dh_ref[...]
        r = lax.rsqrt(jnp.mean(xv * xv, axis=-1, keepdims=True) + EPS)
        n = xv * r
        acc_ref[0:1, :] += jnp.sum(dhv * n, axis=0, keepdims=True)
        acc_ref[1:2, :] += jnp.sum(dhv, axis=0, keepdims=True)
        dn = dhv * ((1.0 + sc_ref[...]) * g_ref[...])
        dx = dres_ref[...] + r * (dn - n * jnp.mean(dn * n, axis=-1, keepdims=True))
        dx_ref[...] = dx
        yv = y_ref[...]
        ry = lax.rsqrt(jnp.mean(yv * yv, axis=-1, keepdims=True) + EPS)
        ny = yv * ry
        acc_ref[2:3, :] += jnp.sum(dx * ny, axis=0, keepdims=True)
        dny = dx * (gt_ref[...] * gp_ref[...])
        dy_ref[...] = (ry * (dny - ny * jnp.mean(dny * ny, axis=-1, keepdims=True))).astype(dy_ref.dtype)
        dg_ref[...] = (1.0 + sc_ref[...]) * acc_ref[0:1, :]
        dsc_ref[...] = g_ref[...] * acc_ref[0:1, :]
        dsh_ref[...] = acc_ref[1:2, :]
        dgp_ref[...] = gt_ref[...] * acc_ref[2:3, :]
        dgt_ref[...] = gp_ref[...] * acc_ref[2:3, :]

    row, vec = _row_specs(s, d, ts)
    vshape = jax.ShapeDtypeStruct((1, d), F32)
    return pl.pallas_call(
        body, grid=(s // ts,), in_specs=[row, row, vec, vec, row, row, vec, vec], out_specs=[row, row] + [vec] * 5,
        out_shape=[jax.ShapeDtypeStruct((s, d), F32), jax.ShapeDtypeStruct((s, d), MXU_DTYPE)] + [vshape] * 5,
        scratch_shapes=[pltpu.VMEM((SUBLANES, d), F32)], compiler_params=_params("arbitrary"), name=name,
    )(dh, x, g, sc, dres, y, gp, gt)


def _post_loss(x, y, g, gt, tgt, name):
    s, d = x.shape
    ts = _tile(s, ROW_TILES)

    def body(x_ref, y_ref, g_ref, gt_ref, t_ref, col_ref, dx_ref):
        i = pl.program_id(0)

        @pl.when(i == 0)
        def _():
            col_ref[...] = jnp.zeros_like(col_ref)

        yv = y_ref[...]
        r = lax.rsqrt(jnp.mean(yv * yv, axis=-1, keepdims=True) + EPS)
        e = (x_ref[...] + gt_ref[...] * ((yv * r) * g_ref[...])) - t_ref[...]
        col_ref[...] += jnp.sum(e * e, axis=0, keepdims=True)
        dx_ref[...] = e * (1.0 / d)

    row, vec = _row_specs(s, d, ts)
    return pl.pallas_call(
        body, grid=(s // ts,), in_specs=[row, row, vec, vec, row], out_specs=[vec, row],
        out_shape=[jax.ShapeDtypeStruct((1, d), F32), jax.ShapeDtypeStruct((s, d), F32)],
        compiler_params=_params("arbitrary"), name=name,
    )(x, y, g, gt, tgt)


_GELU_C = 0.7978845608028654
_GELU_A = 0.044715


def _gelu(x):
    t = jnp.tanh(_GELU_C * (x + _GELU_A * x * x * x))
    return 0.5 * x * (1.0 + t), t


def _gelu_grad(x, t):
    return 0.5 * (1.0 + t) + 0.5 * x * (1.0 - t * t) * (_GELU_C * (1.0 + 3.0 * _GELU_A * x * x))


def _sigmoid(x):
    return 1.0 / (1.0 + jnp.exp(-x))


def _log1p_pos(y):
    u = 1.0 + y
    return jnp.where(u == 1.0, y, jnp.log(u) * (y / jnp.where(u == 1.0, 1.0, u - 1.0)))


def _softplus(x):
    return jnp.maximum(x, 0.0) + _log1p_pos(jnp.exp(-jnp.abs(x)))


def _one_minus_sq_exp(x, ex):
    z = 2.0 * x
    series = -z * (1.0 + z * (1.0 / 2 + z * (1.0 / 6 + z * (1.0 / 24 + z * (1.0 / 120)))))
    return jnp.where(z > -0.05, series, 1.0 - ex * ex)


SLAB = 16


def _cat(a, b):
    return jnp.concatenate([a, b], axis=1)


def _pair_specs(shape, nb, index):
    return [pl.BlockSpec(shape, lambda j, t: index(j, t) + (j,)), pl.BlockSpec(shape, lambda j, t: index(j, t) + (j + nb,))]


def _halo_row(ts, time_of):
    return lambda j, t: (jnp.maximum(time_of(t) * (ts // SUBLANES) - 1, 0),)


def _rows_from(groups, k):
    row = lax.broadcasted_iota(jnp.int32, groups[0].shape, 0)
    turned = [pltpu.roll(g, SUBLANES - k, axis=0) for g in groups]
    return [jnp.where(row < SUBLANES - k, lo, hi) for lo, hi in zip(turned[:-1], turned[1:])]


def _ffn_mid_fwd(p, cw, cb, name):
    s, f2 = p.shape
    ts = _tile(s, (1024, 512))
    nb, nt = f2 // (2 * CB), s // ts
    n_grp = SLAB // SUBLANES

    def body(pg_ref, pv_ref, hg_ref, hv_ref, cwg_ref, cwv_ref, cbg_ref, cbv_ref, a_ref, ga_ref, gb_ref):
        t = pl.program_id(1)
        cwv, bias = _cat(cwg_ref[...], cwv_ref[...]), _cat(cbg_ref[...], cbv_ref[...])
        w0, w1, w2 = cwv[0:1], cwv[1:2], cwv[2:3]

        def slab(before, cur, r0):
            pm2, pm1 = _rows_from([before] + cur, SUBLANES - 2), _rows_from([before] + cur, SUBLANES - 1)
            u = jnp.concatenate([bias + w0 * pm2[i] + w1 * pm1[i] + w2 * cur[i] for i in range(n_grp)], axis=0)
            g, v = u[:, :CB], u[:, CB:]
            gel, th = _gelu(g)
            rows = pl.ds(r0, SLAB)
            a_ref[rows, :] = (gel * v).astype(a_ref.dtype)
            ga_ref[rows, :] = gel.astype(ga_ref.dtype)
            gb_ref[rows, :] = (v * _gelu_grad(g, th)).astype(gb_ref.dtype)

        def pieces(rows):
            blk = _cat(pg_ref[rows, :], pv_ref[rows, :])
            return [blk[i * SUBLANES:(i + 1) * SUBLANES] for i in range(blk.shape[0] // SUBLANES)]

        slab(jnp.where(t > 0, _cat(hg_ref[...], hv_ref[...]), 0.0), pieces(pl.ds(0, SLAB)), 0)

        def loop(i, carry):
            r0 = pl.multiple_of(i * SLAB, SLAB)
            got = pieces(pl.ds(pl.multiple_of(r0 - SUBLANES, SUBLANES), SLAB + SUBLANES))
            slab(got[0], got[1:], r0)
            return carry

        lax.fori_loop(1, ts // SLAB, loop, 0, unroll=2)

    fwd = lambda t: t
    out = pl.BlockSpec((ts, CB), lambda j, t: (t, j))
    shape = jax.ShapeDtypeStruct((s, f2 // 2), MXU_DTYPE)
    return pl.pallas_call(
        body, grid=(nb, nt),
        in_specs=(_pair_specs((ts, CB), nb, lambda j, t: (t,)) + _pair_specs((SUBLANES, CB), nb, _halo_row(ts, fwd))
                  + _pair_specs((FFN_CONV, CB), nb, lambda j, t: (0,)) + _pair_specs((1, CB), nb, lambda j, t: (0,))),
        out_specs=[out, out, out], out_shape=[shape, shape, shape],
        compiler_params=_params("parallel", "arbitrary"), name=name,
    )(p, p, p, p, cw, cw, cb, cb)


def _ffn_mid_bwd(da, p, ga, gb, cw, name):
    s, f2 = p.shape
    ts = _tile(s, (1024, 512))
    nb, nt = f2 // (2 * CB), s // ts
    n_slab = ts // SLAB
    n_grp = SLAB // SUBLANES
    per_trip = 2

    def body(da_ref, ga_ref, gb_ref, pg_ref, pv_ref, cwg_ref, cwv_ref, dp_ref, dcw_ref, dcb_ref, next_du, acc):
        tt = pl.program_id(1)
        cwv = _cat(cwg_ref[...], cwv_ref[...])
        w0, w1, w2 = cwv[0:1], cwv[1:2], cwv[2:3]

        @pl.when(tt == 0)
        def _():
            next_du[...] = jnp.zeros_like(next_du)
            acc[...] = jnp.zeros_like(acc)

        def slab(r0, after, sums):
            rows = pl.ds(r0, SLAB)
            dav = da_ref[rows, :]
            du = _cat(dav * gb_ref[rows, :].astype(F32), dav * ga_ref[rows, :].astype(F32))
            p0 = _cat(pg_ref[rows, :], pv_ref[rows, :])
            cur = [du[i * SUBLANES:(i + 1) * SUBLANES] for i in range(n_grp)]
            du1, du2 = _rows_from(cur + [after], 1), _rows_from(cur + [after], 2)
            dpv = jnp.concatenate([w2 * cur[i] + w1 * du1[i] + w0 * du2[i] for i in range(n_grp)], axis=0).astype(dp_ref.dtype)
            dp_ref[0, rows, :] = dpv[:, :CB]
            dp_ref[1, rows, :] = dpv[:, CB:]
            for i in range(n_grp):
                pi = p0[i * SUBLANES:(i + 1) * SUBLANES]
                parts = (cur[i], du2[i] * pi, du1[i] * pi, cur[i] * pi)
                sums = parts if sums is None else tuple(x + y for x, y in zip(sums, parts))
            return cur[0], sums

        def loop(k, after):
            sums = None
            for j in range(per_trip):
                r0 = pl.multiple_of((n_slab - 1 - (k * per_trip + j)) * SLAB, SLAB)
                after, sums = slab(r0, after, sums)
            for q, part in enumerate(sums):
                acc[q] += part
            return after

        next_du[...] = lax.fori_loop(0, n_slab // per_trip, loop, next_du[...])

        @pl.when(tt == nt - 1)
        def _():
            for half in range(2):
                cols = slice(half * CB, (half + 1) * CB)
                dcb_ref[half] = jnp.sum(acc[0][:, cols], axis=0, keepdims=True)
                for k in range(FFN_CONV):
                    dcw_ref[half, k:k + 1, :] = jnp.sum(acc[1 + k][:, cols], axis=0, keepdims=True)

    rev = lambda t: nt - 1 - t
    tile = pl.BlockSpec((ts, CB), lambda j, t: (rev(t), j))
    return pl.pallas_call(
        body, grid=(nb, nt),
        in_specs=([tile, tile, tile] + _pair_specs((ts, CB), nb, lambda j, t: (rev(t),))
                  + _pair_specs((FFN_CONV, CB), nb, lambda j, t: (0,))),
        out_specs=[pl.BlockSpec((2, ts, CB), lambda j, t: (0, rev(t), j)),
                   pl.BlockSpec((2, FFN_CONV, CB), lambda j, t: (0, 0, j)),
                   pl.BlockSpec((2, 1, CB), lambda j, t: (0, 0, j))],
        out_shape=[jax.ShapeDtypeStruct((2, s, f2 // 2), MXU_DTYPE), jax.ShapeDtypeStruct((2, FFN_CONV, f2 // 2), F32),
                   jax.ShapeDtypeStruct((2, 1, f2 // 2), F32)],
        scratch_shapes=[pltpu.VMEM((SUBLANES, 2 * CB), F32), pltpu.VMEM((1 + FFN_CONV, SUBLANES, 2 * CB), F32)],
        compiler_params=_params("parallel", "arbitrary"), name=name,
    )(da, ga, gb, p, p, cw, cw)


def _rg_gates(xc, wa_ref, ba_ref, wx_ref, bx_ref, lam_ref):
    r = _sigmoid(_dot_nn(xc, wa_ref[0]) + ba_ref[...])
    ig = _sigmoid(_dot_nn(xc, wx_ref[0]) + bx_ref[...])
    sp = _softplus(-lam_ref[...])
    log_a = (-RG_C) * r * sp
    a = jnp.exp(log_a)
    mult = jnp.sqrt(_one_minus_sq_exp(log_a, a))
    return r, ig, sp, a, mult


def _rg_conv(scr, cw_ref, cb_ref, ts):
    views = [scr[5 + k:5 + k + ts, :] for k in range(RG_CONV)]
    xc = cb_ref[...]
    for k in range(RG_CONV):
        xc = xc + cw_ref[k:k + 1, :] * views[k]
    return xc, views


def _rg_param_specs():
    vec = pl.BlockSpec((1, CB), lambda g, t: (0, g))
    mat = pl.BlockSpec((1, CB, CB), lambda g, t: (g, 0, 0))
    return [pl.BlockSpec((RG_CONV, CB), lambda g, t: (0, g)), vec, mat, vec, mat, vec, vec]


def _scan_rows(a_scr, x_scr, out_ref, carry, ts, reverse):
    n = ts // SUBLANES
    row = lax.broadcasted_iota(jnp.int32, (SUBLANES, a_scr.shape[1]), 0)
    last = SUBLANES - 1

    def rows_of(k):
        return pl.ds(pl.multiple_of(k * SUBLANES, SUBLANES), SUBLANES)

    def local(k, _):
        rows = rows_of(k)
        a, x = a_scr[rows, :], x_scr[rows, :]
        if reverse:
            a = jnp.where(row == last, 1.0, pltpu.roll(a, last, axis=0))
            for sh in (1, 2, 4):
                keep = row < SUBLANES - sh
                x = x + a * jnp.where(keep, pltpu.roll(x, SUBLANES - sh, axis=0), 0.0)
                a = a * jnp.where(keep, pltpu.roll(a, SUBLANES - sh, axis=0), 1.0)
        else:
            for sh in (1, 2, 4):
                keep = row >= sh
                x = a * jnp.where(keep, pltpu.roll(x, sh, axis=0), 0.0) + x
                a = a * jnp.where(keep, pltpu.roll(a, sh, axis=0), 1.0)
        out_ref[rows, :] = x
        x_scr[rows, :] = a
        return 0

    lax.fori_loop(0, n, local, 0, unroll=4)

    def chain(k, c):
        rows = rows_of(n - 1 - k if reverse else k)
        v = out_ref[rows, :] + x_scr[rows, :] * c
        out_ref[rows, :] = v
        return a_scr[rows, :][0:1] * v[0:1] if reverse else v[last:last + 1]

    return lax.fori_loop(0, n, chain, carry, unroll=4)


def _rg_mid_fwd(pj, cw, cb, wa, ba, wx, bx, lam, name):
    s = pj.shape[0]
    nb = pj.shape[1] // (2 * CB)
    ts = _tile(s, (512,))
    nt = s // ts

    def body(gate_ref, x_ref, halo_ref, cw_ref, cb_ref, wa_ref, ba_ref, wx_ref, bx_ref, lam_ref, y_ref, hs_ref,
             scr, a_scr, u_scr, h_scr):
        t = pl.program_id(1)

        @pl.when(t == 0)
        def _():
            h_scr[...] = jnp.zeros_like(h_scr)

        scr[0:SUBLANES, :] = jnp.where(t > 0, halo_ref[...], 0.0)
        scr[SUBLANES:, :] = x_ref[...]
        xc, _ = _rg_conv(scr, cw_ref, cb_ref, ts)
        _, ig, _, a, mult = _rg_gates(xc, wa_ref, ba_ref, wx_ref, bx_ref, lam_ref)
        a_scr[...] = a
        u_scr[...] = mult * (ig * xc)
        h_scr[0:1, :] = _scan_rows(a_scr, u_scr, hs_ref, h_scr[0:1, :], ts, False)
        y_ref[...] = (_gelu(gate_ref[...])[0] * hs_ref[...]).astype(y_ref.dtype)

    blk = pl.BlockSpec((ts, CB), lambda g, t: (t, g))
    return pl.pallas_call(
        body, grid=(nb, nt),
        in_specs=_pair_specs((ts, CB), nb, lambda g, t: (t,))
        + [pl.BlockSpec((SUBLANES, CB), lambda g, t: _halo_row(ts, lambda u: u)(g, t) + (g + nb,))] + _rg_param_specs(),
        out_specs=[blk, blk],
        out_shape=[jax.ShapeDtypeStruct((s, nb * CB), MXU_DTYPE), jax.ShapeDtypeStruct((s, nb * CB), F32)],
        scratch_shapes=[pltpu.VMEM((ts + SUBLANES, CB), F32), pltpu.VMEM((ts, CB), F32), pltpu.VMEM((ts, CB), F32),
                        pltpu.VMEM((SUBLANES, CB), F32)],
        compiler_params=_params("parallel", "arbitrary"), name=name,
    )(pj, pj, pj, cw, cb, wa, ba, wx, bx, lam)


def _rg_mid_bwd(dy, pj, hs, cw, cb, wa, ba, wx, bx, lam, name):
    s = pj.shape[0]
    nb = pj.shape[1] // (2 * CB)
    ts = _tile(s, (512,))
    nt = s // ts

    def body(dy_ref, gate_ref, x_ref, halo_ref, hs_ref, hsh_ref, cw_ref, cb_ref, wa_ref, ba_ref, wx_ref, bx_ref, lam_ref,
             dpj_ref, dcw_ref, dcb_ref, dwa_ref, dba_ref, dwx_ref, dbx_ref, dlam_ref,
             scr, hscr, a_scr, d_scr, g_scr, dxscr, c_scr):
        tt = pl.program_id(1)
        t = nt - 1 - tt

        @pl.when(tt == 0)
        def _():
            c_scr[...] = jnp.zeros_like(c_scr)
            dxscr[ts:, :] = jnp.zeros((SUBLANES, CB), F32)
            for ref in (dcw_ref, dcb_ref, dwa_ref, dba_ref, dwx_ref, dbx_ref, dlam_ref):
                ref[...] = jnp.zeros_like(ref)

        scr[0:SUBLANES, :] = jnp.where(t > 0, halo_ref[...], 0.0)
        scr[SUBLANES:, :] = x_ref[...]
        hscr[0:SUBLANES, :] = jnp.where(t > 0, hsh_ref[...], 0.0)
        hscr[SUBLANES:, :] = hs_ref[...]
        xc, views = _rg_conv(scr, cw_ref, cb_ref, ts)
        r, ig, sp, a, mult = _rg_gates(xc, wa_ref, ba_ref, wx_ref, bx_ref, lam_ref)
        gate = gate_ref[...]
        gel, th = _gelu(gate)
        dyv = dy_ref[...]
        dpj_ref[0] = (dyv * hs_ref[...] * _gelu_grad(gate, th)).astype(dpj_ref.dtype)
        a_scr[...] = a
        d_scr[...] = dyv * gel
        c_scr[0:1, :] = _scan_rows(a_scr, d_scr, g_scr, c_scr[0:1, :], ts, True)
        du = g_scr[...]
        da = du * hscr[7:7 + ts, :]
        dmult = du * (ig * xc)
        dig = du * (mult * xc)
        dxc = du * (mult * ig)
        dlog_a = da * a - dmult * (a * a / mult)
        dlam_ref[...] += jnp.sum(dlog_a * r, axis=0, keepdims=True) * (RG_C * _sigmoid(-lam_ref[...]))
        dpr = dlog_a * ((-RG_C) * sp) * (r * (1.0 - r))
        dpi = dig * (ig * (1.0 - ig))
        dba_ref[...] += jnp.sum(dpr, axis=0, keepdims=True)
        dbx_ref[...] += jnp.sum(dpi, axis=0, keepdims=True)
        dwa_ref[0] += _dot_tn(xc, dpr)
        dwx_ref[0] += _dot_tn(xc, dpi)
        dxc = dxc + _dot_nt(dpr, wa_ref[0]) + _dot_nt(dpi, wx_ref[0])
        dcb_ref[...] += jnp.sum(dxc, axis=0, keepdims=True)
        for k in range(RG_CONV):
            dcw_ref[k:k + 1, :] += jnp.sum(dxc * views[k], axis=0, keepdims=True)
        dxscr[0:ts, :] = dxc
        dxp = cw_ref[3:4, :] * dxc
        for k in range(RG_CONV - 1):
            dxp = dxp + cw_ref[k:k + 1, :] * dxscr[3 - k:3 - k + ts, :]
        dpj_ref[1] = dxp.astype(dpj_ref.dtype)
        dxscr[ts:, :] = dxscr[0:SUBLANES, :]

    rev = lambda g, t: (nt - 1 - t, g)
    rev_halo = lambda g, t: (jnp.maximum((nt - 1 - t) * (ts // SUBLANES) - 1, 0), g)
    vec = pl.BlockSpec((1, CB), lambda g, t: (0, g))
    mat = pl.BlockSpec((1, CB, CB), lambda g, t: (g, 0, 0))
    d = nb * CB
    vshape = jax.ShapeDtypeStruct((1, d), F32)
    mshape = jax.ShapeDtypeStruct((nb, CB, CB), F32)
    return pl.pallas_call(
        body, grid=(nb, nt),
        in_specs=[pl.BlockSpec((ts, CB), rev)] + _pair_specs((ts, CB), nb, lambda g, t: (nt - 1 - t,))
        + [pl.BlockSpec((SUBLANES, CB), lambda g, t: (rev_halo(g, t)[0], g + nb)),
           pl.BlockSpec((ts, CB), rev), pl.BlockSpec((SUBLANES, CB), rev_halo)] + _rg_param_specs(),
        out_specs=[pl.BlockSpec((2, ts, CB), lambda g, t: (0, nt - 1 - t, g)), pl.BlockSpec((RG_CONV, CB), lambda g, t: (0, g)),
                   vec, mat, vec, mat, vec, vec],
        out_shape=[jax.ShapeDtypeStruct((2, s, d), MXU_DTYPE), jax.ShapeDtypeStruct((RG_CONV, d), F32), vshape, mshape, vshape,
                   mshape, vshape, vshape],
        scratch_shapes=[pltpu.VMEM((ts + SUBLANES, CB), F32), pltpu.VMEM((ts + SUBLANES, CB), F32), pltpu.VMEM((ts, CB), F32),
                        pltpu.VMEM((ts, CB), F32), pltpu.VMEM((ts, CB), F32), pltpu.VMEM((ts + SUBLANES, CB), F32),
                        pltpu.VMEM((SUBLANES, CB), F32)],
        compiler_params=_params("parallel", "arbitrary"), name=name,
    )(dy, pj, pj, pj, hs, hs, cw, cb, wa, ba, wx, bx, lam)


GLA_DK = 128
GLA_DV = 256
GLA_O_K = GLA_HEADS * GLA_DK
GLA_O_V = 2 * GLA_HEADS * GLA_DK
GLA_O_R = GLA_O_V + GLA_HEADS * GLA_DV
GLA_O_Z = GLA_O_R + GLA_HEADS * GLA_DV
GLA_IN = GLA_O_Z + GLA_RANK
GLA_TS = 256


def _dk(h, base=0):
    return slice(base + h * GLA_DK, base + (h + 1) * GLA_DK)


def _dv(h, base=0):
    return slice(base + h * GLA_DV, base + (h + 1) * GLA_DV)


def _split3(x):
    hi = x.astype(BF16)
    r1 = x - hi.astype(F32)
    mid = r1.astype(BF16)
    lo = (r1 - mid.astype(F32)).astype(BF16)
    return hi, mid, lo


def _chunk_cumsum(x, reverse):
    n = x.shape[0]
    i = lax.broadcasted_iota(jnp.int32, (n, n), 0)
    j = lax.broadcasted_iota(jnp.int32, (n, n), 1)
    same = (i // GLA_CHUNK) == (j // GLA_CHUNK)
    tri = jnp.where(same & ((j >= i) if reverse else (j <= i)), 1.0, 0.0).astype(BF16)
    out = jnp.zeros(x.shape, F32)
    for piece in _split3(x):
        out = out + lax.dot_general(tri, piece, (((1,), (0,)), ((), ())), preferred_element_type=F32)
    return out


def _gla_head(pj_ref, h):
    return (pj_ref[:, _dk(h)] * (GLA_DK ** -0.5), pj_ref[:, _dk(h, GLA_O_K)], pj_ref[:, _dv(h, GLA_O_V)],
            pj_ref[:, _dv(h, GLA_O_R)])


def _gla_decays(gc):
    gref = gc[GLA_CHUNK // 2:GLA_CHUNK // 2 + 1, :]
    glast = gc[GLA_CHUNK - 1:GLA_CHUNK, :]
    return jnp.exp(gc), jnp.exp(gc - gref), jnp.exp(gref - gc), jnp.exp(glast - gc), jnp.exp(glast)


def _causal_mask():
    i = lax.broadcasted_iota(jnp.int32, (GLA_CHUNK, GLA_CHUNK), 0)
    j = lax.broadcasted_iota(jnp.int32, (GLA_CHUNK, GLA_CHUNK), 1)
    return j <= i


def _log_sigmoid(x):
    return jnp.minimum(x, 0.0) - _log1p_pos(jnp.exp(-jnp.abs(x)))


def _gla_mid_fwd(pj, wal, bal, ng, name):
    s, nh = pj.shape[0], GLA_HEADS
    ts = _tile(s, (GLA_TS,))
    nt, nc = s // ts, ts // GLA_CHUNK

    def body(pj_ref, wal_ref, bal_ref, ng_ref, act_ref, o_ref, st_ref, s_scr):
        t = pl.program_id(0)

        @pl.when(t == 0)
        def _():
            s_scr[...] = jnp.zeros_like(s_scr)

        heads = []
        z = pj_ref[:, GLA_O_Z:]
        for h in range(nh):
            q, k, v, r = _gla_head(pj_ref, h)
            g = _log_sigmoid(_dot_nn(z, wal_ref[:, _dk(h)]) + bal_ref[:, _dk(h)]) * (1.0 / GLA_TAU)
            heads.append((q, k, v, r, _chunk_cumsum(g, False)))
        mask = _causal_mask()
        for c in range(nc):
            sl = slice(c * GLA_CHUNK, (c + 1) * GLA_CHUNK)
            for h, (q, k, v, r, gcum) in enumerate(heads):
                eg, eq, ek, ekd, egl = _gla_decays(gcum[sl])
                st = s_scr[h]
                st_ref[c, h] = st
                attn = jnp.where(mask, _dot_nt(q[sl] * eq, k[sl] * ek), 0.0)
                o_ref[sl, h * GLA_DV:(h + 1) * GLA_DV] = _dot_nt(q[sl] * eg, st) + _dot_nn(attn, v[sl])
                s_scr[h] = st * egl + _dot_tn(v[sl], k[sl] * ekd)
        for h, (q, k, v, r, gcum) in enumerate(heads):
            cols = slice(h * GLA_DV, (h + 1) * GLA_DV)
            o = o_ref[:, cols]
            on = o * lax.rsqrt(jnp.mean(o * o, axis=-1, keepdims=True) + EPS)
            act_ref[:, cols] = ((on * ng_ref[...]) * (r * _sigmoid(r))).astype(act_ref.dtype)

    blk = pl.BlockSpec((ts, nh * GLA_DV), lambda t: (t, 0))
    whole = lambda shape: pl.BlockSpec(shape, lambda t: (0,) * len(shape))
    return pl.pallas_call(
        body, grid=(nt,),
        in_specs=[pl.BlockSpec((ts, GLA_IN), lambda t: (t, 0)), whole((GLA_RANK, nh * GLA_DK)), whole((1, nh * GLA_DK)),
                  whole((1, GLA_DV))],
        out_specs=[blk, blk, pl.BlockSpec((nc, nh, GLA_DV, GLA_DK), lambda t: (t, 0, 0, 0))],
        out_shape=[jax.ShapeDtypeStruct((s, nh * GLA_DV), MXU_DTYPE), jax.ShapeDtypeStruct((s, nh * GLA_DV), F32),
                   jax.ShapeDtypeStruct((s // GLA_CHUNK, nh, GLA_DV, GLA_DK), F32)],
        scratch_shapes=[pltpu.VMEM((nh, GLA_DV, GLA_DK), F32)],
        compiler_params=_params("arbitrary"), name=name,
    )(pj, wal, bal, ng)


def _gla_mid_bwd(dact, pj, o, st, wal, bal, ng, name):
    s, nh = pj.shape[0], GLA_HEADS
    ts = _tile(s, (GLA_TS,))
    nt, nc = s // ts, ts // GLA_CHUNK

    def body(dact_ref, pj_ref, o_ref, st_ref, wal_ref, bal_ref, ng_ref, dpj_ref, dwal_ref, dbal_ref, dng_ref,
             ds_scr, dg_scr):
        tt = pl.program_id(0)

        @pl.when(tt == 0)
        def _():
            ds_scr[...] = jnp.zeros_like(ds_scr)
            dwal_ref[...] = jnp.zeros_like(dwal_ref)
            dbal_ref[...] = jnp.zeros_like(dbal_ref)
            dng_ref[...] = jnp.zeros_like(dng_ref)

        heads = []
        z = pj_ref[:, GLA_O_Z:]
        for h in range(nh):
            q, k, v, r = _gla_head(pj_ref, h)
            logit = _dot_nn(z, wal_ref[:, _dk(h)]) + bal_ref[:, _dk(h)]
            gcum = _chunk_cumsum(_log_sigmoid(logit) * (1.0 / GLA_TAU), False)
            ov = o_ref[:, h * GLA_DV:(h + 1) * GLA_DV]
            ro = lax.rsqrt(jnp.mean(ov * ov, axis=-1, keepdims=True) + EPS)
            on = ov * ro
            sg = _sigmoid(r)
            sil = r * sg
            dav = dact_ref[:, h * GLA_DV:(h + 1) * GLA_DV]
            dpj_ref[:, _dv(h, GLA_O_R)] = (dav * (on * ng_ref[...]) * (sg + sil * (1.0 - sg))).astype(dpj_ref.dtype)
            t1 = dav * sil
            dng_ref[...] += jnp.sum(t1 * on, axis=0, keepdims=True)
            dn = t1 * ng_ref[...]
            do = ro * (dn - on * jnp.mean(dn * on, axis=-1, keepdims=True))
            heads.append((q, k, v, logit, gcum, do))
        mask = _causal_mask()
        scale = GLA_DK ** -0.5
        last_row = lax.broadcasted_iota(jnp.int32, (GLA_CHUNK, GLA_DK), 0) == GLA_CHUNK - 1
        for c in reversed(range(nc)):
            sl = slice(c * GLA_CHUNK, (c + 1) * GLA_CHUNK)
            for h, (q, k, v, logit, gcum, do) in enumerate(heads):
                eg, eq, ek, ekd, egl = _gla_decays(gcum[sl])
                qc, kc, vc, doc = q[sl], k[sl], v[sl], do[sl]
                qg, qt, kt, kd = qc * eg, qc * eq, kc * ek, kc * ekd
                sp = st_ref[c, h]
                ds = ds_scr[h]
                attn = jnp.where(mask, _dot_nt(qt, kt), 0.0)
                dattn = jnp.where(mask, _dot_nt(doc, vc), 0.0)
                dqg = _dot_nn(doc, sp)
                dqt = _dot_nn(dattn, kt)
                dkt = _dot_tn(dattn, qt)
                dkd = _dot_nn(vc, ds)
                dpj_ref[sl, _dv(h, GLA_O_V)] = (_dot_tn(attn, doc) + _dot_nt(kd, ds)).astype(dpj_ref.dtype)
                dpj_ref[sl, _dk(h)] = (scale * (dqg * eg + dqt * eq)).astype(dpj_ref.dtype)
                dpj_ref[sl, _dk(h, GLA_O_K)] = (dkt * ek + dkd * ekd).astype(dpj_ref.dtype)
                kdd = dkd * kd
                dgl = jnp.sum(kdd, axis=0, keepdims=True) + jnp.sum(ds * sp, axis=0, keepdims=True) * egl
                dg_scr[h, sl, :] = dqg * qg + dqt * qt - dkt * kt - kdd + jnp.where(last_row, dgl, 0.0)
                ds_scr[h] = ds * egl + _dot_tn(doc, qg)
        dz = jnp.zeros((ts, GLA_RANK), F32)
        for h, (q, k, v, logit, gcum, do) in enumerate(heads):
            dlogit = _chunk_cumsum(dg_scr[h], True) * (1.0 / GLA_TAU) * _sigmoid(-logit)
            dz = dz + _dot_nt(dlogit, wal_ref[:, _dk(h)])
            dwal_ref[:, _dk(h)] += _dot_tn(z, dlogit)
            dbal_ref[:, _dk(h)] += jnp.sum(dlogit, axis=0, keepdims=True)
        dpj_ref[:, GLA_O_Z:] = dz.astype(dpj_ref.dtype)

    rev = lambda t: (nt - 1 - t, 0)
    whole = lambda shape: pl.BlockSpec(shape, lambda t: (0,) * len(shape))
    wide = pl.BlockSpec((ts, nh * GLA_DV), rev)
    return pl.pallas_call(
        body, grid=(nt,),
        in_specs=[wide, pl.BlockSpec((ts, GLA_IN), rev), wide,
                  pl.BlockSpec((nc, nh, GLA_DV, GLA_DK), lambda t: (nt - 1 - t, 0, 0, 0)),
                  whole((GLA_RANK, nh * GLA_DK)), whole((1, nh * GLA_DK)), whole((1, GLA_DV))],
        out_specs=[pl.BlockSpec((ts, GLA_IN), rev), whole((GLA_RANK, nh * GLA_DK)), whole((1, nh * GLA_DK)), whole((1, GLA_DV))],
        out_shape=[jax.ShapeDtypeStruct((s, GLA_IN), MXU_DTYPE), jax.ShapeDtypeStruct((GLA_RANK, nh * GLA_DK), F32),
                   jax.ShapeDtypeStruct((1, nh * GLA_DK), F32), jax.ShapeDtypeStruct((1, GLA_DV), F32)],
        scratch_shapes=[pltpu.VMEM((nh, GLA_DV, GLA_DK), F32), pltpu.VMEM((nh, ts, GLA_DK), F32)],
        compiler_params=_params("arbitrary"), name=name,
    )(dact, pj, o, st, wal, bal, ng)


def _adamw(w, gs, m, v, name, after=None):
    layers, rows, cols = w.shape
    gs = list(gs) if isinstance(gs, (list, tuple)) else gs
    n_g = len(gs) if isinstance(gs, list) else 1
    if rows % SUBLANES == 0:
        tr, tc = _tile(rows, (256, 128, 64, 32, 16, 8)), cols
    else:
        tr, tc = rows, _tile(cols, (256, 128))
    c1 = 1.0 / (1.0 - ADAM_B1 ** ADAM_STEP)
    c2 = 1.0 / (1.0 - ADAM_B2 ** ADAM_STEP)

    def body(*refs):
        g_refs, (w_ref, m_ref, v_ref) = refs[:n_g], refs[n_g:n_g + 3]
        go_ref, d_ref, mo_ref, vo_ref = refs[-4:]
        gv = g_refs[0][...]
        for l in range(1, n_g):
            gv = jnp.where(pl.program_id(0) == l, g_refs[l][...], gv)
        m2 = ADAM_B1 * m_ref[...] + (1.0 - ADAM_B1) * gv
        v2 = ADAM_B2 * v_ref[...] + (1.0 - ADAM_B2) * (gv * gv)
        d_ref[...] = (-ADAM_LR) * ((m2 * c1) / (jnp.sqrt(v2 * c2) + ADAM_EPS) + ADAM_WD * w_ref[...])
        go_ref[...] = gv
        mo_ref[...] = m2
        vo_ref[...] = v2

    spec = pl.BlockSpec((None, tr, tc), lambda l, i, j: (l, i, j))
    g_specs = [pl.BlockSpec((tr, tc), lambda l, i, j: (i, j))] * n_g if isinstance(gs, list) else [spec]
    extra = [] if after is None else [(after, _ANY)]
    shape = jax.ShapeDtypeStruct((layers, rows, cols), F32)
    return pl.pallas_call(
        body, grid=(layers, rows // tr, cols // tc), in_specs=g_specs + [spec] * 3 + [sp for _, sp in extra],
        out_specs=[spec] * 4, out_shape=[shape] * 4, compiler_params=_params("parallel", "parallel", "parallel"), name=name,
    )(*(gs if isinstance(gs, list) else [gs]), w, m, v, *[a for a, _ in extra])


def _col_slots(w, name):
    r, cc = w.shape
    c = cc // N_CHIP
    tr = _tile(r, (256,))

    def body(w_ref, o_ref):
        for j in range(N_CHIP):
            o_ref[j] = w_ref[:, j * c:(j + 1) * c]

    return pl.pallas_call(
        body, grid=(r // tr,), in_specs=[pl.BlockSpec((tr, cc), lambda i: (i, 0))],
        out_specs=pl.BlockSpec((N_CHIP, tr, c), lambda i: (0, i, 0)), out_shape=jax.ShapeDtypeStruct((N_CHIP, r, c), w.dtype),
        compiler_params=_params("parallel"), name=name,
    )(w)


def _from_col_slots(w, name):
    n, r, c = w.shape
    tr = _tile(r, (256,))

    def body(w_ref, o_ref):
        for j in range(n):
            o_ref[:, j * c:(j + 1) * c] = w_ref[j]

    return pl.pallas_call(
        body, grid=(r // tr,), in_specs=[pl.BlockSpec((n, tr, c), lambda i: (0, i, 0))],
        out_specs=pl.BlockSpec((tr, n * c), lambda i: (i, 0)), out_shape=jax.ShapeDtypeStruct((r, n * c), w.dtype),
        compiler_params=_params("parallel"), name=name,
    )(w)


def _block_rows_to_slots(w):
    g, r4, cc = w.shape
    return jnp.swapaxes(w.reshape(g, N_CHIP, r4 // N_CHIP, cc), 0, 1).reshape(N_CHIP, g * (r4 // N_CHIP), cc)


def _slots_to_block_rows(w, g):
    n, gr, cc = w.shape
    return jnp.swapaxes(w.reshape(n, g, gr // g, cc), 0, 1).reshape(g, n * (gr // g), cc)


def _local_step(x, tgt, mod, w, fetch=None, done=None, later=None):
    depth = mod.shape[0]
    row = lambda v: v.reshape(1, -1)
    w = dict(w)
    w["ffn_w_up"], w["ffn_w_down"] = dict(enumerate(w["ffn_w_up"])), dict(enumerate(w["ffn_w_down"]))

    def arrive(stage, after):
        if fetch is not None:
            for k, v in fetch(stage, after).items():
                if isinstance(v, dict):
                    w[k].update(v)
                else:
                    w[k] = v

    saved = []
    for i in range(depth):
        if i == 1:
            arrive("gla", x)
        sh_m, sc_m, gt_m, sh_f, sc_f, gt_f = (mod[i, j:j + 1] for j in range(6))
        g0, g1, g2, g3 = (w["norm_g"][i, j:j + 1] for j in range(4))
        tag = f"_l{i}"
        if i == 0:
            h = _norm_mod_fwd(x, g0, sc_m, sh_m, "norm_mix" + tag)
        if i % 2 == 0:
            pj = _mm(h, w["rg_w_in"], w_slots=N_CHIP, name="rg_in" + tag)
            act, aux = _rg_mid_fwd(pj, w["rg_conv_w"], row(w["rg_conv_b"]), w["rg_wa"], row(w["rg_ba"]), w["rg_wx"],
                                   row(w["rg_bx"]), row(w["rg_lambda"]), "rg_mid" + tag)
            y = _mm(act, w["rg_w_out"], name="rg_out" + tag)
        else:
            pj = _mm(h, w["gla_w_in"], tm_max=512, name="gla_in" + tag)
            act, *aux = _gla_mid_fwd(pj, w["gla_w_alpha"], row(w["gla_b_alpha"]), row(w["gla_norm_g"]), "gla_mid" + tag)
            y = _mm(act, w["gla_w_out"], name="gla_out" + tag)
        x1, h2 = _post_norm_fwd(x, y, g1, gt_m, g2, sc_f, sh_f, "post_mix" + tag)
        arrive(f"ffn{i}", x1)
        p = _mm(h2, w["ffn_w_up"][i], w_slots=N_CHIP, name="ffn_up" + tag)
        a, ga, gb = _ffn_mid_fwd(p, w["ffn_conv_w"][i], w["ffn_conv_b"][i:i + 1], "ffn_mid" + tag)
        y2 = _mm(a, w["ffn_w_down"][i], name="ffn_down" + tag)
        saved_h = h
        if i + 1 < depth:
            nxt = [mod[i + 1, j:j + 1] for j in range(2)] + [w["norm_g"][i + 1, 0:1]]
            x2, h = _post_norm_fwd(x1, y2, g3, gt_f, nxt[2], nxt[1], nxt[0], "post_ffn" + tag)
        else:
            x2 = None
            cols, dx = _post_loss(x1, y2, g3, gt_f, tgt, "post_ffn_loss")
        saved.append((x, saved_h, pj, act, aux, y, x1, h2, p, (a, ga, gb), y2))
        x = x2

    stacked = ("norm_g", "ffn_conv_w", "ffn_conv_b", "mod")
    gr = {k: [None] * depth for k in stacked + ("ffn_w_up", "ffn_w_down")}
    told = lambda stage: done(stage, gr) if done is not None else 0.0
    told_later = lambda stage, after: later(stage, after) if later is not None else 0.0
    for i in reversed(range(depth)):
        x0, h, pj, act, aux, y, x1, h2, p, (a, ga, gb), y2 = saved[i]
        sh_m, sc_m, gt_m, sh_f, sc_f, gt_f = (mod[i, j:j + 1] for j in range(6))
        g0, g1, g2, g3 = (w["norm_g"][i, j:j + 1] for j in range(4))
        tag = f"_l{i}"
        if i == depth - 1:
            dy2, d_g3, d_gt_f = _post_bwd(dx, y2, g3, gt_f, "post_ffn_b" + tag)
        else:
            dy2, d_g3, d_gt_f = ahead
        da = _mm(dy2, w["ffn_w_down"][i], tb=True, name="ffn_down_dx" + tag)
        gr["ffn_w_down"][i] = _mm(a, dy2, ta=True, name="ffn_down_dw" + tag)
        conv_w = w["ffn_conv_w"][i] + (told_later("l1", da) if i == 0 else 0.0)
        dp, dcw, dcb = _ffn_mid_bwd(da, p, ga, gb, conv_w, "ffn_mid_b" + tag)
        gr["ffn_conv_w"][i], gr["ffn_conv_b"][i] = _cat(dcw[0], dcw[1]), _cat(dcb[0], dcb[1])[0]
        dh2 = _mm(dp, w["ffn_w_up"][i], tb=True, a_parts=2, w_slots=N_CHIP, name="ffn_up_dx" + tag)
        gr["ffn_w_up"][i] = _mm(h2, dp, ta=True, b_parts=2, out_slots=N_CHIP, name="ffn_up_dw" + tag)
        if i == 0:
            gt_m = gt_m + told("ffn0")
        dx1, dy, d_g2, d_sc_f, d_sh_f, d_g1, d_gt_m = _norm_post_bwd(dh2, x1, g2, sc_f, dx, y, g1, gt_m, "norm_ffn_b" + tag)
        if i % 2 == 0:
            dact = _mm(dy, w["rg_w_out"], tb=True, name="rg_out_dx" + tag)
            gr["rg_w_out"] = _mm(act, dy, ta=True, name="rg_out_dw" + tag)
            lam = row(w["rg_lambda"]) + told_later("ffn0", gr["rg_w_out"])
            dpj, gr["rg_conv_w"], d_cb, gr["rg_wa"], d_ba, gr["rg_wx"], d_bx, d_lam = _rg_mid_bwd(
                dact, pj, aux, w["rg_conv_w"], row(w["rg_conv_b"]), w["rg_wa"], row(w["rg_ba"]), w["rg_wx"],
                row(w["rg_bx"]), lam, "rg_mid_b" + tag)
            gr["rg_conv_b"], gr["rg_ba"], gr["rg_bx"], gr["rg_lambda"] = d_cb[0], d_ba[0], d_bx[0], d_lam[0]
            dh = _mm(dpj, w["rg_w_in"], tb=True, a_parts=2, w_slots=N_CHIP, name="rg_in_dx" + tag)
            gr["rg_w_in"] = _mm(h, dpj, ta=True, b_parts=2, out_slots=N_CHIP, name="rg_in_dw" + tag)
            sc_m = sc_m + told("rg")
        else:
            dact = _mm(dy, w["gla_w_out"], tb=True, name="gla_out_dx" + tag)
            gr["gla_w_out"] = _mm(act, dy, ta=True, name="gla_out_dw" + tag)
            dpj, gr["gla_w_alpha"], d_bal, d_ng = _gla_mid_bwd(dact, pj, aux[0], aux[1], w["gla_w_alpha"], row(w["gla_b_alpha"]),
                                                               row(w["gla_norm_g"]), "gla_mid_b" + tag)
            gr["gla_b_alpha"], gr["gla_norm_g"] = d_bal[0], d_ng[0]
            dh = _mm(dpj, w["gla_w_in"], tb=True, name="gla_in_dx" + tag)
            gr["gla_w_in"] = _mm(h, dpj, ta=True, tm_max=512, name="gla_in_dw" + tag)
            mod = mod.at[0].add(told("l1"))
        if i > 0:
            dx, dy_below, d_g0, d_sc_m, d_sh_m, d_g_below, d_gt_below = _norm_post_bwd(
                dh, x0, g0, sc_m, dx1, saved[i - 1][-1], w["norm_g"][i - 1, 3:4], mod[i - 1, 5:6], "norm_mix_b" + tag)
            ahead = (dy_below, d_g_below, d_gt_below)
        else:
            dx, d_g0, d_sc_m, d_sh_m = _norm_mod_bwd(dh, x0, g0, sc_m, dx1, "norm_mix_b" + tag)
        gr["norm_g"][i] = jnp.concatenate([d_g0, d_g1, d_g2, d_g3], axis=0)
        gr["mod"][i] = jnp.concatenate([d_sh_m, d_sc_m, d_gt_m, d_sh_f, d_sc_f, d_gt_f], axis=0)
    for k in stacked:
        gr[k] = jnp.stack(gr[k])
    return cols, dx, gr


ADA_ROWS = 16


def _ada_fwd(c16, ada_w, ada_b, name):
    depth, d, n = ada_w.shape
    tn = _tile(n, (512, 256, 128))

    def body(c_ref, w_ref, b_ref, o_ref):
        cv = c_ref[...]
        o_ref[0] = _dot_nn(cv * _sigmoid(cv), w_ref[0]) + b_ref[0]

    return pl.pallas_call(
        body, grid=(depth, n // tn),
        in_specs=[pl.BlockSpec((ADA_ROWS, d), lambda l, j: (0, 0)), pl.BlockSpec((1, d, tn), lambda l, j: (l, 0, j)),
                  pl.BlockSpec((1, 1, tn), lambda l, j: (l, 0, j))],
        out_specs=pl.BlockSpec((1, ADA_ROWS, tn), lambda l, j: (l, 0, j)),
        out_shape=jax.ShapeDtypeStruct((depth, ADA_ROWS, n), F32),
        compiler_params=_params("parallel", "parallel"), name=name,
    )(c16, ada_w, ada_b)


def _ada_bwd(c16, dmod16, name):
    depth, _, n = dmod16.shape
    d = c16.shape[1]
    tn = _tile(n, (512, 256, 128))

    def body(c_ref, dm_ref, o_ref):
        cv = c_ref[...]
        o_ref[0] = _dot_tn(cv * _sigmoid(cv), dm_ref[0])

    return pl.pallas_call(
        body, grid=(depth, n // tn),
        in_specs=[pl.BlockSpec((ADA_ROWS, d), lambda l, j: (0, 0)), pl.BlockSpec((1, ADA_ROWS, tn), lambda l, j: (l, 0, j))],
        out_specs=pl.BlockSpec((1, d, tn), lambda l, j: (l, 0, j)),
        out_shape=jax.ShapeDtypeStruct((depth, d, n), F32),
        compiler_params=_params("parallel", "parallel"), name=name,
    )(c16, dmod16)


PACK_COLS = 1024
_ANY = pl.BlockSpec(memory_space=pl.ANY)
_VMEM = pl.BlockSpec(memory_space=pltpu.VMEM)


def _place():
    return lax.axis_index("x"), lax.axis_index("y"), lax.axis_index("c")


def _other_chips(x, y):
    return [(1 - x, y), (x, 1 - y), (1 - x, 1 - y)]


def _rcopy(src, dst, send_sems, recv_sems, k, peer):
    return pltpu.make_async_remote_copy(src_ref=src, dst_ref=dst, send_sem=send_sems.at[k], recv_sem=recv_sems.at[k],
                                        device_id=peer, device_id_type=MESH)


def _all_gather_8(v, name):
    r, cc = v.shape

    def body(v_ref, out_ref, send_sems, recv_sems, local_sem):
        x, y, c = _place()
        me = 4 * x + 2 * y + c
        mine = pltpu.make_async_copy(v_ref, out_ref.at[me], local_sem)
        mine.start()
        peers = []
        for k in range(1, N_DEV):
            px = 1 - x if k & 4 else x
            py = 1 - y if k & 2 else y
            pc = 1 - c if k & 1 else c
            peers.append((px, py, pc))
        sends = [_rcopy(v_ref, out_ref.at[me], send_sems, recv_sems, k, p) for k, p in enumerate(peers)]
        for cp in sends:
            cp.start()
        for k, (px, py, pc) in enumerate(peers):
            _rcopy(v_ref, out_ref.at[4 * px + 2 * py + pc], send_sems, recv_sems, k, (px, py, pc)).wait_recv()
        for cp in sends:
            cp.wait_send()
        mine.wait()

    return pl.pallas_call(
        body, in_specs=[_VMEM], out_specs=_VMEM, out_shape=jax.ShapeDtypeStruct((N_DEV, r, cc), v.dtype),
        scratch_shapes=[pltpu.SemaphoreType.DMA((N_DEV - 1,)), pltpu.SemaphoreType.DMA((N_DEV - 1,)), pltpu.SemaphoreType.DMA],
        compiler_params=pltpu.CompilerParams(vmem_limit_bytes=VMEM_LIMIT), name=name,
    )(v)


def _gather_chips(shards, name):
    n = len(shards)
    per = 2 * (N_CHIP - 1)

    def body(*refs):
        ins, outs, (send_sems, recv_sems) = refs[:n], refs[n:2 * n], refs[2 * n:]
        x, y, c = _place()
        chip = 2 * x + y
        chips = _other_chips(x, y)
        rows = [(pl.ds(c * (r.shape[0] // 2), r.shape[0] // 2), pl.ds((1 - c) * (r.shape[0] // 2), r.shape[0] // 2)) for r in ins]
        first = [_rcopy(ins[i].at[rows[i][0]], outs[i].at[chip, rows[i][0]], send_sems, recv_sems, per * i + j, (px, py, c))
                 for i in range(n) for j, (px, py) in enumerate(chips)]
        for cp in first:
            cp.start()
        passed = []
        for i in range(n):
            for j, (px, py) in enumerate(chips):
                landed = outs[i].at[2 * px + py, rows[i][0]]
                _rcopy(ins[i].at[rows[i][0]], landed, send_sems, recv_sems, per * i + j, (px, py, c)).wait_recv()
                fw = _rcopy(landed, landed, send_sems, recv_sems, per * i + N_CHIP - 1 + j, (x, y, 1 - c))
                fw.start()
                passed.append(fw)
        for i in range(n):
            for j, (px, py) in enumerate(chips):
                landed = outs[i].at[2 * px + py, rows[i][1]]
                _rcopy(landed, landed, send_sems, recv_sems, per * i + N_CHIP - 1 + j, (x, y, 1 - c)).wait_recv()
        for cp in first + passed:
            cp.wait_send()

    return pl.pallas_call(
        body, in_specs=[_ANY] * n, out_specs=[_ANY] * n,
        out_shape=[jax.ShapeDtypeStruct((N_CHIP,) + sh.shape, sh.dtype) for sh in shards],
        scratch_shapes=[pltpu.SemaphoreType.DMA((per * n,)), pltpu.SemaphoreType.DMA((per * n,))], name=name,
    )(*shards)


def _pair_exchange(gs, name):
    n = len(gs)

    def body(*refs):
        ins, outs, (send_sems, recv_sems) = refs[:n], refs[n:2 * n], refs[2 * n:]
        x, y, c = _place()
        copies = []
        for i in range(n):
            half = ins[i].shape[1] // 2
            copies.append(_rcopy(ins[i].at[:, pl.ds((1 - c) * half, half)], outs[i], send_sems, recv_sems, i, (x, y, 1 - c)))
        for cp in copies:
            cp.start()
        for cp in copies:
            cp.wait()

    return pl.pallas_call(
        body, in_specs=[_ANY] * n, out_specs=[_ANY] * n,
        out_shape=[jax.ShapeDtypeStruct((g.shape[0], g.shape[1] // 2, g.shape[2]), g.dtype) for g in gs],
        scratch_shapes=[pltpu.SemaphoreType.DMA((n,)), pltpu.SemaphoreType.DMA((n,))], name=name,
    )(*gs)


_ROW_TILES = (640, 512, 352, 256, 128, 64, 32, 16)


def _pair_sum(g, other, c_idx, name):
    n, half, cc = other.shape
    tr = _tile(half, _ROW_TILES)

    def body(c_ref, g_ref, o_ref, out_ref):
        out_ref[...] = (g_ref[...] + o_ref[...]).astype(out_ref.dtype)

    return pl.pallas_call(
        body,
        grid_spec=pltpu.PrefetchScalarGridSpec(
            num_scalar_prefetch=1, grid=(n, half // tr),
            in_specs=[pl.BlockSpec((None, None, tr, cc), lambda k, i, c_ref: (k, c_ref[0], i, 0)),
                      pl.BlockSpec((None, tr, cc), lambda k, i, c_ref: (k, i, 0))],
            out_specs=pl.BlockSpec((None, tr, cc), lambda k, i, c_ref: (k, i, 0))),
        out_shape=jax.ShapeDtypeStruct((n, half, cc), BF16),
        compiler_params=_params("parallel", "parallel"), name=name,
    )(c_idx, g.reshape(n, 2, half, cc), other)


def _chip_exchange(ps, name):
    n = len(ps)
    per = N_CHIP - 1

    def body(*refs):
        ins, outs, (send_sems, recv_sems) = refs[:n], refs[n:2 * n], refs[2 * n:]
        x, y, c = _place()
        chip = 2 * x + y
        chips = _other_chips(x, y)
        sends = [_rcopy(ins[i].at[2 * px + py], outs[i].at[chip], send_sems, recv_sems, per * i + j, (px, py, c))
                 for i in range(n) for j, (px, py) in enumerate(chips)]
        for cp in sends:
            cp.start()
        for i in range(n):
            for j, (px, py) in enumerate(chips):
                _rcopy(ins[i].at[chip], outs[i].at[2 * px + py], send_sems, recv_sems, per * i + j, (px, py, c)).wait_recv()
        for cp in sends:
            cp.wait_send()

    return pl.pallas_call(
        body, in_specs=[_ANY] * n, out_specs=[_ANY] * n, out_shape=[jax.ShapeDtypeStruct(p.shape, p.dtype) for p in ps],
        scratch_shapes=[pltpu.SemaphoreType.DMA((per * n,)), pltpu.SemaphoreType.DMA((per * n,))], name=name,
    )(*ps)


_HBM = pl.BlockSpec(memory_space=pltpu.HBM)
_SEM = pl.BlockSpec(memory_space=pltpu.SEMAPHORE)
_DATAFLOW = pltpu.SideEffectType.DATAFLOW_SIDE_EFFECTING


def _split_copies(srcs, lands, send_sems, recv_sems, mode, arriving):
    x, y, c = _place()
    chip = 2 * x + y
    out = []
    for i, (src, land) in enumerate(zip(srcs, lands)):
        if mode == "all":
            for k in range(1, N_DEV):
                px, py, pc = (1 - x if k & 4 else x), (1 - y if k & 2 else y), (1 - c if k & 1 else c)
                slot = 4 * px + 2 * py + pc if arriving else 2 * chip + c
                out.append(_rcopy(src, land.at[slot], send_sems, recv_sems, (N_DEV - 1) * i + k - 1, (px, py, pc)))
            continue
        if mode == "pair":
            half = src.shape[1] // 2
            out.append(_rcopy(src.at[:, pl.ds((1 - c) * half, half)], land, send_sems, recv_sems, i, (x, y, 1 - c)))
            continue
        for j, (px, py) in enumerate(_other_chips(x, y)):
            there = 2 * px + py
            part = src.at[there] if mode == "slots" else src
            out.append(_rcopy(part, land.at[there if arriving else chip], send_sems, recv_sems, (N_CHIP - 1) * i + j, (px, py, c)))
    return out


def _land_shape(src, mode):
    if mode == "pair":
        return (src.shape[0], src.shape[1] // 2, src.shape[2])
    if mode == "all":
        return (N_DEV,) + src.shape
    return (N_CHIP,) + (src.shape[1:] if mode == "slots" else src.shape)


def _send_start(srcs, mode, name):
    n = len(srcs)
    n_sem = {"pair": 1, "all": N_DEV - 1}.get(mode, N_CHIP - 1) * n
    lands = [lax.empty(_land_shape(s, mode), s.dtype) for s in srcs]

    def body(*refs):
        ins, zones, (send_sems, recv_sems) = refs[:n], refs[n:2 * n], refs[2 * n:2 * n + 2]
        for cp in _split_copies(ins, zones, send_sems, recv_sems, mode, False):
            cp.start()
        refs[-1][...] = jnp.zeros_like(refs[-1])

    hbm = lambda a: pltpu.HBM(a.shape, a.dtype)
    outs = pl.pallas_call(
        body, name=name, in_specs=[_HBM] * (2 * n),
        out_shape=(pltpu.SemaphoreType.DMA((n_sem,)), pltpu.SemaphoreType.DMA((n_sem,)), *[hbm(a) for a in srcs],
                   *[hbm(a) for a in lands], jax.ShapeDtypeStruct((SUBLANES, LANES), F32)),
        out_specs=(_SEM, _SEM, *[_HBM] * (2 * n), _VMEM), input_output_aliases={i: 2 + i for i in range(2 * n)},
        compiler_params=pltpu.CompilerParams(has_side_effects=_DATAFLOW),
    )(*[pltpu.with_memory_space_constraint(a, pltpu.HBM) for a in list(srcs) + lands])
    return (outs[0], outs[1], list(outs[2:2 + n]), list(outs[2 + n:2 + 2 * n])), outs[-1]


def _send_wait(state, after, mode, name):
    send_sems, recv_sems, srcs, lands = state
    n = len(srcs)

    def body(*refs):
        ins, zones, (send_s, recv_s) = refs[:n], refs[n:2 * n], refs[2 * n:2 * n + 2]
        for cp in _split_copies(ins, zones, send_s, recv_s, mode, True):
            cp.wait_send()
            cp.wait_recv()

    hbm = lambda a: pltpu.HBM(a.shape, a.dtype)
    outs = pl.pallas_call(
        body, name=name, in_specs=[_HBM] * (2 * n) + [_SEM, _SEM, _ANY],
        out_shape=tuple(hbm(a) for a in srcs + lands), out_specs=tuple([_HBM] * (2 * n)),
        input_output_aliases={i: i for i in range(2 * n)},
        compiler_params=pltpu.CompilerParams(has_side_effects=_DATAFLOW),
    )(*srcs, *lands, send_sems, recv_sems, after)
    return list(outs[:n]), list(outs[n:])


def _sum_lead(v, name):
    n, r, cc = v.shape
    tr = _tile(r, _ROW_TILES + (8,))

    def body(v_ref, o_ref):
        acc = v_ref[0].astype(F32)
        for k in range(1, n):
            acc = acc + v_ref[k].astype(F32)
        o_ref[...] = acc

    return pl.pallas_call(
        body, grid=(r // tr,), in_specs=[pl.BlockSpec((n, tr, cc), lambda i: (0, i, 0))],
        out_specs=pl.BlockSpec((tr, cc), lambda i: (i, 0)), out_shape=jax.ShapeDtypeStruct((r, cc), F32),
        compiler_params=_params("parallel"), name=name,
    )(v)


def _chip_sum(arrived, mine, chip_idx, name):
    n, r, cc = arrived.shape
    tr = _tile(r, _ROW_TILES)

    def body(chip_ref, a_ref, m_ref, o_ref):
        acc = jnp.zeros((tr, cc), F32)
        for k in range(n):
            acc = acc + jnp.where(chip_ref[0] == k, m_ref[...], a_ref[k]).astype(F32)
        o_ref[...] = acc

    return pl.pallas_call(
        body,
        grid_spec=pltpu.PrefetchScalarGridSpec(
            num_scalar_prefetch=1, grid=(r // tr,),
            in_specs=[pl.BlockSpec((n, tr, cc), lambda i, chip_ref: (0, i, 0)),
                      pl.BlockSpec((None, tr, cc), lambda i, chip_ref: (chip_ref[0], i, 0))],
            out_specs=pl.BlockSpec((tr, cc), lambda i, chip_ref: (i, 0))),
        out_shape=jax.ShapeDtypeStruct((r, cc), F32), compiler_params=_params("parallel"), name=name,
    )(chip_idx, arrived, mine)


def _pair_share(reds, name):
    n = len(reds)

    def body(*refs):
        ins, outs, (send_sems, recv_sems) = refs[:n], refs[n:2 * n], refs[2 * n:]
        x, y, c = _place()
        copies = [_rcopy(ins[i], outs[i].at[c], send_sems, recv_sems, i, (x, y, 1 - c)) for i in range(n)]
        for cp in copies:
            cp.start()
        for i in range(n):
            _rcopy(ins[i], outs[i].at[1 - c], send_sems, recv_sems, i, (x, y, 1 - c)).wait_recv()
        for cp in copies:
            cp.wait_send()

    return pl.pallas_call(
        body, in_specs=[_ANY] * n, out_specs=[_ANY] * n, out_shape=[jax.ShapeDtypeStruct((2,) + r.shape, r.dtype) for r in reds],
        scratch_shapes=[pltpu.SemaphoreType.DMA((n,)), pltpu.SemaphoreType.DMA((n,))], name=name,
    )(*reds)


def _pack(arrs, rows_multiple, dtype):
    flat = jnp.concatenate([a.reshape(-1).astype(dtype) for a in arrs])
    unit = rows_multiple * PACK_COLS
    total = -(-flat.shape[0] // unit) * unit
    return jnp.pad(flat, (0, total - flat.shape[0])).reshape(-1, PACK_COLS)


def _unpack(buf, shapes):
    lead = buf.shape[:-2]
    flat = buf.reshape(*lead, -1)
    out, off = [], 0
    for shp in shapes:
        n = 1
        for s in shp:
            n *= s
        out.append(flat[..., off:off + n].reshape(*lead, *shp))
        off += n
    return out


def _join_shards(parts, axis):
    moved = jnp.moveaxis(parts, 0, axis)
    shp = list(moved.shape)
    shp[axis:axis + 2] = [shp[axis] * shp[axis + 1]]
    return moved.reshape(shp)


def _my_shard(full, axis, chip):
    n = full.shape[axis] // N_CHIP
    return lax.dynamic_slice_in_dim(full, chip * n, n, axis)


SMALL = {"norm_g": 2, "ffn_conv_w": 2, "rg_conv_w": 2, "gla_w_alpha": 2, "gla_b_alpha": 1, "gla_norm_g": 1,
         "ada_b": None, "ffn_conv_b": None, "rg_conv_b": None, "rg_ba": None, "rg_bx": None, "rg_lambda": None}
BIG = {"rg_w_in": True, "rg_wa": False, "rg_wx": False, "rg_w_out": False, "ffn_w_up": True, "ffn_w_down": False,
       "gla_w_in": True, "gla_w_out": False}
WEIGHTS = ["ada_w", "ada_b", "norm_g", "ffn_w_up", "ffn_conv_w", "ffn_conv_b", "ffn_w_down", "rg_w_in", "rg_conv_w", "rg_conv_b",
           "rg_wa", "rg_ba", "rg_wx", "rg_bx", "rg_lambda", "rg_w_out", "gla_w_in", "gla_w_alpha", "gla_b_alpha", "gla_norm_g",
           "gla_w_out"]


def kernel(x, c, ada_w, ada_b, norm_g, ffn_w_up, ffn_conv_w, ffn_conv_b, ffn_w_down, rg_w_in, rg_conv_w, rg_conv_b, rg_wa, rg_ba, rg_wx, rg_bx, rg_lambda, rg_w_out, gla_w_in, gla_w_alpha, gla_b_alpha, gla_norm_g, gla_w_out, loss_target, m_ada_w, m_ada_b, m_norm_g, m_ffn_w_up, m_ffn_conv_w, m_ffn_conv_b, m_ffn_w_down, m_rg_w_in, m_rg_conv_w, m_rg_conv_b, m_rg_wa, m_rg_ba, m_rg_wx, m_rg_bx, m_rg_lambda, m_rg_w_out, m_gla_w_in, m_gla_w_alpha, m_gla_b_alpha, m_gla_norm_g, m_gla_w_out, v_ada_w, v_ada_b, v_norm_g, v_ffn_w_up, v_ffn_conv_w, v_ffn_conv_b, v_ffn_w_down, v_rg_w_in, v_rg_conv_w, v_rg_conv_b, v_rg_wa, v_rg_ba, v_rg_wx, v_rg_bx, v_rg_lambda, v_rg_w_out, v_gla_w_in, v_gla_w_alpha, v_gla_b_alpha, v_gla_norm_g, v_gla_w_out):
    wts = dict(ada_w=ada_w, ada_b=ada_b, norm_g=norm_g, ffn_w_up=ffn_w_up, ffn_conv_w=ffn_conv_w, ffn_conv_b=ffn_conv_b,
               ffn_w_down=ffn_w_down, rg_w_in=rg_w_in, rg_conv_w=rg_conv_w, rg_conv_b=rg_conv_b, rg_wa=rg_wa, rg_ba=rg_ba,
               rg_wx=rg_wx, rg_bx=rg_bx, rg_lambda=rg_lambda, rg_w_out=rg_w_out, gla_w_in=gla_w_in, gla_w_alpha=gla_w_alpha,
               gla_b_alpha=gla_b_alpha, gla_norm_g=gla_norm_g, gla_w_out=gla_w_out)
    mom1 = dict(ada_w=m_ada_w, ada_b=m_ada_b, norm_g=m_norm_g, ffn_w_up=m_ffn_w_up, ffn_conv_w=m_ffn_conv_w,
                ffn_conv_b=m_ffn_conv_b, ffn_w_down=m_ffn_w_down, rg_w_in=m_rg_w_in, rg_conv_w=m_rg_conv_w,
                rg_conv_b=m_rg_conv_b, rg_wa=m_rg_wa, rg_ba=m_rg_ba, rg_wx=m_rg_wx, rg_bx=m_rg_bx, rg_lambda=m_rg_lambda,
                rg_w_out=m_rg_w_out, gla_w_in=m_gla_w_in, gla_w_alpha=m_gla_w_alpha, gla_b_alpha=m_gla_b_alpha,
                gla_norm_g=m_gla_norm_g, gla_w_out=m_gla_w_out)
    mom2 = dict(ada_w=v_ada_w, ada_b=v_ada_b, norm_g=v_norm_g, ffn_w_up=v_ffn_w_up, ffn_conv_w=v_ffn_conv_w,
                ffn_conv_b=v_ffn_conv_b, ffn_w_down=v_ffn_w_down, rg_w_in=v_rg_w_in, rg_conv_w=v_rg_conv_w,
                rg_conv_b=v_rg_conv_b, rg_wa=v_rg_wa, rg_ba=v_rg_ba, rg_wx=v_rg_wx, rg_bx=v_rg_bx, rg_lambda=v_rg_lambda,
                rg_w_out=v_rg_w_out, gla_w_in=v_gla_w_in, gla_w_alpha=v_gla_w_alpha, gla_b_alpha=v_gla_b_alpha,
                gla_norm_g=v_gla_norm_g, gla_w_out=v_gla_w_out)
    xi, yi, ci = _place()
    chip, me = 2 * xi + yi, 4 * xi + 2 * yi + ci
    d = x.shape[-1]
    depth = ada_w.shape[0]
    n_ada = ada_w.shape[-1]
    sharded_small = [k for k, ax in SMALL.items() if ax is not None]

    sm = _all_gather_8(_pack([c] + [wts[k] for k in sharded_small], SUBLANES, F32), "gather_small")
    c_all = sm[:, 0, :]
    parts = _unpack(sm[0::2], [c.shape] + [wts[k].shape for k in sharded_small])[1:]
    full = {k: _join_shards(p, SMALL[k]) for k, p in zip(sharded_small, parts)}
    for k, ax in SMALL.items():
        if ax is None:
            full[k] = wts[k]

    c16 = jnp.pad(c_all, ((0, ADA_ROWS - N_DEV), (0, 0)))
    ada_b_mine = lax.dynamic_slice_in_dim(ada_b, chip * n_ada, n_ada, 1)[:, None, :]
    mod_cols = _ada_fwd(c16, ada_w, ada_b_mine, "ada_fwd")
    mod_all = _all_gather_8(mod_cols.reshape(-1, PACK_COLS), "gather_mod")[0::2].reshape(N_CHIP, depth, ADA_ROWS, n_ada)
    mod = jnp.swapaxes(lax.dynamic_index_in_dim(mod_all, me, 2, keepdims=False), 0, 1).reshape(depth, 6, d)

    items = [(k, l) for k in BIG for l in range(wts[k].shape[0])]
    stage_of = lambda k, l: "rg" if k.startswith("rg_") else ("ffn0" if (k.startswith("ffn_") and l == 0) else "l1")
    staged = {st: [it for it in items if stage_of(*it) == st] for st in ("rg", "ffn0", "l1")}
    staged["gla"] = [it for it in staged["l1"] if it[0].startswith("gla_")]
    staged["ffn1"] = [it for it in staged["l1"] if it[0].startswith("ffn_")]
    staged["l1"] = staged["gla"] + staged["ffn1"]
    shard = lambda k, l: wts[k][l].reshape(-1, wts[k].shape[-1]).astype(BF16)
    own = lambda got, mine: [lax.dynamic_update_index_in_dim(g, m, chip, 0) for g, m in zip(got, mine)]
    rows_joined = lambda v: v.reshape(-1, v.shape[-1])

    def placed(its, slots):
        out = {"ffn_w_up": {}, "ffn_w_down": {}}
        for (k, l), v in zip(its, slots):
            if k == "ffn_w_up":
                out[k][l] = v
            elif k == "ffn_w_down":
                out[k][l] = rows_joined(v)
            elif k in ("rg_wa", "rg_wx"):
                out[k] = _slots_to_block_rows(v, RG_BLOCKS)
            elif k == "gla_w_in":
                out[k] = _from_col_slots(v, "gla_w_in_join")
            else:
                out[k] = v if BIG[k] else rows_joined(v)
        return out

    after_mod = (mod[0, 0, 0] * 0.0).astype(BF16)
    sh_rg = [shard(k, l) + after_mod for k, l in staged["rg"]]
    local = {k: (v if k in ("norm_g", "ffn_conv_w", "ffn_conv_b") else v[0]) for k, v in full.items()}
    local.update(placed(staged["rg"], own(_gather_chips(sh_rg, "gather_weights_rg"), sh_rg)))
    sh_late, flying = {}, {}
    after_rg = (local["rg_w_out"][0, 0].astype(F32) * 0.0).astype(BF16)
    sh_late["ffn0"] = [shard(k, l) + after_rg for k, l in staged["ffn0"]]
    flying["ffn0"], tok = _send_start(sh_late["ffn0"], "whole", "weights_ffn0_start")
    for stage in ("gla", "ffn1"):
        sh_late[stage] = [shard(k, l) + tok[0, 0].astype(BF16) for k, l in staged[stage]]
        flying[stage], tok = _send_start(sh_late[stage], "whole", f"weights_{stage}_start")
    mod = mod + tok[0, 0]

    def fetch(stage, after):
        mine, got = _send_wait(flying[stage], after, "whole", f"weights_{stage}_wait")
        return placed(staged[stage], own(got, mine))

    c_idx = ci.reshape(1).astype(jnp.int32)
    gslots, paired, psums, sent, started = {}, {}, {}, {}, {}

    def grad_slots(gr, k, l):
        g = gr[k][l] if k in ("ffn_w_up", "ffn_w_down") else gr[k]
        if k in ("rg_wa", "rg_wx"):
            return _block_rows_to_slots(g)
        if k == "gla_w_in":
            return _col_slots(g, "gla_w_in_grad_slots")
        return g if BIG[k] else g.reshape(N_CHIP, -1, g.shape[-1])

    def done(stage, gr):
        gslots[stage] = [grad_slots(gr, k, l) for k, l in staged[stage]]
        paired[stage], token = _send_start(gslots[stage], "pair", f"grads_{stage}_pair_start")
        return token[0, 0]

    def later(stage, after):
        mine, theirs = _send_wait(paired[stage], after, "pair", f"grads_{stage}_pair_wait")
        psums[stage] = [_pair_sum(g, t, c_idx, f"grads_pair_sum_{k}{l}") for (k, l), g, t in zip(staged[stage], mine, theirs)]
        sent[stage], started[stage] = _send_start(psums[stage], "slots", f"grads_{stage}_start")
        return started[stage][0, 0]

    cols, grad_x, gr = _local_step(x[0], loss_target[0], mod, local, fetch, done, later)
    loss = lax.psum(0.5 * jnp.sum(cols) / d, ("x", "y", "c"))

    small_names = [k for k in SMALL if k != "ada_b"]
    small_flying, small_sent = _send_start([_pack([gr[k] for k in small_names] + [gr["mod"]], SUBLANES, F32)], "all",
                                           "grads_small_start")
    small_shapes = [full[k].shape for k in small_names] + [(depth, 6 * d)]
    chip_idx = chip.reshape(1).astype(jnp.int32)
    delta, new_m, new_v = {}, {}, {}
    grads = {}

    def reduce_and_update(stages, after, dep):
        its = [(st, n) for st in stages for n in range(len(staged[st]))]
        back = {st: _send_wait(sent[st], after, "slots", f"grads_{st}_wait") for st in stages}
        halves = [_chip_sum(back[st][1][n], back[st][0][n], chip_idx, "grads_chip_sum_%s%d" % staged[st][n]) for st, n in its]
        shared = _pair_share(halves, "grads_pair_share_" + stages[0])
        reduced = [lax.dynamic_update_index_in_dim(s2, h, ci, 0).reshape(-1, h.shape[-1]) for s2, h in zip(shared, halves)]
        last = None
        for k in BIG:
            gs_k = [g for (st, n), g in zip(its, reduced) if staged[st][n][0] == k]
            if gs_k:
                last = update(k, gs_k, dep)
        return last

    def update(k, gs_k, dep=None):
        shp = wts[k].shape
        if k == "gla_w_in":
            view, back = (lambda a: jnp.swapaxes(a, 1, 2)), (lambda o: jnp.swapaxes(o, 1, 2))
            gs_k = [g.T for g in gs_k]
        else:
            view, back = (lambda a: a.reshape(a.shape[0], -1, a.shape[-1])), (lambda o: o.reshape(shp))
        outs = _adamw(view(wts[k]), gs_k, view(mom1[k]), view(mom2[k]), "adamw_" + k, dep)
        grads[k], delta[k], new_m[k], new_v[k] = (back(o) for o in outs)
        return new_v[k]

    later("rg", small_sent)
    done_late = reduce_and_update(("ffn0", "l1"), grad_x, started["rg"])
    (small_mine,), (gs,) = _send_wait(small_flying, done_late, "all", "grads_small_wait")
    gs = lax.dynamic_update_index_in_dim(gs, small_mine, 2 * chip + ci, 0)
    *small_sum, g_ada_b = _unpack(_sum_lead(gs, "sum_small_grads"), small_shapes)
    grads.update(zip(small_names, small_sum))
    grads["ada_b"] = g_ada_b
    for k in sharded_small:
        grads[k] = _my_shard(grads[k], SMALL[k], chip)
    dmod_all = _unpack(gs, small_shapes)[-1].reshape(N_DEV, depth, N_CHIP, n_ada)
    dmod_mine = jnp.swapaxes(lax.dynamic_index_in_dim(dmod_all, chip, 2, keepdims=False), 0, 1)
    g_ada_w = _ada_bwd(c16, jnp.pad(dmod_mine, ((0, 0), (0, ADA_ROWS - N_DEV), (0, 0))), "ada_bwd")
    update("ada_w", g_ada_w)
    small_shard_shapes = [wts[k].shape for k in SMALL]
    packed = [_pack([src[k] for k in SMALL], SUBLANES, F32) for src in (wts, grads, mom1, mom2)]
    outs = _adamw(packed[0][None], [packed[1]], packed[2][None], packed[3][None], "adamw_small")
    for dst, o in zip((delta, new_m, new_v), outs[1:]):
        for k, a in zip(SMALL, _unpack(o[0], small_shard_shapes)):
            dst[k] = a
    reduce_and_update(("rg",), outs[3], None)

    return (loss, grad_x[None], *[grads[k] for k in WEIGHTS], *[delta[k] for k in WEIGHTS], *[new_m[k] for k in WEIGHTS],
            *[new_v[k] for k in WEIGHTS])
```

```python
import jax
import jax.numpy as jnp
from jax import lax
from jax.experimental import pallas as pl
from jax.experimental.pallas import tpu as pltpu

F32 = jnp.float32
BF16 = jnp.bfloat16
MXU_DTYPE = BF16

EPS = 1e-6
RG_C = 8.0
RG_BLOCKS = 4
RG_CONV = 4
GLA_HEADS = 4
GLA_TAU = 16.0
GLA_CHUNK = 64
GLA_RANK = 16
FFN_CONV = 3
ADAM_LR = 0.001
ADAM_B1 = 0.9
ADAM_B2 = 0.999
ADAM_EPS = 1e-08
ADAM_WD = 0.01
ADAM_STEP = 10

LANES = 128
SUBLANES = 8
VMEM_LIMIT = 56 * 1024 * 1024
CB = 256
MESH = pl.DeviceIdType.MESH
N_DEV = 8
N_CHIP = 4


def _params(*sem):
    return pltpu.CompilerParams(dimension_semantics=sem, vmem_limit_bytes=VMEM_LIMIT)


def _tile(dim, prefs):
    for p in prefs:
        if dim % p == 0:
            return p
    return dim


def _dot(a, b, dims):
    return lax.dot_general(a.astype(MXU_DTYPE), b.astype(MXU_DTYPE), (dims, ((), ())), preferred_element_type=F32)


def _dot_nn(a, b):
    return _dot(a, b, ((1,), (0,)))


def _dot_nt(a, b):
    return _dot(a, b, ((1,), (1,)))


def _dot_tn(a, b):
    return _dot(a, b, ((0,), (0,)))


def _mm(a, b, *, ta=False, tb=False, a_parts=1, b_parts=1, w_slots=1, out_slots=1, out_dtype=F32, tm_max=1408, name):
    if ta:
        k_dim, m_dim = a.shape
        n_dim = b.shape[-1] * b_parts
    else:
        m_dim, k_dim = a.shape[-2], a.shape[-1] * a_parts
        n_dim = b.shape[-2] if tb else b.shape[-1] * w_slots
    n_unit = n_dim // max(b_parts, out_slots, 1 if tb else w_slots)
    k_unit = k_dim // max(a_parts, w_slots if tb else 1)
    tm = _tile(m_dim, tuple(t for t in (1024, 1408, 512, 256, 128) if t <= max(tm_max, 128)))
    tn = _tile(n_unit, (1024, 1408, 896, 512, 256, 128))
    tk = _tile(k_unit, (1024, 1408, 896, 512, 256, 128))
    nk = k_dim // tk
    dims = ((0 if ta else 1,), (1 if tb else 0,))

    def spec(shape, parts, total, tile, col_grid, row_grid):
        per = total // parts // tile

        def index(i, j, k):
            g = {"i": i, "j": j, "k": k}
            col, row = g[col_grid], g[row_grid]
            return (row, col) if parts == 1 else (col // per, row, col % per)

        return pl.BlockSpec(shape if parts == 1 else (None,) + shape, index)

    def body(a_ref, b_ref, o_ref, *acc):
        if nk == 1:
            o_ref[...] = _dot(a_ref[...], b_ref[...], dims).astype(o_ref.dtype)
            return
        acc_ref, k = acc[0], pl.program_id(2)

        @pl.when(k == 0)
        def _():
            acc_ref[...] = jnp.zeros_like(acc_ref)

        acc_ref[...] += _dot(a_ref[...], b_ref[...], dims)

        @pl.when(k == nk - 1)
        def _():
            o_ref[...] = acc_ref[...].astype(o_ref.dtype)

    if ta:
        a_spec = spec((tk, tm), 1, m_dim, tm, "i", "k")
        b_spec = spec((tk, tn), b_parts, n_dim, tn, "j", "k")
    elif tb:
        a_spec = spec((tm, tk), a_parts, k_dim, tk, "k", "i")
        b_spec = spec((tn, tk), w_slots, k_dim, tk, "k", "j")
    else:
        a_spec = spec((tm, tk), a_parts, k_dim, tk, "k", "i")
        b_spec = spec((tk, tn), w_slots, n_dim, tn, "j", "k")
    out_shape = (m_dim, n_dim) if out_slots == 1 else (out_slots, m_dim, n_dim // out_slots)
    return pl.pallas_call(
        body,
        grid=(m_dim // tm, n_dim // tn, nk),
        in_specs=[a_spec, b_spec],
        out_specs=spec((tm, tn), out_slots, n_dim, tn, "j", "i"),
        out_shape=jax.ShapeDtypeStruct(out_shape, out_dtype),
        scratch_shapes=[pltpu.VMEM((tm, tn), F32)] if nk > 1 else [],
        compiler_params=_params("parallel", "parallel", "arbitrary"),
        name=name,
    )(a, b)


ROW_TILES = (1024, 512)


def _row_specs(s, d, ts):
    return pl.BlockSpec((ts, d), lambda i: (i, 0)), pl.BlockSpec((1, d), lambda i: (0, 0))


def _norm_mod_fwd(x, g, sc, sh, name):
    s, d = x.shape
    ts = _tile(s, ROW_TILES)

    def body(x_ref, g_ref, sc_ref, sh_ref, h_ref):
        xv = x_ref[...]
        r = lax.rsqrt(jnp.mean(xv * xv, axis=-1, keepdims=True) + EPS)
        h_ref[...] = (((xv * r) * g_ref[...]) * (1.0 + sc_ref[...]) + sh_ref[...]).astype(h_ref.dtype)

    row, vec = _row_specs(s, d, ts)
    return pl.pallas_call(
        body, grid=(s // ts,), in_specs=[row, vec, vec, vec], out_specs=row,
        out_shape=jax.ShapeDtypeStruct((s, d), MXU_DTYPE), compiler_params=_params("parallel"), name=name,
    )(x, g, sc, sh)


def _norm_mod_bwd(dh, x, g, sc, dres, name):
    s, d = x.shape
    ts = _tile(s, ROW_TILES)

    def body(dh_ref, x_ref, g_ref, sc_ref, dres_ref, dx_ref, dg_ref, dsc_ref, dsh_ref, acc_ref):
        i = pl.program_id(0)

        @pl.when(i == 0)
        def _():
            acc_ref[...] = jnp.zeros_like(acc_ref)

        xv, dhv = x_ref[...], dh_ref[...]
        r = lax.rsqrt(jnp.mean(xv * xv, axis=-1, keepdims=True) + EPS)
        n = xv * r
        acc_ref[0:1, :] += jnp.sum(dhv * n, axis=0, keepdims=True)
        acc_ref[1:2, :] += jnp.sum(dhv, axis=0, keepdims=True)
        dn = dhv * ((1.0 + sc_ref[...]) * g_ref[...])
        dx_ref[...] = dres_ref[...] + r * (dn - n * jnp.mean(dn * n, axis=-1, keepdims=True))
        dg_ref[...] = (1.0 + sc_ref[...]) * acc_ref[0:1, :]
        dsc_ref[...] = g_ref[...] * acc_ref[0:1, :]
        dsh_ref[...] = acc_ref[1:2, :]

    row, vec = _row_specs(s, d, ts)
    vshape = jax.ShapeDtypeStruct((1, d), F32)
    return pl.pallas_call(
        body, grid=(s // ts,), in_specs=[row, row, vec, vec, row], out_specs=[row, vec, vec, vec],
        out_shape=[jax.ShapeDtypeStruct((s, d), F32), vshape, vshape, vshape],
        scratch_shapes=[pltpu.VMEM((SUBLANES, d), F32)], compiler_params=_params("arbitrary"), name=name,
    )(dh, x, g, sc, dres)


def _post_norm_fwd(x, y, g, gt, g2, sc, sh, name):
    s, d = x.shape
    ts = _tile(s, ROW_TILES)

    def body(x_ref, y_ref, g_ref, gt_ref, g2_ref, sc_ref, sh_ref, o_ref, h_ref):
        yv = y_ref[...]
        r = lax.rsqrt(jnp.mean(yv * yv, axis=-1, keepdims=True) + EPS)
        xn = x_ref[...] + gt_ref[...] * ((yv * r) * g_ref[...])
        o_ref[...] = xn
        r2 = lax.rsqrt(jnp.mean(xn * xn, axis=-1, keepdims=True) + EPS)
        h_ref[...] = (((xn * r2) * g2_ref[...]) * (1.0 + sc_ref[...]) + sh_ref[...]).astype(h_ref.dtype)

    row, vec = _row_specs(s, d, ts)
    return pl.pallas_call(
        body, grid=(s // ts,), in_specs=[row, row] + [vec] * 5, out_specs=[row, row],
        out_shape=[jax.ShapeDtypeStruct((s, d), F32), jax.ShapeDtypeStruct((s, d), MXU_DTYPE)],
        compiler_params=_params("parallel"), name=name,
    )(x, y, g, gt, g2, sc, sh)


def _post_bwd(dxn, y, g, gt, name):
    s, d = y.shape
    ts = _tile(s, ROW_TILES)

    def body(dxn_ref, y_ref, g_ref, gt_ref, dy_ref, dg_ref, dgt_ref, acc_ref):
        i = pl.program_id(0)

        @pl.when(i == 0)
        def _():
            acc_ref[...] = jnp.zeros_like(acc_ref)

        yv, dv = y_ref[...], dxn_ref[...]
        r = lax.rsqrt(jnp.mean(yv * yv, axis=-1, keepdims=True) + EPS)
        n = yv * r
        acc_ref[0:1, :] += jnp.sum(dv * n, axis=0, keepdims=True)
        dn = dv * (gt_ref[...] * g_ref[...])
        dy_ref[...] = (r * (dn - n * jnp.mean(dn * n, axis=-1, keepdims=True))).astype(dy_ref.dtype)
        dg_ref[...] = gt_ref[...] * acc_ref[0:1, :]
        dgt_ref[...] = g_ref[...] * acc_ref[0:1, :]

    row, vec = _row_specs(s, d, ts)
    vshape = jax.ShapeDtypeStruct((1, d), F32)
    return pl.pallas_call(
        body, grid=(s // ts,), in_specs=[row, row, vec, vec], out_specs=[row, vec, vec],
        out_shape=[jax.ShapeDtypeStruct((s, d), MXU_DTYPE), vshape, vshape],
        scratch_shapes=[pltpu.VMEM((SUBLANES, d), F32)], compiler_params=_params("arbitrary"), name=name,
    )(dxn, y, g, gt)


def _norm_post_bwd(dh, x, g, sc, dres, y, gp, gt, name):
    s, d = x.shape
    ts = _tile(s, (512,))

    def body(dh_ref, x_ref, g_ref, sc_ref, dres_ref, y_ref, gp_ref, gt_ref,
             dx_ref, dy_ref, dg_ref, dsc_ref, dsh_ref, dgp_ref, dgt_ref, acc_ref):
        i = pl.program_id(0)

        @pl.when(i == 0)
        def _():
            acc_ref[...] = jnp.zeros_like(acc_ref)

        xv, dhv = x_ref[...], dh_ref[...]
        r = lax.rsqrt(jnp.mean(xv * xv, axis=-1, keepdims=True) + EPS)
        n = xv * r
        acc_ref[0:1, :] += jnp.sum(dhv * n, axis=0, keepdims=True)
        acc_ref[1:2, :] += jnp.sum(dhv, axis=0, keepdims=True)
        dn = dhv * ((1.0 + sc_ref[...]) * g_ref[...])
        dx = dres_ref[...] + r * (dn - n * jnp.mean(dn * n, axis=-1, keepdims=True))
        dx_ref[...] = dx
        yv = y_ref[...]
        ry = lax.rsqrt(jnp.mean(yv * yv, axis=-1, keepdims=True) + EPS)
        ny = yv * ry
        acc_ref[2:3, :] += jnp.sum(dx * ny, axis=0, keepdims=True)
        dny = dx * (gt_ref[...] * gp_ref[...])
        dy_ref[...] = (ry * (dny - ny * jnp.mean(dny * ny, axis=-1, keepdims=True))).astype(dy_ref.dtype)
        dg_ref[...] = (1.0 + sc_ref[...]) * acc_ref[0:1, :]
        dsc_ref[...] = g_ref[...] * acc_ref[0:1, :]
        dsh_ref[...] = acc_ref[1:2, :]
        dgp_ref[...] = gt_ref[...] * acc_ref[2:3, :]
        dgt_ref[...] = gp_ref[...] * acc_ref[2:3, :]

    row, vec = _row_specs(s, d, ts)
    vshape = jax.ShapeDtypeStruct((1, d), F32)
    return pl.pallas_call(
        body, grid=(s // ts,), in_specs=[row, row, vec, vec, row, row, vec, vec], out_specs=[row, row] + [vec] * 5,
        out_shape=[jax.ShapeDtypeStruct((s, d), F32), jax.ShapeDtypeStruct((s, d), MXU_DTYPE)] + [vshape] * 5,
        scratch_shapes=[pltpu.VMEM((SUBLANES, d), F32)], compiler_params=_params("arbitrary"), name=name,
    )(dh, x, g, sc, dres, y, gp, gt)


def _post_loss(x, y, g, gt, tgt, name):
    s, d = x.shape
    ts = _tile(s, ROW_TILES)

    def body(x_ref, y_ref, g_ref, gt_ref, t_ref, col_ref, dx_ref):
        i = pl.program_id(0)

        @pl.when(i == 0)
        def _():
            col_ref[...] = jnp.zeros_like(col_ref)

        yv = y_ref[...]
        r = lax.rsqrt(jnp.mean(yv * yv, axis=-1, keepdims=True) + EPS)
        e = (x_ref[...] + gt_ref[...] * ((yv * r) * g_ref[...])) - t_ref[...]
        col_ref[...] += jnp.sum(e * e, axis=0, keepdims=True)
        dx_ref[...] = e * (1.0 / d)

    row, vec = _row_specs(s, d, ts)
    return pl.pallas_call(
        body, grid=(s // ts,), in_specs=[row, row, vec, vec, row], out_specs=[vec, row],
        out_shape=[jax.ShapeDtypeStruct((1, d), F32), jax.ShapeDtypeStruct((s, d), F32)],
        compiler_params=_params("arbitrary"), name=name,
    )(x, y, g, gt, tgt)


_GELU_C = 0.7978845608028654
_GELU_A = 0.044715


def _gelu(x):
    t = jnp.tanh(_GELU_C * (x + _GELU_A * x * x * x))
    return 0.5 * x * (1.0 + t), t


def _gelu_grad(x, t):
    return 0.5 * (1.0 + t) + 0.5 * x * (1.0 - t * t) * (_GELU_C * (1.0 + 3.0 * _GELU_A * x * x))


def _sigmoid(x):
    return 1.0 / (1.0 + jnp.exp(-x))


def _log1p_pos(y):
    u = 1.0 + y
    return jnp.where(u == 1.0, y, jnp.log(u) * (y / jnp.where(u == 1.0, 1.0, u - 1.0)))


def _softplus(x):
    return jnp.maximum(x, 0.0) + _log1p_pos(jnp.exp(-jnp.abs(x)))


def _one_minus_sq_exp(x, ex):
    z = 2.0 * x
    series = -z * (1.0 + z * (1.0 / 2 + z * (1.0 / 6 + z * (1.0 / 24 + z * (1.0 / 120)))))
    return jnp.where(z > -0.05, series, 1.0 - ex * ex)


SLAB = 16


def _cat(a, b):
    return jnp.concatenate([a, b], axis=1)


def _pair_specs(shape, nb, index):
    return [pl.BlockSpec(shape, lambda j, t: index(j, t) + (j,)), pl.BlockSpec(shape, lambda j, t: index(j, t) + (j + nb,))]


def _halo_row(ts, time_of):
    return lambda j, t: (jnp.maximum(time_of(t) * (ts // SUBLANES) - 1, 0),)


def _rows_from(groups, k):
    row = lax.broadcasted_iota(jnp.int32, groups[0].shape, 0)
    turned = [pltpu.roll(g, SUBLANES - k, axis=0) for g in groups]
    return [jnp.where(row < SUBLANES - k, lo, hi) for lo, hi in zip(turned[:-1], turned[1:])]


def _ffn_mid_fwd(p, cw, cb, name):
    s, f2 = p.shape
    ts = _tile(s, (1024, 512))
    nb, nt = f2 // (2 * CB), s // ts
    n_grp = SLAB // SUBLANES

    def body(pg_ref, pv_ref, hg_ref, hv_ref, cwg_ref, cwv_ref, cbg_ref, cbv_ref, a_ref, ga_ref, gb_ref):
        t = pl.program_id(1)
        cwv, bias = _cat(cwg_ref[...], cwv_ref[...]), _cat(cbg_ref[...], cbv_ref[...])
        w0, w1, w2 = cwv[0:1], cwv[1:2], cwv[2:3]

        def slab(before, cur, r0):
            pm2, pm1 = _rows_from([before] + cur, SUBLANES - 2), _rows_from([before] + cur, SUBLANES - 1)
            u = jnp.concatenate([bias + w0 * pm2[i] + w1 * pm1[i] + w2 * cur[i] for i in range(n_grp)], axis=0)
            g, v = u[:, :CB], u[:, CB:]
            gel, th = _gelu(g)
            rows = pl.ds(r0, SLAB)
            a_ref[rows, :] = (gel * v).astype(a_ref.dtype)
            ga_ref[rows, :] = gel.astype(ga_ref.dtype)
            gb_ref[rows, :] = (v * _gelu_grad(g, th)).astype(gb_ref.dtype)

        def pieces(rows):
            blk = _cat(pg_ref[rows, :], pv_ref[rows, :])
            return [blk[i * SUBLANES:(i + 1) * SUBLANES] for i in range(blk.shape[0] // SUBLANES)]

        slab(jnp.where(t > 0, _cat(hg_ref[...], hv_ref[...]), 0.0), pieces(pl.ds(0, SLAB)), 0)

        def loop(i, carry):
            r0 = pl.multiple_of(i * SLAB, SLAB)
            got = pieces(pl.ds(pl.multiple_of(r0 - SUBLANES, SUBLANES), SLAB + SUBLANES))
            slab(got[0], got[1:], r0)
            return carry

        lax.fori_loop(1, ts // SLAB, loop, 0, unroll=2)

    fwd = lambda t: t
    out = pl.BlockSpec((ts, CB), lambda j, t: (t, j))
    shape = jax.ShapeDtypeStruct((s, f2 // 2), MXU_DTYPE)
    return pl.pallas_call(
        body, grid=(nb, nt),
        in_specs=(_pair_specs((ts, CB), nb, lambda j, t: (t,)) + _pair_specs((SUBLANES, CB), nb, _halo_row(ts, fwd))
                  + _pair_specs((FFN_CONV, CB), nb, lambda j, t: (0,)) + _pair_specs((1, CB), nb, lambda j, t: (0,))),
        out_specs=[out, out, out], out_shape=[shape, shape, shape],
        compiler_params=_params("parallel", "arbitrary"), name=name,
    )(p, p, p, p, cw, cw, cb, cb)


def _ffn_mid_bwd(da, p, ga, gb, cw, name):
    s, f2 = p.shape
    ts = _tile(s, (1024, 512))
    nb, nt = f2 // (2 * CB), s // ts
    n_slab = ts // SLAB
    n_grp = SLAB // SUBLANES
    per_trip = 2

    def body(da_ref, ga_ref, gb_ref, pg_ref, pv_ref, cwg_ref, cwv_ref, dp_ref, dcw_ref, dcb_ref, next_du, acc):
        tt = pl.program_id(1)
        cwv = _cat(cwg_ref[...], cwv_ref[...])
        w0, w1, w2 = cwv[0:1], cwv[1:2], cwv[2:3]

        @pl.when(tt == 0)
        def _():
            next_du[...] = jnp.zeros_like(next_du)
            acc[...] = jnp.zeros_like(acc)

        def slab(r0, after, sums):
            rows = pl.ds(r0, SLAB)
            dav = da_ref[rows, :]
            du = _cat(dav * gb_ref[rows, :].astype(F32), dav * ga_ref[rows, :].astype(F32))
            p0 = _cat(pg_ref[rows, :], pv_ref[rows, :])
            cur = [du[i * SUBLANES:(i + 1) * SUBLANES] for i in range(n_grp)]
            du1, du2 = _rows_from(cur + [after], 1), _rows_from(cur + [after], 2)
            dpv = jnp.concatenate([w2 * cur[i] + w1 * du1[i] + w0 * du2[i] for i in range(n_grp)], axis=0).astype(dp_ref.dtype)
            dp_ref[0, rows, :] = dpv[:, :CB]
            dp_ref[1, rows, :] = dpv[:, CB:]
            for i in range(n_grp):
                pi = p0[i * SUBLANES:(i + 1) * SUBLANES]
                parts = (cur[i], du2[i] * pi, du1[i] * pi, cur[i] * pi)
                sums = parts if sums is None else tuple(x + y for x, y in zip(sums, parts))
            return cur[0], sums

        def loop(k, after):
            sums = None
            for j in range(per_trip):
                r0 = pl.multiple_of((n_slab - 1 - (k * per_trip + j)) * SLAB, SLAB)
                after, sums = slab(r0, after, sums)
            for q, part in enumerate(sums):
                acc[q] += part
            return after

        next_du[...] = lax.fori_loop(0, n_slab // per_trip, loop, next_du[...])

        @pl.when(tt == nt - 1)
        def _():
            for half in range(2):
                cols = slice(half * CB, (half + 1) * CB)
                dcb_ref[half] = jnp.sum(acc[0][:, cols], axis=0, keepdims=True)
                for k in range(FFN_CONV):
                    dcw_ref[half, k:k + 1, :] = jnp.sum(acc[1 + k][:, cols], axis=0, keepdims=True)

    rev = lambda t: nt - 1 - t
    tile = pl.BlockSpec((ts, CB), lambda j, t: (rev(t), j))
    return pl.pallas_call(
        body, grid=(nb, nt),
        in_specs=([tile, tile, tile] + _pair_specs((ts, CB), nb, lambda j, t: (rev(t),))
                  + _pair_specs((FFN_CONV, CB), nb, lambda j, t: (0,))),
        out_specs=[pl.BlockSpec((2, ts, CB), lambda j, t: (0, rev(t), j)),
                   pl.BlockSpec((2, FFN_CONV, CB), lambda j, t: (0, 0, j)),
                   pl.BlockSpec((2, 1, CB), lambda j, t: (0, 0, j))],
        out_shape=[jax.ShapeDtypeStruct((2, s, f2 // 2), MXU_DTYPE), jax.ShapeDtypeStruct((2, FFN_CONV, f2 // 2), F32),
                   jax.ShapeDtypeStruct((2, 1, f2 // 2), F32)],
        scratch_shapes=[pltpu.VMEM((SUBLANES, 2 * CB), F32), pltpu.VMEM((1 + FFN_CONV, SUBLANES, 2 * CB), F32)],
        compiler_params=_params("parallel", "arbitrary"), name=name,
    )(da, ga, gb, p, p, cw, cw)


def _rg_gates(xc, wa_ref, ba_ref, wx_ref, bx_ref, lam_ref):
    r = _sigmoid(_dot_nn(xc, wa_ref[0]) + ba_ref[...])
    ig = _sigmoid(_dot_nn(xc, wx_ref[0]) + bx_ref[...])
    sp = _softplus(-lam_ref[...])
    log_a = (-RG_C) * r * sp
    a = jnp.exp(log_a)
    mult = jnp.sqrt(_one_minus_sq_exp(log_a, a))
    return r, ig, sp, a, mult


def _rg_conv(scr, cw_ref, cb_ref, ts):
    views = [scr[5 + k:5 + k + ts, :] for k in range(RG_CONV)]
    xc = cb_ref[...]
    for k in range(RG_CONV):
        xc = xc + cw_ref[k:k + 1, :] * views[k]
    return xc, views


def _rg_param_specs():
    vec = pl.BlockSpec((1, CB), lambda g, t: (0, g))
    mat = pl.BlockSpec((1, CB, CB), lambda g, t: (g, 0, 0))
    return [pl.BlockSpec((RG_CONV, CB), lambda g, t: (0, g)), vec, mat, vec, mat, vec, vec]


def _scan_rows(a_scr, x_scr, out_ref, carry, ts, reverse):
    n = ts // SUBLANES
    row = lax.broadcasted_iota(jnp.int32, (SUBLANES, a_scr.shape[1]), 0)
    last = SUBLANES - 1

    def rows_of(k):
        return pl.ds(pl.multiple_of(k * SUBLANES, SUBLANES), SUBLANES)

    def local(k, _):
        rows = rows_of(k)
        a, x = a_scr[rows, :], x_scr[rows, :]
        if reverse:
            a = jnp.where(row == last, 1.0, pltpu.roll(a, last, axis=0))
            for sh in (1, 2, 4):
                keep = row < SUBLANES - sh
                x = x + a * jnp.where(keep, pltpu.roll(x, SUBLANES - sh, axis=0), 0.0)
                a = a * jnp.where(keep, pltpu.roll(a, SUBLANES - sh, axis=0), 1.0)
        else:
            for sh in (1, 2, 4):
                keep = row >= sh
                x = a * jnp.where(keep, pltpu.roll(x, sh, axis=0), 0.0) + x
                a = a * jnp.where(keep, pltpu.roll(a, sh, axis=0), 1.0)
        out_ref[rows, :] = x
        x_scr[rows, :] = a
        return 0

    lax.fori_loop(0, n, local, 0, unroll=4)

    def chain(k, c):
        rows = rows_of(n - 1 - k if reverse else k)
        v = out_ref[rows, :] + x_scr[rows, :] * c
        out_ref[rows, :] = v
        return a_scr[rows, :][0:1] * v[0:1] if reverse else v[last:last + 1]

    return lax.fori_loop(0, n, chain, carry, unroll=4)


def _rg_mid_fwd(pj, cw, cb, wa, ba, wx, bx, lam, name):
    s = pj.shape[0]
    nb = pj.shape[1] // (2 * CB)
    ts = _tile(s, (512,))
    nt = s // ts

    def body(gate_ref, x_ref, halo_ref, cw_ref, cb_ref, wa_ref, ba_ref, wx_ref, bx_ref, lam_ref, y_ref, hs_ref,
             scr, a_scr, u_scr, h_scr):
        t = pl.program_id(1)

        @pl.when(t == 0)
        def _():
            h_scr[...] = jnp.zeros_like(h_scr)

        scr[0:SUBLANES, :] = jnp.where(t > 0, halo_ref[...], 0.0)
        scr[SUBLANES:, :] = x_ref[...]
        xc, _ = _rg_conv(scr, cw_ref, cb_ref, ts)
        _, ig, _, a, mult = _rg_gates(xc, wa_ref, ba_ref, wx_ref, bx_ref, lam_ref)
        a_scr[...] = a
        u_scr[...] = mult * (ig * xc)
        h_scr[0:1, :] = _scan_rows(a_scr, u_scr, hs_ref, h_scr[0:1, :], ts, False)
        y_ref[...] = (_gelu(gate_ref[...])[0] * hs_ref[...]).astype(y_ref.dtype)

    blk = pl.BlockSpec((ts, CB), lambda g, t: (t, g))
    return pl.pallas_call(
        body, grid=(nb, nt),
        in_specs=_pair_specs((ts, CB), nb, lambda g, t: (t,))
        + [pl.BlockSpec((SUBLANES, CB), lambda g, t: _halo_row(ts, lambda u: u)(g, t) + (g + nb,))] + _rg_param_specs(),
        out_specs=[blk, blk],
        out_shape=[jax.ShapeDtypeStruct((s, nb * CB), MXU_DTYPE), jax.ShapeDtypeStruct((s, nb * CB), F32)],
        scratch_shapes=[pltpu.VMEM((ts + SUBLANES, CB), F32), pltpu.VMEM((ts, CB), F32), pltpu.VMEM((ts, CB), F32),
                        pltpu.VMEM((SUBLANES, CB), F32)],
        compiler_params=_params("parallel", "arbitrary"), name=name,
    )(pj, pj, pj, cw, cb, wa, ba, wx, bx, lam)


def _rg_mid_bwd(dy, pj, hs, cw, cb, wa, ba, wx, bx, lam, name):
    s = pj.shape[0]
    nb = pj.shape[1] // (2 * CB)
    ts = _tile(s, (512,))
    nt = s // ts

    def body(dy_ref, gate_ref, x_ref, halo_ref, hs_ref, hsh_ref, cw_ref, cb_ref, wa_ref, ba_ref, wx_ref, bx_ref, lam_ref,
             dpj_ref, dcw_ref, dcb_ref, dwa_ref, dba_ref, dwx_ref, dbx_ref, dlam_ref,
             scr, hscr, a_scr, d_scr, g_scr, dxscr, c_scr):
        tt = pl.program_id(1)
        t = nt - 1 - tt

        @pl.when(tt == 0)
        def _():
            c_scr[...] = jnp.zeros_like(c_scr)
            dxscr[ts:, :] = jnp.zeros((SUBLANES, CB), F32)
            for ref in (dcw_ref, dcb_ref, dwa_ref, dba_ref, dwx_ref, dbx_ref, dlam_ref):
                ref[...] = jnp.zeros_like(ref)

        scr[0:SUBLANES, :] = jnp.where(t > 0, halo_ref[...], 0.0)
        scr[SUBLANES:, :] = x_ref[...]
        hscr[0:SUBLANES, :] = jnp.where(t > 0, hsh_ref[...], 0.0)
        hscr[SUBLANES:, :] = hs_ref[...]
        xc, views = _rg_conv(scr, cw_ref, cb_ref, ts)
        r, ig, sp, a, mult = _rg_gates(xc, wa_ref, ba_ref, wx_ref, bx_ref, lam_ref)
        gate = gate_ref[...]
        gel, th = _gelu(gate)
        dyv = dy_ref[...]
        dpj_ref[0] = (dyv * hs_ref[...] * _gelu_grad(gate, th)).astype(dpj_ref.dtype)
        a_scr[...] = a
        d_scr[...] = dyv * gel
        c_scr[0:1, :] = _scan_rows(a_scr, d_scr, g_scr, c_scr[0:1, :], ts, True)
        du = g_scr[...]
        da = du * hscr[7:7 + ts, :]
        dmult = du * (ig * xc)
        dig = du * (mult * xc)
        dxc = du * (mult * ig)
        dlog_a = da * a - dmult * (a * a / mult)
        dlam_ref[...] += jnp.sum(dlog_a * r, axis=0, keepdims=True) * (RG_C * _sigmoid(-lam_ref[...]))
        dpr = dlog_a * ((-RG_C) * sp) * (r * (1.0 - r))
        dpi = dig * (ig * (1.0 - ig))
        dba_ref[...] += jnp.sum(dpr, axis=0, keepdims=True)
        dbx_ref[...] += jnp.sum(dpi, axis=0, keepdims=True)
        dwa_ref[0] += _dot_tn(xc, dpr)
        dwx_ref[0] += _dot_tn(xc, dpi)
        dxc = dxc + _dot_nt(dpr, wa_ref[0]) + _dot_nt(dpi, wx_ref[0])
        dcb_ref[...] += jnp.sum(dxc, axis=0, keepdims=True)
        for k in range(RG_CONV):
            dcw_ref[k:k + 1, :] += jnp.sum(dxc * views[k], axis=0, keepdims=True)
        dxscr[0:ts, :] = dxc
        dxp = cw_ref[3:4, :] * dxc
        for k in range(RG_CONV - 1):
            dxp = dxp + cw_ref[k:k + 1, :] * dxscr[3 - k:3 - k + ts, :]
        dpj_ref[1] = dxp.astype(dpj_ref.dtype)
        dxscr[ts:, :] = dxscr[0:SUBLANES, :]

    rev = lambda g, t: (nt - 1 - t, g)
    rev_halo = lambda g, t: (jnp.maximum((nt - 1 - t) * (ts // SUBLANES) - 1, 0), g)
    vec = pl.BlockSpec((1, CB), lambda g, t: (0, g))
    mat = pl.BlockSpec((1, CB, CB), lambda g, t: (g, 0, 0))
    d = nb * CB
    vshape = jax.ShapeDtypeStruct((1, d), F32)
    mshape = jax.ShapeDtypeStruct((nb, CB, CB), F32)
    return pl.pallas_call(
        body, grid=(nb, nt),
        in_specs=[pl.BlockSpec((ts, CB), rev)] + _pair_specs((ts, CB), nb, lambda g, t: (nt - 1 - t,))
        + [pl.BlockSpec((SUBLANES, CB), lambda g, t: (rev_halo(g, t)[0], g + nb)),
           pl.BlockSpec((ts, CB), rev), pl.BlockSpec((SUBLANES, CB), rev_halo)] + _rg_param_specs(),
        out_specs=[pl.BlockSpec((2, ts, CB), lambda g, t: (0, nt - 1 - t, g)), pl.BlockSpec((RG_CONV, CB), lambda g, t: (0, g)),
                   vec, mat, vec, mat, vec, vec],
        out_shape=[jax.ShapeDtypeStruct((2, s, d), MXU_DTYPE), jax.ShapeDtypeStruct((RG_CONV, d), F32), vshape, mshape, vshape,
                   mshape, vshape, vshape],
        scratch_shapes=[pltpu.VMEM((ts + SUBLANES, CB), F32), pltpu.VMEM((ts + SUBLANES, CB), F32), pltpu.VMEM((ts, CB), F32),
                        pltpu.VMEM((ts, CB), F32), pltpu.VMEM((ts, CB), F32), pltpu.VMEM((ts + SUBLANES, CB), F32),
                        pltpu.VMEM((SUBLANES, CB), F32)],
        compiler_params=_params("parallel", "arbitrary"), name=name,
    )(dy, pj, pj, pj, hs, hs, cw, cb, wa, ba, wx, bx, lam)


GLA_DK = 128
GLA_DV = 256
GLA_O_K = GLA_HEADS * GLA_DK
GLA_O_V = 2 * GLA_HEADS * GLA_DK
GLA_O_R = GLA_O_V + GLA_HEADS * GLA_DV
GLA_O_Z = GLA_O_R + GLA_HEADS * GLA_DV
GLA_IN = GLA_O_Z + GLA_RANK
GLA_TS = 256


def _dk(h, base=0):
    return slice(base + h * GLA_DK, base + (h + 1) * GLA_DK)


def _dv(h, base=0):
    return slice(base + h * GLA_DV, base + (h + 1) * GLA_DV)


def _split3(x):
    hi = x.astype(BF16)
    r1 = x - hi.astype(F32)
    mid = r1.astype(BF16)
    lo = (r1 - mid.astype(F32)).astype(BF16)
    return hi, mid, lo


def _chunk_cumsum(x, reverse):
    n = x.shape[0]
    i = lax.broadcasted_iota(jnp.int32, (n, n), 0)
    j = lax.broadcasted_iota(jnp.int32, (n, n), 1)
    same = (i // GLA_CHUNK) == (j // GLA_CHUNK)
    tri = jnp.where(same & ((j >= i) if reverse else (j <= i)), 1.0, 0.0).astype(BF16)
    out = jnp.zeros(x.shape, F32)
    for piece in _split3(x):
        out = out + lax.dot_general(tri, piece, (((1,), (0,)), ((), ())), preferred_element_type=F32)
    return out


def _gla_head(pj_ref, h):
    return (pj_ref[:, _dk(h)] * (GLA_DK ** -0.5), pj_ref[:, _dk(h, GLA_O_K)], pj_ref[:, _dv(h, GLA_O_V)],
            pj_ref[:, _dv(h, GLA_O_R)])


def _gla_decays(gc):
    gref = gc[GLA_CHUNK // 2:GLA_CHUNK // 2 + 1, :]
    glast = gc[GLA_CHUNK - 1:GLA_CHUNK, :]
    return jnp.exp(gc), jnp.exp(gc - gref), jnp.exp(gref - gc), jnp.exp(glast - gc), jnp.exp(glast)


def _causal_mask():
    i = lax.broadcasted_iota(jnp.int32, (GLA_CHUNK, GLA_CHUNK), 0)
    j = lax.broadcasted_iota(jnp.int32, (GLA_CHUNK, GLA_CHUNK), 1)
    return j <= i


def _log_sigmoid(x):
    return jnp.minimum(x, 0.0) - _log1p_pos(jnp.exp(-jnp.abs(x)))


def _gla_mid_fwd(pj, wal, bal, ng, name):
    s, nh = pj.shape[0], GLA_HEADS
    ts = _tile(s, (GLA_TS,))
    nt, nc = s // ts, ts // GLA_CHUNK

    def body(pj_ref, wal_ref, bal_ref, ng_ref, act_ref, o_ref, st_ref, s_scr):
        t = pl.program_id(0)

        @pl.when(t == 0)
        def _():
            s_scr[...] = jnp.zeros_like(s_scr)

        heads = []
        z = pj_ref[:, GLA_O_Z:]
        for h in range(nh):
            q, k, v, r = _gla_head(pj_ref, h)
            g = _log_sigmoid(_dot_nn(z, wal_ref[:, _dk(h)]) + bal_ref[:, _dk(h)]) * (1.0 / GLA_TAU)
            heads.append((q, k, v, r, _chunk_cumsum(g, False)))
        mask = _causal_mask()
        for c in range(nc):
            sl = slice(c * GLA_CHUNK, (c + 1) * GLA_CHUNK)
            for h, (q, k, v, r, gcum) in enumerate(heads):
                eg, eq, ek, ekd, egl = _gla_decays(gcum[sl])
                st = s_scr[h]
                st_ref[c, h] = st
                attn = jnp.where(mask, _dot_nt(q[sl] * eq, k[sl] * ek), 0.0)
                o_ref[sl, h * GLA_DV:(h + 1) * GLA_DV] = _dot_nt(q[sl] * eg, st) + _dot_nn(attn, v[sl])
                s_scr[h] = st * egl + _dot_tn(v[sl], k[sl] * ekd)
        for h, (q, k, v, r, gcum) in enumerate(heads):
            cols = slice(h * GLA_DV, (h + 1) * GLA_DV)
            o = o_ref[:, cols]
            on = o * lax.rsqrt(jnp.mean(o * o, axis=-1, keepdims=True) + EPS)
            act_ref[:, cols] = ((on * ng_ref[...]) * (r * _sigmoid(r))).astype(act_ref.dtype)

    blk = pl.BlockSpec((ts, nh * GLA_DV), lambda t: (t, 0))
    whole = lambda shape: pl.BlockSpec(shape, lambda t: (0,) * len(shape))
    return pl.pallas_call(
        body, grid=(nt,),
        in_specs=[pl.BlockSpec((ts, GLA_IN), lambda t: (t, 0)), whole((GLA_RANK, nh * GLA_DK)), whole((1, nh * GLA_DK)),
                  whole((1, GLA_DV))],
        out_specs=[blk, blk, pl.BlockSpec((nc, nh, GLA_DV, GLA_DK), lambda t: (t, 0, 0, 0))],
        out_shape=[jax.ShapeDtypeStruct((s, nh * GLA_DV), MXU_DTYPE), jax.ShapeDtypeStruct((s, nh * GLA_DV), F32),
                   jax.ShapeDtypeStruct((s // GLA_CHUNK, nh, GLA_DV, GLA_DK), F32)],
        scratch_shapes=[pltpu.VMEM((nh, GLA_DV, GLA_DK), F32)],
        compiler_params=_params("arbitrary"), name=name,
    )(pj, wal, bal, ng)


def _gla_mid_bwd(dact, pj, o, st, wal, bal, ng, name):
    s, nh = pj.shape[0], GLA_HEADS
    ts = _tile(s, (GLA_TS,))
    nt, nc = s // ts, ts // GLA_CHUNK

    def body(dact_ref, pj_ref, o_ref, st_ref, wal_ref, bal_ref, ng_ref, dpj_ref, dwal_ref, dbal_ref, dng_ref,
             ds_scr, dg_scr):
        tt = pl.program_id(0)

        @pl.when(tt == 0)
        def _():
            ds_scr[...] = jnp.zeros_like(ds_scr)
            dwal_ref[...] = jnp.zeros_like(dwal_ref)
            dbal_ref[...] = jnp.zeros_like(dbal_ref)
            dng_ref[...] = jnp.zeros_like(dng_ref)

        heads = []
        z = pj_ref[:, GLA_O_Z:]
        for h in range(nh):
            q, k, v, r = _gla_head(pj_ref, h)
            logit = _dot_nn(z, wal_ref[:, _dk(h)]) + bal_ref[:, _dk(h)]
            gcum = _chunk_cumsum(_log_sigmoid(logit) * (1.0 / GLA_TAU), False)
            ov = o_ref[:, h * GLA_DV:(h + 1) * GLA_DV]
            ro = lax.rsqrt(jnp.mean(ov * ov, axis=-1, keepdims=True) + EPS)
            on = ov * ro
            sg = _sigmoid(r)
            sil = r * sg
            dav = dact_ref[:, h * GLA_DV:(h + 1) * GLA_DV]
            dpj_ref[:, _dv(h, GLA_O_R)] = (dav * (on * ng_ref[...]) * (sg + sil * (1.0 - sg))).astype(dpj_ref.dtype)
            t1 = dav * sil
            dng_ref[...] += jnp.sum(t1 * on, axis=0, keepdims=True)
            dn = t1 * ng_ref[...]
            do = ro * (dn - on * jnp.mean(dn * on, axis=-1, keepdims=True))
            heads.append((q, k, v, logit, gcum, do))
        mask = _causal_mask()
        scale = GLA_DK ** -0.5
        last_row = lax.broadcasted_iota(jnp.int32, (GLA_CHUNK, GLA_DK), 0) == GLA_CHUNK - 1
        for c in reversed(range(nc)):
            sl = slice(c * GLA_CHUNK, (c + 1) * GLA_CHUNK)
            for h, (q, k, v, logit, gcum, do) in enumerate(heads):
                eg, eq, ek, ekd, egl = _gla_decays(gcum[sl])
                qc, kc, vc, doc = q[sl], k[sl], v[sl], do[sl]
                qg, qt, kt, kd = qc * eg, qc * eq, kc * ek, kc * ekd
                sp = st_ref[c, h]
                ds = ds_scr[h]
                attn = jnp.where(mask, _dot_nt(qt, kt), 0.0)
                dattn = jnp.where(mask, _dot_nt(doc, vc), 0.0)
                dqg = _dot_nn(doc, sp)
                dqt = _dot_nn(dattn, kt)
                dkt = _dot_tn(dattn, qt)
                dkd = _dot_nn(vc, ds)
                dpj_ref[sl, _dv(h, GLA_O_V)] = (_dot_tn(attn, doc) + _dot_nt(kd, ds)).astype(dpj_ref.dtype)
                dpj_ref[sl, _dk(h)] = (scale * (dqg * eg + dqt * eq)).astype(dpj_ref.dtype)
                dpj_ref[sl, _dk(h, GLA_O_K)] = (dkt * ek + dkd * ekd).astype(dpj_ref.dtype)
                kdd = dkd * kd
                dgl = jnp.sum(kdd, axis=0, keepdims=True) + jnp.sum(ds * sp, axis=0, keepdims=True) * egl
                dg_scr[h, sl, :] = dqg * qg + dqt * qt - dkt * kt - kdd + jnp.where(last_row, dgl, 0.0)
                ds_scr[h] = ds * egl + _dot_tn(doc, qg)
        dz = jnp.zeros((ts, GLA_RANK), F32)
        for h, (q, k, v, logit, gcum, do) in enumerate(heads):
            dlogit = _chunk_cumsum(dg_scr[h], True) * (1.0 / GLA_TAU) * _sigmoid(-logit)
            dz = dz + _dot_nt(dlogit, wal_ref[:, _dk(h)])
            dwal_ref[:, _dk(h)] += _dot_tn(z, dlogit)
            dbal_ref[:, _dk(h)] += jnp.sum(dlogit, axis=0, keepdims=True)
        dpj_ref[:, GLA_O_Z:] = dz.astype(dpj_ref.dtype)

    rev = lambda t: (nt - 1 - t, 0)
    whole = lambda shape: pl.BlockSpec(shape, lambda t: (0,) * len(shape))
    wide = pl.BlockSpec((ts, nh * GLA_DV), rev)
    return pl.pallas_call(
        body, grid=(nt,),
        in_specs=[wide, pl.BlockSpec((ts, GLA_IN), rev), wide,
                  pl.BlockSpec((nc, nh, GLA_DV, GLA_DK), lambda t: (nt - 1 - t, 0, 0, 0)),
                  whole((GLA_RANK, nh * GLA_DK)), whole((1, nh * GLA_DK)), whole((1, GLA_DV))],
        out_specs=[pl.BlockSpec((ts, GLA_IN), rev), whole((GLA_RANK, nh * GLA_DK)), whole((1, nh * GLA_DK)), whole((1, GLA_DV))],
        out_shape=[jax.ShapeDtypeStruct((s, GLA_IN), MXU_DTYPE), jax.ShapeDtypeStruct((GLA_RANK, nh * GLA_DK), F32),
                   jax.ShapeDtypeStruct((1, nh * GLA_DK), F32), jax.ShapeDtypeStruct((1, GLA_DV), F32)],
        scratch_shapes=[pltpu.VMEM((nh, GLA_DV, GLA_DK), F32), pltpu.VMEM((nh, ts, GLA_DK), F32)],
        compiler_params=_params("arbitrary"), name=name,
    )(dact, pj, o, st, wal, bal, ng)


def _adamw(w, gs, m, v, name, after=None):
    layers, rows, cols = w.shape
    gs = list(gs) if isinstance(gs, (list, tuple)) else gs
    n_g = len(gs) if isinstance(gs, list) else 1
    if rows % SUBLANES == 0:
        tr, tc = _tile(rows, (256, 128, 64, 32, 16, 8)), cols
    else:
        tr, tc = rows, _tile(cols, (256, 128))
    c1 = 1.0 / (1.0 - ADAM_B1 ** ADAM_STEP)
    c2 = 1.0 / (1.0 - ADAM_B2 ** ADAM_STEP)

    def body(*refs):
        g_refs, (w_ref, m_ref, v_ref) = refs[:n_g], refs[n_g:n_g + 3]
        go_ref, d_ref, mo_ref, vo_ref = refs[-4:]
        gv = g_refs[0][...]
        for l in range(1, n_g):
            gv = jnp.where(pl.program_id(0) == l, g_refs[l][...], gv)
        m2 = ADAM_B1 * m_ref[...] + (1.0 - ADAM_B1) * gv
        v2 = ADAM_B2 * v_ref[...] + (1.0 - ADAM_B2) * (gv * gv)
        d_ref[...] = (-ADAM_LR) * ((m2 * c1) / (jnp.sqrt(v2 * c2) + ADAM_EPS) + ADAM_WD * w_ref[...])
        go_ref[...] = gv
        mo_ref[...] = m2
        vo_ref[...] = v2

    spec = pl.BlockSpec((None, tr, tc), lambda l, i, j: (l, i, j))
    g_specs = [pl.BlockSpec((tr, tc), lambda l, i, j: (i, j))] * n_g if isinstance(gs, list) else [spec]
    extra = [] if after is None else [(after, _ANY)]
    shape = jax.ShapeDtypeStruct((layers, rows, cols), F32)
    return pl.pallas_call(
        body, grid=(layers, rows // tr, cols // tc), in_specs=g_specs + [spec] * 3 + [sp for _, sp in extra],
        out_specs=[spec] * 4, out_shape=[shape] * 4, compiler_params=_params("parallel", "parallel", "parallel"), name=name,
    )(*(gs if isinstance(gs, list) else [gs]), w, m, v, *[a for a, _ in extra])


def _col_slots(w, name):
    r, cc = w.shape
    c = cc // N_CHIP
    tr = _tile(r, (256,))

    def body(w_ref, o_ref):
        for j in range(N_CHIP):
            o_ref[j] = w_ref[:, j * c:(j + 1) * c]

    return pl.pallas_call(
        body, grid=(r // tr,), in_specs=[pl.BlockSpec((tr, cc), lambda i: (i, 0))],
        out_specs=pl.BlockSpec((N_CHIP, tr, c), lambda i: (0, i, 0)), out_shape=jax.ShapeDtypeStruct((N_CHIP, r, c), w.dtype),
        compiler_params=_params("parallel"), name=name,
    )(w)


def _block_rows_to_slots(w):
    g, r4, cc = w.shape
    return jnp.swapaxes(w.reshape(g, N_CHIP, r4 // N_CHIP, cc), 0, 1).reshape(N_CHIP, g * (r4 // N_CHIP), cc)


def _slots_to_block_rows(w, g):
    n, gr, cc = w.shape
    return jnp.swapaxes(w.reshape(n, g, gr // g, cc), 0, 1).reshape(g, n * (gr // g), cc)


def _local_step(x, tgt, mod, w, fetch=None, done=None, later=None):
    depth = mod.shape[0]
    row = lambda v: v.reshape(1, -1)
    w = dict(w)
    w["ffn_w_up"], w["ffn_w_down"] = dict(enumerate(w["ffn_w_up"])), dict(enumerate(w["ffn_w_down"]))

    def arrive(stage, after):
        if fetch is not None:
            for k, v in fetch(stage, after).items():
                if isinstance(v, dict):
                    w[k].update(v)
                else:
                    w[k] = v

    saved = []
    for i in range(depth):
        if i == 1:
            arrive("gla", x)
        sh_m, sc_m, gt_m, sh_f, sc_f, gt_f = (mod[i, j:j + 1] for j in range(6))
        g0, g1, g2, g3 = (w["norm_g"][i, j:j + 1] for j in range(4))
        tag = f"_l{i}"
        if i == 0:
            h = _norm_mod_fwd(x, g0, sc_m, sh_m, "norm_mix" + tag)
        if i % 2 == 0:
            pj = _mm(h, w["rg_w_in"], w_slots=N_CHIP, name="rg_in" + tag)
            act, aux = _rg_mid_fwd(pj, w["rg_conv_w"], row(w["rg_conv_b"]), w["rg_wa"], row(w["rg_ba"]), w["rg_wx"],
                                   row(w["rg_bx"]), row(w["rg_lambda"]), "rg_mid" + tag)
            y = _mm(act, w["rg_w_out"], name="rg_out" + tag)
        else:
            pj = _mm(h, w["gla_w_in"], tb=True, tm_max=512, name="gla_in" + tag)
            act, *aux = _gla_mid_fwd(pj, w["gla_w_alpha"], row(w["gla_b_alpha"]), row(w["gla_norm_g"]), "gla_mid" + tag)
            y = _mm(act, w["gla_w_out"], name="gla_out" + tag)
        x1, h2 = _post_norm_fwd(x, y, g1, gt_m, g2, sc_f, sh_f, "post_mix" + tag)
        arrive(f"ffn{i}", x1)
        p = _mm(h2, w["ffn_w_up"][i], w_slots=N_CHIP, name="ffn_up" + tag)
        a, ga, gb = _ffn_mid_fwd(p, w["ffn_conv_w"][i], w["ffn_conv_b"][i:i + 1], "ffn_mid" + tag)
        y2 = _mm(a, w["ffn_w_down"][i], name="ffn_down" + tag)
        saved_h = h
        if i + 1 < depth:
            nxt = [mod[i + 1, j:j + 1] for j in range(2)] + [w["norm_g"][i + 1, 0:1]]
            x2, h = _post_norm_fwd(x1, y2, g3, gt_f, nxt[2], nxt[1], nxt[0], "post_ffn" + tag)
        else:
            x2 = None
            cols, dx = _post_loss(x1, y2, g3, gt_f, tgt, "post_ffn_loss")
        saved.append((x, saved_h, pj, act, aux, y, x1, h2, p, (a, ga, gb), y2))
        x = x2

    stacked = ("norm_g", "ffn_conv_w", "ffn_conv_b", "mod")
    gr = {k: [None] * depth for k in stacked + ("ffn_w_up", "ffn_w_down")}
    told = lambda stage: done(stage, gr) if done is not None else 0.0
    told_later = lambda stage, after: later(stage, after) if later is not None else 0.0
    for i in reversed(range(depth)):
        x0, h, pj, act, aux, y, x1, h2, p, (a, ga, gb), y2 = saved[i]
        sh_m, sc_m, gt_m, sh_f, sc_f, gt_f = (mod[i, j:j + 1] for j in range(6))
        g0, g1, g2, g3 = (w["norm_g"][i, j:j + 1] for j in range(4))
        tag = f"_l{i}"
        if i == depth - 1:
            dy2, d_g3, d_gt_f = _post_bwd(dx, y2, g3, gt_f, "post_ffn_b" + tag)
        else:
            dy2, d_g3, d_gt_f = ahead
        da = _mm(dy2, w["ffn_w_down"][i], tb=True, name="ffn_down_dx" + tag)
        gr["ffn_w_down"][i] = _mm(a, dy2, ta=True, name="ffn_down_dw" + tag)
        conv_w = w["ffn_conv_w"][i] + (told_later("l1", da) if i == 0 else 0.0)
        dp, dcw, dcb = _ffn_mid_bwd(da, p, ga, gb, conv_w, "ffn_mid_b" + tag)
        gr["ffn_conv_w"][i], gr["ffn_conv_b"][i] = _cat(dcw[0], dcw[1]), _cat(dcb[0], dcb[1])[0]
        dh2 = _mm(dp, w["ffn_w_up"][i], tb=True, a_parts=2, w_slots=N_CHIP, name="ffn_up_dx" + tag)
        gr["ffn_w_up"][i] = _mm(h2, dp, ta=True, b_parts=2, out_slots=N_CHIP, name="ffn_up_dw" + tag)
        if i == 0:
            gt_m = gt_m + told("ffn0")
        dx1, dy, d_g2, d_sc_f, d_sh_f, d_g1, d_gt_m = _norm_post_bwd(dh2, x1, g2, sc_f, dx, y, g1, gt_m, "norm_ffn_b" + tag)
        if i % 2 == 0:
            dact = _mm(dy, w["rg_w_out"], tb=True, name="rg_out_dx" + tag)
            gr["rg_w_out"] = _mm(act, dy, ta=True, name="rg_out_dw" + tag)
            lam = row(w["rg_lambda"]) + told_later("ffn0", gr["rg_w_out"])
            dpj, gr["rg_conv_w"], d_cb, gr["rg_wa"], d_ba, gr["rg_wx"], d_bx, d_lam = _rg_mid_bwd(
                dact, pj, aux, w["rg_conv_w"], row(w["rg_conv_b"]), w["rg_wa"], row(w["rg_ba"]), w["rg_wx"],
                row(w["rg_bx"]), lam, "rg_mid_b" + tag)
            gr["rg_conv_b"], gr["rg_ba"], gr["rg_bx"], gr["rg_lambda"] = d_cb[0], d_ba[0], d_bx[0], d_lam[0]
            dh = _mm(dpj, w["rg_w_in"], tb=True, a_parts=2, w_slots=N_CHIP, name="rg_in_dx" + tag)
            gr["rg_w_in"] = _mm(h, dpj, ta=True, b_parts=2, out_slots=N_CHIP, name="rg_in_dw" + tag)
            sc_m = sc_m + told("rg")
        else:
            dact = _mm(dy, w["gla_w_out"], tb=True, name="gla_out_dx" + tag)
            gr["gla_w_out"] = _mm(act, dy, ta=True, name="gla_out_dw" + tag)
            dpj, gr["gla_w_alpha"], d_bal, d_ng = _gla_mid_bwd(dact, pj, aux[0], aux[1], w["gla_w_alpha"], row(w["gla_b_alpha"]),
                                                               row(w["gla_norm_g"]), "gla_mid_b" + tag)
            gr["gla_b_alpha"], gr["gla_norm_g"] = d_bal[0], d_ng[0]
            dh = _mm(dpj, w["gla_w_in"], name="gla_in_dx" + tag)
            gr["gla_w_in"] = _mm(h, dpj, ta=True, tm_max=512, name="gla_in_dw" + tag)
            mod = mod.at[0].add(told("l1"))
        if i > 0:
            dx, dy_below, d_g0, d_sc_m, d_sh_m, d_g_below, d_gt_below = _norm_post_bwd(
                dh, x0, g0, sc_m, dx1, saved[i - 1][-1], w["norm_g"][i - 1, 3:4], mod[i - 1, 5:6], "norm_mix_b" + tag)
            ahead = (dy_below, d_g_below, d_gt_below)
        else:
            dx, d_g0, d_sc_m, d_sh_m = _norm_mod_bwd(dh, x0, g0, sc_m, dx1, "norm_mix_b" + tag)
        gr["norm_g"][i] = jnp.concatenate([d_g0, d_g1, d_g2, d_g3], axis=0)
        gr["mod"][i] = jnp.concatenate([d_sh_m, d_sc_m, d_gt_m, d_sh_f, d_sc_f, d_gt_f], axis=0)
    for k in stacked:
        gr[k] = jnp.stack(gr[k])
    return cols, dx, gr


ADA_ROWS = 16


def _ada_fwd(c16, ada_w, ada_b, name):
    depth, d, n = ada_w.shape
    tn = _tile(n, (512, 256, 128))

    def body(c_ref, w_ref, b_ref, o_ref):
        cv = c_ref[...]
        o_ref[0] = _dot_nn(cv * _sigmoid(cv), w_ref[0]) + b_ref[0]

    return pl.pallas_call(
        body, grid=(depth, n // tn),
        in_specs=[pl.BlockSpec((ADA_ROWS, d), lambda l, j: (0, 0)), pl.BlockSpec((1, d, tn), lambda l, j: (l, 0, j)),
                  pl.BlockSpec((1, 1, tn), lambda l, j: (l, 0, j))],
        out_specs=pl.BlockSpec((1, ADA_ROWS, tn), lambda l, j: (l, 0, j)),
        out_shape=jax.ShapeDtypeStruct((depth, ADA_ROWS, n), F32),
        compiler_params=_params("parallel", "parallel"), name=name,
    )(c16, ada_w, ada_b)


def _ada_bwd(c16, dmod16, name):
    depth, _, n = dmod16.shape
    d = c16.shape[1]
    tn = _tile(n, (512, 256, 128))

    def body(c_ref, dm_ref, o_ref):
        cv = c_ref[...]
        o_ref[0] = _dot_tn(cv * _sigmoid(cv), dm_ref[0])

    return pl.pallas_call(
        body, grid=(depth, n // tn),
        in_specs=[pl.BlockSpec((ADA_ROWS, d), lambda l, j: (0, 0)), pl.BlockSpec((1, ADA_ROWS, tn), lambda l, j: (l, 0, j))],
        out_specs=pl.BlockSpec((1, d, tn), lambda l, j: (l, 0, j)),
        out_shape=jax.ShapeDtypeStruct((depth, d, n), F32),
        compiler_params=_params("parallel", "parallel"), name=name,
    )(c16, dmod16)


PACK_COLS = 1024
_ANY = pl.BlockSpec(memory_space=pl.ANY)
_VMEM = pl.BlockSpec(memory_space=pltpu.VMEM)


def _place():
    return lax.axis_index("x"), lax.axis_index("y"), lax.axis_index("c")


def _other_chips(x, y):
    return [(1 - x, y), (x, 1 - y), (1 - x, 1 - y)]


def _rcopy(src, dst, send_sems, recv_sems, k, peer):
    return pltpu.make_async_remote_copy(src_ref=src, dst_ref=dst, send_sem=send_sems.at[k], recv_sem=recv_sems.at[k],
                                        device_id=peer, device_id_type=MESH)


def _all_gather_8(v, name):
    r, cc = v.shape

    def body(v_ref, out_ref, send_sems, recv_sems, local_sem):
        x, y, c = _place()
        me = 4 * x + 2 * y + c
        mine = pltpu.make_async_copy(v_ref, out_ref.at[me], local_sem)
        mine.start()
        peers = []
        for k in range(1, N_DEV):
            px = 1 - x if k & 4 else x
            py = 1 - y if k & 2 else y
            pc = 1 - c if k & 1 else c
            peers.append((px, py, pc))
        sends = [_rcopy(v_ref, out_ref.at[me], send_sems, recv_sems, k, p) for k, p in enumerate(peers)]
        for cp in sends:
            cp.start()
        for k, (px, py, pc) in enumerate(peers):
            _rcopy(v_ref, out_ref.at[4 * px + 2 * py + pc], send_sems, recv_sems, k, (px, py, pc)).wait_recv()
        for cp in sends:
            cp.wait_send()
        mine.wait()

    return pl.pallas_call(
        body, in_specs=[_VMEM], out_specs=_VMEM, out_shape=jax.ShapeDtypeStruct((N_DEV, r, cc), v.dtype),
        scratch_shapes=[pltpu.SemaphoreType.DMA((N_DEV - 1,)), pltpu.SemaphoreType.DMA((N_DEV - 1,)), pltpu.SemaphoreType.DMA],
        compiler_params=pltpu.CompilerParams(vmem_limit_bytes=VMEM_LIMIT), name=name,
    )(v)


def _gather_chips(shards, name):
    n = len(shards)
    per = 2 * (N_CHIP - 1)

    def body(*refs):
        ins, outs, (send_sems, recv_sems) = refs[:n], refs[n:2 * n], refs[2 * n:]
        x, y, c = _place()
        chip = 2 * x + y
        chips = _other_chips(x, y)
        rows = [(pl.ds(c * (r.shape[0] // 2), r.shape[0] // 2), pl.ds((1 - c) * (r.shape[0] // 2), r.shape[0] // 2)) for r in ins]
        first = [_rcopy(ins[i].at[rows[i][0]], outs[i].at[chip, rows[i][0]], send_sems, recv_sems, per * i + j, (px, py, c))
                 for i in range(n) for j, (px, py) in enumerate(chips)]
        for cp in first:
            cp.start()
        passed = []
        for i in range(n):
            for j, (px, py) in enumerate(chips):
                landed = outs[i].at[2 * px + py, rows[i][0]]
                _rcopy(ins[i].at[rows[i][0]], landed, send_sems, recv_sems, per * i + j, (px, py, c)).wait_recv()
                fw = _rcopy(landed, landed, send_sems, recv_sems, per * i + N_CHIP - 1 + j, (x, y, 1 - c))
                fw.start()
                passed.append(fw)
        for i in range(n):
            for j, (px, py) in enumerate(chips):
                landed = outs[i].at[2 * px + py, rows[i][1]]
                _rcopy(landed, landed, send_sems, recv_sems, per * i + N_CHIP - 1 + j, (x, y, 1 - c)).wait_recv()
        for cp in first + passed:
            cp.wait_send()

    return pl.pallas_call(
        body, in_specs=[_ANY] * n, out_specs=[_ANY] * n,
        out_shape=[jax.ShapeDtypeStruct((N_CHIP,) + sh.shape, sh.dtype) for sh in shards],
        scratch_shapes=[pltpu.SemaphoreType.DMA((per * n,)), pltpu.SemaphoreType.DMA((per * n,))], name=name,
    )(*shards)


def _pair_exchange(gs, name):
    n = len(gs)

    def body(*refs):
        ins, outs, (send_sems, recv_sems) = refs[:n], refs[n:2 * n], refs[2 * n:]
        x, y, c = _place()
        copies = []
        for i in range(n):
            half = ins[i].shape[1] // 2
            copies.append(_rcopy(ins[i].at[:, pl.ds((1 - c) * half, half)], outs[i], send_sems, recv_sems, i, (x, y, 1 - c)))
        for cp in copies:
            cp.start()
        for cp in copies:
            cp.wait()

    return pl.pallas_call(
        body, in_specs=[_ANY] * n, out_specs=[_ANY] * n,
        out_shape=[jax.ShapeDtypeStruct((g.shape[0], g.shape[1] // 2, g.shape[2]), g.dtype) for g in gs],
        scratch_shapes=[pltpu.SemaphoreType.DMA((n,)), pltpu.SemaphoreType.DMA((n,))], name=name,
    )(*gs)


_ROW_TILES = (640, 512, 352, 256, 128, 64, 32, 16)


def _pair_sum(g, other, c_idx, name):
    n, half, cc = other.shape
    tr = _tile(half, _ROW_TILES)

    def body(c_ref, g_ref, o_ref, out_ref):
        out_ref[...] = (g_ref[...] + o_ref[...]).astype(out_ref.dtype)

    return pl.pallas_call(
        body,
        grid_spec=pltpu.PrefetchScalarGridSpec(
            num_scalar_prefetch=1, grid=(n, half // tr),
            in_specs=[pl.BlockSpec((None, None, tr, cc), lambda k, i, c_ref: (k, c_ref[0], i, 0)),
                      pl.BlockSpec((None, tr, cc), lambda k, i, c_ref: (k, i, 0))],
            out_specs=pl.BlockSpec((None, tr, cc), lambda k, i, c_ref: (k, i, 0))),
        out_shape=jax.ShapeDtypeStruct((n, half, cc), BF16),
        compiler_params=_params("parallel", "parallel"), name=name,
    )(c_idx, g.reshape(n, 2, half, cc), other)


def _chip_exchange(ps, name):
    n = len(ps)
    per = N_CHIP - 1

    def body(*refs):
        ins, outs, (send_sems, recv_sems) = refs[:n], refs[n:2 * n], refs[2 * n:]
        x, y, c = _place()
        chip = 2 * x + y
        chips = _other_chips(x, y)
        sends = [_rcopy(ins[i].at[2 * px + py], outs[i].at[chip], send_sems, recv_sems, per * i + j, (px, py, c))
                 for i in range(n) for j, (px, py) in enumerate(chips)]
        for cp in sends:
            cp.start()
        for i in range(n):
            for j, (px, py) in enumerate(chips):
                _rcopy(ins[i].at[chip], outs[i].at[2 * px + py], send_sems, recv_sems, per * i + j, (px, py, c)).wait_recv()
        for cp in sends:
            cp.wait_send()

    return pl.pallas_call(
        body, in_specs=[_ANY] * n, out_specs=[_ANY] * n, out_shape=[jax.ShapeDtypeStruct(p.shape, p.dtype) for p in ps],
        scratch_shapes=[pltpu.SemaphoreType.DMA((per * n,)), pltpu.SemaphoreType.DMA((per * n,))], name=name,
    )(*ps)


_HBM = pl.BlockSpec(memory_space=pltpu.HBM)
_SEM = pl.BlockSpec(memory_space=pltpu.SEMAPHORE)
_DATAFLOW = pltpu.SideEffectType.DATAFLOW_SIDE_EFFECTING


def _split_copies(srcs, lands, send_sems, recv_sems, mode, arriving):
    x, y, c = _place()
    chip = 2 * x + y
    out = []
    for i, (src, land) in enumerate(zip(srcs, lands)):
        if mode == "all":
            for k in range(1, N_DEV):
                px, py, pc = (1 - x if k & 4 else x), (1 - y if k & 2 else y), (1 - c if k & 1 else c)
                slot = 4 * px + 2 * py + pc if arriving else 2 * chip + c
                out.append(_rcopy(src, land.at[slot], send_sems, recv_sems, (N_DEV - 1) * i + k - 1, (px, py, pc)))
            continue
        if mode == "pair":
            half = src.shape[1] // 2
            out.append(_rcopy(src.at[:, pl.ds((1 - c) * half, half)], land, send_sems, recv_sems, i, (x, y, 1 - c)))
            continue
        for j, (px, py) in enumerate(_other_chips(x, y)):
            there = 2 * px + py
            part = src.at[there] if mode == "slots" else src
            out.append(_rcopy(part, land.at[there if arriving else chip], send_sems, recv_sems, (N_CHIP - 1) * i + j, (px, py, c)))
    return out


def _land_shape(src, mode):
    if mode == "pair":
        return (src.shape[0], src.shape[1] // 2, src.shape[2])
    if mode == "all":
        return (N_DEV,) + src.shape
    return (N_CHIP,) + (src.shape[1:] if mode == "slots" else src.shape)


def _send_start(srcs, mode, name):
    n = len(srcs)
    n_sem = {"pair": 1, "all": N_DEV - 1}.get(mode, N_CHIP - 1) * n
    lands = [lax.empty(_land_shape(s, mode), s.dtype) for s in srcs]

    def body(*refs):
        ins, zones, (send_sems, recv_sems) = refs[:n], refs[n:2 * n], refs[2 * n:2 * n + 2]
        for cp in _split_copies(ins, zones, send_sems, recv_sems, mode, False):
            cp.start()
        refs[-1][...] = jnp.zeros_like(refs[-1])

    hbm = lambda a: pltpu.HBM(a.shape, a.dtype)
    outs = pl.pallas_call(
        body, name=name, in_specs=[_HBM] * (2 * n),
        out_shape=(pltpu.SemaphoreType.DMA((n_sem,)), pltpu.SemaphoreType.DMA((n_sem,)), *[hbm(a) for a in srcs],
                   *[hbm(a) for a in lands], jax.ShapeDtypeStruct((SUBLANES, LANES), F32)),
        out_specs=(_SEM, _SEM, *[_HBM] * (2 * n), _VMEM), input_output_aliases={i: 2 + i for i in range(2 * n)},
        compiler_params=pltpu.CompilerParams(has_side_effects=_DATAFLOW),
    )(*[pltpu.with_memory_space_constraint(a, pltpu.HBM) for a in list(srcs) + lands])
    return (outs[0], outs[1], list(outs[2:2 + n]), list(outs[2 + n:2 + 2 * n])), outs[-1]


def _send_wait(state, after, mode, name):
    send_sems, recv_sems, srcs, lands = state
    n = len(srcs)

    def body(*refs):
        ins, zones, (send_s, recv_s) = refs[:n], refs[n:2 * n], refs[2 * n:2 * n + 2]
        for cp in _split_copies(ins, zones, send_s, recv_s, mode, True):
            cp.wait_send()
            cp.wait_recv()

    hbm = lambda a: pltpu.HBM(a.shape, a.dtype)
    outs = pl.pallas_call(
        body, name=name, in_specs=[_HBM] * (2 * n) + [_SEM, _SEM, _ANY],
        out_shape=tuple(hbm(a) for a in srcs + lands), out_specs=tuple([_HBM] * (2 * n)),
        input_output_aliases={i: i for i in range(2 * n)},
        compiler_params=pltpu.CompilerParams(has_side_effects=_DATAFLOW),
    )(*srcs, *lands, send_sems, recv_sems, after)
    return list(outs[:n]), list(outs[n:])


def _sum_lead(v, name):
    n, r, cc = v.shape
    tr = _tile(r, _ROW_TILES + (8,))

    def body(v_ref, o_ref):
        acc = v_ref[0].astype(F32)
        for k in range(1, n):
            acc = acc + v_ref[k].astype(F32)
        o_ref[...] = acc

    return pl.pallas_call(
        body, grid=(r // tr,), in_specs=[pl.BlockSpec((n, tr, cc), lambda i: (0, i, 0))],
        out_specs=pl.BlockSpec((tr, cc), lambda i: (i, 0)), out_shape=jax.ShapeDtypeStruct((r, cc), F32),
        compiler_params=_params("parallel"), name=name,
    )(v)


def _chip_sum(arrived, mine, chip_idx, name):
    n, r, cc = arrived.shape
    tr = _tile(r, _ROW_TILES)

    def body(chip_ref, a_ref, m_ref, o_ref):
        acc = jnp.zeros((tr, cc), F32)
        for k in range(n):
            acc = acc + jnp.where(chip_ref[0] == k, m_ref[...], a_ref[k]).astype(F32)
        o_ref[...] = acc

    return pl.pallas_call(
        body,
        grid_spec=pltpu.PrefetchScalarGridSpec(
            num_scalar_prefetch=1, grid=(r // tr,),
            in_specs=[pl.BlockSpec((n, tr, cc), lambda i, chip_ref: (0, i, 0)),
                      pl.BlockSpec((None, tr, cc), lambda i, chip_ref: (chip_ref[0], i, 0))],
            out_specs=pl.BlockSpec((tr, cc), lambda i, chip_ref: (i, 0))),
        out_shape=jax.ShapeDtypeStruct((r, cc), F32), compiler_params=_params("parallel"), name=name,
    )(chip_idx, arrived, mine)


def _pair_share(reds, name):
    n = len(reds)

    def body(*refs):
        ins, outs, (send_sems, recv_sems) = refs[:n], refs[n:2 * n], refs[2 * n:]
        x, y, c = _place()
        copies = [_rcopy(ins[i], outs[i].at[c], send_sems, recv_sems, i, (x, y, 1 - c)) for i in range(n)]
        for cp in copies:
            cp.start()
        for i in range(n):
            _rcopy(ins[i], outs[i].at[1 - c], send_sems, recv_sems, i, (x, y, 1 - c)).wait_recv()
        for cp in copies:
            cp.wait_send()

    return pl.pallas_call(
        body, in_specs=[_ANY] * n, out_specs=[_ANY] * n, out_shape=[jax.ShapeDtypeStruct((2,) + r.shape, r.dtype) for r in reds],
        scratch_shapes=[pltpu.SemaphoreType.DMA((n,)), pltpu.SemaphoreType.DMA((n,))], name=name,
    )(*reds)


def _pack(arrs, rows_multiple, dtype):
    flat = jnp.concatenate([a.reshape(-1).astype(dtype) for a in arrs])
    unit = rows_multiple * PACK_COLS
    total = -(-flat.shape[0] // unit) * unit
    return jnp.pad(flat, (0, total - flat.shape[0])).reshape(-1, PACK_COLS)


def _unpack(buf, shapes):
    lead = buf.shape[:-2]
    flat = buf.reshape(*lead, -1)
    out, off = [], 0
    for shp in shapes:
        n = 1
        for s in shp:
            n *= s
        out.append(flat[..., off:off + n].reshape(*lead, *shp))
        off += n
    return out


def _join_shards(parts, axis):
    moved = jnp.moveaxis(parts, 0, axis)
    shp = list(moved.shape)
    shp[axis:axis + 2] = [shp[axis] * shp[axis + 1]]
    return moved.reshape(shp)


def _my_shard(full, axis, chip):
    n = full.shape[axis] // N_CHIP
    return lax.dynamic_slice_in_dim(full, chip * n, n, axis)


SMALL = {"norm_g": 2, "ffn_conv_w": 2, "rg_conv_w": 2, "gla_w_alpha": 2, "gla_b_alpha": 1, "gla_norm_g": 1,
         "ada_b": None, "ffn_conv_b": None, "rg_conv_b": None, "rg_ba": None, "rg_bx": None, "rg_lambda": None}
BIG = {"rg_w_in": True, "rg_wa": False, "rg_wx": False, "rg_w_out": False, "ffn_w_up": True, "ffn_w_down": False,
       "gla_w_in": True, "gla_w_out": False}
WEIGHTS = ["ada_w", "ada_b", "norm_g", "ffn_w_up", "ffn_conv_w", "ffn_conv_b", "ffn_w_down", "rg_w_in", "rg_conv_w", "rg_conv_b",
           "rg_wa", "rg_ba", "rg_wx", "rg_bx", "rg_lambda", "rg_w_out", "gla_w_in", "gla_w_alpha", "gla_b_alpha", "gla_norm_g",
           "gla_w_out"]


def kernel(x, c, ada_w, ada_b, norm_g, ffn_w_up, ffn_conv_w, ffn_conv_b, ffn_w_down, rg_w_in, rg_conv_w, rg_conv_b, rg_wa, rg_ba, rg_wx, rg_bx, rg_lambda, rg_w_out, gla_w_in, gla_w_alpha, gla_b_alpha, gla_norm_g, gla_w_out, loss_target, m_ada_w, m_ada_b, m_norm_g, m_ffn_w_up, m_ffn_conv_w, m_ffn_conv_b, m_ffn_w_down, m_rg_w_in, m_rg_conv_w, m_rg_conv_b, m_rg_wa, m_rg_ba, m_rg_wx, m_rg_bx, m_rg_lambda, m_rg_w_out, m_gla_w_in, m_gla_w_alpha, m_gla_b_alpha, m_gla_norm_g, m_gla_w_out, v_ada_w, v_ada_b, v_norm_g, v_ffn_w_up, v_ffn_conv_w, v_ffn_conv_b, v_ffn_w_down, v_rg_w_in, v_rg_conv_w, v_rg_conv_b, v_rg_wa, v_rg_ba, v_rg_wx, v_rg_bx, v_rg_lambda, v_rg_w_out, v_gla_w_in, v_gla_w_alpha, v_gla_b_alpha, v_gla_norm_g, v_gla_w_out):
    wts = dict(ada_w=ada_w, ada_b=ada_b, norm_g=norm_g, ffn_w_up=ffn_w_up, ffn_conv_w=ffn_conv_w, ffn_conv_b=ffn_conv_b,
               ffn_w_down=ffn_w_down, rg_w_in=rg_w_in, rg_conv_w=rg_conv_w, rg_conv_b=rg_conv_b, rg_wa=rg_wa, rg_ba=rg_ba,
               rg_wx=rg_wx, rg_bx=rg_bx, rg_lambda=rg_lambda, rg_w_out=rg_w_out, gla_w_in=gla_w_in, gla_w_alpha=gla_w_alpha,
               gla_b_alpha=gla_b_alpha, gla_norm_g=gla_norm_g, gla_w_out=gla_w_out)
    mom1 = dict(ada_w=m_ada_w, ada_b=m_ada_b, norm_g=m_norm_g, ffn_w_up=m_ffn_w_up, ffn_conv_w=m_ffn_conv_w,
                ffn_conv_b=m_ffn_conv_b, ffn_w_down=m_ffn_w_down, rg_w_in=m_rg_w_in, rg_conv_w=m_rg_conv_w,
                rg_conv_b=m_rg_conv_b, rg_wa=m_rg_wa, rg_ba=m_rg_ba, rg_wx=m_rg_wx, rg_bx=m_rg_bx, rg_lambda=m_rg_lambda,
                rg_w_out=m_rg_w_out, gla_w_in=m_gla_w_in, gla_w_alpha=m_gla_w_alpha, gla_b_alpha=m_gla_b_alpha,
                gla_norm_g=m_gla_norm_g, gla_w_out=m_gla_w_out)
    mom2 = dict(ada_w=v_ada_w, ada_b=v_ada_b, norm_g=v_norm_g, ffn_w_up=v_ffn_w_up, ffn_conv_w=v_ffn_conv_w,
                ffn_conv_b=v_ffn_conv_b, ffn_w_down=v_ffn_w_down, rg_w_in=v_rg_w_in, rg_conv_w=v_rg_conv_w,
                rg_conv_b=v_rg_conv_b, rg_wa=v_rg_wa, rg_ba=v_rg_ba, rg_wx=v_rg_wx, rg_bx=v_rg_bx, rg_lambda=v_rg_lambda,
                rg_w_out=v_rg_w_out, gla_w_in=v_gla_w_in, gla_w_alpha=v_gla_w_alpha, gla_b_alpha=v_gla_b_alpha,
                gla_norm_g=v_gla_norm_g, gla_w_out=v_gla_w_out)
    xi, yi, ci = _place()
    chip, me = 2 * xi + yi, 4 * xi + 2 * yi + ci
    d = x.shape[-1]
    depth = ada_w.shape[0]
    n_ada = ada_w.shape[-1]
    sharded_small = [k for k, ax in SMALL.items() if ax is not None]

    sm = _all_gather_8(_pack([c] + [wts[k] for k in sharded_small], SUBLANES, F32), "gather_small")
    c_all = sm[:, 0, :]
    parts = _unpack(sm[0::2], [c.shape] + [wts[k].shape for k in sharded_small])[1:]
    full = {k: _join_shards(p, SMALL[k]) for k, p in zip(sharded_small, parts)}
    for k, ax in SMALL.items():
        if ax is None:
            full[k] = wts[k]

    c16 = jnp.pad(c_all, ((0, ADA_ROWS - N_DEV), (0, 0)))
    ada_b_mine = lax.dynamic_slice_in_dim(ada_b, chip * n_ada, n_ada, 1)[:, None, :]
    mod_cols = _ada_fwd(c16, ada_w, ada_b_mine, "ada_fwd")
    mod_all = _all_gather_8(mod_cols.reshape(-1, PACK_COLS), "gather_mod")[0::2].reshape(N_CHIP, depth, ADA_ROWS, n_ada)
    mod = jnp.swapaxes(lax.dynamic_index_in_dim(mod_all, me, 2, keepdims=False), 0, 1).reshape(depth, 6, d)

    items = [(k, l) for k in BIG for l in range(wts[k].shape[0])]
    stage_of = lambda k, l: "rg" if k.startswith("rg_") else ("ffn0" if (k.startswith("ffn_") and l == 0) else "l1")
    staged = {st: [it for it in items if stage_of(*it) == st] for st in ("rg", "ffn0", "l1")}
    staged["gla"] = [it for it in staged["l1"] if it[0].startswith("gla_")]
    staged["ffn1"] = [it for it in staged["l1"] if it[0].startswith("ffn_")]
    staged["l1"] = staged["gla"] + staged["ffn1"]
    shard = lambda k, l: (wts[k][l].T if k == "gla_w_in" else wts[k][l].reshape(-1, wts[k].shape[-1])).astype(BF16)
    own = lambda got, mine: [lax.dynamic_update_index_in_dim(g, m, chip, 0) for g, m in zip(got, mine)]
    rows_joined = lambda v: v.reshape(-1, v.shape[-1])

    def placed(its, slots):
        out = {"ffn_w_up": {}, "ffn_w_down": {}}
        for (k, l), v in zip(its, slots):
            if k == "ffn_w_up":
                out[k][l] = v
            elif k == "ffn_w_down":
                out[k][l] = rows_joined(v)
            elif k in ("rg_wa", "rg_wx"):
                out[k] = _slots_to_block_rows(v, RG_BLOCKS)
            elif k == "gla_w_in":
                out[k] = rows_joined(v)
            else:
                out[k] = v if BIG[k] else rows_joined(v)
        return out

    after_mod = (mod[0, 0, 0] * 0.0).astype(BF16)
    sh_rg = [shard(k, l) + after_mod for k, l in staged["rg"]]
    local = {k: (v if k in ("norm_g", "ffn_conv_w", "ffn_conv_b") else v[0]) for k, v in full.items()}
    local.update(placed(staged["rg"], own(_gather_chips(sh_rg, "gather_weights_rg"), sh_rg)))
    sh_late, flying = {}, {}
    after_rg = (local["rg_w_out"][0, 0].astype(F32) * 0.0).astype(BF16)
    sh_late["ffn0"] = [shard(k, l) + after_rg for k, l in staged["ffn0"]]
    flying["ffn0"], tok = _send_start(sh_late["ffn0"], "whole", "weights_ffn0_start")
    for stage in ("gla", "ffn1"):
        sh_late[stage] = [shard(k, l) + tok[0, 0].astype(BF16) for k, l in staged[stage]]
        flying[stage], tok = _send_start(sh_late[stage], "whole", f"weights_{stage}_start")
    mod = mod + tok[0, 0]

    def fetch(stage, after):
        mine, got = _send_wait(flying[stage], after, "whole", f"weights_{stage}_wait")
        return placed(staged[stage], own(got, mine))

    c_idx = ci.reshape(1).astype(jnp.int32)
    gslots, paired, psums, sent, started = {}, {}, {}, {}, {}

    def grad_slots(gr, k, l):
        g = gr[k][l] if k in ("ffn_w_up", "ffn_w_down") else gr[k]
        if k in ("rg_wa", "rg_wx"):
            return _block_rows_to_slots(g)
        if k == "gla_w_in":
            return _col_slots(g, "gla_w_in_grad_slots")
        return g if BIG[k] else g.reshape(N_CHIP, -1, g.shape[-1])

    def done(stage, gr):
        gslots[stage] = [grad_slots(gr, k, l) for k, l in staged[stage]]
        paired[stage], token = _send_start(gslots[stage], "pair", f"grads_{stage}_pair_start")
        return token[0, 0]

    def later(stage, after):
        mine, theirs = _send_wait(paired[stage], after, "pair", f"grads_{stage}_pair_wait")
        psums[stage] = [_pair_sum(g, t, c_idx, f"grads_pair_sum_{k}{l}") for (k, l), g, t in zip(staged[stage], mine, theirs)]
        sent[stage], started[stage] = _send_start(psums[stage], "slots", f"grads_{stage}_start")
        return started[stage][0, 0]

    cols, grad_x, gr = _local_step(x[0], loss_target[0], mod, local, fetch, done, later)
    loss_mine = (0.5 * jnp.sum(cols) / d).reshape(1)

    small_names = [k for k in SMALL if k != "ada_b"]
    small_flying, small_sent = _send_start([_pack([gr[k] for k in small_names] + [gr["mod"], loss_mine], SUBLANES, F32)], "all",
                                           "grads_small_start")
    small_shapes = [full[k].shape for k in small_names] + [(depth, 6 * d), (1,)]
    chip_idx = chip.reshape(1).astype(jnp.int32)
    delta, new_m, new_v = {}, {}, {}
    grads = {}

    def reduce_and_update(stages, after, dep):
        its = [(st, n) for st in stages for n in range(len(staged[st]))]
        back = {st: _send_wait(sent[st], after, "slots", f"grads_{st}_wait") for st in stages}
        halves = [_chip_sum(back[st][1][n], back[st][0][n], chip_idx, "grads_chip_sum_%s%d" % staged[st][n]) for st, n in its]
        shared = _pair_share(halves, "grads_pair_share_" + stages[0])
        reduced = [lax.dynamic_update_index_in_dim(s2, h, ci, 0).reshape(-1, h.shape[-1]) for s2, h in zip(shared, halves)]
        last = None
        for k in BIG:
            gs_k = [g for (st, n), g in zip(its, reduced) if staged[st][n][0] == k]
            if gs_k:
                last = update(k, gs_k, dep)
        return last

    def update(k, gs_k, dep=None):
        shp = wts[k].shape
        if k == "gla_w_in":
            view, back = (lambda a: jnp.swapaxes(a, 1, 2)), (lambda o: jnp.swapaxes(o, 1, 2))
            gs_k = [g.T for g in gs_k]
        else:
            view, back = (lambda a: a.reshape(a.shape[0], -1, a.shape[-1])), (lambda o: o.reshape(shp))
        outs = _adamw(view(wts[k]), gs_k, view(mom1[k]), view(mom2[k]), "adamw_" + k, dep)
        grads[k], delta[k], new_m[k], new_v[k] = (back(o) for o in outs)
        return new_v[k]

    later("rg", small_sent)
    done_late = reduce_and_update(("ffn0", "l1"), grad_x, started["rg"])
    (small_mine,), (gs,) = _send_wait(small_flying, done_late, "all", "grads_small_wait")
    gs = lax.dynamic_update_index_in_dim(gs, small_mine, 2 * chip + ci, 0)
    *small_sum, g_ada_b, loss = _unpack(_sum_lead(gs, "sum_small_grads"), small_shapes)
    loss = loss[0]
    grads.update(zip(small_names, small_sum))
    grads["ada_b"] = g_ada_b
    for k in sharded_small:
        grads[k] = _my_shard(grads[k], SMALL[k], chip)
    dmod_all = _unpack(gs, small_shapes)[-2].reshape(N_DEV, depth, N_CHIP, n_ada)
    dmod_mine = jnp.swapaxes(lax.dynamic_index_in_dim(dmod_all, chip, 2, keepdims=False), 0, 1)
    g_ada_w = _ada_bwd(c16, jnp.pad(dmod_mine, ((0, 0), (0, ADA_ROWS - N_DEV), (0, 0))), "ada_bwd")
    update("ada_w", g_ada_w)
    small_shard_shapes = [wts[k].shape for k in SMALL]
    packed = [_pack([src[k] for k in SMALL], SUBLANES, F32) for src in (wts, grads, mom1, mom2)]
    outs = _adamw(packed[0][None], [packed[1]], packed[2][None], packed[3][None], "adamw_small")
    for dst, o in zip((delta, new_m, new_v), outs[1:]):
        for k, a in zip(SMALL, _unpack(o[0], small_shard_shapes)):
            dst[k] = a
    reduce_and_update(("rg",), outs[3], None)

    return (loss, grad_x[None], *[grads[k] for k in WEIGHTS], *[delta[k] for k in WEIGHTS], *[new_m[k] for k in WEIGHTS],
            *[new_v[k] for k in WEIGHTS])
```

```python
import jax
import jax.numpy as jnp
from jax import lax
from jax.experimental import pallas as pl
from jax.experimental.pallas import tpu as pltpu

F32 = jnp.float32
BF16 = jnp.bfloat16
MXU_DTYPE = BF16

EPS = 1e-6
RG_C = 8.0
RG_BLOCKS = 4
RG_CONV = 4
GLA_HEADS = 4
GLA_TAU = 16.0
GLA_CHUNK = 64
GLA_RANK = 16
FFN_CONV = 3
ADAM_LR = 0.001
ADAM_B1 = 0.9
ADAM_B2 = 0.999
ADAM_EPS = 1e-08
ADAM_WD = 0.01
ADAM_STEP = 10

LANES = 128
SUBLANES = 8
VMEM_LIMIT = 56 * 1024 * 1024
CB = 256
MESH = pl.DeviceIdType.MESH
N_DEV = 8
N_CHIP = 4


def _params(*sem):
    return pltpu.CompilerParams(dimension_semantics=sem, vmem_limit_bytes=VMEM_LIMIT)


def _tile(dim, prefs):
    for p in prefs:
        if dim % p == 0:
            return p
    return dim


def _dot(a, b, dims):
    return lax.dot_general(a.astype(MXU_DTYPE), b.astype(MXU_DTYPE), (dims, ((), ())), preferred_element_type=F32)


def _dot_nn(a, b):
    return _dot(a, b, ((1,), (0,)))


def _dot_nt(a, b):
    return _dot(a, b, ((1,), (1,)))


def _dot_tn(a, b):
    return _dot(a, b, ((0,), (0,)))


def _mm(a, b, *, ta=False, tb=False, a_parts=1, b_parts=1, w_slots=1, out_slots=1, out_dtype=F32, tm_max=1408, name):
    if ta:
        k_dim, m_dim = a.shape
        n_dim = b.shape[-1] * b_parts
    else:
        m_dim, k_dim = a.shape[-2], a.shape[-1] * a_parts
        n_dim = b.shape[-2] if tb else b.shape[-1] * w_slots
    n_unit = n_dim // max(b_parts, out_slots, 1 if tb else w_slots)
    k_unit = k_dim // max(a_parts, w_slots if tb else 1)
    tm = _tile(m_dim, tuple(t for t in (1024, 1408, 512, 256, 128) if t <= max(tm_max, 128)))
    tn = _tile(n_unit, (1024, 1408, 896, 512, 256, 128))
    tk = _tile(k_unit, (1024, 1408, 896, 512, 256, 128))
    nk = k_dim // tk
    dims = ((0 if ta else 1,), (1 if tb else 0,))

    def spec(shape, parts, total, tile, col_grid, row_grid):
        per = total // parts // tile

        def index(i, j, k):
            g = {"i": i, "j": j, "k": k}
            col, row = g[col_grid], g[row_grid]
            return (row, col) if parts == 1 else (col // per, row, col % per)

        return pl.BlockSpec(shape if parts == 1 else (None,) + shape, index)

    def body(a_ref, b_ref, o_ref, *acc):
        if nk == 1:
            o_ref[...] = _dot(a_ref[...], b_ref[...], dims).astype(o_ref.dtype)
            return
        acc_ref, k = acc[0], pl.program_id(2)

        @pl.when(k == 0)
        def _():
            acc_ref[...] = jnp.zeros_like(acc_ref)

        acc_ref[...] += _dot(a_ref[...], b_ref[...], dims)

        @pl.when(k == nk - 1)
        def _():
            o_ref[...] = acc_ref[...].astype(o_ref.dtype)

    if ta:
        a_spec = spec((tk, tm), 1, m_dim, tm, "i", "k")
        b_spec = spec((tk, tn), b_parts, n_dim, tn, "j", "k")
    elif tb:
        a_spec = spec((tm, tk), a_parts, k_dim, tk, "k", "i")
        b_spec = spec((tn, tk), w_slots, k_dim, tk, "k", "j")
    else:
        a_spec = spec((tm, tk), a_parts, k_dim, tk, "k", "i")
        b_spec = spec((tk, tn), w_slots, n_dim, tn, "j", "k")
    out_shape = (m_dim, n_dim) if out_slots == 1 else (out_slots, m_dim, n_dim // out_slots)
    return pl.pallas_call(
        body,
        grid=(m_dim // tm, n_dim // tn, nk),
        in_specs=[a_spec, b_spec],
        out_specs=spec((tm, tn), out_slots, n_dim, tn, "j", "i"),
        out_shape=jax.ShapeDtypeStruct(out_shape, out_dtype),
        scratch_shapes=[pltpu.VMEM((tm, tn), F32)] if nk > 1 else [],
        compiler_params=_params("parallel", "parallel", "arbitrary"),
        name=name,
    )(a, b)


ROW_TILES = (1024, 512)


def _row_specs(s, d, ts):
    return pl.BlockSpec((ts, d), lambda i: (i, 0)), pl.BlockSpec((1, d), lambda i: (0, 0))


def _norm_mod_fwd(x, g, sc, sh, name):
    s, d = x.shape
    ts = _tile(s, ROW_TILES)

    def body(x_ref, g_ref, sc_ref, sh_ref, h_ref):
        xv = x_ref[...]
        r = lax.rsqrt(jnp.mean(xv * xv, axis=-1, keepdims=True) + EPS)
        h_ref[...] = (((xv * r) * g_ref[...]) * (1.0 + sc_ref[...]) + sh_ref[...]).astype(h_ref.dtype)

    row, vec = _row_specs(s, d, ts)
    return pl.pallas_call(
        body, grid=(s // ts,), in_specs=[row, vec, vec, vec], out_specs=row,
        out_shape=jax.ShapeDtypeStruct((s, d), MXU_DTYPE), compiler_params=_params("parallel"), name=name,
    )(x, g, sc, sh)


def _norm_mod_bwd(dh, x, g, sc, dres, name):
    s, d = x.shape
    ts = _tile(s, ROW_TILES)

    def body(dh_ref, x_ref, g_ref, sc_ref, dres_ref, dx_ref, dg_ref, dsc_ref, dsh_ref, acc_ref):
        i = pl.program_id(0)

        @pl.when(i == 0)
        def _():
            acc_ref[...] = jnp.zeros_like(acc_ref)

        xv, dhv = x_ref[...], dh_ref[...]
        r = lax.rsqrt(jnp.mean(xv * xv, axis=-1, keepdims=True) + EPS)
        n = xv * r
        acc_ref[0:1, :] += jnp.sum(dhv * n, axis=0, keepdims=True)
        acc_ref[1:2, :] += jnp.sum(dhv, axis=0, keepdims=True)
        dn = dhv * ((1.0 + sc_ref[...]) * g_ref[...])
        dx_ref[...] = dres_ref[...] + r * (dn - n * jnp.mean(dn * n, axis=-1, keepdims=True))
        dg_ref[...] = (1.0 + sc_ref[...]) * acc_ref[0:1, :]
        dsc_ref[...] = g_ref[...] * acc_ref[0:1, :]
        dsh_ref[...] = acc_ref[1:2, :]

    row, vec = _row_specs(s, d, ts)
    vshape = jax.ShapeDtypeStruct((1, d), F32)
    return pl.pallas_call(
        body, grid=(s // ts,), in_specs=[row, row, vec, vec, row], out_specs=[row, vec, vec, vec],
        out_shape=[jax.ShapeDtypeStruct((s, d), F32), vshape, vshape, vshape],
        scratch_shapes=[pltpu.VMEM((SUBLANES, d), F32)], compiler_params=_params("arbitrary"), name=name,
    )(dh, x, g, sc, dres)


def _post_norm_fwd(x, y, g, gt, g2, sc, sh, name):
    s, d = x.shape
    ts = _tile(s, ROW_TILES)

    def body(x_ref, y_ref, g_ref, gt_ref, g2_ref, sc_ref, sh_ref, o_ref, h_ref):
        yv = y_ref[...]
        r = lax.rsqrt(jnp.mean(yv * yv, axis=-1, keepdims=True) + EPS)
        xn = x_ref[...] + gt_ref[...] * ((yv * r) * g_ref[...])
        o_ref[...] = xn
        r2 = lax.rsqrt(jnp.mean(xn * xn, axis=-1, keepdims=True) + EPS)
        h_ref[...] = (((xn * r2) * g2_ref[...]) * (1.0 + sc_ref[...]) + sh_ref[...]).astype(h_ref.dtype)

    row, vec = _row_specs(s, d, ts)
    return pl.pallas_call(
        body, grid=(s // ts,), in_specs=[row, row] + [vec] * 5, out_specs=[row, row],
        out_shape=[jax.ShapeDtypeStruct((s, d), F32), jax.ShapeDtypeStruct((s, d), MXU_DTYPE)],
        compiler_params=_params("parallel"), name=name,
    )(x, y, g, gt, g2, sc, sh)


def _post_bwd(dxn, y, g, gt, name):
    s, d = y.shape
    ts = _tile(s, ROW_TILES)

    def body(dxn_ref, y_ref, g_ref, gt_ref, dy_ref, dg_ref, dgt_ref, acc_ref):
        i = pl.program_id(0)

        @pl.when(i == 0)
        def _():
            acc_ref[...] = jnp.zeros_like(acc_ref)

        yv, dv = y_ref[...], dxn_ref[...]
        r = lax.rsqrt(jnp.mean(yv * yv, axis=-1, keepdims=True) + EPS)
        n = yv * r
        acc_ref[0:1, :] += jnp.sum(dv * n, axis=0, keepdims=True)
        dn = dv * (gt_ref[...] * g_ref[...])
        dy_ref[...] = (r * (dn - n * jnp.mean(dn * n, axis=-1, keepdims=True))).astype(dy_ref.dtype)
        dg_ref[...] = gt_ref[...] * acc_ref[0:1, :]
        dgt_ref[...] = g_ref[...] * acc_ref[0:1, :]

    row, vec = _row_specs(s, d, ts)
    vshape = jax.ShapeDtypeStruct((1, d), F32)
    return pl.pallas_call(
        body, grid=(s // ts,), in_specs=[row, row, vec, vec], out_specs=[row, vec, vec],
        out_shape=[jax.ShapeDtypeStruct((s, d), MXU_DTYPE), vshape, vshape],
        scratch_shapes=[pltpu.VMEM((SUBLANES, d), F32)], compiler_params=_params("arbitrary"), name=name,
    )(dxn, y, g, gt)


def _norm_post_bwd(dh, x, g, sc, dres, y, gp, gt, name):
    s, d = x.shape
    ts = _tile(s, (512,))

    def body(dh_ref, x_ref, g_ref, sc_ref, dres_ref, y_ref, gp_ref, gt_ref,
             dx_ref, dy_ref, dg_ref, dsc_ref, dsh_ref, dgp_ref, dgt_ref, acc_ref):
        i = pl.program_id(0)

        @pl.when(i == 0)
        def _():
            acc_ref[...] = jnp.zeros_like(acc_ref)

        xv, dhv = x_ref[...], dh_ref[...]
        r = lax.rsqrt(jnp.mean(xv * xv, axis=-1, keepdims=True) + EPS)
        n = xv * r
        acc_ref[0:1, :] += jnp.sum(dhv * n, axis=0, keepdims=True)
        acc_ref[1:2, :] += jnp.sum(dhv, axis=0, keepdims=True)
        dn = dhv * ((1.0 + sc_ref[...]) * g_ref[...])
        dx = dres_ref[...] + r * (dn - n * jnp.mean(dn * n, axis=-1, keepdims=True))
        dx_ref[...] = dx
        yv = y_ref[...]
        ry = lax.rsqrt(jnp.mean(yv * yv, axis=-1, keepdims=True) + EPS)
        ny = yv * ry
        acc_ref[2:3, :] += jnp.sum(dx * ny, axis=0, keepdims=True)
        dny = dx * (gt_ref[...] * gp_ref[...])
        dy_ref[...] = (ry * (dny - ny * jnp.mean(dny * ny, axis=-1, keepdims=True))).astype(dy_ref.dtype)
        dg_ref[...] = (1.0 + sc_ref[...]) * acc_ref[0:1, :]
        dsc_ref[...] = g_ref[...] * acc_ref[0:1, :]
        dsh_ref[...] = acc_ref[1:2, :]
        dgp_ref[...] = gt_ref[...] * acc_ref[2:3, :]
        dgt_ref[...] = gp_ref[...] * acc_ref[2:3, :]

    row, vec = _row_specs(s, d, ts)
    vshape = jax.ShapeDtypeStruct((1, d), F32)
    return pl.pallas_call(
        body, grid=(s // ts,), in_specs=[row, row, vec, vec, row, row, vec, vec], out_specs=[row, row] + [vec] * 5,
        out_shape=[jax.ShapeDtypeStruct((s, d), F32), jax.ShapeDtypeStruct((s, d), MXU_DTYPE)] + [vshape] * 5,
        scratch_shapes=[pltpu.VMEM((SUBLANES, d), F32)], compiler_params=_params("arbitrary"), name=name,
    )(dh, x, g, sc, dres, y, gp, gt)


def _post_loss(x, y, g, gt, tgt, name):
    s, d = x.shape
    ts = _tile(s, ROW_TILES)

    def body(x_ref, y_ref, g_ref, gt_ref, t_ref, col_ref, dx_ref):
        i = pl.program_id(0)

        @pl.when(i == 0)
        def _():
            col_ref[...] = jnp.zeros_like(col_ref)

        yv = y_ref[...]
        r = lax.rsqrt(jnp.mean(yv * yv, axis=-1, keepdims=True) + EPS)
        e = (x_ref[...] + gt_ref[...] * ((yv * r) * g_ref[...])) - t_ref[...]
        col_ref[...] += jnp.sum(e * e, axis=0, keepdims=True)
        dx_ref[...] = e * (1.0 / d)

    row, vec = _row_specs(s, d, ts)
    return pl.pallas_call(
        body, grid=(s // ts,), in_specs=[row, row, vec, vec, row], out_specs=[vec, row],
        out_shape=[jax.ShapeDtypeStruct((1, d), F32), jax.ShapeDtypeStruct((s, d), F32)],
        compiler_params=_params("arbitrary"), name=name,
    )(x, y, g, gt, tgt)


_GELU_C = 0.7978845608028654
_GELU_A = 0.044715


def _gelu(x):
    t = jnp.tanh(_GELU_C * (x + _GELU_A * x * x * x))
    return 0.5 * x * (1.0 + t), t


def _gelu_grad(x, t):
    return 0.5 * (1.0 + t) + 0.5 * x * (1.0 - t * t) * (_GELU_C * (1.0 + 3.0 * _GELU_A * x * x))


def _sigmoid(x):
    return 1.0 / (1.0 + jnp.exp(-x))


def _log1p_pos(y):
    u = 1.0 + y
    return jnp.where(u == 1.0, y, jnp.log(u) * (y / jnp.where(u == 1.0, 1.0, u - 1.0)))


def _softplus(x):
    return jnp.maximum(x, 0.0) + _log1p_pos(jnp.exp(-jnp.abs(x)))


def _one_minus_sq_exp(x, ex):
    z = 2.0 * x
    series = -z * (1.0 + z * (1.0 / 2 + z * (1.0 / 6 + z * (1.0 / 24 + z * (1.0 / 120)))))
    return jnp.where(z > -0.05, series, 1.0 - ex * ex)


SLAB = 16


def _cat(a, b):
    return jnp.concatenate([a, b], axis=1)


def _pair_specs(shape, nb, index):
    return [pl.BlockSpec(shape, lambda j, t: index(j, t) + (j,)), pl.BlockSpec(shape, lambda j, t: index(j, t) + (j + nb,))]


def _halo_row(ts, time_of):
    return lambda j, t: (jnp.maximum(time_of(t) * (ts // SUBLANES) - 1, 0),)


def _rows_from(groups, k):
    row = lax.broadcasted_iota(jnp.int32, groups[0].shape, 0)
    turned = [pltpu.roll(g, SUBLANES - k, axis=0) for g in groups]
    return [jnp.where(row < SUBLANES - k, lo, hi) for lo, hi in zip(turned[:-1], turned[1:])]


def _ffn_mid_fwd(p, cw, cb, name):
    s, f2 = p.shape
    ts = _tile(s, (1024, 512))
    nb, nt = f2 // (2 * CB), s // ts
    n_grp = SLAB // SUBLANES

    def body(pg_ref, pv_ref, hg_ref, hv_ref, cwg_ref, cwv_ref, cbg_ref, cbv_ref, a_ref, ga_ref, gb_ref):
        t = pl.program_id(1)
        cwv, bias = _cat(cwg_ref[...], cwv_ref[...]), _cat(cbg_ref[...], cbv_ref[...])
        w0, w1, w2 = cwv[0:1], cwv[1:2], cwv[2:3]

        def slab(before, cur, r0):
            pm2, pm1 = _rows_from([before] + cur, SUBLANES - 2), _rows_from([before] + cur, SUBLANES - 1)
            u = jnp.concatenate([bias + w0 * pm2[i] + w1 * pm1[i] + w2 * cur[i] for i in range(n_grp)], axis=0)
            g, v = u[:, :CB], u[:, CB:]
            gel, th = _gelu(g)
            rows = pl.ds(r0, SLAB)
            a_ref[rows, :] = (gel * v).astype(a_ref.dtype)
            ga_ref[rows, :] = gel.astype(ga_ref.dtype)
            gb_ref[rows, :] = (v * _gelu_grad(g, th)).astype(gb_ref.dtype)

        def pieces(rows):
            blk = _cat(pg_ref[rows, :], pv_ref[rows, :])
            return [blk[i * SUBLANES:(i + 1) * SUBLANES] for i in range(blk.shape[0] // SUBLANES)]

        slab(jnp.where(t > 0, _cat(hg_ref[...], hv_ref[...]), 0.0), pieces(pl.ds(0, SLAB)), 0)

        def loop(i, carry):
            r0 = pl.multiple_of(i * SLAB, SLAB)
            got = pieces(pl.ds(pl.multiple_of(r0 - SUBLANES, SUBLANES), SLAB + SUBLANES))
            slab(got[0], got[1:], r0)
            return carry

        lax.fori_loop(1, ts // SLAB, loop, 0, unroll=2)

    fwd = lambda t: t
    out = pl.BlockSpec((ts, CB), lambda j, t: (t, j))
    shape = jax.ShapeDtypeStruct((s, f2 // 2), MXU_DTYPE)
    return pl.pallas_call(
        body, grid=(nb, nt),
        in_specs=(_pair_specs((ts, CB), nb, lambda j, t: (t,)) + _pair_specs((SUBLANES, CB), nb, _halo_row(ts, fwd))
                  + _pair_specs((FFN_CONV, CB), nb, lambda j, t: (0,)) + _pair_specs((1, CB), nb, lambda j, t: (0,))),
        out_specs=[out, out, out], out_shape=[shape, shape, shape],
        compiler_params=_params("parallel", "arbitrary"), name=name,
    )(p, p, p, p, cw, cw, cb, cb)


def _ffn_mid_bwd(da, p, ga, gb, cw, name):
    s, f2 = p.shape
    ts = _tile(s, (1024, 512))
    nb, nt = f2 // (2 * CB), s // ts
    n_slab = ts // SLAB
    n_grp = SLAB // SUBLANES
    per_trip = 2

    def body(da_ref, ga_ref, gb_ref, pg_ref, pv_ref, cwg_ref, cwv_ref, dp_ref, dcw_ref, dcb_ref, next_du, acc):
        tt = pl.program_id(1)
        cwv = _cat(cwg_ref[...], cwv_ref[...])
        w0, w1, w2 = cwv[0:1], cwv[1:2], cwv[2:3]

        @pl.when(tt == 0)
        def _():
            next_du[...] = jnp.zeros_like(next_du)
            acc[...] = jnp.zeros_like(acc)

        def slab(r0, after, sums):
            rows = pl.ds(r0, SLAB)
            dav = da_ref[rows, :]
            du = _cat(dav * gb_ref[rows, :].astype(F32), dav * ga_ref[rows, :].astype(F32))
            p0 = _cat(pg_ref[rows, :], pv_ref[rows, :])
            cur = [du[i * SUBLANES:(i + 1) * SUBLANES] for i in range(n_grp)]
            du1, du2 = _rows_from(cur + [after], 1), _rows_from(cur + [after], 2)
            dpv = jnp.concatenate([w2 * cur[i] + w1 * du1[i] + w0 * du2[i] for i in range(n_grp)], axis=0).astype(dp_ref.dtype)
            dp_ref[0, rows, :] = dpv[:, :CB]
            dp_ref[1, rows, :] = dpv[:, CB:]
            for i in range(n_grp):
                pi = p0[i * SUBLANES:(i + 1) * SUBLANES]
                parts = (cur[i], du2[i] * pi, du1[i] * pi, cur[i] * pi)
                sums = parts if sums is None else tuple(x + y for x, y in zip(sums, parts))
            return cur[0], sums

        def loop(k, after):
            sums = None
            for j in range(per_trip):
                r0 = pl.multiple_of((n_slab - 1 - (k * per_trip + j)) * SLAB, SLAB)
                after, sums = slab(r0, after, sums)
            for q, part in enumerate(sums):
                acc[q] += part
            return after

        next_du[...] = lax.fori_loop(0, n_slab // per_trip, loop, next_du[...])

        @pl.when(tt == nt - 1)
        def _():
            for half in range(2):
                cols = slice(half * CB, (half + 1) * CB)
                dcb_ref[half] = jnp.sum(acc[0][:, cols], axis=0, keepdims=True)
                for k in range(FFN_CONV):
                    dcw_ref[half, k:k + 1, :] = jnp.sum(acc[1 + k][:, cols], axis=0, keepdims=True)

    rev = lambda t: nt - 1 - t
    tile = pl.BlockSpec((ts, CB), lambda j, t: (rev(t), j))
    return pl.pallas_call(
        body, grid=(nb, nt),
        in_specs=([tile, tile, tile] + _pair_specs((ts, CB), nb, lambda j, t: (rev(t),))
                  + _pair_specs((FFN_CONV, CB), nb, lambda j, t: (0,))),
        out_specs=[pl.BlockSpec((2, ts, CB), lambda j, t: (0, rev(t), j)),
                   pl.BlockSpec((2, FFN_CONV, CB), lambda j, t: (0, 0, j)),
                   pl.BlockSpec((2, 1, CB), lambda j, t: (0, 0, j))],
        out_shape=[jax.ShapeDtypeStruct((2, s, f2 // 2), MXU_DTYPE), jax.ShapeDtypeStruct((2, FFN_CONV, f2 // 2), F32),
                   jax.ShapeDtypeStruct((2, 1, f2 // 2), F32)],
        scratch_shapes=[pltpu.VMEM((SUBLANES, 2 * CB), F32), pltpu.VMEM((1 + FFN_CONV, SUBLANES, 2 * CB), F32)],
        compiler_params=_params("parallel", "arbitrary"), name=name,
    )(da, ga, gb, p, p, cw, cw)


def _rg_gates(xc, wa_ref, ba_ref, wx_ref, bx_ref, lam_ref):
    r = _sigmoid(_dot_nn(xc, wa_ref[0]) + ba_ref[...])
    ig = _sigmoid(_dot_nn(xc, wx_ref[0]) + bx_ref[...])
    sp = _softplus(-lam_ref[...])
    log_a = (-RG_C) * r * sp
    a = jnp.exp(log_a)
    mult = jnp.sqrt(_one_minus_sq_exp(log_a, a))
    return r, ig, sp, a, mult


def _rg_conv(scr, cw_ref, cb_ref, ts):
    views = [scr[5 + k:5 + k + ts, :] for k in range(RG_CONV)]
    xc = cb_ref[...]
    for k in range(RG_CONV):
        xc = xc + cw_ref[k:k + 1, :] * views[k]
    return xc, views


def _rg_param_specs():
    vec = pl.BlockSpec((1, CB), lambda g, t: (0, g))
    mat = pl.BlockSpec((1, CB, CB), lambda g, t: (g, 0, 0))
    return [pl.BlockSpec((RG_CONV, CB), lambda g, t: (0, g)), vec, mat, vec, mat, vec, vec]


def _scan_rows(a_scr, x_scr, out_ref, carry, ts, reverse):
    n = ts // SUBLANES
    row = lax.broadcasted_iota(jnp.int32, (SUBLANES, a_scr.shape[1]), 0)
    last = SUBLANES - 1

    def rows_of(k):
        return pl.ds(pl.multiple_of(k * SUBLANES, SUBLANES), SUBLANES)

    def local(k, _):
        rows = rows_of(k)
        a, x = a_scr[rows, :], x_scr[rows, :]
        if reverse:
            a = jnp.where(row == last, 1.0, pltpu.roll(a, last, axis=0))
            for sh in (1, 2, 4):
                keep = row < SUBLANES - sh
                x = x + a * jnp.where(keep, pltpu.roll(x, SUBLANES - sh, axis=0), 0.0)
                a = a * jnp.where(keep, pltpu.roll(a, SUBLANES - sh, axis=0), 1.0)
        else:
            for sh in (1, 2, 4):
                keep = row >= sh
                x = a * jnp.where(keep, pltpu.roll(x, sh, axis=0), 0.0) + x
                a = a * jnp.where(keep, pltpu.roll(a, sh, axis=0), 1.0)
        out_ref[rows, :] = x
        x_scr[rows, :] = a
        return 0

    lax.fori_loop(0, n, local, 0, unroll=4)

    def chain(k, c):
        rows = rows_of(n - 1 - k if reverse else k)
        v = out_ref[rows, :] + x_scr[rows, :] * c
        out_ref[rows, :] = v
        return a_scr[rows, :][0:1] * v[0:1] if reverse else v[last:last + 1]

    return lax.fori_loop(0, n, chain, carry, unroll=4)


def _rg_mid_fwd(pj, cw, cb, wa, ba, wx, bx, lam, name):
    s = pj.shape[0]
    nb = pj.shape[1] // (2 * CB)
    ts = _tile(s, (512,))
    nt = s // ts

    def body(gate_ref, x_ref, halo_ref, cw_ref, cb_ref, wa_ref, ba_ref, wx_ref, bx_ref, lam_ref, y_ref, hs_ref,
             scr, a_scr, u_scr, h_scr):
        t = pl.program_id(1)

        @pl.when(t == 0)
        def _():
            h_scr[...] = jnp.zeros_like(h_scr)

        scr[0:SUBLANES, :] = jnp.where(t > 0, halo_ref[...], 0.0)
        scr[SUBLANES:, :] = x_ref[...]
        xc, _ = _rg_conv(scr, cw_ref, cb_ref, ts)
        _, ig, _, a, mult = _rg_gates(xc, wa_ref, ba_ref, wx_ref, bx_ref, lam_ref)
        a_scr[...] = a
        u_scr[...] = mult * (ig * xc)
        h_scr[0:1, :] = _scan_rows(a_scr, u_scr, hs_ref, h_scr[0:1, :], ts, False)
        y_ref[...] = (_gelu(gate_ref[...])[0] * hs_ref[...]).astype(y_ref.dtype)

    blk = pl.BlockSpec((ts, CB), lambda g, t: (t, g))
    return pl.pallas_call(
        body, grid=(nb, nt),
        in_specs=_pair_specs((ts, CB), nb, lambda g, t: (t,))
        + [pl.BlockSpec((SUBLANES, CB), lambda g, t: _halo_row(ts, lambda u: u)(g, t) + (g + nb,))] + _rg_param_specs(),
        out_specs=[blk, blk],
        out_shape=[jax.ShapeDtypeStruct((s, nb * CB), MXU_DTYPE), jax.ShapeDtypeStruct((s, nb * CB), F32)],
        scratch_shapes=[pltpu.VMEM((ts + SUBLANES, CB), F32), pltpu.VMEM((ts, CB), F32), pltpu.VMEM((ts, CB), F32),
                        pltpu.VMEM((SUBLANES, CB), F32)],
        compiler_params=_params("parallel", "arbitrary"), name=name,
    )(pj, pj, pj, cw, cb, wa, ba, wx, bx, lam)


def _rg_mid_bwd(dy, pj, hs, cw, cb, wa, ba, wx, bx, lam, name):
    s = pj.shape[0]
    nb = pj.shape[1] // (2 * CB)
    ts = _tile(s, (512,))
    nt = s // ts

    def body(dy_ref, gate_ref, x_ref, halo_ref, hs_ref, hsh_ref, cw_ref, cb_ref, wa_ref, ba_ref, wx_ref, bx_ref, lam_ref,
             dpj_ref, dcw_ref, dcb_ref, dwa_ref, dba_ref, dwx_ref, dbx_ref, dlam_ref,
             scr, hscr, a_scr, d_scr, g_scr, dxscr, c_scr):
        tt = pl.program_id(1)
        t = nt - 1 - tt

        @pl.when(tt == 0)
        def _():
            c_scr[...] = jnp.zeros_like(c_scr)
            dxscr[ts:, :] = jnp.zeros((SUBLANES, CB), F32)
            for ref in (dcw_ref, dcb_ref, dwa_ref, dba_ref, dwx_ref, dbx_ref, dlam_ref):
                ref[...] = jnp.zeros_like(ref)

        scr[0:SUBLANES, :] = jnp.where(t > 0, halo_ref[...], 0.0)
        scr[SUBLANES:, :] = x_ref[...]
        hscr[0:SUBLANES, :] = jnp.where(t > 0, hsh_ref[...], 0.0)
        hscr[SUBLANES:, :] = hs_ref[...]
        xc, views = _rg_conv(scr, cw_ref, cb_ref, ts)
        r, ig, sp, a, mult = _rg_gates(xc, wa_ref, ba_ref, wx_ref, bx_ref, lam_ref)
        gate = gate_ref[...]
        gel, th = _gelu(gate)
        dyv = dy_ref[...]
        dpj_ref[0] = (dyv * hs_ref[...] * _gelu_grad(gate, th)).astype(dpj_ref.dtype)
        a_scr[...] = a
        d_scr[...] = dyv * gel
        c_scr[0:1, :] = _scan_rows(a_scr, d_scr, g_scr, c_scr[0:1, :], ts, True)
        du = g_scr[...]
        da = du * hscr[7:7 + ts, :]
        dmult = du * (ig * xc)
        dig = du * (mult * xc)
        dxc = du * (mult * ig)
        dlog_a = da * a - dmult * (a * a / mult)
        dlam_ref[...] += jnp.sum(dlog_a * r, axis=0, keepdims=True) * (RG_C * _sigmoid(-lam_ref[...]))
        dpr = dlog_a * ((-RG_C) * sp) * (r * (1.0 - r))
        dpi = dig * (ig * (1.0 - ig))
        dba_ref[...] += jnp.sum(dpr, axis=0, keepdims=True)
        dbx_ref[...] += jnp.sum(dpi, axis=0, keepdims=True)
        dwa_ref[0] += _dot_tn(xc, dpr)
        dwx_ref[0] += _dot_tn(xc, dpi)
        dxc = dxc + _dot_nt(dpr, wa_ref[0]) + _dot_nt(dpi, wx_ref[0])
        dcb_ref[...] += jnp.sum(dxc, axis=0, keepdims=True)
        for k in range(RG_CONV):
            dcw_ref[k:k + 1, :] += jnp.sum(dxc * views[k], axis=0, keepdims=True)
        dxscr[0:ts, :] = dxc
        dxp = cw_ref[3:4, :] * dxc
        for k in range(RG_CONV - 1):
            dxp = dxp + cw_ref[k:k + 1, :] * dxscr[3 - k:3 - k + ts, :]
        dpj_ref[1] = dxp.astype(dpj_ref.dtype)
        dxscr[ts:, :] = dxscr[0:SUBLANES, :]

    rev = lambda g, t: (nt - 1 - t, g)
    rev_halo = lambda g, t: (jnp.maximum((nt - 1 - t) * (ts // SUBLANES) - 1, 0), g)
    vec = pl.BlockSpec((1, CB), lambda g, t: (0, g))
    mat = pl.BlockSpec((1, CB, CB), lambda g, t: (g, 0, 0))
    d = nb * CB
    vshape = jax.ShapeDtypeStruct((1, d), F32)
    mshape = jax.ShapeDtypeStruct((nb, CB, CB), F32)
    return pl.pallas_call(
        body, grid=(nb, nt),
        in_specs=[pl.BlockSpec((ts, CB), rev)] + _pair_specs((ts, CB), nb, lambda g, t: (nt - 1 - t,))
        + [pl.BlockSpec((SUBLANES, CB), lambda g, t: (rev_halo(g, t)[0], g + nb)),
           pl.BlockSpec((ts, CB), rev), pl.BlockSpec((SUBLANES, CB), rev_halo)] + _rg_param_specs(),
        out_specs=[pl.BlockSpec((2, ts, CB), lambda g, t: (0, nt - 1 - t, g)), pl.BlockSpec((RG_CONV, CB), lambda g, t: (0, g)),
                   vec, mat, vec, mat, vec, vec],
        out_shape=[jax.ShapeDtypeStruct((2, s, d), MXU_DTYPE), jax.ShapeDtypeStruct((RG_CONV, d), F32), vshape, mshape, vshape,
                   mshape, vshape, vshape],
        scratch_shapes=[pltpu.VMEM((ts + SUBLANES, CB), F32), pltpu.VMEM((ts + SUBLANES, CB), F32), pltpu.VMEM((ts, CB), F32),
                        pltpu.VMEM((ts, CB), F32), pltpu.VMEM((ts, CB), F32), pltpu.VMEM((ts + SUBLANES, CB), F32),
                        pltpu.VMEM((SUBLANES, CB), F32)],
        compiler_params=_params("parallel", "arbitrary"), name=name,
    )(dy, pj, pj, pj, hs, hs, cw, cb, wa, ba, wx, bx, lam)


GLA_DK = 128
GLA_DV = 256
GLA_O_K = GLA_HEADS * GLA_DK
GLA_O_V = 2 * GLA_HEADS * GLA_DK
GLA_O_R = GLA_O_V + GLA_HEADS * GLA_DV
GLA_O_Z = GLA_O_R + GLA_HEADS * GLA_DV
GLA_IN = GLA_O_Z + GLA_RANK
GLA_TS = 256


def _dk(h, base=0):
    return slice(base + h * GLA_DK, base + (h + 1) * GLA_DK)


def _dv(h, base=0):
    return slice(base + h * GLA_DV, base + (h + 1) * GLA_DV)


def _split3(x):
    hi = x.astype(BF16)
    r1 = x - hi.astype(F32)
    mid = r1.astype(BF16)
    lo = (r1 - mid.astype(F32)).astype(BF16)
    return hi, mid, lo


def _chunk_cumsum(x, reverse):
    n = x.shape[0]
    i = lax.broadcasted_iota(jnp.int32, (n, n), 0)
    j = lax.broadcasted_iota(jnp.int32, (n, n), 1)
    same = (i // GLA_CHUNK) == (j // GLA_CHUNK)
    tri = jnp.where(same & ((j >= i) if reverse else (j <= i)), 1.0, 0.0).astype(BF16)
    out = jnp.zeros(x.shape, F32)
    for piece in _split3(x):
        out = out + lax.dot_general(tri, piece, (((1,), (0,)), ((), ())), preferred_element_type=F32)
    return out


def _gla_head(pj_ref, h):
    return (pj_ref[:, _dk(h)] * (GLA_DK ** -0.5), pj_ref[:, _dk(h, GLA_O_K)], pj_ref[:, _dv(h, GLA_O_V)],
            pj_ref[:, _dv(h, GLA_O_R)])


def _gla_decays(gc):
    gref = gc[GLA_CHUNK // 2:GLA_CHUNK // 2 + 1, :]
    glast = gc[GLA_CHUNK - 1:GLA_CHUNK, :]
    return jnp.exp(gc), jnp.exp(gc - gref), jnp.exp(gref - gc), jnp.exp(glast - gc), jnp.exp(glast)


def _causal_mask():
    i = lax.broadcasted_iota(jnp.int32, (GLA_CHUNK, GLA_CHUNK), 0)
    j = lax.broadcasted_iota(jnp.int32, (GLA_CHUNK, GLA_CHUNK), 1)
    return j <= i


def _log_sigmoid(x):
    return jnp.minimum(x, 0.0) - _log1p_pos(jnp.exp(-jnp.abs(x)))


def _gla_mid_fwd(pj, wal, bal, ng, name):
    s, nh = pj.shape[0], GLA_HEADS
    ts = _tile(s, (GLA_TS,))
    nt, nc = s // ts, ts // GLA_CHUNK

    def body(pj_ref, wal_ref, bal_ref, ng_ref, act_ref, o_ref, st_ref, s_scr):
        t = pl.program_id(0)

        @pl.when(t == 0)
        def _():
            s_scr[...] = jnp.zeros_like(s_scr)

        heads = []
        z = pj_ref[:, GLA_O_Z:]
        for h in range(nh):
            q, k, v, r = _gla_head(pj_ref, h)
            g = _log_sigmoid(_dot_nn(z, wal_ref[:, _dk(h)]) + bal_ref[:, _dk(h)]) * (1.0 / GLA_TAU)
            heads.append((q, k, v, r, _chunk_cumsum(g, False)))
        mask = _causal_mask()
        for c in range(nc):
            sl = slice(c * GLA_CHUNK, (c + 1) * GLA_CHUNK)
            for h, (q, k, v, r, gcum) in enumerate(heads):
                eg, eq, ek, ekd, egl = _gla_decays(gcum[sl])
                st = s_scr[h]
                st_ref[c, h] = st
                attn = jnp.where(mask, _dot_nt(q[sl] * eq, k[sl] * ek), 0.0)
                o_ref[sl, h * GLA_DV:(h + 1) * GLA_DV] = _dot_nt(q[sl] * eg, st) + _dot_nn(attn, v[sl])
                s_scr[h] = st * egl + _dot_tn(v[sl], k[sl] * ekd)
        for h, (q, k, v, r, gcum) in enumerate(heads):
            cols = slice(h * GLA_DV, (h + 1) * GLA_DV)
            o = o_ref[:, cols]
            on = o * lax.rsqrt(jnp.mean(o * o, axis=-1, keepdims=True) + EPS)
            act_ref[:, cols] = ((on * ng_ref[...]) * (r * _sigmoid(r))).astype(act_ref.dtype)

    blk = pl.BlockSpec((ts, nh * GLA_DV), lambda t: (t, 0))
    whole = lambda shape: pl.BlockSpec(shape, lambda t: (0,) * len(shape))
    return pl.pallas_call(
        body, grid=(nt,),
        in_specs=[pl.BlockSpec((ts, GLA_IN), lambda t: (t, 0)), whole((GLA_RANK, nh * GLA_DK)), whole((1, nh * GLA_DK)),
                  whole((1, GLA_DV))],
        out_specs=[blk, blk, pl.BlockSpec((nc, nh, GLA_DV, GLA_DK), lambda t: (t, 0, 0, 0))],
        out_shape=[jax.ShapeDtypeStruct((s, nh * GLA_DV), MXU_DTYPE), jax.ShapeDtypeStruct((s, nh * GLA_DV), F32),
                   jax.ShapeDtypeStruct((s // GLA_CHUNK, nh, GLA_DV, GLA_DK), F32)],
        scratch_shapes=[pltpu.VMEM((nh, GLA_DV, GLA_DK), F32)],
        compiler_params=_params("arbitrary"), name=name,
    )(pj, wal, bal, ng)


def _gla_mid_bwd(dact, pj, o, st, wal, bal, ng, name):
    s, nh = pj.shape[0], GLA_HEADS
    ts = _tile(s, (GLA_TS,))
    nt, nc = s // ts, ts // GLA_CHUNK

    def body(dact_ref, pj_ref, o_ref, st_ref, wal_ref, bal_ref, ng_ref, dpj_ref, dwal_ref, dbal_ref, dng_ref,
             ds_scr, dg_scr):
        tt = pl.program_id(0)

        @pl.when(tt == 0)
        def _():
            ds_scr[...] = jnp.zeros_like(ds_scr)
            dwal_ref[...] = jnp.zeros_like(dwal_ref)
            dbal_ref[...] = jnp.zeros_like(dbal_ref)
            dng_ref[...] = jnp.zeros_like(dng_ref)

        heads = []
        z = pj_ref[:, GLA_O_Z:]
        for h in range(nh):
            q, k, v, r = _gla_head(pj_ref, h)
            logit = _dot_nn(z, wal_ref[:, _dk(h)]) + bal_ref[:, _dk(h)]
            gcum = _chunk_cumsum(_log_sigmoid(logit) * (1.0 / GLA_TAU), False)
            ov = o_ref[:, h * GLA_DV:(h + 1) * GLA_DV]
            ro = lax.rsqrt(jnp.mean(ov * ov, axis=-1, keepdims=True) + EPS)
            on = ov * ro
            sg = _sigmoid(r)
            sil = r * sg
            dav = dact_ref[:, h * GLA_DV:(h + 1) * GLA_DV]
            dpj_ref[:, _dv(h, GLA_O_R)] = (dav * (on * ng_ref[...]) * (sg + sil * (1.0 - sg))).astype(dpj_ref.dtype)
            t1 = dav * sil
            dng_ref[...] += jnp.sum(t1 * on, axis=0, keepdims=True)
            dn = t1 * ng_ref[...]
            do = ro * (dn - on * jnp.mean(dn * on, axis=-1, keepdims=True))
            heads.append((q, k, v, logit, gcum, do))
        mask = _causal_mask()
        scale = GLA_DK ** -0.5
        last_row = lax.broadcasted_iota(jnp.int32, (GLA_CHUNK, GLA_DK), 0) == GLA_CHUNK - 1
        for c in reversed(range(nc)):
            sl = slice(c * GLA_CHUNK, (c + 1) * GLA_CHUNK)
            for h, (q, k, v, logit, gcum, do) in enumerate(heads):
                eg, eq, ek, ekd, egl = _gla_decays(gcum[sl])
                qc, kc, vc, doc = q[sl], k[sl], v[sl], do[sl]
                qg, qt, kt, kd = qc * eg, qc * eq, kc * ek, kc * ekd
                sp = st_ref[c, h]
                ds = ds_scr[h]
                attn = jnp.where(mask, _dot_nt(qt, kt), 0.0)
                dattn = jnp.where(mask, _dot_nt(doc, vc), 0.0)
                dqg = _dot_nn(doc, sp)
                dqt = _dot_nn(dattn, kt)
                dkt = _dot_tn(dattn, qt)
                dkd = _dot_nn(vc, ds)
                dpj_ref[sl, _dv(h, GLA_O_V)] = (_dot_tn(attn, doc) + _dot_nt(kd, ds)).astype(dpj_ref.dtype)
                dpj_ref[sl, _dk(h)] = (scale * (dqg * eg + dqt * eq)).astype(dpj_ref.dtype)
                dpj_ref[sl, _dk(h, GLA_O_K)] = (dkt * ek + dkd * ekd).astype(dpj_ref.dtype)
                kdd = dkd * kd
                dgl = jnp.sum(kdd, axis=0, keepdims=True) + jnp.sum(ds * sp, axis=0, keepdims=True) * egl
                dg_scr[h, sl, :] = dqg * qg + dqt * qt - dkt * kt - kdd + jnp.where(last_row, dgl, 0.0)
                ds_scr[h] = ds * egl + _dot_tn(doc, qg)
        dz = jnp.zeros((ts, GLA_RANK), F32)
        for h, (q, k, v, logit, gcum, do) in enumerate(heads):
            dlogit = _chunk_cumsum(dg_scr[h], True) * (1.0 / GLA_TAU) * _sigmoid(-logit)
            dz = dz + _dot_nt(dlogit, wal_ref[:, _dk(h)])
            dwal_ref[:, _dk(h)] += _dot_tn(z, dlogit)
            dbal_ref[:, _dk(h)] += jnp.sum(dlogit, axis=0, keepdims=True)
        dpj_ref[:, GLA_O_Z:] = dz.astype(dpj_ref.dtype)

    rev = lambda t: (nt - 1 - t, 0)
    whole = lambda shape: pl.BlockSpec(shape, lambda t: (0,) * len(shape))
    wide = pl.BlockSpec((ts, nh * GLA_DV), rev)
    return pl.pallas_call(
        body, grid=(nt,),
        in_specs=[wide, pl.BlockSpec((ts, GLA_IN), rev), wide,
                  pl.BlockSpec((nc, nh, GLA_DV, GLA_DK), lambda t: (nt - 1 - t, 0, 0, 0)),
                  whole((GLA_RANK, nh * GLA_DK)), whole((1, nh * GLA_DK)), whole((1, GLA_DV))],
        out_specs=[pl.BlockSpec((ts, GLA_IN), rev), whole((GLA_RANK, nh * GLA_DK)), whole((1, nh * GLA_DK)), whole((1, GLA_DV))],
        out_shape=[jax.ShapeDtypeStruct((s, GLA_IN), MXU_DTYPE), jax.ShapeDtypeStruct((GLA_RANK, nh * GLA_DK), F32),
                   jax.ShapeDtypeStruct((1, nh * GLA_DK), F32), jax.ShapeDtypeStruct((1, GLA_DV), F32)],
        scratch_shapes=[pltpu.VMEM((nh, GLA_DV, GLA_DK), F32), pltpu.VMEM((nh, ts, GLA_DK), F32)],
        compiler_params=_params("arbitrary"), name=name,
    )(dact, pj, o, st, wal, bal, ng)


def _adamw(w, gs, m, v, name, after=None):
    layers, rows, cols = w.shape
    gs = list(gs) if isinstance(gs, (list, tuple)) else gs
    n_g = len(gs) if isinstance(gs, list) else 1
    if rows % SUBLANES == 0:
        tr, tc = _tile(rows, (256, 128, 64, 32, 16, 8)), cols
    else:
        tr, tc = rows, _tile(cols, (256, 128))
    c1 = 1.0 / (1.0 - ADAM_B1 ** ADAM_STEP)
    c2 = 1.0 / (1.0 - ADAM_B2 ** ADAM_STEP)

    def body(*refs):
        g_refs, (w_ref, m_ref, v_ref) = refs[:n_g], refs[n_g:n_g + 3]
        go_ref, d_ref, mo_ref, vo_ref = refs[-4:]
        gv = g_refs[0][...]
        for l in range(1, n_g):
            gv = jnp.where(pl.program_id(0) == l, g_refs[l][...], gv)
        m2 = ADAM_B1 * m_ref[...] + (1.0 - ADAM_B1) * gv
        v2 = ADAM_B2 * v_ref[...] + (1.0 - ADAM_B2) * (gv * gv)
        d_ref[...] = (-ADAM_LR) * ((m2 * c1) / (jnp.sqrt(v2 * c2) + ADAM_EPS) + ADAM_WD * w_ref[...])
        go_ref[...] = gv
        mo_ref[...] = m2
        vo_ref[...] = v2

    spec = pl.BlockSpec((None, tr, tc), lambda l, i, j: (l, i, j))
    g_specs = [pl.BlockSpec((tr, tc), lambda l, i, j: (i, j))] * n_g if isinstance(gs, list) else [spec]
    extra = [] if after is None else [(after, _ANY)]
    shape = jax.ShapeDtypeStruct((layers, rows, cols), F32)
    return pl.pallas_call(
        body, grid=(layers, rows // tr, cols // tc), in_specs=g_specs + [spec] * 3 + [sp for _, sp in extra],
        out_specs=[spec] * 4, out_shape=[shape] * 4, compiler_params=_params("parallel", "parallel", "parallel"), name=name,
    )(*(gs if isinstance(gs, list) else [gs]), w, m, v, *[a for a, _ in extra])


def _col_slots(w, name):
    r, cc = w.shape
    c = cc // N_CHIP
    tr = _tile(r, (256,))

    def body(w_ref, o_ref):
        for j in range(N_CHIP):
            o_ref[j] = w_ref[:, j * c:(j + 1) * c]

    return pl.pallas_call(
        body, grid=(r // tr,), in_specs=[pl.BlockSpec((tr, cc), lambda i: (i, 0))],
        out_specs=pl.BlockSpec((N_CHIP, tr, c), lambda i: (0, i, 0)), out_shape=jax.ShapeDtypeStruct((N_CHIP, r, c), w.dtype),
        compiler_params=_params("parallel"), name=name,
    )(w)


def _from_col_slots(w, name):
    n, r, c = w.shape
    tr = _tile(r, (256,))

    def body(w_ref, o_ref):
        for j in range(n):
            o_ref[:, j * c:(j + 1) * c] = w_ref[j]

    return pl.pallas_call(
        body, grid=(r // tr,), in_specs=[pl.BlockSpec((n, tr, c), lambda i: (0, i, 0))],
        out_specs=pl.BlockSpec((tr, n * c), lambda i: (i, 0)), out_shape=jax.ShapeDtypeStruct((r, n * c), w.dtype),
        compiler_params=_params("parallel"), name=name,
    )(w)


def _block_rows_to_slots(w):
    g, r4, cc = w.shape
    return jnp.swapaxes(w.reshape(g, N_CHIP, r4 // N_CHIP, cc), 0, 1).reshape(N_CHIP, g * (r4 // N_CHIP), cc)


def _slots_to_block_rows(w, g):
    n, gr, cc = w.shape
    return jnp.swapaxes(w.reshape(n, g, gr // g, cc), 0, 1).reshape(g, n * (gr // g), cc)


def _local_step(x, tgt, mod, w, fetch=None, done=None, later=None):
    depth = mod.shape[0]
    row = lambda v: v.reshape(1, -1)
    w = dict(w)
    w["ffn_w_up"], w["ffn_w_down"] = dict(enumerate(w["ffn_w_up"])), dict(enumerate(w["ffn_w_down"]))

    def arrive(stage, after):
        if fetch is not None:
            for k, v in fetch(stage, after).items():
                if isinstance(v, dict):
                    w[k].update(v)
                else:
                    w[k] = v

    saved = []
    for i in range(depth):
        if i == 1:
            arrive("gla", x)
        sh_m, sc_m, gt_m, sh_f, sc_f, gt_f = (mod[i, j:j + 1] for j in range(6))
        g0, g1, g2, g3 = (w["norm_g"][i, j:j + 1] for j in range(4))
        tag = f"_l{i}"
        if i == 0:
            h = _norm_mod_fwd(x, g0, sc_m, sh_m, "norm_mix" + tag)
        if i % 2 == 0:
            pj = _mm(h, w["rg_w_in"], w_slots=N_CHIP, name="rg_in" + tag)
            act, aux = _rg_mid_fwd(pj, w["rg_conv_w"], row(w["rg_conv_b"]), w["rg_wa"], row(w["rg_ba"]), w["rg_wx"],
                                   row(w["rg_bx"]), row(w["rg_lambda"]), "rg_mid" + tag)
            y = _mm(act, w["rg_w_out"], name="rg_out" + tag)
        else:
            pj = _mm(h, w["gla_w_in"], tm_max=512, name="gla_in" + tag)
            act, *aux = _gla_mid_fwd(pj, w["gla_w_alpha"], row(w["gla_b_alpha"]), row(w["gla_norm_g"]), "gla_mid" + tag)
            y = _mm(act, w["gla_w_out"], name="gla_out" + tag)
        x1, h2 = _post_norm_fwd(x, y, g1, gt_m, g2, sc_f, sh_f, "post_mix" + tag)
        arrive(f"ffn{i}", x1)
        p = _mm(h2, w["ffn_w_up"][i], w_slots=N_CHIP, name="ffn_up" + tag)
        a, ga, gb = _ffn_mid_fwd(p, w["ffn_conv_w"][i], w["ffn_conv_b"][i:i + 1], "ffn_mid" + tag)
        y2 = _mm(a, w["ffn_w_down"][i], name="ffn_down" + tag)
        saved_h = h
        if i + 1 < depth:
            nxt = [mod[i + 1, j:j + 1] for j in range(2)] + [w["norm_g"][i + 1, 0:1]]
            x2, h = _post_norm_fwd(x1, y2, g3, gt_f, nxt[2], nxt[1], nxt[0], "post_ffn" + tag)
        else:
            x2 = None
            cols, dx = _post_loss(x1, y2, g3, gt_f, tgt, "post_ffn_loss")
        saved.append((x, saved_h, pj, act, aux, y, x1, h2, p, (a, ga, gb), y2))
        x = x2

    stacked = ("norm_g", "ffn_conv_w", "ffn_conv_b", "mod")
    gr = {k: [None] * depth for k in stacked + ("ffn_w_up", "ffn_w_down")}
    told = lambda stage, after: done(stage, gr, after) if done is not None else 0.0
    told_later = lambda stage, after: later(stage, after) if later is not None else 0.0
    for i in reversed(range(depth)):
        x0, h, pj, act, aux, y, x1, h2, p, (a, ga, gb), y2 = saved[i]
        sh_m, sc_m, gt_m, sh_f, sc_f, gt_f = (mod[i, j:j + 1] for j in range(6))
        g0, g1, g2, g3 = (w["norm_g"][i, j:j + 1] for j in range(4))
        tag = f"_l{i}"
        if i == depth - 1:
            dy2, d_g3, d_gt_f = _post_bwd(dx, y2, g3, gt_f, "post_ffn_b" + tag)
        else:
            dy2, d_g3, d_gt_f = ahead
        da = _mm(dy2, w["ffn_w_down"][i], tb=True, name="ffn_down_dx" + tag)
        gr["ffn_w_down"][i] = _mm(a, dy2, ta=True, name="ffn_down_dw" + tag)
        conv_w = w["ffn_conv_w"][i] + (told_later("l1", da) if i == 0 else 0.0)
        dp, dcw, dcb = _ffn_mid_bwd(da, p, ga, gb, conv_w, "ffn_mid_b" + tag)
        gr["ffn_conv_w"][i], gr["ffn_conv_b"][i] = _cat(dcw[0], dcw[1]), _cat(dcb[0], dcb[1])[0]
        dh2 = _mm(dp, w["ffn_w_up"][i], tb=True, a_parts=2, w_slots=N_CHIP, name="ffn_up_dx" + tag)
        gr["ffn_w_up"][i] = _mm(h2, dp, ta=True, b_parts=2, out_slots=N_CHIP, name="ffn_up_dw" + tag)
        if i == 0:
            gt_m = gt_m + told("ffn0", dh2)
        dx1, dy, d_g2, d_sc_f, d_sh_f, d_g1, d_gt_m = _norm_post_bwd(dh2, x1, g2, sc_f, dx, y, g1, gt_m, "norm_ffn_b" + tag)
        if i % 2 == 0:
            dact = _mm(dy, w["rg_w_out"], tb=True, name="rg_out_dx" + tag)
            gr["rg_w_out"] = _mm(act, dy, ta=True, name="rg_out_dw" + tag)
            lam = row(w["rg_lambda"]) + told_later("ffn0", gr["rg_w_out"])
            dpj, gr["rg_conv_w"], d_cb, gr["rg_wa"], d_ba, gr["rg_wx"], d_bx, d_lam = _rg_mid_bwd(
                dact, pj, aux, w["rg_conv_w"], row(w["rg_conv_b"]), w["rg_wa"], row(w["rg_ba"]), w["rg_wx"],
                row(w["rg_bx"]), lam, "rg_mid_b" + tag)
            gr["rg_conv_b"], gr["rg_ba"], gr["rg_bx"], gr["rg_lambda"] = d_cb[0], d_ba[0], d_bx[0], d_lam[0]
            dh = _mm(dpj, w["rg_w_in"], tb=True, a_parts=2, w_slots=N_CHIP, name="rg_in_dx" + tag)
            gr["rg_w_in"] = _mm(h, dpj, ta=True, b_parts=2, out_slots=N_CHIP, name="rg_in_dw" + tag)
            sc_m = sc_m + told("rg", dh)
        else:
            dact = _mm(dy, w["gla_w_out"], tb=True, name="gla_out_dx" + tag)
            gr["gla_w_out"] = _mm(act, dy, ta=True, name="gla_out_dw" + tag)
            dpj, gr["gla_w_alpha"], d_bal, d_ng = _gla_mid_bwd(dact, pj, aux[0], aux[1], w["gla_w_alpha"], row(w["gla_b_alpha"]),
                                                               row(w["gla_norm_g"]), "gla_mid_b" + tag)
            gr["gla_b_alpha"], gr["gla_norm_g"] = d_bal[0], d_ng[0]
            dh = _mm(dpj, w["gla_w_in"], tb=True, name="gla_in_dx" + tag)
            gr["gla_w_in"] = _mm(h, dpj, ta=True, tm_max=512, name="gla_in_dw" + tag)
            mod = mod.at[0].add(told("l1", dh))
        if i > 0:
            dx, dy_below, d_g0, d_sc_m, d_sh_m, d_g_below, d_gt_below = _norm_post_bwd(
                dh, x0, g0, sc_m, dx1, saved[i - 1][-1], w["norm_g"][i - 1, 3:4], mod[i - 1, 5:6], "norm_mix_b" + tag)
            ahead = (dy_below, d_g_below, d_gt_below)
        else:
            dx, d_g0, d_sc_m, d_sh_m = _norm_mod_bwd(dh, x0, g0, sc_m, dx1, "norm_mix_b" + tag)
        gr["norm_g"][i] = jnp.concatenate([d_g0, d_g1, d_g2, d_g3], axis=0)
        gr["mod"][i] = jnp.concatenate([d_sh_m, d_sc_m, d_gt_m, d_sh_f, d_sc_f, d_gt_f], axis=0)
    for k in stacked:
        gr[k] = jnp.stack(gr[k])
    return cols, dx, gr


ADA_ROWS = 16


def _ada_fwd(c16, ada_w, ada_b, name):
    depth, d, n = ada_w.shape
    tn = _tile(n, (512, 256, 128))

    def body(c_ref, w_ref, b_ref, o_ref):
        cv = c_ref[...]
        o_ref[0] = _dot_nn(cv * _sigmoid(cv), w_ref[0]) + b_ref[0]

    return pl.pallas_call(
        body, grid=(depth, n // tn),
        in_specs=[pl.BlockSpec((ADA_ROWS, d), lambda l, j: (0, 0)), pl.BlockSpec((1, d, tn), lambda l, j: (l, 0, j)),
                  pl.BlockSpec((1, 1, tn), lambda l, j: (l, 0, j))],
        out_specs=pl.BlockSpec((1, ADA_ROWS, tn), lambda l, j: (l, 0, j)),
        out_shape=jax.ShapeDtypeStruct((depth, ADA_ROWS, n), F32),
        compiler_params=_params("parallel", "parallel"), name=name,
    )(c16, ada_w, ada_b)


def _ada_bwd(c16, dmod16, name):
    depth, _, n = dmod16.shape
    d = c16.shape[1]
    tn = _tile(n, (512, 256, 128))

    def body(c_ref, dm_ref, o_ref):
        cv = c_ref[...]
        o_ref[0] = _dot_tn(cv * _sigmoid(cv), dm_ref[0])

    return pl.pallas_call(
        body, grid=(depth, n // tn),
        in_specs=[pl.BlockSpec((ADA_ROWS, d), lambda l, j: (0, 0)), pl.BlockSpec((1, ADA_ROWS, tn), lambda l, j: (l, 0, j))],
        out_specs=pl.BlockSpec((1, d, tn), lambda l, j: (l, 0, j)),
        out_shape=jax.ShapeDtypeStruct((depth, d, n), F32),
        compiler_params=_params("parallel", "parallel"), name=name,
    )(c16, dmod16)


PACK_COLS = 1024
_ANY = pl.BlockSpec(memory_space=pl.ANY)
_VMEM = pl.BlockSpec(memory_space=pltpu.VMEM)


def _place():
    return lax.axis_index("x"), lax.axis_index("y"), lax.axis_index("c")


def _other_chips(x, y):
    return [(1 - x, y), (x, 1 - y), (1 - x, 1 - y)]


def _rcopy(src, dst, send_sems, recv_sems, k, peer):
    return pltpu.make_async_remote_copy(src_ref=src, dst_ref=dst, send_sem=send_sems.at[k], recv_sem=recv_sems.at[k],
                                        device_id=peer, device_id_type=MESH)


def _all_gather_8(v, name):
    r, cc = v.shape

    def body(v_ref, out_ref, send_sems, recv_sems, local_sem):
        x, y, c = _place()
        me = 4 * x + 2 * y + c
        mine = pltpu.make_async_copy(v_ref, out_ref.at[me], local_sem)
        mine.start()
        peers = []
        for k in range(1, N_DEV):
            px = 1 - x if k & 4 else x
            py = 1 - y if k & 2 else y
            pc = 1 - c if k & 1 else c
            peers.append((px, py, pc))
        sends = [_rcopy(v_ref, out_ref.at[me], send_sems, recv_sems, k, p) for k, p in enumerate(peers)]
        for cp in sends:
            cp.start()
        for k, (px, py, pc) in enumerate(peers):
            _rcopy(v_ref, out_ref.at[4 * px + 2 * py + pc], send_sems, recv_sems, k, (px, py, pc)).wait_recv()
        for cp in sends:
            cp.wait_send()
        mine.wait()

    return pl.pallas_call(
        body, in_specs=[_VMEM], out_specs=_VMEM, out_shape=jax.ShapeDtypeStruct((N_DEV, r, cc), v.dtype),
        scratch_shapes=[pltpu.SemaphoreType.DMA((N_DEV - 1,)), pltpu.SemaphoreType.DMA((N_DEV - 1,)), pltpu.SemaphoreType.DMA],
        compiler_params=pltpu.CompilerParams(vmem_limit_bytes=VMEM_LIMIT), name=name,
    )(v)


def _gather_chips(shards, name):
    n = len(shards)
    per = 2 * (N_CHIP - 1)

    def body(*refs):
        ins, outs, (send_sems, recv_sems) = refs[:n], refs[n:2 * n], refs[2 * n:]
        x, y, c = _place()
        chip = 2 * x + y
        chips = _other_chips(x, y)
        rows = [(pl.ds(c * (r.shape[0] // 2), r.shape[0] // 2), pl.ds((1 - c) * (r.shape[0] // 2), r.shape[0] // 2)) for r in ins]
        first = [_rcopy(ins[i].at[rows[i][0]], outs[i].at[chip, rows[i][0]], send_sems, recv_sems, per * i + j, (px, py, c))
                 for i in range(n) for j, (px, py) in enumerate(chips)]
        for cp in first:
            cp.start()
        passed = []
        for i in range(n):
            for j, (px, py) in enumerate(chips):
                landed = outs[i].at[2 * px + py, rows[i][0]]
                _rcopy(ins[i].at[rows[i][0]], landed, send_sems, recv_sems, per * i + j, (px, py, c)).wait_recv()
                fw = _rcopy(landed, landed, send_sems, recv_sems, per * i + N_CHIP - 1 + j, (x, y, 1 - c))
                fw.start()
                passed.append(fw)
        for i in range(n):
            for j, (px, py) in enumerate(chips):
                landed = outs[i].at[2 * px + py, rows[i][1]]
                _rcopy(landed, landed, send_sems, recv_sems, per * i + N_CHIP - 1 + j, (x, y, 1 - c)).wait_recv()
        for cp in first + passed:
            cp.wait_send()

    return pl.pallas_call(
        body, in_specs=[_ANY] * n, out_specs=[_ANY] * n,
        out_shape=[jax.ShapeDtypeStruct((N_CHIP,) + sh.shape, sh.dtype) for sh in shards],
        scratch_shapes=[pltpu.SemaphoreType.DMA((per * n,)), pltpu.SemaphoreType.DMA((per * n,))], name=name,
    )(*shards)


def _pair_exchange(gs, name):
    n = len(gs)

    def body(*refs):
        ins, outs, (send_sems, recv_sems) = refs[:n], refs[n:2 * n], refs[2 * n:]
        x, y, c = _place()
        copies = []
        for i in range(n):
            half = ins[i].shape[1] // 2
            copies.append(_rcopy(ins[i].at[:, pl.ds((1 - c) * half, half)], outs[i], send_sems, recv_sems, i, (x, y, 1 - c)))
        for cp in copies:
            cp.start()
        for cp in copies:
            cp.wait()

    return pl.pallas_call(
        body, in_specs=[_ANY] * n, out_specs=[_ANY] * n,
        out_shape=[jax.ShapeDtypeStruct((g.shape[0], g.shape[1] // 2, g.shape[2]), g.dtype) for g in gs],
        scratch_shapes=[pltpu.SemaphoreType.DMA((n,)), pltpu.SemaphoreType.DMA((n,))], name=name,
    )(*gs)


_ROW_TILES = (640, 512, 352, 256, 128, 64, 32, 16)


def _pair_sum(g, other, c_idx, name):
    n, half, cc = other.shape
    tr = _tile(half, _ROW_TILES)

    def body(c_ref, g_ref, o_ref, out_ref):
        out_ref[...] = (g_ref[...] + o_ref[...]).astype(out_ref.dtype)

    return pl.pallas_call(
        body,
        grid_spec=pltpu.PrefetchScalarGridSpec(
            num_scalar_prefetch=1, grid=(n, half // tr),
            in_specs=[pl.BlockSpec((None, None, tr, cc), lambda k, i, c_ref: (k, c_ref[0], i, 0)),
                      pl.BlockSpec((None, tr, cc), lambda k, i, c_ref: (k, i, 0))],
            out_specs=pl.BlockSpec((None, tr, cc), lambda k, i, c_ref: (k, i, 0))),
        out_shape=jax.ShapeDtypeStruct((n, half, cc), BF16),
        compiler_params=_params("parallel", "parallel"), name=name,
    )(c_idx, g.reshape(n, 2, half, cc), other)


def _chip_exchange(ps, name):
    n = len(ps)
    per = N_CHIP - 1

    def body(*refs):
        ins, outs, (send_sems, recv_sems) = refs[:n], refs[n:2 * n], refs[2 * n:]
        x, y, c = _place()
        chip = 2 * x + y
        chips = _other_chips(x, y)
        sends = [_rcopy(ins[i].at[2 * px + py], outs[i].at[chip], send_sems, recv_sems, per * i + j, (px, py, c))
                 for i in range(n) for j, (px, py) in enumerate(chips)]
        for cp in sends:
            cp.start()
        for i in range(n):
            for j, (px, py) in enumerate(chips):
                _rcopy(ins[i].at[chip], outs[i].at[2 * px + py], send_sems, recv_sems, per * i + j, (px, py, c)).wait_recv()
        for cp in sends:
            cp.wait_send()

    return pl.pallas_call(
        body, in_specs=[_ANY] * n, out_specs=[_ANY] * n, out_shape=[jax.ShapeDtypeStruct(p.shape, p.dtype) for p in ps],
        scratch_shapes=[pltpu.SemaphoreType.DMA((per * n,)), pltpu.SemaphoreType.DMA((per * n,))], name=name,
    )(*ps)


_HBM = pl.BlockSpec(memory_space=pltpu.HBM)
_SEM = pl.BlockSpec(memory_space=pltpu.SEMAPHORE)
_DATAFLOW = pltpu.SideEffectType.DATAFLOW_SIDE_EFFECTING


def _split_copies(srcs, lands, send_sems, recv_sems, mode, arriving):
    x, y, c = _place()
    chip = 2 * x + y
    out = []
    for i, (src, land) in enumerate(zip(srcs, lands)):
        if mode == "all":
            for k in range(1, N_DEV):
                px, py, pc = (1 - x if k & 4 else x), (1 - y if k & 2 else y), (1 - c if k & 1 else c)
                slot = 4 * px + 2 * py + pc if arriving else 2 * chip + c
                out.append(_rcopy(src, land.at[slot], send_sems, recv_sems, (N_DEV - 1) * i + k - 1, (px, py, pc)))
            continue
        if mode == "pair":
            half = src.shape[1] // 2
            out.append(_rcopy(src.at[:, pl.ds((1 - c) * half, half)], land, send_sems, recv_sems, i, (x, y, 1 - c)))
            continue
        if mode == "share":
            out.append(_rcopy(src, land.at[1 - c if arriving else c], send_sems, recv_sems, i, (x, y, 1 - c)))
            continue
        for j, (px, py) in enumerate(_other_chips(x, y)):
            there = 2 * px + py
            part = src.at[there] if mode == "slots" else src
            out.append(_rcopy(part, land.at[there if arriving else chip], send_sems, recv_sems, (N_CHIP - 1) * i + j, (px, py, c)))
    return out


def _land_shape(src, mode):
    if mode == "pair":
        return (src.shape[0], src.shape[1] // 2, src.shape[2])
    if mode == "all":
        return (N_DEV,) + src.shape
    if mode == "share":
        return (2,) + src.shape
    return (N_CHIP,) + (src.shape[1:] if mode == "slots" else src.shape)


def _send_start(srcs, mode, name):
    n = len(srcs)
    n_sem = {"pair": 1, "share": 1, "all": N_DEV - 1}.get(mode, N_CHIP - 1) * n
    lands = [lax.empty(_land_shape(s, mode), s.dtype) for s in srcs]

    def body(*refs):
        ins, zones, (send_sems, recv_sems) = refs[:n], refs[n:2 * n], refs[2 * n:2 * n + 2]
        for cp in _split_copies(ins, zones, send_sems, recv_sems, mode, False):
            cp.start()
        refs[-1][...] = jnp.zeros_like(refs[-1])

    hbm = lambda a: pltpu.HBM(a.shape, a.dtype)
    outs = pl.pallas_call(
        body, name=name, in_specs=[_HBM] * (2 * n),
        out_shape=(pltpu.SemaphoreType.DMA((n_sem,)), pltpu.SemaphoreType.DMA((n_sem,)), *[hbm(a) for a in srcs],
                   *[hbm(a) for a in lands], jax.ShapeDtypeStruct((SUBLANES, LANES), F32)),
        out_specs=(_SEM, _SEM, *[_HBM] * (2 * n), _VMEM), input_output_aliases={i: 2 + i for i in range(2 * n)},
        compiler_params=pltpu.CompilerParams(has_side_effects=_DATAFLOW),
    )(*[pltpu.with_memory_space_constraint(a, pltpu.HBM) for a in list(srcs) + lands])
    return (outs[0], outs[1], list(outs[2:2 + n]), list(outs[2 + n:2 + 2 * n])), outs[-1]


def _send_wait(state, after, mode, name):
    send_sems, recv_sems, srcs, lands = state
    n = len(srcs)

    def body(*refs):
        ins, zones, (send_s, recv_s) = refs[:n], refs[n:2 * n], refs[2 * n:2 * n + 2]
        for cp in _split_copies(ins, zones, send_s, recv_s, mode, True):
            cp.wait_send()
            cp.wait_recv()

    hbm = lambda a: pltpu.HBM(a.shape, a.dtype)
    outs = pl.pallas_call(
        body, name=name, in_specs=[_HBM] * (2 * n) + [_SEM, _SEM, _ANY],
        out_shape=tuple(hbm(a) for a in srcs + lands), out_specs=tuple([_HBM] * (2 * n)),
        input_output_aliases={i: i for i in range(2 * n)},
        compiler_params=pltpu.CompilerParams(has_side_effects=_DATAFLOW),
    )(*srcs, *lands, send_sems, recv_sems, after)
    return list(outs[:n]), list(outs[n:])


def _sum_lead(v, name):
    n, r, cc = v.shape
    tr = _tile(r, _ROW_TILES + (8,))

    def body(v_ref, o_ref):
        acc = v_ref[0].astype(F32)
        for k in range(1, n):
            acc = acc + v_ref[k].astype(F32)
        o_ref[...] = acc

    return pl.pallas_call(
        body, grid=(r // tr,), in_specs=[pl.BlockSpec((n, tr, cc), lambda i: (0, i, 0))],
        out_specs=pl.BlockSpec((tr, cc), lambda i: (i, 0)), out_shape=jax.ShapeDtypeStruct((r, cc), F32),
        compiler_params=_params("parallel"), name=name,
    )(v)


def _chip_sum(arrived, mine, chip_idx, name):
    n, r, cc = arrived.shape
    tr = _tile(r, _ROW_TILES)

    def body(chip_ref, a_ref, m_ref, o_ref):
        acc = jnp.zeros((tr, cc), F32)
        for k in range(n):
            acc = acc + jnp.where(chip_ref[0] == k, m_ref[...], a_ref[k]).astype(F32)
        o_ref[...] = acc

    return pl.pallas_call(
        body,
        grid_spec=pltpu.PrefetchScalarGridSpec(
            num_scalar_prefetch=1, grid=(r // tr,),
            in_specs=[pl.BlockSpec((n, tr, cc), lambda i, chip_ref: (0, i, 0)),
                      pl.BlockSpec((None, tr, cc), lambda i, chip_ref: (chip_ref[0], i, 0))],
            out_specs=pl.BlockSpec((tr, cc), lambda i, chip_ref: (i, 0))),
        out_shape=jax.ShapeDtypeStruct((r, cc), F32), compiler_params=_params("parallel"), name=name,
    )(chip_idx, arrived, mine)


def _pair_share(reds, name):
    n = len(reds)

    def body(*refs):
        ins, outs, (send_sems, recv_sems) = refs[:n], refs[n:2 * n], refs[2 * n:]
        x, y, c = _place()
        copies = [_rcopy(ins[i], outs[i].at[c], send_sems, recv_sems, i, (x, y, 1 - c)) for i in range(n)]
        for cp in copies:
            cp.start()
        for i in range(n):
            _rcopy(ins[i], outs[i].at[1 - c], send_sems, recv_sems, i, (x, y, 1 - c)).wait_recv()
        for cp in copies:
            cp.wait_send()

    return pl.pallas_call(
        body, in_specs=[_ANY] * n, out_specs=[_ANY] * n, out_shape=[jax.ShapeDtypeStruct((2,) + r.shape, r.dtype) for r in reds],
        scratch_shapes=[pltpu.SemaphoreType.DMA((n,)), pltpu.SemaphoreType.DMA((n,))], name=name,
    )(*reds)


def _pack(arrs, rows_multiple, dtype):
    flat = jnp.concatenate([a.reshape(-1).astype(dtype) for a in arrs])
    unit = rows_multiple * PACK_COLS
    total = -(-flat.shape[0] // unit) * unit
    return jnp.pad(flat, (0, total - flat.shape[0])).reshape(-1, PACK_COLS)


def _unpack(buf, shapes):
    lead = buf.shape[:-2]
    flat = buf.reshape(*lead, -1)
    out, off = [], 0
    for shp in shapes:
        n = 1
        for s in shp:
            n *= s
        out.append(flat[..., off:off + n].reshape(*lead, *shp))
        off += n
    return out


def _join_shards(parts, axis):
    moved = jnp.moveaxis(parts, 0, axis)
    shp = list(moved.shape)
    shp[axis:axis + 2] = [shp[axis] * shp[axis + 1]]
    return moved.reshape(shp)


def _my_shard(full, axis, chip):
    n = full.shape[axis] // N_CHIP
    return lax.dynamic_slice_in_dim(full, chip * n, n, axis)


SMALL = {"norm_g": 2, "ffn_conv_w": 2, "rg_conv_w": 2, "gla_w_alpha": 2, "gla_b_alpha": 1, "gla_norm_g": 1,
         "ada_b": None, "ffn_conv_b": None, "rg_conv_b": None, "rg_ba": None, "rg_bx": None, "rg_lambda": None}
BIG = {"rg_w_in": True, "rg_wa": False, "rg_wx": False, "rg_w_out": False, "ffn_w_up": True, "ffn_w_down": False,
       "gla_w_in": True, "gla_w_out": False}
WEIGHTS = ["ada_w", "ada_b", "norm_g", "ffn_w_up", "ffn_conv_w", "ffn_conv_b", "ffn_w_down", "rg_w_in", "rg_conv_w", "rg_conv_b",
           "rg_wa", "rg_ba", "rg_wx", "rg_bx", "rg_lambda", "rg_w_out", "gla_w_in", "gla_w_alpha", "gla_b_alpha", "gla_norm_g",
           "gla_w_out"]


def kernel(x, c, ada_w, ada_b, norm_g, ffn_w_up, ffn_conv_w, ffn_conv_b, ffn_w_down, rg_w_in, rg_conv_w, rg_conv_b, rg_wa, rg_ba, rg_wx, rg_bx, rg_lambda, rg_w_out, gla_w_in, gla_w_alpha, gla_b_alpha, gla_norm_g, gla_w_out, loss_target, m_ada_w, m_ada_b, m_norm_g, m_ffn_w_up, m_ffn_conv_w, m_ffn_conv_b, m_ffn_w_down, m_rg_w_in, m_rg_conv_w, m_rg_conv_b, m_rg_wa, m_rg_ba, m_rg_wx, m_rg_bx, m_rg_lambda, m_rg_w_out, m_gla_w_in, m_gla_w_alpha, m_gla_b_alpha, m_gla_norm_g, m_gla_w_out, v_ada_w, v_ada_b, v_norm_g, v_ffn_w_up, v_ffn_conv_w, v_ffn_conv_b, v_ffn_w_down, v_rg_w_in, v_rg_conv_w, v_rg_conv_b, v_rg_wa, v_rg_ba, v_rg_wx, v_rg_bx, v_rg_lambda, v_rg_w_out, v_gla_w_in, v_gla_w_alpha, v_gla_b_alpha, v_gla_norm_g, v_gla_w_out):
    wts = dict(ada_w=ada_w, ada_b=ada_b, norm_g=norm_g, ffn_w_up=ffn_w_up, ffn_conv_w=ffn_conv_w, ffn_conv_b=ffn_conv_b,
               ffn_w_down=ffn_w_down, rg_w_in=rg_w_in, rg_conv_w=rg_conv_w, rg_conv_b=rg_conv_b, rg_wa=rg_wa, rg_ba=rg_ba,
               rg_wx=rg_wx, rg_bx=rg_bx, rg_lambda=rg_lambda, rg_w_out=rg_w_out, gla_w_in=gla_w_in, gla_w_alpha=gla_w_alpha,
               gla_b_alpha=gla_b_alpha, gla_norm_g=gla_norm_g, gla_w_out=gla_w_out)
    mom1 = dict(ada_w=m_ada_w, ada_b=m_ada_b, norm_g=m_norm_g, ffn_w_up=m_ffn_w_up, ffn_conv_w=m_ffn_conv_w,
                ffn_conv_b=m_ffn_conv_b, ffn_w_down=m_ffn_w_down, rg_w_in=m_rg_w_in, rg_conv_w=m_rg_conv_w,
                rg_conv_b=m_rg_conv_b, rg_wa=m_rg_wa, rg_ba=m_rg_ba, rg_wx=m_rg_wx, rg_bx=m_rg_bx, rg_lambda=m_rg_lambda,
                rg_w_out=m_rg_w_out, gla_w_in=m_gla_w_in, gla_w_alpha=m_gla_w_alpha, gla_b_alpha=m_gla_b_alpha,
                gla_norm_g=m_gla_norm_g, gla_w_out=m_gla_w_out)
    mom2 = dict(ada_w=v_ada_w, ada_b=v_ada_b, norm_g=v_norm_g, ffn_w_up=v_ffn_w_up, ffn_conv_w=v_ffn_conv_w,
                ffn_conv_b=v_ffn_conv_b, ffn_w_down=v_ffn_w_down, rg_w_in=v_rg_w_in, rg_conv_w=v_rg_conv_w,
                rg_conv_b=v_rg_conv_b, rg_wa=v_rg_wa, rg_ba=v_rg_ba, rg_wx=v_rg_wx, rg_bx=v_rg_bx, rg_lambda=v_rg_lambda,
                rg_w_out=v_rg_w_out, gla_w_in=v_gla_w_in, gla_w_alpha=v_gla_w_alpha, gla_b_alpha=v_gla_b_alpha,
                gla_norm_g=v_gla_norm_g, gla_w_out=v_gla_w_out)
    xi, yi, ci = _place()
    chip, me = 2 * xi + yi, 4 * xi + 2 * yi + ci
    d = x.shape[-1]
    depth = ada_w.shape[0]
    n_ada = ada_w.shape[-1]
    sharded_small = [k for k, ax in SMALL.items() if ax is not None]

    sm = _all_gather_8(_pack([c] + [wts[k] for k in sharded_small], SUBLANES, F32), "gather_small")
    c_all = sm[:, 0, :]
    parts = _unpack(sm[0::2], [c.shape] + [wts[k].shape for k in sharded_small])[1:]
    full = {k: _join_shards(p, SMALL[k]) for k, p in zip(sharded_small, parts)}
    for k, ax in SMALL.items():
        if ax is None:
            full[k] = wts[k]

    c16 = jnp.pad(c_all, ((0, ADA_ROWS - N_DEV), (0, 0)))
    ada_b_mine = lax.dynamic_slice_in_dim(ada_b, chip * n_ada, n_ada, 1)[:, None, :]
    mod_cols = _ada_fwd(c16, ada_w, ada_b_mine, "ada_fwd")
    mod_all = _all_gather_8(mod_cols.reshape(-1, PACK_COLS), "gather_mod")[0::2].reshape(N_CHIP, depth, ADA_ROWS, n_ada)
    mod = jnp.swapaxes(lax.dynamic_index_in_dim(mod_all, me, 2, keepdims=False), 0, 1).reshape(depth, 6, d)

    items = [(k, l) for k in BIG for l in range(wts[k].shape[0])]
    stage_of = lambda k, l: "rg" if k.startswith("rg_") else ("ffn0" if (k.startswith("ffn_") and l == 0) else "l1")
    staged = {st: [it for it in items if stage_of(*it) == st] for st in ("rg", "ffn0", "l1")}
    staged["gla"] = [it for it in staged["l1"] if it[0].startswith("gla_")]
    staged["ffn1"] = [it for it in staged["l1"] if it[0].startswith("ffn_")]
    staged["l1"] = staged["gla"] + staged["ffn1"]
    shard = lambda k, l: wts[k][l].reshape(-1, wts[k].shape[-1]).astype(BF16)
    own = lambda got, mine: [lax.dynamic_update_index_in_dim(g, m, chip, 0) for g, m in zip(got, mine)]
    rows_joined = lambda v: v.reshape(-1, v.shape[-1])

    def placed(its, slots):
        out = {"ffn_w_up": {}, "ffn_w_down": {}}
        for (k, l), v in zip(its, slots):
            if k == "ffn_w_up":
                out[k][l] = v
            elif k == "ffn_w_down":
                out[k][l] = rows_joined(v)
            elif k in ("rg_wa", "rg_wx"):
                out[k] = _slots_to_block_rows(v, RG_BLOCKS)
            elif k == "gla_w_in":
                out[k] = _from_col_slots(v, "gla_w_in_join")
            else:
                out[k] = v if BIG[k] else rows_joined(v)
        return out

    after_mod = (mod[0, 0, 0] * 0.0).astype(BF16)
    sh_rg = [shard(k, l) + after_mod for k, l in staged["rg"]]
    local = {k: (v if k in ("norm_g", "ffn_conv_w", "ffn_conv_b") else v[0]) for k, v in full.items()}
    local.update(placed(staged["rg"], own(_gather_chips(sh_rg, "gather_weights_rg"), sh_rg)))
    sh_late, flying = {}, {}
    after_rg = (local["rg_w_out"][0, 0].astype(F32) * 0.0).astype(BF16)
    sh_late["ffn0"] = [shard(k, l) + after_rg for k, l in staged["ffn0"]]
    flying["ffn0"], tok = _send_start(sh_late["ffn0"], "whole", "weights_ffn0_start")
    for stage in ("gla", "ffn1"):
        sh_late[stage] = [shard(k, l) + tok[0, 0].astype(BF16) for k, l in staged[stage]]
        flying[stage], tok = _send_start(sh_late[stage], "whole", f"weights_{stage}_start")
    mod = mod + tok[0, 0]

    def fetch(stage, after):
        mine, got = _send_wait(flying[stage], after, "whole", f"weights_{stage}_wait")
        return placed(staged[stage], own(got, mine))

    c_idx = ci.reshape(1).astype(jnp.int32)
    chip_idx = chip.reshape(1).astype(jnp.int32)
    gslots, paired, psums, sent, started, sharing = {}, {}, {}, {}, {}, {}
    before = {"ffn0": "l1", "rg": "ffn0"}

    def grad_slots(gr, k, l):
        g = gr[k][l] if k in ("ffn_w_up", "ffn_w_down") else gr[k]
        if k in ("rg_wa", "rg_wx"):
            return _block_rows_to_slots(g)
        if k == "gla_w_in":
            return _col_slots(g, "gla_w_in_grad_slots")
        return g if BIG[k] else g.reshape(N_CHIP, -1, g.shape[-1])

    def done(stage, gr, after):
        gslots[stage] = [grad_slots(gr, k, l) for k, l in staged[stage]]
        paired[stage], token = _send_start(gslots[stage], "pair", f"grads_{stage}_pair_start")
        if stage not in before:
            return token[0, 0]
        prev = before[stage]
        mine, arrived = _send_wait(sent[prev], after, "slots", f"grads_{prev}_wait")
        halves = [_chip_sum(a, m, chip_idx, "grads_chip_sum_%s%d" % it) for it, a, m in zip(staged[prev], arrived, mine)]
        sharing[prev], shared = _send_start(halves, "share", f"grads_{prev}_share_start")
        return token[0, 0] + shared[0, 0]

    def later(stage, after):
        mine, theirs = _send_wait(paired[stage], after, "pair", f"grads_{stage}_pair_wait")
        psums[stage] = [_pair_sum(g, t, c_idx, f"grads_pair_sum_{k}{l}") for (k, l), g, t in zip(staged[stage], mine, theirs)]
        sent[stage], started[stage] = _send_start(psums[stage], "slots", f"grads_{stage}_start")
        return started[stage][0, 0]

    cols, grad_x, gr = _local_step(x[0], loss_target[0], mod, local, fetch, done, later)
    loss_mine = (0.5 * jnp.sum(cols) / d).reshape(1)

    small_names = [k for k in SMALL if k != "ada_b"]
    small_flying, small_sent = _send_start([_pack([gr[k] for k in small_names] + [gr["mod"], loss_mine], SUBLANES, F32)], "all",
                                           "grads_small_start")
    small_shapes = [full[k].shape for k in small_names] + [(depth, 6 * d), (1,)]
    delta, new_m, new_v = {}, {}, {}
    grads = {}

    def update_all(reduced, dep):
        last = None
        for k in BIG:
            gs_k = [g.reshape(-1, g.shape[-1]) for (k2, _), g in sorted(reduced.items()) if k2 == k]
            if gs_k:
                last = update(k, gs_k, dep)
        return last

    def update(k, gs_k, dep=None):
        shp = wts[k].shape
        if k == "gla_w_in":
            view, back = (lambda a: jnp.swapaxes(a, 1, 2)), (lambda o: jnp.swapaxes(o, 1, 2))
            gs_k = [g.T for g in gs_k]
        else:
            view, back = (lambda a: a.reshape(a.shape[0], -1, a.shape[-1])), (lambda o: o.reshape(shp))
        outs = _adamw(view(wts[k]), gs_k, view(mom1[k]), view(mom2[k]), "adamw_" + k, dep)
        grads[k], delta[k], new_m[k], new_v[k] = (back(o) for o in outs)
        return new_v[k]

    later("rg", small_sent)
    reduced = {}
    for st in ("l1", "ffn0"):
        halves, lands = _send_wait(sharing[st], grad_x, "share", f"grads_{st}_share_wait")
        reduced.update({it: lax.dynamic_update_index_in_dim(l, h, ci, 0) for it, h, l in zip(staged[st], halves, lands)})
    done_late = update_all(reduced, started["rg"])
    (small_mine,), (gs,) = _send_wait(small_flying, done_late, "all", "grads_small_wait")
    gs = lax.dynamic_update_index_in_dim(gs, small_mine, 2 * chip + ci, 0)
    *small_sum, g_ada_b, loss = _unpack(_sum_lead(gs, "sum_small_grads"), small_shapes)
    loss = loss[0]
    grads.update(zip(small_names, small_sum))
    grads["ada_b"] = g_ada_b
    for k in sharded_small:
        grads[k] = _my_shard(grads[k], SMALL[k], chip)
    dmod_all = _unpack(gs, small_shapes)[-2].reshape(N_DEV, depth, N_CHIP, n_ada)
    dmod_mine = jnp.swapaxes(lax.dynamic_index_in_dim(dmod_all, chip, 2, keepdims=False), 0, 1)
    g_ada_w = _ada_bwd(c16, jnp.pad(dmod_mine, ((0, 0), (0, ADA_ROWS - N_DEV), (0, 0))), "ada_bwd")
    update("ada_w", g_ada_w)
    small_shard_shapes = [wts[k].shape for k in SMALL]
    packed = [_pack([src[k] for k in SMALL], SUBLANES, F32) for src in (wts, grads, mom1, mom2)]
    outs = _adamw(packed[0][None], [packed[1]], packed[2][None], packed[3][None], "adamw_small")
    for dst, o in zip((delta, new_m, new_v), outs[1:]):
        for k, a in zip(SMALL, _unpack(o[0], small_shard_shapes)):
            dst[k] = a
    mine, arrived = _send_wait(sent["rg"], outs[3], "slots", "grads_rg_wait")
    halves = [_chip_sum(a, m, chip_idx, "grads_chip_sum_%s%d" % it) for it, a, m in zip(staged["rg"], arrived, mine)]
    shared = _pair_share(halves, "grads_pair_share_rg")
    update_all({it: lax.dynamic_update_index_in_dim(s2, h, ci, 0) for it, s2, h in zip(staged["rg"], shared, halves)}, None)

    return (loss, grad_x[None], *[grads[k] for k in WEIGHTS], *[delta[k] for k in WEIGHTS], *[new_m[k] for k in WEIGHTS],
            *[new_v[k] for k in WEIGHTS])
```

```python
import jax
import jax.numpy as jnp
from jax import lax
from jax.experimental import pallas as pl
from jax.experimental.pallas import tpu as pltpu

F32 = jnp.float32
BF16 = jnp.bfloat16
MXU_DTYPE = BF16

EPS = 1e-6
RG_C = 8.0
RG_BLOCKS = 4
RG_CONV = 4
GLA_HEADS = 4
GLA_TAU = 16.0
GLA_CHUNK = 64
GLA_RANK = 16
FFN_CONV = 3
ADAM_LR = 0.001
ADAM_B1 = 0.9
ADAM_B2 = 0.999
ADAM_EPS = 1e-08
ADAM_WD = 0.01
ADAM_STEP = 10

LANES = 128
SUBLANES = 8
VMEM_LIMIT = 56 * 1024 * 1024
CB = 256
MESH = pl.DeviceIdType.MESH
N_DEV = 8
N_CHIP = 4


def _params(*sem):
    return pltpu.CompilerParams(dimension_semantics=sem, vmem_limit_bytes=VMEM_LIMIT)


def _tile(dim, prefs):
    for p in prefs:
        if dim % p == 0:
            return p
    return dim


def _dot(a, b, dims):
    return lax.dot_general(a.astype(MXU_DTYPE), b.astype(MXU_DTYPE), (dims, ((), ())), preferred_element_type=F32)


def _dot_nn(a, b):
    return _dot(a, b, ((1,), (0,)))


def _dot_nt(a, b):
    return _dot(a, b, ((1,), (1,)))


def _dot_tn(a, b):
    return _dot(a, b, ((0,), (0,)))


def _mm(a, b, *, ta=False, tb=False, a_parts=1, b_parts=1, w_slots=1, out_slots=1, out_dtype=F32, tm_max=1408, name):
    if ta:
        k_dim, m_dim = a.shape
        n_dim = b.shape[-1] * b_parts
    else:
        m_dim, k_dim = a.shape[-2], a.shape[-1] * a_parts
        n_dim = b.shape[-2] if tb else b.shape[-1] * w_slots
    n_unit = n_dim // max(b_parts, out_slots, 1 if tb else w_slots)
    k_unit = k_dim // max(a_parts, w_slots if tb else 1)
    tm = _tile(m_dim, tuple(t for t in (1024, 1408, 512, 256, 128) if t <= max(tm_max, 128)))
    tn = _tile(n_unit, (1024, 1408, 896, 512, 256, 128))
    tk = _tile(k_unit, (1024, 1408, 896, 512, 256, 128))
    nk = k_dim // tk
    dims = ((0 if ta else 1,), (1 if tb else 0,))

    def spec(shape, parts, total, tile, col_grid, row_grid):
        per = total // parts // tile

        def index(i, j, k):
            g = {"i": i, "j": j, "k": k}
            col, row = g[col_grid], g[row_grid]
            return (row, col) if parts == 1 else (col // per, row, col % per)

        return pl.BlockSpec(shape if parts == 1 else (None,) + shape, index)

    def body(a_ref, b_ref, o_ref, *acc):
        if nk == 1:
            o_ref[...] = _dot(a_ref[...], b_ref[...], dims).astype(o_ref.dtype)
            return
        acc_ref, k = acc[0], pl.program_id(2)

        @pl.when(k == 0)
        def _():
            acc_ref[...] = jnp.zeros_like(acc_ref)

        acc_ref[...] += _dot(a_ref[...], b_ref[...], dims)

        @pl.when(k == nk - 1)
        def _():
            o_ref[...] = acc_ref[...].astype(o_ref.dtype)

    if ta:
        a_spec = spec((tk, tm), 1, m_dim, tm, "i", "k")
        b_spec = spec((tk, tn), b_parts, n_dim, tn, "j", "k")
    elif tb:
        a_spec = spec((tm, tk), a_parts, k_dim, tk, "k", "i")
        b_spec = spec((tn, tk), w_slots, k_dim, tk, "k", "j")
    else:
        a_spec = spec((tm, tk), a_parts, k_dim, tk, "k", "i")
        b_spec = spec((tk, tn), w_slots, n_dim, tn, "j", "k")
    out_shape = (m_dim, n_dim) if out_slots == 1 else (out_slots, m_dim, n_dim // out_slots)
    return pl.pallas_call(
        body,
        grid=(m_dim // tm, n_dim // tn, nk),
        in_specs=[a_spec, b_spec],
        out_specs=spec((tm, tn), out_slots, n_dim, tn, "j", "i"),
        out_shape=jax.ShapeDtypeStruct(out_shape, out_dtype),
        scratch_shapes=[pltpu.VMEM((tm, tn), F32)] if nk > 1 else [],
        compiler_params=_params("parallel", "parallel", "arbitrary"),
        name=name,
    )(a, b)


ROW_TILES = (1024, 512)


def _row_specs(s, d, ts):
    return pl.BlockSpec((ts, d), lambda i: (i, 0)), pl.BlockSpec((1, d), lambda i: (0, 0))


def _norm_mod_fwd(x, g, sc, sh, name):
    s, d = x.shape
    ts = _tile(s, ROW_TILES)

    def body(x_ref, g_ref, sc_ref, sh_ref, h_ref):
        xv = x_ref[...]
        r = lax.rsqrt(jnp.mean(xv * xv, axis=-1, keepdims=True) + EPS)
        h_ref[...] = (((xv * r) * g_ref[...]) * (1.0 + sc_ref[...]) + sh_ref[...]).astype(h_ref.dtype)

    row, vec = _row_specs(s, d, ts)
    return pl.pallas_call(
        body, grid=(s // ts,), in_specs=[row, vec, vec, vec], out_specs=row,
        out_shape=jax.ShapeDtypeStruct((s, d), MXU_DTYPE), compiler_params=_params("parallel"), name=name,
    )(x, g, sc, sh)


def _norm_mod_bwd(dh, x, g, sc, dres, name):
    s, d = x.shape
    ts = _tile(s, ROW_TILES)

    def body(dh_ref, x_ref, g_ref, sc_ref, dres_ref, dx_ref, dg_ref, dsc_ref, dsh_ref, acc_ref):
        i = pl.program_id(0)

        @pl.when(i == 0)
        def _():
            acc_ref[...] = jnp.zeros_like(acc_ref)

        xv, dhv = x_ref[...], dh_ref[...]
        r = lax.rsqrt(jnp.mean(xv * xv, axis=-1, keepdims=True) + EPS)
        n = xv * r
        acc_ref[0:1, :] += jnp.sum(dhv * n, axis=0, keepdims=True)
        acc_ref[1:2, :] += jnp.sum(dhv, axis=0, keepdims=True)
        dn = dhv * ((1.0 + sc_ref[...]) * g_ref[...])
        dx_ref[...] = dres_ref[...] + r * (dn - n * jnp.mean(dn * n, axis=-1, keepdims=True))
        dg_ref[...] = (1.0 + sc_ref[...]) * acc_ref[0:1, :]
        dsc_ref[...] = g_ref[...] * acc_ref[0:1, :]
        dsh_ref[...] = acc_ref[1:2, :]

    row, vec = _row_specs(s, d, ts)
    vshape = jax.ShapeDtypeStruct((1, d), F32)
    return pl.pallas_call(
        body, grid=(s // ts,), in_specs=[row, row, vec, vec, row], out_specs=[row, vec, vec, vec],
        out_shape=[jax.ShapeDtypeStruct((s, d), F32), vshape, vshape, vshape],
        scratch_shapes=[pltpu.VMEM((SUBLANES, d), F32)], compiler_params=_params("arbitrary"), name=name,
    )(dh, x, g, sc, dres)


def _post_norm_fwd(x, y, g, gt, g2, sc, sh, name):
    s, d = x.shape
    ts = _tile(s, ROW_TILES)

    def body(x_ref, y_ref, g_ref, gt_ref, g2_ref, sc_ref, sh_ref, o_ref, h_ref):
        yv = y_ref[...]
        r = lax.rsqrt(jnp.mean(yv * yv, axis=-1, keepdims=True) + EPS)
        xn = x_ref[...] + gt_ref[...] * ((yv * r) * g_ref[...])
        o_ref[...] = xn
        r2 = lax.rsqrt(jnp.mean(xn * xn, axis=-1, keepdims=True) + EPS)
        h_ref[...] = (((xn * r2) * g2_ref[...]) * (1.0 + sc_ref[...]) + sh_ref[...]).astype(h_ref.dtype)

    row, vec = _row_specs(s, d, ts)
    return pl.pallas_call(
        body, grid=(s // ts,), in_specs=[row, row] + [vec] * 5, out_specs=[row, row],
        out_shape=[jax.ShapeDtypeStruct((s, d), F32), jax.ShapeDtypeStruct((s, d), MXU_DTYPE)],
        compiler_params=_params("parallel"), name=name,
    )(x, y, g, gt, g2, sc, sh)


def _post_bwd(dxn, y, g, gt, name):
    s, d = y.shape
    ts = _tile(s, ROW_TILES)

    def body(dxn_ref, y_ref, g_ref, gt_ref, dy_ref, dg_ref, dgt_ref, acc_ref):
        i = pl.program_id(0)

        @pl.when(i == 0)
        def _():
            acc_ref[...] = jnp.zeros_like(acc_ref)

        yv, dv = y_ref[...], dxn_ref[...]
        r = lax.rsqrt(jnp.mean(yv * yv, axis=-1, keepdims=True) + EPS)
        n = yv * r
        acc_ref[0:1, :] += jnp.sum(dv * n, axis=0, keepdims=True)
        dn = dv * (gt_ref[...] * g_ref[...])
        dy_ref[...] = (r * (dn - n * jnp.mean(dn * n, axis=-1, keepdims=True))).astype(dy_ref.dtype)
        dg_ref[...] = gt_ref[...] * acc_ref[0:1, :]
        dgt_ref[...] = g_ref[...] * acc_ref[0:1, :]

    row, vec = _row_specs(s, d, ts)
    vshape = jax.ShapeDtypeStruct((1, d), F32)
    return pl.pallas_call(
        body, grid=(s // ts,), in_specs=[row, row, vec, vec], out_specs=[row, vec, vec],
        out_shape=[jax.ShapeDtypeStruct((s, d), MXU_DTYPE), vshape, vshape],
        scratch_shapes=[pltpu.VMEM((SUBLANES, d), F32)], compiler_params=_params("arbitrary"), name=name,
    )(dxn, y, g, gt)


def _norm_post_bwd(dh, x, g, sc, dres, y, gp, gt, name):
    s, d = x.shape
    ts = _tile(s, (512,))

    def body(dh_ref, x_ref, g_ref, sc_ref, dres_ref, y_ref, gp_ref, gt_ref,
             dx_ref, dy_ref, dg_ref, dsc_ref, dsh_ref, dgp_ref, dgt_ref, acc_ref):
        i = pl.program_id(0)

        @pl.when(i == 0)
        def _():
            acc_ref[...] = jnp.zeros_like(acc_ref)

        xv, dhv = x_ref[...], dh_ref[...]
        r = lax.rsqrt(jnp.mean(xv * xv, axis=-1, keepdims=True) + EPS)
        n = xv * r
        acc_ref[0:1, :] += jnp.sum(dhv * n, axis=0, keepdims=True)
        acc_ref[1:2, :] += jnp.sum(dhv, axis=0, keepdims=True)
        dn = dhv * ((1.0 + sc_ref[...]) * g_ref[...])
        dx = dres_ref[...] + r * (dn - n * jnp.mean(dn * n, axis=-1, keepdims=True))
        dx_ref[...] = dx
        yv = y_ref[...]
        ry = lax.rsqrt(jnp.mean(yv * yv, axis=-1, keepdims=True) + EPS)
        ny = yv * ry
        acc_ref[2:3, :] += jnp.sum(dx * ny, axis=0, keepdims=True)
        dny = dx * (gt_ref[...] * gp_ref[...])
        dy_ref[...] = (ry * (dny - ny * jnp.mean(dny * ny, axis=-1, keepdims=True))).astype(dy_ref.dtype)
        dg_ref[...] = (1.0 + sc_ref[...]) * acc_ref[0:1, :]
        dsc_ref[...] = g_ref[...] * acc_ref[0:1, :]
        dsh_ref[...] = acc_ref[1:2, :]
        dgp_ref[...] = gt_ref[...] * acc_ref[2:3, :]
        dgt_ref[...] = gp_ref[...] * acc_ref[2:3, :]

    row, vec = _row_specs(s, d, ts)
    vshape = jax.ShapeDtypeStruct((1, d), F32)
    return pl.pallas_call(
        body, grid=(s // ts,), in_specs=[row, row, vec, vec, row, row, vec, vec], out_specs=[row, row] + [vec] * 5,
        out_shape=[jax.ShapeDtypeStruct((s, d), F32), jax.ShapeDtypeStruct((s, d), MXU_DTYPE)] + [vshape] * 5,
        scratch_shapes=[pltpu.VMEM((SUBLANES, d), F32)], compiler_params=_params("arbitrary"), name=name,
    )(dh, x, g, sc, dres, y, gp, gt)


def _post_loss(x, y, g, gt, tgt, name):
    s, d = x.shape
    ts = _tile(s, ROW_TILES)

    def body(x_ref, y_ref, g_ref, gt_ref, t_ref, col_ref, dx_ref):
        i = pl.program_id(0)

        @pl.when(i == 0)
        def _():
            col_ref[...] = jnp.zeros_like(col_ref)

        yv = y_ref[...]
        r = lax.rsqrt(jnp.mean(yv * yv, axis=-1, keepdims=True) + EPS)
        e = (x_ref[...] + gt_ref[...] * ((yv * r) * g_ref[...])) - t_ref[...]
        col_ref[...] += jnp.sum(e * e, axis=0, keepdims=True)
        dx_ref[...] = e * (1.0 / d)

    row, vec = _row_specs(s, d, ts)
    return pl.pallas_call(
        body, grid=(s // ts,), in_specs=[row, row, vec, vec, row], out_specs=[vec, row],
        out_shape=[jax.ShapeDtypeStruct((1, d), F32), jax.ShapeDtypeStruct((s, d), F32)],
        compiler_params=_params("arbitrary"), name=name,
    )(x, y, g, gt, tgt)


_GELU_C = 0.7978845608028654
_GELU_A = 0.044715


def _gelu(x):
    t = jnp.tanh(_GELU_C * (x + _GELU_A * x * x * x))
    return 0.5 * x * (1.0 + t), t


def _gelu_grad(x, t):
    return 0.5 * (1.0 + t) + 0.5 * x * (1.0 - t * t) * (_GELU_C * (1.0 + 3.0 * _GELU_A * x * x))


def _sigmoid(x):
    return 1.0 / (1.0 + jnp.exp(-x))


def _log1p_pos(y):
    u = 1.0 + y
    return jnp.where(u == 1.0, y, jnp.log(u) * (y / jnp.where(u == 1.0, 1.0, u - 1.0)))


def _softplus(x):
    return jnp.maximum(x, 0.0) + _log1p_pos(jnp.exp(-jnp.abs(x)))


def _one_minus_sq_exp(x, ex):
    z = 2.0 * x
    series = -z * (1.0 + z * (1.0 / 2 + z * (1.0 / 6 + z * (1.0 / 24 + z * (1.0 / 120)))))
    return jnp.where(z > -0.05, series, 1.0 - ex * ex)


SLAB = 16


def _cat(a, b):
    return jnp.concatenate([a, b], axis=1)


def _pair_specs(shape, nb, index):
    return [pl.BlockSpec(shape, lambda j, t: index(j, t) + (j,)), pl.BlockSpec(shape, lambda j, t: index(j, t) + (j + nb,))]


def _halo_row(ts, time_of):
    return lambda j, t: (jnp.maximum(time_of(t) * (ts // SUBLANES) - 1, 0),)


def _rows_from(groups, k):
    row = lax.broadcasted_iota(jnp.int32, groups[0].shape, 0)
    turned = [pltpu.roll(g, SUBLANES - k, axis=0) for g in groups]
    return [jnp.where(row < SUBLANES - k, lo, hi) for lo, hi in zip(turned[:-1], turned[1:])]


def _ffn_mid_fwd(p, cw, cb, name):
    s, f2 = p.shape
    ts = _tile(s, (1024, 512))
    nb, nt = f2 // (2 * CB), s // ts
    n_grp = SLAB // SUBLANES

    def body(pg_ref, pv_ref, hg_ref, hv_ref, cwg_ref, cwv_ref, cbg_ref, cbv_ref, a_ref, ga_ref, gb_ref):
        t = pl.program_id(1)
        cwv, bias = _cat(cwg_ref[...], cwv_ref[...]), _cat(cbg_ref[...], cbv_ref[...])
        w0, w1, w2 = cwv[0:1], cwv[1:2], cwv[2:3]

        def slab(before, cur, r0):
            pm2, pm1 = _rows_from([before] + cur, SUBLANES - 2), _rows_from([before] + cur, SUBLANES - 1)
            u = jnp.concatenate([bias + w0 * pm2[i] + w1 * pm1[i] + w2 * cur[i] for i in range(n_grp)], axis=0)
            g, v = u[:, :CB], u[:, CB:]
            gel, th = _gelu(g)
            rows = pl.ds(r0, SLAB)
            a_ref[rows, :] = (gel * v).astype(a_ref.dtype)
            ga_ref[rows, :] = gel.astype(ga_ref.dtype)
            gb_ref[rows, :] = (v * _gelu_grad(g, th)).astype(gb_ref.dtype)

        def pieces(rows):
            blk = _cat(pg_ref[rows, :], pv_ref[rows, :])
            return [blk[i * SUBLANES:(i + 1) * SUBLANES] for i in range(blk.shape[0] // SUBLANES)]

        slab(jnp.where(t > 0, _cat(hg_ref[...], hv_ref[...]), 0.0), pieces(pl.ds(0, SLAB)), 0)

        def loop(i, carry):
            r0 = pl.multiple_of(i * SLAB, SLAB)
            got = pieces(pl.ds(pl.multiple_of(r0 - SUBLANES, SUBLANES), SLAB + SUBLANES))
            slab(got[0], got[1:], r0)
            return carry

        lax.fori_loop(1, ts // SLAB, loop, 0, unroll=2)

    fwd = lambda t: t
    out = pl.BlockSpec((ts, CB), lambda j, t: (t, j))
    shape = jax.ShapeDtypeStruct((s, f2 // 2), MXU_DTYPE)
    return pl.pallas_call(
        body, grid=(nb, nt),
        in_specs=(_pair_specs((ts, CB), nb, lambda j, t: (t,)) + _pair_specs((SUBLANES, CB), nb, _halo_row(ts, fwd))
                  + _pair_specs((FFN_CONV, CB), nb, lambda j, t: (0,)) + _pair_specs((1, CB), nb, lambda j, t: (0,))),
        out_specs=[out, out, out], out_shape=[shape, shape, shape],
        compiler_params=_params("parallel", "arbitrary"), name=name,
    )(p, p, p, p, cw, cw, cb, cb)


def _ffn_mid_bwd(da, p, ga, gb, cw, name):
    s, f2 = p.shape
    ts = _tile(s, (1024, 512))
    nb, nt = f2 // (2 * CB), s // ts
    n_slab = ts // SLAB
    n_grp = SLAB // SUBLANES
    per_trip = 2

    def body(da_ref, ga_ref, gb_ref, pg_ref, pv_ref, cwg_ref, cwv_ref, dp_ref, dcw_ref, dcb_ref, next_du, acc):
        tt = pl.program_id(1)
        cwv = _cat(cwg_ref[...], cwv_ref[...])
        w0, w1, w2 = cwv[0:1], cwv[1:2], cwv[2:3]

        @pl.when(tt == 0)
        def _():
            next_du[...] = jnp.zeros_like(next_du)
            acc[...] = jnp.zeros_like(acc)

        def slab(r0, after, sums):
            rows = pl.ds(r0, SLAB)
            dav = da_ref[rows, :]
            du = _cat(dav * gb_ref[rows, :].astype(F32), dav * ga_ref[rows, :].astype(F32))
            p0 = _cat(pg_ref[rows, :], pv_ref[rows, :])
            cur = [du[i * SUBLANES:(i + 1) * SUBLANES] for i in range(n_grp)]
            du1, du2 = _rows_from(cur + [after], 1), _rows_from(cur + [after], 2)
            dpv = jnp.concatenate([w2 * cur[i] + w1 * du1[i] + w0 * du2[i] for i in range(n_grp)], axis=0).astype(dp_ref.dtype)
            dp_ref[0, rows, :] = dpv[:, :CB]
            dp_ref[1, rows, :] = dpv[:, CB:]
            for i in range(n_grp):
                pi = p0[i * SUBLANES:(i + 1) * SUBLANES]
                parts = (cur[i], du2[i] * pi, du1[i] * pi, cur[i] * pi)
                sums = parts if sums is None else tuple(x + y for x, y in zip(sums, parts))
            return cur[0], sums

        def loop(k, after):
            sums = None
            for j in range(per_trip):
                r0 = pl.multiple_of((n_slab - 1 - (k * per_trip + j)) * SLAB, SLAB)
                after, sums = slab(r0, after, sums)
            for q, part in enumerate(sums):
                acc[q] += part
            return after

        next_du[...] = lax.fori_loop(0, n_slab // per_trip, loop, next_du[...])

        @pl.when(tt == nt - 1)
        def _():
            for half in range(2):
                cols = slice(half * CB, (half + 1) * CB)
                dcb_ref[half] = jnp.sum(acc[0][:, cols], axis=0, keepdims=True)
                for k in range(FFN_CONV):
                    dcw_ref[half, k:k + 1, :] = jnp.sum(acc[1 + k][:, cols], axis=0, keepdims=True)

    rev = lambda t: nt - 1 - t
    tile = pl.BlockSpec((ts, CB), lambda j, t: (rev(t), j))
    return pl.pallas_call(
        body, grid=(nb, nt),
        in_specs=([tile, tile, tile] + _pair_specs((ts, CB), nb, lambda j, t: (rev(t),))
                  + _pair_specs((FFN_CONV, CB), nb, lambda j, t: (0,))),
        out_specs=[pl.BlockSpec((2, ts, CB), lambda j, t: (0, rev(t), j)),
                   pl.BlockSpec((2, FFN_CONV, CB), lambda j, t: (0, 0, j)),
                   pl.BlockSpec((2, 1, CB), lambda j, t: (0, 0, j))],
        out_shape=[jax.ShapeDtypeStruct((2, s, f2 // 2), MXU_DTYPE), jax.ShapeDtypeStruct((2, FFN_CONV, f2 // 2), F32),
                   jax.ShapeDtypeStruct((2, 1, f2 // 2), F32)],
        scratch_shapes=[pltpu.VMEM((SUBLANES, 2 * CB), F32), pltpu.VMEM((1 + FFN_CONV, SUBLANES, 2 * CB), F32)],
        compiler_params=_params("parallel", "arbitrary"), name=name,
    )(da, ga, gb, p, p, cw, cw)


def _rg_gates(xc, wa_ref, ba_ref, wx_ref, bx_ref, lam_ref):
    r = _sigmoid(_dot_nn(xc, wa_ref[0]) + ba_ref[...])
    ig = _sigmoid(_dot_nn(xc, wx_ref[0]) + bx_ref[...])
    sp = _softplus(-lam_ref[...])
    log_a = (-RG_C) * r * sp
    a = jnp.exp(log_a)
    mult = jnp.sqrt(_one_minus_sq_exp(log_a, a))
    return r, ig, sp, a, mult


def _rg_conv(scr, cw_ref, cb_ref, ts):
    views = [scr[5 + k:5 + k + ts, :] for k in range(RG_CONV)]
    xc = cb_ref[...]
    for k in range(RG_CONV):
        xc = xc + cw_ref[k:k + 1, :] * views[k]
    return xc, views


def _rg_param_specs():
    vec = pl.BlockSpec((1, CB), lambda g, t: (0, g))
    mat = pl.BlockSpec((1, CB, CB), lambda g, t: (g, 0, 0))
    return [pl.BlockSpec((RG_CONV, CB), lambda g, t: (0, g)), vec, mat, vec, mat, vec, vec]


def _scan_rows(a_scr, x_scr, out_ref, carry, ts, reverse):
    n = ts // SUBLANES
    row = lax.broadcasted_iota(jnp.int32, (SUBLANES, a_scr.shape[1]), 0)
    last = SUBLANES - 1

    def rows_of(k):
        return pl.ds(pl.multiple_of(k * SUBLANES, SUBLANES), SUBLANES)

    def local(k, _):
        rows = rows_of(k)
        a, x = a_scr[rows, :], x_scr[rows, :]
        if reverse:
            a = jnp.where(row == last, 1.0, pltpu.roll(a, last, axis=0))
            for sh in (1, 2, 4):
                keep = row < SUBLANES - sh
                x = x + a * jnp.where(keep, pltpu.roll(x, SUBLANES - sh, axis=0), 0.0)
                a = a * jnp.where(keep, pltpu.roll(a, SUBLANES - sh, axis=0), 1.0)
        else:
            for sh in (1, 2, 4):
                keep = row >= sh
                x = a * jnp.where(keep, pltpu.roll(x, sh, axis=0), 0.0) + x
                a = a * jnp.where(keep, pltpu.roll(a, sh, axis=0), 1.0)
        out_ref[rows, :] = x
        x_scr[rows, :] = a
        return 0

    lax.fori_loop(0, n, local, 0, unroll=4)

    def chain(k, c):
        rows = rows_of(n - 1 - k if reverse else k)
        v = out_ref[rows, :] + x_scr[rows, :] * c
        out_ref[rows, :] = v
        return a_scr[rows, :][0:1] * v[0:1] if reverse else v[last:last + 1]

    return lax.fori_loop(0, n, chain, carry, unroll=4)


def _rg_mid_fwd(pj, cw, cb, wa, ba, wx, bx, lam, name):
    s = pj.shape[0]
    nb = pj.shape[1] // (2 * CB)
    ts = _tile(s, (512,))
    nt = s // ts

    def body(gate_ref, x_ref, halo_ref, cw_ref, cb_ref, wa_ref, ba_ref, wx_ref, bx_ref, lam_ref, y_ref, hs_ref,
             scr, a_scr, u_scr, h_scr):
        t = pl.program_id(1)

        @pl.when(t == 0)
        def _():
            h_scr[...] = jnp.zeros_like(h_scr)

        scr[0:SUBLANES, :] = jnp.where(t > 0, halo_ref[...], 0.0)
        scr[SUBLANES:, :] = x_ref[...]
        xc, _ = _rg_conv(scr, cw_ref, cb_ref, ts)
        _, ig, _, a, mult = _rg_gates(xc, wa_ref, ba_ref, wx_ref, bx_ref, lam_ref)
        a_scr[...] = a
        u_scr[...] = mult * (ig * xc)
        h_scr[0:1, :] = _scan_rows(a_scr, u_scr, hs_ref, h_scr[0:1, :], ts, False)
        y_ref[...] = (_gelu(gate_ref[...])[0] * hs_ref[...]).astype(y_ref.dtype)

    blk = pl.BlockSpec((ts, CB), lambda g, t: (t, g))
    return pl.pallas_call(
        body, grid=(nb, nt),
        in_specs=_pair_specs((ts, CB), nb, lambda g, t: (t,))
        + [pl.BlockSpec((SUBLANES, CB), lambda g, t: _halo_row(ts, lambda u: u)(g, t) + (g + nb,))] + _rg_param_specs(),
        out_specs=[blk, blk],
        out_shape=[jax.ShapeDtypeStruct((s, nb * CB), MXU_DTYPE), jax.ShapeDtypeStruct((s, nb * CB), F32)],
        scratch_shapes=[pltpu.VMEM((ts + SUBLANES, CB), F32), pltpu.VMEM((ts, CB), F32), pltpu.VMEM((ts, CB), F32),
                        pltpu.VMEM((SUBLANES, CB), F32)],
        compiler_params=_params("parallel", "arbitrary"), name=name,
    )(pj, pj, pj, cw, cb, wa, ba, wx, bx, lam)


def _rg_mid_bwd(dy, pj, hs, cw, cb, wa, ba, wx, bx, lam, name):
    s = pj.shape[0]
    nb = pj.shape[1] // (2 * CB)
    ts = _tile(s, (512,))
    nt = s // ts

    def body(dy_ref, gate_ref, x_ref, halo_ref, hs_ref, hsh_ref, cw_ref, cb_ref, wa_ref, ba_ref, wx_ref, bx_ref, lam_ref,
             dpj_ref, dcw_ref, dcb_ref, dwa_ref, dba_ref, dwx_ref, dbx_ref, dlam_ref,
             scr, hscr, a_scr, d_scr, g_scr, dxscr, c_scr):
        tt = pl.program_id(1)
        t = nt - 1 - tt

        @pl.when(tt == 0)
        def _():
            c_scr[...] = jnp.zeros_like(c_scr)
            dxscr[ts:, :] = jnp.zeros((SUBLANES, CB), F32)
            for ref in (dcw_ref, dcb_ref, dwa_ref, dba_ref, dwx_ref, dbx_ref, dlam_ref):
                ref[...] = jnp.zeros_like(ref)

        scr[0:SUBLANES, :] = jnp.where(t > 0, halo_ref[...], 0.0)
        scr[SUBLANES:, :] = x_ref[...]
        hscr[0:SUBLANES, :] = jnp.where(t > 0, hsh_ref[...], 0.0)
        hscr[SUBLANES:, :] = hs_ref[...]
        xc, views = _rg_conv(scr, cw_ref, cb_ref, ts)
        r, ig, sp, a, mult = _rg_gates(xc, wa_ref, ba_ref, wx_ref, bx_ref, lam_ref)
        gate = gate_ref[...]
        gel, th = _gelu(gate)
        dyv = dy_ref[...]
        dpj_ref[0] = (dyv * hs_ref[...] * _gelu_grad(gate, th)).astype(dpj_ref.dtype)
        a_scr[...] = a
        d_scr[...] = dyv * gel
        c_scr[0:1, :] = _scan_rows(a_scr, d_scr, g_scr, c_scr[0:1, :], ts, True)
        du = g_scr[...]
        da = du * hscr[7:7 + ts, :]
        dmult = du * (ig * xc)
        dig = du * (mult * xc)
        dxc = du * (mult * ig)
        dlog_a = da * a - dmult * (a * a / mult)
        dlam_ref[...] += jnp.sum(dlog_a * r, axis=0, keepdims=True) * (RG_C * _sigmoid(-lam_ref[...]))
        dpr = dlog_a * ((-RG_C) * sp) * (r * (1.0 - r))
        dpi = dig * (ig * (1.0 - ig))
        dba_ref[...] += jnp.sum(dpr, axis=0, keepdims=True)
        dbx_ref[...] += jnp.sum(dpi, axis=0, keepdims=True)
        dwa_ref[0] += _dot_tn(xc, dpr)
        dwx_ref[0] += _dot_tn(xc, dpi)
        dxc = dxc + _dot_nt(dpr, wa_ref[0]) + _dot_nt(dpi, wx_ref[0])
        dcb_ref[...] += jnp.sum(dxc, axis=0, keepdims=True)
        for k in range(RG_CONV):
            dcw_ref[k:k + 1, :] += jnp.sum(dxc * views[k], axis=0, keepdims=True)
        dxscr[0:ts, :] = dxc
        dxp = cw_ref[3:4, :] * dxc
        for k in range(RG_CONV - 1):
            dxp = dxp + cw_ref[k:k + 1, :] * dxscr[3 - k:3 - k + ts, :]
        dpj_ref[1] = dxp.astype(dpj_ref.dtype)
        dxscr[ts:, :] = dxscr[0:SUBLANES, :]

    rev = lambda g, t: (nt - 1 - t, g)
    rev_halo = lambda g, t: (jnp.maximum((nt - 1 - t) * (ts // SUBLANES) - 1, 0), g)
    vec = pl.BlockSpec((1, CB), lambda g, t: (0, g))
    mat = pl.BlockSpec((1, CB, CB), lambda g, t: (g, 0, 0))
    d = nb * CB
    vshape = jax.ShapeDtypeStruct((1, d), F32)
    mshape = jax.ShapeDtypeStruct((nb, CB, CB), F32)
    return pl.pallas_call(
        body, grid=(nb, nt),
        in_specs=[pl.BlockSpec((ts, CB), rev)] + _pair_specs((ts, CB), nb, lambda g, t: (nt - 1 - t,))
        + [pl.BlockSpec((SUBLANES, CB), lambda g, t: (rev_halo(g, t)[0], g + nb)),
           pl.BlockSpec((ts, CB), rev), pl.BlockSpec((SUBLANES, CB), rev_halo)] + _rg_param_specs(),
        out_specs=[pl.BlockSpec((2, ts, CB), lambda g, t: (0, nt - 1 - t, g)), pl.BlockSpec((RG_CONV, CB), lambda g, t: (0, g)),
                   vec, mat, vec, mat, vec, vec],
        out_shape=[jax.ShapeDtypeStruct((2, s, d), MXU_DTYPE), jax.ShapeDtypeStruct((RG_CONV, d), F32), vshape, mshape, vshape,
                   mshape, vshape, vshape],
        scratch_shapes=[pltpu.VMEM((ts + SUBLANES, CB), F32), pltpu.VMEM((ts + SUBLANES, CB), F32), pltpu.VMEM((ts, CB), F32),
                        pltpu.VMEM((ts, CB), F32), pltpu.VMEM((ts, CB), F32), pltpu.VMEM((ts + SUBLANES, CB), F32),
                        pltpu.VMEM((SUBLANES, CB), F32)],
        compiler_params=_params("parallel", "arbitrary"), name=name,
    )(dy, pj, pj, pj, hs, hs, cw, cb, wa, ba, wx, bx, lam)


GLA_DK = 128
GLA_DV = 256
GLA_O_K = GLA_HEADS * GLA_DK
GLA_O_V = 2 * GLA_HEADS * GLA_DK
GLA_O_R = GLA_O_V + GLA_HEADS * GLA_DV
GLA_O_Z = GLA_O_R + GLA_HEADS * GLA_DV
GLA_IN = GLA_O_Z + GLA_RANK
GLA_TS = 256


def _dk(h, base=0):
    return slice(base + h * GLA_DK, base + (h + 1) * GLA_DK)


def _dv(h, base=0):
    return slice(base + h * GLA_DV, base + (h + 1) * GLA_DV)


def _split3(x):
    hi = x.astype(BF16)
    r1 = x - hi.astype(F32)
    mid = r1.astype(BF16)
    lo = (r1 - mid.astype(F32)).astype(BF16)
    return hi, mid, lo


def _chunk_cumsum(x, reverse):
    n = x.shape[0]
    i = lax.broadcasted_iota(jnp.int32, (n, n), 0)
    j = lax.broadcasted_iota(jnp.int32, (n, n), 1)
    same = (i // GLA_CHUNK) == (j // GLA_CHUNK)
    tri = jnp.where(same & ((j >= i) if reverse else (j <= i)), 1.0, 0.0).astype(BF16)
    out = jnp.zeros(x.shape, F32)
    for piece in _split3(x):
        out = out + lax.dot_general(tri, piece, (((1,), (0,)), ((), ())), preferred_element_type=F32)
    return out


def _gla_head(pj_ref, h):
    return (pj_ref[:, _dk(h)] * (GLA_DK ** -0.5), pj_ref[:, _dk(h, GLA_O_K)], pj_ref[:, _dv(h, GLA_O_V)],
            pj_ref[:, _dv(h, GLA_O_R)])


def _gla_decays(gc):
    gref = gc[GLA_CHUNK // 2:GLA_CHUNK // 2 + 1, :]
    glast = gc[GLA_CHUNK - 1:GLA_CHUNK, :]
    return jnp.exp(gc), jnp.exp(gc - gref), jnp.exp(gref - gc), jnp.exp(glast - gc), jnp.exp(glast)


def _causal_mask():
    i = lax.broadcasted_iota(jnp.int32, (GLA_CHUNK, GLA_CHUNK), 0)
    j = lax.broadcasted_iota(jnp.int32, (GLA_CHUNK, GLA_CHUNK), 1)
    return j <= i


def _log_sigmoid(x):
    return jnp.minimum(x, 0.0) - _log1p_pos(jnp.exp(-jnp.abs(x)))


def _gla_mid_fwd(pj, wal, bal, ng, name):
    s, nh = pj.shape[0], GLA_HEADS
    ts = _tile(s, (GLA_TS,))
    nt, nc = s // ts, ts // GLA_CHUNK

    def body(pj_ref, wal_ref, bal_ref, ng_ref, act_ref, o_ref, st_ref, s_scr):
        t = pl.program_id(0)

        @pl.when(t == 0)
        def _():
            s_scr[...] = jnp.zeros_like(s_scr)

        heads = []
        z = pj_ref[:, GLA_O_Z:]
        for h in range(nh):
            q, k, v, r = _gla_head(pj_ref, h)
            g = _log_sigmoid(_dot_nn(z, wal_ref[:, _dk(h)]) + bal_ref[:, _dk(h)]) * (1.0 / GLA_TAU)
            heads.append((q, k, v, r, _chunk_cumsum(g, False)))
        mask = _causal_mask()
        for c in range(nc):
            sl = slice(c * GLA_CHUNK, (c + 1) * GLA_CHUNK)
            for h, (q, k, v, r, gcum) in enumerate(heads):
                eg, eq, ek, ekd, egl = _gla_decays(gcum[sl])
                st = s_scr[h]
                st_ref[c, h] = st
                attn = jnp.where(mask, _dot_nt(q[sl] * eq, k[sl] * ek), 0.0)
                o_ref[sl, h * GLA_DV:(h + 1) * GLA_DV] = _dot_nt(q[sl] * eg, st) + _dot_nn(attn, v[sl])
                s_scr[h] = st * egl + _dot_tn(v[sl], k[sl] * ekd)
        for h, (q, k, v, r, gcum) in enumerate(heads):
            cols = slice(h * GLA_DV, (h + 1) * GLA_DV)
            o = o_ref[:, cols]
            on = o * lax.rsqrt(jnp.mean(o * o, axis=-1, keepdims=True) + EPS)
            act_ref[:, cols] = ((on * ng_ref[...]) * (r * _sigmoid(r))).astype(act_ref.dtype)

    blk = pl.BlockSpec((ts, nh * GLA_DV), lambda t: (t, 0))
    whole = lambda shape: pl.BlockSpec(shape, lambda t: (0,) * len(shape))
    return pl.pallas_call(
        body, grid=(nt,),
        in_specs=[pl.BlockSpec((ts, GLA_IN), lambda t: (t, 0)), whole((GLA_RANK, nh * GLA_DK)), whole((1, nh * GLA_DK)),
                  whole((1, GLA_DV))],
        out_specs=[blk, blk, pl.BlockSpec((nc, nh, GLA_DV, GLA_DK), lambda t: (t, 0, 0, 0))],
        out_shape=[jax.ShapeDtypeStruct((s, nh * GLA_DV), MXU_DTYPE), jax.ShapeDtypeStruct((s, nh * GLA_DV), F32),
                   jax.ShapeDtypeStruct((s // GLA_CHUNK, nh, GLA_DV, GLA_DK), F32)],
        scratch_shapes=[pltpu.VMEM((nh, GLA_DV, GLA_DK), F32)],
        compiler_params=_params("arbitrary"), name=name,
    )(pj, wal, bal, ng)


def _gla_mid_bwd(dact, pj, o, st, wal, bal, ng, name):
    s, nh = pj.shape[0], GLA_HEADS
    ts = _tile(s, (GLA_TS,))
    nt, nc = s // ts, ts // GLA_CHUNK

    def body(dact_ref, pj_ref, o_ref, st_ref, wal_ref, bal_ref, ng_ref, dpj_ref, dwal_ref, dbal_ref, dng_ref,
             ds_scr, dg_scr):
        tt = pl.program_id(0)

        @pl.when(tt == 0)
        def _():
            ds_scr[...] = jnp.zeros_like(ds_scr)
            dwal_ref[...] = jnp.zeros_like(dwal_ref)
            dbal_ref[...] = jnp.zeros_like(dbal_ref)
            dng_ref[...] = jnp.zeros_like(dng_ref)

        heads = []
        z = pj_ref[:, GLA_O_Z:]
        for h in range(nh):
            q, k, v, r = _gla_head(pj_ref, h)
            logit = _dot_nn(z, wal_ref[:, _dk(h)]) + bal_ref[:, _dk(h)]
            gcum = _chunk_cumsum(_log_sigmoid(logit) * (1.0 / GLA_TAU), False)
            ov = o_ref[:, h * GLA_DV:(h + 1) * GLA_DV]
            ro = lax.rsqrt(jnp.mean(ov * ov, axis=-1, keepdims=True) + EPS)
            on = ov * ro
            sg = _sigmoid(r)
            sil = r * sg
            dav = dact_ref[:, h * GLA_DV:(h + 1) * GLA_DV]
            dpj_ref[:, _dv(h, GLA_O_R)] = (dav * (on * ng_ref[...]) * (sg + sil * (1.0 - sg))).astype(dpj_ref.dtype)
            t1 = dav * sil
            dng_ref[...] += jnp.sum(t1 * on, axis=0, keepdims=True)
            dn = t1 * ng_ref[...]
            do = ro * (dn - on * jnp.mean(dn * on, axis=-1, keepdims=True))
            heads.append((q, k, v, logit, gcum, do))
        mask = _causal_mask()
        scale = GLA_DK ** -0.5
        last_row = lax.broadcasted_iota(jnp.int32, (GLA_CHUNK, GLA_DK), 0) == GLA_CHUNK - 1
        for c in reversed(range(nc)):
            sl = slice(c * GLA_CHUNK, (c + 1) * GLA_CHUNK)
            for h, (q, k, v, logit, gcum, do) in enumerate(heads):
                eg, eq, ek, ekd, egl = _gla_decays(gcum[sl])
                qc, kc, vc, doc = q[sl], k[sl], v[sl], do[sl]
                qg, qt, kt, kd = qc * eg, qc * eq, kc * ek, kc * ekd
                sp = st_ref[c, h]
                ds = ds_scr[h]
                attn = jnp.where(mask, _dot_nt(qt, kt), 0.0)
                dattn = jnp.where(mask, _dot_nt(doc, vc), 0.0)
                dqg = _dot_nn(doc, sp)
                dqt = _dot_nn(dattn, kt)
                dkt = _dot_tn(dattn, qt)
                dkd = _dot_nn(vc, ds)
                dpj_ref[sl, _dv(h, GLA_O_V)] = (_dot_tn(attn, doc) + _dot_nt(kd, ds)).astype(dpj_ref.dtype)
                dpj_ref[sl, _dk(h)] = (scale * (dqg * eg + dqt * eq)).astype(dpj_ref.dtype)
                dpj_ref[sl, _dk(h, GLA_O_K)] = (dkt * ek + dkd * ekd).astype(dpj_ref.dtype)
                kdd = dkd * kd
                dgl = jnp.sum(kdd, axis=0, keepdims=True) + jnp.sum(ds * sp, axis=0, keepdims=True) * egl
                dg_scr[h, sl, :] = dqg * qg + dqt * qt - dkt * kt - kdd + jnp.where(last_row, dgl, 0.0)
                ds_scr[h] = ds * egl + _dot_tn(doc, qg)
        dz = jnp.zeros((ts, GLA_RANK), F32)
        for h, (q, k, v, logit, gcum, do) in enumerate(heads):
            dlogit = _chunk_cumsum(dg_scr[h], True) * (1.0 / GLA_TAU) * _sigmoid(-logit)
            dz = dz + _dot_nt(dlogit, wal_ref[:, _dk(h)])
            dwal_ref[:, _dk(h)] += _dot_tn(z, dlogit)
            dbal_ref[:, _dk(h)] += jnp.sum(dlogit, axis=0, keepdims=True)
        dpj_ref[:, GLA_O_Z:] = dz.astype(dpj_ref.dtype)

    rev = lambda t: (nt - 1 - t, 0)
    whole = lambda shape: pl.BlockSpec(shape, lambda t: (0,) * len(shape))
    wide = pl.BlockSpec((ts, nh * GLA_DV), rev)
    return pl.pallas_call(
        body, grid=(nt,),
        in_specs=[wide, pl.BlockSpec((ts, GLA_IN), rev), wide,
                  pl.BlockSpec((nc, nh, GLA_DV, GLA_DK), lambda t: (nt - 1 - t, 0, 0, 0)),
                  whole((GLA_RANK, nh * GLA_DK)), whole((1, nh * GLA_DK)), whole((1, GLA_DV))],
        out_specs=[pl.BlockSpec((ts, GLA_IN), rev), whole((GLA_RANK, nh * GLA_DK)), whole((1, nh * GLA_DK)), whole((1, GLA_DV))],
        out_shape=[jax.ShapeDtypeStruct((s, GLA_IN), MXU_DTYPE), jax.ShapeDtypeStruct((GLA_RANK, nh * GLA_DK), F32),
                   jax.ShapeDtypeStruct((1, nh * GLA_DK), F32), jax.ShapeDtypeStruct((1, GLA_DV), F32)],
        scratch_shapes=[pltpu.VMEM((nh, GLA_DV, GLA_DK), F32), pltpu.VMEM((nh, ts, GLA_DK), F32)],
        compiler_params=_params("arbitrary"), name=name,
    )(dact, pj, o, st, wal, bal, ng)


def _adamw(w, gs, m, v, name, after=None):
    layers, rows, cols = w.shape
    gs = list(gs) if isinstance(gs, (list, tuple)) else gs
    n_g = len(gs) if isinstance(gs, list) else 1
    if rows % SUBLANES == 0:
        tr, tc = _tile(rows, (256, 128, 64, 32, 16, 8)), cols
    else:
        tr, tc = rows, _tile(cols, (256, 128))
    c1 = 1.0 / (1.0 - ADAM_B1 ** ADAM_STEP)
    c2 = 1.0 / (1.0 - ADAM_B2 ** ADAM_STEP)

    def body(*refs):
        g_refs, (w_ref, m_ref, v_ref) = refs[:n_g], refs[n_g:n_g + 3]
        go_ref, d_ref, mo_ref, vo_ref = refs[-4:]
        gv = g_refs[0][...]
        for l in range(1, n_g):
            gv = jnp.where(pl.program_id(0) == l, g_refs[l][...], gv)
        m2 = ADAM_B1 * m_ref[...] + (1.0 - ADAM_B1) * gv
        v2 = ADAM_B2 * v_ref[...] + (1.0 - ADAM_B2) * (gv * gv)
        d_ref[...] = (-ADAM_LR) * ((m2 * c1) / (jnp.sqrt(v2 * c2) + ADAM_EPS) + ADAM_WD * w_ref[...])
        go_ref[...] = gv
        mo_ref[...] = m2
        vo_ref[...] = v2

    spec = pl.BlockSpec((None, tr, tc), lambda l, i, j: (l, i, j))
    g_specs = [pl.BlockSpec((tr, tc), lambda l, i, j: (i, j))] * n_g if isinstance(gs, list) else [spec]
    extra = [] if after is None else [(after, _ANY)]
    shape = jax.ShapeDtypeStruct((layers, rows, cols), F32)
    return pl.pallas_call(
        body, grid=(layers, rows // tr, cols // tc), in_specs=g_specs + [spec] * 3 + [sp for _, sp in extra],
        out_specs=[spec] * 4, out_shape=[shape] * 4, compiler_params=_params("parallel", "parallel", "parallel"), name=name,
    )(*(gs if isinstance(gs, list) else [gs]), w, m, v, *[a for a, _ in extra])


def _col_slots(w, name):
    r, cc = w.shape
    c = cc // N_CHIP
    tr = _tile(r, (256,))

    def body(w_ref, o_ref):
        for j in range(N_CHIP):
            o_ref[j] = w_ref[:, j * c:(j + 1) * c]

    return pl.pallas_call(
        body, grid=(r // tr,), in_specs=[pl.BlockSpec((tr, cc), lambda i: (i, 0))],
        out_specs=pl.BlockSpec((N_CHIP, tr, c), lambda i: (0, i, 0)), out_shape=jax.ShapeDtypeStruct((N_CHIP, r, c), w.dtype),
        compiler_params=_params("parallel"), name=name,
    )(w)


def _from_col_slots(w, name):
    n, r, c = w.shape
    tr = _tile(r, (256,))

    def body(w_ref, o_ref):
        for j in range(n):
            o_ref[:, j * c:(j + 1) * c] = w_ref[j]

    return pl.pallas_call(
        body, grid=(r // tr,), in_specs=[pl.BlockSpec((n, tr, c), lambda i: (0, i, 0))],
        out_specs=pl.BlockSpec((tr, n * c), lambda i: (i, 0)), out_shape=jax.ShapeDtypeStruct((r, n * c), w.dtype),
        compiler_params=_params("parallel"), name=name,
    )(w)


def _block_rows_to_slots(w):
    g, r4, cc = w.shape
    return jnp.swapaxes(w.reshape(g, N_CHIP, r4 // N_CHIP, cc), 0, 1).reshape(N_CHIP, g * (r4 // N_CHIP), cc)


def _slots_to_block_rows(w, g):
    n, gr, cc = w.shape
    return jnp.swapaxes(w.reshape(n, g, gr // g, cc), 0, 1).reshape(g, n * (gr // g), cc)


def _local_step(x, tgt, mod, w, fetch=None, done=None, later=None):
    depth = mod.shape[0]
    row = lambda v: v.reshape(1, -1)
    w = dict(w)
    w["ffn_w_up"], w["ffn_w_down"] = dict(enumerate(w["ffn_w_up"])), dict(enumerate(w["ffn_w_down"]))

    def arrive(stage, after):
        if fetch is not None:
            for k, v in fetch(stage, after).items():
                if isinstance(v, dict):
                    w[k].update(v)
                else:
                    w[k] = v

    saved = []
    for i in range(depth):
        if i == 1:
            arrive("gla", x)
        sh_m, sc_m, gt_m, sh_f, sc_f, gt_f = (mod[i, j:j + 1] for j in range(6))
        g0, g1, g2, g3 = (w["norm_g"][i, j:j + 1] for j in range(4))
        tag = f"_l{i}"
        if i == 0:
            h = _norm_mod_fwd(x, g0, sc_m, sh_m, "norm_mix" + tag)
        if i % 2 == 0:
            pj = _mm(h, w["rg_w_in"], w_slots=N_CHIP, name="rg_in" + tag)
            act, aux = _rg_mid_fwd(pj, w["rg_conv_w"], row(w["rg_conv_b"]), w["rg_wa"], row(w["rg_ba"]), w["rg_wx"],
                                   row(w["rg_bx"]), row(w["rg_lambda"]), "rg_mid" + tag)
            y = _mm(act, w["rg_w_out"], name="rg_out" + tag)
        else:
            pj = _mm(h, w["gla_w_in"], tm_max=512, name="gla_in" + tag)
            act, *aux = _gla_mid_fwd(pj, w["gla_w_alpha"], row(w["gla_b_alpha"]), row(w["gla_norm_g"]), "gla_mid" + tag)
            y = _mm(act, w["gla_w_out"], name="gla_out" + tag)
        x1, h2 = _post_norm_fwd(x, y, g1, gt_m, g2, sc_f, sh_f, "post_mix" + tag)
        arrive(f"ffn{i}", x1)
        p = _mm(h2, w["ffn_w_up"][i], w_slots=N_CHIP, name="ffn_up" + tag)
        a, ga, gb = _ffn_mid_fwd(p, w["ffn_conv_w"][i], w["ffn_conv_b"][i:i + 1], "ffn_mid" + tag)
        y2 = _mm(a, w["ffn_w_down"][i], name="ffn_down" + tag)
        saved_h = h
        if i + 1 < depth:
            nxt = [mod[i + 1, j:j + 1] for j in range(2)] + [w["norm_g"][i + 1, 0:1]]
            x2, h = _post_norm_fwd(x1, y2, g3, gt_f, nxt[2], nxt[1], nxt[0], "post_ffn" + tag)
        else:
            x2 = None
            cols, dx = _post_loss(x1, y2, g3, gt_f, tgt, "post_ffn_loss")
        saved.append((x, saved_h, pj, act, aux, y, x1, h2, p, (a, ga, gb), y2))
        x = x2

    stacked = ("norm_g", "ffn_conv_w", "ffn_conv_b", "mod")
    gr = {k: [None] * depth for k in stacked + ("ffn_w_up", "ffn_w_down")}
    told = lambda stage, after: done(stage, gr, after) if done is not None else 0.0
    told_later = lambda stage, after: later(stage, after) if later is not None else 0.0
    for i in reversed(range(depth)):
        x0, h, pj, act, aux, y, x1, h2, p, (a, ga, gb), y2 = saved[i]
        sh_m, sc_m, gt_m, sh_f, sc_f, gt_f = (mod[i, j:j + 1] for j in range(6))
        g0, g1, g2, g3 = (w["norm_g"][i, j:j + 1] for j in range(4))
        tag = f"_l{i}"
        if i == depth - 1:
            dy2, d_g3, d_gt_f = _post_bwd(dx, y2, g3, gt_f, "post_ffn_b" + tag)
        else:
            dy2, d_g3, d_gt_f = ahead
        da = _mm(dy2, w["ffn_w_down"][i], tb=True, name="ffn_down_dx" + tag)
        gr["ffn_w_down"][i] = _mm(a, dy2, ta=True, name="ffn_down_dw" + tag)
        conv_w = w["ffn_conv_w"][i] + (told_later("l1", da) if i == 0 else 0.0)
        dp, dcw, dcb = _ffn_mid_bwd(da, p, ga, gb, conv_w, "ffn_mid_b" + tag)
        gr["ffn_conv_w"][i], gr["ffn_conv_b"][i] = _cat(dcw[0], dcw[1]), _cat(dcb[0], dcb[1])[0]
        dh2 = _mm(dp, w["ffn_w_up"][i], tb=True, a_parts=2, w_slots=N_CHIP, name="ffn_up_dx" + tag)
        gr["ffn_w_up"][i] = _mm(h2, dp, ta=True, b_parts=2, out_slots=N_CHIP, name="ffn_up_dw" + tag)
        if i == 0:
            gt_m = gt_m + told("ffn0", dh2)
        dx1, dy, d_g2, d_sc_f, d_sh_f, d_g1, d_gt_m = _norm_post_bwd(dh2, x1, g2, sc_f, dx, y, g1, gt_m, "norm_ffn_b" + tag)
        if i % 2 == 0:
            dact = _mm(dy, w["rg_w_out"], tb=True, name="rg_out_dx" + tag)
            gr["rg_w_out"] = _mm(act, dy, ta=True, name="rg_out_dw" + tag)
            lam = row(w["rg_lambda"]) + told_later("ffn0", gr["rg_w_out"])
            dpj, gr["rg_conv_w"], d_cb, gr["rg_wa"], d_ba, gr["rg_wx"], d_bx, d_lam = _rg_mid_bwd(
                dact, pj, aux, w["rg_conv_w"], row(w["rg_conv_b"]), w["rg_wa"], row(w["rg_ba"]), w["rg_wx"],
                row(w["rg_bx"]), lam, "rg_mid_b" + tag)
            gr["rg_conv_b"], gr["rg_ba"], gr["rg_bx"], gr["rg_lambda"] = d_cb[0], d_ba[0], d_bx[0], d_lam[0]
            dh = _mm(dpj, w["rg_w_in"], tb=True, a_parts=2, w_slots=N_CHIP, name="rg_in_dx" + tag)
            gr["rg_w_in"] = _mm(h, dpj, ta=True, b_parts=2, out_slots=N_CHIP, name="rg_in_dw" + tag)
            sc_m = sc_m + told("rg", dh)
        else:
            dact = _mm(dy, w["gla_w_out"], tb=True, name="gla_out_dx" + tag)
            gr["gla_w_out"] = _mm(act, dy, ta=True, name="gla_out_dw" + tag)
            dpj, gr["gla_w_alpha"], d_bal, d_ng = _gla_mid_bwd(dact, pj, aux[0], aux[1], w["gla_w_alpha"], row(w["gla_b_alpha"]),
                                                               row(w["gla_norm_g"]), "gla_mid_b" + tag)
            gr["gla_b_alpha"], gr["gla_norm_g"] = d_bal[0], d_ng[0]
            dh = _mm(dpj, w["gla_w_in"], tb=True, name="gla_in_dx" + tag)
            gr["gla_w_in"] = _mm(h, dpj, ta=True, tm_max=512, name="gla_in_dw" + tag)
            mod = mod.at[0].add(told("l1", dh))
        if i > 0:
            dx, dy_below, d_g0, d_sc_m, d_sh_m, d_g_below, d_gt_below = _norm_post_bwd(
                dh, x0, g0, sc_m, dx1, saved[i - 1][-1], w["norm_g"][i - 1, 3:4], mod[i - 1, 5:6], "norm_mix_b" + tag)
            ahead = (dy_below, d_g_below, d_gt_below)
        else:
            dx, d_g0, d_sc_m, d_sh_m = _norm_mod_bwd(dh, x0, g0, sc_m, dx1, "norm_mix_b" + tag)
        gr["norm_g"][i] = jnp.concatenate([d_g0, d_g1, d_g2, d_g3], axis=0)
        gr["mod"][i] = jnp.concatenate([d_sh_m, d_sc_m, d_gt_m, d_sh_f, d_sc_f, d_gt_f], axis=0)
    for k in stacked:
        gr[k] = jnp.stack(gr[k])
    return cols, dx, gr


ADA_ROWS = 16


def _ada_fwd(c16, ada_w, ada_b, name):
    depth, d, n = ada_w.shape
    tn = _tile(n, (512, 256, 128))

    def body(c_ref, w_ref, b_ref, o_ref):
        cv = c_ref[...]
        o_ref[0] = _dot_nn(cv * _sigmoid(cv), w_ref[0]) + b_ref[0]

    return pl.pallas_call(
        body, grid=(depth, n // tn),
        in_specs=[pl.BlockSpec((ADA_ROWS, d), lambda l, j: (0, 0)), pl.BlockSpec((1, d, tn), lambda l, j: (l, 0, j)),
                  pl.BlockSpec((1, 1, tn), lambda l, j: (l, 0, j))],
        out_specs=pl.BlockSpec((1, ADA_ROWS, tn), lambda l, j: (l, 0, j)),
        out_shape=jax.ShapeDtypeStruct((depth, ADA_ROWS, n), F32),
        compiler_params=_params("parallel", "parallel"), name=name,
    )(c16, ada_w, ada_b)


def _ada_bwd(c16, dmod16, name):
    depth, _, n = dmod16.shape
    d = c16.shape[1]
    tn = _tile(n, (512, 256, 128))

    def body(c_ref, dm_ref, o_ref):
        cv = c_ref[...]
        o_ref[0] = _dot_tn(cv * _sigmoid(cv), dm_ref[0])

    return pl.pallas_call(
        body, grid=(depth, n // tn),
        in_specs=[pl.BlockSpec((ADA_ROWS, d), lambda l, j: (0, 0)), pl.BlockSpec((1, ADA_ROWS, tn), lambda l, j: (l, 0, j))],
        out_specs=pl.BlockSpec((1, d, tn), lambda l, j: (l, 0, j)),
        out_shape=jax.ShapeDtypeStruct((depth, d, n), F32),
        compiler_params=_params("parallel", "parallel"), name=name,
    )(c16, dmod16)


PACK_COLS = 1024
_ANY = pl.BlockSpec(memory_space=pl.ANY)
_VMEM = pl.BlockSpec(memory_space=pltpu.VMEM)


def _place():
    return lax.axis_index("x"), lax.axis_index("y"), lax.axis_index("c")


def _other_chips(x, y):
    return [(1 - x, y), (x, 1 - y), (1 - x, 1 - y)]


def _rcopy(src, dst, send_sems, recv_sems, k, peer):
    return pltpu.make_async_remote_copy(src_ref=src, dst_ref=dst, send_sem=send_sems.at[k], recv_sem=recv_sems.at[k],
                                        device_id=peer, device_id_type=MESH)


def _all_gather_8(v, name):
    r, cc = v.shape

    def body(v_ref, out_ref, send_sems, recv_sems, local_sem):
        x, y, c = _place()
        me = 4 * x + 2 * y + c
        mine = pltpu.make_async_copy(v_ref, out_ref.at[me], local_sem)
        mine.start()
        peers = []
        for k in range(1, N_DEV):
            px = 1 - x if k & 4 else x
            py = 1 - y if k & 2 else y
            pc = 1 - c if k & 1 else c
            peers.append((px, py, pc))
        sends = [_rcopy(v_ref, out_ref.at[me], send_sems, recv_sems, k, p) for k, p in enumerate(peers)]
        for cp in sends:
            cp.start()
        for k, (px, py, pc) in enumerate(peers):
            _rcopy(v_ref, out_ref.at[4 * px + 2 * py + pc], send_sems, recv_sems, k, (px, py, pc)).wait_recv()
        for cp in sends:
            cp.wait_send()
        mine.wait()

    return pl.pallas_call(
        body, in_specs=[_VMEM], out_specs=_VMEM, out_shape=jax.ShapeDtypeStruct((N_DEV, r, cc), v.dtype),
        scratch_shapes=[pltpu.SemaphoreType.DMA((N_DEV - 1,)), pltpu.SemaphoreType.DMA((N_DEV - 1,)), pltpu.SemaphoreType.DMA],
        compiler_params=pltpu.CompilerParams(vmem_limit_bytes=VMEM_LIMIT), name=name,
    )(v)


def _gather_chips(shards, name):
    n = len(shards)
    per = 2 * (N_CHIP - 1)

    def body(*refs):
        ins, outs, (send_sems, recv_sems) = refs[:n], refs[n:2 * n], refs[2 * n:]
        x, y, c = _place()
        chip = 2 * x + y
        chips = _other_chips(x, y)
        rows = [(pl.ds(c * (r.shape[0] // 2), r.shape[0] // 2), pl.ds((1 - c) * (r.shape[0] // 2), r.shape[0] // 2)) for r in ins]
        first = [_rcopy(ins[i].at[rows[i][0]], outs[i].at[chip, rows[i][0]], send_sems, recv_sems, per * i + j, (px, py, c))
                 for i in range(n) for j, (px, py) in enumerate(chips)]
        for cp in first:
            cp.start()
        passed = []
        for i in range(n):
            for j, (px, py) in enumerate(chips):
                landed = outs[i].at[2 * px + py, rows[i][0]]
                _rcopy(ins[i].at[rows[i][0]], landed, send_sems, recv_sems, per * i + j, (px, py, c)).wait_recv()
                fw = _rcopy(landed, landed, send_sems, recv_sems, per * i + N_CHIP - 1 + j, (x, y, 1 - c))
                fw.start()
                passed.append(fw)
        for i in range(n):
            for j, (px, py) in enumerate(chips):
                landed = outs[i].at[2 * px + py, rows[i][1]]
                _rcopy(landed, landed, send_sems, recv_sems, per * i + N_CHIP - 1 + j, (x, y, 1 - c)).wait_recv()
        for cp in first + passed:
            cp.wait_send()

    return pl.pallas_call(
        body, in_specs=[_ANY] * n, out_specs=[_ANY] * n,
        out_shape=[jax.ShapeDtypeStruct((N_CHIP,) + sh.shape, sh.dtype) for sh in shards],
        scratch_shapes=[pltpu.SemaphoreType.DMA((per * n,)), pltpu.SemaphoreType.DMA((per * n,))], name=name,
    )(*shards)


def _pair_exchange(gs, name):
    n = len(gs)

    def body(*refs):
        ins, outs, (send_sems, recv_sems) = refs[:n], refs[n:2 * n], refs[2 * n:]
        x, y, c = _place()
        copies = []
        for i in range(n):
            half = ins[i].shape[1] // 2
            copies.append(_rcopy(ins[i].at[:, pl.ds((1 - c) * half, half)], outs[i], send_sems, recv_sems, i, (x, y, 1 - c)))
        for cp in copies:
            cp.start()
        for cp in copies:
            cp.wait()

    return pl.pallas_call(
        body, in_specs=[_ANY] * n, out_specs=[_ANY] * n,
        out_shape=[jax.ShapeDtypeStruct((g.shape[0], g.shape[1] // 2, g.shape[2]), g.dtype) for g in gs],
        scratch_shapes=[pltpu.SemaphoreType.DMA((n,)), pltpu.SemaphoreType.DMA((n,))], name=name,
    )(*gs)


_ROW_TILES = (640, 512, 352, 256, 128, 64, 32, 16)


def _pair_sum(g, other, c_idx, name):
    n, half, cc = other.shape
    tr = _tile(half, _ROW_TILES)

    def body(c_ref, g_ref, o_ref, out_ref):
        out_ref[...] = (g_ref[...] + o_ref[...]).astype(out_ref.dtype)

    return pl.pallas_call(
        body,
        grid_spec=pltpu.PrefetchScalarGridSpec(
            num_scalar_prefetch=1, grid=(n, half // tr),
            in_specs=[pl.BlockSpec((None, None, tr, cc), lambda k, i, c_ref: (k, c_ref[0], i, 0)),
                      pl.BlockSpec((None, tr, cc), lambda k, i, c_ref: (k, i, 0))],
            out_specs=pl.BlockSpec((None, tr, cc), lambda k, i, c_ref: (k, i, 0))),
        out_shape=jax.ShapeDtypeStruct((n, half, cc), BF16),
        compiler_params=_params("parallel", "parallel"), name=name,
    )(c_idx, g.reshape(n, 2, half, cc), other)


def _chip_exchange(ps, name):
    n = len(ps)
    per = N_CHIP - 1

    def body(*refs):
        ins, outs, (send_sems, recv_sems) = refs[:n], refs[n:2 * n], refs[2 * n:]
        x, y, c = _place()
        chip = 2 * x + y
        chips = _other_chips(x, y)
        sends = [_rcopy(ins[i].at[2 * px + py], outs[i].at[chip], send_sems, recv_sems, per * i + j, (px, py, c))
                 for i in range(n) for j, (px, py) in enumerate(chips)]
        for cp in sends:
            cp.start()
        for i in range(n):
            for j, (px, py) in enumerate(chips):
                _rcopy(ins[i].at[chip], outs[i].at[2 * px + py], send_sems, recv_sems, per * i + j, (px, py, c)).wait_recv()
        for cp in sends:
            cp.wait_send()

    return pl.pallas_call(
        body, in_specs=[_ANY] * n, out_specs=[_ANY] * n, out_shape=[jax.ShapeDtypeStruct(p.shape, p.dtype) for p in ps],
        scratch_shapes=[pltpu.SemaphoreType.DMA((per * n,)), pltpu.SemaphoreType.DMA((per * n,))], name=name,
    )(*ps)


_HBM = pl.BlockSpec(memory_space=pltpu.HBM)
_SEM = pl.BlockSpec(memory_space=pltpu.SEMAPHORE)
_DATAFLOW = pltpu.SideEffectType.DATAFLOW_SIDE_EFFECTING


def _split_copies(srcs, lands, send_sems, recv_sems, mode, arriving):
    x, y, c = _place()
    chip = 2 * x + y
    out = []
    for i, (src, land) in enumerate(zip(srcs, lands)):
        if mode == "all":
            for k in range(1, N_DEV):
                px, py, pc = (1 - x if k & 4 else x), (1 - y if k & 2 else y), (1 - c if k & 1 else c)
                slot = 4 * px + 2 * py + pc if arriving else 2 * chip + c
                out.append(_rcopy(src, land.at[slot], send_sems, recv_sems, (N_DEV - 1) * i + k - 1, (px, py, pc)))
            continue
        if mode == "pair":
            half = src.shape[1] // 2
            out.append(_rcopy(src.at[:, pl.ds((1 - c) * half, half)], land, send_sems, recv_sems, i, (x, y, 1 - c)))
            continue
        if mode == "share":
            out.append(_rcopy(src, land.at[1 - c if arriving else c], send_sems, recv_sems, i, (x, y, 1 - c)))
            continue
        for j, (px, py) in enumerate(_other_chips(x, y)):
            there = 2 * px + py
            part = src.at[there] if mode == "slots" else src
            out.append(_rcopy(part, land.at[there if arriving else chip], send_sems, recv_sems, (N_CHIP - 1) * i + j, (px, py, c)))
    return out


def _land_shape(src, mode):
    if mode == "pair":
        return (src.shape[0], src.shape[1] // 2, src.shape[2])
    if mode == "all":
        return (N_DEV,) + src.shape
    if mode == "share":
        return (2,) + src.shape
    return (N_CHIP,) + (src.shape[1:] if mode == "slots" else src.shape)


def _send_start(srcs, mode, name):
    n = len(srcs)
    n_sem = {"pair": 1, "share": 1, "all": N_DEV - 1}.get(mode, N_CHIP - 1) * n
    lands = [lax.empty(_land_shape(s, mode), s.dtype) for s in srcs]

    def body(*refs):
        ins, zones, (send_sems, recv_sems) = refs[:n], refs[n:2 * n], refs[2 * n:2 * n + 2]
        for cp in _split_copies(ins, zones, send_sems, recv_sems, mode, False):
            cp.start()
        refs[-1][...] = jnp.zeros_like(refs[-1])

    hbm = lambda a: pltpu.HBM(a.shape, a.dtype)
    outs = pl.pallas_call(
        body, name=name, in_specs=[_HBM] * (2 * n),
        out_shape=(pltpu.SemaphoreType.DMA((n_sem,)), pltpu.SemaphoreType.DMA((n_sem,)), *[hbm(a) for a in srcs],
                   *[hbm(a) for a in lands], jax.ShapeDtypeStruct((SUBLANES, LANES), F32)),
        out_specs=(_SEM, _SEM, *[_HBM] * (2 * n), _VMEM), input_output_aliases={i: 2 + i for i in range(2 * n)},
        compiler_params=pltpu.CompilerParams(has_side_effects=_DATAFLOW),
    )(*[pltpu.with_memory_space_constraint(a, pltpu.HBM) for a in list(srcs) + lands])
    return (outs[0], outs[1], list(outs[2:2 + n]), list(outs[2 + n:2 + 2 * n])), outs[-1]


def _send_wait(state, after, mode, name):
    send_sems, recv_sems, srcs, lands = state
    n = len(srcs)

    def body(*refs):
        ins, zones, (send_s, recv_s) = refs[:n], refs[n:2 * n], refs[2 * n:2 * n + 2]
        for cp in _split_copies(ins, zones, send_s, recv_s, mode, True):
            cp.wait_send()
            cp.wait_recv()

    hbm = lambda a: pltpu.HBM(a.shape, a.dtype)
    outs = pl.pallas_call(
        body, name=name, in_specs=[_HBM] * (2 * n) + [_SEM, _SEM, _ANY],
        out_shape=tuple(hbm(a) for a in srcs + lands), out_specs=tuple([_HBM] * (2 * n)),
        input_output_aliases={i: i for i in range(2 * n)},
        compiler_params=pltpu.CompilerParams(has_side_effects=_DATAFLOW),
    )(*srcs, *lands, send_sems, recv_sems, after)
    return list(outs[:n]), list(outs[n:])


def _sum_lead(v, name):
    n, r, cc = v.shape
    tr = _tile(r, _ROW_TILES + (8,))

    def body(v_ref, o_ref):
        acc = v_ref[0].astype(F32)
        for k in range(1, n):
            acc = acc + v_ref[k].astype(F32)
        o_ref[...] = acc

    return pl.pallas_call(
        body, grid=(r // tr,), in_specs=[pl.BlockSpec((n, tr, cc), lambda i: (0, i, 0))],
        out_specs=pl.BlockSpec((tr, cc), lambda i: (i, 0)), out_shape=jax.ShapeDtypeStruct((r, cc), F32),
        compiler_params=_params("parallel"), name=name,
    )(v)


def _chip_sum(arrived, mine, chip_idx, name):
    n, r, cc = arrived.shape
    tr = _tile(r, _ROW_TILES)

    def body(chip_ref, a_ref, m_ref, o_ref):
        acc = jnp.zeros((tr, cc), F32)
        for k in range(n):
            acc = acc + jnp.where(chip_ref[0] == k, m_ref[...], a_ref[k]).astype(F32)
        o_ref[...] = acc

    return pl.pallas_call(
        body,
        grid_spec=pltpu.PrefetchScalarGridSpec(
            num_scalar_prefetch=1, grid=(r // tr,),
            in_specs=[pl.BlockSpec((n, tr, cc), lambda i, chip_ref: (0, i, 0)),
                      pl.BlockSpec((None, tr, cc), lambda i, chip_ref: (chip_ref[0], i, 0))],
            out_specs=pl.BlockSpec((tr, cc), lambda i, chip_ref: (i, 0))),
        out_shape=jax.ShapeDtypeStruct((r, cc), F32), compiler_params=_params("parallel"), name=name,
    )(chip_idx, arrived, mine)


def _pair_share(reds, name):
    n = len(reds)

    def body(*refs):
        ins, outs, (send_sems, recv_sems) = refs[:n], refs[n:2 * n], refs[2 * n:]
        x, y, c = _place()
        copies = [_rcopy(ins[i], outs[i].at[c], send_sems, recv_sems, i, (x, y, 1 - c)) for i in range(n)]
        for cp in copies:
            cp.start()
        for i in range(n):
            _rcopy(ins[i], outs[i].at[1 - c], send_sems, recv_sems, i, (x, y, 1 - c)).wait_recv()
        for cp in copies:
            cp.wait_send()

    return pl.pallas_call(
        body, in_specs=[_ANY] * n, out_specs=[_ANY] * n, out_shape=[jax.ShapeDtypeStruct((2,) + r.shape, r.dtype) for r in reds],
        scratch_shapes=[pltpu.SemaphoreType.DMA((n,)), pltpu.SemaphoreType.DMA((n,))], name=name,
    )(*reds)


def _pack(arrs, rows_multiple, dtype):
    flat = jnp.concatenate([a.reshape(-1).astype(dtype) for a in arrs])
    unit = rows_multiple * PACK_COLS
    total = -(-flat.shape[0] // unit) * unit
    return jnp.pad(flat, (0, total - flat.shape[0])).reshape(-1, PACK_COLS)


def _unpack(buf, shapes):
    lead = buf.shape[:-2]
    flat = buf.reshape(*lead, -1)
    out, off = [], 0
    for shp in shapes:
        n = 1
        for s in shp:
            n *= s
        out.append(flat[..., off:off + n].reshape(*lead, *shp))
        off += n
    return out


def _join_shards(parts, axis):
    moved = jnp.moveaxis(parts, 0, axis)
    shp = list(moved.shape)
    shp[axis:axis + 2] = [shp[axis] * shp[axis + 1]]
    return moved.reshape(shp)


def _my_shard(full, axis, chip):
    n = full.shape[axis] // N_CHIP
    return lax.dynamic_slice_in_dim(full, chip * n, n, axis)


SMALL = {"norm_g": 2, "ffn_conv_w": 2, "rg_conv_w": 2, "gla_w_alpha": 2, "gla_b_alpha": 1, "gla_norm_g": 1,
         "ada_b": None, "ffn_conv_b": None, "rg_conv_b": None, "rg_ba": None, "rg_bx": None, "rg_lambda": None}
BIG = {"rg_w_in": True, "rg_wa": False, "rg_wx": False, "rg_w_out": False, "ffn_w_up": True, "ffn_w_down": False,
       "gla_w_in": True, "gla_w_out": False}
WEIGHTS = ["ada_w", "ada_b", "norm_g", "ffn_w_up", "ffn_conv_w", "ffn_conv_b", "ffn_w_down", "rg_w_in", "rg_conv_w", "rg_conv_b",
           "rg_wa", "rg_ba", "rg_wx", "rg_bx", "rg_lambda", "rg_w_out", "gla_w_in", "gla_w_alpha", "gla_b_alpha", "gla_norm_g",
           "gla_w_out"]


def kernel(x, c, ada_w, ada_b, norm_g, ffn_w_up, ffn_conv_w, ffn_conv_b, ffn_w_down, rg_w_in, rg_conv_w, rg_conv_b, rg_wa, rg_ba, rg_wx, rg_bx, rg_lambda, rg_w_out, gla_w_in, gla_w_alpha, gla_b_alpha, gla_norm_g, gla_w_out, loss_target, m_ada_w, m_ada_b, m_norm_g, m_ffn_w_up, m_ffn_conv_w, m_ffn_conv_b, m_ffn_w_down, m_rg_w_in, m_rg_conv_w, m_rg_conv_b, m_rg_wa, m_rg_ba, m_rg_wx, m_rg_bx, m_rg_lambda, m_rg_w_out, m_gla_w_in, m_gla_w_alpha, m_gla_b_alpha, m_gla_norm_g, m_gla_w_out, v_ada_w, v_ada_b, v_norm_g, v_ffn_w_up, v_ffn_conv_w, v_ffn_conv_b, v_ffn_w_down, v_rg_w_in, v_rg_conv_w, v_rg_conv_b, v_rg_wa, v_rg_ba, v_rg_wx, v_rg_bx, v_rg_lambda, v_rg_w_out, v_gla_w_in, v_gla_w_alpha, v_gla_b_alpha, v_gla_norm_g, v_gla_w_out):
    wts = dict(ada_w=ada_w, ada_b=ada_b, norm_g=norm_g, ffn_w_up=ffn_w_up, ffn_conv_w=ffn_conv_w, ffn_conv_b=ffn_conv_b,
               ffn_w_down=ffn_w_down, rg_w_in=rg_w_in, rg_conv_w=rg_conv_w, rg_conv_b=rg_conv_b, rg_wa=rg_wa, rg_ba=rg_ba,
               rg_wx=rg_wx, rg_bx=rg_bx, rg_lambda=rg_lambda, rg_w_out=rg_w_out, gla_w_in=gla_w_in, gla_w_alpha=gla_w_alpha,
               gla_b_alpha=gla_b_alpha, gla_norm_g=gla_norm_g, gla_w_out=gla_w_out)
    mom1 = dict(ada_w=m_ada_w, ada_b=m_ada_b, norm_g=m_norm_g, ffn_w_up=m_ffn_w_up, ffn_conv_w=m_ffn_conv_w,
                ffn_conv_b=m_ffn_conv_b, ffn_w_down=m_ffn_w_down, rg_w_in=m_rg_w_in, rg_conv_w=m_rg_conv_w,
                rg_conv_b=m_rg_conv_b, rg_wa=m_rg_wa, rg_ba=m_rg_ba, rg_wx=m_rg_wx, rg_bx=m_rg_bx, rg_lambda=m_rg_lambda,
                rg_w_out=m_rg_w_out, gla_w_in=m_gla_w_in, gla_w_alpha=m_gla_w_alpha, gla_b_alpha=m_gla_b_alpha,
                gla_norm_g=m_gla_norm_g, gla_w_out=m_gla_w_out)
    mom2 = dict(ada_w=v_ada_w, ada_b=v_ada_b, norm_g=v_norm_g, ffn_w_up=v_ffn_w_up, ffn_conv_w=v_ffn_conv_w,
                ffn_conv_b=v_ffn_conv_b, ffn_w_down=v_ffn_w_down, rg_w_in=v_rg_w_in, rg_conv_w=v_rg_conv_w,
                rg_conv_b=v_rg_conv_b, rg_wa=v_rg_wa, rg_ba=v_rg_ba, rg_wx=v_rg_wx, rg_bx=v_rg_bx, rg_lambda=v_rg_lambda,
                rg_w_out=v_rg_w_out, gla_w_in=v_gla_w_in, gla_w_alpha=v_gla_w_alpha, gla_b_alpha=v_gla_b_alpha,
                gla_norm_g=v_gla_norm_g, gla_w_out=v_gla_w_out)
    xi, yi, ci = _place()
    chip, me = 2 * xi + yi, 4 * xi + 2 * yi + ci
    d = x.shape[-1]
    depth = ada_w.shape[0]
    n_ada = ada_w.shape[-1]
    sharded_small = [k for k, ax in SMALL.items() if ax is not None]

    sm = _all_gather_8(_pack([c] + [wts[k] for k in sharded_small], SUBLANES, F32), "gather_small")
    c_all = sm[:, 0, :]
    parts = _unpack(sm[0::2], [c.shape] + [wts[k].shape for k in sharded_small])[1:]
    full = {k: _join_shards(p, SMALL[k]) for k, p in zip(sharded_small, parts)}
    for k, ax in SMALL.items():
        if ax is None:
            full[k] = wts[k]

    c16 = jnp.pad(c_all, ((0, ADA_ROWS - N_DEV), (0, 0)))
    ada_b_mine = lax.dynamic_slice_in_dim(ada_b, chip * n_ada, n_ada, 1)[:, None, :]
    mod_cols = _ada_fwd(c16, ada_w, ada_b_mine, "ada_fwd")
    mod_all = _all_gather_8(mod_cols.reshape(-1, PACK_COLS), "gather_mod")[0::2].reshape(N_CHIP, depth, ADA_ROWS, n_ada)
    mod = jnp.swapaxes(lax.dynamic_index_in_dim(mod_all, me, 2, keepdims=False), 0, 1).reshape(depth, 6, d)

    items = [(k, l) for k in BIG for l in range(wts[k].shape[0])]
    stage_of = lambda k, l: "rg" if k.startswith("rg_") else ("ffn0" if (k.startswith("ffn_") and l == 0) else "l1")
    staged = {st: [it for it in items if stage_of(*it) == st] for st in ("rg", "ffn0", "l1")}
    staged["gla"] = [it for it in staged["l1"] if it[0].startswith("gla_")]
    staged["ffn1"] = [it for it in staged["l1"] if it[0].startswith("ffn_")]
    staged["l1"] = staged["gla"] + staged["ffn1"]
    shard = lambda k, l: wts[k][l].reshape(-1, wts[k].shape[-1]).astype(BF16)
    own = lambda got, mine: [lax.dynamic_update_index_in_dim(g, m, chip, 0) for g, m in zip(got, mine)]
    rows_joined = lambda v: v.reshape(-1, v.shape[-1])

    def placed(its, slots):
        out = {"ffn_w_up": {}, "ffn_w_down": {}}
        for (k, l), v in zip(its, slots):
            if k == "ffn_w_up":
                out[k][l] = v
            elif k == "ffn_w_down":
                out[k][l] = rows_joined(v)
            elif k in ("rg_wa", "rg_wx"):
                out[k] = _slots_to_block_rows(v, RG_BLOCKS)
            elif k == "gla_w_in":
                out[k] = _from_col_slots(v, "gla_w_in_join")
            else:
                out[k] = v if BIG[k] else rows_joined(v)
        return out

    after_mod = (mod[0, 0, 0] * 0.0).astype(BF16)
    sh_rg = [shard(k, l) + after_mod for k, l in staged["rg"]]
    local = {k: (v if k in ("norm_g", "ffn_conv_w", "ffn_conv_b") else v[0]) for k, v in full.items()}
    local.update(placed(staged["rg"], own(_gather_chips(sh_rg, "gather_weights_rg"), sh_rg)))
    sh_late, flying = {}, {}
    after_rg = (local["rg_w_out"][0, 0].astype(F32) * 0.0).astype(BF16)
    sh_late["ffn0"] = [shard(k, l) + after_rg for k, l in staged["ffn0"]]
    flying["ffn0"], tok = _send_start(sh_late["ffn0"], "whole", "weights_ffn0_start")
    for stage in ("gla", "ffn1"):
        sh_late[stage] = [shard(k, l) + tok[0, 0].astype(BF16) for k, l in staged[stage]]
        flying[stage], tok = _send_start(sh_late[stage], "whole", f"weights_{stage}_start")
    mod = mod + tok[0, 0]

    def fetch(stage, after):
        mine, got = _send_wait(flying[stage], after, "whole", f"weights_{stage}_wait")
        return placed(staged[stage], own(got, mine))

    c_idx = ci.reshape(1).astype(jnp.int32)
    chip_idx = chip.reshape(1).astype(jnp.int32)
    gslots, paired, psums, sent, started, sharing = {}, {}, {}, {}, {}, {}
    before = {"ffn0": "l1", "rg": "ffn0"}

    def grad_slots(gr, k, l):
        g = gr[k][l] if k in ("ffn_w_up", "ffn_w_down") else gr[k]
        if k in ("rg_wa", "rg_wx"):
            return _block_rows_to_slots(g)
        if k == "gla_w_in":
            return _col_slots(g, "gla_w_in_grad_slots")
        return g if BIG[k] else g.reshape(N_CHIP, -1, g.shape[-1])

    def done(stage, gr, after):
        gslots[stage] = [grad_slots(gr, k, l) for k, l in staged[stage]]
        paired[stage], token = _send_start(gslots[stage], "pair", f"grads_{stage}_pair_start")
        if stage not in before:
            return token[0, 0]
        prev = before[stage]
        mine, arrived = _send_wait(sent[prev], after, "slots", f"grads_{prev}_wait")
        halves = [_chip_sum(a, m, chip_idx, "grads_chip_sum_%s%d" % it) for it, a, m in zip(staged[prev], arrived, mine)]
        sharing[prev], shared = _send_start(halves, "share", f"grads_{prev}_share_start")
        return token[0, 0] + shared[0, 0]

    def later(stage, after):
        mine, theirs = _send_wait(paired[stage], after, "pair", f"grads_{stage}_pair_wait")
        psums[stage] = [_pair_sum(g, t, c_idx, f"grads_pair_sum_{k}{l}") for (k, l), g, t in zip(staged[stage], mine, theirs)]
        sent[stage], started[stage] = _send_start(psums[stage], "slots", f"grads_{stage}_start")
        return started[stage][0, 0]

    cols, grad_x, gr = _local_step(x[0], loss_target[0], mod, local, fetch, done, later)
    loss = lax.psum(0.5 * jnp.sum(cols) / d, ("x", "y", "c"))

    small_names = [k for k in SMALL if k != "ada_b"]
    small_flying, small_sent = _send_start([_pack([gr[k] for k in small_names] + [gr["mod"]], SUBLANES, F32)], "all",
                                           "grads_small_start")
    small_shapes = [full[k].shape for k in small_names] + [(depth, 6 * d)]
    delta, new_m, new_v = {}, {}, {}
    grads = {}

    def update_all(reduced, dep):
        last = None
        for k in BIG:
            gs_k = [g.reshape(-1, g.shape[-1]) for (k2, _), g in sorted(reduced.items()) if k2 == k]
            if gs_k:
                last = update(k, gs_k, dep)
        return last

    def update(k, gs_k, dep=None):
        shp = wts[k].shape
        if k == "gla_w_in":
            view, back = (lambda a: jnp.swapaxes(a, 1, 2)), (lambda o: jnp.swapaxes(o, 1, 2))
            gs_k = [g.T for g in gs_k]
        else:
            view, back = (lambda a: a.reshape(a.shape[0], -1, a.shape[-1])), (lambda o: o.reshape(shp))
        outs = _adamw(view(wts[k]), gs_k, view(mom1[k]), view(mom2[k]), "adamw_" + k, dep)
        grads[k], delta[k], new_m[k], new_v[k] = (back(o) for o in outs)
        return new_v[k]

    later("rg", small_sent)
    reduced = {}
    for st in ("l1", "ffn0"):
        halves, lands = _send_wait(sharing[st], grad_x, "share", f"grads_{st}_share_wait")
        reduced.update({it: lax.dynamic_update_index_in_dim(l, h, ci, 0) for it, h, l in zip(staged[st], halves, lands)})
    done_late = update_all(reduced, started["rg"])
    (small_mine,), (gs,) = _send_wait(small_flying, done_late, "all", "grads_small_wait")
    gs = lax.dynamic_update_index_in_dim(gs, small_mine, 2 * chip + ci, 0)
    *small_sum, g_ada_b = _unpack(_sum_lead(gs, "sum_small_grads"), small_shapes)
    grads.update(zip(small_names, small_sum))
    grads["ada_b"] = g_ada_b
    for k in sharded_small:
        grads[k] = _my_shard(grads[k], SMALL[k], chip)
    dmod_all = _unpack(gs, small_shapes)[-1].reshape(N_DEV, depth, N_CHIP, n_ada)
    dmod_mine = jnp.swapaxes(lax.dynamic_index_in_dim(dmod_all, chip, 2, keepdims=False), 0, 1)
    g_ada_w = _ada_bwd(c16, jnp.pad(dmod_mine, ((0, 0), (0, ADA_ROWS - N_DEV), (0, 0))), "ada_bwd")
    update("ada_w", g_ada_w)
    small_shard_shapes = [wts[k].shape for k in SMALL]
    packed = [_pack([src[k] for k in SMALL], SUBLANES, F32) for src in (wts, grads, mom1, mom2)]
    outs = _adamw(packed[0][None], [packed[1]], packed[2][None], packed[3][None], "adamw_small")
    for dst, o in zip((delta, new_m, new_v), outs[1:]):
        for k, a in zip(SMALL, _unpack(o[0], small_shard_shapes)):
            dst[k] = a
    mine, arrived = _send_wait(sent["rg"], outs[3], "slots", "grads_rg_wait")
    halves = [_chip_sum(a, m, chip_idx, "grads_chip_sum_%s%d" % it) for it, a, m in zip(staged["rg"], arrived, mine)]
    shared = _pair_share(halves, "grads_pair_share_rg")
    update_all({it: lax.dynamic_update_index_in_dim(s2, h, ci, 0) for it, s2, h in zip(staged["rg"], shared, halves)}, None)

    return (loss, grad_x[None], *[grads[k] for k in WEIGHTS], *[delta[k] for k in WEIGHTS], *[new_m[k] for k in WEIGHTS],
            *[new_v[k] for k in WEIGHTS])
```

```python
import jax
import jax.numpy as jnp
from jax import lax
from jax.experimental import pallas as pl
from jax.experimental.pallas import tpu as pltpu

F32 = jnp.float32
BF16 = jnp.bfloat16
MXU_DTYPE = BF16

EPS = 1e-6
RG_C = 8.0
RG_BLOCKS = 4
RG_CONV = 4
GLA_HEADS = 4
GLA_TAU = 16.0
GLA_CHUNK = 64
GLA_RANK = 16
FFN_CONV = 3
ADAM_LR = 0.001
ADAM_B1 = 0.9
ADAM_B2 = 0.999
ADAM_EPS = 1e-08
ADAM_WD = 0.01
ADAM_STEP = 10

LANES = 128
SUBLANES = 8
VMEM_LIMIT = 56 * 1024 * 1024
CB = 256
MESH = pl.DeviceIdType.MESH
N_DEV = 8
N_CHIP = 4


def _params(*sem):
    return pltpu.CompilerParams(dimension_semantics=sem, vmem_limit_bytes=VMEM_LIMIT)


def _tile(dim, prefs):
    for p in prefs:
        if dim % p == 0:
            return p
    return dim


def _dot(a, b, dims):
    return lax.dot_general(a.astype(MXU_DTYPE), b.astype(MXU_DTYPE), (dims, ((), ())), preferred_element_type=F32)


def _dot_nn(a, b):
    return _dot(a, b, ((1,), (0,)))


def _dot_nt(a, b):
    return _dot(a, b, ((1,), (1,)))


def _dot_tn(a, b):
    return _dot(a, b, ((0,), (0,)))


def _mm(a, b, *, ta=False, tb=False, a_parts=1, b_parts=1, w_slots=1, out_slots=1, out_dtype=F32, tm_max=1408, name):
    if ta:
        k_dim, m_dim = a.shape
        n_dim = b.shape[-1] * b_parts
    else:
        m_dim, k_dim = a.shape[-2], a.shape[-1] * a_parts
        n_dim = b.shape[-2] if tb else b.shape[-1] * w_slots
    n_unit = n_dim // max(b_parts, out_slots, 1 if tb else w_slots)
    k_unit = k_dim // max(a_parts, w_slots if tb else 1)
    tm = _tile(m_dim, tuple(t for t in (1024, 1408, 512, 256, 128) if t <= max(tm_max, 128)))
    tn = _tile(n_unit, (1024, 1408, 896, 512, 256, 128))
    tk = _tile(k_unit, (1024, 1408, 896, 512, 256, 128))
    nk = k_dim // tk
    dims = ((0 if ta else 1,), (1 if tb else 0,))

    def spec(shape, parts, total, tile, col_grid, row_grid):
        per = total // parts // tile

        def index(i, j, k):
            g = {"i": i, "j": j, "k": k}
            col, row = g[col_grid], g[row_grid]
            return (row, col) if parts == 1 else (col // per, row, col % per)

        return pl.BlockSpec(shape if parts == 1 else (None,) + shape, index)

    def body(a_ref, b_ref, o_ref, *acc):
        if nk == 1:
            o_ref[...] = _dot(a_ref[...], b_ref[...], dims).astype(o_ref.dtype)
            return
        acc_ref, k = acc[0], pl.program_id(2)

        @pl.when(k == 0)
        def _():
            acc_ref[...] = jnp.zeros_like(acc_ref)

        acc_ref[...] += _dot(a_ref[...], b_ref[...], dims)

        @pl.when(k == nk - 1)
        def _():
            o_ref[...] = acc_ref[...].astype(o_ref.dtype)

    if ta:
        a_spec = spec((tk, tm), 1, m_dim, tm, "i", "k")
        b_spec = spec((tk, tn), b_parts, n_dim, tn, "j", "k")
    elif tb:
        a_spec = spec((tm, tk), a_parts, k_dim, tk, "k", "i")
        b_spec = spec((tn, tk), w_slots, k_dim, tk, "k", "j")
    else:
        a_spec = spec((tm, tk), a_parts, k_dim, tk, "k", "i")
        b_spec = spec((tk, tn), w_slots, n_dim, tn, "j", "k")
    out_shape = (m_dim, n_dim) if out_slots == 1 else (out_slots, m_dim, n_dim // out_slots)
    return pl.pallas_call(
        body,
        grid=(m_dim // tm, n_dim // tn, nk),
        in_specs=[a_spec, b_spec],
        out_specs=spec((tm, tn), out_slots, n_dim, tn, "j", "i"),
        out_shape=jax.ShapeDtypeStruct(out_shape, out_dtype),
        scratch_shapes=[pltpu.VMEM((tm, tn), F32)] if nk > 1 else [],
        compiler_params=_params("parallel", "parallel", "arbitrary"),
        name=name,
    )(a, b)


ROW_TILES = (1024, 512)


def _row_specs(s, d, ts):
    return pl.BlockSpec((ts, d), lambda i: (i, 0)), pl.BlockSpec((1, d), lambda i: (0, 0))


def _norm_mod_fwd(x, g, sc, sh, name):
    s, d = x.shape
    ts = _tile(s, ROW_TILES)

    def body(x_ref, g_ref, sc_ref, sh_ref, h_ref):
        xv = x_ref[...]
        r = lax.rsqrt(jnp.mean(xv * xv, axis=-1, keepdims=True) + EPS)
        h_ref[...] = (((xv * r) * g_ref[...]) * (1.0 + sc_ref[...]) + sh_ref[...]).astype(h_ref.dtype)

    row, vec = _row_specs(s, d, ts)
    return pl.pallas_call(
        body, grid=(s // ts,), in_specs=[row, vec, vec, vec], out_specs=row,
        out_shape=jax.ShapeDtypeStruct((s, d), MXU_DTYPE), compiler_params=_params("parallel"), name=name,
    )(x, g, sc, sh)


def _norm_mod_bwd(dh, x, g, sc, dres, name):
    s, d = x.shape
    ts = _tile(s, ROW_TILES)

    def body(dh_ref, x_ref, g_ref, sc_ref, dres_ref, dx_ref, dg_ref, dsc_ref, dsh_ref, acc_ref):
        i = pl.program_id(0)

        @pl.when(i == 0)
        def _():
            acc_ref[...] = jnp.zeros_like(acc_ref)

        xv, dhv = x_ref[...], dh_ref[...]
        r = lax.rsqrt(jnp.mean(xv * xv, axis=-1, keepdims=True) + EPS)
        n = xv * r
        acc_ref[0:1, :] += jnp.sum(dhv * n, axis=0, keepdims=True)
        acc_ref[1:2, :] += jnp.sum(dhv, axis=0, keepdims=True)
        dn = dhv * ((1.0 + sc_ref[...]) * g_ref[...])
        dx_ref[...] = dres_ref[...] + r * (dn - n * jnp.mean(dn * n, axis=-1, keepdims=True))
        dg_ref[...] = (1.0 + sc_ref[...]) * acc_ref[0:1, :]
        dsc_ref[...] = g_ref[...] * acc_ref[0:1, :]
        dsh_ref[...] = acc_ref[1:2, :]

    row, vec = _row_specs(s, d, ts)
    vshape = jax.ShapeDtypeStruct((1, d), F32)
    return pl.pallas_call(
        body, grid=(s // ts,), in_specs=[row, row, vec, vec, row], out_specs=[row, vec, vec, vec],
        out_shape=[jax.ShapeDtypeStruct((s, d), F32), vshape, vshape, vshape],
        scratch_shapes=[pltpu.VMEM((SUBLANES, d), F32)], compiler_params=_params("arbitrary"), name=name,
    )(dh, x, g, sc, dres)


def _post_norm_fwd(x, y, g, gt, g2, sc, sh, name):
    s, d = x.shape
    ts = _tile(s, ROW_TILES)

    def body(x_ref, y_ref, g_ref, gt_ref, g2_ref, sc_ref, sh_ref, o_ref, h_ref):
        yv = y_ref[...]
        r = lax.rsqrt(jnp.mean(yv * yv, axis=-1, keepdims=True) + EPS)
        xn = x_ref[...] + gt_ref[...] * ((yv * r) * g_ref[...])
        o_ref[...] = xn
        r2 = lax.rsqrt(jnp.mean(xn * xn, axis=-1, keepdims=True) + EPS)
        h_ref[...] = (((xn * r2) * g2_ref[...]) * (1.0 + sc_ref[...]) + sh_ref[...]).astype(h_ref.dtype)

    row, vec = _row_specs(s, d, ts)
    return pl.pallas_call(
        body, grid=(s // ts,), in_specs=[row, row] + [vec] * 5, out_specs=[row, row],
        out_shape=[jax.ShapeDtypeStruct((s, d), F32), jax.ShapeDtypeStruct((s, d), MXU_DTYPE)],
        compiler_params=_params("parallel"), name=name,
    )(x, y, g, gt, g2, sc, sh)


def _post_bwd(dxn, y, g, gt, name):
    s, d = y.shape
    ts = _tile(s, ROW_TILES)

    def body(dxn_ref, y_ref, g_ref, gt_ref, dy_ref, dg_ref, dgt_ref, acc_ref):
        i = pl.program_id(0)

        @pl.when(i == 0)
        def _():
            acc_ref[...] = jnp.zeros_like(acc_ref)

        yv, dv = y_ref[...], dxn_ref[...]
        r = lax.rsqrt(jnp.mean(yv * yv, axis=-1, keepdims=True) + EPS)
        n = yv * r
        acc_ref[0:1, :] += jnp.sum(dv * n, axis=0, keepdims=True)
        dn = dv * (gt_ref[...] * g_ref[...])
        dy_ref[...] = (r * (dn - n * jnp.mean(dn * n, axis=-1, keepdims=True))).astype(dy_ref.dtype)
        dg_ref[...] = gt_ref[...] * acc_ref[0:1, :]
        dgt_ref[...] = g_ref[...] * acc_ref[0:1, :]

    row, vec = _row_specs(s, d, ts)
    vshape = jax.ShapeDtypeStruct((1, d), F32)
    return pl.pallas_call(
        body, grid=(s // ts,), in_specs=[row, row, vec, vec], out_specs=[row, vec, vec],
        out_shape=[jax.ShapeDtypeStruct((s, d), MXU_DTYPE), vshape, vshape],
        scratch_shapes=[pltpu.VMEM((SUBLANES, d), F32)], compiler_params=_params("arbitrary"), name=name,
    )(dxn, y, g, gt)


def _norm_post_bwd(dh, x, g, sc, dres, y, gp, gt, name):
    s, d = x.shape
    ts = _tile(s, (512,))

    def body(dh_ref, x_ref, g_ref, sc_ref, dres_ref, y_ref, gp_ref, gt_ref,
             dx_ref, dy_ref, dg_ref, dsc_ref, dsh_ref, dgp_ref, dgt_ref, acc_ref):
        i = pl.program_id(0)

        @pl.when(i == 0)
        def _():
            acc_ref[...] = jnp.zeros_like(acc_ref)

        xv, dhv = x_ref[...], dh_ref[...]
        r = lax.rsqrt(jnp.mean(xv * xv, axis=-1, keepdims=True) + EPS)
        n = xv * r
        acc_ref[0:1, :] += jnp.sum(dhv * n, axis=0, keepdims=True)
        acc_ref[1:2, :] += jnp.sum(dhv, axis=0, keepdims=True)
        dn = dhv * ((1.0 + sc_ref[...]) * g_ref[...])
        dx = dres_ref[...] + r * (dn - n * jnp.mean(dn * n, axis=-1, keepdims=True))
        dx_ref[...] = dx
        yv = y_ref[...]
        ry = lax.rsqrt(jnp.mean(yv * yv, axis=-1, keepdims=True) + EPS)
        ny = yv * ry
        acc_ref[2:3, :] += jnp.sum(dx * ny, axis=0, keepdims=True)
        dny = dx * (gt_ref[...] * gp_ref[...])
        dy_ref[...] = (ry * (dny - ny * jnp.mean(dny * ny, axis=-1, keepdims=True))).astype(dy_ref.dtype)
        dg_ref[...] = (1.0 + sc_ref[...]) * acc_ref[0:1, :]
        dsc_ref[...] = g_ref[...] * acc_ref[0:1, :]
        dsh_ref[...] = acc_ref[1:2, :]
        dgp_ref[...] = gt_ref[...] * acc_ref[2:3, :]
        dgt_ref[...] = gp_ref[...] * acc_ref[2:3, :]

    row, vec = _row_specs(s, d, ts)
    vshape = jax.ShapeDtypeStruct((1, d), F32)
    return pl.pallas_call(
        body, grid=(s // ts,), in_specs=[row, row, vec, vec, row, row, vec, vec], out_specs=[row, row] + [vec] * 5,
        out_shape=[jax.ShapeDtypeStruct((s, d), F32), jax.ShapeDtypeStruct((s, d), MXU_DTYPE)] + [vshape] * 5,
        scratch_shapes=[pltpu.VMEM((SUBLANES, d), F32)], compiler_params=_params("arbitrary"), name=name,
    )(dh, x, g, sc, dres, y, gp, gt)


def _post_loss(x, y, g, gt, tgt, name):
    s, d = x.shape
    ts = _tile(s, ROW_TILES)

    def body(x_ref, y_ref, g_ref, gt_ref, t_ref, col_ref, dx_ref):
        i = pl.program_id(0)

        @pl.when(i == 0)
        def _():
            col_ref[...] = jnp.zeros_like(col_ref)

        yv = y_ref[...]
        r = lax.rsqrt(jnp.mean(yv * yv, axis=-1, keepdims=True) + EPS)
        e = (x_ref[...] + gt_ref[...] * ((yv * r) * g_ref[...])) - t_ref[...]
        col_ref[...] += jnp.sum(e * e, axis=0, keepdims=True)
        dx_ref[...] = e * (1.0 / d)

    row, vec = _row_specs(s, d, ts)
    return pl.pallas_call(
        body, grid=(s // ts,), in_specs=[row, row, vec, vec, row], out_specs=[vec, row],
        out_shape=[jax.ShapeDtypeStruct((1, d), F32), jax.ShapeDtypeStruct((s, d), F32)],
        compiler_params=_params("arbitrary"), name=name,
    )(x, y, g, gt, tgt)


_GELU_C = 0.7978845608028654
_GELU_A = 0.044715


def _gelu(x):
    t = jnp.tanh(_GELU_C * (x + _GELU_A * x * x * x))
    return 0.5 * x * (1.0 + t), t


def _gelu_grad(x, t):
    return 0.5 * (1.0 + t) + 0.5 * x * (1.0 - t * t) * (_GELU_C * (1.0 + 3.0 * _GELU_A * x * x))


def _sigmoid(x):
    return 1.0 / (1.0 + jnp.exp(-x))


def _log1p_pos(y):
    u = 1.0 + y
    return jnp.where(u == 1.0, y, jnp.log(u) * (y / jnp.where(u == 1.0, 1.0, u - 1.0)))


def _softplus(x):
    return jnp.maximum(x, 0.0) + _log1p_pos(jnp.exp(-jnp.abs(x)))


def _one_minus_sq_exp(x, ex):
    z = 2.0 * x
    series = -z * (1.0 + z * (1.0 / 2 + z * (1.0 / 6 + z * (1.0 / 24 + z * (1.0 / 120)))))
    return jnp.where(z > -0.05, series, 1.0 - ex * ex)


SLAB = 16


def _cat(a, b):
    return jnp.concatenate([a, b], axis=1)


def _pair_specs(shape, nb, index):
    return [pl.BlockSpec(shape, lambda j, t: index(j, t) + (j,)), pl.BlockSpec(shape, lambda j, t: index(j, t) + (j + nb,))]


def _halo_row(ts, time_of):
    return lambda j, t: (jnp.maximum(time_of(t) * (ts // SUBLANES) - 1, 0),)


def _rows_from(groups, k):
    row = lax.broadcasted_iota(jnp.int32, groups[0].shape, 0)
    turned = [pltpu.roll(g, SUBLANES - k, axis=0) for g in groups]
    return [jnp.where(row < SUBLANES - k, lo, hi) for lo, hi in zip(turned[:-1], turned[1:])]


def _ffn_mid_fwd(p, cw, cb, name):
    s, f2 = p.shape
    ts = _tile(s, (1024, 512))
    nb, nt = f2 // (2 * CB), s // ts
    n_grp = SLAB // SUBLANES

    def body(pg_ref, pv_ref, hg_ref, hv_ref, cwg_ref, cwv_ref, cbg_ref, cbv_ref, a_ref, ga_ref, gb_ref):
        t = pl.program_id(1)
        cwv, bias = _cat(cwg_ref[...], cwv_ref[...]), _cat(cbg_ref[...], cbv_ref[...])
        w0, w1, w2 = cwv[0:1], cwv[1:2], cwv[2:3]

        def slab(before, cur, r0):
            pm2, pm1 = _rows_from([before] + cur, SUBLANES - 2), _rows_from([before] + cur, SUBLANES - 1)
            u = jnp.concatenate([bias + w0 * pm2[i] + w1 * pm1[i] + w2 * cur[i] for i in range(n_grp)], axis=0)
            g, v = u[:, :CB], u[:, CB:]
            gel, th = _gelu(g)
            rows = pl.ds(r0, SLAB)
            a_ref[rows, :] = (gel * v).astype(a_ref.dtype)
            ga_ref[rows, :] = gel.astype(ga_ref.dtype)
            gb_ref[rows, :] = (v * _gelu_grad(g, th)).astype(gb_ref.dtype)

        def pieces(rows):
            blk = _cat(pg_ref[rows, :], pv_ref[rows, :])
            return [blk[i * SUBLANES:(i + 1) * SUBLANES] for i in range(blk.shape[0] // SUBLANES)]

        slab(jnp.where(t > 0, _cat(hg_ref[...], hv_ref[...]), 0.0), pieces(pl.ds(0, SLAB)), 0)

        def loop(i, carry):
            r0 = pl.multiple_of(i * SLAB, SLAB)
            got = pieces(pl.ds(pl.multiple_of(r0 - SUBLANES, SUBLANES), SLAB + SUBLANES))
            slab(got[0], got[1:], r0)
            return carry

        lax.fori_loop(1, ts // SLAB, loop, 0, unroll=2)

    fwd = lambda t: t
    out = pl.BlockSpec((ts, CB), lambda j, t: (t, j))
    shape = jax.ShapeDtypeStruct((s, f2 // 2), MXU_DTYPE)
    return pl.pallas_call(
        body, grid=(nb, nt),
        in_specs=(_pair_specs((ts, CB), nb, lambda j, t: (t,)) + _pair_specs((SUBLANES, CB), nb, _halo_row(ts, fwd))
                  + _pair_specs((FFN_CONV, CB), nb, lambda j, t: (0,)) + _pair_specs((1, CB), nb, lambda j, t: (0,))),
        out_specs=[out, out, out], out_shape=[shape, shape, shape],
        compiler_params=_params("parallel", "arbitrary"), name=name,
    )(p, p, p, p, cw, cw, cb, cb)


def _ffn_mid_bwd(da, p, ga, gb, cw, name):
    s, f2 = p.shape
    ts = _tile(s, (1024, 512))
    nb, nt = f2 // (2 * CB), s // ts
    n_slab = ts // SLAB
    n_grp = SLAB // SUBLANES
    per_trip = 2

    def body(da_ref, ga_ref, gb_ref, pg_ref, pv_ref, cwg_ref, cwv_ref, dp_ref, dcw_ref, dcb_ref, next_du, acc):
        tt = pl.program_id(1)
        cwv = _cat(cwg_ref[...], cwv_ref[...])
        w0, w1, w2 = cwv[0:1], cwv[1:2], cwv[2:3]

        @pl.when(tt == 0)
        def _():
            next_du[...] = jnp.zeros_like(next_du)
            acc[...] = jnp.zeros_like(acc)

        def slab(r0, after, sums):
            rows = pl.ds(r0, SLAB)
            dav = da_ref[rows, :]
            du = _cat(dav * gb_ref[rows, :].astype(F32), dav * ga_ref[rows, :].astype(F32))
            p0 = _cat(pg_ref[rows, :], pv_ref[rows, :])
            cur = [du[i * SUBLANES:(i + 1) * SUBLANES] for i in range(n_grp)]
            du1, du2 = _rows_from(cur + [after], 1), _rows_from(cur + [after], 2)
            dpv = jnp.concatenate([w2 * cur[i] + w1 * du1[i] + w0 * du2[i] for i in range(n_grp)], axis=0).astype(dp_ref.dtype)
            dp_ref[0, rows, :] = dpv[:, :CB]
            dp_ref[1, rows, :] = dpv[:, CB:]
            for i in range(n_grp):
                pi = p0[i * SUBLANES:(i + 1) * SUBLANES]
                parts = (cur[i], du2[i] * pi, du1[i] * pi, cur[i] * pi)
                sums = parts if sums is None else tuple(x + y for x, y in zip(sums, parts))
            return cur[0], sums

        def loop(k, after):
            sums = None
            for j in range(per_trip):
                r0 = pl.multiple_of((n_slab - 1 - (k * per_trip + j)) * SLAB, SLAB)
                after, sums = slab(r0, after, sums)
            for q, part in enumerate(sums):
                acc[q] += part
            return after

        next_du[...] = lax.fori_loop(0, n_slab // per_trip, loop, next_du[...])

        @pl.when(tt == nt - 1)
        def _():
            for half in range(2):
                cols = slice(half * CB, (half + 1) * CB)
                dcb_ref[half] = jnp.sum(acc[0][:, cols], axis=0, keepdims=True)
                for k in range(FFN_CONV):
                    dcw_ref[half, k:k + 1, :] = jnp.sum(acc[1 + k][:, cols], axis=0, keepdims=True)

    rev = lambda t: nt - 1 - t
    tile = pl.BlockSpec((ts, CB), lambda j, t: (rev(t), j))
    return pl.pallas_call(
        body, grid=(nb, nt),
        in_specs=([tile, tile, tile] + _pair_specs((ts, CB), nb, lambda j, t: (rev(t),))
                  + _pair_specs((FFN_CONV, CB), nb, lambda j, t: (0,))),
        out_specs=[pl.BlockSpec((2, ts, CB), lambda j, t: (0, rev(t), j)),
                   pl.BlockSpec((2, FFN_CONV, CB), lambda j, t: (0, 0, j)),
                   pl.BlockSpec((2, 1, CB), lambda j, t: (0, 0, j))],
        out_shape=[jax.ShapeDtypeStruct((2, s, f2 // 2), MXU_DTYPE), jax.ShapeDtypeStruct((2, FFN_CONV, f2 // 2), F32),
                   jax.ShapeDtypeStruct((2, 1, f2 // 2), F32)],
        scratch_shapes=[pltpu.VMEM((SUBLANES, 2 * CB), F32), pltpu.VMEM((1 + FFN_CONV, SUBLANES, 2 * CB), F32)],
        compiler_params=_params("parallel", "arbitrary"), name=name,
    )(da, ga, gb, p, p, cw, cw)


def _rg_gates(xc, wa_ref, ba_ref, wx_ref, bx_ref, lam_ref):
    r = _sigmoid(_dot_nn(xc, wa_ref[0]) + ba_ref[...])
    ig = _sigmoid(_dot_nn(xc, wx_ref[0]) + bx_ref[...])
    sp = _softplus(-lam_ref[...])
    log_a = (-RG_C) * r * sp
    a = jnp.exp(log_a)
    mult = jnp.sqrt(_one_minus_sq_exp(log_a, a))
    return r, ig, sp, a, mult


def _rg_conv(scr, cw_ref, cb_ref, ts):
    views = [scr[5 + k:5 + k + ts, :] for k in range(RG_CONV)]
    xc = cb_ref[...]
    for k in range(RG_CONV):
        xc = xc + cw_ref[k:k + 1, :] * views[k]
    return xc, views


def _rg_param_specs():
    vec = pl.BlockSpec((1, CB), lambda g, t: (0, g))
    mat = pl.BlockSpec((1, CB, CB), lambda g, t: (g, 0, 0))
    return [pl.BlockSpec((RG_CONV, CB), lambda g, t: (0, g)), vec, mat, vec, mat, vec, vec]


def _scan_rows(a_scr, x_scr, out_ref, carry, ts, reverse):
    n = ts // SUBLANES
    row = lax.broadcasted_iota(jnp.int32, (SUBLANES, a_scr.shape[1]), 0)
    last = SUBLANES - 1

    def rows_of(k):
        return pl.ds(pl.multiple_of(k * SUBLANES, SUBLANES), SUBLANES)

    def local(k, _):
        rows = rows_of(k)
        a, x = a_scr[rows, :], x_scr[rows, :]
        if reverse:
            a = jnp.where(row == last, 1.0, pltpu.roll(a, last, axis=0))
            for sh in (1, 2, 4):
                keep = row < SUBLANES - sh
                x = x + a * jnp.where(keep, pltpu.roll(x, SUBLANES - sh, axis=0), 0.0)
                a = a * jnp.where(keep, pltpu.roll(a, SUBLANES - sh, axis=0), 1.0)
        else:
            for sh in (1, 2, 4):
                keep = row >= sh
                x = a * jnp.where(keep, pltpu.roll(x, sh, axis=0), 0.0) + x
                a = a * jnp.where(keep, pltpu.roll(a, sh, axis=0), 1.0)
        out_ref[rows, :] = x
        x_scr[rows, :] = a
        return 0

    lax.fori_loop(0, n, local, 0, unroll=4)

    def chain(k, c):
        rows = rows_of(n - 1 - k if reverse else k)
        v = out_ref[rows, :] + x_scr[rows, :] * c
        out_ref[rows, :] = v
        return a_scr[rows, :][0:1] * v[0:1] if reverse else v[last:last + 1]

    return lax.fori_loop(0, n, chain, carry, unroll=4)


def _rg_mid_fwd(pj, cw, cb, wa, ba, wx, bx, lam, name):
    s = pj.shape[0]
    nb = pj.shape[1] // (2 * CB)
    ts = _tile(s, (512,))
    nt = s // ts

    def body(gate_ref, x_ref, halo_ref, cw_ref, cb_ref, wa_ref, ba_ref, wx_ref, bx_ref, lam_ref, y_ref, hs_ref,
             scr, a_scr, u_scr, h_scr):
        t = pl.program_id(1)

        @pl.when(t == 0)
        def _():
            h_scr[...] = jnp.zeros_like(h_scr)

        scr[0:SUBLANES, :] = jnp.where(t > 0, halo_ref[...], 0.0)
        scr[SUBLANES:, :] = x_ref[...]
        xc, _ = _rg_conv(scr, cw_ref, cb_ref, ts)
        _, ig, _, a, mult = _rg_gates(xc, wa_ref, ba_ref, wx_ref, bx_ref, lam_ref)
        a_scr[...] = a
        u_scr[...] = mult * (ig * xc)
        h_scr[0:1, :] = _scan_rows(a_scr, u_scr, hs_ref, h_scr[0:1, :], ts, False)
        y_ref[...] = (_gelu(gate_ref[...])[0] * hs_ref[...]).astype(y_ref.dtype)

    blk = pl.BlockSpec((ts, CB), lambda g, t: (t, g))
    return pl.pallas_call(
        body, grid=(nb, nt),
        in_specs=_pair_specs((ts, CB), nb, lambda g, t: (t,))
        + [pl.BlockSpec((SUBLANES, CB), lambda g, t: _halo_row(ts, lambda u: u)(g, t) + (g + nb,))] + _rg_param_specs(),
        out_specs=[blk, blk],
        out_shape=[jax.ShapeDtypeStruct((s, nb * CB), MXU_DTYPE), jax.ShapeDtypeStruct((s, nb * CB), F32)],
        scratch_shapes=[pltpu.VMEM((ts + SUBLANES, CB), F32), pltpu.VMEM((ts, CB), F32), pltpu.VMEM((ts, CB), F32),
                        pltpu.VMEM((SUBLANES, CB), F32)],
        compiler_params=_params("parallel", "arbitrary"), name=name,
    )(pj, pj, pj, cw, cb, wa, ba, wx, bx, lam)


def _rg_mid_bwd(dy, pj, hs, cw, cb, wa, ba, wx, bx, lam, name):
    s = pj.shape[0]
    nb = pj.shape[1] // (2 * CB)
    ts = _tile(s, (512,))
    nt = s // ts

    def body(dy_ref, gate_ref, x_ref, halo_ref, hs_ref, hsh_ref, cw_ref, cb_ref, wa_ref, ba_ref, wx_ref, bx_ref, lam_ref,
             dpj_ref, dcw_ref, dcb_ref, dwa_ref, dba_ref, dwx_ref, dbx_ref, dlam_ref,
             scr, hscr, a_scr, d_scr, g_scr, dxscr, c_scr):
        tt = pl.program_id(1)
        t = nt - 1 - tt

        @pl.when(tt == 0)
        def _():
            c_scr[...] = jnp.zeros_like(c_scr)
            dxscr[ts:, :] = jnp.zeros((SUBLANES, CB), F32)
            for ref in (dcw_ref, dcb_ref, dwa_ref, dba_ref, dwx_ref, dbx_ref, dlam_ref):
                ref[...] = jnp.zeros_like(ref)

        scr[0:SUBLANES, :] = jnp.where(t > 0, halo_ref[...], 0.0)
        scr[SUBLANES:, :] = x_ref[...]
        hscr[0:SUBLANES, :] = jnp.where(t > 0, hsh_ref[...], 0.0)
        hscr[SUBLANES:, :] = hs_ref[...]
        xc, views = _rg_conv(scr, cw_ref, cb_ref, ts)
        r, ig, sp, a, mult = _rg_gates(xc, wa_ref, ba_ref, wx_ref, bx_ref, lam_ref)
        gate = gate_ref[...]
        gel, th = _gelu(gate)
        dyv = dy_ref[...]
        dpj_ref[0] = (dyv * hs_ref[...] * _gelu_grad(gate, th)).astype(dpj_ref.dtype)
        a_scr[...] = a
        d_scr[...] = dyv * gel
        c_scr[0:1, :] = _scan_rows(a_scr, d_scr, g_scr, c_scr[0:1, :], ts, True)
        du = g_scr[...]
        da = du * hscr[7:7 + ts, :]
        dmult = du * (ig * xc)
        dig = du * (mult * xc)
        dxc = du * (mult * ig)
        dlog_a = da * a - dmult * (a * a / mult)
        dlam_ref[...] += jnp.sum(dlog_a * r, axis=0, keepdims=True) * (RG_C * _sigmoid(-lam_ref[...]))
        dpr = dlog_a * ((-RG_C) * sp) * (r * (1.0 - r))
        dpi = dig * (ig * (1.0 - ig))
        dba_ref[...] += jnp.sum(dpr, axis=0, keepdims=True)
        dbx_ref[...] += jnp.sum(dpi, axis=0, keepdims=True)
        dwa_ref[0] += _dot_tn(xc, dpr)
        dwx_ref[0] += _dot_tn(xc, dpi)
        dxc = dxc + _dot_nt(dpr, wa_ref[0]) + _dot_nt(dpi, wx_ref[0])
        dcb_ref[...] += jnp.sum(dxc, axis=0, keepdims=True)
        for k in range(RG_CONV):
            dcw_ref[k:k + 1, :] += jnp.sum(dxc * views[k], axis=0, keepdims=True)
        dxscr[0:ts, :] = dxc
        dxp = cw_ref[3:4, :] * dxc
        for k in range(RG_CONV - 1):
            dxp = dxp + cw_ref[k:k + 1, :] * dxscr[3 - k:3 - k + ts, :]
        dpj_ref[1] = dxp.astype(dpj_ref.dtype)
        dxscr[ts:, :] = dxscr[0:SUBLANES, :]

    rev = lambda g, t: (nt - 1 - t, g)
    rev_halo = lambda g, t: (jnp.maximum((nt - 1 - t) * (ts // SUBLANES) - 1, 0), g)
    vec = pl.BlockSpec((1, CB), lambda g, t: (0, g))
    mat = pl.BlockSpec((1, CB, CB), lambda g, t: (g, 0, 0))
    d = nb * CB
    vshape = jax.ShapeDtypeStruct((1, d), F32)
    mshape = jax.ShapeDtypeStruct((nb, CB, CB), F32)
    return pl.pallas_call(
        body, grid=(nb, nt),
        in_specs=[pl.BlockSpec((ts, CB), rev)] + _pair_specs((ts, CB), nb, lambda g, t: (nt - 1 - t,))
        + [pl.BlockSpec((SUBLANES, CB), lambda g, t: (rev_halo(g, t)[0], g + nb)),
           pl.BlockSpec((ts, CB), rev), pl.BlockSpec((SUBLANES, CB), rev_halo)] + _rg_param_specs(),
        out_specs=[pl.BlockSpec((2, ts, CB), lambda g, t: (0, nt - 1 - t, g)), pl.BlockSpec((RG_CONV, CB), lambda g, t: (0, g)),
                   vec, mat, vec, mat, vec, vec],
        out_shape=[jax.ShapeDtypeStruct((2, s, d), MXU_DTYPE), jax.ShapeDtypeStruct((RG_CONV, d), F32), vshape, mshape, vshape,
                   mshape, vshape, vshape],
        scratch_shapes=[pltpu.VMEM((ts + SUBLANES, CB), F32), pltpu.VMEM((ts + SUBLANES, CB), F32), pltpu.VMEM((ts, CB), F32),
                        pltpu.VMEM((ts, CB), F32), pltpu.VMEM((ts, CB), F32), pltpu.VMEM((ts + SUBLANES, CB), F32),
                        pltpu.VMEM((SUBLANES, CB), F32)],
        compiler_params=_params("parallel", "arbitrary"), name=name,
    )(dy, pj, pj, pj, hs, hs, cw, cb, wa, ba, wx, bx, lam)


GLA_DK = 128
GLA_DV = 256
GLA_O_K = GLA_HEADS * GLA_DK
GLA_O_V = 2 * GLA_HEADS * GLA_DK
GLA_O_R = GLA_O_V + GLA_HEADS * GLA_DV
GLA_O_Z = GLA_O_R + GLA_HEADS * GLA_DV
GLA_IN = GLA_O_Z + GLA_RANK
GLA_TS = 256


def _dk(h, base=0):
    return slice(base + h * GLA_DK, base + (h + 1) * GLA_DK)


def _dv(h, base=0):
    return slice(base + h * GLA_DV, base + (h + 1) * GLA_DV)


def _split3(x):
    hi = x.astype(BF16)
    r1 = x - hi.astype(F32)
    mid = r1.astype(BF16)
    lo = (r1 - mid.astype(F32)).astype(BF16)
    return hi, mid, lo


def _chunk_cumsum(x, reverse):
    n = x.shape[0]
    i = lax.broadcasted_iota(jnp.int32, (n, n), 0)
    j = lax.broadcasted_iota(jnp.int32, (n, n), 1)
    same = (i // GLA_CHUNK) == (j // GLA_CHUNK)
    tri = jnp.where(same & ((j >= i) if reverse else (j <= i)), 1.0, 0.0).astype(BF16)
    out = jnp.zeros(x.shape, F32)
    for piece in _split3(x):
        out = out + lax.dot_general(tri, piece, (((1,), (0,)), ((), ())), preferred_element_type=F32)
    return out


def _gla_head(pj_ref, h):
    return (pj_ref[:, _dk(h)] * (GLA_DK ** -0.5), pj_ref[:, _dk(h, GLA_O_K)], pj_ref[:, _dv(h, GLA_O_V)],
            pj_ref[:, _dv(h, GLA_O_R)])


def _gla_decays(gc):
    gref = gc[GLA_CHUNK // 2:GLA_CHUNK // 2 + 1, :]
    glast = gc[GLA_CHUNK - 1:GLA_CHUNK, :]
    return jnp.exp(gc), jnp.exp(gc - gref), jnp.exp(gref - gc), jnp.exp(glast - gc), jnp.exp(glast)


def _causal_mask():
    i = lax.broadcasted_iota(jnp.int32, (GLA_CHUNK, GLA_CHUNK), 0)
    j = lax.broadcasted_iota(jnp.int32, (GLA_CHUNK, GLA_CHUNK), 1)
    return j <= i


def _log_sigmoid(x):
    return jnp.minimum(x, 0.0) - _log1p_pos(jnp.exp(-jnp.abs(x)))


def _gla_mid_fwd(pj, wal, bal, ng, name):
    s, nh = pj.shape[0], GLA_HEADS
    ts = _tile(s, (GLA_TS,))
    nt, nc = s // ts, ts // GLA_CHUNK

    def body(pj_ref, wal_ref, bal_ref, ng_ref, act_ref, o_ref, st_ref, s_scr):
        t = pl.program_id(0)

        @pl.when(t == 0)
        def _():
            s_scr[...] = jnp.zeros_like(s_scr)

        heads = []
        z = pj_ref[:, GLA_O_Z:]
        for h in range(nh):
            q, k, v, r = _gla_head(pj_ref, h)
            g = _log_sigmoid(_dot_nn(z, wal_ref[:, _dk(h)]) + bal_ref[:, _dk(h)]) * (1.0 / GLA_TAU)
            heads.append((q, k, v, r, _chunk_cumsum(g, False)))
        mask = _causal_mask()
        for c in range(nc):
            sl = slice(c * GLA_CHUNK, (c + 1) * GLA_CHUNK)
            for h, (q, k, v, r, gcum) in enumerate(heads):
                eg, eq, ek, ekd, egl = _gla_decays(gcum[sl])
                st = s_scr[h]
                st_ref[c, h] = st
                attn = jnp.where(mask, _dot_nt(q[sl] * eq, k[sl] * ek), 0.0)
                o_ref[sl, h * GLA_DV:(h + 1) * GLA_DV] = _dot_nt(q[sl] * eg, st) + _dot_nn(attn, v[sl])
                s_scr[h] = st * egl + _dot_tn(v[sl], k[sl] * ekd)
        for h, (q, k, v, r, gcum) in enumerate(heads):
            cols = slice(h * GLA_DV, (h + 1) * GLA_DV)
            o = o_ref[:, cols]
            on = o * lax.rsqrt(jnp.mean(o * o, axis=-1, keepdims=True) + EPS)
            act_ref[:, cols] = ((on * ng_ref[...]) * (r * _sigmoid(r))).astype(act_ref.dtype)

    blk = pl.BlockSpec((ts, nh * GLA_DV), lambda t: (t, 0))
    whole = lambda shape: pl.BlockSpec(shape, lambda t: (0,) * len(shape))
    return pl.pallas_call(
        body, grid=(nt,),
        in_specs=[pl.BlockSpec((ts, GLA_IN), lambda t: (t, 0)), whole((GLA_RANK, nh * GLA_DK)), whole((1, nh * GLA_DK)),
                  whole((1, GLA_DV))],
        out_specs=[blk, blk, pl.BlockSpec((nc, nh, GLA_DV, GLA_DK), lambda t: (t, 0, 0, 0))],
        out_shape=[jax.ShapeDtypeStruct((s, nh * GLA_DV), MXU_DTYPE), jax.ShapeDtypeStruct((s, nh * GLA_DV), F32),
                   jax.ShapeDtypeStruct((s // GLA_CHUNK, nh, GLA_DV, GLA_DK), F32)],
        scratch_shapes=[pltpu.VMEM((nh, GLA_DV, GLA_DK), F32)],
        compiler_params=_params("arbitrary"), name=name,
    )(pj, wal, bal, ng)


def _gla_mid_bwd(dact, pj, o, st, wal, bal, ng, name):
    s, nh = pj.shape[0], GLA_HEADS
    ts = _tile(s, (GLA_TS,))
    nt, nc = s // ts, ts // GLA_CHUNK

    def body(dact_ref, pj_ref, o_ref, st_ref, wal_ref, bal_ref, ng_ref, dpj_ref, dwal_ref, dbal_ref, dng_ref,
             ds_scr, dg_scr):
        tt = pl.program_id(0)

        @pl.when(tt == 0)
        def _():
            ds_scr[...] = jnp.zeros_like(ds_scr)
            dwal_ref[...] = jnp.zeros_like(dwal_ref)
            dbal_ref[...] = jnp.zeros_like(dbal_ref)
            dng_ref[...] = jnp.zeros_like(dng_ref)

        heads = []
        z = pj_ref[:, GLA_O_Z:]
        for h in range(nh):
            q, k, v, r = _gla_head(pj_ref, h)
            logit = _dot_nn(z, wal_ref[:, _dk(h)]) + bal_ref[:, _dk(h)]
            gcum = _chunk_cumsum(_log_sigmoid(logit) * (1.0 / GLA_TAU), False)
            ov = o_ref[:, h * GLA_DV:(h + 1) * GLA_DV]
            ro = lax.rsqrt(jnp.mean(ov * ov, axis=-1, keepdims=True) + EPS)
            on = ov * ro
            sg = _sigmoid(r)
            sil = r * sg
            dav = dact_ref[:, h * GLA_DV:(h + 1) * GLA_DV]
            dpj_ref[:, _dv(h, GLA_O_R)] = (dav * (on * ng_ref[...]) * (sg + sil * (1.0 - sg))).astype(dpj_ref.dtype)
            t1 = dav * sil
            dng_ref[...] += jnp.sum(t1 * on, axis=0, keepdims=True)
            dn = t1 * ng_ref[...]
            do = ro * (dn - on * jnp.mean(dn * on, axis=-1, keepdims=True))
            heads.append((q, k, v, logit, gcum, do))
        mask = _causal_mask()
        scale = GLA_DK ** -0.5
        last_row = lax.broadcasted_iota(jnp.int32, (GLA_CHUNK, GLA_DK), 0) == GLA_CHUNK - 1
        for c in reversed(range(nc)):
            sl = slice(c * GLA_CHUNK, (c + 1) * GLA_CHUNK)
            for h, (q, k, v, logit, gcum, do) in enumerate(heads):
                eg, eq, ek, ekd, egl = _gla_decays(gcum[sl])
                qc, kc, vc, doc = q[sl], k[sl], v[sl], do[sl]
                qg, qt, kt, kd = qc * eg, qc * eq, kc * ek, kc * ekd
                sp = st_ref[c, h]
                ds = ds_scr[h]
                attn = jnp.where(mask, _dot_nt(qt, kt), 0.0)
                dattn = jnp.where(mask, _dot_nt(doc, vc), 0.0)
                dqg = _dot_nn(doc, sp)
                dqt = _dot_nn(dattn, kt)
                dkt = _dot_tn(dattn, qt)
                dkd = _dot_nn(vc, ds)
                dpj_ref[sl, _dv(h, GLA_O_V)] = (_dot_tn(attn, doc) + _dot_nt(kd, ds)).astype(dpj_ref.dtype)
                dpj_ref[sl, _dk(h)] = (scale * (dqg * eg + dqt * eq)).astype(dpj_ref.dtype)
                dpj_ref[sl, _dk(h, GLA_O_K)] = (dkt * ek + dkd * ekd).astype(dpj_ref.dtype)
                kdd = dkd * kd
                dgl = jnp.sum(kdd, axis=0, keepdims=True) + jnp.sum(ds * sp, axis=0, keepdims=True) * egl
                dg_scr[h, sl, :] = dqg * qg + dqt * qt - dkt * kt - kdd + jnp.where(last_row, dgl, 0.0)
                ds_scr[h] = ds * egl + _dot_tn(doc, qg)
        dz = jnp.zeros((ts, GLA_RANK), F32)
        for h, (q, k, v, logit, gcum, do) in enumerate(heads):
            dlogit = _chunk_cumsum(dg_scr[h], True) * (1.0 / GLA_TAU) * _sigmoid(-logit)
            dz = dz + _dot_nt(dlogit, wal_ref[:, _dk(h)])
            dwal_ref[:, _dk(h)] += _dot_tn(z, dlogit)
            dbal_ref[:, _dk(h)] += jnp.sum(dlogit, axis=0, keepdims=True)
        dpj_ref[:, GLA_O_Z:] = dz.astype(dpj_ref.dtype)

    rev = lambda t: (nt - 1 - t, 0)
    whole = lambda shape: pl.BlockSpec(shape, lambda t: (0,) * len(shape))
    wide = pl.BlockSpec((ts, nh * GLA_DV), rev)
    return pl.pallas_call(
        body, grid=(nt,),
        in_specs=[wide, pl.BlockSpec((ts, GLA_IN), rev), wide,
                  pl.BlockSpec((nc, nh, GLA_DV, GLA_DK), lambda t: (nt - 1 - t, 0, 0, 0)),
                  whole((GLA_RANK, nh * GLA_DK)), whole((1, nh * GLA_DK)), whole((1, GLA_DV))],
        out_specs=[pl.BlockSpec((ts, GLA_IN), rev), whole((GLA_RANK, nh * GLA_DK)), whole((1, nh * GLA_DK)), whole((1, GLA_DV))],
        out_shape=[jax.ShapeDtypeStruct((s, GLA_IN), MXU_DTYPE), jax.ShapeDtypeStruct((GLA_RANK, nh * GLA_DK), F32),
                   jax.ShapeDtypeStruct((1, nh * GLA_DK), F32), jax.ShapeDtypeStruct((1, GLA_DV), F32)],
        scratch_shapes=[pltpu.VMEM((nh, GLA_DV, GLA_DK), F32), pltpu.VMEM((nh, ts, GLA_DK), F32)],
        compiler_params=_params("arbitrary"), name=name,
    )(dact, pj, o, st, wal, bal, ng)


def _adamw(w, gs, m, v, name, after=None):
    layers, rows, cols = w.shape
    gs = list(gs) if isinstance(gs, (list, tuple)) else gs
    n_g = len(gs) if isinstance(gs, list) else 1
    if rows % SUBLANES == 0:
        tr, tc = _tile(rows, (256, 128, 64, 32, 16, 8)), cols
    else:
        tr, tc = rows, _tile(cols, (256, 128))
    c1 = 1.0 / (1.0 - ADAM_B1 ** ADAM_STEP)
    c2 = 1.0 / (1.0 - ADAM_B2 ** ADAM_STEP)

    def body(*refs):
        g_refs, (w_ref, m_ref, v_ref) = refs[:n_g], refs[n_g:n_g + 3]
        go_ref, d_ref, mo_ref, vo_ref = refs[-4:]
        gv = g_refs[0][...]
        for l in range(1, n_g):
            gv = jnp.where(pl.program_id(0) == l, g_refs[l][...], gv)
        m2 = ADAM_B1 * m_ref[...] + (1.0 - ADAM_B1) * gv
        v2 = ADAM_B2 * v_ref[...] + (1.0 - ADAM_B2) * (gv * gv)
        d_ref[...] = (-ADAM_LR) * ((m2 * c1) / (jnp.sqrt(v2 * c2) + ADAM_EPS) + ADAM_WD * w_ref[...])
        go_ref[...] = gv
        mo_ref[...] = m2
        vo_ref[...] = v2

    spec = pl.BlockSpec((None, tr, tc), lambda l, i, j: (l, i, j))
    g_specs = [pl.BlockSpec((tr, tc), lambda l, i, j: (i, j))] * n_g if isinstance(gs, list) else [spec]
    extra = [] if after is None else [(after, _ANY)]
    shape = jax.ShapeDtypeStruct((layers, rows, cols), F32)
    return pl.pallas_call(
        body, grid=(layers, rows // tr, cols // tc), in_specs=g_specs + [spec] * 3 + [sp for _, sp in extra],
        out_specs=[spec] * 4, out_shape=[shape] * 4, compiler_params=_params("parallel", "parallel", "parallel"), name=name,
    )(*(gs if isinstance(gs, list) else [gs]), w, m, v, *[a for a, _ in extra])


def _col_slots(w, name):
    r, cc = w.shape
    c = cc // N_CHIP
    tr = _tile(r, (256,))

    def body(w_ref, o_ref):
        for j in range(N_CHIP):
            o_ref[j] = w_ref[:, j * c:(j + 1) * c]

    return pl.pallas_call(
        body, grid=(r // tr,), in_specs=[pl.BlockSpec((tr, cc), lambda i: (i, 0))],
        out_specs=pl.BlockSpec((N_CHIP, tr, c), lambda i: (0, i, 0)), out_shape=jax.ShapeDtypeStruct((N_CHIP, r, c), w.dtype),
        compiler_params=_params("parallel"), name=name,
    )(w)


def _from_col_slots(w, name):
    n, r, c = w.shape
    tr = _tile(r, (256,))

    def body(w_ref, o_ref):
        for j in range(n):
            o_ref[:, j * c:(j + 1) * c] = w_ref[j]

    return pl.pallas_call(
        body, grid=(r // tr,), in_specs=[pl.BlockSpec((n, tr, c), lambda i: (0, i, 0))],
        out_specs=pl.BlockSpec((tr, n * c), lambda i: (i, 0)), out_shape=jax.ShapeDtypeStruct((r, n * c), w.dtype),
        compiler_params=_params("parallel"), name=name,
    )(w)


def _block_rows_to_slots(w):
    g, r4, cc = w.shape
    return jnp.swapaxes(w.reshape(g, N_CHIP, r4 // N_CHIP, cc), 0, 1).reshape(N_CHIP, g * (r4 // N_CHIP), cc)


def _slots_to_block_rows(w, g):
    n, gr, cc = w.shape
    return jnp.swapaxes(w.reshape(n, g, gr // g, cc), 0, 1).reshape(g, n * (gr // g), cc)


def _local_step(x, tgt, mod, w, fetch=None, done=None, later=None):
    depth = mod.shape[0]
    row = lambda v: v.reshape(1, -1)
    w = dict(w)
    w["ffn_w_up"], w["ffn_w_down"] = dict(enumerate(w["ffn_w_up"])), dict(enumerate(w["ffn_w_down"]))

    def arrive(stage, after):
        if fetch is not None:
            for k, v in fetch(stage, after).items():
                if isinstance(v, dict):
                    w[k].update(v)
                else:
                    w[k] = v

    saved = []
    for i in range(depth):
        if i == 1:
            arrive("gla", x)
        sh_m, sc_m, gt_m, sh_f, sc_f, gt_f = (mod[i, j:j + 1] for j in range(6))
        g0, g1, g2, g3 = (w["norm_g"][i, j:j + 1] for j in range(4))
        tag = f"_l{i}"
        if i == 0:
            h = _norm_mod_fwd(x, g0, sc_m, sh_m, "norm_mix" + tag)
        if i % 2 == 0:
            pj = _mm(h, w["rg_w_in"], w_slots=N_CHIP, name="rg_in" + tag)
            act, aux = _rg_mid_fwd(pj, w["rg_conv_w"], row(w["rg_conv_b"]), w["rg_wa"], row(w["rg_ba"]), w["rg_wx"],
                                   row(w["rg_bx"]), row(w["rg_lambda"]), "rg_mid" + tag)
            y = _mm(act, w["rg_w_out"], name="rg_out" + tag)
        else:
            pj = _mm(h, w["gla_w_in"], tm_max=512, name="gla_in" + tag)
            act, *aux = _gla_mid_fwd(pj, w["gla_w_alpha"], row(w["gla_b_alpha"]), row(w["gla_norm_g"]), "gla_mid" + tag)
            y = _mm(act, w["gla_w_out"], name="gla_out" + tag)
        x1, h2 = _post_norm_fwd(x, y, g1, gt_m, g2, sc_f, sh_f, "post_mix" + tag)
        arrive(f"ffn{i}", x1)
        p = _mm(h2, w["ffn_w_up"][i], w_slots=N_CHIP, name="ffn_up" + tag)
        a, ga, gb = _ffn_mid_fwd(p, w["ffn_conv_w"][i], w["ffn_conv_b"][i:i + 1], "ffn_mid" + tag)
        y2 = _mm(a, w["ffn_w_down"][i], name="ffn_down" + tag)
        saved_h = h
        if i + 1 < depth:
            nxt = [mod[i + 1, j:j + 1] for j in range(2)] + [w["norm_g"][i + 1, 0:1]]
            x2, h = _post_norm_fwd(x1, y2, g3, gt_f, nxt[2], nxt[1], nxt[0], "post_ffn" + tag)
        else:
            x2 = None
            cols, dx = _post_loss(x1, y2, g3, gt_f, tgt, "post_ffn_loss")
        saved.append((x, saved_h, pj, act, aux, y, x1, h2, p, (a, ga, gb), y2))
        x = x2

    stacked = ("norm_g", "ffn_conv_w", "ffn_conv_b", "mod")
    gr = {k: [None] * depth for k in stacked + ("ffn_w_up", "ffn_w_down")}
    told = lambda stage: done(stage, gr) if done is not None else 0.0
    told_later = lambda stage, after: later(stage, after) if later is not None else 0.0
    for i in reversed(range(depth)):
        x0, h, pj, act, aux, y, x1, h2, p, (a, ga, gb), y2 = saved[i]
        sh_m, sc_m, gt_m, sh_f, sc_f, gt_f = (mod[i, j:j + 1] for j in range(6))
        g0, g1, g2, g3 = (w["norm_g"][i, j:j + 1] for j in range(4))
        tag = f"_l{i}"
        if i == depth - 1:
            dy2, d_g3, d_gt_f = _post_bwd(dx, y2, g3, gt_f, "post_ffn_b" + tag)
        else:
            dy2, d_g3, d_gt_f = ahead
        da = _mm(dy2, w["ffn_w_down"][i], tb=True, name="ffn_down_dx" + tag)
        gr["ffn_w_down"][i] = _mm(a, dy2, ta=True, name="ffn_down_dw" + tag)
        conv_w = w["ffn_conv_w"][i] + (told_later("l1", da) if i == 0 else 0.0)
        dp, dcw, dcb = _ffn_mid_bwd(da, p, ga, gb, conv_w, "ffn_mid_b" + tag)
        gr["ffn_conv_w"][i], gr["ffn_conv_b"][i] = _cat(dcw[0], dcw[1]), _cat(dcb[0], dcb[1])[0]
        dh2 = _mm(dp, w["ffn_w_up"][i], tb=True, a_parts=2, w_slots=N_CHIP, name="ffn_up_dx" + tag)
        gr["ffn_w_up"][i] = _mm(h2, dp, ta=True, b_parts=2, out_slots=N_CHIP, name="ffn_up_dw" + tag)
        if i == 0:
            gt_m = gt_m + told("ffn0")
        dx1, dy, d_g2, d_sc_f, d_sh_f, d_g1, d_gt_m = _norm_post_bwd(dh2, x1, g2, sc_f, dx, y, g1, gt_m, "norm_ffn_b" + tag)
        if i % 2 == 0:
            dact = _mm(dy, w["rg_w_out"], tb=True, name="rg_out_dx" + tag)
            gr["rg_w_out"] = _mm(act, dy, ta=True, name="rg_out_dw" + tag)
            lam = row(w["rg_lambda"]) + told_later("ffn0", gr["rg_w_out"])
            dpj, gr["rg_conv_w"], d_cb, gr["rg_wa"], d_ba, gr["rg_wx"], d_bx, d_lam = _rg_mid_bwd(
                dact, pj, aux, w["rg_conv_w"], row(w["rg_conv_b"]), w["rg_wa"], row(w["rg_ba"]), w["rg_wx"],
                row(w["rg_bx"]), lam, "rg_mid_b" + tag)
            gr["rg_conv_b"], gr["rg_ba"], gr["rg_bx"], gr["rg_lambda"] = d_cb[0], d_ba[0], d_bx[0], d_lam[0]
            dh = _mm(dpj, w["rg_w_in"], tb=True, a_parts=2, w_slots=N_CHIP, name="rg_in_dx" + tag)
            gr["rg_w_in"] = _mm(h, dpj, ta=True, b_parts=2, out_slots=N_CHIP, name="rg_in_dw" + tag)
            sc_m = sc_m + told("rg")
        else:
            dact = _mm(dy, w["gla_w_out"], tb=True, name="gla_out_dx" + tag)
            gr["gla_w_out"] = _mm(act, dy, ta=True, name="gla_out_dw" + tag)
            dpj, gr["gla_w_alpha"], d_bal, d_ng = _gla_mid_bwd(dact, pj, aux[0], aux[1], w["gla_w_alpha"], row(w["gla_b_alpha"]),
                                                               row(w["gla_norm_g"]), "gla_mid_b" + tag)
            gr["gla_b_alpha"], gr["gla_norm_g"] = d_bal[0], d_ng[0]
            dh = _mm(dpj, w["gla_w_in"], tb=True, name="gla_in_dx" + tag)
            gr["gla_w_in"] = _mm(h, dpj, ta=True, tm_max=512, name="gla_in_dw" + tag)
            mod = mod.at[0].add(told("l1"))
        if i > 0:
            dx, dy_below, d_g0, d_sc_m, d_sh_m, d_g_below, d_gt_below = _norm_post_bwd(
                dh, x0, g0, sc_m, dx1, saved[i - 1][-1], w["norm_g"][i - 1, 3:4], mod[i - 1, 5:6], "norm_mix_b" + tag)
            ahead = (dy_below, d_g_below, d_gt_below)
        else:
            dx, d_g0, d_sc_m, d_sh_m = _norm_mod_bwd(dh, x0, g0, sc_m, dx1, "norm_mix_b" + tag)
        gr["norm_g"][i] = jnp.concatenate([d_g0, d_g1, d_g2, d_g3], axis=0)
        gr["mod"][i] = jnp.concatenate([d_sh_m, d_sc_m, d_gt_m, d_sh_f, d_sc_f, d_gt_f], axis=0)
    for k in stacked:
        gr[k] = jnp.stack(gr[k])
    return cols, dx, gr


ADA_ROWS = 16


def _ada_fwd(c16, ada_w, ada_b, name):
    depth, d, n = ada_w.shape
    tn = _tile(n, (512, 256, 128))

    def body(c_ref, w_ref, b_ref, o_ref):
        cv = c_ref[...]
        o_ref[0] = _dot_nn(cv * _sigmoid(cv), w_ref[0]) + b_ref[0]

    return pl.pallas_call(
        body, grid=(depth, n // tn),
        in_specs=[pl.BlockSpec((ADA_ROWS, d), lambda l, j: (0, 0)), pl.BlockSpec((1, d, tn), lambda l, j: (l, 0, j)),
                  pl.BlockSpec((1, 1, tn), lambda l, j: (l, 0, j))],
        out_specs=pl.BlockSpec((1, ADA_ROWS, tn), lambda l, j: (l, 0, j)),
        out_shape=jax.ShapeDtypeStruct((depth, ADA_ROWS, n), F32),
        compiler_params=_params("parallel", "parallel"), name=name,
    )(c16, ada_w, ada_b)


def _ada_bwd(c16, dmod16, name):
    depth, _, n = dmod16.shape
    d = c16.shape[1]
    tn = _tile(n, (512, 256, 128))

    def body(c_ref, dm_ref, o_ref):
        cv = c_ref[...]
        o_ref[0] = _dot_tn(cv * _sigmoid(cv), dm_ref[0])

    return pl.pallas_call(
        body, grid=(depth, n // tn),
        in_specs=[pl.BlockSpec((ADA_ROWS, d), lambda l, j: (0, 0)), pl.BlockSpec((1, ADA_ROWS, tn), lambda l, j: (l, 0, j))],
        out_specs=pl.BlockSpec((1, d, tn), lambda l, j: (l, 0, j)),
        out_shape=jax.ShapeDtypeStruct((depth, d, n), F32),
        compiler_params=_params("parallel", "parallel"), name=name,
    )(c16, dmod16)


PACK_COLS = 1024
_ANY = pl.BlockSpec(memory_space=pl.ANY)
_VMEM = pl.BlockSpec(memory_space=pltpu.VMEM)


def _place():
    return lax.axis_index("x"), lax.axis_index("y"), lax.axis_index("c")


def _other_chips(x, y):
    return [(1 - x, y), (x, 1 - y), (1 - x, 1 - y)]


def _rcopy(src, dst, send_sems, recv_sems, k, peer):
    return pltpu.make_async_remote_copy(src_ref=src, dst_ref=dst, send_sem=send_sems.at[k], recv_sem=recv_sems.at[k],
                                        device_id=peer, device_id_type=MESH)


def _all_gather_8(v, name):
    r, cc = v.shape

    def body(v_ref, out_ref, send_sems, recv_sems, local_sem):
        x, y, c = _place()
        me = 4 * x + 2 * y + c
        mine = pltpu.make_async_copy(v_ref, out_ref.at[me], local_sem)
        mine.start()
        peers = []
        for k in range(1, N_DEV):
            px = 1 - x if k & 4 else x
            py = 1 - y if k & 2 else y
            pc = 1 - c if k & 1 else c
            peers.append((px, py, pc))
        sends = [_rcopy(v_ref, out_ref.at[me], send_sems, recv_sems, k, p) for k, p in enumerate(peers)]
        for cp in sends:
            cp.start()
        for k, (px, py, pc) in enumerate(peers):
            _rcopy(v_ref, out_ref.at[4 * px + 2 * py + pc], send_sems, recv_sems, k, (px, py, pc)).wait_recv()
        for cp in sends:
            cp.wait_send()
        mine.wait()

    return pl.pallas_call(
        body, in_specs=[_VMEM], out_specs=_VMEM, out_shape=jax.ShapeDtypeStruct((N_DEV, r, cc), v.dtype),
        scratch_shapes=[pltpu.SemaphoreType.DMA((N_DEV - 1,)), pltpu.SemaphoreType.DMA((N_DEV - 1,)), pltpu.SemaphoreType.DMA],
        compiler_params=pltpu.CompilerParams(vmem_limit_bytes=VMEM_LIMIT), name=name,
    )(v)


def _gather_chips(shards, name):
    n = len(shards)
    per = 2 * (N_CHIP - 1)

    def body(*refs):
        ins, outs, (send_sems, recv_sems) = refs[:n], refs[n:2 * n], refs[2 * n:]
        x, y, c = _place()
        chip = 2 * x + y
        chips = _other_chips(x, y)
        rows = [(pl.ds(c * (r.shape[0] // 2), r.shape[0] // 2), pl.ds((1 - c) * (r.shape[0] // 2), r.shape[0] // 2)) for r in ins]
        first = [_rcopy(ins[i].at[rows[i][0]], outs[i].at[chip, rows[i][0]], send_sems, recv_sems, per * i + j, (px, py, c))
                 for i in range(n) for j, (px, py) in enumerate(chips)]
        for cp in first:
            cp.start()
        passed = []
        for i in range(n):
            for j, (px, py) in enumerate(chips):
                landed = outs[i].at[2 * px + py, rows[i][0]]
                _rcopy(ins[i].at[rows[i][0]], landed, send_sems, recv_sems, per * i + j, (px, py, c)).wait_recv()
                fw = _rcopy(landed, landed, send_sems, recv_sems, per * i + N_CHIP - 1 + j, (x, y, 1 - c))
                fw.start()
                passed.append(fw)
        for i in range(n):
            for j, (px, py) in enumerate(chips):
                landed = outs[i].at[2 * px + py, rows[i][1]]
                _rcopy(landed, landed, send_sems, recv_sems, per * i + N_CHIP - 1 + j, (x, y, 1 - c)).wait_recv()
        for cp in first + passed:
            cp.wait_send()

    return pl.pallas_call(
        body, in_specs=[_ANY] * n, out_specs=[_ANY] * n,
        out_shape=[jax.ShapeDtypeStruct((N_CHIP,) + sh.shape, sh.dtype) for sh in shards],
        scratch_shapes=[pltpu.SemaphoreType.DMA((per * n,)), pltpu.SemaphoreType.DMA((per * n,))], name=name,
    )(*shards)


def _pair_exchange(gs, name):
    n = len(gs)

    def body(*refs):
        ins, outs, (send_sems, recv_sems) = refs[:n], refs[n:2 * n], refs[2 * n:]
        x, y, c = _place()
        copies = []
        for i in range(n):
            half = ins[i].shape[1] // 2
            copies.append(_rcopy(ins[i].at[:, pl.ds((1 - c) * half, half)], outs[i], send_sems, recv_sems, i, (x, y, 1 - c)))
        for cp in copies:
            cp.start()
        for cp in copies:
            cp.wait()

    return pl.pallas_call(
        body, in_specs=[_ANY] * n, out_specs=[_ANY] * n,
        out_shape=[jax.ShapeDtypeStruct((g.shape[0], g.shape[1] // 2, g.shape[2]), g.dtype) for g in gs],
        scratch_shapes=[pltpu.SemaphoreType.DMA((n,)), pltpu.SemaphoreType.DMA((n,))], name=name,
    )(*gs)


_ROW_TILES = (640, 512, 352, 256, 128, 64, 32, 16)


def _pair_sum(g, other, c_idx, name):
    n, half, cc = other.shape
    tr = _tile(half, _ROW_TILES)

    def body(c_ref, g_ref, o_ref, out_ref):
        out_ref[...] = (g_ref[...] + o_ref[...]).astype(out_ref.dtype)

    return pl.pallas_call(
        body,
        grid_spec=pltpu.PrefetchScalarGridSpec(
            num_scalar_prefetch=1, grid=(n, half // tr),
            in_specs=[pl.BlockSpec((None, None, tr, cc), lambda k, i, c_ref: (k, c_ref[0], i, 0)),
                      pl.BlockSpec((None, tr, cc), lambda k, i, c_ref: (k, i, 0))],
            out_specs=pl.BlockSpec((None, tr, cc), lambda k, i, c_ref: (k, i, 0))),
        out_shape=jax.ShapeDtypeStruct((n, half, cc), BF16),
        compiler_params=_params("parallel", "parallel"), name=name,
    )(c_idx, g.reshape(n, 2, half, cc), other)


def _chip_exchange(ps, name):
    n = len(ps)
    per = N_CHIP - 1

    def body(*refs):
        ins, outs, (send_sems, recv_sems) = refs[:n], refs[n:2 * n], refs[2 * n:]
        x, y, c = _place()
        chip = 2 * x + y
        chips = _other_chips(x, y)
        sends = [_rcopy(ins[i].at[2 * px + py], outs[i].at[chip], send_sems, recv_sems, per * i + j, (px, py, c))
                 for i in range(n) for j, (px, py) in enumerate(chips)]
        for cp in sends:
            cp.start()
        for i in range(n):
            for j, (px, py) in enumerate(chips):
                _rcopy(ins[i].at[chip], outs[i].at[2 * px + py], send_sems, recv_sems, per * i + j, (px, py, c)).wait_recv()
        for cp in sends:
            cp.wait_send()

    return pl.pallas_call(
        body, in_specs=[_ANY] * n, out_specs=[_ANY] * n, out_shape=[jax.ShapeDtypeStruct(p.shape, p.dtype) for p in ps],
        scratch_shapes=[pltpu.SemaphoreType.DMA((per * n,)), pltpu.SemaphoreType.DMA((per * n,))], name=name,
    )(*ps)


_HBM = pl.BlockSpec(memory_space=pltpu.HBM)
_SEM = pl.BlockSpec(memory_space=pltpu.SEMAPHORE)
_DATAFLOW = pltpu.SideEffectType.DATAFLOW_SIDE_EFFECTING


def _split_copies(srcs, lands, send_sems, recv_sems, mode, arriving):
    x, y, c = _place()
    chip = 2 * x + y
    out = []
    for i, (src, land) in enumerate(zip(srcs, lands)):
        if mode == "all":
            for k in range(1, N_DEV):
                px, py, pc = (1 - x if k & 4 else x), (1 - y if k & 2 else y), (1 - c if k & 1 else c)
                slot = 4 * px + 2 * py + pc if arriving else 2 * chip + c
                out.append(_rcopy(src, land.at[slot], send_sems, recv_sems, (N_DEV - 1) * i + k - 1, (px, py, pc)))
            continue
        if mode == "pair":
            half = src.shape[1] // 2
            out.append(_rcopy(src.at[:, pl.ds((1 - c) * half, half)], land, send_sems, recv_sems, i, (x, y, 1 - c)))
            continue
        for j, (px, py) in enumerate(_other_chips(x, y)):
            there = 2 * px + py
            part = src.at[there] if mode == "slots" else src
            out.append(_rcopy(part, land.at[there if arriving else chip], send_sems, recv_sems, (N_CHIP - 1) * i + j, (px, py, c)))
    return out


def _land_shape(src, mode):
    if mode == "pair":
        return (src.shape[0], src.shape[1] // 2, src.shape[2])
    if mode == "all":
        return (N_DEV,) + src.shape
    return (N_CHIP,) + (src.shape[1:] if mode == "slots" else src.shape)


def _send_start(srcs, mode, name):
    n = len(srcs)
    n_sem = {"pair": 1, "all": N_DEV - 1}.get(mode, N_CHIP - 1) * n
    lands = [lax.empty(_land_shape(s, mode), s.dtype) for s in srcs]

    def body(*refs):
        ins, zones, (send_sems, recv_sems) = refs[:n], refs[n:2 * n], refs[2 * n:2 * n + 2]
        for cp in _split_copies(ins, zones, send_sems, recv_sems, mode, False):
            cp.start()
        refs[-1][...] = jnp.zeros_like(refs[-1])

    hbm = lambda a: pltpu.HBM(a.shape, a.dtype)
    outs = pl.pallas_call(
        body, name=name, in_specs=[_HBM] * (2 * n),
        out_shape=(pltpu.SemaphoreType.DMA((n_sem,)), pltpu.SemaphoreType.DMA((n_sem,)), *[hbm(a) for a in srcs],
                   *[hbm(a) for a in lands], jax.ShapeDtypeStruct((SUBLANES, LANES), F32)),
        out_specs=(_SEM, _SEM, *[_HBM] * (2 * n), _VMEM), input_output_aliases={i: 2 + i for i in range(2 * n)},
        compiler_params=pltpu.CompilerParams(has_side_effects=_DATAFLOW),
    )(*[pltpu.with_memory_space_constraint(a, pltpu.HBM) for a in list(srcs) + lands])
    return (outs[0], outs[1], list(outs[2:2 + n]), list(outs[2 + n:2 + 2 * n])), outs[-1]


def _send_wait(state, after, mode, name):
    send_sems, recv_sems, srcs, lands = state
    n = len(srcs)

    def body(*refs):
        ins, zones, (send_s, recv_s) = refs[:n], refs[n:2 * n], refs[2 * n:2 * n + 2]
        for cp in _split_copies(ins, zones, send_s, recv_s, mode, True):
            cp.wait_send()
            cp.wait_recv()

    hbm = lambda a: pltpu.HBM(a.shape, a.dtype)
    outs = pl.pallas_call(
        body, name=name, in_specs=[_HBM] * (2 * n) + [_SEM, _SEM, _ANY],
        out_shape=tuple(hbm(a) for a in srcs + lands), out_specs=tuple([_HBM] * (2 * n)),
        input_output_aliases={i: i for i in range(2 * n)},
        compiler_params=pltpu.CompilerParams(has_side_effects=_DATAFLOW),
    )(*srcs, *lands, send_sems, recv_sems, after)
    return list(outs[:n]), list(outs[n:])


def _sum_lead(v, name):
    n, r, cc = v.shape
    tr = _tile(r, _ROW_TILES + (8,))

    def body(v_ref, o_ref):
        acc = v_ref[0].astype(F32)
        for k in range(1, n):
            acc = acc + v_ref[k].astype(F32)
        o_ref[...] = acc

    return pl.pallas_call(
        body, grid=(r // tr,), in_specs=[pl.BlockSpec((n, tr, cc), lambda i: (0, i, 0))],
        out_specs=pl.BlockSpec((tr, cc), lambda i: (i, 0)), out_shape=jax.ShapeDtypeStruct((r, cc), F32),
        compiler_params=_params("parallel"), name=name,
    )(v)


def _chip_sum(arrived, mine, chip_idx, name):
    n, r, cc = arrived.shape
    tr = _tile(r, _ROW_TILES)

    def body(chip_ref, a_ref, m_ref, o_ref):
        acc = jnp.zeros((tr, cc), F32)
        for k in range(n):
            acc = acc + jnp.where(chip_ref[0] == k, m_ref[...], a_ref[k]).astype(F32)
        o_ref[...] = acc

    return pl.pallas_call(
        body,
        grid_spec=pltpu.PrefetchScalarGridSpec(
            num_scalar_prefetch=1, grid=(r // tr,),
            in_specs=[pl.BlockSpec((n, tr, cc), lambda i, chip_ref: (0, i, 0)),
                      pl.BlockSpec((None, tr, cc), lambda i, chip_ref: (chip_ref[0], i, 0))],
            out_specs=pl.BlockSpec((tr, cc), lambda i, chip_ref: (i, 0))),
        out_shape=jax.ShapeDtypeStruct((r, cc), F32), compiler_params=_params("parallel"), name=name,
    )(chip_idx, arrived, mine)


def _pair_share(reds, name):
    n = len(reds)

    def body(*refs):
        ins, outs, (send_sems, recv_sems) = refs[:n], refs[n:2 * n], refs[2 * n:]
        x, y, c = _place()
        copies = [_rcopy(ins[i], outs[i].at[c], send_sems, recv_sems, i, (x, y, 1 - c)) for i in range(n)]
        for cp in copies:
            cp.start()
        for i in range(n):
            _rcopy(ins[i], outs[i].at[1 - c], send_sems, recv_sems, i, (x, y, 1 - c)).wait_recv()
        for cp in copies:
            cp.wait_send()

    return pl.pallas_call(
        body, in_specs=[_ANY] * n, out_specs=[_ANY] * n, out_shape=[jax.ShapeDtypeStruct((2,) + r.shape, r.dtype) for r in reds],
        scratch_shapes=[pltpu.SemaphoreType.DMA((n,)), pltpu.SemaphoreType.DMA((n,))], name=name,
    )(*reds)


def _pack(arrs, rows_multiple, dtype):
    flat = jnp.concatenate([a.reshape(-1).astype(dtype) for a in arrs])
    unit = rows_multiple * PACK_COLS
    total = -(-flat.shape[0] // unit) * unit
    return jnp.pad(flat, (0, total - flat.shape[0])).reshape(-1, PACK_COLS)


def _unpack(buf, shapes):
    lead = buf.shape[:-2]
    flat = buf.reshape(*lead, -1)
    out, off = [], 0
    for shp in shapes:
        n = 1
        for s in shp:
            n *= s
        out.append(flat[..., off:off + n].reshape(*lead, *shp))
        off += n
    return out


def _join_shards(parts, axis):
    moved = jnp.moveaxis(parts, 0, axis)
    shp = list(moved.shape)
    shp[axis:axis + 2] = [shp[axis] * shp[axis + 1]]
    return moved.reshape(shp)


def _my_shard(full, axis, chip):
    n = full.shape[axis] // N_CHIP
    return lax.dynamic_slice_in_dim(full, chip * n, n, axis)


SMALL = {"norm_g": 2, "ffn_conv_w": 2, "rg_conv_w": 2, "gla_w_alpha": 2, "gla_b_alpha": 1, "gla_norm_g": 1,
         "ada_b": None, "ffn_conv_b": None, "rg_conv_b": None, "rg_ba": None, "rg_bx": None, "rg_lambda": None}
BIG = {"rg_w_in": True, "rg_wa": False, "rg_wx": False, "rg_w_out": False, "ffn_w_up": True, "ffn_w_down": False,
       "gla_w_in": True, "gla_w_out": False}
WEIGHTS = ["ada_w", "ada_b", "norm_g", "ffn_w_up", "ffn_conv_w", "ffn_conv_b", "ffn_w_down", "rg_w_in", "rg_conv_w", "rg_conv_b",
           "rg_wa", "rg_ba", "rg_wx", "rg_bx", "rg_lambda", "rg_w_out", "gla_w_in", "gla_w_alpha", "gla_b_alpha", "gla_norm_g",
           "gla_w_out"]


def kernel(x, c, ada_w, ada_b, norm_g, ffn_w_up, ffn_conv_w, ffn_conv_b, ffn_w_down, rg_w_in, rg_conv_w, rg_conv_b, rg_wa, rg_ba, rg_wx, rg_bx, rg_lambda, rg_w_out, gla_w_in, gla_w_alpha, gla_b_alpha, gla_norm_g, gla_w_out, loss_target, m_ada_w, m_ada_b, m_norm_g, m_ffn_w_up, m_ffn_conv_w, m_ffn_conv_b, m_ffn_w_down, m_rg_w_in, m_rg_conv_w, m_rg_conv_b, m_rg_wa, m_rg_ba, m_rg_wx, m_rg_bx, m_rg_lambda, m_rg_w_out, m_gla_w_in, m_gla_w_alpha, m_gla_b_alpha, m_gla_norm_g, m_gla_w_out, v_ada_w, v_ada_b, v_norm_g, v_ffn_w_up, v_ffn_conv_w, v_ffn_conv_b, v_ffn_w_down, v_rg_w_in, v_rg_conv_w, v_rg_conv_b, v_rg_wa, v_rg_ba, v_rg_wx, v_rg_bx, v_rg_lambda, v_rg_w_out, v_gla_w_in, v_gla_w_alpha, v_gla_b_alpha, v_gla_norm_g, v_gla_w_out):
    wts = dict(ada_w=ada_w, ada_b=ada_b, norm_g=norm_g, ffn_w_up=ffn_w_up, ffn_conv_w=ffn_conv_w, ffn_conv_b=ffn_conv_b,
               ffn_w_down=ffn_w_down, rg_w_in=rg_w_in, rg_conv_w=rg_conv_w, rg_conv_b=rg_conv_b, rg_wa=rg_wa, rg_ba=rg_ba,
               rg_wx=rg_wx, rg_bx=rg_bx, rg_lambda=rg_lambda, rg_w_out=rg_w_out, gla_w_in=gla_w_in, gla_w_alpha=gla_w_alpha,
               gla_b_alpha=gla_b_alpha, gla_norm_g=gla_norm_g, gla_w_out=gla_w_out)
    mom1 = dict(ada_w=m_ada_w, ada_b=m_ada_b, norm_g=m_norm_g, ffn_w_up=m_ffn_w_up, ffn_conv_w=m_ffn_conv_w,
                ffn_conv_b=m_ffn_conv_b, ffn_w_down=m_ffn_w_down, rg_w_in=m_rg_w_in, rg_conv_w=m_rg_conv_w,
                rg_conv_b=m_rg_conv_b, rg_wa=m_rg_wa, rg_ba=m_rg_ba, rg_wx=m_rg_wx, rg_bx=m_rg_bx, rg_lambda=m_rg_lambda,
                rg_w_out=m_rg_w_out, gla_w_in=m_gla_w_in, gla_w_alpha=m_gla_w_alpha, gla_b_alpha=m_gla_b_alpha,
                gla_norm_g=m_gla_norm_g, gla_w_out=m_gla_w_out)
    mom2 = dict(ada_w=v_ada_w, ada_b=v_ada_b, norm_g=v_norm_g, ffn_w_up=v_ffn_w_up, ffn_conv_w=v_ffn_conv_w,
                ffn_conv_b=v_ffn_conv_b, ffn_w_down=v_ffn_w_down, rg_w_in=v_rg_w_in, rg_conv_w=v_rg_conv_w,
                rg_conv_b=v_rg_conv_b, rg_wa=v_rg_wa, rg_ba=v_rg_ba, rg_wx=v_rg_wx, rg_bx=v_rg_bx, rg_lambda=v_rg_lambda,
                rg_w_out=v_rg_w_out, gla_w_in=v_gla_w_in, gla_w_alpha=v_gla_w_alpha, gla_b_alpha=v_gla_b_alpha,
                gla_norm_g=v_gla_norm_g, gla_w_out=v_gla_w_out)
    xi, yi, ci = _place()
    chip, me = 2 * xi + yi, 4 * xi + 2 * yi + ci
    d = x.shape[-1]
    depth = ada_w.shape[0]
    n_ada = ada_w.shape[-1]
    sharded_small = [k for k, ax in SMALL.items() if ax is not None]

    sm = _all_gather_8(_pack([c] + [wts[k] for k in sharded_small], SUBLANES, F32), "gather_small")
    c_all = sm[:, 0, :]
    parts = _unpack(sm[0::2], [c.shape] + [wts[k].shape for k in sharded_small])[1:]
    full = {k: _join_shards(p, SMALL[k]) for k, p in zip(sharded_small, parts)}
    for k, ax in SMALL.items():
        if ax is None:
            full[k] = wts[k]

    c16 = jnp.pad(c_all, ((0, ADA_ROWS - N_DEV), (0, 0)))
    ada_b_mine = lax.dynamic_slice_in_dim(ada_b, chip * n_ada, n_ada, 1)[:, None, :]
    mod_cols = _ada_fwd(c16, ada_w, ada_b_mine, "ada_fwd")
    mod_all = _all_gather_8(mod_cols.reshape(-1, PACK_COLS), "gather_mod")[0::2].reshape(N_CHIP, depth, ADA_ROWS, n_ada)
    mod = jnp.swapaxes(lax.dynamic_index_in_dim(mod_all, me, 2, keepdims=False), 0, 1).reshape(depth, 6, d)

    items = [(k, l) for k in BIG for l in range(wts[k].shape[0])]
    stage_of = lambda k, l: "rg" if k.startswith("rg_") else ("ffn0" if (k.startswith("ffn_") and l == 0) else "l1")
    staged = {st: [it for it in items if stage_of(*it) == st] for st in ("rg", "ffn0", "l1")}
    staged["gla"] = [it for it in staged["l1"] if it[0].startswith("gla_")]
    staged["ffn1"] = [it for it in staged["l1"] if it[0].startswith("ffn_")]
    staged["l1"] = staged["gla"] + staged["ffn1"]
    shard = lambda k, l: wts[k][l].reshape(-1, wts[k].shape[-1]).astype(BF16)
    own = lambda got, mine: [lax.dynamic_update_index_in_dim(g, m, chip, 0) for g, m in zip(got, mine)]
    rows_joined = lambda v: v.reshape(-1, v.shape[-1])

    def placed(its, slots):
        out = {"ffn_w_up": {}, "ffn_w_down": {}}
        for (k, l), v in zip(its, slots):
            if k == "ffn_w_up":
                out[k][l] = v
            elif k == "ffn_w_down":
                out[k][l] = rows_joined(v)
            elif k in ("rg_wa", "rg_wx"):
                out[k] = _slots_to_block_rows(v, RG_BLOCKS)
            elif k == "gla_w_in":
                out[k] = _from_col_slots(v, "gla_w_in_join")
            else:
                out[k] = v if BIG[k] else rows_joined(v)
        return out

    after_mod = (mod[0, 0, 0] * 0.0).astype(BF16)
    sh_rg = [shard(k, l) + after_mod for k, l in staged["rg"]]
    local = {k: (v if k in ("norm_g", "ffn_conv_w", "ffn_conv_b") else v[0]) for k, v in full.items()}
    local.update(placed(staged["rg"], own(_gather_chips(sh_rg, "gather_weights_rg"), sh_rg)))
    sh_late, flying = {}, {}
    after_rg = (local["rg_w_out"][0, 0].astype(F32) * 0.0).astype(BF16)
    sh_late["ffn0"] = [shard(k, l) + after_rg for k, l in staged["ffn0"]]
    flying["ffn0"], tok = _send_start(sh_late["ffn0"], "whole", "weights_ffn0_start")
    for stage in ("gla", "ffn1"):
        sh_late[stage] = [shard(k, l) + tok[0, 0].astype(BF16) for k, l in staged[stage]]
        flying[stage], tok = _send_start(sh_late[stage], "whole", f"weights_{stage}_start")
    mod = mod + tok[0, 0]

    def fetch(stage, after):
        mine, got = _send_wait(flying[stage], after, "whole", f"weights_{stage}_wait")
        return placed(staged[stage], own(got, mine))

    c_idx = ci.reshape(1).astype(jnp.int32)
    gslots, paired, psums, sent, started = {}, {}, {}, {}, {}

    def grad_slots(gr, k, l):
        g = gr[k][l] if k in ("ffn_w_up", "ffn_w_down") else gr[k]
        if k in ("rg_wa", "rg_wx"):
            return _block_rows_to_slots(g)
        if k == "gla_w_in":
            return _col_slots(g, "gla_w_in_grad_slots")
        return g if BIG[k] else g.reshape(N_CHIP, -1, g.shape[-1])

    def done(stage, gr):
        gslots[stage] = [grad_slots(gr, k, l) for k, l in staged[stage]]
        paired[stage], token = _send_start(gslots[stage], "pair", f"grads_{stage}_pair_start")
        return token[0, 0]

    def later(stage, after):
        mine, theirs = _send_wait(paired[stage], after, "pair", f"grads_{stage}_pair_wait")
        psums[stage] = [_pair_sum(g, t, c_idx, f"grads_pair_sum_{k}{l}") for (k, l), g, t in zip(staged[stage], mine, theirs)]
        sent[stage], started[stage] = _send_start(psums[stage], "slots", f"grads_{stage}_start")
        return started[stage][0, 0]

    cols, grad_x, gr = _local_step(x[0], loss_target[0], mod, local, fetch, done, later)
    loss_mine = (0.5 * jnp.sum(cols) / d).reshape(1)

    small_names = [k for k in SMALL if k != "ada_b"]
    small_flying, small_sent = _send_start([_pack([gr[k] for k in small_names] + [gr["mod"], loss_mine], SUBLANES, F32)], "all",
                                           "grads_small_start")
    small_shapes = [full[k].shape for k in small_names] + [(depth, 6 * d), (1,)]
    chip_idx = chip.reshape(1).astype(jnp.int32)
    delta, new_m, new_v = {}, {}, {}
    grads = {}

    def reduce_and_update(stages, after, dep):
        its = [(st, n) for st in stages for n in range(len(staged[st]))]
        back = {st: _send_wait(sent[st], after, "slots", f"grads_{st}_wait") for st in stages}
        halves = [_chip_sum(back[st][1][n], back[st][0][n], chip_idx, "grads_chip_sum_%s%d" % staged[st][n]) for st, n in its]
        shared = _pair_share(halves, "grads_pair_share_" + stages[0])
        reduced = [lax.dynamic_update_index_in_dim(s2, h, ci, 0).reshape(-1, h.shape[-1]) for s2, h in zip(shared, halves)]
        last = None
        for k in BIG:
            gs_k = [g for (st, n), g in zip(its, reduced) if staged[st][n][0] == k]
            if gs_k:
                last = update(k, gs_k, dep)
        return last

    def update(k, gs_k, dep=None):
        shp = wts[k].shape
        if k == "gla_w_in":
            view, back = (lambda a: jnp.swapaxes(a, 1, 2)), (lambda o: jnp.swapaxes(o, 1, 2))
            gs_k = [g.T for g in gs_k]
        else:
            view, back = (lambda a: a.reshape(a.shape[0], -1, a.shape[-1])), (lambda o: o.reshape(shp))
        outs = _adamw(view(wts[k]), gs_k, view(mom1[k]), view(mom2[k]), "adamw_" + k, dep)
        grads[k], delta[k], new_m[k], new_v[k] = (back(o) for o in outs)
        return new_v[k]

    later("rg", small_sent)
    done_late = reduce_and_update(("ffn0", "l1"), grad_x, started["rg"])
    (small_mine,), (gs,) = _send_wait(small_flying, done_late, "all", "grads_small_wait")
    gs = lax.dynamic_update_index_in_dim(gs, small_mine, 2 * chip + ci, 0)
    *small_sum, g_ada_b, loss = _unpack(_sum_lead(gs, "sum_small_grads"), small_shapes)
    loss = loss[0]
    grads.update(zip(small_names, small_sum))
    grads["ada_b"] = g_ada_b
    for k in sharded_small:
        grads[k] = _my_shard(grads[k], SMALL[k], chip)
    dmod_all = _unpack(gs, small_shapes)[-2].reshape(N_DEV, depth, N_CHIP, n_ada)
    dmod_mine = jnp.swapaxes(lax.dynamic_index_in_dim(dmod_all, chip, 2, keepdims=False), 0, 1)
    g_ada_w = _ada_bwd(c16, jnp.pad(dmod_mine, ((0, 0), (0, ADA_ROWS - N_DEV), (0, 0))), "ada_bwd")
    update("ada_w", g_ada_w)
    small_shard_shapes = [wts[k].shape for k in SMALL]
    packed = [_pack([src[k] for k in SMALL], SUBLANES, F32) for src in (wts, grads, mom1, mom2)]
    outs = _adamw(packed[0][None], [packed[1]], packed[2][None], packed[3][None], "adamw_small")
    for dst, o in zip((delta, new_m, new_v), outs[1:]):
        for k, a in zip(SMALL, _unpack(o[0], small_shard_shapes)):
            dst[k] = a
    reduce_and_update(("rg",), outs[3], None)

    return (loss, grad_x[None], *[grads[k] for k in WEIGHTS], *[delta[k] for k in WEIGHTS], *[new_m[k] for k in WEIGHTS],
            *[new_v[k] for k in WEIGHTS])
```

```python
import jax
import jax.numpy as jnp
from jax import lax
from jax.experimental import pallas as pl
from jax.experimental.pallas import tpu as pltpu

F32 = jnp.float32
BF16 = jnp.bfloat16
MXU_DTYPE = BF16

EPS = 1e-6
RG_C = 8.0
RG_BLOCKS = 4
RG_CONV = 4
GLA_HEADS = 4
GLA_TAU = 16.0
GLA_CHUNK = 64
GLA_RANK = 16
FFN_CONV = 3
ADAM_LR = 0.001
ADAM_B1 = 0.9
ADAM_B2 = 0.999
ADAM_EPS = 1e-08
ADAM_WD = 0.01
ADAM_STEP = 10

LANES = 128
SUBLANES = 8
VMEM_LIMIT = 56 * 1024 * 1024
CB = 256
MESH = pl.DeviceIdType.MESH
N_DEV = 8
N_CHIP = 4


def _params(*sem):
    return pltpu.CompilerParams(dimension_semantics=sem, vmem_limit_bytes=VMEM_LIMIT)


def _tile(dim, prefs):
    for p in prefs:
        if dim % p == 0:
            return p
    return dim


def _dot(a, b, dims):
    return lax.dot_general(a.astype(MXU_DTYPE), b.astype(MXU_DTYPE), (dims, ((), ())), preferred_element_type=F32)


def _dot_nn(a, b):
    return _dot(a, b, ((1,), (0,)))


def _dot_nt(a, b):
    return _dot(a, b, ((1,), (1,)))


def _dot_tn(a, b):
    return _dot(a, b, ((0,), (0,)))


def _mm(a, b, *, ta=False, tb=False, a_parts=1, b_parts=1, w_slots=1, out_slots=1, out_dtype=F32, tm_max=1408, name):
    if ta:
        k_dim, m_dim = a.shape
        n_dim = b.shape[-1] * b_parts
    else:
        m_dim, k_dim = a.shape[-2], a.shape[-1] * a_parts
        n_dim = b.shape[-2] if tb else b.shape[-1] * w_slots
    n_unit = n_dim // max(b_parts, out_slots, 1 if tb else w_slots)
    k_unit = k_dim // max(a_parts, w_slots if tb else 1)
    tm = _tile(m_dim, tuple(t for t in (1024, 1408, 512, 256, 128) if t <= max(tm_max, 128)))
    tn = _tile(n_unit, (1024, 1408, 896, 512, 256, 128))
    tk = _tile(k_unit, (1024, 1408, 896, 512, 256, 128))
    nk = k_dim // tk
    dims = ((0 if ta else 1,), (1 if tb else 0,))

    def spec(shape, parts, total, tile, col_grid, row_grid):
        per = total // parts // tile

        def index(i, j, k):
            g = {"i": i, "j": j, "k": k}
            col, row = g[col_grid], g[row_grid]
            return (row, col) if parts == 1 else (col // per, row, col % per)

        return pl.BlockSpec(shape if parts == 1 else (None,) + shape, index)

    def body(a_ref, b_ref, o_ref, *acc):
        if nk == 1:
            o_ref[...] = _dot(a_ref[...], b_ref[...], dims).astype(o_ref.dtype)
            return
        acc_ref, k = acc[0], pl.program_id(2)

        @pl.when(k == 0)
        def _():
            acc_ref[...] = jnp.zeros_like(acc_ref)

        acc_ref[...] += _dot(a_ref[...], b_ref[...], dims)

        @pl.when(k == nk - 1)
        def _():
            o_ref[...] = acc_ref[...].astype(o_ref.dtype)

    if ta:
        a_spec = spec((tk, tm), 1, m_dim, tm, "i", "k")
        b_spec = spec((tk, tn), b_parts, n_dim, tn, "j", "k")
    elif tb:
        a_spec = spec((tm, tk), a_parts, k_dim, tk, "k", "i")
        b_spec = spec((tn, tk), w_slots, k_dim, tk, "k", "j")
    else:
        a_spec = spec((tm, tk), a_parts, k_dim, tk, "k", "i")
        b_spec = spec((tk, tn), w_slots, n_dim, tn, "j", "k")
    out_shape = (m_dim, n_dim) if out_slots == 1 else (out_slots, m_dim, n_dim // out_slots)
    return pl.pallas_call(
        body,
        grid=(m_dim // tm, n_dim // tn, nk),
        in_specs=[a_spec, b_spec],
        out_specs=spec((tm, tn), out_slots, n_dim, tn, "j", "i"),
        out_shape=jax.ShapeDtypeStruct(out_shape, out_dtype),
        scratch_shapes=[pltpu.VMEM((tm, tn), F32)] if nk > 1 else [],
        compiler_params=_params("parallel", "parallel", "arbitrary"),
        name=name,
    )(a, b)


ROW_TILES = (1024, 512)


def _row_specs(s, d, ts):
    return pl.BlockSpec((ts, d), lambda i: (i, 0)), pl.BlockSpec((1, d), lambda i: (0, 0))


def _norm_mod_fwd(x, g, sc, sh, name):
    s, d = x.shape
    ts = _tile(s, ROW_TILES)

    def body(x_ref, g_ref, sc_ref, sh_ref, h_ref):
        xv = x_ref[...]
        r = lax.rsqrt(jnp.mean(xv * xv, axis=-1, keepdims=True) + EPS)
        h_ref[...] = (((xv * r) * g_ref[...]) * (1.0 + sc_ref[...]) + sh_ref[...]).astype(h_ref.dtype)

    row, vec = _row_specs(s, d, ts)
    return pl.pallas_call(
        body, grid=(s // ts,), in_specs=[row, vec, vec, vec], out_specs=row,
        out_shape=jax.ShapeDtypeStruct((s, d), MXU_DTYPE), compiler_params=_params("parallel"), name=name,
    )(x, g, sc, sh)


def _norm_mod_bwd(dh, x, g, sc, dres, name):
    s, d = x.shape
    ts = _tile(s, ROW_TILES)

    def body(dh_ref, x_ref, g_ref, sc_ref, dres_ref, dx_ref, dg_ref, dsc_ref, dsh_ref, acc_ref):
        i = pl.program_id(0)

        @pl.when(i == 0)
        def _():
            acc_ref[...] = jnp.zeros_like(acc_ref)

        xv, dhv = x_ref[...], dh_ref[...]
        r = lax.rsqrt(jnp.mean(xv * xv, axis=-1, keepdims=True) + EPS)
        n = xv * r
        acc_ref[0:1, :] += jnp.sum(dhv * n, axis=0, keepdims=True)
        acc_ref[1:2, :] += jnp.sum(dhv, axis=0, keepdims=True)
        dn = dhv * ((1.0 + sc_ref[...]) * g_ref[...])
        dx_ref[...] = dres_ref[...] + r * (dn - n * jnp.mean(dn * n, axis=-1, keepdims=True))
        dg_ref[...] = (1.0 + sc_ref[...]) * acc_ref[0:1, :]
        dsc_ref[...] = g_ref[...] * acc_ref[0:1, :]
        dsh_ref[...] = acc_ref[1:2, :]

    row, vec = _row_specs(s, d, ts)
    vshape = jax.ShapeDtypeStruct((1, d), F32)
    return pl.pallas_call(
        body, grid=(s // ts,), in_specs=[row, row, vec, vec, row], out_specs=[row, vec, vec, vec],
        out_shape=[jax.ShapeDtypeStruct((s, d), F32), vshape, vshape, vshape],
        scratch_shapes=[pltpu.VMEM((SUBLANES, d), F32)], compiler_params=_params("arbitrary"), name=name,
    )(dh, x, g, sc, dres)


def _post_norm_fwd(x, y, g, gt, g2, sc, sh, name):
    s, d = x.shape
    ts = _tile(s, ROW_TILES)

    def body(x_ref, y_ref, g_ref, gt_ref, g2_ref, sc_ref, sh_ref, o_ref, h_ref):
        yv = y_ref[...]
        r = lax.rsqrt(jnp.mean(yv * yv, axis=-1, keepdims=True) + EPS)
        xn = x_ref[...] + gt_ref[...] * ((yv * r) * g_ref[...])
        o_ref[...] = xn
        r2 = lax.rsqrt(jnp.mean(xn * xn, axis=-1, keepdims=True) + EPS)
        h_ref[...] = (((xn * r2) * g2_ref[...]) * (1.0 + sc_ref[...]) + sh_ref[...]).astype(h_ref.dtype)

    row, vec = _row_specs(s, d, ts)
    return pl.pallas_call(
        body, grid=(s // ts,), in_specs=[row, row] + [vec] * 5, out_specs=[row, row],
        out_shape=[jax.ShapeDtypeStruct((s, d), F32), jax.ShapeDtypeStruct((s, d), MXU_DTYPE)],
        compiler_params=_params("parallel"), name=name,
    )(x, y, g, gt, g2, sc, sh)


def _post_bwd(dxn, y, g, gt, name):
    s, d = y.shape
    ts = _tile(s, ROW_TILES)

    def body(dxn_ref, y_ref, g_ref, gt_ref, dy_ref, dg_ref, dgt_ref, acc_ref):
        i = pl.program_id(0)

        @pl.when(i == 0)
        def _():
            acc_ref[...] = jnp.zeros_like(acc_ref)

        yv, dv = y_ref[...], dxn_ref[...]
        r = lax.rsqrt(jnp.mean(yv * yv, axis=-1, keepdims=True) + EPS)
        n = yv * r
        acc_ref[0:1, :] += jnp.sum(dv * n, axis=0, keepdims=True)
        dn = dv * (gt_ref[...] * g_ref[...])
        dy_ref[...] = (r * (dn - n * jnp.mean(dn * n, axis=-1, keepdims=True))).astype(dy_ref.dtype)
        dg_ref[...] = gt_ref[...] * acc_ref[0:1, :]
        dgt_ref[...] = g_ref[...] * acc_ref[0:1, :]

    row, vec = _row_specs(s, d, ts)
    vshape = jax.ShapeDtypeStruct((1, d), F32)
    return pl.pallas_call(
        body, grid=(s // ts,), in_specs=[row, row, vec, vec], out_specs=[row, vec, vec],
        out_shape=[jax.ShapeDtypeStruct((s, d), MXU_DTYPE), vshape, vshape],
        scratch_shapes=[pltpu.VMEM((SUBLANES, d), F32)], compiler_params=_params("arbitrary"), name=name,
    )(dxn, y, g, gt)


def _norm_post_bwd(dh, x, g, sc, dres, y, gp, gt, name):
    s, d = x.shape
    ts = _tile(s, (512,))

    def body(dh_ref, x_ref, g_ref, sc_ref, dres_ref, y_ref, gp_ref, gt_ref,
             dx_ref, dy_ref, dg_ref, dsc_ref, dsh_ref, dgp_ref, dgt_ref, acc_ref):
        i = pl.program_id(0)

        @pl.when(i == 0)
        def _():
            acc_ref[...] = jnp.zeros_like(acc_ref)

        xv, dhv = x_ref[...], dh_ref[...]
        r = lax.rsqrt(jnp.mean(xv * xv, axis=-1, keepdims=True) + EPS)
        n = xv * r
        acc_ref[0:1, :] += jnp.sum(dhv * n, axis=0, keepdims=True)
        acc_ref[1:2, :] += jnp.sum(dhv, axis=0, keepdims=True)
        dn = dhv * ((1.0 + sc_ref[...]) * g_ref[...])
        dx = dres_ref[...] + r * (dn - n * jnp.mean(dn * n, axis=-1, keepdims=True))
        dx_ref[...] = dx
        yv = y_ref[...]
        ry = lax.rsqrt(jnp.mean(yv * yv, axis=-1, keepdims=True) + EPS)
        ny = yv * ry
        acc_ref[2:3, :] += jnp.sum(dx * ny, axis=0, keepdims=True)
        dny = dx * (gt_ref[...] * gp_ref[...])
        dy_ref[...] = (ry * (dny - ny * jnp.mean(dny * ny, axis=-1, keepdims=True))).astype(dy_ref.dtype)
        dg_ref[...] = (1.0 + sc_ref[...]) * acc_ref[0:1, :]
        dsc_ref[...] = g_ref[...] * acc_ref[0:1, :]
        dsh_ref[...] = acc_ref[1:2, :]
        dgp_ref[...] = gt_ref[...] * acc_ref[2:3, :]
        dgt_ref[...] = gp_ref[...] * acc_ref[2:3, :]

    row, vec = _row_specs(s, d, ts)
    vshape = jax.ShapeDtypeStruct((1, d), F32)
    return pl.pallas_call(
        body, grid=(s // ts,), in_specs=[row, row, vec, vec, row, row, vec, vec], out_specs=[row, row] + [vec] * 5,
        out_shape=[jax.ShapeDtypeStruct((s, d), F32), jax.ShapeDtypeStruct((s, d), MXU_DTYPE)] + [vshape] * 5,
        scratch_shapes=[pltpu.VMEM((SUBLANES, d), F32)], compiler_params=_params("arbitrary"), name=name,
    )(dh, x, g, sc, dres, y, gp, gt)


def _post_loss(x, y, g, gt, tgt, name):
    s, d = x.shape
    ts = _tile(s, ROW_TILES)

    def body(x_ref, y_ref, g_ref, gt_ref, t_ref, col_ref, dx_ref):
        i = pl.program_id(0)

        @pl.when(i == 0)
        def _():
            col_ref[...] = jnp.zeros_like(col_ref)

        yv = y_ref[...]
        r = lax.rsqrt(jnp.mean(yv * yv, axis=-1, keepdims=True) + EPS)
        e = (x_ref[...] + gt_ref[...] * ((yv * r) * g_ref[...])) - t_ref[...]
        col_ref[...] += jnp.sum(e * e, axis=0, keepdims=True)
        dx_ref[...] = e * (1.0 / d)

    row, vec = _row_specs(s, d, ts)
    return pl.pallas_call(
        body, grid=(s // ts,), in_specs=[row, row, vec, vec, row], out_specs=[vec, row],
        out_shape=[jax.ShapeDtypeStruct((1, d), F32), jax.ShapeDtypeStruct((s, d), F32)],
        compiler_params=_params("arbitrary"), name=name,
    )(x, y, g, gt, tgt)


_GELU_C = 0.7978845608028654
_GELU_A = 0.044715


def _gelu(x):
    t = jnp.tanh(_GELU_C * (x + _GELU_A * x * x * x))
    return 0.5 * x * (1.0 + t), t


def _gelu_grad(x, t):
    return 0.5 * (1.0 + t) + 0.5 * x * (1.0 - t * t) * (_GELU_C * (1.0 + 3.0 * _GELU_A * x * x))


def _sigmoid(x):
    return 1.0 / (1.0 + jnp.exp(-x))


def _log1p_pos(y):
    u = 1.0 + y
    return jnp.where(u == 1.0, y, jnp.log(u) * (y / jnp.where(u == 1.0, 1.0, u - 1.0)))


def _softplus(x):
    return jnp.maximum(x, 0.0) + _log1p_pos(jnp.exp(-jnp.abs(x)))


def _one_minus_sq_exp(x, ex):
    z = 2.0 * x
    series = -z * (1.0 + z * (1.0 / 2 + z * (1.0 / 6 + z * (1.0 / 24 + z * (1.0 / 120)))))
    return jnp.where(z > -0.05, series, 1.0 - ex * ex)


SLAB = 16


def _cat(a, b):
    return jnp.concatenate([a, b], axis=1)


def _pair_specs(shape, nb, index):
    return [pl.BlockSpec(shape, lambda j, t: index(j, t) + (j,)), pl.BlockSpec(shape, lambda j, t: index(j, t) + (j + nb,))]


def _halo_row(ts, time_of):
    return lambda j, t: (jnp.maximum(time_of(t) * (ts // SUBLANES) - 1, 0),)


def _rows_from(groups, k):
    row = lax.broadcasted_iota(jnp.int32, groups[0].shape, 0)
    turned = [pltpu.roll(g, SUBLANES - k, axis=0) for g in groups]
    return [jnp.where(row < SUBLANES - k, lo, hi) for lo, hi in zip(turned[:-1], turned[1:])]


def _ffn_mid_fwd(p, cw, cb, name):
    s, f2 = p.shape
    ts = _tile(s, (1024, 512))
    nb, nt = f2 // (2 * CB), s // ts
    n_grp = SLAB // SUBLANES

    def body(pg_ref, pv_ref, hg_ref, hv_ref, cwg_ref, cwv_ref, cbg_ref, cbv_ref, a_ref, ga_ref, gb_ref):
        t = pl.program_id(1)
        cwv, bias = _cat(cwg_ref[...], cwv_ref[...]), _cat(cbg_ref[...], cbv_ref[...])
        w0, w1, w2 = cwv[0:1], cwv[1:2], cwv[2:3]

        def slab(before, cur, r0):
            pm2, pm1 = _rows_from([before] + cur, SUBLANES - 2), _rows_from([before] + cur, SUBLANES - 1)
            u = jnp.concatenate([bias + w0 * pm2[i] + w1 * pm1[i] + w2 * cur[i] for i in range(n_grp)], axis=0)
            g, v = u[:, :CB], u[:, CB:]
            gel, th = _gelu(g)
            rows = pl.ds(r0, SLAB)
            a_ref[rows, :] = (gel * v).astype(a_ref.dtype)
            ga_ref[rows, :] = gel.astype(ga_ref.dtype)
            gb_ref[rows, :] = (v * _gelu_grad(g, th)).astype(gb_ref.dtype)

        def pieces(rows):
            blk = _cat(pg_ref[rows, :], pv_ref[rows, :])
            return [blk[i * SUBLANES:(i + 1) * SUBLANES] for i in range(blk.shape[0] // SUBLANES)]

        slab(jnp.where(t > 0, _cat(hg_ref[...], hv_ref[...]), 0.0), pieces(pl.ds(0, SLAB)), 0)

        def loop(i, carry):
            r0 = pl.multiple_of(i * SLAB, SLAB)
            got = pieces(pl.ds(pl.multiple_of(r0 - SUBLANES, SUBLANES), SLAB + SUBLANES))
            slab(got[0], got[1:], r0)
            return carry

        lax.fori_loop(1, ts // SLAB, loop, 0, unroll=2)

    fwd = lambda t: t
    out = pl.BlockSpec((ts, CB), lambda j, t: (t, j))
    shape = jax.ShapeDtypeStruct((s, f2 // 2), MXU_DTYPE)
    return pl.pallas_call(
        body, grid=(nb, nt),
        in_specs=(_pair_specs((ts, CB), nb, lambda j, t: (t,)) + _pair_specs((SUBLANES, CB), nb, _halo_row(ts, fwd))
                  + _pair_specs((FFN_CONV, CB), nb, lambda j, t: (0,)) + _pair_specs((1, CB), nb, lambda j, t: (0,))),
        out_specs=[out, out, out], out_shape=[shape, shape, shape],
        compiler_params=_params("parallel", "arbitrary"), name=name,
    )(p, p, p, p, cw, cw, cb, cb)


def _ffn_mid_bwd(da, p, ga, gb, cw, name):
    s, f2 = p.shape
    ts = _tile(s, (1024, 512))
    nb, nt = f2 // (2 * CB), s // ts
    n_slab = ts // SLAB
    n_grp = SLAB // SUBLANES
    per_trip = 2

    def body(da_ref, ga_ref, gb_ref, pg_ref, pv_ref, cwg_ref, cwv_ref, dp_ref, dcw_ref, dcb_ref, next_du, acc):
        tt = pl.program_id(1)
        cwv = _cat(cwg_ref[...], cwv_ref[...])
        w0, w1, w2 = cwv[0:1], cwv[1:2], cwv[2:3]

        @pl.when(tt == 0)
        def _():
            next_du[...] = jnp.zeros_like(next_du)
            acc[...] = jnp.zeros_like(acc)

        def slab(r0, after, sums):
            rows = pl.ds(r0, SLAB)
            dav = da_ref[rows, :]
            du = _cat(dav * gb_ref[rows, :].astype(F32), dav * ga_ref[rows, :].astype(F32))
            p0 = _cat(pg_ref[rows, :], pv_ref[rows, :])
            cur = [du[i * SUBLANES:(i + 1) * SUBLANES] for i in range(n_grp)]
            du1, du2 = _rows_from(cur + [after], 1), _rows_from(cur + [after], 2)
            dpv = jnp.concatenate([w2 * cur[i] + w1 * du1[i] + w0 * du2[i] for i in range(n_grp)], axis=0).astype(dp_ref.dtype)
            dp_ref[0, rows, :] = dpv[:, :CB]
            dp_ref[1, rows, :] = dpv[:, CB:]
            for i in range(n_grp):
                pi = p0[i * SUBLANES:(i + 1) * SUBLANES]
                parts = (cur[i], du2[i] * pi, du1[i] * pi, cur[i] * pi)
                sums = parts if sums is None else tuple(x + y for x, y in zip(sums, parts))
            return cur[0], sums

        def loop(k, after):
            sums = None
            for j in range(per_trip):
                r0 = pl.multiple_of((n_slab - 1 - (k * per_trip + j)) * SLAB, SLAB)
                after, sums = slab(r0, after, sums)
            for q, part in enumerate(sums):
                acc[q] += part
            return after

        next_du[...] = lax.fori_loop(0, n_slab // per_trip, loop, next_du[...])

        @pl.when(tt == nt - 1)
        def _():
            for half in range(2):
                cols = slice(half * CB, (half + 1) * CB)
                dcb_ref[half] = jnp.sum(acc[0][:, cols], axis=0, keepdims=True)
                for k in range(FFN_CONV):
                    dcw_ref[half, k:k + 1, :] = jnp.sum(acc[1 + k][:, cols], axis=0, keepdims=True)

    rev = lambda t: nt - 1 - t
    tile = pl.BlockSpec((ts, CB), lambda j, t: (rev(t), j))
    return pl.pallas_call(
        body, grid=(nb, nt),
        in_specs=([tile, tile, tile] + _pair_specs((ts, CB), nb, lambda j, t: (rev(t),))
                  + _pair_specs((FFN_CONV, CB), nb, lambda j, t: (0,))),
        out_specs=[pl.BlockSpec((2, ts, CB), lambda j, t: (0, rev(t), j)),
                   pl.BlockSpec((2, FFN_CONV, CB), lambda j, t: (0, 0, j)),
                   pl.BlockSpec((2, 1, CB), lambda j, t: (0, 0, j))],
        out_shape=[jax.ShapeDtypeStruct((2, s, f2 // 2), MXU_DTYPE), jax.ShapeDtypeStruct((2, FFN_CONV, f2 // 2), F32),
                   jax.ShapeDtypeStruct((2, 1, f2 // 2), F32)],
        scratch_shapes=[pltpu.VMEM((SUBLANES, 2 * CB), F32), pltpu.VMEM((1 + FFN_CONV, SUBLANES, 2 * CB), F32)],
        compiler_params=_params("parallel", "arbitrary"), name=name,
    )(da, ga, gb, p, p, cw, cw)


def _rg_gates(xc, wa_ref, ba_ref, wx_ref, bx_ref, lam_ref):
    r = _sigmoid(_dot_nn(xc, wa_ref[0]) + ba_ref[...])
    ig = _sigmoid(_dot_nn(xc, wx_ref[0]) + bx_ref[...])
    sp = _softplus(-lam_ref[...])
    log_a = (-RG_C) * r * sp
    a = jnp.exp(log_a)
    mult = jnp.sqrt(_one_minus_sq_exp(log_a, a))
    return r, ig, sp, a, mult


def _rg_conv(scr, cw_ref, cb_ref, ts):
    views = [scr[5 + k:5 + k + ts, :] for k in range(RG_CONV)]
    xc = cb_ref[...]
    for k in range(RG_CONV):
        xc = xc + cw_ref[k:k + 1, :] * views[k]
    return xc, views


def _rg_param_specs():
    vec = pl.BlockSpec((1, CB), lambda g, t: (0, g))
    mat = pl.BlockSpec((1, CB, CB), lambda g, t: (g, 0, 0))
    return [pl.BlockSpec((RG_CONV, CB), lambda g, t: (0, g)), vec, mat, vec, mat, vec, vec]


def _scan_rows(a_scr, x_scr, out_ref, carry, ts, reverse):
    n = ts // SUBLANES
    row = lax.broadcasted_iota(jnp.int32, (SUBLANES, a_scr.shape[1]), 0)
    last = SUBLANES - 1

    def rows_of(k):
        return pl.ds(pl.multiple_of(k * SUBLANES, SUBLANES), SUBLANES)

    def local(k, _):
        rows = rows_of(k)
        a, x = a_scr[rows, :], x_scr[rows, :]
        if reverse:
            a = jnp.where(row == last, 1.0, pltpu.roll(a, last, axis=0))
            for sh in (1, 2, 4):
                keep = row < SUBLANES - sh
                x = x + a * jnp.where(keep, pltpu.roll(x, SUBLANES - sh, axis=0), 0.0)
                a = a * jnp.where(keep, pltpu.roll(a, SUBLANES - sh, axis=0), 1.0)
        else:
            for sh in (1, 2, 4):
                keep = row >= sh
                x = a * jnp.where(keep, pltpu.roll(x, sh, axis=0), 0.0) + x
                a = a * jnp.where(keep, pltpu.roll(a, sh, axis=0), 1.0)
        out_ref[rows, :] = x
        x_scr[rows, :] = a
        return 0

    lax.fori_loop(0, n, local, 0, unroll=4)

    def chain(k, c):
        rows = rows_of(n - 1 - k if reverse else k)
        v = out_ref[rows, :] + x_scr[rows, :] * c
        out_ref[rows, :] = v
        return a_scr[rows, :][0:1] * v[0:1] if reverse else v[last:last + 1]

    return lax.fori_loop(0, n, chain, carry, unroll=4)


def _rg_mid_fwd(pj, cw, cb, wa, ba, wx, bx, lam, name):
    s = pj.shape[0]
    nb = pj.shape[1] // (2 * CB)
    ts = _tile(s, (512,))
    nt = s // ts

    def body(gate_ref, x_ref, halo_ref, cw_ref, cb_ref, wa_ref, ba_ref, wx_ref, bx_ref, lam_ref, y_ref, hs_ref,
             scr, a_scr, u_scr, h_scr):
        t = pl.program_id(1)

        @pl.when(t == 0)
        def _():
            h_scr[...] = jnp.zeros_like(h_scr)

        scr[0:SUBLANES, :] = jnp.where(t > 0, halo_ref[...], 0.0)
        scr[SUBLANES:, :] = x_ref[...]
        xc, _ = _rg_conv(scr, cw_ref, cb_ref, ts)
        _, ig, _, a, mult = _rg_gates(xc, wa_ref, ba_ref, wx_ref, bx_ref, lam_ref)
        a_scr[...] = a
        u_scr[...] = mult * (ig * xc)
        h_scr[0:1, :] = _scan_rows(a_scr, u_scr, hs_ref, h_scr[0:1, :], ts, False)
        y_ref[...] = (_gelu(gate_ref[...])[0] * hs_ref[...]).astype(y_ref.dtype)

    blk = pl.BlockSpec((ts, CB), lambda g, t: (t, g))
    return pl.pallas_call(
        body, grid=(nb, nt),
        in_specs=_pair_specs((ts, CB), nb, lambda g, t: (t,))
        + [pl.BlockSpec((SUBLANES, CB), lambda g, t: _halo_row(ts, lambda u: u)(g, t) + (g + nb,))] + _rg_param_specs(),
        out_specs=[blk, blk],
        out_shape=[jax.ShapeDtypeStruct((s, nb * CB), MXU_DTYPE), jax.ShapeDtypeStruct((s, nb * CB), F32)],
        scratch_shapes=[pltpu.VMEM((ts + SUBLANES, CB), F32), pltpu.VMEM((ts, CB), F32), pltpu.VMEM((ts, CB), F32),
                        pltpu.VMEM((SUBLANES, CB), F32)],
        compiler_params=_params("parallel", "arbitrary"), name=name,
    )(pj, pj, pj, cw, cb, wa, ba, wx, bx, lam)


def _rg_mid_bwd(dy, pj, hs, cw, cb, wa, ba, wx, bx, lam, name):
    s = pj.shape[0]
    nb = pj.shape[1] // (2 * CB)
    ts = _tile(s, (512,))
    nt = s // ts

    def body(dy_ref, gate_ref, x_ref, halo_ref, hs_ref, hsh_ref, cw_ref, cb_ref, wa_ref, ba_ref, wx_ref, bx_ref, lam_ref,
             dpj_ref, dcw_ref, dcb_ref, dwa_ref, dba_ref, dwx_ref, dbx_ref, dlam_ref,
             scr, hscr, a_scr, d_scr, g_scr, dxscr, c_scr):
        tt = pl.program_id(1)
        t = nt - 1 - tt

        @pl.when(tt == 0)
        def _():
            c_scr[...] = jnp.zeros_like(c_scr)
            dxscr[ts:, :] = jnp.zeros((SUBLANES, CB), F32)
            for ref in (dcw_ref, dcb_ref, dwa_ref, dba_ref, dwx_ref, dbx_ref, dlam_ref):
                ref[...] = jnp.zeros_like(ref)

        scr[0:SUBLANES, :] = jnp.where(t > 0, halo_ref[...], 0.0)
        scr[SUBLANES:, :] = x_ref[...]
        hscr[0:SUBLANES, :] = jnp.where(t > 0, hsh_ref[...], 0.0)
        hscr[SUBLANES:, :] = hs_ref[...]
        xc, views = _rg_conv(scr, cw_ref, cb_ref, ts)
        r, ig, sp, a, mult = _rg_gates(xc, wa_ref, ba_ref, wx_ref, bx_ref, lam_ref)
        gate = gate_ref[...]
        gel, th = _gelu(gate)
        dyv = dy_ref[...]
        dpj_ref[0] = (dyv * hs_ref[...] * _gelu_grad(gate, th)).astype(dpj_ref.dtype)
        a_scr[...] = a
        d_scr[...] = dyv * gel
        c_scr[0:1, :] = _scan_rows(a_scr, d_scr, g_scr, c_scr[0:1, :], ts, True)
        du = g_scr[...]
        da = du * hscr[7:7 + ts, :]
        dmult = du * (ig * xc)
        dig = du * (mult * xc)
        dxc = du * (mult * ig)
        dlog_a = da * a - dmult * (a * a / mult)
        dlam_ref[...] += jnp.sum(dlog_a * r, axis=0, keepdims=True) * (RG_C * _sigmoid(-lam_ref[...]))
        dpr = dlog_a * ((-RG_C) * sp) * (r * (1.0 - r))
        dpi = dig * (ig * (1.0 - ig))
        dba_ref[...] += jnp.sum(dpr, axis=0, keepdims=True)
        dbx_ref[...] += jnp.sum(dpi, axis=0, keepdims=True)
        dwa_ref[0] += _dot_tn(xc, dpr)
        dwx_ref[0] += _dot_tn(xc, dpi)
        dxc = dxc + _dot_nt(dpr, wa_ref[0]) + _dot_nt(dpi, wx_ref[0])
        dcb_ref[...] += jnp.sum(dxc, axis=0, keepdims=True)
        for k in range(RG_CONV):
            dcw_ref[k:k + 1, :] += jnp.sum(dxc * views[k], axis=0, keepdims=True)
        dxscr[0:ts, :] = dxc
        dxp = cw_ref[3:4, :] * dxc
        for k in range(RG_CONV - 1):
            dxp = dxp + cw_ref[k:k + 1, :] * dxscr[3 - k:3 - k + ts, :]
        dpj_ref[1] = dxp.astype(dpj_ref.dtype)
        dxscr[ts:, :] = dxscr[0:SUBLANES, :]

    rev = lambda g, t: (nt - 1 - t, g)
    rev_halo = lambda g, t: (jnp.maximum((nt - 1 - t) * (ts // SUBLANES) - 1, 0), g)
    vec = pl.BlockSpec((1, CB), lambda g, t: (0, g))
    mat = pl.BlockSpec((1, CB, CB), lambda g, t: (g, 0, 0))
    d = nb * CB
    vshape = jax.ShapeDtypeStruct((1, d), F32)
    mshape = jax.ShapeDtypeStruct((nb, CB, CB), F32)
    return pl.pallas_call(
        body, grid=(nb, nt),
        in_specs=[pl.BlockSpec((ts, CB), rev)] + _pair_specs((ts, CB), nb, lambda g, t: (nt - 1 - t,))
        + [pl.BlockSpec((SUBLANES, CB), lambda g, t: (rev_halo(g, t)[0], g + nb)),
           pl.BlockSpec((ts, CB), rev), pl.BlockSpec((SUBLANES, CB), rev_halo)] + _rg_param_specs(),
        out_specs=[pl.BlockSpec((2, ts, CB), lambda g, t: (0, nt - 1 - t, g)), pl.BlockSpec((RG_CONV, CB), lambda g, t: (0, g)),
                   vec, mat, vec, mat, vec, vec],
        out_shape=[jax.ShapeDtypeStruct((2, s, d), MXU_DTYPE), jax.ShapeDtypeStruct((RG_CONV, d), F32), vshape, mshape, vshape,
                   mshape, vshape, vshape],
        scratch_shapes=[pltpu.VMEM((ts + SUBLANES, CB), F32), pltpu.VMEM((ts + SUBLANES, CB), F32), pltpu.VMEM((ts, CB), F32),
                        pltpu.VMEM((ts, CB), F32), pltpu.VMEM((ts, CB), F32), pltpu.VMEM((ts + SUBLANES, CB), F32),
                        pltpu.VMEM((SUBLANES, CB), F32)],
        compiler_params=_params("parallel", "arbitrary"), name=name,
    )(dy, pj, pj, pj, hs, hs, cw, cb, wa, ba, wx, bx, lam)


GLA_DK = 128
GLA_DV = 256
GLA_O_K = GLA_HEADS * GLA_DK
GLA_O_V = 2 * GLA_HEADS * GLA_DK
GLA_O_R = GLA_O_V + GLA_HEADS * GLA_DV
GLA_O_Z = GLA_O_R + GLA_HEADS * GLA_DV
GLA_IN = GLA_O_Z + GLA_RANK
GLA_TS = 256


def _dk(h, base=0):
    return slice(base + h * GLA_DK, base + (h + 1) * GLA_DK)


def _dv(h, base=0):
    return slice(base + h * GLA_DV, base + (h + 1) * GLA_DV)


def _split3(x):
    hi = x.astype(BF16)
    r1 = x - hi.astype(F32)
    mid = r1.astype(BF16)
    lo = (r1 - mid.astype(F32)).astype(BF16)
    return hi, mid, lo


def _chunk_cumsum(x, reverse):
    n = x.shape[0]
    i = lax.broadcasted_iota(jnp.int32, (n, n), 0)
    j = lax.broadcasted_iota(jnp.int32, (n, n), 1)
    same = (i // GLA_CHUNK) == (j // GLA_CHUNK)
    tri = jnp.where(same & ((j >= i) if reverse else (j <= i)), 1.0, 0.0).astype(BF16)
    out = jnp.zeros(x.shape, F32)
    for piece in _split3(x):
        out = out + lax.dot_general(tri, piece, (((1,), (0,)), ((), ())), preferred_element_type=F32)
    return out


def _gla_head(pj_ref, h):
    return (pj_ref[:, _dk(h)] * (GLA_DK ** -0.5), pj_ref[:, _dk(h, GLA_O_K)], pj_ref[:, _dv(h, GLA_O_V)],
            pj_ref[:, _dv(h, GLA_O_R)])


def _gla_decays(gc):
    gref = gc[GLA_CHUNK // 2:GLA_CHUNK // 2 + 1, :]
    glast = gc[GLA_CHUNK - 1:GLA_CHUNK, :]
    return jnp.exp(gc), jnp.exp(gc - gref), jnp.exp(gref - gc), jnp.exp(glast - gc), jnp.exp(glast)


def _causal_mask():
    i = lax.broadcasted_iota(jnp.int32, (GLA_CHUNK, GLA_CHUNK), 0)
    j = lax.broadcasted_iota(jnp.int32, (GLA_CHUNK, GLA_CHUNK), 1)
    return j <= i


def _log_sigmoid(x):
    return jnp.minimum(x, 0.0) - _log1p_pos(jnp.exp(-jnp.abs(x)))


def _gla_mid_fwd(pj, wal, bal, ng, name):
    s, nh = pj.shape[0], GLA_HEADS
    ts = _tile(s, (GLA_TS,))
    nt, nc = s // ts, ts // GLA_CHUNK

    def body(pj_ref, wal_ref, bal_ref, ng_ref, act_ref, o_ref, st_ref, s_scr):
        t = pl.program_id(0)

        @pl.when(t == 0)
        def _():
            s_scr[...] = jnp.zeros_like(s_scr)

        heads = []
        z = pj_ref[:, GLA_O_Z:]
        for h in range(nh):
            q, k, v, r = _gla_head(pj_ref, h)
            g = _log_sigmoid(_dot_nn(z, wal_ref[:, _dk(h)]) + bal_ref[:, _dk(h)]) * (1.0 / GLA_TAU)
            heads.append((q, k, v, r, _chunk_cumsum(g, False)))
        mask = _causal_mask()
        for c in range(nc):
            sl = slice(c * GLA_CHUNK, (c + 1) * GLA_CHUNK)
            for h, (q, k, v, r, gcum) in enumerate(heads):
                eg, eq, ek, ekd, egl = _gla_decays(gcum[sl])
                st = s_scr[h]
                st_ref[c, h] = st
                attn = jnp.where(mask, _dot_nt(q[sl] * eq, k[sl] * ek), 0.0)
                o_ref[sl, h * GLA_DV:(h + 1) * GLA_DV] = _dot_nt(q[sl] * eg, st) + _dot_nn(attn, v[sl])
                s_scr[h] = st * egl + _dot_tn(v[sl], k[sl] * ekd)
        for h, (q, k, v, r, gcum) in enumerate(heads):
            cols = slice(h * GLA_DV, (h + 1) * GLA_DV)
            o = o_ref[:, cols]
            on = o * lax.rsqrt(jnp.mean(o * o, axis=-1, keepdims=True) + EPS)
            act_ref[:, cols] = ((on * ng_ref[...]) * (r * _sigmoid(r))).astype(act_ref.dtype)

    blk = pl.BlockSpec((ts, nh * GLA_DV), lambda t: (t, 0))
    whole = lambda shape: pl.BlockSpec(shape, lambda t: (0,) * len(shape))
    return pl.pallas_call(
        body, grid=(nt,),
        in_specs=[pl.BlockSpec((ts, GLA_IN), lambda t: (t, 0)), whole((GLA_RANK, nh * GLA_DK)), whole((1, nh * GLA_DK)),
                  whole((1, GLA_DV))],
        out_specs=[blk, blk, pl.BlockSpec((nc, nh, GLA_DV, GLA_DK), lambda t: (t, 0, 0, 0))],
        out_shape=[jax.ShapeDtypeStruct((s, nh * GLA_DV), MXU_DTYPE), jax.ShapeDtypeStruct((s, nh * GLA_DV), F32),
                   jax.ShapeDtypeStruct((s // GLA_CHUNK, nh, GLA_DV, GLA_DK), F32)],
        scratch_shapes=[pltpu.VMEM((nh, GLA_DV, GLA_DK), F32)],
        compiler_params=_params("arbitrary"), name=name,
    )(pj, wal, bal, ng)


def _gla_mid_bwd(dact, pj, o, st, wal, bal, ng, name):
    s, nh = pj.shape[0], GLA_HEADS
    ts = _tile(s, (GLA_TS,))
    nt, nc = s // ts, ts // GLA_CHUNK

    def body(dact_ref, pj_ref, o_ref, st_ref, wal_ref, bal_ref, ng_ref, dpj_ref, dwal_ref, dbal_ref, dng_ref,
             ds_scr, dg_scr):
        tt = pl.program_id(0)

        @pl.when(tt == 0)
        def _():
            ds_scr[...] = jnp.zeros_like(ds_scr)
            dwal_ref[...] = jnp.zeros_like(dwal_ref)
            dbal_ref[...] = jnp.zeros_like(dbal_ref)
            dng_ref[...] = jnp.zeros_like(dng_ref)

        heads = []
        z = pj_ref[:, GLA_O_Z:]
        for h in range(nh):
            q, k, v, r = _gla_head(pj_ref, h)
            logit = _dot_nn(z, wal_ref[:, _dk(h)]) + bal_ref[:, _dk(h)]
            gcum = _chunk_cumsum(_log_sigmoid(logit) * (1.0 / GLA_TAU), False)
            ov = o_ref[:, h * GLA_DV:(h + 1) * GLA_DV]
            ro = lax.rsqrt(jnp.mean(ov * ov, axis=-1, keepdims=True) + EPS)
            on = ov * ro
            sg = _sigmoid(r)
            sil = r * sg
            dav = dact_ref[:, h * GLA_DV:(h + 1) * GLA_DV]
            dpj_ref[:, _dv(h, GLA_O_R)] = (dav * (on * ng_ref[...]) * (sg + sil * (1.0 - sg))).astype(dpj_ref.dtype)
            t1 = dav * sil
            dng_ref[...] += jnp.sum(t1 * on, axis=0, keepdims=True)
            dn = t1 * ng_ref[...]
            do = ro * (dn - on * jnp.mean(dn * on, axis=-1, keepdims=True))
            heads.append((q, k, v, logit, gcum, do))
        mask = _causal_mask()
        scale = GLA_DK ** -0.5
        last_row = lax.broadcasted_iota(jnp.int32, (GLA_CHUNK, GLA_DK), 0) == GLA_CHUNK - 1
        for c in reversed(range(nc)):
            sl = slice(c * GLA_CHUNK, (c + 1) * GLA_CHUNK)
            for h, (q, k, v, logit, gcum, do) in enumerate(heads):
                eg, eq, ek, ekd, egl = _gla_decays(gcum[sl])
                qc, kc, vc, doc = q[sl], k[sl], v[sl], do[sl]
                qg, qt, kt, kd = qc * eg, qc * eq, kc * ek, kc * ekd
                sp = st_ref[c, h]
                ds = ds_scr[h]
                attn = jnp.where(mask, _dot_nt(qt, kt), 0.0)
                dattn = jnp.where(mask, _dot_nt(doc, vc), 0.0)
                dqg = _dot_nn(doc, sp)
                dqt = _dot_nn(dattn, kt)
                dkt = _dot_tn(dattn, qt)
                dkd = _dot_nn(vc, ds)
                dpj_ref[sl, _dv(h, GLA_O_V)] = (_dot_tn(attn, doc) + _dot_nt(kd, ds)).astype(dpj_ref.dtype)
                dpj_ref[sl, _dk(h)] = (scale * (dqg * eg + dqt * eq)).astype(dpj_ref.dtype)
                dpj_ref[sl, _dk(h, GLA_O_K)] = (dkt * ek + dkd * ekd).astype(dpj_ref.dtype)
                kdd = dkd * kd
                dgl = jnp.sum(kdd, axis=0, keepdims=True) + jnp.sum(ds * sp, axis=0, keepdims=True) * egl
                dg_scr[h, sl, :] = dqg * qg + dqt * qt - dkt * kt - kdd + jnp.where(last_row, dgl, 0.0)
                ds_scr[h] = ds * egl + _dot_tn(doc, qg)
        dz = jnp.zeros((ts, GLA_RANK), F32)
        for h, (q, k, v, logit, gcum, do) in enumerate(heads):
            dlogit = _chunk_cumsum(dg_scr[h], True) * (1.0 / GLA_TAU) * _sigmoid(-logit)
            dz = dz + _dot_nt(dlogit, wal_ref[:, _dk(h)])
            dwal_ref[:, _dk(h)] += _dot_tn(z, dlogit)
            dbal_ref[:, _dk(h)] += jnp.sum(dlogit, axis=0, keepdims=True)
        dpj_ref[:, GLA_O_Z:] = dz.astype(dpj_ref.dtype)

    rev = lambda t: (nt - 1 - t, 0)
    whole = lambda shape: pl.BlockSpec(shape, lambda t: (0,) * len(shape))
    wide = pl.BlockSpec((ts, nh * GLA_DV), rev)
    return pl.pallas_call(
        body, grid=(nt,),
        in_specs=[wide, pl.BlockSpec((ts, GLA_IN), rev), wide,
                  pl.BlockSpec((nc, nh, GLA_DV, GLA_DK), lambda t: (nt - 1 - t, 0, 0, 0)),
                  whole((GLA_RANK, nh * GLA_DK)), whole((1, nh * GLA_DK)), whole((1, GLA_DV))],
        out_specs=[pl.BlockSpec((ts, GLA_IN), rev), whole((GLA_RANK, nh * GLA_DK)), whole((1, nh * GLA_DK)), whole((1, GLA_DV))],
        out_shape=[jax.ShapeDtypeStruct((s, GLA_IN), MXU_DTYPE), jax.ShapeDtypeStruct((GLA_RANK, nh * GLA_DK), F32),
                   jax.ShapeDtypeStruct((1, nh * GLA_DK), F32), jax.ShapeDtypeStruct((1, GLA_DV), F32)],
        scratch_shapes=[pltpu.VMEM((nh, GLA_DV, GLA_DK), F32), pltpu.VMEM((nh, ts, GLA_DK), F32)],
        compiler_params=_params("arbitrary"), name=name,
    )(dact, pj, o, st, wal, bal, ng)


def _adamw(w, gs, m, v, name, after=None):
    layers, rows, cols = w.shape
    gs = list(gs) if isinstance(gs, (list, tuple)) else gs
    n_g = len(gs) if isinstance(gs, list) else 1
    if rows % SUBLANES == 0:
        tr, tc = _tile(rows, (256, 128, 64, 32, 16, 8)), cols
    else:
        tr, tc = rows, _tile(cols, (256, 128))
    c1 = 1.0 / (1.0 - ADAM_B1 ** ADAM_STEP)
    c2 = 1.0 / (1.0 - ADAM_B2 ** ADAM_STEP)

    def body(*refs):
        g_refs, (w_ref, m_ref, v_ref) = refs[:n_g], refs[n_g:n_g + 3]
        go_ref, d_ref, mo_ref, vo_ref = refs[-4:]
        gv = g_refs[0][...]
        for l in range(1, n_g):
            gv = jnp.where(pl.program_id(0) == l, g_refs[l][...], gv)
        m2 = ADAM_B1 * m_ref[...] + (1.0 - ADAM_B1) * gv
        v2 = ADAM_B2 * v_ref[...] + (1.0 - ADAM_B2) * (gv * gv)
        d_ref[...] = (-ADAM_LR) * ((m2 * c1) / (jnp.sqrt(v2 * c2) + ADAM_EPS) + ADAM_WD * w_ref[...])
        go_ref[...] = gv
        mo_ref[...] = m2
        vo_ref[...] = v2

    spec = pl.BlockSpec((None, tr, tc), lambda l, i, j: (l, i, j))
    g_specs = [pl.BlockSpec((tr, tc), lambda l, i, j: (i, j))] * n_g if isinstance(gs, list) else [spec]
    extra = [] if after is None else [(after, _ANY)]
    shape = jax.ShapeDtypeStruct((layers, rows, cols), F32)
    return pl.pallas_call(
        body, grid=(layers, rows // tr, cols // tc), in_specs=g_specs + [spec] * 3 + [sp for _, sp in extra],
        out_specs=[spec] * 4, out_shape=[shape] * 4, compiler_params=_params("parallel", "parallel", "parallel"), name=name,
    )(*(gs if isinstance(gs, list) else [gs]), w, m, v, *[a for a, _ in extra])


def _col_slots(w, name):
    r, cc = w.shape
    c = cc // N_CHIP
    tr = _tile(r, (256,))

    def body(w_ref, o_ref):
        for j in range(N_CHIP):
            o_ref[j] = w_ref[:, j * c:(j + 1) * c]

    return pl.pallas_call(
        body, grid=(r // tr,), in_specs=[pl.BlockSpec((tr, cc), lambda i: (i, 0))],
        out_specs=pl.BlockSpec((N_CHIP, tr, c), lambda i: (0, i, 0)), out_shape=jax.ShapeDtypeStruct((N_CHIP, r, c), w.dtype),
        compiler_params=_params("parallel"), name=name,
    )(w)


def _from_col_slots(w, name):
    n, r, c = w.shape
    tr = _tile(r, (256,))

    def body(w_ref, o_ref):
        for j in range(n):
            o_ref[:, j * c:(j + 1) * c] = w_ref[j]

    return pl.pallas_call(
        body, grid=(r // tr,), in_specs=[pl.BlockSpec((n, tr, c), lambda i: (0, i, 0))],
        out_specs=pl.BlockSpec((tr, n * c), lambda i: (i, 0)), out_shape=jax.ShapeDtypeStruct((r, n * c), w.dtype),
        compiler_params=_params("parallel"), name=name,
    )(w)


def _block_rows_to_slots(w):
    g, r4, cc = w.shape
    return jnp.swapaxes(w.reshape(g, N_CHIP, r4 // N_CHIP, cc), 0, 1).reshape(N_CHIP, g * (r4 // N_CHIP), cc)


def _slots_to_block_rows(w, g):
    n, gr, cc = w.shape
    return jnp.swapaxes(w.reshape(n, g, gr // g, cc), 0, 1).reshape(g, n * (gr // g), cc)


def _local_step(x, tgt, mod, w, fetch=None, done=None, later=None):
    depth = mod.shape[0]
    row = lambda v: v.reshape(1, -1)
    w = dict(w)
    w["ffn_w_up"], w["ffn_w_down"] = dict(enumerate(w["ffn_w_up"])), dict(enumerate(w["ffn_w_down"]))

    def arrive(stage, after):
        if fetch is not None:
            for k, v in fetch(stage, after).items():
                if isinstance(v, dict):
                    w[k].update(v)
                else:
                    w[k] = v

    saved = []
    for i in range(depth):
        if i == 1:
            arrive("gla", x)
        sh_m, sc_m, gt_m, sh_f, sc_f, gt_f = (mod[i, j:j + 1] for j in range(6))
        g0, g1, g2, g3 = (w["norm_g"][i, j:j + 1] for j in range(4))
        tag = f"_l{i}"
        if i == 0:
            h = _norm_mod_fwd(x, g0, sc_m, sh_m, "norm_mix" + tag)
        if i % 2 == 0:
            pj = _mm(h, w["rg_w_in"], w_slots=N_CHIP, name="rg_in" + tag)
            act, aux = _rg_mid_fwd(pj, w["rg_conv_w"], row(w["rg_conv_b"]), w["rg_wa"], row(w["rg_ba"]), w["rg_wx"],
                                   row(w["rg_bx"]), row(w["rg_lambda"]), "rg_mid" + tag)
            y = _mm(act, w["rg_w_out"], name="rg_out" + tag)
        else:
            pj = _mm(h, w["gla_w_in"], tm_max=512, name="gla_in" + tag)
            act, *aux = _gla_mid_fwd(pj, w["gla_w_alpha"], row(w["gla_b_alpha"]), row(w["gla_norm_g"]), "gla_mid" + tag)
            y = _mm(act, w["gla_w_out"], name="gla_out" + tag)
        x1, h2 = _post_norm_fwd(x, y, g1, gt_m, g2, sc_f, sh_f, "post_mix" + tag)
        arrive(f"ffn{i}", x1)
        p = _mm(h2, w["ffn_w_up"][i], w_slots=N_CHIP, name="ffn_up" + tag)
        a, ga, gb = _ffn_mid_fwd(p, w["ffn_conv_w"][i], w["ffn_conv_b"][i:i + 1], "ffn_mid" + tag)
        y2 = _mm(a, w["ffn_w_down"][i], name="ffn_down" + tag)
        saved_h = h
        if i + 1 < depth:
            nxt = [mod[i + 1, j:j + 1] for j in range(2)] + [w["norm_g"][i + 1, 0:1]]
            x2, h = _post_norm_fwd(x1, y2, g3, gt_f, nxt[2], nxt[1], nxt[0], "post_ffn" + tag)
        else:
            x2 = None
            cols, dx = _post_loss(x1, y2, g3, gt_f, tgt, "post_ffn_loss")
        saved.append((x, saved_h, pj, act, aux, y, x1, h2, p, (a, ga, gb), y2))
        x = x2

    stacked = ("norm_g", "ffn_conv_w", "ffn_conv_b", "mod")
    gr = {k: [None] * depth for k in stacked + ("ffn_w_up", "ffn_w_down")}
    told = lambda stage: done(stage, gr) if done is not None else 0.0
    told_later = lambda stage, after: later(stage, after) if later is not None else 0.0
    for i in reversed(range(depth)):
        x0, h, pj, act, aux, y, x1, h2, p, (a, ga, gb), y2 = saved[i]
        sh_m, sc_m, gt_m, sh_f, sc_f, gt_f = (mod[i, j:j + 1] for j in range(6))
        g0, g1, g2, g3 = (w["norm_g"][i, j:j + 1] for j in range(4))
        tag = f"_l{i}"
        if i == depth - 1:
            dy2, d_g3, d_gt_f = _post_bwd(dx, y2, g3, gt_f, "post_ffn_b" + tag)
        else:
            dy2, d_g3, d_gt_f = ahead
        da = _mm(dy2, w["ffn_w_down"][i], tb=True, name="ffn_down_dx" + tag)
        gr["ffn_w_down"][i] = _mm(a, dy2, ta=True, name="ffn_down_dw" + tag)
        conv_w = w["ffn_conv_w"][i] + (told_later("l1", da) if i == 0 else 0.0)
        dp, dcw, dcb = _ffn_mid_bwd(da, p, ga, gb, conv_w, "ffn_mid_b" + tag)
        gr["ffn_conv_w"][i], gr["ffn_conv_b"][i] = _cat(dcw[0], dcw[1]), _cat(dcb[0], dcb[1])[0]
        dh2 = _mm(dp, w["ffn_w_up"][i], tb=True, a_parts=2, w_slots=N_CHIP, name="ffn_up_dx" + tag)
        gr["ffn_w_up"][i] = _mm(h2, dp, ta=True, b_parts=2, out_slots=N_CHIP, name="ffn_up_dw" + tag)
        if i == 0:
            gt_m = gt_m + told("ffn0")
        dx1, dy, d_g2, d_sc_f, d_sh_f, d_g1, d_gt_m = _norm_post_bwd(dh2, x1, g2, sc_f, dx, y, g1, gt_m, "norm_ffn_b" + tag)
        if i % 2 == 0:
            dact = _mm(dy, w["rg_w_out"], tb=True, name="rg_out_dx" + tag)
            gr["rg_w_out"] = _mm(act, dy, ta=True, name="rg_out_dw" + tag)
            lam = row(w["rg_lambda"]) + told_later("ffn0", gr["rg_w_out"])
            dpj, gr["rg_conv_w"], d_cb, gr["rg_wa"], d_ba, gr["rg_wx"], d_bx, d_lam = _rg_mid_bwd(
                dact, pj, aux, w["rg_conv_w"], row(w["rg_conv_b"]), w["rg_wa"], row(w["rg_ba"]), w["rg_wx"],
                row(w["rg_bx"]), lam, "rg_mid_b" + tag)
            gr["rg_conv_b"], gr["rg_ba"], gr["rg_bx"], gr["rg_lambda"] = d_cb[0], d_ba[0], d_bx[0], d_lam[0]
            dh = _mm(dpj, w["rg_w_in"], tb=True, a_parts=2, w_slots=N_CHIP, name="rg_in_dx" + tag)
            gr["rg_w_in"] = _mm(h, dpj, ta=True, b_parts=2, out_slots=N_CHIP, name="rg_in_dw" + tag)
            sc_m = sc_m + told("rg")
        else:
            dact = _mm(dy, w["gla_w_out"], tb=True, name="gla_out_dx" + tag)
            gr["gla_w_out"] = _mm(act, dy, ta=True, name="gla_out_dw" + tag)
            dpj, gr["gla_w_alpha"], d_bal, d_ng = _gla_mid_bwd(dact, pj, aux[0], aux[1], w["gla_w_alpha"], row(w["gla_b_alpha"]),
                                                               row(w["gla_norm_g"]), "gla_mid_b" + tag)
            gr["gla_b_alpha"], gr["gla_norm_g"] = d_bal[0], d_ng[0]
            dh = _mm(dpj, w["gla_w_in"], tb=True, name="gla_in_dx" + tag)
            gr["gla_w_in"] = _mm(h, dpj, ta=True, tm_max=512, name="gla_in_dw" + tag)
            mod = mod.at[0].add(told("l1"))
        if i > 0:
            dx, dy_below, d_g0, d_sc_m, d_sh_m, d_g_below, d_gt_below = _norm_post_bwd(
                dh, x0, g0, sc_m, dx1, saved[i - 1][-1], w["norm_g"][i - 1, 3:4], mod[i - 1, 5:6], "norm_mix_b" + tag)
            ahead = (dy_below, d_g_below, d_gt_below)
        else:
            dx, d_g0, d_sc_m, d_sh_m = _norm_mod_bwd(dh, x0, g0, sc_m, dx1, "norm_mix_b" + tag)
        gr["norm_g"][i] = jnp.concatenate([d_g0, d_g1, d_g2, d_g3], axis=0)
        gr["mod"][i] = jnp.concatenate([d_sh_m, d_sc_m, d_gt_m, d_sh_f, d_sc_f, d_gt_f], axis=0)
    for k in stacked:
        gr[k] = jnp.stack(gr[k])
    return cols, dx, gr


ADA_ROWS = 16


def _ada_fwd(c16, ada_w, ada_b, name):
    depth, d, n = ada_w.shape
    tn = _tile(n, (512, 256, 128))

    def body(c_ref, w_ref, b_ref, o_ref):
        cv = c_ref[...]
        o_ref[0] = _dot_nn(cv * _sigmoid(cv), w_ref[0]) + b_ref[0]

    return pl.pallas_call(
        body, grid=(depth, n // tn),
        in_specs=[pl.BlockSpec((ADA_ROWS, d), lambda l, j: (0, 0)), pl.BlockSpec((1, d, tn), lambda l, j: (l, 0, j)),
                  pl.BlockSpec((1, 1, tn), lambda l, j: (l, 0, j))],
        out_specs=pl.BlockSpec((1, ADA_ROWS, tn), lambda l, j: (l, 0, j)),
        out_shape=jax.ShapeDtypeStruct((depth, ADA_ROWS, n), F32),
        compiler_params=_params("parallel", "parallel"), name=name,
    )(c16, ada_w, ada_b)


def _ada_bwd(c16, dmod16, name):
    depth, _, n = dmod16.shape
    d = c16.shape[1]
    tn = _tile(n, (512, 256, 128))

    def body(c_ref, dm_ref, o_ref):
        cv = c_ref[...]
        o_ref[0] = _dot_tn(cv * _sigmoid(cv), dm_ref[0])

    return pl.pallas_call(
        body, grid=(depth, n // tn),
        in_specs=[pl.BlockSpec((ADA_ROWS, d), lambda l, j: (0, 0)), pl.BlockSpec((1, ADA_ROWS, tn), lambda l, j: (l, 0, j))],
        out_specs=pl.BlockSpec((1, d, tn), lambda l, j: (l, 0, j)),
        out_shape=jax.ShapeDtypeStruct((depth, d, n), F32),
        compiler_params=_params("parallel", "parallel"), name=name,
    )(c16, dmod16)


PACK_COLS = 1024
_ANY = pl.BlockSpec(memory_space=pl.ANY)
_VMEM = pl.BlockSpec(memory_space=pltpu.VMEM)


def _place():
    return lax.axis_index("x"), lax.axis_index("y"), lax.axis_index("c")


def _other_chips(x, y):
    return [(1 - x, y), (x, 1 - y), (1 - x, 1 - y)]


def _rcopy(src, dst, send_sems, recv_sems, k, peer):
    return pltpu.make_async_remote_copy(src_ref=src, dst_ref=dst, send_sem=send_sems.at[k], recv_sem=recv_sems.at[k],
                                        device_id=peer, device_id_type=MESH)


def _all_gather_8(v, name):
    r, cc = v.shape

    def body(v_ref, out_ref, send_sems, recv_sems, local_sem):
        x, y, c = _place()
        me = 4 * x + 2 * y + c
        mine = pltpu.make_async_copy(v_ref, out_ref.at[me], local_sem)
        mine.start()
        peers = []
        for k in range(1, N_DEV):
            px = 1 - x if k & 4 else x
            py = 1 - y if k & 2 else y
            pc = 1 - c if k & 1 else c
            peers.append((px, py, pc))
        sends = [_rcopy(v_ref, out_ref.at[me], send_sems, recv_sems, k, p) for k, p in enumerate(peers)]
        for cp in sends:
            cp.start()
        for k, (px, py, pc) in enumerate(peers):
            _rcopy(v_ref, out_ref.at[4 * px + 2 * py + pc], send_sems, recv_sems, k, (px, py, pc)).wait_recv()
        for cp in sends:
            cp.wait_send()
        mine.wait()

    return pl.pallas_call(
        body, in_specs=[_VMEM], out_specs=_VMEM, out_shape=jax.ShapeDtypeStruct((N_DEV, r, cc), v.dtype),
        scratch_shapes=[pltpu.SemaphoreType.DMA((N_DEV - 1,)), pltpu.SemaphoreType.DMA((N_DEV - 1,)), pltpu.SemaphoreType.DMA],
        compiler_params=pltpu.CompilerParams(vmem_limit_bytes=VMEM_LIMIT), name=name,
    )(v)


def _gather_chips(shards, name):
    n = len(shards)
    per = 2 * (N_CHIP - 1)

    def body(*refs):
        ins, outs, (send_sems, recv_sems) = refs[:n], refs[n:2 * n], refs[2 * n:]
        x, y, c = _place()
        chip = 2 * x + y
        chips = _other_chips(x, y)
        rows = [(pl.ds(c * (r.shape[0] // 2), r.shape[0] // 2), pl.ds((1 - c) * (r.shape[0] // 2), r.shape[0] // 2)) for r in ins]
        first = [_rcopy(ins[i].at[rows[i][0]], outs[i].at[chip, rows[i][0]], send_sems, recv_sems, per * i + j, (px, py, c))
                 for i in range(n) for j, (px, py) in enumerate(chips)]
        for cp in first:
            cp.start()
        passed = []
        for i in range(n):
            for j, (px, py) in enumerate(chips):
                landed = outs[i].at[2 * px + py, rows[i][0]]
                _rcopy(ins[i].at[rows[i][0]], landed, send_sems, recv_sems, per * i + j, (px, py, c)).wait_recv()
                fw = _rcopy(landed, landed, send_sems, recv_sems, per * i + N_CHIP - 1 + j, (x, y, 1 - c))
                fw.start()
                passed.append(fw)
        for i in range(n):
            for j, (px, py) in enumerate(chips):
                landed = outs[i].at[2 * px + py, rows[i][1]]
                _rcopy(landed, landed, send_sems, recv_sems, per * i + N_CHIP - 1 + j, (x, y, 1 - c)).wait_recv()
        for cp in first + passed:
            cp.wait_send()

    return pl.pallas_call(
        body, in_specs=[_ANY] * n, out_specs=[_ANY] * n,
        out_shape=[jax.ShapeDtypeStruct((N_CHIP,) + sh.shape, sh.dtype) for sh in shards],
        scratch_shapes=[pltpu.SemaphoreType.DMA((per * n,)), pltpu.SemaphoreType.DMA((per * n,))], name=name,
    )(*shards)


_ROW_TILES = (640, 512, 352, 256, 128, 64, 32, 16)


def _pair_sum(g, other, c_idx, name):
    n, half, cc = other.shape
    tr = _tile(half, _ROW_TILES)

    def body(c_ref, g_ref, o_ref, out_ref):
        out_ref[...] = (g_ref[...] + o_ref[...]).astype(out_ref.dtype)

    return pl.pallas_call(
        body,
        grid_spec=pltpu.PrefetchScalarGridSpec(
            num_scalar_prefetch=1, grid=(n, half // tr),
            in_specs=[pl.BlockSpec((None, None, tr, cc), lambda k, i, c_ref: (k, c_ref[0], i, 0)),
                      pl.BlockSpec((None, tr, cc), lambda k, i, c_ref: (k, i, 0))],
            out_specs=pl.BlockSpec((None, tr, cc), lambda k, i, c_ref: (k, i, 0))),
        out_shape=jax.ShapeDtypeStruct((n, half, cc), BF16),
        compiler_params=_params("parallel", "parallel"), name=name,
    )(c_idx, g.reshape(n, 2, half, cc), other)


_HBM = pl.BlockSpec(memory_space=pltpu.HBM)
_SEM = pl.BlockSpec(memory_space=pltpu.SEMAPHORE)
_DATAFLOW = pltpu.SideEffectType.DATAFLOW_SIDE_EFFECTING


def _split_copies(srcs, lands, send_sems, recv_sems, mode, arriving):
    x, y, c = _place()
    chip = 2 * x + y
    out = []
    for i, (src, land) in enumerate(zip(srcs, lands)):
        if mode == "all":
            for k in range(1, N_DEV):
                px, py, pc = (1 - x if k & 4 else x), (1 - y if k & 2 else y), (1 - c if k & 1 else c)
                slot = 4 * px + 2 * py + pc if arriving else 2 * chip + c
                out.append(_rcopy(src, land.at[slot], send_sems, recv_sems, (N_DEV - 1) * i + k - 1, (px, py, pc)))
            continue
        if mode == "pair":
            half = src.shape[1] // 2
            out.append(_rcopy(src.at[:, pl.ds((1 - c) * half, half)], land, send_sems, recv_sems, i, (x, y, 1 - c)))
            continue
        for j, (px, py) in enumerate(_other_chips(x, y)):
            there = 2 * px + py
            part = src.at[there] if mode == "slots" else src
            out.append(_rcopy(part, land.at[there if arriving else chip], send_sems, recv_sems, (N_CHIP - 1) * i + j, (px, py, c)))
    return out


def _land_shape(src, mode):
    if mode == "pair":
        return (src.shape[0], src.shape[1] // 2, src.shape[2])
    if mode == "all":
        return (N_DEV,) + src.shape
    return (N_CHIP,) + (src.shape[1:] if mode == "slots" else src.shape)


def _send_start(srcs, mode, name):
    n = len(srcs)
    n_sem = {"pair": 1, "all": N_DEV - 1}.get(mode, N_CHIP - 1) * n
    lands = [lax.empty(_land_shape(s, mode), s.dtype) for s in srcs]

    def body(*refs):
        ins, zones, (send_sems, recv_sems) = refs[:n], refs[n:2 * n], refs[2 * n:2 * n + 2]
        for cp in _split_copies(ins, zones, send_sems, recv_sems, mode, False):
            cp.start()
        refs[-1][...] = jnp.zeros_like(refs[-1])

    hbm = lambda a: pltpu.HBM(a.shape, a.dtype)
    outs = pl.pallas_call(
        body, name=name, in_specs=[_HBM] * (2 * n),
        out_shape=(pltpu.SemaphoreType.DMA((n_sem,)), pltpu.SemaphoreType.DMA((n_sem,)), *[hbm(a) for a in srcs],
                   *[hbm(a) for a in lands], jax.ShapeDtypeStruct((SUBLANES, LANES), F32)),
        out_specs=(_SEM, _SEM, *[_HBM] * (2 * n), _VMEM), input_output_aliases={i: 2 + i for i in range(2 * n)},
        compiler_params=pltpu.CompilerParams(has_side_effects=_DATAFLOW),
    )(*[pltpu.with_memory_space_constraint(a, pltpu.HBM) for a in list(srcs) + lands])
    return (outs[0], outs[1], list(outs[2:2 + n]), list(outs[2 + n:2 + 2 * n])), outs[-1]


def _send_wait(state, after, mode, name):
    send_sems, recv_sems, srcs, lands = state
    n = len(srcs)

    def body(*refs):
        ins, zones, (send_s, recv_s) = refs[:n], refs[n:2 * n], refs[2 * n:2 * n + 2]
        for cp in _split_copies(ins, zones, send_s, recv_s, mode, True):
            cp.wait_send()
            cp.wait_recv()

    hbm = lambda a: pltpu.HBM(a.shape, a.dtype)
    outs = pl.pallas_call(
        body, name=name, in_specs=[_HBM] * (2 * n) + [_SEM, _SEM, _ANY],
        out_shape=tuple(hbm(a) for a in srcs + lands), out_specs=tuple([_HBM] * (2 * n)),
        input_output_aliases={i: i for i in range(2 * n)},
        compiler_params=pltpu.CompilerParams(has_side_effects=_DATAFLOW),
    )(*srcs, *lands, send_sems, recv_sems, after)
    return list(outs[:n]), list(outs[n:])


def _sum_lead(v, name):
    n, r, cc = v.shape
    tr = _tile(r, _ROW_TILES + (8,))

    def body(v_ref, o_ref):
        acc = v_ref[0].astype(F32)
        for k in range(1, n):
            acc = acc + v_ref[k].astype(F32)
        o_ref[...] = acc

    return pl.pallas_call(
        body, grid=(r // tr,), in_specs=[pl.BlockSpec((n, tr, cc), lambda i: (0, i, 0))],
        out_specs=pl.BlockSpec((tr, cc), lambda i: (i, 0)), out_shape=jax.ShapeDtypeStruct((r, cc), F32),
        compiler_params=_params("parallel"), name=name,
    )(v)


def _chip_sum(arrived, mine, chip_idx, name):
    n, r, cc = arrived.shape
    tr = _tile(r, _ROW_TILES)

    def body(chip_ref, a_ref, m_ref, o_ref):
        acc = jnp.zeros((tr, cc), F32)
        for k in range(n):
            acc = acc + jnp.where(chip_ref[0] == k, m_ref[...], a_ref[k]).astype(F32)
        o_ref[...] = acc

    return pl.pallas_call(
        body,
        grid_spec=pltpu.PrefetchScalarGridSpec(
            num_scalar_prefetch=1, grid=(r // tr,),
            in_specs=[pl.BlockSpec((n, tr, cc), lambda i, chip_ref: (0, i, 0)),
                      pl.BlockSpec((None, tr, cc), lambda i, chip_ref: (chip_ref[0], i, 0))],
            out_specs=pl.BlockSpec((tr, cc), lambda i, chip_ref: (i, 0))),
        out_shape=jax.ShapeDtypeStruct((r, cc), F32), compiler_params=_params("parallel"), name=name,
    )(chip_idx, arrived, mine)


def _pair_share(reds, name):
    n = len(reds)

    def body(*refs):
        ins, outs, (send_sems, recv_sems) = refs[:n], refs[n:2 * n], refs[2 * n:]
        x, y, c = _place()
        copies = [_rcopy(ins[i], outs[i].at[c], send_sems, recv_sems, i, (x, y, 1 - c)) for i in range(n)]
        for cp in copies:
            cp.start()
        for i in range(n):
            _rcopy(ins[i], outs[i].at[1 - c], send_sems, recv_sems, i, (x, y, 1 - c)).wait_recv()
        for cp in copies:
            cp.wait_send()

    return pl.pallas_call(
        body, in_specs=[_ANY] * n, out_specs=[_ANY] * n, out_shape=[jax.ShapeDtypeStruct((2,) + r.shape, r.dtype) for r in reds],
        scratch_shapes=[pltpu.SemaphoreType.DMA((n,)), pltpu.SemaphoreType.DMA((n,))], name=name,
    )(*reds)


def _pack(arrs, rows_multiple, dtype):
    flat = jnp.concatenate([a.reshape(-1).astype(dtype) for a in arrs])
    unit = rows_multiple * PACK_COLS
    total = -(-flat.shape[0] // unit) * unit
    return jnp.pad(flat, (0, total - flat.shape[0])).reshape(-1, PACK_COLS)


def _unpack(buf, shapes):
    lead = buf.shape[:-2]
    flat = buf.reshape(*lead, -1)
    out, off = [], 0
    for shp in shapes:
        n = 1
        for s in shp:
            n *= s
        out.append(flat[..., off:off + n].reshape(*lead, *shp))
        off += n
    return out


def _join_shards(parts, axis):
    moved = jnp.moveaxis(parts, 0, axis)
    shp = list(moved.shape)
    shp[axis:axis + 2] = [shp[axis] * shp[axis + 1]]
    return moved.reshape(shp)


def _my_shard(full, axis, chip):
    n = full.shape[axis] // N_CHIP
    return lax.dynamic_slice_in_dim(full, chip * n, n, axis)


SMALL = {"norm_g": 2, "ffn_conv_w": 2, "rg_conv_w": 2, "gla_w_alpha": 2, "gla_b_alpha": 1, "gla_norm_g": 1,
         "ada_b": None, "ffn_conv_b": None, "rg_conv_b": None, "rg_ba": None, "rg_bx": None, "rg_lambda": None}
BIG = {"rg_w_in": True, "rg_wa": False, "rg_wx": False, "rg_w_out": False, "ffn_w_up": True, "ffn_w_down": False,
       "gla_w_in": True, "gla_w_out": False}
WEIGHTS = ["ada_w", "ada_b", "norm_g", "ffn_w_up", "ffn_conv_w", "ffn_conv_b", "ffn_w_down", "rg_w_in", "rg_conv_w", "rg_conv_b",
           "rg_wa", "rg_ba", "rg_wx", "rg_bx", "rg_lambda", "rg_w_out", "gla_w_in", "gla_w_alpha", "gla_b_alpha", "gla_norm_g",
           "gla_w_out"]


def kernel(x, c, ada_w, ada_b, norm_g, ffn_w_up, ffn_conv_w, ffn_conv_b, ffn_w_down, rg_w_in, rg_conv_w, rg_conv_b, rg_wa, rg_ba, rg_wx, rg_bx, rg_lambda, rg_w_out, gla_w_in, gla_w_alpha, gla_b_alpha, gla_norm_g, gla_w_out, loss_target, m_ada_w, m_ada_b, m_norm_g, m_ffn_w_up, m_ffn_conv_w, m_ffn_conv_b, m_ffn_w_down, m_rg_w_in, m_rg_conv_w, m_rg_conv_b, m_rg_wa, m_rg_ba, m_rg_wx, m_rg_bx, m_rg_lambda, m_rg_w_out, m_gla_w_in, m_gla_w_alpha, m_gla_b_alpha, m_gla_norm_g, m_gla_w_out, v_ada_w, v_ada_b, v_norm_g, v_ffn_w_up, v_ffn_conv_w, v_ffn_conv_b, v_ffn_w_down, v_rg_w_in, v_rg_conv_w, v_rg_conv_b, v_rg_wa, v_rg_ba, v_rg_wx, v_rg_bx, v_rg_lambda, v_rg_w_out, v_gla_w_in, v_gla_w_alpha, v_gla_b_alpha, v_gla_norm_g, v_gla_w_out):
    wts = dict(ada_w=ada_w, ada_b=ada_b, norm_g=norm_g, ffn_w_up=ffn_w_up, ffn_conv_w=ffn_conv_w, ffn_conv_b=ffn_conv_b,
               ffn_w_down=ffn_w_down, rg_w_in=rg_w_in, rg_conv_w=rg_conv_w, rg_conv_b=rg_conv_b, rg_wa=rg_wa, rg_ba=rg_ba,
               rg_wx=rg_wx, rg_bx=rg_bx, rg_lambda=rg_lambda, rg_w_out=rg_w_out, gla_w_in=gla_w_in, gla_w_alpha=gla_w_alpha,
               gla_b_alpha=gla_b_alpha, gla_norm_g=gla_norm_g, gla_w_out=gla_w_out)
    mom1 = dict(ada_w=m_ada_w, ada_b=m_ada_b, norm_g=m_norm_g, ffn_w_up=m_ffn_w_up, ffn_conv_w=m_ffn_conv_w,
                ffn_conv_b=m_ffn_conv_b, ffn_w_down=m_ffn_w_down, rg_w_in=m_rg_w_in, rg_conv_w=m_rg_conv_w,
                rg_conv_b=m_rg_conv_b, rg_wa=m_rg_wa, rg_ba=m_rg_ba, rg_wx=m_rg_wx, rg_bx=m_rg_bx, rg_lambda=m_rg_lambda,
                rg_w_out=m_rg_w_out, gla_w_in=m_gla_w_in, gla_w_alpha=m_gla_w_alpha, gla_b_alpha=m_gla_b_alpha,
                gla_norm_g=m_gla_norm_g, gla_w_out=m_gla_w_out)
    mom2 = dict(ada_w=v_ada_w, ada_b=v_ada_b, norm_g=v_norm_g, ffn_w_up=v_ffn_w_up, ffn_conv_w=v_ffn_conv_w,
                ffn_conv_b=v_ffn_conv_b, ffn_w_down=v_ffn_w_down, rg_w_in=v_rg_w_in, rg_conv_w=v_rg_conv_w,
                rg_conv_b=v_rg_conv_b, rg_wa=v_rg_wa, rg_ba=v_rg_ba, rg_wx=v_rg_wx, rg_bx=v_rg_bx, rg_lambda=v_rg_lambda,
                rg_w_out=v_rg_w_out, gla_w_in=v_gla_w_in, gla_w_alpha=v_gla_w_alpha, gla_b_alpha=v_gla_b_alpha,
                gla_norm_g=v_gla_norm_g, gla_w_out=v_gla_w_out)
    xi, yi, ci = _place()
    chip, me = 2 * xi + yi, 4 * xi + 2 * yi + ci
    d = x.shape[-1]
    depth = ada_w.shape[0]
    n_ada = ada_w.shape[-1]
    sharded_small = [k for k, ax in SMALL.items() if ax is not None]

    sm = _all_gather_8(_pack([c] + [wts[k] for k in sharded_small], SUBLANES, F32), "gather_small")
    c_all = sm[:, 0, :]
    parts = _unpack(sm[0::2], [c.shape] + [wts[k].shape for k in sharded_small])[1:]
    full = {k: _join_shards(p, SMALL[k]) for k, p in zip(sharded_small, parts)}
    for k, ax in SMALL.items():
        if ax is None:
            full[k] = wts[k]

    c16 = jnp.pad(c_all, ((0, ADA_ROWS - N_DEV), (0, 0)))
    ada_b_mine = lax.dynamic_slice_in_dim(ada_b, chip * n_ada, n_ada, 1)[:, None, :]
    mod_cols = _ada_fwd(c16, ada_w, ada_b_mine, "ada_fwd")
    mod_all = _all_gather_8(mod_cols[:, :N_DEV].reshape(-1, PACK_COLS), "gather_mod")[0::2].reshape(N_CHIP, depth, N_DEV, n_ada)
    mod = jnp.swapaxes(lax.dynamic_index_in_dim(mod_all, me, 2, keepdims=False), 0, 1).reshape(depth, 6, d)

    items = [(k, l) for k in BIG for l in range(wts[k].shape[0])]
    stage_of = lambda k, l: "rg" if k.startswith("rg_") else ("ffn0" if (k.startswith("ffn_") and l == 0) else "l1")
    staged = {st: [it for it in items if stage_of(*it) == st] for st in ("rg", "ffn0", "l1")}
    staged["gla"] = [it for it in staged["l1"] if it[0].startswith("gla_")]
    staged["ffn1"] = [it for it in staged["l1"] if it[0].startswith("ffn_")]
    staged["l1"] = staged["gla"] + staged["ffn1"]
    shard = lambda k, l: wts[k][l].reshape(-1, wts[k].shape[-1]).astype(BF16)
    own = lambda got, mine: [lax.dynamic_update_index_in_dim(g, m, chip, 0) for g, m in zip(got, mine)]
    rows_joined = lambda v: v.reshape(-1, v.shape[-1])

    def placed(its, slots):
        out = {"ffn_w_up": {}, "ffn_w_down": {}}
        for (k, l), v in zip(its, slots):
            if k == "ffn_w_up":
                out[k][l] = v
            elif k == "ffn_w_down":
                out[k][l] = rows_joined(v)
            elif k in ("rg_wa", "rg_wx"):
                out[k] = _slots_to_block_rows(v, RG_BLOCKS)
            elif k == "gla_w_in":
                out[k] = _from_col_slots(v, "gla_w_in_join")
            else:
                out[k] = v if BIG[k] else rows_joined(v)
        return out

    after_mod = (mod[0, 0, 0] * 0.0).astype(BF16)
    sh_rg = [shard(k, l) + after_mod for k, l in staged["rg"]]
    local = {k: (v if k in ("norm_g", "ffn_conv_w", "ffn_conv_b") else v[0]) for k, v in full.items()}
    local.update(placed(staged["rg"], own(_gather_chips(sh_rg, "gather_weights_rg"), sh_rg)))
    sh_late, flying = {}, {}
    after_rg = (local["rg_w_out"][0, 0].astype(F32) * 0.0).astype(BF16)
    sh_late["ffn0"] = [shard(k, l) + after_rg for k, l in staged["ffn0"]]
    flying["ffn0"], tok = _send_start(sh_late["ffn0"], "whole", "weights_ffn0_start")
    for stage in ("gla", "ffn1"):
        sh_late[stage] = [shard(k, l) + tok[0, 0].astype(BF16) for k, l in staged[stage]]
        flying[stage], tok = _send_start(sh_late[stage], "whole", f"weights_{stage}_start")
    mod = mod + tok[0, 0]

    def fetch(stage, after):
        mine, got = _send_wait(flying[stage], after, "whole", f"weights_{stage}_wait")
        return placed(staged[stage], own(got, mine))

    c_idx = ci.reshape(1).astype(jnp.int32)
    gslots, paired, psums, sent, started = {}, {}, {}, {}, {}

    def grad_slots(gr, k, l):
        g = gr[k][l] if k in ("ffn_w_up", "ffn_w_down") else gr[k]
        if k in ("rg_wa", "rg_wx"):
            return _block_rows_to_slots(g)
        if k == "gla_w_in":
            return _col_slots(g, "gla_w_in_grad_slots")
        return g if BIG[k] else g.reshape(N_CHIP, -1, g.shape[-1])

    def done(stage, gr):
        gslots[stage] = [grad_slots(gr, k, l) for k, l in staged[stage]]
        paired[stage], token = _send_start(gslots[stage], "pair", f"grads_{stage}_pair_start")
        return token[0, 0]

    def later(stage, after):
        mine, theirs = _send_wait(paired[stage], after, "pair", f"grads_{stage}_pair_wait")
        psums[stage] = [_pair_sum(g, t, c_idx, f"grads_pair_sum_{k}{l}") for (k, l), g, t in zip(staged[stage], mine, theirs)]
        sent[stage], started[stage] = _send_start(psums[stage], "slots", f"grads_{stage}_start")
        return started[stage][0, 0]

    cols, grad_x, gr = _local_step(x[0], loss_target[0], mod, local, fetch, done, later)
    loss_mine = (0.5 * jnp.sum(cols) / d).reshape(1)

    small_names = [k for k in SMALL if k != "ada_b"]
    small_flying, small_sent = _send_start([_pack([gr[k] for k in small_names] + [gr["mod"], loss_mine], SUBLANES, F32)], "all",
                                           "grads_small_start")
    small_shapes = [full[k].shape for k in small_names] + [(depth, 6 * d), (1,)]
    chip_idx = chip.reshape(1).astype(jnp.int32)
    delta, new_m, new_v = {}, {}, {}
    grads = {}

    def reduce_and_update(stages, after, dep):
        its = [(st, n) for st in stages for n in range(len(staged[st]))]
        back = {st: _send_wait(sent[st], after, "slots", f"grads_{st}_wait") for st in stages}
        halves = [_chip_sum(back[st][1][n], back[st][0][n], chip_idx, "grads_chip_sum_%s%d" % staged[st][n]) for st, n in its]
        shared = _pair_share(halves, "grads_pair_share_" + stages[0])
        reduced = [lax.dynamic_update_index_in_dim(s2, h, ci, 0).reshape(-1, h.shape[-1]) for s2, h in zip(shared, halves)]
        last = None
        for k in BIG:
            gs_k = [g for (st, n), g in zip(its, reduced) if staged[st][n][0] == k]
            if gs_k:
                last = update(k, gs_k, dep)
        return last

    def update(k, gs_k, dep=None):
        shp = wts[k].shape
        if k == "gla_w_in":
            view, back = (lambda a: jnp.swapaxes(a, 1, 2)), (lambda o: jnp.swapaxes(o, 1, 2))
            gs_k = [g.T for g in gs_k]
        else:
            view, back = (lambda a: a.reshape(a.shape[0], -1, a.shape[-1])), (lambda o: o.reshape(shp))
        outs = _adamw(view(wts[k]), gs_k, view(mom1[k]), view(mom2[k]), "adamw_" + k, dep)
        grads[k], delta[k], new_m[k], new_v[k] = (back(o) for o in outs)
        return new_v[k]

    later("rg", small_sent)
    done_late = reduce_and_update(("ffn0", "l1"), grad_x, started["rg"])
    (small_mine,), (gs,) = _send_wait(small_flying, done_late, "all", "grads_small_wait")
    gs = lax.dynamic_update_index_in_dim(gs, small_mine, 2 * chip + ci, 0)
    *small_sum, g_ada_b, loss = _unpack(_sum_lead(gs, "sum_small_grads"), small_shapes)
    loss = loss[0]
    grads.update(zip(small_names, small_sum))
    grads["ada_b"] = g_ada_b
    for k in sharded_small:
        grads[k] = _my_shard(grads[k], SMALL[k], chip)
    dmod_all = _unpack(gs, small_shapes)[-2].reshape(N_DEV, depth, N_CHIP, n_ada)
    dmod_mine = jnp.swapaxes(lax.dynamic_index_in_dim(dmod_all, chip, 2, keepdims=False), 0, 1)
    g_ada_w = _ada_bwd(c16, jnp.pad(dmod_mine, ((0, 0), (0, ADA_ROWS - N_DEV), (0, 0))), "ada_bwd")
    update("ada_w", g_ada_w)
    small_shard_shapes = [wts[k].shape for k in SMALL]
    packed = [_pack([src[k] for k in SMALL], SUBLANES, F32) for src in (wts, grads, mom1, mom2)]
    outs = _adamw(packed[0][None], [packed[1]], packed[2][None], packed[3][None], "adamw_small")
    for dst, o in zip((delta, new_m, new_v), outs[1:]):
        for k, a in zip(SMALL, _unpack(o[0], small_shard_shapes)):
            dst[k] = a
    reduce_and_update(("rg",), outs[3], None)

    return (loss, grad_x[None], *[grads[k] for k in WEIGHTS], *[delta[k] for k in WEIGHTS], *[new_m[k] for k in WEIGHTS],
            *[new_v[k] for k in WEIGHTS])
```

```python
import jax
import jax.numpy as jnp
from jax import lax
from jax.experimental import pallas as pl
from jax.experimental.pallas import tpu as pltpu

F32 = jnp.float32
BF16 = jnp.bfloat16
MXU_DTYPE = BF16

EPS = 1e-6
RG_C = 8.0
RG_BLOCKS = 4
RG_CONV = 4
GLA_HEADS = 4
GLA_TAU = 16.0
GLA_CHUNK = 64
GLA_RANK = 16
FFN_CONV = 3
ADAM_LR = 0.001
ADAM_B1 = 0.9
ADAM_B2 = 0.999
ADAM_EPS = 1e-08
ADAM_WD = 0.01
ADAM_STEP = 10

LANES = 128
SUBLANES = 8
VMEM_LIMIT = 56 * 1024 * 1024
CB = 256
MESH = pl.DeviceIdType.MESH
N_DEV = 8
N_CHIP = 4


def _params(*sem):
    return pltpu.CompilerParams(dimension_semantics=sem, vmem_limit_bytes=VMEM_LIMIT)


def _tile(dim, prefs):
    for p in prefs:
        if dim % p == 0:
            return p
    return dim


def _dot(a, b, dims):
    return lax.dot_general(a.astype(MXU_DTYPE), b.astype(MXU_DTYPE), (dims, ((), ())), preferred_element_type=F32)


def _dot_nn(a, b):
    return _dot(a, b, ((1,), (0,)))


def _dot_nt(a, b):
    return _dot(a, b, ((1,), (1,)))


def _dot_tn(a, b):
    return _dot(a, b, ((0,), (0,)))


def _mm(a, b, *, ta=False, tb=False, a_parts=1, b_parts=1, w_slots=1, out_slots=1, out_dtype=F32, tm_max=1408, name):
    if ta:
        k_dim, m_dim = a.shape
        n_dim = b.shape[-1] * b_parts
    else:
        m_dim, k_dim = a.shape[-2], a.shape[-1] * a_parts
        n_dim = b.shape[-2] if tb else b.shape[-1] * w_slots
    n_unit = n_dim // max(b_parts, out_slots, 1 if tb else w_slots)
    k_unit = k_dim // max(a_parts, w_slots if tb else 1)
    tm = _tile(m_dim, tuple(t for t in (1024, 1408, 512, 256, 128) if t <= max(tm_max, 128)))
    tn = _tile(n_unit, (1024, 1408, 896, 512, 256, 128))
    tk = _tile(k_unit, (1024, 1408, 896, 512, 256, 128))
    nk = k_dim // tk
    dims = ((0 if ta else 1,), (1 if tb else 0,))

    def spec(shape, parts, total, tile, col_grid, row_grid):
        per = total // parts // tile

        def index(i, j, k):
            g = {"i": i, "j": j, "k": k}
            col, row = g[col_grid], g[row_grid]
            return (row, col) if parts == 1 else (col // per, row, col % per)

        return pl.BlockSpec(shape if parts == 1 else (None,) + shape, index)

    def body(a_ref, b_ref, o_ref, *acc):
        if nk == 1:
            o_ref[...] = _dot(a_ref[...], b_ref[...], dims).astype(o_ref.dtype)
            return
        acc_ref, k = acc[0], pl.program_id(2)

        @pl.when(k == 0)
        def _():
            acc_ref[...] = jnp.zeros_like(acc_ref)

        acc_ref[...] += _dot(a_ref[...], b_ref[...], dims)

        @pl.when(k == nk - 1)
        def _():
            o_ref[...] = acc_ref[...].astype(o_ref.dtype)

    if ta:
        a_spec = spec((tk, tm), 1, m_dim, tm, "i", "k")
        b_spec = spec((tk, tn), b_parts, n_dim, tn, "j", "k")
    elif tb:
        a_spec = spec((tm, tk), a_parts, k_dim, tk, "k", "i")
        b_spec = spec((tn, tk), w_slots, k_dim, tk, "k", "j")
    else:
        a_spec = spec((tm, tk), a_parts, k_dim, tk, "k", "i")
        b_spec = spec((tk, tn), w_slots, n_dim, tn, "j", "k")
    out_shape = (m_dim, n_dim) if out_slots == 1 else (out_slots, m_dim, n_dim // out_slots)
    return pl.pallas_call(
        body,
        grid=(m_dim // tm, n_dim // tn, nk),
        in_specs=[a_spec, b_spec],
        out_specs=spec((tm, tn), out_slots, n_dim, tn, "j", "i"),
        out_shape=jax.ShapeDtypeStruct(out_shape, out_dtype),
        scratch_shapes=[pltpu.VMEM((tm, tn), F32)] if nk > 1 else [],
        compiler_params=_params("parallel", "parallel", "arbitrary"),
        name=name,
    )(a, b)


ROW_TILES = (1024, 512)


def _row_specs(s, d, ts):
    return pl.BlockSpec((ts, d), lambda i: (i, 0)), pl.BlockSpec((1, d), lambda i: (0, 0))


def _norm_mod_fwd(x, g, sc, sh, name):
    s, d = x.shape
    ts = _tile(s, ROW_TILES)

    def body(x_ref, g_ref, sc_ref, sh_ref, h_ref):
        xv = x_ref[...]
        r = lax.rsqrt(jnp.mean(xv * xv, axis=-1, keepdims=True) + EPS)
        h_ref[...] = (((xv * r) * g_ref[...]) * (1.0 + sc_ref[...]) + sh_ref[...]).astype(h_ref.dtype)

    row, vec = _row_specs(s, d, ts)
    return pl.pallas_call(
        body, grid=(s // ts,), in_specs=[row, vec, vec, vec], out_specs=row,
        out_shape=jax.ShapeDtypeStruct((s, d), MXU_DTYPE), compiler_params=_params("parallel"), name=name,
    )(x, g, sc, sh)


def _norm_mod_bwd(dh, x, g, sc, dres, name):
    s, d = x.shape
    ts = _tile(s, ROW_TILES)

    def body(dh_ref, x_ref, g_ref, sc_ref, dres_ref, dx_ref, dg_ref, dsc_ref, dsh_ref, acc_ref):
        i = pl.program_id(0)

        @pl.when(i == 0)
        def _():
            acc_ref[...] = jnp.zeros_like(acc_ref)

        xv, dhv = x_ref[...], dh_ref[...]
        r = lax.rsqrt(jnp.mean(xv * xv, axis=-1, keepdims=True) + EPS)
        n = xv * r
        acc_ref[0:1, :] += jnp.sum(dhv * n, axis=0, keepdims=True)
        acc_ref[1:2, :] += jnp.sum(dhv, axis=0, keepdims=True)
        dn = dhv * ((1.0 + sc_ref[...]) * g_ref[...])
        dx_ref[...] = dres_ref[...] + r * (dn - n * jnp.mean(dn * n, axis=-1, keepdims=True))
        dg_ref[...] = (1.0 + sc_ref[...]) * acc_ref[0:1, :]
        dsc_ref[...] = g_ref[...] * acc_ref[0:1, :]
        dsh_ref[...] = acc_ref[1:2, :]

    row, vec = _row_specs(s, d, ts)
    vshape = jax.ShapeDtypeStruct((1, d), F32)
    return pl.pallas_call(
        body, grid=(s // ts,), in_specs=[row, row, vec, vec, row], out_specs=[row, vec, vec, vec],
        out_shape=[jax.ShapeDtypeStruct((s, d), F32), vshape, vshape, vshape],
        scratch_shapes=[pltpu.VMEM((SUBLANES, d), F32)], compiler_params=_params("arbitrary"), name=name,
    )(dh, x, g, sc, dres)


def _post_norm_fwd(x, y, g, gt, g2, sc, sh, name):
    s, d = x.shape
    ts = _tile(s, ROW_TILES)

    def body(x_ref, y_ref, g_ref, gt_ref, g2_ref, sc_ref, sh_ref, o_ref, h_ref):
        yv = y_ref[...]
        r = lax.rsqrt(jnp.mean(yv * yv, axis=-1, keepdims=True) + EPS)
        xn = x_ref[...] + gt_ref[...] * ((yv * r) * g_ref[...])
        o_ref[...] = xn
        r2 = lax.rsqrt(jnp.mean(xn * xn, axis=-1, keepdims=True) + EPS)
        h_ref[...] = (((xn * r2) * g2_ref[...]) * (1.0 + sc_ref[...]) + sh_ref[...]).astype(h_ref.dtype)

    row, vec = _row_specs(s, d, ts)
    return pl.pallas_call(
        body, grid=(s // ts,), in_specs=[row, row] + [vec] * 5, out_specs=[row, row],
        out_shape=[jax.ShapeDtypeStruct((s, d), F32), jax.ShapeDtypeStruct((s, d), MXU_DTYPE)],
        compiler_params=_params("parallel"), name=name,
    )(x, y, g, gt, g2, sc, sh)


def _post_bwd(dxn, y, g, gt, name):
    s, d = y.shape
    ts = _tile(s, ROW_TILES)

    def body(dxn_ref, y_ref, g_ref, gt_ref, dy_ref, dg_ref, dgt_ref, acc_ref):
        i = pl.program_id(0)

        @pl.when(i == 0)
        def _():
            acc_ref[...] = jnp.zeros_like(acc_ref)

        yv, dv = y_ref[...], dxn_ref[...]
        r = lax.rsqrt(jnp.mean(yv * yv, axis=-1, keepdims=True) + EPS)
        n = yv * r
        acc_ref[0:1, :] += jnp.sum(dv * n, axis=0, keepdims=True)
        dn = dv * (gt_ref[...] * g_ref[...])
        dy_ref[...] = (r * (dn - n * jnp.mean(dn * n, axis=-1, keepdims=True))).astype(dy_ref.dtype)
        dg_ref[...] = gt_ref[...] * acc_ref[0:1, :]
        dgt_ref[...] = g_ref[...] * acc_ref[0:1, :]

    row, vec = _row_specs(s, d, ts)
    vshape = jax.ShapeDtypeStruct((1, d), F32)
    return pl.pallas_call(
        body, grid=(s // ts,), in_specs=[row, row, vec, vec], out_specs=[row, vec, vec],
        out_shape=[jax.ShapeDtypeStruct((s, d), MXU_DTYPE), vshape, vshape],
        scratch_shapes=[pltpu.VMEM((SUBLANES, d), F32)], compiler_params=_params("arbitrary"), name=name,
    )(dxn, y, g, gt)


def _norm_post_bwd(dh, x, g, sc, dres, y, gp, gt, name):
    s, d = x.shape
    ts = _tile(s, (512,))

    def body(dh_ref, x_ref, g_ref, sc_ref, dres_ref, y_ref, gp_ref, gt_ref,
             dx_ref, dy_ref, dg_ref, dsc_ref, dsh_ref, dgp_ref, dgt_ref, acc_ref):
        i = pl.program_id(0)

        @pl.when(i == 0)
        def _():
            acc_ref[...] = jnp.zeros_like(acc_ref)

        xv, dhv = x_ref[...], dh_ref[...]
        r = lax.rsqrt(jnp.mean(xv * xv, axis=-1, keepdims=True) + EPS)
        n = xv * r
        acc_ref[0:1, :] += jnp.sum(dhv * n, axis=0, keepdims=True)
        acc_ref[1:2, :] += jnp.sum(dhv, axis=0, keepdims=True)
        dn = dhv * ((1.0 + sc_ref[...]) * g_ref[...])
        dx = dres_ref[...] + r * (dn - n * jnp.mean(dn * n, axis=-1, keepdims=True))
        dx_ref[...] = dx
        yv = y_ref[...]
        ry = lax.rsqrt(jnp.mean(yv * yv, axis=-1, keepdims=True) + EPS)
        ny = yv * ry
        acc_ref[2:3, :] += jnp.sum(dx * ny, axis=0, keepdims=True)
        dny = dx * (gt_ref[...] * gp_ref[...])
        dy_ref[...] = (ry * (dny - ny * jnp.mean(dny * ny, axis=-1, keepdims=True))).astype(dy_ref.dtype)
        dg_ref[...] = (1.0 + sc_ref[...]) * acc_ref[0:1, :]
        dsc_ref[...] = g_ref[...] * acc_ref[0:1, :]
        dsh_ref[...] = acc_ref[1:2, :]
        dgp_ref[...] = gt_ref[...] * acc_ref[2:3, :]
        dgt_ref[...] = gp_ref[...] * acc_ref[2:3, :]

    row, vec = _row_specs(s, d, ts)
    vshape = jax.ShapeDtypeStruct((1, d), F32)
    return pl.pallas_call(
        body, grid=(s // ts,), in_specs=[row, row, vec, vec, row, row, vec, vec], out_specs=[row, row] + [vec] * 5,
        out_shape=[jax.ShapeDtypeStruct((s, d), F32), jax.ShapeDtypeStruct((s, d), MXU_DTYPE)] + [vshape] * 5,
        scratch_shapes=[pltpu.VMEM((SUBLANES, d), F32)], compiler_params=_params("arbitrary"), name=name,
    )(dh, x, g, sc, dres, y, gp, gt)


def _post_loss(x, y, g, gt, tgt, name):
    s, d = x.shape
    ts = _tile(s, ROW_TILES)

    def body(x_ref, y_ref, g_ref, gt_ref, t_ref, col_ref, dx_ref):
        i = pl.program_id(0)

        @pl.when(i == 0)
        def _():
            col_ref[...] = jnp.zeros_like(col_ref)

        yv = y_ref[...]
        r = lax.rsqrt(jnp.mean(yv * yv, axis=-1, keepdims=True) + EPS)
        e = (x_ref[...] + gt_ref[...] * ((yv * r) * g_ref[...])) - t_ref[...]
        col_ref[...] += jnp.sum(e * e, axis=0, keepdims=True)
        dx_ref[...] = e * (1.0 / d)

    row, vec = _row_specs(s, d, ts)
    return pl.pallas_call(
        body, grid=(s // ts,), in_specs=[row, row, vec, vec, row], out_specs=[vec, row],
        out_shape=[jax.ShapeDtypeStruct((1, d), F32), jax.ShapeDtypeStruct((s, d), F32)],
        compiler_params=_params("arbitrary"), name=name,
    )(x, y, g, gt, tgt)


_GELU_C = 0.7978845608028654
_GELU_A = 0.044715


def _gelu(x):
    t = jnp.tanh(_GELU_C * (x + _GELU_A * x * x * x))
    return 0.5 * x * (1.0 + t), t


def _gelu_grad(x, t):
    return 0.5 * (1.0 + t) + 0.5 * x * (1.0 - t * t) * (_GELU_C * (1.0 + 3.0 * _GELU_A * x * x))


def _sigmoid(x):
    return 1.0 / (1.0 + jnp.exp(-x))


def _log1p_pos(y):
    u = 1.0 + y
    return jnp.where(u == 1.0, y, jnp.log(u) * (y / jnp.where(u == 1.0, 1.0, u - 1.0)))


def _softplus(x):
    return jnp.maximum(x, 0.0) + _log1p_pos(jnp.exp(-jnp.abs(x)))


def _one_minus_sq_exp(x, ex):
    z = 2.0 * x
    series = -z * (1.0 + z * (1.0 / 2 + z * (1.0 / 6 + z * (1.0 / 24 + z * (1.0 / 120)))))
    return jnp.where(z > -0.05, series, 1.0 - ex * ex)


SLAB = 16


def _cat(a, b):
    return jnp.concatenate([a, b], axis=1)


def _pair_specs(shape, nb, index):
    return [pl.BlockSpec(shape, lambda j, t: index(j, t) + (j,)), pl.BlockSpec(shape, lambda j, t: index(j, t) + (j + nb,))]


def _halo_row(ts, time_of):
    return lambda j, t: (jnp.maximum(time_of(t) * (ts // SUBLANES) - 1, 0),)


def _rows_from(groups, k):
    row = lax.broadcasted_iota(jnp.int32, groups[0].shape, 0)
    turned = [pltpu.roll(g, SUBLANES - k, axis=0) for g in groups]
    return [jnp.where(row < SUBLANES - k, lo, hi) for lo, hi in zip(turned[:-1], turned[1:])]


def _ffn_mid_fwd(p, cw, cb, name):
    s, f2 = p.shape
    ts = _tile(s, (1024, 512))
    nb, nt = f2 // (2 * CB), s // ts
    n_grp = SLAB // SUBLANES

    def body(pg_ref, pv_ref, hg_ref, hv_ref, cwg_ref, cwv_ref, cbg_ref, cbv_ref, a_ref, ga_ref, gb_ref):
        t = pl.program_id(1)
        cwv, bias = _cat(cwg_ref[...], cwv_ref[...]), _cat(cbg_ref[...], cbv_ref[...])
        w0, w1, w2 = cwv[0:1], cwv[1:2], cwv[2:3]

        def slab(before, cur, r0):
            pm2, pm1 = _rows_from([before] + cur, SUBLANES - 2), _rows_from([before] + cur, SUBLANES - 1)
            u = jnp.concatenate([bias + w0 * pm2[i] + w1 * pm1[i] + w2 * cur[i] for i in range(n_grp)], axis=0)
            g, v = u[:, :CB], u[:, CB:]
            gel, th = _gelu(g)
            rows = pl.ds(r0, SLAB)
            a_ref[rows, :] = (gel * v).astype(a_ref.dtype)
            ga_ref[rows, :] = gel.astype(ga_ref.dtype)
            gb_ref[rows, :] = (v * _gelu_grad(g, th)).astype(gb_ref.dtype)

        def pieces(rows):
            blk = _cat(pg_ref[rows, :], pv_ref[rows, :])
            return [blk[i * SUBLANES:(i + 1) * SUBLANES] for i in range(blk.shape[0] // SUBLANES)]

        slab(jnp.where(t > 0, _cat(hg_ref[...], hv_ref[...]), 0.0), pieces(pl.ds(0, SLAB)), 0)

        def loop(i, carry):
            r0 = pl.multiple_of(i * SLAB, SLAB)
            got = pieces(pl.ds(pl.multiple_of(r0 - SUBLANES, SUBLANES), SLAB + SUBLANES))
            slab(got[0], got[1:], r0)
            return carry

        lax.fori_loop(1, ts // SLAB, loop, 0, unroll=2)

    fwd = lambda t: t
    out = pl.BlockSpec((ts, CB), lambda j, t: (t, j))
    shape = jax.ShapeDtypeStruct((s, f2 // 2), MXU_DTYPE)
    return pl.pallas_call(
        body, grid=(nb, nt),
        in_specs=(_pair_specs((ts, CB), nb, lambda j, t: (t,)) + _pair_specs((SUBLANES, CB), nb, _halo_row(ts, fwd))
                  + _pair_specs((FFN_CONV, CB), nb, lambda j, t: (0,)) + _pair_specs((1, CB), nb, lambda j, t: (0,))),
        out_specs=[out, out, out], out_shape=[shape, shape, shape],
        compiler_params=_params("parallel", "arbitrary"), name=name,
    )(p, p, p, p, cw, cw, cb, cb)


def _ffn_mid_bwd(da, p, ga, gb, cw, name):
    s, f2 = p.shape
    ts = _tile(s, (1024, 512))
    nb, nt = f2 // (2 * CB), s // ts
    n_slab = ts // SLAB
    n_grp = SLAB // SUBLANES
    per_trip = 2

    def body(da_ref, ga_ref, gb_ref, pg_ref, pv_ref, cwg_ref, cwv_ref, dp_ref, dcw_ref, dcb_ref, next_du, acc):
        tt = pl.program_id(1)
        cwv = _cat(cwg_ref[...], cwv_ref[...])
        w0, w1, w2 = cwv[0:1], cwv[1:2], cwv[2:3]

        @pl.when(tt == 0)
        def _():
            next_du[...] = jnp.zeros_like(next_du)
            acc[...] = jnp.zeros_like(acc)

        def slab(r0, after, sums):
            rows = pl.ds(r0, SLAB)
            dav = da_ref[rows, :]
            du = _cat(dav * gb_ref[rows, :].astype(F32), dav * ga_ref[rows, :].astype(F32))
            p0 = _cat(pg_ref[rows, :], pv_ref[rows, :])
            cur = [du[i * SUBLANES:(i + 1) * SUBLANES] for i in range(n_grp)]
            du1, du2 = _rows_from(cur + [after], 1), _rows_from(cur + [after], 2)
            dpv = jnp.concatenate([w2 * cur[i] + w1 * du1[i] + w0 * du2[i] for i in range(n_grp)], axis=0).astype(dp_ref.dtype)
            dp_ref[0, rows, :] = dpv[:, :CB]
            dp_ref[1, rows, :] = dpv[:, CB:]
            for i in range(n_grp):
                pi = p0[i * SUBLANES:(i + 1) * SUBLANES]
                parts = (cur[i], du2[i] * pi, du1[i] * pi, cur[i] * pi)
                sums = parts if sums is None else tuple(x + y for x, y in zip(sums, parts))
            return cur[0], sums

        def loop(k, after):
            sums = None
            for j in range(per_trip):
                r0 = pl.multiple_of((n_slab - 1 - (k * per_trip + j)) * SLAB, SLAB)
                after, sums = slab(r0, after, sums)
            for q, part in enumerate(sums):
                acc[q] += part
            return after

        next_du[...] = lax.fori_loop(0, n_slab // per_trip, loop, next_du[...])

        @pl.when(tt == nt - 1)
        def _():
            for half in range(2):
                cols = slice(half * CB, (half + 1) * CB)
                dcb_ref[half] = jnp.sum(acc[0][:, cols], axis=0, keepdims=True)
                for k in range(FFN_CONV):
                    dcw_ref[half, k:k + 1, :] = jnp.sum(acc[1 + k][:, cols], axis=0, keepdims=True)

    rev = lambda t: nt - 1 - t
    tile = pl.BlockSpec((ts, CB), lambda j, t: (rev(t), j))
    return pl.pallas_call(
        body, grid=(nb, nt),
        in_specs=([tile, tile, tile] + _pair_specs((ts, CB), nb, lambda j, t: (rev(t),))
                  + _pair_specs((FFN_CONV, CB), nb, lambda j, t: (0,))),
        out_specs=[pl.BlockSpec((2, ts, CB), lambda j, t: (0, rev(t), j)),
                   pl.BlockSpec((2, FFN_CONV, CB), lambda j, t: (0, 0, j)),
                   pl.BlockSpec((2, 1, CB), lambda j, t: (0, 0, j))],
        out_shape=[jax.ShapeDtypeStruct((2, s, f2 // 2), MXU_DTYPE), jax.ShapeDtypeStruct((2, FFN_CONV, f2 // 2), F32),
                   jax.ShapeDtypeStruct((2, 1, f2 // 2), F32)],
        scratch_shapes=[pltpu.VMEM((SUBLANES, 2 * CB), F32), pltpu.VMEM((1 + FFN_CONV, SUBLANES, 2 * CB), F32)],
        compiler_params=_params("parallel", "arbitrary"), name=name,
    )(da, ga, gb, p, p, cw, cw)


def _rg_gates(xc, wa_ref, ba_ref, wx_ref, bx_ref, lam_ref):
    r = _sigmoid(_dot_nn(xc, wa_ref[0]) + ba_ref[...])
    ig = _sigmoid(_dot_nn(xc, wx_ref[0]) + bx_ref[...])
    sp = _softplus(-lam_ref[...])
    log_a = (-RG_C) * r * sp
    a = jnp.exp(log_a)
    mult = jnp.sqrt(_one_minus_sq_exp(log_a, a))
    return r, ig, sp, a, mult


def _rg_conv(scr, cw_ref, cb_ref, ts):
    views = [scr[5 + k:5 + k + ts, :] for k in range(RG_CONV)]
    xc = cb_ref[...]
    for k in range(RG_CONV):
        xc = xc + cw_ref[k:k + 1, :] * views[k]
    return xc, views


def _rg_param_specs():
    vec = pl.BlockSpec((1, CB), lambda g, t: (0, g))
    mat = pl.BlockSpec((1, CB, CB), lambda g, t: (g, 0, 0))
    return [pl.BlockSpec((RG_CONV, CB), lambda g, t: (0, g)), vec, mat, vec, mat, vec, vec]


def _scan_rows(a_scr, x_scr, out_ref, carry, ts, reverse):
    n = ts // SUBLANES
    row = lax.broadcasted_iota(jnp.int32, (SUBLANES, a_scr.shape[1]), 0)
    last = SUBLANES - 1

    def rows_of(k):
        return pl.ds(pl.multiple_of(k * SUBLANES, SUBLANES), SUBLANES)

    def local(k, _):
        rows = rows_of(k)
        a, x = a_scr[rows, :], x_scr[rows, :]
        if reverse:
            a = jnp.where(row == last, 1.0, pltpu.roll(a, last, axis=0))
            for sh in (1, 2, 4):
                keep = row < SUBLANES - sh
                x = x + a * jnp.where(keep, pltpu.roll(x, SUBLANES - sh, axis=0), 0.0)
                a = a * jnp.where(keep, pltpu.roll(a, SUBLANES - sh, axis=0), 1.0)
        else:
            for sh in (1, 2, 4):
                keep = row >= sh
                x = a * jnp.where(keep, pltpu.roll(x, sh, axis=0), 0.0) + x
                a = a * jnp.where(keep, pltpu.roll(a, sh, axis=0), 1.0)
        out_ref[rows, :] = x
        x_scr[rows, :] = a
        return 0

    lax.fori_loop(0, n, local, 0, unroll=4)

    def chain(k, c):
        rows = rows_of(n - 1 - k if reverse else k)
        v = out_ref[rows, :] + x_scr[rows, :] * c
        out_ref[rows, :] = v
        return a_scr[rows, :][0:1] * v[0:1] if reverse else v[last:last + 1]

    return lax.fori_loop(0, n, chain, carry, unroll=4)


def _rg_mid_fwd(pj, cw, cb, wa, ba, wx, bx, lam, name):
    s = pj.shape[0]
    nb = pj.shape[1] // (2 * CB)
    ts = _tile(s, (1024, 512))
    nt = s // ts

    def body(gate_ref, x_ref, halo_ref, cw_ref, cb_ref, wa_ref, ba_ref, wx_ref, bx_ref, lam_ref, y_ref, hs_ref,
             scr, a_scr, u_scr, h_scr):
        t = pl.program_id(1)

        @pl.when(t == 0)
        def _():
            h_scr[...] = jnp.zeros_like(h_scr)

        scr[0:SUBLANES, :] = jnp.where(t > 0, halo_ref[...], 0.0)
        scr[SUBLANES:, :] = x_ref[...]
        xc, _ = _rg_conv(scr, cw_ref, cb_ref, ts)
        _, ig, _, a, mult = _rg_gates(xc, wa_ref, ba_ref, wx_ref, bx_ref, lam_ref)
        a_scr[...] = a
        u_scr[...] = mult * (ig * xc)
        h_scr[0:1, :] = _scan_rows(a_scr, u_scr, hs_ref, h_scr[0:1, :], ts, False)
        y_ref[...] = (_gelu(gate_ref[...])[0] * hs_ref[...]).astype(y_ref.dtype)

    blk = pl.BlockSpec((ts, CB), lambda g, t: (t, g))
    return pl.pallas_call(
        body, grid=(nb, nt),
        in_specs=_pair_specs((ts, CB), nb, lambda g, t: (t,))
        + [pl.BlockSpec((SUBLANES, CB), lambda g, t: _halo_row(ts, lambda u: u)(g, t) + (g + nb,))] + _rg_param_specs(),
        out_specs=[blk, blk],
        out_shape=[jax.ShapeDtypeStruct((s, nb * CB), MXU_DTYPE), jax.ShapeDtypeStruct((s, nb * CB), F32)],
        scratch_shapes=[pltpu.VMEM((ts + SUBLANES, CB), F32), pltpu.VMEM((ts, CB), F32), pltpu.VMEM((ts, CB), F32),
                        pltpu.VMEM((SUBLANES, CB), F32)],
        compiler_params=_params("parallel", "arbitrary"), name=name,
    )(pj, pj, pj, cw, cb, wa, ba, wx, bx, lam)


def _rg_mid_bwd(dy, pj, hs, cw, cb, wa, ba, wx, bx, lam, name):
    s = pj.shape[0]
    nb = pj.shape[1] // (2 * CB)
    ts = _tile(s, (1024, 512))
    nt = s // ts

    def body(dy_ref, gate_ref, x_ref, halo_ref, hs_ref, hsh_ref, cw_ref, cb_ref, wa_ref, ba_ref, wx_ref, bx_ref, lam_ref,
             dpj_ref, dcw_ref, dcb_ref, dwa_ref, dba_ref, dwx_ref, dbx_ref, dlam_ref,
             scr, hscr, a_scr, d_scr, g_scr, dxscr, c_scr):
        tt = pl.program_id(1)
        t = nt - 1 - tt

        @pl.when(tt == 0)
        def _():
            c_scr[...] = jnp.zeros_like(c_scr)
            dxscr[ts:, :] = jnp.zeros((SUBLANES, CB), F32)
            for ref in (dcw_ref, dcb_ref, dwa_ref, dba_ref, dwx_ref, dbx_ref, dlam_ref):
                ref[...] = jnp.zeros_like(ref)

        scr[0:SUBLANES, :] = jnp.where(t > 0, halo_ref[...], 0.0)
        scr[SUBLANES:, :] = x_ref[...]
        hscr[0:SUBLANES, :] = jnp.where(t > 0, hsh_ref[...], 0.0)
        hscr[SUBLANES:, :] = hs_ref[...]
        xc, views = _rg_conv(scr, cw_ref, cb_ref, ts)
        r, ig, sp, a, mult = _rg_gates(xc, wa_ref, ba_ref, wx_ref, bx_ref, lam_ref)
        gate = gate_ref[...]
        gel, th = _gelu(gate)
        dyv = dy_ref[...]
        dpj_ref[0] = (dyv * hs_ref[...] * _gelu_grad(gate, th)).astype(dpj_ref.dtype)
        a_scr[...] = a
        d_scr[...] = dyv * gel
        c_scr[0:1, :] = _scan_rows(a_scr, d_scr, g_scr, c_scr[0:1, :], ts, True)
        du = g_scr[...]
        da = du * hscr[7:7 + ts, :]
        dmult = du * (ig * xc)
        dig = du * (mult * xc)
        dxc = du * (mult * ig)
        dlog_a = da * a - dmult * (a * a / mult)
        dlam_ref[...] += jnp.sum(dlog_a * r, axis=0, keepdims=True) * (RG_C * _sigmoid(-lam_ref[...]))
        dpr = dlog_a * ((-RG_C) * sp) * (r * (1.0 - r))
        dpi = dig * (ig * (1.0 - ig))
        dba_ref[...] += jnp.sum(dpr, axis=0, keepdims=True)
        dbx_ref[...] += jnp.sum(dpi, axis=0, keepdims=True)
        dwa_ref[0] += _dot_tn(xc, dpr)
        dwx_ref[0] += _dot_tn(xc, dpi)
        dxc = dxc + _dot_nt(dpr, wa_ref[0]) + _dot_nt(dpi, wx_ref[0])
        dcb_ref[...] += jnp.sum(dxc, axis=0, keepdims=True)
        for k in range(RG_CONV):
            dcw_ref[k:k + 1, :] += jnp.sum(dxc * views[k], axis=0, keepdims=True)
        dxscr[0:ts, :] = dxc
        dxp = cw_ref[3:4, :] * dxc
        for k in range(RG_CONV - 1):
            dxp = dxp + cw_ref[k:k + 1, :] * dxscr[3 - k:3 - k + ts, :]
        dpj_ref[1] = dxp.astype(dpj_ref.dtype)
        dxscr[ts:, :] = dxscr[0:SUBLANES, :]

    rev = lambda g, t: (nt - 1 - t, g)
    rev_halo = lambda g, t: (jnp.maximum((nt - 1 - t) * (ts // SUBLANES) - 1, 0), g)
    vec = pl.BlockSpec((1, CB), lambda g, t: (0, g))
    mat = pl.BlockSpec((1, CB, CB), lambda g, t: (g, 0, 0))
    d = nb * CB
    vshape = jax.ShapeDtypeStruct((1, d), F32)
    mshape = jax.ShapeDtypeStruct((nb, CB, CB), F32)
    return pl.pallas_call(
        body, grid=(nb, nt),
        in_specs=[pl.BlockSpec((ts, CB), rev)] + _pair_specs((ts, CB), nb, lambda g, t: (nt - 1 - t,))
        + [pl.BlockSpec((SUBLANES, CB), lambda g, t: (rev_halo(g, t)[0], g + nb)),
           pl.BlockSpec((ts, CB), rev), pl.BlockSpec((SUBLANES, CB), rev_halo)] + _rg_param_specs(),
        out_specs=[pl.BlockSpec((2, ts, CB), lambda g, t: (0, nt - 1 - t, g)), pl.BlockSpec((RG_CONV, CB), lambda g, t: (0, g)),
                   vec, mat, vec, mat, vec, vec],
        out_shape=[jax.ShapeDtypeStruct((2, s, d), MXU_DTYPE), jax.ShapeDtypeStruct((RG_CONV, d), F32), vshape, mshape, vshape,
                   mshape, vshape, vshape],
        scratch_shapes=[pltpu.VMEM((ts + SUBLANES, CB), F32), pltpu.VMEM((ts + SUBLANES, CB), F32), pltpu.VMEM((ts, CB), F32),
                        pltpu.VMEM((ts, CB), F32), pltpu.VMEM((ts, CB), F32), pltpu.VMEM((ts + SUBLANES, CB), F32),
                        pltpu.VMEM((SUBLANES, CB), F32)],
        compiler_params=_params("parallel", "arbitrary"), name=name,
    )(dy, pj, pj, pj, hs, hs, cw, cb, wa, ba, wx, bx, lam)


GLA_DK = 128
GLA_DV = 256
GLA_O_K = GLA_HEADS * GLA_DK
GLA_O_V = 2 * GLA_HEADS * GLA_DK
GLA_O_R = GLA_O_V + GLA_HEADS * GLA_DV
GLA_O_Z = GLA_O_R + GLA_HEADS * GLA_DV
GLA_IN = GLA_O_Z + GLA_RANK
GLA_TS = 256


def _dk(h, base=0):
    return slice(base + h * GLA_DK, base + (h + 1) * GLA_DK)


def _dv(h, base=0):
    return slice(base + h * GLA_DV, base + (h + 1) * GLA_DV)


def _split3(x):
    hi = x.astype(BF16)
    r1 = x - hi.astype(F32)
    mid = r1.astype(BF16)
    lo = (r1 - mid.astype(F32)).astype(BF16)
    return hi, mid, lo


def _chunk_cumsum(x, reverse):
    n = x.shape[0]
    i = lax.broadcasted_iota(jnp.int32, (n, n), 0)
    j = lax.broadcasted_iota(jnp.int32, (n, n), 1)
    same = (i // GLA_CHUNK) == (j // GLA_CHUNK)
    tri = jnp.where(same & ((j >= i) if reverse else (j <= i)), 1.0, 0.0).astype(BF16)
    out = jnp.zeros(x.shape, F32)
    for piece in _split3(x):
        out = out + lax.dot_general(tri, piece, (((1,), (0,)), ((), ())), preferred_element_type=F32)
    return out


def _gla_head(pj_ref, h):
    return (pj_ref[:, _dk(h)] * (GLA_DK ** -0.5), pj_ref[:, _dk(h, GLA_O_K)], pj_ref[:, _dv(h, GLA_O_V)],
            pj_ref[:, _dv(h, GLA_O_R)])


def _gla_decays(gc):
    gref = gc[GLA_CHUNK // 2:GLA_CHUNK // 2 + 1, :]
    glast = gc[GLA_CHUNK - 1:GLA_CHUNK, :]
    return jnp.exp(gc), jnp.exp(gc - gref), jnp.exp(gref - gc), jnp.exp(glast - gc), jnp.exp(glast)


def _causal_mask():
    i = lax.broadcasted_iota(jnp.int32, (GLA_CHUNK, GLA_CHUNK), 0)
    j = lax.broadcasted_iota(jnp.int32, (GLA_CHUNK, GLA_CHUNK), 1)
    return j <= i


def _log_sigmoid(x):
    return jnp.minimum(x, 0.0) - _log1p_pos(jnp.exp(-jnp.abs(x)))


def _gla_mid_fwd(pj, wal, bal, ng, name):
    s, nh = pj.shape[0], GLA_HEADS
    ts = _tile(s, (GLA_TS,))
    nt, nc = s // ts, ts // GLA_CHUNK

    def body(pj_ref, wal_ref, bal_ref, ng_ref, act_ref, o_ref, st_ref, s_scr):
        t = pl.program_id(0)

        @pl.when(t == 0)
        def _():
            s_scr[...] = jnp.zeros_like(s_scr)

        heads = []
        z = pj_ref[:, GLA_O_Z:]
        for h in range(nh):
            q, k, v, r = _gla_head(pj_ref, h)
            g = _log_sigmoid(_dot_nn(z, wal_ref[:, _dk(h)]) + bal_ref[:, _dk(h)]) * (1.0 / GLA_TAU)
            heads.append((q, k, v, r, _chunk_cumsum(g, False)))
        mask = _causal_mask()
        for c in range(nc):
            sl = slice(c * GLA_CHUNK, (c + 1) * GLA_CHUNK)
            for h, (q, k, v, r, gcum) in enumerate(heads):
                eg, eq, ek, ekd, egl = _gla_decays(gcum[sl])
                st = s_scr[h]
                st_ref[c, h] = st
                attn = jnp.where(mask, _dot_nt(q[sl] * eq, k[sl] * ek), 0.0)
                o_ref[sl, h * GLA_DV:(h + 1) * GLA_DV] = _dot_nt(q[sl] * eg, st) + _dot_nn(attn, v[sl])
                s_scr[h] = st * egl + _dot_tn(v[sl], k[sl] * ekd)
        for h, (q, k, v, r, gcum) in enumerate(heads):
            cols = slice(h * GLA_DV, (h + 1) * GLA_DV)
            o = o_ref[:, cols]
            on = o * lax.rsqrt(jnp.mean(o * o, axis=-1, keepdims=True) + EPS)
            act_ref[:, cols] = ((on * ng_ref[...]) * (r * _sigmoid(r))).astype(act_ref.dtype)

    blk = pl.BlockSpec((ts, nh * GLA_DV), lambda t: (t, 0))
    whole = lambda shape: pl.BlockSpec(shape, lambda t: (0,) * len(shape))
    return pl.pallas_call(
        body, grid=(nt,),
        in_specs=[pl.BlockSpec((ts, GLA_IN), lambda t: (t, 0)), whole((GLA_RANK, nh * GLA_DK)), whole((1, nh * GLA_DK)),
                  whole((1, GLA_DV))],
        out_specs=[blk, blk, pl.BlockSpec((nc, nh, GLA_DV, GLA_DK), lambda t: (t, 0, 0, 0))],
        out_shape=[jax.ShapeDtypeStruct((s, nh * GLA_DV), MXU_DTYPE), jax.ShapeDtypeStruct((s, nh * GLA_DV), F32),
                   jax.ShapeDtypeStruct((s // GLA_CHUNK, nh, GLA_DV, GLA_DK), F32)],
        scratch_shapes=[pltpu.VMEM((nh, GLA_DV, GLA_DK), F32)],
        compiler_params=_params("arbitrary"), name=name,
    )(pj, wal, bal, ng)


def _gla_mid_bwd(dact, pj, o, st, wal, bal, ng, name):
    s, nh = pj.shape[0], GLA_HEADS
    ts = _tile(s, (GLA_TS,))
    nt, nc = s // ts, ts // GLA_CHUNK

    def body(dact_ref, pj_ref, o_ref, st_ref, wal_ref, bal_ref, ng_ref, dpj_ref, dwal_ref, dbal_ref, dng_ref,
             ds_scr, dg_scr):
        tt = pl.program_id(0)

        @pl.when(tt == 0)
        def _():
            ds_scr[...] = jnp.zeros_like(ds_scr)
            dwal_ref[...] = jnp.zeros_like(dwal_ref)
            dbal_ref[...] = jnp.zeros_like(dbal_ref)
            dng_ref[...] = jnp.zeros_like(dng_ref)

        heads = []
        z = pj_ref[:, GLA_O_Z:]
        for h in range(nh):
            q, k, v, r = _gla_head(pj_ref, h)
            logit = _dot_nn(z, wal_ref[:, _dk(h)]) + bal_ref[:, _dk(h)]
            gcum = _chunk_cumsum(_log_sigmoid(logit) * (1.0 / GLA_TAU), False)
            ov = o_ref[:, h * GLA_DV:(h + 1) * GLA_DV]
            ro = lax.rsqrt(jnp.mean(ov * ov, axis=-1, keepdims=True) + EPS)
            on = ov * ro
            sg = _sigmoid(r)
            sil = r * sg
            dav = dact_ref[:, h * GLA_DV:(h + 1) * GLA_DV]
            dpj_ref[:, _dv(h, GLA_O_R)] = (dav * (on * ng_ref[...]) * (sg + sil * (1.0 - sg))).astype(dpj_ref.dtype)
            t1 = dav * sil
            dng_ref[...] += jnp.sum(t1 * on, axis=0, keepdims=True)
            dn = t1 * ng_ref[...]
            do = ro * (dn - on * jnp.mean(dn * on, axis=-1, keepdims=True))
            heads.append((q, k, v, logit, gcum, do))
        mask = _causal_mask()
        scale = GLA_DK ** -0.5
        last_row = lax.broadcasted_iota(jnp.int32, (GLA_CHUNK, GLA_DK), 0) == GLA_CHUNK - 1
        for c in reversed(range(nc)):
            sl = slice(c * GLA_CHUNK, (c + 1) * GLA_CHUNK)
            for h, (q, k, v, logit, gcum, do) in enumerate(heads):
                eg, eq, ek, ekd, egl = _gla_decays(gcum[sl])
                qc, kc, vc, doc = q[sl], k[sl], v[sl], do[sl]
                qg, qt, kt, kd = qc * eg, qc * eq, kc * ek, kc * ekd
                sp = st_ref[c, h]
                ds = ds_scr[h]
                attn = jnp.where(mask, _dot_nt(qt, kt), 0.0)
                dattn = jnp.where(mask, _dot_nt(doc, vc), 0.0)
                dqg = _dot_nn(doc, sp)
                dqt = _dot_nn(dattn, kt)
                dkt = _dot_tn(dattn, qt)
                dkd = _dot_nn(vc, ds)
                dpj_ref[sl, _dv(h, GLA_O_V)] = (_dot_tn(attn, doc) + _dot_nt(kd, ds)).astype(dpj_ref.dtype)
                dpj_ref[sl, _dk(h)] = (scale * (dqg * eg + dqt * eq)).astype(dpj_ref.dtype)
                dpj_ref[sl, _dk(h, GLA_O_K)] = (dkt * ek + dkd * ekd).astype(dpj_ref.dtype)
                kdd = dkd * kd
                dgl = jnp.sum(kdd, axis=0, keepdims=True) + jnp.sum(ds * sp, axis=0, keepdims=True) * egl
                dg_scr[h, sl, :] = dqg * qg + dqt * qt - dkt * kt - kdd + jnp.where(last_row, dgl, 0.0)
                ds_scr[h] = ds * egl + _dot_tn(doc, qg)
        dz = jnp.zeros((ts, GLA_RANK), F32)
        for h, (q, k, v, logit, gcum, do) in enumerate(heads):
            dlogit = _chunk_cumsum(dg_scr[h], True) * (1.0 / GLA_TAU) * _sigmoid(-logit)
            dz = dz + _dot_nt(dlogit, wal_ref[:, _dk(h)])
            dwal_ref[:, _dk(h)] += _dot_tn(z, dlogit)
            dbal_ref[:, _dk(h)] += jnp.sum(dlogit, axis=0, keepdims=True)
        dpj_ref[:, GLA_O_Z:] = dz.astype(dpj_ref.dtype)

    rev = lambda t: (nt - 1 - t, 0)
    whole = lambda shape: pl.BlockSpec(shape, lambda t: (0,) * len(shape))
    wide = pl.BlockSpec((ts, nh * GLA_DV), rev)
    return pl.pallas_call(
        body, grid=(nt,),
        in_specs=[wide, pl.BlockSpec((ts, GLA_IN), rev), wide,
                  pl.BlockSpec((nc, nh, GLA_DV, GLA_DK), lambda t: (nt - 1 - t, 0, 0, 0)),
                  whole((GLA_RANK, nh * GLA_DK)), whole((1, nh * GLA_DK)), whole((1, GLA_DV))],
        out_specs=[pl.BlockSpec((ts, GLA_IN), rev), whole((GLA_RANK, nh * GLA_DK)), whole((1, nh * GLA_DK)), whole((1, GLA_DV))],
        out_shape=[jax.ShapeDtypeStruct((s, GLA_IN), MXU_DTYPE), jax.ShapeDtypeStruct((GLA_RANK, nh * GLA_DK), F32),
                   jax.ShapeDtypeStruct((1, nh * GLA_DK), F32), jax.ShapeDtypeStruct((1, GLA_DV), F32)],
        scratch_shapes=[pltpu.VMEM((nh, GLA_DV, GLA_DK), F32), pltpu.VMEM((nh, ts, GLA_DK), F32)],
        compiler_params=_params("arbitrary"), name=name,
    )(dact, pj, o, st, wal, bal, ng)


def _adamw(w, gs, m, v, name, after=None):
    layers, rows, cols = w.shape
    gs = list(gs) if isinstance(gs, (list, tuple)) else gs
    n_g = len(gs) if isinstance(gs, list) else 1
    if rows % SUBLANES == 0:
        tr, tc = _tile(rows, (256, 128, 64, 32, 16, 8)), cols
    else:
        tr, tc = rows, _tile(cols, (256, 128))
    c1 = 1.0 / (1.0 - ADAM_B1 ** ADAM_STEP)
    c2 = 1.0 / (1.0 - ADAM_B2 ** ADAM_STEP)

    def body(*refs):
        g_refs, (w_ref, m_ref, v_ref) = refs[:n_g], refs[n_g:n_g + 3]
        go_ref, d_ref, mo_ref, vo_ref = refs[-4:]
        gv = g_refs[0][...]
        for l in range(1, n_g):
            gv = jnp.where(pl.program_id(0) == l, g_refs[l][...], gv)
        m2 = ADAM_B1 * m_ref[...] + (1.0 - ADAM_B1) * gv
        v2 = ADAM_B2 * v_ref[...] + (1.0 - ADAM_B2) * (gv * gv)
        d_ref[...] = (-ADAM_LR) * ((m2 * c1) / (jnp.sqrt(v2 * c2) + ADAM_EPS) + ADAM_WD * w_ref[...])
        go_ref[...] = gv
        mo_ref[...] = m2
        vo_ref[...] = v2

    spec = pl.BlockSpec((None, tr, tc), lambda l, i, j: (l, i, j))
    g_specs = [pl.BlockSpec((tr, tc), lambda l, i, j: (i, j))] * n_g if isinstance(gs, list) else [spec]
    extra = [] if after is None else [(after, _ANY)]
    shape = jax.ShapeDtypeStruct((layers, rows, cols), F32)
    return pl.pallas_call(
        body, grid=(layers, rows // tr, cols // tc), in_specs=g_specs + [spec] * 3 + [sp for _, sp in extra],
        out_specs=[spec] * 4, out_shape=[shape] * 4, compiler_params=_params("parallel", "parallel", "parallel"), name=name,
    )(*(gs if isinstance(gs, list) else [gs]), w, m, v, *[a for a, _ in extra])


def _col_slots(w, name):
    r, cc = w.shape
    c = cc // N_CHIP
    tr = _tile(r, (256,))

    def body(w_ref, o_ref):
        for j in range(N_CHIP):
            o_ref[j] = w_ref[:, j * c:(j + 1) * c]

    return pl.pallas_call(
        body, grid=(r // tr,), in_specs=[pl.BlockSpec((tr, cc), lambda i: (i, 0))],
        out_specs=pl.BlockSpec((N_CHIP, tr, c), lambda i: (0, i, 0)), out_shape=jax.ShapeDtypeStruct((N_CHIP, r, c), w.dtype),
        compiler_params=_params("parallel"), name=name,
    )(w)


def _from_col_slots(w, name):
    n, r, c = w.shape
    tr = _tile(r, (256,))

    def body(w_ref, o_ref):
        for j in range(n):
            o_ref[:, j * c:(j + 1) * c] = w_ref[j]

    return pl.pallas_call(
        body, grid=(r // tr,), in_specs=[pl.BlockSpec((n, tr, c), lambda i: (0, i, 0))],
        out_specs=pl.BlockSpec((tr, n * c), lambda i: (i, 0)), out_shape=jax.ShapeDtypeStruct((r, n * c), w.dtype),
        compiler_params=_params("parallel"), name=name,
    )(w)


def _block_rows_to_slots(w):
    g, r4, cc = w.shape
    return jnp.swapaxes(w.reshape(g, N_CHIP, r4 // N_CHIP, cc), 0, 1).reshape(N_CHIP, g * (r4 // N_CHIP), cc)


def _slots_to_block_rows(w, g):
    n, gr, cc = w.shape
    return jnp.swapaxes(w.reshape(n, g, gr // g, cc), 0, 1).reshape(g, n * (gr // g), cc)


def _local_step(x, tgt, mod, w, fetch=None, done=None, later=None):
    depth = mod.shape[0]
    row = lambda v: v.reshape(1, -1)
    w = dict(w)
    w["ffn_w_up"], w["ffn_w_down"] = dict(enumerate(w["ffn_w_up"])), dict(enumerate(w["ffn_w_down"]))

    def arrive(stage, after):
        if fetch is not None:
            for k, v in fetch(stage, after).items():
                if isinstance(v, dict):
                    w[k].update(v)
                else:
                    w[k] = v

    saved = []
    for i in range(depth):
        if i == 1:
            arrive("gla", x)
        sh_m, sc_m, gt_m, sh_f, sc_f, gt_f = (mod[i, j:j + 1] for j in range(6))
        g0, g1, g2, g3 = (w["norm_g"][i, j:j + 1] for j in range(4))
        tag = f"_l{i}"
        if i == 0:
            h = _norm_mod_fwd(x, g0, sc_m, sh_m, "norm_mix" + tag)
        if i % 2 == 0:
            pj = _mm(h, w["rg_w_in"], w_slots=N_CHIP, name="rg_in" + tag)
            act, aux = _rg_mid_fwd(pj, w["rg_conv_w"], row(w["rg_conv_b"]), w["rg_wa"], row(w["rg_ba"]), w["rg_wx"],
                                   row(w["rg_bx"]), row(w["rg_lambda"]), "rg_mid" + tag)
            y = _mm(act, w["rg_w_out"], name="rg_out" + tag)
        else:
            pj = _mm(h, w["gla_w_in"], tm_max=512, name="gla_in" + tag)
            act, *aux = _gla_mid_fwd(pj, w["gla_w_alpha"], row(w["gla_b_alpha"]), row(w["gla_norm_g"]), "gla_mid" + tag)
            y = _mm(act, w["gla_w_out"], name="gla_out" + tag)
        x1, h2 = _post_norm_fwd(x, y, g1, gt_m, g2, sc_f, sh_f, "post_mix" + tag)
        arrive(f"ffn{i}", x1)
        p = _mm(h2, w["ffn_w_up"][i], w_slots=N_CHIP, name="ffn_up" + tag)
        a, ga, gb = _ffn_mid_fwd(p, w["ffn_conv_w"][i], w["ffn_conv_b"][i:i + 1], "ffn_mid" + tag)
        y2 = _mm(a, w["ffn_w_down"][i], name="ffn_down" + tag)
        saved_h = h
        if i + 1 < depth:
            nxt = [mod[i + 1, j:j + 1] for j in range(2)] + [w["norm_g"][i + 1, 0:1]]
            x2, h = _post_norm_fwd(x1, y2, g3, gt_f, nxt[2], nxt[1], nxt[0], "post_ffn" + tag)
        else:
            x2 = None
            cols, dx = _post_loss(x1, y2, g3, gt_f, tgt, "post_ffn_loss")
        saved.append((x, saved_h, pj, act, aux, y, x1, h2, p, (a, ga, gb), y2))
        x = x2

    stacked = ("norm_g", "ffn_conv_w", "ffn_conv_b", "mod")
    gr = {k: [None] * depth for k in stacked + ("ffn_w_up", "ffn_w_down")}
    told = lambda stage: done(stage, gr) if done is not None else 0.0
    told_later = lambda stage, after: later(stage, after) if later is not None else 0.0
    for i in reversed(range(depth)):
        x0, h, pj, act, aux, y, x1, h2, p, (a, ga, gb), y2 = saved[i]
        sh_m, sc_m, gt_m, sh_f, sc_f, gt_f = (mod[i, j:j + 1] for j in range(6))
        g0, g1, g2, g3 = (w["norm_g"][i, j:j + 1] for j in range(4))
        tag = f"_l{i}"
        if i == depth - 1:
            dy2, d_g3, d_gt_f = _post_bwd(dx, y2, g3, gt_f, "post_ffn_b" + tag)
        else:
            dy2, d_g3, d_gt_f = ahead
        da = _mm(dy2, w["ffn_w_down"][i], tb=True, name="ffn_down_dx" + tag)
        gr["ffn_w_down"][i] = _mm(a, dy2, ta=True, name="ffn_down_dw" + tag)
        conv_w = w["ffn_conv_w"][i] + (told_later("l1", da) if i == 0 else 0.0)
        dp, dcw, dcb = _ffn_mid_bwd(da, p, ga, gb, conv_w, "ffn_mid_b" + tag)
        gr["ffn_conv_w"][i], gr["ffn_conv_b"][i] = _cat(dcw[0], dcw[1]), _cat(dcb[0], dcb[1])[0]
        dh2 = _mm(dp, w["ffn_w_up"][i], tb=True, a_parts=2, w_slots=N_CHIP, name="ffn_up_dx" + tag)
        gr["ffn_w_up"][i] = _mm(h2, dp, ta=True, b_parts=2, out_slots=N_CHIP, name="ffn_up_dw" + tag)
        if i == 0:
            gt_m = gt_m + told("ffn0")
        dx1, dy, d_g2, d_sc_f, d_sh_f, d_g1, d_gt_m = _norm_post_bwd(dh2, x1, g2, sc_f, dx, y, g1, gt_m, "norm_ffn_b" + tag)
        if i % 2 == 0:
            dact = _mm(dy, w["rg_w_out"], tb=True, name="rg_out_dx" + tag)
            gr["rg_w_out"] = _mm(act, dy, ta=True, name="rg_out_dw" + tag)
            lam = row(w["rg_lambda"]) + told_later("ffn0", gr["rg_w_out"])
            dpj, gr["rg_conv_w"], d_cb, gr["rg_wa"], d_ba, gr["rg_wx"], d_bx, d_lam = _rg_mid_bwd(
                dact, pj, aux, w["rg_conv_w"], row(w["rg_conv_b"]), w["rg_wa"], row(w["rg_ba"]), w["rg_wx"],
                row(w["rg_bx"]), lam, "rg_mid_b" + tag)
            gr["rg_conv_b"], gr["rg_ba"], gr["rg_bx"], gr["rg_lambda"] = d_cb[0], d_ba[0], d_bx[0], d_lam[0]
            dh = _mm(dpj, w["rg_w_in"], tb=True, a_parts=2, w_slots=N_CHIP, name="rg_in_dx" + tag)
            gr["rg_w_in"] = _mm(h, dpj, ta=True, b_parts=2, out_slots=N_CHIP, name="rg_in_dw" + tag)
            sc_m = sc_m + told("rg")
        else:
            dact = _mm(dy, w["gla_w_out"], tb=True, name="gla_out_dx" + tag)
            gr["gla_w_out"] = _mm(act, dy, ta=True, name="gla_out_dw" + tag)
            dpj, gr["gla_w_alpha"], d_bal, d_ng = _gla_mid_bwd(dact, pj, aux[0], aux[1], w["gla_w_alpha"], row(w["gla_b_alpha"]),
                                                               row(w["gla_norm_g"]), "gla_mid_b" + tag)
            gr["gla_b_alpha"], gr["gla_norm_g"] = d_bal[0], d_ng[0]
            dh = _mm(dpj, w["gla_w_in"], tb=True, name="gla_in_dx" + tag)
            gr["gla_w_in"] = _mm(h, dpj, ta=True, tm_max=512, name="gla_in_dw" + tag)
            mod = mod.at[0].add(told("l1"))
        if i > 0:
            dx, dy_below, d_g0, d_sc_m, d_sh_m, d_g_below, d_gt_below = _norm_post_bwd(
                dh, x0, g0, sc_m, dx1, saved[i - 1][-1], w["norm_g"][i - 1, 3:4], mod[i - 1, 5:6], "norm_mix_b" + tag)
            ahead = (dy_below, d_g_below, d_gt_below)
        else:
            dx, d_g0, d_sc_m, d_sh_m = _norm_mod_bwd(dh, x0, g0, sc_m, dx1, "norm_mix_b" + tag)
        gr["norm_g"][i] = jnp.concatenate([d_g0, d_g1, d_g2, d_g3], axis=0)
        gr["mod"][i] = jnp.concatenate([d_sh_m, d_sc_m, d_gt_m, d_sh_f, d_sc_f, d_gt_f], axis=0)
    for k in stacked:
        gr[k] = jnp.stack(gr[k])
    return cols, dx, gr


ADA_ROWS = 16


def _ada_fwd(c16, ada_w, ada_b, name):
    depth, d, n = ada_w.shape
    tn = _tile(n, (512, 256, 128))

    def body(c_ref, w_ref, b_ref, o_ref):
        cv = c_ref[...]
        o_ref[0] = _dot_nn(cv * _sigmoid(cv), w_ref[0]) + b_ref[0]

    return pl.pallas_call(
        body, grid=(depth, n // tn),
        in_specs=[pl.BlockSpec((ADA_ROWS, d), lambda l, j: (0, 0)), pl.BlockSpec((1, d, tn), lambda l, j: (l, 0, j)),
                  pl.BlockSpec((1, 1, tn), lambda l, j: (l, 0, j))],
        out_specs=pl.BlockSpec((1, ADA_ROWS, tn), lambda l, j: (l, 0, j)),
        out_shape=jax.ShapeDtypeStruct((depth, ADA_ROWS, n), F32),
        compiler_params=_params("parallel", "parallel"), name=name,
    )(c16, ada_w, ada_b)


def _ada_bwd(c16, dmod16, name):
    depth, _, n = dmod16.shape
    d = c16.shape[1]
    tn = _tile(n, (512, 256, 128))

    def body(c_ref, dm_ref, o_ref):
        cv = c_ref[...]
        o_ref[0] = _dot_tn(cv * _sigmoid(cv), dm_ref[0])

    return pl.pallas_call(
        body, grid=(depth, n // tn),
        in_specs=[pl.BlockSpec((ADA_ROWS, d), lambda l, j: (0, 0)), pl.BlockSpec((1, ADA_ROWS, tn), lambda l, j: (l, 0, j))],
        out_specs=pl.BlockSpec((1, d, tn), lambda l, j: (l, 0, j)),
        out_shape=jax.ShapeDtypeStruct((depth, d, n), F32),
        compiler_params=_params("parallel", "parallel"), name=name,
    )(c16, dmod16)


PACK_COLS = 1024
_ANY = pl.BlockSpec(memory_space=pl.ANY)
_VMEM = pl.BlockSpec(memory_space=pltpu.VMEM)


def _place():
    return lax.axis_index("x"), lax.axis_index("y"), lax.axis_index("c")


def _other_chips(x, y):
    return [(1 - x, y), (x, 1 - y), (1 - x, 1 - y)]


def _rcopy(src, dst, send_sems, recv_sems, k, peer):
    return pltpu.make_async_remote_copy(src_ref=src, dst_ref=dst, send_sem=send_sems.at[k], recv_sem=recv_sems.at[k],
                                        device_id=peer, device_id_type=MESH)


def _all_gather_8(v, name):
    r, cc = v.shape

    def body(v_ref, out_ref, send_sems, recv_sems, local_sem):
        x, y, c = _place()
        me = 4 * x + 2 * y + c
        mine = pltpu.make_async_copy(v_ref, out_ref.at[me], local_sem)
        mine.start()
        peers = []
        for k in range(1, N_DEV):
            px = 1 - x if k & 4 else x
            py = 1 - y if k & 2 else y
            pc = 1 - c if k & 1 else c
            peers.append((px, py, pc))
        sends = [_rcopy(v_ref, out_ref.at[me], send_sems, recv_sems, k, p) for k, p in enumerate(peers)]
        for cp in sends:
            cp.start()
        for k, (px, py, pc) in enumerate(peers):
            _rcopy(v_ref, out_ref.at[4 * px + 2 * py + pc], send_sems, recv_sems, k, (px, py, pc)).wait_recv()
        for cp in sends:
            cp.wait_send()
        mine.wait()

    return pl.pallas_call(
        body, in_specs=[_VMEM], out_specs=_VMEM, out_shape=jax.ShapeDtypeStruct((N_DEV, r, cc), v.dtype),
        scratch_shapes=[pltpu.SemaphoreType.DMA((N_DEV - 1,)), pltpu.SemaphoreType.DMA((N_DEV - 1,)), pltpu.SemaphoreType.DMA],
        compiler_params=pltpu.CompilerParams(vmem_limit_bytes=VMEM_LIMIT), name=name,
    )(v)


def _gather_chips(shards, name):
    n = len(shards)
    per = 2 * (N_CHIP - 1)

    def body(*refs):
        ins, outs, (send_sems, recv_sems) = refs[:n], refs[n:2 * n], refs[2 * n:]
        x, y, c = _place()
        chip = 2 * x + y
        chips = _other_chips(x, y)
        rows = [(pl.ds(c * (r.shape[0] // 2), r.shape[0] // 2), pl.ds((1 - c) * (r.shape[0] // 2), r.shape[0] // 2)) for r in ins]
        first = [_rcopy(ins[i].at[rows[i][0]], outs[i].at[chip, rows[i][0]], send_sems, recv_sems, per * i + j, (px, py, c))
                 for i in range(n) for j, (px, py) in enumerate(chips)]
        for cp in first:
            cp.start()
        passed = []
        for i in range(n):
            for j, (px, py) in enumerate(chips):
                landed = outs[i].at[2 * px + py, rows[i][0]]
                _rcopy(ins[i].at[rows[i][0]], landed, send_sems, recv_sems, per * i + j, (px, py, c)).wait_recv()
                fw = _rcopy(landed, landed, send_sems, recv_sems, per * i + N_CHIP - 1 + j, (x, y, 1 - c))
                fw.start()
                passed.append(fw)
        for i in range(n):
            for j, (px, py) in enumerate(chips):
                landed = outs[i].at[2 * px + py, rows[i][1]]
                _rcopy(landed, landed, send_sems, recv_sems, per * i + N_CHIP - 1 + j, (x, y, 1 - c)).wait_recv()
        for cp in first + passed:
            cp.wait_send()

    return pl.pallas_call(
        body, in_specs=[_ANY] * n, out_specs=[_ANY] * n,
        out_shape=[jax.ShapeDtypeStruct((N_CHIP,) + sh.shape, sh.dtype) for sh in shards],
        scratch_shapes=[pltpu.SemaphoreType.DMA((per * n,)), pltpu.SemaphoreType.DMA((per * n,))], name=name,
    )(*shards)


_ROW_TILES = (640, 512, 352, 256, 128, 64, 32, 16)


def _pair_sum(g, other, c_idx, name):
    n, half, cc = other.shape
    tr = _tile(half, _ROW_TILES)

    def body(c_ref, g_ref, o_ref, out_ref):
        out_ref[...] = (g_ref[...] + o_ref[...]).astype(out_ref.dtype)

    return pl.pallas_call(
        body,
        grid_spec=pltpu.PrefetchScalarGridSpec(
            num_scalar_prefetch=1, grid=(n, half // tr),
            in_specs=[pl.BlockSpec((None, None, tr, cc), lambda k, i, c_ref: (k, c_ref[0], i, 0)),
                      pl.BlockSpec((None, tr, cc), lambda k, i, c_ref: (k, i, 0))],
            out_specs=pl.BlockSpec((None, tr, cc), lambda k, i, c_ref: (k, i, 0))),
        out_shape=jax.ShapeDtypeStruct((n, half, cc), BF16),
        compiler_params=_params("parallel", "parallel"), name=name,
    )(c_idx, g.reshape(n, 2, half, cc), other)


_HBM = pl.BlockSpec(memory_space=pltpu.HBM)
_SEM = pl.BlockSpec(memory_space=pltpu.SEMAPHORE)
_DATAFLOW = pltpu.SideEffectType.DATAFLOW_SIDE_EFFECTING


def _split_copies(srcs, lands, send_sems, recv_sems, mode, arriving):
    x, y, c = _place()
    chip = 2 * x + y
    out = []
    for i, (src, land) in enumerate(zip(srcs, lands)):
        if mode == "all":
            for k in range(1, N_DEV):
                px, py, pc = (1 - x if k & 4 else x), (1 - y if k & 2 else y), (1 - c if k & 1 else c)
                slot = 4 * px + 2 * py + pc if arriving else 2 * chip + c
                out.append(_rcopy(src, land.at[slot], send_sems, recv_sems, (N_DEV - 1) * i + k - 1, (px, py, pc)))
            continue
        if mode == "pair":
            half = src.shape[1] // 2
            out.append(_rcopy(src.at[:, pl.ds((1 - c) * half, half)], land, send_sems, recv_sems, i, (x, y, 1 - c)))
            continue
        for j, (px, py) in enumerate(_other_chips(x, y)):
            there = 2 * px + py
            part = src.at[there] if mode == "slots" else src
            out.append(_rcopy(part, land.at[there if arriving else chip], send_sems, recv_sems, (N_CHIP - 1) * i + j, (px, py, c)))
    return out


def _land_shape(src, mode):
    if mode == "pair":
        return (src.shape[0], src.shape[1] // 2, src.shape[2])
    if mode == "all":
        return (N_DEV,) + src.shape
    return (N_CHIP,) + (src.shape[1:] if mode == "slots" else src.shape)


def _send_start(srcs, mode, name):
    n = len(srcs)
    n_sem = {"pair": 1, "all": N_DEV - 1}.get(mode, N_CHIP - 1) * n
    lands = [lax.empty(_land_shape(s, mode), s.dtype) for s in srcs]

    def body(*refs):
        ins, zones, (send_sems, recv_sems) = refs[:n], refs[n:2 * n], refs[2 * n:2 * n + 2]
        for cp in _split_copies(ins, zones, send_sems, recv_sems, mode, False):
            cp.start()
        refs[-1][...] = jnp.zeros_like(refs[-1])

    hbm = lambda a: pltpu.HBM(a.shape, a.dtype)
    outs = pl.pallas_call(
        body, name=name, in_specs=[_HBM] * (2 * n),
        out_shape=(pltpu.SemaphoreType.DMA((n_sem,)), pltpu.SemaphoreType.DMA((n_sem,)), *[hbm(a) for a in srcs],
                   *[hbm(a) for a in lands], jax.ShapeDtypeStruct((SUBLANES, LANES), F32)),
        out_specs=(_SEM, _SEM, *[_HBM] * (2 * n), _VMEM), input_output_aliases={i: 2 + i for i in range(2 * n)},
        compiler_params=pltpu.CompilerParams(has_side_effects=_DATAFLOW),
    )(*[pltpu.with_memory_space_constraint(a, pltpu.HBM) for a in list(srcs) + lands])
    return (outs[0], outs[1], list(outs[2:2 + n]), list(outs[2 + n:2 + 2 * n])), outs[-1]


def _send_wait(state, after, mode, name):
    send_sems, recv_sems, srcs, lands = state
    n = len(srcs)

    def body(*refs):
        ins, zones, (send_s, recv_s) = refs[:n], refs[n:2 * n], refs[2 * n:2 * n + 2]
        for cp in _split_copies(ins, zones, send_s, recv_s, mode, True):
            cp.wait_send()
            cp.wait_recv()

    hbm = lambda a: pltpu.HBM(a.shape, a.dtype)
    outs = pl.pallas_call(
        body, name=name, in_specs=[_HBM] * (2 * n) + [_SEM, _SEM, _ANY],
        out_shape=tuple(hbm(a) for a in srcs + lands), out_specs=tuple([_HBM] * (2 * n)),
        input_output_aliases={i: i for i in range(2 * n)},
        compiler_params=pltpu.CompilerParams(has_side_effects=_DATAFLOW),
    )(*srcs, *lands, send_sems, recv_sems, after)
    return list(outs[:n]), list(outs[n:])


def _sum_lead(v, name):
    n, r, cc = v.shape
    tr = _tile(r, _ROW_TILES + (8,))

    def body(v_ref, o_ref):
        acc = v_ref[0].astype(F32)
        for k in range(1, n):
            acc = acc + v_ref[k].astype(F32)
        o_ref[...] = acc

    return pl.pallas_call(
        body, grid=(r // tr,), in_specs=[pl.BlockSpec((n, tr, cc), lambda i: (0, i, 0))],
        out_specs=pl.BlockSpec((tr, cc), lambda i: (i, 0)), out_shape=jax.ShapeDtypeStruct((r, cc), F32),
        compiler_params=_params("parallel"), name=name,
    )(v)


def _chip_sum(arrived, mine, chip_idx, name):
    n, r, cc = arrived.shape
    tr = _tile(r, _ROW_TILES)

    def body(chip_ref, a_ref, m_ref, o_ref):
        acc = jnp.zeros((tr, cc), F32)
        for k in range(n):
            acc = acc + jnp.where(chip_ref[0] == k, m_ref[...], a_ref[k]).astype(F32)
        o_ref[...] = acc

    return pl.pallas_call(
        body,
        grid_spec=pltpu.PrefetchScalarGridSpec(
            num_scalar_prefetch=1, grid=(r // tr,),
            in_specs=[pl.BlockSpec((n, tr, cc), lambda i, chip_ref: (0, i, 0)),
                      pl.BlockSpec((None, tr, cc), lambda i, chip_ref: (chip_ref[0], i, 0))],
            out_specs=pl.BlockSpec((tr, cc), lambda i, chip_ref: (i, 0))),
        out_shape=jax.ShapeDtypeStruct((r, cc), F32), compiler_params=_params("parallel"), name=name,
    )(chip_idx, arrived, mine)


def _pair_share(reds, name):
    n = len(reds)

    def body(*refs):
        ins, outs, (send_sems, recv_sems) = refs[:n], refs[n:2 * n], refs[2 * n:]
        x, y, c = _place()
        copies = [_rcopy(ins[i], outs[i].at[c], send_sems, recv_sems, i, (x, y, 1 - c)) for i in range(n)]
        for cp in copies:
            cp.start()
        for i in range(n):
            _rcopy(ins[i], outs[i].at[1 - c], send_sems, recv_sems, i, (x, y, 1 - c)).wait_recv()
        for cp in copies:
            cp.wait_send()

    return pl.pallas_call(
        body, in_specs=[_ANY] * n, out_specs=[_ANY] * n, out_shape=[jax.ShapeDtypeStruct((2,) + r.shape, r.dtype) for r in reds],
        scratch_shapes=[pltpu.SemaphoreType.DMA((n,)), pltpu.SemaphoreType.DMA((n,))], name=name,
    )(*reds)


def _pack(arrs, rows_multiple, dtype):
    flat = jnp.concatenate([a.reshape(-1).astype(dtype) for a in arrs])
    unit = rows_multiple * PACK_COLS
    total = -(-flat.shape[0] // unit) * unit
    return jnp.pad(flat, (0, total - flat.shape[0])).reshape(-1, PACK_COLS)


def _unpack(buf, shapes):
    lead = buf.shape[:-2]
    flat = buf.reshape(*lead, -1)
    out, off = [], 0
    for shp in shapes:
        n = 1
        for s in shp:
            n *= s
        out.append(flat[..., off:off + n].reshape(*lead, *shp))
        off += n
    return out


def _join_shards(parts, axis):
    moved = jnp.moveaxis(parts, 0, axis)
    shp = list(moved.shape)
    shp[axis:axis + 2] = [shp[axis] * shp[axis + 1]]
    return moved.reshape(shp)


def _my_shard(full, axis, chip):
    n = full.shape[axis] // N_CHIP
    return lax.dynamic_slice_in_dim(full, chip * n, n, axis)


SMALL = {"norm_g": 2, "ffn_conv_w": 2, "rg_conv_w": 2, "gla_w_alpha": 2, "gla_b_alpha": 1, "gla_norm_g": 1,
         "ada_b": None, "ffn_conv_b": None, "rg_conv_b": None, "rg_ba": None, "rg_bx": None, "rg_lambda": None}
BIG = {"rg_w_in": True, "rg_wa": False, "rg_wx": False, "rg_w_out": False, "ffn_w_up": True, "ffn_w_down": False,
       "gla_w_in": True, "gla_w_out": False}
WEIGHTS = ["ada_w", "ada_b", "norm_g", "ffn_w_up", "ffn_conv_w", "ffn_conv_b", "ffn_w_down", "rg_w_in", "rg_conv_w", "rg_conv_b",
           "rg_wa", "rg_ba", "rg_wx", "rg_bx", "rg_lambda", "rg_w_out", "gla_w_in", "gla_w_alpha", "gla_b_alpha", "gla_norm_g",
           "gla_w_out"]


def kernel(x, c, ada_w, ada_b, norm_g, ffn_w_up, ffn_conv_w, ffn_conv_b, ffn_w_down, rg_w_in, rg_conv_w, rg_conv_b, rg_wa, rg_ba, rg_wx, rg_bx, rg_lambda, rg_w_out, gla_w_in, gla_w_alpha, gla_b_alpha, gla_norm_g, gla_w_out, loss_target, m_ada_w, m_ada_b, m_norm_g, m_ffn_w_up, m_ffn_conv_w, m_ffn_conv_b, m_ffn_w_down, m_rg_w_in, m_rg_conv_w, m_rg_conv_b, m_rg_wa, m_rg_ba, m_rg_wx, m_rg_bx, m_rg_lambda, m_rg_w_out, m_gla_w_in, m_gla_w_alpha, m_gla_b_alpha, m_gla_norm_g, m_gla_w_out, v_ada_w, v_ada_b, v_norm_g, v_ffn_w_up, v_ffn_conv_w, v_ffn_conv_b, v_ffn_w_down, v_rg_w_in, v_rg_conv_w, v_rg_conv_b, v_rg_wa, v_rg_ba, v_rg_wx, v_rg_bx, v_rg_lambda, v_rg_w_out, v_gla_w_in, v_gla_w_alpha, v_gla_b_alpha, v_gla_norm_g, v_gla_w_out):
    wts = dict(ada_w=ada_w, ada_b=ada_b, norm_g=norm_g, ffn_w_up=ffn_w_up, ffn_conv_w=ffn_conv_w, ffn_conv_b=ffn_conv_b,
               ffn_w_down=ffn_w_down, rg_w_in=rg_w_in, rg_conv_w=rg_conv_w, rg_conv_b=rg_conv_b, rg_wa=rg_wa, rg_ba=rg_ba,
               rg_wx=rg_wx, rg_bx=rg_bx, rg_lambda=rg_lambda, rg_w_out=rg_w_out, gla_w_in=gla_w_in, gla_w_alpha=gla_w_alpha,
               gla_b_alpha=gla_b_alpha, gla_norm_g=gla_norm_g, gla_w_out=gla_w_out)
    mom1 = dict(ada_w=m_ada_w, ada_b=m_ada_b, norm_g=m_norm_g, ffn_w_up=m_ffn_w_up, ffn_conv_w=m_ffn_conv_w,
                ffn_conv_b=m_ffn_conv_b, ffn_w_down=m_ffn_w_down, rg_w_in=m_rg_w_in, rg_conv_w=m_rg_conv_w,
                rg_conv_b=m_rg_conv_b, rg_wa=m_rg_wa, rg_ba=m_rg_ba, rg_wx=m_rg_wx, rg_bx=m_rg_bx, rg_lambda=m_rg_lambda,
                rg_w_out=m_rg_w_out, gla_w_in=m_gla_w_in, gla_w_alpha=m_gla_w_alpha, gla_b_alpha=m_gla_b_alpha,
                gla_norm_g=m_gla_norm_g, gla_w_out=m_gla_w_out)
    mom2 = dict(ada_w=v_ada_w, ada_b=v_ada_b, norm_g=v_norm_g, ffn_w_up=v_ffn_w_up, ffn_conv_w=v_ffn_conv_w,
                ffn_conv_b=v_ffn_conv_b, ffn_w_down=v_ffn_w_down, rg_w_in=v_rg_w_in, rg_conv_w=v_rg_conv_w,
                rg_conv_b=v_rg_conv_b, rg_wa=v_rg_wa, rg_ba=v_rg_ba, rg_wx=v_rg_wx, rg_bx=v_rg_bx, rg_lambda=v_rg_lambda,
                rg_w_out=v_rg_w_out, gla_w_in=v_gla_w_in, gla_w_alpha=v_gla_w_alpha, gla_b_alpha=v_gla_b_alpha,
                gla_norm_g=v_gla_norm_g, gla_w_out=v_gla_w_out)
    xi, yi, ci = _place()
    chip, me = 2 * xi + yi, 4 * xi + 2 * yi + ci
    d = x.shape[-1]
    depth = ada_w.shape[0]
    n_ada = ada_w.shape[-1]
    sharded_small = [k for k, ax in SMALL.items() if ax is not None]

    sm = _all_gather_8(_pack([c] + [wts[k] for k in sharded_small], SUBLANES, F32), "gather_small")
    c_all = sm[:, 0, :]
    parts = _unpack(sm[0::2], [c.shape] + [wts[k].shape for k in sharded_small])[1:]
    full = {k: _join_shards(p, SMALL[k]) for k, p in zip(sharded_small, parts)}
    for k, ax in SMALL.items():
        if ax is None:
            full[k] = wts[k]

    c16 = jnp.pad(c_all, ((0, ADA_ROWS - N_DEV), (0, 0)))
    ada_b_mine = lax.dynamic_slice_in_dim(ada_b, chip * n_ada, n_ada, 1)[:, None, :]
    mod_cols = _ada_fwd(c16, ada_w, ada_b_mine, "ada_fwd")
    mod_all = _all_gather_8(mod_cols[:, :N_DEV].reshape(-1, PACK_COLS), "gather_mod")[0::2].reshape(N_CHIP, depth, N_DEV, n_ada)
    mod = jnp.swapaxes(lax.dynamic_index_in_dim(mod_all, me, 2, keepdims=False), 0, 1).reshape(depth, 6, d)

    items = [(k, l) for k in BIG for l in range(wts[k].shape[0])]
    stage_of = lambda k, l: "rg" if k.startswith("rg_") else ("ffn0" if (k.startswith("ffn_") and l == 0) else "l1")
    staged = {st: [it for it in items if stage_of(*it) == st] for st in ("rg", "ffn0", "l1")}
    staged["gla"] = [it for it in staged["l1"] if it[0].startswith("gla_")]
    staged["ffn1"] = [it for it in staged["l1"] if it[0].startswith("ffn_")]
    staged["l1"] = staged["gla"] + staged["ffn1"]
    shard = lambda k, l: wts[k][l].reshape(-1, wts[k].shape[-1]).astype(BF16)
    own = lambda got, mine: [lax.dynamic_update_index_in_dim(g, m, chip, 0) for g, m in zip(got, mine)]
    rows_joined = lambda v: v.reshape(-1, v.shape[-1])

    def placed(its, slots):
        out = {"ffn_w_up": {}, "ffn_w_down": {}}
        for (k, l), v in zip(its, slots):
            if k == "ffn_w_up":
                out[k][l] = v
            elif k == "ffn_w_down":
                out[k][l] = rows_joined(v)
            elif k in ("rg_wa", "rg_wx"):
                out[k] = _slots_to_block_rows(v, RG_BLOCKS)
            elif k == "gla_w_in":
                out[k] = _from_col_slots(v, "gla_w_in_join")
            else:
                out[k] = v if BIG[k] else rows_joined(v)
        return out

    after_mod = (mod[0, 0, 0] * 0.0).astype(BF16)
    sh_rg = [shard(k, l) + after_mod for k, l in staged["rg"]]
    local = {k: (v if k in ("norm_g", "ffn_conv_w", "ffn_conv_b") else v[0]) for k, v in full.items()}
    local.update(placed(staged["rg"], own(_gather_chips(sh_rg, "gather_weights_rg"), sh_rg)))
    sh_late, flying = {}, {}
    after_rg = (local["rg_w_out"][0, 0].astype(F32) * 0.0).astype(BF16)
    sh_late["ffn0"] = [shard(k, l) + after_rg for k, l in staged["ffn0"]]
    flying["ffn0"], tok = _send_start(sh_late["ffn0"], "whole", "weights_ffn0_start")
    for stage in ("gla", "ffn1"):
        sh_late[stage] = [shard(k, l) + tok[0, 0].astype(BF16) for k, l in staged[stage]]
        flying[stage], tok = _send_start(sh_late[stage], "whole", f"weights_{stage}_start")
    mod = mod + tok[0, 0]

    def fetch(stage, after):
        mine, got = _send_wait(flying[stage], after, "whole", f"weights_{stage}_wait")
        return placed(staged[stage], own(got, mine))

    c_idx = ci.reshape(1).astype(jnp.int32)
    gslots, paired, psums, sent, started = {}, {}, {}, {}, {}

    def grad_slots(gr, k, l):
        g = gr[k][l] if k in ("ffn_w_up", "ffn_w_down") else gr[k]
        if k in ("rg_wa", "rg_wx"):
            return _block_rows_to_slots(g)
        if k == "gla_w_in":
            return _col_slots(g, "gla_w_in_grad_slots")
        return g if BIG[k] else g.reshape(N_CHIP, -1, g.shape[-1])

    def done(stage, gr):
        gslots[stage] = [grad_slots(gr, k, l) for k, l in staged[stage]]
        paired[stage], token = _send_start(gslots[stage], "pair", f"grads_{stage}_pair_start")
        return token[0, 0]

    def later(stage, after):
        mine, theirs = _send_wait(paired[stage], after, "pair", f"grads_{stage}_pair_wait")
        psums[stage] = [_pair_sum(g, t, c_idx, f"grads_pair_sum_{k}{l}") for (k, l), g, t in zip(staged[stage], mine, theirs)]
        sent[stage], started[stage] = _send_start(psums[stage], "slots", f"grads_{stage}_start")
        return started[stage][0, 0]

    cols, grad_x, gr = _local_step(x[0], loss_target[0], mod, local, fetch, done, later)
    loss_mine = (0.5 * jnp.sum(cols) / d).reshape(1)

    small_names = [k for k in SMALL if k != "ada_b"]
    small_flying, small_sent = _send_start([_pack([gr[k] for k in small_names] + [gr["mod"], loss_mine], SUBLANES, F32)], "all",
                                           "grads_small_start")
    small_shapes = [full[k].shape for k in small_names] + [(depth, 6 * d), (1,)]
    chip_idx = chip.reshape(1).astype(jnp.int32)
    delta, new_m, new_v = {}, {}, {}
    grads = {}

    def reduce_and_update(stages, after, dep):
        its = [(st, n) for st in stages for n in range(len(staged[st]))]
        back = {st: _send_wait(sent[st], after, "slots", f"grads_{st}_wait") for st in stages}
        halves = [_chip_sum(back[st][1][n], back[st][0][n], chip_idx, "grads_chip_sum_%s%d" % staged[st][n]) for st, n in its]
        shared = _pair_share(halves, "grads_pair_share_" + stages[0])
        reduced = [lax.dynamic_update_index_in_dim(s2, h, ci, 0).reshape(-1, h.shape[-1]) for s2, h in zip(shared, halves)]
        last = None
        for k in BIG:
            gs_k = [g for (st, n), g in zip(its, reduced) if staged[st][n][0] == k]
            if gs_k:
                last = update(k, gs_k, dep)
        return last

    def update(k, gs_k, dep=None):
        shp = wts[k].shape
        if k == "gla_w_in":
            view, back = (lambda a: jnp.swapaxes(a, 1, 2)), (lambda o: jnp.swapaxes(o, 1, 2))
            gs_k = [g.T for g in gs_k]
        else:
            view, back = (lambda a: a.reshape(a.shape[0], -1, a.shape[-1])), (lambda o: o.reshape(shp))
        outs = _adamw(view(wts[k]), gs_k, view(mom1[k]), view(mom2[k]), "adamw_" + k, dep)
        grads[k], delta[k], new_m[k], new_v[k] = (back(o) for o in outs)
        return new_v[k]

    later("rg", small_sent)
    done_late = reduce_and_update(("ffn0", "l1"), grad_x, started["rg"])
    (small_mine,), (gs,) = _send_wait(small_flying, done_late, "all", "grads_small_wait")
    gs = lax.dynamic_update_index_in_dim(gs, small_mine, 2 * chip + ci, 0)
    *small_sum, g_ada_b, loss = _unpack(_sum_lead(gs, "sum_small_grads"), small_shapes)
    loss = loss[0]
    grads.update(zip(small_names, small_sum))
    grads["ada_b"] = g_ada_b
    for k in sharded_small:
        grads[k] = _my_shard(grads[k], SMALL[k], chip)
    dmod_all = _unpack(gs, small_shapes)[-2].reshape(N_DEV, depth, N_CHIP, n_ada)
    dmod_mine = jnp.swapaxes(lax.dynamic_index_in_dim(dmod_all, chip, 2, keepdims=False), 0, 1)
    g_ada_w = _ada_bwd(c16, jnp.pad(dmod_mine, ((0, 0), (0, ADA_ROWS - N_DEV), (0, 0))), "ada_bwd")
    update("ada_w", g_ada_w)
    small_shard_shapes = [wts[k].shape for k in SMALL]
    packed = [_pack([src[k] for k in SMALL], SUBLANES, F32) for src in (wts, grads, mom1, mom2)]
    outs = _adamw(packed[0][None], [packed[1]], packed[2][None], packed[3][None], "adamw_small")
    for dst, o in zip((delta, new_m, new_v), outs[1:]):
        for k, a in zip(SMALL, _unpack(o[0], small_shard_shapes)):
            dst[k] = a
    reduce_and_update(("rg",), outs[3], None)

    return (loss, grad_x[None], *[grads[k] for k in WEIGHTS], *[delta[k] for k in WEIGHTS], *[new_m[k] for k in WEIGHTS],
            *[new_v[k] for k in WEIGHTS])
```

```python
import jax
import jax.numpy as jnp
from jax import lax
from jax.experimental import pallas as pl
from jax.experimental.pallas import tpu as pltpu

F32 = jnp.float32
BF16 = jnp.bfloat16
MXU_DTYPE = BF16

EPS = 1e-6
RG_C = 8.0
RG_BLOCKS = 4
RG_CONV = 4
GLA_HEADS = 4
GLA_TAU = 16.0
GLA_CHUNK = 64
GLA_RANK = 16
FFN_CONV = 3
ADAM_LR = 0.001
ADAM_B1 = 0.9
ADAM_B2 = 0.999
ADAM_EPS = 1e-08
ADAM_WD = 0.01
ADAM_STEP = 10

LANES = 128
SUBLANES = 8
VMEM_LIMIT = 56 * 1024 * 1024
CB = 256
MESH = pl.DeviceIdType.MESH
N_DEV = 8
N_CHIP = 4


def _params(*sem):
    return pltpu.CompilerParams(dimension_semantics=sem, vmem_limit_bytes=VMEM_LIMIT)


def _tile(dim, prefs):
    for p in prefs:
        if dim % p == 0:
            return p
    return dim


def _dot(a, b, dims):
    return lax.dot_general(a.astype(MXU_DTYPE), b.astype(MXU_DTYPE), (dims, ((), ())), preferred_element_type=F32)


def _dot_nn(a, b):
    return _dot(a, b, ((1,), (0,)))


def _dot_nt(a, b):
    return _dot(a, b, ((1,), (1,)))


def _dot_tn(a, b):
    return _dot(a, b, ((0,), (0,)))


def _mm(a, b, *, ta=False, tb=False, a_parts=1, b_parts=1, w_slots=1, out_slots=1, out_dtype=F32, tm_max=1408, name):
    if ta:
        k_dim, m_dim = a.shape
        n_dim = b.shape[-1] * b_parts
    else:
        m_dim, k_dim = a.shape[-2], a.shape[-1] * a_parts
        n_dim = b.shape[-2] if tb else b.shape[-1] * w_slots
    n_unit = n_dim // max(b_parts, out_slots, 1 if tb else w_slots)
    k_unit = k_dim // max(a_parts, w_slots if tb else 1)
    tm = _tile(m_dim, tuple(t for t in (1024, 1408, 512, 256, 128) if t <= max(tm_max, 128)))
    tn = _tile(n_unit, (1024, 1408, 896, 512, 256, 128))
    tk = _tile(k_unit, (1024, 1408, 896, 512, 256, 128))
    nk = k_dim // tk
    dims = ((0 if ta else 1,), (1 if tb else 0,))

    def spec(shape, parts, total, tile, col_grid, row_grid):
        per = total // parts // tile

        def index(i, j, k):
            g = {"i": i, "j": j, "k": k}
            col, row = g[col_grid], g[row_grid]
            return (row, col) if parts == 1 else (col // per, row, col % per)

        return pl.BlockSpec(shape if parts == 1 else (None,) + shape, index)

    def body(a_ref, b_ref, o_ref, *acc):
        if nk == 1:
            o_ref[...] = _dot(a_ref[...], b_ref[...], dims).astype(o_ref.dtype)
            return
        acc_ref, k = acc[0], pl.program_id(2)

        @pl.when(k == 0)
        def _():
            acc_ref[...] = jnp.zeros_like(acc_ref)

        acc_ref[...] += _dot(a_ref[...], b_ref[...], dims)

        @pl.when(k == nk - 1)
        def _():
            o_ref[...] = acc_ref[...].astype(o_ref.dtype)

    if ta:
        a_spec = spec((tk, tm), 1, m_dim, tm, "i", "k")
        b_spec = spec((tk, tn), b_parts, n_dim, tn, "j", "k")
    elif tb:
        a_spec = spec((tm, tk), a_parts, k_dim, tk, "k", "i")
        b_spec = spec((tn, tk), w_slots, k_dim, tk, "k", "j")
    else:
        a_spec = spec((tm, tk), a_parts, k_dim, tk, "k", "i")
        b_spec = spec((tk, tn), w_slots, n_dim, tn, "j", "k")
    out_shape = (m_dim, n_dim) if out_slots == 1 else (out_slots, m_dim, n_dim // out_slots)
    return pl.pallas_call(
        body,
        grid=(m_dim // tm, n_dim // tn, nk),
        in_specs=[a_spec, b_spec],
        out_specs=spec((tm, tn), out_slots, n_dim, tn, "j", "i"),
        out_shape=jax.ShapeDtypeStruct(out_shape, out_dtype),
        scratch_shapes=[pltpu.VMEM((tm, tn), F32)] if nk > 1 else [],
        compiler_params=_params("parallel", "parallel", "arbitrary"),
        name=name,
    )(a, b)


ROW_TILES = (1024, 512)


def _row_specs(s, d, ts):
    return pl.BlockSpec((ts, d), lambda i: (i, 0)), pl.BlockSpec((1, d), lambda i: (0, 0))


def _norm_mod_fwd(x, g, sc, sh, name):
    s, d = x.shape
    ts = _tile(s, ROW_TILES)

    def body(x_ref, g_ref, sc_ref, sh_ref, h_ref):
        xv = x_ref[...]
        r = lax.rsqrt(jnp.mean(xv * xv, axis=-1, keepdims=True) + EPS)
        h_ref[...] = (((xv * r) * g_ref[...]) * (1.0 + sc_ref[...]) + sh_ref[...]).astype(h_ref.dtype)

    row, vec = _row_specs(s, d, ts)
    return pl.pallas_call(
        body, grid=(s // ts,), in_specs=[row, vec, vec, vec], out_specs=row,
        out_shape=jax.ShapeDtypeStruct((s, d), MXU_DTYPE), compiler_params=_params("parallel"), name=name,
    )(x, g, sc, sh)


def _norm_mod_bwd(dh, x, g, sc, dres, name):
    s, d = x.shape
    ts = _tile(s, ROW_TILES)

    def body(dh_ref, x_ref, g_ref, sc_ref, dres_ref, dx_ref, dg_ref, dsc_ref, dsh_ref, acc_ref):
        i = pl.program_id(0)

        @pl.when(i == 0)
        def _():
            acc_ref[...] = jnp.zeros_like(acc_ref)

        xv, dhv = x_ref[...], dh_ref[...]
        r = lax.rsqrt(jnp.mean(xv * xv, axis=-1, keepdims=True) + EPS)
        n = xv * r
        acc_ref[0:1, :] += jnp.sum(dhv * n, axis=0, keepdims=True)
        acc_ref[1:2, :] += jnp.sum(dhv, axis=0, keepdims=True)
        dn = dhv * ((1.0 + sc_ref[...]) * g_ref[...])
        dx_ref[...] = dres_ref[...] + r * (dn - n * jnp.mean(dn * n, axis=-1, keepdims=True))
        dg_ref[...] = (1.0 + sc_ref[...]) * acc_ref[0:1, :]
        dsc_ref[...] = g_ref[...] * acc_ref[0:1, :]
        dsh_ref[...] = acc_ref[1:2, :]

    row, vec = _row_specs(s, d, ts)
    vshape = jax.ShapeDtypeStruct((1, d), F32)
    return pl.pallas_call(
        body, grid=(s // ts,), in_specs=[row, row, vec, vec, row], out_specs=[row, vec, vec, vec],
        out_shape=[jax.ShapeDtypeStruct((s, d), F32), vshape, vshape, vshape],
        scratch_shapes=[pltpu.VMEM((SUBLANES, d), F32)], compiler_params=_params("arbitrary"), name=name,
    )(dh, x, g, sc, dres)


def _post_norm_fwd(x, y, g, gt, g2, sc, sh, name):
    s, d = x.shape
    ts = _tile(s, ROW_TILES)

    def body(x_ref, y_ref, g_ref, gt_ref, g2_ref, sc_ref, sh_ref, o_ref, h_ref):
        yv = y_ref[...]
        r = lax.rsqrt(jnp.mean(yv * yv, axis=-1, keepdims=True) + EPS)
        xn = x_ref[...] + gt_ref[...] * ((yv * r) * g_ref[...])
        o_ref[...] = xn
        r2 = lax.rsqrt(jnp.mean(xn * xn, axis=-1, keepdims=True) + EPS)
        h_ref[...] = (((xn * r2) * g2_ref[...]) * (1.0 + sc_ref[...]) + sh_ref[...]).astype(h_ref.dtype)

    row, vec = _row_specs(s, d, ts)
    return pl.pallas_call(
        body, grid=(s // ts,), in_specs=[row, row] + [vec] * 5, out_specs=[row, row],
        out_shape=[jax.ShapeDtypeStruct((s, d), F32), jax.ShapeDtypeStruct((s, d), MXU_DTYPE)],
        compiler_params=_params("parallel"), name=name,
    )(x, y, g, gt, g2, sc, sh)


def _post_bwd(dxn, y, g, gt, name):
    s, d = y.shape
    ts = _tile(s, ROW_TILES)

    def body(dxn_ref, y_ref, g_ref, gt_ref, dy_ref, dg_ref, dgt_ref, acc_ref):
        i = pl.program_id(0)

        @pl.when(i == 0)
        def _():
            acc_ref[...] = jnp.zeros_like(acc_ref)

        yv, dv = y_ref[...], dxn_ref[...]
        r = lax.rsqrt(jnp.mean(yv * yv, axis=-1, keepdims=True) + EPS)
        n = yv * r
        acc_ref[0:1, :] += jnp.sum(dv * n, axis=0, keepdims=True)
        dn = dv * (gt_ref[...] * g_ref[...])
        dy_ref[...] = (r * (dn - n * jnp.mean(dn * n, axis=-1, keepdims=True))).astype(dy_ref.dtype)
        dg_ref[...] = gt_ref[...] * acc_ref[0:1, :]
        dgt_ref[...] = g_ref[...] * acc_ref[0:1, :]

    row, vec = _row_specs(s, d, ts)
    vshape = jax.ShapeDtypeStruct((1, d), F32)
    return pl.pallas_call(
        body, grid=(s // ts,), in_specs=[row, row, vec, vec], out_specs=[row, vec, vec],
        out_shape=[jax.ShapeDtypeStruct((s, d), MXU_DTYPE), vshape, vshape],
        scratch_shapes=[pltpu.VMEM((SUBLANES, d), F32)], compiler_params=_params("arbitrary"), name=name,
    )(dxn, y, g, gt)


def _norm_post_bwd(dh, x, g, sc, dres, y, gp, gt, name):
    s, d = x.shape
    ts = _tile(s, (512,))

    def body(dh_ref, x_ref, g_ref, sc_ref, dres_ref, y_ref, gp_ref, gt_ref,
             dx_ref, dy_ref, dg_ref, dsc_ref, dsh_ref, dgp_ref, dgt_ref, acc_ref):
        i = pl.program_id(0)

        @pl.when(i == 0)
        def _():
            acc_ref[...] = jnp.zeros_like(acc_ref)

        xv, dhv = x_ref[...], dh_ref[...]
        r = lax.rsqrt(jnp.mean(xv * xv, axis=-1, keepdims=True) + EPS)
        n = xv * r
        acc_ref[0:1, :] += jnp.sum(dhv * n, axis=0, keepdims=True)
        acc_ref[1:2, :] += jnp.sum(dhv, axis=0, keepdims=True)
        dn = dhv * ((1.0 + sc_ref[...]) * g_ref[...])
        dx = dres_ref[...] + r * (dn - n * jnp.mean(dn * n, axis=-1, keepdims=True))
        dx_ref[...] = dx
        yv = y_ref[...]
        ry = lax.rsqrt(jnp.mean(yv * yv, axis=-1, keepdims=True) + EPS)
        ny = yv * ry
        acc_ref[2:3, :] += jnp.sum(dx * ny, axis=0, keepdims=True)
        dny = dx * (gt_ref[...] * gp_ref[...])
        dy_ref[...] = (ry * (dny - ny * jnp.mean(dny * ny, axis=-1, keepdims=True))).astype(dy_ref.dtype)
        dg_ref[...] = (1.0 + sc_ref[...]) * acc_ref[0:1, :]
        dsc_ref[...] = g_ref[...] * acc_ref[0:1, :]
        dsh_ref[...] = acc_ref[1:2, :]
        dgp_ref[...] = gt_ref[...] * acc_ref[2:3, :]
        dgt_ref[...] = gp_ref[...] * acc_ref[2:3, :]

    row, vec = _row_specs(s, d, ts)
    vshape = jax.ShapeDtypeStruct((1, d), F32)
    return pl.pallas_call(
        body, grid=(s // ts,), in_specs=[row, row, vec, vec, row, row, vec, vec], out_specs=[row, row] + [vec] * 5,
        out_shape=[jax.ShapeDtypeStruct((s, d), F32), jax.ShapeDtypeStruct((s, d), MXU_DTYPE)] + [vshape] * 5,
        scratch_shapes=[pltpu.VMEM((SUBLANES, d), F32)], compiler_params=_params("arbitrary"), name=name,
    )(dh, x, g, sc, dres, y, gp, gt)


def _post_loss(x, y, g, gt, tgt, name):
    s, d = x.shape
    ts = _tile(s, ROW_TILES)

    def body(x_ref, y_ref, g_ref, gt_ref, t_ref, col_ref, dx_ref):
        i = pl.program_id(0)

        @pl.when(i == 0)
        def _():
            col_ref[...] = jnp.zeros_like(col_ref)

        yv = y_ref[...]
        r = lax.rsqrt(jnp.mean(yv * yv, axis=-1, keepdims=True) + EPS)
        e = (x_ref[...] + gt_ref[...] * ((yv * r) * g_ref[...])) - t_ref[...]
        col_ref[...] += jnp.sum(e * e, axis=0, keepdims=True)
        dx_ref[...] = e * (1.0 / d)

    row, vec = _row_specs(s, d, ts)
    return pl.pallas_call(
        body, grid=(s // ts,), in_specs=[row, row, vec, vec, row], out_specs=[vec, row],
        out_shape=[jax.ShapeDtypeStruct((1, d), F32), jax.ShapeDtypeStruct((s, d), F32)],
        compiler_params=_params("arbitrary"), name=name,
    )(x, y, g, gt, tgt)


_GELU_C = 0.7978845608028654
_GELU_A = 0.044715


def _gelu(x):
    t = jnp.tanh(_GELU_C * (x + _GELU_A * x * x * x))
    return 0.5 * x * (1.0 + t), t


def _gelu_grad(x, t):
    return 0.5 * (1.0 + t) + 0.5 * x * (1.0 - t * t) * (_GELU_C * (1.0 + 3.0 * _GELU_A * x * x))


def _sigmoid(x):
    return 1.0 / (1.0 + jnp.exp(-x))


def _log1p_pos(y):
    u = 1.0 + y
    return jnp.where(u == 1.0, y, jnp.log(u) * (y / jnp.where(u == 1.0, 1.0, u - 1.0)))


def _softplus(x):
    return jnp.maximum(x, 0.0) + _log1p_pos(jnp.exp(-jnp.abs(x)))


def _one_minus_sq_exp(x, ex):
    z = 2.0 * x
    series = -z * (1.0 + z * (1.0 / 2 + z * (1.0 / 6 + z * (1.0 / 24 + z * (1.0 / 120)))))
    return jnp.where(z > -0.05, series, 1.0 - ex * ex)


SLAB = 16


def _cat(a, b):
    return jnp.concatenate([a, b], axis=1)


def _pair_specs(shape, nb, index):
    return [pl.BlockSpec(shape, lambda j, t: index(j, t) + (j,)), pl.BlockSpec(shape, lambda j, t: index(j, t) + (j + nb,))]


def _halo_row(ts, time_of):
    return lambda j, t: (jnp.maximum(time_of(t) * (ts // SUBLANES) - 1, 0),)


def _rows_from(groups, k):
    row = lax.broadcasted_iota(jnp.int32, groups[0].shape, 0)
    turned = [pltpu.roll(g, SUBLANES - k, axis=0) for g in groups]
    return [jnp.where(row < SUBLANES - k, lo, hi) for lo, hi in zip(turned[:-1], turned[1:])]


def _ffn_mid_fwd(p, cw, cb, name):
    s, f2 = p.shape
    ts = _tile(s, (2048, 1024, 512))
    nb, nt = f2 // (2 * CB), s // ts
    n_grp = SLAB // SUBLANES

    def body(pg_ref, pv_ref, hg_ref, hv_ref, cwg_ref, cwv_ref, cbg_ref, cbv_ref, a_ref, ga_ref, gb_ref):
        t = pl.program_id(1)
        cwv, bias = _cat(cwg_ref[...], cwv_ref[...]), _cat(cbg_ref[...], cbv_ref[...])
        w0, w1, w2 = cwv[0:1], cwv[1:2], cwv[2:3]

        def slab(before, cur, r0):
            pm2, pm1 = _rows_from([before] + cur, SUBLANES - 2), _rows_from([before] + cur, SUBLANES - 1)
            u = jnp.concatenate([bias + w0 * pm2[i] + w1 * pm1[i] + w2 * cur[i] for i in range(n_grp)], axis=0)
            g, v = u[:, :CB], u[:, CB:]
            gel, th = _gelu(g)
            rows = pl.ds(r0, SLAB)
            a_ref[rows, :] = (gel * v).astype(a_ref.dtype)
            ga_ref[rows, :] = gel.astype(ga_ref.dtype)
            gb_ref[rows, :] = (v * _gelu_grad(g, th)).astype(gb_ref.dtype)

        def pieces(rows):
            blk = _cat(pg_ref[rows, :], pv_ref[rows, :])
            return [blk[i * SUBLANES:(i + 1) * SUBLANES] for i in range(blk.shape[0] // SUBLANES)]

        slab(jnp.where(t > 0, _cat(hg_ref[...], hv_ref[...]), 0.0), pieces(pl.ds(0, SLAB)), 0)

        def loop(i, carry):
            r0 = pl.multiple_of(i * SLAB, SLAB)
            got = pieces(pl.ds(pl.multiple_of(r0 - SUBLANES, SUBLANES), SLAB + SUBLANES))
            slab(got[0], got[1:], r0)
            return carry

        lax.fori_loop(1, ts // SLAB, loop, 0, unroll=2)

    fwd = lambda t: t
    out = pl.BlockSpec((ts, CB), lambda j, t: (t, j))
    shape = jax.ShapeDtypeStruct((s, f2 // 2), MXU_DTYPE)
    return pl.pallas_call(
        body, grid=(nb, nt),
        in_specs=(_pair_specs((ts, CB), nb, lambda j, t: (t,)) + _pair_specs((SUBLANES, CB), nb, _halo_row(ts, fwd))
                  + _pair_specs((FFN_CONV, CB), nb, lambda j, t: (0,)) + _pair_specs((1, CB), nb, lambda j, t: (0,))),
        out_specs=[out, out, out], out_shape=[shape, shape, shape],
        compiler_params=_params("parallel", "arbitrary"), name=name,
    )(p, p, p, p, cw, cw, cb, cb)


def _ffn_mid_bwd(da, p, ga, gb, cw, name):
    s, f2 = p.shape
    ts = _tile(s, (2048, 1024, 512))
    nb, nt = f2 // (2 * CB), s // ts
    n_slab = ts // SLAB
    n_grp = SLAB // SUBLANES
    per_trip = 2

    def body(da_ref, ga_ref, gb_ref, pg_ref, pv_ref, cwg_ref, cwv_ref, dp_ref, dcw_ref, dcb_ref, next_du, acc):
        tt = pl.program_id(1)
        cwv = _cat(cwg_ref[...], cwv_ref[...])
        w0, w1, w2 = cwv[0:1], cwv[1:2], cwv[2:3]

        @pl.when(tt == 0)
        def _():
            next_du[...] = jnp.zeros_like(next_du)
            acc[...] = jnp.zeros_like(acc)

        def slab(r0, after, sums):
            rows = pl.ds(r0, SLAB)
            dav = da_ref[rows, :]
            du = _cat(dav * gb_ref[rows, :].astype(F32), dav * ga_ref[rows, :].astype(F32))
            p0 = _cat(pg_ref[rows, :], pv_ref[rows, :])
            cur = [du[i * SUBLANES:(i + 1) * SUBLANES] for i in range(n_grp)]
            du1, du2 = _rows_from(cur + [after], 1), _rows_from(cur + [after], 2)
            dpv = jnp.concatenate([w2 * cur[i] + w1 * du1[i] + w0 * du2[i] for i in range(n_grp)], axis=0).astype(dp_ref.dtype)
            dp_ref[0, rows, :] = dpv[:, :CB]
            dp_ref[1, rows, :] = dpv[:, CB:]
            for i in range(n_grp):
                pi = p0[i * SUBLANES:(i + 1) * SUBLANES]
                parts = (cur[i], du2[i] * pi, du1[i] * pi, cur[i] * pi)
                sums = parts if sums is None else tuple(x + y for x, y in zip(sums, parts))
            return cur[0], sums

        def loop(k, after):
            sums = None
            for j in range(per_trip):
                r0 = pl.multiple_of((n_slab - 1 - (k * per_trip + j)) * SLAB, SLAB)
                after, sums = slab(r0, after, sums)
            for q, part in enumerate(sums):
                acc[q] += part
            return after

        next_du[...] = lax.fori_loop(0, n_slab // per_trip, loop, next_du[...])

        @pl.when(tt == nt - 1)
        def _():
            for half in range(2):
                cols = slice(half * CB, (half + 1) * CB)
                dcb_ref[half] = jnp.sum(acc[0][:, cols], axis=0, keepdims=True)
                for k in range(FFN_CONV):
                    dcw_ref[half, k:k + 1, :] = jnp.sum(acc[1 + k][:, cols], axis=0, keepdims=True)

    rev = lambda t: nt - 1 - t
    tile = pl.BlockSpec((ts, CB), lambda j, t: (rev(t), j))
    return pl.pallas_call(
        body, grid=(nb, nt),
        in_specs=([tile, tile, tile] + _pair_specs((ts, CB), nb, lambda j, t: (rev(t),))
                  + _pair_specs((FFN_CONV, CB), nb, lambda j, t: (0,))),
        out_specs=[pl.BlockSpec((2, ts, CB), lambda j, t: (0, rev(t), j)),
                   pl.BlockSpec((2, FFN_CONV, CB), lambda j, t: (0, 0, j)),
                   pl.BlockSpec((2, 1, CB), lambda j, t: (0, 0, j))],
        out_shape=[jax.ShapeDtypeStruct((2, s, f2 // 2), MXU_DTYPE), jax.ShapeDtypeStruct((2, FFN_CONV, f2 // 2), F32),
                   jax.ShapeDtypeStruct((2, 1, f2 // 2), F32)],
        scratch_shapes=[pltpu.VMEM((SUBLANES, 2 * CB), F32), pltpu.VMEM((1 + FFN_CONV, SUBLANES, 2 * CB), F32)],
        compiler_params=_params("parallel", "arbitrary"), name=name,
    )(da, ga, gb, p, p, cw, cw)


def _rg_gates(xc, wa_ref, ba_ref, wx_ref, bx_ref, lam_ref):
    r = _sigmoid(_dot_nn(xc, wa_ref[0]) + ba_ref[...])
    ig = _sigmoid(_dot_nn(xc, wx_ref[0]) + bx_ref[...])
    sp = _softplus(-lam_ref[...])
    log_a = (-RG_C) * r * sp
    a = jnp.exp(log_a)
    mult = jnp.sqrt(_one_minus_sq_exp(log_a, a))
    return r, ig, sp, a, mult


def _rg_conv(scr, cw_ref, cb_ref, ts):
    views = [scr[5 + k:5 + k + ts, :] for k in range(RG_CONV)]
    xc = cb_ref[...]
    for k in range(RG_CONV):
        xc = xc + cw_ref[k:k + 1, :] * views[k]
    return xc, views


def _rg_param_specs():
    vec = pl.BlockSpec((1, CB), lambda g, t: (0, g))
    mat = pl.BlockSpec((1, CB, CB), lambda g, t: (g, 0, 0))
    return [pl.BlockSpec((RG_CONV, CB), lambda g, t: (0, g)), vec, mat, vec, mat, vec, vec]


def _scan_rows(a_scr, x_scr, out_ref, carry, ts, reverse):
    n = ts // SUBLANES
    row = lax.broadcasted_iota(jnp.int32, (SUBLANES, a_scr.shape[1]), 0)
    last = SUBLANES - 1

    def rows_of(k):
        return pl.ds(pl.multiple_of(k * SUBLANES, SUBLANES), SUBLANES)

    def local(k, _):
        rows = rows_of(k)
        a, x = a_scr[rows, :], x_scr[rows, :]
        if reverse:
            a = jnp.where(row == last, 1.0, pltpu.roll(a, last, axis=0))
            for sh in (1, 2, 4):
                keep = row < SUBLANES - sh
                x = x + a * jnp.where(keep, pltpu.roll(x, SUBLANES - sh, axis=0), 0.0)
                a = a * jnp.where(keep, pltpu.roll(a, SUBLANES - sh, axis=0), 1.0)
        else:
            for sh in (1, 2, 4):
                keep = row >= sh
                x = a * jnp.where(keep, pltpu.roll(x, sh, axis=0), 0.0) + x
                a = a * jnp.where(keep, pltpu.roll(a, sh, axis=0), 1.0)
        out_ref[rows, :] = x
        x_scr[rows, :] = a
        return 0

    lax.fori_loop(0, n, local, 0, unroll=4)

    def chain(k, c):
        rows = rows_of(n - 1 - k if reverse else k)
        v = out_ref[rows, :] + x_scr[rows, :] * c
        out_ref[rows, :] = v
        return a_scr[rows, :][0:1] * v[0:1] if reverse else v[last:last + 1]

    return lax.fori_loop(0, n, chain, carry, unroll=4)


def _rg_mid_fwd(pj, cw, cb, wa, ba, wx, bx, lam, name):
    s = pj.shape[0]
    nb = pj.shape[1] // (2 * CB)
    ts = _tile(s, (1024, 512))
    nt = s // ts

    def body(gate_ref, x_ref, halo_ref, cw_ref, cb_ref, wa_ref, ba_ref, wx_ref, bx_ref, lam_ref, y_ref, hs_ref,
             scr, a_scr, u_scr, h_scr):
        t = pl.program_id(1)

        @pl.when(t == 0)
        def _():
            h_scr[...] = jnp.zeros_like(h_scr)

        scr[0:SUBLANES, :] = jnp.where(t > 0, halo_ref[...], 0.0)
        scr[SUBLANES:, :] = x_ref[...]
        xc, _ = _rg_conv(scr, cw_ref, cb_ref, ts)
        _, ig, _, a, mult = _rg_gates(xc, wa_ref, ba_ref, wx_ref, bx_ref, lam_ref)
        a_scr[...] = a
        u_scr[...] = mult * (ig * xc)
        h_scr[0:1, :] = _scan_rows(a_scr, u_scr, hs_ref, h_scr[0:1, :], ts, False)
        y_ref[...] = (_gelu(gate_ref[...])[0] * hs_ref[...]).astype(y_ref.dtype)

    blk = pl.BlockSpec((ts, CB), lambda g, t: (t, g))
    return pl.pallas_call(
        body, grid=(nb, nt),
        in_specs=_pair_specs((ts, CB), nb, lambda g, t: (t,))
        + [pl.BlockSpec((SUBLANES, CB), lambda g, t: _halo_row(ts, lambda u: u)(g, t) + (g + nb,))] + _rg_param_specs(),
        out_specs=[blk, blk],
        out_shape=[jax.ShapeDtypeStruct((s, nb * CB), MXU_DTYPE), jax.ShapeDtypeStruct((s, nb * CB), F32)],
        scratch_shapes=[pltpu.VMEM((ts + SUBLANES, CB), F32), pltpu.VMEM((ts, CB), F32), pltpu.VMEM((ts, CB), F32),
                        pltpu.VMEM((SUBLANES, CB), F32)],
        compiler_params=_params("parallel", "arbitrary"), name=name,
    )(pj, pj, pj, cw, cb, wa, ba, wx, bx, lam)


def _rg_mid_bwd(dy, pj, hs, cw, cb, wa, ba, wx, bx, lam, name):
    s = pj.shape[0]
    nb = pj.shape[1] // (2 * CB)
    ts = _tile(s, (1024, 512))
    nt = s // ts

    def body(dy_ref, gate_ref, x_ref, halo_ref, hs_ref, hsh_ref, cw_ref, cb_ref, wa_ref, ba_ref, wx_ref, bx_ref, lam_ref,
             dpj_ref, dcw_ref, dcb_ref, dwa_ref, dba_ref, dwx_ref, dbx_ref, dlam_ref,
             scr, hscr, a_scr, d_scr, g_scr, dxscr, c_scr):
        tt = pl.program_id(1)
        t = nt - 1 - tt

        @pl.when(tt == 0)
        def _():
            c_scr[...] = jnp.zeros_like(c_scr)
            dxscr[ts:, :] = jnp.zeros((SUBLANES, CB), F32)
            for ref in (dcw_ref, dcb_ref, dwa_ref, dba_ref, dwx_ref, dbx_ref, dlam_ref):
                ref[...] = jnp.zeros_like(ref)

        scr[0:SUBLANES, :] = jnp.where(t > 0, halo_ref[...], 0.0)
        scr[SUBLANES:, :] = x_ref[...]
        hscr[0:SUBLANES, :] = jnp.where(t > 0, hsh_ref[...], 0.0)
        hscr[SUBLANES:, :] = hs_ref[...]
        xc, views = _rg_conv(scr, cw_ref, cb_ref, ts)
        r, ig, sp, a, mult = _rg_gates(xc, wa_ref, ba_ref, wx_ref, bx_ref, lam_ref)
        gate = gate_ref[...]
        gel, th = _gelu(gate)
        dyv = dy_ref[...]
        dpj_ref[0] = (dyv * hs_ref[...] * _gelu_grad(gate, th)).astype(dpj_ref.dtype)
        a_scr[...] = a
        d_scr[...] = dyv * gel
        c_scr[0:1, :] = _scan_rows(a_scr, d_scr, g_scr, c_scr[0:1, :], ts, True)
        du = g_scr[...]
        da = du * hscr[7:7 + ts, :]
        dmult = du * (ig * xc)
        dig = du * (mult * xc)
        dxc = du * (mult * ig)
        dlog_a = da * a - dmult * (a * a / mult)
        dlam_ref[...] += jnp.sum(dlog_a * r, axis=0, keepdims=True) * (RG_C * _sigmoid(-lam_ref[...]))
        dpr = dlog_a * ((-RG_C) * sp) * (r * (1.0 - r))
        dpi = dig * (ig * (1.0 - ig))
        dba_ref[...] += jnp.sum(dpr, axis=0, keepdims=True)
        dbx_ref[...] += jnp.sum(dpi, axis=0, keepdims=True)
        dwa_ref[0] += _dot_tn(xc, dpr)
        dwx_ref[0] += _dot_tn(xc, dpi)
        dxc = dxc + _dot_nt(dpr, wa_ref[0]) + _dot_nt(dpi, wx_ref[0])
        dcb_ref[...] += jnp.sum(dxc, axis=0, keepdims=True)
        for k in range(RG_CONV):
            dcw_ref[k:k + 1, :] += jnp.sum(dxc * views[k], axis=0, keepdims=True)
        dxscr[0:ts, :] = dxc
        dxp = cw_ref[3:4, :] * dxc
        for k in range(RG_CONV - 1):
            dxp = dxp + cw_ref[k:k + 1, :] * dxscr[3 - k:3 - k + ts, :]
        dpj_ref[1] = dxp.astype(dpj_ref.dtype)
        dxscr[ts:, :] = dxscr[0:SUBLANES, :]

    rev = lambda g, t: (nt - 1 - t, g)
    rev_halo = lambda g, t: (jnp.maximum((nt - 1 - t) * (ts // SUBLANES) - 1, 0), g)
    vec = pl.BlockSpec((1, CB), lambda g, t: (0, g))
    mat = pl.BlockSpec((1, CB, CB), lambda g, t: (g, 0, 0))
    d = nb * CB
    vshape = jax.ShapeDtypeStruct((1, d), F32)
    mshape = jax.ShapeDtypeStruct((nb, CB, CB), F32)
    return pl.pallas_call(
        body, grid=(nb, nt),
        in_specs=[pl.BlockSpec((ts, CB), rev)] + _pair_specs((ts, CB), nb, lambda g, t: (nt - 1 - t,))
        + [pl.BlockSpec((SUBLANES, CB), lambda g, t: (rev_halo(g, t)[0], g + nb)),
           pl.BlockSpec((ts, CB), rev), pl.BlockSpec((SUBLANES, CB), rev_halo)] + _rg_param_specs(),
        out_specs=[pl.BlockSpec((2, ts, CB), lambda g, t: (0, nt - 1 - t, g)), pl.BlockSpec((RG_CONV, CB), lambda g, t: (0, g)),
                   vec, mat, vec, mat, vec, vec],
        out_shape=[jax.ShapeDtypeStruct((2, s, d), MXU_DTYPE), jax.ShapeDtypeStruct((RG_CONV, d), F32), vshape, mshape, vshape,
                   mshape, vshape, vshape],
        scratch_shapes=[pltpu.VMEM((ts + SUBLANES, CB), F32), pltpu.VMEM((ts + SUBLANES, CB), F32), pltpu.VMEM((ts, CB), F32),
                        pltpu.VMEM((ts, CB), F32), pltpu.VMEM((ts, CB), F32), pltpu.VMEM((ts + SUBLANES, CB), F32),
                        pltpu.VMEM((SUBLANES, CB), F32)],
        compiler_params=_params("parallel", "arbitrary"), name=name,
    )(dy, pj, pj, pj, hs, hs, cw, cb, wa, ba, wx, bx, lam)


GLA_DK = 128
GLA_DV = 256
GLA_O_K = GLA_HEADS * GLA_DK
GLA_O_V = 2 * GLA_HEADS * GLA_DK
GLA_O_R = GLA_O_V + GLA_HEADS * GLA_DV
GLA_O_Z = GLA_O_R + GLA_HEADS * GLA_DV
GLA_IN = GLA_O_Z + GLA_RANK
GLA_TS = 256


def _dk(h, base=0):
    return slice(base + h * GLA_DK, base + (h + 1) * GLA_DK)


def _dv(h, base=0):
    return slice(base + h * GLA_DV, base + (h + 1) * GLA_DV)


def _split3(x):
    hi = x.astype(BF16)
    r1 = x - hi.astype(F32)
    mid = r1.astype(BF16)
    lo = (r1 - mid.astype(F32)).astype(BF16)
    return hi, mid, lo


def _chunk_cumsum(x, reverse):
    n = x.shape[0]
    i = lax.broadcasted_iota(jnp.int32, (n, n), 0)
    j = lax.broadcasted_iota(jnp.int32, (n, n), 1)
    same = (i // GLA_CHUNK) == (j // GLA_CHUNK)
    tri = jnp.where(same & ((j >= i) if reverse else (j <= i)), 1.0, 0.0).astype(BF16)
    out = jnp.zeros(x.shape, F32)
    for piece in _split3(x):
        out = out + lax.dot_general(tri, piece, (((1,), (0,)), ((), ())), preferred_element_type=F32)
    return out


def _gla_head(pj_ref, h):
    return (pj_ref[:, _dk(h)] * (GLA_DK ** -0.5), pj_ref[:, _dk(h, GLA_O_K)], pj_ref[:, _dv(h, GLA_O_V)],
            pj_ref[:, _dv(h, GLA_O_R)])


def _gla_decays(gc):
    gref = gc[GLA_CHUNK // 2:GLA_CHUNK // 2 + 1, :]
    glast = gc[GLA_CHUNK - 1:GLA_CHUNK, :]
    return jnp.exp(gc), jnp.exp(gc - gref), jnp.exp(gref - gc), jnp.exp(glast - gc), jnp.exp(glast)


def _causal_mask():
    i = lax.broadcasted_iota(jnp.int32, (GLA_CHUNK, GLA_CHUNK), 0)
    j = lax.broadcasted_iota(jnp.int32, (GLA_CHUNK, GLA_CHUNK), 1)
    return j <= i


def _log_sigmoid(x):
    return jnp.minimum(x, 0.0) - _log1p_pos(jnp.exp(-jnp.abs(x)))


def _gla_mid_fwd(pj, wal, bal, ng, name):
    s, nh = pj.shape[0], GLA_HEADS
    ts = _tile(s, (GLA_TS,))
    nt, nc = s // ts, ts // GLA_CHUNK

    def body(pj_ref, wal_ref, bal_ref, ng_ref, act_ref, o_ref, st_ref, s_scr):
        t = pl.program_id(0)

        @pl.when(t == 0)
        def _():
            s_scr[...] = jnp.zeros_like(s_scr)

        heads = []
        z = pj_ref[:, GLA_O_Z:]
        for h in range(nh):
            q, k, v, r = _gla_head(pj_ref, h)
            g = _log_sigmoid(_dot_nn(z, wal_ref[:, _dk(h)]) + bal_ref[:, _dk(h)]) * (1.0 / GLA_TAU)
            heads.append((q, k, v, r, _chunk_cumsum(g, False)))
        mask = _causal_mask()
        for c in range(nc):
            sl = slice(c * GLA_CHUNK, (c + 1) * GLA_CHUNK)
            for h, (q, k, v, r, gcum) in enumerate(heads):
                eg, eq, ek, ekd, egl = _gla_decays(gcum[sl])
                st = s_scr[h]
                st_ref[c, h] = st
                attn = jnp.where(mask, _dot_nt(q[sl] * eq, k[sl] * ek), 0.0)
                o_ref[sl, h * GLA_DV:(h + 1) * GLA_DV] = _dot_nt(q[sl] * eg, st) + _dot_nn(attn, v[sl])
                s_scr[h] = st * egl + _dot_tn(v[sl], k[sl] * ekd)
        for h, (q, k, v, r, gcum) in enumerate(heads):
            cols = slice(h * GLA_DV, (h + 1) * GLA_DV)
            o = o_ref[:, cols]
            on = o * lax.rsqrt(jnp.mean(o * o, axis=-1, keepdims=True) + EPS)
            act_ref[:, cols] = ((on * ng_ref[...]) * (r * _sigmoid(r))).astype(act_ref.dtype)

    blk = pl.BlockSpec((ts, nh * GLA_DV), lambda t: (t, 0))
    whole = lambda shape: pl.BlockSpec(shape, lambda t: (0,) * len(shape))
    return pl.pallas_call(
        body, grid=(nt,),
        in_specs=[pl.BlockSpec((ts, GLA_IN), lambda t: (t, 0)), whole((GLA_RANK, nh * GLA_DK)), whole((1, nh * GLA_DK)),
                  whole((1, GLA_DV))],
        out_specs=[blk, blk, pl.BlockSpec((nc, nh, GLA_DV, GLA_DK), lambda t: (t, 0, 0, 0))],
        out_shape=[jax.ShapeDtypeStruct((s, nh * GLA_DV), MXU_DTYPE), jax.ShapeDtypeStruct((s, nh * GLA_DV), F32),
                   jax.ShapeDtypeStruct((s // GLA_CHUNK, nh, GLA_DV, GLA_DK), F32)],
        scratch_shapes=[pltpu.VMEM((nh, GLA_DV, GLA_DK), F32)],
        compiler_params=_params("arbitrary"), name=name,
    )(pj, wal, bal, ng)


def _gla_mid_bwd(dact, pj, o, st, wal, bal, ng, name):
    s, nh = pj.shape[0], GLA_HEADS
    ts = _tile(s, (GLA_TS,))
    nt, nc = s // ts, ts // GLA_CHUNK

    def body(dact_ref, pj_ref, o_ref, st_ref, wal_ref, bal_ref, ng_ref, dpj_ref, dwal_ref, dbal_ref, dng_ref,
             ds_scr, dg_scr):
        tt = pl.program_id(0)

        @pl.when(tt == 0)
        def _():
            ds_scr[...] = jnp.zeros_like(ds_scr)
            dwal_ref[...] = jnp.zeros_like(dwal_ref)
            dbal_ref[...] = jnp.zeros_like(dbal_ref)
            dng_ref[...] = jnp.zeros_like(dng_ref)

        heads = []
        z = pj_ref[:, GLA_O_Z:]
        for h in range(nh):
            q, k, v, r = _gla_head(pj_ref, h)
            logit = _dot_nn(z, wal_ref[:, _dk(h)]) + bal_ref[:, _dk(h)]
            gcum = _chunk_cumsum(_log_sigmoid(logit) * (1.0 / GLA_TAU), False)
            ov = o_ref[:, h * GLA_DV:(h + 1) * GLA_DV]
            ro = lax.rsqrt(jnp.mean(ov * ov, axis=-1, keepdims=True) + EPS)
            on = ov * ro
            sg = _sigmoid(r)
            sil = r * sg
            dav = dact_ref[:, h * GLA_DV:(h + 1) * GLA_DV]
            dpj_ref[:, _dv(h, GLA_O_R)] = (dav * (on * ng_ref[...]) * (sg + sil * (1.0 - sg))).astype(dpj_ref.dtype)
            t1 = dav * sil
            dng_ref[...] += jnp.sum(t1 * on, axis=0, keepdims=True)
            dn = t1 * ng_ref[...]
            do = ro * (dn - on * jnp.mean(dn * on, axis=-1, keepdims=True))
            heads.append((q, k, v, logit, gcum, do))
        mask = _causal_mask()
        scale = GLA_DK ** -0.5
        last_row = lax.broadcasted_iota(jnp.int32, (GLA_CHUNK, GLA_DK), 0) == GLA_CHUNK - 1
        for c in reversed(range(nc)):
            sl = slice(c * GLA_CHUNK, (c + 1) * GLA_CHUNK)
            for h, (q, k, v, logit, gcum, do) in enumerate(heads):
                eg, eq, ek, ekd, egl = _gla_decays(gcum[sl])
                qc, kc, vc, doc = q[sl], k[sl], v[sl], do[sl]
                qg, qt, kt, kd = qc * eg, qc * eq, kc * ek, kc * ekd
                sp = st_ref[c, h]
                ds = ds_scr[h]
                attn = jnp.where(mask, _dot_nt(qt, kt), 0.0)
                dattn = jnp.where(mask, _dot_nt(doc, vc), 0.0)
                dqg = _dot_nn(doc, sp)
                dqt = _dot_nn(dattn, kt)
                dkt = _dot_tn(dattn, qt)
                dkd = _dot_nn(vc, ds)
                dpj_ref[sl, _dv(h, GLA_O_V)] = (_dot_tn(attn, doc) + _dot_nt(kd, ds)).astype(dpj_ref.dtype)
                dpj_ref[sl, _dk(h)] = (scale * (dqg * eg + dqt * eq)).astype(dpj_ref.dtype)
                dpj_ref[sl, _dk(h, GLA_O_K)] = (dkt * ek + dkd * ekd).astype(dpj_ref.dtype)
                kdd = dkd * kd
                dgl = jnp.sum(kdd, axis=0, keepdims=True) + jnp.sum(ds * sp, axis=0, keepdims=True) * egl
                dg_scr[h, sl, :] = dqg * qg + dqt * qt - dkt * kt - kdd + jnp.where(last_row, dgl, 0.0)
                ds_scr[h] = ds * egl + _dot_tn(doc, qg)
        dz = jnp.zeros((ts, GLA_RANK), F32)
        for h, (q, k, v, logit, gcum, do) in enumerate(heads):
            dlogit = _chunk_cumsum(dg_scr[h], True) * (1.0 / GLA_TAU) * _sigmoid(-logit)
            dz = dz + _dot_nt(dlogit, wal_ref[:, _dk(h)])
            dwal_ref[:, _dk(h)] += _dot_tn(z, dlogit)
            dbal_ref[:, _dk(h)] += jnp.sum(dlogit, axis=0, keepdims=True)
        dpj_ref[:, GLA_O_Z:] = dz.astype(dpj_ref.dtype)

    rev = lambda t: (nt - 1 - t, 0)
    whole = lambda shape: pl.BlockSpec(shape, lambda t: (0,) * len(shape))
    wide = pl.BlockSpec((ts, nh * GLA_DV), rev)
    return pl.pallas_call(
        body, grid=(nt,),
        in_specs=[wide, pl.BlockSpec((ts, GLA_IN), rev), wide,
                  pl.BlockSpec((nc, nh, GLA_DV, GLA_DK), lambda t: (nt - 1 - t, 0, 0, 0)),
                  whole((GLA_RANK, nh * GLA_DK)), whole((1, nh * GLA_DK)), whole((1, GLA_DV))],
        out_specs=[pl.BlockSpec((ts, GLA_IN), rev), whole((GLA_RANK, nh * GLA_DK)), whole((1, nh * GLA_DK)), whole((1, GLA_DV))],
        out_shape=[jax.ShapeDtypeStruct((s, GLA_IN), MXU_DTYPE), jax.ShapeDtypeStruct((GLA_RANK, nh * GLA_DK), F32),
                   jax.ShapeDtypeStruct((1, nh * GLA_DK), F32), jax.ShapeDtypeStruct((1, GLA_DV), F32)],
        scratch_shapes=[pltpu.VMEM((nh, GLA_DV, GLA_DK), F32), pltpu.VMEM((nh, ts, GLA_DK), F32)],
        compiler_params=_params("arbitrary"), name=name,
    )(dact, pj, o, st, wal, bal, ng)


def _adamw(w, gs, m, v, name, after=None):
    layers, rows, cols = w.shape
    gs = list(gs) if isinstance(gs, (list, tuple)) else gs
    n_g = len(gs) if isinstance(gs, list) else 1
    if rows % SUBLANES == 0:
        tr, tc = _tile(rows, (256, 128, 64, 32, 16, 8)), cols
    else:
        tr, tc = rows, _tile(cols, (256, 128))
    c1 = 1.0 / (1.0 - ADAM_B1 ** ADAM_STEP)
    c2 = 1.0 / (1.0 - ADAM_B2 ** ADAM_STEP)

    def body(*refs):
        g_refs, (w_ref, m_ref, v_ref) = refs[:n_g], refs[n_g:n_g + 3]
        go_ref, d_ref, mo_ref, vo_ref = refs[-4:]
        gv = g_refs[0][...]
        for l in range(1, n_g):
            gv = jnp.where(pl.program_id(0) == l, g_refs[l][...], gv)
        m2 = ADAM_B1 * m_ref[...] + (1.0 - ADAM_B1) * gv
        v2 = ADAM_B2 * v_ref[...] + (1.0 - ADAM_B2) * (gv * gv)
        d_ref[...] = (-ADAM_LR) * ((m2 * c1) / (jnp.sqrt(v2 * c2) + ADAM_EPS) + ADAM_WD * w_ref[...])
        go_ref[...] = gv
        mo_ref[...] = m2
        vo_ref[...] = v2

    spec = pl.BlockSpec((None, tr, tc), lambda l, i, j: (l, i, j))
    g_specs = [pl.BlockSpec((tr, tc), lambda l, i, j: (i, j))] * n_g if isinstance(gs, list) else [spec]
    extra = [] if after is None else [(after, _ANY)]
    shape = jax.ShapeDtypeStruct((layers, rows, cols), F32)
    return pl.pallas_call(
        body, grid=(layers, rows // tr, cols // tc), in_specs=g_specs + [spec] * 3 + [sp for _, sp in extra],
        out_specs=[spec] * 4, out_shape=[shape] * 4, compiler_params=_params("parallel", "parallel", "parallel"), name=name,
    )(*(gs if isinstance(gs, list) else [gs]), w, m, v, *[a for a, _ in extra])


def _col_slots(w, name):
    r, cc = w.shape
    c = cc // N_CHIP
    tr = _tile(r, (256,))

    def body(w_ref, o_ref):
        for j in range(N_CHIP):
            o_ref[j] = w_ref[:, j * c:(j + 1) * c]

    return pl.pallas_call(
        body, grid=(r // tr,), in_specs=[pl.BlockSpec((tr, cc), lambda i: (i, 0))],
        out_specs=pl.BlockSpec((N_CHIP, tr, c), lambda i: (0, i, 0)), out_shape=jax.ShapeDtypeStruct((N_CHIP, r, c), w.dtype),
        compiler_params=_params("parallel"), name=name,
    )(w)


def _from_col_slots(w, name):
    n, r, c = w.shape
    tr = _tile(r, (256,))

    def body(w_ref, o_ref):
        for j in range(n):
            o_ref[:, j * c:(j + 1) * c] = w_ref[j]

    return pl.pallas_call(
        body, grid=(r // tr,), in_specs=[pl.BlockSpec((n, tr, c), lambda i: (0, i, 0))],
        out_specs=pl.BlockSpec((tr, n * c), lambda i: (i, 0)), out_shape=jax.ShapeDtypeStruct((r, n * c), w.dtype),
        compiler_params=_params("parallel"), name=name,
    )(w)


def _block_rows_to_slots(w):
    g, r4, cc = w.shape
    return jnp.swapaxes(w.reshape(g, N_CHIP, r4 // N_CHIP, cc), 0, 1).reshape(N_CHIP, g * (r4 // N_CHIP), cc)


def _slots_to_block_rows(w, g):
    n, gr, cc = w.shape
    return jnp.swapaxes(w.reshape(n, g, gr // g, cc), 0, 1).reshape(g, n * (gr // g), cc)


def _local_step(x, tgt, mod, w, fetch=None, done=None, later=None):
    depth = mod.shape[0]
    row = lambda v: v.reshape(1, -1)
    w = dict(w)
    w["ffn_w_up"], w["ffn_w_down"] = dict(enumerate(w["ffn_w_up"])), dict(enumerate(w["ffn_w_down"]))

    def arrive(stage, after):
        if fetch is not None:
            for k, v in fetch(stage, after).items():
                if isinstance(v, dict):
                    w[k].update(v)
                else:
                    w[k] = v

    saved = []
    for i in range(depth):
        if i == 1:
            arrive("gla", x)
        sh_m, sc_m, gt_m, sh_f, sc_f, gt_f = (mod[i, j:j + 1] for j in range(6))
        g0, g1, g2, g3 = (w["norm_g"][i, j:j + 1] for j in range(4))
        tag = f"_l{i}"
        if i == 0:
            h = _norm_mod_fwd(x, g0, sc_m, sh_m, "norm_mix" + tag)
        if i % 2 == 0:
            pj = _mm(h, w["rg_w_in"], w_slots=N_CHIP, name="rg_in" + tag)
            act, aux = _rg_mid_fwd(pj, w["rg_conv_w"], row(w["rg_conv_b"]), w["rg_wa"], row(w["rg_ba"]), w["rg_wx"],
                                   row(w["rg_bx"]), row(w["rg_lambda"]), "rg_mid" + tag)
            y = _mm(act, w["rg_w_out"], name="rg_out" + tag)
        else:
            pj = _mm(h, w["gla_w_in"], tm_max=512, name="gla_in" + tag)
            act, *aux = _gla_mid_fwd(pj, w["gla_w_alpha"], row(w["gla_b_alpha"]), row(w["gla_norm_g"]), "gla_mid" + tag)
            y = _mm(act, w["gla_w_out"], name="gla_out" + tag)
        x1, h2 = _post_norm_fwd(x, y, g1, gt_m, g2, sc_f, sh_f, "post_mix" + tag)
        arrive(f"ffn{i}", x1)
        p = _mm(h2, w["ffn_w_up"][i], w_slots=N_CHIP, name="ffn_up" + tag)
        a, ga, gb = _ffn_mid_fwd(p, w["ffn_conv_w"][i], w["ffn_conv_b"][i:i + 1], "ffn_mid" + tag)
        y2 = _mm(a, w["ffn_w_down"][i], name="ffn_down" + tag)
        saved_h = h
        if i + 1 < depth:
            nxt = [mod[i + 1, j:j + 1] for j in range(2)] + [w["norm_g"][i + 1, 0:1]]
            x2, h = _post_norm_fwd(x1, y2, g3, gt_f, nxt[2], nxt[1], nxt[0], "post_ffn" + tag)
        else:
            x2 = None
            cols, dx = _post_loss(x1, y2, g3, gt_f, tgt, "post_ffn_loss")
        saved.append((x, saved_h, pj, act, aux, y, x1, h2, p, (a, ga, gb), y2))
        x = x2

    stacked = ("norm_g", "ffn_conv_w", "ffn_conv_b", "mod")
    gr = {k: [None] * depth for k in stacked + ("ffn_w_up", "ffn_w_down")}
    told = lambda stage: done(stage, gr) if done is not None else 0.0
    told_later = lambda stage, after: later(stage, after) if later is not None else 0.0
    for i in reversed(range(depth)):
        x0, h, pj, act, aux, y, x1, h2, p, (a, ga, gb), y2 = saved[i]
        sh_m, sc_m, gt_m, sh_f, sc_f, gt_f = (mod[i, j:j + 1] for j in range(6))
        g0, g1, g2, g3 = (w["norm_g"][i, j:j + 1] for j in range(4))
        tag = f"_l{i}"
        if i == depth - 1:
            dy2, d_g3, d_gt_f = _post_bwd(dx, y2, g3, gt_f, "post_ffn_b" + tag)
        else:
            dy2, d_g3, d_gt_f = ahead
        da = _mm(dy2, w["ffn_w_down"][i], tb=True, name="ffn_down_dx" + tag)
        gr["ffn_w_down"][i] = _mm(a, dy2, ta=True, name="ffn_down_dw" + tag)
        conv_w = w["ffn_conv_w"][i] + (told_later("l1", da) if i == 0 else 0.0)
        dp, dcw, dcb = _ffn_mid_bwd(da, p, ga, gb, conv_w, "ffn_mid_b" + tag)
        gr["ffn_conv_w"][i], gr["ffn_conv_b"][i] = _cat(dcw[0], dcw[1]), _cat(dcb[0], dcb[1])[0]
        dh2 = _mm(dp, w["ffn_w_up"][i], tb=True, a_parts=2, w_slots=N_CHIP, name="ffn_up_dx" + tag)
        gr["ffn_w_up"][i] = _mm(h2, dp, ta=True, b_parts=2, out_slots=N_CHIP, name="ffn_up_dw" + tag)
        if i == 0:
            gt_m = gt_m + told("ffn0")
        dx1, dy, d_g2, d_sc_f, d_sh_f, d_g1, d_gt_m = _norm_post_bwd(dh2, x1, g2, sc_f, dx, y, g1, gt_m, "norm_ffn_b" + tag)
        if i % 2 == 0:
            dact = _mm(dy, w["rg_w_out"], tb=True, name="rg_out_dx" + tag)
            gr["rg_w_out"] = _mm(act, dy, ta=True, name="rg_out_dw" + tag)
            lam = row(w["rg_lambda"]) + told_later("ffn0", gr["rg_w_out"])
            dpj, gr["rg_conv_w"], d_cb, gr["rg_wa"], d_ba, gr["rg_wx"], d_bx, d_lam = _rg_mid_bwd(
                dact, pj, aux, w["rg_conv_w"], row(w["rg_conv_b"]), w["rg_wa"], row(w["rg_ba"]), w["rg_wx"],
                row(w["rg_bx"]), lam, "rg_mid_b" + tag)
            gr["rg_conv_b"], gr["rg_ba"], gr["rg_bx"], gr["rg_lambda"] = d_cb[0], d_ba[0], d_bx[0], d_lam[0]
            dh = _mm(dpj, w["rg_w_in"], tb=True, a_parts=2, w_slots=N_CHIP, name="rg_in_dx" + tag)
            gr["rg_w_in"] = _mm(h, dpj, ta=True, b_parts=2, out_slots=N_CHIP, name="rg_in_dw" + tag)
            sc_m = sc_m + told("rg")
        else:
            dact = _mm(dy, w["gla_w_out"], tb=True, name="gla_out_dx" + tag)
            gr["gla_w_out"] = _mm(act, dy, ta=True, name="gla_out_dw" + tag)
            dpj, gr["gla_w_alpha"], d_bal, d_ng = _gla_mid_bwd(dact, pj, aux[0], aux[1], w["gla_w_alpha"], row(w["gla_b_alpha"]),
                                                               row(w["gla_norm_g"]), "gla_mid_b" + tag)
            gr["gla_b_alpha"], gr["gla_norm_g"] = d_bal[0], d_ng[0]
            dh = _mm(dpj, w["gla_w_in"], tb=True, name="gla_in_dx" + tag)
            gr["gla_w_in"] = _mm(h, dpj, ta=True, tm_max=512, name="gla_in_dw" + tag)
            mod = mod.at[0].add(told("l1"))
        if i > 0:
            dx, dy_below, d_g0, d_sc_m, d_sh_m, d_g_below, d_gt_below = _norm_post_bwd(
                dh, x0, g0, sc_m, dx1, saved[i - 1][-1], w["norm_g"][i - 1, 3:4], mod[i - 1, 5:6], "norm_mix_b" + tag)
            ahead = (dy_below, d_g_below, d_gt_below)
        else:
            dx, d_g0, d_sc_m, d_sh_m = _norm_mod_bwd(dh, x0, g0, sc_m, dx1, "norm_mix_b" + tag)
        gr["norm_g"][i] = jnp.concatenate([d_g0, d_g1, d_g2, d_g3], axis=0)
        gr["mod"][i] = jnp.concatenate([d_sh_m, d_sc_m, d_gt_m, d_sh_f, d_sc_f, d_gt_f], axis=0)
    for k in stacked:
        gr[k] = jnp.stack(gr[k])
    return cols, dx, gr


ADA_ROWS = 16


def _ada_fwd(c16, ada_w, ada_b, name):
    depth, d, n = ada_w.shape
    tn = _tile(n, (512, 256, 128))

    def body(c_ref, w_ref, b_ref, o_ref):
        cv = c_ref[...]
        o_ref[0] = _dot_nn(cv * _sigmoid(cv), w_ref[0]) + b_ref[0]

    return pl.pallas_call(
        body, grid=(depth, n // tn),
        in_specs=[pl.BlockSpec((ADA_ROWS, d), lambda l, j: (0, 0)), pl.BlockSpec((1, d, tn), lambda l, j: (l, 0, j)),
                  pl.BlockSpec((1, 1, tn), lambda l, j: (l, 0, j))],
        out_specs=pl.BlockSpec((1, ADA_ROWS, tn), lambda l, j: (l, 0, j)),
        out_shape=jax.ShapeDtypeStruct((depth, ADA_ROWS, n), F32),
        compiler_params=_params("parallel", "parallel"), name=name,
    )(c16, ada_w, ada_b)


def _ada_bwd(c16, dmod16, name):
    depth, _, n = dmod16.shape
    d = c16.shape[1]
    tn = _tile(n, (512, 256, 128))

    def body(c_ref, dm_ref, o_ref):
        cv = c_ref[...]
        o_ref[0] = _dot_tn(cv * _sigmoid(cv), dm_ref[0])

    return pl.pallas_call(
        body, grid=(depth, n // tn),
        in_specs=[pl.BlockSpec((ADA_ROWS, d), lambda l, j: (0, 0)), pl.BlockSpec((1, ADA_ROWS, tn), lambda l, j: (l, 0, j))],
        out_specs=pl.BlockSpec((1, d, tn), lambda l, j: (l, 0, j)),
        out_shape=jax.ShapeDtypeStruct((depth, d, n), F32),
        compiler_params=_params("parallel", "parallel"), name=name,
    )(c16, dmod16)


PACK_COLS = 1024
_ANY = pl.BlockSpec(memory_space=pl.ANY)
_VMEM = pl.BlockSpec(memory_space=pltpu.VMEM)


def _place():
    return lax.axis_index("x"), lax.axis_index("y"), lax.axis_index("c")


def _other_chips(x, y):
    return [(1 - x, y), (x, 1 - y), (1 - x, 1 - y)]


def _rcopy(src, dst, send_sems, recv_sems, k, peer):
    return pltpu.make_async_remote_copy(src_ref=src, dst_ref=dst, send_sem=send_sems.at[k], recv_sem=recv_sems.at[k],
                                        device_id=peer, device_id_type=MESH)


def _all_gather_8(v, name):
    r, cc = v.shape

    def body(v_ref, out_ref, send_sems, recv_sems, local_sem):
        x, y, c = _place()
        me = 4 * x + 2 * y + c
        mine = pltpu.make_async_copy(v_ref, out_ref.at[me], local_sem)
        mine.start()
        peers = []
        for k in range(1, N_DEV):
            px = 1 - x if k & 4 else x
            py = 1 - y if k & 2 else y
            pc = 1 - c if k & 1 else c
            peers.append((px, py, pc))
        sends = [_rcopy(v_ref, out_ref.at[me], send_sems, recv_sems, k, p) for k, p in enumerate(peers)]
        for cp in sends:
            cp.start()
        for k, (px, py, pc) in enumerate(peers):
            _rcopy(v_ref, out_ref.at[4 * px + 2 * py + pc], send_sems, recv_sems, k, (px, py, pc)).wait_recv()
        for cp in sends:
            cp.wait_send()
        mine.wait()

    return pl.pallas_call(
        body, in_specs=[_VMEM], out_specs=_VMEM, out_shape=jax.ShapeDtypeStruct((N_DEV, r, cc), v.dtype),
        scratch_shapes=[pltpu.SemaphoreType.DMA((N_DEV - 1,)), pltpu.SemaphoreType.DMA((N_DEV - 1,)), pltpu.SemaphoreType.DMA],
        compiler_params=pltpu.CompilerParams(vmem_limit_bytes=VMEM_LIMIT), name=name,
    )(v)


def _gather_chips(shards, name):
    n = len(shards)
    per = 2 * (N_CHIP - 1)

    def body(*refs):
        ins, outs, (send_sems, recv_sems) = refs[:n], refs[n:2 * n], refs[2 * n:]
        x, y, c = _place()
        chip = 2 * x + y
        chips = _other_chips(x, y)
        rows = [(pl.ds(c * (r.shape[0] // 2), r.shape[0] // 2), pl.ds((1 - c) * (r.shape[0] // 2), r.shape[0] // 2)) for r in ins]
        first = [_rcopy(ins[i].at[rows[i][0]], outs[i].at[chip, rows[i][0]], send_sems, recv_sems, per * i + j, (px, py, c))
                 for i in range(n) for j, (px, py) in enumerate(chips)]
        for cp in first:
            cp.start()
        passed = []
        for i in range(n):
            for j, (px, py) in enumerate(chips):
                landed = outs[i].at[2 * px + py, rows[i][0]]
                _rcopy(ins[i].at[rows[i][0]], landed, send_sems, recv_sems, per * i + j, (px, py, c)).wait_recv()
                fw = _rcopy(landed, landed, send_sems, recv_sems, per * i + N_CHIP - 1 + j, (x, y, 1 - c))
                fw.start()
                passed.append(fw)
        for i in range(n):
            for j, (px, py) in enumerate(chips):
                landed = outs[i].at[2 * px + py, rows[i][1]]
                _rcopy(landed, landed, send_sems, recv_sems, per * i + N_CHIP - 1 + j, (x, y, 1 - c)).wait_recv()
        for cp in first + passed:
            cp.wait_send()

    return pl.pallas_call(
        body, in_specs=[_ANY] * n, out_specs=[_ANY] * n,
        out_shape=[jax.ShapeDtypeStruct((N_CHIP,) + sh.shape, sh.dtype) for sh in shards],
        scratch_shapes=[pltpu.SemaphoreType.DMA((per * n,)), pltpu.SemaphoreType.DMA((per * n,))], name=name,
    )(*shards)


_ROW_TILES = (640, 512, 352, 256, 128, 64, 32, 16)


def _pair_sum(g, other, c_idx, name):
    n, half, cc = other.shape
    tr = _tile(half, _ROW_TILES)

    def body(c_ref, g_ref, o_ref, out_ref):
        out_ref[...] = (g_ref[...] + o_ref[...]).astype(out_ref.dtype)

    return pl.pallas_call(
        body,
        grid_spec=pltpu.PrefetchScalarGridSpec(
            num_scalar_prefetch=1, grid=(n, half // tr),
            in_specs=[pl.BlockSpec((None, None, tr, cc), lambda k, i, c_ref: (k, c_ref[0], i, 0)),
                      pl.BlockSpec((None, tr, cc), lambda k, i, c_ref: (k, i, 0))],
            out_specs=pl.BlockSpec((None, tr, cc), lambda k, i, c_ref: (k, i, 0))),
        out_shape=jax.ShapeDtypeStruct((n, half, cc), BF16),
        compiler_params=_params("parallel", "parallel"), name=name,
    )(c_idx, g.reshape(n, 2, half, cc), other)


_HBM = pl.BlockSpec(memory_space=pltpu.HBM)
_SEM = pl.BlockSpec(memory_space=pltpu.SEMAPHORE)
_DATAFLOW = pltpu.SideEffectType.DATAFLOW_SIDE_EFFECTING


def _split_copies(srcs, lands, send_sems, recv_sems, mode, arriving):
    x, y, c = _place()
    chip = 2 * x + y
    out = []
    for i, (src, land) in enumerate(zip(srcs, lands)):
        if mode == "all":
            for k in range(1, N_DEV):
                px, py, pc = (1 - x if k & 4 else x), (1 - y if k & 2 else y), (1 - c if k & 1 else c)
                slot = 4 * px + 2 * py + pc if arriving else 2 * chip + c
                out.append(_rcopy(src, land.at[slot], send_sems, recv_sems, (N_DEV - 1) * i + k - 1, (px, py, pc)))
            continue
        if mode == "pair":
            half = src.shape[1] // 2
            out.append(_rcopy(src.at[:, pl.ds((1 - c) * half, half)], land, send_sems, recv_sems, i, (x, y, 1 - c)))
            continue
        for j, (px, py) in enumerate(_other_chips(x, y)):
            there = 2 * px + py
            part = src.at[there] if mode == "slots" else src
            out.append(_rcopy(part, land.at[there if arriving else chip], send_sems, recv_sems, (N_CHIP - 1) * i + j, (px, py, c)))
    return out


def _land_shape(src, mode):
    if mode == "pair":
        return (src.shape[0], src.shape[1] // 2, src.shape[2])
    if mode == "all":
        return (N_DEV,) + src.shape
    return (N_CHIP,) + (src.shape[1:] if mode == "slots" else src.shape)


def _send_start(srcs, mode, name):
    n = len(srcs)
    n_sem = {"pair": 1, "all": N_DEV - 1}.get(mode, N_CHIP - 1) * n
    lands = [lax.empty(_land_shape(s, mode), s.dtype) for s in srcs]

    def body(*refs):
        ins, zones, (send_sems, recv_sems) = refs[:n], refs[n:2 * n], refs[2 * n:2 * n + 2]
        for cp in _split_copies(ins, zones, send_sems, recv_sems, mode, False):
            cp.start()
        refs[-1][...] = jnp.zeros_like(refs[-1])

    hbm = lambda a: pltpu.HBM(a.shape, a.dtype)
    outs = pl.pallas_call(
        body, name=name, in_specs=[_HBM] * (2 * n),
        out_shape=(pltpu.SemaphoreType.DMA((n_sem,)), pltpu.SemaphoreType.DMA((n_sem,)), *[hbm(a) for a in srcs],
                   *[hbm(a) for a in lands], jax.ShapeDtypeStruct((SUBLANES, LANES), F32)),
        out_specs=(_SEM, _SEM, *[_HBM] * (2 * n), _VMEM), input_output_aliases={i: 2 + i for i in range(2 * n)},
        compiler_params=pltpu.CompilerParams(has_side_effects=_DATAFLOW),
    )(*[pltpu.with_memory_space_constraint(a, pltpu.HBM) for a in list(srcs) + lands])
    return (outs[0], outs[1], list(outs[2:2 + n]), list(outs[2 + n:2 + 2 * n])), outs[-1]


def _send_wait(state, after, mode, name):
    send_sems, recv_sems, srcs, lands = state
    n = len(srcs)

    def body(*refs):
        ins, zones, (send_s, recv_s) = refs[:n], refs[n:2 * n], refs[2 * n:2 * n + 2]
        for cp in _split_copies(ins, zones, send_s, recv_s, mode, True):
            cp.wait_send()
            cp.wait_recv()

    hbm = lambda a: pltpu.HBM(a.shape, a.dtype)
    outs = pl.pallas_call(
        body, name=name, in_specs=[_HBM] * (2 * n) + [_SEM, _SEM, _ANY],
        out_shape=tuple(hbm(a) for a in srcs + lands), out_specs=tuple([_HBM] * (2 * n)),
        input_output_aliases={i: i for i in range(2 * n)},
        compiler_params=pltpu.CompilerParams(has_side_effects=_DATAFLOW),
    )(*srcs, *lands, send_sems, recv_sems, after)
    return list(outs[:n]), list(outs[n:])


def _sum_lead(v, name):
    n, r, cc = v.shape
    tr = _tile(r, _ROW_TILES + (8,))

    def body(v_ref, o_ref):
        acc = v_ref[0].astype(F32)
        for k in range(1, n):
            acc = acc + v_ref[k].astype(F32)
        o_ref[...] = acc

    return pl.pallas_call(
        body, grid=(r // tr,), in_specs=[pl.BlockSpec((n, tr, cc), lambda i: (0, i, 0))],
        out_specs=pl.BlockSpec((tr, cc), lambda i: (i, 0)), out_shape=jax.ShapeDtypeStruct((r, cc), F32),
        compiler_params=_params("parallel"), name=name,
    )(v)


def _chip_sum(arrived, mine, chip_idx, name):
    n, r, cc = arrived.shape
    tr = _tile(r, _ROW_TILES)

    def body(chip_ref, a_ref, m_ref, o_ref):
        acc = jnp.zeros((tr, cc), F32)
        for k in range(n):
            acc = acc + jnp.where(chip_ref[0] == k, m_ref[...], a_ref[k]).astype(F32)
        o_ref[...] = acc

    return pl.pallas_call(
        body,
        grid_spec=pltpu.PrefetchScalarGridSpec(
            num_scalar_prefetch=1, grid=(r // tr,),
            in_specs=[pl.BlockSpec((n, tr, cc), lambda i, chip_ref: (0, i, 0)),
                      pl.BlockSpec((None, tr, cc), lambda i, chip_ref: (chip_ref[0], i, 0))],
            out_specs=pl.BlockSpec((tr, cc), lambda i, chip_ref: (i, 0))),
        out_shape=jax.ShapeDtypeStruct((r, cc), F32), compiler_params=_params("parallel"), name=name,
    )(chip_idx, arrived, mine)


def _pair_share(reds, name):
    n = len(reds)

    def body(*refs):
        ins, outs, (send_sems, recv_sems) = refs[:n], refs[n:2 * n], refs[2 * n:]
        x, y, c = _place()
        copies = [_rcopy(ins[i], outs[i].at[c], send_sems, recv_sems, i, (x, y, 1 - c)) for i in range(n)]
        for cp in copies:
            cp.start()
        for i in range(n):
            _rcopy(ins[i], outs[i].at[1 - c], send_sems, recv_sems, i, (x, y, 1 - c)).wait_recv()
        for cp in copies:
            cp.wait_send()

    return pl.pallas_call(
        body, in_specs=[_ANY] * n, out_specs=[_ANY] * n, out_shape=[jax.ShapeDtypeStruct((2,) + r.shape, r.dtype) for r in reds],
        scratch_shapes=[pltpu.SemaphoreType.DMA((n,)), pltpu.SemaphoreType.DMA((n,))], name=name,
    )(*reds)


def _pack(arrs, rows_multiple, dtype):
    flat = jnp.concatenate([a.reshape(-1).astype(dtype) for a in arrs])
    unit = rows_multiple * PACK_COLS
    total = -(-flat.shape[0] // unit) * unit
    return jnp.pad(flat, (0, total - flat.shape[0])).reshape(-1, PACK_COLS)


def _unpack(buf, shapes):
    lead = buf.shape[:-2]
    flat = buf.reshape(*lead, -1)
    out, off = [], 0
    for shp in shapes:
        n = 1
        for s in shp:
            n *= s
        out.append(flat[..., off:off + n].reshape(*lead, *shp))
        off += n
    return out


def _join_shards(parts, axis):
    moved = jnp.moveaxis(parts, 0, axis)
    shp = list(moved.shape)
    shp[axis:axis + 2] = [shp[axis] * shp[axis + 1]]
    return moved.reshape(shp)


def _my_shard(full, axis, chip):
    n = full.shape[axis] // N_CHIP
    return lax.dynamic_slice_in_dim(full, chip * n, n, axis)


SMALL = {"norm_g": 2, "ffn_conv_w": 2, "rg_conv_w": 2, "gla_w_alpha": 2, "gla_b_alpha": 1, "gla_norm_g": 1,
         "ada_b": None, "ffn_conv_b": None, "rg_conv_b": None, "rg_ba": None, "rg_bx": None, "rg_lambda": None}
BIG = {"rg_w_in": True, "rg_wa": False, "rg_wx": False, "rg_w_out": False, "ffn_w_up": True, "ffn_w_down": False,
       "gla_w_in": True, "gla_w_out": False}
WEIGHTS = ["ada_w", "ada_b", "norm_g", "ffn_w_up", "ffn_conv_w", "ffn_conv_b", "ffn_w_down", "rg_w_in", "rg_conv_w", "rg_conv_b",
           "rg_wa", "rg_ba", "rg_wx", "rg_bx", "rg_lambda", "rg_w_out", "gla_w_in", "gla_w_alpha", "gla_b_alpha", "gla_norm_g",
           "gla_w_out"]


def kernel(x, c, ada_w, ada_b, norm_g, ffn_w_up, ffn_conv_w, ffn_conv_b, ffn_w_down, rg_w_in, rg_conv_w, rg_conv_b, rg_wa, rg_ba, rg_wx, rg_bx, rg_lambda, rg_w_out, gla_w_in, gla_w_alpha, gla_b_alpha, gla_norm_g, gla_w_out, loss_target, m_ada_w, m_ada_b, m_norm_g, m_ffn_w_up, m_ffn_conv_w, m_ffn_conv_b, m_ffn_w_down, m_rg_w_in, m_rg_conv_w, m_rg_conv_b, m_rg_wa, m_rg_ba, m_rg_wx, m_rg_bx, m_rg_lambda, m_rg_w_out, m_gla_w_in, m_gla_w_alpha, m_gla_b_alpha, m_gla_norm_g, m_gla_w_out, v_ada_w, v_ada_b, v_norm_g, v_ffn_w_up, v_ffn_conv_w, v_ffn_conv_b, v_ffn_w_down, v_rg_w_in, v_rg_conv_w, v_rg_conv_b, v_rg_wa, v_rg_ba, v_rg_wx, v_rg_bx, v_rg_lambda, v_rg_w_out, v_gla_w_in, v_gla_w_alpha, v_gla_b_alpha, v_gla_norm_g, v_gla_w_out):
    wts = dict(ada_w=ada_w, ada_b=ada_b, norm_g=norm_g, ffn_w_up=ffn_w_up, ffn_conv_w=ffn_conv_w, ffn_conv_b=ffn_conv_b,
               ffn_w_down=ffn_w_down, rg_w_in=rg_w_in, rg_conv_w=rg_conv_w, rg_conv_b=rg_conv_b, rg_wa=rg_wa, rg_ba=rg_ba,
               rg_wx=rg_wx, rg_bx=rg_bx, rg_lambda=rg_lambda, rg_w_out=rg_w_out, gla_w_in=gla_w_in, gla_w_alpha=gla_w_alpha,
               gla_b_alpha=gla_b_alpha, gla_norm_g=gla_norm_g, gla_w_out=gla_w_out)
    mom1 = dict(ada_w=m_ada_w, ada_b=m_ada_b, norm_g=m_norm_g, ffn_w_up=m_ffn_w_up, ffn_conv_w=m_ffn_conv_w,
                ffn_conv_b=m_ffn_conv_b, ffn_w_down=m_ffn_w_down, rg_w_in=m_rg_w_in, rg_conv_w=m_rg_conv_w,
                rg_conv_b=m_rg_conv_b, rg_wa=m_rg_wa, rg_ba=m_rg_ba, rg_wx=m_rg_wx, rg_bx=m_rg_bx, rg_lambda=m_rg_lambda,
                rg_w_out=m_rg_w_out, gla_w_in=m_gla_w_in, gla_w_alpha=m_gla_w_alpha, gla_b_alpha=m_gla_b_alpha,
                gla_norm_g=m_gla_norm_g, gla_w_out=m_gla_w_out)
    mom2 = dict(ada_w=v_ada_w, ada_b=v_ada_b, norm_g=v_norm_g, ffn_w_up=v_ffn_w_up, ffn_conv_w=v_ffn_conv_w,
                ffn_conv_b=v_ffn_conv_b, ffn_w_down=v_ffn_w_down, rg_w_in=v_rg_w_in, rg_conv_w=v_rg_conv_w,
                rg_conv_b=v_rg_conv_b, rg_wa=v_rg_wa, rg_ba=v_rg_ba, rg_wx=v_rg_wx, rg_bx=v_rg_bx, rg_lambda=v_rg_lambda,
                rg_w_out=v_rg_w_out, gla_w_in=v_gla_w_in, gla_w_alpha=v_gla_w_alpha, gla_b_alpha=v_gla_b_alpha,
                gla_norm_g=v_gla_norm_g, gla_w_out=v_gla_w_out)
    xi, yi, ci = _place()
    chip, me = 2 * xi + yi, 4 * xi + 2 * yi + ci
    d = x.shape[-1]
    depth = ada_w.shape[0]
    n_ada = ada_w.shape[-1]
    sharded_small = [k for k, ax in SMALL.items() if ax is not None]

    sm = _all_gather_8(_pack([c] + [wts[k] for k in sharded_small], SUBLANES, F32), "gather_small")
    c_all = sm[:, 0, :]
    parts = _unpack(sm[0::2], [c.shape] + [wts[k].shape for k in sharded_small])[1:]
    full = {k: _join_shards(p, SMALL[k]) for k, p in zip(sharded_small, parts)}
    for k, ax in SMALL.items():
        if ax is None:
            full[k] = wts[k]

    c16 = jnp.pad(c_all, ((0, ADA_ROWS - N_DEV), (0, 0)))
    ada_b_mine = lax.dynamic_slice_in_dim(ada_b, chip * n_ada, n_ada, 1)[:, None, :]
    mod_cols = _ada_fwd(c16, ada_w, ada_b_mine, "ada_fwd")
    mod_all = _all_gather_8(mod_cols[:, :N_DEV].reshape(-1, PACK_COLS), "gather_mod")[0::2].reshape(N_CHIP, depth, N_DEV, n_ada)
    mod = jnp.swapaxes(lax.dynamic_index_in_dim(mod_all, me, 2, keepdims=False), 0, 1).reshape(depth, 6, d)

    items = [(k, l) for k in BIG for l in range(wts[k].shape[0])]
    stage_of = lambda k, l: "rg" if k.startswith("rg_") else ("ffn0" if (k.startswith("ffn_") and l == 0) else "l1")
    staged = {st: [it for it in items if stage_of(*it) == st] for st in ("rg", "ffn0", "l1")}
    staged["gla"] = [it for it in staged["l1"] if it[0].startswith("gla_")]
    staged["ffn1"] = [it for it in staged["l1"] if it[0].startswith("ffn_")]
    staged["l1"] = staged["gla"] + staged["ffn1"]
    shard = lambda k, l: wts[k][l].reshape(-1, wts[k].shape[-1]).astype(BF16)
    own = lambda got, mine: [lax.dynamic_update_index_in_dim(g, m, chip, 0) for g, m in zip(got, mine)]
    rows_joined = lambda v: v.reshape(-1, v.shape[-1])

    def placed(its, slots):
        out = {"ffn_w_up": {}, "ffn_w_down": {}}
        for (k, l), v in zip(its, slots):
            if k == "ffn_w_up":
                out[k][l] = v
            elif k == "ffn_w_down":
                out[k][l] = rows_joined(v)
            elif k in ("rg_wa", "rg_wx"):
                out[k] = _slots_to_block_rows(v, RG_BLOCKS)
            elif k == "gla_w_in":
                out[k] = _from_col_slots(v, "gla_w_in_join")
            else:
                out[k] = v if BIG[k] else rows_joined(v)
        return out

    after_mod = (mod[0, 0, 0] * 0.0).astype(BF16)
    sh_rg = [shard(k, l) + after_mod for k, l in staged["rg"]]
    local = {k: (v if k in ("norm_g", "ffn_conv_w", "ffn_conv_b") else v[0]) for k, v in full.items()}
    local.update(placed(staged["rg"], own(_gather_chips(sh_rg, "gather_weights_rg"), sh_rg)))
    sh_late, flying = {}, {}
    after_rg = (local["rg_w_out"][0, 0].astype(F32) * 0.0).astype(BF16)
    sh_late["ffn0"] = [shard(k, l) + after_rg for k, l in staged["ffn0"]]
    flying["ffn0"], tok = _send_start(sh_late["ffn0"], "whole", "weights_ffn0_start")
    for stage in ("gla", "ffn1"):
        sh_late[stage] = [shard(k, l) + tok[0, 0].astype(BF16) for k, l in staged[stage]]
        flying[stage], tok = _send_start(sh_late[stage], "whole", f"weights_{stage}_start")
    mod = mod + tok[0, 0]

    def fetch(stage, after):
        mine, got = _send_wait(flying[stage], after, "whole", f"weights_{stage}_wait")
        return placed(staged[stage], own(got, mine))

    c_idx = ci.reshape(1).astype(jnp.int32)
    gslots, paired, psums, sent, started = {}, {}, {}, {}, {}

    def grad_slots(gr, k, l):
        g = gr[k][l] if k in ("ffn_w_up", "ffn_w_down") else gr[k]
        if k in ("rg_wa", "rg_wx"):
            return _block_rows_to_slots(g)
        if k == "gla_w_in":
            return _col_slots(g, "gla_w_in_grad_slots")
        return g if BIG[k] else g.reshape(N_CHIP, -1, g.shape[-1])

    def done(stage, gr):
        gslots[stage] = [grad_slots(gr, k, l) for k, l in staged[stage]]
        paired[stage], token = _send_start(gslots[stage], "pair", f"grads_{stage}_pair_start")
        return token[0, 0]

    def later(stage, after):
        mine, theirs = _send_wait(paired[stage], after, "pair", f"grads_{stage}_pair_wait")
        psums[stage] = [_pair_sum(g, t, c_idx, f"grads_pair_sum_{k}{l}") for (k, l), g, t in zip(staged[stage], mine, theirs)]
        sent[stage], started[stage] = _send_start(psums[stage], "slots", f"grads_{stage}_start")
        return started[stage][0, 0]

    cols, grad_x, gr = _local_step(x[0], loss_target[0], mod, local, fetch, done, later)
    loss_mine = (0.5 * jnp.sum(cols) / d).reshape(1)

    small_names = [k for k in SMALL if k != "ada_b"]
    small_flying, small_sent = _send_start([_pack([gr[k] for k in small_names] + [gr["mod"], loss_mine], SUBLANES, F32)], "all",
                                           "grads_small_start")
    small_shapes = [full[k].shape for k in small_names] + [(depth, 6 * d), (1,)]
    chip_idx = chip.reshape(1).astype(jnp.int32)
    delta, new_m, new_v = {}, {}, {}
    grads = {}

    def reduce_and_update(stages, after, dep):
        its = [(st, n) for st in stages for n in range(len(staged[st]))]
        back = {st: _send_wait(sent[st], after, "slots", f"grads_{st}_wait") for st in stages}
        halves = [_chip_sum(back[st][1][n], back[st][0][n], chip_idx, "grads_chip_sum_%s%d" % staged[st][n]) for st, n in its]
        shared = _pair_share(halves, "grads_pair_share_" + stages[0])
        reduced = [lax.dynamic_update_index_in_dim(s2, h, ci, 0).reshape(-1, h.shape[-1]) for s2, h in zip(shared, halves)]
        last = None
        for k in BIG:
            gs_k = [g for (st, n), g in zip(its, reduced) if staged[st][n][0] == k]
            if gs_k:
                last = update(k, gs_k, dep)
        return last

    def update(k, gs_k, dep=None):
        shp = wts[k].shape
        if k == "gla_w_in":
            view, back = (lambda a: jnp.swapaxes(a, 1, 2)), (lambda o: jnp.swapaxes(o, 1, 2))
            gs_k = [g.T for g in gs_k]
        else:
            view, back = (lambda a: a.reshape(a.shape[0], -1, a.shape[-1])), (lambda o: o.reshape(shp))
        outs = _adamw(view(wts[k]), gs_k, view(mom1[k]), view(mom2[k]), "adamw_" + k, dep)
        grads[k], delta[k], new_m[k], new_v[k] = (back(o) for o in outs)
        return new_v[k]

    later("rg", small_sent)
    done_late = reduce_and_update(("ffn0", "l1"), grad_x, started["rg"])
    (small_mine,), (gs,) = _send_wait(small_flying, done_late, "all", "grads_small_wait")
    gs = lax.dynamic_update_index_in_dim(gs, small_mine, 2 * chip + ci, 0)
    *small_sum, g_ada_b, loss = _unpack(_sum_lead(gs, "sum_small_grads"), small_shapes)
    loss = loss[0]
    grads.update(zip(small_names, small_sum))
    grads["ada_b"] = g_ada_b
    for k in sharded_small:
        grads[k] = _my_shard(grads[k], SMALL[k], chip)
    dmod_all = _unpack(gs, small_shapes)[-2].reshape(N_DEV, depth, N_CHIP, n_ada)
    dmod_mine = jnp.swapaxes(lax.dynamic_index_in_dim(dmod_all, chip, 2, keepdims=False), 0, 1)
    g_ada_w = _ada_bwd(c16, jnp.pad(dmod_mine, ((0, 0), (0, ADA_ROWS - N_DEV), (0, 0))), "ada_bwd")
    update("ada_w", g_ada_w)
    small_shard_shapes = [wts[k].shape for k in SMALL]
    packed = [_pack([src[k] for k in SMALL], SUBLANES, F32) for src in (wts, grads, mom1, mom2)]
    outs = _adamw(packed[0][None], [packed[1]], packed[2][None], packed[3][None], "adamw_small")
    for dst, o in zip((delta, new_m, new_v), outs[1:]):
        for k, a in zip(SMALL, _unpack(o[0], small_shard_shapes)):
            dst[k] = a
    reduce_and_update(("rg",), outs[3], None)

    return (loss, grad_x[None], *[grads[k] for k in WEIGHTS], *[delta[k] for k in WEIGHTS], *[new_m[k] for k in WEIGHTS],
            *[new_v[k] for k in WEIGHTS])
```
